```python
import math
import jax, jax.numpy as jnp
from jax import lax
import numpy as np

D_MODEL = 1024
BATCH = 8
SEQ = 4096
DEPTH = 2

CHUNK = 64
HEAD_DIM = 64
N_A_LAYERS = DEPTH // 2
N_B_LAYERS = DEPTH - N_A_LAYERS
RMS_EPS = 1e-6
A_HEADS = D_MODEL // HEAD_DIM
A_WIDTH = A_HEADS * HEAD_DIM
A_LEFT_CHUNKS = 8
A_BAND = (A_LEFT_CHUNKS + 1) * CHUNK
A_REL_CLIP = 256
B_Q_HEADS = D_MODEL // HEAD_DIM
B_KV_HEADS = max(1, B_Q_HEADS // 8)
B_GROUP = B_Q_HEADS // B_KV_HEADS
B_WIDTH = B_Q_HEADS * HEAD_DIM
B_KV_WIDTH = B_KV_HEADS * HEAD_DIM
B_WINDOW = 128
B_LEFT_CHUNKS = (B_WINDOW - 1) // CHUNK + 1
B_BAND = (B_LEFT_CHUNKS + 1) * CHUNK
T5_BUCKETS = 32
T5_MAX_DIST = 128

kernel_name = "yoco_chunk_relbias_swa_sink_hybrid"


def rmsnorm(x, g):
    xf = x.astype(jnp.float32)
    y = xf * lax.rsqrt(jnp.mean(xf * xf, axis=-1, keepdims=True) + RMS_EPS)
    return (y * g.astype(jnp.float32)).astype(x.dtype)


def t5_bucket(rel):
    nb = T5_BUCKETS // 2
    max_exact = nb // 2
    ret = jnp.where(rel > 0, nb, 0)
    n = jnp.abs(rel)
    nf = jnp.maximum(n, 1).astype(jnp.float32)
    large = max_exact + (jnp.log(nf / max_exact) / math.log(T5_MAX_DIST / max_exact)
                         * (nb - max_exact)).astype(jnp.int32)
    large = jnp.minimum(large, nb - 1)
    return ret + jnp.where(n < max_exact, n, large)


def mixer_a(x, w_in, rel_bias, w_out):
    b, s, _ = x.shape
    nc = s // CHUNK
    pad = A_LEFT_CHUNKS * CHUNK
    q, k, v, g = jnp.split(x @ w_in, 4, axis=-1)
    q = q.reshape(b, nc, CHUNK, A_HEADS, HEAD_DIM)
    k = jnp.pad(k.reshape(b, s, A_HEADS, HEAD_DIM), ((0, 0), (pad, 0), (0, 0), (0, 0)))
    v = jnp.pad(v.reshape(b, s, A_HEADS, HEAD_DIM), ((0, 0), (pad, 0), (0, 0), (0, 0)))
    dist = jnp.arange(CHUNK)[:, None] + pad - jnp.arange(A_BAND)[None, :]
    idx = jnp.clip(dist, -A_REL_CLIP, A_REL_CLIP) + A_REL_CLIP
    bias = jnp.transpose(rel_bias[idx], (2, 0, 1)).astype(jnp.float32)
    scale = HEAD_DIM ** -0.5

    def one_chunk(args):
        qc, c = args
        kc = lax.dynamic_slice_in_dim(k, c * CHUNK, A_BAND, axis=1)
        vc = lax.dynamic_slice_in_dim(v, c * CHUNK, A_BAND, axis=1)
        logits = jnp.einsum('bqhd,bkhd->bhqk', qc, kc).astype(jnp.float32) * scale + bias
        key_pos = c * CHUNK - pad + jnp.arange(A_BAND)
        logits = jnp.where((key_pos >= 0)[None, None, None, :], logits, -jnp.inf)
        p = jax.nn.softmax(logits, axis=-1).astype(vc.dtype)
        return jnp.einsum('bhqk,bkhd->bqhd', p, vc)

    out = lax.map(one_chunk, (jnp.moveaxis(q, 1, 0), jnp.arange(nc)))
    out = jnp.moveaxis(out, 0, 1).reshape(b, s, A_WIDTH)
    return (out * jax.nn.silu(g)) @ w_out


def mixer_b(x, w_in, sinks, t5_table, k_sh, v_sh, w_out):
    b, s, _ = x.shape
    nc = s // CHUNK
    pad = B_LEFT_CHUNKS * CHUNK
    q, g = jnp.split(x @ w_in, 2, axis=-1)
    q = q.reshape(b, nc, CHUNK, B_KV_HEADS, B_GROUP, HEAD_DIM)
    kp = jnp.pad(k_sh, ((0, 0), (pad, 0), (0, 0), (0, 0)))
    vp = jnp.pad(v_sh, ((0, 0), (pad, 0), (0, 0), (0, 0)))
    band_idx = jnp.arange(nc)[:, None] * CHUNK + jnp.arange(B_BAND)[None, :]
    kb = kp[:, band_idx]
    vb = vp[:, band_idx]
    rel = jnp.arange(B_BAND)[None, :] - pad - jnp.arange(CHUNK)[:, None]
    bias = jnp.transpose(t5_table[t5_bucket(rel)], (2, 0, 1)).astype(jnp.float32)
    bias = bias.reshape(B_KV_HEADS, B_GROUP, 1, CHUNK, B_BAND)
    scale = HEAD_DIM ** -0.5
    logits = jnp.einsum('bcqhgd,bckhd->bhgcqk', q, kb).astype(jnp.float32) * scale + bias
    key_pos = band_idx - pad
    logits = jnp.where((key_pos >= 0)[:, None, :], logits, -jnp.inf)
    sink = jnp.broadcast_to(
        sinks.astype(jnp.float32).reshape(1, B_KV_HEADS, B_GROUP, 1, 1, 1),
        logits.shape[:-1] + (1,))
    p = jax.nn.softmax(jnp.concatenate([logits, sink], axis=-1), axis=-1)[..., :-1]
    out = jnp.einsum('bhgcqk,bckhd->bcqhgd', p.astype(vb.dtype), vb)
    out = out.reshape(b, s, B_WIDTH)
    return (out * jax.nn.silu(g)) @ w_out


def _fwd_setup_inputs(seed: int = 0) -> dict:
    key = jax.random.key(seed)
    ks = jax.random.split(key, 13)
    f32 = jnp.float32
    nrm = lambda k, shape, sc: (jax.random.normal(k, shape, f32) * sc).astype(f32)
    return {
        "x": nrm(ks[0], (BATCH, SEQ, D_MODEL), 1.0),
        "a_norm": 1.0 + nrm(ks[1], (N_A_LAYERS, D_MODEL), 0.02),
        "a_w_in": nrm(ks[2], (N_A_LAYERS, D_MODEL, 4 * A_WIDTH), D_MODEL ** -0.5),
        "a_rel_bias": nrm(ks[3], (N_A_LAYERS, 2 * A_REL_CLIP + 1, A_HEADS), 0.3),
        "a_w_out": nrm(ks[4], (N_A_LAYERS, A_WIDTH, D_MODEL), A_WIDTH ** -0.5),
        "kv_norm": 1.0 + nrm(ks[5], (D_MODEL,), 0.02),
        "kv_w": nrm(ks[6], (D_MODEL, 2 * B_KV_WIDTH), D_MODEL ** -0.5),
        "t5_bias": nrm(ks[7], (T5_BUCKETS, B_Q_HEADS), 0.3),
        "b_norm": 1.0 + nrm(ks[8], (N_B_LAYERS, D_MODEL), 0.02),
        "b_w_in": nrm(ks[9], (N_B_LAYERS, D_MODEL, 2 * B_WIDTH), D_MODEL ** -0.5),
        "b_sinks": nrm(ks[10], (N_B_LAYERS, B_Q_HEADS), 0.5),
        "b_w_out": nrm(ks[11], (N_B_LAYERS, B_WIDTH, D_MODEL), B_WIDTH ** -0.5),
        "final_norm": 1.0 + nrm(ks[12], (D_MODEL,), 0.02),
    }


def _fwd_reference(x, a_norm, a_w_in, a_rel_bias, a_w_out, kv_norm, kv_w, t5_bias,
              b_norm, b_w_in, b_sinks, b_w_out, final_norm):
    h = x
    k_sh = v_sh = None
    for layer in range(DEPTH):
        if layer == N_A_LAYERS:
            kv = rmsnorm(h, kv_norm) @ kv_w
            k_sh, v_sh = jnp.split(kv, 2, axis=-1)
            k_sh = k_sh.reshape(h.shape[0], h.shape[1], B_KV_HEADS, HEAD_DIM)
            v_sh = v_sh.reshape(h.shape[0], h.shape[1], B_KV_HEADS, HEAD_DIM)
        if layer < N_A_LAYERS:
            h = h + mixer_a(rmsnorm(h, a_norm[layer]), a_w_in[layer],
                            a_rel_bias[layer], a_w_out[layer])
        else:
            j = layer - N_A_LAYERS
            h = h + mixer_b(rmsnorm(h, b_norm[j]), b_w_in[j], b_sinks[j], t5_bias,
                            k_sh, v_sh, b_w_out[j])
    return rmsnorm(h, final_norm)


import jax as _jax
import jax.numpy as _jnp

TWIN_FORMAT = 'train_step'
FWD_PARAMS = ['x', 'a_norm', 'a_w_in', 'a_rel_bias', 'a_w_out', 'kv_norm', 'kv_w', 't5_bias', 'b_norm', 'b_w_in', 'b_sinks', 'b_w_out', 'final_norm']
TWIN_WEIGHTS = ['a_norm', 'a_w_in', 'a_rel_bias', 'a_w_out', 'kv_norm', 'kv_w', 't5_bias', 'b_norm', 'b_w_in', 'b_sinks', 'b_w_out', 'final_norm']
TWIN_DIFF_INPUT = 'x'
TWIN_INPUTS = ['x', 'a_norm', 'a_w_in', 'a_rel_bias', 'a_w_out', 'kv_norm', 'kv_w', 't5_bias', 'b_norm', 'b_w_in', 'b_sinks', 'b_w_out', 'final_norm', 'loss_target', 'm_a_norm', 'm_a_w_in', 'm_a_rel_bias', 'm_a_w_out', 'm_kv_norm', 'm_kv_w', 'm_t5_bias', 'm_b_norm', 'm_b_w_in', 'm_b_sinks', 'm_b_w_out', 'm_final_norm', 'v_a_norm', 'v_a_w_in', 'v_a_rel_bias', 'v_a_w_out', 'v_kv_norm', 'v_kv_w', 'v_t5_bias', 'v_b_norm', 'v_b_w_in', 'v_b_sinks', 'v_b_w_out', 'v_final_norm']
TWIN_OUTPUTS = ['loss', 'grad_x', 'grad_a_norm', 'grad_a_w_in', 'grad_a_rel_bias', 'grad_a_w_out', 'grad_kv_norm', 'grad_kv_w', 'grad_t5_bias', 'grad_b_norm', 'grad_b_w_in', 'grad_b_sinks', 'grad_b_w_out', 'grad_final_norm', 'delta_a_norm', 'delta_a_w_in', 'delta_a_rel_bias', 'delta_a_w_out', 'delta_kv_norm', 'delta_kv_w', 'delta_t5_bias', 'delta_b_norm', 'delta_b_w_in', 'delta_b_sinks', 'delta_b_w_out', 'delta_final_norm', 'new_m_a_norm', 'new_m_a_w_in', 'new_m_a_rel_bias', 'new_m_a_w_out', 'new_m_kv_norm', 'new_m_kv_w', 'new_m_t5_bias', 'new_m_b_norm', 'new_m_b_w_in', 'new_m_b_sinks', 'new_m_b_w_out', 'new_m_final_norm', 'new_v_a_norm', 'new_v_a_w_in', 'new_v_a_rel_bias', 'new_v_a_w_out', 'new_v_kv_norm', 'new_v_kv_w', 'new_v_t5_bias', 'new_v_b_norm', 'new_v_b_w_in', 'new_v_b_sinks', 'new_v_b_w_out', 'new_v_final_norm']
TWIN_LEAF_KINDS = {'loss': 'loss', 'grad_x': 'grad_x', 'grad_a_norm': 'grad_w', 'grad_a_w_in': 'grad_w', 'grad_a_rel_bias': 'grad_w', 'grad_a_w_out': 'grad_w', 'grad_kv_norm': 'grad_w', 'grad_kv_w': 'grad_w', 'grad_t5_bias': 'grad_w', 'grad_b_norm': 'grad_w', 'grad_b_w_in': 'grad_w', 'grad_b_sinks': 'grad_w', 'grad_b_w_out': 'grad_w', 'grad_final_norm': 'grad_w', 'delta_a_norm': 'delta_w', 'delta_a_w_in': 'delta_w', 'delta_a_rel_bias': 'delta_w', 'delta_a_w_out': 'delta_w', 'delta_kv_norm': 'delta_w', 'delta_kv_w': 'delta_w', 'delta_t5_bias': 'delta_w', 'delta_b_norm': 'delta_w', 'delta_b_w_in': 'delta_w', 'delta_b_sinks': 'delta_w', 'delta_b_w_out': 'delta_w', 'delta_final_norm': 'delta_w', 'new_m_a_norm': 'new_m', 'new_m_a_w_in': 'new_m', 'new_m_a_rel_bias': 'new_m', 'new_m_a_w_out': 'new_m', 'new_m_kv_norm': 'new_m', 'new_m_kv_w': 'new_m', 'new_m_t5_bias': 'new_m', 'new_m_b_norm': 'new_m', 'new_m_b_w_in': 'new_m', 'new_m_b_sinks': 'new_m', 'new_m_b_w_out': 'new_m', 'new_m_final_norm': 'new_m', 'new_v_a_norm': 'new_v', 'new_v_a_w_in': 'new_v', 'new_v_a_rel_bias': 'new_v', 'new_v_a_w_out': 'new_v', 'new_v_kv_norm': 'new_v', 'new_v_kv_w': 'new_v', 'new_v_t5_bias': 'new_v', 'new_v_b_norm': 'new_v', 'new_v_b_w_in': 'new_v', 'new_v_b_sinks': 'new_v', 'new_v_b_w_out': 'new_v', 'new_v_final_norm': 'new_v'}


def _forward(args):
    return _fwd_reference(*[args[k] for k in FWD_PARAMS])


def _output_shape():
    def fwd():
        inp = _fwd_setup_inputs(0)
        return _fwd_reference(*[inp[k] for k in FWD_PARAMS])
    out = _jax.eval_shape(fwd)
    return out.shape, out.dtype

N_MICROBATCH = 1
ADAM_LR = 0.001
ADAM_B1 = 0.9
ADAM_B2 = 0.999
ADAM_EPS = 1e-08
ADAM_WD = 0.01
ADAM_STEP = 10
PER_EXAMPLE_BATCH_AXIS = {'x': 0, 'loss_target': 0}
SHARED_INPUTS = []
_WEIGHT_DTYPES = {'a_norm': _jnp.float32, 'a_w_in': _jnp.float32, 'a_rel_bias': _jnp.float32, 'a_w_out': _jnp.float32, 'kv_norm': _jnp.float32, 'kv_w': _jnp.float32, 't5_bias': _jnp.float32, 'b_norm': _jnp.float32, 'b_w_in': _jnp.float32, 'b_sinks': _jnp.float32, 'b_w_out': _jnp.float32, 'final_norm': _jnp.float32}
MOMENT_SCALE = {'a_norm': 3.260627e-02, 'a_w_in': 1.614177e-02, 'a_rel_bias': 5.427683e-03, 'a_w_out': 1.649633e-02, 'kv_norm': 2.244656e-02, 'kv_w': 4.471463e-02, 't5_bias': 1.909355e-02, 'b_norm': 2.213570e-02, 'b_w_in': 1.596009e-02, 'b_sinks': 1.104108e-03, 'b_w_out': 1.612968e-02, 'final_norm': 3.203400e+01}


def _to_microbatches(a, axis):
    t = _jnp.moveaxis(a, axis, 0)
    t = t.reshape((N_MICROBATCH, t.shape[0] // N_MICROBATCH) + t.shape[1:])
    return _jnp.moveaxis(t, 1, axis + 1)


def setup_inputs(seed: int = 0) -> dict:
    inp = _fwd_setup_inputs(seed)
    key = _jax.random.fold_in(_jax.random.key(seed), 7919)
    shape, _ = _output_shape()
    out = dict(inp)
    out["loss_target"] = _jax.random.normal(_jax.random.fold_in(key, 0), shape, _jnp.float32)
    for i, name in enumerate(TWIN_WEIGHTS):
        w = inp[name].astype(_jnp.float32)
        if MOMENT_SCALE is None:
            s = _jnp.sqrt(_jnp.mean(_jnp.square(w)) + 1e-30)
        else:
            s = MOMENT_SCALE[name]
        km, kv = _jax.random.split(_jax.random.fold_in(key, i + 1))
        out[name] = w
        out["m_" + name] = s * _jax.random.normal(km, w.shape, _jnp.float32)
        out["v_" + name] = (s * s) * _jax.random.uniform(kv, w.shape, _jnp.float32, 0.5, 1.5)
    if N_MICROBATCH > 1:
        for name, axis in PER_EXAMPLE_BATCH_AXIS.items():
            out[name] = _to_microbatches(out[name], axis)
    return {'x': out['x'], 'a_norm': out['a_norm'], 'a_w_in': out['a_w_in'], 'a_rel_bias': out['a_rel_bias'], 'a_w_out': out['a_w_out'], 'kv_norm': out['kv_norm'], 'kv_w': out['kv_w'], 't5_bias': out['t5_bias'], 'b_norm': out['b_norm'], 'b_w_in': out['b_w_in'], 'b_sinks': out['b_sinks'], 'b_w_out': out['b_w_out'], 'final_norm': out['final_norm'], 'loss_target': out['loss_target'], 'm_a_norm': out['m_a_norm'], 'm_a_w_in': out['m_a_w_in'], 'm_a_rel_bias': out['m_a_rel_bias'], 'm_a_w_out': out['m_a_w_out'], 'm_kv_norm': out['m_kv_norm'], 'm_kv_w': out['m_kv_w'], 'm_t5_bias': out['m_t5_bias'], 'm_b_norm': out['m_b_norm'], 'm_b_w_in': out['m_b_w_in'], 'm_b_sinks': out['m_b_sinks'], 'm_b_w_out': out['m_b_w_out'], 'm_final_norm': out['m_final_norm'], 'v_a_norm': out['v_a_norm'], 'v_a_w_in': out['v_a_w_in'], 'v_a_rel_bias': out['v_a_rel_bias'], 'v_a_w_out': out['v_a_w_out'], 'v_kv_norm': out['v_kv_norm'], 'v_kv_w': out['v_kv_w'], 'v_t5_bias': out['v_t5_bias'], 'v_b_norm': out['v_b_norm'], 'v_b_w_in': out['v_b_w_in'], 'v_b_sinks': out['v_b_sinks'], 'v_b_w_out': out['v_b_w_out'], 'v_final_norm': out['v_final_norm']}


def _loss(weights, diff, rest, loss_target):
    with _jax.named_scope("forward"):
        args = {**rest, TWIN_DIFF_INPUT: diff, **{k: w.astype(_WEIGHT_DTYPES[k]) for k, w in weights.items()}}
        y = _forward(args)
    with _jax.named_scope("loss_head"):
        err = _jnp.square(y.astype(_jnp.float32) - loss_target)
        return 0.5 * _jnp.sum(_jnp.mean(err, axis=-1)) if err.ndim else 0.5 * err


def _adamw(w, g, m, v):
    m = ADAM_B1 * m + (1.0 - ADAM_B1) * g
    v = ADAM_B2 * v + (1.0 - ADAM_B2) * _jnp.square(g)
    m_hat = m / (1.0 - ADAM_B1 ** ADAM_STEP)
    v_hat = v / (1.0 - ADAM_B2 ** ADAM_STEP)
    delta = -ADAM_LR * (m_hat / (_jnp.sqrt(v_hat) + ADAM_EPS) + ADAM_WD * w)
    return delta, m, v


def reference(x, a_norm, a_w_in, a_rel_bias, a_w_out, kv_norm, kv_w, t5_bias, b_norm, b_w_in, b_sinks, b_w_out, final_norm, loss_target, m_a_norm, m_a_w_in, m_a_rel_bias, m_a_w_out, m_kv_norm, m_kv_w, m_t5_bias, m_b_norm, m_b_w_in, m_b_sinks, m_b_w_out, m_final_norm, v_a_norm, v_a_w_in, v_a_rel_bias, v_a_w_out, v_kv_norm, v_kv_w, v_t5_bias, v_b_norm, v_b_w_in, v_b_sinks, v_b_w_out, v_final_norm):
    given = dict(x=x, a_norm=a_norm, a_w_in=a_w_in, a_rel_bias=a_rel_bias, a_w_out=a_w_out, kv_norm=kv_norm, kv_w=kv_w, t5_bias=t5_bias, b_norm=b_norm, b_w_in=b_w_in, b_sinks=b_sinks, b_w_out=b_w_out, final_norm=final_norm, loss_target=loss_target, m_a_norm=m_a_norm, m_a_w_in=m_a_w_in, m_a_rel_bias=m_a_rel_bias, m_a_w_out=m_a_w_out, m_kv_norm=m_kv_norm, m_kv_w=m_kv_w, m_t5_bias=m_t5_bias, m_b_norm=m_b_norm, m_b_w_in=m_b_w_in, m_b_sinks=m_b_sinks, m_b_w_out=m_b_w_out, m_final_norm=m_final_norm, v_a_norm=v_a_norm, v_a_w_in=v_a_w_in, v_a_rel_bias=v_a_rel_bias, v_a_w_out=v_a_w_out, v_kv_norm=v_kv_norm, v_kv_w=v_kv_w, v_t5_bias=v_t5_bias, v_b_norm=v_b_norm, v_b_w_in=v_b_w_in, v_b_sinks=v_b_sinks, v_b_w_out=v_b_w_out, v_final_norm=v_final_norm)
    weights = {n: given[n] for n in TWIN_WEIGHTS}
    shared = {n: given[n] for n in SHARED_INPUTS}
    per_example = {n: given[n] for n in ['x']}
    grad_fn = _jax.value_and_grad(_loss, argnums=(0, 1))

    def one_microbatch(ex, loss_target):
        ex = dict(ex)
        diff = ex.pop(TWIN_DIFF_INPUT)
        return grad_fn(weights, diff, {**shared, **ex}, loss_target)

    if N_MICROBATCH == 1:
        loss, (grad_w, grad_x) = one_microbatch(per_example, given["loss_target"])
    else:
        def body(carry, xs):
            loss_sum, grad_sum = carry
            l_k, (gw_k, gx_k) = one_microbatch(xs[0], xs[1])
            with _jax.named_scope("update"):
                return (loss_sum + l_k, _jax.tree.map(_jnp.add, grad_sum, gw_k)), gx_k

        init = (_jnp.zeros((), _jnp.float32), _jax.tree.map(_jnp.zeros_like, weights))
        (loss, grad_w), grad_x = _jax.lax.scan(body, init, (per_example, given["loss_target"]))
    with _jax.named_scope("update"):
        delta_w, new_m, new_v = {}, {}, {}
        for n in TWIN_WEIGHTS:
            delta_w[n], new_m[n], new_v[n] = _adamw(weights[n], grad_w[n], given["m_" + n], given["v_" + n])
    return (loss, grad_x, *[grad_w[n] for n in TWIN_WEIGHTS], *[delta_w[n] for n in TWIN_WEIGHTS],
            *[new_m[n] for n in TWIN_WEIGHTS], *[new_v[n] for n in TWIN_WEIGHTS])
```

```python
import functools
import math

import jax
import jax.numpy as jnp
import numpy as np
from jax import lax
from jax.experimental import pallas as pl
from jax.experimental.pallas import tpu as pltpu

F32 = jnp.float32
BF16 = jnp.bfloat16
MESH = pl.DeviceIdType.MESH

D_MODEL = 1024
HEADS = 16
HEAD_DIM = 64
CHUNK = 64
RMS_EPS = 1e-6
SCALE = HEAD_DIM ** -0.5
A_LEFT_CHUNKS = 8
A_REL_CLIP = 256
B_LEFT_CHUNKS = 2
B_KV_HEADS = 2
B_GROUP = HEADS // B_KV_HEADS
T5_BUCKETS = 32
T5_MAX_DIST = 128
ADAM_LR = 0.001
ADAM_B1 = 0.9
ADAM_B2 = 0.999
ADAM_EPS = 1e-08
ADAM_WD = 0.01
ADAM_STEP = 10

MASKED = -1e30
LANES = 128
TQ = 128
KB = 128
A_KBLOCKS = A_LEFT_CHUNKS * CHUNK // KB + 1
B_KBLOCKS = B_LEFT_CHUNKS * CHUNK // KB + 1
A_WIN = A_KBLOCKS * KB
B_WIN = B_KBLOCKS * KB
TM = 512
VMEM_LIMIT = 56 * 1024 * 1024

NT = (((1,), (1,)), ((), ()))
TN = (((0,), (0,)), ((), ()))
NN = (((1,), (0,)), ((), ()))


def _params(sem=None):
    return pltpu.CompilerParams(dimension_semantics=sem, vmem_limit_bytes=VMEM_LIMIT)


def _matmul(name, a, b, *, dims, grid, a_spec, b_spec, o_spec, out_shape, out_dtype,
            k_axis=None, k_steps=1, resid=None, resid_spec=None):
    if k_axis is not None:
        assert out_dtype == F32 and resid is None

    def body(*refs):
        if resid is not None:
            a_ref, b_ref, r_ref, o_ref = refs
        else:
            a_ref, b_ref, o_ref = refs
        prod = lax.dot_general(a_ref[...].astype(BF16), b_ref[...].astype(BF16), dims,
                               preferred_element_type=F32)
        if k_axis is None:
            if resid is not None:
                prod = r_ref[...] + prod
            o_ref[...] = prod.astype(out_dtype)
        else:
            k = pl.program_id(k_axis)

            @pl.when(k == 0)
            def _():
                o_ref[...] = prod

            @pl.when(k > 0)
            def _():
                o_ref[...] += prod

    in_specs = [a_spec, b_spec]
    args = [a, b]
    if resid is not None:
        in_specs.append(resid_spec)
        args.append(resid)
    sem = ["parallel"] * len(grid)
    if k_axis is not None:
        assert k_axis == len(grid) - 1 and grid[k_axis] == k_steps
        sem[k_axis] = "arbitrary"
    return pl.pallas_call(
        body, name=name, grid=grid, in_specs=in_specs, out_specs=o_spec,
        out_shape=jax.ShapeDtypeStruct(out_shape, out_dtype),
        compiler_params=_params(tuple(sem)),
    )(*args)


def _rms_rows(x):
    return lax.rsqrt(jnp.mean(x * x, axis=-1, keepdims=True) + RMS_EPS)


def _norm_fwd(name, x, gains):
    s, d = x.shape
    n = gains.shape[0]

    def body(x_ref, g_ref, *o_refs):
        xv = x_ref[...]
        xh = xv * _rms_rows(xv)
        for i in range(n):
            o_refs[i][...] = (xh * g_ref[i:i + 1, :]).astype(BF16)

    row = pl.BlockSpec((TM, d), lambda i: (i, 0))
    return pl.pallas_call(
        body, name=name, grid=(s // TM,),
        in_specs=[row, pl.BlockSpec((n, d), lambda i: (0, 0))],
        out_specs=[row] * n,
        out_shape=[jax.ShapeDtypeStruct((s, d), BF16)] * n,
        compiler_params=_params(("parallel",)),
    )(x, gains)


def _norm_bwd(name, x, dres, dns, gains):
    s, d = x.shape
    n = len(dns)

    def body(x_ref, r_ref, g_ref, *refs):
        dn_refs, dx_ref, dg_ref = refs[:n], refs[n], refs[n + 1]
        i = pl.program_id(0)
        xv = x_ref[...]
        r = _rms_rows(xv)
        xh = xv * r
        @pl.when(i == 0)
        def _():
            dg_ref[...] = jnp.zeros_like(dg_ref)

        a = None
        for j in range(n):
            dn = dn_refs[j][...]
            t = dn * g_ref[j:j + 1, :]
            a = t if a is None else a + t
            dg_ref[j:j + 1, :] += jnp.sum(dn * xh, axis=0, keepdims=True)
        dx_ref[...] = r_ref[...] + r * (a - xh * jnp.mean(xh * a, axis=-1, keepdims=True))

    row = pl.BlockSpec((TM, d), lambda i: (i, 0))
    small = pl.BlockSpec((n, d), lambda i: (0, 0))
    return pl.pallas_call(
        body, name=name, grid=(s // TM,),
        in_specs=[row, row, small] + [row] * n,
        out_specs=[row, small],
        out_shape=[jax.ShapeDtypeStruct((s, d), F32), jax.ShapeDtypeStruct((n, d), F32)],
        compiler_params=_params(("arbitrary",)),
    )(x, dres, gains, *dns)


def _loss_head(h2, target, gain):
    s, d = h2.shape

    def body(h_ref, t_ref, g_ref, dh_ref, loss_ref, dg_ref):
        i = pl.program_id(0)
        hv = h_ref[...]
        r = _rms_rows(hv)
        hh = hv * r
        g = g_ref[...]
        err = hh * g - t_ref[...]
        part = 0.5 * jnp.sum(jnp.sum(err * err, axis=-1, keepdims=True) * (1.0 / d), axis=0, keepdims=True)
        dy = err * (1.0 / d)
        a = dy * g
        dh_ref[...] = r * (a - hh * jnp.mean(hh * a, axis=-1, keepdims=True))
        dg = jnp.sum(dy * hh, axis=0, keepdims=True)

        @pl.when(i == 0)
        def _():
            loss_ref[...] = part
            dg_ref[...] = dg

        @pl.when(i > 0)
        def _():
            loss_ref[...] += part
            dg_ref[...] += dg

    row = pl.BlockSpec((TM, d), lambda i: (i, 0))
    return pl.pallas_call(
        body, name="loss_head", grid=(s // TM,),
        in_specs=[row, row, pl.BlockSpec((1, d), lambda i: (0, 0))],
        out_specs=[row, pl.BlockSpec((1, 1), lambda i: (0, 0)), pl.BlockSpec((1, d), lambda i: (0, 0))],
        out_shape=[jax.ShapeDtypeStruct((s, d), F32), jax.ShapeDtypeStruct((1, 1), F32),
                   jax.ShapeDtypeStruct((1, d), F32)],
        compiler_params=_params(("arbitrary",)),
    )(h2, target, gain)


def _silu_parts(g):
    sig = jax.nn.sigmoid(g)
    return g * sig, sig * (1.0 + g * (1.0 - sig))


def _band_valid(b, left_blocks, width):
    blk = lax.broadcasted_iota(jnp.int32, (1, width), 1) // KB
    return (blk + (b - left_blocks)) >= 0


def _attn_a_fwd(zqkv, g, bias):
    s = g.shape[0]
    nb = s // TQ
    left = A_KBLOCKS - 1

    def body(q_ref, *refs):
        k_refs = refs[:A_KBLOCKS]
        v_refs = refs[A_KBLOCKS:2 * A_KBLOCKS]
        g_ref, bias_ref, o_ref, u_ref, lse_ref = refs[2 * A_KBLOCKS:]
        b = pl.program_id(1)
        q = q_ref[...]
        kk = jnp.concatenate([r[...] for r in k_refs], axis=0)
        vv = jnp.concatenate([r[...] for r in v_refs], axis=0)
        valid = _band_valid(b, left, A_WIN)
        lane = lax.broadcasted_iota(jnp.int32, (TQ, LANES), 1)
        outs, lses = [], []
        for h in range(2):
            sl = slice(h * HEAD_DIM, (h + 1) * HEAD_DIM)
            sc = lax.dot_general(q[:, sl], kk[:, sl], NT, preferred_element_type=F32)
            sc = sc * SCALE + bias_ref[h]
            sc = jnp.where(valid, sc, MASKED)
            m = jnp.max(sc, axis=-1, keepdims=True)
            e = jnp.exp(sc - m)
            l = jnp.sum(e, axis=-1, keepdims=True)
            p = (e * (1.0 / l)).astype(BF16)
            outs.append(jnp.dot(p, vv[:, sl], preferred_element_type=F32))
            lses.append(m + jnp.log(l))
        lse_ref[...] = jnp.where(lane < HEAD_DIM, lses[0], lses[1])
        ov = jnp.concatenate(outs, axis=-1)
        o_ref[...] = ov
        sg, _ = _silu_parts(g_ref[...])
        u_ref[...] = (ov * sg).astype(BF16)

    kv_specs = []
    for which in (1, 2):
        for t in range(A_KBLOCKS):
            kv_specs.append(pl.BlockSpec(
                (None, KB, LANES), functools.partial(
                    lambda p, b, which, t: (which, jnp.maximum(b - left + t, 0), p), which=which, t=t)))
    tile = pl.BlockSpec((TQ, LANES), lambda p, b: (b, p))
    return pl.pallas_call(
        body, name="attn_a_fwd", grid=(HEADS // 2, nb),
        in_specs=[pl.BlockSpec((None, TQ, LANES), lambda p, b: (0, b, p))] + kv_specs + [
            tile, pl.BlockSpec((2, TQ, A_WIN), lambda p, b: (p, 0, 0))],
        out_specs=[tile, tile, pl.BlockSpec((None, TQ, LANES), lambda p, b: (p, b, 0))],
        out_shape=[jax.ShapeDtypeStruct((s, D_MODEL), F32), jax.ShapeDtypeStruct((s, D_MODEL), BF16),
                   jax.ShapeDtypeStruct((HEADS // 2, s, LANES), F32)],
        compiler_params=_params(("parallel", "parallel")),
    )(zqkv, *([zqkv] * (2 * A_KBLOCKS)), g, bias)


def _attn_a_bwd(zqkv, g, o, du, lse, bias):
    s = g.shape[0]
    nb = s // TQ
    left = A_KBLOCKS - 1

    def body(q_ref, *refs):
        k_refs = refs[:A_KBLOCKS]
        v_refs = refs[A_KBLOCKS:2 * A_KBLOCKS]
        g_ref, o_ref, du_ref, lse_ref, bias_ref, dz_ref, dbias_ref, dk_acc, dv_acc = refs[2 * A_KBLOCKS:]
        b = pl.program_id(1)

        @pl.when(b == 0)
        def _():
            dk_acc[...] = jnp.zeros_like(dk_acc)
            dv_acc[...] = jnp.zeros_like(dv_acc)
            dbias_ref[...] = jnp.zeros_like(dbias_ref)

        q = q_ref[...]
        kk = jnp.concatenate([r[...] for r in k_refs], axis=0)
        vv = jnp.concatenate([r[...] for r in v_refs], axis=0)
        valid = _band_valid(b, left, A_WIN)
        sg, dsg = _silu_parts(g_ref[...])
        duv = du_ref[...]
        ov = o_ref[...]
        do = duv * sg
        rows = pl.ds(pl.multiple_of(b * TQ, TQ), TQ)
        dz_ref[3, rows, :] = (duv * ov * dsg).astype(BF16)
        do_o = do * ov
        do_bf = do.astype(BF16)
        dqs, dks, dvs = [], [], []
        for h in range(2):
            sl = slice(h * HEAD_DIM, (h + 1) * HEAD_DIM)
            sc = lax.dot_general(q[:, sl], kk[:, sl], NT, preferred_element_type=F32)
            sc = sc * SCALE + bias_ref[h]
            sc = jnp.where(valid, sc, MASKED)
            p = jnp.exp(sc - lse_ref[:, h * HEAD_DIM:h * HEAD_DIM + 1])
            dp = lax.dot_general(do_bf[:, sl], vv[:, sl], NT, preferred_element_type=F32)
            delta = jnp.sum(do_o[:, sl], axis=-1, keepdims=True)
            ds = p * (dp - delta)
            dbias_ref[h] += ds
            dsq = (ds * SCALE).astype(BF16)
            dqs.append(jnp.dot(dsq, kk[:, sl], preferred_element_type=F32))
            dks.append(lax.dot_general(dsq, q[:, sl], TN, preferred_element_type=F32))
            dvs.append(lax.dot_general(p.astype(BF16), do_bf[:, sl], TN, preferred_element_type=F32))
        dz_ref[0, rows, :] = jnp.concatenate(dqs, axis=-1).astype(BF16)
        dk = jnp.concatenate(dks, axis=-1)
        dv = jnp.concatenate(dvs, axis=-1)
        for t in range(A_KBLOCKS):
            krows = pl.ds(pl.multiple_of(jnp.maximum(b - left + t, 0) * KB, KB), KB)
            dk_acc[krows, :] += dk[t * KB:(t + 1) * KB, :]
            dv_acc[krows, :] += dv[t * KB:(t + 1) * KB, :]

        @pl.when(b == nb - 1)
        def _():
            dz_ref[1] = dk_acc[...].astype(BF16)
            dz_ref[2] = dv_acc[...].astype(BF16)

    kv_specs = []
    for which in (1, 2):
        for t in range(A_KBLOCKS):
            kv_specs.append(pl.BlockSpec(
                (None, KB, LANES), functools.partial(
                    lambda p, b, which, t: (which, jnp.maximum(b - left + t, 0), p), which=which, t=t)))
    tile = pl.BlockSpec((TQ, LANES), lambda p, b: (b, p))
    bias_spec = pl.BlockSpec((2, TQ, A_WIN), lambda p, b: (p, 0, 0))
    return pl.pallas_call(
        body, name="attn_a_bwd", grid=(HEADS // 2, nb),
        in_specs=[pl.BlockSpec((None, TQ, LANES), lambda p, b: (0, b, p))] + kv_specs + [
            tile, tile, tile, pl.BlockSpec((None, TQ, LANES), lambda p, b: (p, b, 0)), bias_spec],
        out_specs=[pl.BlockSpec((4, s, LANES), lambda p, b: (0, 0, p)), bias_spec],
        out_shape=[jax.ShapeDtypeStruct((4, s, D_MODEL), BF16),
                   jax.ShapeDtypeStruct((HEADS, TQ, A_WIN), F32)],
        scratch_shapes=[pltpu.VMEM((s, LANES), F32), pltpu.VMEM((s, LANES), F32)],
        compiler_params=_params(("parallel", "arbitrary")),
    )(zqkv, *([zqkv] * (2 * A_KBLOCKS)), g, o, du, lse, bias)


def _attn_b_fwd(qb, kv, gate, bias, sinks):
    s = qb.shape[0]
    nb = s // TQ
    left = B_KBLOCKS - 1
    kvw = B_KV_HEADS * HEAD_DIM

    def body(q_ref, *refs):
        kv_refs = refs[:B_KBLOCKS]
        g_ref, bias_ref, sink_ref, o_ref, u_ref, lse_ref = refs[B_KBLOCKS:]
        b = pl.program_id(0)
        kvv = jnp.concatenate([r[...] for r in kv_refs], axis=0)
        valid = _band_valid(b, left, B_WIN)
        lane = lax.broadcasted_iota(jnp.int32, (TQ, LANES), 1)
        lse_tile = jnp.zeros((TQ, LANES), F32)
        pair = None
        for h in range(HEADS):
            kvh = h // B_GROUP
            sl = slice(h * HEAD_DIM, (h + 1) * HEAD_DIM)
            ksl = slice(kvh * HEAD_DIM, (kvh + 1) * HEAD_DIM)
            vsl = slice(kvw + kvh * HEAD_DIM, kvw + (kvh + 1) * HEAD_DIM)
            sink = sink_ref[0:1, h:h + 1]
            sc = lax.dot_general(q_ref[:, sl], kvv[:, ksl], NT, preferred_element_type=F32)
            sc = sc * SCALE + bias_ref[h]
            sc = jnp.where(valid, sc, MASKED)
            m = jnp.maximum(jnp.max(sc, axis=-1, keepdims=True), sink)
            e = jnp.exp(sc - m)
            l = jnp.sum(e, axis=-1, keepdims=True) + jnp.exp(sink - m)
            p = (e * (1.0 / l)).astype(BF16)
            oh = jnp.dot(p, kvv[:, vsl], preferred_element_type=F32)
            if h % 2 == 0:
                pair = oh
            else:
                o_ref[:, (h - 1) * HEAD_DIM:(h + 1) * HEAD_DIM] = jnp.concatenate([pair, oh], axis=-1)
            lse_tile = jnp.where(lane == h, m + jnp.log(l), lse_tile)
        lse_ref[...] = lse_tile
        sg, _ = _silu_parts(g_ref[...])
        u_ref[...] = (o_ref[...] * sg).astype(BF16)

    kv_specs = [pl.BlockSpec((KB, 2 * kvw), functools.partial(
        lambda b, t: (jnp.maximum(b - left + t, 0), 0), t=t)) for t in range(B_KBLOCKS)]
    row = pl.BlockSpec((TQ, D_MODEL), lambda b: (b, 0))
    return pl.pallas_call(
        body, name="attn_b_fwd", grid=(nb,),
        in_specs=[row] + kv_specs + [row, pl.BlockSpec((HEADS, TQ, B_WIN), lambda b: (0, 0, 0)),
                                     pl.BlockSpec((1, HEADS), lambda b: (0, 0))],
        out_specs=[row, row, pl.BlockSpec((TQ, LANES), lambda b: (b, 0))],
        out_shape=[jax.ShapeDtypeStruct((s, D_MODEL), F32), jax.ShapeDtypeStruct((s, D_MODEL), BF16),
                   jax.ShapeDtypeStruct((s, LANES), F32)],
        compiler_params=_params(("parallel",)),
    )(qb, *([kv] * B_KBLOCKS), gate, bias, sinks)


def _attn_b_bwd(qb, kv, gate, o, du, lse, bias, sinks):
    s = qb.shape[0]
    nb = s // TQ
    left = B_KBLOCKS - 1
    kvw = B_KV_HEADS * HEAD_DIM
    half = D_MODEL // 2

    def body(q_ref, *refs):
        kv_refs = refs[:B_KBLOCKS]
        (g_ref, o_ref, du_ref, lse_ref, bias_ref, sink_ref,
         dz_ref, dkv_ref, dbias_ref, dsink_ref, dkv_acc, dq_buf, dsink_acc) = refs[B_KBLOCKS:]
        b = pl.program_id(0)

        @pl.when(b == 0)
        def _():
            dkv_acc[...] = jnp.zeros_like(dkv_acc)
            dbias_ref[...] = jnp.zeros_like(dbias_ref)
            dsink_acc[...] = jnp.zeros_like(dsink_acc)

        kvv = jnp.concatenate([r[...] for r in kv_refs], axis=0)
        valid = _band_valid(b, left, B_WIN)
        lane = lax.broadcasted_iota(jnp.int32, (TQ, LANES), 1)
        sg, dsg = _silu_parts(g_ref[...])
        duv = du_ref[...]
        ov = o_ref[...]
        do = duv * sg
        dgate = (duv * ov * dsg).astype(BF16)
        dz_ref[2] = dgate[:, :half]
        dz_ref[3] = dgate[:, half:]
        do_o = do * ov
        do_bf = do.astype(BF16)
        dsink_tile = jnp.zeros((TQ, LANES), F32)
        pair = None
        dkg = [None] * B_KV_HEADS
        dvg = [None] * B_KV_HEADS
        for h in range(HEADS):
            kvh = h // B_GROUP
            sl = slice(h * HEAD_DIM, (h + 1) * HEAD_DIM)
            ksl = slice(kvh * HEAD_DIM, (kvh + 1) * HEAD_DIM)
            vsl = slice(kvw + kvh * HEAD_DIM, kvw + (kvh + 1) * HEAD_DIM)
            sink = sink_ref[0:1, h:h + 1]
            lse = lse_ref[:, h:h + 1]
            qh = q_ref[:, sl]
            sc = lax.dot_general(qh, kvv[:, ksl], NT, preferred_element_type=F32)
            sc = sc * SCALE + bias_ref[h]
            sc = jnp.where(valid, sc, MASKED)
            p = jnp.exp(sc - lse)
            dp = lax.dot_general(do_bf[:, sl], kvv[:, vsl], NT, preferred_element_type=F32)
            delta = jnp.sum(do_o[:, sl], axis=-1, keepdims=True)
            ds = p * (dp - delta)
            dbias_ref[h] += ds
            dsink_tile = jnp.where(lane == h, -jnp.exp(sink - lse) * delta, dsink_tile)
            dsq = (ds * SCALE).astype(BF16)
            dqh = jnp.dot(dsq, kvv[:, ksl], preferred_element_type=F32)
            if h % 2 == 0:
                pair = dqh
            else:
                dq_buf[:, (h - 1) * HEAD_DIM:(h + 1) * HEAD_DIM] = jnp.concatenate([pair, dqh], axis=-1).astype(BF16)
            dk = lax.dot_general(dsq, qh, TN, preferred_element_type=F32)
            dv = lax.dot_general(p.astype(BF16), do_bf[:, sl], TN, preferred_element_type=F32)
            dkg[kvh] = dk if dkg[kvh] is None else dkg[kvh] + dk
            dvg[kvh] = dv if dvg[kvh] is None else dvg[kvh] + dv
        dz_ref[0] = dq_buf[:, :half]
        dz_ref[1] = dq_buf[:, half:]
        dsink_acc[...] += dsink_tile
        dkv = jnp.concatenate(dkg + dvg, axis=-1)
        for t in range(B_KBLOCKS):
            krows = pl.ds(pl.multiple_of(jnp.maximum(b - left + t, 0) * KB, KB), KB)
            dkv_acc[krows, :] += dkv[t * KB:(t + 1) * KB, :]

        @pl.when(b == nb - 1)
        def _():
            dkv_ref[...] = dkv_acc[...].astype(BF16)
            tot = jnp.sum(dsink_acc[...], axis=0, keepdims=True)
            dsink_ref[...] = jnp.broadcast_to(tot, dsink_ref.shape)

    kv_specs = [pl.BlockSpec((KB, 2 * kvw), functools.partial(
        lambda b, t: (jnp.maximum(b - left + t, 0), 0), t=t)) for t in range(B_KBLOCKS)]
    row = pl.BlockSpec((TQ, D_MODEL), lambda b: (b, 0))
    bias_spec = pl.BlockSpec((HEADS, TQ, B_WIN), lambda b: (0, 0, 0))
    return pl.pallas_call(
        body, name="attn_b_bwd", grid=(nb,),
        in_specs=[row] + kv_specs + [row, row, row, pl.BlockSpec((TQ, LANES), lambda b: (b, 0)), bias_spec,
                                     pl.BlockSpec((1, HEADS), lambda b: (0, 0))],
        out_specs=[pl.BlockSpec((4, TQ, half), lambda b: (0, b, 0)),
                   pl.BlockSpec((s, 2 * kvw), lambda b: (0, 0)), bias_spec,
                   pl.BlockSpec((8, LANES), lambda b: (0, 0))],
        out_shape=[jax.ShapeDtypeStruct((4, s, half), BF16), jax.ShapeDtypeStruct((s, 2 * kvw), BF16),
                   jax.ShapeDtypeStruct((HEADS, TQ, B_WIN), F32), jax.ShapeDtypeStruct((8, LANES), F32)],
        scratch_shapes=[pltpu.VMEM((s, 2 * kvw), F32), pltpu.VMEM((TQ, D_MODEL), BF16),
                        pltpu.VMEM((TQ, LANES), F32)],
        compiler_params=_params(("arbitrary",)),
    )(qb, *([kv] * B_KBLOCKS), gate, o, du, lse, bias, sinks)


def _t5_bucket(rel):
    nb = T5_BUCKETS // 2
    max_exact = nb // 2
    ret = jnp.where(rel > 0, nb, 0)
    n = jnp.abs(rel)
    nf = jnp.maximum(n, 1).astype(jnp.float32)
    large = max_exact + (jnp.log(nf / max_exact) / math.log(T5_MAX_DIST / max_exact)
                         * (nb - max_exact)).astype(jnp.int32)
    large = jnp.minimum(large, nb - 1)
    return ret + jnp.where(n < max_exact, n, large)


def _band_geometry(win, left_chunks):
    i = np.arange(TQ)[:, None]
    j = np.arange(win)[None, :]
    dist = i + left_chunks * CHUNK - j
    dc = i // CHUNK + left_chunks - j // CHUNK
    return dist, (dc >= 0) & (dc <= left_chunks)


def _a_bias_table(rel_bias):
    dist, valid = _band_geometry(A_WIN, A_LEFT_CHUNKS)
    idx = np.clip(dist, -A_REL_CLIP, A_REL_CLIP) + A_REL_CLIP
    tab = jnp.transpose(rel_bias[idx], (2, 0, 1)).astype(F32)
    return jnp.where(valid[None], tab, MASKED)


def _a_bias_grad(dtab):
    dist, valid = _band_geometry(A_WIN, A_LEFT_CHUNKS)
    idx = np.clip(dist, -A_REL_CLIP, A_REL_CLIP) + A_REL_CLIP
    dtab = jnp.where(valid[None], dtab, 0.0)
    wide = A_WIN + TQ
    padded = jnp.pad(dtab[:, ::-1, :], ((0, 0), (0, 0), (0, TQ)))
    flat = padded.reshape(HEADS, TQ * wide)
    skew = flat[:, :TQ * (wide - 1)].reshape(HEADS, TQ, wide - 1)
    diag = jnp.sum(skew, axis=1)
    c = np.arange(wide - 1)
    d = A_LEFT_CHUNKS * CHUNK - (c - (TQ - 1))
    ridx = np.clip(d, -A_REL_CLIP, A_REL_CLIP) + A_REL_CLIP
    onehot = np.zeros((wide - 1, 2 * A_REL_CLIP + 1), np.float32)
    onehot[c, ridx] = 1.0
    return jnp.dot(diag, jnp.asarray(onehot), precision=lax.Precision.HIGHEST).T


def _b_bias_index():
    dist, valid = _band_geometry(B_WIN, B_LEFT_CHUNKS)
    return _t5_bucket(jnp.asarray(-dist, jnp.int32)), valid


def _b_bias_table(t5_table):
    bucket, valid = _b_bias_index()
    tab = jnp.transpose(t5_table[bucket], (2, 0, 1)).astype(F32)
    return jnp.where(valid[None], tab, MASKED)


def _b_bias_grad(dtab):
    bucket, valid = _b_bias_index()
    onehot = (bucket[..., None] == jnp.arange(T5_BUCKETS)).astype(F32) * valid[..., None]
    return jnp.einsum("hij,ijb->bh", dtab, onehot, precision=lax.Precision.HIGHEST)


def _position():
    x, y, c = lax.axis_index("x"), lax.axis_index("y"), lax.axis_index("c")
    chips = [(1 - x, y), (x, 1 - y), (1 - x, 1 - y)]
    return x, y, c, chips


ANY = pl.BlockSpec(memory_space=pl.ANY)


def _allgather_weights(shards, split):
    n = len(shards)

    def body(*refs):
        ins, outs = refs[:n], refs[n:2 * n]
        send_sems, recv_sems, pass_send, pass_recv, local_sems = refs[2 * n:]
        x, y, c, chips = _position()
        mine = 2 * x + y
        sibling = (x, y, 1 - c)

        def part(ref, t, half):
            if not split[t]:
                return ref
            rows = shards[t].shape[0] // 2
            return ref.at[pl.ds(half * rows, rows)]

        local = [pltpu.make_async_copy(ins[t], outs[t].at[mine], local_sems.at[t]) for t in range(n)]
        for cp in local:
            cp.start()
        sends = []
        for t in range(n):
            for j, chip in enumerate(chips):
                sends.append(pltpu.make_async_remote_copy(
                    src_ref=part(ins[t], t, c), dst_ref=part(outs[t].at[mine], t, c),
                    send_sem=send_sems.at[3 * t + j], recv_sem=recv_sems.at[3 * t + j],
                    device_id=(chip[0], chip[1], c), device_id_type=MESH))
        for cp in sends:
            cp.start()
        passes = []
        for t in range(n):
            for j, chip in enumerate(chips):
                theirs = 2 * chip[0] + chip[1]
                landed = part(outs[t].at[theirs], t, c)
                pltpu.make_async_remote_copy(
                    src_ref=landed, dst_ref=landed, send_sem=send_sems.at[3 * t + j],
                    recv_sem=recv_sems.at[3 * t + j], device_id=(chip[0], chip[1], c),
                    device_id_type=MESH).wait_recv()
                if split[t]:
                    cp = pltpu.make_async_remote_copy(
                        src_ref=landed, dst_ref=landed, send_sem=pass_send.at[3 * t + j],
                        recv_sem=pass_recv.at[3 * t + j], device_id=sibling, device_id_type=MESH)
                    cp.start()
                    passes.append(cp)
        for t in range(n):
            if not split[t]:
                continue
            for j, chip in enumerate(chips):
                theirs = 2 * chip[0] + chip[1]
                other = part(outs[t].at[theirs], t, 1 - c)
                pltpu.make_async_remote_copy(
                    src_ref=other, dst_ref=other, send_sem=pass_send.at[3 * t + j],
                    recv_sem=pass_recv.at[3 * t + j], device_id=sibling, device_id_type=MESH).wait_recv()
        for cp in sends + passes:
            cp.wait_send()
        for cp in local:
            cp.wait()

    return pl.pallas_call(
        body, name="allgather_weights",
        in_specs=[ANY] * n, out_specs=[ANY] * n,
        out_shape=[jax.ShapeDtypeStruct((4,) + w.shape, w.dtype) for w in shards],
        scratch_shapes=[pltpu.SemaphoreType.DMA((3 * n,))] * 4 + [pltpu.SemaphoreType.DMA((n,))],
    )(*shards)


def _scatter_partials(grads):
    n = len(grads)

    def body(*refs):
        ins, outs = refs[:n], refs[n:2 * n]
        send_sems, recv_sems = refs[2 * n:]
        x, y, c, chips = _position()
        sends = []
        for t in range(n):
            for j, chip in enumerate(chips):
                sends.append(pltpu.make_async_remote_copy(
                    src_ref=ins[t].at[2 * chip[0] + chip[1]], dst_ref=outs[t].at[j],
                    send_sem=send_sems.at[3 * t + j], recv_sem=recv_sems.at[3 * t + j],
                    device_id=(chip[0], chip[1], c), device_id_type=MESH))
        for cp in sends:
            cp.start()
        for cp in sends:
            cp.wait()

    return pl.pallas_call(
        body, name="scatter_partials",
        in_specs=[ANY] * n, out_specs=[ANY] * n,
        out_shape=[jax.ShapeDtypeStruct((3,) + g.shape[1:], g.dtype) for g in grads],
        scratch_shapes=[pltpu.SemaphoreType.DMA((3 * n,))] * 2,
    )(*grads)


def _swap_with_sibling(blocks):
    n = len(blocks)

    def body(*refs):
        ins, outs = refs[:n], refs[n:2 * n]
        send_sems, recv_sems = refs[2 * n:]
        x, y, c, _ = _position()
        sends = [pltpu.make_async_remote_copy(
            src_ref=ins[t], dst_ref=outs[t], send_sem=send_sems.at[t], recv_sem=recv_sems.at[t],
            device_id=(x, y, 1 - c), device_id_type=MESH) for t in range(n)]
        for cp in sends:
            cp.start()
        for cp in sends:
            cp.wait()

    return pl.pallas_call(
        body, name="swap_with_sibling",
        in_specs=[ANY] * n, out_specs=[ANY] * n,
        out_shape=[jax.ShapeDtypeStruct(b.shape, b.dtype) for b in blocks],
        scratch_shapes=[pltpu.SemaphoreType.DMA((n,))] * 2,
    )(*blocks)


def _allreduce_small(block):
    rows = block.shape[0]

    def body(in_ref, sum_ref, all_ref, send_sems, recv_sems):
        x, y, c, _ = _position()
        me = 4 * x + 2 * y + c
        all_ref[me] = in_ref[...]
        sends = []
        for k in range(1, 8):
            peer = (x ^ (k >> 2), y ^ ((k >> 1) & 1), c ^ (k & 1))
            sends.append(pltpu.make_async_remote_copy(
                src_ref=in_ref, dst_ref=all_ref.at[me], send_sem=send_sems.at[k - 1],
                recv_sem=recv_sems.at[k - 1], device_id=peer, device_id_type=MESH))
        for cp in sends:
            cp.start()
        for k in range(1, 8):
            theirs = me ^ k
            pltpu.make_async_remote_copy(
                src_ref=in_ref, dst_ref=all_ref.at[theirs], send_sem=send_sems.at[k - 1],
                recv_sem=recv_sems.at[k - 1], device_id=(x, y, c), device_id_type=MESH).wait_recv()
        for cp in sends:
            cp.wait_send()
        acc = all_ref[0]
        for d in range(1, 8):
            acc = acc + all_ref[d]
        sum_ref[...] = acc

    vmem = pl.BlockSpec(memory_space=pltpu.VMEM)
    return pl.pallas_call(
        body, name="allreduce_small",
        in_specs=[vmem], out_specs=[vmem, vmem],
        out_shape=[jax.ShapeDtypeStruct((rows, LANES), F32), jax.ShapeDtypeStruct((8, rows, LANES), F32)],
        scratch_shapes=[pltpu.SemaphoreType.DMA((7,))] * 2,
    )(block)[0]


def _adamw_math(w, g, m, v):
    m = ADAM_B1 * m + (1.0 - ADAM_B1) * g
    v = ADAM_B2 * v + (1.0 - ADAM_B2) * (g * g)
    m_hat = m / (1.0 - ADAM_B1 ** ADAM_STEP)
    v_hat = v / (1.0 - ADAM_B2 ** ADAM_STEP)
    delta = -ADAM_LR * (m_hat / (jnp.sqrt(v_hat) + ADAM_EPS) + ADAM_WD * w)
    return delta, m, v


def _row_tile(rows):
    return min(rows, 256)


def _sum_partials(name, own, recv):
    rows, cols = own.shape
    tr = _row_tile(rows)

    def body(own_ref, recv_ref, o_ref):
        acc = own_ref[...]
        for j in range(3):
            acc = acc + recv_ref[j].astype(F32)
        o_ref[...] = acc

    return pl.pallas_call(
        body, name=name, grid=(rows // tr,),
        in_specs=[pl.BlockSpec((tr, cols), lambda i: (i, 0)), pl.BlockSpec((3, tr, cols), lambda i: (0, i, 0))],
        out_specs=pl.BlockSpec((tr, cols), lambda i: (i, 0)),
        out_shape=jax.ShapeDtypeStruct((rows, cols), F32),
        compiler_params=_params(("parallel",)),
    )(own, recv)


def _adamw(name, w, m, v, g_parts):
    rows, cols = w.shape
    tr = _row_tile(rows)
    n = len(g_parts)

    def body(w_ref, m_ref, v_ref, *refs):
        g_refs = refs[:n]
        go_ref, d_ref, mo_ref, vo_ref = refs[n:]
        g = g_refs[0][...]
        for r in g_refs[1:]:
            g = g + r[...]
        delta, mn, vn = _adamw_math(w_ref[...], g, m_ref[...], v_ref[...])
        go_ref[...] = g
        d_ref[...] = delta
        mo_ref[...] = mn
        vo_ref[...] = vn

    spec = pl.BlockSpec((tr, cols), lambda i: (i, 0))
    return pl.pallas_call(
        body, name=name, grid=(rows // tr,),
        in_specs=[spec] * (3 + n), out_specs=[spec] * 4,
        out_shape=[jax.ShapeDtypeStruct((rows, cols), F32)] * 4,
        compiler_params=_params(("parallel",)),
    )(w, m, v, *g_parts)


def _local_step(x, target, ga, wa_in, rel_bias, wa_out, gk, wkv, t5, gb, wb_in, sinks, wb_out, gf):
    s, d = x.shape
    nt = s // TM
    half = d // 2
    row = pl.BlockSpec((TM, d), lambda i: (i, 0))
    whole = lambda shape: pl.BlockSpec(shape, lambda *_: (0,) * len(shape))

    n1, = _norm_fwd("norm_a", x, ga)
    zqkv = _matmul("proj_a_qkv", n1, wa_in, dims=NN, grid=(3, nt),
                   a_spec=pl.BlockSpec((TM, d), lambda j, i: (i, 0)),
                   b_spec=pl.BlockSpec((None, d, d), lambda j, i: (j, 0, 0)),
                   o_spec=pl.BlockSpec((None, TM, d), lambda j, i: (j, i, 0)),
                   out_shape=(3, s, d), out_dtype=BF16)
    gate_a = _matmul("proj_a_gate", n1, wa_in, dims=NN, grid=(nt,),
                     a_spec=row, b_spec=pl.BlockSpec((None, d, d), lambda i: (3, 0, 0)), o_spec=row,
                     out_shape=(s, d), out_dtype=F32)
    bias_a = _a_bias_table(rel_bias)
    o_a, u_a, lse_a = _attn_a_fwd(zqkv, gate_a, bias_a)
    h1 = _matmul("out_a", u_a, wa_out, dims=NN, grid=(nt,), a_spec=row, b_spec=whole((d, d)), o_spec=row,
                 out_shape=(s, d), out_dtype=F32, resid=x, resid_spec=row)

    nk, n2 = _norm_fwd("norm_kv_b", h1, jnp.concatenate([gk, gb], axis=0))
    kvw = wkv.shape[1]
    kv = _matmul("proj_kv", nk, wkv, dims=NN, grid=(nt,), a_spec=row, b_spec=whole((d, kvw)),
                 o_spec=pl.BlockSpec((TM, kvw), lambda i: (i, 0)), out_shape=(s, kvw), out_dtype=BF16)
    qb = _matmul("proj_b_q", n2, wb_in, dims=NN, grid=(2, nt),
                 a_spec=pl.BlockSpec((TM, d), lambda j, i: (i, 0)),
                 b_spec=pl.BlockSpec((None, d, half), lambda j, i: (j, 0, 0)),
                 o_spec=pl.BlockSpec((TM, half), lambda j, i: (i, j)), out_shape=(s, d), out_dtype=BF16)
    gate_b = _matmul("proj_b_gate", n2, wb_in, dims=NN, grid=(2, nt),
                     a_spec=pl.BlockSpec((TM, d), lambda j, i: (i, 0)),
                     b_spec=pl.BlockSpec((None, d, half), lambda j, i: (2 + j, 0, 0)),
                     o_spec=pl.BlockSpec((TM, half), lambda j, i: (i, j)), out_shape=(s, d), out_dtype=F32)
    bias_b = _b_bias_table(t5)
    o_b, u_b, lse_b = _attn_b_fwd(qb, kv, gate_b, bias_b, sinks)
    h2 = _matmul("out_b", u_b, wb_out, dims=NN, grid=(nt,), a_spec=row, b_spec=whole((d, d)), o_spec=row,
                 out_shape=(s, d), out_dtype=F32, resid=h1, resid_spec=row)

    dh2, loss, d_gf = _loss_head(h2, target, gf)

    du_b = _matmul("dout_b", dh2, wb_out, dims=NT, grid=(nt,), a_spec=row, b_spec=whole((d, d)), o_spec=row,
                   out_shape=(s, d), out_dtype=F32)
    d_wb_out = _matmul("dw_out_b", u_b, dh2, dims=TN, grid=(2, nt),
                       a_spec=pl.BlockSpec((TM, d), lambda j, k: (k, 0)),
                       b_spec=pl.BlockSpec((TM, half), lambda j, k: (k, j)),
                       o_spec=pl.BlockSpec((d, half), lambda j, k: (0, j)),
                       out_shape=(d, d), out_dtype=F32, k_axis=1, k_steps=nt)
    dz_b, dkv, dbias_b, dsinks = _attn_b_bwd(qb, kv, gate_b, o_b, du_b, lse_b, bias_b, sinks)
    dn2 = _matmul("dproj_b", dz_b, wb_in, dims=NT, grid=(nt, 4),
                  a_spec=pl.BlockSpec((None, TM, half), lambda i, k: (k, i, 0)),
                  b_spec=pl.BlockSpec((None, d, half), lambda i, k: (k, 0, 0)),
                  o_spec=pl.BlockSpec((TM, d), lambda i, k: (i, 0)),
                  out_shape=(s, d), out_dtype=F32, k_axis=1, k_steps=4)
    d_wb_in = _matmul("dw_in_b", n2, dz_b, dims=TN, grid=(4, nt),
                      a_spec=pl.BlockSpec((TM, d), lambda j, k: (k, 0)),
                      b_spec=pl.BlockSpec((None, TM, half), lambda j, k: (j, k, 0)),
                      o_spec=pl.BlockSpec((None, d, half), lambda j, k: (j, 0, 0)),
                      out_shape=(4, d, half), out_dtype=F32, k_axis=1, k_steps=nt)
    dnk = _matmul("dproj_kv", dkv, wkv, dims=NT, grid=(nt,),
                  a_spec=pl.BlockSpec((TM, kvw), lambda i: (i, 0)), b_spec=whole((d, kvw)), o_spec=row,
                  out_shape=(s, d), out_dtype=F32)
    d_wkv = _matmul("dw_kv", nk, dkv, dims=TN, grid=(nt,),
                    a_spec=pl.BlockSpec((TM, d), lambda k: (k, 0)),
                    b_spec=pl.BlockSpec((TM, kvw), lambda k: (k, 0)),
                    o_spec=pl.BlockSpec((d, kvw), lambda k: (0, 0)),
                    out_shape=(d, kvw), out_dtype=F32, k_axis=0, k_steps=nt)
    dh1, d_gkb = _norm_bwd("dnorm_kv_b", h1, dh2, [dnk, dn2], jnp.concatenate([gk, gb], axis=0))

    du_a = _matmul("dout_a", dh1, wa_out, dims=NT, grid=(nt,), a_spec=row, b_spec=whole((d, d)), o_spec=row,
                   out_shape=(s, d), out_dtype=F32)
    d_wa_out = _matmul("dw_out_a", u_a, dh1, dims=TN, grid=(2, nt),
                       a_spec=pl.BlockSpec((TM, d), lambda j, k: (k, 0)),
                       b_spec=pl.BlockSpec((TM, half), lambda j, k: (k, j)),
                       o_spec=pl.BlockSpec((d, half), lambda j, k: (0, j)),
                       out_shape=(d, d), out_dtype=F32, k_axis=1, k_steps=nt)
    dz_a, dbias_a = _attn_a_bwd(zqkv, gate_a, o_a, du_a, lse_a, bias_a)
    dn1 = _matmul("dproj_a", dz_a, wa_in, dims=NT, grid=(nt, 4),
                  a_spec=pl.BlockSpec((None, TM, d), lambda i, k: (k, i, 0)),
                  b_spec=pl.BlockSpec((None, d, d), lambda i, k: (k, 0, 0)),
                  o_spec=pl.BlockSpec((TM, d), lambda i, k: (i, 0)),
                  out_shape=(s, d), out_dtype=F32, k_axis=1, k_steps=4)
    d_wa_in = _matmul("dw_in_a", n1, dz_a, dims=TN, grid=(4, 2, nt),
                      a_spec=pl.BlockSpec((TM, d), lambda j, h, k: (k, 0)),
                      b_spec=pl.BlockSpec((None, TM, half), lambda j, h, k: (j, k, h)),
                      o_spec=pl.BlockSpec((None, d, half), lambda j, h, k: (j, 0, h)),
                      out_shape=(4, d, d), out_dtype=F32, k_axis=2, k_steps=nt)
    grad_x, d_ga = _norm_bwd("dnorm_a", x, dh1, [dn1], ga)

    small = dict(
        a_norm=d_ga, a_rel_bias=_a_bias_grad(dbias_a), kv_norm=d_gkb[0:1], t5_bias=_b_bias_grad(dbias_b),
        b_norm=d_gkb[1:2], b_sinks=dsinks[0:1, :HEADS], final_norm=d_gf)
    big = dict(a_w_in=d_wa_in, a_w_out=d_wa_out, kv_w=d_wkv, b_w_in=d_wb_in, b_w_out=d_wb_out)
    return loss, grad_x, small, big


SMALL = ("a_norm", "a_rel_bias", "kv_norm", "t5_bias", "b_norm", "b_sinks", "final_norm")
BIG = ("a_w_in", "a_w_out", "kv_w", "b_w_in", "b_w_out")
ORDER = ("a_norm", "a_w_in", "a_rel_bias", "a_w_out", "kv_norm", "kv_w", "t5_bias", "b_norm", "b_w_in",
         "b_sinks", "b_w_out", "final_norm")


def _pack(parts, rows):
    flat = jnp.concatenate([p.reshape(-1).astype(F32) for p in parts])
    return jnp.pad(flat, (0, rows * LANES - flat.shape[0])).reshape(rows, LANES)


def _unpack(block, shapes):
    flat = block.reshape(-1)
    out, at = [], 0
    for shp in shapes:
        size = int(np.prod(shp))
        out.append(flat[at:at + size].reshape(shp))
        at += size
    return out


def kernel(x, a_norm, a_w_in, a_rel_bias, a_w_out, kv_norm, kv_w, t5_bias, b_norm, b_w_in, b_sinks, b_w_out, final_norm, loss_target, m_a_norm, m_a_w_in, m_a_rel_bias, m_a_w_out, m_kv_norm, m_kv_w, m_t5_bias, m_b_norm, m_b_w_in, m_b_sinks, m_b_w_out, m_final_norm, v_a_norm, v_a_w_in, v_a_rel_bias, v_a_w_out, v_kv_norm, v_kv_w, v_t5_bias, v_b_norm, v_b_w_in, v_b_sinks, v_b_w_out, v_final_norm):
    w = dict(a_norm=a_norm, a_w_in=a_w_in, a_rel_bias=a_rel_bias, a_w_out=a_w_out, kv_norm=kv_norm, kv_w=kv_w,
             t5_bias=t5_bias, b_norm=b_norm, b_w_in=b_w_in, b_sinks=b_sinks, b_w_out=b_w_out,
             final_norm=final_norm)
    m = dict(a_norm=m_a_norm, a_w_in=m_a_w_in, a_rel_bias=m_a_rel_bias, a_w_out=m_a_w_out, kv_norm=m_kv_norm,
             kv_w=m_kv_w, t5_bias=m_t5_bias, b_norm=m_b_norm, b_w_in=m_b_w_in, b_sinks=m_b_sinks,
             b_w_out=m_b_w_out, final_norm=m_final_norm)
    v = dict(a_norm=v_a_norm, a_w_in=v_a_w_in, a_rel_bias=v_a_rel_bias, a_w_out=v_a_w_out, kv_norm=v_kv_norm,
             kv_w=v_kv_w, t5_bias=v_t5_bias, b_norm=v_b_norm, b_w_in=v_b_w_in, b_sinks=v_b_sinks,
             b_w_out=v_b_w_out, final_norm=v_final_norm)
    d = D_MODEL
    chip = 2 * lax.axis_index("x") + lax.axis_index("y")

    shard2d = dict(a_w_in=a_w_in[0], a_w_out=a_w_out[0], kv_w=kv_w, b_w_in=b_w_in[0], b_w_out=b_w_out[0])

    gathered = _allgather_weights(
        [shard2d[n].astype(BF16) for n in BIG] + [a_norm],
        [True] * len(BIG) + [False])
    wa_in, wa_out, wkv, wb_in, wb_out, ga = gathered
    ga = ga.reshape(1, d)

    loss, grad_x, small, big = _local_step(
        x[0], loss_target[0], ga, wa_in, a_rel_bias[0], wa_out.reshape(d, d), kv_norm.reshape(1, d),
        wkv.reshape(d, -1), t5_bias, b_norm, wb_in, b_sinks, wb_out.reshape(d, d), final_norm.reshape(1, d))

    small_shapes = [small[n].shape for n in SMALL] + [(1, 1)]
    total = sum(int(np.prod(s)) for s in small_shapes)
    rows = -(-total // (8 * LANES)) * 8
    reduced = _unpack(_allreduce_small(_pack([small[n] for n in SMALL] + [loss], rows)), small_shapes)
    g_small = dict(zip(SMALL, reduced[:-1]))
    loss_out = reduced[-1].reshape(())
    g_small["a_norm"] = lax.dynamic_slice_in_dim(g_small["a_norm"], chip * (d // 4), d // 4, axis=1)

    shard_major = dict(a_w_in=big["a_w_in"], a_w_out=big["a_w_out"].reshape(4, d // 4, d),
                       kv_w=big["kv_w"].reshape(4, d // 4, -1), b_w_in=big["b_w_in"],
                       b_w_out=big["b_w_out"].reshape(4, d // 4, d))
    received = _scatter_partials([shard_major[n].astype(BF16) for n in BIG])
    core_sums = [
        _sum_partials("sum_" + n, lax.dynamic_index_in_dim(shard_major[n], chip, 0, keepdims=False), r)
        for n, r in zip(BIG, received)]
    sibling_sums = _swap_with_sibling(core_sums)

    out = {}
    for n, mine, theirs in zip(BIG, core_sums, sibling_sums):
        res = _adamw("adamw_" + n, shard2d[n], m[n].reshape(shard2d[n].shape), v[n].reshape(shard2d[n].shape),
                     [mine, theirs])
        out[n] = [r.reshape(w[n].shape) for r in res]
    small_w_shapes = [w[n].shape for n in SMALL]
    total_w = sum(int(np.prod(s)) for s in small_w_shapes)
    rows_w = -(-total_w // (8 * LANES)) * 8
    packed = [_pack([t[n] for n in SMALL], rows_w) for t in (w, m, v)]
    g_packed = _pack([g_small[n] for n in SMALL], rows_w)
    res = _adamw("adamw_small", packed[0], packed[1], packed[2], [g_packed])
    unpacked = [_unpack(r, small_w_shapes) for r in res]
    for i, n in enumerate(SMALL):
        out[n] = [unpacked[k][i] for k in range(4)]

    grads = [out[n][0] for n in ORDER]
    deltas = [out[n][1] for n in ORDER]
    new_m = [out[n][2] for n in ORDER]
    new_v = [out[n][3] for n in ORDER]
    return (loss_out, grad_x[None], *grads, *deltas, *new_m, *new_v)
```

```python
import functools
import math

import jax
import jax.numpy as jnp
import numpy as np
from jax import lax
from jax.experimental import pallas as pl
from jax.experimental.pallas import tpu as pltpu

F32 = jnp.float32
BF16 = jnp.bfloat16
MESH = pl.DeviceIdType.MESH

D_MODEL = 1024
HEADS = 16
HEAD_DIM = 64
CHUNK = 64
RMS_EPS = 1e-6
SCALE = HEAD_DIM ** -0.5
A_LEFT_CHUNKS = 8
A_REL_CLIP = 256
B_LEFT_CHUNKS = 2
B_KV_HEADS = 2
B_GROUP = HEADS // B_KV_HEADS
T5_BUCKETS = 32
T5_MAX_DIST = 128
ADAM_LR = 0.001
ADAM_B1 = 0.9
ADAM_B2 = 0.999
ADAM_EPS = 1e-08
ADAM_WD = 0.01
ADAM_STEP = 10

MASKED = -1e30
LANES = 128
TQ = 128
A_PAIRS = 2
KB = 128
A_KBLOCKS = A_LEFT_CHUNKS * CHUNK // KB + 1
B_KBLOCKS = B_LEFT_CHUNKS * CHUNK // KB + 1
A_WIN = A_KBLOCKS * KB
B_WIN = B_KBLOCKS * KB
TM = 512
VMEM_LIMIT = 56 * 1024 * 1024

NT = (((1,), (1,)), ((), ()))
TN = (((0,), (0,)), ((), ()))
NN = (((1,), (0,)), ((), ()))


def _params(sem=None):
    return pltpu.CompilerParams(dimension_semantics=sem, vmem_limit_bytes=VMEM_LIMIT)


def _matmul(name, a, b, *, dims, grid, a_spec, b_spec, o_spec, out_shape, out_dtype,
            k_axis=None, k_steps=1, resid=None, resid_spec=None):
    if k_axis is not None:
        assert out_dtype == F32 and resid is None

    def body(*refs):
        if resid is not None:
            a_ref, b_ref, r_ref, o_ref = refs
        else:
            a_ref, b_ref, o_ref = refs
        prod = lax.dot_general(a_ref[...].astype(BF16), b_ref[...].astype(BF16), dims,
                               preferred_element_type=F32)
        if k_axis is None:
            if resid is not None:
                prod = r_ref[...] + prod
            o_ref[...] = prod.astype(out_dtype)
        else:
            k = pl.program_id(k_axis)

            @pl.when(k == 0)
            def _():
                o_ref[...] = prod

            @pl.when(k > 0)
            def _():
                o_ref[...] += prod

    in_specs = [a_spec, b_spec]
    args = [a, b]
    if resid is not None:
        in_specs.append(resid_spec)
        args.append(resid)
    sem = ["parallel"] * len(grid)
    if k_axis is not None:
        assert k_axis == len(grid) - 1 and grid[k_axis] == k_steps
        sem[k_axis] = "arbitrary"
    return pl.pallas_call(
        body, name=name, grid=grid, in_specs=in_specs, out_specs=o_spec,
        out_shape=jax.ShapeDtypeStruct(out_shape, out_dtype),
        compiler_params=_params(tuple(sem)),
    )(*args)


def _rms_rows(x):
    return lax.rsqrt(jnp.mean(x * x, axis=-1, keepdims=True) + RMS_EPS)


def _norm_fwd(name, x, gains):
    s, d = x.shape
    n = gains.shape[0]

    def body(x_ref, g_ref, *o_refs):
        xv = x_ref[...]
        xh = xv * _rms_rows(xv)
        for i in range(n):
            o_refs[i][...] = (xh * g_ref[i:i + 1, :]).astype(BF16)

    row = pl.BlockSpec((TM, d), lambda i: (i, 0))
    return pl.pallas_call(
        body, name=name, grid=(s // TM,),
        in_specs=[row, pl.BlockSpec((n, d), lambda i: (0, 0))],
        out_specs=[row] * n,
        out_shape=[jax.ShapeDtypeStruct((s, d), BF16)] * n,
        compiler_params=_params(("parallel",)),
    )(x, gains)


def _norm_bwd(name, x, dres, dns, gains):
    s, d = x.shape
    n = len(dns)

    def body(x_ref, r_ref, g_ref, *refs):
        dn_refs, dx_ref, dg_ref = refs[:n], refs[n], refs[n + 1]
        i = pl.program_id(0)
        xv = x_ref[...]
        r = _rms_rows(xv)
        xh = xv * r

        @pl.when(i == 0)
        def _():
            dg_ref[...] = jnp.zeros_like(dg_ref)

        a = None
        for j in range(n):
            dn = dn_refs[j][...]
            t = dn * g_ref[j:j + 1, :]
            a = t if a is None else a + t
            dg_ref[j:j + 1, :] += jnp.sum(dn * xh, axis=0, keepdims=True)
        dx_ref[...] = r_ref[...] + r * (a - xh * jnp.mean(xh * a, axis=-1, keepdims=True))

    row = pl.BlockSpec((TM, d), lambda i: (i, 0))
    small = pl.BlockSpec((n, d), lambda i: (0, 0))
    return pl.pallas_call(
        body, name=name, grid=(s // TM,),
        in_specs=[row, row, small] + [row] * n,
        out_specs=[row, small],
        out_shape=[jax.ShapeDtypeStruct((s, d), F32), jax.ShapeDtypeStruct((n, d), F32)],
        compiler_params=_params(("arbitrary",)),
    )(x, dres, gains, *dns)


def _loss_head(h2, target, gain):
    s, d = h2.shape

    def body(h_ref, t_ref, g_ref, dh_ref, loss_ref, dg_ref):
        i = pl.program_id(0)
        hv = h_ref[...]
        r = _rms_rows(hv)
        hh = hv * r
        g = g_ref[...]
        err = hh * g - t_ref[...]
        part = 0.5 * jnp.sum(jnp.sum(err * err, axis=-1, keepdims=True) * (1.0 / d), axis=0, keepdims=True)
        dy = err * (1.0 / d)
        a = dy * g
        dh_ref[...] = r * (a - hh * jnp.mean(hh * a, axis=-1, keepdims=True))
        dg = jnp.sum(dy * hh, axis=0, keepdims=True)

        @pl.when(i == 0)
        def _():
            loss_ref[...] = part
            dg_ref[...] = dg

        @pl.when(i > 0)
        def _():
            loss_ref[...] += part
            dg_ref[...] += dg

    row = pl.BlockSpec((TM, d), lambda i: (i, 0))
    return pl.pallas_call(
        body, name="loss_head", grid=(s // TM,),
        in_specs=[row, row, pl.BlockSpec((1, d), lambda i: (0, 0))],
        out_specs=[row, pl.BlockSpec((1, 1), lambda i: (0, 0)), pl.BlockSpec((1, d), lambda i: (0, 0))],
        out_shape=[jax.ShapeDtypeStruct((s, d), F32), jax.ShapeDtypeStruct((1, 1), F32),
                   jax.ShapeDtypeStruct((1, d), F32)],
        compiler_params=_params(("arbitrary",)),
    )(h2, target, gain)


def _silu_parts(g):
    sig = jax.nn.sigmoid(g)
    return g * sig, sig * (1.0 + g * (1.0 - sig))


def _lane_lo(rows):
    return lax.broadcasted_iota(jnp.int32, (rows, LANES), 1) < HEAD_DIM


def _stack_pair(x):
    lo = _lane_lo(x.shape[0])
    zero = jnp.zeros_like(x)
    return jnp.concatenate([jnp.where(lo, x, zero), jnp.where(lo, zero, x)], axis=0)


def _unstack_pair(y, w):
    return jnp.where(_lane_lo(w), y[:w], y[w:])


def _block_valid(b, left_blocks, width):
    col = lax.broadcasted_iota(jnp.int32, (1, 2 * width), 1)
    col = jnp.where(col >= width, col - width, col)
    return (col // KB + (b - left_blocks)) >= 0


def _toeplitz_tile(diag_row, width, left_chunks):
    wide = width + TQ
    rolled = pltpu.roll(jnp.broadcast_to(diag_row, (TQ, wide)), 1, 1, stride=1, stride_axis=0)
    i = lax.broadcasted_iota(jnp.int32, (TQ, width), 0) // CHUNK
    j = lax.broadcasted_iota(jnp.int32, (TQ, width), 1) // CHUNK
    dc = i + left_chunks - j
    return jnp.where((dc >= 0) & (dc <= left_chunks), rolled[:, TQ:], MASKED)


def _toeplitz_sum(tile, width):
    flip = (lax.broadcasted_iota(jnp.int32, (TQ, TQ), 0) + lax.broadcasted_iota(jnp.int32, (TQ, TQ), 1)
            == TQ - 1).astype(F32)
    reversed_rows = jnp.dot(flip, tile, precision=lax.Precision.HIGHEST, preferred_element_type=F32)
    padded = jnp.concatenate([reversed_rows, jnp.zeros((TQ, TQ), F32)], axis=1)
    rolled = pltpu.roll(padded, 0, 1, stride=1, stride_axis=0)
    return jnp.sum(rolled, axis=0, keepdims=True)


def _softmax_pair(sc, w, sink=None):
    ps, lses = [], []
    for e in range(2):
        sh = sc[:, e * w:(e + 1) * w]
        m = jnp.max(sh, axis=-1, keepdims=True)
        if sink is not None:
            m = jnp.maximum(m, sink[e])
        ex = jnp.exp(sh - m)
        l = jnp.sum(ex, axis=-1, keepdims=True)
        if sink is not None:
            l = l + jnp.exp(sink[e] - m)
        ps.append((ex * (1.0 / l)).astype(BF16))
        lses.append(m + jnp.log(l))
    return jnp.concatenate(ps, axis=-1), lses


def _softmax_pair_bwd(sc, dp, lse, delta, w):
    ps, dss = [], []
    for e in range(2):
        p = jnp.exp(sc[:, e * w:(e + 1) * w] - lse[e])
        ps.append(p)
        dss.append(p * (dp[:, e * w:(e + 1) * w] - delta[e]))
    return jnp.concatenate(ps, axis=-1), jnp.concatenate(dss, axis=-1)


def _pair_rowsums(x, lo):
    zero = jnp.zeros_like(x)
    return (jnp.sum(jnp.where(lo, x, zero), axis=-1, keepdims=True),
            jnp.sum(jnp.where(lo, zero, x), axis=-1, keepdims=True))


def _a_kv_specs(left, pw):
    specs = []
    for which in (1, 2):
        for t in range(A_KBLOCKS):
            specs.append(pl.BlockSpec(
                (None, KB, pw), functools.partial(
                    lambda p, b, which, t: (which, jnp.maximum(b - left + t, 0), p), which=which, t=t)))
    return specs


def _attn_a_fwd(zqkv, g, diag):
    s = g.shape[0]
    nb = s // TQ
    left = A_KBLOCKS - 1
    pw = A_PAIRS * LANES
    wide = A_WIN + TQ

    def body(q_ref, *refs):
        k_refs = refs[:A_KBLOCKS]
        v_refs = refs[A_KBLOCKS:2 * A_KBLOCKS]
        g_ref, diag_ref, o_ref, u_ref, lse_ref, bias_scr = refs[2 * A_KBLOCKS:]
        b = pl.program_id(1)

        @pl.when(b == 0)
        def _():
            for hh in range(2 * A_PAIRS):
                bias_scr[hh // 2, :, (hh % 2) * A_WIN:(hh % 2 + 1) * A_WIN] = _toeplitz_tile(
                    diag_ref[hh], A_WIN, A_LEFT_CHUNKS)

        valid = _block_valid(b, left, A_WIN)
        lo = _lane_lo(TQ)
        for pp in range(A_PAIRS):
            ln = slice(pp * LANES, (pp + 1) * LANES)
            kcat = _stack_pair(jnp.concatenate([r[:, ln] for r in k_refs], axis=0))
            vcat = _stack_pair(jnp.concatenate([r[:, ln] for r in v_refs], axis=0))
            sc = lax.dot_general(q_ref[:, ln], kcat, NT, preferred_element_type=F32) * SCALE + bias_scr[pp]
            sc = jnp.where(valid, sc, MASKED)
            p, lses = _softmax_pair(sc, A_WIN)
            ov = jnp.dot(p, vcat, preferred_element_type=F32)
            o_ref[:, ln] = ov
            lse_ref[pp] = jnp.where(lo, lses[0], lses[1])
            sg, _ = _silu_parts(g_ref[:, ln])
            u_ref[:, ln] = (ov * sg).astype(BF16)

    tile = pl.BlockSpec((TQ, pw), lambda p, b: (b, p))
    return pl.pallas_call(
        body, name="attn_a_fwd", grid=(HEADS // 2 // A_PAIRS, nb),
        in_specs=[pl.BlockSpec((None, TQ, pw), lambda p, b: (0, b, p))] + _a_kv_specs(left, pw) + [
            tile, pl.BlockSpec((2 * A_PAIRS, 1, wide), lambda p, b: (p, 0, 0))],
        out_specs=[tile, tile, pl.BlockSpec((A_PAIRS, TQ, LANES), lambda p, b: (p, b, 0))],
        out_shape=[jax.ShapeDtypeStruct((s, D_MODEL), F32), jax.ShapeDtypeStruct((s, D_MODEL), BF16),
                   jax.ShapeDtypeStruct((HEADS // 2, s, LANES), F32)],
        scratch_shapes=[pltpu.VMEM((A_PAIRS, TQ, 2 * A_WIN), F32)],
        compiler_params=_params(("parallel", "arbitrary")),
    )(zqkv, *([zqkv] * (2 * A_KBLOCKS)), g, diag)


def _attn_a_bwd(zqkv, g, o, du, lse, diag):
    s = g.shape[0]
    nb = s // TQ
    left = A_KBLOCKS - 1
    pw = A_PAIRS * LANES
    wide = A_WIN + TQ

    def body(q_ref, *refs):
        k_refs = refs[:A_KBLOCKS]
        v_refs = refs[A_KBLOCKS:2 * A_KBLOCKS]
        (g_ref, o_ref, du_ref, lse_ref, diag_ref, dz_ref, ddiag_ref,
         bias_scr, dbias_acc, dk_acc, dv_acc) = refs[2 * A_KBLOCKS:]
        b = pl.program_id(1)

        @pl.when(b == 0)
        def _():
            for hh in range(2 * A_PAIRS):
                bias_scr[hh // 2, :, (hh % 2) * A_WIN:(hh % 2 + 1) * A_WIN] = _toeplitz_tile(
                    diag_ref[hh], A_WIN, A_LEFT_CHUNKS)
            dbias_acc[...] = jnp.zeros_like(dbias_acc)
            dk_acc[...] = jnp.zeros_like(dk_acc)
            dv_acc[...] = jnp.zeros_like(dv_acc)

        valid = _block_valid(b, left, A_WIN)
        lo = _lane_lo(TQ)
        rows = pl.ds(pl.multiple_of(b * TQ, TQ), TQ)
        sg, dsg = _silu_parts(g_ref[...])
        duv = du_ref[...]
        ov = o_ref[...]
        do = duv * sg
        dz_ref[3, rows, :] = (duv * ov * dsg).astype(BF16)
        do_o = do * ov
        do_bf = do.astype(BF16)
        for pp in range(A_PAIRS):
            ln = slice(pp * LANES, (pp + 1) * LANES)
            q = q_ref[:, ln]
            kcat = _stack_pair(jnp.concatenate([r[:, ln] for r in k_refs], axis=0))
            vcat = _stack_pair(jnp.concatenate([r[:, ln] for r in v_refs], axis=0))
            sc = lax.dot_general(q, kcat, NT, preferred_element_type=F32) * SCALE + bias_scr[pp]
            sc = jnp.where(valid, sc, MASKED)
            lse_t = lse_ref[pp]
            dp = lax.dot_general(do_bf[:, ln], vcat, NT, preferred_element_type=F32)
            p, ds = _softmax_pair_bwd(sc, dp, (lse_t[:, 0:1], lse_t[:, HEAD_DIM:HEAD_DIM + 1]),
                                      _pair_rowsums(do_o[:, ln], lo), A_WIN)
            dbias_acc[pp] += ds
            dsq = (ds * SCALE).astype(BF16)
            dz_ref[0, rows, ln] = jnp.dot(dsq, kcat, preferred_element_type=F32).astype(BF16)
            dk = _unstack_pair(lax.dot_general(dsq, q, TN, preferred_element_type=F32), A_WIN)
            dv = _unstack_pair(lax.dot_general(p.astype(BF16), do_bf[:, ln], TN, preferred_element_type=F32),
                               A_WIN)
            for t in range(A_KBLOCKS):
                krows = pl.ds(pl.multiple_of(jnp.maximum(b - left + t, 0) * KB, KB), KB)
                dk_acc[krows, ln] += dk[t * KB:(t + 1) * KB, :]
                dv_acc[krows, ln] += dv[t * KB:(t + 1) * KB, :]

        @pl.when(b == nb - 1)
        def _():
            dz_ref[1] = dk_acc[...].astype(BF16)
            dz_ref[2] = dv_acc[...].astype(BF16)
            for hh in range(2 * A_PAIRS):
                ddiag_ref[hh] = _toeplitz_sum(
                    dbias_acc[hh // 2, :, (hh % 2) * A_WIN:(hh % 2 + 1) * A_WIN], A_WIN)

    tile = pl.BlockSpec((TQ, pw), lambda p, b: (b, p))
    diag_spec = pl.BlockSpec((2 * A_PAIRS, 1, wide), lambda p, b: (p, 0, 0))
    return pl.pallas_call(
        body, name="attn_a_bwd", grid=(HEADS // 2 // A_PAIRS, nb),
        in_specs=[pl.BlockSpec((None, TQ, pw), lambda p, b: (0, b, p))] + _a_kv_specs(left, pw) + [
            tile, tile, tile, pl.BlockSpec((A_PAIRS, TQ, LANES), lambda p, b: (p, b, 0)), diag_spec],
        out_specs=[pl.BlockSpec((4, s, pw), lambda p, b: (0, 0, p)), diag_spec],
        out_shape=[jax.ShapeDtypeStruct((4, s, D_MODEL), BF16),
                   jax.ShapeDtypeStruct((HEADS, 1, wide), F32)],
        scratch_shapes=[pltpu.VMEM((A_PAIRS, TQ, 2 * A_WIN), F32), pltpu.VMEM((A_PAIRS, TQ, 2 * A_WIN), F32),
                        pltpu.VMEM((s, pw), F32), pltpu.VMEM((s, pw), F32)],
        compiler_params=_params(("parallel", "arbitrary")),
    )(zqkv, *([zqkv] * (2 * A_KBLOCKS)), g, o, du, lse, diag)


B_STACK = B_GROUP // 2
B_KVX = 4 * LANES


def _b_head_place(h):
    return h // B_GROUP, (h % B_GROUP) // 2, h % 2


def _b_build_bias(diag_ref, bias_scr):
    for h in range(HEADS):
        gi, pr, e = _b_head_place(h)
        bias_scr[gi, pr * TQ:(pr + 1) * TQ, e * B_WIN:(e + 1) * B_WIN] = _toeplitz_tile(
            diag_ref[h], B_WIN, B_LEFT_CHUNKS)


def _b_stack(x, gi):
    return jnp.concatenate(
        [x[:, (B_STACK * gi + pr) * LANES:(B_STACK * gi + pr + 1) * LANES] for pr in range(B_STACK)], axis=0)


def _b_sinks(sink_ref, gi):
    return [jnp.concatenate(
        [jnp.broadcast_to(sink_ref[0:1, h:h + 1], (TQ, 1))
         for h in range(B_GROUP * gi + e, B_GROUP * (gi + 1), 2)], axis=0) for e in range(2)]


def _attn_b_fwd(qb, kvx, gate, diag, sinks):
    s = qb.shape[0]
    nb = s // TQ
    left = B_KBLOCKS - 1
    rows4 = B_STACK * TQ

    def body(q_ref, *refs):
        kv_refs = refs[:B_KBLOCKS]
        g_ref, diag_ref, sink_ref, o_ref, u_ref, lse_ref, bias_scr = refs[B_KBLOCKS:]
        b = pl.program_id(0)

        @pl.when(b == 0)
        def _():
            _b_build_bias(diag_ref, bias_scr)

        kvv = jnp.concatenate([r[...] for r in kv_refs], axis=0)
        valid = _block_valid(b, left, B_WIN)
        lo = _lane_lo(rows4)
        for gi in range(B_KV_HEADS):
            kcat = _stack_pair(kvv[:, gi * LANES:(gi + 1) * LANES])
            vcat = _stack_pair(kvv[:, (B_KV_HEADS + gi) * LANES:(B_KV_HEADS + gi + 1) * LANES])
            qs = _b_stack(q_ref, gi)
            sc = lax.dot_general(qs, kcat, NT, preferred_element_type=F32) * SCALE + bias_scr[gi]
            sc = jnp.where(valid, sc, MASKED)
            p, lses = _softmax_pair(sc, B_WIN, _b_sinks(sink_ref, gi))
            ov = jnp.dot(p, vcat, preferred_element_type=F32)
            lse_t = jnp.where(lo, lses[0], lses[1])
            for pr in range(B_STACK):
                pair = B_STACK * gi + pr
                o_ref[:, pair * LANES:(pair + 1) * LANES] = ov[pr * TQ:(pr + 1) * TQ]
                lse_ref[pair] = lse_t[pr * TQ:(pr + 1) * TQ]
        sg, _ = _silu_parts(g_ref[...])
        u_ref[...] = (o_ref[...] * sg).astype(BF16)

    kv_specs = [pl.BlockSpec((KB, B_KVX), functools.partial(
        lambda b, t: (jnp.maximum(b - left + t, 0), 0), t=t)) for t in range(B_KBLOCKS)]
    row = pl.BlockSpec((TQ, D_MODEL), lambda b: (b, 0))
    return pl.pallas_call(
        body, name="attn_b_fwd", grid=(nb,),
        in_specs=[row] + kv_specs + [row, pl.BlockSpec((HEADS, 1, B_WIN + TQ), lambda b: (0, 0, 0)),
                                     pl.BlockSpec((1, HEADS), lambda b: (0, 0))],
        out_specs=[row, row, pl.BlockSpec((HEADS // 2, TQ, LANES), lambda b: (0, b, 0))],
        out_shape=[jax.ShapeDtypeStruct((s, D_MODEL), F32), jax.ShapeDtypeStruct((s, D_MODEL), BF16),
                   jax.ShapeDtypeStruct((HEADS // 2, s, LANES), F32)],
        scratch_shapes=[pltpu.VMEM((B_KV_HEADS, rows4, 2 * B_WIN), F32)],
        compiler_params=_params(("arbitrary",)),
    )(qb, *([kvx] * B_KBLOCKS), gate, diag, sinks)


def _attn_b_bwd(qb, kvx, gate, o, du, lse, diag, sinks):
    s = qb.shape[0]
    nb = s // TQ
    left = B_KBLOCKS - 1
    rows4 = B_STACK * TQ
    half = D_MODEL // 2

    def body(q_ref, *refs):
        kv_refs = refs[:B_KBLOCKS]
        (g_ref, o_ref, du_ref, lse_ref, diag_ref, sink_ref, dz_ref, dkv_ref, ddiag_ref, dsink_ref,
         bias_scr, dbias_acc, dkv_acc, dsink_acc) = refs[B_KBLOCKS:]
        b = pl.program_id(0)

        @pl.when(b == 0)
        def _():
            _b_build_bias(diag_ref, bias_scr)
            dbias_acc[...] = jnp.zeros_like(dbias_acc)
            dkv_acc[...] = jnp.zeros_like(dkv_acc)
            dsink_acc[...] = jnp.zeros_like(dsink_acc)

        kvv = jnp.concatenate([r[...] for r in kv_refs], axis=0)
        valid = _block_valid(b, left, B_WIN)
        lo = _lane_lo(rows4)
        sg, dsg = _silu_parts(g_ref[...])
        duv = du_ref[...]
        ov = o_ref[...]
        do = duv * sg
        dgate = (duv * ov * dsg).astype(BF16)
        dz_ref[2] = dgate[:, :half]
        dz_ref[3] = dgate[:, half:]
        do_o = do * ov
        do_bf = do.astype(BF16)
        for gi in range(B_KV_HEADS):
            kcat = _stack_pair(kvv[:, gi * LANES:(gi + 1) * LANES])
            vcat = _stack_pair(kvv[:, (B_KV_HEADS + gi) * LANES:(B_KV_HEADS + gi + 1) * LANES])
            qs = _b_stack(q_ref, gi)
            dos = _b_stack(do_bf, gi)
            delta = _pair_rowsums(_b_stack(do_o, gi), lo)
            lse_t = jnp.concatenate([lse_ref[B_STACK * gi + pr] for pr in range(B_STACK)], axis=0)
            lse2 = (lse_t[:, 0:1], lse_t[:, HEAD_DIM:HEAD_DIM + 1])
            sink = _b_sinks(sink_ref, gi)
            sc = lax.dot_general(qs, kcat, NT, preferred_element_type=F32) * SCALE + bias_scr[gi]
            sc = jnp.where(valid, sc, MASKED)
            dp = lax.dot_general(dos, vcat, NT, preferred_element_type=F32)
            p, ds = _softmax_pair_bwd(sc, dp, lse2, delta, B_WIN)
            dbias_acc[gi] += ds
            dsink_acc[gi] += jnp.where(lo, -jnp.exp(sink[0] - lse2[0]) * delta[0],
                                       -jnp.exp(sink[1] - lse2[1]) * delta[1])
            dsq = (ds * SCALE).astype(BF16)
            dq = jnp.dot(dsq, kcat, preferred_element_type=F32).astype(BF16)
            for pr in range(B_STACK):
                dz_ref[gi, :, pr * LANES:(pr + 1) * LANES] = dq[pr * TQ:(pr + 1) * TQ]
            dk = _unstack_pair(lax.dot_general(dsq, qs, TN, preferred_element_type=F32), B_WIN)
            dv = _unstack_pair(lax.dot_general(p.astype(BF16), dos, TN, preferred_element_type=F32), B_WIN)
            for t in range(B_KBLOCKS):
                krows = pl.ds(pl.multiple_of(jnp.maximum(b - left + t, 0) * KB, KB), KB)
                dkv_acc[krows, gi * LANES:(gi + 1) * LANES] += dk[t * KB:(t + 1) * KB, :]
                dkv_acc[krows, (B_KV_HEADS + gi) * LANES:(B_KV_HEADS + gi + 1) * LANES] += dv[t * KB:(t + 1) * KB, :]

        @pl.when(b == nb - 1)
        def _():
            lo_s = _lane_lo(s)
            for which in range(2):
                folded = []
                for gi in range(B_KV_HEADS):
                    part = dkv_acc[:, (which * B_KV_HEADS + gi) * LANES:(which * B_KV_HEADS + gi + 1) * LANES]
                    folded.append(part + pltpu.roll(part, HEAD_DIM, 1))
                dkv_ref[:, which * LANES:(which + 1) * LANES] = jnp.where(lo_s, folded[0], folded[1]).astype(BF16)
            lane8 = lax.broadcasted_iota(jnp.int32, dsink_ref.shape, 1)
            tot = jnp.zeros(dsink_ref.shape, F32)
            for h in range(HEADS):
                gi, pr, e = _b_head_place(h)
                ddiag_ref[h] = _toeplitz_sum(
                    dbias_acc[gi, pr * TQ:(pr + 1) * TQ, e * B_WIN:(e + 1) * B_WIN], B_WIN)
                col = dsink_acc[gi, pr * TQ:(pr + 1) * TQ, e * HEAD_DIM:e * HEAD_DIM + 1]
                tot = jnp.where(lane8 == h, jnp.sum(col, axis=0, keepdims=True), tot)
            dsink_ref[...] = tot

    kv_specs = [pl.BlockSpec((KB, B_KVX), functools.partial(
        lambda b, t: (jnp.maximum(b - left + t, 0), 0), t=t)) for t in range(B_KBLOCKS)]
    row = pl.BlockSpec((TQ, D_MODEL), lambda b: (b, 0))
    diag_spec = pl.BlockSpec((HEADS, 1, B_WIN + TQ), lambda b: (0, 0, 0))
    return pl.pallas_call(
        body, name="attn_b_bwd", grid=(nb,),
        in_specs=[row] + kv_specs + [row, row, row, pl.BlockSpec((HEADS // 2, TQ, LANES), lambda b: (0, b, 0)),
                                     diag_spec, pl.BlockSpec((1, HEADS), lambda b: (0, 0))],
        out_specs=[pl.BlockSpec((4, TQ, half), lambda b: (0, b, 0)),
                   pl.BlockSpec((s, 2 * LANES), lambda b: (0, 0)), diag_spec,
                   pl.BlockSpec((8, LANES), lambda b: (0, 0))],
        out_shape=[jax.ShapeDtypeStruct((4, s, half), BF16), jax.ShapeDtypeStruct((s, 2 * LANES), BF16),
                   jax.ShapeDtypeStruct((HEADS, 1, B_WIN + TQ), F32), jax.ShapeDtypeStruct((8, LANES), F32)],
        scratch_shapes=[pltpu.VMEM((B_KV_HEADS, rows4, 2 * B_WIN), F32),
                        pltpu.VMEM((B_KV_HEADS, rows4, 2 * B_WIN), F32),
                        pltpu.VMEM((s, B_KVX), F32), pltpu.VMEM((B_KV_HEADS, rows4, LANES), F32)],
        compiler_params=_params(("arbitrary",)),
    )(qb, *([kvx] * B_KBLOCKS), gate, o, du, lse, diag, sinks)


def _t5_bucket(rel):
    nb = T5_BUCKETS // 2
    max_exact = nb // 2
    ret = jnp.where(rel > 0, nb, 0)
    n = jnp.abs(rel)
    nf = jnp.maximum(n, 1).astype(jnp.float32)
    large = max_exact + (jnp.log(nf / max_exact) / math.log(T5_MAX_DIST / max_exact)
                         * (nb - max_exact)).astype(jnp.int32)
    large = jnp.minimum(large, nb - 1)
    return ret + jnp.where(n < max_exact, n, large)


def _a_offset_onehot():
    c = np.arange(A_WIN + TQ)
    dist = A_LEFT_CHUNKS * CHUNK + TQ - 1 - c
    idx = np.clip(dist, -A_REL_CLIP, A_REL_CLIP) + A_REL_CLIP
    onehot = np.zeros((A_WIN + TQ, 2 * A_REL_CLIP + 1), np.float32)
    onehot[c, idx] = 1.0
    return jnp.asarray(onehot)


def _b_offset_onehot():
    c = jnp.arange(B_WIN + TQ, dtype=jnp.int32)
    rel = c - (TQ - 1) - B_LEFT_CHUNKS * CHUNK
    return (_t5_bucket(rel)[:, None] == jnp.arange(T5_BUCKETS)[None, :]).astype(F32)


def _diag_rows(onehot, table):
    rows = jnp.dot(onehot, table.astype(F32), precision=lax.Precision.HIGHEST)
    return rows.T.reshape(HEADS, 1, onehot.shape[0])


def _diag_rows_grad(onehot, ddiag):
    return jnp.dot(ddiag.reshape(HEADS, onehot.shape[0]), onehot, precision=lax.Precision.HIGHEST).T


def _position():
    x, y, c = lax.axis_index("x"), lax.axis_index("y"), lax.axis_index("c")
    chips = [(1 - x, y), (x, 1 - y), (1 - x, 1 - y)]
    return x, y, c, chips


ANY = pl.BlockSpec(memory_space=pl.ANY)


def _allgather_weights(shards, split):
    n = len(shards)

    def body(*refs):
        ins, outs = refs[:n], refs[n:2 * n]
        send_sems, recv_sems, pass_send, pass_recv, local_sems = refs[2 * n:]
        x, y, c, chips = _position()
        mine = 2 * x + y
        sibling = (x, y, 1 - c)

        def part(ref, t, half):
            if not split[t]:
                return ref
            rows = shards[t].shape[0] // 2
            return ref.at[pl.ds(half * rows, rows)]

        local = [pltpu.make_async_copy(ins[t], outs[t].at[mine], local_sems.at[t]) for t in range(n)]
        for cp in local:
            cp.start()
        sends = []
        for t in range(n):
            for j, chip in enumerate(chips):
                sends.append(pltpu.make_async_remote_copy(
                    src_ref=part(ins[t], t, c), dst_ref=part(outs[t].at[mine], t, c),
                    send_sem=send_sems.at[3 * t + j], recv_sem=recv_sems.at[3 * t + j],
                    device_id=(chip[0], chip[1], c), device_id_type=MESH))
        for cp in sends:
            cp.start()
        passes = []
        for t in range(n):
            for j, chip in enumerate(chips):
                theirs = 2 * chip[0] + chip[1]
                landed = part(outs[t].at[theirs], t, c)
                pltpu.make_async_remote_copy(
                    src_ref=landed, dst_ref=landed, send_sem=send_sems.at[3 * t + j],
                    recv_sem=recv_sems.at[3 * t + j], device_id=(chip[0], chip[1], c),
                    device_id_type=MESH).wait_recv()
                if split[t]:
                    cp = pltpu.make_async_remote_copy(
                        src_ref=landed, dst_ref=landed, send_sem=pass_send.at[3 * t + j],
                        recv_sem=pass_recv.at[3 * t + j], device_id=sibling, device_id_type=MESH)
                    cp.start()
                    passes.append(cp)
        for t in range(n):
            if not split[t]:
                continue
            for j, chip in enumerate(chips):
                theirs = 2 * chip[0] + chip[1]
                other = part(outs[t].at[theirs], t, 1 - c)
                pltpu.make_async_remote_copy(
                    src_ref=other, dst_ref=other, send_sem=pass_send.at[3 * t + j],
                    recv_sem=pass_recv.at[3 * t + j], device_id=sibling, device_id_type=MESH).wait_recv()
        for cp in sends + passes:
            cp.wait_send()
        for cp in local:
            cp.wait()

    return pl.pallas_call(
        body, name="allgather_weights",
        in_specs=[ANY] * n, out_specs=[ANY] * n,
        out_shape=[jax.ShapeDtypeStruct((4,) + w.shape, w.dtype) for w in shards],
        scratch_shapes=[pltpu.SemaphoreType.DMA((3 * n,))] * 4 + [pltpu.SemaphoreType.DMA((n,))],
    )(*shards)


def _scatter_partials(grads):
    n = len(grads)

    def body(*refs):
        ins, outs = refs[:n], refs[n:2 * n]
        send_sems, recv_sems = refs[2 * n:]
        x, y, c, chips = _position()
        sends = []
        for t in range(n):
            for j, chip in enumerate(chips):
                sends.append(pltpu.make_async_remote_copy(
                    src_ref=ins[t].at[2 * chip[0] + chip[1]], dst_ref=outs[t].at[j],
                    send_sem=send_sems.at[3 * t + j], recv_sem=recv_sems.at[3 * t + j],
                    device_id=(chip[0], chip[1], c), device_id_type=MESH))
        for cp in sends:
            cp.start()
        for cp in sends:
            cp.wait()

    return pl.pallas_call(
        body, name="scatter_partials",
        in_specs=[ANY] * n, out_specs=[ANY] * n,
        out_shape=[jax.ShapeDtypeStruct((3,) + g.shape[1:], g.dtype) for g in grads],
        scratch_shapes=[pltpu.SemaphoreType.DMA((3 * n,))] * 2,
    )(*grads)


def _swap_with_sibling(blocks):
    n = len(blocks)

    def body(*refs):
        ins, outs = refs[:n], refs[n:2 * n]
        send_sems, recv_sems = refs[2 * n:]
        x, y, c, _ = _position()
        sends = [pltpu.make_async_remote_copy(
            src_ref=ins[t], dst_ref=outs[t], send_sem=send_sems.at[t], recv_sem=recv_sems.at[t],
            device_id=(x, y, 1 - c), device_id_type=MESH) for t in range(n)]
        for cp in sends:
            cp.start()
        for cp in sends:
            cp.wait()

    return pl.pallas_call(
        body, name="swap_with_sibling",
        in_specs=[ANY] * n, out_specs=[ANY] * n,
        out_shape=[jax.ShapeDtypeStruct(b.shape, b.dtype) for b in blocks],
        scratch_shapes=[pltpu.SemaphoreType.DMA((n,))] * 2,
    )(*blocks)


def _allreduce_small(block):
    rows = block.shape[0]

    def body(in_ref, sum_ref, all_ref, send_sems, recv_sems):
        x, y, c, _ = _position()
        me = 4 * x + 2 * y + c
        all_ref[me] = in_ref[...]
        sends = []
        for k in range(1, 8):
            peer = (x ^ (k >> 2), y ^ ((k >> 1) & 1), c ^ (k & 1))
            sends.append(pltpu.make_async_remote_copy(
                src_ref=in_ref, dst_ref=all_ref.at[me], send_sem=send_sems.at[k - 1],
                recv_sem=recv_sems.at[k - 1], device_id=peer, device_id_type=MESH))
        for cp in sends:
            cp.start()
        for k in range(1, 8):
            theirs = me ^ k
            pltpu.make_async_remote_copy(
                src_ref=in_ref, dst_ref=all_ref.at[theirs], send_sem=send_sems.at[k - 1],
                recv_sem=recv_sems.at[k - 1], device_id=(x, y, c), device_id_type=MESH).wait_recv()
        for cp in sends:
            cp.wait_send()
        acc = all_ref[0]
        for d in range(1, 8):
            acc = acc + all_ref[d]
        sum_ref[...] = acc

    vmem = pl.BlockSpec(memory_space=pltpu.VMEM)
    return pl.pallas_call(
        body, name="allreduce_small",
        in_specs=[vmem], out_specs=[vmem, vmem],
        out_shape=[jax.ShapeDtypeStruct((rows, LANES), F32), jax.ShapeDtypeStruct((8, rows, LANES), F32)],
        scratch_shapes=[pltpu.SemaphoreType.DMA((7,))] * 2,
    )(block)[0]


def _adamw_math(w, g, m, v):
    m = ADAM_B1 * m + (1.0 - ADAM_B1) * g
    v = ADAM_B2 * v + (1.0 - ADAM_B2) * (g * g)
    m_hat = m / (1.0 - ADAM_B1 ** ADAM_STEP)
    v_hat = v / (1.0 - ADAM_B2 ** ADAM_STEP)
    delta = -ADAM_LR * (m_hat / (jnp.sqrt(v_hat) + ADAM_EPS) + ADAM_WD * w)
    return delta, m, v


def _row_tile(rows):
    return min(rows, 256)


def _sum_partials(name, own, recv):
    rows, cols = own.shape
    tr = _row_tile(rows)

    def body(own_ref, recv_ref, o_ref):
        acc = own_ref[...]
        for j in range(3):
            acc = acc + recv_ref[j].astype(F32)
        o_ref[...] = acc

    return pl.pallas_call(
        body, name=name, grid=(rows // tr,),
        in_specs=[pl.BlockSpec((tr, cols), lambda i: (i, 0)), pl.BlockSpec((3, tr, cols), lambda i: (0, i, 0))],
        out_specs=pl.BlockSpec((tr, cols), lambda i: (i, 0)),
        out_shape=jax.ShapeDtypeStruct((rows, cols), F32),
        compiler_params=_params(("parallel",)),
    )(own, recv)


def _adamw(name, w, m, v, g_parts):
    rows, cols = w.shape
    tr = _row_tile(rows)
    n = len(g_parts)

    def body(w_ref, m_ref, v_ref, *refs):
        g_refs = refs[:n]
        go_ref, d_ref, mo_ref, vo_ref = refs[n:]
        g = g_refs[0][...]
        for r in g_refs[1:]:
            g = g + r[...]
        delta, mn, vn = _adamw_math(w_ref[...], g, m_ref[...], v_ref[...])
        go_ref[...] = g
        d_ref[...] = delta
        mo_ref[...] = mn
        vo_ref[...] = vn

    spec = pl.BlockSpec((tr, cols), lambda i: (i, 0))
    return pl.pallas_call(
        body, name=name, grid=(rows // tr,),
        in_specs=[spec] * (3 + n), out_specs=[spec] * 4,
        out_shape=[jax.ShapeDtypeStruct((rows, cols), F32)] * 4,
        compiler_params=_params(("parallel",)),
    )(w, m, v, *g_parts)


def _local_step(x, target, ga, wa_in, rel_bias, wa_out, gk, wkv, t5, gb, wb_in, sinks, wb_out, gf):
    s, d = x.shape
    nt = s // TM
    half = d // 2
    row = pl.BlockSpec((TM, d), lambda i: (i, 0))
    whole = lambda shape: pl.BlockSpec(shape, lambda *_: (0,) * len(shape))

    n1, = _norm_fwd("norm_a", x, ga)
    zqkv = _matmul("proj_a_qkv", n1, wa_in, dims=NN, grid=(3, nt),
                   a_spec=pl.BlockSpec((TM, d), lambda j, i: (i, 0)),
                   b_spec=pl.BlockSpec((None, d, d), lambda j, i: (j, 0, 0)),
                   o_spec=pl.BlockSpec((None, TM, d), lambda j, i: (j, i, 0)),
                   out_shape=(3, s, d), out_dtype=BF16)
    gate_a = _matmul("proj_a_gate", n1, wa_in, dims=NN, grid=(nt,),
                     a_spec=row, b_spec=pl.BlockSpec((None, d, d), lambda i: (3, 0, 0)), o_spec=row,
                     out_shape=(s, d), out_dtype=F32)
    onehot_a = _a_offset_onehot()
    diag_a = _diag_rows(onehot_a, rel_bias)
    o_a, u_a, lse_a = _attn_a_fwd(zqkv, gate_a, diag_a)
    h1 = _matmul("out_a", u_a, wa_out, dims=NN, grid=(nt,), a_spec=row, b_spec=whole((d, d)), o_spec=row,
                 out_shape=(s, d), out_dtype=F32, resid=x, resid_spec=row)

    nk, n2 = _norm_fwd("norm_kv_b", h1, jnp.concatenate([gk, gb], axis=0))
    kvw = wkv.shape[1]
    wkv_x = jnp.concatenate([wkv[:, (i // 2) * HEAD_DIM:(i // 2 + 1) * HEAD_DIM] for i in range(8)], axis=1)
    kvx = _matmul("proj_kv", nk, wkv_x, dims=NN, grid=(nt,), a_spec=row, b_spec=whole((d, B_KVX)),
                  o_spec=pl.BlockSpec((TM, B_KVX), lambda i: (i, 0)), out_shape=(s, B_KVX), out_dtype=BF16)
    qb = _matmul("proj_b_q", n2, wb_in, dims=NN, grid=(2, nt),
                 a_spec=pl.BlockSpec((TM, d), lambda j, i: (i, 0)),
                 b_spec=pl.BlockSpec((None, d, half), lambda j, i: (j, 0, 0)),
                 o_spec=pl.BlockSpec((TM, half), lambda j, i: (i, j)), out_shape=(s, d), out_dtype=BF16)
    gate_b = _matmul("proj_b_gate", n2, wb_in, dims=NN, grid=(2, nt),
                     a_spec=pl.BlockSpec((TM, d), lambda j, i: (i, 0)),
                     b_spec=pl.BlockSpec((None, d, half), lambda j, i: (2 + j, 0, 0)),
                     o_spec=pl.BlockSpec((TM, half), lambda j, i: (i, j)), out_shape=(s, d), out_dtype=F32)
    onehot_b = _b_offset_onehot()
    diag_b = _diag_rows(onehot_b, t5)
    o_b, u_b, lse_b = _attn_b_fwd(qb, kvx, gate_b, diag_b, sinks)
    h2 = _matmul("out_b", u_b, wb_out, dims=NN, grid=(nt,), a_spec=row, b_spec=whole((d, d)), o_spec=row,
                 out_shape=(s, d), out_dtype=F32, resid=h1, resid_spec=row)

    dh2, loss, d_gf = _loss_head(h2, target, gf)

    du_b = _matmul("dout_b", dh2, wb_out, dims=NT, grid=(nt,), a_spec=row, b_spec=whole((d, d)), o_spec=row,
                   out_shape=(s, d), out_dtype=F32)
    d_wb_out = _matmul("dw_out_b", u_b, dh2, dims=TN, grid=(2, nt),
                       a_spec=pl.BlockSpec((TM, d), lambda j, k: (k, 0)),
                       b_spec=pl.BlockSpec((TM, half), lambda j, k: (k, j)),
                       o_spec=pl.BlockSpec((d, half), lambda j, k: (0, j)),
                       out_shape=(d, d), out_dtype=F32, k_axis=1, k_steps=nt)
    dz_b, dkv, ddiag_b, dsinks = _attn_b_bwd(qb, kvx, gate_b, o_b, du_b, lse_b, diag_b, sinks)
    dn2 = _matmul("dproj_b", dz_b, wb_in, dims=NT, grid=(nt, 4),
                  a_spec=pl.BlockSpec((None, TM, half), lambda i, k: (k, i, 0)),
                  b_spec=pl.BlockSpec((None, d, half), lambda i, k: (k, 0, 0)),
                  o_spec=pl.BlockSpec((TM, d), lambda i, k: (i, 0)),
                  out_shape=(s, d), out_dtype=F32, k_axis=1, k_steps=4)
    d_wb_in = _matmul("dw_in_b", n2, dz_b, dims=TN, grid=(4, nt),
                      a_spec=pl.BlockSpec((TM, d), lambda j, k: (k, 0)),
                      b_spec=pl.BlockSpec((None, TM, half), lambda j, k: (j, k, 0)),
                      o_spec=pl.BlockSpec((None, d, half), lambda j, k: (j, 0, 0)),
                      out_shape=(4, d, half), out_dtype=F32, k_axis=1, k_steps=nt)
    dnk = _matmul("dproj_kv", dkv, wkv, dims=NT, grid=(nt,),
                  a_spec=pl.BlockSpec((TM, kvw), lambda i: (i, 0)), b_spec=whole((d, kvw)), o_spec=row,
                  out_shape=(s, d), out_dtype=F32)
    d_wkv = _matmul("dw_kv", nk, dkv, dims=TN, grid=(nt,),
                    a_spec=pl.BlockSpec((TM, d), lambda k: (k, 0)),
                    b_spec=pl.BlockSpec((TM, kvw), lambda k: (k, 0)),
                    o_spec=pl.BlockSpec((d, kvw), lambda k: (0, 0)),
                    out_shape=(d, kvw), out_dtype=F32, k_axis=0, k_steps=nt)
    dh1, d_gkb = _norm_bwd("dnorm_kv_b", h1, dh2, [dnk, dn2], jnp.concatenate([gk, gb], axis=0))

    du_a = _matmul("dout_a", dh1, wa_out, dims=NT, grid=(nt,), a_spec=row, b_spec=whole((d, d)), o_spec=row,
                   out_shape=(s, d), out_dtype=F32)
    d_wa_out = _matmul("dw_out_a", u_a, dh1, dims=TN, grid=(2, nt),
                       a_spec=pl.BlockSpec((TM, d), lambda j, k: (k, 0)),
                       b_spec=pl.BlockSpec((TM, half), lambda j, k: (k, j)),
                       o_spec=pl.BlockSpec((d, half), lambda j, k: (0, j)),
                       out_shape=(d, d), out_dtype=F32, k_axis=1, k_steps=nt)
    dz_a, ddiag_a = _attn_a_bwd(zqkv, gate_a, o_a, du_a, lse_a, diag_a)
    dn1 = _matmul("dproj_a", dz_a, wa_in, dims=NT, grid=(nt, 4),
                  a_spec=pl.BlockSpec((None, TM, d), lambda i, k: (k, i, 0)),
                  b_spec=pl.BlockSpec((None, d, d), lambda i, k: (k, 0, 0)),
                  o_spec=pl.BlockSpec((TM, d), lambda i, k: (i, 0)),
                  out_shape=(s, d), out_dtype=F32, k_axis=1, k_steps=4)
    d_wa_in = _matmul("dw_in_a", n1, dz_a, dims=TN, grid=(4, 2, nt),
                      a_spec=pl.BlockSpec((TM, d), lambda j, h, k: (k, 0)),
                      b_spec=pl.BlockSpec((None, TM, half), lambda j, h, k: (j, k, h)),
                      o_spec=pl.BlockSpec((None, d, half), lambda j, h, k: (j, 0, h)),
                      out_shape=(4, d, d), out_dtype=F32, k_axis=2, k_steps=nt)
    grad_x, d_ga = _norm_bwd("dnorm_a", x, dh1, [dn1], ga)

    small = dict(
        a_norm=d_ga, a_rel_bias=_diag_rows_grad(onehot_a, ddiag_a), kv_norm=d_gkb[0:1],
        t5_bias=_diag_rows_grad(onehot_b, ddiag_b), b_norm=d_gkb[1:2], b_sinks=dsinks[0:1, :HEADS],
        final_norm=d_gf)
    big = dict(a_w_in=d_wa_in, a_w_out=d_wa_out, kv_w=d_wkv, b_w_in=d_wb_in, b_w_out=d_wb_out)
    return loss, grad_x, small, big


SMALL = ("a_norm", "a_rel_bias", "kv_norm", "t5_bias", "b_norm", "b_sinks", "final_norm")
BIG = ("a_w_in", "a_w_out", "kv_w", "b_w_in", "b_w_out")
ORDER = ("a_norm", "a_w_in", "a_rel_bias", "a_w_out", "kv_norm", "kv_w", "t5_bias", "b_norm", "b_w_in",
         "b_sinks", "b_w_out", "final_norm")


def _pack(parts, rows):
    flat = jnp.concatenate([p.reshape(-1).astype(F32) for p in parts])
    return jnp.pad(flat, (0, rows * LANES - flat.shape[0])).reshape(rows, LANES)


def _unpack(block, shapes):
    flat = block.reshape(-1)
    out, at = [], 0
    for shp in shapes:
        size = int(np.prod(shp))
        out.append(flat[at:at + size].reshape(shp))
        at += size
    return out


def kernel(x, a_norm, a_w_in, a_rel_bias, a_w_out, kv_norm, kv_w, t5_bias, b_norm, b_w_in, b_sinks, b_w_out, final_norm, loss_target, m_a_norm, m_a_w_in, m_a_rel_bias, m_a_w_out, m_kv_norm, m_kv_w, m_t5_bias, m_b_norm, m_b_w_in, m_b_sinks, m_b_w_out, m_final_norm, v_a_norm, v_a_w_in, v_a_rel_bias, v_a_w_out, v_kv_norm, v_kv_w, v_t5_bias, v_b_norm, v_b_w_in, v_b_sinks, v_b_w_out, v_final_norm):
    w = dict(a_norm=a_norm, a_w_in=a_w_in, a_rel_bias=a_rel_bias, a_w_out=a_w_out, kv_norm=kv_norm, kv_w=kv_w,
             t5_bias=t5_bias, b_norm=b_norm, b_w_in=b_w_in, b_sinks=b_sinks, b_w_out=b_w_out,
             final_norm=final_norm)
    m = dict(a_norm=m_a_norm, a_w_in=m_a_w_in, a_rel_bias=m_a_rel_bias, a_w_out=m_a_w_out, kv_norm=m_kv_norm,
             kv_w=m_kv_w, t5_bias=m_t5_bias, b_norm=m_b_norm, b_w_in=m_b_w_in, b_sinks=m_b_sinks,
             b_w_out=m_b_w_out, final_norm=m_final_norm)
    v = dict(a_norm=v_a_norm, a_w_in=v_a_w_in, a_rel_bias=v_a_rel_bias, a_w_out=v_a_w_out, kv_norm=v_kv_norm,
             kv_w=v_kv_w, t5_bias=v_t5_bias, b_norm=v_b_norm, b_w_in=v_b_w_in, b_sinks=v_b_sinks,
             b_w_out=v_b_w_out, final_norm=v_final_norm)
    d = D_MODEL
    chip = 2 * lax.axis_index("x") + lax.axis_index("y")

    shard2d = dict(a_w_in=a_w_in[0], a_w_out=a_w_out[0], kv_w=kv_w, b_w_in=b_w_in[0], b_w_out=b_w_out[0])

    gathered = _allgather_weights(
        [shard2d[n].astype(BF16) for n in BIG] + [a_norm],
        [True] * len(BIG) + [False])
    wa_in, wa_out, wkv, wb_in, wb_out, ga = gathered
    ga = ga.reshape(1, d)

    loss, grad_x, small, big = _local_step(
        x[0], loss_target[0], ga, wa_in, a_rel_bias[0], wa_out.reshape(d, d), kv_norm.reshape(1, d),
        wkv.reshape(d, -1), t5_bias, b_norm, wb_in, b_sinks, wb_out.reshape(d, d), final_norm.reshape(1, d))

    small_shapes = [small[n].shape for n in SMALL] + [(1, 1)]
    total = sum(int(np.prod(s)) for s in small_shapes)
    rows = -(-total // (8 * LANES)) * 8
    reduced = _unpack(_allreduce_small(_pack([small[n] for n in SMALL] + [loss], rows)), small_shapes)
    g_small = dict(zip(SMALL, reduced[:-1]))
    loss_out = reduced[-1].reshape(())
    g_small["a_norm"] = lax.dynamic_slice_in_dim(g_small["a_norm"], chip * (d // 4), d // 4, axis=1)

    shard_major = dict(a_w_in=big["a_w_in"], a_w_out=big["a_w_out"].reshape(4, d // 4, d),
                       kv_w=big["kv_w"].reshape(4, d // 4, -1), b_w_in=big["b_w_in"],
                       b_w_out=big["b_w_out"].reshape(4, d // 4, d))
    received = _scatter_partials([shard_major[n].astype(BF16) for n in BIG])
    core_sums = [
        _sum_partials("sum_" + n, lax.dynamic_index_in_dim(shard_major[n], chip, 0, keepdims=False), r)
        for n, r in zip(BIG, received)]
    sibling_sums = _swap_with_sibling(core_sums)

    out = {}
    for n, mine, theirs in zip(BIG, core_sums, sibling_sums):
        res = _adamw("adamw_" + n, shard2d[n], m[n].reshape(shard2d[n].shape), v[n].reshape(shard2d[n].shape),
                     [mine, theirs])
        out[n] = [r.reshape(w[n].shape) for r in res]
    small_w_shapes = [w[n].shape for n in SMALL]
    total_w = sum(int(np.prod(s)) for s in small_w_shapes)
    rows_w = -(-total_w // (8 * LANES)) * 8
    packed = [_pack([t[n] for n in SMALL], rows_w) for t in (w, m, v)]
    g_packed = _pack([g_small[n] for n in SMALL], rows_w)
    res = _adamw("adamw_small", packed[0], packed[1], packed[2], [g_packed])
    unpacked = [_unpack(r, small_w_shapes) for r in res]
    for i, n in enumerate(SMALL):
        out[n] = [unpacked[k][i] for k in range(4)]

    grads = [out[n][0] for n in ORDER]
    deltas = [out[n][1] for n in ORDER]
    new_m = [out[n][2] for n in ORDER]
    new_v = [out[n][3] for n in ORDER]
    return (loss_out, grad_x[None], *grads, *deltas, *new_m, *new_v)
```

```python
import functools
import math

import jax
import jax.numpy as jnp
import numpy as np
from jax import lax
from jax.experimental import pallas as pl
from jax.experimental.pallas import tpu as pltpu

F32 = jnp.float32
BF16 = jnp.bfloat16
MESH = pl.DeviceIdType.MESH

D_MODEL = 1024
HEADS = 16
HEAD_DIM = 64
CHUNK = 64
RMS_EPS = 1e-6
SCALE = HEAD_DIM ** -0.5
A_LEFT_CHUNKS = 8
A_REL_CLIP = 256
B_LEFT_CHUNKS = 2
B_KV_HEADS = 2
B_GROUP = HEADS // B_KV_HEADS
T5_BUCKETS = 32
T5_MAX_DIST = 128
ADAM_LR = 0.001
ADAM_B1 = 0.9
ADAM_B2 = 0.999
ADAM_EPS = 1e-08
ADAM_WD = 0.01
ADAM_STEP = 10

MASKED = -1e30
LANES = 128
TQ = 128
A_PAIRS = 2
KB = 128
A_KBLOCKS = A_LEFT_CHUNKS * CHUNK // KB + 1
B_KBLOCKS = B_LEFT_CHUNKS * CHUNK // KB + 1
A_WIN = A_KBLOCKS * KB
B_WIN = B_KBLOCKS * KB
TM = 512
VMEM_LIMIT = 56 * 1024 * 1024

NT = (((1,), (1,)), ((), ()))
TN = (((0,), (0,)), ((), ()))
NN = (((1,), (0,)), ((), ()))


def _params(sem=None):
    return pltpu.CompilerParams(dimension_semantics=sem, vmem_limit_bytes=VMEM_LIMIT)


class _Hosted:
    def __init__(self, inputs, out_shapes, sems, first, middle, last):
        self.inputs, self.out_shapes, self.sems = list(inputs), list(out_shapes), list(sems)
        self.first, self.middle, self.last = first, middle, last


def _call(body, *, name, grid, in_specs, out_specs, out_shape, args, scratch_shapes=(), sem=None, hosted=None):
    in_specs, out_specs, out_shape = list(in_specs), list(out_specs), list(out_shape)
    scratch_shapes = list(scratch_shapes)
    if hosted is None:
        out = pl.pallas_call(
            body, name=name, grid=grid, in_specs=in_specs, out_specs=out_specs, out_shape=out_shape,
            scratch_shapes=scratch_shapes, compiler_params=_params(sem))(*args)
        return list(out), []
    n_in, n_out, n_scr = len(in_specs), len(out_shape), len(scratch_shapes)
    h_in, h_out = len(hosted.inputs), len(hosted.out_shapes)
    total = int(np.prod(grid)) if grid else 1

    def wrapped(*refs):
        ins, refs = refs[:n_in], refs[n_in:]
        h_ins, refs = refs[:h_in], refs[h_in:]
        outs, refs = refs[:n_out], refs[n_out:]
        h_outs, refs = refs[:h_out], refs[h_out:]
        scr, h_sems = refs[:n_scr], refs[n_scr:]
        step = 0
        for axis, size in enumerate(grid):
            step = step * size + pl.program_id(axis)

        @pl.when(step == 0)
        def _():
            hosted.first(h_ins, h_outs, h_sems)

        body(*ins, *outs, *scr)
        if hosted.middle is not None:
            @pl.when(step == total // 2)
            def _():
                hosted.middle(h_ins, h_outs, h_sems)

        @pl.when(step == total - 1)
        def _():
            hosted.last(h_ins, h_outs, h_sems)

    out = pl.pallas_call(
        wrapped, name=name, grid=grid, in_specs=in_specs + [ANY] * h_in, out_specs=out_specs + [ANY] * h_out,
        out_shape=out_shape + hosted.out_shapes, scratch_shapes=scratch_shapes + hosted.sems,
        compiler_params=_params(("arbitrary",) * len(grid)))(*args, *hosted.inputs)
    return list(out[:n_out]), list(out[n_out:])


def _matmul(name, a, b, *, dims, grid, a_spec, b_spec, o_spec, out_shape, out_dtype,
            k_axis=None, k_steps=1, resid=None, resid_spec=None, also_bf16=False, hosted=None):
    if k_axis is not None:
        assert out_dtype == F32 and resid is None
    assert not also_bf16 or k_axis is not None

    def body(*refs):
        a_ref, b_ref = refs[:2]
        r_ref = refs[2] if resid is not None else None
        o_ref = refs[3] if resid is not None else refs[2]
        prod = lax.dot_general(a_ref[...].astype(BF16), b_ref[...].astype(BF16), dims,
                               preferred_element_type=F32)
        if k_axis is None:
            if resid is not None:
                prod = r_ref[...] + prod
            o_ref[...] = prod.astype(out_dtype)
        else:
            k = pl.program_id(k_axis)

            @pl.when(k == 0)
            def _():
                o_ref[...] = prod

            @pl.when(k > 0)
            def _():
                o_ref[...] += prod

            if also_bf16:
                @pl.when(k == k_steps - 1)
                def _():
                    refs[-1][...] = o_ref[...].astype(BF16)

    in_specs = [a_spec, b_spec]
    args = [a, b]
    if resid is not None:
        in_specs.append(resid_spec)
        args.append(resid)
    sem = ["parallel"] * len(grid)
    if k_axis is not None:
        assert k_axis == len(grid) - 1 and grid[k_axis] == k_steps
        sem[k_axis] = "arbitrary"
    out_specs = [o_spec]
    out_shapes = [jax.ShapeDtypeStruct(out_shape, out_dtype)]
    if also_bf16:
        out_specs.append(o_spec)
        out_shapes.append(jax.ShapeDtypeStruct(out_shape, BF16))
    out, extra = _call(body, name=name, grid=grid, in_specs=in_specs, out_specs=out_specs, out_shape=out_shapes,
                       args=args, sem=tuple(sem), hosted=hosted)
    res = out[0] if not also_bf16 else tuple(out)
    return res if hosted is None else (res, extra)


def _rms_rows(x):
    return lax.rsqrt(jnp.mean(x * x, axis=-1, keepdims=True) + RMS_EPS)


def _norm_fwd(name, x, gains):
    s, d = x.shape
    n = gains.shape[0]

    def body(x_ref, g_ref, *o_refs):
        xv = x_ref[...]
        xh = xv * _rms_rows(xv)
        for i in range(n):
            o_refs[i][...] = (xh * g_ref[i:i + 1, :]).astype(BF16)

    row = pl.BlockSpec((TM, d), lambda i: (i, 0))
    return pl.pallas_call(
        body, name=name, grid=(s // TM,),
        in_specs=[row, pl.BlockSpec((n, d), lambda i: (0, 0))],
        out_specs=[row] * n,
        out_shape=[jax.ShapeDtypeStruct((s, d), BF16)] * n,
        compiler_params=_params(("parallel",)),
    )(x, gains)


def _norm_bwd(name, x, dres, dns, gains):
    s, d = x.shape
    n = len(dns)

    def body(x_ref, r_ref, g_ref, *refs):
        dn_refs, dx_ref, dg_ref = refs[:n], refs[n], refs[n + 1]
        i = pl.program_id(0)
        xv = x_ref[...]
        r = _rms_rows(xv)
        xh = xv * r

        @pl.when(i == 0)
        def _():
            dg_ref[...] = jnp.zeros_like(dg_ref)

        a = None
        for j in range(n):
            dn = dn_refs[j][...]
            t = dn * g_ref[j:j + 1, :]
            a = t if a is None else a + t
            dg_ref[j:j + 1, :] += jnp.sum(dn * xh, axis=0, keepdims=True)
        dx_ref[...] = r_ref[...] + r * (a - xh * jnp.mean(xh * a, axis=-1, keepdims=True))

    row = pl.BlockSpec((TM, d), lambda i: (i, 0))
    small = pl.BlockSpec((n, d), lambda i: (0, 0))
    return pl.pallas_call(
        body, name=name, grid=(s // TM,),
        in_specs=[row, row, small] + [row] * n,
        out_specs=[row, small],
        out_shape=[jax.ShapeDtypeStruct((s, d), F32), jax.ShapeDtypeStruct((n, d), F32)],
        compiler_params=_params(("arbitrary",)),
    )(x, dres, gains, *dns)


def _loss_head(h2, target, gain):
    s, d = h2.shape

    def body(h_ref, t_ref, g_ref, dh_ref, loss_ref, dg_ref):
        i = pl.program_id(0)
        hv = h_ref[...]
        r = _rms_rows(hv)
        hh = hv * r
        g = g_ref[...]
        err = hh * g - t_ref[...]
        part = 0.5 * jnp.sum(jnp.sum(err * err, axis=-1, keepdims=True) * (1.0 / d), axis=0, keepdims=True)
        dy = err * (1.0 / d)
        a = dy * g
        dh_ref[...] = r * (a - hh * jnp.mean(hh * a, axis=-1, keepdims=True))
        dg = jnp.sum(dy * hh, axis=0, keepdims=True)

        @pl.when(i == 0)
        def _():
            loss_ref[...] = part
            dg_ref[...] = dg

        @pl.when(i > 0)
        def _():
            loss_ref[...] += part
            dg_ref[...] += dg

    row = pl.BlockSpec((TM, d), lambda i: (i, 0))
    return pl.pallas_call(
        body, name="loss_head", grid=(s // TM,),
        in_specs=[row, row, pl.BlockSpec((1, d), lambda i: (0, 0))],
        out_specs=[row, pl.BlockSpec((1, 1), lambda i: (0, 0)), pl.BlockSpec((1, d), lambda i: (0, 0))],
        out_shape=[jax.ShapeDtypeStruct((s, d), F32), jax.ShapeDtypeStruct((1, 1), F32),
                   jax.ShapeDtypeStruct((1, d), F32)],
        compiler_params=_params(("arbitrary",)),
    )(h2, target, gain)


def _silu_parts(g):
    sig = jax.nn.sigmoid(g)
    return g * sig, sig * (1.0 + g * (1.0 - sig))


def _lane_lo(rows):
    return lax.broadcasted_iota(jnp.int32, (rows, LANES), 1) < HEAD_DIM


def _stack_pair(x):
    lo = _lane_lo(x.shape[0])
    zero = jnp.zeros_like(x)
    return jnp.concatenate([jnp.where(lo, x, zero), jnp.where(lo, zero, x)], axis=0)


def _unstack_pair(y, w):
    return jnp.where(_lane_lo(w), y[:w], y[w:])


def _block_valid(b, left_blocks, width):
    col = lax.broadcasted_iota(jnp.int32, (1, 2 * width), 1)
    col = jnp.where(col >= width, col - width, col)
    return (col // KB + (b - left_blocks)) >= 0


def _toeplitz_tile(diag_row, width, left_chunks):
    wide = width + TQ
    rolled = pltpu.roll(jnp.broadcast_to(diag_row, (TQ, wide)), 1, 1, stride=1, stride_axis=0)
    i = lax.broadcasted_iota(jnp.int32, (TQ, width), 0) // CHUNK
    j = lax.broadcasted_iota(jnp.int32, (TQ, width), 1) // CHUNK
    dc = i + left_chunks - j
    return jnp.where((dc >= 0) & (dc <= left_chunks), rolled[:, TQ:], MASKED)


def _toeplitz_sum(tile, width):
    flip = (lax.broadcasted_iota(jnp.int32, (TQ, TQ), 0) + lax.broadcasted_iota(jnp.int32, (TQ, TQ), 1)
            == TQ - 1).astype(F32)
    reversed_rows = jnp.dot(flip, tile, precision=lax.Precision.HIGHEST, preferred_element_type=F32)
    padded = jnp.concatenate([reversed_rows, jnp.zeros((TQ, TQ), F32)], axis=1)
    rolled = pltpu.roll(padded, 0, 1, stride=1, stride_axis=0)
    return jnp.sum(rolled, axis=0, keepdims=True)


def _softmax_pair(sc, w, sink=None):
    ps, lses = [], []
    for e in range(2):
        sh = sc[:, e * w:(e + 1) * w]
        m = jnp.max(sh, axis=-1, keepdims=True)
        if sink is not None:
            m = jnp.maximum(m, sink[e])
        ex = jnp.exp(sh - m)
        l = jnp.sum(ex, axis=-1, keepdims=True)
        if sink is not None:
            l = l + jnp.exp(sink[e] - m)
        ps.append((ex * (1.0 / l)).astype(BF16))
        lses.append(m + jnp.log(l))
    return jnp.concatenate(ps, axis=-1), lses


def _softmax_pair_bwd(sc, dp, lse, delta, w):
    ps, dss = [], []
    for e in range(2):
        p = jnp.exp(sc[:, e * w:(e + 1) * w] - lse[e])
        ps.append(p)
        dss.append(p * (dp[:, e * w:(e + 1) * w] - delta[e]))
    return jnp.concatenate(ps, axis=-1), jnp.concatenate(dss, axis=-1)


def _pair_rowsums(x, lo):
    zero = jnp.zeros_like(x)
    return (jnp.sum(jnp.where(lo, x, zero), axis=-1, keepdims=True),
            jnp.sum(jnp.where(lo, zero, x), axis=-1, keepdims=True))


def _a_kv_specs(left, pw):
    specs = []
    for which in (1, 2):
        for t in range(A_KBLOCKS):
            specs.append(pl.BlockSpec(
                (None, KB, pw), functools.partial(
                    lambda p, b, which, t: (which, jnp.maximum(b - left + t, 0), p), which=which, t=t)))
    return specs


def _attn_a_fwd(zqkv, g, diag, hosted=None):
    s = g.shape[0]
    nb = s // TQ
    left = A_KBLOCKS - 1
    pw = A_PAIRS * LANES
    wide = A_WIN + TQ

    def body(q_ref, *refs):
        k_refs = refs[:A_KBLOCKS]
        v_refs = refs[A_KBLOCKS:2 * A_KBLOCKS]
        g_ref, diag_ref, o_ref, u_ref, lse_ref, bias_scr = refs[2 * A_KBLOCKS:]
        b = pl.program_id(1)

        @pl.when(b == 0)
        def _():
            for hh in range(2 * A_PAIRS):
                bias_scr[hh // 2, :, (hh % 2) * A_WIN:(hh % 2 + 1) * A_WIN] = _toeplitz_tile(
                    diag_ref[hh], A_WIN, A_LEFT_CHUNKS)

        valid = _block_valid(b, left, A_WIN)
        lo = _lane_lo(TQ)
        for pp in range(A_PAIRS):
            ln = slice(pp * LANES, (pp + 1) * LANES)
            kcat = _stack_pair(jnp.concatenate([r[:, ln] for r in k_refs], axis=0))
            vcat = _stack_pair(jnp.concatenate([r[:, ln] for r in v_refs], axis=0))
            sc = lax.dot_general(q_ref[:, ln], kcat, NT, preferred_element_type=F32) * SCALE + bias_scr[pp]
            sc = jnp.where(valid, sc, MASKED)
            p, lses = _softmax_pair(sc, A_WIN)
            ov = jnp.dot(p, vcat, preferred_element_type=F32)
            o_ref[:, ln] = ov
            lse_ref[pp] = jnp.where(lo, lses[0], lses[1])
            sg, _ = _silu_parts(g_ref[:, ln])
            u_ref[:, ln] = (ov * sg).astype(BF16)

    tile = pl.BlockSpec((TQ, pw), lambda p, b: (b, p))
    return _call(
        body, name="attn_a_fwd", grid=(HEADS // 2 // A_PAIRS, nb),
        in_specs=[pl.BlockSpec((None, TQ, pw), lambda p, b: (0, b, p))] + _a_kv_specs(left, pw) + [
            tile, pl.BlockSpec((2 * A_PAIRS, 1, wide), lambda p, b: (p, 0, 0))],
        out_specs=[tile, tile, pl.BlockSpec((A_PAIRS, TQ, LANES), lambda p, b: (p, b, 0))],
        out_shape=[jax.ShapeDtypeStruct((s, D_MODEL), F32), jax.ShapeDtypeStruct((s, D_MODEL), BF16),
                   jax.ShapeDtypeStruct((HEADS // 2, s, LANES), F32)],
        scratch_shapes=[pltpu.VMEM((A_PAIRS, TQ, 2 * A_WIN), F32)],
        sem=("parallel", "arbitrary"), hosted=hosted,
        args=(zqkv, *([zqkv] * (2 * A_KBLOCKS)), g, diag))


def _attn_a_bwd(zqkv, g, o, du, lse, diag, hosted=None):
    s = g.shape[0]
    nb = s // TQ
    left = A_KBLOCKS - 1
    pw = A_PAIRS * LANES
    wide = A_WIN + TQ

    def body(q_ref, *refs):
        k_refs = refs[:A_KBLOCKS]
        v_refs = refs[A_KBLOCKS:2 * A_KBLOCKS]
        (g_ref, o_ref, du_ref, lse_ref, diag_ref, dz_ref, ddiag_ref,
         bias_scr, dbias_acc, dk_acc, dv_acc) = refs[2 * A_KBLOCKS:]
        b = pl.program_id(1)

        @pl.when(b == 0)
        def _():
            for hh in range(2 * A_PAIRS):
                bias_scr[hh // 2, :, (hh % 2) * A_WIN:(hh % 2 + 1) * A_WIN] = _toeplitz_tile(
                    diag_ref[hh], A_WIN, A_LEFT_CHUNKS)
            dbias_acc[...] = jnp.zeros_like(dbias_acc)
            dk_acc[...] = jnp.zeros_like(dk_acc)
            dv_acc[...] = jnp.zeros_like(dv_acc)

        valid = _block_valid(b, left, A_WIN)
        lo = _lane_lo(TQ)
        rows = pl.ds(pl.multiple_of(b * TQ, TQ), TQ)
        sg, dsg = _silu_parts(g_ref[...])
        duv = du_ref[...]
        ov = o_ref[...]
        do = duv * sg
        dz_ref[3, rows, :] = (duv * ov * dsg).astype(BF16)
        do_o = do * ov
        do_bf = do.astype(BF16)
        for pp in range(A_PAIRS):
            ln = slice(pp * LANES, (pp + 1) * LANES)
            q = q_ref[:, ln]
            kcat = _stack_pair(jnp.concatenate([r[:, ln] for r in k_refs], axis=0))
            vcat = _stack_pair(jnp.concatenate([r[:, ln] for r in v_refs], axis=0))
            sc = lax.dot_general(q, kcat, NT, preferred_element_type=F32) * SCALE + bias_scr[pp]
            sc = jnp.where(valid, sc, MASKED)
            lse_t = lse_ref[pp]
            dp = lax.dot_general(do_bf[:, ln], vcat, NT, preferred_element_type=F32)
            p, ds = _softmax_pair_bwd(sc, dp, (lse_t[:, 0:1], lse_t[:, HEAD_DIM:HEAD_DIM + 1]),
                                      _pair_rowsums(do_o[:, ln], lo), A_WIN)
            dbias_acc[pp] += ds
            dsq = (ds * SCALE).astype(BF16)
            dz_ref[0, rows, ln] = jnp.dot(dsq, kcat, preferred_element_type=F32).astype(BF16)
            dk = _unstack_pair(lax.dot_general(dsq, q, TN, preferred_element_type=F32), A_WIN)
            dv = _unstack_pair(lax.dot_general(p.astype(BF16), do_bf[:, ln], TN, preferred_element_type=F32),
                               A_WIN)
            for t in range(A_KBLOCKS):
                krows = pl.ds(pl.multiple_of(jnp.maximum(b - left + t, 0) * KB, KB), KB)
                dk_acc[krows, ln] += dk[t * KB:(t + 1) * KB, :]
                dv_acc[krows, ln] += dv[t * KB:(t + 1) * KB, :]

        @pl.when(b == nb - 1)
        def _():
            dz_ref[1] = dk_acc[...].astype(BF16)
            dz_ref[2] = dv_acc[...].astype(BF16)
            for hh in range(2 * A_PAIRS):
                ddiag_ref[hh] = _toeplitz_sum(
                    dbias_acc[hh // 2, :, (hh % 2) * A_WIN:(hh % 2 + 1) * A_WIN], A_WIN)

    tile = pl.BlockSpec((TQ, pw), lambda p, b: (b, p))
    diag_spec = pl.BlockSpec((2 * A_PAIRS, 1, wide), lambda p, b: (p, 0, 0))
    return _call(
        body, name="attn_a_bwd", grid=(HEADS // 2 // A_PAIRS, nb),
        in_specs=[pl.BlockSpec((None, TQ, pw), lambda p, b: (0, b, p))] + _a_kv_specs(left, pw) + [
            tile, tile, tile, pl.BlockSpec((A_PAIRS, TQ, LANES), lambda p, b: (p, b, 0)), diag_spec],
        out_specs=[pl.BlockSpec((4, s, pw), lambda p, b: (0, 0, p)), diag_spec],
        out_shape=[jax.ShapeDtypeStruct((4, s, D_MODEL), BF16),
                   jax.ShapeDtypeStruct((HEADS, 1, wide), F32)],
        scratch_shapes=[pltpu.VMEM((A_PAIRS, TQ, 2 * A_WIN), F32), pltpu.VMEM((A_PAIRS, TQ, 2 * A_WIN), F32),
                        pltpu.VMEM((s, pw), F32), pltpu.VMEM((s, pw), F32)],
        sem=("parallel", "arbitrary"), hosted=hosted,
        args=(zqkv, *([zqkv] * (2 * A_KBLOCKS)), g, o, du, lse, diag))


B_STACK = B_GROUP // 2
B_KVX = 4 * LANES


def _b_head_place(h):
    return h // B_GROUP, (h % B_GROUP) // 2, h % 2


def _b_build_bias(diag_ref, bias_scr):
    for h in range(HEADS):
        gi, pr, e = _b_head_place(h)
        bias_scr[gi, pr * TQ:(pr + 1) * TQ, e * B_WIN:(e + 1) * B_WIN] = _toeplitz_tile(
            diag_ref[h], B_WIN, B_LEFT_CHUNKS)


def _b_stack(x, gi):
    return jnp.concatenate(
        [x[:, (B_STACK * gi + pr) * LANES:(B_STACK * gi + pr + 1) * LANES] for pr in range(B_STACK)], axis=0)


def _b_sinks(sink_ref, gi):
    return [jnp.concatenate(
        [jnp.broadcast_to(sink_ref[0:1, h:h + 1], (TQ, 1))
         for h in range(B_GROUP * gi + e, B_GROUP * (gi + 1), 2)], axis=0) for e in range(2)]


def _attn_b_fwd(qb, kvx, gate, diag, sinks):
    s = qb.shape[0]
    nb = s // TQ
    left = B_KBLOCKS - 1
    rows4 = B_STACK * TQ

    def body(q_ref, *refs):
        kv_refs = refs[:B_KBLOCKS]
        g_ref, diag_ref, sink_ref, o_ref, u_ref, lse_ref, bias_scr = refs[B_KBLOCKS:]
        b = pl.program_id(0)

        @pl.when(b == 0)
        def _():
            _b_build_bias(diag_ref, bias_scr)

        kvv = jnp.concatenate([r[...] for r in kv_refs], axis=0)
        valid = _block_valid(b, left, B_WIN)
        lo = _lane_lo(rows4)
        for gi in range(B_KV_HEADS):
            kcat = _stack_pair(kvv[:, gi * LANES:(gi + 1) * LANES])
            vcat = _stack_pair(kvv[:, (B_KV_HEADS + gi) * LANES:(B_KV_HEADS + gi + 1) * LANES])
            qs = _b_stack(q_ref, gi)
            sc = lax.dot_general(qs, kcat, NT, preferred_element_type=F32) * SCALE + bias_scr[gi]
            sc = jnp.where(valid, sc, MASKED)
            p, lses = _softmax_pair(sc, B_WIN, _b_sinks(sink_ref, gi))
            ov = jnp.dot(p, vcat, preferred_element_type=F32)
            lse_t = jnp.where(lo, lses[0], lses[1])
            for pr in range(B_STACK):
                pair = B_STACK * gi + pr
                o_ref[:, pair * LANES:(pair + 1) * LANES] = ov[pr * TQ:(pr + 1) * TQ]
                lse_ref[pair] = lse_t[pr * TQ:(pr + 1) * TQ]
        sg, _ = _silu_parts(g_ref[...])
        u_ref[...] = (o_ref[...] * sg).astype(BF16)

    kv_specs = [pl.BlockSpec((KB, B_KVX), functools.partial(
        lambda b, t: (jnp.maximum(b - left + t, 0), 0), t=t)) for t in range(B_KBLOCKS)]
    row = pl.BlockSpec((TQ, D_MODEL), lambda b: (b, 0))
    return pl.pallas_call(
        body, name="attn_b_fwd", grid=(nb,),
        in_specs=[row] + kv_specs + [row, pl.BlockSpec((HEADS, 1, B_WIN + TQ), lambda b: (0, 0, 0)),
                                     pl.BlockSpec((1, HEADS), lambda b: (0, 0))],
        out_specs=[row, row, pl.BlockSpec((HEADS // 2, TQ, LANES), lambda b: (0, b, 0))],
        out_shape=[jax.ShapeDtypeStruct((s, D_MODEL), F32), jax.ShapeDtypeStruct((s, D_MODEL), BF16),
                   jax.ShapeDtypeStruct((HEADS // 2, s, LANES), F32)],
        scratch_shapes=[pltpu.VMEM((B_KV_HEADS, rows4, 2 * B_WIN), F32)],
        compiler_params=_params(("arbitrary",)),
    )(qb, *([kvx] * B_KBLOCKS), gate, diag, sinks)


def _attn_b_bwd(qb, kvx, gate, o, du, lse, diag, sinks):
    s = qb.shape[0]
    nb = s // TQ
    left = B_KBLOCKS - 1
    rows4 = B_STACK * TQ
    half = D_MODEL // 2

    def body(q_ref, *refs):
        kv_refs = refs[:B_KBLOCKS]
        (g_ref, o_ref, du_ref, lse_ref, diag_ref, sink_ref, dz_ref, dkv_ref, ddiag_ref, dsink_ref,
         bias_scr, dbias_acc, dkv_acc, dsink_acc) = refs[B_KBLOCKS:]
        b = pl.program_id(0)

        @pl.when(b == 0)
        def _():
            _b_build_bias(diag_ref, bias_scr)
            dbias_acc[...] = jnp.zeros_like(dbias_acc)
            dkv_acc[...] = jnp.zeros_like(dkv_acc)
            dsink_acc[...] = jnp.zeros_like(dsink_acc)

        kvv = jnp.concatenate([r[...] for r in kv_refs], axis=0)
        valid = _block_valid(b, left, B_WIN)
        lo = _lane_lo(rows4)
        sg, dsg = _silu_parts(g_ref[...])
        duv = du_ref[...]
        ov = o_ref[...]
        do = duv * sg
        dgate = (duv * ov * dsg).astype(BF16)
        dz_ref[2] = dgate[:, :half]
        dz_ref[3] = dgate[:, half:]
        do_o = do * ov
        do_bf = do.astype(BF16)
        for gi in range(B_KV_HEADS):
            kcat = _stack_pair(kvv[:, gi * LANES:(gi + 1) * LANES])
            vcat = _stack_pair(kvv[:, (B_KV_HEADS + gi) * LANES:(B_KV_HEADS + gi + 1) * LANES])
            qs = _b_stack(q_ref, gi)
            dos = _b_stack(do_bf, gi)
            delta = _pair_rowsums(_b_stack(do_o, gi), lo)
            lse_t = jnp.concatenate([lse_ref[B_STACK * gi + pr] for pr in range(B_STACK)], axis=0)
            lse2 = (lse_t[:, 0:1], lse_t[:, HEAD_DIM:HEAD_DIM + 1])
            sink = _b_sinks(sink_ref, gi)
            sc = lax.dot_general(qs, kcat, NT, preferred_element_type=F32) * SCALE + bias_scr[gi]
            sc = jnp.where(valid, sc, MASKED)
            dp = lax.dot_general(dos, vcat, NT, preferred_element_type=F32)
            p, ds = _softmax_pair_bwd(sc, dp, lse2, delta, B_WIN)
            dbias_acc[gi] += ds
            dsink_acc[gi] += jnp.where(lo, -jnp.exp(sink[0] - lse2[0]) * delta[0],
                                       -jnp.exp(sink[1] - lse2[1]) * delta[1])
            dsq = (ds * SCALE).astype(BF16)
            dq = jnp.dot(dsq, kcat, preferred_element_type=F32).astype(BF16)
            for pr in range(B_STACK):
                dz_ref[gi, :, pr * LANES:(pr + 1) * LANES] = dq[pr * TQ:(pr + 1) * TQ]
            dk = _unstack_pair(lax.dot_general(dsq, qs, TN, preferred_element_type=F32), B_WIN)
            dv = _unstack_pair(lax.dot_general(p.astype(BF16), dos, TN, preferred_element_type=F32), B_WIN)
            for t in range(B_KBLOCKS):
                krows = pl.ds(pl.multiple_of(jnp.maximum(b - left + t, 0) * KB, KB), KB)
                dkv_acc[krows, gi * LANES:(gi + 1) * LANES] += dk[t * KB:(t + 1) * KB, :]
                dkv_acc[krows, (B_KV_HEADS + gi) * LANES:(B_KV_HEADS + gi + 1) * LANES] += dv[t * KB:(t + 1) * KB, :]

        @pl.when(b == nb - 1)
        def _():
            lo_s = _lane_lo(s)
            for which in range(2):
                folded = []
                for gi in range(B_KV_HEADS):
                    part = dkv_acc[:, (which * B_KV_HEADS + gi) * LANES:(which * B_KV_HEADS + gi + 1) * LANES]
                    folded.append(part + pltpu.roll(part, HEAD_DIM, 1))
                dkv_ref[:, which * LANES:(which + 1) * LANES] = jnp.where(lo_s, folded[0], folded[1]).astype(BF16)
            lane8 = lax.broadcasted_iota(jnp.int32, dsink_ref.shape, 1)
            tot = jnp.zeros(dsink_ref.shape, F32)
            for h in range(HEADS):
                gi, pr, e = _b_head_place(h)
                ddiag_ref[h] = _toeplitz_sum(
                    dbias_acc[gi, pr * TQ:(pr + 1) * TQ, e * B_WIN:(e + 1) * B_WIN], B_WIN)
                col = dsink_acc[gi, pr * TQ:(pr + 1) * TQ, e * HEAD_DIM:e * HEAD_DIM + 1]
                tot = jnp.where(lane8 == h, jnp.sum(col, axis=0, keepdims=True), tot)
            dsink_ref[...] = tot

    kv_specs = [pl.BlockSpec((KB, B_KVX), functools.partial(
        lambda b, t: (jnp.maximum(b - left + t, 0), 0), t=t)) for t in range(B_KBLOCKS)]
    row = pl.BlockSpec((TQ, D_MODEL), lambda b: (b, 0))
    diag_spec = pl.BlockSpec((HEADS, 1, B_WIN + TQ), lambda b: (0, 0, 0))
    return pl.pallas_call(
        body, name="attn_b_bwd", grid=(nb,),
        in_specs=[row] + kv_specs + [row, row, row, pl.BlockSpec((HEADS // 2, TQ, LANES), lambda b: (0, b, 0)),
                                     diag_spec, pl.BlockSpec((1, HEADS), lambda b: (0, 0))],
        out_specs=[pl.BlockSpec((4, TQ, half), lambda b: (0, b, 0)),
                   pl.BlockSpec((s, 2 * LANES), lambda b: (0, 0)), diag_spec,
                   pl.BlockSpec((8, LANES), lambda b: (0, 0))],
        out_shape=[jax.ShapeDtypeStruct((4, s, half), BF16), jax.ShapeDtypeStruct((s, 2 * LANES), BF16),
                   jax.ShapeDtypeStruct((HEADS, 1, B_WIN + TQ), F32), jax.ShapeDtypeStruct((8, LANES), F32)],
        scratch_shapes=[pltpu.VMEM((B_KV_HEADS, rows4, 2 * B_WIN), F32),
                        pltpu.VMEM((B_KV_HEADS, rows4, 2 * B_WIN), F32),
                        pltpu.VMEM((s, B_KVX), F32), pltpu.VMEM((B_KV_HEADS, rows4, LANES), F32)],
        compiler_params=_params(("arbitrary",)),
    )(qb, *([kvx] * B_KBLOCKS), gate, o, du, lse, diag, sinks)


def _t5_bucket(rel):
    nb = T5_BUCKETS // 2
    max_exact = nb // 2
    ret = jnp.where(rel > 0, nb, 0)
    n = jnp.abs(rel)
    nf = jnp.maximum(n, 1).astype(jnp.float32)
    large = max_exact + (jnp.log(nf / max_exact) / math.log(T5_MAX_DIST / max_exact)
                         * (nb - max_exact)).astype(jnp.int32)
    large = jnp.minimum(large, nb - 1)
    return ret + jnp.where(n < max_exact, n, large)


def _a_offset_onehot():
    c = np.arange(A_WIN + TQ)
    dist = A_LEFT_CHUNKS * CHUNK + TQ - 1 - c
    idx = np.clip(dist, -A_REL_CLIP, A_REL_CLIP) + A_REL_CLIP
    onehot = np.zeros((A_WIN + TQ, 2 * A_REL_CLIP + 1), np.float32)
    onehot[c, idx] = 1.0
    return jnp.asarray(onehot)


def _b_offset_onehot():
    c = jnp.arange(B_WIN + TQ, dtype=jnp.int32)
    rel = c - (TQ - 1) - B_LEFT_CHUNKS * CHUNK
    return (_t5_bucket(rel)[:, None] == jnp.arange(T5_BUCKETS)[None, :]).astype(F32)


def _diag_rows(onehot, table):
    rows = jnp.dot(onehot, table.astype(F32), precision=lax.Precision.HIGHEST)
    return rows.T.reshape(HEADS, 1, onehot.shape[0])


def _diag_rows_grad(onehot, ddiag):
    return jnp.dot(ddiag.reshape(HEADS, onehot.shape[0]), onehot, precision=lax.Precision.HIGHEST).T


def _position():
    x, y, c = lax.axis_index("x"), lax.axis_index("y"), lax.axis_index("c")
    chips = [(1 - x, y), (x, 1 - y), (1 - x, 1 - y)]
    return x, y, c, chips


ANY = pl.BlockSpec(memory_space=pl.ANY)


def _allgather_hosted(shards, split):
    n = len(shards)

    def part(ref, t, half):
        if not split[t]:
            return ref
        rows = shards[t].shape[0] // 2
        return ref.at[pl.ds(half * rows, rows)]

    def copies(kind, ins, outs, sems):
        send_sems, recv_sems, pass_send, pass_recv, local_sems = sems
        x, y, c, chips = _position()
        mine = 2 * x + y
        if kind == "local":
            return [pltpu.make_async_copy(ins[t], outs[t].at[mine], local_sems.at[t]) for t in range(n)]
        made = []
        for t in range(n):
            for j, chip in enumerate(chips):
                theirs = 2 * chip[0] + chip[1]
                far = dict(send_sem=send_sems.at[3 * t + j], recv_sem=recv_sems.at[3 * t + j],
                           device_id=(chip[0], chip[1], c), device_id_type=MESH)
                near = dict(send_sem=pass_send.at[3 * t + j], recv_sem=pass_recv.at[3 * t + j],
                            device_id=(x, y, 1 - c), device_id_type=MESH)
                here = part(outs[t].at[theirs], t, c)
                if kind == "send":
                    made.append(pltpu.make_async_remote_copy(
                        src_ref=part(ins[t], t, c), dst_ref=part(outs[t].at[mine], t, c), **far))
                elif kind == "landed":
                    made.append(pltpu.make_async_remote_copy(src_ref=here, dst_ref=here, **far))
                elif not split[t]:
                    made.append(None)
                elif kind == "pass":
                    made.append(pltpu.make_async_remote_copy(src_ref=here, dst_ref=here, **near))
                else:
                    other = part(outs[t].at[theirs], t, 1 - c)
                    made.append(pltpu.make_async_remote_copy(src_ref=other, dst_ref=other, **near))
        return made

    def first(ins, outs, sems):
        for cp in copies("local", ins, outs, sems) + copies("send", ins, outs, sems):
            cp.start()

    def middle(ins, outs, sems):
        for got, cp in zip(copies("landed", ins, outs, sems), copies("pass", ins, outs, sems)):
            got.wait_recv()
            if cp is not None:
                cp.start()

    def last(ins, outs, sems):
        for cp in copies("passed", ins, outs, sems):
            if cp is not None:
                cp.wait_recv()
        for cp in copies("send", ins, outs, sems) + copies("pass", ins, outs, sems):
            if cp is not None:
                cp.wait_send()
        for cp in copies("local", ins, outs, sems):
            cp.wait()

    return _Hosted(shards, [jax.ShapeDtypeStruct((4,) + w.shape, w.dtype) for w in shards],
                   [pltpu.SemaphoreType.DMA((3 * n,))] * 4 + [pltpu.SemaphoreType.DMA((n,))],
                   first, middle, last)


def _scatter_hosted(grads):
    n = len(grads)

    def copies(ins, outs, sems):
        send_sems, recv_sems = sems
        x, y, c, chips = _position()
        return [pltpu.make_async_remote_copy(
            src_ref=ins[t].at[2 * chip[0] + chip[1]], dst_ref=outs[t].at[j],
            send_sem=send_sems.at[3 * t + j], recv_sem=recv_sems.at[3 * t + j],
            device_id=(chip[0], chip[1], c), device_id_type=MESH)
            for t in range(n) for j, chip in enumerate(chips)]

    def first(ins, outs, sems):
        for cp in copies(ins, outs, sems):
            cp.start()

    def last(ins, outs, sems):
        for cp in copies(ins, outs, sems):
            cp.wait()

    return _Hosted(grads, [jax.ShapeDtypeStruct((3,) + g.shape[1:], g.dtype) for g in grads],
                   [pltpu.SemaphoreType.DMA((3 * n,))] * 2, first, None, last)


def _run_alone(name, hosted):
    n_in = len(hosted.inputs)
    n_out = len(hosted.out_shapes)

    def body(*refs):
        ins, outs, sems = refs[:n_in], refs[n_in:n_in + n_out], refs[n_in + n_out:]
        hosted.first(ins, outs, sems)
        if hosted.middle is not None:
            hosted.middle(ins, outs, sems)
        hosted.last(ins, outs, sems)

    return pl.pallas_call(
        body, name=name, in_specs=[ANY] * n_in, out_specs=[ANY] * n_out, out_shape=hosted.out_shapes,
        scratch_shapes=hosted.sems)(*hosted.inputs)


def _swap_with_sibling(blocks):
    n = len(blocks)

    def body(*refs):
        ins, outs = refs[:n], refs[n:2 * n]
        send_sems, recv_sems = refs[2 * n:]
        x, y, c, _ = _position()
        sends = [pltpu.make_async_remote_copy(
            src_ref=ins[t], dst_ref=outs[t], send_sem=send_sems.at[t], recv_sem=recv_sems.at[t],
            device_id=(x, y, 1 - c), device_id_type=MESH) for t in range(n)]
        for cp in sends:
            cp.start()
        for cp in sends:
            cp.wait()

    return pl.pallas_call(
        body, name="swap_with_sibling",
        in_specs=[ANY] * n, out_specs=[ANY] * n,
        out_shape=[jax.ShapeDtypeStruct(b.shape, b.dtype) for b in blocks],
        scratch_shapes=[pltpu.SemaphoreType.DMA((n,))] * 2,
    )(*blocks)


def _allreduce_small(block):
    rows = block.shape[0]

    def body(in_ref, sum_ref, all_ref, send_sems, recv_sems):
        x, y, c, _ = _position()
        me = 4 * x + 2 * y + c
        all_ref[me] = in_ref[...]
        sends = []
        for k in range(1, 8):
            peer = (x ^ (k >> 2), y ^ ((k >> 1) & 1), c ^ (k & 1))
            sends.append(pltpu.make_async_remote_copy(
                src_ref=in_ref, dst_ref=all_ref.at[me], send_sem=send_sems.at[k - 1],
                recv_sem=recv_sems.at[k - 1], device_id=peer, device_id_type=MESH))
        for cp in sends:
            cp.start()
        for k in range(1, 8):
            theirs = me ^ k
            pltpu.make_async_remote_copy(
                src_ref=in_ref, dst_ref=all_ref.at[theirs], send_sem=send_sems.at[k - 1],
                recv_sem=recv_sems.at[k - 1], device_id=(x, y, c), device_id_type=MESH).wait_recv()
        for cp in sends:
            cp.wait_send()
        acc = all_ref[0]
        for d in range(1, 8):
            acc = acc + all_ref[d]
        sum_ref[...] = acc

    vmem = pl.BlockSpec(memory_space=pltpu.VMEM)
    return pl.pallas_call(
        body, name="allreduce_small",
        in_specs=[vmem], out_specs=[vmem, vmem],
        out_shape=[jax.ShapeDtypeStruct((rows, LANES), F32), jax.ShapeDtypeStruct((8, rows, LANES), F32)],
        scratch_shapes=[pltpu.SemaphoreType.DMA((7,))] * 2,
    )(block)[0]


def _adamw_math(w, g, m, v):
    m = ADAM_B1 * m + (1.0 - ADAM_B1) * g
    v = ADAM_B2 * v + (1.0 - ADAM_B2) * (g * g)
    m_hat = m / (1.0 - ADAM_B1 ** ADAM_STEP)
    v_hat = v / (1.0 - ADAM_B2 ** ADAM_STEP)
    delta = -ADAM_LR * (m_hat / (jnp.sqrt(v_hat) + ADAM_EPS) + ADAM_WD * w)
    return delta, m, v


def _row_tile(rows):
    return min(rows, 256)


def _sum_partials(name, own, recv):
    rows, cols = own.shape
    tr = _row_tile(rows)

    def body(own_ref, recv_ref, o_ref):
        acc = own_ref[...]
        for j in range(3):
            acc = acc + recv_ref[j].astype(F32)
        o_ref[...] = acc

    return pl.pallas_call(
        body, name=name, grid=(rows // tr,),
        in_specs=[pl.BlockSpec((tr, cols), lambda i: (i, 0)), pl.BlockSpec((3, tr, cols), lambda i: (0, i, 0))],
        out_specs=pl.BlockSpec((tr, cols), lambda i: (i, 0)),
        out_shape=jax.ShapeDtypeStruct((rows, cols), F32),
        compiler_params=_params(("parallel",)),
    )(own, recv)


def _adamw(name, w, m, v, g_parts):
    rows, cols = w.shape
    tr = _row_tile(rows)
    n = len(g_parts)

    def body(w_ref, m_ref, v_ref, *refs):
        g_refs = refs[:n]
        go_ref, d_ref, mo_ref, vo_ref = refs[n:]
        g = g_refs[0][...]
        for r in g_refs[1:]:
            g = g + r[...]
        delta, mn, vn = _adamw_math(w_ref[...], g, m_ref[...], v_ref[...])
        go_ref[...] = g
        d_ref[...] = delta
        mo_ref[...] = mn
        vo_ref[...] = vn

    spec = pl.BlockSpec((tr, cols), lambda i: (i, 0))
    return pl.pallas_call(
        body, name=name, grid=(rows // tr,),
        in_specs=[spec] * (3 + n), out_specs=[spec] * 4,
        out_shape=[jax.ShapeDtypeStruct((rows, cols), F32)] * 4,
        compiler_params=_params(("parallel",)),
    )(w, m, v, *g_parts)


def _local_step(x, target, ga, wa_in, rel_bias, later_shards, gk, t5, gb, sinks, gf):
    s, d = x.shape
    nt = s // TM
    half = d // 2
    row = pl.BlockSpec((TM, d), lambda i: (i, 0))
    whole = lambda shape: pl.BlockSpec(shape, lambda *_: (0,) * len(shape))

    n1, = _norm_fwd("norm_a", x, ga)
    zqkv = _matmul("proj_a_qkv", n1, wa_in, dims=NN, grid=(3, nt),
                   a_spec=pl.BlockSpec((TM, d), lambda j, i: (i, 0)),
                   b_spec=pl.BlockSpec((None, d, d), lambda j, i: (j, 0, 0)),
                   o_spec=pl.BlockSpec((None, TM, d), lambda j, i: (j, i, 0)),
                   out_shape=(3, s, d), out_dtype=BF16)
    gate_a = _matmul("proj_a_gate", n1, wa_in, dims=NN, grid=(nt,),
                     a_spec=row, b_spec=pl.BlockSpec((None, d, d), lambda i: (3, 0, 0)), o_spec=row,
                     out_shape=(s, d), out_dtype=F32)
    onehot_a = _a_offset_onehot()
    diag_a = _diag_rows(onehot_a, rel_bias)
    (o_a, u_a, lse_a), gathered = _attn_a_fwd(
        zqkv, gate_a, diag_a, hosted=_allgather_hosted(later_shards, [True] * len(later_shards)))
    wa_out, wkv, wb_in, wb_out = gathered
    wa_out = wa_out.reshape(d, d)
    wkv = wkv.reshape(d, -1)
    wb_out = wb_out.reshape(d, d)
    h1 = _matmul("out_a", u_a, wa_out, dims=NN, grid=(nt,), a_spec=row, b_spec=whole((d, d)), o_spec=row,
                 out_shape=(s, d), out_dtype=F32, resid=x, resid_spec=row)

    nk, n2 = _norm_fwd("norm_kv_b", h1, jnp.concatenate([gk, gb], axis=0))
    kvw = wkv.shape[1]
    wkv_x = jnp.concatenate([wkv[:, (i // 2) * HEAD_DIM:(i // 2 + 1) * HEAD_DIM] for i in range(8)], axis=1)
    kvx = _matmul("proj_kv", nk, wkv_x, dims=NN, grid=(nt,), a_spec=row, b_spec=whole((d, B_KVX)),
                  o_spec=pl.BlockSpec((TM, B_KVX), lambda i: (i, 0)), out_shape=(s, B_KVX), out_dtype=BF16)
    qb = _matmul("proj_b_q", n2, wb_in, dims=NN, grid=(2, nt),
                 a_spec=pl.BlockSpec((TM, d), lambda j, i: (i, 0)),
                 b_spec=pl.BlockSpec((None, d, half), lambda j, i: (j, 0, 0)),
                 o_spec=pl.BlockSpec((TM, half), lambda j, i: (i, j)), out_shape=(s, d), out_dtype=BF16)
    gate_b = _matmul("proj_b_gate", n2, wb_in, dims=NN, grid=(2, nt),
                     a_spec=pl.BlockSpec((TM, d), lambda j, i: (i, 0)),
                     b_spec=pl.BlockSpec((None, d, half), lambda j, i: (2 + j, 0, 0)),
                     o_spec=pl.BlockSpec((TM, half), lambda j, i: (i, j)), out_shape=(s, d), out_dtype=F32)
    onehot_b = _b_offset_onehot()
    diag_b = _diag_rows(onehot_b, t5)
    o_b, u_b, lse_b = _attn_b_fwd(qb, kvx, gate_b, diag_b, sinks)
    h2 = _matmul("out_b", u_b, wb_out, dims=NN, grid=(nt,), a_spec=row, b_spec=whole((d, d)), o_spec=row,
                 out_shape=(s, d), out_dtype=F32, resid=h1, resid_spec=row)

    dh2, loss, d_gf = _loss_head(h2, target, gf)

    du_b = _matmul("dout_b", dh2, wb_out, dims=NT, grid=(nt,), a_spec=row, b_spec=whole((d, d)), o_spec=row,
                   out_shape=(s, d), out_dtype=F32)
    d_wb_out = _matmul("dw_out_b", u_b, dh2, dims=TN, grid=(2, nt),
                       a_spec=pl.BlockSpec((TM, d), lambda j, k: (k, 0)),
                       b_spec=pl.BlockSpec((TM, half), lambda j, k: (k, j)),
                       o_spec=pl.BlockSpec((d, half), lambda j, k: (0, j)),
                       out_shape=(d, d), out_dtype=F32, k_axis=1, k_steps=nt, also_bf16=True)
    dz_b, dkv, ddiag_b, dsinks = _attn_b_bwd(qb, kvx, gate_b, o_b, du_b, lse_b, diag_b, sinks)
    dn2 = _matmul("dproj_b", dz_b, wb_in, dims=NT, grid=(nt, 4),
                  a_spec=pl.BlockSpec((None, TM, half), lambda i, k: (k, i, 0)),
                  b_spec=pl.BlockSpec((None, d, half), lambda i, k: (k, 0, 0)),
                  o_spec=pl.BlockSpec((TM, d), lambda i, k: (i, 0)),
                  out_shape=(s, d), out_dtype=F32, k_axis=1, k_steps=4)
    d_wb_in = _matmul("dw_in_b", n2, dz_b, dims=TN, grid=(4, nt),
                      a_spec=pl.BlockSpec((TM, d), lambda j, k: (k, 0)),
                      b_spec=pl.BlockSpec((None, TM, half), lambda j, k: (j, k, 0)),
                      o_spec=pl.BlockSpec((None, d, half), lambda j, k: (j, 0, 0)),
                      out_shape=(4, d, half), out_dtype=F32, k_axis=1, k_steps=nt, also_bf16=True)
    dnk = _matmul("dproj_kv", dkv, wkv, dims=NT, grid=(nt,),
                  a_spec=pl.BlockSpec((TM, kvw), lambda i: (i, 0)), b_spec=whole((d, kvw)), o_spec=row,
                  out_shape=(s, d), out_dtype=F32)
    d_wkv = _matmul("dw_kv", nk, dkv, dims=TN, grid=(nt,),
                    a_spec=pl.BlockSpec((TM, d), lambda k: (k, 0)),
                    b_spec=pl.BlockSpec((TM, kvw), lambda k: (k, 0)),
                    o_spec=pl.BlockSpec((d, kvw), lambda k: (0, 0)),
                    out_shape=(d, kvw), out_dtype=F32, k_axis=0, k_steps=nt, also_bf16=True)
    dh1, d_gkb = _norm_bwd("dnorm_kv_b", h1, dh2, [dnk, dn2], jnp.concatenate([gk, gb], axis=0))

    du_a = _matmul("dout_a", dh1, wa_out, dims=NT, grid=(nt,), a_spec=row, b_spec=whole((d, d)), o_spec=row,
                   out_shape=(s, d), out_dtype=F32)
    d_wa_out = _matmul("dw_out_a", u_a, dh1, dims=TN, grid=(2, nt),
                       a_spec=pl.BlockSpec((TM, d), lambda j, k: (k, 0)),
                       b_spec=pl.BlockSpec((TM, half), lambda j, k: (k, j)),
                       o_spec=pl.BlockSpec((d, half), lambda j, k: (0, j)),
                       out_shape=(d, d), out_dtype=F32, k_axis=1, k_steps=nt, also_bf16=True)
    early = dict(a_w_out=[g.reshape(4, d // 4, d) for g in d_wa_out],
                 kv_w=[g.reshape(4, d // 4, kvw) for g in d_wkv], b_w_in=list(d_wb_in),
                 b_w_out=[g.reshape(4, d // 4, d) for g in d_wb_out])
    (dz_a, ddiag_a), early_recv = _attn_a_bwd(
        zqkv, gate_a, o_a, du_a, lse_a, diag_a, hosted=_scatter_hosted([early[n][1] for n in early]))
    d_wa_in = _matmul("dw_in_a", n1, dz_a, dims=TN, grid=(4, 2, nt),
                      a_spec=pl.BlockSpec((TM, d), lambda j, h, k: (k, 0)),
                      b_spec=pl.BlockSpec((None, TM, half), lambda j, h, k: (j, k, h)),
                      o_spec=pl.BlockSpec((None, d, half), lambda j, h, k: (j, 0, h)),
                      out_shape=(4, d, d), out_dtype=F32, k_axis=2, k_steps=nt, also_bf16=True)
    dn1, late_recv = _matmul("dproj_a", dz_a, wa_in, dims=NT, grid=(nt, 4),
                             a_spec=pl.BlockSpec((None, TM, d), lambda i, k: (k, i, 0)),
                             b_spec=pl.BlockSpec((None, d, d), lambda i, k: (k, 0, 0)),
                             o_spec=pl.BlockSpec((TM, d), lambda i, k: (i, 0)),
                             out_shape=(s, d), out_dtype=F32, k_axis=1, k_steps=4,
                             hosted=_scatter_hosted([d_wa_in[1]]))
    grad_x, d_ga = _norm_bwd("dnorm_a", x, dh1, [dn1], ga)

    small = dict(
        a_norm=d_ga, a_rel_bias=_diag_rows_grad(onehot_a, ddiag_a), kv_norm=d_gkb[0:1],
        t5_bias=_diag_rows_grad(onehot_b, ddiag_b), b_norm=d_gkb[1:2], b_sinks=dsinks[0:1, :HEADS],
        final_norm=d_gf)
    own = dict(a_w_in=d_wa_in[0], **{n: early[n][0] for n in early})
    received = dict(a_w_in=late_recv[0], **dict(zip(early, early_recv)))
    return loss, grad_x, small, own, received


SMALL = ("a_norm", "a_rel_bias", "kv_norm", "t5_bias", "b_norm", "b_sinks", "final_norm")
BIG = ("a_w_in", "a_w_out", "kv_w", "b_w_in", "b_w_out")
ORDER = ("a_norm", "a_w_in", "a_rel_bias", "a_w_out", "kv_norm", "kv_w", "t5_bias", "b_norm", "b_w_in",
         "b_sinks", "b_w_out", "final_norm")


def _pack(parts, rows):
    flat = jnp.concatenate([p.reshape(-1).astype(F32) for p in parts])
    return jnp.pad(flat, (0, rows * LANES - flat.shape[0])).reshape(rows, LANES)


def _unpack(block, shapes):
    flat = block.reshape(-1)
    out, at = [], 0
    for shp in shapes:
        size = int(np.prod(shp))
        out.append(flat[at:at + size].reshape(shp))
        at += size
    return out


def kernel(x, a_norm, a_w_in, a_rel_bias, a_w_out, kv_norm, kv_w, t5_bias, b_norm, b_w_in, b_sinks, b_w_out, final_norm, loss_target, m_a_norm, m_a_w_in, m_a_rel_bias, m_a_w_out, m_kv_norm, m_kv_w, m_t5_bias, m_b_norm, m_b_w_in, m_b_sinks, m_b_w_out, m_final_norm, v_a_norm, v_a_w_in, v_a_rel_bias, v_a_w_out, v_kv_norm, v_kv_w, v_t5_bias, v_b_norm, v_b_w_in, v_b_sinks, v_b_w_out, v_final_norm):
    w = dict(a_norm=a_norm, a_w_in=a_w_in, a_rel_bias=a_rel_bias, a_w_out=a_w_out, kv_norm=kv_norm, kv_w=kv_w,
             t5_bias=t5_bias, b_norm=b_norm, b_w_in=b_w_in, b_sinks=b_sinks, b_w_out=b_w_out,
             final_norm=final_norm)
    m = dict(a_norm=m_a_norm, a_w_in=m_a_w_in, a_rel_bias=m_a_rel_bias, a_w_out=m_a_w_out, kv_norm=m_kv_norm,
             kv_w=m_kv_w, t5_bias=m_t5_bias, b_norm=m_b_norm, b_w_in=m_b_w_in, b_sinks=m_b_sinks,
             b_w_out=m_b_w_out, final_norm=m_final_norm)
    v = dict(a_norm=v_a_norm, a_w_in=v_a_w_in, a_rel_bias=v_a_rel_bias, a_w_out=v_a_w_out, kv_norm=v_kv_norm,
             kv_w=v_kv_w, t5_bias=v_t5_bias, b_norm=v_b_norm, b_w_in=v_b_w_in, b_sinks=v_b_sinks,
             b_w_out=v_b_w_out, final_norm=v_final_norm)
    d = D_MODEL
    chip = 2 * lax.axis_index("x") + lax.axis_index("y")

    shard2d = dict(a_w_in=a_w_in[0], a_w_out=a_w_out[0], kv_w=kv_w, b_w_in=b_w_in[0], b_w_out=b_w_out[0])

    wa_in, ga = _run_alone("allgather_first",
                           _allgather_hosted([shard2d["a_w_in"].astype(BF16), a_norm], [True, False]))
    ga = ga.reshape(1, d)

    loss, grad_x, small, own, received = _local_step(
        x[0], loss_target[0], ga, wa_in, a_rel_bias[0], [shard2d[n].astype(BF16) for n in BIG[1:]],
        kv_norm.reshape(1, d), t5_bias, b_norm, b_sinks, final_norm.reshape(1, d))

    small_shapes = [small[n].shape for n in SMALL] + [(1, 1)]
    total = sum(int(np.prod(s)) for s in small_shapes)
    rows = -(-total // (8 * LANES)) * 8
    reduced = _unpack(_allreduce_small(_pack([small[n] for n in SMALL] + [loss], rows)), small_shapes)
    g_small = dict(zip(SMALL, reduced[:-1]))
    loss_out = reduced[-1].reshape(())
    g_small["a_norm"] = lax.dynamic_slice_in_dim(g_small["a_norm"], chip * (d // 4), d // 4, axis=1)

    core_sums = [
        _sum_partials("sum_" + n, lax.dynamic_index_in_dim(own[n], chip, 0, keepdims=False), received[n])
        for n in BIG]
    sibling_sums = _swap_with_sibling(core_sums)

    out = {}
    for n, mine, theirs in zip(BIG, core_sums, sibling_sums):
        res = _adamw("adamw_" + n, shard2d[n], m[n].reshape(shard2d[n].shape), v[n].reshape(shard2d[n].shape),
                     [mine, theirs])
        out[n] = [r.reshape(w[n].shape) for r in res]
    small_w_shapes = [w[n].shape for n in SMALL]
    total_w = sum(int(np.prod(s)) for s in small_w_shapes)
    rows_w = -(-total_w // (8 * LANES)) * 8
    packed = [_pack([t[n] for n in SMALL], rows_w) for t in (w, m, v)]
    g_packed = _pack([g_small[n] for n in SMALL], rows_w)
    res = _adamw("adamw_small", packed[0], packed[1], packed[2], [g_packed])
    unpacked = [_unpack(r, small_w_shapes) for r in res]
    for i, n in enumerate(SMALL):
        out[n] = [unpacked[k][i] for k in range(4)]

    grads = [out[n][0] for n in ORDER]
    deltas = [out[n][1] for n in ORDER]
    new_m = [out[n][2] for n in ORDER]
    new_v = [out[n][3] for n in ORDER]
    return (loss_out, grad_x[None], *grads, *deltas, *new_m, *new_v)
```

```python
import functools
import math

import jax
import jax.numpy as jnp
import numpy as np
from jax import lax
from jax.experimental import pallas as pl
from jax.experimental.pallas import tpu as pltpu

F32 = jnp.float32
BF16 = jnp.bfloat16
MESH = pl.DeviceIdType.MESH

D_MODEL = 1024
HEADS = 16
HEAD_DIM = 64
CHUNK = 64
RMS_EPS = 1e-6
SCALE = HEAD_DIM ** -0.5
A_LEFT_CHUNKS = 8
A_REL_CLIP = 256
B_LEFT_CHUNKS = 2
B_KV_HEADS = 2
B_GROUP = HEADS // B_KV_HEADS
T5_BUCKETS = 32
T5_MAX_DIST = 128
ADAM_LR = 0.001
ADAM_B1 = 0.9
ADAM_B2 = 0.999
ADAM_EPS = 1e-08
ADAM_WD = 0.01
ADAM_STEP = 10

MASKED = -1e30
LANES = 128
TQ = 128
A_PAIRS = 2
KB = 128
A_KBLOCKS = A_LEFT_CHUNKS * CHUNK // KB + 1
B_KBLOCKS = B_LEFT_CHUNKS * CHUNK // KB + 1
A_WIN = A_KBLOCKS * KB
B_WIN = B_KBLOCKS * KB
TM = 512
TM_DENSE = 1024
TM_PARTS = 512
VMEM_LIMIT = 56 * 1024 * 1024

NT = (((1,), (1,)), ((), ()))
TN = (((0,), (0,)), ((), ()))
NN = (((1,), (0,)), ((), ()))


def _params(sem=None):
    return pltpu.CompilerParams(dimension_semantics=sem, vmem_limit_bytes=VMEM_LIMIT)


class _Hosted:
    def __init__(self, inputs, out_shapes, sems, first, middle, last):
        self.inputs, self.out_shapes, self.sems = list(inputs), list(out_shapes), list(sems)
        self.first, self.middle, self.last = first, middle, last


def _call(body, *, name, grid, in_specs, out_specs, out_shape, args, scratch_shapes=(), sem=None, hosted=None):
    in_specs, out_specs, out_shape = list(in_specs), list(out_specs), list(out_shape)
    scratch_shapes = list(scratch_shapes)
    if hosted is None:
        out = pl.pallas_call(
            body, name=name, grid=grid, in_specs=in_specs, out_specs=out_specs, out_shape=out_shape,
            scratch_shapes=scratch_shapes, compiler_params=_params(sem))(*args)
        return list(out), []
    n_in, n_out, n_scr = len(in_specs), len(out_shape), len(scratch_shapes)
    h_in, h_out = len(hosted.inputs), len(hosted.out_shapes)
    total = int(np.prod(grid)) if grid else 1

    def wrapped(*refs):
        ins, refs = refs[:n_in], refs[n_in:]
        h_ins, refs = refs[:h_in], refs[h_in:]
        outs, refs = refs[:n_out], refs[n_out:]
        h_outs, refs = refs[:h_out], refs[h_out:]
        scr, h_sems = refs[:n_scr], refs[n_scr:]
        step = 0
        for axis, size in enumerate(grid):
            step = step * size + pl.program_id(axis)

        @pl.when(step == 0)
        def _():
            hosted.first(h_ins, h_outs, h_sems)

        body(*ins, *outs, *scr)
        if hosted.middle is not None:
            @pl.when(step == total // 2)
            def _():
                hosted.middle(h_ins, h_outs, h_sems)

        @pl.when(step == total - 1)
        def _():
            hosted.last(h_ins, h_outs, h_sems)

    out = pl.pallas_call(
        wrapped, name=name, grid=grid, in_specs=in_specs + [ANY] * h_in, out_specs=out_specs + [ANY] * h_out,
        out_shape=out_shape + hosted.out_shapes, scratch_shapes=scratch_shapes + hosted.sems,
        compiler_params=_params(("arbitrary",) * len(grid)))(*args, *hosted.inputs)
    return list(out[:n_out]), list(out[n_out:])


def _matmul(name, a, b, *, dims, grid, a_spec, b_spec, o_spec, out_shape, out_dtype,
            parts=1, resid=None, resid_spec=None, also_bf16=False, hosted=None):
    def body(*refs):
        a_ref, b_ref = refs[:2]
        r_ref = refs[2] if resid is not None else None
        o_ref = refs[3] if resid is not None else refs[2]
        if parts == 1:
            prod = lax.dot_general(a_ref[...].astype(BF16), b_ref[...].astype(BF16), dims,
                                   preferred_element_type=F32)
        else:
            prod = None
            for part in range(parts):
                term = lax.dot_general(a_ref[part].astype(BF16), b_ref[part].astype(BF16), dims,
                                       preferred_element_type=F32)
                prod = term if prod is None else prod + term
        if resid is not None:
            prod = r_ref[...] + prod
        o_ref[...] = prod.astype(out_dtype)
        if also_bf16:
            refs[-1][...] = prod.astype(BF16)

    in_specs = [a_spec, b_spec]
    args = [a, b]
    if resid is not None:
        in_specs.append(resid_spec)
        args.append(resid)
    sem = ["parallel"] * len(grid)
    out_specs = [o_spec]
    out_shapes = [jax.ShapeDtypeStruct(out_shape, out_dtype)]
    if also_bf16:
        out_specs.append(o_spec)
        out_shapes.append(jax.ShapeDtypeStruct(out_shape, BF16))
    out, extra = _call(body, name=name, grid=grid, in_specs=in_specs, out_specs=out_specs, out_shape=out_shapes,
                       args=args, sem=tuple(sem), hosted=hosted)
    res = out[0] if not also_bf16 else tuple(out)
    return res if hosted is None else (res, extra)


def _rms_rows(x):
    return lax.rsqrt(jnp.mean(x * x, axis=-1, keepdims=True) + RMS_EPS)


def _norm_fwd(name, x, gains):
    s, d = x.shape
    n = gains.shape[0]

    def body(x_ref, g_ref, *o_refs):
        xv = x_ref[...]
        xh = xv * _rms_rows(xv)
        for i in range(n):
            o_refs[i][...] = (xh * g_ref[i:i + 1, :]).astype(BF16)

    row = pl.BlockSpec((TM, d), lambda i: (i, 0))
    return pl.pallas_call(
        body, name=name, grid=(s // TM,),
        in_specs=[row, pl.BlockSpec((n, d), lambda i: (0, 0))],
        out_specs=[row] * n,
        out_shape=[jax.ShapeDtypeStruct((s, d), BF16)] * n,
        compiler_params=_params(("parallel",)),
    )(x, gains)


def _norm_bwd(name, x, dres, dns, gains):
    s, d = x.shape
    n = len(dns)

    def body(x_ref, r_ref, g_ref, *refs):
        dn_refs, dx_ref, dg_ref = refs[:n], refs[n], refs[n + 1]
        i = pl.program_id(0)
        xv = x_ref[...]
        r = _rms_rows(xv)
        xh = xv * r

        @pl.when(i == 0)
        def _():
            dg_ref[...] = jnp.zeros_like(dg_ref)

        a = None
        for j in range(n):
            dn = dn_refs[j][...]
            t = dn * g_ref[j:j + 1, :]
            a = t if a is None else a + t
            dg_ref[j:j + 1, :] += jnp.sum(dn * xh, axis=0, keepdims=True)
        dx_ref[...] = r_ref[...] + r * (a - xh * jnp.mean(xh * a, axis=-1, keepdims=True))

    row = pl.BlockSpec((TM, d), lambda i: (i, 0))
    small = pl.BlockSpec((n, d), lambda i: (0, 0))
    return pl.pallas_call(
        body, name=name, grid=(s // TM,),
        in_specs=[row, row, small] + [row] * n,
        out_specs=[row, small],
        out_shape=[jax.ShapeDtypeStruct((s, d), F32), jax.ShapeDtypeStruct((n, d), F32)],
        compiler_params=_params(("arbitrary",)),
    )(x, dres, gains, *dns)


def _loss_head(h2, target, gain):
    s, d = h2.shape

    def body(h_ref, t_ref, g_ref, dh_ref, loss_ref, dg_ref):
        i = pl.program_id(0)
        hv = h_ref[...]
        r = _rms_rows(hv)
        hh = hv * r
        g = g_ref[...]
        err = hh * g - t_ref[...]
        part = 0.5 * jnp.sum(jnp.sum(err * err, axis=-1, keepdims=True) * (1.0 / d), axis=0, keepdims=True)
        dy = err * (1.0 / d)
        a = dy * g
        dh_ref[...] = r * (a - hh * jnp.mean(hh * a, axis=-1, keepdims=True))
        dg = jnp.sum(dy * hh, axis=0, keepdims=True)

        @pl.when(i == 0)
        def _():
            loss_ref[...] = part
            dg_ref[...] = dg

        @pl.when(i > 0)
        def _():
            loss_ref[...] += part
            dg_ref[...] += dg

    row = pl.BlockSpec((TM, d), lambda i: (i, 0))
    return pl.pallas_call(
        body, name="loss_head", grid=(s // TM,),
        in_specs=[row, row, pl.BlockSpec((1, d), lambda i: (0, 0))],
        out_specs=[row, pl.BlockSpec((1, 1), lambda i: (0, 0)), pl.BlockSpec((1, d), lambda i: (0, 0))],
        out_shape=[jax.ShapeDtypeStruct((s, d), F32), jax.ShapeDtypeStruct((1, 1), F32),
                   jax.ShapeDtypeStruct((1, d), F32)],
        compiler_params=_params(("arbitrary",)),
    )(h2, target, gain)


def _silu_parts(g):
    sig = jax.nn.sigmoid(g)
    return g * sig, sig * (1.0 + g * (1.0 - sig))


def _lane_lo(rows):
    return lax.broadcasted_iota(jnp.int32, (rows, LANES), 1) < HEAD_DIM


def _stack_pair(x):
    lo = _lane_lo(x.shape[0])
    zero = jnp.zeros_like(x)
    return jnp.concatenate([jnp.where(lo, x, zero), jnp.where(lo, zero, x)], axis=0)


def _unstack_pair(y, w):
    return jnp.where(_lane_lo(w), y[:w], y[w:])


def _block_valid(b, left_blocks, width):
    col = lax.broadcasted_iota(jnp.int32, (1, 2 * width), 1)
    col = jnp.where(col >= width, col - width, col)
    return (col // KB + (b - left_blocks)) >= 0


def _toeplitz_tile(diag_row, width, left_chunks):
    wide = width + TQ
    rolled = pltpu.roll(jnp.broadcast_to(diag_row, (TQ, wide)), 1, 1, stride=1, stride_axis=0)
    i = lax.broadcasted_iota(jnp.int32, (TQ, width), 0) // CHUNK
    j = lax.broadcasted_iota(jnp.int32, (TQ, width), 1) // CHUNK
    dc = i + left_chunks - j
    return jnp.where((dc >= 0) & (dc <= left_chunks), rolled[:, TQ:], MASKED)


def _toeplitz_sum(tile, width):
    flip = (lax.broadcasted_iota(jnp.int32, (TQ, TQ), 0) + lax.broadcasted_iota(jnp.int32, (TQ, TQ), 1)
            == TQ - 1).astype(F32)
    reversed_rows = jnp.dot(flip, tile, precision=lax.Precision.HIGHEST, preferred_element_type=F32)
    padded = jnp.concatenate([reversed_rows, jnp.zeros((TQ, TQ), F32)], axis=1)
    rolled = pltpu.roll(padded, 0, 1, stride=1, stride_axis=0)
    return jnp.sum(rolled, axis=0, keepdims=True)


def _softmax_pair(sc, w, sink=None):
    ps, lses = [], []
    for e in range(2):
        sh = sc[:, e * w:(e + 1) * w]
        m = jnp.max(sh, axis=-1, keepdims=True)
        if sink is not None:
            m = jnp.maximum(m, sink[e])
        ex = jnp.exp(sh - m)
        l = jnp.sum(ex, axis=-1, keepdims=True)
        if sink is not None:
            l = l + jnp.exp(sink[e] - m)
        ps.append((ex * (1.0 / l)).astype(BF16))
        lses.append(m + jnp.log(l))
    return jnp.concatenate(ps, axis=-1), lses


def _softmax_pair_bwd(sc, dp, lse, delta, w):
    ps, dss = [], []
    for e in range(2):
        p = jnp.exp(sc[:, e * w:(e + 1) * w] - lse[e])
        ps.append(p)
        dss.append(p * (dp[:, e * w:(e + 1) * w] - delta[e]))
    return jnp.concatenate(ps, axis=-1), jnp.concatenate(dss, axis=-1)


def _pair_rowsums(x, lo):
    zero = jnp.zeros_like(x)
    return (jnp.sum(jnp.where(lo, x, zero), axis=-1, keepdims=True),
            jnp.sum(jnp.where(lo, zero, x), axis=-1, keepdims=True))


def _a_kv_specs(left, pw):
    specs = []
    for which in (1, 2):
        for t in range(A_KBLOCKS):
            specs.append(pl.BlockSpec(
                (None, KB, pw), functools.partial(
                    lambda p, b, which, t: (which, jnp.maximum(b - left + t, 0), p), which=which, t=t)))
    return specs


def _attn_a_fwd(zqkv, g, diag, hosted=None):
    s = g.shape[0]
    nb = s // TQ
    left = A_KBLOCKS - 1
    pw = A_PAIRS * LANES
    wide = A_WIN + TQ

    def body(q_ref, *refs):
        k_refs = refs[:A_KBLOCKS]
        v_refs = refs[A_KBLOCKS:2 * A_KBLOCKS]
        g_ref, diag_ref, o_ref, u_ref, lse_ref, bias_scr = refs[2 * A_KBLOCKS:]
        b = pl.program_id(1)

        @pl.when(b == 0)
        def _():
            for hh in range(2 * A_PAIRS):
                bias_scr[hh // 2, :, (hh % 2) * A_WIN:(hh % 2 + 1) * A_WIN] = _toeplitz_tile(
                    diag_ref[hh], A_WIN, A_LEFT_CHUNKS)

        valid = _block_valid(b, left, A_WIN)
        lo = _lane_lo(TQ)
        for pp in range(A_PAIRS):
            ln = slice(pp * LANES, (pp + 1) * LANES)
            kcat = _stack_pair(jnp.concatenate([r[:, ln] for r in k_refs], axis=0))
            vcat = _stack_pair(jnp.concatenate([r[:, ln] for r in v_refs], axis=0))
            sc = lax.dot_general(q_ref[:, ln], kcat, NT, preferred_element_type=F32) * SCALE + bias_scr[pp]
            sc = jnp.where(valid, sc, MASKED)
            p, lses = _softmax_pair(sc, A_WIN)
            ov = jnp.dot(p, vcat, preferred_element_type=F32)
            o_ref[:, ln] = ov
            lse_ref[pp] = jnp.where(lo, lses[0], lses[1])
            sg, _ = _silu_parts(g_ref[:, ln])
            u_ref[:, ln] = (ov * sg).astype(BF16)

    tile = pl.BlockSpec((TQ, pw), lambda p, b: (b, p))
    return _call(
        body, name="attn_a_fwd", grid=(HEADS // 2 // A_PAIRS, nb),
        in_specs=[pl.BlockSpec((None, TQ, pw), lambda p, b: (0, b, p))] + _a_kv_specs(left, pw) + [
            tile, pl.BlockSpec((2 * A_PAIRS, 1, wide), lambda p, b: (p, 0, 0))],
        out_specs=[tile, tile, pl.BlockSpec((A_PAIRS, TQ, LANES), lambda p, b: (p, b, 0))],
        out_shape=[jax.ShapeDtypeStruct((s, D_MODEL), F32), jax.ShapeDtypeStruct((s, D_MODEL), BF16),
                   jax.ShapeDtypeStruct((HEADS // 2, s, LANES), F32)],
        scratch_shapes=[pltpu.VMEM((A_PAIRS, TQ, 2 * A_WIN), F32)],
        sem=("parallel", "arbitrary"), hosted=hosted,
        args=(zqkv, *([zqkv] * (2 * A_KBLOCKS)), g, diag))


def _attn_a_bwd(zqkv, g, o, du, lse, diag, hosted=None):
    s = g.shape[0]
    nb = s // TQ
    left = A_KBLOCKS - 1
    pw = A_PAIRS * LANES
    wide = A_WIN + TQ

    def body(q_ref, *refs):
        k_refs = refs[:A_KBLOCKS]
        v_refs = refs[A_KBLOCKS:2 * A_KBLOCKS]
        (g_ref, o_ref, du_ref, lse_ref, diag_ref, dz_ref, ddiag_ref,
         bias_scr, dbias_acc, dk_acc, dv_acc) = refs[2 * A_KBLOCKS:]
        b = pl.program_id(1)

        @pl.when(b == 0)
        def _():
            for hh in range(2 * A_PAIRS):
                bias_scr[hh // 2, :, (hh % 2) * A_WIN:(hh % 2 + 1) * A_WIN] = _toeplitz_tile(
                    diag_ref[hh], A_WIN, A_LEFT_CHUNKS)
            dbias_acc[...] = jnp.zeros_like(dbias_acc)
            dk_acc[...] = jnp.zeros_like(dk_acc)
            dv_acc[...] = jnp.zeros_like(dv_acc)

        valid = _block_valid(b, left, A_WIN)
        lo = _lane_lo(TQ)
        rows = pl.ds(pl.multiple_of(b * TQ, TQ), TQ)
        sg, dsg = _silu_parts(g_ref[...])
        duv = du_ref[...]
        ov = o_ref[...]
        do = duv * sg
        dz_ref[3, rows, :] = (duv * ov * dsg).astype(BF16)
        do_o = do * ov
        do_bf = do.astype(BF16)
        for pp in range(A_PAIRS):
            ln = slice(pp * LANES, (pp + 1) * LANES)
            q = q_ref[:, ln]
            kcat = _stack_pair(jnp.concatenate([r[:, ln] for r in k_refs], axis=0))
            vcat = _stack_pair(jnp.concatenate([r[:, ln] for r in v_refs], axis=0))
            sc = lax.dot_general(q, kcat, NT, preferred_element_type=F32) * SCALE + bias_scr[pp]
            sc = jnp.where(valid, sc, MASKED)
            lse_t = lse_ref[pp]
            dp = lax.dot_general(do_bf[:, ln], vcat, NT, preferred_element_type=F32)
            p, ds = _softmax_pair_bwd(sc, dp, (lse_t[:, 0:1], lse_t[:, HEAD_DIM:HEAD_DIM + 1]),
                                      _pair_rowsums(do_o[:, ln], lo), A_WIN)
            dbias_acc[pp] += ds
            dsq = (ds * SCALE).astype(BF16)
            dz_ref[0, rows, ln] = jnp.dot(dsq, kcat, preferred_element_type=F32).astype(BF16)
            dk = _unstack_pair(lax.dot_general(dsq, q, TN, preferred_element_type=F32), A_WIN)
            dv = _unstack_pair(lax.dot_general(p.astype(BF16), do_bf[:, ln], TN, preferred_element_type=F32),
                               A_WIN)
            for t in range(A_KBLOCKS):
                krows = pl.ds(pl.multiple_of(jnp.maximum(b - left + t, 0) * KB, KB), KB)
                dk_acc[krows, ln] += dk[t * KB:(t + 1) * KB, :]
                dv_acc[krows, ln] += dv[t * KB:(t + 1) * KB, :]

        @pl.when(b == nb - 1)
        def _():
            dz_ref[1] = dk_acc[...].astype(BF16)
            dz_ref[2] = dv_acc[...].astype(BF16)
            for hh in range(2 * A_PAIRS):
                ddiag_ref[hh] = _toeplitz_sum(
                    dbias_acc[hh // 2, :, (hh % 2) * A_WIN:(hh % 2 + 1) * A_WIN], A_WIN)

    tile = pl.BlockSpec((TQ, pw), lambda p, b: (b, p))
    diag_spec = pl.BlockSpec((2 * A_PAIRS, 1, wide), lambda p, b: (p, 0, 0))
    return _call(
        body, name="attn_a_bwd", grid=(HEADS // 2 // A_PAIRS, nb),
        in_specs=[pl.BlockSpec((None, TQ, pw), lambda p, b: (0, b, p))] + _a_kv_specs(left, pw) + [
            tile, tile, tile, pl.BlockSpec((A_PAIRS, TQ, LANES), lambda p, b: (p, b, 0)), diag_spec],
        out_specs=[pl.BlockSpec((4, s, pw), lambda p, b: (0, 0, p)), diag_spec],
        out_shape=[jax.ShapeDtypeStruct((4, s, D_MODEL), BF16),
                   jax.ShapeDtypeStruct((HEADS, 1, wide), F32)],
        scratch_shapes=[pltpu.VMEM((A_PAIRS, TQ, 2 * A_WIN), F32), pltpu.VMEM((A_PAIRS, TQ, 2 * A_WIN), F32),
                        pltpu.VMEM((s, pw), F32), pltpu.VMEM((s, pw), F32)],
        sem=("parallel", "arbitrary"), hosted=hosted,
        args=(zqkv, *([zqkv] * (2 * A_KBLOCKS)), g, o, du, lse, diag))


B_STACK = B_GROUP // 2
B_KVX = 4 * LANES


def _b_head_place(h):
    return h // B_GROUP, (h % B_GROUP) // 2, h % 2


def _b_build_bias(diag_ref, bias_scr):
    for h in range(HEADS):
        gi, pr, e = _b_head_place(h)
        bias_scr[gi, pr * TQ:(pr + 1) * TQ, e * B_WIN:(e + 1) * B_WIN] = _toeplitz_tile(
            diag_ref[h], B_WIN, B_LEFT_CHUNKS)


def _b_stack(x, gi):
    return jnp.concatenate(
        [x[:, (B_STACK * gi + pr) * LANES:(B_STACK * gi + pr + 1) * LANES] for pr in range(B_STACK)], axis=0)


def _b_sinks(sink_ref, gi):
    return [jnp.concatenate(
        [jnp.broadcast_to(sink_ref[0:1, h:h + 1], (TQ, 1))
         for h in range(B_GROUP * gi + e, B_GROUP * (gi + 1), 2)], axis=0) for e in range(2)]


def _attn_b_fwd(qb, kvx, gate, diag, sinks):
    s = qb.shape[0]
    nb = s // TQ
    left = B_KBLOCKS - 1
    rows4 = B_STACK * TQ

    def body(q_ref, *refs):
        kv_refs = refs[:B_KBLOCKS]
        g_ref, diag_ref, sink_ref, o_ref, u_ref, lse_ref, bias_scr = refs[B_KBLOCKS:]
        b = pl.program_id(0)

        @pl.when(b == 0)
        def _():
            _b_build_bias(diag_ref, bias_scr)

        kvv = jnp.concatenate([r[...] for r in kv_refs], axis=0)
        valid = _block_valid(b, left, B_WIN)
        lo = _lane_lo(rows4)
        for gi in range(B_KV_HEADS):
            kcat = _stack_pair(kvv[:, gi * LANES:(gi + 1) * LANES])
            vcat = _stack_pair(kvv[:, (B_KV_HEADS + gi) * LANES:(B_KV_HEADS + gi + 1) * LANES])
            qs = _b_stack(q_ref, gi)
            sc = lax.dot_general(qs, kcat, NT, preferred_element_type=F32) * SCALE + bias_scr[gi]
            sc = jnp.where(valid, sc, MASKED)
            p, lses = _softmax_pair(sc, B_WIN, _b_sinks(sink_ref, gi))
            ov = jnp.dot(p, vcat, preferred_element_type=F32)
            lse_t = jnp.where(lo, lses[0], lses[1])
            for pr in range(B_STACK):
                pair = B_STACK * gi + pr
                o_ref[:, pair * LANES:(pair + 1) * LANES] = ov[pr * TQ:(pr + 1) * TQ]
                lse_ref[pair] = lse_t[pr * TQ:(pr + 1) * TQ]
        sg, _ = _silu_parts(g_ref[...])
        u_ref[...] = (o_ref[...] * sg).astype(BF16)

    kv_specs = [pl.BlockSpec((KB, B_KVX), functools.partial(
        lambda b, t: (jnp.maximum(b - left + t, 0), 0), t=t)) for t in range(B_KBLOCKS)]
    row = pl.BlockSpec((TQ, D_MODEL), lambda b: (b, 0))
    return pl.pallas_call(
        body, name="attn_b_fwd", grid=(nb,),
        in_specs=[row] + kv_specs + [row, pl.BlockSpec((HEADS, 1, B_WIN + TQ), lambda b: (0, 0, 0)),
                                     pl.BlockSpec((1, HEADS), lambda b: (0, 0))],
        out_specs=[row, row, pl.BlockSpec((HEADS // 2, TQ, LANES), lambda b: (0, b, 0))],
        out_shape=[jax.ShapeDtypeStruct((s, D_MODEL), F32), jax.ShapeDtypeStruct((s, D_MODEL), BF16),
                   jax.ShapeDtypeStruct((HEADS // 2, s, LANES), F32)],
        scratch_shapes=[pltpu.VMEM((B_KV_HEADS, rows4, 2 * B_WIN), F32)],
        compiler_params=_params(("arbitrary",)),
    )(qb, *([kvx] * B_KBLOCKS), gate, diag, sinks)


def _attn_b_bwd(qb, kvx, gate, o, du, lse, diag, sinks):
    s = qb.shape[0]
    nb = s // TQ
    left = B_KBLOCKS - 1
    rows4 = B_STACK * TQ
    half = D_MODEL // 2

    def body(q_ref, *refs):
        kv_refs = refs[:B_KBLOCKS]
        (g_ref, o_ref, du_ref, lse_ref, diag_ref, sink_ref, dz_ref, dkv_ref, ddiag_ref, dsink_ref,
         bias_scr, dbias_acc, dkv_acc, dsink_acc) = refs[B_KBLOCKS:]
        b = pl.program_id(0)

        @pl.when(b == 0)
        def _():
            _b_build_bias(diag_ref, bias_scr)
            dbias_acc[...] = jnp.zeros_like(dbias_acc)
            dkv_acc[...] = jnp.zeros_like(dkv_acc)
            dsink_acc[...] = jnp.zeros_like(dsink_acc)

        kvv = jnp.concatenate([r[...] for r in kv_refs], axis=0)
        valid = _block_valid(b, left, B_WIN)
        lo = _lane_lo(rows4)
        sg, dsg = _silu_parts(g_ref[...])
        duv = du_ref[...]
        ov = o_ref[...]
        do = duv * sg
        dgate = (duv * ov * dsg).astype(BF16)
        dz_ref[2] = dgate[:, :half]
        dz_ref[3] = dgate[:, half:]
        do_o = do * ov
        do_bf = do.astype(BF16)
        for gi in range(B_KV_HEADS):
            kcat = _stack_pair(kvv[:, gi * LANES:(gi + 1) * LANES])
            vcat = _stack_pair(kvv[:, (B_KV_HEADS + gi) * LANES:(B_KV_HEADS + gi + 1) * LANES])
            qs = _b_stack(q_ref, gi)
            dos = _b_stack(do_bf, gi)
            delta = _pair_rowsums(_b_stack(do_o, gi), lo)
            lse_t = jnp.concatenate([lse_ref[B_STACK * gi + pr] for pr in range(B_STACK)], axis=0)
            lse2 = (lse_t[:, 0:1], lse_t[:, HEAD_DIM:HEAD_DIM + 1])
            sink = _b_sinks(sink_ref, gi)
            sc = lax.dot_general(qs, kcat, NT, preferred_element_type=F32) * SCALE + bias_scr[gi]
            sc = jnp.where(valid, sc, MASKED)
            dp = lax.dot_general(dos, vcat, NT, preferred_element_type=F32)
            p, ds = _softmax_pair_bwd(sc, dp, lse2, delta, B_WIN)
            dbias_acc[gi] += ds
            dsink_acc[gi] += jnp.where(lo, -jnp.exp(sink[0] - lse2[0]) * delta[0],
                                       -jnp.exp(sink[1] - lse2[1]) * delta[1])
            dsq = (ds * SCALE).astype(BF16)
            dq = jnp.dot(dsq, kcat, preferred_element_type=F32).astype(BF16)
            for pr in range(B_STACK):
                dz_ref[gi, :, pr * LANES:(pr + 1) * LANES] = dq[pr * TQ:(pr + 1) * TQ]
            dk = _unstack_pair(lax.dot_general(dsq, qs, TN, preferred_element_type=F32), B_WIN)
            dv = _unstack_pair(lax.dot_general(p.astype(BF16), dos, TN, preferred_element_type=F32), B_WIN)
            for t in range(B_KBLOCKS):
                krows = pl.ds(pl.multiple_of(jnp.maximum(b - left + t, 0) * KB, KB), KB)
                dkv_acc[krows, gi * LANES:(gi + 1) * LANES] += dk[t * KB:(t + 1) * KB, :]
                dkv_acc[krows, (B_KV_HEADS + gi) * LANES:(B_KV_HEADS + gi + 1) * LANES] += dv[t * KB:(t + 1) * KB, :]

        @pl.when(b == nb - 1)
        def _():
            lo_s = _lane_lo(s)
            for which in range(2):
                folded = []
                for gi in range(B_KV_HEADS):
                    part = dkv_acc[:, (which * B_KV_HEADS + gi) * LANES:(which * B_KV_HEADS + gi + 1) * LANES]
                    folded.append(part + pltpu.roll(part, HEAD_DIM, 1))
                dkv_ref[:, which * LANES:(which + 1) * LANES] = jnp.where(lo_s, folded[0], folded[1]).astype(BF16)
            lane8 = lax.broadcasted_iota(jnp.int32, dsink_ref.shape, 1)
            tot = jnp.zeros(dsink_ref.shape, F32)
            for h in range(HEADS):
                gi, pr, e = _b_head_place(h)
                ddiag_ref[h] = _toeplitz_sum(
                    dbias_acc[gi, pr * TQ:(pr + 1) * TQ, e * B_WIN:(e + 1) * B_WIN], B_WIN)
                col = dsink_acc[gi, pr * TQ:(pr + 1) * TQ, e * HEAD_DIM:e * HEAD_DIM + 1]
                tot = jnp.where(lane8 == h, jnp.sum(col, axis=0, keepdims=True), tot)
            dsink_ref[...] = tot

    kv_specs = [pl.BlockSpec((KB, B_KVX), functools.partial(
        lambda b, t: (jnp.maximum(b - left + t, 0), 0), t=t)) for t in range(B_KBLOCKS)]
    row = pl.BlockSpec((TQ, D_MODEL), lambda b: (b, 0))
    diag_spec = pl.BlockSpec((HEADS, 1, B_WIN + TQ), lambda b: (0, 0, 0))
    return pl.pallas_call(
        body, name="attn_b_bwd", grid=(nb,),
        in_specs=[row] + kv_specs + [row, row, row, pl.BlockSpec((HEADS // 2, TQ, LANES), lambda b: (0, b, 0)),
                                     diag_spec, pl.BlockSpec((1, HEADS), lambda b: (0, 0))],
        out_specs=[pl.BlockSpec((4, TQ, half), lambda b: (0, b, 0)),
                   pl.BlockSpec((s, 2 * LANES), lambda b: (0, 0)), diag_spec,
                   pl.BlockSpec((8, LANES), lambda b: (0, 0))],
        out_shape=[jax.ShapeDtypeStruct((4, s, half), BF16), jax.ShapeDtypeStruct((s, 2 * LANES), BF16),
                   jax.ShapeDtypeStruct((HEADS, 1, B_WIN + TQ), F32), jax.ShapeDtypeStruct((8, LANES), F32)],
        scratch_shapes=[pltpu.VMEM((B_KV_HEADS, rows4, 2 * B_WIN), F32),
                        pltpu.VMEM((B_KV_HEADS, rows4, 2 * B_WIN), F32),
                        pltpu.VMEM((s, B_KVX), F32), pltpu.VMEM((B_KV_HEADS, rows4, LANES), F32)],
        compiler_params=_params(("arbitrary",)),
    )(qb, *([kvx] * B_KBLOCKS), gate, o, du, lse, diag, sinks)


def _t5_bucket(rel):
    nb = T5_BUCKETS // 2
    max_exact = nb // 2
    ret = jnp.where(rel > 0, nb, 0)
    n = jnp.abs(rel)
    nf = jnp.maximum(n, 1).astype(jnp.float32)
    large = max_exact + (jnp.log(nf / max_exact) / math.log(T5_MAX_DIST / max_exact)
                         * (nb - max_exact)).astype(jnp.int32)
    large = jnp.minimum(large, nb - 1)
    return ret + jnp.where(n < max_exact, n, large)


def _a_offset_onehot():
    c = np.arange(A_WIN + TQ)
    dist = A_LEFT_CHUNKS * CHUNK + TQ - 1 - c
    idx = np.clip(dist, -A_REL_CLIP, A_REL_CLIP) + A_REL_CLIP
    onehot = np.zeros((A_WIN + TQ, 2 * A_REL_CLIP + 1), np.float32)
    onehot[c, idx] = 1.0
    return jnp.asarray(onehot)


def _b_offset_onehot():
    c = jnp.arange(B_WIN + TQ, dtype=jnp.int32)
    rel = c - (TQ - 1) - B_LEFT_CHUNKS * CHUNK
    return (_t5_bucket(rel)[:, None] == jnp.arange(T5_BUCKETS)[None, :]).astype(F32)


def _diag_rows(onehot, table):
    rows = jnp.dot(onehot, table.astype(F32), precision=lax.Precision.HIGHEST)
    return rows.T.reshape(HEADS, 1, onehot.shape[0])


def _diag_rows_grad(onehot, ddiag):
    return jnp.dot(ddiag.reshape(HEADS, onehot.shape[0]), onehot, precision=lax.Precision.HIGHEST).T


def _position():
    x, y, c = lax.axis_index("x"), lax.axis_index("y"), lax.axis_index("c")
    chips = [(1 - x, y), (x, 1 - y), (1 - x, 1 - y)]
    return x, y, c, chips


ANY = pl.BlockSpec(memory_space=pl.ANY)


def _allgather_hosted(shards, split):
    n = len(shards)

    def part(ref, t, half):
        if not split[t]:
            return ref
        rows = shards[t].shape[0] // 2
        return ref.at[pl.ds(half * rows, rows)]

    def copies(kind, ins, outs, sems):
        send_sems, recv_sems, pass_send, pass_recv, local_sems = sems
        x, y, c, chips = _position()
        mine = 2 * x + y
        if kind == "local":
            return [pltpu.make_async_copy(ins[t], outs[t].at[mine], local_sems.at[t]) for t in range(n)]
        made = []
        for t in range(n):
            for j, chip in enumerate(chips):
                theirs = 2 * chip[0] + chip[1]
                far = dict(send_sem=send_sems.at[3 * t + j], recv_sem=recv_sems.at[3 * t + j],
                           device_id=(chip[0], chip[1], c), device_id_type=MESH)
                near = dict(send_sem=pass_send.at[3 * t + j], recv_sem=pass_recv.at[3 * t + j],
                            device_id=(x, y, 1 - c), device_id_type=MESH)
                here = part(outs[t].at[theirs], t, c)
                if kind == "send":
                    made.append(pltpu.make_async_remote_copy(
                        src_ref=part(ins[t], t, c), dst_ref=part(outs[t].at[mine], t, c), **far))
                elif kind == "landed":
                    made.append(pltpu.make_async_remote_copy(src_ref=here, dst_ref=here, **far))
                elif not split[t]:
                    made.append(None)
                elif kind == "pass":
                    made.append(pltpu.make_async_remote_copy(src_ref=here, dst_ref=here, **near))
                else:
                    other = part(outs[t].at[theirs], t, 1 - c)
                    made.append(pltpu.make_async_remote_copy(src_ref=other, dst_ref=other, **near))
        return made

    def first(ins, outs, sems):
        for cp in copies("local", ins, outs, sems) + copies("send", ins, outs, sems):
            cp.start()

    def middle(ins, outs, sems):
        for got, cp in zip(copies("landed", ins, outs, sems), copies("pass", ins, outs, sems)):
            got.wait_recv()
            if cp is not None:
                cp.start()

    def last(ins, outs, sems):
        for cp in copies("passed", ins, outs, sems):
            if cp is not None:
                cp.wait_recv()
        for cp in copies("send", ins, outs, sems) + copies("pass", ins, outs, sems):
            if cp is not None:
                cp.wait_send()
        for cp in copies("local", ins, outs, sems):
            cp.wait()

    return _Hosted(shards, [jax.ShapeDtypeStruct((4,) + w.shape, w.dtype) for w in shards],
                   [pltpu.SemaphoreType.DMA((3 * n,))] * 4 + [pltpu.SemaphoreType.DMA((n,))],
                   first, middle, last)


def _scatter_hosted(grads):
    n = len(grads)

    def copies(ins, outs, sems):
        send_sems, recv_sems = sems
        x, y, c, chips = _position()
        return [pltpu.make_async_remote_copy(
            src_ref=ins[t].at[2 * chip[0] + chip[1]], dst_ref=outs[t].at[j],
            send_sem=send_sems.at[3 * t + j], recv_sem=recv_sems.at[3 * t + j],
            device_id=(chip[0], chip[1], c), device_id_type=MESH)
            for t in range(n) for j, chip in enumerate(chips)]

    def first(ins, outs, sems):
        for cp in copies(ins, outs, sems):
            cp.start()

    def last(ins, outs, sems):
        for cp in copies(ins, outs, sems):
            cp.wait()

    return _Hosted(grads, [jax.ShapeDtypeStruct((3,) + g.shape[1:], g.dtype) for g in grads],
                   [pltpu.SemaphoreType.DMA((3 * n,))] * 2, first, None, last)


def _run_alone(name, hosted):
    n_in = len(hosted.inputs)
    n_out = len(hosted.out_shapes)

    def body(*refs):
        ins, outs, sems = refs[:n_in], refs[n_in:n_in + n_out], refs[n_in + n_out:]
        hosted.first(ins, outs, sems)
        if hosted.middle is not None:
            hosted.middle(ins, outs, sems)
        hosted.last(ins, outs, sems)

    return pl.pallas_call(
        body, name=name, in_specs=[ANY] * n_in, out_specs=[ANY] * n_out, out_shape=hosted.out_shapes,
        scratch_shapes=hosted.sems)(*hosted.inputs)


def _swap_with_sibling(blocks):
    n = len(blocks)

    def body(*refs):
        ins, outs = refs[:n], refs[n:2 * n]
        send_sems, recv_sems = refs[2 * n:]
        x, y, c, _ = _position()
        sends = [pltpu.make_async_remote_copy(
            src_ref=ins[t], dst_ref=outs[t], send_sem=send_sems.at[t], recv_sem=recv_sems.at[t],
            device_id=(x, y, 1 - c), device_id_type=MESH) for t in range(n)]
        for cp in sends:
            cp.start()
        for cp in sends:
            cp.wait()

    return pl.pallas_call(
        body, name="swap_with_sibling",
        in_specs=[ANY] * n, out_specs=[ANY] * n,
        out_shape=[jax.ShapeDtypeStruct(b.shape, b.dtype) for b in blocks],
        scratch_shapes=[pltpu.SemaphoreType.DMA((n,))] * 2,
    )(*blocks)


def _allreduce_small(block):
    rows = block.shape[0]

    def body(in_ref, sum_ref, all_ref, send_sems, recv_sems):
        x, y, c, _ = _position()
        me = 4 * x + 2 * y + c
        all_ref[me] = in_ref[...]
        sends = []
        for k in range(1, 8):
            peer = (x ^ (k >> 2), y ^ ((k >> 1) & 1), c ^ (k & 1))
            sends.append(pltpu.make_async_remote_copy(
                src_ref=in_ref, dst_ref=all_ref.at[me], send_sem=send_sems.at[k - 1],
                recv_sem=recv_sems.at[k - 1], device_id=peer, device_id_type=MESH))
        for cp in sends:
            cp.start()
        for k in range(1, 8):
            theirs = me ^ k
            pltpu.make_async_remote_copy(
                src_ref=in_ref, dst_ref=all_ref.at[theirs], send_sem=send_sems.at[k - 1],
                recv_sem=recv_sems.at[k - 1], device_id=(x, y, c), device_id_type=MESH).wait_recv()
        for cp in sends:
            cp.wait_send()
        acc = all_ref[0]
        for d in range(1, 8):
            acc = acc + all_ref[d]
        sum_ref[...] = acc

    vmem = pl.BlockSpec(memory_space=pltpu.VMEM)
    return pl.pallas_call(
        body, name="allreduce_small",
        in_specs=[vmem], out_specs=[vmem, vmem],
        out_shape=[jax.ShapeDtypeStruct((rows, LANES), F32), jax.ShapeDtypeStruct((8, rows, LANES), F32)],
        scratch_shapes=[pltpu.SemaphoreType.DMA((7,))] * 2,
    )(block)[0]


def _adamw_math(w, g, m, v):
    m = ADAM_B1 * m + (1.0 - ADAM_B1) * g
    v = ADAM_B2 * v + (1.0 - ADAM_B2) * (g * g)
    m_hat = m / (1.0 - ADAM_B1 ** ADAM_STEP)
    v_hat = v / (1.0 - ADAM_B2 ** ADAM_STEP)
    delta = -ADAM_LR * (m_hat / (jnp.sqrt(v_hat) + ADAM_EPS) + ADAM_WD * w)
    return delta, m, v


def _row_tile(rows):
    return min(rows, 256)


def _sum_partials(name, own, recv):
    rows, cols = own.shape
    tr = _row_tile(rows)

    def body(own_ref, recv_ref, o_ref):
        acc = own_ref[...]
        for j in range(3):
            acc = acc + recv_ref[j].astype(F32)
        o_ref[...] = acc

    return pl.pallas_call(
        body, name=name, grid=(rows // tr,),
        in_specs=[pl.BlockSpec((tr, cols), lambda i: (i, 0)), pl.BlockSpec((3, tr, cols), lambda i: (0, i, 0))],
        out_specs=pl.BlockSpec((tr, cols), lambda i: (i, 0)),
        out_shape=jax.ShapeDtypeStruct((rows, cols), F32),
        compiler_params=_params(("parallel",)),
    )(own, recv)


def _adamw(name, w, m, v, g_parts):
    rows, cols = w.shape
    tr = _row_tile(rows)
    n = len(g_parts)

    def body(w_ref, m_ref, v_ref, *refs):
        g_refs = refs[:n]
        go_ref, d_ref, mo_ref, vo_ref = refs[n:]
        g = g_refs[0][...]
        for r in g_refs[1:]:
            g = g + r[...]
        delta, mn, vn = _adamw_math(w_ref[...], g, m_ref[...], v_ref[...])
        go_ref[...] = g
        d_ref[...] = delta
        mo_ref[...] = mn
        vo_ref[...] = vn

    spec = pl.BlockSpec((tr, cols), lambda i: (i, 0))
    return pl.pallas_call(
        body, name=name, grid=(rows // tr,),
        in_specs=[spec] * (3 + n), out_specs=[spec] * 4,
        out_shape=[jax.ShapeDtypeStruct((rows, cols), F32)] * 4,
        compiler_params=_params(("parallel",)),
    )(w, m, v, *g_parts)


def _local_step(x, target, ga, wa_in, rel_bias, later_shards, gk, t5, gb, sinks, gf):
    s, d = x.shape
    tm = min(TM_DENSE, s)
    nt = s // tm
    half = d // 2
    row = pl.BlockSpec((tm, d), lambda i: (i, 0))
    whole = lambda shape: pl.BlockSpec(shape, lambda *_: (0,) * len(shape))

    n1, = _norm_fwd("norm_a", x, ga)
    zqkv = _matmul("proj_a_qkv", n1, wa_in, dims=NN, grid=(3, nt),
                   a_spec=pl.BlockSpec((tm, d), lambda j, i: (i, 0)),
                   b_spec=pl.BlockSpec((None, d, d), lambda j, i: (j, 0, 0)),
                   o_spec=pl.BlockSpec((None, tm, d), lambda j, i: (j, i, 0)),
                   out_shape=(3, s, d), out_dtype=BF16)
    gate_a = _matmul("proj_a_gate", n1, wa_in, dims=NN, grid=(nt,),
                     a_spec=row, b_spec=pl.BlockSpec((None, d, d), lambda i: (3, 0, 0)), o_spec=row,
                     out_shape=(s, d), out_dtype=F32)
    onehot_a = _a_offset_onehot()
    diag_a = _diag_rows(onehot_a, rel_bias)
    (o_a, u_a, lse_a), gathered = _attn_a_fwd(
        zqkv, gate_a, diag_a, hosted=_allgather_hosted(later_shards, [True] * len(later_shards)))
    wa_out, wkv, wb_in, wb_out = gathered
    wa_out = wa_out.reshape(d, d)
    wkv = wkv.reshape(d, -1)
    wb_out = wb_out.reshape(d, d)
    h1 = _matmul("out_a", u_a, wa_out, dims=NN, grid=(nt,), a_spec=row, b_spec=whole((d, d)), o_spec=row,
                 out_shape=(s, d), out_dtype=F32, resid=x, resid_spec=row)

    nk, n2 = _norm_fwd("norm_kv_b", h1, jnp.concatenate([gk, gb], axis=0))
    kvw = wkv.shape[1]
    wkv_x = jnp.concatenate([wkv[:, (i // 2) * HEAD_DIM:(i // 2 + 1) * HEAD_DIM] for i in range(8)], axis=1)
    kvx = _matmul("proj_kv", nk, wkv_x, dims=NN, grid=(nt,), a_spec=row, b_spec=whole((d, B_KVX)),
                  o_spec=pl.BlockSpec((tm, B_KVX), lambda i: (i, 0)), out_shape=(s, B_KVX), out_dtype=BF16)
    qb = _matmul("proj_b_q", n2, wb_in, dims=NN, grid=(2, nt),
                 a_spec=pl.BlockSpec((tm, d), lambda j, i: (i, 0)),
                 b_spec=pl.BlockSpec((None, d, half), lambda j, i: (j, 0, 0)),
                 o_spec=pl.BlockSpec((tm, half), lambda j, i: (i, j)), out_shape=(s, d), out_dtype=BF16)
    gate_b = _matmul("proj_b_gate", n2, wb_in, dims=NN, grid=(2, nt),
                     a_spec=pl.BlockSpec((tm, d), lambda j, i: (i, 0)),
                     b_spec=pl.BlockSpec((None, d, half), lambda j, i: (2 + j, 0, 0)),
                     o_spec=pl.BlockSpec((tm, half), lambda j, i: (i, j)), out_shape=(s, d), out_dtype=F32)
    onehot_b = _b_offset_onehot()
    diag_b = _diag_rows(onehot_b, t5)
    o_b, u_b, lse_b = _attn_b_fwd(qb, kvx, gate_b, diag_b, sinks)
    h2 = _matmul("out_b", u_b, wb_out, dims=NN, grid=(nt,), a_spec=row, b_spec=whole((d, d)), o_spec=row,
                 out_shape=(s, d), out_dtype=F32, resid=h1, resid_spec=row)

    dh2, loss, d_gf = _loss_head(h2, target, gf)

    du_b = _matmul("dout_b", dh2, wb_out, dims=NT, grid=(nt,), a_spec=row, b_spec=whole((d, d)), o_spec=row,
                   out_shape=(s, d), out_dtype=F32)
    d_wb_out = _matmul("dw_out_b", u_b, dh2, dims=TN, grid=(2,),
                       a_spec=whole((s, d)), b_spec=pl.BlockSpec((s, half), lambda j: (0, j)),
                       o_spec=pl.BlockSpec((d, half), lambda j: (0, j)),
                       out_shape=(d, d), out_dtype=F32, also_bf16=True)
    dz_b, dkv, ddiag_b, dsinks = _attn_b_bwd(qb, kvx, gate_b, o_b, du_b, lse_b, diag_b, sinks)
    dn2 = _matmul("dproj_b", dz_b, wb_in, dims=NT, grid=(nt,), parts=4,
                  a_spec=pl.BlockSpec((4, tm, half), lambda i: (0, i, 0)), b_spec=whole((4, d, half)),
                  o_spec=row, out_shape=(s, d), out_dtype=F32)
    d_wb_in = _matmul("dw_in_b", n2, dz_b, dims=TN, grid=(4,),
                      a_spec=whole((s, d)), b_spec=pl.BlockSpec((None, s, half), lambda j: (j, 0, 0)),
                      o_spec=pl.BlockSpec((None, d, half), lambda j: (j, 0, 0)),
                      out_shape=(4, d, half), out_dtype=F32, also_bf16=True)
    dnk = _matmul("dproj_kv", dkv, wkv, dims=NT, grid=(nt,),
                  a_spec=pl.BlockSpec((tm, kvw), lambda i: (i, 0)), b_spec=whole((d, kvw)), o_spec=row,
                  out_shape=(s, d), out_dtype=F32)
    d_wkv = _matmul("dw_kv", nk, dkv, dims=TN, grid=(1,),
                    a_spec=whole((s, d)), b_spec=whole((s, kvw)), o_spec=whole((d, kvw)),
                    out_shape=(d, kvw), out_dtype=F32, also_bf16=True)
    dh1, d_gkb = _norm_bwd("dnorm_kv_b", h1, dh2, [dnk, dn2], jnp.concatenate([gk, gb], axis=0))

    du_a = _matmul("dout_a", dh1, wa_out, dims=NT, grid=(nt,), a_spec=row, b_spec=whole((d, d)), o_spec=row,
                   out_shape=(s, d), out_dtype=F32)
    d_wa_out = _matmul("dw_out_a", u_a, dh1, dims=TN, grid=(2,),
                       a_spec=whole((s, d)), b_spec=pl.BlockSpec((s, half), lambda j: (0, j)),
                       o_spec=pl.BlockSpec((d, half), lambda j: (0, j)),
                       out_shape=(d, d), out_dtype=F32, also_bf16=True)
    early = dict(a_w_out=[g.reshape(4, d // 4, d) for g in d_wa_out],
                 kv_w=[g.reshape(4, d // 4, kvw) for g in d_wkv], b_w_in=list(d_wb_in),
                 b_w_out=[g.reshape(4, d // 4, d) for g in d_wb_out])
    (dz_a, ddiag_a), early_recv = _attn_a_bwd(
        zqkv, gate_a, o_a, du_a, lse_a, diag_a, hosted=_scatter_hosted([early[n][1] for n in early]))
    d_wa_in = _matmul("dw_in_a", n1, dz_a, dims=TN, grid=(4, 2),
                      a_spec=whole((s, d)), b_spec=pl.BlockSpec((None, s, half), lambda j, h: (j, 0, h)),
                      o_spec=pl.BlockSpec((None, d, half), lambda j, h: (j, 0, h)),
                      out_shape=(4, d, d), out_dtype=F32, also_bf16=True)
    tp = min(TM_PARTS, s)
    dn1, late_recv = _matmul("dproj_a", dz_a, wa_in, dims=NT, grid=(s // tp,), parts=4,
                             a_spec=pl.BlockSpec((4, tp, d), lambda i: (0, i, 0)), b_spec=whole((4, d, d)),
                             o_spec=pl.BlockSpec((tp, d), lambda i: (i, 0)),
                             out_shape=(s, d), out_dtype=F32, hosted=_scatter_hosted([d_wa_in[1]]))
    grad_x, d_ga = _norm_bwd("dnorm_a", x, dh1, [dn1], ga)

    small = dict(
        a_norm=d_ga, a_rel_bias=_diag_rows_grad(onehot_a, ddiag_a), kv_norm=d_gkb[0:1],
        t5_bias=_diag_rows_grad(onehot_b, ddiag_b), b_norm=d_gkb[1:2], b_sinks=dsinks[0:1, :HEADS],
        final_norm=d_gf)
    own = dict(a_w_in=d_wa_in[0], **{n: early[n][0] for n in early})
    received = dict(a_w_in=late_recv[0], **dict(zip(early, early_recv)))
    return loss, grad_x, small, own, received


SMALL = ("a_norm", "a_rel_bias", "kv_norm", "t5_bias", "b_norm", "b_sinks", "final_norm")
BIG = ("a_w_in", "a_w_out", "kv_w", "b_w_in", "b_w_out")
ORDER = ("a_norm", "a_w_in", "a_rel_bias", "a_w_out", "kv_norm", "kv_w", "t5_bias", "b_norm", "b_w_in",
         "b_sinks", "b_w_out", "final_norm")


def _pack(parts, rows):
    flat = jnp.concatenate([p.reshape(-1).astype(F32) for p in parts])
    return jnp.pad(flat, (0, rows * LANES - flat.shape[0])).reshape(rows, LANES)


def _unpack(block, shapes):
    flat = block.reshape(-1)
    out, at = [], 0
    for shp in shapes:
        size = int(np.prod(shp))
        out.append(flat[at:at + size].reshape(shp))
        at += size
    return out


def kernel(x, a_norm, a_w_in, a_rel_bias, a_w_out, kv_norm, kv_w, t5_bias, b_norm, b_w_in, b_sinks, b_w_out, final_norm, loss_target, m_a_norm, m_a_w_in, m_a_rel_bias, m_a_w_out, m_kv_norm, m_kv_w, m_t5_bias, m_b_norm, m_b_w_in, m_b_sinks, m_b_w_out, m_final_norm, v_a_norm, v_a_w_in, v_a_rel_bias, v_a_w_out, v_kv_norm, v_kv_w, v_t5_bias, v_b_norm, v_b_w_in, v_b_sinks, v_b_w_out, v_final_norm):
    w = dict(a_norm=a_norm, a_w_in=a_w_in, a_rel_bias=a_rel_bias, a_w_out=a_w_out, kv_norm=kv_norm, kv_w=kv_w,
             t5_bias=t5_bias, b_norm=b_norm, b_w_in=b_w_in, b_sinks=b_sinks, b_w_out=b_w_out,
             final_norm=final_norm)
    m = dict(a_norm=m_a_norm, a_w_in=m_a_w_in, a_rel_bias=m_a_rel_bias, a_w_out=m_a_w_out, kv_norm=m_kv_norm,
             kv_w=m_kv_w, t5_bias=m_t5_bias, b_norm=m_b_norm, b_w_in=m_b_w_in, b_sinks=m_b_sinks,
             b_w_out=m_b_w_out, final_norm=m_final_norm)
    v = dict(a_norm=v_a_norm, a_w_in=v_a_w_in, a_rel_bias=v_a_rel_bias, a_w_out=v_a_w_out, kv_norm=v_kv_norm,
             kv_w=v_kv_w, t5_bias=v_t5_bias, b_norm=v_b_norm, b_w_in=v_b_w_in, b_sinks=v_b_sinks,
             b_w_out=v_b_w_out, final_norm=v_final_norm)
    d = D_MODEL
    chip = 2 * lax.axis_index("x") + lax.axis_index("y")

    shard2d = dict(a_w_in=a_w_in[0], a_w_out=a_w_out[0], kv_w=kv_w, b_w_in=b_w_in[0], b_w_out=b_w_out[0])

    wa_in, ga = _run_alone("allgather_first",
                           _allgather_hosted([shard2d["a_w_in"].astype(BF16), a_norm], [True, False]))
    ga = ga.reshape(1, d)

    loss, grad_x, small, own, received = _local_step(
        x[0], loss_target[0], ga, wa_in, a_rel_bias[0], [shard2d[n].astype(BF16) for n in BIG[1:]],
        kv_norm.reshape(1, d), t5_bias, b_norm, b_sinks, final_norm.reshape(1, d))

    small_shapes = [small[n].shape for n in SMALL] + [(1, 1)]
    total = sum(int(np.prod(s)) for s in small_shapes)
    rows = -(-total // (8 * LANES)) * 8
    reduced = _unpack(_allreduce_small(_pack([small[n] for n in SMALL] + [loss], rows)), small_shapes)
    g_small = dict(zip(SMALL, reduced[:-1]))
    loss_out = reduced[-1].reshape(())
    g_small["a_norm"] = lax.dynamic_slice_in_dim(g_small["a_norm"], chip * (d // 4), d // 4, axis=1)

    core_sums = [
        _sum_partials("sum_" + n, lax.dynamic_index_in_dim(own[n], chip, 0, keepdims=False), received[n])
        for n in BIG]
    sibling_sums = _swap_with_sibling(core_sums)

    out = {}
    for n, mine, theirs in zip(BIG, core_sums, sibling_sums):
        res = _adamw("adamw_" + n, shard2d[n], m[n].reshape(shard2d[n].shape), v[n].reshape(shard2d[n].shape),
                     [mine, theirs])
        out[n] = [r.reshape(w[n].shape) for r in res]
    small_w_shapes = [w[n].shape for n in SMALL]
    total_w = sum(int(np.prod(s)) for s in small_w_shapes)
    rows_w = -(-total_w // (8 * LANES)) * 8
    packed = [_pack([t[n] for n in SMALL], rows_w) for t in (w, m, v)]
    g_packed = _pack([g_small[n] for n in SMALL], rows_w)
    res = _adamw("adamw_small", packed[0], packed[1], packed[2], [g_packed])
    unpacked = [_unpack(r, small_w_shapes) for r in res]
    for i, n in enumerate(SMALL):
        out[n] = [unpacked[k][i] for k in range(4)]

    grads = [out[n][0] for n in ORDER]
    deltas = [out[n][1] for n in ORDER]
    new_m = [out[n][2] for n in ORDER]
    new_v = [out[n][3] for n in ORDER]
    return (loss_out, grad_x[None], *grads, *deltas, *new_m, *new_v)
```

```python
import functools
import math

import jax
import jax.numpy as jnp
import numpy as np
from jax import lax
from jax.experimental import pallas as pl
from jax.experimental.pallas import tpu as pltpu

F32 = jnp.float32
BF16 = jnp.bfloat16
MESH = pl.DeviceIdType.MESH

D_MODEL = 1024
HEADS = 16
HEAD_DIM = 64
CHUNK = 64
RMS_EPS = 1e-6
SCALE = HEAD_DIM ** -0.5
A_LEFT_CHUNKS = 8
A_REL_CLIP = 256
B_LEFT_CHUNKS = 2
B_KV_HEADS = 2
B_GROUP = HEADS // B_KV_HEADS
T5_BUCKETS = 32
T5_MAX_DIST = 128
ADAM_LR = 0.001
ADAM_B1 = 0.9
ADAM_B2 = 0.999
ADAM_EPS = 1e-08
ADAM_WD = 0.01
ADAM_STEP = 10

MASKED = -1e30
LANES = 128
TQ = 128
A_PAIRS = 2
A_PAIRS_FWD = 4
KB = 128
A_KBLOCKS = A_LEFT_CHUNKS * CHUNK // KB + 1
B_KBLOCKS = B_LEFT_CHUNKS * CHUNK // KB + 1
A_WIN = A_KBLOCKS * KB
B_WIN = B_KBLOCKS * KB
TM = 512
TM_DENSE = 1024
TM_PARTS = 512
VMEM_LIMIT = 56 * 1024 * 1024

NT = (((1,), (1,)), ((), ()))
TN = (((0,), (0,)), ((), ()))
NN = (((1,), (0,)), ((), ()))


def _params(sem=None):
    return pltpu.CompilerParams(dimension_semantics=sem, vmem_limit_bytes=VMEM_LIMIT)


class _Hosted:
    def __init__(self, inputs, out_shapes, sems, first, middle, last):
        self.inputs, self.out_shapes, self.sems = list(inputs), list(out_shapes), list(sems)
        self.first, self.middle, self.last = first, middle, last


def _call(body, *, name, grid, in_specs, out_specs, out_shape, args, scratch_shapes=(), sem=None, hosted=None):
    in_specs, out_specs, out_shape = list(in_specs), list(out_specs), list(out_shape)
    scratch_shapes = list(scratch_shapes)
    if hosted is None:
        out = pl.pallas_call(
            body, name=name, grid=grid, in_specs=in_specs, out_specs=out_specs, out_shape=out_shape,
            scratch_shapes=scratch_shapes, compiler_params=_params(sem))(*args)
        return list(out), []
    n_in, n_out, n_scr = len(in_specs), len(out_shape), len(scratch_shapes)
    h_in, h_out = len(hosted.inputs), len(hosted.out_shapes)
    total = int(np.prod(grid)) if grid else 1

    def wrapped(*refs):
        ins, refs = refs[:n_in], refs[n_in:]
        h_ins, refs = refs[:h_in], refs[h_in:]
        outs, refs = refs[:n_out], refs[n_out:]
        h_outs, refs = refs[:h_out], refs[h_out:]
        scr, h_sems = refs[:n_scr], refs[n_scr:]
        step = 0
        for axis, size in enumerate(grid):
            step = step * size + pl.program_id(axis)

        @pl.when(step == 0)
        def _():
            hosted.first(h_ins, h_outs, h_sems)

        body(*ins, *outs, *scr)
        if hosted.middle is not None:
            @pl.when(step == total // 2)
            def _():
                hosted.middle(h_ins, h_outs, h_sems)

        @pl.when(step == total - 1)
        def _():
            hosted.last(h_ins, h_outs, h_sems)

    out = pl.pallas_call(
        wrapped, name=name, grid=grid, in_specs=in_specs + [ANY] * h_in, out_specs=out_specs + [ANY] * h_out,
        out_shape=out_shape + hosted.out_shapes, scratch_shapes=scratch_shapes + hosted.sems,
        compiler_params=_params(("arbitrary",) * len(grid)))(*args, *hosted.inputs)
    return list(out[:n_out]), list(out[n_out:])


def _matmul(name, a, b, *, dims, grid, a_spec, b_spec, o_spec, out_shape, out_dtype,
            parts=1, resid=None, resid_spec=None, also_bf16=False, hosted=None):
    def body(*refs):
        a_ref, b_ref = refs[:2]
        r_ref = refs[2] if resid is not None else None
        o_ref = refs[3] if resid is not None else refs[2]
        if parts == 1:
            prod = lax.dot_general(a_ref[...].astype(BF16), b_ref[...].astype(BF16), dims,
                                   preferred_element_type=F32)
        else:
            prod = None
            for part in range(parts):
                term = lax.dot_general(a_ref[part].astype(BF16), b_ref[part].astype(BF16), dims,
                                       preferred_element_type=F32)
                prod = term if prod is None else prod + term
        if resid is not None:
            prod = r_ref[...] + prod
        o_ref[...] = prod.astype(out_dtype)
        if also_bf16:
            refs[-1][...] = prod.astype(BF16)

    in_specs = [a_spec, b_spec]
    args = [a, b]
    if resid is not None:
        in_specs.append(resid_spec)
        args.append(resid)
    sem = ["parallel"] * len(grid)
    out_specs = [o_spec]
    out_shapes = [jax.ShapeDtypeStruct(out_shape, out_dtype)]
    if also_bf16:
        out_specs.append(o_spec)
        out_shapes.append(jax.ShapeDtypeStruct(out_shape, BF16))
    out, extra = _call(body, name=name, grid=grid, in_specs=in_specs, out_specs=out_specs, out_shape=out_shapes,
                       args=args, sem=tuple(sem), hosted=hosted)
    res = out[0] if not also_bf16 else tuple(out)
    return res if hosted is None else (res, extra)


def _rms_rows(x):
    return lax.rsqrt(jnp.mean(x * x, axis=-1, keepdims=True) + RMS_EPS)


def _norm_fwd(name, x, gains):
    s, d = x.shape
    n = gains.shape[0]

    def body(x_ref, g_ref, *o_refs):
        xv = x_ref[...]
        xh = xv * _rms_rows(xv)
        for i in range(n):
            o_refs[i][...] = (xh * g_ref[i:i + 1, :]).astype(BF16)

    row = pl.BlockSpec((TM, d), lambda i: (i, 0))
    return pl.pallas_call(
        body, name=name, grid=(s // TM,),
        in_specs=[row, pl.BlockSpec((n, d), lambda i: (0, 0))],
        out_specs=[row] * n,
        out_shape=[jax.ShapeDtypeStruct((s, d), BF16)] * n,
        compiler_params=_params(("parallel",)),
    )(x, gains)


def _norm_bwd(name, x, dres, dns, gains):
    s, d = x.shape
    n = len(dns)

    def body(x_ref, r_ref, g_ref, *refs):
        dn_refs, dx_ref, dg_ref = refs[:n], refs[n], refs[n + 1]
        i = pl.program_id(0)
        xv = x_ref[...]
        r = _rms_rows(xv)
        xh = xv * r

        @pl.when(i == 0)
        def _():
            dg_ref[...] = jnp.zeros_like(dg_ref)

        a = None
        for j in range(n):
            dn = dn_refs[j][...]
            t = dn * g_ref[j:j + 1, :]
            a = t if a is None else a + t
            dg_ref[j:j + 1, :] += jnp.sum(dn * xh, axis=0, keepdims=True)
        dx_ref[...] = r_ref[...] + r * (a - xh * jnp.mean(xh * a, axis=-1, keepdims=True))

    row = pl.BlockSpec((TM, d), lambda i: (i, 0))
    small = pl.BlockSpec((n, d), lambda i: (0, 0))
    return pl.pallas_call(
        body, name=name, grid=(s // TM,),
        in_specs=[row, row, small] + [row] * n,
        out_specs=[row, small],
        out_shape=[jax.ShapeDtypeStruct((s, d), F32), jax.ShapeDtypeStruct((n, d), F32)],
        compiler_params=_params(("arbitrary",)),
    )(x, dres, gains, *dns)


def _loss_head(h2, target, gain):
    s, d = h2.shape

    def body(h_ref, t_ref, g_ref, dh_ref, loss_ref, dg_ref):
        i = pl.program_id(0)
        hv = h_ref[...]
        r = _rms_rows(hv)
        hh = hv * r
        g = g_ref[...]
        err = hh * g - t_ref[...]
        part = 0.5 * jnp.sum(jnp.sum(err * err, axis=-1, keepdims=True) * (1.0 / d), axis=0, keepdims=True)
        dy = err * (1.0 / d)
        a = dy * g
        dh_ref[...] = r * (a - hh * jnp.mean(hh * a, axis=-1, keepdims=True))
        dg = jnp.sum(dy * hh, axis=0, keepdims=True)

        @pl.when(i == 0)
        def _():
            loss_ref[...] = part
            dg_ref[...] = dg

        @pl.when(i > 0)
        def _():
            loss_ref[...] += part
            dg_ref[...] += dg

    row = pl.BlockSpec((TM, d), lambda i: (i, 0))
    return pl.pallas_call(
        body, name="loss_head", grid=(s // TM,),
        in_specs=[row, row, pl.BlockSpec((1, d), lambda i: (0, 0))],
        out_specs=[row, pl.BlockSpec((1, 1), lambda i: (0, 0)), pl.BlockSpec((1, d), lambda i: (0, 0))],
        out_shape=[jax.ShapeDtypeStruct((s, d), F32), jax.ShapeDtypeStruct((1, 1), F32),
                   jax.ShapeDtypeStruct((1, d), F32)],
        compiler_params=_params(("arbitrary",)),
    )(h2, target, gain)


def _silu_parts(g):
    sig = jax.nn.sigmoid(g)
    return g * sig, sig * (1.0 + g * (1.0 - sig))


def _lane_lo(rows):
    return lax.broadcasted_iota(jnp.int32, (rows, LANES), 1) < HEAD_DIM


def _stack_pair(x):
    lo = _lane_lo(x.shape[0])
    zero = jnp.zeros_like(x)
    return jnp.concatenate([jnp.where(lo, x, zero), jnp.where(lo, zero, x)], axis=0)


def _unstack_pair(y, w):
    return jnp.where(_lane_lo(w), y[:w], y[w:])


def _block_valid(b, left_blocks, width):
    col = lax.broadcasted_iota(jnp.int32, (1, 2 * width), 1)
    col = jnp.where(col >= width, col - width, col)
    return (col // KB + (b - left_blocks)) >= 0


def _toeplitz_tile(diag_row, width, left_chunks):
    wide = width + TQ
    rolled = pltpu.roll(jnp.broadcast_to(diag_row, (TQ, wide)), 1, 1, stride=1, stride_axis=0)
    i = lax.broadcasted_iota(jnp.int32, (TQ, width), 0) // CHUNK
    j = lax.broadcasted_iota(jnp.int32, (TQ, width), 1) // CHUNK
    dc = i + left_chunks - j
    return jnp.where((dc >= 0) & (dc <= left_chunks), rolled[:, TQ:], MASKED)


def _toeplitz_sum(tile, width):
    flip = (lax.broadcasted_iota(jnp.int32, (TQ, TQ), 0) + lax.broadcasted_iota(jnp.int32, (TQ, TQ), 1)
            == TQ - 1).astype(F32)
    reversed_rows = jnp.dot(flip, tile, precision=lax.Precision.HIGHEST, preferred_element_type=F32)
    padded = jnp.concatenate([reversed_rows, jnp.zeros((TQ, TQ), F32)], axis=1)
    rolled = pltpu.roll(padded, 0, 1, stride=1, stride_axis=0)
    return jnp.sum(rolled, axis=0, keepdims=True)


def _softmax_pair(sc, w, sink=None):
    ps, inv, lses = [], [], []
    for e in range(2):
        sh = sc[:, e * w:(e + 1) * w]
        m = jnp.max(sh, axis=-1, keepdims=True)
        if sink is not None:
            m = jnp.maximum(m, sink[e])
        ex = jnp.exp(sh - m)
        l = jnp.sum(ex, axis=-1, keepdims=True)
        if sink is not None:
            l = l + jnp.exp(sink[e] - m)
        ps.append(ex.astype(BF16))
        inv.append(1.0 / l)
        lses.append(m + jnp.log(l))
    return jnp.concatenate(ps, axis=-1), inv, lses


def _softmax_pair_bwd(sc, dp, lse, delta, w):
    ps, dss = [], []
    for e in range(2):
        p = jnp.exp(sc[:, e * w:(e + 1) * w] - lse[e])
        ps.append(p)
        dss.append(p * (dp[:, e * w:(e + 1) * w] - delta[e]))
    return jnp.concatenate(ps, axis=-1), jnp.concatenate(dss, axis=-1)


def _pair_rowsums(x, lo):
    zero = jnp.zeros_like(x)
    return (jnp.sum(jnp.where(lo, x, zero), axis=-1, keepdims=True),
            jnp.sum(jnp.where(lo, zero, x), axis=-1, keepdims=True))


def _a_kv_specs(left, pw):
    specs = []
    for which in (1, 2):
        for t in range(A_KBLOCKS):
            specs.append(pl.BlockSpec(
                (None, KB, pw), functools.partial(
                    lambda p, b, which, t: (which, jnp.maximum(b - left + t, 0), p), which=which, t=t)))
    return specs


def _attn_a_fwd(zqkv, g, diag, hosted=None):
    s = g.shape[0]
    nb = s // TQ
    left = A_KBLOCKS - 1
    pairs = A_PAIRS_FWD
    pw = pairs * LANES
    wide = A_WIN + TQ

    def body(q_ref, *refs):
        k_refs = refs[:A_KBLOCKS]
        v_refs = refs[A_KBLOCKS:2 * A_KBLOCKS]
        g_ref, diag_ref, o_ref, u_ref, lse_ref, bias_scr = refs[2 * A_KBLOCKS:]
        b = pl.program_id(1)

        @pl.when(b == 0)
        def _():
            for hh in range(2 * pairs):
                bias_scr[hh // 2, :, (hh % 2) * A_WIN:(hh % 2 + 1) * A_WIN] = _toeplitz_tile(
                    diag_ref[hh], A_WIN, A_LEFT_CHUNKS)

        def step(first_blocks):
            lo = _lane_lo(TQ)
            for pp in range(pairs):
                ln = slice(pp * LANES, (pp + 1) * LANES)
                kcat = _stack_pair(jnp.concatenate([r[:, ln] for r in k_refs], axis=0))
                vcat = _stack_pair(jnp.concatenate([r[:, ln] for r in v_refs], axis=0))
                sc = lax.dot_general(q_ref[:, ln] * SCALE, kcat, NT, preferred_element_type=F32) + bias_scr[pp]
                if first_blocks:
                    sc = jnp.where(_block_valid(b, left, A_WIN), sc, MASKED)
                p, inv, lses = _softmax_pair(sc, A_WIN)
                ov = jnp.dot(p, vcat, preferred_element_type=F32) * jnp.where(lo, inv[0], inv[1])
                o_ref[:, ln] = ov
                lse_ref[pp] = jnp.where(lo, lses[0], lses[1])
                sg, _ = _silu_parts(g_ref[:, ln])
                u_ref[:, ln] = (ov * sg).astype(BF16)

        @pl.when(b < left)
        def _():
            step(True)

        @pl.when(b >= left)
        def _():
            step(False)

    tile = pl.BlockSpec((TQ, pw), lambda p, b: (b, p))
    return _call(
        body, name="attn_a_fwd", grid=(HEADS // 2 // pairs, nb),
        in_specs=[pl.BlockSpec((None, TQ, pw), lambda p, b: (0, b, p))] + _a_kv_specs(left, pw) + [
            tile, pl.BlockSpec((2 * pairs, 1, wide), lambda p, b: (p, 0, 0))],
        out_specs=[tile, tile, pl.BlockSpec((pairs, TQ, LANES), lambda p, b: (p, b, 0))],
        out_shape=[jax.ShapeDtypeStruct((s, D_MODEL), F32), jax.ShapeDtypeStruct((s, D_MODEL), BF16),
                   jax.ShapeDtypeStruct((HEADS // 2, s, LANES), F32)],
        scratch_shapes=[pltpu.VMEM((pairs, TQ, 2 * A_WIN), F32)],
        sem=("parallel", "arbitrary"), hosted=hosted,
        args=(zqkv, *([zqkv] * (2 * A_KBLOCKS)), g, diag))


def _attn_a_bwd(zqkv, g, o, du, lse, diag, hosted=None):
    s = g.shape[0]
    nb = s // TQ
    left = A_KBLOCKS - 1
    pw = A_PAIRS * LANES
    wide = A_WIN + TQ

    def body(q_ref, *refs):
        k_refs = refs[:A_KBLOCKS]
        v_refs = refs[A_KBLOCKS:2 * A_KBLOCKS]
        (g_ref, o_ref, du_ref, lse_ref, diag_ref, dz_ref, ddiag_ref,
         bias_scr, dbias_acc, dk_acc, dv_acc) = refs[2 * A_KBLOCKS:]
        b = pl.program_id(1)

        @pl.when(b == 0)
        def _():
            for hh in range(2 * A_PAIRS):
                bias_scr[hh // 2, :, (hh % 2) * A_WIN:(hh % 2 + 1) * A_WIN] = _toeplitz_tile(
                    diag_ref[hh], A_WIN, A_LEFT_CHUNKS)
            dbias_acc[...] = jnp.zeros_like(dbias_acc)
            dk_acc[...] = jnp.zeros_like(dk_acc)
            dv_acc[...] = jnp.zeros_like(dv_acc)

        def step(first_blocks):
            lo = _lane_lo(TQ)
            rows = pl.ds(pl.multiple_of(b * TQ, TQ), TQ)
            sg, dsg = _silu_parts(g_ref[...])
            duv = du_ref[...]
            ov = o_ref[...]
            do = duv * sg
            dz_ref[3, rows, :] = (duv * ov * dsg).astype(BF16)
            do_o = do * ov
            do_bf = do.astype(BF16)
            for pp in range(A_PAIRS):
                ln = slice(pp * LANES, (pp + 1) * LANES)
                q = q_ref[:, ln] * SCALE
                kcat = _stack_pair(jnp.concatenate([r[:, ln] for r in k_refs], axis=0))
                vcat = _stack_pair(jnp.concatenate([r[:, ln] for r in v_refs], axis=0))
                sc = lax.dot_general(q, kcat, NT, preferred_element_type=F32) + bias_scr[pp]
                if first_blocks:
                    sc = jnp.where(_block_valid(b, left, A_WIN), sc, MASKED)
                lse_t = lse_ref[pp]
                dp = lax.dot_general(do_bf[:, ln], vcat, NT, preferred_element_type=F32)
                p, ds = _softmax_pair_bwd(sc, dp, (lse_t[:, 0:1], lse_t[:, HEAD_DIM:HEAD_DIM + 1]),
                                          _pair_rowsums(do_o[:, ln], lo), A_WIN)
                dbias_acc[pp] += ds
                dsb = ds.astype(BF16)
                dz_ref[0, rows, ln] = (jnp.dot(dsb, kcat, preferred_element_type=F32) * SCALE).astype(BF16)
                dk = _unstack_pair(lax.dot_general(dsb, q, TN, preferred_element_type=F32), A_WIN)
                dv = _unstack_pair(
                    lax.dot_general(p.astype(BF16), do_bf[:, ln], TN, preferred_element_type=F32), A_WIN)
                for t in range(A_KBLOCKS):
                    krows = pl.ds(pl.multiple_of(jnp.maximum(b - left + t, 0) * KB, KB), KB)
                    dk_acc[krows, ln] += dk[t * KB:(t + 1) * KB, :]
                    dv_acc[krows, ln] += dv[t * KB:(t + 1) * KB, :]

        @pl.when(b < left)
        def _():
            step(True)

        @pl.when(b >= left)
        def _():
            step(False)

        @pl.when(b == nb - 1)
        def _():
            dz_ref[1] = dk_acc[...].astype(BF16)
            dz_ref[2] = dv_acc[...].astype(BF16)
            for hh in range(2 * A_PAIRS):
                ddiag_ref[hh] = _toeplitz_sum(
                    dbias_acc[hh // 2, :, (hh % 2) * A_WIN:(hh % 2 + 1) * A_WIN], A_WIN)

    tile = pl.BlockSpec((TQ, pw), lambda p, b: (b, p))
    diag_spec = pl.BlockSpec((2 * A_PAIRS, 1, wide), lambda p, b: (p, 0, 0))
    return _call(
        body, name="attn_a_bwd", grid=(HEADS // 2 // A_PAIRS, nb),
        in_specs=[pl.BlockSpec((None, TQ, pw), lambda p, b: (0, b, p))] + _a_kv_specs(left, pw) + [
            tile, tile, tile, pl.BlockSpec((A_PAIRS, TQ, LANES), lambda p, b: (p, b, 0)), diag_spec],
        out_specs=[pl.BlockSpec((4, s, pw), lambda p, b: (0, 0, p)), diag_spec],
        out_shape=[jax.ShapeDtypeStruct((4, s, D_MODEL), BF16),
                   jax.ShapeDtypeStruct((HEADS, 1, wide), F32)],
        scratch_shapes=[pltpu.VMEM((A_PAIRS, TQ, 2 * A_WIN), F32), pltpu.VMEM((A_PAIRS, TQ, 2 * A_WIN), F32),
                        pltpu.VMEM((s, pw), F32), pltpu.VMEM((s, pw), F32)],
        sem=("parallel", "arbitrary"), hosted=hosted,
        args=(zqkv, *([zqkv] * (2 * A_KBLOCKS)), g, o, du, lse, diag))


B_STACK = B_GROUP // 2
B_KVX = 4 * LANES


def _b_head_place(h):
    return h // B_GROUP, (h % B_GROUP) // 2, h % 2


def _b_build_bias(diag_ref, bias_scr):
    for h in range(HEADS):
        gi, pr, e = _b_head_place(h)
        bias_scr[gi, pr * TQ:(pr + 1) * TQ, e * B_WIN:(e + 1) * B_WIN] = _toeplitz_tile(
            diag_ref[h], B_WIN, B_LEFT_CHUNKS)


def _b_stack(x, gi):
    return jnp.concatenate(
        [x[:, (B_STACK * gi + pr) * LANES:(B_STACK * gi + pr + 1) * LANES] for pr in range(B_STACK)], axis=0)


def _b_sinks(sink_ref, gi):
    return [jnp.concatenate(
        [jnp.broadcast_to(sink_ref[0:1, h:h + 1], (TQ, 1))
         for h in range(B_GROUP * gi + e, B_GROUP * (gi + 1), 2)], axis=0) for e in range(2)]


def _attn_b_fwd(qb, kvx, gate, diag, sinks):
    s = qb.shape[0]
    nb = s // TQ
    left = B_KBLOCKS - 1
    rows4 = B_STACK * TQ

    def body(q_ref, *refs):
        kv_refs = refs[:B_KBLOCKS]
        g_ref, diag_ref, sink_ref, o_ref, u_ref, lse_ref, bias_scr = refs[B_KBLOCKS:]
        b = pl.program_id(0)

        @pl.when(b == 0)
        def _():
            _b_build_bias(diag_ref, bias_scr)

        def step(first_blocks):
            kvv = jnp.concatenate([r[...] for r in kv_refs], axis=0)
            lo = _lane_lo(rows4)
            for gi in range(B_KV_HEADS):
                kcat = _stack_pair(kvv[:, gi * LANES:(gi + 1) * LANES])
                vcat = _stack_pair(kvv[:, (B_KV_HEADS + gi) * LANES:(B_KV_HEADS + gi + 1) * LANES])
                qs = _b_stack(q_ref, gi) * SCALE
                sc = lax.dot_general(qs, kcat, NT, preferred_element_type=F32) + bias_scr[gi]
                if first_blocks:
                    sc = jnp.where(_block_valid(b, left, B_WIN), sc, MASKED)
                p, inv, lses = _softmax_pair(sc, B_WIN, _b_sinks(sink_ref, gi))
                ov = jnp.dot(p, vcat, preferred_element_type=F32) * jnp.where(lo, inv[0], inv[1])
                lse_t = jnp.where(lo, lses[0], lses[1])
                for pr in range(B_STACK):
                    pair = B_STACK * gi + pr
                    o_ref[:, pair * LANES:(pair + 1) * LANES] = ov[pr * TQ:(pr + 1) * TQ]
                    lse_ref[pair] = lse_t[pr * TQ:(pr + 1) * TQ]
            sg, _ = _silu_parts(g_ref[...])
            u_ref[...] = (o_ref[...] * sg).astype(BF16)

        @pl.when(b < left)
        def _():
            step(True)

        @pl.when(b >= left)
        def _():
            step(False)

    kv_specs = [pl.BlockSpec((KB, B_KVX), functools.partial(
        lambda b, t: (jnp.maximum(b - left + t, 0), 0), t=t)) for t in range(B_KBLOCKS)]
    row = pl.BlockSpec((TQ, D_MODEL), lambda b: (b, 0))
    return pl.pallas_call(
        body, name="attn_b_fwd", grid=(nb,),
        in_specs=[row] + kv_specs + [row, pl.BlockSpec((HEADS, 1, B_WIN + TQ), lambda b: (0, 0, 0)),
                                     pl.BlockSpec((1, HEADS), lambda b: (0, 0))],
        out_specs=[row, row, pl.BlockSpec((HEADS // 2, TQ, LANES), lambda b: (0, b, 0))],
        out_shape=[jax.ShapeDtypeStruct((s, D_MODEL), F32), jax.ShapeDtypeStruct((s, D_MODEL), BF16),
                   jax.ShapeDtypeStruct((HEADS // 2, s, LANES), F32)],
        scratch_shapes=[pltpu.VMEM((B_KV_HEADS, rows4, 2 * B_WIN), F32)],
        compiler_params=_params(("arbitrary",)),
    )(qb, *([kvx] * B_KBLOCKS), gate, diag, sinks)


def _attn_b_bwd(qb, kvx, gate, o, du, lse, diag, sinks):
    s = qb.shape[0]
    nb = s // TQ
    left = B_KBLOCKS - 1
    rows4 = B_STACK * TQ
    half = D_MODEL // 2

    def body(q_ref, *refs):
        kv_refs = refs[:B_KBLOCKS]
        (g_ref, o_ref, du_ref, lse_ref, diag_ref, sink_ref, dz_ref, dkv_ref, ddiag_ref, dsink_ref,
         bias_scr, dbias_acc, dkv_acc, dsink_acc) = refs[B_KBLOCKS:]
        b = pl.program_id(0)

        @pl.when(b == 0)
        def _():
            _b_build_bias(diag_ref, bias_scr)
            dbias_acc[...] = jnp.zeros_like(dbias_acc)
            dkv_acc[...] = jnp.zeros_like(dkv_acc)
            dsink_acc[...] = jnp.zeros_like(dsink_acc)

        kvv = jnp.concatenate([r[...] for r in kv_refs], axis=0)
        valid = _block_valid(b, left, B_WIN)
        lo = _lane_lo(rows4)
        sg, dsg = _silu_parts(g_ref[...])
        duv = du_ref[...]
        ov = o_ref[...]
        do = duv * sg
        dgate = (duv * ov * dsg).astype(BF16)
        dz_ref[2] = dgate[:, :half]
        dz_ref[3] = dgate[:, half:]
        do_o = do * ov
        do_bf = do.astype(BF16)
        for gi in range(B_KV_HEADS):
            kcat = _stack_pair(kvv[:, gi * LANES:(gi + 1) * LANES])
            vcat = _stack_pair(kvv[:, (B_KV_HEADS + gi) * LANES:(B_KV_HEADS + gi + 1) * LANES])
            qs = _b_stack(q_ref, gi) * SCALE
            dos = _b_stack(do_bf, gi)
            delta = _pair_rowsums(_b_stack(do_o, gi), lo)
            lse_t = jnp.concatenate([lse_ref[B_STACK * gi + pr] for pr in range(B_STACK)], axis=0)
            lse2 = (lse_t[:, 0:1], lse_t[:, HEAD_DIM:HEAD_DIM + 1])
            sink = _b_sinks(sink_ref, gi)
            sc = lax.dot_general(qs, kcat, NT, preferred_element_type=F32) + bias_scr[gi]
            sc = jnp.where(valid, sc, MASKED)
            dp = lax.dot_general(dos, vcat, NT, preferred_element_type=F32)
            p, ds = _softmax_pair_bwd(sc, dp, lse2, delta, B_WIN)
            dbias_acc[gi] += ds
            dsink_acc[gi] += jnp.where(lo, -jnp.exp(sink[0] - lse2[0]) * delta[0],
                                       -jnp.exp(sink[1] - lse2[1]) * delta[1])
            dsq = ds.astype(BF16)
            dq = (jnp.dot(dsq, kcat, preferred_element_type=F32) * SCALE).astype(BF16)
            for pr in range(B_STACK):
                dz_ref[gi, :, pr * LANES:(pr + 1) * LANES] = dq[pr * TQ:(pr + 1) * TQ]
            dk = _unstack_pair(lax.dot_general(dsq, qs, TN, preferred_element_type=F32), B_WIN)
            dv = _unstack_pair(lax.dot_general(p.astype(BF16), dos, TN, preferred_element_type=F32), B_WIN)
            for t in range(B_KBLOCKS):
                krows = pl.ds(pl.multiple_of(jnp.maximum(b - left + t, 0) * KB, KB), KB)
                dkv_acc[krows, gi * LANES:(gi + 1) * LANES] += dk[t * KB:(t + 1) * KB, :]
                dkv_acc[krows, (B_KV_HEADS + gi) * LANES:(B_KV_HEADS + gi + 1) * LANES] += dv[t * KB:(t + 1) * KB, :]

        @pl.when(b == nb - 1)
        def _():
            lo_s = _lane_lo(s)
            for which in range(2):
                folded = []
                for gi in range(B_KV_HEADS):
                    part = dkv_acc[:, (which * B_KV_HEADS + gi) * LANES:(which * B_KV_HEADS + gi + 1) * LANES]
                    folded.append(part + pltpu.roll(part, HEAD_DIM, 1))
                dkv_ref[:, which * LANES:(which + 1) * LANES] = jnp.where(lo_s, folded[0], folded[1]).astype(BF16)
            lane8 = lax.broadcasted_iota(jnp.int32, dsink_ref.shape, 1)
            tot = jnp.zeros(dsink_ref.shape, F32)
            for h in range(HEADS):
                gi, pr, e = _b_head_place(h)
                ddiag_ref[h] = _toeplitz_sum(
                    dbias_acc[gi, pr * TQ:(pr + 1) * TQ, e * B_WIN:(e + 1) * B_WIN], B_WIN)
                col = dsink_acc[gi, pr * TQ:(pr + 1) * TQ, e * HEAD_DIM:e * HEAD_DIM + 1]
                tot = jnp.where(lane8 == h, jnp.sum(col, axis=0, keepdims=True), tot)
            dsink_ref[...] = tot

    kv_specs = [pl.BlockSpec((KB, B_KVX), functools.partial(
        lambda b, t: (jnp.maximum(b - left + t, 0), 0), t=t)) for t in range(B_KBLOCKS)]
    row = pl.BlockSpec((TQ, D_MODEL), lambda b: (b, 0))
    diag_spec = pl.BlockSpec((HEADS, 1, B_WIN + TQ), lambda b: (0, 0, 0))
    return pl.pallas_call(
        body, name="attn_b_bwd", grid=(nb,),
        in_specs=[row] + kv_specs + [row, row, row, pl.BlockSpec((HEADS // 2, TQ, LANES), lambda b: (0, b, 0)),
                                     diag_spec, pl.BlockSpec((1, HEADS), lambda b: (0, 0))],
        out_specs=[pl.BlockSpec((4, TQ, half), lambda b: (0, b, 0)),
                   pl.BlockSpec((s, 2 * LANES), lambda b: (0, 0)), diag_spec,
                   pl.BlockSpec((8, LANES), lambda b: (0, 0))],
        out_shape=[jax.ShapeDtypeStruct((4, s, half), BF16), jax.ShapeDtypeStruct((s, 2 * LANES), BF16),
                   jax.ShapeDtypeStruct((HEADS, 1, B_WIN + TQ), F32), jax.ShapeDtypeStruct((8, LANES), F32)],
        scratch_shapes=[pltpu.VMEM((B_KV_HEADS, rows4, 2 * B_WIN), F32),
                        pltpu.VMEM((B_KV_HEADS, rows4, 2 * B_WIN), F32),
                        pltpu.VMEM((s, B_KVX), F32), pltpu.VMEM((B_KV_HEADS, rows4, LANES), F32)],
        compiler_params=_params(("arbitrary",)),
    )(qb, *([kvx] * B_KBLOCKS), gate, o, du, lse, diag, sinks)


def _t5_bucket(rel):
    nb = T5_BUCKETS // 2
    max_exact = nb // 2
    ret = jnp.where(rel > 0, nb, 0)
    n = jnp.abs(rel)
    nf = jnp.maximum(n, 1).astype(jnp.float32)
    large = max_exact + (jnp.log(nf / max_exact) / math.log(T5_MAX_DIST / max_exact)
                         * (nb - max_exact)).astype(jnp.int32)
    large = jnp.minimum(large, nb - 1)
    return ret + jnp.where(n < max_exact, n, large)


def _a_offset_onehot():
    c = np.arange(A_WIN + TQ)
    dist = A_LEFT_CHUNKS * CHUNK + TQ - 1 - c
    idx = np.clip(dist, -A_REL_CLIP, A_REL_CLIP) + A_REL_CLIP
    onehot = np.zeros((A_WIN + TQ, 2 * A_REL_CLIP + 1), np.float32)
    onehot[c, idx] = 1.0
    return jnp.asarray(onehot)


def _b_offset_onehot():
    c = jnp.arange(B_WIN + TQ, dtype=jnp.int32)
    rel = c - (TQ - 1) - B_LEFT_CHUNKS * CHUNK
    return (_t5_bucket(rel)[:, None] == jnp.arange(T5_BUCKETS)[None, :]).astype(F32)


def _diag_rows(onehot, table):
    rows = jnp.dot(onehot, table.astype(F32), precision=lax.Precision.HIGHEST)
    return rows.T.reshape(HEADS, 1, onehot.shape[0])


def _diag_rows_grad(onehot, ddiag):
    return jnp.dot(ddiag.reshape(HEADS, onehot.shape[0]), onehot, precision=lax.Precision.HIGHEST).T


def _position():
    x, y, c = lax.axis_index("x"), lax.axis_index("y"), lax.axis_index("c")
    chips = [(1 - x, y), (x, 1 - y), (1 - x, 1 - y)]
    return x, y, c, chips


ANY = pl.BlockSpec(memory_space=pl.ANY)


def _allgather_hosted(shards, split):
    n = len(shards)

    def part(ref, t, half):
        if not split[t]:
            return ref
        rows = shards[t].shape[0] // 2
        return ref.at[pl.ds(half * rows, rows)]

    def copies(kind, ins, outs, sems):
        send_sems, recv_sems, pass_send, pass_recv, local_sems = sems
        x, y, c, chips = _position()
        mine = 2 * x + y
        if kind == "local":
            return [pltpu.make_async_copy(ins[t], outs[t].at[mine], local_sems.at[t]) for t in range(n)]
        made = []
        for t in range(n):
            for j, chip in enumerate(chips):
                theirs = 2 * chip[0] + chip[1]
                far = dict(send_sem=send_sems.at[3 * t + j], recv_sem=recv_sems.at[3 * t + j],
                           device_id=(chip[0], chip[1], c), device_id_type=MESH)
                near = dict(send_sem=pass_send.at[3 * t + j], recv_sem=pass_recv.at[3 * t + j],
                            device_id=(x, y, 1 - c), device_id_type=MESH)
                here = part(outs[t].at[theirs], t, c)
                if kind == "send":
                    made.append(pltpu.make_async_remote_copy(
                        src_ref=part(ins[t], t, c), dst_ref=part(outs[t].at[mine], t, c), **far))
                elif kind == "landed":
                    made.append(pltpu.make_async_remote_copy(src_ref=here, dst_ref=here, **far))
                elif not split[t]:
                    made.append(None)
                elif kind == "pass":
                    made.append(pltpu.make_async_remote_copy(src_ref=here, dst_ref=here, **near))
                else:
                    other = part(outs[t].at[theirs], t, 1 - c)
                    made.append(pltpu.make_async_remote_copy(src_ref=other, dst_ref=other, **near))
        return made

    def first(ins, outs, sems):
        for cp in copies("local", ins, outs, sems) + copies("send", ins, outs, sems):
            cp.start()

    def middle(ins, outs, sems):
        for got, cp in zip(copies("landed", ins, outs, sems), copies("pass", ins, outs, sems)):
            got.wait_recv()
            if cp is not None:
                cp.start()

    def last(ins, outs, sems):
        for cp in copies("passed", ins, outs, sems):
            if cp is not None:
                cp.wait_recv()
        for cp in copies("send", ins, outs, sems) + copies("pass", ins, outs, sems):
            if cp is not None:
                cp.wait_send()
        for cp in copies("local", ins, outs, sems):
            cp.wait()

    return _Hosted(shards, [jax.ShapeDtypeStruct((4,) + w.shape, w.dtype) for w in shards],
                   [pltpu.SemaphoreType.DMA((3 * n,))] * 4 + [pltpu.SemaphoreType.DMA((n,))],
                   first, middle, last)


def _scatter_hosted(grads):
    n = len(grads)

    def copies(ins, outs, sems):
        send_sems, recv_sems = sems
        x, y, c, chips = _position()
        return [pltpu.make_async_remote_copy(
            src_ref=ins[t].at[2 * chip[0] + chip[1]], dst_ref=outs[t].at[j],
            send_sem=send_sems.at[3 * t + j], recv_sem=recv_sems.at[3 * t + j],
            device_id=(chip[0], chip[1], c), device_id_type=MESH)
            for t in range(n) for j, chip in enumerate(chips)]

    def first(ins, outs, sems):
        for cp in copies(ins, outs, sems):
            cp.start()

    def last(ins, outs, sems):
        for cp in copies(ins, outs, sems):
            cp.wait()

    return _Hosted(grads, [jax.ShapeDtypeStruct((3,) + g.shape[1:], g.dtype) for g in grads],
                   [pltpu.SemaphoreType.DMA((3 * n,))] * 2, first, None, last)


def _run_alone(name, hosted):
    n_in = len(hosted.inputs)
    n_out = len(hosted.out_shapes)

    def body(*refs):
        ins, outs, sems = refs[:n_in], refs[n_in:n_in + n_out], refs[n_in + n_out:]
        hosted.first(ins, outs, sems)
        if hosted.middle is not None:
            hosted.middle(ins, outs, sems)
        hosted.last(ins, outs, sems)

    return pl.pallas_call(
        body, name=name, in_specs=[ANY] * n_in, out_specs=[ANY] * n_out, out_shape=hosted.out_shapes,
        scratch_shapes=hosted.sems)(*hosted.inputs)


def _swap_with_sibling(blocks):
    n = len(blocks)

    def body(*refs):
        ins, outs = refs[:n], refs[n:2 * n]
        send_sems, recv_sems = refs[2 * n:]
        x, y, c, _ = _position()
        sends = [pltpu.make_async_remote_copy(
            src_ref=ins[t], dst_ref=outs[t], send_sem=send_sems.at[t], recv_sem=recv_sems.at[t],
            device_id=(x, y, 1 - c), device_id_type=MESH) for t in range(n)]
        for cp in sends:
            cp.start()
        for cp in sends:
            cp.wait()

    return pl.pallas_call(
        body, name="swap_with_sibling",
        in_specs=[ANY] * n, out_specs=[ANY] * n,
        out_shape=[jax.ShapeDtypeStruct(b.shape, b.dtype) for b in blocks],
        scratch_shapes=[pltpu.SemaphoreType.DMA((n,))] * 2,
    )(*blocks)


def _allreduce_small(block):
    rows = block.shape[0]

    def body(in_ref, sum_ref, all_ref, send_sems, recv_sems):
        x, y, c, _ = _position()
        me = 4 * x + 2 * y + c
        all_ref[me] = in_ref[...]
        sends = []
        for k in range(1, 8):
            peer = (x ^ (k >> 2), y ^ ((k >> 1) & 1), c ^ (k & 1))
            sends.append(pltpu.make_async_remote_copy(
                src_ref=in_ref, dst_ref=all_ref.at[me], send_sem=send_sems.at[k - 1],
                recv_sem=recv_sems.at[k - 1], device_id=peer, device_id_type=MESH))
        for cp in sends:
            cp.start()
        for k in range(1, 8):
            theirs = me ^ k
            pltpu.make_async_remote_copy(
                src_ref=in_ref, dst_ref=all_ref.at[theirs], send_sem=send_sems.at[k - 1],
                recv_sem=recv_sems.at[k - 1], device_id=(x, y, c), device_id_type=MESH).wait_recv()
        for cp in sends:
            cp.wait_send()
        acc = all_ref[0]
        for d in range(1, 8):
            acc = acc + all_ref[d]
        sum_ref[...] = acc

    vmem = pl.BlockSpec(memory_space=pltpu.VMEM)
    return pl.pallas_call(
        body, name="allreduce_small",
        in_specs=[vmem], out_specs=[vmem, vmem],
        out_shape=[jax.ShapeDtypeStruct((rows, LANES), F32), jax.ShapeDtypeStruct((8, rows, LANES), F32)],
        scratch_shapes=[pltpu.SemaphoreType.DMA((7,))] * 2,
    )(block)[0]


def _adamw_math(w, g, m, v):
    m = ADAM_B1 * m + (1.0 - ADAM_B1) * g
    v = ADAM_B2 * v + (1.0 - ADAM_B2) * (g * g)
    m_hat = m / (1.0 - ADAM_B1 ** ADAM_STEP)
    v_hat = v / (1.0 - ADAM_B2 ** ADAM_STEP)
    delta = -ADAM_LR * (m_hat / (jnp.sqrt(v_hat) + ADAM_EPS) + ADAM_WD * w)
    return delta, m, v


def _row_tile(rows):
    return min(rows, 256)


def _sum_partials(name, own, recv):
    rows, cols = own.shape
    tr = _row_tile(rows)

    def body(own_ref, recv_ref, o_ref):
        acc = own_ref[...]
        for j in range(3):
            acc = acc + recv_ref[j].astype(F32)
        o_ref[...] = acc

    return pl.pallas_call(
        body, name=name, grid=(rows // tr,),
        in_specs=[pl.BlockSpec((tr, cols), lambda i: (i, 0)), pl.BlockSpec((3, tr, cols), lambda i: (0, i, 0))],
        out_specs=pl.BlockSpec((tr, cols), lambda i: (i, 0)),
        out_shape=jax.ShapeDtypeStruct((rows, cols), F32),
        compiler_params=_params(("parallel",)),
    )(own, recv)


def _adamw(name, w, m, v, g_parts):
    rows, cols = w.shape
    tr = _row_tile(rows)
    n = len(g_parts)

    def body(w_ref, m_ref, v_ref, *refs):
        g_refs = refs[:n]
        go_ref, d_ref, mo_ref, vo_ref = refs[n:]
        g = g_refs[0][...]
        for r in g_refs[1:]:
            g = g + r[...]
        delta, mn, vn = _adamw_math(w_ref[...], g, m_ref[...], v_ref[...])
        go_ref[...] = g
        d_ref[...] = delta
        mo_ref[...] = mn
        vo_ref[...] = vn

    spec = pl.BlockSpec((tr, cols), lambda i: (i, 0))
    return pl.pallas_call(
        body, name=name, grid=(rows // tr,),
        in_specs=[spec] * (3 + n), out_specs=[spec] * 4,
        out_shape=[jax.ShapeDtypeStruct((rows, cols), F32)] * 4,
        compiler_params=_params(("parallel",)),
    )(w, m, v, *g_parts)


def _local_step(x, target, ga, wa_in, rel_bias, later_shards, gk, t5, gb, sinks, gf):
    s, d = x.shape
    tm = min(TM_DENSE, s)
    nt = s // tm
    half = d // 2
    row = pl.BlockSpec((tm, d), lambda i: (i, 0))
    whole = lambda shape: pl.BlockSpec(shape, lambda *_: (0,) * len(shape))

    n1, = _norm_fwd("norm_a", x, ga)
    zqkv = _matmul("proj_a_qkv", n1, wa_in, dims=NN, grid=(3, nt),
                   a_spec=pl.BlockSpec((tm, d), lambda j, i: (i, 0)),
                   b_spec=pl.BlockSpec((None, d, d), lambda j, i: (j, 0, 0)),
                   o_spec=pl.BlockSpec((None, tm, d), lambda j, i: (j, i, 0)),
                   out_shape=(3, s, d), out_dtype=BF16)
    gate_a = _matmul("proj_a_gate", n1, wa_in, dims=NN, grid=(nt,),
                     a_spec=row, b_spec=pl.BlockSpec((None, d, d), lambda i: (3, 0, 0)), o_spec=row,
                     out_shape=(s, d), out_dtype=F32)
    onehot_a = _a_offset_onehot()
    diag_a = _diag_rows(onehot_a, rel_bias)
    (o_a, u_a, lse_a), gathered = _attn_a_fwd(
        zqkv, gate_a, diag_a, hosted=_allgather_hosted(later_shards, [True] * len(later_shards)))
    wa_out, wkv, wb_in, wb_out = gathered
    wa_out = wa_out.reshape(d, d)
    wkv = wkv.reshape(d, -1)
    wb_out = wb_out.reshape(d, d)
    h1 = _matmul("out_a", u_a, wa_out, dims=NN, grid=(nt,), a_spec=row, b_spec=whole((d, d)), o_spec=row,
                 out_shape=(s, d), out_dtype=F32, resid=x, resid_spec=row)

    nk, n2 = _norm_fwd("norm_kv_b", h1, jnp.concatenate([gk, gb], axis=0))
    kvw = wkv.shape[1]
    wkv_x = jnp.concatenate([wkv[:, (i // 2) * HEAD_DIM:(i // 2 + 1) * HEAD_DIM] for i in range(8)], axis=1)
    kvx = _matmul("proj_kv", nk, wkv_x, dims=NN, grid=(nt,), a_spec=row, b_spec=whole((d, B_KVX)),
                  o_spec=pl.BlockSpec((tm, B_KVX), lambda i: (i, 0)), out_shape=(s, B_KVX), out_dtype=BF16)
    qb = _matmul("proj_b_q", n2, wb_in, dims=NN, grid=(2, nt),
                 a_spec=pl.BlockSpec((tm, d), lambda j, i: (i, 0)),
                 b_spec=pl.BlockSpec((None, d, half), lambda j, i: (j, 0, 0)),
                 o_spec=pl.BlockSpec((tm, half), lambda j, i: (i, j)), out_shape=(s, d), out_dtype=BF16)
    gate_b = _matmul("proj_b_gate", n2, wb_in, dims=NN, grid=(2, nt),
                     a_spec=pl.BlockSpec((tm, d), lambda j, i: (i, 0)),
                     b_spec=pl.BlockSpec((None, d, half), lambda j, i: (2 + j, 0, 0)),
                     o_spec=pl.BlockSpec((tm, half), lambda j, i: (i, j)), out_shape=(s, d), out_dtype=F32)
    onehot_b = _b_offset_onehot()
    diag_b = _diag_rows(onehot_b, t5)
    o_b, u_b, lse_b = _attn_b_fwd(qb, kvx, gate_b, diag_b, sinks)
    h2 = _matmul("out_b", u_b, wb_out, dims=NN, grid=(nt,), a_spec=row, b_spec=whole((d, d)), o_spec=row,
                 out_shape=(s, d), out_dtype=F32, resid=h1, resid_spec=row)

    dh2, loss, d_gf = _loss_head(h2, target, gf)

    du_b = _matmul("dout_b", dh2, wb_out, dims=NT, grid=(nt,), a_spec=row, b_spec=whole((d, d)), o_spec=row,
                   out_shape=(s, d), out_dtype=F32)
    d_wb_out = _matmul("dw_out_b", u_b, dh2, dims=TN, grid=(2,),
                       a_spec=whole((s, d)), b_spec=pl.BlockSpec((s, half), lambda j: (0, j)),
                       o_spec=pl.BlockSpec((d, half), lambda j: (0, j)),
                       out_shape=(d, d), out_dtype=F32, also_bf16=True)
    dz_b, dkv, ddiag_b, dsinks = _attn_b_bwd(qb, kvx, gate_b, o_b, du_b, lse_b, diag_b, sinks)
    dn2 = _matmul("dproj_b", dz_b, wb_in, dims=NT, grid=(nt,), parts=4,
                  a_spec=pl.BlockSpec((4, tm, half), lambda i: (0, i, 0)), b_spec=whole((4, d, half)),
                  o_spec=row, out_shape=(s, d), out_dtype=F32)
    d_wb_in = _matmul("dw_in_b", n2, dz_b, dims=TN, grid=(4,),
                      a_spec=whole((s, d)), b_spec=pl.BlockSpec((None, s, half), lambda j: (j, 0, 0)),
                      o_spec=pl.BlockSpec((None, d, half), lambda j: (j, 0, 0)),
                      out_shape=(4, d, half), out_dtype=F32, also_bf16=True)
    dnk = _matmul("dproj_kv", dkv, wkv, dims=NT, grid=(nt,),
                  a_spec=pl.BlockSpec((tm, kvw), lambda i: (i, 0)), b_spec=whole((d, kvw)), o_spec=row,
                  out_shape=(s, d), out_dtype=F32)
    d_wkv = _matmul("dw_kv", nk, dkv, dims=TN, grid=(1,),
                    a_spec=whole((s, d)), b_spec=whole((s, kvw)), o_spec=whole((d, kvw)),
                    out_shape=(d, kvw), out_dtype=F32, also_bf16=True)
    dh1, d_gkb = _norm_bwd("dnorm_kv_b", h1, dh2, [dnk, dn2], jnp.concatenate([gk, gb], axis=0))

    du_a = _matmul("dout_a", dh1, wa_out, dims=NT, grid=(nt,), a_spec=row, b_spec=whole((d, d)), o_spec=row,
                   out_shape=(s, d), out_dtype=F32)
    d_wa_out = _matmul("dw_out_a", u_a, dh1, dims=TN, grid=(2,),
                       a_spec=whole((s, d)), b_spec=pl.BlockSpec((s, half), lambda j: (0, j)),
                       o_spec=pl.BlockSpec((d, half), lambda j: (0, j)),
                       out_shape=(d, d), out_dtype=F32, also_bf16=True)
    early = dict(a_w_out=[g.reshape(4, d // 4, d) for g in d_wa_out],
                 kv_w=[g.reshape(4, d // 4, kvw) for g in d_wkv], b_w_in=list(d_wb_in),
                 b_w_out=[g.reshape(4, d // 4, d) for g in d_wb_out])
    (dz_a, ddiag_a), early_recv = _attn_a_bwd(
        zqkv, gate_a, o_a, du_a, lse_a, diag_a, hosted=_scatter_hosted([early[n][1] for n in early]))
    d_wa_in = _matmul("dw_in_a", n1, dz_a, dims=TN, grid=(4, 2),
                      a_spec=whole((s, d)), b_spec=pl.BlockSpec((None, s, half), lambda j, h: (j, 0, h)),
                      o_spec=pl.BlockSpec((None, d, half), lambda j, h: (j, 0, h)),
                      out_shape=(4, d, d), out_dtype=F32, also_bf16=True)
    tp = min(TM_PARTS, s)
    dn1, late_recv = _matmul("dproj_a", dz_a, wa_in, dims=NT, grid=(s // tp,), parts=4,
                             a_spec=pl.BlockSpec((4, tp, d), lambda i: (0, i, 0)), b_spec=whole((4, d, d)),
                             o_spec=pl.BlockSpec((tp, d), lambda i: (i, 0)),
                             out_shape=(s, d), out_dtype=F32, hosted=_scatter_hosted([d_wa_in[1]]))
    grad_x, d_ga = _norm_bwd("dnorm_a", x, dh1, [dn1], ga)

    small = dict(
        a_norm=d_ga, a_rel_bias=_diag_rows_grad(onehot_a, ddiag_a), kv_norm=d_gkb[0:1],
        t5_bias=_diag_rows_grad(onehot_b, ddiag_b), b_norm=d_gkb[1:2], b_sinks=dsinks[0:1, :HEADS],
        final_norm=d_gf)
    own = dict(a_w_in=d_wa_in[0], **{n: early[n][0] for n in early})
    received = dict(a_w_in=late_recv[0], **dict(zip(early, early_recv)))
    return loss, grad_x, small, own, received


SMALL = ("a_norm", "a_rel_bias", "kv_norm", "t5_bias", "b_norm", "b_sinks", "final_norm")
BIG = ("a_w_in", "a_w_out", "kv_w", "b_w_in", "b_w_out")
ORDER = ("a_norm", "a_w_in", "a_rel_bias", "a_w_out", "kv_norm", "kv_w", "t5_bias", "b_norm", "b_w_in",
         "b_sinks", "b_w_out", "final_norm")


def _pack(parts, rows):
    flat = jnp.concatenate([p.reshape(-1).astype(F32) for p in parts])
    return jnp.pad(flat, (0, rows * LANES - flat.shape[0])).reshape(rows, LANES)


def _unpack(block, shapes):
    flat = block.reshape(-1)
    out, at = [], 0
    for shp in shapes:
        size = int(np.prod(shp))
        out.append(flat[at:at + size].reshape(shp))
        at += size
    return out


def kernel(x, a_norm, a_w_in, a_rel_bias, a_w_out, kv_norm, kv_w, t5_bias, b_norm, b_w_in, b_sinks, b_w_out, final_norm, loss_target, m_a_norm, m_a_w_in, m_a_rel_bias, m_a_w_out, m_kv_norm, m_kv_w, m_t5_bias, m_b_norm, m_b_w_in, m_b_sinks, m_b_w_out, m_final_norm, v_a_norm, v_a_w_in, v_a_rel_bias, v_a_w_out, v_kv_norm, v_kv_w, v_t5_bias, v_b_norm, v_b_w_in, v_b_sinks, v_b_w_out, v_final_norm):
    w = dict(a_norm=a_norm, a_w_in=a_w_in, a_rel_bias=a_rel_bias, a_w_out=a_w_out, kv_norm=kv_norm, kv_w=kv_w,
             t5_bias=t5_bias, b_norm=b_norm, b_w_in=b_w_in, b_sinks=b_sinks, b_w_out=b_w_out,
             final_norm=final_norm)
    m = dict(a_norm=m_a_norm, a_w_in=m_a_w_in, a_rel_bias=m_a_rel_bias, a_w_out=m_a_w_out, kv_norm=m_kv_norm,
             kv_w=m_kv_w, t5_bias=m_t5_bias, b_norm=m_b_norm, b_w_in=m_b_w_in, b_sinks=m_b_sinks,
             b_w_out=m_b_w_out, final_norm=m_final_norm)
    v = dict(a_norm=v_a_norm, a_w_in=v_a_w_in, a_rel_bias=v_a_rel_bias, a_w_out=v_a_w_out, kv_norm=v_kv_norm,
             kv_w=v_kv_w, t5_bias=v_t5_bias, b_norm=v_b_norm, b_w_in=v_b_w_in, b_sinks=v_b_sinks,
             b_w_out=v_b_w_out, final_norm=v_final_norm)
    d = D_MODEL
    chip = 2 * lax.axis_index("x") + lax.axis_index("y")

    shard2d = dict(a_w_in=a_w_in[0], a_w_out=a_w_out[0], kv_w=kv_w, b_w_in=b_w_in[0], b_w_out=b_w_out[0])

    wa_in, ga = _run_alone("allgather_first",
                           _allgather_hosted([shard2d["a_w_in"].astype(BF16), a_norm], [True, False]))
    ga = ga.reshape(1, d)

    loss, grad_x, small, own, received = _local_step(
        x[0], loss_target[0], ga, wa_in, a_rel_bias[0], [shard2d[n].astype(BF16) for n in BIG[1:]],
        kv_norm.reshape(1, d), t5_bias, b_norm, b_sinks, final_norm.reshape(1, d))

    small_shapes = [small[n].shape for n in SMALL] + [(1, 1)]
    total = sum(int(np.prod(s)) for s in small_shapes)
    rows = -(-total // (8 * LANES)) * 8
    reduced = _unpack(_allreduce_small(_pack([small[n] for n in SMALL] + [loss], rows)), small_shapes)
    g_small = dict(zip(SMALL, reduced[:-1]))
    loss_out = reduced[-1].reshape(())
    g_small["a_norm"] = lax.dynamic_slice_in_dim(g_small["a_norm"], chip * (d // 4), d // 4, axis=1)

    core_sums = [
        _sum_partials("sum_" + n, lax.dynamic_index_in_dim(own[n], chip, 0, keepdims=False), received[n])
        for n in BIG]
    sibling_sums = _swap_with_sibling(core_sums)

    out = {}
    for n, mine, theirs in zip(BIG, core_sums, sibling_sums):
        res = _adamw("adamw_" + n, shard2d[n], m[n].reshape(shard2d[n].shape), v[n].reshape(shard2d[n].shape),
                     [mine, theirs])
        out[n] = [r.reshape(w[n].shape) for r in res]
    small_w_shapes = [w[n].shape for n in SMALL]
    total_w = sum(int(np.prod(s)) for s in small_w_shapes)
    rows_w = -(-total_w // (8 * LANES)) * 8
    packed = [_pack([t[n] for n in SMALL], rows_w) for t in (w, m, v)]
    g_packed = _pack([g_small[n] for n in SMALL], rows_w)
    res = _adamw("adamw_small", packed[0], packed[1], packed[2], [g_packed])
    unpacked = [_unpack(r, small_w_shapes) for r in res]
    for i, n in enumerate(SMALL):
        out[n] = [unpacked[k][i] for k in range(4)]

    grads = [out[n][0] for n in ORDER]
    deltas = [out[n][1] for n in ORDER]
    new_m = [out[n][2] for n in ORDER]
    new_v = [out[n][3] for n in ORDER]
    return (loss_out, grad_x[None], *grads, *deltas, *new_m, *new_v)
```

```python
import functools
import math

import jax
import jax.numpy as jnp
import numpy as np
from jax import lax
from jax.experimental import pallas as pl
from jax.experimental.pallas import tpu as pltpu

F32 = jnp.float32
BF16 = jnp.bfloat16
MESH = pl.DeviceIdType.MESH

D_MODEL = 1024
HEADS = 16
HEAD_DIM = 64
CHUNK = 64
RMS_EPS = 1e-6
SCALE = HEAD_DIM ** -0.5
A_LEFT_CHUNKS = 8
A_REL_CLIP = 256
B_LEFT_CHUNKS = 2
B_KV_HEADS = 2
B_GROUP = HEADS // B_KV_HEADS
T5_BUCKETS = 32
T5_MAX_DIST = 128
ADAM_LR = 0.001
ADAM_B1 = 0.9
ADAM_B2 = 0.999
ADAM_EPS = 1e-08
ADAM_WD = 0.01
ADAM_STEP = 10

MASKED = -1e30
LANES = 128
TQ = 128
A_PAIRS = 2
A_PAIRS_FWD = 4
KB = 128
A_KBLOCKS = A_LEFT_CHUNKS * CHUNK // KB + 1
B_KBLOCKS = B_LEFT_CHUNKS * CHUNK // KB + 1
A_WIN = A_KBLOCKS * KB
B_WIN = B_KBLOCKS * KB
TM = 512
TM_DENSE = 1024
TM_PARTS = 512
VMEM_LIMIT = 56 * 1024 * 1024

NT = (((1,), (1,)), ((), ()))
TN = (((0,), (0,)), ((), ()))
NN = (((1,), (0,)), ((), ()))


def _params(sem=None):
    return pltpu.CompilerParams(dimension_semantics=sem, vmem_limit_bytes=VMEM_LIMIT)


class _Hosted:
    def __init__(self, inputs, out_shapes, sems, first, middle, last):
        self.inputs, self.out_shapes, self.sems = list(inputs), list(out_shapes), list(sems)
        self.first, self.middle, self.last = first, middle, last


def _call(body, *, name, grid, in_specs, out_specs, out_shape, args, scratch_shapes=(), sem=None, hosted=None):
    in_specs, out_specs, out_shape = list(in_specs), list(out_specs), list(out_shape)
    scratch_shapes = list(scratch_shapes)
    if hosted is None:
        out = pl.pallas_call(
            body, name=name, grid=grid, in_specs=in_specs, out_specs=out_specs, out_shape=out_shape,
            scratch_shapes=scratch_shapes, compiler_params=_params(sem))(*args)
        return list(out), []
    n_in, n_out, n_scr = len(in_specs), len(out_shape), len(scratch_shapes)
    h_in, h_out = len(hosted.inputs), len(hosted.out_shapes)
    total = int(np.prod(grid)) if grid else 1

    def wrapped(*refs):
        ins, refs = refs[:n_in], refs[n_in:]
        h_ins, refs = refs[:h_in], refs[h_in:]
        outs, refs = refs[:n_out], refs[n_out:]
        h_outs, refs = refs[:h_out], refs[h_out:]
        scr, h_sems = refs[:n_scr], refs[n_scr:]
        step = 0
        for axis, size in enumerate(grid):
            step = step * size + pl.program_id(axis)

        @pl.when(step == 0)
        def _():
            hosted.first(h_ins, h_outs, h_sems)

        body(*ins, *outs, *scr)
        if hosted.middle is not None:
            @pl.when(step == total // 2)
            def _():
                hosted.middle(h_ins, h_outs, h_sems)

        @pl.when(step == total - 1)
        def _():
            hosted.last(h_ins, h_outs, h_sems)

    out = pl.pallas_call(
        wrapped, name=name, grid=grid, in_specs=in_specs + [ANY] * h_in, out_specs=out_specs + [ANY] * h_out,
        out_shape=out_shape + hosted.out_shapes, scratch_shapes=scratch_shapes + hosted.sems,
        compiler_params=_params(("arbitrary",) * len(grid)))(*args, *hosted.inputs)
    return list(out[:n_out]), list(out[n_out:])


def _matmul(name, a, b, *, dims, grid, a_spec, b_spec, o_spec, out_shape, out_dtype,
            parts=1, resid=None, resid_spec=None, also_bf16=False, hosted=None):
    def body(*refs):
        a_ref, b_ref = refs[:2]
        r_ref = refs[2] if resid is not None else None
        o_ref = refs[3] if resid is not None else refs[2]
        if parts == 1:
            prod = lax.dot_general(a_ref[...].astype(BF16), b_ref[...].astype(BF16), dims,
                                   preferred_element_type=F32)
        else:
            prod = None
            for part in range(parts):
                term = lax.dot_general(a_ref[part].astype(BF16), b_ref[part].astype(BF16), dims,
                                       preferred_element_type=F32)
                prod = term if prod is None else prod + term
        if resid is not None:
            prod = r_ref[...] + prod
        o_ref[...] = prod.astype(out_dtype)
        if also_bf16:
            refs[-1][...] = prod.astype(BF16)

    in_specs = [a_spec, b_spec]
    args = [a, b]
    if resid is not None:
        in_specs.append(resid_spec)
        args.append(resid)
    sem = ["parallel"] * len(grid)
    out_specs = [o_spec]
    out_shapes = [jax.ShapeDtypeStruct(out_shape, out_dtype)]
    if also_bf16:
        out_specs.append(o_spec)
        out_shapes.append(jax.ShapeDtypeStruct(out_shape, BF16))
    out, extra = _call(body, name=name, grid=grid, in_specs=in_specs, out_specs=out_specs, out_shape=out_shapes,
                       args=args, sem=tuple(sem), hosted=hosted)
    res = out[0] if not also_bf16 else tuple(out)
    return res if hosted is None else (res, extra)


def _rms_rows(x):
    return lax.rsqrt(jnp.mean(x * x, axis=-1, keepdims=True) + RMS_EPS)


def _norm_fwd(name, x, gains):
    s, d = x.shape
    n = gains.shape[0]

    def body(x_ref, g_ref, *o_refs):
        xv = x_ref[...]
        xh = xv * _rms_rows(xv)
        for i in range(n):
            o_refs[i][...] = (xh * g_ref[i:i + 1, :]).astype(BF16)

    row = pl.BlockSpec((TM, d), lambda i: (i, 0))
    return pl.pallas_call(
        body, name=name, grid=(s // TM,),
        in_specs=[row, pl.BlockSpec((n, d), lambda i: (0, 0))],
        out_specs=[row] * n,
        out_shape=[jax.ShapeDtypeStruct((s, d), BF16)] * n,
        compiler_params=_params(("parallel",)),
    )(x, gains)


def _norm_bwd(name, x, dres, dns, gains):
    s, d = x.shape
    n = len(dns)

    def body(x_ref, r_ref, g_ref, *refs):
        dn_refs, dx_ref, dg_ref = refs[:n], refs[n], refs[n + 1]
        i = pl.program_id(0)
        xv = x_ref[...]
        r = _rms_rows(xv)
        xh = xv * r

        @pl.when(i == 0)
        def _():
            dg_ref[...] = jnp.zeros_like(dg_ref)

        a = None
        for j in range(n):
            dn = dn_refs[j][...]
            t = dn * g_ref[j:j + 1, :]
            a = t if a is None else a + t
            dg_ref[j:j + 1, :] += jnp.sum(dn * xh, axis=0, keepdims=True)
        dx_ref[...] = r_ref[...] + r * (a - xh * jnp.mean(xh * a, axis=-1, keepdims=True))

    row = pl.BlockSpec((TM, d), lambda i: (i, 0))
    small = pl.BlockSpec((n, d), lambda i: (0, 0))
    return pl.pallas_call(
        body, name=name, grid=(s // TM,),
        in_specs=[row, row, small] + [row] * n,
        out_specs=[row, small],
        out_shape=[jax.ShapeDtypeStruct((s, d), F32), jax.ShapeDtypeStruct((n, d), F32)],
        compiler_params=_params(("arbitrary",)),
    )(x, dres, gains, *dns)


def _loss_head(h2, target, gain):
    s, d = h2.shape

    def body(h_ref, t_ref, g_ref, dh_ref, loss_ref, dg_ref):
        i = pl.program_id(0)
        hv = h_ref[...]
        r = _rms_rows(hv)
        hh = hv * r
        g = g_ref[...]
        err = hh * g - t_ref[...]
        part = 0.5 * jnp.sum(jnp.sum(err * err, axis=-1, keepdims=True) * (1.0 / d), axis=0, keepdims=True)
        dy = err * (1.0 / d)
        a = dy * g
        dh_ref[...] = r * (a - hh * jnp.mean(hh * a, axis=-1, keepdims=True))
        dg = jnp.sum(dy * hh, axis=0, keepdims=True)

        @pl.when(i == 0)
        def _():
            loss_ref[...] = part
            dg_ref[...] = dg

        @pl.when(i > 0)
        def _():
            loss_ref[...] += part
            dg_ref[...] += dg

    row = pl.BlockSpec((TM, d), lambda i: (i, 0))
    return pl.pallas_call(
        body, name="loss_head", grid=(s // TM,),
        in_specs=[row, row, pl.BlockSpec((1, d), lambda i: (0, 0))],
        out_specs=[row, pl.BlockSpec((1, 1), lambda i: (0, 0)), pl.BlockSpec((1, d), lambda i: (0, 0))],
        out_shape=[jax.ShapeDtypeStruct((s, d), F32), jax.ShapeDtypeStruct((1, 1), F32),
                   jax.ShapeDtypeStruct((1, d), F32)],
        compiler_params=_params(("arbitrary",)),
    )(h2, target, gain)


def _silu_parts(g):
    sig = jax.nn.sigmoid(g)
    return g * sig, sig * (1.0 + g * (1.0 - sig))


def _lane_lo(rows):
    return lax.broadcasted_iota(jnp.int32, (rows, LANES), 1) < HEAD_DIM


def _stack_pair(x):
    lo = _lane_lo(x.shape[0])
    zero = jnp.zeros_like(x)
    return jnp.concatenate([jnp.where(lo, x, zero), jnp.where(lo, zero, x)], axis=0)


def _unstack_pair(y, w):
    return jnp.where(_lane_lo(w), y[:w], y[w:])


def _block_valid(b, left_blocks, width):
    col = lax.broadcasted_iota(jnp.int32, (1, 2 * width), 1)
    col = jnp.where(col >= width, col - width, col)
    return (col // KB + (b - left_blocks)) >= 0


def _toeplitz_tile(diag_row, width, left_chunks):
    wide = width + TQ
    rolled = pltpu.roll(jnp.broadcast_to(diag_row, (TQ, wide)), 1, 1, stride=1, stride_axis=0)
    i = lax.broadcasted_iota(jnp.int32, (TQ, width), 0) // CHUNK
    j = lax.broadcasted_iota(jnp.int32, (TQ, width), 1) // CHUNK
    dc = i + left_chunks - j
    return jnp.where((dc >= 0) & (dc <= left_chunks), rolled[:, TQ:], MASKED)


def _toeplitz_sum(tile, width):
    flip = (lax.broadcasted_iota(jnp.int32, (TQ, TQ), 0) + lax.broadcasted_iota(jnp.int32, (TQ, TQ), 1)
            == TQ - 1).astype(F32)
    reversed_rows = jnp.dot(flip, tile, precision=lax.Precision.HIGHEST, preferred_element_type=F32)
    padded = jnp.concatenate([reversed_rows, jnp.zeros((TQ, TQ), F32)], axis=1)
    rolled = pltpu.roll(padded, 0, 1, stride=1, stride_axis=0)
    return jnp.sum(rolled, axis=0, keepdims=True)


def _softmax_pair(sc, w, sink=None):
    ps, inv, lses = [], [], []
    for e in range(2):
        sh = sc[:, e * w:(e + 1) * w]
        m = jnp.max(sh, axis=-1, keepdims=True)
        if sink is not None:
            m = jnp.maximum(m, sink[e])
        ex = jnp.exp(sh - m)
        l = jnp.sum(ex, axis=-1, keepdims=True)
        if sink is not None:
            l = l + jnp.exp(sink[e] - m)
        ps.append(ex.astype(BF16))
        inv.append(1.0 / l)
        lses.append(m + jnp.log(l))
    return jnp.concatenate(ps, axis=-1), inv, lses


def _softmax_pair_bwd(sc, dp, lse, delta, w):
    ps, dss = [], []
    for e in range(2):
        p = jnp.exp(sc[:, e * w:(e + 1) * w] - lse[e])
        ps.append(p)
        dss.append(p * (dp[:, e * w:(e + 1) * w] - delta[e]))
    return jnp.concatenate(ps, axis=-1), jnp.concatenate(dss, axis=-1)


def _pair_rowsums(x, lo):
    zero = jnp.zeros_like(x)
    return (jnp.sum(jnp.where(lo, x, zero), axis=-1, keepdims=True),
            jnp.sum(jnp.where(lo, zero, x), axis=-1, keepdims=True))


def _a_kv_specs(left, pw):
    specs = []
    for which in (1, 2):
        for t in range(A_KBLOCKS):
            specs.append(pl.BlockSpec(
                (None, KB, pw), functools.partial(
                    lambda p, b, which, t: (which, jnp.maximum(b - left + t, 0), p), which=which, t=t)))
    return specs


def _attn_a_fwd(zqkv, g, diag, hosted=None):
    s = g.shape[0]
    nb = s // TQ
    left = A_KBLOCKS - 1
    pairs = A_PAIRS_FWD
    pw = pairs * LANES
    wide = A_WIN + TQ

    def body(q_ref, *refs):
        k_refs = refs[:A_KBLOCKS]
        v_refs = refs[A_KBLOCKS:2 * A_KBLOCKS]
        g_ref, diag_ref, o_ref, u_ref, lse_ref, bias_scr = refs[2 * A_KBLOCKS:]
        b = pl.program_id(1)

        @pl.when(b == 0)
        def _():
            for hh in range(2 * pairs):
                bias_scr[hh // 2, :, (hh % 2) * A_WIN:(hh % 2 + 1) * A_WIN] = _toeplitz_tile(
                    diag_ref[hh], A_WIN, A_LEFT_CHUNKS)

        def step(first_blocks):
            lo = _lane_lo(TQ)
            for pp in range(pairs):
                ln = slice(pp * LANES, (pp + 1) * LANES)
                kcat = _stack_pair(jnp.concatenate([r[:, ln] for r in k_refs], axis=0))
                vcat = _stack_pair(jnp.concatenate([r[:, ln] for r in v_refs], axis=0))
                sc = lax.dot_general(q_ref[:, ln] * SCALE, kcat, NT, preferred_element_type=F32) + bias_scr[pp]
                if first_blocks:
                    sc = jnp.where(_block_valid(b, left, A_WIN), sc, MASKED)
                p, inv, lses = _softmax_pair(sc, A_WIN)
                ov = jnp.dot(p, vcat, preferred_element_type=F32) * jnp.where(lo, inv[0], inv[1])
                o_ref[:, ln] = ov
                lse_ref[pp] = jnp.where(lo, lses[0], lses[1])
                sg, _ = _silu_parts(g_ref[:, ln])
                u_ref[:, ln] = (ov * sg).astype(BF16)

        @pl.when(b < left)
        def _():
            step(True)

        @pl.when(b >= left)
        def _():
            step(False)

    tile = pl.BlockSpec((TQ, pw), lambda p, b: (b, p))
    return _call(
        body, name="attn_a_fwd", grid=(HEADS // 2 // pairs, nb),
        in_specs=[pl.BlockSpec((None, TQ, pw), lambda p, b: (0, b, p))] + _a_kv_specs(left, pw) + [
            tile, pl.BlockSpec((2 * pairs, 1, wide), lambda p, b: (p, 0, 0))],
        out_specs=[tile, tile, pl.BlockSpec((pairs, TQ, LANES), lambda p, b: (p, b, 0))],
        out_shape=[jax.ShapeDtypeStruct((s, D_MODEL), F32), jax.ShapeDtypeStruct((s, D_MODEL), BF16),
                   jax.ShapeDtypeStruct((HEADS // 2, s, LANES), F32)],
        scratch_shapes=[pltpu.VMEM((pairs, TQ, 2 * A_WIN), F32)],
        sem=("parallel", "arbitrary"), hosted=hosted,
        args=(zqkv, *([zqkv] * (2 * A_KBLOCKS)), g, diag))


def _attn_a_bwd(zqkv, g, o, du, lse, diag, hosted=None):
    s = g.shape[0]
    nb = s // TQ
    left = A_KBLOCKS - 1
    pw = A_PAIRS * LANES
    wide = A_WIN + TQ

    def body(q_ref, *refs):
        k_refs = refs[:A_KBLOCKS]
        v_refs = refs[A_KBLOCKS:2 * A_KBLOCKS]
        (g_ref, o_ref, du_ref, lse_ref, diag_ref, dz_ref, ddiag_ref,
         bias_scr, dbias_acc, dk_acc, dv_acc) = refs[2 * A_KBLOCKS:]
        b = pl.program_id(1)

        @pl.when(b == 0)
        def _():
            for hh in range(2 * A_PAIRS):
                bias_scr[hh // 2, :, (hh % 2) * A_WIN:(hh % 2 + 1) * A_WIN] = _toeplitz_tile(
                    diag_ref[hh], A_WIN, A_LEFT_CHUNKS)
            dbias_acc[...] = jnp.zeros_like(dbias_acc)
            dk_acc[...] = jnp.zeros_like(dk_acc)
            dv_acc[...] = jnp.zeros_like(dv_acc)

        def step(first_blocks):
            lo = _lane_lo(TQ)
            rows = pl.ds(pl.multiple_of(b * TQ, TQ), TQ)
            sg, dsg = _silu_parts(g_ref[...])
            duv = du_ref[...]
            ov = o_ref[...]
            do = duv * sg
            dz_ref[3, rows, :] = (duv * ov * dsg).astype(BF16)
            do_o = do * ov
            do_bf = do.astype(BF16)
            for pp in range(A_PAIRS):
                ln = slice(pp * LANES, (pp + 1) * LANES)
                q = q_ref[:, ln] * SCALE
                kcat = _stack_pair(jnp.concatenate([r[:, ln] for r in k_refs], axis=0))
                vcat = _stack_pair(jnp.concatenate([r[:, ln] for r in v_refs], axis=0))
                sc = lax.dot_general(q, kcat, NT, preferred_element_type=F32) + bias_scr[pp]
                if first_blocks:
                    sc = jnp.where(_block_valid(b, left, A_WIN), sc, MASKED)
                lse_t = lse_ref[pp]
                dp = lax.dot_general(do_bf[:, ln], vcat, NT, preferred_element_type=F32)
                p, ds = _softmax_pair_bwd(sc, dp, (lse_t[:, 0:1], lse_t[:, HEAD_DIM:HEAD_DIM + 1]),
                                          _pair_rowsums(do_o[:, ln], lo), A_WIN)
                dbias_acc[pp] += ds
                dsb = ds.astype(BF16)
                dz_ref[0, rows, ln] = (jnp.dot(dsb, kcat, preferred_element_type=F32) * SCALE).astype(BF16)
                dk = _unstack_pair(lax.dot_general(dsb, q, TN, preferred_element_type=F32), A_WIN)
                dv = _unstack_pair(
                    lax.dot_general(p.astype(BF16), do_bf[:, ln], TN, preferred_element_type=F32), A_WIN)
                for t in range(A_KBLOCKS):
                    krows = pl.ds(pl.multiple_of(jnp.maximum(b - left + t, 0) * KB, KB), KB)
                    dk_acc[krows, ln] += dk[t * KB:(t + 1) * KB, :]
                    dv_acc[krows, ln] += dv[t * KB:(t + 1) * KB, :]

        @pl.when(b < left)
        def _():
            step(True)

        @pl.when(b >= left)
        def _():
            step(False)

        @pl.when(b == nb - 1)
        def _():
            dz_ref[1] = dk_acc[...].astype(BF16)
            dz_ref[2] = dv_acc[...].astype(BF16)
            for hh in range(2 * A_PAIRS):
                ddiag_ref[hh] = _toeplitz_sum(
                    dbias_acc[hh // 2, :, (hh % 2) * A_WIN:(hh % 2 + 1) * A_WIN], A_WIN)

    tile = pl.BlockSpec((TQ, pw), lambda p, b: (b, p))
    diag_spec = pl.BlockSpec((2 * A_PAIRS, 1, wide), lambda p, b: (p, 0, 0))
    return _call(
        body, name="attn_a_bwd", grid=(HEADS // 2 // A_PAIRS, nb),
        in_specs=[pl.BlockSpec((None, TQ, pw), lambda p, b: (0, b, p))] + _a_kv_specs(left, pw) + [
            tile, tile, tile, pl.BlockSpec((A_PAIRS, TQ, LANES), lambda p, b: (p, b, 0)), diag_spec],
        out_specs=[pl.BlockSpec((4, s, pw), lambda p, b: (0, 0, p)), diag_spec],
        out_shape=[jax.ShapeDtypeStruct((4, s, D_MODEL), BF16),
                   jax.ShapeDtypeStruct((HEADS, 1, wide), F32)],
        scratch_shapes=[pltpu.VMEM((A_PAIRS, TQ, 2 * A_WIN), F32), pltpu.VMEM((A_PAIRS, TQ, 2 * A_WIN), F32),
                        pltpu.VMEM((s, pw), F32), pltpu.VMEM((s, pw), F32)],
        sem=("parallel", "arbitrary"), hosted=hosted,
        args=(zqkv, *([zqkv] * (2 * A_KBLOCKS)), g, o, du, lse, diag))


B_STACK = B_GROUP // 2
B_KVX = 4 * LANES
B_ROWS = B_STACK * TQ
B_WIDE = B_WIN + TQ


def _b_head_place(h):
    return h // B_GROUP, (h % B_GROUP) // 2, h % 2


def _toeplitz_tile_t(base_row, width, left_chunks):
    wide = width + TQ
    rolled = pltpu.roll(jnp.broadcast_to(base_row, (width, wide)), 0, 1, stride=1, stride_axis=0)
    j = lax.broadcasted_iota(jnp.int32, (width, TQ), 0) // CHUNK
    i = lax.broadcasted_iota(jnp.int32, (width, TQ), 1) // CHUNK
    dc = i + left_chunks - j
    return jnp.where((dc >= 0) & (dc <= left_chunks), rolled[:, :TQ], MASKED)


def _toeplitz_sum_t(tile_t, width):
    flip = (lax.broadcasted_iota(jnp.int32, (width, width), 0) + lax.broadcasted_iota(jnp.int32, (width, width), 1)
            == width - 1).astype(F32)
    reversed_rows = jnp.dot(flip, tile_t, precision=lax.Precision.HIGHEST, preferred_element_type=F32)
    padded = jnp.concatenate([reversed_rows, jnp.zeros((width, width), F32)], axis=1)
    rolled = pltpu.roll(padded, 0, 1, stride=1, stride_axis=0)
    return jnp.sum(rolled, axis=0, keepdims=True)


def _b_build_bias(base_ref, bias_scr):
    for h in range(HEADS):
        gi, pr, e = _b_head_place(h)
        bias_scr[gi, e * B_WIN:(e + 1) * B_WIN, pr * TQ:(pr + 1) * TQ] = _toeplitz_tile_t(
            base_ref[h], B_WIN, B_LEFT_CHUNKS)


def _b_stack(x, gi):
    return jnp.concatenate(
        [x[:, (B_STACK * gi + pr) * LANES:(B_STACK * gi + pr + 1) * LANES] for pr in range(B_STACK)], axis=0)


def _b_sink_rows(sink_ref, gi):
    block = lax.broadcasted_iota(jnp.int32, (1, B_ROWS), 1) // TQ
    rows = []
    for e in range(2):
        row = jnp.zeros((1, B_ROWS), F32)
        for pr in range(B_STACK):
            h = B_GROUP * gi + 2 * pr + e
            row = jnp.where(block == pr, sink_ref[0:1, h:h + 1], row)
        rows.append(row)
    return rows


def _b_scores_t(q_ref, kvv, bias_scr, gi, b, left, first_blocks):
    kcat = _stack_pair(kvv[:, gi * LANES:(gi + 1) * LANES])
    vcat = _stack_pair(kvv[:, (B_KV_HEADS + gi) * LANES:(B_KV_HEADS + gi + 1) * LANES])
    qs = _b_stack(q_ref, gi) * SCALE
    sc = lax.dot_general(kcat, qs, NT, preferred_element_type=F32) + bias_scr[gi]
    if first_blocks:
        row = lax.broadcasted_iota(jnp.int32, (2 * B_WIN, 1), 0)
        row = jnp.where(row >= B_WIN, row - B_WIN, row)
        sc = jnp.where((row // KB + (b - left)) >= 0, sc, MASKED)
    return kcat, vcat, qs, sc


def _attn_b_fwd(qb, kvx, gate, base, sinks):
    s = qb.shape[0]
    nb = s // TQ
    left = B_KBLOCKS - 1

    def body(q_ref, *refs):
        kv_refs = refs[:B_KBLOCKS]
        g_ref, base_ref, sink_ref, o_ref, u_ref, lse_ref, bias_scr = refs[B_KBLOCKS:]
        b = pl.program_id(0)

        @pl.when(b == 0)
        def _():
            _b_build_bias(base_ref, bias_scr)

        def step(first_blocks):
            kvv = jnp.concatenate([r[...] for r in kv_refs], axis=0)
            upper = lax.broadcasted_iota(jnp.int32, (LANES, B_ROWS), 0) < HEAD_DIM
            lse_rows = []
            for gi in range(B_KV_HEADS):
                kcat, vcat, qs, sc = _b_scores_t(q_ref, kvv, bias_scr, gi, b, left, first_blocks)
                sink = _b_sink_rows(sink_ref, gi)
                ps, inv = [], []
                for e in range(2):
                    sh = sc[e * B_WIN:(e + 1) * B_WIN]
                    m = jnp.maximum(jnp.max(sh, axis=0, keepdims=True), sink[e])
                    ex = jnp.exp(sh - m)
                    l = jnp.sum(ex, axis=0, keepdims=True) + jnp.exp(sink[e] - m)
                    ps.append(ex.astype(BF16))
                    inv.append(1.0 / l)
                    lse_rows.append(m + jnp.log(l))
                pt = jnp.concatenate(ps, axis=0)
                ot = lax.dot_general(vcat, pt, TN, preferred_element_type=F32) * jnp.where(upper, inv[0], inv[1])
                ov = ot.T
                for pr in range(B_STACK):
                    pair = B_STACK * gi + pr
                    o_ref[:, pair * LANES:(pair + 1) * LANES] = ov[pr * TQ:(pr + 1) * TQ]
            lse_ref[0] = jnp.concatenate(lse_rows + [jnp.zeros((8 - len(lse_rows), B_ROWS), F32)], axis=0)
            sg, _ = _silu_parts(g_ref[...])
            u_ref[...] = (o_ref[...] * sg).astype(BF16)

        @pl.when(b < left)
        def _():
            step(True)

        @pl.when(b >= left)
        def _():
            step(False)

    kv_specs = [pl.BlockSpec((KB, B_KVX), functools.partial(
        lambda b, t: (jnp.maximum(b - left + t, 0), 0), t=t)) for t in range(B_KBLOCKS)]
    row = pl.BlockSpec((TQ, D_MODEL), lambda b: (b, 0))
    return pl.pallas_call(
        body, name="attn_b_fwd", grid=(nb,),
        in_specs=[row] + kv_specs + [row, pl.BlockSpec((HEADS, 1, B_WIDE), lambda b: (0, 0, 0)),
                                     pl.BlockSpec((1, HEADS), lambda b: (0, 0))],
        out_specs=[row, row, pl.BlockSpec((1, 8, B_ROWS), lambda b: (b, 0, 0))],
        out_shape=[jax.ShapeDtypeStruct((s, D_MODEL), F32), jax.ShapeDtypeStruct((s, D_MODEL), BF16),
                   jax.ShapeDtypeStruct((nb, 8, B_ROWS), F32)],
        scratch_shapes=[pltpu.VMEM((B_KV_HEADS, 2 * B_WIN, B_ROWS), F32)],
        compiler_params=_params(("arbitrary",)),
    )(qb, *([kvx] * B_KBLOCKS), gate, base, sinks)


def _attn_b_bwd(qb, kvx, gate, o, du, lse, base, sinks):
    s = qb.shape[0]
    nb = s // TQ
    left = B_KBLOCKS - 1
    half = D_MODEL // 2

    def body(q_ref, *refs):
        kv_refs = refs[:B_KBLOCKS]
        (g_ref, o_ref, du_ref, lse_ref, base_ref, sink_ref, dz_ref, dkv_ref, dsum_ref, dsink_ref,
         bias_scr, dbias_acc, dkv_acc, dsink_acc) = refs[B_KBLOCKS:]
        b = pl.program_id(0)

        @pl.when(b == 0)
        def _():
            _b_build_bias(base_ref, bias_scr)
            dbias_acc[...] = jnp.zeros_like(dbias_acc)
            dkv_acc[...] = jnp.zeros_like(dkv_acc)
            dsink_acc[...] = jnp.zeros_like(dsink_acc)

        def step(first_blocks):
            kvv = jnp.concatenate([r[...] for r in kv_refs], axis=0)
            sg, dsg = _silu_parts(g_ref[...])
            duv = du_ref[...]
            ov = o_ref[...]
            do = duv * sg
            dgate = (duv * ov * dsg).astype(BF16)
            dz_ref[2] = dgate[:, :half]
            dz_ref[3] = dgate[:, half:]
            do_o = do * ov
            do_bf = do.astype(BF16)
            lane = lax.broadcasted_iota(jnp.int32, (8, LANES), 1)
            sub = lax.broadcasted_iota(jnp.int32, (8, LANES), 0)
            halves = (((sub == 0) & (lane < HEAD_DIM)) | ((sub == 1) & (lane >= HEAD_DIM))).astype(F32)
            lse_all = lse_ref[0]
            dsink_rows = []
            for gi in range(B_KV_HEADS):
                kcat, vcat, qs, sc = _b_scores_t(q_ref, kvv, bias_scr, gi, b, left, first_blocks)
                dos = _b_stack(do_bf, gi)
                delta = lax.dot_general(halves, _b_stack(do_o, gi), NT, precision=lax.Precision.HIGHEST,
                                        preferred_element_type=F32)
                sink = _b_sink_rows(sink_ref, gi)
                dp = lax.dot_general(vcat, dos, NT, preferred_element_type=F32)
                ps, dss = [], []
                for e in range(2):
                    lse_e = lse_all[2 * gi + e:2 * gi + e + 1]
                    delta_e = delta[e:e + 1]
                    p = jnp.exp(sc[e * B_WIN:(e + 1) * B_WIN] - lse_e)
                    ps.append(p.astype(BF16))
                    dss.append(p * (dp[e * B_WIN:(e + 1) * B_WIN] - delta_e))
                    dsink_rows.append(-jnp.exp(sink[e] - lse_e) * delta_e)
                ds = jnp.concatenate(dss, axis=0)
                dbias_acc[gi] += ds
                dsb = ds.astype(BF16)
                dq = (lax.dot_general(dsb, kcat, TN, preferred_element_type=F32) * SCALE).astype(BF16)
                for pr in range(B_STACK):
                    dz_ref[gi, :, pr * LANES:(pr + 1) * LANES] = dq[pr * TQ:(pr + 1) * TQ]
                dk = _unstack_pair(jnp.dot(dsb, qs, preferred_element_type=F32), B_WIN)
                dv = _unstack_pair(jnp.dot(jnp.concatenate(ps, axis=0), dos, preferred_element_type=F32), B_WIN)
                for t in range(B_KBLOCKS):
                    krows = pl.ds(pl.multiple_of(jnp.maximum(b - left + t, 0) * KB, KB), KB)
                    dkv_acc[krows, gi * LANES:(gi + 1) * LANES] += dk[t * KB:(t + 1) * KB, :]
                    dkv_acc[krows, (B_KV_HEADS + gi) * LANES:(B_KV_HEADS + gi + 1) * LANES] += dv[t * KB:(t + 1) * KB, :]
            dsink_acc[...] += jnp.concatenate(
                dsink_rows + [jnp.zeros((8 - len(dsink_rows), B_ROWS), F32)], axis=0)

        @pl.when(b < left)
        def _():
            step(True)

        @pl.when(b >= left)
        def _():
            step(False)

        @pl.when(b == nb - 1)
        def _():
            lo_s = _lane_lo(s)
            for which in range(2):
                folded = []
                for gi in range(B_KV_HEADS):
                    part = dkv_acc[:, (which * B_KV_HEADS + gi) * LANES:(which * B_KV_HEADS + gi + 1) * LANES]
                    folded.append(part + pltpu.roll(part, HEAD_DIM, 1))
                dkv_ref[:, which * LANES:(which + 1) * LANES] = jnp.where(lo_s, folded[0], folded[1]).astype(BF16)
            lane8 = lax.broadcasted_iota(jnp.int32, dsink_ref.shape, 1)
            tot = jnp.zeros(dsink_ref.shape, F32)
            for h in range(HEADS):
                gi, pr, e = _b_head_place(h)
                dsum_ref[h] = _toeplitz_sum_t(
                    dbias_acc[gi, e * B_WIN:(e + 1) * B_WIN, pr * TQ:(pr + 1) * TQ], B_WIN)
                per_query = dsink_acc[2 * gi + e:2 * gi + e + 1, pr * TQ:(pr + 1) * TQ]
                tot = jnp.where(lane8 == h, jnp.sum(per_query, axis=1, keepdims=True), tot)
            dsink_ref[...] = tot

    kv_specs = [pl.BlockSpec((KB, B_KVX), functools.partial(
        lambda b, t: (jnp.maximum(b - left + t, 0), 0), t=t)) for t in range(B_KBLOCKS)]
    row = pl.BlockSpec((TQ, D_MODEL), lambda b: (b, 0))
    base_spec = pl.BlockSpec((HEADS, 1, B_WIDE), lambda b: (0, 0, 0))
    return pl.pallas_call(
        body, name="attn_b_bwd", grid=(nb,),
        in_specs=[row] + kv_specs + [row, row, row, pl.BlockSpec((1, 8, B_ROWS), lambda b: (b, 0, 0)),
                                     base_spec, pl.BlockSpec((1, HEADS), lambda b: (0, 0))],
        out_specs=[pl.BlockSpec((4, TQ, half), lambda b: (0, b, 0)),
                   pl.BlockSpec((s, 2 * LANES), lambda b: (0, 0)), base_spec,
                   pl.BlockSpec((8, LANES), lambda b: (0, 0))],
        out_shape=[jax.ShapeDtypeStruct((4, s, half), BF16), jax.ShapeDtypeStruct((s, 2 * LANES), BF16),
                   jax.ShapeDtypeStruct((HEADS, 1, B_WIDE), F32), jax.ShapeDtypeStruct((8, LANES), F32)],
        scratch_shapes=[pltpu.VMEM((B_KV_HEADS, 2 * B_WIN, B_ROWS), F32),
                        pltpu.VMEM((B_KV_HEADS, 2 * B_WIN, B_ROWS), F32),
                        pltpu.VMEM((s, B_KVX), F32), pltpu.VMEM((8, B_ROWS), F32)],
        compiler_params=_params(("arbitrary",)),
    )(qb, *([kvx] * B_KBLOCKS), gate, o, du, lse, base, sinks)


def _t5_bucket(rel):
    nb = T5_BUCKETS // 2
    max_exact = nb // 2
    ret = jnp.where(rel > 0, nb, 0)
    n = jnp.abs(rel)
    nf = jnp.maximum(n, 1).astype(jnp.float32)
    large = max_exact + (jnp.log(nf / max_exact) / math.log(T5_MAX_DIST / max_exact)
                         * (nb - max_exact)).astype(jnp.int32)
    large = jnp.minimum(large, nb - 1)
    return ret + jnp.where(n < max_exact, n, large)


def _a_offset_onehot():
    c = np.arange(A_WIN + TQ)
    dist = A_LEFT_CHUNKS * CHUNK + TQ - 1 - c
    idx = np.clip(dist, -A_REL_CLIP, A_REL_CLIP) + A_REL_CLIP
    onehot = np.zeros((A_WIN + TQ, 2 * A_REL_CLIP + 1), np.float32)
    onehot[c, idx] = 1.0
    return jnp.asarray(onehot)


def _b_offset_onehot():
    c = jnp.arange(B_WIN + TQ, dtype=jnp.int32)
    rel = c - (TQ - 1) - B_LEFT_CHUNKS * CHUNK
    return (_t5_bucket(rel)[:, None] == jnp.arange(T5_BUCKETS)[None, :]).astype(F32)


def _diag_rows(onehot, table):
    rows = jnp.dot(onehot, table.astype(F32), precision=lax.Precision.HIGHEST)
    return rows.T.reshape(HEADS, 1, onehot.shape[0])


def _diag_rows_grad(onehot, ddiag):
    return jnp.dot(ddiag.reshape(HEADS, onehot.shape[0]), onehot, precision=lax.Precision.HIGHEST).T


def _position():
    x, y, c = lax.axis_index("x"), lax.axis_index("y"), lax.axis_index("c")
    chips = [(1 - x, y), (x, 1 - y), (1 - x, 1 - y)]
    return x, y, c, chips


ANY = pl.BlockSpec(memory_space=pl.ANY)


def _allgather_hosted(shards, split):
    n = len(shards)

    def part(ref, t, half):
        if not split[t]:
            return ref
        rows = shards[t].shape[0] // 2
        return ref.at[pl.ds(half * rows, rows)]

    def copies(kind, ins, outs, sems):
        send_sems, recv_sems, pass_send, pass_recv, local_sems = sems
        x, y, c, chips = _position()
        mine = 2 * x + y
        if kind == "local":
            return [pltpu.make_async_copy(ins[t], outs[t].at[mine], local_sems.at[t]) for t in range(n)]
        made = []
        for t in range(n):
            for j, chip in enumerate(chips):
                theirs = 2 * chip[0] + chip[1]
                far = dict(send_sem=send_sems.at[3 * t + j], recv_sem=recv_sems.at[3 * t + j],
                           device_id=(chip[0], chip[1], c), device_id_type=MESH)
                near = dict(send_sem=pass_send.at[3 * t + j], recv_sem=pass_recv.at[3 * t + j],
                            device_id=(x, y, 1 - c), device_id_type=MESH)
                here = part(outs[t].at[theirs], t, c)
                if kind == "send":
                    made.append(pltpu.make_async_remote_copy(
                        src_ref=part(ins[t], t, c), dst_ref=part(outs[t].at[mine], t, c), **far))
                elif kind == "landed":
                    made.append(pltpu.make_async_remote_copy(src_ref=here, dst_ref=here, **far))
                elif not split[t]:
                    made.append(None)
                elif kind == "pass":
                    made.append(pltpu.make_async_remote_copy(src_ref=here, dst_ref=here, **near))
                else:
                    other = part(outs[t].at[theirs], t, 1 - c)
                    made.append(pltpu.make_async_remote_copy(src_ref=other, dst_ref=other, **near))
        return made

    def first(ins, outs, sems):
        for cp in copies("local", ins, outs, sems) + copies("send", ins, outs, sems):
            cp.start()

    def middle(ins, outs, sems):
        for got, cp in zip(copies("landed", ins, outs, sems), copies("pass", ins, outs, sems)):
            got.wait_recv()
            if cp is not None:
                cp.start()

    def last(ins, outs, sems):
        for cp in copies("passed", ins, outs, sems):
            if cp is not None:
                cp.wait_recv()
        for cp in copies("send", ins, outs, sems) + copies("pass", ins, outs, sems):
            if cp is not None:
                cp.wait_send()
        for cp in copies("local", ins, outs, sems):
            cp.wait()

    return _Hosted(shards, [jax.ShapeDtypeStruct((4,) + w.shape, w.dtype) for w in shards],
                   [pltpu.SemaphoreType.DMA((3 * n,))] * 4 + [pltpu.SemaphoreType.DMA((n,))],
                   first, middle, last)


def _scatter_hosted(grads):
    n = len(grads)

    def copies(ins, outs, sems):
        send_sems, recv_sems = sems
        x, y, c, chips = _position()
        return [pltpu.make_async_remote_copy(
            src_ref=ins[t].at[2 * chip[0] + chip[1]], dst_ref=outs[t].at[j],
            send_sem=send_sems.at[3 * t + j], recv_sem=recv_sems.at[3 * t + j],
            device_id=(chip[0], chip[1], c), device_id_type=MESH)
            for t in range(n) for j, chip in enumerate(chips)]

    def first(ins, outs, sems):
        for cp in copies(ins, outs, sems):
            cp.start()

    def last(ins, outs, sems):
        for cp in copies(ins, outs, sems):
            cp.wait()

    return _Hosted(grads, [jax.ShapeDtypeStruct((3,) + g.shape[1:], g.dtype) for g in grads],
                   [pltpu.SemaphoreType.DMA((3 * n,))] * 2, first, None, last)


def _run_alone(name, hosted):
    n_in = len(hosted.inputs)
    n_out = len(hosted.out_shapes)

    def body(*refs):
        ins, outs, sems = refs[:n_in], refs[n_in:n_in + n_out], refs[n_in + n_out:]
        hosted.first(ins, outs, sems)
        if hosted.middle is not None:
            hosted.middle(ins, outs, sems)
        hosted.last(ins, outs, sems)

    return pl.pallas_call(
        body, name=name, in_specs=[ANY] * n_in, out_specs=[ANY] * n_out, out_shape=hosted.out_shapes,
        scratch_shapes=hosted.sems)(*hosted.inputs)


def _swap_with_sibling(blocks):
    n = len(blocks)

    def body(*refs):
        ins, outs = refs[:n], refs[n:2 * n]
        send_sems, recv_sems = refs[2 * n:]
        x, y, c, _ = _position()
        sends = [pltpu.make_async_remote_copy(
            src_ref=ins[t], dst_ref=outs[t], send_sem=send_sems.at[t], recv_sem=recv_sems.at[t],
            device_id=(x, y, 1 - c), device_id_type=MESH) for t in range(n)]
        for cp in sends:
            cp.start()
        for cp in sends:
            cp.wait()

    return pl.pallas_call(
        body, name="swap_with_sibling",
        in_specs=[ANY] * n, out_specs=[ANY] * n,
        out_shape=[jax.ShapeDtypeStruct(b.shape, b.dtype) for b in blocks],
        scratch_shapes=[pltpu.SemaphoreType.DMA((n,))] * 2,
    )(*blocks)


def _allreduce_small(block):
    rows = block.shape[0]

    def body(in_ref, sum_ref, all_ref, send_sems, recv_sems):
        x, y, c, _ = _position()
        me = 4 * x + 2 * y + c
        all_ref[me] = in_ref[...]
        sends = []
        for k in range(1, 8):
            peer = (x ^ (k >> 2), y ^ ((k >> 1) & 1), c ^ (k & 1))
            sends.append(pltpu.make_async_remote_copy(
                src_ref=in_ref, dst_ref=all_ref.at[me], send_sem=send_sems.at[k - 1],
                recv_sem=recv_sems.at[k - 1], device_id=peer, device_id_type=MESH))
        for cp in sends:
            cp.start()
        for k in range(1, 8):
            theirs = me ^ k
            pltpu.make_async_remote_copy(
                src_ref=in_ref, dst_ref=all_ref.at[theirs], send_sem=send_sems.at[k - 1],
                recv_sem=recv_sems.at[k - 1], device_id=(x, y, c), device_id_type=MESH).wait_recv()
        for cp in sends:
            cp.wait_send()
        acc = all_ref[0]
        for d in range(1, 8):
            acc = acc + all_ref[d]
        sum_ref[...] = acc

    vmem = pl.BlockSpec(memory_space=pltpu.VMEM)
    return pl.pallas_call(
        body, name="allreduce_small",
        in_specs=[vmem], out_specs=[vmem, vmem],
        out_shape=[jax.ShapeDtypeStruct((rows, LANES), F32), jax.ShapeDtypeStruct((8, rows, LANES), F32)],
        scratch_shapes=[pltpu.SemaphoreType.DMA((7,))] * 2,
    )(block)[0]


def _adamw_math(w, g, m, v):
    m = ADAM_B1 * m + (1.0 - ADAM_B1) * g
    v = ADAM_B2 * v + (1.0 - ADAM_B2) * (g * g)
    m_hat = m / (1.0 - ADAM_B1 ** ADAM_STEP)
    v_hat = v / (1.0 - ADAM_B2 ** ADAM_STEP)
    delta = -ADAM_LR * (m_hat / (jnp.sqrt(v_hat) + ADAM_EPS) + ADAM_WD * w)
    return delta, m, v


def _row_tile(rows):
    return min(rows, 256)


def _sum_partials(name, own, recv):
    rows, cols = own.shape
    tr = _row_tile(rows)

    def body(own_ref, recv_ref, o_ref):
        acc = own_ref[...]
        for j in range(3):
            acc = acc + recv_ref[j].astype(F32)
        o_ref[...] = acc

    return pl.pallas_call(
        body, name=name, grid=(rows // tr,),
        in_specs=[pl.BlockSpec((tr, cols), lambda i: (i, 0)), pl.BlockSpec((3, tr, cols), lambda i: (0, i, 0))],
        out_specs=pl.BlockSpec((tr, cols), lambda i: (i, 0)),
        out_shape=jax.ShapeDtypeStruct((rows, cols), F32),
        compiler_params=_params(("parallel",)),
    )(own, recv)


def _adamw(name, w, m, v, g_parts):
    rows, cols = w.shape
    tr = _row_tile(rows)
    n = len(g_parts)

    def body(w_ref, m_ref, v_ref, *refs):
        g_refs = refs[:n]
        go_ref, d_ref, mo_ref, vo_ref = refs[n:]
        g = g_refs[0][...]
        for r in g_refs[1:]:
            g = g + r[...]
        delta, mn, vn = _adamw_math(w_ref[...], g, m_ref[...], v_ref[...])
        go_ref[...] = g
        d_ref[...] = delta
        mo_ref[...] = mn
        vo_ref[...] = vn

    spec = pl.BlockSpec((tr, cols), lambda i: (i, 0))
    return pl.pallas_call(
        body, name=name, grid=(rows // tr,),
        in_specs=[spec] * (3 + n), out_specs=[spec] * 4,
        out_shape=[jax.ShapeDtypeStruct((rows, cols), F32)] * 4,
        compiler_params=_params(("parallel",)),
    )(w, m, v, *g_parts)


def _local_step(x, target, ga, wa_in, rel_bias, later_shards, gk, t5, gb, sinks, gf):
    s, d = x.shape
    tm = min(TM_DENSE, s)
    nt = s // tm
    half = d // 2
    row = pl.BlockSpec((tm, d), lambda i: (i, 0))
    whole = lambda shape: pl.BlockSpec(shape, lambda *_: (0,) * len(shape))

    n1, = _norm_fwd("norm_a", x, ga)
    zqkv = _matmul("proj_a_qkv", n1, wa_in, dims=NN, grid=(3, nt),
                   a_spec=pl.BlockSpec((tm, d), lambda j, i: (i, 0)),
                   b_spec=pl.BlockSpec((None, d, d), lambda j, i: (j, 0, 0)),
                   o_spec=pl.BlockSpec((None, tm, d), lambda j, i: (j, i, 0)),
                   out_shape=(3, s, d), out_dtype=BF16)
    gate_a = _matmul("proj_a_gate", n1, wa_in, dims=NN, grid=(nt,),
                     a_spec=row, b_spec=pl.BlockSpec((None, d, d), lambda i: (3, 0, 0)), o_spec=row,
                     out_shape=(s, d), out_dtype=F32)
    onehot_a = _a_offset_onehot()
    diag_a = _diag_rows(onehot_a, rel_bias)
    (o_a, u_a, lse_a), gathered = _attn_a_fwd(
        zqkv, gate_a, diag_a, hosted=_allgather_hosted(later_shards, [True] * len(later_shards)))
    wa_out, wkv, wb_in, wb_out = gathered
    wa_out = wa_out.reshape(d, d)
    wkv = wkv.reshape(d, -1)
    wb_out = wb_out.reshape(d, d)
    h1 = _matmul("out_a", u_a, wa_out, dims=NN, grid=(nt,), a_spec=row, b_spec=whole((d, d)), o_spec=row,
                 out_shape=(s, d), out_dtype=F32, resid=x, resid_spec=row)

    nk, n2 = _norm_fwd("norm_kv_b", h1, jnp.concatenate([gk, gb], axis=0))
    kvw = wkv.shape[1]
    wkv_x = jnp.concatenate([wkv[:, (i // 2) * HEAD_DIM:(i // 2 + 1) * HEAD_DIM] for i in range(8)], axis=1)
    kvx = _matmul("proj_kv", nk, wkv_x, dims=NN, grid=(nt,), a_spec=row, b_spec=whole((d, B_KVX)),
                  o_spec=pl.BlockSpec((tm, B_KVX), lambda i: (i, 0)), out_shape=(s, B_KVX), out_dtype=BF16)
    qb = _matmul("proj_b_q", n2, wb_in, dims=NN, grid=(2, nt),
                 a_spec=pl.BlockSpec((tm, d), lambda j, i: (i, 0)),
                 b_spec=pl.BlockSpec((None, d, half), lambda j, i: (j, 0, 0)),
                 o_spec=pl.BlockSpec((tm, half), lambda j, i: (i, j)), out_shape=(s, d), out_dtype=BF16)
    gate_b = _matmul("proj_b_gate", n2, wb_in, dims=NN, grid=(2, nt),
                     a_spec=pl.BlockSpec((tm, d), lambda j, i: (i, 0)),
                     b_spec=pl.BlockSpec((None, d, half), lambda j, i: (2 + j, 0, 0)),
                     o_spec=pl.BlockSpec((tm, half), lambda j, i: (i, j)), out_shape=(s, d), out_dtype=F32)
    onehot_b = _b_offset_onehot()
    base_b = jnp.roll(_diag_rows(onehot_b, t5)[..., ::-1], TQ, axis=-1)
    o_b, u_b, lse_b = _attn_b_fwd(qb, kvx, gate_b, base_b, sinks)
    h2 = _matmul("out_b", u_b, wb_out, dims=NN, grid=(nt,), a_spec=row, b_spec=whole((d, d)), o_spec=row,
                 out_shape=(s, d), out_dtype=F32, resid=h1, resid_spec=row)

    dh2, loss, d_gf = _loss_head(h2, target, gf)

    du_b = _matmul("dout_b", dh2, wb_out, dims=NT, grid=(nt,), a_spec=row, b_spec=whole((d, d)), o_spec=row,
                   out_shape=(s, d), out_dtype=F32)
    d_wb_out = _matmul("dw_out_b", u_b, dh2, dims=TN, grid=(2,),
                       a_spec=whole((s, d)), b_spec=pl.BlockSpec((s, half), lambda j: (0, j)),
                       o_spec=pl.BlockSpec((d, half), lambda j: (0, j)),
                       out_shape=(d, d), out_dtype=F32, also_bf16=True)
    dz_b, dkv, dsum_b, dsinks = _attn_b_bwd(qb, kvx, gate_b, o_b, du_b, lse_b, base_b, sinks)
    ddiag_b = jnp.roll(dsum_b[..., ::-1], -1, axis=-1)
    dn2 = _matmul("dproj_b", dz_b, wb_in, dims=NT, grid=(nt,), parts=4,
                  a_spec=pl.BlockSpec((4, tm, half), lambda i: (0, i, 0)), b_spec=whole((4, d, half)),
                  o_spec=row, out_shape=(s, d), out_dtype=F32)
    d_wb_in = _matmul("dw_in_b", n2, dz_b, dims=TN, grid=(4,),
                      a_spec=whole((s, d)), b_spec=pl.BlockSpec((None, s, half), lambda j: (j, 0, 0)),
                      o_spec=pl.BlockSpec((None, d, half), lambda j: (j, 0, 0)),
                      out_shape=(4, d, half), out_dtype=F32, also_bf16=True)
    dnk = _matmul("dproj_kv", dkv, wkv, dims=NT, grid=(nt,),
                  a_spec=pl.BlockSpec((tm, kvw), lambda i: (i, 0)), b_spec=whole((d, kvw)), o_spec=row,
                  out_shape=(s, d), out_dtype=F32)
    d_wkv = _matmul("dw_kv", nk, dkv, dims=TN, grid=(1,),
                    a_spec=whole((s, d)), b_spec=whole((s, kvw)), o_spec=whole((d, kvw)),
                    out_shape=(d, kvw), out_dtype=F32, also_bf16=True)
    dh1, d_gkb = _norm_bwd("dnorm_kv_b", h1, dh2, [dnk, dn2], jnp.concatenate([gk, gb], axis=0))

    du_a = _matmul("dout_a", dh1, wa_out, dims=NT, grid=(nt,), a_spec=row, b_spec=whole((d, d)), o_spec=row,
                   out_shape=(s, d), out_dtype=F32)
    d_wa_out = _matmul("dw_out_a", u_a, dh1, dims=TN, grid=(2,),
                       a_spec=whole((s, d)), b_spec=pl.BlockSpec((s, half), lambda j: (0, j)),
                       o_spec=pl.BlockSpec((d, half), lambda j: (0, j)),
                       out_shape=(d, d), out_dtype=F32, also_bf16=True)
    early = dict(a_w_out=[g.reshape(4, d // 4, d) for g in d_wa_out],
                 kv_w=[g.reshape(4, d // 4, kvw) for g in d_wkv], b_w_in=list(d_wb_in),
                 b_w_out=[g.reshape(4, d // 4, d) for g in d_wb_out])
    (dz_a, ddiag_a), early_recv = _attn_a_bwd(
        zqkv, gate_a, o_a, du_a, lse_a, diag_a, hosted=_scatter_hosted([early[n][1] for n in early]))
    d_wa_in = _matmul("dw_in_a", n1, dz_a, dims=TN, grid=(4, 2),
                      a_spec=whole((s, d)), b_spec=pl.BlockSpec((None, s, half), lambda j, h: (j, 0, h)),
                      o_spec=pl.BlockSpec((None, d, half), lambda j, h: (j, 0, h)),
                      out_shape=(4, d, d), out_dtype=F32, also_bf16=True)
    tp = min(TM_PARTS, s)
    dn1, late_recv = _matmul("dproj_a", dz_a, wa_in, dims=NT, grid=(s // tp,), parts=4,
                             a_spec=pl.BlockSpec((4, tp, d), lambda i: (0, i, 0)), b_spec=whole((4, d, d)),
                             o_spec=pl.BlockSpec((tp, d), lambda i: (i, 0)),
                             out_shape=(s, d), out_dtype=F32, hosted=_scatter_hosted([d_wa_in[1]]))
    grad_x, d_ga = _norm_bwd("dnorm_a", x, dh1, [dn1], ga)

    small = dict(
        a_norm=d_ga, a_rel_bias=_diag_rows_grad(onehot_a, ddiag_a), kv_norm=d_gkb[0:1],
        t5_bias=_diag_rows_grad(onehot_b, ddiag_b), b_norm=d_gkb[1:2], b_sinks=dsinks[0:1, :HEADS],
        final_norm=d_gf)
    own = dict(a_w_in=d_wa_in[0], **{n: early[n][0] for n in early})
    received = dict(a_w_in=late_recv[0], **dict(zip(early, early_recv)))
    return loss, grad_x, small, own, received


SMALL = ("a_norm", "a_rel_bias", "kv_norm", "t5_bias", "b_norm", "b_sinks", "final_norm")
BIG = ("a_w_in", "a_w_out", "kv_w", "b_w_in", "b_w_out")
ORDER = ("a_norm", "a_w_in", "a_rel_bias", "a_w_out", "kv_norm", "kv_w", "t5_bias", "b_norm", "b_w_in",
         "b_sinks", "b_w_out", "final_norm")


def _pack(parts, rows):
    flat = jnp.concatenate([p.reshape(-1).astype(F32) for p in parts])
    return jnp.pad(flat, (0, rows * LANES - flat.shape[0])).reshape(rows, LANES)


def _unpack(block, shapes):
    flat = block.reshape(-1)
    out, at = [], 0
    for shp in shapes:
        size = int(np.prod(shp))
        out.append(flat[at:at + size].reshape(shp))
        at += size
    return out


def kernel(x, a_norm, a_w_in, a_rel_bias, a_w_out, kv_norm, kv_w, t5_bias, b_norm, b_w_in, b_sinks, b_w_out, final_norm, loss_target, m_a_norm, m_a_w_in, m_a_rel_bias, m_a_w_out, m_kv_norm, m_kv_w, m_t5_bias, m_b_norm, m_b_w_in, m_b_sinks, m_b_w_out, m_final_norm, v_a_norm, v_a_w_in, v_a_rel_bias, v_a_w_out, v_kv_norm, v_kv_w, v_t5_bias, v_b_norm, v_b_w_in, v_b_sinks, v_b_w_out, v_final_norm):
    w = dict(a_norm=a_norm, a_w_in=a_w_in, a_rel_bias=a_rel_bias, a_w_out=a_w_out, kv_norm=kv_norm, kv_w=kv_w,
             t5_bias=t5_bias, b_norm=b_norm, b_w_in=b_w_in, b_sinks=b_sinks, b_w_out=b_w_out,
             final_norm=final_norm)
    m = dict(a_norm=m_a_norm, a_w_in=m_a_w_in, a_rel_bias=m_a_rel_bias, a_w_out=m_a_w_out, kv_norm=m_kv_norm,
             kv_w=m_kv_w, t5_bias=m_t5_bias, b_norm=m_b_norm, b_w_in=m_b_w_in, b_sinks=m_b_sinks,
             b_w_out=m_b_w_out, final_norm=m_final_norm)
    v = dict(a_norm=v_a_norm, a_w_in=v_a_w_in, a_rel_bias=v_a_rel_bias, a_w_out=v_a_w_out, kv_norm=v_kv_norm,
             kv_w=v_kv_w, t5_bias=v_t5_bias, b_norm=v_b_norm, b_w_in=v_b_w_in, b_sinks=v_b_sinks,
             b_w_out=v_b_w_out, final_norm=v_final_norm)
    d = D_MODEL
    chip = 2 * lax.axis_index("x") + lax.axis_index("y")

    shard2d = dict(a_w_in=a_w_in[0], a_w_out=a_w_out[0], kv_w=kv_w, b_w_in=b_w_in[0], b_w_out=b_w_out[0])

    wa_in, ga = _run_alone("allgather_first",
                           _allgather_hosted([shard2d["a_w_in"].astype(BF16), a_norm], [True, False]))
    ga = ga.reshape(1, d)

    loss, grad_x, small, own, received = _local_step(
        x[0], loss_target[0], ga, wa_in, a_rel_bias[0], [shard2d[n].astype(BF16) for n in BIG[1:]],
        kv_norm.reshape(1, d), t5_bias, b_norm, b_sinks, final_norm.reshape(1, d))

    small_shapes = [small[n].shape for n in SMALL] + [(1, 1)]
    total = sum(int(np.prod(s)) for s in small_shapes)
    rows = -(-total // (8 * LANES)) * 8
    reduced = _unpack(_allreduce_small(_pack([small[n] for n in SMALL] + [loss], rows)), small_shapes)
    g_small = dict(zip(SMALL, reduced[:-1]))
    loss_out = reduced[-1].reshape(())
    g_small["a_norm"] = lax.dynamic_slice_in_dim(g_small["a_norm"], chip * (d // 4), d // 4, axis=1)

    core_sums = [
        _sum_partials("sum_" + n, lax.dynamic_index_in_dim(own[n], chip, 0, keepdims=False), received[n])
        for n in BIG]
    sibling_sums = _swap_with_sibling(core_sums)

    out = {}
    for n, mine, theirs in zip(BIG, core_sums, sibling_sums):
        res = _adamw("adamw_" + n, shard2d[n], m[n].reshape(shard2d[n].shape), v[n].reshape(shard2d[n].shape),
                     [mine, theirs])
        out[n] = [r.reshape(w[n].shape) for r in res]
    small_w_shapes = [w[n].shape for n in SMALL]
    total_w = sum(int(np.prod(s)) for s in small_w_shapes)
    rows_w = -(-total_w // (8 * LANES)) * 8
    packed = [_pack([t[n] for n in SMALL], rows_w) for t in (w, m, v)]
    g_packed = _pack([g_small[n] for n in SMALL], rows_w)
    res = _adamw("adamw_small", packed[0], packed[1], packed[2], [g_packed])
    unpacked = [_unpack(r, small_w_shapes) for r in res]
    for i, n in enumerate(SMALL):
        out[n] = [unpacked[k][i] for k in range(4)]

    grads = [out[n][0] for n in ORDER]
    deltas = [out[n][1] for n in ORDER]
    new_m = [out[n][2] for n in ORDER]
    new_v = [out[n][3] for n in ORDER]
    return (loss_out, grad_x[None], *grads, *deltas, *new_m, *new_v)
```

```python
import functools
import math

import jax
import jax.numpy as jnp
import numpy as np
from jax import lax
from jax.experimental import pallas as pl
from jax.experimental.pallas import tpu as pltpu

F32 = jnp.float32
BF16 = jnp.bfloat16
MESH = pl.DeviceIdType.MESH

D_MODEL = 1024
HEADS = 16
HEAD_DIM = 64
CHUNK = 64
RMS_EPS = 1e-6
SCALE = HEAD_DIM ** -0.5
A_LEFT_CHUNKS = 8
A_REL_CLIP = 256
B_LEFT_CHUNKS = 2
B_KV_HEADS = 2
B_GROUP = HEADS // B_KV_HEADS
T5_BUCKETS = 32
T5_MAX_DIST = 128
ADAM_LR = 0.001
ADAM_B1 = 0.9
ADAM_B2 = 0.999
ADAM_EPS = 1e-08
ADAM_WD = 0.01
ADAM_STEP = 10

MASKED = -1e30
LANES = 128
TQ = 128
A_PAIRS = 2
A_PAIRS_FWD = 4
KB = 128
A_KBLOCKS = A_LEFT_CHUNKS * CHUNK // KB + 1
B_KBLOCKS = B_LEFT_CHUNKS * CHUNK // KB + 1
A_WIN = A_KBLOCKS * KB
B_WIN = B_KBLOCKS * KB
TM = 512
TM_DENSE = 1024
TM_PARTS = 512
VMEM_LIMIT = 56 * 1024 * 1024

NT = (((1,), (1,)), ((), ()))
TN = (((0,), (0,)), ((), ()))
NN = (((1,), (0,)), ((), ()))


def _params(sem=None):
    return pltpu.CompilerParams(dimension_semantics=sem, vmem_limit_bytes=VMEM_LIMIT)


class _Hosted:
    def __init__(self, inputs, out_shapes, sems, first, middle, last):
        self.inputs, self.out_shapes, self.sems = list(inputs), list(out_shapes), list(sems)
        self.first, self.middle, self.last = first, middle, last


def _call(body, *, name, grid, in_specs, out_specs, out_shape, args, scratch_shapes=(), sem=None, hosted=None):
    in_specs, out_specs, out_shape = list(in_specs), list(out_specs), list(out_shape)
    scratch_shapes = list(scratch_shapes)
    if hosted is None:
        out = pl.pallas_call(
            body, name=name, grid=grid, in_specs=in_specs, out_specs=out_specs, out_shape=out_shape,
            scratch_shapes=scratch_shapes, compiler_params=_params(sem))(*args)
        return list(out), []
    n_in, n_out, n_scr = len(in_specs), len(out_shape), len(scratch_shapes)
    h_in, h_out = len(hosted.inputs), len(hosted.out_shapes)
    total = int(np.prod(grid)) if grid else 1

    def wrapped(*refs):
        ins, refs = refs[:n_in], refs[n_in:]
        h_ins, refs = refs[:h_in], refs[h_in:]
        outs, refs = refs[:n_out], refs[n_out:]
        h_outs, refs = refs[:h_out], refs[h_out:]
        scr, h_sems = refs[:n_scr], refs[n_scr:]
        step = 0
        for axis, size in enumerate(grid):
            step = step * size + pl.program_id(axis)

        @pl.when(step == 0)
        def _():
            hosted.first(h_ins, h_outs, h_sems)

        body(*ins, *outs, *scr)
        if hosted.middle is not None:
            @pl.when(step == total // 2)
            def _():
                hosted.middle(h_ins, h_outs, h_sems)

        @pl.when(step == total - 1)
        def _():
            hosted.last(h_ins, h_outs, h_sems)

    out = pl.pallas_call(
        wrapped, name=name, grid=grid, in_specs=in_specs + [ANY] * h_in, out_specs=out_specs + [ANY] * h_out,
        out_shape=out_shape + hosted.out_shapes, scratch_shapes=scratch_shapes + hosted.sems,
        compiler_params=_params(("arbitrary",) * len(grid)))(*args, *hosted.inputs)
    return list(out[:n_out]), list(out[n_out:])


def _matmul(name, a, b, *, dims, grid, a_spec, b_spec, o_spec, out_shape, out_dtype,
            parts=1, resid=None, resid_spec=None, also_bf16=False, hosted=None):
    def body(*refs):
        a_ref, b_ref = refs[:2]
        r_ref = refs[2] if resid is not None else None
        o_ref = refs[3] if resid is not None else refs[2]
        if parts == 1:
            prod = lax.dot_general(a_ref[...].astype(BF16), b_ref[...].astype(BF16), dims,
                                   preferred_element_type=F32)
        else:
            prod = None
            for part in range(parts):
                term = lax.dot_general(a_ref[part].astype(BF16), b_ref[part].astype(BF16), dims,
                                       preferred_element_type=F32)
                prod = term if prod is None else prod + term
        if resid is not None:
            prod = r_ref[...] + prod
        o_ref[...] = prod.astype(out_dtype)
        if also_bf16:
            refs[-1][...] = prod.astype(BF16)

    in_specs = [a_spec, b_spec]
    args = [a, b]
    if resid is not None:
        in_specs.append(resid_spec)
        args.append(resid)
    sem = ["parallel"] * len(grid)
    out_specs = [o_spec]
    out_shapes = [jax.ShapeDtypeStruct(out_shape, out_dtype)]
    if also_bf16:
        out_specs.append(o_spec)
        out_shapes.append(jax.ShapeDtypeStruct(out_shape, BF16))
    out, extra = _call(body, name=name, grid=grid, in_specs=in_specs, out_specs=out_specs, out_shape=out_shapes,
                       args=args, sem=tuple(sem), hosted=hosted)
    res = out[0] if not also_bf16 else tuple(out)
    return res if hosted is None else (res, extra)


def _rms_rows(x):
    return lax.rsqrt(jnp.mean(x * x, axis=-1, keepdims=True) + RMS_EPS)


def _norm_fwd(name, x, gains):
    s, d = x.shape
    n = gains.shape[0]

    def body(x_ref, g_ref, *o_refs):
        xv = x_ref[...]
        xh = xv * _rms_rows(xv)
        for i in range(n):
            o_refs[i][...] = (xh * g_ref[i:i + 1, :]).astype(BF16)

    row = pl.BlockSpec((TM, d), lambda i: (i, 0))
    return pl.pallas_call(
        body, name=name, grid=(s // TM,),
        in_specs=[row, pl.BlockSpec((n, d), lambda i: (0, 0))],
        out_specs=[row] * n,
        out_shape=[jax.ShapeDtypeStruct((s, d), BF16)] * n,
        compiler_params=_params(("parallel",)),
    )(x, gains)


def _norm_bwd(name, x, dres, dns, gains):
    s, d = x.shape
    n = len(dns)

    def body(x_ref, r_ref, g_ref, *refs):
        dn_refs, dx_ref, dg_ref = refs[:n], refs[n], refs[n + 1]
        i = pl.program_id(0)
        xv = x_ref[...]
        r = _rms_rows(xv)
        xh = xv * r

        @pl.when(i == 0)
        def _():
            dg_ref[...] = jnp.zeros_like(dg_ref)

        a = None
        for j in range(n):
            dn = dn_refs[j][...]
            t = dn * g_ref[j:j + 1, :]
            a = t if a is None else a + t
            dg_ref[j:j + 1, :] += jnp.sum(dn * xh, axis=0, keepdims=True)
        dx_ref[...] = r_ref[...] + r * (a - xh * jnp.mean(xh * a, axis=-1, keepdims=True))

    row = pl.BlockSpec((TM, d), lambda i: (i, 0))
    small = pl.BlockSpec((n, d), lambda i: (0, 0))
    return pl.pallas_call(
        body, name=name, grid=(s // TM,),
        in_specs=[row, row, small] + [row] * n,
        out_specs=[row, small],
        out_shape=[jax.ShapeDtypeStruct((s, d), F32), jax.ShapeDtypeStruct((n, d), F32)],
        compiler_params=_params(("arbitrary",)),
    )(x, dres, gains, *dns)


def _loss_head(h2, target, gain):
    s, d = h2.shape

    def body(h_ref, t_ref, g_ref, dh_ref, loss_ref, dg_ref):
        i = pl.program_id(0)
        hv = h_ref[...]
        r = _rms_rows(hv)
        hh = hv * r
        g = g_ref[...]
        err = hh * g - t_ref[...]
        part = 0.5 * jnp.sum(jnp.sum(err * err, axis=-1, keepdims=True) * (1.0 / d), axis=0, keepdims=True)
        dy = err * (1.0 / d)
        a = dy * g
        dh_ref[...] = r * (a - hh * jnp.mean(hh * a, axis=-1, keepdims=True))
        dg = jnp.sum(dy * hh, axis=0, keepdims=True)

        @pl.when(i == 0)
        def _():
            loss_ref[...] = part
            dg_ref[...] = dg

        @pl.when(i > 0)
        def _():
            loss_ref[...] += part
            dg_ref[...] += dg

    row = pl.BlockSpec((TM, d), lambda i: (i, 0))
    return pl.pallas_call(
        body, name="loss_head", grid=(s // TM,),
        in_specs=[row, row, pl.BlockSpec((1, d), lambda i: (0, 0))],
        out_specs=[row, pl.BlockSpec((1, 1), lambda i: (0, 0)), pl.BlockSpec((1, d), lambda i: (0, 0))],
        out_shape=[jax.ShapeDtypeStruct((s, d), F32), jax.ShapeDtypeStruct((1, 1), F32),
                   jax.ShapeDtypeStruct((1, d), F32)],
        compiler_params=_params(("arbitrary",)),
    )(h2, target, gain)


def _silu_parts(g):
    sig = jax.nn.sigmoid(g)
    return g * sig, sig * (1.0 + g * (1.0 - sig))


def _lane_lo(rows):
    return lax.broadcasted_iota(jnp.int32, (rows, LANES), 1) < HEAD_DIM


def _stack_pair(x):
    lo = _lane_lo(x.shape[0])
    zero = jnp.zeros_like(x)
    return jnp.concatenate([jnp.where(lo, x, zero), jnp.where(lo, zero, x)], axis=0)


def _unstack_pair(y, w):
    return jnp.where(_lane_lo(w), y[:w], y[w:])


def _block_valid(b, left_blocks, width):
    col = lax.broadcasted_iota(jnp.int32, (1, 2 * width), 1)
    col = jnp.where(col >= width, col - width, col)
    return (col // KB + (b - left_blocks)) >= 0


def _toeplitz_tile(diag_row, width, left_chunks):
    wide = width + TQ
    rolled = pltpu.roll(jnp.broadcast_to(diag_row, (TQ, wide)), 1, 1, stride=1, stride_axis=0)
    i = lax.broadcasted_iota(jnp.int32, (TQ, width), 0) // CHUNK
    j = lax.broadcasted_iota(jnp.int32, (TQ, width), 1) // CHUNK
    dc = i + left_chunks - j
    return jnp.where((dc >= 0) & (dc <= left_chunks), rolled[:, TQ:], MASKED)


def _toeplitz_sum(tile, width):
    flip = (lax.broadcasted_iota(jnp.int32, (TQ, TQ), 0) + lax.broadcasted_iota(jnp.int32, (TQ, TQ), 1)
            == TQ - 1).astype(F32)
    reversed_rows = jnp.dot(flip, tile, precision=lax.Precision.HIGHEST, preferred_element_type=F32)
    padded = jnp.concatenate([reversed_rows, jnp.zeros((TQ, TQ), F32)], axis=1)
    rolled = pltpu.roll(padded, 0, 1, stride=1, stride_axis=0)
    return jnp.sum(rolled, axis=0, keepdims=True)


def _softmax_pair(sc, w, sink=None):
    ps, inv, lses = [], [], []
    for e in range(2):
        sh = sc[:, e * w:(e + 1) * w]
        m = jnp.max(sh, axis=-1, keepdims=True)
        if sink is not None:
            m = jnp.maximum(m, sink[e])
        ex = jnp.exp(sh - m)
        l = jnp.sum(ex, axis=-1, keepdims=True)
        if sink is not None:
            l = l + jnp.exp(sink[e] - m)
        ps.append(ex.astype(BF16))
        inv.append(1.0 / l)
        lses.append(m + jnp.log(l))
    return jnp.concatenate(ps, axis=-1), inv, lses


def _softmax_pair_bwd(sc, dp, lse, delta, w):
    ps, dss = [], []
    for e in range(2):
        p = jnp.exp(sc[:, e * w:(e + 1) * w] - lse[e])
        ps.append(p)
        dss.append(p * (dp[:, e * w:(e + 1) * w] - delta[e]))
    return jnp.concatenate(ps, axis=-1), jnp.concatenate(dss, axis=-1)


def _pair_rowsums(x, lo):
    zero = jnp.zeros_like(x)
    return (jnp.sum(jnp.where(lo, x, zero), axis=-1, keepdims=True),
            jnp.sum(jnp.where(lo, zero, x), axis=-1, keepdims=True))


def _a_kv_specs(left, pw):
    specs = []
    for which in (1, 2):
        for t in range(A_KBLOCKS):
            specs.append(pl.BlockSpec(
                (None, KB, pw), functools.partial(
                    lambda p, b, which, t: (which, jnp.maximum(b - left + t, 0), p), which=which, t=t)))
    return specs


def _attn_a_fwd(zqkv, g, diag, hosted=None):
    s = g.shape[0]
    nb = s // TQ
    left = A_KBLOCKS - 1
    pairs = A_PAIRS_FWD
    pw = pairs * LANES
    wide = A_WIN + TQ

    def body(q_ref, *refs):
        k_refs = refs[:A_KBLOCKS]
        v_refs = refs[A_KBLOCKS:2 * A_KBLOCKS]
        g_ref, diag_ref, o_ref, u_ref, lse_ref, bias_scr = refs[2 * A_KBLOCKS:]
        b = pl.program_id(1)

        @pl.when(b == 0)
        def _():
            for hh in range(2 * pairs):
                bias_scr[hh // 2, :, (hh % 2) * A_WIN:(hh % 2 + 1) * A_WIN] = _toeplitz_tile(
                    diag_ref[hh], A_WIN, A_LEFT_CHUNKS)

        def step(first_blocks):
            lo = _lane_lo(TQ)
            for pp in range(pairs):
                ln = slice(pp * LANES, (pp + 1) * LANES)
                kcat = _stack_pair(jnp.concatenate([r[:, ln] for r in k_refs], axis=0))
                vcat = _stack_pair(jnp.concatenate([r[:, ln] for r in v_refs], axis=0))
                sc = lax.dot_general(q_ref[:, ln] * SCALE, kcat, NT, preferred_element_type=F32) + bias_scr[pp]
                if first_blocks:
                    sc = jnp.where(_block_valid(b, left, A_WIN), sc, MASKED)
                p, inv, lses = _softmax_pair(sc, A_WIN)
                ov = jnp.dot(p, vcat, preferred_element_type=F32) * jnp.where(lo, inv[0], inv[1])
                o_ref[:, ln] = ov
                lse_ref[pp] = jnp.where(lo, lses[0], lses[1])
                sg, _ = _silu_parts(g_ref[:, ln])
                u_ref[:, ln] = (ov * sg).astype(BF16)

        @pl.when(b < left)
        def _():
            step(True)

        @pl.when(b >= left)
        def _():
            step(False)

    tile = pl.BlockSpec((TQ, pw), lambda p, b: (b, p))
    return _call(
        body, name="attn_a_fwd", grid=(HEADS // 2 // pairs, nb),
        in_specs=[pl.BlockSpec((None, TQ, pw), lambda p, b: (0, b, p))] + _a_kv_specs(left, pw) + [
            tile, pl.BlockSpec((2 * pairs, 1, wide), lambda p, b: (p, 0, 0))],
        out_specs=[tile, tile, pl.BlockSpec((pairs, TQ, LANES), lambda p, b: (p, b, 0))],
        out_shape=[jax.ShapeDtypeStruct((s, D_MODEL), F32), jax.ShapeDtypeStruct((s, D_MODEL), BF16),
                   jax.ShapeDtypeStruct((HEADS // 2, s, LANES), F32)],
        scratch_shapes=[pltpu.VMEM((pairs, TQ, 2 * A_WIN), F32)],
        sem=("parallel", "arbitrary"), hosted=hosted,
        args=(zqkv, *([zqkv] * (2 * A_KBLOCKS)), g, diag))


def _attn_a_bwd(zqkv, g, o, du, lse, diag, hosted=None):
    s = g.shape[0]
    nb = s // TQ
    left = A_KBLOCKS - 1
    pw = A_PAIRS * LANES
    wide = A_WIN + TQ

    def body(q_ref, *refs):
        k_refs = refs[:A_KBLOCKS]
        v_refs = refs[A_KBLOCKS:2 * A_KBLOCKS]
        (g_ref, o_ref, du_ref, lse_ref, diag_ref, dz_ref, ddiag_ref,
         bias_scr, dbias_acc, dk_acc, dv_acc) = refs[2 * A_KBLOCKS:]
        b = pl.program_id(1)

        @pl.when(b == 0)
        def _():
            for hh in range(2 * A_PAIRS):
                bias_scr[hh // 2, :, (hh % 2) * A_WIN:(hh % 2 + 1) * A_WIN] = _toeplitz_tile(
                    diag_ref[hh], A_WIN, A_LEFT_CHUNKS)
            dbias_acc[...] = jnp.zeros_like(dbias_acc)
            dk_acc[...] = jnp.zeros_like(dk_acc)
            dv_acc[...] = jnp.zeros_like(dv_acc)

        def step(first_blocks):
            lo = _lane_lo(TQ)
            upper = lax.broadcasted_iota(jnp.int32, (LANES, A_WIN), 0) < HEAD_DIM
            rows = pl.ds(pl.multiple_of(b * TQ, TQ), TQ)
            sg, dsg = _silu_parts(g_ref[...])
            duv = du_ref[...]
            ov = o_ref[...]
            do = duv * sg
            dz_ref[3, rows, :] = (duv * ov * dsg).astype(BF16)
            do_o = do * ov
            do_bf = do.astype(BF16)
            for pp in range(A_PAIRS):
                ln = slice(pp * LANES, (pp + 1) * LANES)
                q = q_ref[:, ln] * SCALE
                kcat = _stack_pair(jnp.concatenate([r[:, ln] for r in k_refs], axis=0))
                vcat = _stack_pair(jnp.concatenate([r[:, ln] for r in v_refs], axis=0))
                sc = lax.dot_general(q, kcat, NT, preferred_element_type=F32) + bias_scr[pp]
                if first_blocks:
                    sc = jnp.where(_block_valid(b, left, A_WIN), sc, MASKED)
                lse_t = lse_ref[pp]
                dp = lax.dot_general(do_bf[:, ln], vcat, NT, preferred_element_type=F32)
                p, ds = _softmax_pair_bwd(sc, dp, (lse_t[:, 0:1], lse_t[:, HEAD_DIM:HEAD_DIM + 1]),
                                          _pair_rowsums(do_o[:, ln], lo), A_WIN)
                dbias_acc[pp] += ds
                dsb = ds.astype(BF16)
                dz_ref[0, rows, ln] = (jnp.dot(dsb, kcat, preferred_element_type=F32) * SCALE).astype(BF16)
                dkt = lax.dot_general(q, dsb, TN, preferred_element_type=F32)
                dvt = lax.dot_general(do_bf[:, ln], p.astype(BF16), TN, preferred_element_type=F32)
                dkt = jnp.where(upper, dkt[:, :A_WIN], dkt[:, A_WIN:])
                dvt = jnp.where(upper, dvt[:, :A_WIN], dvt[:, A_WIN:])
                for t in range(A_KBLOCKS):
                    blk = jnp.maximum(b - left + t, 0)
                    dk_acc[blk, ln, :] += dkt[:, t * KB:(t + 1) * KB]
                    dv_acc[blk, ln, :] += dvt[:, t * KB:(t + 1) * KB]

        @pl.when(b < left)
        def _():
            step(True)

        @pl.when(b >= left)
        def _():
            step(False)

        @pl.when(b == nb - 1)
        def _():
            for kb in range(s // KB):
                dz_ref[1, kb * KB:(kb + 1) * KB, :] = dk_acc[kb].T.astype(BF16)
                dz_ref[2, kb * KB:(kb + 1) * KB, :] = dv_acc[kb].T.astype(BF16)
            for hh in range(2 * A_PAIRS):
                ddiag_ref[hh] = _toeplitz_sum(
                    dbias_acc[hh // 2, :, (hh % 2) * A_WIN:(hh % 2 + 1) * A_WIN], A_WIN)

    tile = pl.BlockSpec((TQ, pw), lambda p, b: (b, p))
    diag_spec = pl.BlockSpec((2 * A_PAIRS, 1, wide), lambda p, b: (p, 0, 0))
    return _call(
        body, name="attn_a_bwd", grid=(HEADS // 2 // A_PAIRS, nb),
        in_specs=[pl.BlockSpec((None, TQ, pw), lambda p, b: (0, b, p))] + _a_kv_specs(left, pw) + [
            tile, tile, tile, pl.BlockSpec((A_PAIRS, TQ, LANES), lambda p, b: (p, b, 0)), diag_spec],
        out_specs=[pl.BlockSpec((4, s, pw), lambda p, b: (0, 0, p)), diag_spec],
        out_shape=[jax.ShapeDtypeStruct((4, s, D_MODEL), BF16),
                   jax.ShapeDtypeStruct((HEADS, 1, wide), F32)],
        scratch_shapes=[pltpu.VMEM((A_PAIRS, TQ, 2 * A_WIN), F32), pltpu.VMEM((A_PAIRS, TQ, 2 * A_WIN), F32),
                        pltpu.VMEM((s // KB, pw, KB), F32), pltpu.VMEM((s // KB, pw, KB), F32)],
        sem=("parallel", "arbitrary"), hosted=hosted,
        args=(zqkv, *([zqkv] * (2 * A_KBLOCKS)), g, o, du, lse, diag))


B_STACK = B_GROUP // 2
B_KVX = 4 * LANES
B_ROWS = B_STACK * TQ
B_WIDE = B_WIN + TQ


def _b_head_place(h):
    return h // B_GROUP, (h % B_GROUP) // 2, h % 2


def _toeplitz_tile_t(base_row, width, left_chunks):
    wide = width + TQ
    rolled = pltpu.roll(jnp.broadcast_to(base_row, (width, wide)), 0, 1, stride=1, stride_axis=0)
    j = lax.broadcasted_iota(jnp.int32, (width, TQ), 0) // CHUNK
    i = lax.broadcasted_iota(jnp.int32, (width, TQ), 1) // CHUNK
    dc = i + left_chunks - j
    return jnp.where((dc >= 0) & (dc <= left_chunks), rolled[:, :TQ], MASKED)


def _toeplitz_sum_t(tile_t, width):
    flip = (lax.broadcasted_iota(jnp.int32, (width, width), 0) + lax.broadcasted_iota(jnp.int32, (width, width), 1)
            == width - 1).astype(F32)
    reversed_rows = jnp.dot(flip, tile_t, precision=lax.Precision.HIGHEST, preferred_element_type=F32)
    padded = jnp.concatenate([reversed_rows, jnp.zeros((width, width), F32)], axis=1)
    rolled = pltpu.roll(padded, 0, 1, stride=1, stride_axis=0)
    return jnp.sum(rolled, axis=0, keepdims=True)


def _b_build_bias(base_ref, bias_scr):
    for h in range(HEADS):
        gi, pr, e = _b_head_place(h)
        bias_scr[gi, e * B_WIN:(e + 1) * B_WIN, pr * TQ:(pr + 1) * TQ] = _toeplitz_tile_t(
            base_ref[h], B_WIN, B_LEFT_CHUNKS)


def _b_stack(x, gi):
    return jnp.concatenate(
        [x[:, (B_STACK * gi + pr) * LANES:(B_STACK * gi + pr + 1) * LANES] for pr in range(B_STACK)], axis=0)


def _b_sink_rows(sink_ref, gi):
    block = lax.broadcasted_iota(jnp.int32, (1, B_ROWS), 1) // TQ
    rows = []
    for e in range(2):
        row = jnp.zeros((1, B_ROWS), F32)
        for pr in range(B_STACK):
            h = B_GROUP * gi + 2 * pr + e
            row = jnp.where(block == pr, sink_ref[0:1, h:h + 1], row)
        rows.append(row)
    return rows


def _b_scores_t(q_ref, kvv, bias_scr, gi, b, left, first_blocks):
    kcat = _stack_pair(kvv[:, gi * LANES:(gi + 1) * LANES])
    vcat = _stack_pair(kvv[:, (B_KV_HEADS + gi) * LANES:(B_KV_HEADS + gi + 1) * LANES])
    qs = _b_stack(q_ref, gi) * SCALE
    sc = lax.dot_general(kcat, qs, NT, preferred_element_type=F32) + bias_scr[gi]
    if first_blocks:
        row = lax.broadcasted_iota(jnp.int32, (2 * B_WIN, 1), 0)
        row = jnp.where(row >= B_WIN, row - B_WIN, row)
        sc = jnp.where((row // KB + (b - left)) >= 0, sc, MASKED)
    return kcat, vcat, qs, sc


def _attn_b_fwd(qb, kvx, gate, base, sinks):
    s = qb.shape[0]
    nb = s // TQ
    left = B_KBLOCKS - 1

    def body(q_ref, *refs):
        kv_refs = refs[:B_KBLOCKS]
        g_ref, base_ref, sink_ref, o_ref, u_ref, lse_ref, bias_scr = refs[B_KBLOCKS:]
        b = pl.program_id(0)

        @pl.when(b == 0)
        def _():
            _b_build_bias(base_ref, bias_scr)

        def step(first_blocks):
            kvv = jnp.concatenate([r[...] for r in kv_refs], axis=0)
            upper = lax.broadcasted_iota(jnp.int32, (LANES, B_ROWS), 0) < HEAD_DIM
            lse_rows = []
            for gi in range(B_KV_HEADS):
                kcat, vcat, qs, sc = _b_scores_t(q_ref, kvv, bias_scr, gi, b, left, first_blocks)
                sink = _b_sink_rows(sink_ref, gi)
                ps, inv = [], []
                for e in range(2):
                    sh = sc[e * B_WIN:(e + 1) * B_WIN]
                    m = jnp.maximum(jnp.max(sh, axis=0, keepdims=True), sink[e])
                    ex = jnp.exp(sh - m)
                    l = jnp.sum(ex, axis=0, keepdims=True) + jnp.exp(sink[e] - m)
                    ps.append(ex.astype(BF16))
                    inv.append(1.0 / l)
                    lse_rows.append(m + jnp.log(l))
                pt = jnp.concatenate(ps, axis=0)
                ot = lax.dot_general(vcat, pt, TN, preferred_element_type=F32) * jnp.where(upper, inv[0], inv[1])
                ov = ot.T
                for pr in range(B_STACK):
                    pair = B_STACK * gi + pr
                    o_ref[:, pair * LANES:(pair + 1) * LANES] = ov[pr * TQ:(pr + 1) * TQ]
            lse_ref[0] = jnp.concatenate(lse_rows + [jnp.zeros((8 - len(lse_rows), B_ROWS), F32)], axis=0)
            sg, _ = _silu_parts(g_ref[...])
            u_ref[...] = (o_ref[...] * sg).astype(BF16)

        @pl.when(b < left)
        def _():
            step(True)

        @pl.when(b >= left)
        def _():
            step(False)

    kv_specs = [pl.BlockSpec((KB, B_KVX), functools.partial(
        lambda b, t: (jnp.maximum(b - left + t, 0), 0), t=t)) for t in range(B_KBLOCKS)]
    row = pl.BlockSpec((TQ, D_MODEL), lambda b: (b, 0))
    return pl.pallas_call(
        body, name="attn_b_fwd", grid=(nb,),
        in_specs=[row] + kv_specs + [row, pl.BlockSpec((HEADS, 1, B_WIDE), lambda b: (0, 0, 0)),
                                     pl.BlockSpec((1, HEADS), lambda b: (0, 0))],
        out_specs=[row, row, pl.BlockSpec((1, 8, B_ROWS), lambda b: (b, 0, 0))],
        out_shape=[jax.ShapeDtypeStruct((s, D_MODEL), F32), jax.ShapeDtypeStruct((s, D_MODEL), BF16),
                   jax.ShapeDtypeStruct((nb, 8, B_ROWS), F32)],
        scratch_shapes=[pltpu.VMEM((B_KV_HEADS, 2 * B_WIN, B_ROWS), F32)],
        compiler_params=_params(("arbitrary",)),
    )(qb, *([kvx] * B_KBLOCKS), gate, base, sinks)


def _attn_b_bwd(qb, kvx, gate, o, du, lse, base, sinks):
    s = qb.shape[0]
    nb = s // TQ
    left = B_KBLOCKS - 1
    half = D_MODEL // 2

    def body(q_ref, *refs):
        kv_refs = refs[:B_KBLOCKS]
        (g_ref, o_ref, du_ref, lse_ref, base_ref, sink_ref, dz_ref, dkv_ref, dsum_ref, dsink_ref,
         bias_scr, dbias_acc, dkv_acc, dsink_acc) = refs[B_KBLOCKS:]
        b = pl.program_id(0)

        @pl.when(b == 0)
        def _():
            _b_build_bias(base_ref, bias_scr)
            dbias_acc[...] = jnp.zeros_like(dbias_acc)
            dkv_acc[...] = jnp.zeros_like(dkv_acc)
            dsink_acc[...] = jnp.zeros_like(dsink_acc)

        def step(first_blocks):
            kvv = jnp.concatenate([r[...] for r in kv_refs], axis=0)
            sg, dsg = _silu_parts(g_ref[...])
            duv = du_ref[...]
            ov = o_ref[...]
            do = duv * sg
            dgate = (duv * ov * dsg).astype(BF16)
            dz_ref[2] = dgate[:, :half]
            dz_ref[3] = dgate[:, half:]
            do_o = do * ov
            do_bf = do.astype(BF16)
            lane = lax.broadcasted_iota(jnp.int32, (8, LANES), 1)
            sub = lax.broadcasted_iota(jnp.int32, (8, LANES), 0)
            halves = (((sub == 0) & (lane < HEAD_DIM)) | ((sub == 1) & (lane >= HEAD_DIM))).astype(F32)
            lse_all = lse_ref[0]
            dsink_rows = []
            for gi in range(B_KV_HEADS):
                kcat, vcat, qs, sc = _b_scores_t(q_ref, kvv, bias_scr, gi, b, left, first_blocks)
                dos = _b_stack(do_bf, gi)
                delta = lax.dot_general(halves, _b_stack(do_o, gi), NT, precision=lax.Precision.HIGHEST,
                                        preferred_element_type=F32)
                sink = _b_sink_rows(sink_ref, gi)
                dp = lax.dot_general(vcat, dos, NT, preferred_element_type=F32)
                ps, dss = [], []
                for e in range(2):
                    lse_e = lse_all[2 * gi + e:2 * gi + e + 1]
                    delta_e = delta[e:e + 1]
                    p = jnp.exp(sc[e * B_WIN:(e + 1) * B_WIN] - lse_e)
                    ps.append(p.astype(BF16))
                    dss.append(p * (dp[e * B_WIN:(e + 1) * B_WIN] - delta_e))
                    dsink_rows.append(-jnp.exp(sink[e] - lse_e) * delta_e)
                ds = jnp.concatenate(dss, axis=0)
                dbias_acc[gi] += ds
                dsb = ds.astype(BF16)
                dq = (lax.dot_general(kcat, dsb, TN, preferred_element_type=F32) * SCALE).T.astype(BF16)
                for pr in range(B_STACK):
                    dz_ref[gi, :, pr * LANES:(pr + 1) * LANES] = dq[pr * TQ:(pr + 1) * TQ]
                dk = _unstack_pair(jnp.dot(dsb, qs, preferred_element_type=F32), B_WIN)
                dv = _unstack_pair(jnp.dot(jnp.concatenate(ps, axis=0), dos, preferred_element_type=F32), B_WIN)
                for t in range(B_KBLOCKS):
                    krows = pl.ds(pl.multiple_of(jnp.maximum(b - left + t, 0) * KB, KB), KB)
                    dkv_acc[krows, gi * LANES:(gi + 1) * LANES] += dk[t * KB:(t + 1) * KB, :]
                    dkv_acc[krows, (B_KV_HEADS + gi) * LANES:(B_KV_HEADS + gi + 1) * LANES] += dv[t * KB:(t + 1) * KB, :]
            dsink_acc[...] += jnp.concatenate(
                dsink_rows + [jnp.zeros((8 - len(dsink_rows), B_ROWS), F32)], axis=0)

        @pl.when(b < left)
        def _():
            step(True)

        @pl.when(b >= left)
        def _():
            step(False)

        @pl.when(b == nb - 1)
        def _():
            lo_s = _lane_lo(s)
            for which in range(2):
                folded = []
                for gi in range(B_KV_HEADS):
                    part = dkv_acc[:, (which * B_KV_HEADS + gi) * LANES:(which * B_KV_HEADS + gi + 1) * LANES]
                    folded.append(part + pltpu.roll(part, HEAD_DIM, 1))
                dkv_ref[:, which * LANES:(which + 1) * LANES] = jnp.where(lo_s, folded[0], folded[1]).astype(BF16)
            lane8 = lax.broadcasted_iota(jnp.int32, dsink_ref.shape, 1)
            tot = jnp.zeros(dsink_ref.shape, F32)
            for h in range(HEADS):
                gi, pr, e = _b_head_place(h)
                dsum_ref[h] = _toeplitz_sum_t(
                    dbias_acc[gi, e * B_WIN:(e + 1) * B_WIN, pr * TQ:(pr + 1) * TQ], B_WIN)
                per_query = dsink_acc[2 * gi + e:2 * gi + e + 1, pr * TQ:(pr + 1) * TQ]
                tot = jnp.where(lane8 == h, jnp.sum(per_query, axis=1, keepdims=True), tot)
            dsink_ref[...] = tot

    kv_specs = [pl.BlockSpec((KB, B_KVX), functools.partial(
        lambda b, t: (jnp.maximum(b - left + t, 0), 0), t=t)) for t in range(B_KBLOCKS)]
    row = pl.BlockSpec((TQ, D_MODEL), lambda b: (b, 0))
    base_spec = pl.BlockSpec((HEADS, 1, B_WIDE), lambda b: (0, 0, 0))
    return pl.pallas_call(
        body, name="attn_b_bwd", grid=(nb,),
        in_specs=[row] + kv_specs + [row, row, row, pl.BlockSpec((1, 8, B_ROWS), lambda b: (b, 0, 0)),
                                     base_spec, pl.BlockSpec((1, HEADS), lambda b: (0, 0))],
        out_specs=[pl.BlockSpec((4, TQ, half), lambda b: (0, b, 0)),
                   pl.BlockSpec((s, 2 * LANES), lambda b: (0, 0)), base_spec,
                   pl.BlockSpec((8, LANES), lambda b: (0, 0))],
        out_shape=[jax.ShapeDtypeStruct((4, s, half), BF16), jax.ShapeDtypeStruct((s, 2 * LANES), BF16),
                   jax.ShapeDtypeStruct((HEADS, 1, B_WIDE), F32), jax.ShapeDtypeStruct((8, LANES), F32)],
        scratch_shapes=[pltpu.VMEM((B_KV_HEADS, 2 * B_WIN, B_ROWS), F32),
                        pltpu.VMEM((B_KV_HEADS, 2 * B_WIN, B_ROWS), F32),
                        pltpu.VMEM((s, B_KVX), F32), pltpu.VMEM((8, B_ROWS), F32)],
        compiler_params=_params(("arbitrary",)),
    )(qb, *([kvx] * B_KBLOCKS), gate, o, du, lse, base, sinks)


def _t5_bucket(rel):
    nb = T5_BUCKETS // 2
    max_exact = nb // 2
    ret = jnp.where(rel > 0, nb, 0)
    n = jnp.abs(rel)
    nf = jnp.maximum(n, 1).astype(jnp.float32)
    large = max_exact + (jnp.log(nf / max_exact) / math.log(T5_MAX_DIST / max_exact)
                         * (nb - max_exact)).astype(jnp.int32)
    large = jnp.minimum(large, nb - 1)
    return ret + jnp.where(n < max_exact, n, large)


def _a_offset_onehot():
    c = np.arange(A_WIN + TQ)
    dist = A_LEFT_CHUNKS * CHUNK + TQ - 1 - c
    idx = np.clip(dist, -A_REL_CLIP, A_REL_CLIP) + A_REL_CLIP
    onehot = np.zeros((A_WIN + TQ, 2 * A_REL_CLIP + 1), np.float32)
    onehot[c, idx] = 1.0
    return jnp.asarray(onehot)


def _b_offset_onehot():
    c = jnp.arange(B_WIN + TQ, dtype=jnp.int32)
    rel = c - (TQ - 1) - B_LEFT_CHUNKS * CHUNK
    return (_t5_bucket(rel)[:, None] == jnp.arange(T5_BUCKETS)[None, :]).astype(F32)


def _diag_rows(onehot, table):
    rows = jnp.dot(onehot, table.astype(F32), precision=lax.Precision.HIGHEST)
    return rows.T.reshape(HEADS, 1, onehot.shape[0])


def _diag_rows_grad(onehot, ddiag):
    return jnp.dot(ddiag.reshape(HEADS, onehot.shape[0]), onehot, precision=lax.Precision.HIGHEST).T


def _position():
    x, y, c = lax.axis_index("x"), lax.axis_index("y"), lax.axis_index("c")
    chips = [(1 - x, y), (x, 1 - y), (1 - x, 1 - y)]
    return x, y, c, chips


ANY = pl.BlockSpec(memory_space=pl.ANY)


def _allgather_hosted(shards, split):
    n = len(shards)

    def part(ref, t, half):
        if not split[t]:
            return ref
        rows = shards[t].shape[0] // 2
        return ref.at[pl.ds(half * rows, rows)]

    def copies(kind, ins, outs, sems):
        send_sems, recv_sems, pass_send, pass_recv, local_sems = sems
        x, y, c, chips = _position()
        mine = 2 * x + y
        if kind == "local":
            return [pltpu.make_async_copy(ins[t], outs[t].at[mine], local_sems.at[t]) for t in range(n)]
        made = []
        for t in range(n):
            for j, chip in enumerate(chips):
                theirs = 2 * chip[0] + chip[1]
                far = dict(send_sem=send_sems.at[3 * t + j], recv_sem=recv_sems.at[3 * t + j],
                           device_id=(chip[0], chip[1], c), device_id_type=MESH)
                near = dict(send_sem=pass_send.at[3 * t + j], recv_sem=pass_recv.at[3 * t + j],
                            device_id=(x, y, 1 - c), device_id_type=MESH)
                here = part(outs[t].at[theirs], t, c)
                if kind == "send":
                    made.append(pltpu.make_async_remote_copy(
                        src_ref=part(ins[t], t, c), dst_ref=part(outs[t].at[mine], t, c), **far))
                elif kind == "landed":
                    made.append(pltpu.make_async_remote_copy(src_ref=here, dst_ref=here, **far))
                elif not split[t]:
                    made.append(None)
                elif kind == "pass":
                    made.append(pltpu.make_async_remote_copy(src_ref=here, dst_ref=here, **near))
                else:
                    other = part(outs[t].at[theirs], t, 1 - c)
                    made.append(pltpu.make_async_remote_copy(src_ref=other, dst_ref=other, **near))
        return made

    def first(ins, outs, sems):
        for cp in copies("local", ins, outs, sems) + copies("send", ins, outs, sems):
            cp.start()

    def middle(ins, outs, sems):
        for got, cp in zip(copies("landed", ins, outs, sems), copies("pass", ins, outs, sems)):
            got.wait_recv()
            if cp is not None:
                cp.start()

    def last(ins, outs, sems):
        for cp in copies("passed", ins, outs, sems):
            if cp is not None:
                cp.wait_recv()
        for cp in copies("send", ins, outs, sems) + copies("pass", ins, outs, sems):
            if cp is not None:
                cp.wait_send()
        for cp in copies("local", ins, outs, sems):
            cp.wait()

    return _Hosted(shards, [jax.ShapeDtypeStruct((4,) + w.shape, w.dtype) for w in shards],
                   [pltpu.SemaphoreType.DMA((3 * n,))] * 4 + [pltpu.SemaphoreType.DMA((n,))],
                   first, middle, last)


def _scatter_hosted(grads):
    n = len(grads)

    def copies(ins, outs, sems):
        send_sems, recv_sems = sems
        x, y, c, chips = _position()
        return [pltpu.make_async_remote_copy(
            src_ref=ins[t].at[2 * chip[0] + chip[1]], dst_ref=outs[t].at[j],
            send_sem=send_sems.at[3 * t + j], recv_sem=recv_sems.at[3 * t + j],
            device_id=(chip[0], chip[1], c), device_id_type=MESH)
            for t in range(n) for j, chip in enumerate(chips)]

    def first(ins, outs, sems):
        for cp in copies(ins, outs, sems):
            cp.start()

    def last(ins, outs, sems):
        for cp in copies(ins, outs, sems):
            cp.wait()

    return _Hosted(grads, [jax.ShapeDtypeStruct((3,) + g.shape[1:], g.dtype) for g in grads],
                   [pltpu.SemaphoreType.DMA((3 * n,))] * 2, first, None, last)


def _run_alone(name, hosted):
    n_in = len(hosted.inputs)
    n_out = len(hosted.out_shapes)

    def body(*refs):
        ins, outs, sems = refs[:n_in], refs[n_in:n_in + n_out], refs[n_in + n_out:]
        hosted.first(ins, outs, sems)
        if hosted.middle is not None:
            hosted.middle(ins, outs, sems)
        hosted.last(ins, outs, sems)

    return pl.pallas_call(
        body, name=name, in_specs=[ANY] * n_in, out_specs=[ANY] * n_out, out_shape=hosted.out_shapes,
        scratch_shapes=hosted.sems)(*hosted.inputs)


def _swap_with_sibling(blocks):
    n = len(blocks)

    def body(*refs):
        ins, outs = refs[:n], refs[n:2 * n]
        send_sems, recv_sems = refs[2 * n:]
        x, y, c, _ = _position()
        sends = [pltpu.make_async_remote_copy(
            src_ref=ins[t], dst_ref=outs[t], send_sem=send_sems.at[t], recv_sem=recv_sems.at[t],
            device_id=(x, y, 1 - c), device_id_type=MESH) for t in range(n)]
        for cp in sends:
            cp.start()
        for cp in sends:
            cp.wait()

    return pl.pallas_call(
        body, name="swap_with_sibling",
        in_specs=[ANY] * n, out_specs=[ANY] * n,
        out_shape=[jax.ShapeDtypeStruct(b.shape, b.dtype) for b in blocks],
        scratch_shapes=[pltpu.SemaphoreType.DMA((n,))] * 2,
    )(*blocks)


def _allreduce_small(block):
    rows = block.shape[0]

    def body(in_ref, sum_ref, all_ref, send_sems, recv_sems):
        x, y, c, _ = _position()
        me = 4 * x + 2 * y + c
        all_ref[me] = in_ref[...]
        sends = []
        for k in range(1, 8):
            peer = (x ^ (k >> 2), y ^ ((k >> 1) & 1), c ^ (k & 1))
            sends.append(pltpu.make_async_remote_copy(
                src_ref=in_ref, dst_ref=all_ref.at[me], send_sem=send_sems.at[k - 1],
                recv_sem=recv_sems.at[k - 1], device_id=peer, device_id_type=MESH))
        for cp in sends:
            cp.start()
        for k in range(1, 8):
            theirs = me ^ k
            pltpu.make_async_remote_copy(
                src_ref=in_ref, dst_ref=all_ref.at[theirs], send_sem=send_sems.at[k - 1],
                recv_sem=recv_sems.at[k - 1], device_id=(x, y, c), device_id_type=MESH).wait_recv()
        for cp in sends:
            cp.wait_send()
        acc = all_ref[0]
        for d in range(1, 8):
            acc = acc + all_ref[d]
        sum_ref[...] = acc

    vmem = pl.BlockSpec(memory_space=pltpu.VMEM)
    return pl.pallas_call(
        body, name="allreduce_small",
        in_specs=[vmem], out_specs=[vmem, vmem],
        out_shape=[jax.ShapeDtypeStruct((rows, LANES), F32), jax.ShapeDtypeStruct((8, rows, LANES), F32)],
        scratch_shapes=[pltpu.SemaphoreType.DMA((7,))] * 2,
    )(block)[0]


def _adamw_math(w, g, m, v):
    m = ADAM_B1 * m + (1.0 - ADAM_B1) * g
    v = ADAM_B2 * v + (1.0 - ADAM_B2) * (g * g)
    m_hat = m / (1.0 - ADAM_B1 ** ADAM_STEP)
    v_hat = v / (1.0 - ADAM_B2 ** ADAM_STEP)
    delta = -ADAM_LR * (m_hat / (jnp.sqrt(v_hat) + ADAM_EPS) + ADAM_WD * w)
    return delta, m, v


def _row_tile(rows):
    return min(rows, 256)


def _sum_partials(name, own, recv):
    rows, cols = own.shape
    tr = _row_tile(rows)

    def body(own_ref, recv_ref, o_ref):
        acc = own_ref[...]
        for j in range(3):
            acc = acc + recv_ref[j].astype(F32)
        o_ref[...] = acc

    return pl.pallas_call(
        body, name=name, grid=(rows // tr,),
        in_specs=[pl.BlockSpec((tr, cols), lambda i: (i, 0)), pl.BlockSpec((3, tr, cols), lambda i: (0, i, 0))],
        out_specs=pl.BlockSpec((tr, cols), lambda i: (i, 0)),
        out_shape=jax.ShapeDtypeStruct((rows, cols), F32),
        compiler_params=_params(("parallel",)),
    )(own, recv)


def _adamw(name, w, m, v, g_parts):
    rows, cols = w.shape
    tr = _row_tile(rows)
    n = len(g_parts)

    def body(w_ref, m_ref, v_ref, *refs):
        g_refs = refs[:n]
        go_ref, d_ref, mo_ref, vo_ref = refs[n:]
        g = g_refs[0][...]
        for r in g_refs[1:]:
            g = g + r[...]
        delta, mn, vn = _adamw_math(w_ref[...], g, m_ref[...], v_ref[...])
        go_ref[...] = g
        d_ref[...] = delta
        mo_ref[...] = mn
        vo_ref[...] = vn

    spec = pl.BlockSpec((tr, cols), lambda i: (i, 0))
    return pl.pallas_call(
        body, name=name, grid=(rows // tr,),
        in_specs=[spec] * (3 + n), out_specs=[spec] * 4,
        out_shape=[jax.ShapeDtypeStruct((rows, cols), F32)] * 4,
        compiler_params=_params(("parallel",)),
    )(w, m, v, *g_parts)


def _local_step(x, target, ga, wa_in, rel_bias, later_shards, gk, t5, gb, sinks, gf):
    s, d = x.shape
    tm = min(TM_DENSE, s)
    nt = s // tm
    half = d // 2
    row = pl.BlockSpec((tm, d), lambda i: (i, 0))
    whole = lambda shape: pl.BlockSpec(shape, lambda *_: (0,) * len(shape))

    n1, = _norm_fwd("norm_a", x, ga)
    zqkv = _matmul("proj_a_qkv", n1, wa_in, dims=NN, grid=(3, nt),
                   a_spec=pl.BlockSpec((tm, d), lambda j, i: (i, 0)),
                   b_spec=pl.BlockSpec((None, d, d), lambda j, i: (j, 0, 0)),
                   o_spec=pl.BlockSpec((None, tm, d), lambda j, i: (j, i, 0)),
                   out_shape=(3, s, d), out_dtype=BF16)
    gate_a = _matmul("proj_a_gate", n1, wa_in, dims=NN, grid=(nt,),
                     a_spec=row, b_spec=pl.BlockSpec((None, d, d), lambda i: (3, 0, 0)), o_spec=row,
                     out_shape=(s, d), out_dtype=F32)
    onehot_a = _a_offset_onehot()
    diag_a = _diag_rows(onehot_a, rel_bias)
    (o_a, u_a, lse_a), gathered = _attn_a_fwd(
        zqkv, gate_a, diag_a, hosted=_allgather_hosted(later_shards, [True] * len(later_shards)))
    wa_out, wkv, wb_in, wb_out = gathered
    wa_out = wa_out.reshape(d, d)
    wkv = wkv.reshape(d, -1)
    wb_out = wb_out.reshape(d, d)
    h1 = _matmul("out_a", u_a, wa_out, dims=NN, grid=(nt,), a_spec=row, b_spec=whole((d, d)), o_spec=row,
                 out_shape=(s, d), out_dtype=F32, resid=x, resid_spec=row)

    nk, n2 = _norm_fwd("norm_kv_b", h1, jnp.concatenate([gk, gb], axis=0))
    kvw = wkv.shape[1]
    wkv_x = jnp.concatenate([wkv[:, (i // 2) * HEAD_DIM:(i // 2 + 1) * HEAD_DIM] for i in range(8)], axis=1)
    kvx = _matmul("proj_kv", nk, wkv_x, dims=NN, grid=(nt,), a_spec=row, b_spec=whole((d, B_KVX)),
                  o_spec=pl.BlockSpec((tm, B_KVX), lambda i: (i, 0)), out_shape=(s, B_KVX), out_dtype=BF16)
    qb = _matmul("proj_b_q", n2, wb_in, dims=NN, grid=(2, nt),
                 a_spec=pl.BlockSpec((tm, d), lambda j, i: (i, 0)),
                 b_spec=pl.BlockSpec((None, d, half), lambda j, i: (j, 0, 0)),
                 o_spec=pl.BlockSpec((tm, half), lambda j, i: (i, j)), out_shape=(s, d), out_dtype=BF16)
    gate_b = _matmul("proj_b_gate", n2, wb_in, dims=NN, grid=(2, nt),
                     a_spec=pl.BlockSpec((tm, d), lambda j, i: (i, 0)),
                     b_spec=pl.BlockSpec((None, d, half), lambda j, i: (2 + j, 0, 0)),
                     o_spec=pl.BlockSpec((tm, half), lambda j, i: (i, j)), out_shape=(s, d), out_dtype=F32)
    onehot_b = _b_offset_onehot()
    base_b = jnp.roll(_diag_rows(onehot_b, t5)[..., ::-1], TQ, axis=-1)
    o_b, u_b, lse_b = _attn_b_fwd(qb, kvx, gate_b, base_b, sinks)
    h2 = _matmul("out_b", u_b, wb_out, dims=NN, grid=(nt,), a_spec=row, b_spec=whole((d, d)), o_spec=row,
                 out_shape=(s, d), out_dtype=F32, resid=h1, resid_spec=row)

    dh2, loss, d_gf = _loss_head(h2, target, gf)

    du_b = _matmul("dout_b", dh2, wb_out, dims=NT, grid=(nt,), a_spec=row, b_spec=whole((d, d)), o_spec=row,
                   out_shape=(s, d), out_dtype=F32)
    d_wb_out = _matmul("dw_out_b", u_b, dh2, dims=TN, grid=(2,),
                       a_spec=whole((s, d)), b_spec=pl.BlockSpec((s, half), lambda j: (0, j)),
                       o_spec=pl.BlockSpec((d, half), lambda j: (0, j)),
                       out_shape=(d, d), out_dtype=F32, also_bf16=True)
    dz_b, dkv, dsum_b, dsinks = _attn_b_bwd(qb, kvx, gate_b, o_b, du_b, lse_b, base_b, sinks)
    ddiag_b = jnp.roll(dsum_b[..., ::-1], -1, axis=-1)
    dn2 = _matmul("dproj_b", dz_b, wb_in, dims=NT, grid=(nt,), parts=4,
                  a_spec=pl.BlockSpec((4, tm, half), lambda i: (0, i, 0)), b_spec=whole((4, d, half)),
                  o_spec=row, out_shape=(s, d), out_dtype=F32)
    d_wb_in = _matmul("dw_in_b", n2, dz_b, dims=TN, grid=(4,),
                      a_spec=whole((s, d)), b_spec=pl.BlockSpec((None, s, half), lambda j: (j, 0, 0)),
                      o_spec=pl.BlockSpec((None, d, half), lambda j: (j, 0, 0)),
                      out_shape=(4, d, half), out_dtype=F32, also_bf16=True)
    dnk = _matmul("dproj_kv", dkv, wkv, dims=NT, grid=(nt,),
                  a_spec=pl.BlockSpec((tm, kvw), lambda i: (i, 0)), b_spec=whole((d, kvw)), o_spec=row,
                  out_shape=(s, d), out_dtype=F32)
    d_wkv = _matmul("dw_kv", nk, dkv, dims=TN, grid=(1,),
                    a_spec=whole((s, d)), b_spec=whole((s, kvw)), o_spec=whole((d, kvw)),
                    out_shape=(d, kvw), out_dtype=F32, also_bf16=True)
    dh1, d_gkb = _norm_bwd("dnorm_kv_b", h1, dh2, [dnk, dn2], jnp.concatenate([gk, gb], axis=0))

    du_a = _matmul("dout_a", dh1, wa_out, dims=NT, grid=(nt,), a_spec=row, b_spec=whole((d, d)), o_spec=row,
                   out_shape=(s, d), out_dtype=F32)
    d_wa_out = _matmul("dw_out_a", u_a, dh1, dims=TN, grid=(2,),
                       a_spec=whole((s, d)), b_spec=pl.BlockSpec((s, half), lambda j: (0, j)),
                       o_spec=pl.BlockSpec((d, half), lambda j: (0, j)),
                       out_shape=(d, d), out_dtype=F32, also_bf16=True)
    early = dict(a_w_out=[g.reshape(4, d // 4, d) for g in d_wa_out],
                 kv_w=[g.reshape(4, d // 4, kvw) for g in d_wkv], b_w_in=list(d_wb_in),
                 b_w_out=[g.reshape(4, d // 4, d) for g in d_wb_out])
    (dz_a, ddiag_a), early_recv = _attn_a_bwd(
        zqkv, gate_a, o_a, du_a, lse_a, diag_a, hosted=_scatter_hosted([early[n][1] for n in early]))
    d_wa_in = _matmul("dw_in_a", n1, dz_a, dims=TN, grid=(4, 2),
                      a_spec=whole((s, d)), b_spec=pl.BlockSpec((None, s, half), lambda j, h: (j, 0, h)),
                      o_spec=pl.BlockSpec((None, d, half), lambda j, h: (j, 0, h)),
                      out_shape=(4, d, d), out_dtype=F32, also_bf16=True)
    tp = min(TM_PARTS, s)
    dn1, late_recv = _matmul("dproj_a", dz_a, wa_in, dims=NT, grid=(s // tp,), parts=4,
                             a_spec=pl.BlockSpec((4, tp, d), lambda i: (0, i, 0)), b_spec=whole((4, d, d)),
                             o_spec=pl.BlockSpec((tp, d), lambda i: (i, 0)),
                             out_shape=(s, d), out_dtype=F32, hosted=_scatter_hosted([d_wa_in[1]]))
    grad_x, d_ga = _norm_bwd("dnorm_a", x, dh1, [dn1], ga)

    small = dict(
        a_norm=d_ga, a_rel_bias=_diag_rows_grad(onehot_a, ddiag_a), kv_norm=d_gkb[0:1],
        t5_bias=_diag_rows_grad(onehot_b, ddiag_b), b_norm=d_gkb[1:2], b_sinks=dsinks[0:1, :HEADS],
        final_norm=d_gf)
    own = dict(a_w_in=d_wa_in[0], **{n: early[n][0] for n in early})
    received = dict(a_w_in=late_recv[0], **dict(zip(early, early_recv)))
    return loss, grad_x, small, own, received


SMALL = ("a_norm", "a_rel_bias", "kv_norm", "t5_bias", "b_norm", "b_sinks", "final_norm")
BIG = ("a_w_in", "a_w_out", "kv_w", "b_w_in", "b_w_out")
ORDER = ("a_norm", "a_w_in", "a_rel_bias", "a_w_out", "kv_norm", "kv_w", "t5_bias", "b_norm", "b_w_in",
         "b_sinks", "b_w_out", "final_norm")


def _pack(parts, rows):
    flat = jnp.concatenate([p.reshape(-1).astype(F32) for p in parts])
    return jnp.pad(flat, (0, rows * LANES - flat.shape[0])).reshape(rows, LANES)


def _unpack(block, shapes):
    flat = block.reshape(-1)
    out, at = [], 0
    for shp in shapes:
        size = int(np.prod(shp))
        out.append(flat[at:at + size].reshape(shp))
        at += size
    return out


def kernel(x, a_norm, a_w_in, a_rel_bias, a_w_out, kv_norm, kv_w, t5_bias, b_norm, b_w_in, b_sinks, b_w_out, final_norm, loss_target, m_a_norm, m_a_w_in, m_a_rel_bias, m_a_w_out, m_kv_norm, m_kv_w, m_t5_bias, m_b_norm, m_b_w_in, m_b_sinks, m_b_w_out, m_final_norm, v_a_norm, v_a_w_in, v_a_rel_bias, v_a_w_out, v_kv_norm, v_kv_w, v_t5_bias, v_b_norm, v_b_w_in, v_b_sinks, v_b_w_out, v_final_norm):
    w = dict(a_norm=a_norm, a_w_in=a_w_in, a_rel_bias=a_rel_bias, a_w_out=a_w_out, kv_norm=kv_norm, kv_w=kv_w,
             t5_bias=t5_bias, b_norm=b_norm, b_w_in=b_w_in, b_sinks=b_sinks, b_w_out=b_w_out,
             final_norm=final_norm)
    m = dict(a_norm=m_a_norm, a_w_in=m_a_w_in, a_rel_bias=m_a_rel_bias, a_w_out=m_a_w_out, kv_norm=m_kv_norm,
             kv_w=m_kv_w, t5_bias=m_t5_bias, b_norm=m_b_norm, b_w_in=m_b_w_in, b_sinks=m_b_sinks,
             b_w_out=m_b_w_out, final_norm=m_final_norm)
    v = dict(a_norm=v_a_norm, a_w_in=v_a_w_in, a_rel_bias=v_a_rel_bias, a_w_out=v_a_w_out, kv_norm=v_kv_norm,
             kv_w=v_kv_w, t5_bias=v_t5_bias, b_norm=v_b_norm, b_w_in=v_b_w_in, b_sinks=v_b_sinks,
             b_w_out=v_b_w_out, final_norm=v_final_norm)
    d = D_MODEL
    chip = 2 * lax.axis_index("x") + lax.axis_index("y")

    shard2d = dict(a_w_in=a_w_in[0], a_w_out=a_w_out[0], kv_w=kv_w, b_w_in=b_w_in[0], b_w_out=b_w_out[0])

    wa_in, ga = _run_alone("allgather_first",
                           _allgather_hosted([shard2d["a_w_in"].astype(BF16), a_norm], [True, False]))
    ga = ga.reshape(1, d)

    loss, grad_x, small, own, received = _local_step(
        x[0], loss_target[0], ga, wa_in, a_rel_bias[0], [shard2d[n].astype(BF16) for n in BIG[1:]],
        kv_norm.reshape(1, d), t5_bias, b_norm, b_sinks, final_norm.reshape(1, d))

    small_shapes = [small[n].shape for n in SMALL] + [(1, 1)]
    total = sum(int(np.prod(s)) for s in small_shapes)
    rows = -(-total // (8 * LANES)) * 8
    reduced = _unpack(_allreduce_small(_pack([small[n] for n in SMALL] + [loss], rows)), small_shapes)
    g_small = dict(zip(SMALL, reduced[:-1]))
    loss_out = reduced[-1].reshape(())
    g_small["a_norm"] = lax.dynamic_slice_in_dim(g_small["a_norm"], chip * (d // 4), d // 4, axis=1)

    core_sums = [
        _sum_partials("sum_" + n, lax.dynamic_index_in_dim(own[n], chip, 0, keepdims=False), received[n])
        for n in BIG]
    sibling_sums = _swap_with_sibling(core_sums)

    out = {}
    for n, mine, theirs in zip(BIG, core_sums, sibling_sums):
        res = _adamw("adamw_" + n, shard2d[n], m[n].reshape(shard2d[n].shape), v[n].reshape(shard2d[n].shape),
                     [mine, theirs])
        out[n] = [r.reshape(w[n].shape) for r in res]
    small_w_shapes = [w[n].shape for n in SMALL]
    total_w = sum(int(np.prod(s)) for s in small_w_shapes)
    rows_w = -(-total_w // (8 * LANES)) * 8
    packed = [_pack([t[n] for n in SMALL], rows_w) for t in (w, m, v)]
    g_packed = _pack([g_small[n] for n in SMALL], rows_w)
    res = _adamw("adamw_small", packed[0], packed[1], packed[2], [g_packed])
    unpacked = [_unpack(r, small_w_shapes) for r in res]
    for i, n in enumerate(SMALL):
        out[n] = [unpacked[k][i] for k in range(4)]

    grads = [out[n][0] for n in ORDER]
    deltas = [out[n][1] for n in ORDER]
    new_m = [out[n][2] for n in ORDER]
    new_v = [out[n][3] for n in ORDER]
    return (loss_out, grad_x[None], *grads, *deltas, *new_m, *new_v)
```

```python
import functools
import math

import jax
import jax.numpy as jnp
import numpy as np
from jax import lax
from jax.experimental import pallas as pl
from jax.experimental.pallas import tpu as pltpu

F32 = jnp.float32
BF16 = jnp.bfloat16
MESH = pl.DeviceIdType.MESH

D_MODEL = 1024
HEADS = 16
HEAD_DIM = 64
CHUNK = 64
RMS_EPS = 1e-6
SCALE = HEAD_DIM ** -0.5
A_LEFT_CHUNKS = 8
A_REL_CLIP = 256
B_LEFT_CHUNKS = 2
B_KV_HEADS = 2
B_GROUP = HEADS // B_KV_HEADS
T5_BUCKETS = 32
T5_MAX_DIST = 128
ADAM_LR = 0.001
ADAM_B1 = 0.9
ADAM_B2 = 0.999
ADAM_EPS = 1e-08
ADAM_WD = 0.01
ADAM_STEP = 10

MASKED = -1e30
LANES = 128
TQ = 128
A_PAIRS = 2
A_PAIRS_FWD = 4
KB = 128
A_KBLOCKS = A_LEFT_CHUNKS * CHUNK // KB + 1
B_KBLOCKS = B_LEFT_CHUNKS * CHUNK // KB + 1
A_WIN = A_KBLOCKS * KB
B_WIN = B_KBLOCKS * KB
TM = 512
TM_DENSE = 1024
TM_PARTS = 512
VMEM_LIMIT = 56 * 1024 * 1024

NT = (((1,), (1,)), ((), ()))
TN = (((0,), (0,)), ((), ()))
NN = (((1,), (0,)), ((), ()))


def _params(sem=None):
    return pltpu.CompilerParams(dimension_semantics=sem, vmem_limit_bytes=VMEM_LIMIT)


class _Hosted:
    def __init__(self, inputs, out_shapes, sems, first, middle, last):
        self.inputs, self.out_shapes, self.sems = list(inputs), list(out_shapes), list(sems)
        self.first, self.middle, self.last = first, middle, last


def _call(body, *, name, grid, in_specs, out_specs, out_shape, args, scratch_shapes=(), sem=None, hosted=None):
    in_specs, out_specs, out_shape = list(in_specs), list(out_specs), list(out_shape)
    scratch_shapes = list(scratch_shapes)
    if hosted is None:
        out = pl.pallas_call(
            body, name=name, grid=grid, in_specs=in_specs, out_specs=out_specs, out_shape=out_shape,
            scratch_shapes=scratch_shapes, compiler_params=_params(sem))(*args)
        return list(out), []
    n_in, n_out, n_scr = len(in_specs), len(out_shape), len(scratch_shapes)
    h_in, h_out = len(hosted.inputs), len(hosted.out_shapes)
    total = int(np.prod(grid)) if grid else 1

    def wrapped(*refs):
        ins, refs = refs[:n_in], refs[n_in:]
        h_ins, refs = refs[:h_in], refs[h_in:]
        outs, refs = refs[:n_out], refs[n_out:]
        h_outs, refs = refs[:h_out], refs[h_out:]
        scr, h_sems = refs[:n_scr], refs[n_scr:]
        step = 0
        for axis, size in enumerate(grid):
            step = step * size + pl.program_id(axis)

        @pl.when(step == 0)
        def _():
            hosted.first(h_ins, h_outs, h_sems)

        body(*ins, *outs, *scr)
        if hosted.middle is not None:
            @pl.when(step == total // 2)
            def _():
                hosted.middle(h_ins, h_outs, h_sems)

        @pl.when(step == total - 1)
        def _():
            hosted.last(h_ins, h_outs, h_sems)

    out = pl.pallas_call(
        wrapped, name=name, grid=grid, in_specs=in_specs + [ANY] * h_in, out_specs=out_specs + [ANY] * h_out,
        out_shape=out_shape + hosted.out_shapes, scratch_shapes=scratch_shapes + hosted.sems,
        compiler_params=_params(("arbitrary",) * len(grid)))(*args, *hosted.inputs)
    return list(out[:n_out]), list(out[n_out:])


def _matmul(name, a, b, *, dims, grid, a_spec, b_spec, o_spec, out_shape, out_dtype,
            parts=1, resid=None, resid_spec=None, also_bf16=False, hosted=None, zero_axis=None):
    def body(*refs):
        if zero_axis is None:
            product(*refs)
        else:
            @pl.when(pl.program_id(zero_axis) == 0)
            def _():
                refs[2][...] = jnp.zeros_like(refs[2])

            @pl.when(pl.program_id(zero_axis) > 0)
            def _():
                product(*refs)

    def product(*refs):
        a_ref, b_ref = refs[:2]
        r_ref = refs[2] if resid is not None else None
        o_ref = refs[3] if resid is not None else refs[2]
        if parts == 1:
            prod = lax.dot_general(a_ref[...].astype(BF16), b_ref[...].astype(BF16), dims,
                                   preferred_element_type=F32)
        else:
            prod = None
            for part in range(parts):
                term = lax.dot_general(a_ref[part].astype(BF16), b_ref[part].astype(BF16), dims,
                                       preferred_element_type=F32)
                prod = term if prod is None else prod + term
        if resid is not None:
            prod = r_ref[...] + prod
        o_ref[...] = prod.astype(out_dtype)
        if also_bf16:
            refs[-1][...] = prod.astype(BF16)

    in_specs = [a_spec, b_spec]
    args = [a, b]
    if resid is not None:
        in_specs.append(resid_spec)
        args.append(resid)
    sem = ["parallel"] * len(grid)
    out_specs = [o_spec]
    out_shapes = [jax.ShapeDtypeStruct(out_shape, out_dtype)]
    if also_bf16:
        out_specs.append(o_spec)
        out_shapes.append(jax.ShapeDtypeStruct(out_shape, BF16))
    out, extra = _call(body, name=name, grid=grid, in_specs=in_specs, out_specs=out_specs, out_shape=out_shapes,
                       args=args, sem=tuple(sem), hosted=hosted)
    res = out[0] if not also_bf16 else tuple(out)
    return res if hosted is None else (res, extra)


def _rms_rows(x):
    return lax.rsqrt(jnp.mean(x * x, axis=-1, keepdims=True) + RMS_EPS)


def _norm_fwd(name, x, gains):
    s, d = x.shape
    n = gains.shape[0]

    def body(x_ref, g_ref, *o_refs):
        xv = x_ref[...]
        xh = xv * _rms_rows(xv)
        for i in range(n):
            o_refs[i][...] = (xh * g_ref[i:i + 1, :]).astype(BF16)

    row = pl.BlockSpec((TM, d), lambda i: (i, 0))
    return pl.pallas_call(
        body, name=name, grid=(s // TM,),
        in_specs=[row, pl.BlockSpec((n, d), lambda i: (0, 0))],
        out_specs=[row] * n,
        out_shape=[jax.ShapeDtypeStruct((s, d), BF16)] * n,
        compiler_params=_params(("parallel",)),
    )(x, gains)


def _norm_bwd(name, x, dres, dns, gains):
    s, d = x.shape
    n = len(dns)

    def body(x_ref, r_ref, g_ref, *refs):
        dn_refs, dx_ref, dg_ref = refs[:n], refs[n], refs[n + 1]
        i = pl.program_id(0)
        xv = x_ref[...]
        r = _rms_rows(xv)
        xh = xv * r

        @pl.when(i == 0)
        def _():
            dg_ref[...] = jnp.zeros_like(dg_ref)

        a = None
        for j in range(n):
            dn = dn_refs[j][...]
            t = dn * g_ref[j:j + 1, :]
            a = t if a is None else a + t
            dg_ref[j:j + 1, :] += jnp.sum(dn * xh, axis=0, keepdims=True)
        dx_ref[...] = r_ref[...] + r * (a - xh * jnp.mean(xh * a, axis=-1, keepdims=True))

    row = pl.BlockSpec((TM, d), lambda i: (i, 0))
    small = pl.BlockSpec((n, d), lambda i: (0, 0))
    return pl.pallas_call(
        body, name=name, grid=(s // TM,),
        in_specs=[row, row, small] + [row] * n,
        out_specs=[row, small],
        out_shape=[jax.ShapeDtypeStruct((s, d), F32), jax.ShapeDtypeStruct((n, d), F32)],
        compiler_params=_params(("arbitrary",)),
    )(x, dres, gains, *dns)


def _loss_head(h2, target, gain):
    s, d = h2.shape

    def body(h_ref, t_ref, g_ref, dh_ref, loss_ref, dg_ref):
        i = pl.program_id(0)
        hv = h_ref[...]
        r = _rms_rows(hv)
        hh = hv * r
        g = g_ref[...]
        err = hh * g - t_ref[...]
        part = 0.5 * jnp.sum(jnp.sum(err * err, axis=-1, keepdims=True) * (1.0 / d), axis=0, keepdims=True)
        dy = err * (1.0 / d)
        a = dy * g
        dh_ref[...] = r * (a - hh * jnp.mean(hh * a, axis=-1, keepdims=True))
        dg = jnp.sum(dy * hh, axis=0, keepdims=True)

        @pl.when(i == 0)
        def _():
            loss_ref[...] = part
            dg_ref[...] = dg

        @pl.when(i > 0)
        def _():
            loss_ref[...] += part
            dg_ref[...] += dg

    row = pl.BlockSpec((TM, d), lambda i: (i, 0))
    return pl.pallas_call(
        body, name="loss_head", grid=(s // TM,),
        in_specs=[row, row, pl.BlockSpec((1, d), lambda i: (0, 0))],
        out_specs=[row, pl.BlockSpec((1, 1), lambda i: (0, 0)), pl.BlockSpec((1, d), lambda i: (0, 0))],
        out_shape=[jax.ShapeDtypeStruct((s, d), F32), jax.ShapeDtypeStruct((1, 1), F32),
                   jax.ShapeDtypeStruct((1, d), F32)],
        compiler_params=_params(("arbitrary",)),
    )(h2, target, gain)


def _silu_parts(g):
    sig = jax.nn.sigmoid(g)
    return g * sig, sig * (1.0 + g * (1.0 - sig))


def _lane_lo(rows):
    return lax.broadcasted_iota(jnp.int32, (rows, LANES), 1) < HEAD_DIM


def _stack_pair(x):
    lo = _lane_lo(x.shape[0])
    zero = jnp.zeros_like(x)
    return jnp.concatenate([jnp.where(lo, x, zero), jnp.where(lo, zero, x)], axis=0)


def _unstack_pair(y, w):
    return jnp.where(_lane_lo(w), y[:w], y[w:])


def _block_valid(b, left_blocks, width):
    col = lax.broadcasted_iota(jnp.int32, (1, 2 * width), 1)
    col = jnp.where(col >= width, col - width, col)
    return (col // KB + (b - left_blocks)) >= 0


def _toeplitz_tile(diag_row, width, left_chunks):
    wide = width + TQ
    rolled = pltpu.roll(jnp.broadcast_to(diag_row, (TQ, wide)), 1, 1, stride=1, stride_axis=0)
    i = lax.broadcasted_iota(jnp.int32, (TQ, width), 0) // CHUNK
    j = lax.broadcasted_iota(jnp.int32, (TQ, width), 1) // CHUNK
    dc = i + left_chunks - j
    return jnp.where((dc >= 0) & (dc <= left_chunks), rolled[:, TQ:], MASKED)


def _toeplitz_sum(tile, width):
    flip = (lax.broadcasted_iota(jnp.int32, (TQ, TQ), 0) + lax.broadcasted_iota(jnp.int32, (TQ, TQ), 1)
            == TQ - 1).astype(F32)
    reversed_rows = jnp.dot(flip, tile, precision=lax.Precision.HIGHEST, preferred_element_type=F32)
    padded = jnp.concatenate([reversed_rows, jnp.zeros((TQ, TQ), F32)], axis=1)
    rolled = pltpu.roll(padded, 0, 1, stride=1, stride_axis=0)
    return jnp.sum(rolled, axis=0, keepdims=True)


def _softmax_pair(sc, w, sink=None):
    ps, inv, lses = [], [], []
    for e in range(2):
        sh = sc[:, e * w:(e + 1) * w]
        m = jnp.max(sh, axis=-1, keepdims=True)
        if sink is not None:
            m = jnp.maximum(m, sink[e])
        ex = jnp.exp(sh - m)
        l = jnp.sum(ex, axis=-1, keepdims=True)
        if sink is not None:
            l = l + jnp.exp(sink[e] - m)
        ps.append(ex.astype(BF16))
        inv.append(1.0 / l)
        lses.append(m + jnp.log(l))
    return jnp.concatenate(ps, axis=-1), inv, lses


def _softmax_pair_bwd(sc, dp, lse, delta, w):
    ps, dss = [], []
    for e in range(2):
        p = jnp.exp(sc[:, e * w:(e + 1) * w] - lse[e])
        ps.append(p)
        dss.append(p * (dp[:, e * w:(e + 1) * w] - delta[e]))
    return jnp.concatenate(ps, axis=-1), jnp.concatenate(dss, axis=-1)


def _pair_rowsums(x, lo):
    zero = jnp.zeros_like(x)
    return (jnp.sum(jnp.where(lo, x, zero), axis=-1, keepdims=True),
            jnp.sum(jnp.where(lo, zero, x), axis=-1, keepdims=True))


def _a_qkv_specs(rows, pad, pw):
    return [pl.BlockSpec((None, TQ, pw), lambda p, b: (0, b + pad // TQ, p)),
            pl.BlockSpec((None, rows, pw), lambda p, b: (1, 0, p)),
            pl.BlockSpec((None, rows, pw), lambda p, b: (2, 0, p))]


def _window(ref, b, pad, win, lanes):
    start = pl.multiple_of(b * TQ + pad - (win - TQ), KB)
    return ref[pl.ds(start, win), lanes]


def _attn_a_fwd(zqkv, g, diag, hosted=None):
    s = g.shape[0]
    pad = zqkv.shape[1] - s
    nb = s // TQ
    left = A_KBLOCKS - 1
    pairs = A_PAIRS_FWD
    pw = pairs * LANES
    wide = A_WIN + TQ

    def body(q_ref, k_ref, v_ref, g_ref, diag_ref, o_ref, u_ref, lse_ref, bias_scr):
        b = pl.program_id(1)

        @pl.when(b == 0)
        def _():
            for hh in range(2 * pairs):
                bias_scr[hh // 2, :, (hh % 2) * A_WIN:(hh % 2 + 1) * A_WIN] = _toeplitz_tile(
                    diag_ref[hh], A_WIN, A_LEFT_CHUNKS)

        def step(first_blocks):
            lo = _lane_lo(TQ)
            for pp in range(pairs):
                ln = slice(pp * LANES, (pp + 1) * LANES)
                kcat = _stack_pair(_window(k_ref, b, pad, A_WIN, ln))
                vcat = _stack_pair(_window(v_ref, b, pad, A_WIN, ln))
                sc = lax.dot_general(q_ref[:, ln] * SCALE, kcat, NT, preferred_element_type=F32) + bias_scr[pp]
                if first_blocks:
                    sc = jnp.where(_block_valid(b, left, A_WIN), sc, MASKED)
                p, inv, lses = _softmax_pair(sc, A_WIN)
                ov = jnp.dot(p, vcat, preferred_element_type=F32) * jnp.where(lo, inv[0], inv[1])
                o_ref[:, ln] = ov
                lse_ref[pp] = jnp.where(lo, lses[0], lses[1])
                sg, _ = _silu_parts(g_ref[:, ln])
                u_ref[:, ln] = (ov * sg).astype(BF16)

        @pl.when(b < left)
        def _():
            step(True)

        @pl.when(b >= left)
        def _():
            step(False)

    tile = pl.BlockSpec((TQ, pw), lambda p, b: (b, p))
    return _call(
        body, name="attn_a_fwd", grid=(HEADS // 2 // pairs, nb),
        in_specs=_a_qkv_specs(pad + s, pad, pw) + [
            tile, pl.BlockSpec((2 * pairs, 1, wide), lambda p, b: (p, 0, 0))],
        out_specs=[tile, tile, pl.BlockSpec((pairs, TQ, LANES), lambda p, b: (p, b, 0))],
        out_shape=[jax.ShapeDtypeStruct((s, D_MODEL), F32), jax.ShapeDtypeStruct((s, D_MODEL), BF16),
                   jax.ShapeDtypeStruct((HEADS // 2, s, LANES), F32)],
        scratch_shapes=[pltpu.VMEM((pairs, TQ, 2 * A_WIN), F32)],
        sem=("parallel", "arbitrary"), hosted=hosted,
        args=(zqkv, zqkv, zqkv, g, diag))


def _attn_a_bwd(zqkv, g, o, du, lse, diag, hosted=None):
    s = g.shape[0]
    pad = zqkv.shape[1] - s
    nb = s // TQ
    left = A_KBLOCKS - 1
    pw = A_PAIRS * LANES
    wide = A_WIN + TQ

    def body(q_ref, k_ref, v_ref, g_ref, o_ref, du_ref, lse_ref, diag_ref, dz_ref, ddiag_ref,
             bias_scr, dbias_acc, dk_acc, dv_acc):
        b = pl.program_id(1)

        @pl.when(b == 0)
        def _():
            for hh in range(2 * A_PAIRS):
                bias_scr[hh // 2, :, (hh % 2) * A_WIN:(hh % 2 + 1) * A_WIN] = _toeplitz_tile(
                    diag_ref[hh], A_WIN, A_LEFT_CHUNKS)
            dbias_acc[...] = jnp.zeros_like(dbias_acc)
            dk_acc[...] = jnp.zeros_like(dk_acc)
            dv_acc[...] = jnp.zeros_like(dv_acc)

        def step(first_blocks):
            lo = _lane_lo(TQ)
            upper = lax.broadcasted_iota(jnp.int32, (LANES, A_WIN), 0) < HEAD_DIM
            rows = pl.ds(pl.multiple_of(b * TQ, TQ), TQ)
            sg, dsg = _silu_parts(g_ref[...])
            duv = du_ref[...]
            ov = o_ref[...]
            do = duv * sg
            dz_ref[3, rows, :] = (duv * ov * dsg).astype(BF16)
            do_o = do * ov
            do_bf = do.astype(BF16)
            for pp in range(A_PAIRS):
                ln = slice(pp * LANES, (pp + 1) * LANES)
                q = q_ref[:, ln] * SCALE
                kcat = _stack_pair(_window(k_ref, b, pad, A_WIN, ln))
                vcat = _stack_pair(_window(v_ref, b, pad, A_WIN, ln))
                sc = lax.dot_general(q, kcat, NT, preferred_element_type=F32) + bias_scr[pp]
                if first_blocks:
                    sc = jnp.where(_block_valid(b, left, A_WIN), sc, MASKED)
                lse_t = lse_ref[pp]
                dp = lax.dot_general(do_bf[:, ln], vcat, NT, preferred_element_type=F32)
                p, ds = _softmax_pair_bwd(sc, dp, (lse_t[:, 0:1], lse_t[:, HEAD_DIM:HEAD_DIM + 1]),
                                          _pair_rowsums(do_o[:, ln], lo), A_WIN)
                dbias_acc[pp] += ds
                dsb = ds.astype(BF16)
                dz_ref[0, rows, ln] = (jnp.dot(dsb, kcat, preferred_element_type=F32) * SCALE).astype(BF16)
                dkt = lax.dot_general(q, dsb, TN, preferred_element_type=F32)
                dvt = lax.dot_general(do_bf[:, ln], p.astype(BF16), TN, preferred_element_type=F32)
                dkt = jnp.where(upper, dkt[:, :A_WIN], dkt[:, A_WIN:])
                dvt = jnp.where(upper, dvt[:, :A_WIN], dvt[:, A_WIN:])
                for t in range(A_KBLOCKS):
                    blk = b + (pad // KB - left + t)
                    dk_acc[blk, ln, :] += dkt[:, t * KB:(t + 1) * KB]
                    dv_acc[blk, ln, :] += dvt[:, t * KB:(t + 1) * KB]

        @pl.when(b < left)
        def _():
            step(True)

        @pl.when(b >= left)
        def _():
            step(False)

        @pl.when(b == nb - 1)
        def _():
            for kb in range(s // KB):
                dz_ref[1, kb * KB:(kb + 1) * KB, :] = dk_acc[pad // KB + kb].T.astype(BF16)
                dz_ref[2, kb * KB:(kb + 1) * KB, :] = dv_acc[pad // KB + kb].T.astype(BF16)
            for hh in range(2 * A_PAIRS):
                ddiag_ref[hh] = _toeplitz_sum(
                    dbias_acc[hh // 2, :, (hh % 2) * A_WIN:(hh % 2 + 1) * A_WIN], A_WIN)

    tile = pl.BlockSpec((TQ, pw), lambda p, b: (b, p))
    diag_spec = pl.BlockSpec((2 * A_PAIRS, 1, wide), lambda p, b: (p, 0, 0))
    return _call(
        body, name="attn_a_bwd", grid=(HEADS // 2 // A_PAIRS, nb),
        in_specs=_a_qkv_specs(pad + s, pad, pw) + [
            tile, tile, tile, pl.BlockSpec((A_PAIRS, TQ, LANES), lambda p, b: (p, b, 0)), diag_spec],
        out_specs=[pl.BlockSpec((4, s, pw), lambda p, b: (0, 0, p)), diag_spec],
        out_shape=[jax.ShapeDtypeStruct((4, s, D_MODEL), BF16),
                   jax.ShapeDtypeStruct((HEADS, 1, wide), F32)],
        scratch_shapes=[pltpu.VMEM((A_PAIRS, TQ, 2 * A_WIN), F32), pltpu.VMEM((A_PAIRS, TQ, 2 * A_WIN), F32),
                        pltpu.VMEM(((pad + s) // KB, pw, KB), F32), pltpu.VMEM(((pad + s) // KB, pw, KB), F32)],
        sem=("parallel", "arbitrary"), hosted=hosted,
        args=(zqkv, zqkv, zqkv, g, o, du, lse, diag))


B_STACK = B_GROUP // 2
B_KVX = 4 * LANES
B_ROWS = B_STACK * TQ
B_WIDE = B_WIN + TQ


def _b_head_place(h):
    return h // B_GROUP, (h % B_GROUP) // 2, h % 2


def _toeplitz_tile_t(base_row, width, left_chunks):
    wide = width + TQ
    rolled = pltpu.roll(jnp.broadcast_to(base_row, (width, wide)), 0, 1, stride=1, stride_axis=0)
    j = lax.broadcasted_iota(jnp.int32, (width, TQ), 0) // CHUNK
    i = lax.broadcasted_iota(jnp.int32, (width, TQ), 1) // CHUNK
    dc = i + left_chunks - j
    return jnp.where((dc >= 0) & (dc <= left_chunks), rolled[:, :TQ], MASKED)


def _toeplitz_sum_t(tile_t, width):
    flip = (lax.broadcasted_iota(jnp.int32, (width, width), 0) + lax.broadcasted_iota(jnp.int32, (width, width), 1)
            == width - 1).astype(F32)
    reversed_rows = jnp.dot(flip, tile_t, precision=lax.Precision.HIGHEST, preferred_element_type=F32)
    padded = jnp.concatenate([reversed_rows, jnp.zeros((width, width), F32)], axis=1)
    rolled = pltpu.roll(padded, 0, 1, stride=1, stride_axis=0)
    return jnp.sum(rolled, axis=0, keepdims=True)


def _b_build_bias(base_ref, bias_scr):
    for h in range(HEADS):
        gi, pr, e = _b_head_place(h)
        bias_scr[gi, e * B_WIN:(e + 1) * B_WIN, pr * TQ:(pr + 1) * TQ] = _toeplitz_tile_t(
            base_ref[h], B_WIN, B_LEFT_CHUNKS)


def _b_stack(x, gi):
    return jnp.concatenate(
        [x[:, (B_STACK * gi + pr) * LANES:(B_STACK * gi + pr + 1) * LANES] for pr in range(B_STACK)], axis=0)


def _b_sink_rows(sink_ref, gi):
    block = lax.broadcasted_iota(jnp.int32, (1, B_ROWS), 1) // TQ
    rows = []
    for e in range(2):
        row = jnp.zeros((1, B_ROWS), F32)
        for pr in range(B_STACK):
            h = B_GROUP * gi + 2 * pr + e
            row = jnp.where(block == pr, sink_ref[0:1, h:h + 1], row)
        rows.append(row)
    return rows


def _b_scores_t(q_ref, kvv, bias_scr, gi, b, left, first_blocks):
    kcat = _stack_pair(kvv[:, gi * LANES:(gi + 1) * LANES])
    vcat = _stack_pair(kvv[:, (B_KV_HEADS + gi) * LANES:(B_KV_HEADS + gi + 1) * LANES])
    qs = _b_stack(q_ref, gi) * SCALE
    sc = lax.dot_general(kcat, qs, NT, preferred_element_type=F32) + bias_scr[gi]
    if first_blocks:
        row = lax.broadcasted_iota(jnp.int32, (2 * B_WIN, 1), 0)
        row = jnp.where(row >= B_WIN, row - B_WIN, row)
        sc = jnp.where((row // KB + (b - left)) >= 0, sc, MASKED)
    return kcat, vcat, qs, sc


def _attn_b_fwd(qb, kvx, gate, base, sinks):
    s = qb.shape[0]
    pad = kvx.shape[0] - s
    nb = s // TQ
    left = B_KBLOCKS - 1

    def body(q_ref, kv_ref, g_ref, base_ref, sink_ref, o_ref, u_ref, lse_ref, bias_scr):
        b = pl.program_id(0)

        @pl.when(b == 0)
        def _():
            _b_build_bias(base_ref, bias_scr)

        def step(first_blocks):
            kvv = _window(kv_ref, b, pad, B_WIN, slice(None))
            upper = lax.broadcasted_iota(jnp.int32, (LANES, B_ROWS), 0) < HEAD_DIM
            lse_rows = []
            for gi in range(B_KV_HEADS):
                kcat, vcat, qs, sc = _b_scores_t(q_ref, kvv, bias_scr, gi, b, left, first_blocks)
                sink = _b_sink_rows(sink_ref, gi)
                ps, inv = [], []
                for e in range(2):
                    sh = sc[e * B_WIN:(e + 1) * B_WIN]
                    m = jnp.maximum(jnp.max(sh, axis=0, keepdims=True), sink[e])
                    ex = jnp.exp(sh - m)
                    l = jnp.sum(ex, axis=0, keepdims=True) + jnp.exp(sink[e] - m)
                    ps.append(ex.astype(BF16))
                    inv.append(1.0 / l)
                    lse_rows.append(m + jnp.log(l))
                pt = jnp.concatenate(ps, axis=0)
                ot = lax.dot_general(vcat, pt, TN, preferred_element_type=F32) * jnp.where(upper, inv[0], inv[1])
                ov = ot.T
                for pr in range(B_STACK):
                    pair = B_STACK * gi + pr
                    o_ref[:, pair * LANES:(pair + 1) * LANES] = ov[pr * TQ:(pr + 1) * TQ]
            lse_ref[0] = jnp.concatenate(lse_rows + [jnp.zeros((8 - len(lse_rows), B_ROWS), F32)], axis=0)
            sg, _ = _silu_parts(g_ref[...])
            u_ref[...] = (o_ref[...] * sg).astype(BF16)

        @pl.when(b < left)
        def _():
            step(True)

        @pl.when(b >= left)
        def _():
            step(False)

    row = pl.BlockSpec((TQ, D_MODEL), lambda b: (b, 0))
    return pl.pallas_call(
        body, name="attn_b_fwd", grid=(nb,),
        in_specs=[row, pl.BlockSpec((pad + s, B_KVX), lambda b: (0, 0)), row,
                  pl.BlockSpec((HEADS, 1, B_WIDE), lambda b: (0, 0, 0)), pl.BlockSpec((1, HEADS), lambda b: (0, 0))],
        out_specs=[row, row, pl.BlockSpec((1, 8, B_ROWS), lambda b: (b, 0, 0))],
        out_shape=[jax.ShapeDtypeStruct((s, D_MODEL), F32), jax.ShapeDtypeStruct((s, D_MODEL), BF16),
                   jax.ShapeDtypeStruct((nb, 8, B_ROWS), F32)],
        scratch_shapes=[pltpu.VMEM((B_KV_HEADS, 2 * B_WIN, B_ROWS), F32)],
        compiler_params=_params(("arbitrary",)),
    )(qb, kvx, gate, base, sinks)


def _attn_b_bwd(qb, kvx, gate, o, du, lse, base, sinks):
    s = qb.shape[0]
    pad = kvx.shape[0] - s
    nb = s // TQ
    left = B_KBLOCKS - 1
    half = D_MODEL // 2

    def body(q_ref, kv_ref, g_ref, o_ref, du_ref, lse_ref, base_ref, sink_ref, dz_ref, dkv_ref, dsum_ref,
             dsink_ref, bias_scr, dbias_acc, dkv_acc, dsink_acc):
        b = pl.program_id(0)

        @pl.when(b == 0)
        def _():
            _b_build_bias(base_ref, bias_scr)
            dbias_acc[...] = jnp.zeros_like(dbias_acc)
            dkv_acc[...] = jnp.zeros_like(dkv_acc)
            dsink_acc[...] = jnp.zeros_like(dsink_acc)

        def step(first_blocks):
            kvv = _window(kv_ref, b, pad, B_WIN, slice(None))
            sg, dsg = _silu_parts(g_ref[...])
            duv = du_ref[...]
            ov = o_ref[...]
            do = duv * sg
            dgate = (duv * ov * dsg).astype(BF16)
            dz_ref[2] = dgate[:, :half]
            dz_ref[3] = dgate[:, half:]
            do_o = do * ov
            do_bf = do.astype(BF16)
            lane = lax.broadcasted_iota(jnp.int32, (8, LANES), 1)
            sub = lax.broadcasted_iota(jnp.int32, (8, LANES), 0)
            halves = (((sub == 0) & (lane < HEAD_DIM)) | ((sub == 1) & (lane >= HEAD_DIM))).astype(F32)
            lse_all = lse_ref[0]
            dsink_rows = []
            for gi in range(B_KV_HEADS):
                kcat, vcat, qs, sc = _b_scores_t(q_ref, kvv, bias_scr, gi, b, left, first_blocks)
                dos = _b_stack(do_bf, gi)
                delta = lax.dot_general(halves, _b_stack(do_o, gi), NT, precision=lax.Precision.HIGHEST,
                                        preferred_element_type=F32)
                sink = _b_sink_rows(sink_ref, gi)
                dp = lax.dot_general(vcat, dos, NT, preferred_element_type=F32)
                ps, dss = [], []
                for e in range(2):
                    lse_e = lse_all[2 * gi + e:2 * gi + e + 1]
                    delta_e = delta[e:e + 1]
                    p = jnp.exp(sc[e * B_WIN:(e + 1) * B_WIN] - lse_e)
                    ps.append(p.astype(BF16))
                    dss.append(p * (dp[e * B_WIN:(e + 1) * B_WIN] - delta_e))
                    dsink_rows.append(-jnp.exp(sink[e] - lse_e) * delta_e)
                ds = jnp.concatenate(dss, axis=0)
                dbias_acc[gi] += ds
                dsb = ds.astype(BF16)
                dq = (lax.dot_general(kcat, dsb, TN, preferred_element_type=F32) * SCALE).T.astype(BF16)
                for pr in range(B_STACK):
                    dz_ref[gi, :, pr * LANES:(pr + 1) * LANES] = dq[pr * TQ:(pr + 1) * TQ]
                dk = _unstack_pair(jnp.dot(dsb, qs, preferred_element_type=F32), B_WIN)
                dv = _unstack_pair(jnp.dot(jnp.concatenate(ps, axis=0), dos, preferred_element_type=F32), B_WIN)
                krows = pl.ds(pl.multiple_of(b * TQ + pad - (B_WIN - TQ), KB), B_WIN)
                dkv_acc[krows, gi * LANES:(gi + 1) * LANES] += dk
                dkv_acc[krows, (B_KV_HEADS + gi) * LANES:(B_KV_HEADS + gi + 1) * LANES] += dv
            dsink_acc[...] += jnp.concatenate(
                dsink_rows + [jnp.zeros((8 - len(dsink_rows), B_ROWS), F32)], axis=0)

        @pl.when(b < left)
        def _():
            step(True)

        @pl.when(b >= left)
        def _():
            step(False)

        @pl.when(b == nb - 1)
        def _():
            lo_s = _lane_lo(s)
            for which in range(2):
                folded = []
                for gi in range(B_KV_HEADS):
                    part = dkv_acc[pad:pad + s, (which * B_KV_HEADS + gi) * LANES:(which * B_KV_HEADS + gi + 1) * LANES]
                    folded.append(part + pltpu.roll(part, HEAD_DIM, 1))
                dkv_ref[:, which * LANES:(which + 1) * LANES] = jnp.where(lo_s, folded[0], folded[1]).astype(BF16)
            lane8 = lax.broadcasted_iota(jnp.int32, dsink_ref.shape, 1)
            tot = jnp.zeros(dsink_ref.shape, F32)
            for h in range(HEADS):
                gi, pr, e = _b_head_place(h)
                dsum_ref[h] = _toeplitz_sum_t(
                    dbias_acc[gi, e * B_WIN:(e + 1) * B_WIN, pr * TQ:(pr + 1) * TQ], B_WIN)
                per_query = dsink_acc[2 * gi + e:2 * gi + e + 1, pr * TQ:(pr + 1) * TQ]
                tot = jnp.where(lane8 == h, jnp.sum(per_query, axis=1, keepdims=True), tot)
            dsink_ref[...] = tot

    row = pl.BlockSpec((TQ, D_MODEL), lambda b: (b, 0))
    base_spec = pl.BlockSpec((HEADS, 1, B_WIDE), lambda b: (0, 0, 0))
    return pl.pallas_call(
        body, name="attn_b_bwd", grid=(nb,),
        in_specs=[row, pl.BlockSpec((pad + s, B_KVX), lambda b: (0, 0)), row, row, row,
                  pl.BlockSpec((1, 8, B_ROWS), lambda b: (b, 0, 0)), base_spec,
                  pl.BlockSpec((1, HEADS), lambda b: (0, 0))],
        out_specs=[pl.BlockSpec((4, TQ, half), lambda b: (0, b, 0)),
                   pl.BlockSpec((s, 2 * LANES), lambda b: (0, 0)), base_spec,
                   pl.BlockSpec((8, LANES), lambda b: (0, 0))],
        out_shape=[jax.ShapeDtypeStruct((4, s, half), BF16), jax.ShapeDtypeStruct((s, 2 * LANES), BF16),
                   jax.ShapeDtypeStruct((HEADS, 1, B_WIDE), F32), jax.ShapeDtypeStruct((8, LANES), F32)],
        scratch_shapes=[pltpu.VMEM((B_KV_HEADS, 2 * B_WIN, B_ROWS), F32),
                        pltpu.VMEM((B_KV_HEADS, 2 * B_WIN, B_ROWS), F32),
                        pltpu.VMEM((pad + s, B_KVX), F32), pltpu.VMEM((8, B_ROWS), F32)],
        compiler_params=_params(("arbitrary",)),
    )(qb, kvx, gate, o, du, lse, base, sinks)


def _t5_bucket(rel):
    nb = T5_BUCKETS // 2
    max_exact = nb // 2
    ret = jnp.where(rel > 0, nb, 0)
    n = jnp.abs(rel)
    nf = jnp.maximum(n, 1).astype(jnp.float32)
    large = max_exact + (jnp.log(nf / max_exact) / math.log(T5_MAX_DIST / max_exact)
                         * (nb - max_exact)).astype(jnp.int32)
    large = jnp.minimum(large, nb - 1)
    return ret + jnp.where(n < max_exact, n, large)


def _a_offset_onehot():
    c = np.arange(A_WIN + TQ)
    dist = A_LEFT_CHUNKS * CHUNK + TQ - 1 - c
    idx = np.clip(dist, -A_REL_CLIP, A_REL_CLIP) + A_REL_CLIP
    onehot = np.zeros((A_WIN + TQ, 2 * A_REL_CLIP + 1), np.float32)
    onehot[c, idx] = 1.0
    return jnp.asarray(onehot)


def _b_offset_onehot():
    c = jnp.arange(B_WIN + TQ, dtype=jnp.int32)
    rel = c - (TQ - 1) - B_LEFT_CHUNKS * CHUNK
    return (_t5_bucket(rel)[:, None] == jnp.arange(T5_BUCKETS)[None, :]).astype(F32)


def _diag_rows(onehot, table):
    rows = jnp.dot(onehot, table.astype(F32), precision=lax.Precision.HIGHEST)
    return rows.T.reshape(HEADS, 1, onehot.shape[0])


def _diag_rows_grad(onehot, ddiag):
    return jnp.dot(ddiag.reshape(HEADS, onehot.shape[0]), onehot, precision=lax.Precision.HIGHEST).T


def _position():
    x, y, c = lax.axis_index("x"), lax.axis_index("y"), lax.axis_index("c")
    chips = [(1 - x, y), (x, 1 - y), (1 - x, 1 - y)]
    return x, y, c, chips


ANY = pl.BlockSpec(memory_space=pl.ANY)


def _allgather_hosted(shards, split):
    n = len(shards)

    def part(ref, t, half):
        if not split[t]:
            return ref
        rows = shards[t].shape[0] // 2
        return ref.at[pl.ds(half * rows, rows)]

    def copies(kind, ins, outs, sems):
        send_sems, recv_sems, pass_send, pass_recv, local_sems = sems
        x, y, c, chips = _position()
        mine = 2 * x + y
        if kind == "local":
            return [pltpu.make_async_copy(ins[t], outs[t].at[mine], local_sems.at[t]) for t in range(n)]
        made = []
        for t in range(n):
            for j, chip in enumerate(chips):
                theirs = 2 * chip[0] + chip[1]
                far = dict(send_sem=send_sems.at[3 * t + j], recv_sem=recv_sems.at[3 * t + j],
                           device_id=(chip[0], chip[1], c), device_id_type=MESH)
                near = dict(send_sem=pass_send.at[3 * t + j], recv_sem=pass_recv.at[3 * t + j],
                            device_id=(x, y, 1 - c), device_id_type=MESH)
                here = part(outs[t].at[theirs], t, c)
                if kind == "send":
                    made.append(pltpu.make_async_remote_copy(
                        src_ref=part(ins[t], t, c), dst_ref=part(outs[t].at[mine], t, c), **far))
                elif kind == "landed":
                    made.append(pltpu.make_async_remote_copy(src_ref=here, dst_ref=here, **far))
                elif not split[t]:
                    made.append(None)
                elif kind == "pass":
                    made.append(pltpu.make_async_remote_copy(src_ref=here, dst_ref=here, **near))
                else:
                    other = part(outs[t].at[theirs], t, 1 - c)
                    made.append(pltpu.make_async_remote_copy(src_ref=other, dst_ref=other, **near))
        return made

    def first(ins, outs, sems):
        for cp in copies("local", ins, outs, sems) + copies("send", ins, outs, sems):
            cp.start()

    def middle(ins, outs, sems):
        for got, cp in zip(copies("landed", ins, outs, sems), copies("pass", ins, outs, sems)):
            got.wait_recv()
            if cp is not None:
                cp.start()

    def last(ins, outs, sems):
        for cp in copies("passed", ins, outs, sems):
            if cp is not None:
                cp.wait_recv()
        for cp in copies("send", ins, outs, sems) + copies("pass", ins, outs, sems):
            if cp is not None:
                cp.wait_send()
        for cp in copies("local", ins, outs, sems):
            cp.wait()

    return _Hosted(shards, [jax.ShapeDtypeStruct((4,) + w.shape, w.dtype) for w in shards],
                   [pltpu.SemaphoreType.DMA((3 * n,))] * 4 + [pltpu.SemaphoreType.DMA((n,))],
                   first, middle, last)


def _scatter_hosted(grads):
    n = len(grads)

    def copies(ins, outs, sems):
        send_sems, recv_sems = sems
        x, y, c, chips = _position()
        return [pltpu.make_async_remote_copy(
            src_ref=ins[t].at[2 * chip[0] + chip[1]], dst_ref=outs[t].at[j],
            send_sem=send_sems.at[3 * t + j], recv_sem=recv_sems.at[3 * t + j],
            device_id=(chip[0], chip[1], c), device_id_type=MESH)
            for t in range(n) for j, chip in enumerate(chips)]

    def first(ins, outs, sems):
        for cp in copies(ins, outs, sems):
            cp.start()

    def last(ins, outs, sems):
        for cp in copies(ins, outs, sems):
            cp.wait()

    return _Hosted(grads, [jax.ShapeDtypeStruct((3,) + g.shape[1:], g.dtype) for g in grads],
                   [pltpu.SemaphoreType.DMA((3 * n,))] * 2, first, None, last)


def _run_alone(name, hosted):
    n_in = len(hosted.inputs)
    n_out = len(hosted.out_shapes)

    def body(*refs):
        ins, outs, sems = refs[:n_in], refs[n_in:n_in + n_out], refs[n_in + n_out:]
        hosted.first(ins, outs, sems)
        if hosted.middle is not None:
            hosted.middle(ins, outs, sems)
        hosted.last(ins, outs, sems)

    return pl.pallas_call(
        body, name=name, in_specs=[ANY] * n_in, out_specs=[ANY] * n_out, out_shape=hosted.out_shapes,
        scratch_shapes=hosted.sems)(*hosted.inputs)


def _swap_with_sibling(blocks):
    n = len(blocks)

    def body(*refs):
        ins, outs = refs[:n], refs[n:2 * n]
        send_sems, recv_sems = refs[2 * n:]
        x, y, c, _ = _position()
        sends = [pltpu.make_async_remote_copy(
            src_ref=ins[t], dst_ref=outs[t], send_sem=send_sems.at[t], recv_sem=recv_sems.at[t],
            device_id=(x, y, 1 - c), device_id_type=MESH) for t in range(n)]
        for cp in sends:
            cp.start()
        for cp in sends:
            cp.wait()

    return pl.pallas_call(
        body, name="swap_with_sibling",
        in_specs=[ANY] * n, out_specs=[ANY] * n,
        out_shape=[jax.ShapeDtypeStruct(b.shape, b.dtype) for b in blocks],
        scratch_shapes=[pltpu.SemaphoreType.DMA((n,))] * 2,
    )(*blocks)


def _allreduce_small(block):
    rows = block.shape[0]

    def body(in_ref, sum_ref, all_ref, send_sems, recv_sems):
        x, y, c, _ = _position()
        me = 4 * x + 2 * y + c
        all_ref[me] = in_ref[...]
        sends = []
        for k in range(1, 8):
            peer = (x ^ (k >> 2), y ^ ((k >> 1) & 1), c ^ (k & 1))
            sends.append(pltpu.make_async_remote_copy(
                src_ref=in_ref, dst_ref=all_ref.at[me], send_sem=send_sems.at[k - 1],
                recv_sem=recv_sems.at[k - 1], device_id=peer, device_id_type=MESH))
        for cp in sends:
            cp.start()
        for k in range(1, 8):
            theirs = me ^ k
            pltpu.make_async_remote_copy(
                src_ref=in_ref, dst_ref=all_ref.at[theirs], send_sem=send_sems.at[k - 1],
                recv_sem=recv_sems.at[k - 1], device_id=(x, y, c), device_id_type=MESH).wait_recv()
        for cp in sends:
            cp.wait_send()
        acc = all_ref[0]
        for d in range(1, 8):
            acc = acc + all_ref[d]
        sum_ref[...] = acc

    vmem = pl.BlockSpec(memory_space=pltpu.VMEM)
    return pl.pallas_call(
        body, name="allreduce_small",
        in_specs=[vmem], out_specs=[vmem, vmem],
        out_shape=[jax.ShapeDtypeStruct((rows, LANES), F32), jax.ShapeDtypeStruct((8, rows, LANES), F32)],
        scratch_shapes=[pltpu.SemaphoreType.DMA((7,))] * 2,
    )(block)[0]


def _adamw_math(w, g, m, v):
    m = ADAM_B1 * m + (1.0 - ADAM_B1) * g
    v = ADAM_B2 * v + (1.0 - ADAM_B2) * (g * g)
    m_hat = m / (1.0 - ADAM_B1 ** ADAM_STEP)
    v_hat = v / (1.0 - ADAM_B2 ** ADAM_STEP)
    delta = -ADAM_LR * (m_hat / (jnp.sqrt(v_hat) + ADAM_EPS) + ADAM_WD * w)
    return delta, m, v


def _row_tile(rows):
    return min(rows, 256)


def _sum_partials(name, own, recv):
    rows, cols = own.shape
    tr = _row_tile(rows)

    def body(own_ref, recv_ref, o_ref):
        acc = own_ref[...]
        for j in range(3):
            acc = acc + recv_ref[j].astype(F32)
        o_ref[...] = acc

    return pl.pallas_call(
        body, name=name, grid=(rows // tr,),
        in_specs=[pl.BlockSpec((tr, cols), lambda i: (i, 0)), pl.BlockSpec((3, tr, cols), lambda i: (0, i, 0))],
        out_specs=pl.BlockSpec((tr, cols), lambda i: (i, 0)),
        out_shape=jax.ShapeDtypeStruct((rows, cols), F32),
        compiler_params=_params(("parallel",)),
    )(own, recv)


def _adamw(name, w, m, v, g_parts):
    rows, cols = w.shape
    tr = _row_tile(rows)
    n = len(g_parts)

    def body(w_ref, m_ref, v_ref, *refs):
        g_refs = refs[:n]
        go_ref, d_ref, mo_ref, vo_ref = refs[n:]
        g = g_refs[0][...]
        for r in g_refs[1:]:
            g = g + r[...]
        delta, mn, vn = _adamw_math(w_ref[...], g, m_ref[...], v_ref[...])
        go_ref[...] = g
        d_ref[...] = delta
        mo_ref[...] = mn
        vo_ref[...] = vn

    spec = pl.BlockSpec((tr, cols), lambda i: (i, 0))
    return pl.pallas_call(
        body, name=name, grid=(rows // tr,),
        in_specs=[spec] * (3 + n), out_specs=[spec] * 4,
        out_shape=[jax.ShapeDtypeStruct((rows, cols), F32)] * 4,
        compiler_params=_params(("parallel",)),
    )(w, m, v, *g_parts)


def _local_step(x, target, ga, wa_in, rel_bias, later_shards, gk, t5, gb, sinks, gf):
    s, d = x.shape
    tm = min(TM_DENSE, s)
    nt = s // tm
    half = d // 2
    row = pl.BlockSpec((tm, d), lambda i: (i, 0))
    whole = lambda shape: pl.BlockSpec(shape, lambda *_: (0,) * len(shape))

    n1, = _norm_fwd("norm_a", x, ga)
    zqkv = _matmul("proj_a_qkv", n1, wa_in, dims=NN, grid=(3, nt + 1), zero_axis=1,
                   a_spec=pl.BlockSpec((tm, d), lambda j, i: (jnp.maximum(i - 1, 0), 0)),
                   b_spec=pl.BlockSpec((None, d, d), lambda j, i: (j, 0, 0)),
                   o_spec=pl.BlockSpec((None, tm, d), lambda j, i: (j, i, 0)),
                   out_shape=(3, tm + s, d), out_dtype=BF16)
    gate_a = _matmul("proj_a_gate", n1, wa_in, dims=NN, grid=(nt,),
                     a_spec=row, b_spec=pl.BlockSpec((None, d, d), lambda i: (3, 0, 0)), o_spec=row,
                     out_shape=(s, d), out_dtype=F32)
    onehot_a = _a_offset_onehot()
    diag_a = _diag_rows(onehot_a, rel_bias)
    (o_a, u_a, lse_a), gathered = _attn_a_fwd(
        zqkv, gate_a, diag_a, hosted=_allgather_hosted(later_shards, [True] * len(later_shards)))
    wa_out, wkv, wb_in, wb_out = gathered
    wa_out = wa_out.reshape(d, d)
    wkv = wkv.reshape(d, -1)
    wb_out = wb_out.reshape(d, d)
    h1 = _matmul("out_a", u_a, wa_out, dims=NN, grid=(nt,), a_spec=row, b_spec=whole((d, d)), o_spec=row,
                 out_shape=(s, d), out_dtype=F32, resid=x, resid_spec=row)

    nk, n2 = _norm_fwd("norm_kv_b", h1, jnp.concatenate([gk, gb], axis=0))
    kvw = wkv.shape[1]
    wkv_x = jnp.concatenate([wkv[:, (i // 2) * HEAD_DIM:(i // 2 + 1) * HEAD_DIM] for i in range(8)], axis=1)
    kvx = _matmul("proj_kv", nk, wkv_x, dims=NN, grid=(nt + 1,), zero_axis=0,
                  a_spec=pl.BlockSpec((tm, d), lambda i: (jnp.maximum(i - 1, 0), 0)), b_spec=whole((d, B_KVX)),
                  o_spec=pl.BlockSpec((tm, B_KVX), lambda i: (i, 0)), out_shape=(tm + s, B_KVX), out_dtype=BF16)
    qb = _matmul("proj_b_q", n2, wb_in, dims=NN, grid=(2, nt),
                 a_spec=pl.BlockSpec((tm, d), lambda j, i: (i, 0)),
                 b_spec=pl.BlockSpec((None, d, half), lambda j, i: (j, 0, 0)),
                 o_spec=pl.BlockSpec((tm, half), lambda j, i: (i, j)), out_shape=(s, d), out_dtype=BF16)
    gate_b = _matmul("proj_b_gate", n2, wb_in, dims=NN, grid=(2, nt),
                     a_spec=pl.BlockSpec((tm, d), lambda j, i: (i, 0)),
                     b_spec=pl.BlockSpec((None, d, half), lambda j, i: (2 + j, 0, 0)),
                     o_spec=pl.BlockSpec((tm, half), lambda j, i: (i, j)), out_shape=(s, d), out_dtype=F32)
    onehot_b = _b_offset_onehot()
    base_b = jnp.roll(_diag_rows(onehot_b, t5)[..., ::-1], TQ, axis=-1)
    o_b, u_b, lse_b = _attn_b_fwd(qb, kvx, gate_b, base_b, sinks)
    h2 = _matmul("out_b", u_b, wb_out, dims=NN, grid=(nt,), a_spec=row, b_spec=whole((d, d)), o_spec=row,
                 out_shape=(s, d), out_dtype=F32, resid=h1, resid_spec=row)

    dh2, loss, d_gf = _loss_head(h2, target, gf)

    du_b = _matmul("dout_b", dh2, wb_out, dims=NT, grid=(nt,), a_spec=row, b_spec=whole((d, d)), o_spec=row,
                   out_shape=(s, d), out_dtype=F32)
    d_wb_out = _matmul("dw_out_b", u_b, dh2, dims=TN, grid=(2,),
                       a_spec=whole((s, d)), b_spec=pl.BlockSpec((s, half), lambda j: (0, j)),
                       o_spec=pl.BlockSpec((d, half), lambda j: (0, j)),
                       out_shape=(d, d), out_dtype=F32, also_bf16=True)
    dz_b, dkv, dsum_b, dsinks = _attn_b_bwd(qb, kvx, gate_b, o_b, du_b, lse_b, base_b, sinks)
    ddiag_b = jnp.roll(dsum_b[..., ::-1], -1, axis=-1)
    dn2 = _matmul("dproj_b", dz_b, wb_in, dims=NT, grid=(nt,), parts=4,
                  a_spec=pl.BlockSpec((4, tm, half), lambda i: (0, i, 0)), b_spec=whole((4, d, half)),
                  o_spec=row, out_shape=(s, d), out_dtype=F32)
    d_wb_in = _matmul("dw_in_b", n2, dz_b, dims=TN, grid=(4,),
                      a_spec=whole((s, d)), b_spec=pl.BlockSpec((None, s, half), lambda j: (j, 0, 0)),
                      o_spec=pl.BlockSpec((None, d, half), lambda j: (j, 0, 0)),
                      out_shape=(4, d, half), out_dtype=F32, also_bf16=True)
    dnk = _matmul("dproj_kv", dkv, wkv, dims=NT, grid=(nt,),
                  a_spec=pl.BlockSpec((tm, kvw), lambda i: (i, 0)), b_spec=whole((d, kvw)), o_spec=row,
                  out_shape=(s, d), out_dtype=F32)
    d_wkv = _matmul("dw_kv", nk, dkv, dims=TN, grid=(1,),
                    a_spec=whole((s, d)), b_spec=whole((s, kvw)), o_spec=whole((d, kvw)),
                    out_shape=(d, kvw), out_dtype=F32, also_bf16=True)
    dh1, d_gkb = _norm_bwd("dnorm_kv_b", h1, dh2, [dnk, dn2], jnp.concatenate([gk, gb], axis=0))

    du_a = _matmul("dout_a", dh1, wa_out, dims=NT, grid=(nt,), a_spec=row, b_spec=whole((d, d)), o_spec=row,
                   out_shape=(s, d), out_dtype=F32)
    d_wa_out = _matmul("dw_out_a", u_a, dh1, dims=TN, grid=(2,),
                       a_spec=whole((s, d)), b_spec=pl.BlockSpec((s, half), lambda j: (0, j)),
                       o_spec=pl.BlockSpec((d, half), lambda j: (0, j)),
                       out_shape=(d, d), out_dtype=F32, also_bf16=True)
    early = dict(a_w_out=[g.reshape(4, d // 4, d) for g in d_wa_out],
                 kv_w=[g.reshape(4, d // 4, kvw) for g in d_wkv], b_w_in=list(d_wb_in),
                 b_w_out=[g.reshape(4, d // 4, d) for g in d_wb_out])
    (dz_a, ddiag_a), early_recv = _attn_a_bwd(
        zqkv, gate_a, o_a, du_a, lse_a, diag_a, hosted=_scatter_hosted([early[n][1] for n in early]))
    d_wa_in = _matmul("dw_in_a", n1, dz_a, dims=TN, grid=(4, 2),
                      a_spec=whole((s, d)), b_spec=pl.BlockSpec((None, s, half), lambda j, h: (j, 0, h)),
                      o_spec=pl.BlockSpec((None, d, half), lambda j, h: (j, 0, h)),
                      out_shape=(4, d, d), out_dtype=F32, also_bf16=True)
    tp = min(TM_PARTS, s)
    dn1, late_recv = _matmul("dproj_a", dz_a, wa_in, dims=NT, grid=(s // tp,), parts=4,
                             a_spec=pl.BlockSpec((4, tp, d), lambda i: (0, i, 0)), b_spec=whole((4, d, d)),
                             o_spec=pl.BlockSpec((tp, d), lambda i: (i, 0)),
                             out_shape=(s, d), out_dtype=F32, hosted=_scatter_hosted([d_wa_in[1]]))
    grad_x, d_ga = _norm_bwd("dnorm_a", x, dh1, [dn1], ga)

    small = dict(
        a_norm=d_ga, a_rel_bias=_diag_rows_grad(onehot_a, ddiag_a), kv_norm=d_gkb[0:1],
        t5_bias=_diag_rows_grad(onehot_b, ddiag_b), b_norm=d_gkb[1:2], b_sinks=dsinks[0:1, :HEADS],
        final_norm=d_gf)
    own = dict(a_w_in=d_wa_in[0], **{n: early[n][0] for n in early})
    received = dict(a_w_in=late_recv[0], **dict(zip(early, early_recv)))
    return loss, grad_x, small, own, received


SMALL = ("a_norm", "a_rel_bias", "kv_norm", "t5_bias", "b_norm", "b_sinks", "final_norm")
BIG = ("a_w_in", "a_w_out", "kv_w", "b_w_in", "b_w_out")
ORDER = ("a_norm", "a_w_in", "a_rel_bias", "a_w_out", "kv_norm", "kv_w", "t5_bias", "b_norm", "b_w_in",
         "b_sinks", "b_w_out", "final_norm")


def _pack(parts, rows):
    flat = jnp.concatenate([p.reshape(-1).astype(F32) for p in parts])
    return jnp.pad(flat, (0, rows * LANES - flat.shape[0])).reshape(rows, LANES)


def _unpack(block, shapes):
    flat = block.reshape(-1)
    out, at = [], 0
    for shp in shapes:
        size = int(np.prod(shp))
        out.append(flat[at:at + size].reshape(shp))
        at += size
    return out


def kernel(x, a_norm, a_w_in, a_rel_bias, a_w_out, kv_norm, kv_w, t5_bias, b_norm, b_w_in, b_sinks, b_w_out, final_norm, loss_target, m_a_norm, m_a_w_in, m_a_rel_bias, m_a_w_out, m_kv_norm, m_kv_w, m_t5_bias, m_b_norm, m_b_w_in, m_b_sinks, m_b_w_out, m_final_norm, v_a_norm, v_a_w_in, v_a_rel_bias, v_a_w_out, v_kv_norm, v_kv_w, v_t5_bias, v_b_norm, v_b_w_in, v_b_sinks, v_b_w_out, v_final_norm):
    w = dict(a_norm=a_norm, a_w_in=a_w_in, a_rel_bias=a_rel_bias, a_w_out=a_w_out, kv_norm=kv_norm, kv_w=kv_w,
             t5_bias=t5_bias, b_norm=b_norm, b_w_in=b_w_in, b_sinks=b_sinks, b_w_out=b_w_out,
             final_norm=final_norm)
    m = dict(a_norm=m_a_norm, a_w_in=m_a_w_in, a_rel_bias=m_a_rel_bias, a_w_out=m_a_w_out, kv_norm=m_kv_norm,
             kv_w=m_kv_w, t5_bias=m_t5_bias, b_norm=m_b_norm, b_w_in=m_b_w_in, b_sinks=m_b_sinks,
             b_w_out=m_b_w_out, final_norm=m_final_norm)
    v = dict(a_norm=v_a_norm, a_w_in=v_a_w_in, a_rel_bias=v_a_rel_bias, a_w_out=v_a_w_out, kv_norm=v_kv_norm,
             kv_w=v_kv_w, t5_bias=v_t5_bias, b_norm=v_b_norm, b_w_in=v_b_w_in, b_sinks=v_b_sinks,
             b_w_out=v_b_w_out, final_norm=v_final_norm)
    d = D_MODEL
    chip = 2 * lax.axis_index("x") + lax.axis_index("y")

    shard2d = dict(a_w_in=a_w_in[0], a_w_out=a_w_out[0], kv_w=kv_w, b_w_in=b_w_in[0], b_w_out=b_w_out[0])

    wa_in, ga = _run_alone("allgather_first",
                           _allgather_hosted([shard2d["a_w_in"].astype(BF16), a_norm], [True, False]))
    ga = ga.reshape(1, d)

    loss, grad_x, small, own, received = _local_step(
        x[0], loss_target[0], ga, wa_in, a_rel_bias[0], [shard2d[n].astype(BF16) for n in BIG[1:]],
        kv_norm.reshape(1, d), t5_bias, b_norm, b_sinks, final_norm.reshape(1, d))

    small_shapes = [small[n].shape for n in SMALL] + [(1, 1)]
    total = sum(int(np.prod(s)) for s in small_shapes)
    rows = -(-total // (8 * LANES)) * 8
    reduced = _unpack(_allreduce_small(_pack([small[n] for n in SMALL] + [loss], rows)), small_shapes)
    g_small = dict(zip(SMALL, reduced[:-1]))
    loss_out = reduced[-1].reshape(())
    g_small["a_norm"] = lax.dynamic_slice_in_dim(g_small["a_norm"], chip * (d // 4), d // 4, axis=1)

    core_sums = [
        _sum_partials("sum_" + n, lax.dynamic_index_in_dim(own[n], chip, 0, keepdims=False), received[n])
        for n in BIG]
    sibling_sums = _swap_with_sibling(core_sums)

    out = {}
    for n, mine, theirs in zip(BIG, core_sums, sibling_sums):
        res = _adamw("adamw_" + n, shard2d[n], m[n].reshape(shard2d[n].shape), v[n].reshape(shard2d[n].shape),
                     [mine, theirs])
        out[n] = [r.reshape(w[n].shape) for r in res]
    small_w_shapes = [w[n].shape for n in SMALL]
    total_w = sum(int(np.prod(s)) for s in small_w_shapes)
    rows_w = -(-total_w // (8 * LANES)) * 8
    packed = [_pack([t[n] for n in SMALL], rows_w) for t in (w, m, v)]
    g_packed = _pack([g_small[n] for n in SMALL], rows_w)
    res = _adamw("adamw_small", packed[0], packed[1], packed[2], [g_packed])
    unpacked = [_unpack(r, small_w_shapes) for r in res]
    for i, n in enumerate(SMALL):
        out[n] = [unpacked[k][i] for k in range(4)]

    grads = [out[n][0] for n in ORDER]
    deltas = [out[n][1] for n in ORDER]
    new_m = [out[n][2] for n in ORDER]
    new_v = [out[n][3] for n in ORDER]
    return (loss_out, grad_x[None], *grads, *deltas, *new_m, *new_v)
```

```python
import functools
import math

import jax
import jax.numpy as jnp
import numpy as np
from jax import lax
from jax.experimental import pallas as pl
from jax.experimental.pallas import tpu as pltpu
from jax.experimental.pallas import tpu_sc as plsc

F32 = jnp.float32
BF16 = jnp.bfloat16
MESH = pl.DeviceIdType.MESH

D_MODEL = 1024
HEADS = 16
HEAD_DIM = 64
CHUNK = 64
RMS_EPS = 1e-6
SCALE = HEAD_DIM ** -0.5
A_LEFT_CHUNKS = 8
A_REL_CLIP = 256
B_LEFT_CHUNKS = 2
B_KV_HEADS = 2
B_GROUP = HEADS // B_KV_HEADS
T5_BUCKETS = 32
T5_MAX_DIST = 128
ADAM_LR = 0.001
ADAM_B1 = 0.9
ADAM_B2 = 0.999
ADAM_EPS = 1e-08
ADAM_WD = 0.01
ADAM_STEP = 10

MASKED = -1e30
LANES = 128
TQ = 128
A_PAIRS = 2
A_PAIRS_FWD = 4
KB = 128
A_KBLOCKS = A_LEFT_CHUNKS * CHUNK // KB + 1
B_KBLOCKS = B_LEFT_CHUNKS * CHUNK // KB + 1
A_WIN = A_KBLOCKS * KB
B_WIN = B_KBLOCKS * KB
TM = 512
TM_DENSE = 1024
TM_PARTS = 512
VMEM_LIMIT = 56 * 1024 * 1024

NT = (((1,), (1,)), ((), ()))
TN = (((0,), (0,)), ((), ()))
NN = (((1,), (0,)), ((), ()))


def _params(sem=None):
    return pltpu.CompilerParams(dimension_semantics=sem, vmem_limit_bytes=VMEM_LIMIT)


class _Hosted:
    def __init__(self, inputs, out_shapes, sems, first, middle, last):
        self.inputs, self.out_shapes, self.sems = list(inputs), list(out_shapes), list(sems)
        self.first, self.middle, self.last = first, middle, last


def _call(body, *, name, grid, in_specs, out_specs, out_shape, args, scratch_shapes=(), sem=None, hosted=None):
    in_specs, out_specs, out_shape = list(in_specs), list(out_specs), list(out_shape)
    scratch_shapes = list(scratch_shapes)
    if hosted is None:
        out = pl.pallas_call(
            body, name=name, grid=grid, in_specs=in_specs, out_specs=out_specs, out_shape=out_shape,
            scratch_shapes=scratch_shapes, compiler_params=_params(sem))(*args)
        return list(out), []
    n_in, n_out, n_scr = len(in_specs), len(out_shape), len(scratch_shapes)
    h_in, h_out = len(hosted.inputs), len(hosted.out_shapes)
    total = int(np.prod(grid)) if grid else 1

    def wrapped(*refs):
        ins, refs = refs[:n_in], refs[n_in:]
        h_ins, refs = refs[:h_in], refs[h_in:]
        outs, refs = refs[:n_out], refs[n_out:]
        h_outs, refs = refs[:h_out], refs[h_out:]
        scr, h_sems = refs[:n_scr], refs[n_scr:]
        step = 0
        for axis, size in enumerate(grid):
            step = step * size + pl.program_id(axis)

        @pl.when(step == 0)
        def _():
            hosted.first(h_ins, h_outs, h_sems)

        body(*ins, *outs, *scr)
        if hosted.middle is not None:
            @pl.when(step == total // 2)
            def _():
                hosted.middle(h_ins, h_outs, h_sems)

        @pl.when(step == total - 1)
        def _():
            hosted.last(h_ins, h_outs, h_sems)

    out = pl.pallas_call(
        wrapped, name=name, grid=grid, in_specs=in_specs + [ANY] * h_in, out_specs=out_specs + [ANY] * h_out,
        out_shape=out_shape + hosted.out_shapes, scratch_shapes=scratch_shapes + hosted.sems,
        compiler_params=_params(("arbitrary",) * len(grid)))(*args, *hosted.inputs)
    return list(out[:n_out]), list(out[n_out:])


def _matmul(name, a, b, *, dims, grid, a_spec, b_spec, o_spec, out_shape, out_dtype,
            parts=1, resid=None, resid_spec=None, also_bf16=False, hosted=None, zero_axis=None):
    def body(*refs):
        if zero_axis is None:
            product(*refs)
        else:
            @pl.when(pl.program_id(zero_axis) == 0)
            def _():
                refs[2][...] = jnp.zeros_like(refs[2])

            @pl.when(pl.program_id(zero_axis) > 0)
            def _():
                product(*refs)

    def product(*refs):
        a_ref, b_ref = refs[:2]
        r_ref = refs[2] if resid is not None else None
        o_ref = refs[3] if resid is not None else refs[2]
        if parts == 1:
            prod = lax.dot_general(a_ref[...].astype(BF16), b_ref[...].astype(BF16), dims,
                                   preferred_element_type=F32)
        else:
            prod = None
            for part in range(parts):
                term = lax.dot_general(a_ref[part].astype(BF16), b_ref[part].astype(BF16), dims,
                                       preferred_element_type=F32)
                prod = term if prod is None else prod + term
        if resid is not None:
            prod = r_ref[...] + prod
        o_ref[...] = prod.astype(out_dtype)
        if also_bf16:
            refs[-1][...] = prod.astype(BF16)

    in_specs = [a_spec, b_spec]
    args = [a, b]
    if resid is not None:
        in_specs.append(resid_spec)
        args.append(resid)
    sem = ["parallel"] * len(grid)
    out_specs = [o_spec]
    out_shapes = [jax.ShapeDtypeStruct(out_shape, out_dtype)]
    if also_bf16:
        out_specs.append(o_spec)
        out_shapes.append(jax.ShapeDtypeStruct(out_shape, BF16))
    out, extra = _call(body, name=name, grid=grid, in_specs=in_specs, out_specs=out_specs, out_shape=out_shapes,
                       args=args, sem=tuple(sem), hosted=hosted)
    res = out[0] if not also_bf16 else tuple(out)
    return res if hosted is None else (res, extra)


def _rms_rows(x):
    return lax.rsqrt(jnp.mean(x * x, axis=-1, keepdims=True) + RMS_EPS)


def _norm_fwd(name, x, gains):
    s, d = x.shape
    n = gains.shape[0]

    def body(x_ref, g_ref, *o_refs):
        xv = x_ref[...]
        xh = xv * _rms_rows(xv)
        for i in range(n):
            o_refs[i][...] = (xh * g_ref[i:i + 1, :]).astype(BF16)

    row = pl.BlockSpec((TM, d), lambda i: (i, 0))
    return pl.pallas_call(
        body, name=name, grid=(s // TM,),
        in_specs=[row, pl.BlockSpec((n, d), lambda i: (0, 0))],
        out_specs=[row] * n,
        out_shape=[jax.ShapeDtypeStruct((s, d), BF16)] * n,
        compiler_params=_params(("parallel",)),
    )(x, gains)


def _norm_bwd(name, x, dres, dns, gains):
    s, d = x.shape
    n = len(dns)

    def body(x_ref, r_ref, g_ref, *refs):
        dn_refs, dx_ref, dg_ref = refs[:n], refs[n], refs[n + 1]
        i = pl.program_id(0)
        xv = x_ref[...]
        r = _rms_rows(xv)
        xh = xv * r

        @pl.when(i == 0)
        def _():
            dg_ref[...] = jnp.zeros_like(dg_ref)

        a = None
        for j in range(n):
            dn = dn_refs[j][...]
            t = dn * g_ref[j:j + 1, :]
            a = t if a is None else a + t
            dg_ref[j:j + 1, :] += jnp.sum(dn * xh, axis=0, keepdims=True)
        dx_ref[...] = r_ref[...] + r * (a - xh * jnp.mean(xh * a, axis=-1, keepdims=True))

    row = pl.BlockSpec((TM, d), lambda i: (i, 0))
    small = pl.BlockSpec((n, d), lambda i: (0, 0))
    return pl.pallas_call(
        body, name=name, grid=(s // TM,),
        in_specs=[row, row, small] + [row] * n,
        out_specs=[row, small],
        out_shape=[jax.ShapeDtypeStruct((s, d), F32), jax.ShapeDtypeStruct((n, d), F32)],
        compiler_params=_params(("arbitrary",)),
    )(x, dres, gains, *dns)


def _loss_head(h2, target, gain):
    s, d = h2.shape

    def body(h_ref, t_ref, g_ref, dh_ref, loss_ref, dg_ref):
        i = pl.program_id(0)
        hv = h_ref[...]
        r = _rms_rows(hv)
        hh = hv * r
        g = g_ref[...]
        err = hh * g - t_ref[...]
        part = 0.5 * jnp.sum(jnp.sum(err * err, axis=-1, keepdims=True) * (1.0 / d), axis=0, keepdims=True)
        dy = err * (1.0 / d)
        a = dy * g
        dh_ref[...] = r * (a - hh * jnp.mean(hh * a, axis=-1, keepdims=True))
        dg = jnp.sum(dy * hh, axis=0, keepdims=True)

        @pl.when(i == 0)
        def _():
            loss_ref[...] = part
            dg_ref[...] = dg

        @pl.when(i > 0)
        def _():
            loss_ref[...] += part
            dg_ref[...] += dg

    row = pl.BlockSpec((TM, d), lambda i: (i, 0))
    return pl.pallas_call(
        body, name="loss_head", grid=(s // TM,),
        in_specs=[row, row, pl.BlockSpec((1, d), lambda i: (0, 0))],
        out_specs=[row, pl.BlockSpec((1, 1), lambda i: (0, 0)), pl.BlockSpec((1, d), lambda i: (0, 0))],
        out_shape=[jax.ShapeDtypeStruct((s, d), F32), jax.ShapeDtypeStruct((1, 1), F32),
                   jax.ShapeDtypeStruct((1, d), F32)],
        compiler_params=_params(("arbitrary",)),
    )(h2, target, gain)


def _silu_parts(g):
    sig = jax.nn.sigmoid(g)
    return g * sig, sig * (1.0 + g * (1.0 - sig))


def _lane_lo(rows):
    return lax.broadcasted_iota(jnp.int32, (rows, LANES), 1) < HEAD_DIM


def _stack_pair(x):
    lo = _lane_lo(x.shape[0])
    zero = jnp.zeros_like(x)
    return jnp.concatenate([jnp.where(lo, x, zero), jnp.where(lo, zero, x)], axis=0)


def _unstack_pair(y, w):
    return jnp.where(_lane_lo(w), y[:w], y[w:])


def _block_valid(b, left_blocks, width):
    col = lax.broadcasted_iota(jnp.int32, (1, 2 * width), 1)
    col = jnp.where(col >= width, col - width, col)
    return (col // KB + (b - left_blocks)) >= 0


def _toeplitz_tile(diag_row, width, left_chunks):
    wide = width + TQ
    rolled = pltpu.roll(jnp.broadcast_to(diag_row, (TQ, wide)), 1, 1, stride=1, stride_axis=0)
    i = lax.broadcasted_iota(jnp.int32, (TQ, width), 0) // CHUNK
    j = lax.broadcasted_iota(jnp.int32, (TQ, width), 1) // CHUNK
    dc = i + left_chunks - j
    return jnp.where((dc >= 0) & (dc <= left_chunks), rolled[:, TQ:], MASKED)


def _toeplitz_sum(tile, width):
    flip = (lax.broadcasted_iota(jnp.int32, (TQ, TQ), 0) + lax.broadcasted_iota(jnp.int32, (TQ, TQ), 1)
            == TQ - 1).astype(F32)
    reversed_rows = jnp.dot(flip, tile, precision=lax.Precision.HIGHEST, preferred_element_type=F32)
    padded = jnp.concatenate([reversed_rows, jnp.zeros((TQ, TQ), F32)], axis=1)
    rolled = pltpu.roll(padded, 0, 1, stride=1, stride_axis=0)
    return jnp.sum(rolled, axis=0, keepdims=True)


def _softmax_pair(sc, w, sink=None):
    ps, inv, lses = [], [], []
    for e in range(2):
        sh = sc[:, e * w:(e + 1) * w]
        m = jnp.max(sh, axis=-1, keepdims=True)
        if sink is not None:
            m = jnp.maximum(m, sink[e])
        ex = jnp.exp(sh - m)
        l = jnp.sum(ex, axis=-1, keepdims=True)
        if sink is not None:
            l = l + jnp.exp(sink[e] - m)
        ps.append(ex.astype(BF16))
        inv.append(1.0 / l)
        lses.append(m + jnp.log(l))
    return jnp.concatenate(ps, axis=-1), inv, lses


def _softmax_pair_bwd(sc, dp, lse, delta, w):
    ps, dss = [], []
    for e in range(2):
        p = jnp.exp(sc[:, e * w:(e + 1) * w] - lse[e])
        ps.append(p)
        dss.append(p * (dp[:, e * w:(e + 1) * w] - delta[e]))
    return jnp.concatenate(ps, axis=-1), jnp.concatenate(dss, axis=-1)


def _pair_rowsums(x, lo):
    zero = jnp.zeros_like(x)
    return (jnp.sum(jnp.where(lo, x, zero), axis=-1, keepdims=True),
            jnp.sum(jnp.where(lo, zero, x), axis=-1, keepdims=True))


def _a_qkv_specs(rows, pad, pw):
    return [pl.BlockSpec((None, TQ, pw), lambda p, b: (0, b + pad // TQ, p)),
            pl.BlockSpec((None, rows, pw), lambda p, b: (1, 0, p)),
            pl.BlockSpec((None, rows, pw), lambda p, b: (2, 0, p))]


def _window(ref, b, pad, win, lanes):
    start = pl.multiple_of(b * TQ + pad - (win - TQ), KB)
    return ref[pl.ds(start, win), lanes]


def _attn_a_fwd(zqkv, g, diag, hosted=None):
    s = g.shape[0]
    pad = zqkv.shape[1] - s
    nb = s // TQ
    left = A_KBLOCKS - 1
    pairs = A_PAIRS_FWD
    pw = pairs * LANES
    wide = A_WIN + TQ

    def body(q_ref, k_ref, v_ref, g_ref, diag_ref, o_ref, u_ref, lse_ref, bias_scr):
        b = pl.program_id(1)

        @pl.when(b == 0)
        def _():
            for hh in range(2 * pairs):
                bias_scr[hh // 2, :, (hh % 2) * A_WIN:(hh % 2 + 1) * A_WIN] = _toeplitz_tile(
                    diag_ref[hh], A_WIN, A_LEFT_CHUNKS)

        def step(first_blocks):
            lo = _lane_lo(TQ)
            for pp in range(pairs):
                ln = slice(pp * LANES, (pp + 1) * LANES)
                kcat = _stack_pair(_window(k_ref, b, pad, A_WIN, ln))
                vcat = _stack_pair(_window(v_ref, b, pad, A_WIN, ln))
                sc = lax.dot_general(q_ref[:, ln] * SCALE, kcat, NT, preferred_element_type=F32) + bias_scr[pp]
                if first_blocks:
                    sc = jnp.where(_block_valid(b, left, A_WIN), sc, MASKED)
                p, inv, lses = _softmax_pair(sc, A_WIN)
                ov = jnp.dot(p, vcat, preferred_element_type=F32) * jnp.where(lo, inv[0], inv[1])
                o_ref[:, ln] = ov
                lse_ref[pp] = jnp.where(lo, lses[0], lses[1])
                sg, _ = _silu_parts(g_ref[:, ln])
                u_ref[:, ln] = (ov * sg).astype(BF16)

        @pl.when(b < left)
        def _():
            step(True)

        @pl.when(b >= left)
        def _():
            step(False)

    tile = pl.BlockSpec((TQ, pw), lambda p, b: (b, p))
    return _call(
        body, name="attn_a_fwd", grid=(HEADS // 2 // pairs, nb),
        in_specs=_a_qkv_specs(pad + s, pad, pw) + [
            tile, pl.BlockSpec((2 * pairs, 1, wide), lambda p, b: (p, 0, 0))],
        out_specs=[tile, tile, pl.BlockSpec((pairs, TQ, LANES), lambda p, b: (p, b, 0))],
        out_shape=[jax.ShapeDtypeStruct((s, D_MODEL), F32), jax.ShapeDtypeStruct((s, D_MODEL), BF16),
                   jax.ShapeDtypeStruct((HEADS // 2, s, LANES), F32)],
        scratch_shapes=[pltpu.VMEM((pairs, TQ, 2 * A_WIN), F32)],
        sem=("parallel", "arbitrary"), hosted=hosted,
        args=(zqkv, zqkv, zqkv, g, diag))


def _attn_a_bwd(zqkv, g, o, du, lse, diag, hosted=None):
    s = g.shape[0]
    pad = zqkv.shape[1] - s
    nb = s // TQ
    left = A_KBLOCKS - 1
    pw = A_PAIRS * LANES
    wide = A_WIN + TQ

    def body(q_ref, k_ref, v_ref, g_ref, o_ref, du_ref, lse_ref, diag_ref, dz_ref, ddiag_ref,
             bias_scr, dbias_acc, dk_acc, dv_acc):
        b = pl.program_id(1)

        @pl.when(b == 0)
        def _():
            for hh in range(2 * A_PAIRS):
                bias_scr[hh // 2, :, (hh % 2) * A_WIN:(hh % 2 + 1) * A_WIN] = _toeplitz_tile(
                    diag_ref[hh], A_WIN, A_LEFT_CHUNKS)
            dbias_acc[...] = jnp.zeros_like(dbias_acc)
            dk_acc[...] = jnp.zeros_like(dk_acc)
            dv_acc[...] = jnp.zeros_like(dv_acc)

        def step(first_blocks):
            lo = _lane_lo(TQ)
            upper = lax.broadcasted_iota(jnp.int32, (LANES, A_WIN), 0) < HEAD_DIM
            rows = pl.ds(pl.multiple_of(b * TQ, TQ), TQ)
            sg, dsg = _silu_parts(g_ref[...])
            duv = du_ref[...]
            ov = o_ref[...]
            do = duv * sg
            dz_ref[3, rows, :] = (duv * ov * dsg).astype(BF16)
            do_o = do * ov
            do_bf = do.astype(BF16)
            for pp in range(A_PAIRS):
                ln = slice(pp * LANES, (pp + 1) * LANES)
                q = q_ref[:, ln] * SCALE
                kcat = _stack_pair(_window(k_ref, b, pad, A_WIN, ln))
                vcat = _stack_pair(_window(v_ref, b, pad, A_WIN, ln))
                sc = lax.dot_general(q, kcat, NT, preferred_element_type=F32) + bias_scr[pp]
                if first_blocks:
                    sc = jnp.where(_block_valid(b, left, A_WIN), sc, MASKED)
                lse_t = lse_ref[pp]
                dp = lax.dot_general(do_bf[:, ln], vcat, NT, preferred_element_type=F32)
                p, ds = _softmax_pair_bwd(sc, dp, (lse_t[:, 0:1], lse_t[:, HEAD_DIM:HEAD_DIM + 1]),
                                          _pair_rowsums(do_o[:, ln], lo), A_WIN)
                dbias_acc[pp] += ds
                dsb = ds.astype(BF16)
                dz_ref[0, rows, ln] = (jnp.dot(dsb, kcat, preferred_element_type=F32) * SCALE).astype(BF16)
                dkt = lax.dot_general(q, dsb, TN, preferred_element_type=F32)
                dvt = lax.dot_general(do_bf[:, ln], p.astype(BF16), TN, preferred_element_type=F32)
                dkt = jnp.where(upper, dkt[:, :A_WIN], dkt[:, A_WIN:])
                dvt = jnp.where(upper, dvt[:, :A_WIN], dvt[:, A_WIN:])
                for t in range(A_KBLOCKS):
                    blk = b + (pad // KB - left + t)
                    dk_acc[blk, ln, :] += dkt[:, t * KB:(t + 1) * KB]
                    dv_acc[blk, ln, :] += dvt[:, t * KB:(t + 1) * KB]

        @pl.when(b < left)
        def _():
            step(True)

        @pl.when(b >= left)
        def _():
            step(False)

        @pl.when(b == nb - 1)
        def _():
            for kb in range(s // KB):
                dz_ref[1, kb * KB:(kb + 1) * KB, :] = dk_acc[pad // KB + kb].T.astype(BF16)
                dz_ref[2, kb * KB:(kb + 1) * KB, :] = dv_acc[pad // KB + kb].T.astype(BF16)
            for hh in range(2 * A_PAIRS):
                ddiag_ref[hh] = _toeplitz_sum(
                    dbias_acc[hh // 2, :, (hh % 2) * A_WIN:(hh % 2 + 1) * A_WIN], A_WIN)

    tile = pl.BlockSpec((TQ, pw), lambda p, b: (b, p))
    diag_spec = pl.BlockSpec((2 * A_PAIRS, 1, wide), lambda p, b: (p, 0, 0))
    return _call(
        body, name="attn_a_bwd", grid=(HEADS // 2 // A_PAIRS, nb),
        in_specs=_a_qkv_specs(pad + s, pad, pw) + [
            tile, tile, tile, pl.BlockSpec((A_PAIRS, TQ, LANES), lambda p, b: (p, b, 0)), diag_spec],
        out_specs=[pl.BlockSpec((4, s, pw), lambda p, b: (0, 0, p)), diag_spec],
        out_shape=[jax.ShapeDtypeStruct((4, s, D_MODEL), BF16),
                   jax.ShapeDtypeStruct((HEADS, 1, wide), F32)],
        scratch_shapes=[pltpu.VMEM((A_PAIRS, TQ, 2 * A_WIN), F32), pltpu.VMEM((A_PAIRS, TQ, 2 * A_WIN), F32),
                        pltpu.VMEM(((pad + s) // KB, pw, KB), F32), pltpu.VMEM(((pad + s) // KB, pw, KB), F32)],
        sem=("parallel", "arbitrary"), hosted=hosted,
        args=(zqkv, zqkv, zqkv, g, o, du, lse, diag))


B_STACK = B_GROUP // 2
B_KVX = 4 * LANES
B_ROWS = B_STACK * TQ
B_WIDE = B_WIN + TQ


def _b_head_place(h):
    return h // B_GROUP, (h % B_GROUP) // 2, h % 2


def _toeplitz_tile_t(base_row, width, left_chunks):
    wide = width + TQ
    rolled = pltpu.roll(jnp.broadcast_to(base_row, (width, wide)), 0, 1, stride=1, stride_axis=0)
    j = lax.broadcasted_iota(jnp.int32, (width, TQ), 0) // CHUNK
    i = lax.broadcasted_iota(jnp.int32, (width, TQ), 1) // CHUNK
    dc = i + left_chunks - j
    return jnp.where((dc >= 0) & (dc <= left_chunks), rolled[:, :TQ], MASKED)


def _toeplitz_sum_t(tile_t, width):
    flip = (lax.broadcasted_iota(jnp.int32, (width, width), 0) + lax.broadcasted_iota(jnp.int32, (width, width), 1)
            == width - 1).astype(F32)
    reversed_rows = jnp.dot(flip, tile_t, precision=lax.Precision.HIGHEST, preferred_element_type=F32)
    padded = jnp.concatenate([reversed_rows, jnp.zeros((width, width), F32)], axis=1)
    rolled = pltpu.roll(padded, 0, 1, stride=1, stride_axis=0)
    return jnp.sum(rolled, axis=0, keepdims=True)


def _b_build_bias(base_ref, bias_scr):
    for h in range(HEADS):
        gi, pr, e = _b_head_place(h)
        bias_scr[gi, e * B_WIN:(e + 1) * B_WIN, pr * TQ:(pr + 1) * TQ] = _toeplitz_tile_t(
            base_ref[h], B_WIN, B_LEFT_CHUNKS)


def _b_stack(x, gi):
    return jnp.concatenate(
        [x[:, (B_STACK * gi + pr) * LANES:(B_STACK * gi + pr + 1) * LANES] for pr in range(B_STACK)], axis=0)


def _b_sink_rows(sink_ref, gi):
    block = lax.broadcasted_iota(jnp.int32, (1, B_ROWS), 1) // TQ
    rows = []
    for e in range(2):
        row = jnp.zeros((1, B_ROWS), F32)
        for pr in range(B_STACK):
            h = B_GROUP * gi + 2 * pr + e
            row = jnp.where(block == pr, sink_ref[0:1, h:h + 1], row)
        rows.append(row)
    return rows


def _b_scores_t(q_ref, kvv, bias_scr, gi, b, left, first_blocks):
    kcat = _stack_pair(kvv[:, gi * LANES:(gi + 1) * LANES])
    vcat = _stack_pair(kvv[:, (B_KV_HEADS + gi) * LANES:(B_KV_HEADS + gi + 1) * LANES])
    qs = _b_stack(q_ref, gi) * SCALE
    sc = lax.dot_general(kcat, qs, NT, preferred_element_type=F32) + bias_scr[gi]
    if first_blocks:
        row = lax.broadcasted_iota(jnp.int32, (2 * B_WIN, 1), 0)
        row = jnp.where(row >= B_WIN, row - B_WIN, row)
        sc = jnp.where((row // KB + (b - left)) >= 0, sc, MASKED)
    return kcat, vcat, qs, sc


def _attn_b_fwd(qb, kvx, gate, base, sinks):
    s = qb.shape[0]
    pad = kvx.shape[0] - s
    nb = s // TQ
    left = B_KBLOCKS - 1

    def body(q_ref, kv_ref, g_ref, base_ref, sink_ref, o_ref, u_ref, lse_ref, bias_scr):
        b = pl.program_id(0)

        @pl.when(b == 0)
        def _():
            _b_build_bias(base_ref, bias_scr)

        def step(first_blocks):
            kvv = _window(kv_ref, b, pad, B_WIN, slice(None))
            upper = lax.broadcasted_iota(jnp.int32, (LANES, B_ROWS), 0) < HEAD_DIM
            lse_rows = []
            for gi in range(B_KV_HEADS):
                kcat, vcat, qs, sc = _b_scores_t(q_ref, kvv, bias_scr, gi, b, left, first_blocks)
                sink = _b_sink_rows(sink_ref, gi)
                ps, inv = [], []
                for e in range(2):
                    sh = sc[e * B_WIN:(e + 1) * B_WIN]
                    m = jnp.maximum(jnp.max(sh, axis=0, keepdims=True), sink[e])
                    ex = jnp.exp(sh - m)
                    l = jnp.sum(ex, axis=0, keepdims=True) + jnp.exp(sink[e] - m)
                    ps.append(ex.astype(BF16))
                    inv.append(1.0 / l)
                    lse_rows.append(m + jnp.log(l))
                pt = jnp.concatenate(ps, axis=0)
                ot = lax.dot_general(vcat, pt, TN, preferred_element_type=F32) * jnp.where(upper, inv[0], inv[1])
                ov = ot.T
                for pr in range(B_STACK):
                    pair = B_STACK * gi + pr
                    o_ref[:, pair * LANES:(pair + 1) * LANES] = ov[pr * TQ:(pr + 1) * TQ]
            lse_ref[0] = jnp.concatenate(lse_rows + [jnp.zeros((8 - len(lse_rows), B_ROWS), F32)], axis=0)
            sg, _ = _silu_parts(g_ref[...])
            u_ref[...] = (o_ref[...] * sg).astype(BF16)

        @pl.when(b < left)
        def _():
            step(True)

        @pl.when(b >= left)
        def _():
            step(False)

    row = pl.BlockSpec((TQ, D_MODEL), lambda b: (b, 0))
    return pl.pallas_call(
        body, name="attn_b_fwd", grid=(nb,),
        in_specs=[row, pl.BlockSpec((pad + s, B_KVX), lambda b: (0, 0)), row,
                  pl.BlockSpec((HEADS, 1, B_WIDE), lambda b: (0, 0, 0)), pl.BlockSpec((1, HEADS), lambda b: (0, 0))],
        out_specs=[row, row, pl.BlockSpec((1, 8, B_ROWS), lambda b: (b, 0, 0))],
        out_shape=[jax.ShapeDtypeStruct((s, D_MODEL), F32), jax.ShapeDtypeStruct((s, D_MODEL), BF16),
                   jax.ShapeDtypeStruct((nb, 8, B_ROWS), F32)],
        scratch_shapes=[pltpu.VMEM((B_KV_HEADS, 2 * B_WIN, B_ROWS), F32)],
        compiler_params=_params(("arbitrary",)),
    )(qb, kvx, gate, base, sinks)


def _attn_b_bwd(qb, kvx, gate, o, du, lse, base, sinks):
    s = qb.shape[0]
    pad = kvx.shape[0] - s
    nb = s // TQ
    left = B_KBLOCKS - 1
    half = D_MODEL // 2

    def body(q_ref, kv_ref, g_ref, o_ref, du_ref, lse_ref, base_ref, sink_ref, dz_ref, dkv_ref, dsum_ref,
             dsink_ref, bias_scr, dbias_acc, dkv_acc, dsink_acc):
        b = pl.program_id(0)

        @pl.when(b == 0)
        def _():
            _b_build_bias(base_ref, bias_scr)
            dbias_acc[...] = jnp.zeros_like(dbias_acc)
            dkv_acc[...] = jnp.zeros_like(dkv_acc)
            dsink_acc[...] = jnp.zeros_like(dsink_acc)

        def step(first_blocks):
            kvv = _window(kv_ref, b, pad, B_WIN, slice(None))
            sg, dsg = _silu_parts(g_ref[...])
            duv = du_ref[...]
            ov = o_ref[...]
            do = duv * sg
            dgate = (duv * ov * dsg).astype(BF16)
            dz_ref[2] = dgate[:, :half]
            dz_ref[3] = dgate[:, half:]
            do_o = do * ov
            do_bf = do.astype(BF16)
            lane = lax.broadcasted_iota(jnp.int32, (8, LANES), 1)
            sub = lax.broadcasted_iota(jnp.int32, (8, LANES), 0)
            halves = (((sub == 0) & (lane < HEAD_DIM)) | ((sub == 1) & (lane >= HEAD_DIM))).astype(F32)
            lse_all = lse_ref[0]
            dsink_rows = []
            for gi in range(B_KV_HEADS):
                kcat, vcat, qs, sc = _b_scores_t(q_ref, kvv, bias_scr, gi, b, left, first_blocks)
                dos = _b_stack(do_bf, gi)
                delta = lax.dot_general(halves, _b_stack(do_o, gi), NT, precision=lax.Precision.HIGHEST,
                                        preferred_element_type=F32)
                sink = _b_sink_rows(sink_ref, gi)
                dp = lax.dot_general(vcat, dos, NT, preferred_element_type=F32)
                ps, dss = [], []
                for e in range(2):
                    lse_e = lse_all[2 * gi + e:2 * gi + e + 1]
                    delta_e = delta[e:e + 1]
                    p = jnp.exp(sc[e * B_WIN:(e + 1) * B_WIN] - lse_e)
                    ps.append(p.astype(BF16))
                    dss.append(p * (dp[e * B_WIN:(e + 1) * B_WIN] - delta_e))
                    dsink_rows.append(-jnp.exp(sink[e] - lse_e) * delta_e)
                ds = jnp.concatenate(dss, axis=0)
                dbias_acc[gi] += ds
                dsb = ds.astype(BF16)
                dq = (lax.dot_general(kcat, dsb, TN, preferred_element_type=F32) * SCALE).T.astype(BF16)
                for pr in range(B_STACK):
                    dz_ref[gi, :, pr * LANES:(pr + 1) * LANES] = dq[pr * TQ:(pr + 1) * TQ]
                dk = _unstack_pair(jnp.dot(dsb, qs, preferred_element_type=F32), B_WIN)
                dv = _unstack_pair(jnp.dot(jnp.concatenate(ps, axis=0), dos, preferred_element_type=F32), B_WIN)
                krows = pl.ds(pl.multiple_of(b * TQ + pad - (B_WIN - TQ), KB), B_WIN)
                dkv_acc[krows, gi * LANES:(gi + 1) * LANES] += dk
                dkv_acc[krows, (B_KV_HEADS + gi) * LANES:(B_KV_HEADS + gi + 1) * LANES] += dv
            dsink_acc[...] += jnp.concatenate(
                dsink_rows + [jnp.zeros((8 - len(dsink_rows), B_ROWS), F32)], axis=0)

        @pl.when(b < left)
        def _():
            step(True)

        @pl.when(b >= left)
        def _():
            step(False)

        @pl.when(b == nb - 1)
        def _():
            lo_s = _lane_lo(s)
            for which in range(2):
                folded = []
                for gi in range(B_KV_HEADS):
                    part = dkv_acc[pad:pad + s, (which * B_KV_HEADS + gi) * LANES:(which * B_KV_HEADS + gi + 1) * LANES]
                    folded.append(part + pltpu.roll(part, HEAD_DIM, 1))
                dkv_ref[:, which * LANES:(which + 1) * LANES] = jnp.where(lo_s, folded[0], folded[1]).astype(BF16)
            lane8 = lax.broadcasted_iota(jnp.int32, dsink_ref.shape, 1)
            tot = jnp.zeros(dsink_ref.shape, F32)
            for h in range(HEADS):
                gi, pr, e = _b_head_place(h)
                dsum_ref[h] = _toeplitz_sum_t(
                    dbias_acc[gi, e * B_WIN:(e + 1) * B_WIN, pr * TQ:(pr + 1) * TQ], B_WIN)
                per_query = dsink_acc[2 * gi + e:2 * gi + e + 1, pr * TQ:(pr + 1) * TQ]
                tot = jnp.where(lane8 == h, jnp.sum(per_query, axis=1, keepdims=True), tot)
            dsink_ref[...] = tot

    row = pl.BlockSpec((TQ, D_MODEL), lambda b: (b, 0))
    base_spec = pl.BlockSpec((HEADS, 1, B_WIDE), lambda b: (0, 0, 0))
    return pl.pallas_call(
        body, name="attn_b_bwd", grid=(nb,),
        in_specs=[row, pl.BlockSpec((pad + s, B_KVX), lambda b: (0, 0)), row, row, row,
                  pl.BlockSpec((1, 8, B_ROWS), lambda b: (b, 0, 0)), base_spec,
                  pl.BlockSpec((1, HEADS), lambda b: (0, 0))],
        out_specs=[pl.BlockSpec((4, TQ, half), lambda b: (0, b, 0)),
                   pl.BlockSpec((s, 2 * LANES), lambda b: (0, 0)), base_spec,
                   pl.BlockSpec((8, LANES), lambda b: (0, 0))],
        out_shape=[jax.ShapeDtypeStruct((4, s, half), BF16), jax.ShapeDtypeStruct((s, 2 * LANES), BF16),
                   jax.ShapeDtypeStruct((HEADS, 1, B_WIDE), F32), jax.ShapeDtypeStruct((8, LANES), F32)],
        scratch_shapes=[pltpu.VMEM((B_KV_HEADS, 2 * B_WIN, B_ROWS), F32),
                        pltpu.VMEM((B_KV_HEADS, 2 * B_WIN, B_ROWS), F32),
                        pltpu.VMEM((pad + s, B_KVX), F32), pltpu.VMEM((8, B_ROWS), F32)],
        compiler_params=_params(("arbitrary",)),
    )(qb, kvx, gate, o, du, lse, base, sinks)


def _t5_bucket(rel):
    nb = T5_BUCKETS // 2
    max_exact = nb // 2
    ret = jnp.where(rel > 0, nb, 0)
    n = jnp.abs(rel)
    nf = jnp.maximum(n, 1).astype(jnp.float32)
    large = max_exact + (jnp.log(nf / max_exact) / math.log(T5_MAX_DIST / max_exact)
                         * (nb - max_exact)).astype(jnp.int32)
    large = jnp.minimum(large, nb - 1)
    return ret + jnp.where(n < max_exact, n, large)


def _a_offset_onehot():
    c = np.arange(A_WIN + TQ)
    dist = A_LEFT_CHUNKS * CHUNK + TQ - 1 - c
    idx = np.clip(dist, -A_REL_CLIP, A_REL_CLIP) + A_REL_CLIP
    onehot = np.zeros((A_WIN + TQ, 2 * A_REL_CLIP + 1), np.float32)
    onehot[c, idx] = 1.0
    return jnp.asarray(onehot)


def _b_offset_onehot():
    c = jnp.arange(B_WIN + TQ, dtype=jnp.int32)
    rel = c - (TQ - 1) - B_LEFT_CHUNKS * CHUNK
    return (_t5_bucket(rel)[:, None] == jnp.arange(T5_BUCKETS)[None, :]).astype(F32)


def _diag_rows(onehot, table):
    rows = jnp.dot(onehot, table.astype(F32), precision=lax.Precision.HIGHEST)
    return rows.T.reshape(HEADS, 1, onehot.shape[0])


def _diag_rows_grad(onehot, ddiag):
    return jnp.dot(ddiag.reshape(HEADS, onehot.shape[0]), onehot, precision=lax.Precision.HIGHEST).T


def _position():
    x, y, c = lax.axis_index("x"), lax.axis_index("y"), lax.axis_index("c")
    chips = [(1 - x, y), (x, 1 - y), (1 - x, 1 - y)]
    return x, y, c, chips


ANY = pl.BlockSpec(memory_space=pl.ANY)


def _allgather_hosted(shards, split):
    n = len(shards)

    def part(ref, t, half):
        if not split[t]:
            return ref
        rows = shards[t].shape[0] // 2
        return ref.at[pl.ds(half * rows, rows)]

    def copies(kind, ins, outs, sems):
        send_sems, recv_sems, pass_send, pass_recv, local_sems = sems
        x, y, c, chips = _position()
        mine = 2 * x + y
        if kind == "local":
            return [pltpu.make_async_copy(ins[t], outs[t].at[mine], local_sems.at[t]) for t in range(n)]
        made = []
        for t in range(n):
            for j, chip in enumerate(chips):
                theirs = 2 * chip[0] + chip[1]
                far = dict(send_sem=send_sems.at[3 * t + j], recv_sem=recv_sems.at[3 * t + j],
                           device_id=(chip[0], chip[1], c), device_id_type=MESH)
                near = dict(send_sem=pass_send.at[3 * t + j], recv_sem=pass_recv.at[3 * t + j],
                            device_id=(x, y, 1 - c), device_id_type=MESH)
                here = part(outs[t].at[theirs], t, c)
                if kind == "send":
                    made.append(pltpu.make_async_remote_copy(
                        src_ref=part(ins[t], t, c), dst_ref=part(outs[t].at[mine], t, c), **far))
                elif kind == "landed":
                    made.append(pltpu.make_async_remote_copy(src_ref=here, dst_ref=here, **far))
                elif not split[t]:
                    made.append(None)
                elif kind == "pass":
                    made.append(pltpu.make_async_remote_copy(src_ref=here, dst_ref=here, **near))
                else:
                    other = part(outs[t].at[theirs], t, 1 - c)
                    made.append(pltpu.make_async_remote_copy(src_ref=other, dst_ref=other, **near))
        return made

    def first(ins, outs, sems):
        for cp in copies("local", ins, outs, sems) + copies("send", ins, outs, sems):
            cp.start()

    def middle(ins, outs, sems):
        for got, cp in zip(copies("landed", ins, outs, sems), copies("pass", ins, outs, sems)):
            got.wait_recv()
            if cp is not None:
                cp.start()

    def last(ins, outs, sems):
        for cp in copies("passed", ins, outs, sems):
            if cp is not None:
                cp.wait_recv()
        for cp in copies("send", ins, outs, sems) + copies("pass", ins, outs, sems):
            if cp is not None:
                cp.wait_send()
        for cp in copies("local", ins, outs, sems):
            cp.wait()

    return _Hosted(shards, [jax.ShapeDtypeStruct((4,) + w.shape, w.dtype) for w in shards],
                   [pltpu.SemaphoreType.DMA((3 * n,))] * 4 + [pltpu.SemaphoreType.DMA((n,))],
                   first, middle, last)


def _scatter_hosted(grads):
    n = len(grads)

    def copies(ins, outs, sems):
        send_sems, recv_sems = sems
        x, y, c, chips = _position()
        return [pltpu.make_async_remote_copy(
            src_ref=ins[t].at[2 * chip[0] + chip[1]], dst_ref=outs[t].at[j],
            send_sem=send_sems.at[3 * t + j], recv_sem=recv_sems.at[3 * t + j],
            device_id=(chip[0], chip[1], c), device_id_type=MESH)
            for t in range(n) for j, chip in enumerate(chips)]

    def first(ins, outs, sems):
        for cp in copies(ins, outs, sems):
            cp.start()

    def last(ins, outs, sems):
        for cp in copies(ins, outs, sems):
            cp.wait()

    return _Hosted(grads, [jax.ShapeDtypeStruct((3,) + g.shape[1:], g.dtype) for g in grads],
                   [pltpu.SemaphoreType.DMA((3 * n,))] * 2, first, None, last)


def _scatter_on_sequencer(name, grad):
    src = jax.new_ref(grad, memory_space=pltpu.MemorySpace.HBM)
    dst = jax.empty_ref(jax.ShapeDtypeStruct((3,) + grad.shape[1:], grad.dtype),
                        memory_space=pltpu.MemorySpace.HBM)

    @pl.kernel(mesh=plsc.ScalarSubcoreMesh(axis_name="sequencer", num_cores=1), name=name,
               scratch_types=(pltpu.SemaphoreType.DMA((3,)), pltpu.SemaphoreType.DMA((3,))),
               compiler_params=pltpu.CompilerParams(collective_id=0))
    def launch(send_sems, recv_sems):
        x, y, c, chips = _position()
        barrier = pltpu.get_barrier_semaphore()
        for chip in chips:
            pl.semaphore_signal(barrier, inc=1, device_id=(chip[0], chip[1], c), device_id_type=MESH)
        pl.semaphore_wait(barrier, len(chips))
        copies = [pltpu.make_async_remote_copy(
            src_ref=src.at[2 * chip[0] + chip[1]], dst_ref=dst.at[j], send_sem=send_sems.at[j],
            recv_sem=recv_sems.at[j], device_id=(chip[0], chip[1], c), device_id_type=MESH)
            for j, chip in enumerate(chips)]
        for cp in copies:
            cp.start()
        for cp in copies:
            cp.wait()

    launch()
    return dst[...]


def _run_alone(name, hosted):
    n_in = len(hosted.inputs)
    n_out = len(hosted.out_shapes)

    def body(*refs):
        ins, outs, sems = refs[:n_in], refs[n_in:n_in + n_out], refs[n_in + n_out:]
        hosted.first(ins, outs, sems)
        if hosted.middle is not None:
            hosted.middle(ins, outs, sems)
        hosted.last(ins, outs, sems)

    return pl.pallas_call(
        body, name=name, in_specs=[ANY] * n_in, out_specs=[ANY] * n_out, out_shape=hosted.out_shapes,
        scratch_shapes=hosted.sems)(*hosted.inputs)


def _swap_with_sibling(blocks):
    n = len(blocks)

    def body(*refs):
        ins, outs = refs[:n], refs[n:2 * n]
        send_sems, recv_sems = refs[2 * n:]
        x, y, c, _ = _position()
        sends = [pltpu.make_async_remote_copy(
            src_ref=ins[t], dst_ref=outs[t], send_sem=send_sems.at[t], recv_sem=recv_sems.at[t],
            device_id=(x, y, 1 - c), device_id_type=MESH) for t in range(n)]
        for cp in sends:
            cp.start()
        for cp in sends:
            cp.wait()

    return pl.pallas_call(
        body, name="swap_with_sibling",
        in_specs=[ANY] * n, out_specs=[ANY] * n,
        out_shape=[jax.ShapeDtypeStruct(b.shape, b.dtype) for b in blocks],
        scratch_shapes=[pltpu.SemaphoreType.DMA((n,))] * 2,
    )(*blocks)


def _allreduce_small(block):
    rows = block.shape[0]

    def body(in_ref, sum_ref, all_ref, send_sems, recv_sems):
        x, y, c, _ = _position()
        me = 4 * x + 2 * y + c
        all_ref[me] = in_ref[...]
        sends = []
        for k in range(1, 8):
            peer = (x ^ (k >> 2), y ^ ((k >> 1) & 1), c ^ (k & 1))
            sends.append(pltpu.make_async_remote_copy(
                src_ref=in_ref, dst_ref=all_ref.at[me], send_sem=send_sems.at[k - 1],
                recv_sem=recv_sems.at[k - 1], device_id=peer, device_id_type=MESH))
        for cp in sends:
            cp.start()
        for k in range(1, 8):
            theirs = me ^ k
            pltpu.make_async_remote_copy(
                src_ref=in_ref, dst_ref=all_ref.at[theirs], send_sem=send_sems.at[k - 1],
                recv_sem=recv_sems.at[k - 1], device_id=(x, y, c), device_id_type=MESH).wait_recv()
        for cp in sends:
            cp.wait_send()
        acc = all_ref[0]
        for d in range(1, 8):
            acc = acc + all_ref[d]
        sum_ref[...] = acc

    vmem = pl.BlockSpec(memory_space=pltpu.VMEM)
    return pl.pallas_call(
        body, name="allreduce_small",
        in_specs=[vmem], out_specs=[vmem, vmem],
        out_shape=[jax.ShapeDtypeStruct((rows, LANES), F32), jax.ShapeDtypeStruct((8, rows, LANES), F32)],
        scratch_shapes=[pltpu.SemaphoreType.DMA((7,))] * 2,
    )(block)[0]


def _adamw_math(w, g, m, v):
    m = ADAM_B1 * m + (1.0 - ADAM_B1) * g
    v = ADAM_B2 * v + (1.0 - ADAM_B2) * (g * g)
    m_hat = m / (1.0 - ADAM_B1 ** ADAM_STEP)
    v_hat = v / (1.0 - ADAM_B2 ** ADAM_STEP)
    delta = -ADAM_LR * (m_hat / (jnp.sqrt(v_hat) + ADAM_EPS) + ADAM_WD * w)
    return delta, m, v


def _row_tile(rows):
    return min(rows, 256)


def _sum_partials(name, own, recv):
    rows, cols = own.shape
    tr = _row_tile(rows)

    def body(own_ref, recv_ref, o_ref):
        acc = own_ref[...]
        for j in range(3):
            acc = acc + recv_ref[j].astype(F32)
        o_ref[...] = acc

    return pl.pallas_call(
        body, name=name, grid=(rows // tr,),
        in_specs=[pl.BlockSpec((tr, cols), lambda i: (i, 0)), pl.BlockSpec((3, tr, cols), lambda i: (0, i, 0))],
        out_specs=pl.BlockSpec((tr, cols), lambda i: (i, 0)),
        out_shape=jax.ShapeDtypeStruct((rows, cols), F32),
        compiler_params=_params(("parallel",)),
    )(own, recv)


def _adamw(name, w, m, v, g_parts):
    rows, cols = w.shape
    tr = _row_tile(rows)
    n = len(g_parts)

    def body(w_ref, m_ref, v_ref, *refs):
        g_refs = refs[:n]
        go_ref, d_ref, mo_ref, vo_ref = refs[n:]
        g = g_refs[0][...]
        for r in g_refs[1:]:
            g = g + r[...]
        delta, mn, vn = _adamw_math(w_ref[...], g, m_ref[...], v_ref[...])
        go_ref[...] = g
        d_ref[...] = delta
        mo_ref[...] = mn
        vo_ref[...] = vn

    spec = pl.BlockSpec((tr, cols), lambda i: (i, 0))
    return pl.pallas_call(
        body, name=name, grid=(rows // tr,),
        in_specs=[spec] * (3 + n), out_specs=[spec] * 4,
        out_shape=[jax.ShapeDtypeStruct((rows, cols), F32)] * 4,
        compiler_params=_params(("parallel",)),
    )(w, m, v, *g_parts)


def _local_step(x, target, ga, wa_in, rel_bias, later_shards, gk, t5, gb, sinks, gf):
    s, d = x.shape
    tm = min(TM_DENSE, s)
    nt = s // tm
    half = d // 2
    row = pl.BlockSpec((tm, d), lambda i: (i, 0))
    whole = lambda shape: pl.BlockSpec(shape, lambda *_: (0,) * len(shape))

    n1, = _norm_fwd("norm_a", x, ga)
    zqkv = _matmul("proj_a_qkv", n1, wa_in, dims=NN, grid=(3, nt + 1), zero_axis=1,
                   a_spec=pl.BlockSpec((tm, d), lambda j, i: (jnp.maximum(i - 1, 0), 0)),
                   b_spec=pl.BlockSpec((None, d, d), lambda j, i: (j, 0, 0)),
                   o_spec=pl.BlockSpec((None, tm, d), lambda j, i: (j, i, 0)),
                   out_shape=(3, tm + s, d), out_dtype=BF16)
    gate_a = _matmul("proj_a_gate", n1, wa_in, dims=NN, grid=(nt,),
                     a_spec=row, b_spec=pl.BlockSpec((None, d, d), lambda i: (3, 0, 0)), o_spec=row,
                     out_shape=(s, d), out_dtype=F32)
    onehot_a = _a_offset_onehot()
    diag_a = _diag_rows(onehot_a, rel_bias)
    (o_a, u_a, lse_a), gathered = _attn_a_fwd(
        zqkv, gate_a, diag_a, hosted=_allgather_hosted(later_shards, [True] * len(later_shards)))
    wa_out, wkv, wb_in, wb_out = gathered
    wa_out = wa_out.reshape(d, d)
    wkv = wkv.reshape(d, -1)
    wb_out = wb_out.reshape(d, d)
    h1 = _matmul("out_a", u_a, wa_out, dims=NN, grid=(nt,), a_spec=row, b_spec=whole((d, d)), o_spec=row,
                 out_shape=(s, d), out_dtype=F32, resid=x, resid_spec=row)

    nk, n2 = _norm_fwd("norm_kv_b", h1, jnp.concatenate([gk, gb], axis=0))
    kvw = wkv.shape[1]
    wkv_x = jnp.concatenate([wkv[:, (i // 2) * HEAD_DIM:(i // 2 + 1) * HEAD_DIM] for i in range(8)], axis=1)
    kvx = _matmul("proj_kv", nk, wkv_x, dims=NN, grid=(nt + 1,), zero_axis=0,
                  a_spec=pl.BlockSpec((tm, d), lambda i: (jnp.maximum(i - 1, 0), 0)), b_spec=whole((d, B_KVX)),
                  o_spec=pl.BlockSpec((tm, B_KVX), lambda i: (i, 0)), out_shape=(tm + s, B_KVX), out_dtype=BF16)
    qb = _matmul("proj_b_q", n2, wb_in, dims=NN, grid=(2, nt),
                 a_spec=pl.BlockSpec((tm, d), lambda j, i: (i, 0)),
                 b_spec=pl.BlockSpec((None, d, half), lambda j, i: (j, 0, 0)),
                 o_spec=pl.BlockSpec((tm, half), lambda j, i: (i, j)), out_shape=(s, d), out_dtype=BF16)
    gate_b = _matmul("proj_b_gate", n2, wb_in, dims=NN, grid=(2, nt),
                     a_spec=pl.BlockSpec((tm, d), lambda j, i: (i, 0)),
                     b_spec=pl.BlockSpec((None, d, half), lambda j, i: (2 + j, 0, 0)),
                     o_spec=pl.BlockSpec((tm, half), lambda j, i: (i, j)), out_shape=(s, d), out_dtype=F32)
    onehot_b = _b_offset_onehot()
    base_b = jnp.roll(_diag_rows(onehot_b, t5)[..., ::-1], TQ, axis=-1)
    o_b, u_b, lse_b = _attn_b_fwd(qb, kvx, gate_b, base_b, sinks)
    h2 = _matmul("out_b", u_b, wb_out, dims=NN, grid=(nt,), a_spec=row, b_spec=whole((d, d)), o_spec=row,
                 out_shape=(s, d), out_dtype=F32, resid=h1, resid_spec=row)

    dh2, loss, d_gf = _loss_head(h2, target, gf)

    du_b = _matmul("dout_b", dh2, wb_out, dims=NT, grid=(nt,), a_spec=row, b_spec=whole((d, d)), o_spec=row,
                   out_shape=(s, d), out_dtype=F32)
    d_wb_out = _matmul("dw_out_b", u_b, dh2, dims=TN, grid=(2,),
                       a_spec=whole((s, d)), b_spec=pl.BlockSpec((s, half), lambda j: (0, j)),
                       o_spec=pl.BlockSpec((d, half), lambda j: (0, j)),
                       out_shape=(d, d), out_dtype=F32, also_bf16=True)
    dz_b, dkv, dsum_b, dsinks = _attn_b_bwd(qb, kvx, gate_b, o_b, du_b, lse_b, base_b, sinks)
    ddiag_b = jnp.roll(dsum_b[..., ::-1], -1, axis=-1)
    dn2 = _matmul("dproj_b", dz_b, wb_in, dims=NT, grid=(nt,), parts=4,
                  a_spec=pl.BlockSpec((4, tm, half), lambda i: (0, i, 0)), b_spec=whole((4, d, half)),
                  o_spec=row, out_shape=(s, d), out_dtype=F32)
    d_wb_in = _matmul("dw_in_b", n2, dz_b, dims=TN, grid=(4,),
                      a_spec=whole((s, d)), b_spec=pl.BlockSpec((None, s, half), lambda j: (j, 0, 0)),
                      o_spec=pl.BlockSpec((None, d, half), lambda j: (j, 0, 0)),
                      out_shape=(4, d, half), out_dtype=F32, also_bf16=True)
    dnk = _matmul("dproj_kv", dkv, wkv, dims=NT, grid=(nt,),
                  a_spec=pl.BlockSpec((tm, kvw), lambda i: (i, 0)), b_spec=whole((d, kvw)), o_spec=row,
                  out_shape=(s, d), out_dtype=F32)
    d_wkv = _matmul("dw_kv", nk, dkv, dims=TN, grid=(1,),
                    a_spec=whole((s, d)), b_spec=whole((s, kvw)), o_spec=whole((d, kvw)),
                    out_shape=(d, kvw), out_dtype=F32, also_bf16=True)
    dh1, d_gkb = _norm_bwd("dnorm_kv_b", h1, dh2, [dnk, dn2], jnp.concatenate([gk, gb], axis=0))

    du_a = _matmul("dout_a", dh1, wa_out, dims=NT, grid=(nt,), a_spec=row, b_spec=whole((d, d)), o_spec=row,
                   out_shape=(s, d), out_dtype=F32)
    d_wa_out = _matmul("dw_out_a", u_a, dh1, dims=TN, grid=(2,),
                       a_spec=whole((s, d)), b_spec=pl.BlockSpec((s, half), lambda j: (0, j)),
                       o_spec=pl.BlockSpec((d, half), lambda j: (0, j)),
                       out_shape=(d, d), out_dtype=F32, also_bf16=True)
    early = dict(a_w_out=[g.reshape(4, d // 4, d) for g in d_wa_out],
                 kv_w=[g.reshape(4, d // 4, kvw) for g in d_wkv], b_w_in=list(d_wb_in),
                 b_w_out=[g.reshape(4, d // 4, d) for g in d_wb_out])
    (dz_a, ddiag_a), early_recv = _attn_a_bwd(
        zqkv, gate_a, o_a, du_a, lse_a, diag_a, hosted=_scatter_hosted([early[n][1] for n in early]))
    d_wa_in = _matmul("dw_in_a", n1, dz_a, dims=TN, grid=(4, 2),
                      a_spec=whole((s, d)), b_spec=pl.BlockSpec((None, s, half), lambda j, h: (j, 0, h)),
                      o_spec=pl.BlockSpec((None, d, half), lambda j, h: (j, 0, h)),
                      out_shape=(4, d, d), out_dtype=F32, also_bf16=True)
    late_recv = [_scatter_on_sequencer("scatter_a_w_in", d_wa_in[1])]
    tp = min(TM_PARTS, s)
    dn1 = _matmul("dproj_a", dz_a, wa_in, dims=NT, grid=(s // tp,), parts=4,
                  a_spec=pl.BlockSpec((4, tp, d), lambda i: (0, i, 0)), b_spec=whole((4, d, d)),
                  o_spec=pl.BlockSpec((tp, d), lambda i: (i, 0)), out_shape=(s, d), out_dtype=F32)
    grad_x, d_ga = _norm_bwd("dnorm_a", x, dh1, [dn1], ga)

    small = dict(
        a_norm=d_ga, a_rel_bias=_diag_rows_grad(onehot_a, ddiag_a), kv_norm=d_gkb[0:1],
        t5_bias=_diag_rows_grad(onehot_b, ddiag_b), b_norm=d_gkb[1:2], b_sinks=dsinks[0:1, :HEADS],
        final_norm=d_gf)
    own = dict(a_w_in=d_wa_in[0], **{n: early[n][0] for n in early})
    received = dict(a_w_in=late_recv[0], **dict(zip(early, early_recv)))
    return loss, grad_x, small, own, received


SMALL = ("a_norm", "a_rel_bias", "kv_norm", "t5_bias", "b_norm", "b_sinks", "final_norm")
BIG = ("a_w_in", "a_w_out", "kv_w", "b_w_in", "b_w_out")
ORDER = ("a_norm", "a_w_in", "a_rel_bias", "a_w_out", "kv_norm", "kv_w", "t5_bias", "b_norm", "b_w_in",
         "b_sinks", "b_w_out", "final_norm")


def _pack(parts, rows):
    flat = jnp.concatenate([p.reshape(-1).astype(F32) for p in parts])
    return jnp.pad(flat, (0, rows * LANES - flat.shape[0])).reshape(rows, LANES)


def _unpack(block, shapes):
    flat = block.reshape(-1)
    out, at = [], 0
    for shp in shapes:
        size = int(np.prod(shp))
        out.append(flat[at:at + size].reshape(shp))
        at += size
    return out


def kernel(x, a_norm, a_w_in, a_rel_bias, a_w_out, kv_norm, kv_w, t5_bias, b_norm, b_w_in, b_sinks, b_w_out, final_norm, loss_target, m_a_norm, m_a_w_in, m_a_rel_bias, m_a_w_out, m_kv_norm, m_kv_w, m_t5_bias, m_b_norm, m_b_w_in, m_b_sinks, m_b_w_out, m_final_norm, v_a_norm, v_a_w_in, v_a_rel_bias, v_a_w_out, v_kv_norm, v_kv_w, v_t5_bias, v_b_norm, v_b_w_in, v_b_sinks, v_b_w_out, v_final_norm):
    w = dict(a_norm=a_norm, a_w_in=a_w_in, a_rel_bias=a_rel_bias, a_w_out=a_w_out, kv_norm=kv_norm, kv_w=kv_w,
             t5_bias=t5_bias, b_norm=b_norm, b_w_in=b_w_in, b_sinks=b_sinks, b_w_out=b_w_out,
             final_norm=final_norm)
    m = dict(a_norm=m_a_norm, a_w_in=m_a_w_in, a_rel_bias=m_a_rel_bias, a_w_out=m_a_w_out, kv_norm=m_kv_norm,
             kv_w=m_kv_w, t5_bias=m_t5_bias, b_norm=m_b_norm, b_w_in=m_b_w_in, b_sinks=m_b_sinks,
             b_w_out=m_b_w_out, final_norm=m_final_norm)
    v = dict(a_norm=v_a_norm, a_w_in=v_a_w_in, a_rel_bias=v_a_rel_bias, a_w_out=v_a_w_out, kv_norm=v_kv_norm,
             kv_w=v_kv_w, t5_bias=v_t5_bias, b_norm=v_b_norm, b_w_in=v_b_w_in, b_sinks=v_b_sinks,
             b_w_out=v_b_w_out, final_norm=v_final_norm)
    d = D_MODEL
    chip = 2 * lax.axis_index("x") + lax.axis_index("y")

    shard2d = dict(a_w_in=a_w_in[0], a_w_out=a_w_out[0], kv_w=kv_w, b_w_in=b_w_in[0], b_w_out=b_w_out[0])

    wa_in, ga = _run_alone("allgather_first",
                           _allgather_hosted([shard2d["a_w_in"].astype(BF16), a_norm], [True, False]))
    ga = ga.reshape(1, d)

    loss, grad_x, small, own, received = _local_step(
        x[0], loss_target[0], ga, wa_in, a_rel_bias[0], [shard2d[n].astype(BF16) for n in BIG[1:]],
        kv_norm.reshape(1, d), t5_bias, b_norm, b_sinks, final_norm.reshape(1, d))

    small_shapes = [small[n].shape for n in SMALL] + [(1, 1)]
    total = sum(int(np.prod(s)) for s in small_shapes)
    rows = -(-total // (8 * LANES)) * 8
    reduced = _unpack(_allreduce_small(_pack([small[n] for n in SMALL] + [loss], rows)), small_shapes)
    g_small = dict(zip(SMALL, reduced[:-1]))
    loss_out = reduced[-1].reshape(())
    g_small["a_norm"] = lax.dynamic_slice_in_dim(g_small["a_norm"], chip * (d // 4), d // 4, axis=1)

    core_sums = [
        _sum_partials("sum_" + n, lax.dynamic_index_in_dim(own[n], chip, 0, keepdims=False), received[n])
        for n in BIG]
    sibling_sums = _swap_with_sibling(core_sums)

    out = {}
    for n, mine, theirs in zip(BIG, core_sums, sibling_sums):
        res = _adamw("adamw_" + n, shard2d[n], m[n].reshape(shard2d[n].shape), v[n].reshape(shard2d[n].shape),
                     [mine, theirs])
        out[n] = [r.reshape(w[n].shape) for r in res]
    small_w_shapes = [w[n].shape for n in SMALL]
    total_w = sum(int(np.prod(s)) for s in small_w_shapes)
    rows_w = -(-total_w // (8 * LANES)) * 8
    packed = [_pack([t[n] for n in SMALL], rows_w) for t in (w, m, v)]
    g_packed = _pack([g_small[n] for n in SMALL], rows_w)
    res = _adamw("adamw_small", packed[0], packed[1], packed[2], [g_packed])
    unpacked = [_unpack(r, small_w_shapes) for r in res]
    for i, n in enumerate(SMALL):
        out[n] = [unpacked[k][i] for k in range(4)]

    grads = [out[n][0] for n in ORDER]
    deltas = [out[n][1] for n in ORDER]
    new_m = [out[n][2] for n in ORDER]
    new_v = [out[n][3] for n in ORDER]
    return (loss_out, grad_x[None], *grads, *deltas, *new_m, *new_v)
```

```python
import functools
import math

import jax
import jax.numpy as jnp
import numpy as np
from jax import lax
from jax.experimental import pallas as pl
from jax.experimental.pallas import tpu as pltpu
from jax.experimental.pallas import tpu_sc as plsc

F32 = jnp.float32
BF16 = jnp.bfloat16
MESH = pl.DeviceIdType.MESH

D_MODEL = 1024
HEADS = 16
HEAD_DIM = 64
CHUNK = 64
RMS_EPS = 1e-6
SCALE = HEAD_DIM ** -0.5
A_LEFT_CHUNKS = 8
A_REL_CLIP = 256
B_LEFT_CHUNKS = 2
B_KV_HEADS = 2
B_GROUP = HEADS // B_KV_HEADS
T5_BUCKETS = 32
T5_MAX_DIST = 128
ADAM_LR = 0.001
ADAM_B1 = 0.9
ADAM_B2 = 0.999
ADAM_EPS = 1e-08
ADAM_WD = 0.01
ADAM_STEP = 10

MASKED = -1e30
LANES = 128
TQ = 128
A_PAIRS = 2
A_PAIRS_FWD = 4
KB = 128
A_KBLOCKS = A_LEFT_CHUNKS * CHUNK // KB + 1
B_KBLOCKS = B_LEFT_CHUNKS * CHUNK // KB + 1
A_WIN = A_KBLOCKS * KB
B_WIN = B_KBLOCKS * KB
TM = 512
TM_DENSE = 1024
TM_PARTS = 512
VMEM_LIMIT = 56 * 1024 * 1024

NT = (((1,), (1,)), ((), ()))
TN = (((0,), (0,)), ((), ()))
NN = (((1,), (0,)), ((), ()))


def _params(sem=None):
    return pltpu.CompilerParams(dimension_semantics=sem, vmem_limit_bytes=VMEM_LIMIT)


class _Hosted:
    def __init__(self, inputs, out_shapes, sems, first, middle, last):
        self.inputs, self.out_shapes, self.sems = list(inputs), list(out_shapes), list(sems)
        self.first, self.middle, self.last = first, middle, last


def _call(body, *, name, grid, in_specs, out_specs, out_shape, args, scratch_shapes=(), sem=None, hosted=None):
    in_specs, out_specs, out_shape = list(in_specs), list(out_specs), list(out_shape)
    scratch_shapes = list(scratch_shapes)
    if hosted is None:
        out = pl.pallas_call(
            body, name=name, grid=grid, in_specs=in_specs, out_specs=out_specs, out_shape=out_shape,
            scratch_shapes=scratch_shapes, compiler_params=_params(sem))(*args)
        return list(out), []
    n_in, n_out, n_scr = len(in_specs), len(out_shape), len(scratch_shapes)
    h_in, h_out = len(hosted.inputs), len(hosted.out_shapes)
    total = int(np.prod(grid)) if grid else 1

    def wrapped(*refs):
        ins, refs = refs[:n_in], refs[n_in:]
        h_ins, refs = refs[:h_in], refs[h_in:]
        outs, refs = refs[:n_out], refs[n_out:]
        h_outs, refs = refs[:h_out], refs[h_out:]
        scr, h_sems = refs[:n_scr], refs[n_scr:]
        step = 0
        for axis, size in enumerate(grid):
            step = step * size + pl.program_id(axis)

        @pl.when(step == 0)
        def _():
            hosted.first(h_ins, h_outs, h_sems)

        body(*ins, *outs, *scr)
        if hosted.middle is not None:
            @pl.when(step == total // 2)
            def _():
                hosted.middle(h_ins, h_outs, h_sems)

        @pl.when(step == total - 1)
        def _():
            hosted.last(h_ins, h_outs, h_sems)

    out = pl.pallas_call(
        wrapped, name=name, grid=grid, in_specs=in_specs + [ANY] * h_in, out_specs=out_specs + [ANY] * h_out,
        out_shape=out_shape + hosted.out_shapes, scratch_shapes=scratch_shapes + hosted.sems,
        compiler_params=_params(("arbitrary",) * len(grid)))(*args, *hosted.inputs)
    return list(out[:n_out]), list(out[n_out:])


def _matmul(name, a, b, *, dims, grid, a_spec, b_spec, o_spec, out_shape, out_dtype,
            parts=1, resid=None, resid_spec=None, also_bf16=False, hosted=None, zero_axis=None):
    def body(*refs):
        if zero_axis is None:
            product(*refs)
        else:
            @pl.when(pl.program_id(zero_axis) == 0)
            def _():
                refs[2][...] = jnp.zeros_like(refs[2])

            @pl.when(pl.program_id(zero_axis) > 0)
            def _():
                product(*refs)

    def product(*refs):
        a_ref, b_ref = refs[:2]
        r_ref = refs[2] if resid is not None else None
        o_ref = refs[3] if resid is not None else refs[2]
        if parts == 1:
            prod = lax.dot_general(a_ref[...].astype(BF16), b_ref[...].astype(BF16), dims,
                                   preferred_element_type=F32)
        else:
            prod = None
            for part in range(parts):
                term = lax.dot_general(a_ref[part].astype(BF16), b_ref[part].astype(BF16), dims,
                                       preferred_element_type=F32)
                prod = term if prod is None else prod + term
        if resid is not None:
            prod = r_ref[...] + prod
        o_ref[...] = prod.astype(out_dtype)
        if also_bf16:
            refs[-1][...] = prod.astype(BF16)

    in_specs = [a_spec, b_spec]
    args = [a, b]
    if resid is not None:
        in_specs.append(resid_spec)
        args.append(resid)
    sem = ["parallel"] * len(grid)
    out_specs = [o_spec]
    out_shapes = [jax.ShapeDtypeStruct(out_shape, out_dtype)]
    if also_bf16:
        out_specs.append(o_spec)
        out_shapes.append(jax.ShapeDtypeStruct(out_shape, BF16))
    out, extra = _call(body, name=name, grid=grid, in_specs=in_specs, out_specs=out_specs, out_shape=out_shapes,
                       args=args, sem=tuple(sem), hosted=hosted)
    res = out[0] if not also_bf16 else tuple(out)
    return res if hosted is None else (res, extra)


def _rms_rows(x):
    return lax.rsqrt(jnp.mean(x * x, axis=-1, keepdims=True) + RMS_EPS)


def _norm_fwd(name, x, gains):
    s, d = x.shape
    n = gains.shape[0]

    def body(x_ref, g_ref, *o_refs):
        xv = x_ref[...]
        xh = xv * _rms_rows(xv)
        for i in range(n):
            o_refs[i][...] = (xh * g_ref[i:i + 1, :]).astype(BF16)

    row = pl.BlockSpec((TM, d), lambda i: (i, 0))
    return pl.pallas_call(
        body, name=name, grid=(s // TM,),
        in_specs=[row, pl.BlockSpec((n, d), lambda i: (0, 0))],
        out_specs=[row] * n,
        out_shape=[jax.ShapeDtypeStruct((s, d), BF16)] * n,
        compiler_params=_params(("parallel",)),
    )(x, gains)


def _norm_bwd(name, x, dres, dns, gains):
    s, d = x.shape
    n = len(dns)

    def body(x_ref, r_ref, g_ref, *refs):
        dn_refs, dx_ref, dg_ref = refs[:n], refs[n], refs[n + 1]
        i = pl.program_id(0)
        xv = x_ref[...]
        r = _rms_rows(xv)
        xh = xv * r

        @pl.when(i == 0)
        def _():
            dg_ref[...] = jnp.zeros_like(dg_ref)

        a = None
        for j in range(n):
            dn = dn_refs[j][...]
            t = dn * g_ref[j:j + 1, :]
            a = t if a is None else a + t
            dg_ref[j:j + 1, :] += jnp.sum(dn * xh, axis=0, keepdims=True)
        dx_ref[...] = r_ref[...] + r * (a - xh * jnp.mean(xh * a, axis=-1, keepdims=True))

    row = pl.BlockSpec((TM, d), lambda i: (i, 0))
    small = pl.BlockSpec((n, d), lambda i: (0, 0))
    return pl.pallas_call(
        body, name=name, grid=(s // TM,),
        in_specs=[row, row, small] + [row] * n,
        out_specs=[row, small],
        out_shape=[jax.ShapeDtypeStruct((s, d), F32), jax.ShapeDtypeStruct((n, d), F32)],
        compiler_params=_params(("arbitrary",)),
    )(x, dres, gains, *dns)


def _loss_head(h2, target, gain):
    s, d = h2.shape

    def body(h_ref, t_ref, g_ref, dh_ref, loss_ref, dg_ref):
        i = pl.program_id(0)
        hv = h_ref[...]
        r = _rms_rows(hv)
        hh = hv * r
        g = g_ref[...]
        err = hh * g - t_ref[...]
        part = 0.5 * jnp.sum(jnp.sum(err * err, axis=-1, keepdims=True) * (1.0 / d), axis=0, keepdims=True)
        dy = err * (1.0 / d)
        a = dy * g
        dh_ref[...] = r * (a - hh * jnp.mean(hh * a, axis=-1, keepdims=True))
        dg = jnp.sum(dy * hh, axis=0, keepdims=True)

        @pl.when(i == 0)
        def _():
            loss_ref[...] = part
            dg_ref[...] = dg

        @pl.when(i > 0)
        def _():
            loss_ref[...] += part
            dg_ref[...] += dg

    row = pl.BlockSpec((TM, d), lambda i: (i, 0))
    return pl.pallas_call(
        body, name="loss_head", grid=(s // TM,),
        in_specs=[row, row, pl.BlockSpec((1, d), lambda i: (0, 0))],
        out_specs=[row, pl.BlockSpec((1, 1), lambda i: (0, 0)), pl.BlockSpec((1, d), lambda i: (0, 0))],
        out_shape=[jax.ShapeDtypeStruct((s, d), F32), jax.ShapeDtypeStruct((1, 1), F32),
                   jax.ShapeDtypeStruct((1, d), F32)],
        compiler_params=_params(("arbitrary",)),
    )(h2, target, gain)


def _silu_parts(g):
    sig = jax.nn.sigmoid(g)
    return g * sig, sig * (1.0 + g * (1.0 - sig))


def _lane_lo(rows):
    return lax.broadcasted_iota(jnp.int32, (rows, LANES), 1) < HEAD_DIM


def _stack_pair(x):
    lo = _lane_lo(x.shape[0])
    zero = jnp.zeros_like(x)
    return jnp.concatenate([jnp.where(lo, x, zero), jnp.where(lo, zero, x)], axis=0)


def _unstack_pair(y, w):
    return jnp.where(_lane_lo(w), y[:w], y[w:])


def _block_valid(b, left_blocks, width):
    col = lax.broadcasted_iota(jnp.int32, (1, 2 * width), 1)
    col = jnp.where(col >= width, col - width, col)
    return (col // KB + (b - left_blocks)) >= 0


def _toeplitz_tile(diag_row, width, left_chunks):
    wide = width + TQ
    rolled = pltpu.roll(jnp.broadcast_to(diag_row, (TQ, wide)), 1, 1, stride=1, stride_axis=0)
    i = lax.broadcasted_iota(jnp.int32, (TQ, width), 0) // CHUNK
    j = lax.broadcasted_iota(jnp.int32, (TQ, width), 1) // CHUNK
    dc = i + left_chunks - j
    return jnp.where((dc >= 0) & (dc <= left_chunks), rolled[:, TQ:], MASKED)


def _toeplitz_sum(tile, width):
    flip = (lax.broadcasted_iota(jnp.int32, (TQ, TQ), 0) + lax.broadcasted_iota(jnp.int32, (TQ, TQ), 1)
            == TQ - 1).astype(F32)
    reversed_rows = jnp.dot(flip, tile, precision=lax.Precision.HIGHEST, preferred_element_type=F32)
    padded = jnp.concatenate([reversed_rows, jnp.zeros((TQ, TQ), F32)], axis=1)
    rolled = pltpu.roll(padded, 0, 1, stride=1, stride_axis=0)
    return jnp.sum(rolled, axis=0, keepdims=True)


def _softmax_pair(sc, w, sink=None):
    ps, inv, lses = [], [], []
    for e in range(2):
        sh = sc[:, e * w:(e + 1) * w]
        m = jnp.max(sh, axis=-1, keepdims=True)
        if sink is not None:
            m = jnp.maximum(m, sink[e])
        ex = jnp.exp(sh - m)
        l = jnp.sum(ex, axis=-1, keepdims=True)
        if sink is not None:
            l = l + jnp.exp(sink[e] - m)
        ps.append(ex.astype(BF16))
        inv.append(1.0 / l)
        lses.append(m + jnp.log(l))
    return jnp.concatenate(ps, axis=-1), inv, lses


def _softmax_pair_bwd(sc, dp, lse, delta, w):
    ps, dss = [], []
    for e in range(2):
        p = jnp.exp(sc[:, e * w:(e + 1) * w] - lse[e])
        ps.append(p)
        dss.append(p * (dp[:, e * w:(e + 1) * w] - delta[e]))
    return jnp.concatenate(ps, axis=-1), jnp.concatenate(dss, axis=-1)


def _pair_rowsums(x, lo):
    zero = jnp.zeros_like(x)
    return (jnp.sum(jnp.where(lo, x, zero), axis=-1, keepdims=True),
            jnp.sum(jnp.where(lo, zero, x), axis=-1, keepdims=True))


def _a_qkv_specs(rows, pad, pw):
    return [pl.BlockSpec((None, TQ, pw), lambda p, b: (0, b + pad // TQ, p)),
            pl.BlockSpec((None, rows, pw), lambda p, b: (1, 0, p)),
            pl.BlockSpec((None, rows, pw), lambda p, b: (2, 0, p))]


def _window(ref, b, pad, win, lanes):
    start = pl.multiple_of(b * TQ + pad - (win - TQ), KB)
    return ref[pl.ds(start, win), lanes]


def _attn_a_fwd(zqkv, g, diag, hosted=None):
    s = g.shape[0]
    pad = zqkv.shape[1] - s
    nb = s // TQ
    left = A_KBLOCKS - 1
    pairs = A_PAIRS_FWD
    pw = pairs * LANES
    wide = A_WIN + TQ

    def body(q_ref, k_ref, v_ref, g_ref, diag_ref, o_ref, u_ref, lse_ref, bias_scr):
        b = pl.program_id(1)

        @pl.when(b == 0)
        def _():
            for hh in range(2 * pairs):
                bias_scr[hh // 2, :, (hh % 2) * A_WIN:(hh % 2 + 1) * A_WIN] = _toeplitz_tile(
                    diag_ref[hh], A_WIN, A_LEFT_CHUNKS)

        def step(first_blocks):
            lo = _lane_lo(TQ)
            for pp in range(pairs):
                ln = slice(pp * LANES, (pp + 1) * LANES)
                kcat = _stack_pair(_window(k_ref, b, pad, A_WIN, ln))
                vcat = _stack_pair(_window(v_ref, b, pad, A_WIN, ln))
                sc = lax.dot_general(q_ref[:, ln] * SCALE, kcat, NT, preferred_element_type=F32) + bias_scr[pp]
                if first_blocks:
                    sc = jnp.where(_block_valid(b, left, A_WIN), sc, MASKED)
                p, inv, lses = _softmax_pair(sc, A_WIN)
                ov = jnp.dot(p, vcat, preferred_element_type=F32) * jnp.where(lo, inv[0], inv[1])
                o_ref[:, ln] = ov
                lse_ref[pp] = jnp.where(lo, lses[0], lses[1])
                sg, _ = _silu_parts(g_ref[:, ln])
                u_ref[:, ln] = (ov * sg).astype(BF16)

        @pl.when(b < left)
        def _():
            step(True)

        @pl.when(b >= left)
        def _():
            step(False)

    tile = pl.BlockSpec((TQ, pw), lambda p, b: (b, p))
    return _call(
        body, name="attn_a_fwd", grid=(HEADS // 2 // pairs, nb),
        in_specs=_a_qkv_specs(pad + s, pad, pw) + [
            tile, pl.BlockSpec((2 * pairs, 1, wide), lambda p, b: (p, 0, 0))],
        out_specs=[tile, tile, pl.BlockSpec((pairs, TQ, LANES), lambda p, b: (p, b, 0))],
        out_shape=[jax.ShapeDtypeStruct((s, D_MODEL), F32), jax.ShapeDtypeStruct((s, D_MODEL), BF16),
                   jax.ShapeDtypeStruct((HEADS // 2, s, LANES), F32)],
        scratch_shapes=[pltpu.VMEM((pairs, TQ, 2 * A_WIN), F32)],
        sem=("parallel", "arbitrary"), hosted=hosted,
        args=(zqkv, zqkv, zqkv, g, diag))


def _attn_a_bwd(zqkv, g, o, du, lse, diag, hosted=None):
    s = g.shape[0]
    pad = zqkv.shape[1] - s
    nb = s // TQ
    left = A_KBLOCKS - 1
    pw = A_PAIRS * LANES
    wide = A_WIN + TQ

    def body(q_ref, k_ref, v_ref, g_ref, o_ref, du_ref, lse_ref, diag_ref, dz_ref, ddiag_ref,
             bias_scr, dbias_acc, dk_acc, dv_acc):
        b = pl.program_id(1)

        @pl.when(b == 0)
        def _():
            for hh in range(2 * A_PAIRS):
                bias_scr[hh // 2, :, (hh % 2) * A_WIN:(hh % 2 + 1) * A_WIN] = _toeplitz_tile(
                    diag_ref[hh], A_WIN, A_LEFT_CHUNKS)
            dbias_acc[...] = jnp.zeros_like(dbias_acc)
            dk_acc[...] = jnp.zeros_like(dk_acc)
            dv_acc[...] = jnp.zeros_like(dv_acc)

        def step(first_blocks):
            lo = _lane_lo(TQ)
            upper = lax.broadcasted_iota(jnp.int32, (LANES, A_WIN), 0) < HEAD_DIM
            rows = pl.ds(pl.multiple_of(b * TQ, TQ), TQ)
            sg, dsg = _silu_parts(g_ref[...])
            duv = du_ref[...]
            ov = o_ref[...]
            do = duv * sg
            dz_ref[3, rows, :] = (duv * ov * dsg).astype(BF16)
            do_o = do * ov
            do_bf = do.astype(BF16)
            for pp in range(A_PAIRS):
                ln = slice(pp * LANES, (pp + 1) * LANES)
                q = q_ref[:, ln] * SCALE
                kcat = _stack_pair(_window(k_ref, b, pad, A_WIN, ln))
                vcat = _stack_pair(_window(v_ref, b, pad, A_WIN, ln))
                sc = lax.dot_general(q, kcat, NT, preferred_element_type=F32) + bias_scr[pp]
                if first_blocks:
                    sc = jnp.where(_block_valid(b, left, A_WIN), sc, MASKED)
                lse_t = lse_ref[pp]
                dp = lax.dot_general(do_bf[:, ln], vcat, NT, preferred_element_type=F32)
                p, ds = _softmax_pair_bwd(sc, dp, (lse_t[:, 0:1], lse_t[:, HEAD_DIM:HEAD_DIM + 1]),
                                          _pair_rowsums(do_o[:, ln], lo), A_WIN)
                dbias_acc[pp] += ds
                dsb = ds.astype(BF16)
                dz_ref[0, rows, ln] = (jnp.dot(dsb, kcat, preferred_element_type=F32) * SCALE).astype(BF16)
                dkt = lax.dot_general(q, dsb, TN, preferred_element_type=F32)
                dvt = lax.dot_general(do_bf[:, ln], p.astype(BF16), TN, preferred_element_type=F32)
                dkt = jnp.where(upper, dkt[:, :A_WIN], dkt[:, A_WIN:])
                dvt = jnp.where(upper, dvt[:, :A_WIN], dvt[:, A_WIN:])
                for t in range(A_KBLOCKS):
                    blk = b + (pad // KB - left + t)
                    dk_acc[blk, ln, :] += dkt[:, t * KB:(t + 1) * KB]
                    dv_acc[blk, ln, :] += dvt[:, t * KB:(t + 1) * KB]

        @pl.when(b < left)
        def _():
            step(True)

        @pl.when(b >= left)
        def _():
            step(False)

        @pl.when(b == nb - 1)
        def _():
            for kb in range(s // KB):
                dz_ref[1, kb * KB:(kb + 1) * KB, :] = dk_acc[pad // KB + kb].T.astype(BF16)
                dz_ref[2, kb * KB:(kb + 1) * KB, :] = dv_acc[pad // KB + kb].T.astype(BF16)
            for hh in range(2 * A_PAIRS):
                ddiag_ref[hh] = _toeplitz_sum(
                    dbias_acc[hh // 2, :, (hh % 2) * A_WIN:(hh % 2 + 1) * A_WIN], A_WIN)

    tile = pl.BlockSpec((TQ, pw), lambda p, b: (b, p))
    diag_spec = pl.BlockSpec((2 * A_PAIRS, 1, wide), lambda p, b: (p, 0, 0))
    return _call(
        body, name="attn_a_bwd", grid=(HEADS // 2 // A_PAIRS, nb),
        in_specs=_a_qkv_specs(pad + s, pad, pw) + [
            tile, tile, tile, pl.BlockSpec((A_PAIRS, TQ, LANES), lambda p, b: (p, b, 0)), diag_spec],
        out_specs=[pl.BlockSpec((4, s, pw), lambda p, b: (0, 0, p)), diag_spec],
        out_shape=[jax.ShapeDtypeStruct((4, s, D_MODEL), BF16),
                   jax.ShapeDtypeStruct((HEADS, 1, wide), F32)],
        scratch_shapes=[pltpu.VMEM((A_PAIRS, TQ, 2 * A_WIN), F32), pltpu.VMEM((A_PAIRS, TQ, 2 * A_WIN), F32),
                        pltpu.VMEM(((pad + s) // KB, pw, KB), F32), pltpu.VMEM(((pad + s) // KB, pw, KB), F32)],
        sem=("parallel", "arbitrary"), hosted=hosted,
        args=(zqkv, zqkv, zqkv, g, o, du, lse, diag))


B_STACK = B_GROUP // 2
B_KVX = 4 * LANES
B_ROWS = B_STACK * TQ
B_WIDE = B_WIN + TQ


def _b_head_place(h):
    return h // B_GROUP, (h % B_GROUP) // 2, h % 2


def _toeplitz_tile_t(base_row, width, left_chunks):
    wide = width + TQ
    rolled = pltpu.roll(jnp.broadcast_to(base_row, (width, wide)), 0, 1, stride=1, stride_axis=0)
    j = lax.broadcasted_iota(jnp.int32, (width, TQ), 0) // CHUNK
    i = lax.broadcasted_iota(jnp.int32, (width, TQ), 1) // CHUNK
    dc = i + left_chunks - j
    return jnp.where((dc >= 0) & (dc <= left_chunks), rolled[:, :TQ], MASKED)


def _toeplitz_sum_t(tile_t, width):
    flip = (lax.broadcasted_iota(jnp.int32, (width, width), 0) + lax.broadcasted_iota(jnp.int32, (width, width), 1)
            == width - 1).astype(F32)
    reversed_rows = jnp.dot(flip, tile_t, precision=lax.Precision.HIGHEST, preferred_element_type=F32)
    padded = jnp.concatenate([reversed_rows, jnp.zeros((width, width), F32)], axis=1)
    rolled = pltpu.roll(padded, 0, 1, stride=1, stride_axis=0)
    return jnp.sum(rolled, axis=0, keepdims=True)


def _b_build_bias(base_ref, bias_scr):
    for h in range(HEADS):
        gi, pr, e = _b_head_place(h)
        bias_scr[gi, e * B_WIN:(e + 1) * B_WIN, pr * TQ:(pr + 1) * TQ] = _toeplitz_tile_t(
            base_ref[h], B_WIN, B_LEFT_CHUNKS)


def _b_stack(x, gi):
    return jnp.concatenate(
        [x[:, (B_STACK * gi + pr) * LANES:(B_STACK * gi + pr + 1) * LANES] for pr in range(B_STACK)], axis=0)


def _b_sink_rows(sink_ref, gi):
    block = lax.broadcasted_iota(jnp.int32, (1, B_ROWS), 1) // TQ
    rows = []
    for e in range(2):
        row = jnp.zeros((1, B_ROWS), F32)
        for pr in range(B_STACK):
            h = B_GROUP * gi + 2 * pr + e
            row = jnp.where(block == pr, sink_ref[0:1, h:h + 1], row)
        rows.append(row)
    return rows


def _b_scores_t(q_ref, kvv, bias_scr, gi, b, left, first_blocks):
    kcat = _stack_pair(kvv[:, gi * LANES:(gi + 1) * LANES])
    vcat = _stack_pair(kvv[:, (B_KV_HEADS + gi) * LANES:(B_KV_HEADS + gi + 1) * LANES])
    qs = _b_stack(q_ref, gi) * SCALE
    sc = lax.dot_general(kcat, qs, NT, preferred_element_type=F32) + bias_scr[gi]
    if first_blocks:
        row = lax.broadcasted_iota(jnp.int32, (2 * B_WIN, 1), 0)
        row = jnp.where(row >= B_WIN, row - B_WIN, row)
        sc = jnp.where((row // KB + (b - left)) >= 0, sc, MASKED)
    return kcat, vcat, qs, sc


def _attn_b_fwd(qb, kvx, gate, base, sinks):
    s = qb.shape[0]
    pad = kvx.shape[0] - s
    nb = s // TQ
    left = B_KBLOCKS - 1

    def body(q_ref, kv_ref, g_ref, base_ref, sink_ref, o_ref, u_ref, lse_ref, bias_scr):
        b = pl.program_id(0)

        @pl.when(b == 0)
        def _():
            _b_build_bias(base_ref, bias_scr)

        def step(first_blocks):
            kvv = _window(kv_ref, b, pad, B_WIN, slice(None))
            upper = lax.broadcasted_iota(jnp.int32, (LANES, B_ROWS), 0) < HEAD_DIM
            lse_rows = []
            for gi in range(B_KV_HEADS):
                kcat, vcat, qs, sc = _b_scores_t(q_ref, kvv, bias_scr, gi, b, left, first_blocks)
                sink = _b_sink_rows(sink_ref, gi)
                ps, inv = [], []
                for e in range(2):
                    sh = sc[e * B_WIN:(e + 1) * B_WIN]
                    m = jnp.maximum(jnp.max(sh, axis=0, keepdims=True), sink[e])
                    ex = jnp.exp(sh - m)
                    l = jnp.sum(ex, axis=0, keepdims=True) + jnp.exp(sink[e] - m)
                    ps.append(ex.astype(BF16))
                    inv.append(1.0 / l)
                    lse_rows.append(m + jnp.log(l))
                pt = jnp.concatenate(ps, axis=0)
                ot = lax.dot_general(vcat, pt, TN, preferred_element_type=F32) * jnp.where(upper, inv[0], inv[1])
                ov = ot.T
                for pr in range(B_STACK):
                    pair = B_STACK * gi + pr
                    o_ref[:, pair * LANES:(pair + 1) * LANES] = ov[pr * TQ:(pr + 1) * TQ]
            lse_ref[0] = jnp.concatenate(lse_rows + [jnp.zeros((8 - len(lse_rows), B_ROWS), F32)], axis=0)
            sg, _ = _silu_parts(g_ref[...])
            u_ref[...] = (o_ref[...] * sg).astype(BF16)

        @pl.when(b < left)
        def _():
            step(True)

        @pl.when(b >= left)
        def _():
            step(False)

    row = pl.BlockSpec((TQ, D_MODEL), lambda b: (b, 0))
    return pl.pallas_call(
        body, name="attn_b_fwd", grid=(nb,),
        in_specs=[row, pl.BlockSpec((pad + s, B_KVX), lambda b: (0, 0)), row,
                  pl.BlockSpec((HEADS, 1, B_WIDE), lambda b: (0, 0, 0)), pl.BlockSpec((1, HEADS), lambda b: (0, 0))],
        out_specs=[row, row, pl.BlockSpec((1, 8, B_ROWS), lambda b: (b, 0, 0))],
        out_shape=[jax.ShapeDtypeStruct((s, D_MODEL), F32), jax.ShapeDtypeStruct((s, D_MODEL), BF16),
                   jax.ShapeDtypeStruct((nb, 8, B_ROWS), F32)],
        scratch_shapes=[pltpu.VMEM((B_KV_HEADS, 2 * B_WIN, B_ROWS), F32)],
        compiler_params=_params(("arbitrary",)),
    )(qb, kvx, gate, base, sinks)


def _attn_b_bwd(qb, kvx, gate, o, du, lse, base, sinks):
    s = qb.shape[0]
    pad = kvx.shape[0] - s
    nb = s // TQ
    left = B_KBLOCKS - 1
    half = D_MODEL // 2

    def body(q_ref, kv_ref, g_ref, o_ref, du_ref, lse_ref, base_ref, sink_ref, dz_ref, dkv_ref, dsum_ref,
             dsink_ref, bias_scr, dbias_acc, dkv_acc, dsink_acc):
        b = pl.program_id(0)

        @pl.when(b == 0)
        def _():
            _b_build_bias(base_ref, bias_scr)
            dbias_acc[...] = jnp.zeros_like(dbias_acc)
            dkv_acc[...] = jnp.zeros_like(dkv_acc)
            dsink_acc[...] = jnp.zeros_like(dsink_acc)

        def step(first_blocks):
            kvv = _window(kv_ref, b, pad, B_WIN, slice(None))
            sg, dsg = _silu_parts(g_ref[...])
            duv = du_ref[...]
            ov = o_ref[...]
            do = duv * sg
            dgate = (duv * ov * dsg).astype(BF16)
            dz_ref[2] = dgate[:, :half]
            dz_ref[3] = dgate[:, half:]
            do_o = do * ov
            do_bf = do.astype(BF16)
            lse_all = lse_ref[0]
            dsink_rows = []
            for gi in range(B_KV_HEADS):
                kcat, vcat, qs, sc = _b_scores_t(q_ref, kvv, bias_scr, gi, b, left, first_blocks)
                dos = _b_stack(do_bf, gi)
                doo_t = _b_stack(do_o, gi).T
                delta = (jnp.sum(doo_t[:HEAD_DIM], axis=0, keepdims=True),
                         jnp.sum(doo_t[HEAD_DIM:], axis=0, keepdims=True))
                sink = _b_sink_rows(sink_ref, gi)
                dp = lax.dot_general(vcat, dos, NT, preferred_element_type=F32)
                ps, dss = [], []
                for e in range(2):
                    lse_e = lse_all[2 * gi + e:2 * gi + e + 1]
                    delta_e = delta[e]
                    p = jnp.exp(sc[e * B_WIN:(e + 1) * B_WIN] - lse_e)
                    ps.append(p.astype(BF16))
                    dss.append(p * (dp[e * B_WIN:(e + 1) * B_WIN] - delta_e))
                    dsink_rows.append(-jnp.exp(sink[e] - lse_e) * delta_e)
                ds = jnp.concatenate(dss, axis=0)
                dbias_acc[gi] += ds
                dsb = ds.astype(BF16)
                dq = (lax.dot_general(kcat, dsb, TN, preferred_element_type=F32) * SCALE).T.astype(BF16)
                for pr in range(B_STACK):
                    dz_ref[gi, :, pr * LANES:(pr + 1) * LANES] = dq[pr * TQ:(pr + 1) * TQ]
                dk = _unstack_pair(jnp.dot(dsb, qs, preferred_element_type=F32), B_WIN)
                dv = _unstack_pair(jnp.dot(jnp.concatenate(ps, axis=0), dos, preferred_element_type=F32), B_WIN)
                krows = pl.ds(pl.multiple_of(b * TQ + pad - (B_WIN - TQ), KB), B_WIN)
                dkv_acc[krows, gi * LANES:(gi + 1) * LANES] += dk
                dkv_acc[krows, (B_KV_HEADS + gi) * LANES:(B_KV_HEADS + gi + 1) * LANES] += dv
            dsink_acc[...] += jnp.concatenate(
                dsink_rows + [jnp.zeros((8 - len(dsink_rows), B_ROWS), F32)], axis=0)

        @pl.when(b < left)
        def _():
            step(True)

        @pl.when(b >= left)
        def _():
            step(False)

        @pl.when(b == nb - 1)
        def _():
            lo_s = _lane_lo(s)
            for which in range(2):
                folded = []
                for gi in range(B_KV_HEADS):
                    part = dkv_acc[pad:pad + s, (which * B_KV_HEADS + gi) * LANES:(which * B_KV_HEADS + gi + 1) * LANES]
                    folded.append(part + pltpu.roll(part, HEAD_DIM, 1))
                dkv_ref[:, which * LANES:(which + 1) * LANES] = jnp.where(lo_s, folded[0], folded[1]).astype(BF16)
            lane8 = lax.broadcasted_iota(jnp.int32, dsink_ref.shape, 1)
            tot = jnp.zeros(dsink_ref.shape, F32)
            for h in range(HEADS):
                gi, pr, e = _b_head_place(h)
                dsum_ref[h] = _toeplitz_sum_t(
                    dbias_acc[gi, e * B_WIN:(e + 1) * B_WIN, pr * TQ:(pr + 1) * TQ], B_WIN)
                per_query = dsink_acc[2 * gi + e:2 * gi + e + 1, pr * TQ:(pr + 1) * TQ]
                tot = jnp.where(lane8 == h, jnp.sum(per_query, axis=1, keepdims=True), tot)
            dsink_ref[...] = tot

    row = pl.BlockSpec((TQ, D_MODEL), lambda b: (b, 0))
    base_spec = pl.BlockSpec((HEADS, 1, B_WIDE), lambda b: (0, 0, 0))
    return pl.pallas_call(
        body, name="attn_b_bwd", grid=(nb,),
        in_specs=[row, pl.BlockSpec((pad + s, B_KVX), lambda b: (0, 0)), row, row, row,
                  pl.BlockSpec((1, 8, B_ROWS), lambda b: (b, 0, 0)), base_spec,
                  pl.BlockSpec((1, HEADS), lambda b: (0, 0))],
        out_specs=[pl.BlockSpec((4, TQ, half), lambda b: (0, b, 0)),
                   pl.BlockSpec((s, 2 * LANES), lambda b: (0, 0)), base_spec,
                   pl.BlockSpec((8, LANES), lambda b: (0, 0))],
        out_shape=[jax.ShapeDtypeStruct((4, s, half), BF16), jax.ShapeDtypeStruct((s, 2 * LANES), BF16),
                   jax.ShapeDtypeStruct((HEADS, 1, B_WIDE), F32), jax.ShapeDtypeStruct((8, LANES), F32)],
        scratch_shapes=[pltpu.VMEM((B_KV_HEADS, 2 * B_WIN, B_ROWS), F32),
                        pltpu.VMEM((B_KV_HEADS, 2 * B_WIN, B_ROWS), F32),
                        pltpu.VMEM((pad + s, B_KVX), F32), pltpu.VMEM((8, B_ROWS), F32)],
        compiler_params=_params(("arbitrary",)),
    )(qb, kvx, gate, o, du, lse, base, sinks)


def _t5_bucket(rel):
    nb = T5_BUCKETS // 2
    max_exact = nb // 2
    ret = jnp.where(rel > 0, nb, 0)
    n = jnp.abs(rel)
    nf = jnp.maximum(n, 1).astype(jnp.float32)
    large = max_exact + (jnp.log(nf / max_exact) / math.log(T5_MAX_DIST / max_exact)
                         * (nb - max_exact)).astype(jnp.int32)
    large = jnp.minimum(large, nb - 1)
    return ret + jnp.where(n < max_exact, n, large)


def _a_offset_onehot():
    c = np.arange(A_WIN + TQ)
    dist = A_LEFT_CHUNKS * CHUNK + TQ - 1 - c
    idx = np.clip(dist, -A_REL_CLIP, A_REL_CLIP) + A_REL_CLIP
    onehot = np.zeros((A_WIN + TQ, 2 * A_REL_CLIP + 1), np.float32)
    onehot[c, idx] = 1.0
    return jnp.asarray(onehot)


def _b_offset_onehot():
    c = jnp.arange(B_WIN + TQ, dtype=jnp.int32)
    rel = c - (TQ - 1) - B_LEFT_CHUNKS * CHUNK
    return (_t5_bucket(rel)[:, None] == jnp.arange(T5_BUCKETS)[None, :]).astype(F32)


def _diag_rows(onehot, table):
    rows = jnp.dot(onehot, table.astype(F32), precision=lax.Precision.HIGHEST)
    return rows.T.reshape(HEADS, 1, onehot.shape[0])


def _diag_rows_grad(onehot, ddiag):
    return jnp.dot(ddiag.reshape(HEADS, onehot.shape[0]), onehot, precision=lax.Precision.HIGHEST).T


def _position():
    x, y, c = lax.axis_index("x"), lax.axis_index("y"), lax.axis_index("c")
    chips = [(1 - x, y), (x, 1 - y), (1 - x, 1 - y)]
    return x, y, c, chips


ANY = pl.BlockSpec(memory_space=pl.ANY)


def _allgather_hosted(shards, split):
    n = len(shards)

    def part(ref, t, half):
        if not split[t]:
            return ref
        rows = shards[t].shape[0] // 2
        return ref.at[pl.ds(half * rows, rows)]

    def copies(kind, ins, outs, sems):
        send_sems, recv_sems, pass_send, pass_recv, local_sems = sems
        x, y, c, chips = _position()
        mine = 2 * x + y
        if kind == "local":
            return [pltpu.make_async_copy(ins[t], outs[t].at[mine], local_sems.at[t]) for t in range(n)]
        made = []
        for t in range(n):
            for j, chip in enumerate(chips):
                theirs = 2 * chip[0] + chip[1]
                far = dict(send_sem=send_sems.at[3 * t + j], recv_sem=recv_sems.at[3 * t + j],
                           device_id=(chip[0], chip[1], c), device_id_type=MESH)
                near = dict(send_sem=pass_send.at[3 * t + j], recv_sem=pass_recv.at[3 * t + j],
                            device_id=(x, y, 1 - c), device_id_type=MESH)
                here = part(outs[t].at[theirs], t, c)
                if kind == "send":
                    made.append(pltpu.make_async_remote_copy(
                        src_ref=part(ins[t], t, c), dst_ref=part(outs[t].at[mine], t, c), **far))
                elif kind == "landed":
                    made.append(pltpu.make_async_remote_copy(src_ref=here, dst_ref=here, **far))
                elif not split[t]:
                    made.append(None)
                elif kind == "pass":
                    made.append(pltpu.make_async_remote_copy(src_ref=here, dst_ref=here, **near))
                else:
                    other = part(outs[t].at[theirs], t, 1 - c)
                    made.append(pltpu.make_async_remote_copy(src_ref=other, dst_ref=other, **near))
        return made

    def first(ins, outs, sems):
        for cp in copies("local", ins, outs, sems) + copies("send", ins, outs, sems):
            cp.start()

    def middle(ins, outs, sems):
        for got, cp in zip(copies("landed", ins, outs, sems), copies("pass", ins, outs, sems)):
            got.wait_recv()
            if cp is not None:
                cp.start()

    def last(ins, outs, sems):
        for cp in copies("passed", ins, outs, sems):
            if cp is not None:
                cp.wait_recv()
        for cp in copies("send", ins, outs, sems) + copies("pass", ins, outs, sems):
            if cp is not None:
                cp.wait_send()
        for cp in copies("local", ins, outs, sems):
            cp.wait()

    return _Hosted(shards, [jax.ShapeDtypeStruct((4,) + w.shape, w.dtype) for w in shards],
                   [pltpu.SemaphoreType.DMA((3 * n,))] * 4 + [pltpu.SemaphoreType.DMA((n,))],
                   first, middle, last)


def _scatter_hosted(grads):
    n = len(grads)

    def copies(ins, outs, sems):
        send_sems, recv_sems = sems
        x, y, c, chips = _position()
        return [pltpu.make_async_remote_copy(
            src_ref=ins[t].at[2 * chip[0] + chip[1]], dst_ref=outs[t].at[j],
            send_sem=send_sems.at[3 * t + j], recv_sem=recv_sems.at[3 * t + j],
            device_id=(chip[0], chip[1], c), device_id_type=MESH)
            for t in range(n) for j, chip in enumerate(chips)]

    def first(ins, outs, sems):
        for cp in copies(ins, outs, sems):
            cp.start()

    def last(ins, outs, sems):
        for cp in copies(ins, outs, sems):
            cp.wait()

    return _Hosted(grads, [jax.ShapeDtypeStruct((3,) + g.shape[1:], g.dtype) for g in grads],
                   [pltpu.SemaphoreType.DMA((3 * n,))] * 2, first, None, last)


def _scatter_on_sequencer(name, grad):
    src = jax.new_ref(grad, memory_space=pltpu.MemorySpace.HBM)
    dst = jax.empty_ref(jax.ShapeDtypeStruct((3,) + grad.shape[1:], grad.dtype),
                        memory_space=pltpu.MemorySpace.HBM)

    @pl.kernel(mesh=plsc.ScalarSubcoreMesh(axis_name="sequencer", num_cores=1), name=name,
               scratch_types=(pltpu.SemaphoreType.DMA((3,)), pltpu.SemaphoreType.DMA((3,))),
               compiler_params=pltpu.CompilerParams(collective_id=0))
    def launch(send_sems, recv_sems):
        x, y, c, chips = _position()
        barrier = pltpu.get_barrier_semaphore()
        for chip in chips:
            pl.semaphore_signal(barrier, inc=1, device_id=(chip[0], chip[1], c), device_id_type=MESH)
        pl.semaphore_wait(barrier, len(chips))
        copies = [pltpu.make_async_remote_copy(
            src_ref=src.at[2 * chip[0] + chip[1]], dst_ref=dst.at[j], send_sem=send_sems.at[j],
            recv_sem=recv_sems.at[j], device_id=(chip[0], chip[1], c), device_id_type=MESH)
            for j, chip in enumerate(chips)]
        for cp in copies:
            cp.start()
        for cp in copies:
            cp.wait()

    launch()
    return dst[...]


def _run_on_sequencer(name, hosted):
    ins = [jax.new_ref(a, memory_space=pltpu.MemorySpace.HBM) for a in hosted.inputs]
    outs = [jax.empty_ref(shape, memory_space=pltpu.MemorySpace.HBM) for shape in hosted.out_shapes]

    @pl.kernel(mesh=plsc.ScalarSubcoreMesh(axis_name="sequencer", num_cores=1), name=name,
               scratch_types=tuple(hosted.sems), compiler_params=pltpu.CompilerParams(collective_id=1))
    def launch(*sems):
        x, y, c, chips = _position()
        peers = [(chip[0], chip[1], c) for chip in chips] + [(x, y, 1 - c)]
        barrier = pltpu.get_barrier_semaphore()
        for peer in peers:
            pl.semaphore_signal(barrier, inc=1, device_id=peer, device_id_type=MESH)
        pl.semaphore_wait(barrier, len(peers))
        hosted.first(ins, outs, sems)
        hosted.middle(ins, outs, sems)
        hosted.last(ins, outs, sems)

    launch()
    return [o[...] for o in outs]


def _run_alone(name, hosted):
    n_in = len(hosted.inputs)
    n_out = len(hosted.out_shapes)

    def body(*refs):
        ins, outs, sems = refs[:n_in], refs[n_in:n_in + n_out], refs[n_in + n_out:]
        hosted.first(ins, outs, sems)
        if hosted.middle is not None:
            hosted.middle(ins, outs, sems)
        hosted.last(ins, outs, sems)

    return pl.pallas_call(
        body, name=name, in_specs=[ANY] * n_in, out_specs=[ANY] * n_out, out_shape=hosted.out_shapes,
        scratch_shapes=hosted.sems)(*hosted.inputs)


def _swap_with_sibling(blocks):
    n = len(blocks)

    def body(*refs):
        ins, outs = refs[:n], refs[n:2 * n]
        send_sems, recv_sems = refs[2 * n:]
        x, y, c, _ = _position()
        sends = [pltpu.make_async_remote_copy(
            src_ref=ins[t], dst_ref=outs[t], send_sem=send_sems.at[t], recv_sem=recv_sems.at[t],
            device_id=(x, y, 1 - c), device_id_type=MESH) for t in range(n)]
        for cp in sends:
            cp.start()
        for cp in sends:
            cp.wait()

    return pl.pallas_call(
        body, name="swap_with_sibling",
        in_specs=[ANY] * n, out_specs=[ANY] * n,
        out_shape=[jax.ShapeDtypeStruct(b.shape, b.dtype) for b in blocks],
        scratch_shapes=[pltpu.SemaphoreType.DMA((n,))] * 2,
    )(*blocks)


def _allreduce_small(block):
    rows = block.shape[0]

    def body(in_ref, sum_ref, all_ref, send_sems, recv_sems):
        x, y, c, _ = _position()
        me = 4 * x + 2 * y + c
        all_ref[me] = in_ref[...]
        sends = []
        for k in range(1, 8):
            peer = (x ^ (k >> 2), y ^ ((k >> 1) & 1), c ^ (k & 1))
            sends.append(pltpu.make_async_remote_copy(
                src_ref=in_ref, dst_ref=all_ref.at[me], send_sem=send_sems.at[k - 1],
                recv_sem=recv_sems.at[k - 1], device_id=peer, device_id_type=MESH))
        for cp in sends:
            cp.start()
        for k in range(1, 8):
            theirs = me ^ k
            pltpu.make_async_remote_copy(
                src_ref=in_ref, dst_ref=all_ref.at[theirs], send_sem=send_sems.at[k - 1],
                recv_sem=recv_sems.at[k - 1], device_id=(x, y, c), device_id_type=MESH).wait_recv()
        for cp in sends:
            cp.wait_send()
        acc = all_ref[0]
        for d in range(1, 8):
            acc = acc + all_ref[d]
        sum_ref[...] = acc

    vmem = pl.BlockSpec(memory_space=pltpu.VMEM)
    return pl.pallas_call(
        body, name="allreduce_small",
        in_specs=[vmem], out_specs=[vmem, vmem],
        out_shape=[jax.ShapeDtypeStruct((rows, LANES), F32), jax.ShapeDtypeStruct((8, rows, LANES), F32)],
        scratch_shapes=[pltpu.SemaphoreType.DMA((7,))] * 2,
    )(block)[0]


def _adamw_math(w, g, m, v):
    m = ADAM_B1 * m + (1.0 - ADAM_B1) * g
    v = ADAM_B2 * v + (1.0 - ADAM_B2) * (g * g)
    m_hat = m / (1.0 - ADAM_B1 ** ADAM_STEP)
    v_hat = v / (1.0 - ADAM_B2 ** ADAM_STEP)
    delta = -ADAM_LR * (m_hat / (jnp.sqrt(v_hat) + ADAM_EPS) + ADAM_WD * w)
    return delta, m, v


def _row_tile(rows):
    return min(rows, 256)


def _sum_partials(name, own, recv):
    rows, cols = own.shape
    tr = _row_tile(rows)

    def body(own_ref, recv_ref, o_ref):
        acc = own_ref[...]
        for j in range(3):
            acc = acc + recv_ref[j].astype(F32)
        o_ref[...] = acc

    return pl.pallas_call(
        body, name=name, grid=(rows // tr,),
        in_specs=[pl.BlockSpec((tr, cols), lambda i: (i, 0)), pl.BlockSpec((3, tr, cols), lambda i: (0, i, 0))],
        out_specs=pl.BlockSpec((tr, cols), lambda i: (i, 0)),
        out_shape=jax.ShapeDtypeStruct((rows, cols), F32),
        compiler_params=_params(("parallel",)),
    )(own, recv)


def _adamw(name, w, m, v, g_parts):
    rows, cols = w.shape
    tr = _row_tile(rows)
    n = len(g_parts)

    def body(w_ref, m_ref, v_ref, *refs):
        g_refs = refs[:n]
        go_ref, d_ref, mo_ref, vo_ref = refs[n:]
        g = g_refs[0][...]
        for r in g_refs[1:]:
            g = g + r[...]
        delta, mn, vn = _adamw_math(w_ref[...], g, m_ref[...], v_ref[...])
        go_ref[...] = g
        d_ref[...] = delta
        mo_ref[...] = mn
        vo_ref[...] = vn

    spec = pl.BlockSpec((tr, cols), lambda i: (i, 0))
    return pl.pallas_call(
        body, name=name, grid=(rows // tr,),
        in_specs=[spec] * (3 + n), out_specs=[spec] * 4,
        out_shape=[jax.ShapeDtypeStruct((rows, cols), F32)] * 4,
        compiler_params=_params(("parallel",)),
    )(w, m, v, *g_parts)


def _local_step(x, target, ga, wa_in, rel_bias, later_shards, gk, t5, gb, sinks, gf):
    s, d = x.shape
    tm = min(TM_DENSE, s)
    nt = s // tm
    half = d // 2
    row = pl.BlockSpec((tm, d), lambda i: (i, 0))
    whole = lambda shape: pl.BlockSpec(shape, lambda *_: (0,) * len(shape))

    n1, = _norm_fwd("norm_a", x, ga)
    zqkv = _matmul("proj_a_qkv", n1, wa_in, dims=NN, grid=(3, nt + 1), zero_axis=1,
                   a_spec=pl.BlockSpec((tm, d), lambda j, i: (jnp.maximum(i - 1, 0), 0)),
                   b_spec=pl.BlockSpec((None, d, d), lambda j, i: (j, 0, 0)),
                   o_spec=pl.BlockSpec((None, tm, d), lambda j, i: (j, i, 0)),
                   out_shape=(3, tm + s, d), out_dtype=BF16)
    gate_a = _matmul("proj_a_gate", n1, wa_in, dims=NN, grid=(nt,),
                     a_spec=row, b_spec=pl.BlockSpec((None, d, d), lambda i: (3, 0, 0)), o_spec=row,
                     out_shape=(s, d), out_dtype=F32)
    onehot_a = _a_offset_onehot()
    diag_a = _diag_rows(onehot_a, rel_bias)
    (o_a, u_a, lse_a), gathered = _attn_a_fwd(
        zqkv, gate_a, diag_a, hosted=_allgather_hosted(later_shards, [True] * len(later_shards)))
    wa_out, wkv, wb_in, wb_out = gathered
    wa_out = wa_out.reshape(d, d)
    wkv = wkv.reshape(d, -1)
    wb_out = wb_out.reshape(d, d)
    h1 = _matmul("out_a", u_a, wa_out, dims=NN, grid=(nt,), a_spec=row, b_spec=whole((d, d)), o_spec=row,
                 out_shape=(s, d), out_dtype=F32, resid=x, resid_spec=row)

    nk, n2 = _norm_fwd("norm_kv_b", h1, jnp.concatenate([gk, gb], axis=0))
    kvw = wkv.shape[1]
    wkv_x = jnp.concatenate([wkv[:, (i // 2) * HEAD_DIM:(i // 2 + 1) * HEAD_DIM] for i in range(8)], axis=1)
    kvx = _matmul("proj_kv", nk, wkv_x, dims=NN, grid=(nt + 1,), zero_axis=0,
                  a_spec=pl.BlockSpec((tm, d), lambda i: (jnp.maximum(i - 1, 0), 0)), b_spec=whole((d, B_KVX)),
                  o_spec=pl.BlockSpec((tm, B_KVX), lambda i: (i, 0)), out_shape=(tm + s, B_KVX), out_dtype=BF16)
    qb = _matmul("proj_b_q", n2, wb_in, dims=NN, grid=(2, nt),
                 a_spec=pl.BlockSpec((tm, d), lambda j, i: (i, 0)),
                 b_spec=pl.BlockSpec((None, d, half), lambda j, i: (j, 0, 0)),
                 o_spec=pl.BlockSpec((tm, half), lambda j, i: (i, j)), out_shape=(s, d), out_dtype=BF16)
    gate_b = _matmul("proj_b_gate", n2, wb_in, dims=NN, grid=(2, nt),
                     a_spec=pl.BlockSpec((tm, d), lambda j, i: (i, 0)),
                     b_spec=pl.BlockSpec((None, d, half), lambda j, i: (2 + j, 0, 0)),
                     o_spec=pl.BlockSpec((tm, half), lambda j, i: (i, j)), out_shape=(s, d), out_dtype=F32)
    onehot_b = _b_offset_onehot()
    base_b = jnp.roll(_diag_rows(onehot_b, t5)[..., ::-1], TQ, axis=-1)
    o_b, u_b, lse_b = _attn_b_fwd(qb, kvx, gate_b, base_b, sinks)
    h2 = _matmul("out_b", u_b, wb_out, dims=NN, grid=(nt,), a_spec=row, b_spec=whole((d, d)), o_spec=row,
                 out_shape=(s, d), out_dtype=F32, resid=h1, resid_spec=row)

    dh2, loss, d_gf = _loss_head(h2, target, gf)

    du_b = _matmul("dout_b", dh2, wb_out, dims=NT, grid=(nt,), a_spec=row, b_spec=whole((d, d)), o_spec=row,
                   out_shape=(s, d), out_dtype=F32)
    d_wb_out = _matmul("dw_out_b", u_b, dh2, dims=TN, grid=(2,),
                       a_spec=whole((s, d)), b_spec=pl.BlockSpec((s, half), lambda j: (0, j)),
                       o_spec=pl.BlockSpec((d, half), lambda j: (0, j)),
                       out_shape=(d, d), out_dtype=F32, also_bf16=True)
    dz_b, dkv, dsum_b, dsinks = _attn_b_bwd(qb, kvx, gate_b, o_b, du_b, lse_b, base_b, sinks)
    ddiag_b = jnp.roll(dsum_b[..., ::-1], -1, axis=-1)
    dn2 = _matmul("dproj_b", dz_b, wb_in, dims=NT, grid=(nt,), parts=4,
                  a_spec=pl.BlockSpec((4, tm, half), lambda i: (0, i, 0)), b_spec=whole((4, d, half)),
                  o_spec=row, out_shape=(s, d), out_dtype=F32)
    d_wb_in = _matmul("dw_in_b", n2, dz_b, dims=TN, grid=(4,),
                      a_spec=whole((s, d)), b_spec=pl.BlockSpec((None, s, half), lambda j: (j, 0, 0)),
                      o_spec=pl.BlockSpec((None, d, half), lambda j: (j, 0, 0)),
                      out_shape=(4, d, half), out_dtype=F32, also_bf16=True)
    dnk = _matmul("dproj_kv", dkv, wkv, dims=NT, grid=(nt,),
                  a_spec=pl.BlockSpec((tm, kvw), lambda i: (i, 0)), b_spec=whole((d, kvw)), o_spec=row,
                  out_shape=(s, d), out_dtype=F32)
    d_wkv = _matmul("dw_kv", nk, dkv, dims=TN, grid=(1,),
                    a_spec=whole((s, d)), b_spec=whole((s, kvw)), o_spec=whole((d, kvw)),
                    out_shape=(d, kvw), out_dtype=F32, also_bf16=True)
    dh1, d_gkb = _norm_bwd("dnorm_kv_b", h1, dh2, [dnk, dn2], jnp.concatenate([gk, gb], axis=0))

    du_a = _matmul("dout_a", dh1, wa_out, dims=NT, grid=(nt,), a_spec=row, b_spec=whole((d, d)), o_spec=row,
                   out_shape=(s, d), out_dtype=F32)
    d_wa_out = _matmul("dw_out_a", u_a, dh1, dims=TN, grid=(2,),
                       a_spec=whole((s, d)), b_spec=pl.BlockSpec((s, half), lambda j: (0, j)),
                       o_spec=pl.BlockSpec((d, half), lambda j: (0, j)),
                       out_shape=(d, d), out_dtype=F32, also_bf16=True)
    early = dict(a_w_out=[g.reshape(4, d // 4, d) for g in d_wa_out],
                 kv_w=[g.reshape(4, d // 4, kvw) for g in d_wkv], b_w_in=list(d_wb_in),
                 b_w_out=[g.reshape(4, d // 4, d) for g in d_wb_out])
    (dz_a, ddiag_a), early_recv = _attn_a_bwd(
        zqkv, gate_a, o_a, du_a, lse_a, diag_a, hosted=_scatter_hosted([early[n][1] for n in early]))
    d_wa_in = _matmul("dw_in_a", n1, dz_a, dims=TN, grid=(4, 2),
                      a_spec=whole((s, d)), b_spec=pl.BlockSpec((None, s, half), lambda j, h: (j, 0, h)),
                      o_spec=pl.BlockSpec((None, d, half), lambda j, h: (j, 0, h)),
                      out_shape=(4, d, d), out_dtype=F32, also_bf16=True)
    late_recv = [_scatter_on_sequencer("scatter_a_w_in", d_wa_in[1])]
    tp = min(TM_PARTS, s)
    dn1 = _matmul("dproj_a", dz_a, wa_in, dims=NT, grid=(s // tp,), parts=4,
                  a_spec=pl.BlockSpec((4, tp, d), lambda i: (0, i, 0)), b_spec=whole((4, d, d)),
                  o_spec=pl.BlockSpec((tp, d), lambda i: (i, 0)), out_shape=(s, d), out_dtype=F32)
    grad_x, d_ga = _norm_bwd("dnorm_a", x, dh1, [dn1], ga)

    small = dict(
        a_norm=d_ga, a_rel_bias=_diag_rows_grad(onehot_a, ddiag_a), kv_norm=d_gkb[0:1],
        t5_bias=_diag_rows_grad(onehot_b, ddiag_b), b_norm=d_gkb[1:2], b_sinks=dsinks[0:1, :HEADS],
        final_norm=d_gf)
    own = dict(a_w_in=d_wa_in[0], **{n: early[n][0] for n in early})
    received = dict(a_w_in=late_recv[0], **dict(zip(early, early_recv)))
    return loss, grad_x, small, own, received


SMALL = ("a_norm", "a_rel_bias", "kv_norm", "t5_bias", "b_norm", "b_sinks", "final_norm")
BIG = ("a_w_in", "a_w_out", "kv_w", "b_w_in", "b_w_out")
ORDER = ("a_norm", "a_w_in", "a_rel_bias", "a_w_out", "kv_norm", "kv_w", "t5_bias", "b_norm", "b_w_in",
         "b_sinks", "b_w_out", "final_norm")


def _pack(parts, rows):
    flat = jnp.concatenate([p.reshape(-1).astype(F32) for p in parts])
    return jnp.pad(flat, (0, rows * LANES - flat.shape[0])).reshape(rows, LANES)


def _unpack(block, shapes):
    flat = block.reshape(-1)
    out, at = [], 0
    for shp in shapes:
        size = int(np.prod(shp))
        out.append(flat[at:at + size].reshape(shp))
        at += size
    return out


def kernel(x, a_norm, a_w_in, a_rel_bias, a_w_out, kv_norm, kv_w, t5_bias, b_norm, b_w_in, b_sinks, b_w_out, final_norm, loss_target, m_a_norm, m_a_w_in, m_a_rel_bias, m_a_w_out, m_kv_norm, m_kv_w, m_t5_bias, m_b_norm, m_b_w_in, m_b_sinks, m_b_w_out, m_final_norm, v_a_norm, v_a_w_in, v_a_rel_bias, v_a_w_out, v_kv_norm, v_kv_w, v_t5_bias, v_b_norm, v_b_w_in, v_b_sinks, v_b_w_out, v_final_norm):
    w = dict(a_norm=a_norm, a_w_in=a_w_in, a_rel_bias=a_rel_bias, a_w_out=a_w_out, kv_norm=kv_norm, kv_w=kv_w,
             t5_bias=t5_bias, b_norm=b_norm, b_w_in=b_w_in, b_sinks=b_sinks, b_w_out=b_w_out,
             final_norm=final_norm)
    m = dict(a_norm=m_a_norm, a_w_in=m_a_w_in, a_rel_bias=m_a_rel_bias, a_w_out=m_a_w_out, kv_norm=m_kv_norm,
             kv_w=m_kv_w, t5_bias=m_t5_bias, b_norm=m_b_norm, b_w_in=m_b_w_in, b_sinks=m_b_sinks,
             b_w_out=m_b_w_out, final_norm=m_final_norm)
    v = dict(a_norm=v_a_norm, a_w_in=v_a_w_in, a_rel_bias=v_a_rel_bias, a_w_out=v_a_w_out, kv_norm=v_kv_norm,
             kv_w=v_kv_w, t5_bias=v_t5_bias, b_norm=v_b_norm, b_w_in=v_b_w_in, b_sinks=v_b_sinks,
             b_w_out=v_b_w_out, final_norm=v_final_norm)
    d = D_MODEL
    chip = 2 * lax.axis_index("x") + lax.axis_index("y")

    shard2d = dict(a_w_in=a_w_in[0], a_w_out=a_w_out[0], kv_w=kv_w, b_w_in=b_w_in[0], b_w_out=b_w_out[0])

    wa_in, = _run_on_sequencer("allgather_first", _allgather_hosted([shard2d["a_w_in"].astype(BF16)], [True]))
    ga, = _run_alone("allgather_norm", _allgather_hosted([a_norm], [False]))
    ga = ga.reshape(1, d)

    loss, grad_x, small, own, received = _local_step(
        x[0], loss_target[0], ga, wa_in, a_rel_bias[0], [shard2d[n].astype(BF16) for n in BIG[1:]],
        kv_norm.reshape(1, d), t5_bias, b_norm, b_sinks, final_norm.reshape(1, d))

    small_shapes = [small[n].shape for n in SMALL] + [(1, 1)]
    total = sum(int(np.prod(s)) for s in small_shapes)
    rows = -(-total // (8 * LANES)) * 8
    reduced = _unpack(_allreduce_small(_pack([small[n] for n in SMALL] + [loss], rows)), small_shapes)
    g_small = dict(zip(SMALL, reduced[:-1]))
    loss_out = reduced[-1].reshape(())
    g_small["a_norm"] = lax.dynamic_slice_in_dim(g_small["a_norm"], chip * (d // 4), d // 4, axis=1)

    core_sums = [
        _sum_partials("sum_" + n, lax.dynamic_index_in_dim(own[n], chip, 0, keepdims=False), received[n])
        for n in BIG]
    sibling_sums = _swap_with_sibling(core_sums)

    out = {}
    for n, mine, theirs in zip(BIG, core_sums, sibling_sums):
        res = _adamw("adamw_" + n, shard2d[n], m[n].reshape(shard2d[n].shape), v[n].reshape(shard2d[n].shape),
                     [mine, theirs])
        out[n] = [r.reshape(w[n].shape) for r in res]
    small_w_shapes = [w[n].shape for n in SMALL]
    total_w = sum(int(np.prod(s)) for s in small_w_shapes)
    rows_w = -(-total_w // (8 * LANES)) * 8
    packed = [_pack([t[n] for n in SMALL], rows_w) for t in (w, m, v)]
    g_packed = _pack([g_small[n] for n in SMALL], rows_w)
    res = _adamw("adamw_small", packed[0], packed[1], packed[2], [g_packed])
    unpacked = [_unpack(r, small_w_shapes) for r in res]
    for i, n in enumerate(SMALL):
        out[n] = [unpacked[k][i] for k in range(4)]

    grads = [out[n][0] for n in ORDER]
    deltas = [out[n][1] for n in ORDER]
    new_m = [out[n][2] for n in ORDER]
    new_v = [out[n][3] for n in ORDER]
    return (loss_out, grad_x[None], *grads, *deltas, *new_m, *new_v)
```

```python
import functools
import math

import jax
import jax.numpy as jnp
import numpy as np
from jax import lax
from jax.experimental import pallas as pl
from jax.experimental.pallas import tpu as pltpu
from jax.experimental.pallas import tpu_sc as plsc

F32 = jnp.float32
BF16 = jnp.bfloat16
MESH = pl.DeviceIdType.MESH

D_MODEL = 1024
HEADS = 16
HEAD_DIM = 64
CHUNK = 64
RMS_EPS = 1e-6
SCALE = HEAD_DIM ** -0.5
A_LEFT_CHUNKS = 8
A_REL_CLIP = 256
B_LEFT_CHUNKS = 2
B_KV_HEADS = 2
B_GROUP = HEADS // B_KV_HEADS
T5_BUCKETS = 32
T5_MAX_DIST = 128
ADAM_LR = 0.001
ADAM_B1 = 0.9
ADAM_B2 = 0.999
ADAM_EPS = 1e-08
ADAM_WD = 0.01
ADAM_STEP = 10

MASKED = -1e30
LANES = 128
TQ = 128
A_PAIRS = 2
A_PAIRS_FWD = 4
KB = 128
A_KBLOCKS = A_LEFT_CHUNKS * CHUNK // KB + 1
B_KBLOCKS = B_LEFT_CHUNKS * CHUNK // KB + 1
A_WIN = A_KBLOCKS * KB
B_WIN = B_KBLOCKS * KB
TM = 512
TM_DENSE = 1024
TM_PARTS = 512
VMEM_LIMIT = 56 * 1024 * 1024

NT = (((1,), (1,)), ((), ()))
TN = (((0,), (0,)), ((), ()))
NN = (((1,), (0,)), ((), ()))


def _params(sem=None):
    return pltpu.CompilerParams(dimension_semantics=sem, vmem_limit_bytes=VMEM_LIMIT)


class _Hosted:
    def __init__(self, inputs, out_shapes, sems, first, middle, last):
        self.inputs, self.out_shapes, self.sems = list(inputs), list(out_shapes), list(sems)
        self.first, self.middle, self.last = first, middle, last


def _call(body, *, name, grid, in_specs, out_specs, out_shape, args, scratch_shapes=(), sem=None, hosted=None):
    in_specs, out_specs, out_shape = list(in_specs), list(out_specs), list(out_shape)
    scratch_shapes = list(scratch_shapes)
    if hosted is None:
        out = pl.pallas_call(
            body, name=name, grid=grid, in_specs=in_specs, out_specs=out_specs, out_shape=out_shape,
            scratch_shapes=scratch_shapes, compiler_params=_params(sem))(*args)
        return list(out), []
    n_in, n_out, n_scr = len(in_specs), len(out_shape), len(scratch_shapes)
    h_in, h_out = len(hosted.inputs), len(hosted.out_shapes)
    total = int(np.prod(grid)) if grid else 1

    def wrapped(*refs):
        ins, refs = refs[:n_in], refs[n_in:]
        h_ins, refs = refs[:h_in], refs[h_in:]
        outs, refs = refs[:n_out], refs[n_out:]
        h_outs, refs = refs[:h_out], refs[h_out:]
        scr, h_sems = refs[:n_scr], refs[n_scr:]
        step = 0
        for axis, size in enumerate(grid):
            step = step * size + pl.program_id(axis)

        @pl.when(step == 0)
        def _():
            hosted.first(h_ins, h_outs, h_sems)

        body(*ins, *outs, *scr)
        if hosted.middle is not None:
            @pl.when(step == total // 2)
            def _():
                hosted.middle(h_ins, h_outs, h_sems)

        @pl.when(step == total - 1)
        def _():
            hosted.last(h_ins, h_outs, h_sems)

    out = pl.pallas_call(
        wrapped, name=name, grid=grid, in_specs=in_specs + [ANY] * h_in, out_specs=out_specs + [ANY] * h_out,
        out_shape=out_shape + hosted.out_shapes, scratch_shapes=scratch_shapes + hosted.sems,
        compiler_params=_params(("arbitrary",) * len(grid)))(*args, *hosted.inputs)
    return list(out[:n_out]), list(out[n_out:])


def _matmul(name, a, b, *, dims, grid, a_spec, b_spec, o_spec, out_shape, out_dtype,
            parts=1, resid=None, resid_spec=None, also_bf16=False, hosted=None, zero_axis=None):
    def body(*refs):
        if zero_axis is None:
            product(*refs)
        else:
            @pl.when(pl.program_id(zero_axis) == 0)
            def _():
                refs[2][...] = jnp.zeros_like(refs[2])

            @pl.when(pl.program_id(zero_axis) > 0)
            def _():
                product(*refs)

    def product(*refs):
        a_ref, b_ref = refs[:2]
        r_ref = refs[2] if resid is not None else None
        o_ref = refs[3] if resid is not None else refs[2]
        if parts == 1:
            prod = lax.dot_general(a_ref[...].astype(BF16), b_ref[...].astype(BF16), dims,
                                   preferred_element_type=F32)
        else:
            prod = None
            for part in range(parts):
                term = lax.dot_general(a_ref[part].astype(BF16), b_ref[part].astype(BF16), dims,
                                       preferred_element_type=F32)
                prod = term if prod is None else prod + term
        if resid is not None:
            prod = r_ref[...] + prod
        o_ref[...] = prod.astype(out_dtype)
        if also_bf16:
            refs[-1][...] = prod.astype(BF16)

    in_specs = [a_spec, b_spec]
    args = [a, b]
    if resid is not None:
        in_specs.append(resid_spec)
        args.append(resid)
    sem = ["parallel"] * len(grid)
    out_specs = [o_spec]
    out_shapes = [jax.ShapeDtypeStruct(out_shape, out_dtype)]
    if also_bf16:
        out_specs.append(o_spec)
        out_shapes.append(jax.ShapeDtypeStruct(out_shape, BF16))
    out, extra = _call(body, name=name, grid=grid, in_specs=in_specs, out_specs=out_specs, out_shape=out_shapes,
                       args=args, sem=tuple(sem), hosted=hosted)
    res = out[0] if not also_bf16 else tuple(out)
    return res if hosted is None else (res, extra)


def _rms_rows(x):
    return lax.rsqrt(jnp.mean(x * x, axis=-1, keepdims=True) + RMS_EPS)


def _norm_fwd(name, x, gains):
    s, d = x.shape
    n = gains.shape[0]

    def body(x_ref, g_ref, *o_refs):
        xv = x_ref[...]
        xh = xv * _rms_rows(xv)
        for i in range(n):
            o_refs[i][...] = (xh * g_ref[i:i + 1, :]).astype(BF16)

    row = pl.BlockSpec((TM, d), lambda i: (i, 0))
    return pl.pallas_call(
        body, name=name, grid=(s // TM,),
        in_specs=[row, pl.BlockSpec((n, d), lambda i: (0, 0))],
        out_specs=[row] * n,
        out_shape=[jax.ShapeDtypeStruct((s, d), BF16)] * n,
        compiler_params=_params(("parallel",)),
    )(x, gains)


def _norm_bwd(name, x, dres, dns, gains):
    s, d = x.shape
    n = len(dns)

    def body(x_ref, r_ref, g_ref, *refs):
        dn_refs, dx_ref, dg_ref = refs[:n], refs[n], refs[n + 1]
        i = pl.program_id(0)
        xv = x_ref[...]
        r = _rms_rows(xv)
        xh = xv * r

        @pl.when(i == 0)
        def _():
            dg_ref[...] = jnp.zeros_like(dg_ref)

        a = None
        for j in range(n):
            dn = dn_refs[j][...]
            t = dn * g_ref[j:j + 1, :]
            a = t if a is None else a + t
            dg_ref[j:j + 1, :] += jnp.sum(dn * xh, axis=0, keepdims=True)
        dx_ref[...] = r_ref[...] + r * (a - xh * jnp.mean(xh * a, axis=-1, keepdims=True))

    row = pl.BlockSpec((TM, d), lambda i: (i, 0))
    small = pl.BlockSpec((n, d), lambda i: (0, 0))
    return pl.pallas_call(
        body, name=name, grid=(s // TM,),
        in_specs=[row, row, small] + [row] * n,
        out_specs=[row, small],
        out_shape=[jax.ShapeDtypeStruct((s, d), F32), jax.ShapeDtypeStruct((n, d), F32)],
        compiler_params=_params(("arbitrary",)),
    )(x, dres, gains, *dns)


def _loss_head(h2, target, gain):
    s, d = h2.shape

    def body(h_ref, t_ref, g_ref, dh_ref, loss_ref, dg_ref):
        i = pl.program_id(0)
        hv = h_ref[...]
        r = _rms_rows(hv)
        hh = hv * r
        g = g_ref[...]
        err = hh * g - t_ref[...]
        part = 0.5 * jnp.sum(jnp.sum(err * err, axis=-1, keepdims=True) * (1.0 / d), axis=0, keepdims=True)
        dy = err * (1.0 / d)
        a = dy * g
        dh_ref[...] = r * (a - hh * jnp.mean(hh * a, axis=-1, keepdims=True))
        dg = jnp.sum(dy * hh, axis=0, keepdims=True)

        @pl.when(i == 0)
        def _():
            loss_ref[...] = part
            dg_ref[...] = dg

        @pl.when(i > 0)
        def _():
            loss_ref[...] += part
            dg_ref[...] += dg

    row = pl.BlockSpec((TM, d), lambda i: (i, 0))
    return pl.pallas_call(
        body, name="loss_head", grid=(s // TM,),
        in_specs=[row, row, pl.BlockSpec((1, d), lambda i: (0, 0))],
        out_specs=[row, pl.BlockSpec((1, 1), lambda i: (0, 0)), pl.BlockSpec((1, d), lambda i: (0, 0))],
        out_shape=[jax.ShapeDtypeStruct((s, d), F32), jax.ShapeDtypeStruct((1, 1), F32),
                   jax.ShapeDtypeStruct((1, d), F32)],
        compiler_params=_params(("arbitrary",)),
    )(h2, target, gain)


def _silu_parts(g):
    sig = jax.nn.sigmoid(g)
    return g * sig, sig * (1.0 + g * (1.0 - sig))


def _lane_lo(rows):
    return lax.broadcasted_iota(jnp.int32, (rows, LANES), 1) < HEAD_DIM


def _stack_pair(x):
    lo = _lane_lo(x.shape[0])
    zero = jnp.zeros_like(x)
    return jnp.concatenate([jnp.where(lo, x, zero), jnp.where(lo, zero, x)], axis=0)


def _unstack_pair(y, w):
    return jnp.where(_lane_lo(w), y[:w], y[w:])


def _block_valid(b, left_blocks, width):
    col = lax.broadcasted_iota(jnp.int32, (1, 2 * width), 1)
    col = jnp.where(col >= width, col - width, col)
    return (col // KB + (b - left_blocks)) >= 0


def _toeplitz_tile(diag_row, width, left_chunks):
    wide = width + TQ
    rolled = pltpu.roll(jnp.broadcast_to(diag_row, (TQ, wide)), 1, 1, stride=1, stride_axis=0)
    i = lax.broadcasted_iota(jnp.int32, (TQ, width), 0) // CHUNK
    j = lax.broadcasted_iota(jnp.int32, (TQ, width), 1) // CHUNK
    dc = i + left_chunks - j
    return jnp.where((dc >= 0) & (dc <= left_chunks), rolled[:, TQ:], MASKED)


def _toeplitz_sum(tile, width):
    flip = (lax.broadcasted_iota(jnp.int32, (TQ, TQ), 0) + lax.broadcasted_iota(jnp.int32, (TQ, TQ), 1)
            == TQ - 1).astype(F32)
    reversed_rows = jnp.dot(flip, tile, precision=lax.Precision.HIGHEST, preferred_element_type=F32)
    padded = jnp.concatenate([reversed_rows, jnp.zeros((TQ, TQ), F32)], axis=1)
    rolled = pltpu.roll(padded, 0, 1, stride=1, stride_axis=0)
    return jnp.sum(rolled, axis=0, keepdims=True)


def _softmax_pair(sc, w, sink=None):
    ps, inv, lses = [], [], []
    for e in range(2):
        sh = sc[:, e * w:(e + 1) * w]
        m = jnp.max(sh, axis=-1, keepdims=True)
        if sink is not None:
            m = jnp.maximum(m, sink[e])
        ex = jnp.exp(sh - m)
        l = jnp.sum(ex, axis=-1, keepdims=True)
        if sink is not None:
            l = l + jnp.exp(sink[e] - m)
        ps.append(ex.astype(BF16))
        inv.append(1.0 / l)
        lses.append(m + jnp.log(l))
    return jnp.concatenate(ps, axis=-1), inv, lses


def _softmax_pair_bwd(sc, dp, lse, delta, w):
    ps, dss = [], []
    for e in range(2):
        p = jnp.exp(sc[:, e * w:(e + 1) * w] - lse[e])
        ps.append(p)
        dss.append(p * (dp[:, e * w:(e + 1) * w] - delta[e]))
    return jnp.concatenate(ps, axis=-1), jnp.concatenate(dss, axis=-1)


def _pair_rowsums(x, lo):
    zero = jnp.zeros_like(x)
    return (jnp.sum(jnp.where(lo, x, zero), axis=-1, keepdims=True),
            jnp.sum(jnp.where(lo, zero, x), axis=-1, keepdims=True))


def _a_qkv_specs(rows, pad, pw):
    return [pl.BlockSpec((None, TQ, pw), lambda p, b: (0, b + pad // TQ, p)),
            pl.BlockSpec((None, rows, pw), lambda p, b: (1, 0, p)),
            pl.BlockSpec((None, rows, pw), lambda p, b: (2, 0, p))]


def _window(ref, b, pad, win, lanes):
    start = pl.multiple_of(b * TQ + pad - (win - TQ), KB)
    return ref[pl.ds(start, win), lanes]


def _attn_a_fwd(zqkv, g, diag, hosted=None):
    s = g.shape[0]
    pad = zqkv.shape[1] - s
    nb = s // TQ
    left = A_KBLOCKS - 1
    pairs = A_PAIRS_FWD
    pw = pairs * LANES
    wide = A_WIN + TQ

    def body(q_ref, k_ref, v_ref, g_ref, diag_ref, o_ref, u_ref, lse_ref, bias_scr):
        b = pl.program_id(1)

        @pl.when(b == 0)
        def _():
            for hh in range(2 * pairs):
                bias_scr[hh // 2, :, (hh % 2) * A_WIN:(hh % 2 + 1) * A_WIN] = _toeplitz_tile(
                    diag_ref[hh], A_WIN, A_LEFT_CHUNKS)

        def step(first_blocks):
            lo = _lane_lo(TQ)
            for pp in range(pairs):
                ln = slice(pp * LANES, (pp + 1) * LANES)
                kcat = _stack_pair(_window(k_ref, b, pad, A_WIN, ln))
                vcat = _stack_pair(_window(v_ref, b, pad, A_WIN, ln))
                sc = lax.dot_general(q_ref[:, ln] * SCALE, kcat, NT, preferred_element_type=F32) + bias_scr[pp]
                if first_blocks:
                    sc = jnp.where(_block_valid(b, left, A_WIN), sc, MASKED)
                p, inv, lses = _softmax_pair(sc, A_WIN)
                ov = jnp.dot(p, vcat, preferred_element_type=F32) * jnp.where(lo, inv[0], inv[1])
                o_ref[:, ln] = ov
                lse_ref[pp] = jnp.where(lo, lses[0], lses[1])
                sg, _ = _silu_parts(g_ref[:, ln])
                u_ref[:, ln] = (ov * sg).astype(BF16)

        @pl.when(b < left)
        def _():
            step(True)

        @pl.when(b >= left)
        def _():
            step(False)

    tile = pl.BlockSpec((TQ, pw), lambda p, b: (b, p))
    return _call(
        body, name="attn_a_fwd", grid=(HEADS // 2 // pairs, nb),
        in_specs=_a_qkv_specs(pad + s, pad, pw) + [
            tile, pl.BlockSpec((2 * pairs, 1, wide), lambda p, b: (p, 0, 0))],
        out_specs=[tile, tile, pl.BlockSpec((pairs, TQ, LANES), lambda p, b: (p, b, 0))],
        out_shape=[jax.ShapeDtypeStruct((s, D_MODEL), F32), jax.ShapeDtypeStruct((s, D_MODEL), BF16),
                   jax.ShapeDtypeStruct((HEADS // 2, s, LANES), F32)],
        scratch_shapes=[pltpu.VMEM((pairs, TQ, 2 * A_WIN), F32)],
        sem=("parallel", "arbitrary"), hosted=hosted,
        args=(zqkv, zqkv, zqkv, g, diag))


def _attn_a_bwd(zqkv, g, o, du, lse, diag, hosted=None):
    s = g.shape[0]
    pad = zqkv.shape[1] - s
    nb = s // TQ
    left = A_KBLOCKS - 1
    pw = A_PAIRS * LANES
    wide = A_WIN + TQ

    def body(q_ref, k_ref, v_ref, g_ref, o_ref, du_ref, lse_ref, diag_ref, dz_ref, ddiag_ref,
             bias_scr, dbias_acc, dk_acc, dv_acc):
        b = pl.program_id(1)

        @pl.when(b == 0)
        def _():
            for hh in range(2 * A_PAIRS):
                bias_scr[hh // 2, :, (hh % 2) * A_WIN:(hh % 2 + 1) * A_WIN] = _toeplitz_tile(
                    diag_ref[hh], A_WIN, A_LEFT_CHUNKS)
            dbias_acc[...] = jnp.zeros_like(dbias_acc)
            dk_acc[...] = jnp.zeros_like(dk_acc)
            dv_acc[...] = jnp.zeros_like(dv_acc)

        def step(first_blocks):
            lo = _lane_lo(TQ)
            upper = lax.broadcasted_iota(jnp.int32, (LANES, A_WIN), 0) < HEAD_DIM
            rows = pl.ds(pl.multiple_of(b * TQ, TQ), TQ)
            sg, dsg = _silu_parts(g_ref[...])
            duv = du_ref[...]
            ov = o_ref[...]
            do = duv * sg
            dz_ref[3, rows, :] = (duv * ov * dsg).astype(BF16)
            do_o = do * ov
            do_bf = do.astype(BF16)
            for pp in range(A_PAIRS):
                ln = slice(pp * LANES, (pp + 1) * LANES)
                q = q_ref[:, ln] * SCALE
                kcat = _stack_pair(_window(k_ref, b, pad, A_WIN, ln))
                vcat = _stack_pair(_window(v_ref, b, pad, A_WIN, ln))
                sc = lax.dot_general(q, kcat, NT, preferred_element_type=F32) + bias_scr[pp]
                if first_blocks:
                    sc = jnp.where(_block_valid(b, left, A_WIN), sc, MASKED)
                lse_t = lse_ref[pp]
                dp = lax.dot_general(do_bf[:, ln], vcat, NT, preferred_element_type=F32)
                p, ds = _softmax_pair_bwd(sc, dp, (lse_t[:, 0:1], lse_t[:, HEAD_DIM:HEAD_DIM + 1]),
                                          _pair_rowsums(do_o[:, ln], lo), A_WIN)
                dbias_acc[pp] += ds
                dsb = ds.astype(BF16)
                dz_ref[0, rows, ln] = (jnp.dot(dsb, kcat, preferred_element_type=F32) * SCALE).astype(BF16)
                dkt = lax.dot_general(q, dsb, TN, preferred_element_type=F32)
                dvt = lax.dot_general(do_bf[:, ln], p.astype(BF16), TN, preferred_element_type=F32)
                dkt = jnp.where(upper, dkt[:, :A_WIN], dkt[:, A_WIN:])
                dvt = jnp.where(upper, dvt[:, :A_WIN], dvt[:, A_WIN:])
                for t in range(A_KBLOCKS):
                    blk = b + (pad // KB - left + t)
                    dk_acc[blk, ln, :] += dkt[:, t * KB:(t + 1) * KB]
                    dv_acc[blk, ln, :] += dvt[:, t * KB:(t + 1) * KB]

        @pl.when(b < left)
        def _():
            step(True)

        @pl.when(b >= left)
        def _():
            step(False)

        @pl.when(b == nb - 1)
        def _():
            for kb in range(s // KB):
                dz_ref[1, kb * KB:(kb + 1) * KB, :] = dk_acc[pad // KB + kb].T.astype(BF16)
                dz_ref[2, kb * KB:(kb + 1) * KB, :] = dv_acc[pad // KB + kb].T.astype(BF16)
            for hh in range(2 * A_PAIRS):
                ddiag_ref[hh] = _toeplitz_sum(
                    dbias_acc[hh // 2, :, (hh % 2) * A_WIN:(hh % 2 + 1) * A_WIN], A_WIN)

    tile = pl.BlockSpec((TQ, pw), lambda p, b: (b, p))
    diag_spec = pl.BlockSpec((2 * A_PAIRS, 1, wide), lambda p, b: (p, 0, 0))
    return _call(
        body, name="attn_a_bwd", grid=(HEADS // 2 // A_PAIRS, nb),
        in_specs=_a_qkv_specs(pad + s, pad, pw) + [
            tile, tile, tile, pl.BlockSpec((A_PAIRS, TQ, LANES), lambda p, b: (p, b, 0)), diag_spec],
        out_specs=[pl.BlockSpec((4, s, pw), lambda p, b: (0, 0, p)), diag_spec],
        out_shape=[jax.ShapeDtypeStruct((4, s, D_MODEL), BF16),
                   jax.ShapeDtypeStruct((HEADS, 1, wide), F32)],
        scratch_shapes=[pltpu.VMEM((A_PAIRS, TQ, 2 * A_WIN), F32), pltpu.VMEM((A_PAIRS, TQ, 2 * A_WIN), F32),
                        pltpu.VMEM(((pad + s) // KB, pw, KB), F32), pltpu.VMEM(((pad + s) // KB, pw, KB), F32)],
        sem=("parallel", "arbitrary"), hosted=hosted,
        args=(zqkv, zqkv, zqkv, g, o, du, lse, diag))


B_STACK = B_GROUP // 2
B_KVX = 4 * LANES
B_ROWS = B_STACK * TQ
B_WIDE = B_WIN + TQ


def _b_head_place(h):
    return h // B_GROUP, (h % B_GROUP) // 2, h % 2


def _toeplitz_tile_t(base_row, width, left_chunks):
    wide = width + TQ
    rolled = pltpu.roll(jnp.broadcast_to(base_row, (width, wide)), 0, 1, stride=1, stride_axis=0)
    j = lax.broadcasted_iota(jnp.int32, (width, TQ), 0) // CHUNK
    i = lax.broadcasted_iota(jnp.int32, (width, TQ), 1) // CHUNK
    dc = i + left_chunks - j
    return jnp.where((dc >= 0) & (dc <= left_chunks), rolled[:, :TQ], MASKED)


def _toeplitz_sum_t(tile_t, width):
    flip = (lax.broadcasted_iota(jnp.int32, (width, width), 0) + lax.broadcasted_iota(jnp.int32, (width, width), 1)
            == width - 1).astype(F32)
    reversed_rows = jnp.dot(flip, tile_t, precision=lax.Precision.HIGHEST, preferred_element_type=F32)
    padded = jnp.concatenate([reversed_rows, jnp.zeros((width, width), F32)], axis=1)
    rolled = pltpu.roll(padded, 0, 1, stride=1, stride_axis=0)
    return jnp.sum(rolled, axis=0, keepdims=True)


def _b_build_bias(base_ref, bias_scr):
    for h in range(HEADS):
        gi, pr, e = _b_head_place(h)
        bias_scr[gi, e * B_WIN:(e + 1) * B_WIN, pr * TQ:(pr + 1) * TQ] = _toeplitz_tile_t(
            base_ref[h], B_WIN, B_LEFT_CHUNKS)


def _b_stack(x, gi):
    return jnp.concatenate(
        [x[:, (B_STACK * gi + pr) * LANES:(B_STACK * gi + pr + 1) * LANES] for pr in range(B_STACK)], axis=0)


def _b_sink_rows(sink_ref, gi):
    block = lax.broadcasted_iota(jnp.int32, (1, B_ROWS), 1) // TQ
    rows = []
    for e in range(2):
        row = jnp.zeros((1, B_ROWS), F32)
        for pr in range(B_STACK):
            h = B_GROUP * gi + 2 * pr + e
            row = jnp.where(block == pr, sink_ref[0:1, h:h + 1], row)
        rows.append(row)
    return rows


def _b_scores_t(q_ref, kvv, bias_scr, gi, b, left, first_blocks):
    kcat = _stack_pair(kvv[:, gi * LANES:(gi + 1) * LANES])
    vcat = _stack_pair(kvv[:, (B_KV_HEADS + gi) * LANES:(B_KV_HEADS + gi + 1) * LANES])
    qs = _b_stack(q_ref, gi) * SCALE
    sc = lax.dot_general(kcat, qs, NT, preferred_element_type=F32) + bias_scr[gi]
    if first_blocks:
        row = lax.broadcasted_iota(jnp.int32, (2 * B_WIN, 1), 0)
        row = jnp.where(row >= B_WIN, row - B_WIN, row)
        sc = jnp.where((row // KB + (b - left)) >= 0, sc, MASKED)
    return kcat, vcat, qs, sc


def _attn_b_fwd(qb, kvx, gate, base, sinks):
    s = qb.shape[0]
    pad = kvx.shape[0] - s
    nb = s // TQ
    left = B_KBLOCKS - 1

    def body(q_ref, kv_ref, g_ref, base_ref, sink_ref, o_ref, u_ref, lse_ref, bias_scr):
        b = pl.program_id(0)

        @pl.when(b == 0)
        def _():
            _b_build_bias(base_ref, bias_scr)

        def step(first_blocks):
            kvv = _window(kv_ref, b, pad, B_WIN, slice(None))
            upper = lax.broadcasted_iota(jnp.int32, (LANES, B_ROWS), 0) < HEAD_DIM
            lse_rows = []
            for gi in range(B_KV_HEADS):
                kcat, vcat, qs, sc = _b_scores_t(q_ref, kvv, bias_scr, gi, b, left, first_blocks)
                sink = _b_sink_rows(sink_ref, gi)
                ps, inv = [], []
                for e in range(2):
                    sh = sc[e * B_WIN:(e + 1) * B_WIN]
                    m = jnp.maximum(jnp.max(sh, axis=0, keepdims=True), sink[e])
                    ex = jnp.exp(sh - m)
                    l = jnp.sum(ex, axis=0, keepdims=True) + jnp.exp(sink[e] - m)
                    ps.append(ex.astype(BF16))
                    inv.append(1.0 / l)
                    lse_rows.append(m + jnp.log(l))
                pt = jnp.concatenate(ps, axis=0)
                ot = lax.dot_general(vcat, pt, TN, preferred_element_type=F32) * jnp.where(upper, inv[0], inv[1])
                ov = ot.T
                for pr in range(B_STACK):
                    pair = B_STACK * gi + pr
                    o_ref[:, pair * LANES:(pair + 1) * LANES] = ov[pr * TQ:(pr + 1) * TQ]
            lse_ref[0] = jnp.concatenate(lse_rows + [jnp.zeros((8 - len(lse_rows), B_ROWS), F32)], axis=0)
            sg, _ = _silu_parts(g_ref[...])
            u_ref[...] = (o_ref[...] * sg).astype(BF16)

        @pl.when(b < left)
        def _():
            step(True)

        @pl.when(b >= left)
        def _():
            step(False)

    row = pl.BlockSpec((TQ, D_MODEL), lambda b: (b, 0))
    return pl.pallas_call(
        body, name="attn_b_fwd", grid=(nb,),
        in_specs=[row, pl.BlockSpec((pad + s, B_KVX), lambda b: (0, 0)), row,
                  pl.BlockSpec((HEADS, 1, B_WIDE), lambda b: (0, 0, 0)), pl.BlockSpec((1, HEADS), lambda b: (0, 0))],
        out_specs=[row, row, pl.BlockSpec((1, 8, B_ROWS), lambda b: (b, 0, 0))],
        out_shape=[jax.ShapeDtypeStruct((s, D_MODEL), F32), jax.ShapeDtypeStruct((s, D_MODEL), BF16),
                   jax.ShapeDtypeStruct((nb, 8, B_ROWS), F32)],
        scratch_shapes=[pltpu.VMEM((B_KV_HEADS, 2 * B_WIN, B_ROWS), F32)],
        compiler_params=_params(("arbitrary",)),
    )(qb, kvx, gate, base, sinks)


def _attn_b_bwd(qb, kvx, gate, o, du, lse, base, sinks):
    s = qb.shape[0]
    pad = kvx.shape[0] - s
    nb = s // TQ
    left = B_KBLOCKS - 1
    half = D_MODEL // 2

    def body(q_ref, kv_ref, g_ref, o_ref, du_ref, lse_ref, base_ref, sink_ref, dz_ref, dkv_ref, dsum_ref,
             dsink_ref, bias_scr, dbias_acc, dkv_acc, dsink_acc):
        b = pl.program_id(0)

        @pl.when(b == 0)
        def _():
            _b_build_bias(base_ref, bias_scr)
            dbias_acc[...] = jnp.zeros_like(dbias_acc)
            dkv_acc[...] = jnp.zeros_like(dkv_acc)
            dsink_acc[...] = jnp.zeros_like(dsink_acc)

        def step(first_blocks):
            kvv = _window(kv_ref, b, pad, B_WIN, slice(None))
            sg, dsg = _silu_parts(g_ref[...])
            duv = du_ref[...]
            ov = o_ref[...]
            do = duv * sg
            dgate = (duv * ov * dsg).astype(BF16)
            dz_ref[2] = dgate[:, :half]
            dz_ref[3] = dgate[:, half:]
            do_o = do * ov
            do_bf = do.astype(BF16)
            lse_all = lse_ref[0]
            dsink_rows = []
            for gi in range(B_KV_HEADS):
                kcat, vcat, qs, sc = _b_scores_t(q_ref, kvv, bias_scr, gi, b, left, first_blocks)
                dos = _b_stack(do_bf, gi)
                doo_t = _b_stack(do_o, gi).T
                delta = (jnp.sum(doo_t[:HEAD_DIM], axis=0, keepdims=True),
                         jnp.sum(doo_t[HEAD_DIM:], axis=0, keepdims=True))
                sink = _b_sink_rows(sink_ref, gi)
                dp = lax.dot_general(vcat, dos, NT, preferred_element_type=F32)
                ps, dss = [], []
                for e in range(2):
                    lse_e = lse_all[2 * gi + e:2 * gi + e + 1]
                    delta_e = delta[e]
                    p = jnp.exp(sc[e * B_WIN:(e + 1) * B_WIN] - lse_e)
                    ps.append(p.astype(BF16))
                    dss.append(p * (dp[e * B_WIN:(e + 1) * B_WIN] - delta_e))
                    dsink_rows.append(-jnp.exp(sink[e] - lse_e) * delta_e)
                ds = jnp.concatenate(dss, axis=0)
                dbias_acc[gi] += ds
                dsb = ds.astype(BF16)
                dq = (lax.dot_general(kcat, dsb, TN, preferred_element_type=F32) * SCALE).T.astype(BF16)
                for pr in range(B_STACK):
                    dz_ref[gi, :, pr * LANES:(pr + 1) * LANES] = dq[pr * TQ:(pr + 1) * TQ]
                dk = _unstack_pair(jnp.dot(dsb, qs, preferred_element_type=F32), B_WIN)
                dv = _unstack_pair(jnp.dot(jnp.concatenate(ps, axis=0), dos, preferred_element_type=F32), B_WIN)
                krows = pl.ds(pl.multiple_of(b * TQ + pad - (B_WIN - TQ), KB), B_WIN)
                dkv_acc[krows, gi * LANES:(gi + 1) * LANES] += dk
                dkv_acc[krows, (B_KV_HEADS + gi) * LANES:(B_KV_HEADS + gi + 1) * LANES] += dv
            dsink_acc[...] += jnp.concatenate(
                dsink_rows + [jnp.zeros((8 - len(dsink_rows), B_ROWS), F32)], axis=0)

        @pl.when(b < left)
        def _():
            step(True)

        @pl.when(b >= left)
        def _():
            step(False)

        @pl.when(b == nb - 1)
        def _():
            lo_s = _lane_lo(s)
            for which in range(2):
                folded = []
                for gi in range(B_KV_HEADS):
                    part = dkv_acc[pad:pad + s, (which * B_KV_HEADS + gi) * LANES:(which * B_KV_HEADS + gi + 1) * LANES]
                    folded.append(part + pltpu.roll(part, HEAD_DIM, 1))
                dkv_ref[:, which * LANES:(which + 1) * LANES] = jnp.where(lo_s, folded[0], folded[1]).astype(BF16)
            lane8 = lax.broadcasted_iota(jnp.int32, dsink_ref.shape, 1)
            tot = jnp.zeros(dsink_ref.shape, F32)
            for h in range(HEADS):
                gi, pr, e = _b_head_place(h)
                dsum_ref[h] = _toeplitz_sum_t(
                    dbias_acc[gi, e * B_WIN:(e + 1) * B_WIN, pr * TQ:(pr + 1) * TQ], B_WIN)
                per_query = dsink_acc[2 * gi + e:2 * gi + e + 1, pr * TQ:(pr + 1) * TQ]
                tot = jnp.where(lane8 == h, jnp.sum(per_query, axis=1, keepdims=True), tot)
            dsink_ref[...] = tot

    row = pl.BlockSpec((TQ, D_MODEL), lambda b: (b, 0))
    base_spec = pl.BlockSpec((HEADS, 1, B_WIDE), lambda b: (0, 0, 0))
    return pl.pallas_call(
        body, name="attn_b_bwd", grid=(nb,),
        in_specs=[row, pl.BlockSpec((pad + s, B_KVX), lambda b: (0, 0)), row, row, row,
                  pl.BlockSpec((1, 8, B_ROWS), lambda b: (b, 0, 0)), base_spec,
                  pl.BlockSpec((1, HEADS), lambda b: (0, 0))],
        out_specs=[pl.BlockSpec((4, TQ, half), lambda b: (0, b, 0)),
                   pl.BlockSpec((s, 2 * LANES), lambda b: (0, 0)), base_spec,
                   pl.BlockSpec((8, LANES), lambda b: (0, 0))],
        out_shape=[jax.ShapeDtypeStruct((4, s, half), BF16), jax.ShapeDtypeStruct((s, 2 * LANES), BF16),
                   jax.ShapeDtypeStruct((HEADS, 1, B_WIDE), F32), jax.ShapeDtypeStruct((8, LANES), F32)],
        scratch_shapes=[pltpu.VMEM((B_KV_HEADS, 2 * B_WIN, B_ROWS), F32),
                        pltpu.VMEM((B_KV_HEADS, 2 * B_WIN, B_ROWS), F32),
                        pltpu.VMEM((pad + s, B_KVX), F32), pltpu.VMEM((8, B_ROWS), F32)],
        compiler_params=_params(("arbitrary",)),
    )(qb, kvx, gate, o, du, lse, base, sinks)


def _t5_bucket(rel):
    nb = T5_BUCKETS // 2
    max_exact = nb // 2
    ret = jnp.where(rel > 0, nb, 0)
    n = jnp.abs(rel)
    nf = jnp.maximum(n, 1).astype(jnp.float32)
    large = max_exact + (jnp.log(nf / max_exact) / math.log(T5_MAX_DIST / max_exact)
                         * (nb - max_exact)).astype(jnp.int32)
    large = jnp.minimum(large, nb - 1)
    return ret + jnp.where(n < max_exact, n, large)


def _a_offset_onehot():
    c = np.arange(A_WIN + TQ)
    dist = A_LEFT_CHUNKS * CHUNK + TQ - 1 - c
    idx = np.clip(dist, -A_REL_CLIP, A_REL_CLIP) + A_REL_CLIP
    onehot = np.zeros((A_WIN + TQ, 2 * A_REL_CLIP + 1), np.float32)
    onehot[c, idx] = 1.0
    return jnp.asarray(onehot)


def _b_offset_onehot():
    c = jnp.arange(B_WIN + TQ, dtype=jnp.int32)
    rel = c - (TQ - 1) - B_LEFT_CHUNKS * CHUNK
    return (_t5_bucket(rel)[:, None] == jnp.arange(T5_BUCKETS)[None, :]).astype(F32)


def _diag_rows(onehot, table):
    rows = jnp.dot(onehot, table.astype(F32), precision=lax.Precision.HIGHEST)
    return rows.T.reshape(HEADS, 1, onehot.shape[0])


def _diag_rows_grad(onehot, ddiag):
    return jnp.dot(ddiag.reshape(HEADS, onehot.shape[0]), onehot, precision=lax.Precision.HIGHEST).T


def _position():
    x, y, c = lax.axis_index("x"), lax.axis_index("y"), lax.axis_index("c")
    chips = [(1 - x, y), (x, 1 - y), (1 - x, 1 - y)]
    return x, y, c, chips


ANY = pl.BlockSpec(memory_space=pl.ANY)


def _allgather_hosted(shards, split):
    n = len(shards)

    def part(ref, t, half):
        if not split[t]:
            return ref
        rows = shards[t].shape[0] // 2
        return ref.at[pl.ds(half * rows, rows)]

    def copies(kind, ins, outs, sems):
        send_sems, recv_sems, pass_send, pass_recv, local_sems = sems
        x, y, c, chips = _position()
        mine = 2 * x + y
        if kind == "local":
            return [pltpu.make_async_copy(ins[t], outs[t].at[mine], local_sems.at[t]) for t in range(n)]
        made = []
        for t in range(n):
            for j, chip in enumerate(chips):
                theirs = 2 * chip[0] + chip[1]
                far = dict(send_sem=send_sems.at[3 * t + j], recv_sem=recv_sems.at[3 * t + j],
                           device_id=(chip[0], chip[1], c), device_id_type=MESH)
                near = dict(send_sem=pass_send.at[3 * t + j], recv_sem=pass_recv.at[3 * t + j],
                            device_id=(x, y, 1 - c), device_id_type=MESH)
                here = part(outs[t].at[theirs], t, c)
                if kind == "send":
                    made.append(pltpu.make_async_remote_copy(
                        src_ref=part(ins[t], t, c), dst_ref=part(outs[t].at[mine], t, c), **far))
                elif kind == "landed":
                    made.append(pltpu.make_async_remote_copy(src_ref=here, dst_ref=here, **far))
                elif not split[t]:
                    made.append(None)
                elif kind == "pass":
                    made.append(pltpu.make_async_remote_copy(src_ref=here, dst_ref=here, **near))
                else:
                    other = part(outs[t].at[theirs], t, 1 - c)
                    made.append(pltpu.make_async_remote_copy(src_ref=other, dst_ref=other, **near))
        return made

    def first(ins, outs, sems):
        for cp in copies("local", ins, outs, sems) + copies("send", ins, outs, sems):
            cp.start()

    def middle(ins, outs, sems):
        for got, cp in zip(copies("landed", ins, outs, sems), copies("pass", ins, outs, sems)):
            got.wait_recv()
            if cp is not None:
                cp.start()

    def last(ins, outs, sems):
        for cp in copies("passed", ins, outs, sems):
            if cp is not None:
                cp.wait_recv()
        for cp in copies("send", ins, outs, sems) + copies("pass", ins, outs, sems):
            if cp is not None:
                cp.wait_send()
        for cp in copies("local", ins, outs, sems):
            cp.wait()

    return _Hosted(shards, [jax.ShapeDtypeStruct((4,) + w.shape, w.dtype) for w in shards],
                   [pltpu.SemaphoreType.DMA((3 * n,))] * 4 + [pltpu.SemaphoreType.DMA((n,))],
                   first, middle, last)


def _scatter_hosted(grads):
    n = len(grads)

    def copies(ins, outs, sems):
        send_sems, recv_sems = sems
        x, y, c, chips = _position()
        return [pltpu.make_async_remote_copy(
            src_ref=ins[t].at[2 * chip[0] + chip[1]], dst_ref=outs[t].at[j],
            send_sem=send_sems.at[3 * t + j], recv_sem=recv_sems.at[3 * t + j],
            device_id=(chip[0], chip[1], c), device_id_type=MESH)
            for t in range(n) for j, chip in enumerate(chips)]

    def first(ins, outs, sems):
        for cp in copies(ins, outs, sems):
            cp.start()

    def last(ins, outs, sems):
        for cp in copies(ins, outs, sems):
            cp.wait()

    return _Hosted(grads, [jax.ShapeDtypeStruct((3,) + g.shape[1:], g.dtype) for g in grads],
                   [pltpu.SemaphoreType.DMA((3 * n,))] * 2, first, None, last)


def _scatter_on_sequencer(name, grad):
    src = jax.new_ref(grad, memory_space=pltpu.MemorySpace.HBM)
    dst = jax.empty_ref(jax.ShapeDtypeStruct((3,) + grad.shape[1:], grad.dtype),
                        memory_space=pltpu.MemorySpace.HBM)

    @pl.kernel(mesh=plsc.ScalarSubcoreMesh(axis_name="sequencer", num_cores=1), name=name,
               scratch_types=(pltpu.SemaphoreType.DMA((3,)), pltpu.SemaphoreType.DMA((3,))),
               compiler_params=pltpu.CompilerParams(collective_id=0))
    def launch(send_sems, recv_sems):
        x, y, c, chips = _position()
        barrier = pltpu.get_barrier_semaphore()
        for chip in chips:
            pl.semaphore_signal(barrier, inc=1, device_id=(chip[0], chip[1], c), device_id_type=MESH)
        pl.semaphore_wait(barrier, len(chips))
        copies = [pltpu.make_async_remote_copy(
            src_ref=src.at[2 * chip[0] + chip[1]], dst_ref=dst.at[j], send_sem=send_sems.at[j],
            recv_sem=recv_sems.at[j], device_id=(chip[0], chip[1], c), device_id_type=MESH)
            for j, chip in enumerate(chips)]
        for cp in copies:
            cp.start()
        for cp in copies:
            cp.wait()

    launch()
    return dst[...]


def _run_on_sequencer(name, hosted):
    ins = [jax.new_ref(a, memory_space=pltpu.MemorySpace.HBM) for a in hosted.inputs]
    outs = [jax.empty_ref(shape, memory_space=pltpu.MemorySpace.HBM) for shape in hosted.out_shapes]

    @pl.kernel(mesh=plsc.ScalarSubcoreMesh(axis_name="sequencer", num_cores=1), name=name,
               scratch_types=tuple(hosted.sems), compiler_params=pltpu.CompilerParams(collective_id=1))
    def launch(*sems):
        x, y, c, chips = _position()
        peers = [(chip[0], chip[1], c) for chip in chips] + [(x, y, 1 - c)]
        barrier = pltpu.get_barrier_semaphore()
        for peer in peers:
            pl.semaphore_signal(barrier, inc=1, device_id=peer, device_id_type=MESH)
        pl.semaphore_wait(barrier, len(peers))
        hosted.first(ins, outs, sems)
        hosted.middle(ins, outs, sems)
        hosted.last(ins, outs, sems)

    launch()
    return [o[...] for o in outs]


def _run_alone(name, hosted):
    n_in = len(hosted.inputs)
    n_out = len(hosted.out_shapes)

    def body(*refs):
        ins, outs, sems = refs[:n_in], refs[n_in:n_in + n_out], refs[n_in + n_out:]
        hosted.first(ins, outs, sems)
        if hosted.middle is not None:
            hosted.middle(ins, outs, sems)
        hosted.last(ins, outs, sems)

    return pl.pallas_call(
        body, name=name, in_specs=[ANY] * n_in, out_specs=[ANY] * n_out, out_shape=hosted.out_shapes,
        scratch_shapes=hosted.sems)(*hosted.inputs)


def _swap_with_sibling(blocks):
    n = len(blocks)

    def body(*refs):
        ins, outs = refs[:n], refs[n:2 * n]
        send_sems, recv_sems = refs[2 * n:]
        x, y, c, _ = _position()
        sends = [pltpu.make_async_remote_copy(
            src_ref=ins[t], dst_ref=outs[t], send_sem=send_sems.at[t], recv_sem=recv_sems.at[t],
            device_id=(x, y, 1 - c), device_id_type=MESH) for t in range(n)]
        for cp in sends:
            cp.start()
        for cp in sends:
            cp.wait()

    return pl.pallas_call(
        body, name="swap_with_sibling",
        in_specs=[ANY] * n, out_specs=[ANY] * n,
        out_shape=[jax.ShapeDtypeStruct(b.shape, b.dtype) for b in blocks],
        scratch_shapes=[pltpu.SemaphoreType.DMA((n,))] * 2,
    )(*blocks)


def _small_step(partials, loss, ws, ms, vs, shard_of):
    n = len(partials)
    terms = list(partials) + [loss]

    def body(*refs):
        ins, refs = refs[:n + 1], refs[n + 1:]
        w_refs, refs = refs[:n], refs[n:]
        m_refs, refs = refs[:n], refs[n:]
        v_refs, refs = refs[:n], refs[n:]
        outs, refs = refs[:4 * n + 1], refs[4 * n + 1:]
        slots, (send_sems, recv_sems) = refs[:n + 1], refs[n + 1:]
        x, y, c, _ = _position()
        me = 4 * x + 2 * y + c
        sends = []
        for t in range(n + 1):
            slots[t][me] = ins[t][...]
            for k in range(1, 8):
                peer = (x ^ (k >> 2), y ^ ((k >> 1) & 1), c ^ (k & 1))
                sends.append(pltpu.make_async_remote_copy(
                    src_ref=ins[t], dst_ref=slots[t].at[me], send_sem=send_sems.at[7 * t + k - 1],
                    recv_sem=recv_sems.at[7 * t + k - 1], device_id=peer, device_id_type=MESH))
        for cp in sends:
            cp.start()
        for t in range(n + 1):
            for k in range(1, 8):
                pltpu.make_async_remote_copy(
                    src_ref=ins[t], dst_ref=slots[t].at[me ^ k], send_sem=send_sems.at[7 * t + k - 1],
                    recv_sem=recv_sems.at[7 * t + k - 1], device_id=(x, y, c), device_id_type=MESH).wait_recv()
        for cp in sends:
            cp.wait_send()
        chip = 2 * x + y
        for t in range(n + 1):
            g = slots[t][0]
            for dev in range(1, 8):
                g = g + slots[t][dev]
            if t == n:
                outs[4 * n][...] = g
                continue
            if shard_of[t]:
                width = ws[t].shape[-1]
                mine = jnp.zeros(ws[t].shape, F32)
                for s in range(4):
                    mine = jnp.where(chip == s, g[:, s * width:(s + 1) * width], mine)
                g = mine
            delta, mn, vn = _adamw_math(w_refs[t][...], g, m_refs[t][...], v_refs[t][...])
            outs[4 * t][...] = g
            outs[4 * t + 1][...] = delta
            outs[4 * t + 2][...] = mn
            outs[4 * t + 3][...] = vn

    vmem = pl.BlockSpec(memory_space=pltpu.VMEM)
    out_shapes = []
    for t in range(n):
        out_shapes += [jax.ShapeDtypeStruct(ws[t].shape, F32)] * 4
    out_shapes.append(jax.ShapeDtypeStruct((1, 1), F32))
    out_shapes += [jax.ShapeDtypeStruct((8,) + a.shape, F32) for a in terms]
    res = pl.pallas_call(
        body, name="small_step",
        in_specs=[vmem] * (4 * n + 1), out_specs=[vmem] * len(out_shapes), out_shape=out_shapes,
        scratch_shapes=[pltpu.SemaphoreType.DMA((7 * (n + 1),))] * 2,
    )(*terms, *ws, *ms, *vs)
    return [res[4 * t:4 * t + 4] for t in range(n)], res[4 * n]


def _adamw_math(w, g, m, v):
    m = ADAM_B1 * m + (1.0 - ADAM_B1) * g
    v = ADAM_B2 * v + (1.0 - ADAM_B2) * (g * g)
    m_hat = m / (1.0 - ADAM_B1 ** ADAM_STEP)
    v_hat = v / (1.0 - ADAM_B2 ** ADAM_STEP)
    delta = -ADAM_LR * (m_hat / (jnp.sqrt(v_hat) + ADAM_EPS) + ADAM_WD * w)
    return delta, m, v


def _row_tile(rows):
    return min(rows, 256)


def _sum_partials(name, own, recv, chip):
    rows, cols = own.shape[1:]
    tr = _row_tile(rows)

    def body(chip_ref, own_ref, recv_ref, o_ref):
        acc = own_ref[...]
        for j in range(3):
            acc = acc + recv_ref[j].astype(F32)
        o_ref[...] = acc

    return pl.pallas_call(
        body, name=name,
        grid_spec=pltpu.PrefetchScalarGridSpec(
            num_scalar_prefetch=1, grid=(rows // tr,),
            in_specs=[pl.BlockSpec((None, tr, cols), lambda i, chip_ref: (chip_ref[0], i, 0)),
                      pl.BlockSpec((3, tr, cols), lambda i, chip_ref: (0, i, 0))],
            out_specs=pl.BlockSpec((tr, cols), lambda i, chip_ref: (i, 0))),
        out_shape=jax.ShapeDtypeStruct((rows, cols), F32),
        compiler_params=_params(("parallel",)),
    )(chip.reshape(1).astype(jnp.int32), own, recv)


def _adamw(name, w, m, v, g_parts):
    rows, cols = w.shape
    tr = _row_tile(rows)
    n = len(g_parts)

    def body(w_ref, m_ref, v_ref, *refs):
        g_refs = refs[:n]
        go_ref, d_ref, mo_ref, vo_ref = refs[n:]
        g = g_refs[0][...]
        for r in g_refs[1:]:
            g = g + r[...]
        delta, mn, vn = _adamw_math(w_ref[...], g, m_ref[...], v_ref[...])
        go_ref[...] = g
        d_ref[...] = delta
        mo_ref[...] = mn
        vo_ref[...] = vn

    spec = pl.BlockSpec((tr, cols), lambda i: (i, 0))
    return pl.pallas_call(
        body, name=name, grid=(rows // tr,),
        in_specs=[spec] * (3 + n), out_specs=[spec] * 4,
        out_shape=[jax.ShapeDtypeStruct((rows, cols), F32)] * 4,
        compiler_params=_params(("parallel",)),
    )(w, m, v, *g_parts)


def _local_step(x, target, ga, wa_in, rel_bias, later_shards, gk, t5, gb, sinks, gf):
    s, d = x.shape
    tm = min(TM_DENSE, s)
    nt = s // tm
    half = d // 2
    row = pl.BlockSpec((tm, d), lambda i: (i, 0))
    whole = lambda shape: pl.BlockSpec(shape, lambda *_: (0,) * len(shape))

    n1, = _norm_fwd("norm_a", x, ga)
    zqkv = _matmul("proj_a_qkv", n1, wa_in, dims=NN, grid=(3, nt + 1), zero_axis=1,
                   a_spec=pl.BlockSpec((tm, d), lambda j, i: (jnp.maximum(i - 1, 0), 0)),
                   b_spec=pl.BlockSpec((None, d, d), lambda j, i: (j, 0, 0)),
                   o_spec=pl.BlockSpec((None, tm, d), lambda j, i: (j, i, 0)),
                   out_shape=(3, tm + s, d), out_dtype=BF16)
    gate_a = _matmul("proj_a_gate", n1, wa_in, dims=NN, grid=(nt,),
                     a_spec=row, b_spec=pl.BlockSpec((None, d, d), lambda i: (3, 0, 0)), o_spec=row,
                     out_shape=(s, d), out_dtype=F32)
    onehot_a = _a_offset_onehot()
    diag_a = _diag_rows(onehot_a, rel_bias)
    (o_a, u_a, lse_a), gathered = _attn_a_fwd(
        zqkv, gate_a, diag_a, hosted=_allgather_hosted(later_shards, [True] * len(later_shards)))
    wa_out, wkv, wb_in, wb_out = gathered
    wa_out = wa_out.reshape(d, d)
    wkv = wkv.reshape(d, -1)
    wb_out = wb_out.reshape(d, d)
    h1 = _matmul("out_a", u_a, wa_out, dims=NN, grid=(nt,), a_spec=row, b_spec=whole((d, d)), o_spec=row,
                 out_shape=(s, d), out_dtype=F32, resid=x, resid_spec=row)

    nk, n2 = _norm_fwd("norm_kv_b", h1, jnp.concatenate([gk, gb], axis=0))
    kvw = wkv.shape[1]
    wkv_x = jnp.concatenate([wkv[:, (i // 2) * HEAD_DIM:(i // 2 + 1) * HEAD_DIM] for i in range(8)], axis=1)
    kvx = _matmul("proj_kv", nk, wkv_x, dims=NN, grid=(nt + 1,), zero_axis=0,
                  a_spec=pl.BlockSpec((tm, d), lambda i: (jnp.maximum(i - 1, 0), 0)), b_spec=whole((d, B_KVX)),
                  o_spec=pl.BlockSpec((tm, B_KVX), lambda i: (i, 0)), out_shape=(tm + s, B_KVX), out_dtype=BF16)
    qb = _matmul("proj_b_q", n2, wb_in, dims=NN, grid=(2, nt),
                 a_spec=pl.BlockSpec((tm, d), lambda j, i: (i, 0)),
                 b_spec=pl.BlockSpec((None, d, half), lambda j, i: (j, 0, 0)),
                 o_spec=pl.BlockSpec((tm, half), lambda j, i: (i, j)), out_shape=(s, d), out_dtype=BF16)
    gate_b = _matmul("proj_b_gate", n2, wb_in, dims=NN, grid=(2, nt),
                     a_spec=pl.BlockSpec((tm, d), lambda j, i: (i, 0)),
                     b_spec=pl.BlockSpec((None, d, half), lambda j, i: (2 + j, 0, 0)),
                     o_spec=pl.BlockSpec((tm, half), lambda j, i: (i, j)), out_shape=(s, d), out_dtype=F32)
    onehot_b = _b_offset_onehot()
    base_b = jnp.roll(_diag_rows(onehot_b, t5)[..., ::-1], TQ, axis=-1)
    o_b, u_b, lse_b = _attn_b_fwd(qb, kvx, gate_b, base_b, sinks)
    h2 = _matmul("out_b", u_b, wb_out, dims=NN, grid=(nt,), a_spec=row, b_spec=whole((d, d)), o_spec=row,
                 out_shape=(s, d), out_dtype=F32, resid=h1, resid_spec=row)

    dh2, loss, d_gf = _loss_head(h2, target, gf)

    du_b = _matmul("dout_b", dh2, wb_out, dims=NT, grid=(nt,), a_spec=row, b_spec=whole((d, d)), o_spec=row,
                   out_shape=(s, d), out_dtype=F32)
    d_wb_out = _matmul("dw_out_b", u_b, dh2, dims=TN, grid=(2,),
                       a_spec=whole((s, d)), b_spec=pl.BlockSpec((s, half), lambda j: (0, j)),
                       o_spec=pl.BlockSpec((d, half), lambda j: (0, j)),
                       out_shape=(d, d), out_dtype=F32, also_bf16=True)
    dz_b, dkv, dsum_b, dsinks = _attn_b_bwd(qb, kvx, gate_b, o_b, du_b, lse_b, base_b, sinks)
    ddiag_b = jnp.roll(dsum_b[..., ::-1], -1, axis=-1)
    dn2 = _matmul("dproj_b", dz_b, wb_in, dims=NT, grid=(nt,), parts=4,
                  a_spec=pl.BlockSpec((4, tm, half), lambda i: (0, i, 0)), b_spec=whole((4, d, half)),
                  o_spec=row, out_shape=(s, d), out_dtype=F32)
    d_wb_in = _matmul("dw_in_b", n2, dz_b, dims=TN, grid=(4,),
                      a_spec=whole((s, d)), b_spec=pl.BlockSpec((None, s, half), lambda j: (j, 0, 0)),
                      o_spec=pl.BlockSpec((None, d, half), lambda j: (j, 0, 0)),
                      out_shape=(4, d, half), out_dtype=F32, also_bf16=True)
    dnk = _matmul("dproj_kv", dkv, wkv, dims=NT, grid=(nt,),
                  a_spec=pl.BlockSpec((tm, kvw), lambda i: (i, 0)), b_spec=whole((d, kvw)), o_spec=row,
                  out_shape=(s, d), out_dtype=F32)
    d_wkv = _matmul("dw_kv", nk, dkv, dims=TN, grid=(1,),
                    a_spec=whole((s, d)), b_spec=whole((s, kvw)), o_spec=whole((d, kvw)),
                    out_shape=(d, kvw), out_dtype=F32, also_bf16=True)
    dh1, d_gkb = _norm_bwd("dnorm_kv_b", h1, dh2, [dnk, dn2], jnp.concatenate([gk, gb], axis=0))

    du_a = _matmul("dout_a", dh1, wa_out, dims=NT, grid=(nt,), a_spec=row, b_spec=whole((d, d)), o_spec=row,
                   out_shape=(s, d), out_dtype=F32)
    d_wa_out = _matmul("dw_out_a", u_a, dh1, dims=TN, grid=(2,),
                       a_spec=whole((s, d)), b_spec=pl.BlockSpec((s, half), lambda j: (0, j)),
                       o_spec=pl.BlockSpec((d, half), lambda j: (0, j)),
                       out_shape=(d, d), out_dtype=F32, also_bf16=True)
    early = dict(a_w_out=[g.reshape(4, d // 4, d) for g in d_wa_out],
                 kv_w=[g.reshape(4, d // 4, kvw) for g in d_wkv], b_w_in=list(d_wb_in),
                 b_w_out=[g.reshape(4, d // 4, d) for g in d_wb_out])
    (dz_a, ddiag_a), early_recv = _attn_a_bwd(
        zqkv, gate_a, o_a, du_a, lse_a, diag_a, hosted=_scatter_hosted([early[n][1] for n in early]))
    d_wa_in = _matmul("dw_in_a", n1, dz_a, dims=TN, grid=(4, 2),
                      a_spec=whole((s, d)), b_spec=pl.BlockSpec((None, s, half), lambda j, h: (j, 0, h)),
                      o_spec=pl.BlockSpec((None, d, half), lambda j, h: (j, 0, h)),
                      out_shape=(4, d, d), out_dtype=F32, also_bf16=True)
    late_recv = [_scatter_on_sequencer("scatter_a_w_in", d_wa_in[1])]
    tp = min(TM_PARTS, s)
    dn1 = _matmul("dproj_a", dz_a, wa_in, dims=NT, grid=(s // tp,), parts=4,
                  a_spec=pl.BlockSpec((4, tp, d), lambda i: (0, i, 0)), b_spec=whole((4, d, d)),
                  o_spec=pl.BlockSpec((tp, d), lambda i: (i, 0)), out_shape=(s, d), out_dtype=F32)
    grad_x, d_ga = _norm_bwd("dnorm_a", x, dh1, [dn1], ga)

    small = dict(
        a_norm=d_ga, a_rel_bias=_diag_rows_grad(onehot_a, ddiag_a), kv_norm=d_gkb[0:1],
        t5_bias=_diag_rows_grad(onehot_b, ddiag_b), b_norm=d_gkb[1:2], b_sinks=dsinks[0:1, :HEADS],
        final_norm=d_gf)
    own = dict(a_w_in=d_wa_in[0], **{n: early[n][0] for n in early})
    received = dict(a_w_in=late_recv[0], **dict(zip(early, early_recv)))
    return loss, grad_x, small, own, received


SMALL = ("a_norm", "a_rel_bias", "kv_norm", "t5_bias", "b_norm", "b_sinks", "final_norm")
BIG = ("a_w_in", "a_w_out", "kv_w", "b_w_in", "b_w_out")
ORDER = ("a_norm", "a_w_in", "a_rel_bias", "a_w_out", "kv_norm", "kv_w", "t5_bias", "b_norm", "b_w_in",
         "b_sinks", "b_w_out", "final_norm")


def kernel(x, a_norm, a_w_in, a_rel_bias, a_w_out, kv_norm, kv_w, t5_bias, b_norm, b_w_in, b_sinks, b_w_out, final_norm, loss_target, m_a_norm, m_a_w_in, m_a_rel_bias, m_a_w_out, m_kv_norm, m_kv_w, m_t5_bias, m_b_norm, m_b_w_in, m_b_sinks, m_b_w_out, m_final_norm, v_a_norm, v_a_w_in, v_a_rel_bias, v_a_w_out, v_kv_norm, v_kv_w, v_t5_bias, v_b_norm, v_b_w_in, v_b_sinks, v_b_w_out, v_final_norm):
    w = dict(a_norm=a_norm, a_w_in=a_w_in, a_rel_bias=a_rel_bias, a_w_out=a_w_out, kv_norm=kv_norm, kv_w=kv_w,
             t5_bias=t5_bias, b_norm=b_norm, b_w_in=b_w_in, b_sinks=b_sinks, b_w_out=b_w_out,
             final_norm=final_norm)
    m = dict(a_norm=m_a_norm, a_w_in=m_a_w_in, a_rel_bias=m_a_rel_bias, a_w_out=m_a_w_out, kv_norm=m_kv_norm,
             kv_w=m_kv_w, t5_bias=m_t5_bias, b_norm=m_b_norm, b_w_in=m_b_w_in, b_sinks=m_b_sinks,
             b_w_out=m_b_w_out, final_norm=m_final_norm)
    v = dict(a_norm=v_a_norm, a_w_in=v_a_w_in, a_rel_bias=v_a_rel_bias, a_w_out=v_a_w_out, kv_norm=v_kv_norm,
             kv_w=v_kv_w, t5_bias=v_t5_bias, b_norm=v_b_norm, b_w_in=v_b_w_in, b_sinks=v_b_sinks,
             b_w_out=v_b_w_out, final_norm=v_final_norm)
    d = D_MODEL
    chip = 2 * lax.axis_index("x") + lax.axis_index("y")

    shard2d = dict(a_w_in=a_w_in[0], a_w_out=a_w_out[0], kv_w=kv_w, b_w_in=b_w_in[0], b_w_out=b_w_out[0])

    wa_in, = _run_on_sequencer("allgather_first", _allgather_hosted([shard2d["a_w_in"].astype(BF16)], [True]))
    ga, = _run_alone("allgather_norm", _allgather_hosted([a_norm], [False]))
    ga = ga.reshape(1, d)

    loss, grad_x, small, own, received = _local_step(
        x[0], loss_target[0], ga, wa_in, a_rel_bias[0], [shard2d[n].astype(BF16) for n in BIG[1:]],
        kv_norm.reshape(1, d), t5_bias, b_norm, b_sinks, final_norm.reshape(1, d))

    out = {}
    as2d = lambda a: a.reshape(-1, a.shape[-1])
    small_res, loss_sum = _small_step(
        [small[n] for n in SMALL], loss, [as2d(w[n]) for n in SMALL], [as2d(m[n]) for n in SMALL],
        [as2d(v[n]) for n in SMALL], [n == "a_norm" for n in SMALL])
    for n, res in zip(SMALL, small_res):
        out[n] = [r.reshape(w[n].shape) for r in res]
    loss_out = loss_sum.reshape(())

    core_sums = [_sum_partials("sum_" + n, own[n], received[n], chip) for n in BIG]
    sibling_sums = _swap_with_sibling(core_sums)

    for n, mine, theirs in zip(BIG, core_sums, sibling_sums):
        res = _adamw("adamw_" + n, shard2d[n], m[n].reshape(shard2d[n].shape), v[n].reshape(shard2d[n].shape),
                     [mine, theirs])
        out[n] = [r.reshape(w[n].shape) for r in res]

    grads = [out[n][0] for n in ORDER]
    deltas = [out[n][1] for n in ORDER]
    new_m = [out[n][2] for n in ORDER]
    new_v = [out[n][3] for n in ORDER]
    return (loss_out, grad_x[None], *grads, *deltas, *new_m, *new_v)
```

```python
import functools
import math

import jax
import jax.numpy as jnp
import numpy as np
from jax import lax
from jax.experimental import pallas as pl
from jax.experimental.pallas import tpu as pltpu
from jax.experimental.pallas import tpu_sc as plsc

F32 = jnp.float32
BF16 = jnp.bfloat16
MESH = pl.DeviceIdType.MESH

D_MODEL = 1024
HEADS = 16
HEAD_DIM = 64
CHUNK = 64
RMS_EPS = 1e-6
SCALE = HEAD_DIM ** -0.5
A_LEFT_CHUNKS = 8
A_REL_CLIP = 256
B_LEFT_CHUNKS = 2
B_KV_HEADS = 2
B_GROUP = HEADS // B_KV_HEADS
T5_BUCKETS = 32
T5_MAX_DIST = 128
ADAM_LR = 0.001
ADAM_B1 = 0.9
ADAM_B2 = 0.999
ADAM_EPS = 1e-08
ADAM_WD = 0.01
ADAM_STEP = 10

MASKED = -1e30
LANES = 128
TQ = 128
A_PAIRS = 2
A_PAIRS_FWD = 4
KB = 128
A_KBLOCKS = A_LEFT_CHUNKS * CHUNK // KB + 1
B_KBLOCKS = B_LEFT_CHUNKS * CHUNK // KB + 1
A_WIN = A_KBLOCKS * KB
B_WIN = B_KBLOCKS * KB
TM = 512
TM_DENSE = 1024
TM_PARTS = 512
VMEM_LIMIT = 56 * 1024 * 1024

NT = (((1,), (1,)), ((), ()))
TN = (((0,), (0,)), ((), ()))
NN = (((1,), (0,)), ((), ()))


def _params(sem=None):
    return pltpu.CompilerParams(dimension_semantics=sem, vmem_limit_bytes=VMEM_LIMIT)


class _Hosted:
    def __init__(self, inputs, out_shapes, sems, first, middle, last):
        self.inputs, self.out_shapes, self.sems = list(inputs), list(out_shapes), list(sems)
        self.first, self.middle, self.last = first, middle, last


def _call(body, *, name, grid, in_specs, out_specs, out_shape, args, scratch_shapes=(), sem=None, hosted=None):
    in_specs, out_specs, out_shape = list(in_specs), list(out_specs), list(out_shape)
    scratch_shapes = list(scratch_shapes)
    if hosted is None:
        out = pl.pallas_call(
            body, name=name, grid=grid, in_specs=in_specs, out_specs=out_specs, out_shape=out_shape,
            scratch_shapes=scratch_shapes, compiler_params=_params(sem))(*args)
        return list(out), []
    n_in, n_out, n_scr = len(in_specs), len(out_shape), len(scratch_shapes)
    h_in, h_out = len(hosted.inputs), len(hosted.out_shapes)
    total = int(np.prod(grid)) if grid else 1

    def wrapped(*refs):
        ins, refs = refs[:n_in], refs[n_in:]
        h_ins, refs = refs[:h_in], refs[h_in:]
        outs, refs = refs[:n_out], refs[n_out:]
        h_outs, refs = refs[:h_out], refs[h_out:]
        scr, h_sems = refs[:n_scr], refs[n_scr:]
        step = 0
        for axis, size in enumerate(grid):
            step = step * size + pl.program_id(axis)

        @pl.when(step == 0)
        def _():
            hosted.first(h_ins, h_outs, h_sems)

        body(*ins, *outs, *scr)
        if hosted.middle is not None:
            @pl.when(step == total // 2)
            def _():
                hosted.middle(h_ins, h_outs, h_sems)

        @pl.when(step == total - 1)
        def _():
            hosted.last(h_ins, h_outs, h_sems)

    out = pl.pallas_call(
        wrapped, name=name, grid=grid, in_specs=in_specs + [ANY] * h_in, out_specs=out_specs + [ANY] * h_out,
        out_shape=out_shape + hosted.out_shapes, scratch_shapes=scratch_shapes + hosted.sems,
        compiler_params=_params(("arbitrary",) * len(grid)))(*args, *hosted.inputs)
    return list(out[:n_out]), list(out[n_out:])


def _matmul(name, a, b, *, dims, grid, a_spec, b_spec, o_spec, out_shape, out_dtype,
            parts=1, resid=None, resid_spec=None, also_bf16=False, hosted=None, zero_axis=None):
    def body(*refs):
        if zero_axis is None:
            product(*refs)
        else:
            @pl.when(pl.program_id(zero_axis) == 0)
            def _():
                refs[2][...] = jnp.zeros_like(refs[2])

            @pl.when(pl.program_id(zero_axis) > 0)
            def _():
                product(*refs)

    def product(*refs):
        a_ref, b_ref = refs[:2]
        r_ref = refs[2] if resid is not None else None
        o_ref = refs[3] if resid is not None else refs[2]
        if parts == 1:
            prod = lax.dot_general(a_ref[...].astype(BF16), b_ref[...].astype(BF16), dims,
                                   preferred_element_type=F32)
        else:
            prod = None
            for part in range(parts):
                term = lax.dot_general(a_ref[part].astype(BF16), b_ref[part].astype(BF16), dims,
                                       preferred_element_type=F32)
                prod = term if prod is None else prod + term
        if resid is not None:
            prod = r_ref[...] + prod
        o_ref[...] = prod.astype(out_dtype)
        if also_bf16:
            refs[-1][...] = prod.astype(BF16)

    in_specs = [a_spec, b_spec]
    args = [a, b]
    if resid is not None:
        in_specs.append(resid_spec)
        args.append(resid)
    sem = ["parallel"] * len(grid)
    out_specs = [o_spec]
    out_shapes = [jax.ShapeDtypeStruct(out_shape, out_dtype)]
    if also_bf16:
        out_specs.append(o_spec)
        out_shapes.append(jax.ShapeDtypeStruct(out_shape, BF16))
    out, extra = _call(body, name=name, grid=grid, in_specs=in_specs, out_specs=out_specs, out_shape=out_shapes,
                       args=args, sem=tuple(sem), hosted=hosted)
    res = out[0] if not also_bf16 else tuple(out)
    return res if hosted is None else (res, extra)


def _rms_rows(x):
    return lax.rsqrt(jnp.mean(x * x, axis=-1, keepdims=True) + RMS_EPS)


def _norm_fwd(name, x, gains):
    s, d = x.shape
    n = gains.shape[0]

    def body(x_ref, g_ref, *o_refs):
        xv = x_ref[...]
        xh = xv * _rms_rows(xv)
        for i in range(n):
            o_refs[i][...] = (xh * g_ref[i:i + 1, :]).astype(BF16)

    row = pl.BlockSpec((TM, d), lambda i: (i, 0))
    return pl.pallas_call(
        body, name=name, grid=(s // TM,),
        in_specs=[row, pl.BlockSpec((n, d), lambda i: (0, 0))],
        out_specs=[row] * n,
        out_shape=[jax.ShapeDtypeStruct((s, d), BF16)] * n,
        compiler_params=_params(("parallel",)),
    )(x, gains)


def _norm_bwd(name, x, dres, dns, gains):
    s, d = x.shape
    n = len(dns)

    def body(x_ref, r_ref, g_ref, *refs):
        dn_refs, dx_ref, dg_ref = refs[:n], refs[n], refs[n + 1]
        i = pl.program_id(0)
        xv = x_ref[...]
        r = _rms_rows(xv)
        xh = xv * r

        @pl.when(i == 0)
        def _():
            dg_ref[...] = jnp.zeros_like(dg_ref)

        a = None
        for j in range(n):
            dn = dn_refs[j][...]
            t = dn * g_ref[j:j + 1, :]
            a = t if a is None else a + t
            dg_ref[j:j + 1, :] += jnp.sum(dn * xh, axis=0, keepdims=True)
        dx_ref[...] = r_ref[...] + r * (a - xh * jnp.mean(xh * a, axis=-1, keepdims=True))

    row = pl.BlockSpec((TM, d), lambda i: (i, 0))
    small = pl.BlockSpec((n, d), lambda i: (0, 0))
    return pl.pallas_call(
        body, name=name, grid=(s // TM,),
        in_specs=[row, row, small] + [row] * n,
        out_specs=[row, small],
        out_shape=[jax.ShapeDtypeStruct((s, d), F32), jax.ShapeDtypeStruct((n, d), F32)],
        compiler_params=_params(("arbitrary",)),
    )(x, dres, gains, *dns)


def _loss_head(h2, target, gain):
    s, d = h2.shape

    def body(h_ref, t_ref, g_ref, dh_ref, loss_ref, dg_ref):
        i = pl.program_id(0)
        hv = h_ref[...]
        r = _rms_rows(hv)
        hh = hv * r
        g = g_ref[...]
        err = hh * g - t_ref[...]
        part = 0.5 * jnp.sum(jnp.sum(err * err, axis=-1, keepdims=True) * (1.0 / d), axis=0, keepdims=True)
        dy = err * (1.0 / d)
        a = dy * g
        dh_ref[...] = r * (a - hh * jnp.mean(hh * a, axis=-1, keepdims=True))
        dg = jnp.sum(dy * hh, axis=0, keepdims=True)

        @pl.when(i == 0)
        def _():
            loss_ref[...] = part
            dg_ref[...] = dg

        @pl.when(i > 0)
        def _():
            loss_ref[...] += part
            dg_ref[...] += dg

    row = pl.BlockSpec((TM, d), lambda i: (i, 0))
    return pl.pallas_call(
        body, name="loss_head", grid=(s // TM,),
        in_specs=[row, row, pl.BlockSpec((1, d), lambda i: (0, 0))],
        out_specs=[row, pl.BlockSpec((1, 1), lambda i: (0, 0)), pl.BlockSpec((1, d), lambda i: (0, 0))],
        out_shape=[jax.ShapeDtypeStruct((s, d), F32), jax.ShapeDtypeStruct((1, 1), F32),
                   jax.ShapeDtypeStruct((1, d), F32)],
        compiler_params=_params(("arbitrary",)),
    )(h2, target, gain)


def _silu_parts(g):
    sig = jax.nn.sigmoid(g)
    return g * sig, sig * (1.0 + g * (1.0 - sig))


def _lane_lo(rows):
    return lax.broadcasted_iota(jnp.int32, (rows, LANES), 1) < HEAD_DIM


def _stack_pair(x):
    lo = _lane_lo(x.shape[0])
    zero = jnp.zeros_like(x)
    return jnp.concatenate([jnp.where(lo, x, zero), jnp.where(lo, zero, x)], axis=0)


def _unstack_pair(y, w):
    return jnp.where(_lane_lo(w), y[:w], y[w:])


def _block_valid(b, left_blocks, width):
    col = lax.broadcasted_iota(jnp.int32, (1, 2 * width), 1)
    col = jnp.where(col >= width, col - width, col)
    return (col // KB + (b - left_blocks)) >= 0


def _toeplitz_tile(diag_row, width, left_chunks):
    wide = width + TQ
    rolled = pltpu.roll(jnp.broadcast_to(diag_row, (TQ, wide)), 1, 1, stride=1, stride_axis=0)
    i = lax.broadcasted_iota(jnp.int32, (TQ, width), 0) // CHUNK
    j = lax.broadcasted_iota(jnp.int32, (TQ, width), 1) // CHUNK
    dc = i + left_chunks - j
    return jnp.where((dc >= 0) & (dc <= left_chunks), rolled[:, TQ:], MASKED)


def _toeplitz_sum(tile, width):
    flip = (lax.broadcasted_iota(jnp.int32, (TQ, TQ), 0) + lax.broadcasted_iota(jnp.int32, (TQ, TQ), 1)
            == TQ - 1).astype(F32)
    reversed_rows = jnp.dot(flip, tile, precision=lax.Precision.HIGHEST, preferred_element_type=F32)
    padded = jnp.concatenate([reversed_rows, jnp.zeros((TQ, TQ), F32)], axis=1)
    rolled = pltpu.roll(padded, 0, 1, stride=1, stride_axis=0)
    return jnp.sum(rolled, axis=0, keepdims=True)


def _softmax_pair(sc, w, sink=None):
    ps, inv, lses = [], [], []
    for e in range(2):
        sh = sc[:, e * w:(e + 1) * w]
        m = jnp.max(sh, axis=-1, keepdims=True)
        if sink is not None:
            m = jnp.maximum(m, sink[e])
        ex = jnp.exp(sh - m)
        l = jnp.sum(ex, axis=-1, keepdims=True)
        if sink is not None:
            l = l + jnp.exp(sink[e] - m)
        ps.append(ex.astype(BF16))
        inv.append(1.0 / l)
        lses.append(m + jnp.log(l))
    return jnp.concatenate(ps, axis=-1), inv, lses


def _softmax_pair_bwd(sc, dp, lse, delta, w):
    ps, dss = [], []
    for e in range(2):
        p = jnp.exp(sc[:, e * w:(e + 1) * w] - lse[e])
        ps.append(p)
        dss.append(p * (dp[:, e * w:(e + 1) * w] - delta[e]))
    return jnp.concatenate(ps, axis=-1), jnp.concatenate(dss, axis=-1)


def _pair_rowsums(x, lo):
    zero = jnp.zeros_like(x)
    return (jnp.sum(jnp.where(lo, x, zero), axis=-1, keepdims=True),
            jnp.sum(jnp.where(lo, zero, x), axis=-1, keepdims=True))


def _a_qkv_specs(rows, pad, pw):
    return [pl.BlockSpec((None, TQ, pw), lambda p, b: (0, b + pad // TQ, p)),
            pl.BlockSpec((None, rows, pw), lambda p, b: (1, 0, p)),
            pl.BlockSpec((None, rows, pw), lambda p, b: (2, 0, p))]


def _window(ref, b, pad, win, lanes):
    start = pl.multiple_of(b * TQ + pad - (win - TQ), KB)
    return ref[pl.ds(start, win), lanes]


def _attn_a_fwd(zqkv, g, diag, hosted=None):
    s = g.shape[0]
    pad = zqkv.shape[1] - s
    nb = s // TQ
    left = A_KBLOCKS - 1
    pairs = A_PAIRS_FWD
    pw = pairs * LANES
    wide = A_WIN + TQ

    def body(q_ref, k_ref, v_ref, g_ref, diag_ref, o_ref, u_ref, lse_ref, bias_scr):
        b = pl.program_id(1)

        @pl.when(b == 0)
        def _():
            for hh in range(2 * pairs):
                bias_scr[hh // 2, :, (hh % 2) * A_WIN:(hh % 2 + 1) * A_WIN] = _toeplitz_tile(
                    diag_ref[hh], A_WIN, A_LEFT_CHUNKS)

        def step(first_blocks):
            lo = _lane_lo(TQ)
            for pp in range(pairs):
                ln = slice(pp * LANES, (pp + 1) * LANES)
                kcat = _stack_pair(_window(k_ref, b, pad, A_WIN, ln))
                vcat = _stack_pair(_window(v_ref, b, pad, A_WIN, ln))
                sc = lax.dot_general(q_ref[:, ln] * SCALE, kcat, NT, preferred_element_type=F32) + bias_scr[pp]
                if first_blocks:
                    sc = jnp.where(_block_valid(b, left, A_WIN), sc, MASKED)
                p, inv, lses = _softmax_pair(sc, A_WIN)
                ov = jnp.dot(p, vcat, preferred_element_type=F32) * jnp.where(lo, inv[0], inv[1])
                o_ref[:, ln] = ov
                lse_ref[pp] = jnp.where(lo, lses[0], lses[1])
                sg, _ = _silu_parts(g_ref[:, ln])
                u_ref[:, ln] = (ov * sg).astype(BF16)

        @pl.when(b < left)
        def _():
            step(True)

        @pl.when(b >= left)
        def _():
            step(False)

    tile = pl.BlockSpec((TQ, pw), lambda p, b: (b, p))
    return _call(
        body, name="attn_a_fwd", grid=(HEADS // 2 // pairs, nb),
        in_specs=_a_qkv_specs(pad + s, pad, pw) + [
            tile, pl.BlockSpec((2 * pairs, 1, wide), lambda p, b: (p, 0, 0))],
        out_specs=[tile, tile, pl.BlockSpec((pairs, TQ, LANES), lambda p, b: (p, b, 0))],
        out_shape=[jax.ShapeDtypeStruct((s, D_MODEL), F32), jax.ShapeDtypeStruct((s, D_MODEL), BF16),
                   jax.ShapeDtypeStruct((HEADS // 2, s, LANES), F32)],
        scratch_shapes=[pltpu.VMEM((pairs, TQ, 2 * A_WIN), F32)],
        sem=("parallel", "arbitrary"), hosted=hosted,
        args=(zqkv, zqkv, zqkv, g, diag))


def _attn_a_bwd(zqkv, g, o, du, lse, diag, hosted=None):
    s = g.shape[0]
    pad = zqkv.shape[1] - s
    nb = s // TQ
    left = A_KBLOCKS - 1
    pw = A_PAIRS * LANES
    wide = A_WIN + TQ

    def body(q_ref, k_ref, v_ref, g_ref, o_ref, du_ref, lse_ref, diag_ref, dz_ref, ddiag_ref,
             bias_scr, dbias_acc, dk_acc, dv_acc):
        b = pl.program_id(1)

        @pl.when(b == 0)
        def _():
            for hh in range(2 * A_PAIRS):
                bias_scr[hh // 2, :, (hh % 2) * A_WIN:(hh % 2 + 1) * A_WIN] = _toeplitz_tile(
                    diag_ref[hh], A_WIN, A_LEFT_CHUNKS)
            dbias_acc[...] = jnp.zeros_like(dbias_acc)
            dk_acc[...] = jnp.zeros_like(dk_acc)
            dv_acc[...] = jnp.zeros_like(dv_acc)

        def step(first_blocks):
            lo = _lane_lo(TQ)
            upper = lax.broadcasted_iota(jnp.int32, (LANES, A_WIN), 0) < HEAD_DIM
            rows = pl.ds(pl.multiple_of(b * TQ, TQ), TQ)
            sg, dsg = _silu_parts(g_ref[...])
            duv = du_ref[...]
            ov = o_ref[...]
            do = duv * sg
            dz_ref[3, rows, :] = (duv * ov * dsg).astype(BF16)
            do_o = do * ov
            do_bf = do.astype(BF16)
            for pp in range(A_PAIRS):
                ln = slice(pp * LANES, (pp + 1) * LANES)
                q = q_ref[:, ln] * SCALE
                kcat = _stack_pair(_window(k_ref, b, pad, A_WIN, ln))
                vcat = _stack_pair(_window(v_ref, b, pad, A_WIN, ln))
                sc = lax.dot_general(q, kcat, NT, preferred_element_type=F32) + bias_scr[pp]
                if first_blocks:
                    sc = jnp.where(_block_valid(b, left, A_WIN), sc, MASKED)
                lse_t = lse_ref[pp]
                dp = lax.dot_general(do_bf[:, ln], vcat, NT, preferred_element_type=F32)
                p, ds = _softmax_pair_bwd(sc, dp, (lse_t[:, 0:1], lse_t[:, HEAD_DIM:HEAD_DIM + 1]),
                                          _pair_rowsums(do_o[:, ln], lo), A_WIN)
                dbias_acc[pp] += ds
                dsb = ds.astype(BF16)
                dz_ref[0, rows, ln] = (jnp.dot(dsb, kcat, preferred_element_type=F32) * SCALE).astype(BF16)
                dkt = lax.dot_general(q, dsb, TN, preferred_element_type=F32)
                dvt = lax.dot_general(do_bf[:, ln], p.astype(BF16), TN, preferred_element_type=F32)
                dkt = jnp.where(upper, dkt[:, :A_WIN], dkt[:, A_WIN:])
                dvt = jnp.where(upper, dvt[:, :A_WIN], dvt[:, A_WIN:])
                for t in range(A_KBLOCKS):
                    blk = b + (pad // KB - left + t)
                    dk_acc[blk, ln, :] += dkt[:, t * KB:(t + 1) * KB]
                    dv_acc[blk, ln, :] += dvt[:, t * KB:(t + 1) * KB]

        @pl.when(b < left)
        def _():
            step(True)

        @pl.when(b >= left)
        def _():
            step(False)

        @pl.when(b == nb - 1)
        def _():
            for kb in range(s // KB):
                dz_ref[1, kb * KB:(kb + 1) * KB, :] = dk_acc[pad // KB + kb].T.astype(BF16)
                dz_ref[2, kb * KB:(kb + 1) * KB, :] = dv_acc[pad // KB + kb].T.astype(BF16)
            for hh in range(2 * A_PAIRS):
                ddiag_ref[hh] = _toeplitz_sum(
                    dbias_acc[hh // 2, :, (hh % 2) * A_WIN:(hh % 2 + 1) * A_WIN], A_WIN)

    tile = pl.BlockSpec((TQ, pw), lambda p, b: (b, p))
    diag_spec = pl.BlockSpec((2 * A_PAIRS, 1, wide), lambda p, b: (p, 0, 0))
    return _call(
        body, name="attn_a_bwd", grid=(HEADS // 2 // A_PAIRS, nb),
        in_specs=_a_qkv_specs(pad + s, pad, pw) + [
            tile, tile, tile, pl.BlockSpec((A_PAIRS, TQ, LANES), lambda p, b: (p, b, 0)), diag_spec],
        out_specs=[pl.BlockSpec((4, s, pw), lambda p, b: (0, 0, p)), diag_spec],
        out_shape=[jax.ShapeDtypeStruct((4, s, D_MODEL), BF16),
                   jax.ShapeDtypeStruct((HEADS, 1, wide), F32)],
        scratch_shapes=[pltpu.VMEM((A_PAIRS, TQ, 2 * A_WIN), F32), pltpu.VMEM((A_PAIRS, TQ, 2 * A_WIN), F32),
                        pltpu.VMEM(((pad + s) // KB, pw, KB), F32), pltpu.VMEM(((pad + s) // KB, pw, KB), F32)],
        sem=("parallel", "arbitrary"), hosted=hosted,
        args=(zqkv, zqkv, zqkv, g, o, du, lse, diag))


B_STACK = B_GROUP // 2
B_KVX = 4 * LANES
B_ROWS = B_STACK * TQ
B_WIDE = B_WIN + TQ


def _b_head_place(h):
    return h // B_GROUP, (h % B_GROUP) // 2, h % 2


def _toeplitz_tile_t(base_row, width, left_chunks):
    wide = width + TQ
    rolled = pltpu.roll(jnp.broadcast_to(base_row, (width, wide)), 0, 1, stride=1, stride_axis=0)
    j = lax.broadcasted_iota(jnp.int32, (width, TQ), 0) // CHUNK
    i = lax.broadcasted_iota(jnp.int32, (width, TQ), 1) // CHUNK
    dc = i + left_chunks - j
    return jnp.where((dc >= 0) & (dc <= left_chunks), rolled[:, :TQ], MASKED)


def _toeplitz_sum_t(tile_t, width):
    flip = (lax.broadcasted_iota(jnp.int32, (width, width), 0) + lax.broadcasted_iota(jnp.int32, (width, width), 1)
            == width - 1).astype(F32)
    reversed_rows = jnp.dot(flip, tile_t, precision=lax.Precision.HIGHEST, preferred_element_type=F32)
    padded = jnp.concatenate([reversed_rows, jnp.zeros((width, width), F32)], axis=1)
    rolled = pltpu.roll(padded, 0, 1, stride=1, stride_axis=0)
    return jnp.sum(rolled, axis=0, keepdims=True)


def _b_build_bias(base_ref, bias_scr):
    for h in range(HEADS):
        gi, pr, e = _b_head_place(h)
        bias_scr[gi, e * B_WIN:(e + 1) * B_WIN, pr * TQ:(pr + 1) * TQ] = _toeplitz_tile_t(
            base_ref[h], B_WIN, B_LEFT_CHUNKS)


def _b_stack(x, gi):
    return jnp.concatenate(
        [x[:, (B_STACK * gi + pr) * LANES:(B_STACK * gi + pr + 1) * LANES] for pr in range(B_STACK)], axis=0)


def _b_sink_rows(sink_ref, gi):
    block = lax.broadcasted_iota(jnp.int32, (1, B_ROWS), 1) // TQ
    rows = []
    for e in range(2):
        row = jnp.zeros((1, B_ROWS), F32)
        for pr in range(B_STACK):
            h = B_GROUP * gi + 2 * pr + e
            row = jnp.where(block == pr, sink_ref[0:1, h:h + 1], row)
        rows.append(row)
    return rows


def _b_scores_t(q_ref, kvv, bias_scr, gi, b, left, first_blocks):
    kcat = _stack_pair(kvv[:, gi * LANES:(gi + 1) * LANES])
    vcat = _stack_pair(kvv[:, (B_KV_HEADS + gi) * LANES:(B_KV_HEADS + gi + 1) * LANES])
    qs = _b_stack(q_ref, gi) * SCALE
    sc = lax.dot_general(kcat, qs, NT, preferred_element_type=F32) + bias_scr[gi]
    if first_blocks:
        row = lax.broadcasted_iota(jnp.int32, (2 * B_WIN, 1), 0)
        row = jnp.where(row >= B_WIN, row - B_WIN, row)
        sc = jnp.where((row // KB + (b - left)) >= 0, sc, MASKED)
    return kcat, vcat, qs, sc


def _attn_b_fwd(qb, kvx, gate, base, sinks):
    s = qb.shape[0]
    pad = kvx.shape[0] - s
    nb = s // TQ
    left = B_KBLOCKS - 1

    def body(q_ref, kv_ref, g_ref, base_ref, sink_ref, o_ref, u_ref, lse_ref, bias_scr):
        b = pl.program_id(0)

        @pl.when(b == 0)
        def _():
            _b_build_bias(base_ref, bias_scr)

        def step(first_blocks):
            kvv = _window(kv_ref, b, pad, B_WIN, slice(None))
            upper = lax.broadcasted_iota(jnp.int32, (LANES, B_ROWS), 0) < HEAD_DIM
            lse_rows = []
            for gi in range(B_KV_HEADS):
                kcat, vcat, qs, sc = _b_scores_t(q_ref, kvv, bias_scr, gi, b, left, first_blocks)
                sink = _b_sink_rows(sink_ref, gi)
                ps, inv = [], []
                for e in range(2):
                    sh = sc[e * B_WIN:(e + 1) * B_WIN]
                    m = jnp.maximum(jnp.max(sh, axis=0, keepdims=True), sink[e])
                    ex = jnp.exp(sh - m)
                    l = jnp.sum(ex, axis=0, keepdims=True) + jnp.exp(sink[e] - m)
                    ps.append(ex.astype(BF16))
                    inv.append(1.0 / l)
                    lse_rows.append(m + jnp.log(l))
                pt = jnp.concatenate(ps, axis=0)
                ot = lax.dot_general(vcat, pt, TN, preferred_element_type=F32) * jnp.where(upper, inv[0], inv[1])
                ov = ot.T
                for pr in range(B_STACK):
                    pair = B_STACK * gi + pr
                    o_ref[:, pair * LANES:(pair + 1) * LANES] = ov[pr * TQ:(pr + 1) * TQ]
            lse_ref[0] = jnp.concatenate(lse_rows + [jnp.zeros((8 - len(lse_rows), B_ROWS), F32)], axis=0)
            sg, _ = _silu_parts(g_ref[...])
            u_ref[...] = (o_ref[...] * sg).astype(BF16)

        @pl.when(b < left)
        def _():
            step(True)

        @pl.when(b >= left)
        def _():
            step(False)

    row = pl.BlockSpec((TQ, D_MODEL), lambda b: (b, 0))
    return pl.pallas_call(
        body, name="attn_b_fwd", grid=(nb,),
        in_specs=[row, pl.BlockSpec((pad + s, B_KVX), lambda b: (0, 0)), row,
                  pl.BlockSpec((HEADS, 1, B_WIDE), lambda b: (0, 0, 0)), pl.BlockSpec((1, HEADS), lambda b: (0, 0))],
        out_specs=[row, row, pl.BlockSpec((1, 8, B_ROWS), lambda b: (b, 0, 0))],
        out_shape=[jax.ShapeDtypeStruct((s, D_MODEL), F32), jax.ShapeDtypeStruct((s, D_MODEL), BF16),
                   jax.ShapeDtypeStruct((nb, 8, B_ROWS), F32)],
        scratch_shapes=[pltpu.VMEM((B_KV_HEADS, 2 * B_WIN, B_ROWS), F32)],
        compiler_params=_params(("arbitrary",)),
    )(qb, kvx, gate, base, sinks)


def _attn_b_bwd(qb, kvx, gate, o, du, lse, base, sinks):
    s = qb.shape[0]
    pad = kvx.shape[0] - s
    nb = s // TQ
    left = B_KBLOCKS - 1
    half = D_MODEL // 2

    def body(q_ref, kv_ref, g_ref, o_ref, du_ref, lse_ref, base_ref, sink_ref, dz_ref, dkv_ref, dsum_ref,
             dsink_ref, bias_scr, dbias_acc, dkv_acc, dsink_acc):
        b = pl.program_id(0)

        @pl.when(b == 0)
        def _():
            _b_build_bias(base_ref, bias_scr)
            dbias_acc[...] = jnp.zeros_like(dbias_acc)
            dkv_acc[...] = jnp.zeros_like(dkv_acc)
            dsink_acc[...] = jnp.zeros_like(dsink_acc)

        def step(first_blocks):
            kvv = _window(kv_ref, b, pad, B_WIN, slice(None))
            sg, dsg = _silu_parts(g_ref[...])
            duv = du_ref[...]
            ov = o_ref[...]
            do = duv * sg
            dgate = (duv * ov * dsg).astype(BF16)
            dz_ref[2] = dgate[:, :half]
            dz_ref[3] = dgate[:, half:]
            do_o = do * ov
            do_bf = do.astype(BF16)
            lse_all = lse_ref[0]
            dsink_rows = []
            for gi in range(B_KV_HEADS):
                kcat, vcat, qs, sc = _b_scores_t(q_ref, kvv, bias_scr, gi, b, left, first_blocks)
                dos = _b_stack(do_bf, gi)
                doo_t = _b_stack(do_o, gi).T
                delta = (jnp.sum(doo_t[:HEAD_DIM], axis=0, keepdims=True),
                         jnp.sum(doo_t[HEAD_DIM:], axis=0, keepdims=True))
                sink = _b_sink_rows(sink_ref, gi)
                dp = lax.dot_general(vcat, dos, NT, preferred_element_type=F32)
                ps, dss = [], []
                for e in range(2):
                    lse_e = lse_all[2 * gi + e:2 * gi + e + 1]
                    delta_e = delta[e]
                    p = jnp.exp(sc[e * B_WIN:(e + 1) * B_WIN] - lse_e)
                    ps.append(p.astype(BF16))
                    dss.append(p * (dp[e * B_WIN:(e + 1) * B_WIN] - delta_e))
                    dsink_rows.append(-jnp.exp(sink[e] - lse_e) * delta_e)
                ds = jnp.concatenate(dss, axis=0)
                dbias_acc[gi] += ds
                dsb = ds.astype(BF16)
                dq = (lax.dot_general(kcat, dsb, TN, preferred_element_type=F32) * SCALE).T.astype(BF16)
                for pr in range(B_STACK):
                    dz_ref[gi, :, pr * LANES:(pr + 1) * LANES] = dq[pr * TQ:(pr + 1) * TQ]
                dk = _unstack_pair(jnp.dot(dsb, qs, preferred_element_type=F32), B_WIN)
                dv = _unstack_pair(jnp.dot(jnp.concatenate(ps, axis=0), dos, preferred_element_type=F32), B_WIN)
                krows = pl.ds(pl.multiple_of(b * TQ + pad - (B_WIN - TQ), KB), B_WIN)
                dkv_acc[krows, gi * LANES:(gi + 1) * LANES] += dk
                dkv_acc[krows, (B_KV_HEADS + gi) * LANES:(B_KV_HEADS + gi + 1) * LANES] += dv
            dsink_acc[...] += jnp.concatenate(
                dsink_rows + [jnp.zeros((8 - len(dsink_rows), B_ROWS), F32)], axis=0)

        @pl.when(b < left)
        def _():
            step(True)

        @pl.when(b >= left)
        def _():
            step(False)

        @pl.when(b == nb - 1)
        def _():
            lo_s = _lane_lo(s)
            for which in range(2):
                folded = []
                for gi in range(B_KV_HEADS):
                    part = dkv_acc[pad:pad + s, (which * B_KV_HEADS + gi) * LANES:(which * B_KV_HEADS + gi + 1) * LANES]
                    folded.append(part + pltpu.roll(part, HEAD_DIM, 1))
                dkv_ref[:, which * LANES:(which + 1) * LANES] = jnp.where(lo_s, folded[0], folded[1]).astype(BF16)
            lane8 = lax.broadcasted_iota(jnp.int32, dsink_ref.shape, 1)
            tot = jnp.zeros(dsink_ref.shape, F32)
            for h in range(HEADS):
                gi, pr, e = _b_head_place(h)
                dsum_ref[h] = _toeplitz_sum_t(
                    dbias_acc[gi, e * B_WIN:(e + 1) * B_WIN, pr * TQ:(pr + 1) * TQ], B_WIN)
                per_query = dsink_acc[2 * gi + e:2 * gi + e + 1, pr * TQ:(pr + 1) * TQ]
                tot = jnp.where(lane8 == h, jnp.sum(per_query, axis=1, keepdims=True), tot)
            dsink_ref[...] = tot

    row = pl.BlockSpec((TQ, D_MODEL), lambda b: (b, 0))
    base_spec = pl.BlockSpec((HEADS, 1, B_WIDE), lambda b: (0, 0, 0))
    return pl.pallas_call(
        body, name="attn_b_bwd", grid=(nb,),
        in_specs=[row, pl.BlockSpec((pad + s, B_KVX), lambda b: (0, 0)), row, row, row,
                  pl.BlockSpec((1, 8, B_ROWS), lambda b: (b, 0, 0)), base_spec,
                  pl.BlockSpec((1, HEADS), lambda b: (0, 0))],
        out_specs=[pl.BlockSpec((4, TQ, half), lambda b: (0, b, 0)),
                   pl.BlockSpec((s, 2 * LANES), lambda b: (0, 0)), base_spec,
                   pl.BlockSpec((8, LANES), lambda b: (0, 0))],
        out_shape=[jax.ShapeDtypeStruct((4, s, half), BF16), jax.ShapeDtypeStruct((s, 2 * LANES), BF16),
                   jax.ShapeDtypeStruct((HEADS, 1, B_WIDE), F32), jax.ShapeDtypeStruct((8, LANES), F32)],
        scratch_shapes=[pltpu.VMEM((B_KV_HEADS, 2 * B_WIN, B_ROWS), F32),
                        pltpu.VMEM((B_KV_HEADS, 2 * B_WIN, B_ROWS), F32),
                        pltpu.VMEM((pad + s, B_KVX), F32), pltpu.VMEM((8, B_ROWS), F32)],
        compiler_params=_params(("arbitrary",)),
    )(qb, kvx, gate, o, du, lse, base, sinks)


def _t5_bucket(rel):
    nb = T5_BUCKETS // 2
    max_exact = nb // 2
    ret = jnp.where(rel > 0, nb, 0)
    n = jnp.abs(rel)
    nf = jnp.maximum(n, 1).astype(jnp.float32)
    large = max_exact + (jnp.log(nf / max_exact) / math.log(T5_MAX_DIST / max_exact)
                         * (nb - max_exact)).astype(jnp.int32)
    large = jnp.minimum(large, nb - 1)
    return ret + jnp.where(n < max_exact, n, large)


def _a_offset_onehot():
    c = np.arange(A_WIN + TQ)
    dist = A_LEFT_CHUNKS * CHUNK + TQ - 1 - c
    idx = np.clip(dist, -A_REL_CLIP, A_REL_CLIP) + A_REL_CLIP
    onehot = np.zeros((A_WIN + TQ, 2 * A_REL_CLIP + 1), np.float32)
    onehot[c, idx] = 1.0
    return jnp.asarray(onehot)


def _b_offset_onehot():
    c = jnp.arange(B_WIN + TQ, dtype=jnp.int32)
    rel = c - (TQ - 1) - B_LEFT_CHUNKS * CHUNK
    return (_t5_bucket(rel)[:, None] == jnp.arange(T5_BUCKETS)[None, :]).astype(F32)


def _diag_rows(onehot, table):
    rows = jnp.dot(onehot, table.astype(F32), precision=lax.Precision.HIGHEST)
    return rows.T.reshape(HEADS, 1, onehot.shape[0])


def _diag_rows_grad(onehot, ddiag):
    return jnp.dot(ddiag.reshape(HEADS, onehot.shape[0]), onehot, precision=lax.Precision.HIGHEST).T


def _position():
    x, y, c = lax.axis_index("x"), lax.axis_index("y"), lax.axis_index("c")
    chips = [(1 - x, y), (x, 1 - y), (1 - x, 1 - y)]
    return x, y, c, chips


ANY = pl.BlockSpec(memory_space=pl.ANY)


def _allgather_hosted(shards, split):
    n = len(shards)

    def part(ref, t, half):
        if not split[t]:
            return ref
        rows = shards[t].shape[0] // 2
        return ref.at[pl.ds(half * rows, rows)]

    def copies(kind, ins, outs, sems):
        send_sems, recv_sems, pass_send, pass_recv, local_sems = sems
        x, y, c, chips = _position()
        mine = 2 * x + y
        if kind == "local":
            return [pltpu.make_async_copy(ins[t], outs[t].at[mine], local_sems.at[t]) for t in range(n)]
        made = []
        for t in range(n):
            for j, chip in enumerate(chips):
                theirs = 2 * chip[0] + chip[1]
                far = dict(send_sem=send_sems.at[3 * t + j], recv_sem=recv_sems.at[3 * t + j],
                           device_id=(chip[0], chip[1], c), device_id_type=MESH)
                near = dict(send_sem=pass_send.at[3 * t + j], recv_sem=pass_recv.at[3 * t + j],
                            device_id=(x, y, 1 - c), device_id_type=MESH)
                here = part(outs[t].at[theirs], t, c)
                if kind == "send":
                    made.append(pltpu.make_async_remote_copy(
                        src_ref=part(ins[t], t, c), dst_ref=part(outs[t].at[mine], t, c), **far))
                elif kind == "landed":
                    made.append(pltpu.make_async_remote_copy(src_ref=here, dst_ref=here, **far))
                elif not split[t]:
                    made.append(None)
                elif kind == "pass":
                    made.append(pltpu.make_async_remote_copy(src_ref=here, dst_ref=here, **near))
                else:
                    other = part(outs[t].at[theirs], t, 1 - c)
                    made.append(pltpu.make_async_remote_copy(src_ref=other, dst_ref=other, **near))
        return made

    def first(ins, outs, sems):
        for cp in copies("local", ins, outs, sems) + copies("send", ins, outs, sems):
            cp.start()

    def middle(ins, outs, sems):
        for got, cp in zip(copies("landed", ins, outs, sems), copies("pass", ins, outs, sems)):
            got.wait_recv()
            if cp is not None:
                cp.start()

    def last(ins, outs, sems):
        for cp in copies("passed", ins, outs, sems):
            if cp is not None:
                cp.wait_recv()
        for cp in copies("send", ins, outs, sems) + copies("pass", ins, outs, sems):
            if cp is not None:
                cp.wait_send()
        for cp in copies("local", ins, outs, sems):
            cp.wait()

    return _Hosted(shards, [jax.ShapeDtypeStruct((4,) + w.shape, w.dtype) for w in shards],
                   [pltpu.SemaphoreType.DMA((3 * n,))] * 4 + [pltpu.SemaphoreType.DMA((n,))],
                   first, middle, last)


def _scatter_hosted(grads):
    n = len(grads)

    def copies(ins, outs, sems):
        send_sems, recv_sems = sems
        x, y, c, chips = _position()
        return [pltpu.make_async_remote_copy(
            src_ref=ins[t].at[2 * chip[0] + chip[1]], dst_ref=outs[t].at[j],
            send_sem=send_sems.at[3 * t + j], recv_sem=recv_sems.at[3 * t + j],
            device_id=(chip[0], chip[1], c), device_id_type=MESH)
            for t in range(n) for j, chip in enumerate(chips)]

    def first(ins, outs, sems):
        for cp in copies(ins, outs, sems):
            cp.start()

    def last(ins, outs, sems):
        for cp in copies(ins, outs, sems):
            cp.wait()

    return _Hosted(grads, [jax.ShapeDtypeStruct((3,) + g.shape[1:], g.dtype) for g in grads],
                   [pltpu.SemaphoreType.DMA((3 * n,))] * 2, first, None, last)


def _scatter_on_sequencer(name, grad):
    src = jax.new_ref(grad, memory_space=pltpu.MemorySpace.HBM)
    dst = jax.empty_ref(jax.ShapeDtypeStruct((3,) + grad.shape[1:], grad.dtype),
                        memory_space=pltpu.MemorySpace.HBM)

    @pl.kernel(mesh=plsc.ScalarSubcoreMesh(axis_name="sequencer", num_cores=1), name=name,
               scratch_types=(pltpu.SemaphoreType.DMA((3,)), pltpu.SemaphoreType.DMA((3,))),
               compiler_params=pltpu.CompilerParams(collective_id=0))
    def launch(send_sems, recv_sems):
        x, y, c, chips = _position()
        barrier = pltpu.get_barrier_semaphore()
        for chip in chips:
            pl.semaphore_signal(barrier, inc=1, device_id=(chip[0], chip[1], c), device_id_type=MESH)
        pl.semaphore_wait(barrier, len(chips))
        copies = [pltpu.make_async_remote_copy(
            src_ref=src.at[2 * chip[0] + chip[1]], dst_ref=dst.at[j], send_sem=send_sems.at[j],
            recv_sem=recv_sems.at[j], device_id=(chip[0], chip[1], c), device_id_type=MESH)
            for j, chip in enumerate(chips)]
        for cp in copies:
            cp.start()
        for cp in copies:
            cp.wait()

    launch()
    return dst[...]


def _run_on_sequencer(name, hosted):
    ins = [jax.new_ref(a, memory_space=pltpu.MemorySpace.HBM) for a in hosted.inputs]
    outs = [jax.empty_ref(shape, memory_space=pltpu.MemorySpace.HBM) for shape in hosted.out_shapes]

    @pl.kernel(mesh=plsc.ScalarSubcoreMesh(axis_name="sequencer", num_cores=1), name=name,
               scratch_types=tuple(hosted.sems), compiler_params=pltpu.CompilerParams(collective_id=1))
    def launch(*sems):
        x, y, c, chips = _position()
        peers = [(chip[0], chip[1], c) for chip in chips] + [(x, y, 1 - c)]
        barrier = pltpu.get_barrier_semaphore()
        for peer in peers:
            pl.semaphore_signal(barrier, inc=1, device_id=peer, device_id_type=MESH)
        pl.semaphore_wait(barrier, len(peers))
        hosted.first(ins, outs, sems)
        hosted.middle(ins, outs, sems)
        hosted.last(ins, outs, sems)

    launch()
    return [o[...] for o in outs]


def _run_alone(name, hosted):
    n_in = len(hosted.inputs)
    n_out = len(hosted.out_shapes)

    def body(*refs):
        ins, outs, sems = refs[:n_in], refs[n_in:n_in + n_out], refs[n_in + n_out:]
        hosted.first(ins, outs, sems)
        if hosted.middle is not None:
            hosted.middle(ins, outs, sems)
        hosted.last(ins, outs, sems)

    return pl.pallas_call(
        body, name=name, in_specs=[ANY] * n_in, out_specs=[ANY] * n_out, out_shape=hosted.out_shapes,
        scratch_shapes=hosted.sems)(*hosted.inputs)


def _swap_with_sibling(blocks):
    n = len(blocks)

    def body(*refs):
        ins, outs = refs[:n], refs[n:2 * n]
        send_sems, recv_sems = refs[2 * n:]
        x, y, c, _ = _position()
        sends = [pltpu.make_async_remote_copy(
            src_ref=ins[t], dst_ref=outs[t], send_sem=send_sems.at[t], recv_sem=recv_sems.at[t],
            device_id=(x, y, 1 - c), device_id_type=MESH) for t in range(n)]
        for cp in sends:
            cp.start()
        for cp in sends:
            cp.wait()

    return pl.pallas_call(
        body, name="swap_with_sibling",
        in_specs=[ANY] * n, out_specs=[ANY] * n,
        out_shape=[jax.ShapeDtypeStruct(b.shape, b.dtype) for b in blocks],
        scratch_shapes=[pltpu.SemaphoreType.DMA((n,))] * 2,
    )(*blocks)


def _small_step(partials, extras, ws, ms, vs, shard_of):
    n = len(partials)
    terms = list(partials) + list(extras)
    nt = len(terms)

    def body(*refs):
        ins, refs = refs[:nt], refs[nt:]
        w_refs, refs = refs[:n], refs[n:]
        m_refs, refs = refs[:n], refs[n:]
        v_refs, refs = refs[:n], refs[n:]
        outs, refs = refs[:4 * n + nt - n], refs[4 * n + nt - n:]
        slots, (send_sems, recv_sems) = refs[:nt], refs[nt:]
        x, y, c, _ = _position()
        me = 4 * x + 2 * y + c
        sends = []
        for t in range(nt):
            slots[t][me] = ins[t][...]
            for k in range(1, 8):
                peer = (x ^ (k >> 2), y ^ ((k >> 1) & 1), c ^ (k & 1))
                sends.append(pltpu.make_async_remote_copy(
                    src_ref=ins[t], dst_ref=slots[t].at[me], send_sem=send_sems.at[7 * t + k - 1],
                    recv_sem=recv_sems.at[7 * t + k - 1], device_id=peer, device_id_type=MESH))
        for cp in sends:
            cp.start()
        for t in range(nt):
            for k in range(1, 8):
                pltpu.make_async_remote_copy(
                    src_ref=ins[t], dst_ref=slots[t].at[me ^ k], send_sem=send_sems.at[7 * t + k - 1],
                    recv_sem=recv_sems.at[7 * t + k - 1], device_id=(x, y, c), device_id_type=MESH).wait_recv()
        for cp in sends:
            cp.wait_send()
        chip = 2 * x + y
        for t in range(nt):
            g = slots[t][0]
            for dev in range(1, 8):
                g = g + slots[t][dev]
            if t >= n:
                outs[4 * n + t - n][...] = g
                continue
            if shard_of[t]:
                width = ws[t].shape[-1]
                mine = jnp.zeros(ws[t].shape, F32)
                for s in range(4):
                    mine = jnp.where(chip == s, g[:, s * width:(s + 1) * width], mine)
                g = mine
            delta, mn, vn = _adamw_math(w_refs[t][...], g, m_refs[t][...], v_refs[t][...])
            outs[4 * t][...] = g
            outs[4 * t + 1][...] = delta
            outs[4 * t + 2][...] = mn
            outs[4 * t + 3][...] = vn

    vmem = pl.BlockSpec(memory_space=pltpu.VMEM)
    out_shapes = []
    for t in range(n):
        out_shapes += [jax.ShapeDtypeStruct(ws[t].shape, F32)] * 4
    out_shapes += [jax.ShapeDtypeStruct(a.shape, F32) for a in extras]
    out_shapes += [jax.ShapeDtypeStruct((8,) + a.shape, F32) for a in terms]
    res = pl.pallas_call(
        body, name="small_step",
        in_specs=[vmem] * (nt + 3 * n), out_specs=[vmem] * len(out_shapes), out_shape=out_shapes,
        scratch_shapes=[pltpu.SemaphoreType.DMA((7 * nt,))] * 2,
    )(*terms, *ws, *ms, *vs)
    return [res[4 * t:4 * t + 4] for t in range(n)], res[4 * n:4 * n + nt - n]


def _adamw_math(w, g, m, v):
    m = ADAM_B1 * m + (1.0 - ADAM_B1) * g
    v = ADAM_B2 * v + (1.0 - ADAM_B2) * (g * g)
    m_hat = m / (1.0 - ADAM_B1 ** ADAM_STEP)
    v_hat = v / (1.0 - ADAM_B2 ** ADAM_STEP)
    delta = -ADAM_LR * (m_hat / (jnp.sqrt(v_hat) + ADAM_EPS) + ADAM_WD * w)
    return delta, m, v


def _row_tile(rows):
    return 256 if rows % 256 == 0 else rows


def _sum_partials(name, own, recv, chip):
    rows, cols = own.shape[1:]
    tr = _row_tile(rows)

    def body(chip_ref, own_ref, recv_ref, o_ref):
        acc = own_ref[...]
        for j in range(3):
            acc = acc + recv_ref[j].astype(F32)
        o_ref[...] = acc

    return pl.pallas_call(
        body, name=name,
        grid_spec=pltpu.PrefetchScalarGridSpec(
            num_scalar_prefetch=1, grid=(rows // tr,),
            in_specs=[pl.BlockSpec((None, tr, cols), lambda i, chip_ref: (chip_ref[0], i, 0)),
                      pl.BlockSpec((3, tr, cols), lambda i, chip_ref: (0, i, 0))],
            out_specs=pl.BlockSpec((tr, cols), lambda i, chip_ref: (i, 0))),
        out_shape=jax.ShapeDtypeStruct((rows, cols), F32),
        compiler_params=_params(("parallel",)),
    )(chip.reshape(1).astype(jnp.int32), own, recv)


def _adamw(name, w, m, v, g_parts):
    rows, cols = w.shape
    tr = _row_tile(rows)
    n = len(g_parts)

    def body(w_ref, m_ref, v_ref, *refs):
        g_refs = refs[:n]
        go_ref, d_ref, mo_ref, vo_ref = refs[n:]
        g = g_refs[0][...]
        for r in g_refs[1:]:
            g = g + r[...]
        delta, mn, vn = _adamw_math(w_ref[...], g, m_ref[...], v_ref[...])
        go_ref[...] = g
        d_ref[...] = delta
        mo_ref[...] = mn
        vo_ref[...] = vn

    spec = pl.BlockSpec((tr, cols), lambda i: (i, 0))
    return pl.pallas_call(
        body, name=name, grid=(rows // tr,),
        in_specs=[spec] * (3 + n), out_specs=[spec] * 4,
        out_shape=[jax.ShapeDtypeStruct((rows, cols), F32)] * 4,
        compiler_params=_params(("parallel",)),
    )(w, m, v, *g_parts)


def _local_step(x, target, ga, wa_in, rel_bias, later_shards, gk, t5, gb, sinks, gf):
    s, d = x.shape
    tm = min(TM_DENSE, s)
    nt = s // tm
    half = d // 2
    row = pl.BlockSpec((tm, d), lambda i: (i, 0))
    whole = lambda shape: pl.BlockSpec(shape, lambda *_: (0,) * len(shape))

    n1, = _norm_fwd("norm_a", x, ga)
    zqkv = _matmul("proj_a_qkv", n1, wa_in, dims=NN, grid=(3, nt + 1), zero_axis=1,
                   a_spec=pl.BlockSpec((tm, d), lambda j, i: (jnp.maximum(i - 1, 0), 0)),
                   b_spec=pl.BlockSpec((None, d, d), lambda j, i: (j, 0, 0)),
                   o_spec=pl.BlockSpec((None, tm, d), lambda j, i: (j, i, 0)),
                   out_shape=(3, tm + s, d), out_dtype=BF16)
    gate_a = _matmul("proj_a_gate", n1, wa_in, dims=NN, grid=(nt,),
                     a_spec=row, b_spec=pl.BlockSpec((None, d, d), lambda i: (3, 0, 0)), o_spec=row,
                     out_shape=(s, d), out_dtype=F32)
    onehot_a = _a_offset_onehot()
    diag_a = _diag_rows(onehot_a, rel_bias)
    (o_a, u_a, lse_a), gathered = _attn_a_fwd(
        zqkv, gate_a, diag_a, hosted=_allgather_hosted(later_shards, [True] * len(later_shards)))
    wa_out, wkv, wb_in, wb_out = gathered
    wa_out = wa_out.reshape(d, d)
    wkv = wkv.reshape(d, -1)
    wb_out = wb_out.reshape(d, d)
    h1 = _matmul("out_a", u_a, wa_out, dims=NN, grid=(nt,), a_spec=row, b_spec=whole((d, d)), o_spec=row,
                 out_shape=(s, d), out_dtype=F32, resid=x, resid_spec=row)

    nk, n2 = _norm_fwd("norm_kv_b", h1, jnp.concatenate([gk, gb], axis=0))
    kvw = wkv.shape[1]
    wkv_x = jnp.concatenate([wkv[:, (i // 2) * HEAD_DIM:(i // 2 + 1) * HEAD_DIM] for i in range(8)], axis=1)
    kvx = _matmul("proj_kv", nk, wkv_x, dims=NN, grid=(nt + 1,), zero_axis=0,
                  a_spec=pl.BlockSpec((tm, d), lambda i: (jnp.maximum(i - 1, 0), 0)), b_spec=whole((d, B_KVX)),
                  o_spec=pl.BlockSpec((tm, B_KVX), lambda i: (i, 0)), out_shape=(tm + s, B_KVX), out_dtype=BF16)
    qb = _matmul("proj_b_q", n2, wb_in, dims=NN, grid=(2, nt),
                 a_spec=pl.BlockSpec((tm, d), lambda j, i: (i, 0)),
                 b_spec=pl.BlockSpec((None, d, half), lambda j, i: (j, 0, 0)),
                 o_spec=pl.BlockSpec((tm, half), lambda j, i: (i, j)), out_shape=(s, d), out_dtype=BF16)
    gate_b = _matmul("proj_b_gate", n2, wb_in, dims=NN, grid=(2, nt),
                     a_spec=pl.BlockSpec((tm, d), lambda j, i: (i, 0)),
                     b_spec=pl.BlockSpec((None, d, half), lambda j, i: (2 + j, 0, 0)),
                     o_spec=pl.BlockSpec((tm, half), lambda j, i: (i, j)), out_shape=(s, d), out_dtype=F32)
    onehot_b = _b_offset_onehot()
    base_b = jnp.roll(_diag_rows(onehot_b, t5)[..., ::-1], TQ, axis=-1)
    o_b, u_b, lse_b = _attn_b_fwd(qb, kvx, gate_b, base_b, sinks)
    h2 = _matmul("out_b", u_b, wb_out, dims=NN, grid=(nt,), a_spec=row, b_spec=whole((d, d)), o_spec=row,
                 out_shape=(s, d), out_dtype=F32, resid=h1, resid_spec=row)

    dh2, loss, d_gf = _loss_head(h2, target, gf)

    du_b = _matmul("dout_b", dh2, wb_out, dims=NT, grid=(nt,), a_spec=row, b_spec=whole((d, d)), o_spec=row,
                   out_shape=(s, d), out_dtype=F32)
    d_wb_out = _matmul("dw_out_b", u_b, dh2, dims=TN, grid=(2,),
                       a_spec=whole((s, d)), b_spec=pl.BlockSpec((s, half), lambda j: (0, j)),
                       o_spec=pl.BlockSpec((d, half), lambda j: (0, j)),
                       out_shape=(d, d), out_dtype=F32, also_bf16=True)
    dz_b, dkv, dsum_b, dsinks = _attn_b_bwd(qb, kvx, gate_b, o_b, du_b, lse_b, base_b, sinks)
    ddiag_b = jnp.roll(dsum_b[..., ::-1], -1, axis=-1)
    dn2 = _matmul("dproj_b", dz_b, wb_in, dims=NT, grid=(nt,), parts=4,
                  a_spec=pl.BlockSpec((4, tm, half), lambda i: (0, i, 0)), b_spec=whole((4, d, half)),
                  o_spec=row, out_shape=(s, d), out_dtype=F32)
    d_wb_in = _matmul("dw_in_b", n2, dz_b, dims=TN, grid=(4,),
                      a_spec=whole((s, d)), b_spec=pl.BlockSpec((None, s, half), lambda j: (j, 0, 0)),
                      o_spec=pl.BlockSpec((None, d, half), lambda j: (j, 0, 0)),
                      out_shape=(4, d, half), out_dtype=F32, also_bf16=True)
    dnk = _matmul("dproj_kv", dkv, wkv, dims=NT, grid=(nt,),
                  a_spec=pl.BlockSpec((tm, kvw), lambda i: (i, 0)), b_spec=whole((d, kvw)), o_spec=row,
                  out_shape=(s, d), out_dtype=F32)
    d_wkv = _matmul("dw_kv", nk, dkv, dims=TN, grid=(1,),
                    a_spec=whole((s, d)), b_spec=whole((s, kvw)), o_spec=whole((d, kvw)),
                    out_shape=(d, kvw), out_dtype=F32, also_bf16=True)
    dh1, d_gkb = _norm_bwd("dnorm_kv_b", h1, dh2, [dnk, dn2], jnp.concatenate([gk, gb], axis=0))

    du_a = _matmul("dout_a", dh1, wa_out, dims=NT, grid=(nt,), a_spec=row, b_spec=whole((d, d)), o_spec=row,
                   out_shape=(s, d), out_dtype=F32)
    d_wa_out = _matmul("dw_out_a", u_a, dh1, dims=TN, grid=(2,),
                       a_spec=whole((s, d)), b_spec=pl.BlockSpec((s, half), lambda j: (0, j)),
                       o_spec=pl.BlockSpec((d, half), lambda j: (0, j)),
                       out_shape=(d, d), out_dtype=F32, also_bf16=True)
    early = dict(a_w_out=[g.reshape(4, d // 4, d) for g in d_wa_out],
                 kv_w=[g.reshape(4, d // 4, kvw) for g in d_wkv], b_w_in=list(d_wb_in),
                 b_w_out=[g.reshape(4, d // 4, d) for g in d_wb_out])
    (dz_a, ddiag_a), early_recv = _attn_a_bwd(
        zqkv, gate_a, o_a, du_a, lse_a, diag_a, hosted=_scatter_hosted([early[n][1] for n in early]))
    d_wa_in = _matmul("dw_in_a", n1, dz_a, dims=TN, grid=(4, 2),
                      a_spec=whole((s, d)), b_spec=pl.BlockSpec((None, s, half), lambda j, h: (j, 0, h)),
                      o_spec=pl.BlockSpec((None, d, half), lambda j, h: (j, 0, h)),
                      out_shape=(4, d, d), out_dtype=F32, also_bf16=True)
    late_recv = [_scatter_on_sequencer("scatter_a_w_in", d_wa_in[1])]
    tp = min(TM_PARTS, s)
    dn1 = _matmul("dproj_a", dz_a, wa_in, dims=NT, grid=(s // tp,), parts=4,
                  a_spec=pl.BlockSpec((4, tp, d), lambda i: (0, i, 0)), b_spec=whole((4, d, d)),
                  o_spec=pl.BlockSpec((tp, d), lambda i: (i, 0)), out_shape=(s, d), out_dtype=F32)
    grad_x, d_ga = _norm_bwd("dnorm_a", x, dh1, [dn1], ga)

    small = dict(a_norm=d_ga, kv_norm=d_gkb[0:1], b_norm=d_gkb[1:2], b_sinks=dsinks[0:1, :HEADS], final_norm=d_gf)
    small["by_offset"] = dict(a_rel_bias=(onehot_a, ddiag_a.reshape(HEADS, -1)),
                              t5_bias=(onehot_b, ddiag_b.reshape(HEADS, -1)))
    own = dict(a_w_in=d_wa_in[0], **{n: early[n][0] for n in early})
    received = dict(a_w_in=late_recv[0], **dict(zip(early, early_recv)))
    return loss, grad_x, small, own, received


SMALL = ("a_norm", "kv_norm", "b_norm", "b_sinks", "final_norm")
TABLES = ("a_rel_bias", "t5_bias")
BIG = ("a_w_in", "a_w_out", "kv_w", "b_w_in", "b_w_out")
ORDER = ("a_norm", "a_w_in", "a_rel_bias", "a_w_out", "kv_norm", "kv_w", "t5_bias", "b_norm", "b_w_in",
         "b_sinks", "b_w_out", "final_norm")


def kernel(x, a_norm, a_w_in, a_rel_bias, a_w_out, kv_norm, kv_w, t5_bias, b_norm, b_w_in, b_sinks, b_w_out, final_norm, loss_target, m_a_norm, m_a_w_in, m_a_rel_bias, m_a_w_out, m_kv_norm, m_kv_w, m_t5_bias, m_b_norm, m_b_w_in, m_b_sinks, m_b_w_out, m_final_norm, v_a_norm, v_a_w_in, v_a_rel_bias, v_a_w_out, v_kv_norm, v_kv_w, v_t5_bias, v_b_norm, v_b_w_in, v_b_sinks, v_b_w_out, v_final_norm):
    w = dict(a_norm=a_norm, a_w_in=a_w_in, a_rel_bias=a_rel_bias, a_w_out=a_w_out, kv_norm=kv_norm, kv_w=kv_w,
             t5_bias=t5_bias, b_norm=b_norm, b_w_in=b_w_in, b_sinks=b_sinks, b_w_out=b_w_out,
             final_norm=final_norm)
    m = dict(a_norm=m_a_norm, a_w_in=m_a_w_in, a_rel_bias=m_a_rel_bias, a_w_out=m_a_w_out, kv_norm=m_kv_norm,
             kv_w=m_kv_w, t5_bias=m_t5_bias, b_norm=m_b_norm, b_w_in=m_b_w_in, b_sinks=m_b_sinks,
             b_w_out=m_b_w_out, final_norm=m_final_norm)
    v = dict(a_norm=v_a_norm, a_w_in=v_a_w_in, a_rel_bias=v_a_rel_bias, a_w_out=v_a_w_out, kv_norm=v_kv_norm,
             kv_w=v_kv_w, t5_bias=v_t5_bias, b_norm=v_b_norm, b_w_in=v_b_w_in, b_sinks=v_b_sinks,
             b_w_out=v_b_w_out, final_norm=v_final_norm)
    d = D_MODEL
    chip = 2 * lax.axis_index("x") + lax.axis_index("y")

    shard2d = dict(a_w_in=a_w_in[0], a_w_out=a_w_out[0], kv_w=kv_w, b_w_in=b_w_in[0], b_w_out=b_w_out[0])

    wa_in, = _run_on_sequencer("allgather_first", _allgather_hosted([shard2d["a_w_in"].astype(BF16)], [True]))
    ga, = _run_alone("allgather_norm", _allgather_hosted([a_norm], [False]))
    ga = ga.reshape(1, d)

    loss, grad_x, small, own, received = _local_step(
        x[0], loss_target[0], ga, wa_in, a_rel_bias[0], [shard2d[n].astype(BF16) for n in BIG[1:]],
        kv_norm.reshape(1, d), t5_bias, b_norm, b_sinks, final_norm.reshape(1, d))

    out = {}
    as2d = lambda a: a.reshape(-1, a.shape[-1])
    small_res, (loss_sum, *offset_sums) = _small_step(
        [small[n] for n in SMALL], [loss] + [small["by_offset"][n][1] for n in TABLES],
        [as2d(w[n]) for n in SMALL], [as2d(m[n]) for n in SMALL], [as2d(v[n]) for n in SMALL],
        [n == "a_norm" for n in SMALL])
    for n, res in zip(SMALL, small_res):
        out[n] = [r.reshape(w[n].shape) for r in res]
    loss_out = loss_sum.reshape(())
    for n, summed in zip(TABLES, offset_sums):
        grad = _diag_rows_grad(small["by_offset"][n][0], summed)
        res = _adamw("adamw_" + n, as2d(w[n]), as2d(m[n]), as2d(v[n]), [grad])
        out[n] = [r.reshape(w[n].shape) for r in res]

    core_sums = [_sum_partials("sum_" + n, own[n], received[n], chip) for n in BIG]
    sibling_sums = _swap_with_sibling(core_sums)

    for n, mine, theirs in zip(BIG, core_sums, sibling_sums):
        res = _adamw("adamw_" + n, shard2d[n], m[n].reshape(shard2d[n].shape), v[n].reshape(shard2d[n].shape),
                     [mine, theirs])
        out[n] = [r.reshape(w[n].shape) for r in res]

    grads = [out[n][0] for n in ORDER]
    deltas = [out[n][1] for n in ORDER]
    new_m = [out[n][2] for n in ORDER]
    new_v = [out[n][3] for n in ORDER]
    return (loss_out, grad_x[None], *grads, *deltas, *new_m, *new_v)
```

```python
import functools
import math

import jax
import jax.numpy as jnp
import numpy as np
from jax import lax
from jax.experimental import pallas as pl
from jax.experimental.pallas import tpu as pltpu
from jax.experimental.pallas import tpu_sc as plsc

F32 = jnp.float32
BF16 = jnp.bfloat16
MESH = pl.DeviceIdType.MESH

D_MODEL = 1024
HEADS = 16
HEAD_DIM = 64
CHUNK = 64
RMS_EPS = 1e-6
SCALE = HEAD_DIM ** -0.5
A_LEFT_CHUNKS = 8
A_REL_CLIP = 256
B_LEFT_CHUNKS = 2
B_KV_HEADS = 2
B_GROUP = HEADS // B_KV_HEADS
T5_BUCKETS = 32
T5_MAX_DIST = 128
ADAM_LR = 0.001
ADAM_B1 = 0.9
ADAM_B2 = 0.999
ADAM_EPS = 1e-08
ADAM_WD = 0.01
ADAM_STEP = 10

MASKED = -1e30
LANES = 128
TQ = 128
A_PAIRS = 2
A_PAIRS_FWD = 4
KB = 128
A_KBLOCKS = A_LEFT_CHUNKS * CHUNK // KB + 1
B_KBLOCKS = B_LEFT_CHUNKS * CHUNK // KB + 1
A_WIN = A_KBLOCKS * KB
B_WIN = B_KBLOCKS * KB
TM = 512
TM_DENSE = 1024
TM_PARTS = 512
VMEM_LIMIT = 56 * 1024 * 1024

NT = (((1,), (1,)), ((), ()))
TN = (((0,), (0,)), ((), ()))
NN = (((1,), (0,)), ((), ()))


def _params(sem=None):
    return pltpu.CompilerParams(dimension_semantics=sem, vmem_limit_bytes=VMEM_LIMIT)


class _Hosted:
    def __init__(self, inputs, out_shapes, sems, first, middle, last):
        self.inputs, self.out_shapes, self.sems = list(inputs), list(out_shapes), list(sems)
        self.first, self.middle, self.last = first, middle, last


def _call(body, *, name, grid, in_specs, out_specs, out_shape, args, scratch_shapes=(), sem=None, hosted=None):
    in_specs, out_specs, out_shape = list(in_specs), list(out_specs), list(out_shape)
    scratch_shapes = list(scratch_shapes)
    if hosted is None:
        out = pl.pallas_call(
            body, name=name, grid=grid, in_specs=in_specs, out_specs=out_specs, out_shape=out_shape,
            scratch_shapes=scratch_shapes, compiler_params=_params(sem))(*args)
        return list(out), []
    n_in, n_out, n_scr = len(in_specs), len(out_shape), len(scratch_shapes)
    h_in, h_out = len(hosted.inputs), len(hosted.out_shapes)
    total = int(np.prod(grid)) if grid else 1

    def wrapped(*refs):
        ins, refs = refs[:n_in], refs[n_in:]
        h_ins, refs = refs[:h_in], refs[h_in:]
        outs, refs = refs[:n_out], refs[n_out:]
        h_outs, refs = refs[:h_out], refs[h_out:]
        scr, h_sems = refs[:n_scr], refs[n_scr:]
        step = 0
        for axis, size in enumerate(grid):
            step = step * size + pl.program_id(axis)

        @pl.when(step == 0)
        def _():
            hosted.first(h_ins, h_outs, h_sems)

        body(*ins, *outs, *scr)
        if hosted.middle is not None:
            @pl.when(step == total // 2)
            def _():
                hosted.middle(h_ins, h_outs, h_sems)

        @pl.when(step == total - 1)
        def _():
            hosted.last(h_ins, h_outs, h_sems)

    out = pl.pallas_call(
        wrapped, name=name, grid=grid, in_specs=in_specs + [ANY] * h_in, out_specs=out_specs + [ANY] * h_out,
        out_shape=out_shape + hosted.out_shapes, scratch_shapes=scratch_shapes + hosted.sems,
        compiler_params=_params(("arbitrary",) * len(grid)))(*args, *hosted.inputs)
    return list(out[:n_out]), list(out[n_out:])


def _matmul(name, a, b, *, dims, grid, a_spec, b_spec, o_spec, out_shape, out_dtype,
            parts=1, resid=None, resid_spec=None, also_bf16=False, hosted=None, zero_axis=None):
    def body(*refs):
        if zero_axis is None:
            product(*refs)
        else:
            @pl.when(pl.program_id(zero_axis) == 0)
            def _():
                refs[2][...] = jnp.zeros_like(refs[2])

            @pl.when(pl.program_id(zero_axis) > 0)
            def _():
                product(*refs)

    def product(*refs):
        a_ref, b_ref = refs[:2]
        r_ref = refs[2] if resid is not None else None
        o_ref = refs[3] if resid is not None else refs[2]
        if parts == 1:
            prod = lax.dot_general(a_ref[...].astype(BF16), b_ref[...].astype(BF16), dims,
                                   preferred_element_type=F32)
        else:
            prod = None
            for part in range(parts):
                term = lax.dot_general(a_ref[part].astype(BF16), b_ref[part].astype(BF16), dims,
                                       preferred_element_type=F32)
                prod = term if prod is None else prod + term
        if resid is not None:
            prod = r_ref[...] + prod
        o_ref[...] = prod.astype(out_dtype)
        if also_bf16:
            refs[-1][...] = prod.astype(BF16)

    in_specs = [a_spec, b_spec]
    args = [a, b]
    if resid is not None:
        in_specs.append(resid_spec)
        args.append(resid)
    sem = ["parallel"] * len(grid)
    out_specs = [o_spec]
    out_shapes = [jax.ShapeDtypeStruct(out_shape, out_dtype)]
    if also_bf16:
        out_specs.append(o_spec)
        out_shapes.append(jax.ShapeDtypeStruct(out_shape, BF16))
    out, extra = _call(body, name=name, grid=grid, in_specs=in_specs, out_specs=out_specs, out_shape=out_shapes,
                       args=args, sem=tuple(sem), hosted=hosted)
    res = out[0] if not also_bf16 else tuple(out)
    return res if hosted is None else (res, extra)


def _rms_rows(x):
    return lax.rsqrt(jnp.mean(x * x, axis=-1, keepdims=True) + RMS_EPS)


def _norm_fwd(name, x, gains):
    s, d = x.shape
    n = gains.shape[0]

    def body(x_ref, g_ref, *o_refs):
        xv = x_ref[...]
        xh = xv * _rms_rows(xv)
        for i in range(n):
            o_refs[i][...] = (xh * g_ref[i:i + 1, :]).astype(BF16)

    row = pl.BlockSpec((TM, d), lambda i: (i, 0))
    return pl.pallas_call(
        body, name=name, grid=(s // TM,),
        in_specs=[row, pl.BlockSpec((n, d), lambda i: (0, 0))],
        out_specs=[row] * n,
        out_shape=[jax.ShapeDtypeStruct((s, d), BF16)] * n,
        compiler_params=_params(("parallel",)),
    )(x, gains)


def _proj_norm_bwd(name, x, dres, gains, branches):
    s, d = x.shape
    n = len(branches)
    tm = min(TM_PARTS, s)

    def body(x_ref, r_ref, g_ref, *refs):
        ab_refs, dx_ref, dg_ref = refs[:2 * n], refs[2 * n], refs[2 * n + 1]
        i = pl.program_id(0)
        xv = x_ref[...]
        r = _rms_rows(xv)
        xh = xv * r

        @pl.when(i == 0)
        def _():
            dg_ref[...] = jnp.zeros_like(dg_ref)

        a = None
        for j in range(n):
            a_ref, b_ref = ab_refs[2 * j], ab_refs[2 * j + 1]
            dn = None
            for part in range(a_ref.shape[0]):
                term = lax.dot_general(a_ref[part], b_ref[part], NT, preferred_element_type=F32)
                dn = term if dn is None else dn + term
            t = dn * g_ref[j:j + 1, :]
            a = t if a is None else a + t
            dg_ref[j:j + 1, :] += jnp.sum(dn * xh, axis=0, keepdims=True)
        dx_ref[...] = r_ref[...] + r * (a - xh * jnp.mean(xh * a, axis=-1, keepdims=True))

    row = pl.BlockSpec((tm, d), lambda i: (i, 0))
    small = pl.BlockSpec((n, d), lambda i: (0, 0))
    ab_specs, ab_args = [], []
    for a, b in branches:
        ab_specs += [pl.BlockSpec((a.shape[0], tm, a.shape[2]), lambda i: (0, i, 0)),
                     pl.BlockSpec(b.shape, lambda i: (0, 0, 0))]
        ab_args += [a, b]
    return pl.pallas_call(
        body, name=name, grid=(s // tm,),
        in_specs=[row, row, small] + ab_specs,
        out_specs=[row, small],
        out_shape=[jax.ShapeDtypeStruct((s, d), F32), jax.ShapeDtypeStruct((n, d), F32)],
        compiler_params=_params(("arbitrary",)),
    )(x, dres, gains, *ab_args)


def _loss_head(h2, target, gain):
    s, d = h2.shape

    def body(h_ref, t_ref, g_ref, dh_ref, loss_ref, dg_ref):
        i = pl.program_id(0)
        hv = h_ref[...]
        r = _rms_rows(hv)
        hh = hv * r
        g = g_ref[...]
        err = hh * g - t_ref[...]
        part = 0.5 * jnp.sum(jnp.sum(err * err, axis=-1, keepdims=True) * (1.0 / d), axis=0, keepdims=True)
        dy = err * (1.0 / d)
        a = dy * g
        dh_ref[...] = r * (a - hh * jnp.mean(hh * a, axis=-1, keepdims=True))
        dg = jnp.sum(dy * hh, axis=0, keepdims=True)

        @pl.when(i == 0)
        def _():
            loss_ref[...] = part
            dg_ref[...] = dg

        @pl.when(i > 0)
        def _():
            loss_ref[...] += part
            dg_ref[...] += dg

    row = pl.BlockSpec((TM, d), lambda i: (i, 0))
    return pl.pallas_call(
        body, name="loss_head", grid=(s // TM,),
        in_specs=[row, row, pl.BlockSpec((1, d), lambda i: (0, 0))],
        out_specs=[row, pl.BlockSpec((1, 1), lambda i: (0, 0)), pl.BlockSpec((1, d), lambda i: (0, 0))],
        out_shape=[jax.ShapeDtypeStruct((s, d), F32), jax.ShapeDtypeStruct((1, 1), F32),
                   jax.ShapeDtypeStruct((1, d), F32)],
        compiler_params=_params(("arbitrary",)),
    )(h2, target, gain)


def _silu_parts(g):
    sig = jax.nn.sigmoid(g)
    return g * sig, sig * (1.0 + g * (1.0 - sig))


def _lane_lo(rows):
    return lax.broadcasted_iota(jnp.int32, (rows, LANES), 1) < HEAD_DIM


def _stack_pair(x):
    lo = _lane_lo(x.shape[0])
    zero = jnp.zeros_like(x)
    return jnp.concatenate([jnp.where(lo, x, zero), jnp.where(lo, zero, x)], axis=0)


def _unstack_pair(y, w):
    return jnp.where(_lane_lo(w), y[:w], y[w:])


def _block_valid(b, left_blocks, width):
    col = lax.broadcasted_iota(jnp.int32, (1, 2 * width), 1)
    col = jnp.where(col >= width, col - width, col)
    return (col // KB + (b - left_blocks)) >= 0


def _toeplitz_tile(diag_row, width, left_chunks):
    wide = width + TQ
    rolled = pltpu.roll(jnp.broadcast_to(diag_row, (TQ, wide)), 1, 1, stride=1, stride_axis=0)
    i = lax.broadcasted_iota(jnp.int32, (TQ, width), 0) // CHUNK
    j = lax.broadcasted_iota(jnp.int32, (TQ, width), 1) // CHUNK
    dc = i + left_chunks - j
    return jnp.where((dc >= 0) & (dc <= left_chunks), rolled[:, TQ:], MASKED)


def _toeplitz_sum(tile, width):
    flip = (lax.broadcasted_iota(jnp.int32, (TQ, TQ), 0) + lax.broadcasted_iota(jnp.int32, (TQ, TQ), 1)
            == TQ - 1).astype(F32)
    reversed_rows = jnp.dot(flip, tile, precision=lax.Precision.HIGHEST, preferred_element_type=F32)
    padded = jnp.concatenate([reversed_rows, jnp.zeros((TQ, TQ), F32)], axis=1)
    rolled = pltpu.roll(padded, 0, 1, stride=1, stride_axis=0)
    return jnp.sum(rolled, axis=0, keepdims=True)


def _softmax_pair(sc, w, sink=None):
    ps, inv, lses = [], [], []
    for e in range(2):
        sh = sc[:, e * w:(e + 1) * w]
        m = jnp.max(sh, axis=-1, keepdims=True)
        if sink is not None:
            m = jnp.maximum(m, sink[e])
        ex = jnp.exp(sh - m)
        l = jnp.sum(ex, axis=-1, keepdims=True)
        if sink is not None:
            l = l + jnp.exp(sink[e] - m)
        ps.append(ex.astype(BF16))
        inv.append(1.0 / l)
        lses.append(m + jnp.log(l))
    return jnp.concatenate(ps, axis=-1), inv, lses


def _softmax_pair_bwd(sc, dp, lse, delta, w):
    ps, dss = [], []
    for e in range(2):
        p = jnp.exp(sc[:, e * w:(e + 1) * w] - lse[e])
        ps.append(p)
        dss.append(p * (dp[:, e * w:(e + 1) * w] - delta[e]))
    return jnp.concatenate(ps, axis=-1), jnp.concatenate(dss, axis=-1)


def _pair_rowsums(x, lo):
    zero = jnp.zeros_like(x)
    return (jnp.sum(jnp.where(lo, x, zero), axis=-1, keepdims=True),
            jnp.sum(jnp.where(lo, zero, x), axis=-1, keepdims=True))


def _a_qkv_specs(rows, pad, pw):
    return [pl.BlockSpec((None, TQ, pw), lambda p, b: (0, b + pad // TQ, p)),
            pl.BlockSpec((None, rows, pw), lambda p, b: (1, 0, p)),
            pl.BlockSpec((None, rows, pw), lambda p, b: (2, 0, p))]


def _window(ref, b, pad, win, lanes):
    start = pl.multiple_of(b * TQ + pad - (win - TQ), KB)
    return ref[pl.ds(start, win), lanes]


def _attn_a_fwd(zqkv, g, diag, hosted=None):
    s = g.shape[0]
    pad = zqkv.shape[1] - s
    nb = s // TQ
    left = A_KBLOCKS - 1
    pairs = A_PAIRS_FWD
    pw = pairs * LANES
    wide = A_WIN + TQ

    def body(q_ref, k_ref, v_ref, g_ref, diag_ref, o_ref, u_ref, lse_ref, bias_scr):
        b = pl.program_id(1)

        @pl.when(b == 0)
        def _():
            for hh in range(2 * pairs):
                bias_scr[hh // 2, :, (hh % 2) * A_WIN:(hh % 2 + 1) * A_WIN] = _toeplitz_tile(
                    diag_ref[hh], A_WIN, A_LEFT_CHUNKS)

        def step(first_blocks):
            lo = _lane_lo(TQ)
            for pp in range(pairs):
                ln = slice(pp * LANES, (pp + 1) * LANES)
                kcat = _stack_pair(_window(k_ref, b, pad, A_WIN, ln))
                vcat = _stack_pair(_window(v_ref, b, pad, A_WIN, ln))
                sc = lax.dot_general(q_ref[:, ln] * SCALE, kcat, NT, preferred_element_type=F32) + bias_scr[pp]
                if first_blocks:
                    sc = jnp.where(_block_valid(b, left, A_WIN), sc, MASKED)
                p, inv, lses = _softmax_pair(sc, A_WIN)
                ov = jnp.dot(p, vcat, preferred_element_type=F32) * jnp.where(lo, inv[0], inv[1])
                o_ref[:, ln] = ov
                lse_ref[pp] = jnp.where(lo, lses[0], lses[1])
                sg, _ = _silu_parts(g_ref[:, ln])
                u_ref[:, ln] = (ov * sg).astype(BF16)

        @pl.when(b < left)
        def _():
            step(True)

        @pl.when(b >= left)
        def _():
            step(False)

    tile = pl.BlockSpec((TQ, pw), lambda p, b: (b, p))
    return _call(
        body, name="attn_a_fwd", grid=(HEADS // 2 // pairs, nb),
        in_specs=_a_qkv_specs(pad + s, pad, pw) + [
            tile, pl.BlockSpec((2 * pairs, 1, wide), lambda p, b: (p, 0, 0))],
        out_specs=[tile, tile, pl.BlockSpec((pairs, TQ, LANES), lambda p, b: (p, b, 0))],
        out_shape=[jax.ShapeDtypeStruct((s, D_MODEL), F32), jax.ShapeDtypeStruct((s, D_MODEL), BF16),
                   jax.ShapeDtypeStruct((HEADS // 2, s, LANES), F32)],
        scratch_shapes=[pltpu.VMEM((pairs, TQ, 2 * A_WIN), F32)],
        sem=("parallel", "arbitrary"), hosted=hosted,
        args=(zqkv, zqkv, zqkv, g, diag))


def _attn_a_bwd(zqkv, g, o, du, lse, diag, hosted=None):
    s = g.shape[0]
    pad = zqkv.shape[1] - s
    nb = s // TQ
    left = A_KBLOCKS - 1
    pw = A_PAIRS * LANES
    wide = A_WIN + TQ

    def body(q_ref, k_ref, v_ref, g_ref, o_ref, du_ref, lse_ref, diag_ref, dz_ref, ddiag_ref,
             bias_scr, dbias_acc, dk_acc, dv_acc):
        b = pl.program_id(1)

        @pl.when(b == 0)
        def _():
            for hh in range(2 * A_PAIRS):
                bias_scr[hh // 2, :, (hh % 2) * A_WIN:(hh % 2 + 1) * A_WIN] = _toeplitz_tile(
                    diag_ref[hh], A_WIN, A_LEFT_CHUNKS)
            dbias_acc[...] = jnp.zeros_like(dbias_acc)
            dk_acc[...] = jnp.zeros_like(dk_acc)
            dv_acc[...] = jnp.zeros_like(dv_acc)

        def step(first_blocks):
            lo = _lane_lo(TQ)
            upper = lax.broadcasted_iota(jnp.int32, (LANES, A_WIN), 0) < HEAD_DIM
            rows = pl.ds(pl.multiple_of(b * TQ, TQ), TQ)
            sg, dsg = _silu_parts(g_ref[...])
            duv = du_ref[...]
            ov = o_ref[...]
            do = duv * sg
            dz_ref[3, rows, :] = (duv * ov * dsg).astype(BF16)
            do_o = do * ov
            do_bf = do.astype(BF16)
            for pp in range(A_PAIRS):
                ln = slice(pp * LANES, (pp + 1) * LANES)
                q = q_ref[:, ln] * SCALE
                kcat = _stack_pair(_window(k_ref, b, pad, A_WIN, ln))
                vcat = _stack_pair(_window(v_ref, b, pad, A_WIN, ln))
                sc = lax.dot_general(q, kcat, NT, preferred_element_type=F32) + bias_scr[pp]
                if first_blocks:
                    sc = jnp.where(_block_valid(b, left, A_WIN), sc, MASKED)
                lse_t = lse_ref[pp]
                dp = lax.dot_general(do_bf[:, ln], vcat, NT, preferred_element_type=F32)
                p, ds = _softmax_pair_bwd(sc, dp, (lse_t[:, 0:1], lse_t[:, HEAD_DIM:HEAD_DIM + 1]),
                                          _pair_rowsums(do_o[:, ln], lo), A_WIN)
                dbias_acc[pp] += ds
                dsb = ds.astype(BF16)
                dz_ref[0, rows, ln] = (jnp.dot(dsb, kcat, preferred_element_type=F32) * SCALE).astype(BF16)
                dkt = lax.dot_general(q, dsb, TN, preferred_element_type=F32)
                dvt = lax.dot_general(do_bf[:, ln], p.astype(BF16), TN, preferred_element_type=F32)
                dkt = jnp.where(upper, dkt[:, :A_WIN], dkt[:, A_WIN:])
                dvt = jnp.where(upper, dvt[:, :A_WIN], dvt[:, A_WIN:])
                for t in range(A_KBLOCKS):
                    blk = b + (pad // KB - left + t)
                    dk_acc[blk, ln, :] += dkt[:, t * KB:(t + 1) * KB]
                    dv_acc[blk, ln, :] += dvt[:, t * KB:(t + 1) * KB]

        @pl.when(b < left)
        def _():
            step(True)

        @pl.when(b >= left)
        def _():
            step(False)

        @pl.when(b == nb - 1)
        def _():
            for kb in range(s // KB):
                dz_ref[1, kb * KB:(kb + 1) * KB, :] = dk_acc[pad // KB + kb].T.astype(BF16)
                dz_ref[2, kb * KB:(kb + 1) * KB, :] = dv_acc[pad // KB + kb].T.astype(BF16)
            for hh in range(2 * A_PAIRS):
                ddiag_ref[hh] = _toeplitz_sum(
                    dbias_acc[hh // 2, :, (hh % 2) * A_WIN:(hh % 2 + 1) * A_WIN], A_WIN)

    tile = pl.BlockSpec((TQ, pw), lambda p, b: (b, p))
    diag_spec = pl.BlockSpec((2 * A_PAIRS, 1, wide), lambda p, b: (p, 0, 0))
    return _call(
        body, name="attn_a_bwd", grid=(HEADS // 2 // A_PAIRS, nb),
        in_specs=_a_qkv_specs(pad + s, pad, pw) + [
            tile, tile, tile, pl.BlockSpec((A_PAIRS, TQ, LANES), lambda p, b: (p, b, 0)), diag_spec],
        out_specs=[pl.BlockSpec((4, s, pw), lambda p, b: (0, 0, p)), diag_spec],
        out_shape=[jax.ShapeDtypeStruct((4, s, D_MODEL), BF16),
                   jax.ShapeDtypeStruct((HEADS, 1, wide), F32)],
        scratch_shapes=[pltpu.VMEM((A_PAIRS, TQ, 2 * A_WIN), F32), pltpu.VMEM((A_PAIRS, TQ, 2 * A_WIN), F32),
                        pltpu.VMEM(((pad + s) // KB, pw, KB), F32), pltpu.VMEM(((pad + s) // KB, pw, KB), F32)],
        sem=("parallel", "arbitrary"), hosted=hosted,
        args=(zqkv, zqkv, zqkv, g, o, du, lse, diag))


B_STACK = B_GROUP // 2
B_KVX = 4 * LANES
B_ROWS = B_STACK * TQ
B_WIDE = B_WIN + TQ


def _b_head_place(h):
    return h // B_GROUP, (h % B_GROUP) // 2, h % 2


def _toeplitz_tile_t(base_row, width, left_chunks):
    wide = width + TQ
    rolled = pltpu.roll(jnp.broadcast_to(base_row, (width, wide)), 0, 1, stride=1, stride_axis=0)
    j = lax.broadcasted_iota(jnp.int32, (width, TQ), 0) // CHUNK
    i = lax.broadcasted_iota(jnp.int32, (width, TQ), 1) // CHUNK
    dc = i + left_chunks - j
    return jnp.where((dc >= 0) & (dc <= left_chunks), rolled[:, :TQ], MASKED)


def _toeplitz_sum_t(tile_t, width):
    flip = (lax.broadcasted_iota(jnp.int32, (width, width), 0) + lax.broadcasted_iota(jnp.int32, (width, width), 1)
            == width - 1).astype(F32)
    reversed_rows = jnp.dot(flip, tile_t, precision=lax.Precision.HIGHEST, preferred_element_type=F32)
    padded = jnp.concatenate([reversed_rows, jnp.zeros((width, width), F32)], axis=1)
    rolled = pltpu.roll(padded, 0, 1, stride=1, stride_axis=0)
    return jnp.sum(rolled, axis=0, keepdims=True)


def _b_build_bias(base_ref, bias_scr):
    for h in range(HEADS):
        gi, pr, e = _b_head_place(h)
        bias_scr[gi, e * B_WIN:(e + 1) * B_WIN, pr * TQ:(pr + 1) * TQ] = _toeplitz_tile_t(
            base_ref[h], B_WIN, B_LEFT_CHUNKS)


def _b_stack(x, gi):
    return jnp.concatenate(
        [x[:, (B_STACK * gi + pr) * LANES:(B_STACK * gi + pr + 1) * LANES] for pr in range(B_STACK)], axis=0)


def _b_sink_rows(sink_ref, gi):
    block = lax.broadcasted_iota(jnp.int32, (1, B_ROWS), 1) // TQ
    rows = []
    for e in range(2):
        row = jnp.zeros((1, B_ROWS), F32)
        for pr in range(B_STACK):
            h = B_GROUP * gi + 2 * pr + e
            row = jnp.where(block == pr, sink_ref[0:1, h:h + 1], row)
        rows.append(row)
    return rows


def _b_scores_t(q_ref, kvv, bias_scr, gi, b, left, first_blocks):
    kcat = _stack_pair(kvv[:, gi * LANES:(gi + 1) * LANES])
    vcat = _stack_pair(kvv[:, (B_KV_HEADS + gi) * LANES:(B_KV_HEADS + gi + 1) * LANES])
    qs = _b_stack(q_ref, gi) * SCALE
    sc = lax.dot_general(kcat, qs, NT, preferred_element_type=F32) + bias_scr[gi]
    if first_blocks:
        row = lax.broadcasted_iota(jnp.int32, (2 * B_WIN, 1), 0)
        row = jnp.where(row >= B_WIN, row - B_WIN, row)
        sc = jnp.where((row // KB + (b - left)) >= 0, sc, MASKED)
    return kcat, vcat, qs, sc


def _attn_b_fwd(qb, kvx, gate, base, sinks):
    s = qb.shape[0]
    pad = kvx.shape[0] - s
    nb = s // TQ
    left = B_KBLOCKS - 1

    def body(q_ref, kv_ref, g_ref, base_ref, sink_ref, o_ref, u_ref, lse_ref, bias_scr):
        b = pl.program_id(0)

        @pl.when(b == 0)
        def _():
            _b_build_bias(base_ref, bias_scr)

        def step(first_blocks):
            kvv = _window(kv_ref, b, pad, B_WIN, slice(None))
            upper = lax.broadcasted_iota(jnp.int32, (LANES, B_ROWS), 0) < HEAD_DIM
            lse_rows = []
            for gi in range(B_KV_HEADS):
                kcat, vcat, qs, sc = _b_scores_t(q_ref, kvv, bias_scr, gi, b, left, first_blocks)
                sink = _b_sink_rows(sink_ref, gi)
                ps, inv = [], []
                for e in range(2):
                    sh = sc[e * B_WIN:(e + 1) * B_WIN]
                    m = jnp.maximum(jnp.max(sh, axis=0, keepdims=True), sink[e])
                    ex = jnp.exp(sh - m)
                    l = jnp.sum(ex, axis=0, keepdims=True) + jnp.exp(sink[e] - m)
                    ps.append(ex.astype(BF16))
                    inv.append(1.0 / l)
                    lse_rows.append(m + jnp.log(l))
                pt = jnp.concatenate(ps, axis=0)
                ot = lax.dot_general(vcat, pt, TN, preferred_element_type=F32) * jnp.where(upper, inv[0], inv[1])
                ov = ot.T
                for pr in range(B_STACK):
                    pair = B_STACK * gi + pr
                    o_ref[:, pair * LANES:(pair + 1) * LANES] = ov[pr * TQ:(pr + 1) * TQ]
            lse_ref[0] = jnp.concatenate(lse_rows + [jnp.zeros((8 - len(lse_rows), B_ROWS), F32)], axis=0)
            sg, _ = _silu_parts(g_ref[...])
            u_ref[...] = (o_ref[...] * sg).astype(BF16)

        @pl.when(b < left)
        def _():
            step(True)

        @pl.when(b >= left)
        def _():
            step(False)

    row = pl.BlockSpec((TQ, D_MODEL), lambda b: (b, 0))
    return pl.pallas_call(
        body, name="attn_b_fwd", grid=(nb,),
        in_specs=[row, pl.BlockSpec((pad + s, B_KVX), lambda b: (0, 0)), row,
                  pl.BlockSpec((HEADS, 1, B_WIDE), lambda b: (0, 0, 0)), pl.BlockSpec((1, HEADS), lambda b: (0, 0))],
        out_specs=[row, row, pl.BlockSpec((1, 8, B_ROWS), lambda b: (b, 0, 0))],
        out_shape=[jax.ShapeDtypeStruct((s, D_MODEL), F32), jax.ShapeDtypeStruct((s, D_MODEL), BF16),
                   jax.ShapeDtypeStruct((nb, 8, B_ROWS), F32)],
        scratch_shapes=[pltpu.VMEM((B_KV_HEADS, 2 * B_WIN, B_ROWS), F32)],
        compiler_params=_params(("arbitrary",)),
    )(qb, kvx, gate, base, sinks)


def _attn_b_bwd(qb, kvx, gate, o, du, lse, base, sinks):
    s = qb.shape[0]
    pad = kvx.shape[0] - s
    nb = s // TQ
    left = B_KBLOCKS - 1
    half = D_MODEL // 2

    def body(q_ref, kv_ref, g_ref, o_ref, du_ref, lse_ref, base_ref, sink_ref, dz_ref, dkv_ref, dsum_ref,
             dsink_ref, bias_scr, dbias_acc, dkv_acc, dsink_acc):
        b = pl.program_id(0)

        @pl.when(b == 0)
        def _():
            _b_build_bias(base_ref, bias_scr)
            dbias_acc[...] = jnp.zeros_like(dbias_acc)
            dkv_acc[...] = jnp.zeros_like(dkv_acc)
            dsink_acc[...] = jnp.zeros_like(dsink_acc)

        def step(first_blocks):
            kvv = _window(kv_ref, b, pad, B_WIN, slice(None))
            sg, dsg = _silu_parts(g_ref[...])
            duv = du_ref[...]
            ov = o_ref[...]
            do = duv * sg
            dgate = (duv * ov * dsg).astype(BF16)
            dz_ref[2] = dgate[:, :half]
            dz_ref[3] = dgate[:, half:]
            do_o = do * ov
            do_bf = do.astype(BF16)
            lse_all = lse_ref[0]
            dsink_rows = []
            for gi in range(B_KV_HEADS):
                kcat, vcat, qs, sc = _b_scores_t(q_ref, kvv, bias_scr, gi, b, left, first_blocks)
                dos = _b_stack(do_bf, gi)
                doo_t = _b_stack(do_o, gi).T
                delta = (jnp.sum(doo_t[:HEAD_DIM], axis=0, keepdims=True),
                         jnp.sum(doo_t[HEAD_DIM:], axis=0, keepdims=True))
                sink = _b_sink_rows(sink_ref, gi)
                dp = lax.dot_general(vcat, dos, NT, preferred_element_type=F32)
                ps, dss = [], []
                for e in range(2):
                    lse_e = lse_all[2 * gi + e:2 * gi + e + 1]
                    delta_e = delta[e]
                    p = jnp.exp(sc[e * B_WIN:(e + 1) * B_WIN] - lse_e)
                    ps.append(p.astype(BF16))
                    dss.append(p * (dp[e * B_WIN:(e + 1) * B_WIN] - delta_e))
                    dsink_rows.append(-jnp.exp(sink[e] - lse_e) * delta_e)
                ds = jnp.concatenate(dss, axis=0)
                dbias_acc[gi] += ds
                dsb = ds.astype(BF16)
                dq = (lax.dot_general(kcat, dsb, TN, preferred_element_type=F32) * SCALE).T.astype(BF16)
                for pr in range(B_STACK):
                    dz_ref[gi, :, pr * LANES:(pr + 1) * LANES] = dq[pr * TQ:(pr + 1) * TQ]
                dk = _unstack_pair(jnp.dot(dsb, qs, preferred_element_type=F32), B_WIN)
                dv = _unstack_pair(jnp.dot(jnp.concatenate(ps, axis=0), dos, preferred_element_type=F32), B_WIN)
                krows = pl.ds(pl.multiple_of(b * TQ + pad - (B_WIN - TQ), KB), B_WIN)
                dkv_acc[krows, gi * LANES:(gi + 1) * LANES] += dk
                dkv_acc[krows, (B_KV_HEADS + gi) * LANES:(B_KV_HEADS + gi + 1) * LANES] += dv
            dsink_acc[...] += jnp.concatenate(
                dsink_rows + [jnp.zeros((8 - len(dsink_rows), B_ROWS), F32)], axis=0)

        @pl.when(b < left)
        def _():
            step(True)

        @pl.when(b >= left)
        def _():
            step(False)

        @pl.when(b == nb - 1)
        def _():
            lo_s = _lane_lo(s)
            for which in range(2):
                folded = []
                for gi in range(B_KV_HEADS):
                    part = dkv_acc[pad:pad + s, (which * B_KV_HEADS + gi) * LANES:(which * B_KV_HEADS + gi + 1) * LANES]
                    folded.append(part + pltpu.roll(part, HEAD_DIM, 1))
                dkv_ref[:, which * LANES:(which + 1) * LANES] = jnp.where(lo_s, folded[0], folded[1]).astype(BF16)
            lane8 = lax.broadcasted_iota(jnp.int32, dsink_ref.shape, 1)
            tot = jnp.zeros(dsink_ref.shape, F32)
            for h in range(HEADS):
                gi, pr, e = _b_head_place(h)
                dsum_ref[h] = _toeplitz_sum_t(
                    dbias_acc[gi, e * B_WIN:(e + 1) * B_WIN, pr * TQ:(pr + 1) * TQ], B_WIN)
                per_query = dsink_acc[2 * gi + e:2 * gi + e + 1, pr * TQ:(pr + 1) * TQ]
                tot = jnp.where(lane8 == h, jnp.sum(per_query, axis=1, keepdims=True), tot)
            dsink_ref[...] = tot

    row = pl.BlockSpec((TQ, D_MODEL), lambda b: (b, 0))
    base_spec = pl.BlockSpec((HEADS, 1, B_WIDE), lambda b: (0, 0, 0))
    return pl.pallas_call(
        body, name="attn_b_bwd", grid=(nb,),
        in_specs=[row, pl.BlockSpec((pad + s, B_KVX), lambda b: (0, 0)), row, row, row,
                  pl.BlockSpec((1, 8, B_ROWS), lambda b: (b, 0, 0)), base_spec,
                  pl.BlockSpec((1, HEADS), lambda b: (0, 0))],
        out_specs=[pl.BlockSpec((4, TQ, half), lambda b: (0, b, 0)),
                   pl.BlockSpec((s, 2 * LANES), lambda b: (0, 0)), base_spec,
                   pl.BlockSpec((8, LANES), lambda b: (0, 0))],
        out_shape=[jax.ShapeDtypeStruct((4, s, half), BF16), jax.ShapeDtypeStruct((s, 2 * LANES), BF16),
                   jax.ShapeDtypeStruct((HEADS, 1, B_WIDE), F32), jax.ShapeDtypeStruct((8, LANES), F32)],
        scratch_shapes=[pltpu.VMEM((B_KV_HEADS, 2 * B_WIN, B_ROWS), F32),
                        pltpu.VMEM((B_KV_HEADS, 2 * B_WIN, B_ROWS), F32),
                        pltpu.VMEM((pad + s, B_KVX), F32), pltpu.VMEM((8, B_ROWS), F32)],
        compiler_params=_params(("arbitrary",)),
    )(qb, kvx, gate, o, du, lse, base, sinks)


def _t5_bucket(rel):
    nb = T5_BUCKETS // 2
    max_exact = nb // 2
    ret = jnp.where(rel > 0, nb, 0)
    n = jnp.abs(rel)
    nf = jnp.maximum(n, 1).astype(jnp.float32)
    large = max_exact + (jnp.log(nf / max_exact) / math.log(T5_MAX_DIST / max_exact)
                         * (nb - max_exact)).astype(jnp.int32)
    large = jnp.minimum(large, nb - 1)
    return ret + jnp.where(n < max_exact, n, large)


def _a_offset_onehot():
    c = np.arange(A_WIN + TQ)
    dist = A_LEFT_CHUNKS * CHUNK + TQ - 1 - c
    idx = np.clip(dist, -A_REL_CLIP, A_REL_CLIP) + A_REL_CLIP
    onehot = np.zeros((A_WIN + TQ, 2 * A_REL_CLIP + 1), np.float32)
    onehot[c, idx] = 1.0
    return jnp.asarray(onehot)


def _b_offset_onehot():
    c = jnp.arange(B_WIN + TQ, dtype=jnp.int32)
    rel = c - (TQ - 1) - B_LEFT_CHUNKS * CHUNK
    return (_t5_bucket(rel)[:, None] == jnp.arange(T5_BUCKETS)[None, :]).astype(F32)


def _diag_rows(onehot, table):
    rows = jnp.dot(onehot, table.astype(F32), precision=lax.Precision.HIGHEST)
    return rows.T.reshape(HEADS, 1, onehot.shape[0])


def _diag_rows_grad(onehot, ddiag):
    return jnp.dot(ddiag.reshape(HEADS, onehot.shape[0]), onehot, precision=lax.Precision.HIGHEST).T


def _position():
    x, y, c = lax.axis_index("x"), lax.axis_index("y"), lax.axis_index("c")
    chips = [(1 - x, y), (x, 1 - y), (1 - x, 1 - y)]
    return x, y, c, chips


ANY = pl.BlockSpec(memory_space=pl.ANY)


def _allgather_hosted(shards, split):
    n = len(shards)

    def part(ref, t, half):
        if not split[t]:
            return ref
        rows = shards[t].shape[0] // 2
        return ref.at[pl.ds(half * rows, rows)]

    def copies(kind, ins, outs, sems):
        send_sems, recv_sems, pass_send, pass_recv, local_sems = sems
        x, y, c, chips = _position()
        mine = 2 * x + y
        if kind == "local":
            return [pltpu.make_async_copy(ins[t], outs[t].at[mine], local_sems.at[t]) for t in range(n)]
        made = []
        for t in range(n):
            for j, chip in enumerate(chips):
                theirs = 2 * chip[0] + chip[1]
                far = dict(send_sem=send_sems.at[3 * t + j], recv_sem=recv_sems.at[3 * t + j],
                           device_id=(chip[0], chip[1], c), device_id_type=MESH)
                near = dict(send_sem=pass_send.at[3 * t + j], recv_sem=pass_recv.at[3 * t + j],
                            device_id=(x, y, 1 - c), device_id_type=MESH)
                here = part(outs[t].at[theirs], t, c)
                if kind == "send":
                    made.append(pltpu.make_async_remote_copy(
                        src_ref=part(ins[t], t, c), dst_ref=part(outs[t].at[mine], t, c), **far))
                elif kind == "landed":
                    made.append(pltpu.make_async_remote_copy(src_ref=here, dst_ref=here, **far))
                elif not split[t]:
                    made.append(None)
                elif kind == "pass":
                    made.append(pltpu.make_async_remote_copy(src_ref=here, dst_ref=here, **near))
                else:
                    other = part(outs[t].at[theirs], t, 1 - c)
                    made.append(pltpu.make_async_remote_copy(src_ref=other, dst_ref=other, **near))
        return made

    def first(ins, outs, sems):
        for cp in copies("local", ins, outs, sems) + copies("send", ins, outs, sems):
            cp.start()

    def middle(ins, outs, sems):
        for got, cp in zip(copies("landed", ins, outs, sems), copies("pass", ins, outs, sems)):
            got.wait_recv()
            if cp is not None:
                cp.start()

    def last(ins, outs, sems):
        for cp in copies("passed", ins, outs, sems):
            if cp is not None:
                cp.wait_recv()
        for cp in copies("send", ins, outs, sems) + copies("pass", ins, outs, sems):
            if cp is not None:
                cp.wait_send()
        for cp in copies("local", ins, outs, sems):
            cp.wait()

    return _Hosted(shards, [jax.ShapeDtypeStruct((4,) + w.shape, w.dtype) for w in shards],
                   [pltpu.SemaphoreType.DMA((3 * n,))] * 4 + [pltpu.SemaphoreType.DMA((n,))],
                   first, middle, last)


def _scatter_hosted(grads):
    n = len(grads)

    def copies(ins, outs, sems):
        send_sems, recv_sems = sems
        x, y, c, chips = _position()
        return [pltpu.make_async_remote_copy(
            src_ref=ins[t].at[2 * chip[0] + chip[1]], dst_ref=outs[t].at[j],
            send_sem=send_sems.at[3 * t + j], recv_sem=recv_sems.at[3 * t + j],
            device_id=(chip[0], chip[1], c), device_id_type=MESH)
            for t in range(n) for j, chip in enumerate(chips)]

    def first(ins, outs, sems):
        for cp in copies(ins, outs, sems):
            cp.start()

    def last(ins, outs, sems):
        for cp in copies(ins, outs, sems):
            cp.wait()

    return _Hosted(grads, [jax.ShapeDtypeStruct((3,) + g.shape[1:], g.dtype) for g in grads],
                   [pltpu.SemaphoreType.DMA((3 * n,))] * 2, first, None, last)


def _scatter_on_sequencer(name, grad):
    src = jax.new_ref(grad, memory_space=pltpu.MemorySpace.HBM)
    dst = jax.empty_ref(jax.ShapeDtypeStruct((3,) + grad.shape[1:], grad.dtype),
                        memory_space=pltpu.MemorySpace.HBM)

    @pl.kernel(mesh=plsc.ScalarSubcoreMesh(axis_name="sequencer", num_cores=1), name=name,
               scratch_types=(pltpu.SemaphoreType.DMA((3,)), pltpu.SemaphoreType.DMA((3,))),
               compiler_params=pltpu.CompilerParams(collective_id=0))
    def launch(send_sems, recv_sems):
        x, y, c, chips = _position()
        barrier = pltpu.get_barrier_semaphore()
        for chip in chips:
            pl.semaphore_signal(barrier, inc=1, device_id=(chip[0], chip[1], c), device_id_type=MESH)
        pl.semaphore_wait(barrier, len(chips))
        copies = [pltpu.make_async_remote_copy(
            src_ref=src.at[2 * chip[0] + chip[1]], dst_ref=dst.at[j], send_sem=send_sems.at[j],
            recv_sem=recv_sems.at[j], device_id=(chip[0], chip[1], c), device_id_type=MESH)
            for j, chip in enumerate(chips)]
        for cp in copies:
            cp.start()
        for cp in copies:
            cp.wait()

    launch()
    return dst[...]


def _run_on_sequencer(name, hosted):
    ins = [jax.new_ref(a, memory_space=pltpu.MemorySpace.HBM) for a in hosted.inputs]
    outs = [jax.empty_ref(shape, memory_space=pltpu.MemorySpace.HBM) for shape in hosted.out_shapes]

    @pl.kernel(mesh=plsc.ScalarSubcoreMesh(axis_name="sequencer", num_cores=1), name=name,
               scratch_types=tuple(hosted.sems), compiler_params=pltpu.CompilerParams(collective_id=1))
    def launch(*sems):
        x, y, c, chips = _position()
        peers = [(chip[0], chip[1], c) for chip in chips] + [(x, y, 1 - c)]
        barrier = pltpu.get_barrier_semaphore()
        for peer in peers:
            pl.semaphore_signal(barrier, inc=1, device_id=peer, device_id_type=MESH)
        pl.semaphore_wait(barrier, len(peers))
        hosted.first(ins, outs, sems)
        hosted.middle(ins, outs, sems)
        hosted.last(ins, outs, sems)

    launch()
    return [o[...] for o in outs]


def _run_alone(name, hosted):
    n_in = len(hosted.inputs)
    n_out = len(hosted.out_shapes)

    def body(*refs):
        ins, outs, sems = refs[:n_in], refs[n_in:n_in + n_out], refs[n_in + n_out:]
        hosted.first(ins, outs, sems)
        if hosted.middle is not None:
            hosted.middle(ins, outs, sems)
        hosted.last(ins, outs, sems)

    return pl.pallas_call(
        body, name=name, in_specs=[ANY] * n_in, out_specs=[ANY] * n_out, out_shape=hosted.out_shapes,
        scratch_shapes=hosted.sems)(*hosted.inputs)


def _swap_with_sibling(blocks):
    n = len(blocks)

    def body(*refs):
        ins, outs = refs[:n], refs[n:2 * n]
        send_sems, recv_sems = refs[2 * n:]
        x, y, c, _ = _position()
        sends = [pltpu.make_async_remote_copy(
            src_ref=ins[t], dst_ref=outs[t], send_sem=send_sems.at[t], recv_sem=recv_sems.at[t],
            device_id=(x, y, 1 - c), device_id_type=MESH) for t in range(n)]
        for cp in sends:
            cp.start()
        for cp in sends:
            cp.wait()

    return pl.pallas_call(
        body, name="swap_with_sibling",
        in_specs=[ANY] * n, out_specs=[ANY] * n,
        out_shape=[jax.ShapeDtypeStruct(b.shape, b.dtype) for b in blocks],
        scratch_shapes=[pltpu.SemaphoreType.DMA((n,))] * 2,
    )(*blocks)


def _small_step(partials, extras, ws, ms, vs, shard_of):
    n = len(partials)
    terms = list(partials) + list(extras)
    nt = len(terms)
    rows = [t for t in range(nt) if terms[t].shape[0] == 1]
    mats = [t for t in range(nt) if terms[t].shape[0] != 1]
    row_block = (8, max(terms[t].shape[1] for t in rows))
    assert len(rows) <= row_block[0]
    sent = [row_block] + [terms[t].shape for t in mats]

    def body(*refs):
        ins, refs = refs[:nt], refs[nt:]
        w_refs, refs = refs[:n], refs[n:]
        m_refs, refs = refs[:n], refs[n:]
        v_refs, refs = refs[:n], refs[n:]
        outs, refs = refs[:4 * n + nt - n], refs[4 * n + nt - n:]
        slots, (packed, send_sems, recv_sems) = refs[:len(sent)], refs[len(sent):]
        x, y, c, _ = _position()
        me = 4 * x + 2 * y + c
        packed[...] = jnp.zeros_like(packed)
        for i, t in enumerate(rows):
            packed[i:i + 1, 0:terms[t].shape[1]] = ins[t][...]
        sources = [packed] + [ins[t] for t in mats]
        sends = []
        for j, src in enumerate(sources):
            slots[j][me] = src[...]
            for k in range(1, 8):
                peer = (x ^ (k >> 2), y ^ ((k >> 1) & 1), c ^ (k & 1))
                sends.append(pltpu.make_async_remote_copy(
                    src_ref=src, dst_ref=slots[j].at[me], send_sem=send_sems.at[7 * j + k - 1],
                    recv_sem=recv_sems.at[7 * j + k - 1], device_id=peer, device_id_type=MESH))
        for cp in sends:
            cp.start()
        for j, src in enumerate(sources):
            for k in range(1, 8):
                pltpu.make_async_remote_copy(
                    src_ref=src, dst_ref=slots[j].at[me ^ k], send_sem=send_sems.at[7 * j + k - 1],
                    recv_sem=recv_sems.at[7 * j + k - 1], device_id=(x, y, c), device_id_type=MESH).wait_recv()
        for cp in sends:
            cp.wait_send()
        sums = []
        for j in range(len(sources)):
            g = slots[j][0]
            for dev in range(1, 8):
                g = g + slots[j][dev]
            sums.append(g)
        chip = 2 * x + y
        for t in range(nt):
            if t in rows:
                i = rows.index(t)
                g = sums[0][i:i + 1, 0:terms[t].shape[1]]
            else:
                g = sums[1 + mats.index(t)]
            if t >= n:
                outs[4 * n + t - n][...] = g
                continue
            if shard_of[t]:
                width = ws[t].shape[-1]
                mine = jnp.zeros(ws[t].shape, F32)
                for s in range(4):
                    mine = jnp.where(chip == s, g[:, s * width:(s + 1) * width], mine)
                g = mine
            delta, mn, vn = _adamw_math(w_refs[t][...], g, m_refs[t][...], v_refs[t][...])
            outs[4 * t][...] = g
            outs[4 * t + 1][...] = delta
            outs[4 * t + 2][...] = mn
            outs[4 * t + 3][...] = vn

    vmem = pl.BlockSpec(memory_space=pltpu.VMEM)
    out_shapes = []
    for t in range(n):
        out_shapes += [jax.ShapeDtypeStruct(ws[t].shape, F32)] * 4
    out_shapes += [jax.ShapeDtypeStruct(a.shape, F32) for a in extras]
    out_shapes += [jax.ShapeDtypeStruct((8,) + tuple(shape), F32) for shape in sent]
    res = pl.pallas_call(
        body, name="small_step",
        in_specs=[vmem] * (nt + 3 * n), out_specs=[vmem] * len(out_shapes), out_shape=out_shapes,
        scratch_shapes=[pltpu.VMEM(row_block, F32)] + [pltpu.SemaphoreType.DMA((7 * len(sent),))] * 2,
    )(*terms, *ws, *ms, *vs)
    return [res[4 * t:4 * t + 4] for t in range(n)], res[4 * n:4 * n + nt - n]


def _adamw_math(w, g, m, v):
    m = ADAM_B1 * m + (1.0 - ADAM_B1) * g
    v = ADAM_B2 * v + (1.0 - ADAM_B2) * (g * g)
    m_hat = m / (1.0 - ADAM_B1 ** ADAM_STEP)
    v_hat = v / (1.0 - ADAM_B2 ** ADAM_STEP)
    delta = -ADAM_LR * (m_hat / (jnp.sqrt(v_hat) + ADAM_EPS) + ADAM_WD * w)
    return delta, m, v


def _row_tile(rows):
    return 256 if rows % 256 == 0 else rows


def _sum_partials(name, own, recv, chip):
    rows, cols = own.shape[1:]
    tr = _row_tile(rows)

    def body(chip_ref, own_ref, recv_ref, o_ref):
        acc = own_ref[...]
        for j in range(3):
            acc = acc + recv_ref[j].astype(F32)
        o_ref[...] = acc

    return pl.pallas_call(
        body, name=name,
        grid_spec=pltpu.PrefetchScalarGridSpec(
            num_scalar_prefetch=1, grid=(rows // tr,),
            in_specs=[pl.BlockSpec((None, tr, cols), lambda i, chip_ref: (chip_ref[0], i, 0)),
                      pl.BlockSpec((3, tr, cols), lambda i, chip_ref: (0, i, 0))],
            out_specs=pl.BlockSpec((tr, cols), lambda i, chip_ref: (i, 0))),
        out_shape=jax.ShapeDtypeStruct((rows, cols), F32),
        compiler_params=_params(("parallel",)),
    )(chip.reshape(1).astype(jnp.int32), own, recv)


def _adamw(name, w, m, v, g_parts):
    rows, cols = w.shape
    tr = _row_tile(rows)
    n = len(g_parts)

    def body(w_ref, m_ref, v_ref, *refs):
        g_refs = refs[:n]
        go_ref, d_ref, mo_ref, vo_ref = refs[n:]
        g = g_refs[0][...]
        for r in g_refs[1:]:
            g = g + r[...]
        delta, mn, vn = _adamw_math(w_ref[...], g, m_ref[...], v_ref[...])
        go_ref[...] = g
        d_ref[...] = delta
        mo_ref[...] = mn
        vo_ref[...] = vn

    spec = pl.BlockSpec((tr, cols), lambda i: (i, 0))
    return pl.pallas_call(
        body, name=name, grid=(rows // tr,),
        in_specs=[spec] * (3 + n), out_specs=[spec] * 4,
        out_shape=[jax.ShapeDtypeStruct((rows, cols), F32)] * 4,
        compiler_params=_params(("parallel",)),
    )(w, m, v, *g_parts)


def _local_step(x, target, ga, wa_in, rel_bias, later_shards, gk, t5, gb, sinks, gf):
    s, d = x.shape
    tm = min(TM_DENSE, s)
    nt = s // tm
    half = d // 2
    row = pl.BlockSpec((tm, d), lambda i: (i, 0))
    whole = lambda shape: pl.BlockSpec(shape, lambda *_: (0,) * len(shape))

    n1, = _norm_fwd("norm_a", x, ga)
    zqkv = _matmul("proj_a_qkv", n1, wa_in, dims=NN, grid=(3, nt + 1), zero_axis=1,
                   a_spec=pl.BlockSpec((tm, d), lambda j, i: (jnp.maximum(i - 1, 0), 0)),
                   b_spec=pl.BlockSpec((None, d, d), lambda j, i: (j, 0, 0)),
                   o_spec=pl.BlockSpec((None, tm, d), lambda j, i: (j, i, 0)),
                   out_shape=(3, tm + s, d), out_dtype=BF16)
    gate_a = _matmul("proj_a_gate", n1, wa_in, dims=NN, grid=(nt,),
                     a_spec=row, b_spec=pl.BlockSpec((None, d, d), lambda i: (3, 0, 0)), o_spec=row,
                     out_shape=(s, d), out_dtype=F32)
    onehot_a = _a_offset_onehot()
    diag_a = _diag_rows(onehot_a, rel_bias)
    (o_a, u_a, lse_a), gathered = _attn_a_fwd(
        zqkv, gate_a, diag_a, hosted=_allgather_hosted(later_shards, [True] * len(later_shards)))
    wa_out, wkv, wb_in, wb_out = gathered
    wa_out = wa_out.reshape(d, d)
    wkv = wkv.reshape(d, -1)
    wb_out = wb_out.reshape(d, d)
    h1 = _matmul("out_a", u_a, wa_out, dims=NN, grid=(nt,), a_spec=row, b_spec=whole((d, d)), o_spec=row,
                 out_shape=(s, d), out_dtype=F32, resid=x, resid_spec=row)

    nk, n2 = _norm_fwd("norm_kv_b", h1, jnp.concatenate([gk, gb], axis=0))
    kvw = wkv.shape[1]
    wkv_x = jnp.concatenate([wkv[:, (i // 2) * HEAD_DIM:(i // 2 + 1) * HEAD_DIM] for i in range(8)], axis=1)
    kvx = _matmul("proj_kv", nk, wkv_x, dims=NN, grid=(nt + 1,), zero_axis=0,
                  a_spec=pl.BlockSpec((tm, d), lambda i: (jnp.maximum(i - 1, 0), 0)), b_spec=whole((d, B_KVX)),
                  o_spec=pl.BlockSpec((tm, B_KVX), lambda i: (i, 0)), out_shape=(tm + s, B_KVX), out_dtype=BF16)
    qb = _matmul("proj_b_q", n2, wb_in, dims=NN, grid=(2, nt),
                 a_spec=pl.BlockSpec((tm, d), lambda j, i: (i, 0)),
                 b_spec=pl.BlockSpec((None, d, half), lambda j, i: (j, 0, 0)),
                 o_spec=pl.BlockSpec((tm, half), lambda j, i: (i, j)), out_shape=(s, d), out_dtype=BF16)
    gate_b = _matmul("proj_b_gate", n2, wb_in, dims=NN, grid=(2, nt),
                     a_spec=pl.BlockSpec((tm, d), lambda j, i: (i, 0)),
                     b_spec=pl.BlockSpec((None, d, half), lambda j, i: (2 + j, 0, 0)),
                     o_spec=pl.BlockSpec((tm, half), lambda j, i: (i, j)), out_shape=(s, d), out_dtype=F32)
    onehot_b = _b_offset_onehot()
    base_b = jnp.roll(_diag_rows(onehot_b, t5)[..., ::-1], TQ, axis=-1)
    o_b, u_b, lse_b = _attn_b_fwd(qb, kvx, gate_b, base_b, sinks)
    h2 = _matmul("out_b", u_b, wb_out, dims=NN, grid=(nt,), a_spec=row, b_spec=whole((d, d)), o_spec=row,
                 out_shape=(s, d), out_dtype=F32, resid=h1, resid_spec=row)

    dh2, loss, d_gf = _loss_head(h2, target, gf)

    du_b = _matmul("dout_b", dh2, wb_out, dims=NT, grid=(nt,), a_spec=row, b_spec=whole((d, d)), o_spec=row,
                   out_shape=(s, d), out_dtype=F32)
    d_wb_out = _matmul("dw_out_b", u_b, dh2, dims=TN, grid=(2,),
                       a_spec=whole((s, d)), b_spec=pl.BlockSpec((s, half), lambda j: (0, j)),
                       o_spec=pl.BlockSpec((d, half), lambda j: (0, j)),
                       out_shape=(d, d), out_dtype=F32, also_bf16=True)
    dz_b, dkv, dsum_b, dsinks = _attn_b_bwd(qb, kvx, gate_b, o_b, du_b, lse_b, base_b, sinks)
    ddiag_b = jnp.roll(dsum_b[..., ::-1], -1, axis=-1)
    d_wb_in = _matmul("dw_in_b", n2, dz_b, dims=TN, grid=(4,),
                      a_spec=whole((s, d)), b_spec=pl.BlockSpec((None, s, half), lambda j: (j, 0, 0)),
                      o_spec=pl.BlockSpec((None, d, half), lambda j: (j, 0, 0)),
                      out_shape=(4, d, half), out_dtype=F32, also_bf16=True)
    d_wkv = _matmul("dw_kv", nk, dkv, dims=TN, grid=(1,),
                    a_spec=whole((s, d)), b_spec=whole((s, kvw)), o_spec=whole((d, kvw)),
                    out_shape=(d, kvw), out_dtype=F32, also_bf16=True)
    dh1, d_gkb = _proj_norm_bwd("dproj_kv_b", h1, dh2, jnp.concatenate([gk, gb], axis=0),
                                [(dkv[None], wkv[None]), (dz_b, wb_in)])

    du_a = _matmul("dout_a", dh1, wa_out, dims=NT, grid=(nt,), a_spec=row, b_spec=whole((d, d)), o_spec=row,
                   out_shape=(s, d), out_dtype=F32)
    d_wa_out = _matmul("dw_out_a", u_a, dh1, dims=TN, grid=(2,),
                       a_spec=whole((s, d)), b_spec=pl.BlockSpec((s, half), lambda j: (0, j)),
                       o_spec=pl.BlockSpec((d, half), lambda j: (0, j)),
                       out_shape=(d, d), out_dtype=F32, also_bf16=True)
    early = dict(a_w_out=[g.reshape(4, d // 4, d) for g in d_wa_out],
                 kv_w=[g.reshape(4, d // 4, kvw) for g in d_wkv], b_w_in=list(d_wb_in),
                 b_w_out=[g.reshape(4, d // 4, d) for g in d_wb_out])
    (dz_a, ddiag_a), early_recv = _attn_a_bwd(
        zqkv, gate_a, o_a, du_a, lse_a, diag_a, hosted=_scatter_hosted([early[n][1] for n in early]))
    d_wa_in = _matmul("dw_in_a", n1, dz_a, dims=TN, grid=(4, 2),
                      a_spec=whole((s, d)), b_spec=pl.BlockSpec((None, s, half), lambda j, h: (j, 0, h)),
                      o_spec=pl.BlockSpec((None, d, half), lambda j, h: (j, 0, h)),
                      out_shape=(4, d, d), out_dtype=F32, also_bf16=True)
    late_recv = [_scatter_on_sequencer("scatter_a_w_in", d_wa_in[1])]
    grad_x, d_ga = _proj_norm_bwd("dproj_a", x, dh1, ga, [(dz_a, wa_in)])

    small = dict(a_norm=d_ga, kv_norm=d_gkb[0:1], b_norm=d_gkb[1:2], b_sinks=dsinks[0:1, :HEADS], final_norm=d_gf)
    small["by_offset"] = dict(a_rel_bias=(onehot_a, ddiag_a.reshape(HEADS, -1)),
                              t5_bias=(onehot_b, ddiag_b.reshape(HEADS, -1)))
    own = dict(a_w_in=d_wa_in[0], **{n: early[n][0] for n in early})
    received = dict(a_w_in=late_recv[0], **dict(zip(early, early_recv)))
    return loss, grad_x, small, own, received


SMALL = ("a_norm", "kv_norm", "b_norm", "b_sinks", "final_norm")
TABLES = ("a_rel_bias", "t5_bias")
BIG = ("a_w_in", "a_w_out", "kv_w", "b_w_in", "b_w_out")
ORDER = ("a_norm", "a_w_in", "a_rel_bias", "a_w_out", "kv_norm", "kv_w", "t5_bias", "b_norm", "b_w_in",
         "b_sinks", "b_w_out", "final_norm")


def kernel(x, a_norm, a_w_in, a_rel_bias, a_w_out, kv_norm, kv_w, t5_bias, b_norm, b_w_in, b_sinks, b_w_out, final_norm, loss_target, m_a_norm, m_a_w_in, m_a_rel_bias, m_a_w_out, m_kv_norm, m_kv_w, m_t5_bias, m_b_norm, m_b_w_in, m_b_sinks, m_b_w_out, m_final_norm, v_a_norm, v_a_w_in, v_a_rel_bias, v_a_w_out, v_kv_norm, v_kv_w, v_t5_bias, v_b_norm, v_b_w_in, v_b_sinks, v_b_w_out, v_final_norm):
    w = dict(a_norm=a_norm, a_w_in=a_w_in, a_rel_bias=a_rel_bias, a_w_out=a_w_out, kv_norm=kv_norm, kv_w=kv_w,
             t5_bias=t5_bias, b_norm=b_norm, b_w_in=b_w_in, b_sinks=b_sinks, b_w_out=b_w_out,
             final_norm=final_norm)
    m = dict(a_norm=m_a_norm, a_w_in=m_a_w_in, a_rel_bias=m_a_rel_bias, a_w_out=m_a_w_out, kv_norm=m_kv_norm,
             kv_w=m_kv_w, t5_bias=m_t5_bias, b_norm=m_b_norm, b_w_in=m_b_w_in, b_sinks=m_b_sinks,
             b_w_out=m_b_w_out, final_norm=m_final_norm)
    v = dict(a_norm=v_a_norm, a_w_in=v_a_w_in, a_rel_bias=v_a_rel_bias, a_w_out=v_a_w_out, kv_norm=v_kv_norm,
             kv_w=v_kv_w, t5_bias=v_t5_bias, b_norm=v_b_norm, b_w_in=v_b_w_in, b_sinks=v_b_sinks,
             b_w_out=v_b_w_out, final_norm=v_final_norm)
    d = D_MODEL
    chip = 2 * lax.axis_index("x") + lax.axis_index("y")

    shard2d = dict(a_w_in=a_w_in[0], a_w_out=a_w_out[0], kv_w=kv_w, b_w_in=b_w_in[0], b_w_out=b_w_out[0])

    wa_in, = _run_on_sequencer("allgather_first", _allgather_hosted([shard2d["a_w_in"].astype(BF16)], [True]))
    ga, = _run_alone("allgather_norm", _allgather_hosted([a_norm], [False]))
    ga = ga.reshape(1, d)

    loss, grad_x, small, own, received = _local_step(
        x[0], loss_target[0], ga, wa_in, a_rel_bias[0], [shard2d[n].astype(BF16) for n in BIG[1:]],
        kv_norm.reshape(1, d), t5_bias, b_norm, b_sinks, final_norm.reshape(1, d))

    out = {}
    as2d = lambda a: a.reshape(-1, a.shape[-1])
    small_res, (loss_sum, *offset_sums) = _small_step(
        [small[n] for n in SMALL], [loss] + [small["by_offset"][n][1] for n in TABLES],
        [as2d(w[n]) for n in SMALL], [as2d(m[n]) for n in SMALL], [as2d(v[n]) for n in SMALL],
        [n == "a_norm" for n in SMALL])
    for n, res in zip(SMALL, small_res):
        out[n] = [r.reshape(w[n].shape) for r in res]
    loss_out = loss_sum.reshape(())
    for n, summed in zip(TABLES, offset_sums):
        grad = _diag_rows_grad(small["by_offset"][n][0], summed)
        res = _adamw("adamw_" + n, as2d(w[n]), as2d(m[n]), as2d(v[n]), [grad])
        out[n] = [r.reshape(w[n].shape) for r in res]

    core_sums = [_sum_partials("sum_" + n, own[n], received[n], chip) for n in BIG]
    sibling_sums = _swap_with_sibling(core_sums)

    for n, mine, theirs in zip(BIG, core_sums, sibling_sums):
        res = _adamw("adamw_" + n, shard2d[n], m[n].reshape(shard2d[n].shape), v[n].reshape(shard2d[n].shape),
                     [mine, theirs])
        out[n] = [r.reshape(w[n].shape) for r in res]

    grads = [out[n][0] for n in ORDER]
    deltas = [out[n][1] for n in ORDER]
    new_m = [out[n][2] for n in ORDER]
    new_v = [out[n][3] for n in ORDER]
    return (loss_out, grad_x[None], *grads, *deltas, *new_m, *new_v)
```

```python
import functools
import math

import jax
import jax.numpy as jnp
import numpy as np
from jax import lax
from jax.experimental import pallas as pl
from jax.experimental.pallas import tpu as pltpu
from jax.experimental.pallas import tpu_sc as plsc

F32 = jnp.float32
BF16 = jnp.bfloat16
MESH = pl.DeviceIdType.MESH

D_MODEL = 1024
HEADS = 16
HEAD_DIM = 64
CHUNK = 64
RMS_EPS = 1e-6
SCALE = HEAD_DIM ** -0.5
A_LEFT_CHUNKS = 8
A_REL_CLIP = 256
B_LEFT_CHUNKS = 2
B_KV_HEADS = 2
B_GROUP = HEADS // B_KV_HEADS
T5_BUCKETS = 32
T5_MAX_DIST = 128
ADAM_LR = 0.001
ADAM_B1 = 0.9
ADAM_B2 = 0.999
ADAM_EPS = 1e-08
ADAM_WD = 0.01
ADAM_STEP = 10

MASKED = -1e30
LANES = 128
TQ = 128
A_PAIRS = 2
A_PAIRS_FWD = 4
KB = 128
A_KBLOCKS = A_LEFT_CHUNKS * CHUNK // KB + 1
B_KBLOCKS = B_LEFT_CHUNKS * CHUNK // KB + 1
A_WIN = A_KBLOCKS * KB
B_WIN = B_KBLOCKS * KB
TM = 512
TM_DENSE = 1024
TM_PARTS = 512
VMEM_LIMIT = 56 * 1024 * 1024

NT = (((1,), (1,)), ((), ()))
TN = (((0,), (0,)), ((), ()))
NN = (((1,), (0,)), ((), ()))


def _params(sem=None):
    return pltpu.CompilerParams(dimension_semantics=sem, vmem_limit_bytes=VMEM_LIMIT)


class _Hosted:
    def __init__(self, inputs, out_shapes, sems, first, middle, last):
        self.inputs, self.out_shapes, self.sems = list(inputs), list(out_shapes), list(sems)
        self.first, self.middle, self.last = first, middle, last


def _call(body, *, name, grid, in_specs, out_specs, out_shape, args, scratch_shapes=(), sem=None, hosted=None):
    in_specs, out_specs, out_shape = list(in_specs), list(out_specs), list(out_shape)
    scratch_shapes = list(scratch_shapes)
    if hosted is None:
        out = pl.pallas_call(
            body, name=name, grid=grid, in_specs=in_specs, out_specs=out_specs, out_shape=out_shape,
            scratch_shapes=scratch_shapes, compiler_params=_params(sem))(*args)
        return list(out), []
    n_in, n_out, n_scr = len(in_specs), len(out_shape), len(scratch_shapes)
    h_in, h_out = len(hosted.inputs), len(hosted.out_shapes)
    total = int(np.prod(grid)) if grid else 1

    def wrapped(*refs):
        ins, refs = refs[:n_in], refs[n_in:]
        h_ins, refs = refs[:h_in], refs[h_in:]
        outs, refs = refs[:n_out], refs[n_out:]
        h_outs, refs = refs[:h_out], refs[h_out:]
        scr, h_sems = refs[:n_scr], refs[n_scr:]
        step = 0
        for axis, size in enumerate(grid):
            step = step * size + pl.program_id(axis)

        @pl.when(step == 0)
        def _():
            hosted.first(h_ins, h_outs, h_sems)

        body(*ins, *outs, *scr)
        if hosted.middle is not None:
            @pl.when(step == total // 2)
            def _():
                hosted.middle(h_ins, h_outs, h_sems)

        @pl.when(step == total - 1)
        def _():
            hosted.last(h_ins, h_outs, h_sems)

    out = pl.pallas_call(
        wrapped, name=name, grid=grid, in_specs=in_specs + [ANY] * h_in, out_specs=out_specs + [ANY] * h_out,
        out_shape=out_shape + hosted.out_shapes, scratch_shapes=scratch_shapes + hosted.sems,
        compiler_params=_params(("arbitrary",) * len(grid)))(*args, *hosted.inputs)
    return list(out[:n_out]), list(out[n_out:])


def _matmul(name, a, b, *, dims, grid, a_spec, b_spec, o_spec, out_shape, out_dtype,
            parts=1, resid=None, resid_spec=None, also_bf16=False, hosted=None, zero_axis=None):
    def body(*refs):
        if zero_axis is None:
            product(*refs)
        else:
            @pl.when(pl.program_id(zero_axis) == 0)
            def _():
                refs[2][...] = jnp.zeros_like(refs[2])

            @pl.when(pl.program_id(zero_axis) > 0)
            def _():
                product(*refs)

    def product(*refs):
        a_ref, b_ref = refs[:2]
        r_ref = refs[2] if resid is not None else None
        o_ref = refs[3] if resid is not None else refs[2]
        if parts == 1:
            prod = lax.dot_general(a_ref[...].astype(BF16), b_ref[...].astype(BF16), dims,
                                   preferred_element_type=F32)
        else:
            prod = None
            for part in range(parts):
                term = lax.dot_general(a_ref[part].astype(BF16), b_ref[part].astype(BF16), dims,
                                       preferred_element_type=F32)
                prod = term if prod is None else prod + term
        if resid is not None:
            prod = r_ref[...] + prod
        o_ref[...] = prod.astype(out_dtype)
        if also_bf16:
            refs[-1][...] = prod.astype(BF16)

    in_specs = [a_spec, b_spec]
    args = [a, b]
    if resid is not None:
        in_specs.append(resid_spec)
        args.append(resid)
    sem = ["parallel"] * len(grid)
    out_specs = [o_spec]
    out_shapes = [jax.ShapeDtypeStruct(out_shape, out_dtype)]
    if also_bf16:
        out_specs.append(o_spec)
        out_shapes.append(jax.ShapeDtypeStruct(out_shape, BF16))
    out, extra = _call(body, name=name, grid=grid, in_specs=in_specs, out_specs=out_specs, out_shape=out_shapes,
                       args=args, sem=tuple(sem), hosted=hosted)
    res = out[0] if not also_bf16 else tuple(out)
    return res if hosted is None else (res, extra)


def _rms_rows(x):
    return lax.rsqrt(jnp.mean(x * x, axis=-1, keepdims=True) + RMS_EPS)


def _norm_fwd(name, x, gains):
    s, d = x.shape
    n = gains.shape[0]

    def body(x_ref, g_ref, *o_refs):
        xv = x_ref[...]
        xh = xv * _rms_rows(xv)
        for i in range(n):
            o_refs[i][...] = (xh * g_ref[i:i + 1, :]).astype(BF16)

    row = pl.BlockSpec((TM, d), lambda i: (i, 0))
    return pl.pallas_call(
        body, name=name, grid=(s // TM,),
        in_specs=[row, pl.BlockSpec((n, d), lambda i: (0, 0))],
        out_specs=[row] * n,
        out_shape=[jax.ShapeDtypeStruct((s, d), BF16)] * n,
        compiler_params=_params(("parallel",)),
    )(x, gains)


def _proj_norm_bwd(name, x, dres, gains, branches):
    s, d = x.shape
    n = len(branches)
    tm = min(TM_PARTS, s)

    def body(x_ref, r_ref, g_ref, *refs):
        ab_refs, dx_ref, dg_ref = refs[:2 * n], refs[2 * n], refs[2 * n + 1]
        i = pl.program_id(0)
        xv = x_ref[...]
        r = _rms_rows(xv)
        xh = xv * r

        @pl.when(i == 0)
        def _():
            dg_ref[...] = jnp.zeros_like(dg_ref)

        a = None
        for j in range(n):
            a_ref, b_ref = ab_refs[2 * j], ab_refs[2 * j + 1]
            dn = None
            for part in range(a_ref.shape[0]):
                term = lax.dot_general(a_ref[part], b_ref[part], NT, preferred_element_type=F32)
                dn = term if dn is None else dn + term
            t = dn * g_ref[j:j + 1, :]
            a = t if a is None else a + t
            dg_ref[j:j + 1, :] += jnp.sum(dn * xh, axis=0, keepdims=True)
        dx_ref[...] = r_ref[...] + r * (a - xh * jnp.mean(xh * a, axis=-1, keepdims=True))

    row = pl.BlockSpec((tm, d), lambda i: (i, 0))
    small = pl.BlockSpec((n, d), lambda i: (0, 0))
    ab_specs, ab_args = [], []
    for a, b in branches:
        ab_specs += [pl.BlockSpec((a.shape[0], tm, a.shape[2]), lambda i: (0, i, 0)),
                     pl.BlockSpec(b.shape, lambda i: (0, 0, 0))]
        ab_args += [a, b]
    return pl.pallas_call(
        body, name=name, grid=(s // tm,),
        in_specs=[row, row, small] + ab_specs,
        out_specs=[row, small],
        out_shape=[jax.ShapeDtypeStruct((s, d), F32), jax.ShapeDtypeStruct((n, d), F32)],
        compiler_params=_params(("arbitrary",)),
    )(x, dres, gains, *ab_args)


def _loss_head(h2, target, gain):
    s, d = h2.shape

    def body(h_ref, t_ref, g_ref, dh_ref, loss_ref, dg_ref):
        i = pl.program_id(0)
        hv = h_ref[...]
        r = _rms_rows(hv)
        hh = hv * r
        g = g_ref[...]
        err = hh * g - t_ref[...]
        part = 0.5 * jnp.sum(jnp.sum(err * err, axis=-1, keepdims=True) * (1.0 / d), axis=0, keepdims=True)
        dy = err * (1.0 / d)
        a = dy * g
        dh_ref[...] = r * (a - hh * jnp.mean(hh * a, axis=-1, keepdims=True))
        dg = jnp.sum(dy * hh, axis=0, keepdims=True)

        @pl.when(i == 0)
        def _():
            loss_ref[...] = part
            dg_ref[...] = dg

        @pl.when(i > 0)
        def _():
            loss_ref[...] += part
            dg_ref[...] += dg

    row = pl.BlockSpec((TM, d), lambda i: (i, 0))
    return pl.pallas_call(
        body, name="loss_head", grid=(s // TM,),
        in_specs=[row, row, pl.BlockSpec((1, d), lambda i: (0, 0))],
        out_specs=[row, pl.BlockSpec((1, 1), lambda i: (0, 0)), pl.BlockSpec((1, d), lambda i: (0, 0))],
        out_shape=[jax.ShapeDtypeStruct((s, d), F32), jax.ShapeDtypeStruct((1, 1), F32),
                   jax.ShapeDtypeStruct((1, d), F32)],
        compiler_params=_params(("arbitrary",)),
    )(h2, target, gain)


def _silu_parts(g):
    sig = jax.nn.sigmoid(g)
    return g * sig, sig * (1.0 + g * (1.0 - sig))


def _lane_lo(rows):
    return lax.broadcasted_iota(jnp.int32, (rows, LANES), 1) < HEAD_DIM


def _stack_pair(x):
    lo = _lane_lo(x.shape[0])
    zero = jnp.zeros_like(x)
    return jnp.concatenate([jnp.where(lo, x, zero), jnp.where(lo, zero, x)], axis=0)


def _unstack_pair(y, w):
    return jnp.where(_lane_lo(w), y[:w], y[w:])


def _block_valid(b, left_blocks, width):
    col = lax.broadcasted_iota(jnp.int32, (1, 2 * width), 1)
    col = jnp.where(col >= width, col - width, col)
    return (col // KB + (b - left_blocks)) >= 0


def _toeplitz_tile(diag_row, width, left_chunks):
    wide = width + TQ
    rolled = pltpu.roll(jnp.broadcast_to(diag_row, (TQ, wide)), 1, 1, stride=1, stride_axis=0)
    i = lax.broadcasted_iota(jnp.int32, (TQ, width), 0) // CHUNK
    j = lax.broadcasted_iota(jnp.int32, (TQ, width), 1) // CHUNK
    dc = i + left_chunks - j
    return jnp.where((dc >= 0) & (dc <= left_chunks), rolled[:, TQ:], MASKED)


def _toeplitz_sum(tile, width):
    flip = (lax.broadcasted_iota(jnp.int32, (TQ, TQ), 0) + lax.broadcasted_iota(jnp.int32, (TQ, TQ), 1)
            == TQ - 1).astype(F32)
    reversed_rows = jnp.dot(flip, tile, precision=lax.Precision.HIGHEST, preferred_element_type=F32)
    padded = jnp.concatenate([reversed_rows, jnp.zeros((TQ, TQ), F32)], axis=1)
    rolled = pltpu.roll(padded, 0, 1, stride=1, stride_axis=0)
    return jnp.sum(rolled, axis=0, keepdims=True)


def _softmax_pair(sc, w, sink=None):
    ps, inv, lses = [], [], []
    for e in range(2):
        sh = sc[:, e * w:(e + 1) * w]
        m = jnp.max(sh, axis=-1, keepdims=True)
        if sink is not None:
            m = jnp.maximum(m, sink[e])
        ex = jnp.exp(sh - m)
        l = jnp.sum(ex, axis=-1, keepdims=True)
        if sink is not None:
            l = l + jnp.exp(sink[e] - m)
        ps.append(ex.astype(BF16))
        inv.append(1.0 / l)
        lses.append(m + jnp.log(l))
    return jnp.concatenate(ps, axis=-1), inv, lses


def _softmax_pair_bwd(sc, dp, lse, delta, w):
    ps, dss = [], []
    for e in range(2):
        p = jnp.exp(sc[:, e * w:(e + 1) * w] - lse[e])
        ps.append(p)
        dss.append(p * (dp[:, e * w:(e + 1) * w] - delta[e]))
    return jnp.concatenate(ps, axis=-1), jnp.concatenate(dss, axis=-1)


def _pair_rowsums(x, lo):
    zero = jnp.zeros_like(x)
    return (jnp.sum(jnp.where(lo, x, zero), axis=-1, keepdims=True),
            jnp.sum(jnp.where(lo, zero, x), axis=-1, keepdims=True))


def _a_qkv_specs(rows, pad, pw):
    return [pl.BlockSpec((None, TQ, pw), lambda p, b: (0, b + pad // TQ, p)),
            pl.BlockSpec((None, rows, pw), lambda p, b: (1, 0, p)),
            pl.BlockSpec((None, rows, pw), lambda p, b: (2, 0, p))]


def _window(ref, b, pad, win, lanes):
    start = pl.multiple_of(b * TQ + pad - (win - TQ), KB)
    return ref[pl.ds(start, win), lanes]


def _attn_a_fwd(zqkv, g, diag, hosted=None):
    s = g.shape[0]
    pad = zqkv.shape[1] - s
    nb = s // TQ
    left = A_KBLOCKS - 1
    pairs = A_PAIRS_FWD
    pw = pairs * LANES
    wide = A_WIN + TQ

    def body(q_ref, k_ref, v_ref, g_ref, diag_ref, o_ref, u_ref, lse_ref, bias_scr):
        b = pl.program_id(1)

        @pl.when(b == 0)
        def _():
            for hh in range(2 * pairs):
                bias_scr[hh // 2, :, (hh % 2) * A_WIN:(hh % 2 + 1) * A_WIN] = _toeplitz_tile(
                    diag_ref[hh], A_WIN, A_LEFT_CHUNKS)

        def step(first_blocks):
            lo = _lane_lo(TQ)
            for pp in range(pairs):
                ln = slice(pp * LANES, (pp + 1) * LANES)
                kcat = _stack_pair(_window(k_ref, b, pad, A_WIN, ln))
                vcat = _stack_pair(_window(v_ref, b, pad, A_WIN, ln))
                sc = lax.dot_general(q_ref[:, ln] * SCALE, kcat, NT, preferred_element_type=F32) + bias_scr[pp]
                if first_blocks:
                    sc = jnp.where(_block_valid(b, left, A_WIN), sc, MASKED)
                p, inv, lses = _softmax_pair(sc, A_WIN)
                ov = jnp.dot(p, vcat, preferred_element_type=F32) * jnp.where(lo, inv[0], inv[1])
                o_ref[:, ln] = ov
                lse_ref[pp] = jnp.where(lo, lses[0], lses[1])
                sg, _ = _silu_parts(g_ref[:, ln])
                u_ref[:, ln] = (ov * sg).astype(BF16)

        @pl.when(b < left)
        def _():
            step(True)

        @pl.when(b >= left)
        def _():
            step(False)

    tile = pl.BlockSpec((TQ, pw), lambda p, b: (b, p))
    return _call(
        body, name="attn_a_fwd", grid=(HEADS // 2 // pairs, nb),
        in_specs=_a_qkv_specs(pad + s, pad, pw) + [
            tile, pl.BlockSpec((2 * pairs, 1, wide), lambda p, b: (p, 0, 0))],
        out_specs=[tile, tile, pl.BlockSpec((pairs, TQ, LANES), lambda p, b: (p, b, 0))],
        out_shape=[jax.ShapeDtypeStruct((s, D_MODEL), F32), jax.ShapeDtypeStruct((s, D_MODEL), BF16),
                   jax.ShapeDtypeStruct((HEADS // 2, s, LANES), F32)],
        scratch_shapes=[pltpu.VMEM((pairs, TQ, 2 * A_WIN), F32)],
        sem=("parallel", "arbitrary"), hosted=hosted,
        args=(zqkv, zqkv, zqkv, g, diag))


def _attn_a_bwd(zqkv, g, o, du, lse, diag, hosted=None):
    s = g.shape[0]
    pad = zqkv.shape[1] - s
    nb = s // TQ
    left = A_KBLOCKS - 1
    pw = A_PAIRS * LANES
    wide = A_WIN + TQ

    def body(q_ref, k_ref, v_ref, g_ref, o_ref, du_ref, lse_ref, diag_ref, dz_ref, ddiag_ref,
             bias_scr, dbias_acc, dk_acc, dv_acc):
        b = pl.program_id(1)

        @pl.when(b == 0)
        def _():
            for hh in range(2 * A_PAIRS):
                bias_scr[hh // 2, :, (hh % 2) * A_WIN:(hh % 2 + 1) * A_WIN] = _toeplitz_tile(
                    diag_ref[hh], A_WIN, A_LEFT_CHUNKS)
            dbias_acc[...] = jnp.zeros_like(dbias_acc)
            dk_acc[...] = jnp.zeros_like(dk_acc)
            dv_acc[...] = jnp.zeros_like(dv_acc)

        def step(first_blocks):
            lo = _lane_lo(TQ)
            upper = lax.broadcasted_iota(jnp.int32, (LANES, A_WIN), 0) < HEAD_DIM
            rows = pl.ds(pl.multiple_of(b * TQ, TQ), TQ)
            sg, dsg = _silu_parts(g_ref[...])
            duv = du_ref[...]
            ov = o_ref[...]
            do = duv * sg
            dz_ref[3, rows, :] = (duv * ov * dsg).astype(BF16)
            do_o = do * ov
            do_bf = do.astype(BF16)
            for pp in range(A_PAIRS):
                ln = slice(pp * LANES, (pp + 1) * LANES)
                q = q_ref[:, ln] * SCALE
                kcat = _stack_pair(_window(k_ref, b, pad, A_WIN, ln))
                vcat = _stack_pair(_window(v_ref, b, pad, A_WIN, ln))
                sc = lax.dot_general(q, kcat, NT, preferred_element_type=F32) + bias_scr[pp]
                if first_blocks:
                    sc = jnp.where(_block_valid(b, left, A_WIN), sc, MASKED)
                lse_t = lse_ref[pp]
                dp = lax.dot_general(do_bf[:, ln], vcat, NT, preferred_element_type=F32)
                p, ds = _softmax_pair_bwd(sc, dp, (lse_t[:, 0:1], lse_t[:, HEAD_DIM:HEAD_DIM + 1]),
                                          _pair_rowsums(do_o[:, ln], lo), A_WIN)
                dbias_acc[pp] += ds
                dsb = ds.astype(BF16)
                dz_ref[0, rows, ln] = (jnp.dot(dsb, kcat, preferred_element_type=F32) * SCALE).astype(BF16)
                dkt = lax.dot_general(q, dsb, TN, preferred_element_type=F32)
                dvt = lax.dot_general(do_bf[:, ln], p.astype(BF16), TN, preferred_element_type=F32)
                dkt = jnp.where(upper, dkt[:, :A_WIN], dkt[:, A_WIN:])
                dvt = jnp.where(upper, dvt[:, :A_WIN], dvt[:, A_WIN:])
                for t in range(A_KBLOCKS):
                    blk = b + (pad // KB - left + t)
                    dk_acc[blk, ln, :] += dkt[:, t * KB:(t + 1) * KB]
                    dv_acc[blk, ln, :] += dvt[:, t * KB:(t + 1) * KB]

        @pl.when(b < left)
        def _():
            step(True)

        @pl.when(b >= left)
        def _():
            step(False)

        @pl.when(b == nb - 1)
        def _():
            for kb in range(s // KB):
                dz_ref[1, kb * KB:(kb + 1) * KB, :] = dk_acc[pad // KB + kb].T.astype(BF16)
                dz_ref[2, kb * KB:(kb + 1) * KB, :] = dv_acc[pad // KB + kb].T.astype(BF16)
            for hh in range(2 * A_PAIRS):
                ddiag_ref[hh] = _toeplitz_sum(
                    dbias_acc[hh // 2, :, (hh % 2) * A_WIN:(hh % 2 + 1) * A_WIN], A_WIN)

    tile = pl.BlockSpec((TQ, pw), lambda p, b: (b, p))
    diag_spec = pl.BlockSpec((2 * A_PAIRS, 1, wide), lambda p, b: (p, 0, 0))
    return _call(
        body, name="attn_a_bwd", grid=(HEADS // 2 // A_PAIRS, nb),
        in_specs=_a_qkv_specs(pad + s, pad, pw) + [
            tile, tile, tile, pl.BlockSpec((A_PAIRS, TQ, LANES), lambda p, b: (p, b, 0)), diag_spec],
        out_specs=[pl.BlockSpec((4, s, pw), lambda p, b: (0, 0, p)), diag_spec],
        out_shape=[jax.ShapeDtypeStruct((4, s, D_MODEL), BF16),
                   jax.ShapeDtypeStruct((HEADS, 1, wide), F32)],
        scratch_shapes=[pltpu.VMEM((A_PAIRS, TQ, 2 * A_WIN), F32), pltpu.VMEM((A_PAIRS, TQ, 2 * A_WIN), F32),
                        pltpu.VMEM(((pad + s) // KB, pw, KB), F32), pltpu.VMEM(((pad + s) // KB, pw, KB), F32)],
        sem=("parallel", "arbitrary"), hosted=hosted,
        args=(zqkv, zqkv, zqkv, g, o, du, lse, diag))


B_STACK = B_GROUP // 2
B_KVX = 4 * LANES
B_ROWS = B_STACK * TQ
B_WIDE = B_WIN + TQ


def _b_head_place(h):
    return h // B_GROUP, (h % B_GROUP) // 2, h % 2


def _toeplitz_tile_t(base_row, width, left_chunks):
    wide = width + TQ
    rolled = pltpu.roll(jnp.broadcast_to(base_row, (width, wide)), 0, 1, stride=1, stride_axis=0)
    j = lax.broadcasted_iota(jnp.int32, (width, TQ), 0) // CHUNK
    i = lax.broadcasted_iota(jnp.int32, (width, TQ), 1) // CHUNK
    dc = i + left_chunks - j
    return jnp.where((dc >= 0) & (dc <= left_chunks), rolled[:, :TQ], MASKED)


def _toeplitz_sum_t(tile_t, width):
    flip = (lax.broadcasted_iota(jnp.int32, (width, width), 0) + lax.broadcasted_iota(jnp.int32, (width, width), 1)
            == width - 1).astype(F32)
    reversed_rows = jnp.dot(flip, tile_t, precision=lax.Precision.HIGHEST, preferred_element_type=F32)
    padded = jnp.concatenate([reversed_rows, jnp.zeros((width, width), F32)], axis=1)
    rolled = pltpu.roll(padded, 0, 1, stride=1, stride_axis=0)
    return jnp.sum(rolled, axis=0, keepdims=True)


def _b_build_bias(base_ref, bias_scr):
    for h in range(HEADS):
        gi, pr, e = _b_head_place(h)
        bias_scr[gi, e * B_WIN:(e + 1) * B_WIN, pr * TQ:(pr + 1) * TQ] = _toeplitz_tile_t(
            base_ref[h], B_WIN, B_LEFT_CHUNKS)


def _b_stack(x, gi):
    return jnp.concatenate(
        [x[:, (B_STACK * gi + pr) * LANES:(B_STACK * gi + pr + 1) * LANES] for pr in range(B_STACK)], axis=0)


def _b_sink_rows(sink_ref, gi):
    block = lax.broadcasted_iota(jnp.int32, (1, B_ROWS), 1) // TQ
    rows = []
    for e in range(2):
        row = jnp.zeros((1, B_ROWS), F32)
        for pr in range(B_STACK):
            h = B_GROUP * gi + 2 * pr + e
            row = jnp.where(block == pr, sink_ref[0:1, h:h + 1], row)
        rows.append(row)
    return rows


def _b_scores_t(q_ref, kvv, bias_scr, gi, b, left, first_blocks):
    kcat = _stack_pair(kvv[:, gi * LANES:(gi + 1) * LANES])
    vcat = _stack_pair(kvv[:, (B_KV_HEADS + gi) * LANES:(B_KV_HEADS + gi + 1) * LANES])
    qs = _b_stack(q_ref, gi) * SCALE
    sc = lax.dot_general(kcat, qs, NT, preferred_element_type=F32) + bias_scr[gi]
    if first_blocks:
        row = lax.broadcasted_iota(jnp.int32, (2 * B_WIN, 1), 0)
        row = jnp.where(row >= B_WIN, row - B_WIN, row)
        sc = jnp.where((row // KB + (b - left)) >= 0, sc, MASKED)
    return kcat, vcat, qs, sc


def _attn_b_fwd(qb, kvx, gate, base, sinks):
    s = qb.shape[0]
    pad = kvx.shape[0] - s
    nb = s // TQ
    left = B_KBLOCKS - 1

    def body(q_ref, kv_ref, g_ref, base_ref, sink_ref, o_ref, u_ref, lse_ref, bias_scr):
        b = pl.program_id(0)

        @pl.when(b == 0)
        def _():
            _b_build_bias(base_ref, bias_scr)

        def step(first_blocks):
            kvv = _window(kv_ref, b, pad, B_WIN, slice(None))
            upper = lax.broadcasted_iota(jnp.int32, (LANES, B_ROWS), 0) < HEAD_DIM
            lse_rows = []
            for gi in range(B_KV_HEADS):
                kcat, vcat, qs, sc = _b_scores_t(q_ref, kvv, bias_scr, gi, b, left, first_blocks)
                sink = _b_sink_rows(sink_ref, gi)
                ps, inv = [], []
                for e in range(2):
                    sh = sc[e * B_WIN:(e + 1) * B_WIN]
                    m = jnp.maximum(jnp.max(sh, axis=0, keepdims=True), sink[e])
                    ex = jnp.exp(sh - m)
                    l = jnp.sum(ex, axis=0, keepdims=True) + jnp.exp(sink[e] - m)
                    ps.append(ex.astype(BF16))
                    inv.append(1.0 / l)
                    lse_rows.append(m + jnp.log(l))
                pt = jnp.concatenate(ps, axis=0)
                ot = lax.dot_general(vcat, pt, TN, preferred_element_type=F32) * jnp.where(upper, inv[0], inv[1])
                ov = ot.T
                for pr in range(B_STACK):
                    pair = B_STACK * gi + pr
                    o_ref[:, pair * LANES:(pair + 1) * LANES] = ov[pr * TQ:(pr + 1) * TQ]
            lse_ref[0] = jnp.concatenate(lse_rows + [jnp.zeros((8 - len(lse_rows), B_ROWS), F32)], axis=0)
            sg, _ = _silu_parts(g_ref[...])
            u_ref[...] = (o_ref[...] * sg).astype(BF16)

        @pl.when(b < left)
        def _():
            step(True)

        @pl.when(b >= left)
        def _():
            step(False)

    row = pl.BlockSpec((TQ, D_MODEL), lambda b: (b, 0))
    return pl.pallas_call(
        body, name="attn_b_fwd", grid=(nb,),
        in_specs=[row, pl.BlockSpec((pad + s, B_KVX), lambda b: (0, 0)), row,
                  pl.BlockSpec((HEADS, 1, B_WIDE), lambda b: (0, 0, 0)), pl.BlockSpec((1, HEADS), lambda b: (0, 0))],
        out_specs=[row, row, pl.BlockSpec((1, 8, B_ROWS), lambda b: (b, 0, 0))],
        out_shape=[jax.ShapeDtypeStruct((s, D_MODEL), F32), jax.ShapeDtypeStruct((s, D_MODEL), BF16),
                   jax.ShapeDtypeStruct((nb, 8, B_ROWS), F32)],
        scratch_shapes=[pltpu.VMEM((B_KV_HEADS, 2 * B_WIN, B_ROWS), F32)],
        compiler_params=_params(("arbitrary",)),
    )(qb, kvx, gate, base, sinks)


def _attn_b_bwd(qb, kvx, gate, o, du, lse, base, sinks):
    s = qb.shape[0]
    pad = kvx.shape[0] - s
    nb = s // TQ
    left = B_KBLOCKS - 1
    half = D_MODEL // 2

    def body(q_ref, kv_ref, g_ref, o_ref, du_ref, lse_ref, base_ref, sink_ref, dz_ref, dkv_ref, dsum_ref,
             dsink_ref, bias_scr, dbias_acc, dkv_acc, dsink_acc):
        b = pl.program_id(0)

        @pl.when(b == 0)
        def _():
            _b_build_bias(base_ref, bias_scr)
            dbias_acc[...] = jnp.zeros_like(dbias_acc)
            dkv_acc[...] = jnp.zeros_like(dkv_acc)
            dsink_acc[...] = jnp.zeros_like(dsink_acc)

        def step(first_blocks):
            kvv = _window(kv_ref, b, pad, B_WIN, slice(None))
            sg, dsg = _silu_parts(g_ref[...])
            duv = du_ref[...]
            ov = o_ref[...]
            do = duv * sg
            dgate = (duv * ov * dsg).astype(BF16)
            dz_ref[2] = dgate[:, :half]
            dz_ref[3] = dgate[:, half:]
            do_o = do * ov
            do_bf = do.astype(BF16)
            lse_all = lse_ref[0]
            dsink_rows = []
            for gi in range(B_KV_HEADS):
                kcat, vcat, qs, sc = _b_scores_t(q_ref, kvv, bias_scr, gi, b, left, first_blocks)
                dos = _b_stack(do_bf, gi)
                doo_t = _b_stack(do_o, gi).T
                delta = (jnp.sum(doo_t[:HEAD_DIM], axis=0, keepdims=True),
                         jnp.sum(doo_t[HEAD_DIM:], axis=0, keepdims=True))
                sink = _b_sink_rows(sink_ref, gi)
                dp = lax.dot_general(vcat, dos, NT, preferred_element_type=F32)
                ps, dss = [], []
                for e in range(2):
                    lse_e = lse_all[2 * gi + e:2 * gi + e + 1]
                    delta_e = delta[e]
                    p = jnp.exp(sc[e * B_WIN:(e + 1) * B_WIN] - lse_e)
                    ps.append(p.astype(BF16))
                    dss.append(p * (dp[e * B_WIN:(e + 1) * B_WIN] - delta_e))
                    dsink_rows.append(-jnp.exp(sink[e] - lse_e) * delta_e)
                ds = jnp.concatenate(dss, axis=0)
                dbias_acc[gi] += ds
                dsb = ds.astype(BF16)
                dq = (lax.dot_general(kcat, dsb, TN, preferred_element_type=F32) * SCALE).T.astype(BF16)
                for pr in range(B_STACK):
                    dz_ref[gi, :, pr * LANES:(pr + 1) * LANES] = dq[pr * TQ:(pr + 1) * TQ]
                dk = _unstack_pair(jnp.dot(dsb, qs, preferred_element_type=F32), B_WIN)
                dv = _unstack_pair(jnp.dot(jnp.concatenate(ps, axis=0), dos, preferred_element_type=F32), B_WIN)
                krows = pl.ds(pl.multiple_of(b * TQ + pad - (B_WIN - TQ), KB), B_WIN)
                dkv_acc[krows, gi * LANES:(gi + 1) * LANES] += dk
                dkv_acc[krows, (B_KV_HEADS + gi) * LANES:(B_KV_HEADS + gi + 1) * LANES] += dv
            dsink_acc[...] += jnp.concatenate(
                dsink_rows + [jnp.zeros((8 - len(dsink_rows), B_ROWS), F32)], axis=0)

        @pl.when(b < left)
        def _():
            step(True)

        @pl.when(b >= left)
        def _():
            step(False)

        @pl.when(b == nb - 1)
        def _():
            lo_s = _lane_lo(s)
            for which in range(2):
                folded = []
                for gi in range(B_KV_HEADS):
                    part = dkv_acc[pad:pad + s, (which * B_KV_HEADS + gi) * LANES:(which * B_KV_HEADS + gi + 1) * LANES]
                    folded.append(part + pltpu.roll(part, HEAD_DIM, 1))
                dkv_ref[:, which * LANES:(which + 1) * LANES] = jnp.where(lo_s, folded[0], folded[1]).astype(BF16)
            lane8 = lax.broadcasted_iota(jnp.int32, dsink_ref.shape, 1)
            tot = jnp.zeros(dsink_ref.shape, F32)
            for h in range(HEADS):
                gi, pr, e = _b_head_place(h)
                dsum_ref[h] = _toeplitz_sum_t(
                    dbias_acc[gi, e * B_WIN:(e + 1) * B_WIN, pr * TQ:(pr + 1) * TQ], B_WIN)
                per_query = dsink_acc[2 * gi + e:2 * gi + e + 1, pr * TQ:(pr + 1) * TQ]
                tot = jnp.where(lane8 == h, jnp.sum(per_query, axis=1, keepdims=True), tot)
            dsink_ref[...] = tot

    row = pl.BlockSpec((TQ, D_MODEL), lambda b: (b, 0))
    base_spec = pl.BlockSpec((HEADS, 1, B_WIDE), lambda b: (0, 0, 0))
    return pl.pallas_call(
        body, name="attn_b_bwd", grid=(nb,),
        in_specs=[row, pl.BlockSpec((pad + s, B_KVX), lambda b: (0, 0)), row, row, row,
                  pl.BlockSpec((1, 8, B_ROWS), lambda b: (b, 0, 0)), base_spec,
                  pl.BlockSpec((1, HEADS), lambda b: (0, 0))],
        out_specs=[pl.BlockSpec((4, TQ, half), lambda b: (0, b, 0)),
                   pl.BlockSpec((s, 2 * LANES), lambda b: (0, 0)), base_spec,
                   pl.BlockSpec((8, LANES), lambda b: (0, 0))],
        out_shape=[jax.ShapeDtypeStruct((4, s, half), BF16), jax.ShapeDtypeStruct((s, 2 * LANES), BF16),
                   jax.ShapeDtypeStruct((HEADS, 1, B_WIDE), F32), jax.ShapeDtypeStruct((8, LANES), F32)],
        scratch_shapes=[pltpu.VMEM((B_KV_HEADS, 2 * B_WIN, B_ROWS), F32),
                        pltpu.VMEM((B_KV_HEADS, 2 * B_WIN, B_ROWS), F32),
                        pltpu.VMEM((pad + s, B_KVX), F32), pltpu.VMEM((8, B_ROWS), F32)],
        compiler_params=_params(("arbitrary",)),
    )(qb, kvx, gate, o, du, lse, base, sinks)


def _t5_bucket(rel):
    nb = T5_BUCKETS // 2
    max_exact = nb // 2
    ret = jnp.where(rel > 0, nb, 0)
    n = jnp.abs(rel)
    nf = jnp.maximum(n, 1).astype(jnp.float32)
    large = max_exact + (jnp.log(nf / max_exact) / math.log(T5_MAX_DIST / max_exact)
                         * (nb - max_exact)).astype(jnp.int32)
    large = jnp.minimum(large, nb - 1)
    return ret + jnp.where(n < max_exact, n, large)


def _a_offset_onehot():
    c = np.arange(A_WIN + TQ)
    dist = A_LEFT_CHUNKS * CHUNK + TQ - 1 - c
    idx = np.clip(dist, -A_REL_CLIP, A_REL_CLIP) + A_REL_CLIP
    onehot = np.zeros((A_WIN + TQ, 2 * A_REL_CLIP + 1), np.float32)
    onehot[c, idx] = 1.0
    return jnp.asarray(onehot)


def _b_offset_onehot():
    c = jnp.arange(B_WIN + TQ, dtype=jnp.int32)
    rel = c - (TQ - 1) - B_LEFT_CHUNKS * CHUNK
    return (_t5_bucket(rel)[:, None] == jnp.arange(T5_BUCKETS)[None, :]).astype(F32)


def _diag_rows(onehot, table):
    rows = jnp.dot(onehot, table.astype(F32), precision=lax.Precision.HIGHEST)
    return rows.T.reshape(HEADS, 1, onehot.shape[0])


def _diag_rows_grad(onehot, ddiag):
    return jnp.dot(ddiag.reshape(HEADS, onehot.shape[0]), onehot, precision=lax.Precision.HIGHEST).T


def _position():
    x, y, c = lax.axis_index("x"), lax.axis_index("y"), lax.axis_index("c")
    chips = [(1 - x, y), (x, 1 - y), (1 - x, 1 - y)]
    return x, y, c, chips


ANY = pl.BlockSpec(memory_space=pl.ANY)


def _allgather_hosted(shards, split):
    n = len(shards)

    def part(ref, t, half):
        if not split[t]:
            return ref
        rows = shards[t].shape[0] // 2
        return ref.at[pl.ds(half * rows, rows)]

    def copies(kind, ins, outs, sems):
        send_sems, recv_sems, pass_send, pass_recv, local_sems = sems
        x, y, c, chips = _position()
        mine = 2 * x + y
        if kind == "local":
            return [pltpu.make_async_copy(ins[t], outs[t].at[mine], local_sems.at[t]) for t in range(n)]
        made = []
        for t in range(n):
            for j, chip in enumerate(chips):
                theirs = 2 * chip[0] + chip[1]
                far = dict(send_sem=send_sems.at[3 * t + j], recv_sem=recv_sems.at[3 * t + j],
                           device_id=(chip[0], chip[1], c), device_id_type=MESH)
                near = dict(send_sem=pass_send.at[3 * t + j], recv_sem=pass_recv.at[3 * t + j],
                            device_id=(x, y, 1 - c), device_id_type=MESH)
                here = part(outs[t].at[theirs], t, c)
                if kind == "send":
                    made.append(pltpu.make_async_remote_copy(
                        src_ref=part(ins[t], t, c), dst_ref=part(outs[t].at[mine], t, c), **far))
                elif kind == "landed":
                    made.append(pltpu.make_async_remote_copy(src_ref=here, dst_ref=here, **far))
                elif not split[t]:
                    made.append(None)
                elif kind == "pass":
                    made.append(pltpu.make_async_remote_copy(src_ref=here, dst_ref=here, **near))
                else:
                    other = part(outs[t].at[theirs], t, 1 - c)
                    made.append(pltpu.make_async_remote_copy(src_ref=other, dst_ref=other, **near))
        return made

    def first(ins, outs, sems):
        for cp in copies("local", ins, outs, sems) + copies("send", ins, outs, sems):
            cp.start()

    def middle(ins, outs, sems):
        for got, cp in zip(copies("landed", ins, outs, sems), copies("pass", ins, outs, sems)):
            got.wait_recv()
            if cp is not None:
                cp.start()

    def last(ins, outs, sems):
        for cp in copies("passed", ins, outs, sems):
            if cp is not None:
                cp.wait_recv()
        for cp in copies("send", ins, outs, sems) + copies("pass", ins, outs, sems):
            if cp is not None:
                cp.wait_send()
        for cp in copies("local", ins, outs, sems):
            cp.wait()

    return _Hosted(shards, [jax.ShapeDtypeStruct((4,) + w.shape, w.dtype) for w in shards],
                   [pltpu.SemaphoreType.DMA((3 * n,))] * 4 + [pltpu.SemaphoreType.DMA((n,))],
                   first, middle, last)


def _allgather_routed(shards):
    n = len(shards)

    def piece(block_ref, t, c, quarter=None):
        half = shards[t].shape[0] // 2
        if quarter is None:
            return block_ref.at[pl.ds(c * half, half)]
        return block_ref.at[pl.ds(c * half + quarter * (half // 2), half // 2)]

    def copies(kind, ins, outs, sems):
        ici_send, ici_recv, pass_send, pass_recv, local_sems = sems
        x, y, c, chips = _position()
        mine = 2 * x + y
        if kind == "local":
            return [pltpu.make_async_copy(ins[t], outs[t].at[mine], local_sems.at[t]) for t in range(n)]
        ids = [2 * chip[0] + chip[1] for chip in chips]
        made = []
        for t in range(n):
            def ici(k, to):
                return dict(send_sem=ici_send.at[4 * t + k], recv_sem=ici_recv.at[4 * t + k],
                            device_id=(chips[to][0], chips[to][1], c), device_id_type=MESH)

            def d2d(k):
                return dict(send_sem=pass_send.at[4 * t + k], recv_sem=pass_recv.at[4 * t + k],
                            device_id=(x, y, 1 - c), device_id_type=MESH)

            def same(ref, where):
                return pltpu.make_async_remote_copy(src_ref=ref, dst_ref=ref, **where)

            if kind == "send":
                for k in range(2):
                    made.append(pltpu.make_async_remote_copy(
                        src_ref=piece(ins[t], t, c), dst_ref=piece(outs[t].at[mine], t, c), **ici(k, k)))
            elif kind == "landed":
                made += [same(piece(outs[t].at[ids[k]], t, c), ici(k, k)) for k in range(2)]
            elif kind == "forward":
                made.append(same(piece(outs[t].at[ids[0]], t, c, 0), ici(2, 1)))
                made.append(same(piece(outs[t].at[ids[1]], t, c, 1), ici(3, 0)))
            elif kind == "arrived":
                made.append(same(piece(outs[t].at[ids[2]], t, c, 0), ici(2, 1)))
                made.append(same(piece(outs[t].at[ids[2]], t, c, 1), ici(3, 0)))
            else:
                core = 1 - c if kind == "passed" else c
                if kind in ("pass halves", "passed"):
                    made += [same(piece(outs[t].at[ids[k]], t, core), d2d(k)) for k in range(2)]
                if kind in ("pass quarters", "passed"):
                    made += [same(piece(outs[t].at[ids[2]], t, core, k), d2d(2 + k)) for k in range(2)]
        return made

    def first(ins, outs, sems):
        for cp in copies("local", ins, outs, sems) + copies("send", ins, outs, sems):
            cp.start()

    def middle(ins, outs, sems):
        for got, onward, near in zip(copies("landed", ins, outs, sems), copies("forward", ins, outs, sems),
                                     copies("pass halves", ins, outs, sems)):
            got.wait_recv()
            near.start()
            onward.start()

    def last(ins, outs, sems):
        quarters = copies("pass quarters", ins, outs, sems)
        for got, near in zip(copies("arrived", ins, outs, sems), quarters):
            got.wait_recv()
            near.start()
        for cp in copies("passed", ins, outs, sems):
            cp.wait_recv()
        for cp in (copies("send", ins, outs, sems) + copies("forward", ins, outs, sems)
                   + copies("pass halves", ins, outs, sems) + quarters):
            cp.wait_send()
        for cp in copies("local", ins, outs, sems):
            cp.wait()

    return _Hosted(shards, [jax.ShapeDtypeStruct((4,) + w.shape, w.dtype) for w in shards],
                   [pltpu.SemaphoreType.DMA((4 * n,))] * 4 + [pltpu.SemaphoreType.DMA((n,))],
                   first, middle, last)


def _scatter_hosted(grads):
    n = len(grads)

    def copies(ins, outs, sems):
        send_sems, recv_sems = sems
        x, y, c, chips = _position()
        return [pltpu.make_async_remote_copy(
            src_ref=ins[t].at[2 * chip[0] + chip[1]], dst_ref=outs[t].at[j],
            send_sem=send_sems.at[3 * t + j], recv_sem=recv_sems.at[3 * t + j],
            device_id=(chip[0], chip[1], c), device_id_type=MESH)
            for t in range(n) for j, chip in enumerate(chips)]

    def first(ins, outs, sems):
        for cp in copies(ins, outs, sems):
            cp.start()

    def last(ins, outs, sems):
        for cp in copies(ins, outs, sems):
            cp.wait()

    return _Hosted(grads, [jax.ShapeDtypeStruct((3,) + g.shape[1:], g.dtype) for g in grads],
                   [pltpu.SemaphoreType.DMA((3 * n,))] * 2, first, None, last)


def _scatter_on_sequencer(name, grad):
    src = jax.new_ref(grad, memory_space=pltpu.MemorySpace.HBM)
    dst = jax.empty_ref(jax.ShapeDtypeStruct((3,) + grad.shape[1:], grad.dtype),
                        memory_space=pltpu.MemorySpace.HBM)

    @pl.kernel(mesh=plsc.ScalarSubcoreMesh(axis_name="sequencer", num_cores=1), name=name,
               scratch_types=(pltpu.SemaphoreType.DMA((3,)), pltpu.SemaphoreType.DMA((3,))),
               compiler_params=pltpu.CompilerParams(collective_id=0))
    def launch(send_sems, recv_sems):
        x, y, c, chips = _position()
        barrier = pltpu.get_barrier_semaphore()
        for chip in chips:
            pl.semaphore_signal(barrier, inc=1, device_id=(chip[0], chip[1], c), device_id_type=MESH)
        pl.semaphore_wait(barrier, len(chips))
        copies = [pltpu.make_async_remote_copy(
            src_ref=src.at[2 * chip[0] + chip[1]], dst_ref=dst.at[j], send_sem=send_sems.at[j],
            recv_sem=recv_sems.at[j], device_id=(chip[0], chip[1], c), device_id_type=MESH)
            for j, chip in enumerate(chips)]
        for cp in copies:
            cp.start()
        for cp in copies:
            cp.wait()

    launch()
    return dst[...]


def _run_on_sequencer(name, hosted):
    ins = [jax.new_ref(a, memory_space=pltpu.MemorySpace.HBM) for a in hosted.inputs]
    outs = [jax.empty_ref(shape, memory_space=pltpu.MemorySpace.HBM) for shape in hosted.out_shapes]

    @pl.kernel(mesh=plsc.ScalarSubcoreMesh(axis_name="sequencer", num_cores=1), name=name,
               scratch_types=tuple(hosted.sems), compiler_params=pltpu.CompilerParams(collective_id=1))
    def launch(*sems):
        x, y, c, chips = _position()
        peers = [(chip[0], chip[1], c) for chip in chips[:2]] + [(x, y, 1 - c)]
        barrier = pltpu.get_barrier_semaphore()
        for peer in peers:
            pl.semaphore_signal(barrier, inc=1, device_id=peer, device_id_type=MESH)
        pl.semaphore_wait(barrier, len(peers))
        hosted.first(ins, outs, sems)
        hosted.middle(ins, outs, sems)
        hosted.last(ins, outs, sems)

    launch()
    return [o[...] for o in outs]


def _run_alone(name, hosted):
    n_in = len(hosted.inputs)
    n_out = len(hosted.out_shapes)

    def body(*refs):
        ins, outs, sems = refs[:n_in], refs[n_in:n_in + n_out], refs[n_in + n_out:]
        hosted.first(ins, outs, sems)
        if hosted.middle is not None:
            hosted.middle(ins, outs, sems)
        hosted.last(ins, outs, sems)

    return pl.pallas_call(
        body, name=name, in_specs=[ANY] * n_in, out_specs=[ANY] * n_out, out_shape=hosted.out_shapes,
        scratch_shapes=hosted.sems)(*hosted.inputs)


def _swap_with_sibling(blocks):
    n = len(blocks)

    def body(*refs):
        ins, outs = refs[:n], refs[n:2 * n]
        send_sems, recv_sems = refs[2 * n:]
        x, y, c, _ = _position()
        sends = [pltpu.make_async_remote_copy(
            src_ref=ins[t], dst_ref=outs[t], send_sem=send_sems.at[t], recv_sem=recv_sems.at[t],
            device_id=(x, y, 1 - c), device_id_type=MESH) for t in range(n)]
        for cp in sends:
            cp.start()
        for cp in sends:
            cp.wait()

    return pl.pallas_call(
        body, name="swap_with_sibling",
        in_specs=[ANY] * n, out_specs=[ANY] * n,
        out_shape=[jax.ShapeDtypeStruct(b.shape, b.dtype) for b in blocks],
        scratch_shapes=[pltpu.SemaphoreType.DMA((n,))] * 2,
    )(*blocks)


def _small_step(partials, extras, ws, ms, vs, shard_of):
    n = len(partials)
    terms = list(partials) + list(extras)
    nt = len(terms)
    rows = [t for t in range(nt) if terms[t].shape[0] == 1]
    mats = [t for t in range(nt) if terms[t].shape[0] != 1]
    row_block = (8, max(terms[t].shape[1] for t in rows))
    assert len(rows) <= row_block[0]
    sent = [row_block] + [terms[t].shape for t in mats]

    def body(*refs):
        ins, refs = refs[:nt], refs[nt:]
        w_refs, refs = refs[:n], refs[n:]
        m_refs, refs = refs[:n], refs[n:]
        v_refs, refs = refs[:n], refs[n:]
        outs, refs = refs[:4 * n + nt - n], refs[4 * n + nt - n:]
        slots, (packed, send_sems, recv_sems) = refs[:len(sent)], refs[len(sent):]
        x, y, c, _ = _position()
        me = 4 * x + 2 * y + c
        packed[...] = jnp.zeros_like(packed)
        for i, t in enumerate(rows):
            packed[i:i + 1, 0:terms[t].shape[1]] = ins[t][...]
        sources = [packed] + [ins[t] for t in mats]
        sends = []
        for j, src in enumerate(sources):
            slots[j][me] = src[...]
            for k in range(1, 8):
                peer = (x ^ (k >> 2), y ^ ((k >> 1) & 1), c ^ (k & 1))
                sends.append(pltpu.make_async_remote_copy(
                    src_ref=src, dst_ref=slots[j].at[me], send_sem=send_sems.at[7 * j + k - 1],
                    recv_sem=recv_sems.at[7 * j + k - 1], device_id=peer, device_id_type=MESH))
        for cp in sends:
            cp.start()
        for j, src in enumerate(sources):
            for k in range(1, 8):
                pltpu.make_async_remote_copy(
                    src_ref=src, dst_ref=slots[j].at[me ^ k], send_sem=send_sems.at[7 * j + k - 1],
                    recv_sem=recv_sems.at[7 * j + k - 1], device_id=(x, y, c), device_id_type=MESH).wait_recv()
        for cp in sends:
            cp.wait_send()
        sums = []
        for j in range(len(sources)):
            g = slots[j][0]
            for dev in range(1, 8):
                g = g + slots[j][dev]
            sums.append(g)
        chip = 2 * x + y
        for t in range(nt):
            if t in rows:
                i = rows.index(t)
                g = sums[0][i:i + 1, 0:terms[t].shape[1]]
            else:
                g = sums[1 + mats.index(t)]
            if t >= n:
                outs[4 * n + t - n][...] = g
                continue
            if shard_of[t]:
                width = ws[t].shape[-1]
                mine = jnp.zeros(ws[t].shape, F32)
                for s in range(4):
                    mine = jnp.where(chip == s, g[:, s * width:(s + 1) * width], mine)
                g = mine
            delta, mn, vn = _adamw_math(w_refs[t][...], g, m_refs[t][...], v_refs[t][...])
            outs[4 * t][...] = g
            outs[4 * t + 1][...] = delta
            outs[4 * t + 2][...] = mn
            outs[4 * t + 3][...] = vn

    vmem = pl.BlockSpec(memory_space=pltpu.VMEM)
    out_shapes = []
    for t in range(n):
        out_shapes += [jax.ShapeDtypeStruct(ws[t].shape, F32)] * 4
    out_shapes += [jax.ShapeDtypeStruct(a.shape, F32) for a in extras]
    out_shapes += [jax.ShapeDtypeStruct((8,) + tuple(shape), F32) for shape in sent]
    res = pl.pallas_call(
        body, name="small_step",
        in_specs=[vmem] * (nt + 3 * n), out_specs=[vmem] * len(out_shapes), out_shape=out_shapes,
        scratch_shapes=[pltpu.VMEM(row_block, F32)] + [pltpu.SemaphoreType.DMA((7 * len(sent),))] * 2,
    )(*terms, *ws, *ms, *vs)
    return [res[4 * t:4 * t + 4] for t in range(n)], res[4 * n:4 * n + nt - n]


def _adamw_math(w, g, m, v):
    m = ADAM_B1 * m + (1.0 - ADAM_B1) * g
    v = ADAM_B2 * v + (1.0 - ADAM_B2) * (g * g)
    m_hat = m / (1.0 - ADAM_B1 ** ADAM_STEP)
    v_hat = v / (1.0 - ADAM_B2 ** ADAM_STEP)
    delta = -ADAM_LR * (m_hat / (jnp.sqrt(v_hat) + ADAM_EPS) + ADAM_WD * w)
    return delta, m, v


def _row_tile(rows):
    return 256 if rows % 256 == 0 else rows


def _sum_partials(name, own, recv, chip):
    rows, cols = own.shape[1:]
    tr = _row_tile(rows)

    def body(chip_ref, own_ref, recv_ref, o_ref):
        acc = own_ref[...]
        for j in range(3):
            acc = acc + recv_ref[j].astype(F32)
        o_ref[...] = acc

    return pl.pallas_call(
        body, name=name,
        grid_spec=pltpu.PrefetchScalarGridSpec(
            num_scalar_prefetch=1, grid=(rows // tr,),
            in_specs=[pl.BlockSpec((None, tr, cols), lambda i, chip_ref: (chip_ref[0], i, 0)),
                      pl.BlockSpec((3, tr, cols), lambda i, chip_ref: (0, i, 0))],
            out_specs=pl.BlockSpec((tr, cols), lambda i, chip_ref: (i, 0))),
        out_shape=jax.ShapeDtypeStruct((rows, cols), F32),
        compiler_params=_params(("parallel",)),
    )(chip.reshape(1).astype(jnp.int32), own, recv)


def _adamw(name, w, m, v, g_parts):
    rows, cols = w.shape
    tr = _row_tile(rows)
    n = len(g_parts)

    def body(w_ref, m_ref, v_ref, *refs):
        g_refs = refs[:n]
        go_ref, d_ref, mo_ref, vo_ref = refs[n:]
        g = g_refs[0][...]
        for r in g_refs[1:]:
            g = g + r[...]
        delta, mn, vn = _adamw_math(w_ref[...], g, m_ref[...], v_ref[...])
        go_ref[...] = g
        d_ref[...] = delta
        mo_ref[...] = mn
        vo_ref[...] = vn

    spec = pl.BlockSpec((tr, cols), lambda i: (i, 0))
    return pl.pallas_call(
        body, name=name, grid=(rows // tr,),
        in_specs=[spec] * (3 + n), out_specs=[spec] * 4,
        out_shape=[jax.ShapeDtypeStruct((rows, cols), F32)] * 4,
        compiler_params=_params(("parallel",)),
    )(w, m, v, *g_parts)


def _local_step(x, target, ga, wa_in, rel_bias, later_shards, gk, t5, gb, sinks, gf):
    s, d = x.shape
    tm = min(TM_DENSE, s)
    nt = s // tm
    half = d // 2
    row = pl.BlockSpec((tm, d), lambda i: (i, 0))
    whole = lambda shape: pl.BlockSpec(shape, lambda *_: (0,) * len(shape))

    n1, = _norm_fwd("norm_a", x, ga)
    zqkv = _matmul("proj_a_qkv", n1, wa_in, dims=NN, grid=(3, nt + 1), zero_axis=1,
                   a_spec=pl.BlockSpec((tm, d), lambda j, i: (jnp.maximum(i - 1, 0), 0)),
                   b_spec=pl.BlockSpec((None, d, d), lambda j, i: (j, 0, 0)),
                   o_spec=pl.BlockSpec((None, tm, d), lambda j, i: (j, i, 0)),
                   out_shape=(3, tm + s, d), out_dtype=BF16)
    gate_a = _matmul("proj_a_gate", n1, wa_in, dims=NN, grid=(nt,),
                     a_spec=row, b_spec=pl.BlockSpec((None, d, d), lambda i: (3, 0, 0)), o_spec=row,
                     out_shape=(s, d), out_dtype=F32)
    onehot_a = _a_offset_onehot()
    diag_a = _diag_rows(onehot_a, rel_bias)
    (o_a, u_a, lse_a), gathered = _attn_a_fwd(
        zqkv, gate_a, diag_a, hosted=_allgather_routed(later_shards))
    wa_out, wkv, wb_in, wb_out = gathered
    wa_out = wa_out.reshape(d, d)
    wkv = wkv.reshape(d, -1)
    wb_out = wb_out.reshape(d, d)
    h1 = _matmul("out_a", u_a, wa_out, dims=NN, grid=(nt,), a_spec=row, b_spec=whole((d, d)), o_spec=row,
                 out_shape=(s, d), out_dtype=F32, resid=x, resid_spec=row)

    nk, n2 = _norm_fwd("norm_kv_b", h1, jnp.concatenate([gk, gb], axis=0))
    kvw = wkv.shape[1]
    wkv_x = jnp.concatenate([wkv[:, (i // 2) * HEAD_DIM:(i // 2 + 1) * HEAD_DIM] for i in range(8)], axis=1)
    kvx = _matmul("proj_kv", nk, wkv_x, dims=NN, grid=(nt + 1,), zero_axis=0,
                  a_spec=pl.BlockSpec((tm, d), lambda i: (jnp.maximum(i - 1, 0), 0)), b_spec=whole((d, B_KVX)),
                  o_spec=pl.BlockSpec((tm, B_KVX), lambda i: (i, 0)), out_shape=(tm + s, B_KVX), out_dtype=BF16)
    qb = _matmul("proj_b_q", n2, wb_in, dims=NN, grid=(2, nt),
                 a_spec=pl.BlockSpec((tm, d), lambda j, i: (i, 0)),
                 b_spec=pl.BlockSpec((None, d, half), lambda j, i: (j, 0, 0)),
                 o_spec=pl.BlockSpec((tm, half), lambda j, i: (i, j)), out_shape=(s, d), out_dtype=BF16)
    gate_b = _matmul("proj_b_gate", n2, wb_in, dims=NN, grid=(2, nt),
                     a_spec=pl.BlockSpec((tm, d), lambda j, i: (i, 0)),
                     b_spec=pl.BlockSpec((None, d, half), lambda j, i: (2 + j, 0, 0)),
                     o_spec=pl.BlockSpec((tm, half), lambda j, i: (i, j)), out_shape=(s, d), out_dtype=F32)
    onehot_b = _b_offset_onehot()
    base_b = jnp.roll(_diag_rows(onehot_b, t5)[..., ::-1], TQ, axis=-1)
    o_b, u_b, lse_b = _attn_b_fwd(qb, kvx, gate_b, base_b, sinks)
    h2 = _matmul("out_b", u_b, wb_out, dims=NN, grid=(nt,), a_spec=row, b_spec=whole((d, d)), o_spec=row,
                 out_shape=(s, d), out_dtype=F32, resid=h1, resid_spec=row)

    dh2, loss, d_gf = _loss_head(h2, target, gf)

    du_b = _matmul("dout_b", dh2, wb_out, dims=NT, grid=(nt,), a_spec=row, b_spec=whole((d, d)), o_spec=row,
                   out_shape=(s, d), out_dtype=F32)
    d_wb_out = _matmul("dw_out_b", u_b, dh2, dims=TN, grid=(2,),
                       a_spec=whole((s, d)), b_spec=pl.BlockSpec((s, half), lambda j: (0, j)),
                       o_spec=pl.BlockSpec((d, half), lambda j: (0, j)),
                       out_shape=(d, d), out_dtype=F32, also_bf16=True)
    dz_b, dkv, dsum_b, dsinks = _attn_b_bwd(qb, kvx, gate_b, o_b, du_b, lse_b, base_b, sinks)
    ddiag_b = jnp.roll(dsum_b[..., ::-1], -1, axis=-1)
    d_wb_in = _matmul("dw_in_b", n2, dz_b, dims=TN, grid=(4,),
                      a_spec=whole((s, d)), b_spec=pl.BlockSpec((None, s, half), lambda j: (j, 0, 0)),
                      o_spec=pl.BlockSpec((None, d, half), lambda j: (j, 0, 0)),
                      out_shape=(4, d, half), out_dtype=F32, also_bf16=True)
    d_wkv = _matmul("dw_kv", nk, dkv, dims=TN, grid=(1,),
                    a_spec=whole((s, d)), b_spec=whole((s, kvw)), o_spec=whole((d, kvw)),
                    out_shape=(d, kvw), out_dtype=F32, also_bf16=True)
    dh1, d_gkb = _proj_norm_bwd("dproj_kv_b", h1, dh2, jnp.concatenate([gk, gb], axis=0),
                                [(dkv[None], wkv[None]), (dz_b, wb_in)])

    du_a = _matmul("dout_a", dh1, wa_out, dims=NT, grid=(nt,), a_spec=row, b_spec=whole((d, d)), o_spec=row,
                   out_shape=(s, d), out_dtype=F32)
    d_wa_out = _matmul("dw_out_a", u_a, dh1, dims=TN, grid=(2,),
                       a_spec=whole((s, d)), b_spec=pl.BlockSpec((s, half), lambda j: (0, j)),
                       o_spec=pl.BlockSpec((d, half), lambda j: (0, j)),
                       out_shape=(d, d), out_dtype=F32, also_bf16=True)
    early = dict(a_w_out=[g.reshape(4, d // 4, d) for g in d_wa_out],
                 kv_w=[g.reshape(4, d // 4, kvw) for g in d_wkv], b_w_in=list(d_wb_in),
                 b_w_out=[g.reshape(4, d // 4, d) for g in d_wb_out])
    (dz_a, ddiag_a), early_recv = _attn_a_bwd(
        zqkv, gate_a, o_a, du_a, lse_a, diag_a, hosted=_scatter_hosted([early[n][1] for n in early]))
    d_wa_in = _matmul("dw_in_a", n1, dz_a, dims=TN, grid=(4, 2),
                      a_spec=whole((s, d)), b_spec=pl.BlockSpec((None, s, half), lambda j, h: (j, 0, h)),
                      o_spec=pl.BlockSpec((None, d, half), lambda j, h: (j, 0, h)),
                      out_shape=(4, d, d), out_dtype=F32, also_bf16=True)
    late_recv = [_scatter_on_sequencer("scatter_a_w_in", d_wa_in[1])]
    grad_x, d_ga = _proj_norm_bwd("dproj_a", x, dh1, ga, [(dz_a, wa_in)])

    small = dict(a_norm=d_ga, kv_norm=d_gkb[0:1], b_norm=d_gkb[1:2], b_sinks=dsinks[0:1, :HEADS], final_norm=d_gf)
    small["by_offset"] = dict(a_rel_bias=(onehot_a, ddiag_a.reshape(HEADS, -1)),
                              t5_bias=(onehot_b, ddiag_b.reshape(HEADS, -1)))
    own = dict(a_w_in=d_wa_in[0], **{n: early[n][0] for n in early})
    received = dict(a_w_in=late_recv[0], **dict(zip(early, early_recv)))
    return loss, grad_x, small, own, received


SMALL = ("a_norm", "kv_norm", "b_norm", "b_sinks", "final_norm")
TABLES = ("a_rel_bias", "t5_bias")
BIG = ("a_w_in", "a_w_out", "kv_w", "b_w_in", "b_w_out")
ORDER = ("a_norm", "a_w_in", "a_rel_bias", "a_w_out", "kv_norm", "kv_w", "t5_bias", "b_norm", "b_w_in",
         "b_sinks", "b_w_out", "final_norm")


def kernel(x, a_norm, a_w_in, a_rel_bias, a_w_out, kv_norm, kv_w, t5_bias, b_norm, b_w_in, b_sinks, b_w_out, final_norm, loss_target, m_a_norm, m_a_w_in, m_a_rel_bias, m_a_w_out, m_kv_norm, m_kv_w, m_t5_bias, m_b_norm, m_b_w_in, m_b_sinks, m_b_w_out, m_final_norm, v_a_norm, v_a_w_in, v_a_rel_bias, v_a_w_out, v_kv_norm, v_kv_w, v_t5_bias, v_b_norm, v_b_w_in, v_b_sinks, v_b_w_out, v_final_norm):
    w = dict(a_norm=a_norm, a_w_in=a_w_in, a_rel_bias=a_rel_bias, a_w_out=a_w_out, kv_norm=kv_norm, kv_w=kv_w,
             t5_bias=t5_bias, b_norm=b_norm, b_w_in=b_w_in, b_sinks=b_sinks, b_w_out=b_w_out,
             final_norm=final_norm)
    m = dict(a_norm=m_a_norm, a_w_in=m_a_w_in, a_rel_bias=m_a_rel_bias, a_w_out=m_a_w_out, kv_norm=m_kv_norm,
             kv_w=m_kv_w, t5_bias=m_t5_bias, b_norm=m_b_norm, b_w_in=m_b_w_in, b_sinks=m_b_sinks,
             b_w_out=m_b_w_out, final_norm=m_final_norm)
    v = dict(a_norm=v_a_norm, a_w_in=v_a_w_in, a_rel_bias=v_a_rel_bias, a_w_out=v_a_w_out, kv_norm=v_kv_norm,
             kv_w=v_kv_w, t5_bias=v_t5_bias, b_norm=v_b_norm, b_w_in=v_b_w_in, b_sinks=v_b_sinks,
             b_w_out=v_b_w_out, final_norm=v_final_norm)
    d = D_MODEL
    chip = 2 * lax.axis_index("x") + lax.axis_index("y")

    shard2d = dict(a_w_in=a_w_in[0], a_w_out=a_w_out[0], kv_w=kv_w, b_w_in=b_w_in[0], b_w_out=b_w_out[0])

    wa_in, = _run_on_sequencer("allgather_first", _allgather_routed([shard2d["a_w_in"].astype(BF16)]))
    ga, = _run_alone("allgather_norm", _allgather_hosted([a_norm], [False]))
    ga = ga.reshape(1, d)

    loss, grad_x, small, own, received = _local_step(
        x[0], loss_target[0], ga, wa_in, a_rel_bias[0], [shard2d[n].astype(BF16) for n in BIG[1:]],
        kv_norm.reshape(1, d), t5_bias, b_norm, b_sinks, final_norm.reshape(1, d))

    out = {}
    as2d = lambda a: a.reshape(-1, a.shape[-1])
    small_res, (loss_sum, *offset_sums) = _small_step(
        [small[n] for n in SMALL], [loss] + [small["by_offset"][n][1] for n in TABLES],
        [as2d(w[n]) for n in SMALL], [as2d(m[n]) for n in SMALL], [as2d(v[n]) for n in SMALL],
        [n == "a_norm" for n in SMALL])
    for n, res in zip(SMALL, small_res):
        out[n] = [r.reshape(w[n].shape) for r in res]
    loss_out = loss_sum.reshape(())
    for n, summed in zip(TABLES, offset_sums):
        grad = _diag_rows_grad(small["by_offset"][n][0], summed)
        res = _adamw("adamw_" + n, as2d(w[n]), as2d(m[n]), as2d(v[n]), [grad])
        out[n] = [r.reshape(w[n].shape) for r in res]

    core_sums = [_sum_partials("sum_" + n, own[n], received[n], chip) for n in BIG]
    sibling_sums = _swap_with_sibling(core_sums)

    for n, mine, theirs in zip(BIG, core_sums, sibling_sums):
        res = _adamw("adamw_" + n, shard2d[n], m[n].reshape(shard2d[n].shape), v[n].reshape(shard2d[n].shape),
                     [mine, theirs])
        out[n] = [r.reshape(w[n].shape) for r in res]

    grads = [out[n][0] for n in ORDER]
    deltas = [out[n][1] for n in ORDER]
    new_m = [out[n][2] for n in ORDER]
    new_v = [out[n][3] for n in ORDER]
    return (loss_out, grad_x[None], *grads, *deltas, *new_m, *new_v)
```

```python
import functools
import math

import jax
import jax.numpy as jnp
import numpy as np
from jax import lax
from jax.experimental import pallas as pl
from jax.experimental.pallas import tpu as pltpu
from jax.experimental.pallas import tpu_sc as plsc

F32 = jnp.float32
BF16 = jnp.bfloat16
MESH = pl.DeviceIdType.MESH

D_MODEL = 1024
HEADS = 16
HEAD_DIM = 64
CHUNK = 64
RMS_EPS = 1e-6
SCALE = HEAD_DIM ** -0.5
A_LEFT_CHUNKS = 8
A_REL_CLIP = 256
B_LEFT_CHUNKS = 2
B_KV_HEADS = 2
B_GROUP = HEADS // B_KV_HEADS
T5_BUCKETS = 32
T5_MAX_DIST = 128
ADAM_LR = 0.001
ADAM_B1 = 0.9
ADAM_B2 = 0.999
ADAM_EPS = 1e-08
ADAM_WD = 0.01
ADAM_STEP = 10

MASKED = -1e30
LANES = 128
TQ = 128
A_PAIRS = 2
A_PAIRS_FWD = 4
KB = 128
A_KBLOCKS = A_LEFT_CHUNKS * CHUNK // KB + 1
B_KBLOCKS = B_LEFT_CHUNKS * CHUNK // KB + 1
A_WIN = A_KBLOCKS * KB
B_WIN = B_KBLOCKS * KB
TM = 512
TM_DENSE = 1024
TM_PARTS = 512
VMEM_LIMIT = 56 * 1024 * 1024

NT = (((1,), (1,)), ((), ()))
TN = (((0,), (0,)), ((), ()))
NN = (((1,), (0,)), ((), ()))


def _params(sem=None):
    return pltpu.CompilerParams(dimension_semantics=sem, vmem_limit_bytes=VMEM_LIMIT)


class _Hosted:
    def __init__(self, inputs, out_shapes, sems, first, middle, last):
        self.inputs, self.out_shapes, self.sems = list(inputs), list(out_shapes), list(sems)
        self.first, self.middle, self.last = first, middle, last


def _call(body, *, name, grid, in_specs, out_specs, out_shape, args, scratch_shapes=(), sem=None, hosted=None):
    in_specs, out_specs, out_shape = list(in_specs), list(out_specs), list(out_shape)
    scratch_shapes = list(scratch_shapes)
    if hosted is None:
        out = pl.pallas_call(
            body, name=name, grid=grid, in_specs=in_specs, out_specs=out_specs, out_shape=out_shape,
            scratch_shapes=scratch_shapes, compiler_params=_params(sem))(*args)
        return list(out), []
    n_in, n_out, n_scr = len(in_specs), len(out_shape), len(scratch_shapes)
    h_in, h_out = len(hosted.inputs), len(hosted.out_shapes)
    total = int(np.prod(grid)) if grid else 1

    def wrapped(*refs):
        ins, refs = refs[:n_in], refs[n_in:]
        h_ins, refs = refs[:h_in], refs[h_in:]
        outs, refs = refs[:n_out], refs[n_out:]
        h_outs, refs = refs[:h_out], refs[h_out:]
        scr, h_sems = refs[:n_scr], refs[n_scr:]
        step = 0
        for axis, size in enumerate(grid):
            step = step * size + pl.program_id(axis)

        @pl.when(step == 0)
        def _():
            hosted.first(h_ins, h_outs, h_sems)

        body(*ins, *outs, *scr)
        if hosted.middle is not None:
            @pl.when(step == total // 2)
            def _():
                hosted.middle(h_ins, h_outs, h_sems)

        @pl.when(step == total - 1)
        def _():
            hosted.last(h_ins, h_outs, h_sems)

    out = pl.pallas_call(
        wrapped, name=name, grid=grid, in_specs=in_specs + [ANY] * h_in, out_specs=out_specs + [ANY] * h_out,
        out_shape=out_shape + hosted.out_shapes, scratch_shapes=scratch_shapes + hosted.sems,
        compiler_params=_params(("arbitrary",) * len(grid)))(*args, *hosted.inputs)
    return list(out[:n_out]), list(out[n_out:])


def _matmul(name, a, b, *, dims, grid, a_spec, b_spec, o_spec, out_shape, out_dtype,
            parts=1, resid=None, resid_spec=None, also_bf16=False, hosted=None, zero_axis=None):
    def body(*refs):
        if zero_axis is None:
            product(*refs)
        else:
            @pl.when(pl.program_id(zero_axis) == 0)
            def _():
                refs[2][...] = jnp.zeros_like(refs[2])

            @pl.when(pl.program_id(zero_axis) > 0)
            def _():
                product(*refs)

    def product(*refs):
        a_ref, b_ref = refs[:2]
        r_ref = refs[2] if resid is not None else None
        o_ref = refs[3] if resid is not None else refs[2]
        if parts == 1:
            prod = lax.dot_general(a_ref[...].astype(BF16), b_ref[...].astype(BF16), dims,
                                   preferred_element_type=F32)
        else:
            prod = None
            for part in range(parts):
                term = lax.dot_general(a_ref[part].astype(BF16), b_ref[part].astype(BF16), dims,
                                       preferred_element_type=F32)
                prod = term if prod is None else prod + term
        if resid is not None:
            prod = r_ref[...] + prod
        o_ref[...] = prod.astype(out_dtype)
        if also_bf16:
            refs[-1][...] = prod.astype(BF16)

    in_specs = [a_spec, b_spec]
    args = [a, b]
    if resid is not None:
        in_specs.append(resid_spec)
        args.append(resid)
    sem = ["parallel"] * len(grid)
    out_specs = [o_spec]
    out_shapes = [jax.ShapeDtypeStruct(out_shape, out_dtype)]
    if also_bf16:
        out_specs.append(o_spec)
        out_shapes.append(jax.ShapeDtypeStruct(out_shape, BF16))
    out, extra = _call(body, name=name, grid=grid, in_specs=in_specs, out_specs=out_specs, out_shape=out_shapes,
                       args=args, sem=tuple(sem), hosted=hosted)
    res = out[0] if not also_bf16 else tuple(out)
    return res if hosted is None else (res, extra)


def _rms_rows(x):
    return lax.rsqrt(jnp.mean(x * x, axis=-1, keepdims=True) + RMS_EPS)


def _norm_fwd(name, x, gains):
    s, d = x.shape
    n = gains.shape[0]

    def body(x_ref, g_ref, *o_refs):
        xv = x_ref[...]
        xh = xv * _rms_rows(xv)
        for i in range(n):
            o_refs[i][...] = (xh * g_ref[i:i + 1, :]).astype(BF16)

    row = pl.BlockSpec((TM, d), lambda i: (i, 0))
    return pl.pallas_call(
        body, name=name, grid=(s // TM,),
        in_specs=[row, pl.BlockSpec((n, d), lambda i: (0, 0))],
        out_specs=[row] * n,
        out_shape=[jax.ShapeDtypeStruct((s, d), BF16)] * n,
        compiler_params=_params(("parallel",)),
    )(x, gains)


def _proj_norm_bwd(name, x, dres, gains, branches):
    s, d = x.shape
    n = len(branches)
    tm = min(TM_PARTS, s)

    def body(x_ref, r_ref, g_ref, *refs):
        ab_refs, dx_ref, dg_ref = refs[:2 * n], refs[2 * n], refs[2 * n + 1]
        i = pl.program_id(0)
        xv = x_ref[...]
        r = _rms_rows(xv)
        xh = xv * r

        @pl.when(i == 0)
        def _():
            dg_ref[...] = jnp.zeros_like(dg_ref)

        a = None
        for j in range(n):
            a_ref, b_ref = ab_refs[2 * j], ab_refs[2 * j + 1]
            dn = None
            for part in range(a_ref.shape[0]):
                term = lax.dot_general(a_ref[part], b_ref[part], NT, preferred_element_type=F32)
                dn = term if dn is None else dn + term
            t = dn * g_ref[j:j + 1, :]
            a = t if a is None else a + t
            dg_ref[j:j + 1, :] += jnp.sum(dn * xh, axis=0, keepdims=True)
        dx_ref[...] = r_ref[...] + r * (a - xh * jnp.mean(xh * a, axis=-1, keepdims=True))

    row = pl.BlockSpec((tm, d), lambda i: (i, 0))
    small = pl.BlockSpec((n, d), lambda i: (0, 0))
    ab_specs, ab_args = [], []
    for a, b in branches:
        ab_specs += [pl.BlockSpec((a.shape[0], tm, a.shape[2]), lambda i: (0, i, 0)),
                     pl.BlockSpec(b.shape, lambda i: (0, 0, 0))]
        ab_args += [a, b]
    return pl.pallas_call(
        body, name=name, grid=(s // tm,),
        in_specs=[row, row, small] + ab_specs,
        out_specs=[row, small],
        out_shape=[jax.ShapeDtypeStruct((s, d), F32), jax.ShapeDtypeStruct((n, d), F32)],
        compiler_params=_params(("arbitrary",)),
    )(x, dres, gains, *ab_args)


def _loss_head(h2, target, gain):
    s, d = h2.shape

    def body(h_ref, t_ref, g_ref, dh_ref, loss_ref, dg_ref):
        i = pl.program_id(0)
        hv = h_ref[...]
        r = _rms_rows(hv)
        hh = hv * r
        g = g_ref[...]
        err = hh * g - t_ref[...]
        part = 0.5 * jnp.sum(jnp.sum(err * err, axis=-1, keepdims=True) * (1.0 / d), axis=0, keepdims=True)
        dy = err * (1.0 / d)
        a = dy * g
        dh_ref[...] = r * (a - hh * jnp.mean(hh * a, axis=-1, keepdims=True))
        dg = jnp.sum(dy * hh, axis=0, keepdims=True)

        @pl.when(i == 0)
        def _():
            loss_ref[...] = part
            dg_ref[...] = dg

        @pl.when(i > 0)
        def _():
            loss_ref[...] += part
            dg_ref[...] += dg

    row = pl.BlockSpec((TM, d), lambda i: (i, 0))
    return pl.pallas_call(
        body, name="loss_head", grid=(s // TM,),
        in_specs=[row, row, pl.BlockSpec((1, d), lambda i: (0, 0))],
        out_specs=[row, pl.BlockSpec((1, 1), lambda i: (0, 0)), pl.BlockSpec((1, d), lambda i: (0, 0))],
        out_shape=[jax.ShapeDtypeStruct((s, d), F32), jax.ShapeDtypeStruct((1, 1), F32),
                   jax.ShapeDtypeStruct((1, d), F32)],
        compiler_params=_params(("arbitrary",)),
    )(h2, target, gain)


def _silu_parts(g):
    sig = jax.nn.sigmoid(g)
    return g * sig, sig * (1.0 + g * (1.0 - sig))


def _lane_lo(rows):
    return lax.broadcasted_iota(jnp.int32, (rows, LANES), 1) < HEAD_DIM


def _stack_pair(x):
    lo = _lane_lo(x.shape[0])
    zero = jnp.zeros_like(x)
    return jnp.concatenate([jnp.where(lo, x, zero), jnp.where(lo, zero, x)], axis=0)


def _unstack_pair(y, w):
    return jnp.where(_lane_lo(w), y[:w], y[w:])


def _block_valid(b, left_blocks, width):
    col = lax.broadcasted_iota(jnp.int32, (1, 2 * width), 1)
    col = jnp.where(col >= width, col - width, col)
    return (col // KB + (b - left_blocks)) >= 0


def _toeplitz_tile(diag_row, width, left_chunks):
    wide = width + TQ
    rolled = pltpu.roll(jnp.broadcast_to(diag_row, (TQ, wide)), 1, 1, stride=1, stride_axis=0)
    i = lax.broadcasted_iota(jnp.int32, (TQ, width), 0) // CHUNK
    j = lax.broadcasted_iota(jnp.int32, (TQ, width), 1) // CHUNK
    dc = i + left_chunks - j
    return jnp.where((dc >= 0) & (dc <= left_chunks), rolled[:, TQ:], MASKED)


def _toeplitz_sum(tile, width):
    flip = (lax.broadcasted_iota(jnp.int32, (TQ, TQ), 0) + lax.broadcasted_iota(jnp.int32, (TQ, TQ), 1)
            == TQ - 1).astype(F32)
    reversed_rows = jnp.dot(flip, tile, precision=lax.Precision.HIGHEST, preferred_element_type=F32)
    padded = jnp.concatenate([reversed_rows, jnp.zeros((TQ, TQ), F32)], axis=1)
    rolled = pltpu.roll(padded, 0, 1, stride=1, stride_axis=0)
    return jnp.sum(rolled, axis=0, keepdims=True)


def _softmax_pair(sc, w, sink=None):
    ps, inv, lses = [], [], []
    for e in range(2):
        sh = sc[:, e * w:(e + 1) * w]
        m = jnp.max(sh, axis=-1, keepdims=True)
        if sink is not None:
            m = jnp.maximum(m, sink[e])
        ex = jnp.exp(sh - m)
        l = jnp.sum(ex, axis=-1, keepdims=True)
        if sink is not None:
            l = l + jnp.exp(sink[e] - m)
        ps.append(ex.astype(BF16))
        inv.append(1.0 / l)
        lses.append(m + jnp.log(l))
    return jnp.concatenate(ps, axis=-1), inv, lses


def _softmax_pair_bwd(sc, dp, lse, delta, w):
    ps, dss = [], []
    for e in range(2):
        p = jnp.exp(sc[:, e * w:(e + 1) * w] - lse[e])
        ps.append(p)
        dss.append(p * (dp[:, e * w:(e + 1) * w] - delta[e]))
    return jnp.concatenate(ps, axis=-1), jnp.concatenate(dss, axis=-1)


def _pair_rowsums(x, lo):
    zero = jnp.zeros_like(x)
    return (jnp.sum(jnp.where(lo, x, zero), axis=-1, keepdims=True),
            jnp.sum(jnp.where(lo, zero, x), axis=-1, keepdims=True))


def _a_qkv_specs(rows, pad, pw):
    return [pl.BlockSpec((None, TQ, pw), lambda p, b: (0, b + pad // TQ, p)),
            pl.BlockSpec((None, rows, pw), lambda p, b: (1, 0, p)),
            pl.BlockSpec((None, rows, pw), lambda p, b: (2, 0, p))]


def _window(ref, b, pad, win, lanes):
    start = pl.multiple_of(b * TQ + pad - (win - TQ), KB)
    return ref[pl.ds(start, win), lanes]


def _attn_a_fwd(zqkv, g, diag, hosted=None):
    s = g.shape[0]
    pad = zqkv.shape[1] - s
    nb = s // TQ
    left = A_KBLOCKS - 1
    pairs = A_PAIRS_FWD
    pw = pairs * LANES
    wide = A_WIN + TQ

    def body(q_ref, k_ref, v_ref, g_ref, diag_ref, o_ref, u_ref, lse_ref, bias_scr):
        b = pl.program_id(1)

        @pl.when(b == 0)
        def _():
            for hh in range(2 * pairs):
                bias_scr[hh // 2, :, (hh % 2) * A_WIN:(hh % 2 + 1) * A_WIN] = _toeplitz_tile(
                    diag_ref[hh], A_WIN, A_LEFT_CHUNKS)

        def step(first_blocks):
            lo = _lane_lo(TQ)
            for pp in range(pairs):
                ln = slice(pp * LANES, (pp + 1) * LANES)
                kcat = _stack_pair(_window(k_ref, b, pad, A_WIN, ln))
                vcat = _stack_pair(_window(v_ref, b, pad, A_WIN, ln))
                sc = lax.dot_general(q_ref[:, ln] * SCALE, kcat, NT, preferred_element_type=F32) + bias_scr[pp]
                if first_blocks:
                    sc = jnp.where(_block_valid(b, left, A_WIN), sc, MASKED)
                p, inv, lses = _softmax_pair(sc, A_WIN)
                ov = jnp.dot(p, vcat, preferred_element_type=F32) * jnp.where(lo, inv[0], inv[1])
                o_ref[:, ln] = ov
                lse_ref[pp] = jnp.where(lo, lses[0], lses[1])
                sg, _ = _silu_parts(g_ref[:, ln])
                u_ref[:, ln] = (ov * sg).astype(BF16)

        @pl.when(b < left)
        def _():
            step(True)

        @pl.when(b >= left)
        def _():
            step(False)

    tile = pl.BlockSpec((TQ, pw), lambda p, b: (b, p))
    return _call(
        body, name="attn_a_fwd", grid=(HEADS // 2 // pairs, nb),
        in_specs=_a_qkv_specs(pad + s, pad, pw) + [
            tile, pl.BlockSpec((2 * pairs, 1, wide), lambda p, b: (p, 0, 0))],
        out_specs=[tile, tile, pl.BlockSpec((pairs, TQ, LANES), lambda p, b: (p, b, 0))],
        out_shape=[jax.ShapeDtypeStruct((s, D_MODEL), F32), jax.ShapeDtypeStruct((s, D_MODEL), BF16),
                   jax.ShapeDtypeStruct((HEADS // 2, s, LANES), F32)],
        scratch_shapes=[pltpu.VMEM((pairs, TQ, 2 * A_WIN), F32)],
        sem=("parallel", "arbitrary"), hosted=hosted,
        args=(zqkv, zqkv, zqkv, g, diag))


def _attn_a_bwd(zqkv, g, o, du, lse, diag, hosted=None):
    s = g.shape[0]
    pad = zqkv.shape[1] - s
    nb = s // TQ
    left = A_KBLOCKS - 1
    pw = A_PAIRS * LANES
    wide = A_WIN + TQ

    def body(q_ref, k_ref, v_ref, g_ref, o_ref, du_ref, lse_ref, diag_ref, dz_ref, ddiag_ref,
             bias_scr, dbias_acc, dk_acc, dv_acc):
        b = pl.program_id(1)

        @pl.when(b == 0)
        def _():
            for hh in range(2 * A_PAIRS):
                bias_scr[hh // 2, :, (hh % 2) * A_WIN:(hh % 2 + 1) * A_WIN] = _toeplitz_tile(
                    diag_ref[hh], A_WIN, A_LEFT_CHUNKS)
            dbias_acc[...] = jnp.zeros_like(dbias_acc)
            dk_acc[...] = jnp.zeros_like(dk_acc)
            dv_acc[...] = jnp.zeros_like(dv_acc)

        def step(first_blocks):
            lo = _lane_lo(TQ)
            upper = lax.broadcasted_iota(jnp.int32, (LANES, A_WIN), 0) < HEAD_DIM
            rows = pl.ds(pl.multiple_of(b * TQ, TQ), TQ)
            sg, dsg = _silu_parts(g_ref[...])
            duv = du_ref[...]
            ov = o_ref[...]
            do = duv * sg
            dz_ref[3, rows, :] = (duv * ov * dsg).astype(BF16)
            do_o = do * ov
            do_bf = do.astype(BF16)
            for pp in range(A_PAIRS):
                ln = slice(pp * LANES, (pp + 1) * LANES)
                q = q_ref[:, ln] * SCALE
                kcat = _stack_pair(_window(k_ref, b, pad, A_WIN, ln))
                vcat = _stack_pair(_window(v_ref, b, pad, A_WIN, ln))
                sc = lax.dot_general(q, kcat, NT, preferred_element_type=F32) + bias_scr[pp]
                if first_blocks:
                    sc = jnp.where(_block_valid(b, left, A_WIN), sc, MASKED)
                lse_t = lse_ref[pp]
                dp = lax.dot_general(do_bf[:, ln], vcat, NT, preferred_element_type=F32)
                p, ds = _softmax_pair_bwd(sc, dp, (lse_t[:, 0:1], lse_t[:, HEAD_DIM:HEAD_DIM + 1]),
                                          _pair_rowsums(do_o[:, ln], lo), A_WIN)
                dbias_acc[pp] += ds
                dsb = ds.astype(BF16)
                dz_ref[0, rows, ln] = (jnp.dot(dsb, kcat, preferred_element_type=F32) * SCALE).astype(BF16)
                dkt = lax.dot_general(q, dsb, TN, preferred_element_type=F32)
                dvt = lax.dot_general(do_bf[:, ln], p.astype(BF16), TN, preferred_element_type=F32)
                dkt = jnp.where(upper, dkt[:, :A_WIN], dkt[:, A_WIN:])
                dvt = jnp.where(upper, dvt[:, :A_WIN], dvt[:, A_WIN:])
                for t in range(A_KBLOCKS):
                    blk = b + (pad // KB - left + t)
                    dk_acc[blk, ln, :] += dkt[:, t * KB:(t + 1) * KB]
                    dv_acc[blk, ln, :] += dvt[:, t * KB:(t + 1) * KB]

        @pl.when(b < left)
        def _():
            step(True)

        @pl.when(b >= left)
        def _():
            step(False)

        @pl.when(b == nb - 1)
        def _():
            for kb in range(s // KB):
                dz_ref[1, kb * KB:(kb + 1) * KB, :] = dk_acc[pad // KB + kb].T.astype(BF16)
                dz_ref[2, kb * KB:(kb + 1) * KB, :] = dv_acc[pad // KB + kb].T.astype(BF16)
            for hh in range(2 * A_PAIRS):
                ddiag_ref[hh] = _toeplitz_sum(
                    dbias_acc[hh // 2, :, (hh % 2) * A_WIN:(hh % 2 + 1) * A_WIN], A_WIN)

    tile = pl.BlockSpec((TQ, pw), lambda p, b: (b, p))
    diag_spec = pl.BlockSpec((2 * A_PAIRS, 1, wide), lambda p, b: (p, 0, 0))
    return _call(
        body, name="attn_a_bwd", grid=(HEADS // 2 // A_PAIRS, nb),
        in_specs=_a_qkv_specs(pad + s, pad, pw) + [
            tile, tile, tile, pl.BlockSpec((A_PAIRS, TQ, LANES), lambda p, b: (p, b, 0)), diag_spec],
        out_specs=[pl.BlockSpec((4, s, pw), lambda p, b: (0, 0, p)), diag_spec],
        out_shape=[jax.ShapeDtypeStruct((4, s, D_MODEL), BF16),
                   jax.ShapeDtypeStruct((HEADS, 1, wide), F32)],
        scratch_shapes=[pltpu.VMEM((A_PAIRS, TQ, 2 * A_WIN), F32), pltpu.VMEM((A_PAIRS, TQ, 2 * A_WIN), F32),
                        pltpu.VMEM(((pad + s) // KB, pw, KB), F32), pltpu.VMEM(((pad + s) // KB, pw, KB), F32)],
        sem=("parallel", "arbitrary"), hosted=hosted,
        args=(zqkv, zqkv, zqkv, g, o, du, lse, diag))


B_STACK = B_GROUP // 2
B_KVX = 4 * LANES
B_ROWS = B_STACK * TQ
B_WIDE = B_WIN + TQ


def _b_head_place(h):
    return h // B_GROUP, (h % B_GROUP) // 2, h % 2


def _toeplitz_tile_t(base_row, width, left_chunks):
    wide = width + TQ
    rolled = pltpu.roll(jnp.broadcast_to(base_row, (width, wide)), 0, 1, stride=1, stride_axis=0)
    j = lax.broadcasted_iota(jnp.int32, (width, TQ), 0) // CHUNK
    i = lax.broadcasted_iota(jnp.int32, (width, TQ), 1) // CHUNK
    dc = i + left_chunks - j
    return jnp.where((dc >= 0) & (dc <= left_chunks), rolled[:, :TQ], MASKED)


def _toeplitz_sum_t(tile_t, width):
    flip = (lax.broadcasted_iota(jnp.int32, (width, width), 0) + lax.broadcasted_iota(jnp.int32, (width, width), 1)
            == width - 1).astype(F32)
    reversed_rows = jnp.dot(flip, tile_t, precision=lax.Precision.HIGHEST, preferred_element_type=F32)
    padded = jnp.concatenate([reversed_rows, jnp.zeros((width, width), F32)], axis=1)
    rolled = pltpu.roll(padded, 0, 1, stride=1, stride_axis=0)
    return jnp.sum(rolled, axis=0, keepdims=True)


def _b_build_bias(base_ref, bias_scr):
    for h in range(HEADS):
        gi, pr, e = _b_head_place(h)
        bias_scr[gi, e * B_WIN:(e + 1) * B_WIN, pr * TQ:(pr + 1) * TQ] = _toeplitz_tile_t(
            base_ref[h], B_WIN, B_LEFT_CHUNKS)


def _b_stack(x, gi):
    return jnp.concatenate(
        [x[:, (B_STACK * gi + pr) * LANES:(B_STACK * gi + pr + 1) * LANES] for pr in range(B_STACK)], axis=0)


def _b_sink_rows(sink_ref, gi):
    block = lax.broadcasted_iota(jnp.int32, (1, B_ROWS), 1) // TQ
    rows = []
    for e in range(2):
        row = jnp.zeros((1, B_ROWS), F32)
        for pr in range(B_STACK):
            h = B_GROUP * gi + 2 * pr + e
            row = jnp.where(block == pr, sink_ref[0:1, h:h + 1], row)
        rows.append(row)
    return rows


def _b_scores_t(q_ref, kvv, bias_scr, gi, b, left, first_blocks):
    kcat = _stack_pair(kvv[:, gi * LANES:(gi + 1) * LANES])
    vcat = _stack_pair(kvv[:, (B_KV_HEADS + gi) * LANES:(B_KV_HEADS + gi + 1) * LANES])
    qs = _b_stack(q_ref, gi) * SCALE
    sc = lax.dot_general(kcat, qs, NT, preferred_element_type=F32) + bias_scr[gi]
    if first_blocks:
        row = lax.broadcasted_iota(jnp.int32, (2 * B_WIN, 1), 0)
        row = jnp.where(row >= B_WIN, row - B_WIN, row)
        sc = jnp.where((row // KB + (b - left)) >= 0, sc, MASKED)
    return kcat, vcat, qs, sc


def _attn_b_fwd(qb, kvx, gate, base, sinks):
    s = qb.shape[0]
    pad = kvx.shape[0] - s
    nb = s // TQ
    left = B_KBLOCKS - 1

    def body(q_ref, kv_ref, g_ref, base_ref, sink_ref, o_ref, u_ref, lse_ref, bias_scr):
        b = pl.program_id(0)

        @pl.when(b == 0)
        def _():
            _b_build_bias(base_ref, bias_scr)

        def step(first_blocks):
            kvv = _window(kv_ref, b, pad, B_WIN, slice(None))
            upper = lax.broadcasted_iota(jnp.int32, (LANES, B_ROWS), 0) < HEAD_DIM
            lse_rows = []
            for gi in range(B_KV_HEADS):
                kcat, vcat, qs, sc = _b_scores_t(q_ref, kvv, bias_scr, gi, b, left, first_blocks)
                sink = _b_sink_rows(sink_ref, gi)
                ps, inv = [], []
                for e in range(2):
                    sh = sc[e * B_WIN:(e + 1) * B_WIN]
                    m = jnp.maximum(jnp.max(sh, axis=0, keepdims=True), sink[e])
                    ex = jnp.exp(sh - m)
                    l = jnp.sum(ex, axis=0, keepdims=True) + jnp.exp(sink[e] - m)
                    ps.append(ex.astype(BF16))
                    inv.append(1.0 / l)
                    lse_rows.append(m + jnp.log(l))
                pt = jnp.concatenate(ps, axis=0)
                ot = lax.dot_general(vcat, pt, TN, preferred_element_type=F32) * jnp.where(upper, inv[0], inv[1])
                ov = ot.T
                for pr in range(B_STACK):
                    pair = B_STACK * gi + pr
                    o_ref[:, pair * LANES:(pair + 1) * LANES] = ov[pr * TQ:(pr + 1) * TQ]
            lse_ref[0] = jnp.concatenate(lse_rows + [jnp.zeros((8 - len(lse_rows), B_ROWS), F32)], axis=0)
            sg, _ = _silu_parts(g_ref[...])
            u_ref[...] = (o_ref[...] * sg).astype(BF16)

        @pl.when(b < left)
        def _():
            step(True)

        @pl.when(b >= left)
        def _():
            step(False)

    row = pl.BlockSpec((TQ, D_MODEL), lambda b: (b, 0))
    return pl.pallas_call(
        body, name="attn_b_fwd", grid=(nb,),
        in_specs=[row, pl.BlockSpec((pad + s, B_KVX), lambda b: (0, 0)), row,
                  pl.BlockSpec((HEADS, 1, B_WIDE), lambda b: (0, 0, 0)), pl.BlockSpec((1, HEADS), lambda b: (0, 0))],
        out_specs=[row, row, pl.BlockSpec((1, 8, B_ROWS), lambda b: (b, 0, 0))],
        out_shape=[jax.ShapeDtypeStruct((s, D_MODEL), F32), jax.ShapeDtypeStruct((s, D_MODEL), BF16),
                   jax.ShapeDtypeStruct((nb, 8, B_ROWS), F32)],
        scratch_shapes=[pltpu.VMEM((B_KV_HEADS, 2 * B_WIN, B_ROWS), F32)],
        compiler_params=_params(("arbitrary",)),
    )(qb, kvx, gate, base, sinks)


def _attn_b_bwd(qb, kvx, gate, o, du, lse, base, sinks):
    s = qb.shape[0]
    pad = kvx.shape[0] - s
    nb = s // TQ
    left = B_KBLOCKS - 1
    half = D_MODEL // 2

    def body(q_ref, kv_ref, g_ref, o_ref, du_ref, lse_ref, base_ref, sink_ref, dz_ref, dkv_ref, dsum_ref,
             dsink_ref, bias_scr, dbias_acc, dkv_acc, dsink_acc):
        b = pl.program_id(0)

        @pl.when(b == 0)
        def _():
            _b_build_bias(base_ref, bias_scr)
            dbias_acc[...] = jnp.zeros_like(dbias_acc)
            dkv_acc[...] = jnp.zeros_like(dkv_acc)
            dsink_acc[...] = jnp.zeros_like(dsink_acc)

        def step(first_blocks):
            kvv = _window(kv_ref, b, pad, B_WIN, slice(None))
            sg, dsg = _silu_parts(g_ref[...])
            duv = du_ref[...]
            ov = o_ref[...]
            do = duv * sg
            dgate = (duv * ov * dsg).astype(BF16)
            dz_ref[2] = dgate[:, :half]
            dz_ref[3] = dgate[:, half:]
            do_o = do * ov
            do_bf = do.astype(BF16)
            lse_all = lse_ref[0]
            dsink_rows = []
            for gi in range(B_KV_HEADS):
                kcat, vcat, qs, sc = _b_scores_t(q_ref, kvv, bias_scr, gi, b, left, first_blocks)
                dos = _b_stack(do_bf, gi)
                doo_t = _b_stack(do_o, gi).T
                delta = (jnp.sum(doo_t[:HEAD_DIM], axis=0, keepdims=True),
                         jnp.sum(doo_t[HEAD_DIM:], axis=0, keepdims=True))
                sink = _b_sink_rows(sink_ref, gi)
                dp = lax.dot_general(vcat, dos, NT, preferred_element_type=F32)
                ps, dss = [], []
                for e in range(2):
                    lse_e = lse_all[2 * gi + e:2 * gi + e + 1]
                    delta_e = delta[e]
                    p = jnp.exp(sc[e * B_WIN:(e + 1) * B_WIN] - lse_e)
                    ps.append(p.astype(BF16))
                    dss.append(p * (dp[e * B_WIN:(e + 1) * B_WIN] - delta_e))
                    dsink_rows.append(-jnp.exp(sink[e] - lse_e) * delta_e)
                ds = jnp.concatenate(dss, axis=0)
                dbias_acc[gi] += ds
                dsb = ds.astype(BF16)
                dq = (lax.dot_general(kcat, dsb, TN, preferred_element_type=F32) * SCALE).T.astype(BF16)
                for pr in range(B_STACK):
                    dz_ref[gi, :, pr * LANES:(pr + 1) * LANES] = dq[pr * TQ:(pr + 1) * TQ]
                dk = _unstack_pair(jnp.dot(dsb, qs, preferred_element_type=F32), B_WIN)
                dv = _unstack_pair(jnp.dot(jnp.concatenate(ps, axis=0), dos, preferred_element_type=F32), B_WIN)
                krows = pl.ds(pl.multiple_of(b * TQ + pad - (B_WIN - TQ), KB), B_WIN)
                dkv_acc[krows, gi * LANES:(gi + 1) * LANES] += dk
                dkv_acc[krows, (B_KV_HEADS + gi) * LANES:(B_KV_HEADS + gi + 1) * LANES] += dv
            dsink_acc[...] += jnp.concatenate(
                dsink_rows + [jnp.zeros((8 - len(dsink_rows), B_ROWS), F32)], axis=0)

        @pl.when(b < left)
        def _():
            step(True)

        @pl.when(b >= left)
        def _():
            step(False)

        @pl.when(b == nb - 1)
        def _():
            lo_s = _lane_lo(s)
            for which in range(2):
                folded = []
                for gi in range(B_KV_HEADS):
                    part = dkv_acc[pad:pad + s, (which * B_KV_HEADS + gi) * LANES:(which * B_KV_HEADS + gi + 1) * LANES]
                    folded.append(part + pltpu.roll(part, HEAD_DIM, 1))
                dkv_ref[:, which * LANES:(which + 1) * LANES] = jnp.where(lo_s, folded[0], folded[1]).astype(BF16)
            lane8 = lax.broadcasted_iota(jnp.int32, dsink_ref.shape, 1)
            tot = jnp.zeros(dsink_ref.shape, F32)
            for h in range(HEADS):
                gi, pr, e = _b_head_place(h)
                dsum_ref[h] = _toeplitz_sum_t(
                    dbias_acc[gi, e * B_WIN:(e + 1) * B_WIN, pr * TQ:(pr + 1) * TQ], B_WIN)
                per_query = dsink_acc[2 * gi + e:2 * gi + e + 1, pr * TQ:(pr + 1) * TQ]
                tot = jnp.where(lane8 == h, jnp.sum(per_query, axis=1, keepdims=True), tot)
            dsink_ref[...] = tot

    row = pl.BlockSpec((TQ, D_MODEL), lambda b: (b, 0))
    base_spec = pl.BlockSpec((HEADS, 1, B_WIDE), lambda b: (0, 0, 0))
    return pl.pallas_call(
        body, name="attn_b_bwd", grid=(nb,),
        in_specs=[row, pl.BlockSpec((pad + s, B_KVX), lambda b: (0, 0)), row, row, row,
                  pl.BlockSpec((1, 8, B_ROWS), lambda b: (b, 0, 0)), base_spec,
                  pl.BlockSpec((1, HEADS), lambda b: (0, 0))],
        out_specs=[pl.BlockSpec((4, TQ, half), lambda b: (0, b, 0)),
                   pl.BlockSpec((s, 2 * LANES), lambda b: (0, 0)), base_spec,
                   pl.BlockSpec((8, LANES), lambda b: (0, 0))],
        out_shape=[jax.ShapeDtypeStruct((4, s, half), BF16), jax.ShapeDtypeStruct((s, 2 * LANES), BF16),
                   jax.ShapeDtypeStruct((HEADS, 1, B_WIDE), F32), jax.ShapeDtypeStruct((8, LANES), F32)],
        scratch_shapes=[pltpu.VMEM((B_KV_HEADS, 2 * B_WIN, B_ROWS), F32),
                        pltpu.VMEM((B_KV_HEADS, 2 * B_WIN, B_ROWS), F32),
                        pltpu.VMEM((pad + s, B_KVX), F32), pltpu.VMEM((8, B_ROWS), F32)],
        compiler_params=_params(("arbitrary",)),
    )(qb, kvx, gate, o, du, lse, base, sinks)


def _t5_bucket(rel):
    nb = T5_BUCKETS // 2
    max_exact = nb // 2
    ret = jnp.where(rel > 0, nb, 0)
    n = jnp.abs(rel)
    nf = jnp.maximum(n, 1).astype(jnp.float32)
    large = max_exact + (jnp.log(nf / max_exact) / math.log(T5_MAX_DIST / max_exact)
                         * (nb - max_exact)).astype(jnp.int32)
    large = jnp.minimum(large, nb - 1)
    return ret + jnp.where(n < max_exact, n, large)


def _a_offset_onehot():
    c = np.arange(A_WIN + TQ)
    dist = A_LEFT_CHUNKS * CHUNK + TQ - 1 - c
    idx = np.clip(dist, -A_REL_CLIP, A_REL_CLIP) + A_REL_CLIP
    onehot = np.zeros((A_WIN + TQ, 2 * A_REL_CLIP + 1), np.float32)
    onehot[c, idx] = 1.0
    return jnp.asarray(onehot)


def _b_offset_onehot():
    c = jnp.arange(B_WIN + TQ, dtype=jnp.int32)
    rel = c - (TQ - 1) - B_LEFT_CHUNKS * CHUNK
    return (_t5_bucket(rel)[:, None] == jnp.arange(T5_BUCKETS)[None, :]).astype(F32)


def _diag_rows(onehot, table):
    rows = jnp.dot(onehot, table.astype(F32), precision=lax.Precision.HIGHEST)
    return rows.T.reshape(HEADS, 1, onehot.shape[0])


def _diag_rows_grad(onehot, ddiag):
    return jnp.dot(ddiag.reshape(HEADS, onehot.shape[0]), onehot, precision=lax.Precision.HIGHEST).T


def _position():
    x, y, c = lax.axis_index("x"), lax.axis_index("y"), lax.axis_index("c")
    chips = [(1 - x, y), (x, 1 - y), (1 - x, 1 - y)]
    return x, y, c, chips


ANY = pl.BlockSpec(memory_space=pl.ANY)


def _allgather_hosted(shards, split):
    n = len(shards)

    def part(ref, t, half):
        if not split[t]:
            return ref
        rows = shards[t].shape[0] // 2
        return ref.at[pl.ds(half * rows, rows)]

    def copies(kind, ins, outs, sems):
        send_sems, recv_sems, pass_send, pass_recv, local_sems = sems
        x, y, c, chips = _position()
        mine = 2 * x + y
        if kind == "local":
            return [pltpu.make_async_copy(ins[t], outs[t].at[mine], local_sems.at[t]) for t in range(n)]
        made = []
        for t in range(n):
            for j, chip in enumerate(chips):
                theirs = 2 * chip[0] + chip[1]
                far = dict(send_sem=send_sems.at[3 * t + j], recv_sem=recv_sems.at[3 * t + j],
                           device_id=(chip[0], chip[1], c), device_id_type=MESH)
                near = dict(send_sem=pass_send.at[3 * t + j], recv_sem=pass_recv.at[3 * t + j],
                            device_id=(x, y, 1 - c), device_id_type=MESH)
                here = part(outs[t].at[theirs], t, c)
                if kind == "send":
                    made.append(pltpu.make_async_remote_copy(
                        src_ref=part(ins[t], t, c), dst_ref=part(outs[t].at[mine], t, c), **far))
                elif kind == "landed":
                    made.append(pltpu.make_async_remote_copy(src_ref=here, dst_ref=here, **far))
                elif not split[t]:
                    made.append(None)
                elif kind == "pass":
                    made.append(pltpu.make_async_remote_copy(src_ref=here, dst_ref=here, **near))
                else:
                    other = part(outs[t].at[theirs], t, 1 - c)
                    made.append(pltpu.make_async_remote_copy(src_ref=other, dst_ref=other, **near))
        return made

    def first(ins, outs, sems):
        for cp in copies("local", ins, outs, sems) + copies("send", ins, outs, sems):
            cp.start()

    def middle(ins, outs, sems):
        for got, cp in zip(copies("landed", ins, outs, sems), copies("pass", ins, outs, sems)):
            got.wait_recv()
            if cp is not None:
                cp.start()

    def last(ins, outs, sems):
        for cp in copies("passed", ins, outs, sems):
            if cp is not None:
                cp.wait_recv()
        for cp in copies("send", ins, outs, sems) + copies("pass", ins, outs, sems):
            if cp is not None:
                cp.wait_send()
        for cp in copies("local", ins, outs, sems):
            cp.wait()

    return _Hosted(shards, [jax.ShapeDtypeStruct((4,) + w.shape, w.dtype) for w in shards],
                   [pltpu.SemaphoreType.DMA((3 * n,))] * 4 + [pltpu.SemaphoreType.DMA((n,))],
                   first, middle, last)


def _allgather_routed(shards):
    n = len(shards)

    def piece(block_ref, t, c, quarter=None):
        half = shards[t].shape[0] // 2
        if quarter is None:
            return block_ref.at[pl.ds(c * half, half)]
        return block_ref.at[pl.ds(c * half + quarter * (half // 2), half // 2)]

    def copies(kind, ins, outs, sems):
        ici_send, ici_recv, pass_send, pass_recv, local_sems = sems
        x, y, c, chips = _position()
        mine = 2 * x + y
        if kind == "local":
            return [pltpu.make_async_copy(ins[t], outs[t].at[mine], local_sems.at[t]) for t in range(n)]
        ids = [2 * chip[0] + chip[1] for chip in chips]
        made = []
        for t in range(n):
            def ici(k, to):
                return dict(send_sem=ici_send.at[4 * t + k], recv_sem=ici_recv.at[4 * t + k],
                            device_id=(chips[to][0], chips[to][1], c), device_id_type=MESH)

            def d2d(k):
                return dict(send_sem=pass_send.at[4 * t + k], recv_sem=pass_recv.at[4 * t + k],
                            device_id=(x, y, 1 - c), device_id_type=MESH)

            def same(ref, where):
                return pltpu.make_async_remote_copy(src_ref=ref, dst_ref=ref, **where)

            if kind == "send":
                for k in range(2):
                    made.append(pltpu.make_async_remote_copy(
                        src_ref=piece(ins[t], t, c), dst_ref=piece(outs[t].at[mine], t, c), **ici(k, k)))
            elif kind == "landed":
                made += [same(piece(outs[t].at[ids[k]], t, c), ici(k, k)) for k in range(2)]
            elif kind == "forward":
                made.append(same(piece(outs[t].at[ids[0]], t, c, 0), ici(2, 1)))
                made.append(same(piece(outs[t].at[ids[1]], t, c, 1), ici(3, 0)))
            elif kind == "arrived":
                made.append(same(piece(outs[t].at[ids[2]], t, c, 0), ici(2, 1)))
                made.append(same(piece(outs[t].at[ids[2]], t, c, 1), ici(3, 0)))
            else:
                core = 1 - c if kind == "passed" else c
                if kind in ("pass halves", "passed"):
                    made += [same(piece(outs[t].at[ids[k]], t, core), d2d(k)) for k in range(2)]
                if kind in ("pass quarters", "passed"):
                    made += [same(piece(outs[t].at[ids[2]], t, core, k), d2d(2 + k)) for k in range(2)]
        return made

    def first(ins, outs, sems):
        for cp in copies("local", ins, outs, sems) + copies("send", ins, outs, sems):
            cp.start()

    def middle(ins, outs, sems):
        for got, onward, near in zip(copies("landed", ins, outs, sems), copies("forward", ins, outs, sems),
                                     copies("pass halves", ins, outs, sems)):
            got.wait_recv()
            near.start()
            onward.start()

    def last(ins, outs, sems):
        quarters = copies("pass quarters", ins, outs, sems)
        for got, near in zip(copies("arrived", ins, outs, sems), quarters):
            got.wait_recv()
            near.start()
        for cp in copies("passed", ins, outs, sems):
            cp.wait_recv()
        for cp in (copies("send", ins, outs, sems) + copies("forward", ins, outs, sems)
                   + copies("pass halves", ins, outs, sems) + quarters):
            cp.wait_send()
        for cp in copies("local", ins, outs, sems):
            cp.wait()

    return _Hosted(shards, [jax.ShapeDtypeStruct((4,) + w.shape, w.dtype) for w in shards],
                   [pltpu.SemaphoreType.DMA((4 * n,))] * 4 + [pltpu.SemaphoreType.DMA((n,))],
                   first, middle, last)


def _scatter_hosted(grads):
    n = len(grads)

    def copies(ins, outs, sems):
        send_sems, recv_sems = sems
        x, y, c, chips = _position()
        return [pltpu.make_async_remote_copy(
            src_ref=ins[t].at[2 * chip[0] + chip[1]], dst_ref=outs[t].at[j],
            send_sem=send_sems.at[3 * t + j], recv_sem=recv_sems.at[3 * t + j],
            device_id=(chip[0], chip[1], c), device_id_type=MESH)
            for t in range(n) for j, chip in enumerate(chips)]

    def first(ins, outs, sems):
        for cp in copies(ins, outs, sems):
            cp.start()

    def last(ins, outs, sems):
        for cp in copies(ins, outs, sems):
            cp.wait()

    return _Hosted(grads, [jax.ShapeDtypeStruct((3,) + g.shape[1:], g.dtype) for g in grads],
                   [pltpu.SemaphoreType.DMA((3 * n,))] * 2, first, None, last)


GATHER_PEERS = "x and y neighbours (same core) and the sibling core"
SCATTER_PEERS = "the same core of the three other chips"


def _run_on_sequencer(name, hosted, peers, collective_id):
    ins = [jax.new_ref(a, memory_space=pltpu.MemorySpace.HBM) for a in hosted.inputs]
    outs = [jax.empty_ref(shape, memory_space=pltpu.MemorySpace.HBM) for shape in hosted.out_shapes]

    @pl.kernel(mesh=plsc.ScalarSubcoreMesh(axis_name="sequencer", num_cores=1), name=name,
               scratch_types=tuple(hosted.sems), compiler_params=pltpu.CompilerParams(collective_id=collective_id))
    def launch(*sems):
        x, y, c, chips = _position()
        if peers == GATHER_PEERS:
            devices = [(chip[0], chip[1], c) for chip in chips[:2]] + [(x, y, 1 - c)]
        else:
            devices = [(chip[0], chip[1], c) for chip in chips]
        barrier = pltpu.get_barrier_semaphore()
        for device in devices:
            pl.semaphore_signal(barrier, inc=1, device_id=device, device_id_type=MESH)
        pl.semaphore_wait(barrier, len(devices))
        hosted.first(ins, outs, sems)
        if hosted.middle is not None:
            hosted.middle(ins, outs, sems)
        hosted.last(ins, outs, sems)

    launch()
    return [o[...] for o in outs]


def _run_alone(name, hosted):
    n_in = len(hosted.inputs)
    n_out = len(hosted.out_shapes)

    def body(*refs):
        ins, outs, sems = refs[:n_in], refs[n_in:n_in + n_out], refs[n_in + n_out:]
        hosted.first(ins, outs, sems)
        if hosted.middle is not None:
            hosted.middle(ins, outs, sems)
        hosted.last(ins, outs, sems)

    return pl.pallas_call(
        body, name=name, in_specs=[ANY] * n_in, out_specs=[ANY] * n_out, out_shape=hosted.out_shapes,
        scratch_shapes=hosted.sems)(*hosted.inputs)


def _swap_with_sibling(blocks):
    n = len(blocks)

    def body(*refs):
        ins, outs = refs[:n], refs[n:2 * n]
        send_sems, recv_sems = refs[2 * n:]
        x, y, c, _ = _position()
        sends = [pltpu.make_async_remote_copy(
            src_ref=ins[t], dst_ref=outs[t], send_sem=send_sems.at[t], recv_sem=recv_sems.at[t],
            device_id=(x, y, 1 - c), device_id_type=MESH) for t in range(n)]
        for cp in sends:
            cp.start()
        for cp in sends:
            cp.wait()

    return pl.pallas_call(
        body, name="swap_with_sibling",
        in_specs=[ANY] * n, out_specs=[ANY] * n,
        out_shape=[jax.ShapeDtypeStruct(b.shape, b.dtype) for b in blocks],
        scratch_shapes=[pltpu.SemaphoreType.DMA((n,))] * 2,
    )(*blocks)


def _small_step(partials, extras, ws, ms, vs, shard_of):
    n = len(partials)
    terms = list(partials) + list(extras)
    nt = len(terms)
    rows = [t for t in range(nt) if terms[t].shape[0] == 1]
    mats = [t for t in range(nt) if terms[t].shape[0] != 1]
    row_block = (8, max(terms[t].shape[1] for t in rows))
    assert len(rows) <= row_block[0]
    sent = [row_block] + [terms[t].shape for t in mats]

    def body(*refs):
        ins, refs = refs[:nt], refs[nt:]
        w_refs, refs = refs[:n], refs[n:]
        m_refs, refs = refs[:n], refs[n:]
        v_refs, refs = refs[:n], refs[n:]
        outs, refs = refs[:4 * n + nt - n], refs[4 * n + nt - n:]
        slots, (packed, send_sems, recv_sems) = refs[:len(sent)], refs[len(sent):]
        x, y, c, _ = _position()
        me = 4 * x + 2 * y + c
        packed[...] = jnp.zeros_like(packed)
        for i, t in enumerate(rows):
            packed[i:i + 1, 0:terms[t].shape[1]] = ins[t][...]
        sources = [packed] + [ins[t] for t in mats]
        sends = []
        for j, src in enumerate(sources):
            slots[j][me] = src[...]
            for k in range(1, 8):
                peer = (x ^ (k >> 2), y ^ ((k >> 1) & 1), c ^ (k & 1))
                sends.append(pltpu.make_async_remote_copy(
                    src_ref=src, dst_ref=slots[j].at[me], send_sem=send_sems.at[7 * j + k - 1],
                    recv_sem=recv_sems.at[7 * j + k - 1], device_id=peer, device_id_type=MESH))
        for cp in sends:
            cp.start()
        for j, src in enumerate(sources):
            for k in range(1, 8):
                pltpu.make_async_remote_copy(
                    src_ref=src, dst_ref=slots[j].at[me ^ k], send_sem=send_sems.at[7 * j + k - 1],
                    recv_sem=recv_sems.at[7 * j + k - 1], device_id=(x, y, c), device_id_type=MESH).wait_recv()
        for cp in sends:
            cp.wait_send()
        sums = []
        for j in range(len(sources)):
            g = slots[j][0]
            for dev in range(1, 8):
                g = g + slots[j][dev]
            sums.append(g)
        chip = 2 * x + y
        for t in range(nt):
            if t in rows:
                i = rows.index(t)
                g = sums[0][i:i + 1, 0:terms[t].shape[1]]
            else:
                g = sums[1 + mats.index(t)]
            if t >= n:
                outs[4 * n + t - n][...] = g
                continue
            if shard_of[t]:
                width = ws[t].shape[-1]
                mine = jnp.zeros(ws[t].shape, F32)
                for s in range(4):
                    mine = jnp.where(chip == s, g[:, s * width:(s + 1) * width], mine)
                g = mine
            delta, mn, vn = _adamw_math(w_refs[t][...], g, m_refs[t][...], v_refs[t][...])
            outs[4 * t][...] = g
            outs[4 * t + 1][...] = delta
            outs[4 * t + 2][...] = mn
            outs[4 * t + 3][...] = vn

    vmem = pl.BlockSpec(memory_space=pltpu.VMEM)
    out_shapes = []
    for t in range(n):
        out_shapes += [jax.ShapeDtypeStruct(ws[t].shape, F32)] * 4
    out_shapes += [jax.ShapeDtypeStruct(a.shape, F32) for a in extras]
    out_shapes += [jax.ShapeDtypeStruct((8,) + tuple(shape), F32) for shape in sent]
    res = pl.pallas_call(
        body, name="small_step",
        in_specs=[vmem] * (nt + 3 * n), out_specs=[vmem] * len(out_shapes), out_shape=out_shapes,
        scratch_shapes=[pltpu.VMEM(row_block, F32)] + [pltpu.SemaphoreType.DMA((7 * len(sent),))] * 2,
    )(*terms, *ws, *ms, *vs)
    return [res[4 * t:4 * t + 4] for t in range(n)], res[4 * n:4 * n + nt - n]


def _adamw_math(w, g, m, v):
    m = ADAM_B1 * m + (1.0 - ADAM_B1) * g
    v = ADAM_B2 * v + (1.0 - ADAM_B2) * (g * g)
    m_hat = m / (1.0 - ADAM_B1 ** ADAM_STEP)
    v_hat = v / (1.0 - ADAM_B2 ** ADAM_STEP)
    delta = -ADAM_LR * (m_hat / (jnp.sqrt(v_hat) + ADAM_EPS) + ADAM_WD * w)
    return delta, m, v


def _row_tile(rows):
    return 256 if rows % 256 == 0 else rows


def _sum_partials(name, own, recv, chip):
    rows, cols = own.shape[1:]
    tr = _row_tile(rows)

    def body(chip_ref, own_ref, recv_ref, o_ref):
        acc = own_ref[...]
        for j in range(3):
            acc = acc + recv_ref[j].astype(F32)
        o_ref[...] = acc

    return pl.pallas_call(
        body, name=name,
        grid_spec=pltpu.PrefetchScalarGridSpec(
            num_scalar_prefetch=1, grid=(rows // tr,),
            in_specs=[pl.BlockSpec((None, tr, cols), lambda i, chip_ref: (chip_ref[0], i, 0)),
                      pl.BlockSpec((3, tr, cols), lambda i, chip_ref: (0, i, 0))],
            out_specs=pl.BlockSpec((tr, cols), lambda i, chip_ref: (i, 0))),
        out_shape=jax.ShapeDtypeStruct((rows, cols), F32),
        compiler_params=_params(("parallel",)),
    )(chip.reshape(1).astype(jnp.int32), own, recv)


def _adamw(name, w, m, v, g_parts):
    rows, cols = w.shape
    tr = _row_tile(rows)
    n = len(g_parts)

    def body(w_ref, m_ref, v_ref, *refs):
        g_refs = refs[:n]
        go_ref, d_ref, mo_ref, vo_ref = refs[n:]
        g = g_refs[0][...]
        for r in g_refs[1:]:
            g = g + r[...]
        delta, mn, vn = _adamw_math(w_ref[...], g, m_ref[...], v_ref[...])
        go_ref[...] = g
        d_ref[...] = delta
        mo_ref[...] = mn
        vo_ref[...] = vn

    spec = pl.BlockSpec((tr, cols), lambda i: (i, 0))
    return pl.pallas_call(
        body, name=name, grid=(rows // tr,),
        in_specs=[spec] * (3 + n), out_specs=[spec] * 4,
        out_shape=[jax.ShapeDtypeStruct((rows, cols), F32)] * 4,
        compiler_params=_params(("parallel",)),
    )(w, m, v, *g_parts)


def _local_step(x, target, ga, wa_in, rel_bias, later_shards, gk, t5, gb, sinks, gf):
    s, d = x.shape
    tm = min(TM_DENSE, s)
    nt = s // tm
    half = d // 2
    row = pl.BlockSpec((tm, d), lambda i: (i, 0))
    whole = lambda shape: pl.BlockSpec(shape, lambda *_: (0,) * len(shape))

    gathered = _run_on_sequencer("allgather_later", _allgather_routed(later_shards), GATHER_PEERS, 2)
    n1, = _norm_fwd("norm_a", x, ga)
    zqkv = _matmul("proj_a_qkv", n1, wa_in, dims=NN, grid=(3, nt + 1), zero_axis=1,
                   a_spec=pl.BlockSpec((tm, d), lambda j, i: (jnp.maximum(i - 1, 0), 0)),
                   b_spec=pl.BlockSpec((None, d, d), lambda j, i: (j, 0, 0)),
                   o_spec=pl.BlockSpec((None, tm, d), lambda j, i: (j, i, 0)),
                   out_shape=(3, tm + s, d), out_dtype=BF16)
    gate_a = _matmul("proj_a_gate", n1, wa_in, dims=NN, grid=(nt,),
                     a_spec=row, b_spec=pl.BlockSpec((None, d, d), lambda i: (3, 0, 0)), o_spec=row,
                     out_shape=(s, d), out_dtype=F32)
    onehot_a = _a_offset_onehot()
    diag_a = _diag_rows(onehot_a, rel_bias)
    (o_a, u_a, lse_a), _ = _attn_a_fwd(zqkv, gate_a, diag_a)
    wa_out, wkv, wb_in, wb_out = gathered
    wa_out = wa_out.reshape(d, d)
    wkv = wkv.reshape(d, -1)
    wb_out = wb_out.reshape(d, d)
    h1 = _matmul("out_a", u_a, wa_out, dims=NN, grid=(nt,), a_spec=row, b_spec=whole((d, d)), o_spec=row,
                 out_shape=(s, d), out_dtype=F32, resid=x, resid_spec=row)

    nk, n2 = _norm_fwd("norm_kv_b", h1, jnp.concatenate([gk, gb], axis=0))
    kvw = wkv.shape[1]
    wkv_x = jnp.concatenate([wkv[:, (i // 2) * HEAD_DIM:(i // 2 + 1) * HEAD_DIM] for i in range(8)], axis=1)
    kvx = _matmul("proj_kv", nk, wkv_x, dims=NN, grid=(nt + 1,), zero_axis=0,
                  a_spec=pl.BlockSpec((tm, d), lambda i: (jnp.maximum(i - 1, 0), 0)), b_spec=whole((d, B_KVX)),
                  o_spec=pl.BlockSpec((tm, B_KVX), lambda i: (i, 0)), out_shape=(tm + s, B_KVX), out_dtype=BF16)
    qb = _matmul("proj_b_q", n2, wb_in, dims=NN, grid=(2, nt),
                 a_spec=pl.BlockSpec((tm, d), lambda j, i: (i, 0)),
                 b_spec=pl.BlockSpec((None, d, half), lambda j, i: (j, 0, 0)),
                 o_spec=pl.BlockSpec((tm, half), lambda j, i: (i, j)), out_shape=(s, d), out_dtype=BF16)
    gate_b = _matmul("proj_b_gate", n2, wb_in, dims=NN, grid=(2, nt),
                     a_spec=pl.BlockSpec((tm, d), lambda j, i: (i, 0)),
                     b_spec=pl.BlockSpec((None, d, half), lambda j, i: (2 + j, 0, 0)),
                     o_spec=pl.BlockSpec((tm, half), lambda j, i: (i, j)), out_shape=(s, d), out_dtype=F32)
    onehot_b = _b_offset_onehot()
    base_b = jnp.roll(_diag_rows(onehot_b, t5)[..., ::-1], TQ, axis=-1)
    o_b, u_b, lse_b = _attn_b_fwd(qb, kvx, gate_b, base_b, sinks)
    h2 = _matmul("out_b", u_b, wb_out, dims=NN, grid=(nt,), a_spec=row, b_spec=whole((d, d)), o_spec=row,
                 out_shape=(s, d), out_dtype=F32, resid=h1, resid_spec=row)

    dh2, loss, d_gf = _loss_head(h2, target, gf)

    du_b = _matmul("dout_b", dh2, wb_out, dims=NT, grid=(nt,), a_spec=row, b_spec=whole((d, d)), o_spec=row,
                   out_shape=(s, d), out_dtype=F32)
    d_wb_out = _matmul("dw_out_b", u_b, dh2, dims=TN, grid=(2,),
                       a_spec=whole((s, d)), b_spec=pl.BlockSpec((s, half), lambda j: (0, j)),
                       o_spec=pl.BlockSpec((d, half), lambda j: (0, j)),
                       out_shape=(d, d), out_dtype=F32, also_bf16=True)
    dz_b, dkv, dsum_b, dsinks = _attn_b_bwd(qb, kvx, gate_b, o_b, du_b, lse_b, base_b, sinks)
    ddiag_b = jnp.roll(dsum_b[..., ::-1], -1, axis=-1)
    d_wb_in = _matmul("dw_in_b", n2, dz_b, dims=TN, grid=(4,),
                      a_spec=whole((s, d)), b_spec=pl.BlockSpec((None, s, half), lambda j: (j, 0, 0)),
                      o_spec=pl.BlockSpec((None, d, half), lambda j: (j, 0, 0)),
                      out_shape=(4, d, half), out_dtype=F32, also_bf16=True)
    d_wkv = _matmul("dw_kv", nk, dkv, dims=TN, grid=(1,),
                    a_spec=whole((s, d)), b_spec=whole((s, kvw)), o_spec=whole((d, kvw)),
                    out_shape=(d, kvw), out_dtype=F32, also_bf16=True)
    dh1, d_gkb = _proj_norm_bwd("dproj_kv_b", h1, dh2, jnp.concatenate([gk, gb], axis=0),
                                [(dkv[None], wkv[None]), (dz_b, wb_in)])

    du_a = _matmul("dout_a", dh1, wa_out, dims=NT, grid=(nt,), a_spec=row, b_spec=whole((d, d)), o_spec=row,
                   out_shape=(s, d), out_dtype=F32)
    d_wa_out = _matmul("dw_out_a", u_a, dh1, dims=TN, grid=(2,),
                       a_spec=whole((s, d)), b_spec=pl.BlockSpec((s, half), lambda j: (0, j)),
                       o_spec=pl.BlockSpec((d, half), lambda j: (0, j)),
                       out_shape=(d, d), out_dtype=F32, also_bf16=True)
    early = dict(a_w_out=[g.reshape(4, d // 4, d) for g in d_wa_out],
                 kv_w=[g.reshape(4, d // 4, kvw) for g in d_wkv], b_w_in=list(d_wb_in),
                 b_w_out=[g.reshape(4, d // 4, d) for g in d_wb_out])
    early_recv = _run_on_sequencer("scatter_early", _scatter_hosted([early[n][1] for n in early]),
                                   SCATTER_PEERS, 3)
    (dz_a, ddiag_a), _ = _attn_a_bwd(zqkv, gate_a, o_a, du_a, lse_a, diag_a)
    d_wa_in = _matmul("dw_in_a", n1, dz_a, dims=TN, grid=(4, 2),
                      a_spec=whole((s, d)), b_spec=pl.BlockSpec((None, s, half), lambda j, h: (j, 0, h)),
                      o_spec=pl.BlockSpec((None, d, half), lambda j, h: (j, 0, h)),
                      out_shape=(4, d, d), out_dtype=F32, also_bf16=True)
    late_recv = _run_on_sequencer("scatter_a_w_in", _scatter_hosted([d_wa_in[1]]), SCATTER_PEERS, 0)
    grad_x, d_ga = _proj_norm_bwd("dproj_a", x, dh1, ga, [(dz_a, wa_in)])

    small = dict(a_norm=d_ga, kv_norm=d_gkb[0:1], b_norm=d_gkb[1:2], b_sinks=dsinks[0:1, :HEADS], final_norm=d_gf)
    small["by_offset"] = dict(a_rel_bias=(onehot_a, ddiag_a.reshape(HEADS, -1)),
                              t5_bias=(onehot_b, ddiag_b.reshape(HEADS, -1)))
    own = dict(a_w_in=d_wa_in[0], **{n: early[n][0] for n in early})
    received = dict(a_w_in=late_recv[0], **dict(zip(early, early_recv)))
    return loss, grad_x, small, own, received


SMALL = ("a_norm", "kv_norm", "b_norm", "b_sinks", "final_norm")
TABLES = ("a_rel_bias", "t5_bias")
BIG = ("a_w_in", "a_w_out", "kv_w", "b_w_in", "b_w_out")
ORDER = ("a_norm", "a_w_in", "a_rel_bias", "a_w_out", "kv_norm", "kv_w", "t5_bias", "b_norm", "b_w_in",
         "b_sinks", "b_w_out", "final_norm")


def kernel(x, a_norm, a_w_in, a_rel_bias, a_w_out, kv_norm, kv_w, t5_bias, b_norm, b_w_in, b_sinks, b_w_out, final_norm, loss_target, m_a_norm, m_a_w_in, m_a_rel_bias, m_a_w_out, m_kv_norm, m_kv_w, m_t5_bias, m_b_norm, m_b_w_in, m_b_sinks, m_b_w_out, m_final_norm, v_a_norm, v_a_w_in, v_a_rel_bias, v_a_w_out, v_kv_norm, v_kv_w, v_t5_bias, v_b_norm, v_b_w_in, v_b_sinks, v_b_w_out, v_final_norm):
    w = dict(a_norm=a_norm, a_w_in=a_w_in, a_rel_bias=a_rel_bias, a_w_out=a_w_out, kv_norm=kv_norm, kv_w=kv_w,
             t5_bias=t5_bias, b_norm=b_norm, b_w_in=b_w_in, b_sinks=b_sinks, b_w_out=b_w_out,
             final_norm=final_norm)
    m = dict(a_norm=m_a_norm, a_w_in=m_a_w_in, a_rel_bias=m_a_rel_bias, a_w_out=m_a_w_out, kv_norm=m_kv_norm,
             kv_w=m_kv_w, t5_bias=m_t5_bias, b_norm=m_b_norm, b_w_in=m_b_w_in, b_sinks=m_b_sinks,
             b_w_out=m_b_w_out, final_norm=m_final_norm)
    v = dict(a_norm=v_a_norm, a_w_in=v_a_w_in, a_rel_bias=v_a_rel_bias, a_w_out=v_a_w_out, kv_norm=v_kv_norm,
             kv_w=v_kv_w, t5_bias=v_t5_bias, b_norm=v_b_norm, b_w_in=v_b_w_in, b_sinks=v_b_sinks,
             b_w_out=v_b_w_out, final_norm=v_final_norm)
    d = D_MODEL
    chip = 2 * lax.axis_index("x") + lax.axis_index("y")

    shard2d = dict(a_w_in=a_w_in[0], a_w_out=a_w_out[0], kv_w=kv_w, b_w_in=b_w_in[0], b_w_out=b_w_out[0])

    wa_in, = _run_on_sequencer("allgather_first", _allgather_routed([shard2d["a_w_in"].astype(BF16)]),
                               GATHER_PEERS, 1)
    ga, = _run_alone("allgather_norm", _allgather_hosted([a_norm], [False]))
    ga = ga.reshape(1, d)

    loss, grad_x, small, own, received = _local_step(
        x[0], loss_target[0], ga, wa_in, a_rel_bias[0], [shard2d[n].astype(BF16) for n in BIG[1:]],
        kv_norm.reshape(1, d), t5_bias, b_norm, b_sinks, final_norm.reshape(1, d))

    out = {}
    as2d = lambda a: a.reshape(-1, a.shape[-1])
    small_res, (loss_sum, *offset_sums) = _small_step(
        [small[n] for n in SMALL], [loss] + [small["by_offset"][n][1] for n in TABLES],
        [as2d(w[n]) for n in SMALL], [as2d(m[n]) for n in SMALL], [as2d(v[n]) for n in SMALL],
        [n == "a_norm" for n in SMALL])
    for n, res in zip(SMALL, small_res):
        out[n] = [r.reshape(w[n].shape) for r in res]
    loss_out = loss_sum.reshape(())
    for n, summed in zip(TABLES, offset_sums):
        grad = _diag_rows_grad(small["by_offset"][n][0], summed)
        res = _adamw("adamw_" + n, as2d(w[n]), as2d(m[n]), as2d(v[n]), [grad])
        out[n] = [r.reshape(w[n].shape) for r in res]

    core_sums = [_sum_partials("sum_" + n, own[n], received[n], chip) for n in BIG]
    sibling_sums = _swap_with_sibling(core_sums)

    for n, mine, theirs in zip(BIG, core_sums, sibling_sums):
        res = _adamw("adamw_" + n, shard2d[n], m[n].reshape(shard2d[n].shape), v[n].reshape(shard2d[n].shape),
                     [mine, theirs])
        out[n] = [r.reshape(w[n].shape) for r in res]

    grads = [out[n][0] for n in ORDER]
    deltas = [out[n][1] for n in ORDER]
    new_m = [out[n][2] for n in ORDER]
    new_v = [out[n][3] for n in ORDER]
    return (loss_out, grad_x[None], *grads, *deltas, *new_m, *new_v)
```

```python
import functools
import math

import jax
import jax.numpy as jnp
import numpy as np
from jax import lax
from jax.experimental import pallas as pl
from jax.experimental.pallas import tpu as pltpu
from jax.experimental.pallas import tpu_sc as plsc

F32 = jnp.float32
BF16 = jnp.bfloat16
MESH = pl.DeviceIdType.MESH

D_MODEL = 1024
HEADS = 16
HEAD_DIM = 64
CHUNK = 64
RMS_EPS = 1e-6
SCALE = HEAD_DIM ** -0.5
A_LEFT_CHUNKS = 8
A_REL_CLIP = 256
B_LEFT_CHUNKS = 2
B_KV_HEADS = 2
B_GROUP = HEADS // B_KV_HEADS
T5_BUCKETS = 32
T5_MAX_DIST = 128
ADAM_LR = 0.001
ADAM_B1 = 0.9
ADAM_B2 = 0.999
ADAM_EPS = 1e-08
ADAM_WD = 0.01
ADAM_STEP = 10

MASKED = -1e30
LANES = 128
TQ = 128
A_PAIRS = 2
A_PAIRS_FWD = 4
KB = 128
A_KBLOCKS = A_LEFT_CHUNKS * CHUNK // KB + 1
B_KBLOCKS = B_LEFT_CHUNKS * CHUNK // KB + 1
A_WIN = A_KBLOCKS * KB
B_WIN = B_KBLOCKS * KB
TM = 512
TM_DENSE = 1024
TM_PARTS = 512
VMEM_LIMIT = 56 * 1024 * 1024

NT = (((1,), (1,)), ((), ()))
TN = (((0,), (0,)), ((), ()))
NN = (((1,), (0,)), ((), ()))


def _params(sem=None):
    return pltpu.CompilerParams(dimension_semantics=sem, vmem_limit_bytes=VMEM_LIMIT)


class _Hosted:
    def __init__(self, inputs, out_shapes, sems, first, middle, last):
        self.inputs, self.out_shapes, self.sems = list(inputs), list(out_shapes), list(sems)
        self.first, self.middle, self.last = first, middle, last


def _call(body, *, name, grid, in_specs, out_specs, out_shape, args, scratch_shapes=(), sem=None, hosted=None):
    in_specs, out_specs, out_shape = list(in_specs), list(out_specs), list(out_shape)
    scratch_shapes = list(scratch_shapes)
    if hosted is None:
        out = pl.pallas_call(
            body, name=name, grid=grid, in_specs=in_specs, out_specs=out_specs, out_shape=out_shape,
            scratch_shapes=scratch_shapes, compiler_params=_params(sem))(*args)
        return list(out), []
    n_in, n_out, n_scr = len(in_specs), len(out_shape), len(scratch_shapes)
    h_in, h_out = len(hosted.inputs), len(hosted.out_shapes)
    total = int(np.prod(grid)) if grid else 1

    def wrapped(*refs):
        ins, refs = refs[:n_in], refs[n_in:]
        h_ins, refs = refs[:h_in], refs[h_in:]
        outs, refs = refs[:n_out], refs[n_out:]
        h_outs, refs = refs[:h_out], refs[h_out:]
        scr, h_sems = refs[:n_scr], refs[n_scr:]
        step = 0
        for axis, size in enumerate(grid):
            step = step * size + pl.program_id(axis)

        @pl.when(step == 0)
        def _():
            hosted.first(h_ins, h_outs, h_sems)

        body(*ins, *outs, *scr)
        if hosted.middle is not None:
            @pl.when(step == total // 2)
            def _():
                hosted.middle(h_ins, h_outs, h_sems)

        @pl.when(step == total - 1)
        def _():
            hosted.last(h_ins, h_outs, h_sems)

    out = pl.pallas_call(
        wrapped, name=name, grid=grid, in_specs=in_specs + [ANY] * h_in, out_specs=out_specs + [ANY] * h_out,
        out_shape=out_shape + hosted.out_shapes, scratch_shapes=scratch_shapes + hosted.sems,
        compiler_params=_params(("arbitrary",) * len(grid)))(*args, *hosted.inputs)
    return list(out[:n_out]), list(out[n_out:])


def _matmul(name, a, b, *, dims, grid, a_spec, b_spec, o_spec, out_shape, out_dtype,
            parts=1, resid=None, resid_spec=None, also_bf16=False, hosted=None, zero_axis=None):
    def body(*refs):
        if zero_axis is None:
            product(*refs)
        else:
            @pl.when(pl.program_id(zero_axis) == 0)
            def _():
                refs[2][...] = jnp.zeros_like(refs[2])

            @pl.when(pl.program_id(zero_axis) > 0)
            def _():
                product(*refs)

    def product(*refs):
        a_ref, b_ref = refs[:2]
        r_ref = refs[2] if resid is not None else None
        o_ref = refs[3] if resid is not None else refs[2]
        if parts == 1:
            prod = lax.dot_general(a_ref[...].astype(BF16), b_ref[...].astype(BF16), dims,
                                   preferred_element_type=F32)
        else:
            prod = None
            for part in range(parts):
                term = lax.dot_general(a_ref[part].astype(BF16), b_ref[part].astype(BF16), dims,
                                       preferred_element_type=F32)
                prod = term if prod is None else prod + term
        if resid is not None:
            prod = r_ref[...] + prod
        o_ref[...] = prod.astype(out_dtype)
        if also_bf16:
            refs[-1][...] = prod.astype(BF16)

    in_specs = [a_spec, b_spec]
    args = [a, b]
    if resid is not None:
        in_specs.append(resid_spec)
        args.append(resid)
    sem = ["parallel"] * len(grid)
    out_specs = [o_spec]
    out_shapes = [jax.ShapeDtypeStruct(out_shape, out_dtype)]
    if also_bf16:
        out_specs.append(o_spec)
        out_shapes.append(jax.ShapeDtypeStruct(out_shape, BF16))
    out, extra = _call(body, name=name, grid=grid, in_specs=in_specs, out_specs=out_specs, out_shape=out_shapes,
                       args=args, sem=tuple(sem), hosted=hosted)
    res = out[0] if not also_bf16 else tuple(out)
    return res if hosted is None else (res, extra)


def _rms_rows(x):
    return lax.rsqrt(jnp.mean(x * x, axis=-1, keepdims=True) + RMS_EPS)


def _norm_fwd(name, x, gains):
    s, d = x.shape
    n = gains.shape[0]

    def body(x_ref, g_ref, *o_refs):
        xv = x_ref[...]
        xh = xv * _rms_rows(xv)
        for i in range(n):
            o_refs[i][...] = (xh * g_ref[i:i + 1, :]).astype(BF16)

    row = pl.BlockSpec((TM, d), lambda i: (i, 0))
    return pl.pallas_call(
        body, name=name, grid=(s // TM,),
        in_specs=[row, pl.BlockSpec((n, d), lambda i: (0, 0))],
        out_specs=[row] * n,
        out_shape=[jax.ShapeDtypeStruct((s, d), BF16)] * n,
        compiler_params=_params(("parallel",)),
    )(x, gains)


def _proj_norm_bwd(name, x, dres, gains, branches):
    s, d = x.shape
    n = len(branches)
    tm = min(TM_PARTS, s)

    def body(x_ref, r_ref, g_ref, *refs):
        ab_refs, dx_ref, dg_ref = refs[:2 * n], refs[2 * n], refs[2 * n + 1]
        i = pl.program_id(0)
        xv = x_ref[...]
        r = _rms_rows(xv)
        xh = xv * r

        @pl.when(i == 0)
        def _():
            dg_ref[...] = jnp.zeros_like(dg_ref)

        a = None
        for j in range(n):
            a_ref, b_ref = ab_refs[2 * j], ab_refs[2 * j + 1]
            dn = None
            for part in range(a_ref.shape[0]):
                term = lax.dot_general(a_ref[part], b_ref[part], NT, preferred_element_type=F32)
                dn = term if dn is None else dn + term
            t = dn * g_ref[j:j + 1, :]
            a = t if a is None else a + t
            dg_ref[j:j + 1, :] += jnp.sum(dn * xh, axis=0, keepdims=True)
        dx_ref[...] = r_ref[...] + r * (a - xh * jnp.mean(xh * a, axis=-1, keepdims=True))

    row = pl.BlockSpec((tm, d), lambda i: (i, 0))
    small = pl.BlockSpec((n, d), lambda i: (0, 0))
    ab_specs, ab_args = [], []
    for a, b in branches:
        ab_specs += [pl.BlockSpec((a.shape[0], tm, a.shape[2]), lambda i: (0, i, 0)),
                     pl.BlockSpec(b.shape, lambda i: (0, 0, 0))]
        ab_args += [a, b]
    return pl.pallas_call(
        body, name=name, grid=(s // tm,),
        in_specs=[row, row, small] + ab_specs,
        out_specs=[row, small],
        out_shape=[jax.ShapeDtypeStruct((s, d), F32), jax.ShapeDtypeStruct((n, d), F32)],
        compiler_params=_params(("arbitrary",)),
    )(x, dres, gains, *ab_args)


def _loss_head(h2, target, gain):
    s, d = h2.shape

    def body(h_ref, t_ref, g_ref, dh_ref, loss_ref, dg_ref):
        i = pl.program_id(0)
        hv = h_ref[...]
        r = _rms_rows(hv)
        hh = hv * r
        g = g_ref[...]
        err = hh * g - t_ref[...]
        part = 0.5 * jnp.sum(jnp.sum(err * err, axis=-1, keepdims=True) * (1.0 / d), axis=0, keepdims=True)
        dy = err * (1.0 / d)
        a = dy * g
        dh_ref[...] = r * (a - hh * jnp.mean(hh * a, axis=-1, keepdims=True))
        dg = jnp.sum(dy * hh, axis=0, keepdims=True)

        @pl.when(i == 0)
        def _():
            loss_ref[...] = part
            dg_ref[...] = dg

        @pl.when(i > 0)
        def _():
            loss_ref[...] += part
            dg_ref[...] += dg

    row = pl.BlockSpec((TM, d), lambda i: (i, 0))
    return pl.pallas_call(
        body, name="loss_head", grid=(s // TM,),
        in_specs=[row, row, pl.BlockSpec((1, d), lambda i: (0, 0))],
        out_specs=[row, pl.BlockSpec((1, 1), lambda i: (0, 0)), pl.BlockSpec((1, d), lambda i: (0, 0))],
        out_shape=[jax.ShapeDtypeStruct((s, d), F32), jax.ShapeDtypeStruct((1, 1), F32),
                   jax.ShapeDtypeStruct((1, d), F32)],
        compiler_params=_params(("arbitrary",)),
    )(h2, target, gain)


def _silu_parts(g):
    sig = jax.nn.sigmoid(g)
    return g * sig, sig * (1.0 + g * (1.0 - sig))


def _lane_lo(rows):
    return lax.broadcasted_iota(jnp.int32, (rows, LANES), 1) < HEAD_DIM


def _stack_pair(x):
    lo = _lane_lo(x.shape[0])
    zero = jnp.zeros_like(x)
    return jnp.concatenate([jnp.where(lo, x, zero), jnp.where(lo, zero, x)], axis=0)


def _unstack_pair(y, w):
    return jnp.where(_lane_lo(w), y[:w], y[w:])


def _block_valid(b, left_blocks, width):
    col = lax.broadcasted_iota(jnp.int32, (1, 2 * width), 1)
    col = jnp.where(col >= width, col - width, col)
    return (col // KB + (b - left_blocks)) >= 0


def _toeplitz_tile(diag_row, width, left_chunks):
    wide = width + TQ
    rolled = pltpu.roll(jnp.broadcast_to(diag_row, (TQ, wide)), 1, 1, stride=1, stride_axis=0)
    i = lax.broadcasted_iota(jnp.int32, (TQ, width), 0) // CHUNK
    j = lax.broadcasted_iota(jnp.int32, (TQ, width), 1) // CHUNK
    dc = i + left_chunks - j
    return jnp.where((dc >= 0) & (dc <= left_chunks), rolled[:, TQ:], MASKED)


def _toeplitz_sum(tile, width):
    flip = (lax.broadcasted_iota(jnp.int32, (TQ, TQ), 0) + lax.broadcasted_iota(jnp.int32, (TQ, TQ), 1)
            == TQ - 1).astype(F32)
    reversed_rows = jnp.dot(flip, tile, precision=lax.Precision.HIGHEST, preferred_element_type=F32)
    padded = jnp.concatenate([reversed_rows, jnp.zeros((TQ, TQ), F32)], axis=1)
    rolled = pltpu.roll(padded, 0, 1, stride=1, stride_axis=0)
    return jnp.sum(rolled, axis=0, keepdims=True)


def _softmax_pair(sc, w, sink=None):
    ps, inv, lses = [], [], []
    for e in range(2):
        sh = sc[:, e * w:(e + 1) * w]
        m = jnp.max(sh, axis=-1, keepdims=True)
        if sink is not None:
            m = jnp.maximum(m, sink[e])
        ex = jnp.exp(sh - m)
        l = jnp.sum(ex, axis=-1, keepdims=True)
        if sink is not None:
            l = l + jnp.exp(sink[e] - m)
        ps.append(ex.astype(BF16))
        inv.append(1.0 / l)
        lses.append(m + jnp.log(l))
    return jnp.concatenate(ps, axis=-1), inv, lses


def _softmax_pair_bwd(sc, dp, lse, delta, w):
    ps, dss = [], []
    for e in range(2):
        p = jnp.exp(sc[:, e * w:(e + 1) * w] - lse[e])
        ps.append(p)
        dss.append(p * (dp[:, e * w:(e + 1) * w] - delta[e]))
    return jnp.concatenate(ps, axis=-1), jnp.concatenate(dss, axis=-1)


def _pair_rowsums(x, lo):
    zero = jnp.zeros_like(x)
    return (jnp.sum(jnp.where(lo, x, zero), axis=-1, keepdims=True),
            jnp.sum(jnp.where(lo, zero, x), axis=-1, keepdims=True))


def _a_qkv_specs(rows, pad, pw):
    return [pl.BlockSpec((None, TQ, pw), lambda p, b: (0, b + pad // TQ, p)),
            pl.BlockSpec((None, rows, pw), lambda p, b: (1, 0, p)),
            pl.BlockSpec((None, rows, pw), lambda p, b: (2, 0, p))]


def _window(ref, b, pad, win, lanes):
    start = pl.multiple_of(b * TQ + pad - (win - TQ), KB)
    return ref[pl.ds(start, win), lanes]


def _attn_a_fwd(zqkv, g, diag, hosted=None):
    s = g.shape[0]
    pad = zqkv.shape[1] - s
    nb = s // TQ
    left = A_KBLOCKS - 1
    pairs = A_PAIRS_FWD
    pw = pairs * LANES
    wide = A_WIN + TQ

    def body(q_ref, k_ref, v_ref, g_ref, diag_ref, o_ref, u_ref, lse_ref, bias_scr):
        b = pl.program_id(1)

        @pl.when(b == 0)
        def _():
            for hh in range(2 * pairs):
                bias_scr[hh // 2, :, (hh % 2) * A_WIN:(hh % 2 + 1) * A_WIN] = _toeplitz_tile(
                    diag_ref[hh], A_WIN, A_LEFT_CHUNKS)

        def step(first_blocks):
            lo = _lane_lo(TQ)
            for pp in range(pairs):
                ln = slice(pp * LANES, (pp + 1) * LANES)
                kcat = _stack_pair(_window(k_ref, b, pad, A_WIN, ln))
                vcat = _stack_pair(_window(v_ref, b, pad, A_WIN, ln))
                sc = lax.dot_general(q_ref[:, ln] * SCALE, kcat, NT, preferred_element_type=F32) + bias_scr[pp]
                if first_blocks:
                    sc = jnp.where(_block_valid(b, left, A_WIN), sc, MASKED)
                p, inv, lses = _softmax_pair(sc, A_WIN)
                ov = jnp.dot(p, vcat, preferred_element_type=F32) * jnp.where(lo, inv[0], inv[1])
                o_ref[:, ln] = ov
                lse_ref[pp] = jnp.where(lo, lses[0], lses[1])
                sg, _ = _silu_parts(g_ref[:, ln])
                u_ref[:, ln] = (ov * sg).astype(BF16)

        @pl.when(b < left)
        def _():
            step(True)

        @pl.when(b >= left)
        def _():
            step(False)

    tile = pl.BlockSpec((TQ, pw), lambda p, b: (b, p))
    return _call(
        body, name="attn_a_fwd", grid=(HEADS // 2 // pairs, nb),
        in_specs=_a_qkv_specs(pad + s, pad, pw) + [
            tile, pl.BlockSpec((2 * pairs, 1, wide), lambda p, b: (p, 0, 0))],
        out_specs=[tile, tile, pl.BlockSpec((pairs, TQ, LANES), lambda p, b: (p, b, 0))],
        out_shape=[jax.ShapeDtypeStruct((s, D_MODEL), F32), jax.ShapeDtypeStruct((s, D_MODEL), BF16),
                   jax.ShapeDtypeStruct((HEADS // 2, s, LANES), F32)],
        scratch_shapes=[pltpu.VMEM((pairs, TQ, 2 * A_WIN), F32)],
        sem=("parallel", "arbitrary"), hosted=hosted,
        args=(zqkv, zqkv, zqkv, g, diag))


def _attn_a_bwd(zqkv, g, o, du, lse, diag, hosted=None):
    s = g.shape[0]
    pad = zqkv.shape[1] - s
    nb = s // TQ
    left = A_KBLOCKS - 1
    pw = A_PAIRS * LANES
    wide = A_WIN + TQ

    def body(q_ref, k_ref, v_ref, g_ref, o_ref, du_ref, lse_ref, diag_ref, dz_ref, ddiag_ref,
             bias_scr, dbias_acc, dk_acc, dv_acc):
        b = pl.program_id(1)

        @pl.when(b == 0)
        def _():
            for hh in range(2 * A_PAIRS):
                bias_scr[hh // 2, :, (hh % 2) * A_WIN:(hh % 2 + 1) * A_WIN] = _toeplitz_tile(
                    diag_ref[hh], A_WIN, A_LEFT_CHUNKS)
            dbias_acc[...] = jnp.zeros_like(dbias_acc)
            dk_acc[...] = jnp.zeros_like(dk_acc)
            dv_acc[...] = jnp.zeros_like(dv_acc)

        def step(first_blocks):
            lo = _lane_lo(TQ)
            upper = lax.broadcasted_iota(jnp.int32, (LANES, A_WIN), 0) < HEAD_DIM
            rows = pl.ds(pl.multiple_of(b * TQ, TQ), TQ)
            sg, dsg = _silu_parts(g_ref[...])
            duv = du_ref[...]
            ov = o_ref[...]
            do = duv * sg
            dz_ref[3, rows, :] = (duv * ov * dsg).astype(BF16)
            do_o = do * ov
            do_bf = do.astype(BF16)
            for pp in range(A_PAIRS):
                ln = slice(pp * LANES, (pp + 1) * LANES)
                q = q_ref[:, ln] * SCALE
                kcat = _stack_pair(_window(k_ref, b, pad, A_WIN, ln))
                vcat = _stack_pair(_window(v_ref, b, pad, A_WIN, ln))
                sc = lax.dot_general(q, kcat, NT, preferred_element_type=F32) + bias_scr[pp]
                if first_blocks:
                    sc = jnp.where(_block_valid(b, left, A_WIN), sc, MASKED)
                lse_t = lse_ref[pp]
                dp = lax.dot_general(do_bf[:, ln], vcat, NT, preferred_element_type=F32)
                p, ds = _softmax_pair_bwd(sc, dp, (lse_t[:, 0:1], lse_t[:, HEAD_DIM:HEAD_DIM + 1]),
                                          _pair_rowsums(do_o[:, ln], lo), A_WIN)
                dbias_acc[pp] += ds
                dsb = ds.astype(BF16)
                dz_ref[0, rows, ln] = (jnp.dot(dsb, kcat, preferred_element_type=F32) * SCALE).astype(BF16)
                dkt = lax.dot_general(q, dsb, TN, preferred_element_type=F32)
                dvt = lax.dot_general(do_bf[:, ln], p.astype(BF16), TN, preferred_element_type=F32)
                dkt = jnp.where(upper, dkt[:, :A_WIN], dkt[:, A_WIN:])
                dvt = jnp.where(upper, dvt[:, :A_WIN], dvt[:, A_WIN:])
                for t in range(A_KBLOCKS):
                    blk = b + (pad // KB - left + t)
                    dk_acc[blk, ln, :] += dkt[:, t * KB:(t + 1) * KB]
                    dv_acc[blk, ln, :] += dvt[:, t * KB:(t + 1) * KB]

        @pl.when(b < left)
        def _():
            step(True)

        @pl.when(b >= left)
        def _():
            step(False)

        @pl.when(b == nb - 1)
        def _():
            for kb in range(s // KB):
                dz_ref[1, kb * KB:(kb + 1) * KB, :] = dk_acc[pad // KB + kb].T.astype(BF16)
                dz_ref[2, kb * KB:(kb + 1) * KB, :] = dv_acc[pad // KB + kb].T.astype(BF16)
            for hh in range(2 * A_PAIRS):
                ddiag_ref[hh] = _toeplitz_sum(
                    dbias_acc[hh // 2, :, (hh % 2) * A_WIN:(hh % 2 + 1) * A_WIN], A_WIN)

    tile = pl.BlockSpec((TQ, pw), lambda p, b: (b, p))
    diag_spec = pl.BlockSpec((2 * A_PAIRS, 1, wide), lambda p, b: (p, 0, 0))
    return _call(
        body, name="attn_a_bwd", grid=(HEADS // 2 // A_PAIRS, nb),
        in_specs=_a_qkv_specs(pad + s, pad, pw) + [
            tile, tile, tile, pl.BlockSpec((A_PAIRS, TQ, LANES), lambda p, b: (p, b, 0)), diag_spec],
        out_specs=[pl.BlockSpec((4, s, pw), lambda p, b: (0, 0, p)), diag_spec],
        out_shape=[jax.ShapeDtypeStruct((4, s, D_MODEL), BF16),
                   jax.ShapeDtypeStruct((HEADS, 1, wide), F32)],
        scratch_shapes=[pltpu.VMEM((A_PAIRS, TQ, 2 * A_WIN), F32), pltpu.VMEM((A_PAIRS, TQ, 2 * A_WIN), F32),
                        pltpu.VMEM(((pad + s) // KB, pw, KB), F32), pltpu.VMEM(((pad + s) // KB, pw, KB), F32)],
        sem=("parallel", "arbitrary"), hosted=hosted,
        args=(zqkv, zqkv, zqkv, g, o, du, lse, diag))


B_STACK = B_GROUP // 2
B_KVX = 4 * LANES
B_ROWS = B_STACK * TQ
B_WIDE = B_WIN + TQ


def _b_head_place(h):
    return h // B_GROUP, (h % B_GROUP) // 2, h % 2


def _toeplitz_tile_t(base_row, width, left_chunks):
    wide = width + TQ
    rolled = pltpu.roll(jnp.broadcast_to(base_row, (width, wide)), 0, 1, stride=1, stride_axis=0)
    j = lax.broadcasted_iota(jnp.int32, (width, TQ), 0) // CHUNK
    i = lax.broadcasted_iota(jnp.int32, (width, TQ), 1) // CHUNK
    dc = i + left_chunks - j
    return jnp.where((dc >= 0) & (dc <= left_chunks), rolled[:, :TQ], MASKED)


def _toeplitz_sum_t(tile_t, width):
    flip = (lax.broadcasted_iota(jnp.int32, (width, width), 0) + lax.broadcasted_iota(jnp.int32, (width, width), 1)
            == width - 1).astype(F32)
    reversed_rows = jnp.dot(flip, tile_t, precision=lax.Precision.HIGHEST, preferred_element_type=F32)
    padded = jnp.concatenate([reversed_rows, jnp.zeros((width, width), F32)], axis=1)
    rolled = pltpu.roll(padded, 0, 1, stride=1, stride_axis=0)
    return jnp.sum(rolled, axis=0, keepdims=True)


def _b_build_bias(base_ref, bias_scr):
    for h in range(HEADS):
        gi, pr, e = _b_head_place(h)
        bias_scr[gi, e * B_WIN:(e + 1) * B_WIN, pr * TQ:(pr + 1) * TQ] = _toeplitz_tile_t(
            base_ref[h], B_WIN, B_LEFT_CHUNKS)


def _b_stack(x, gi):
    return jnp.concatenate(
        [x[:, (B_STACK * gi + pr) * LANES:(B_STACK * gi + pr + 1) * LANES] for pr in range(B_STACK)], axis=0)


def _b_sink_rows(sink_ref, gi):
    block = lax.broadcasted_iota(jnp.int32, (1, B_ROWS), 1) // TQ
    rows = []
    for e in range(2):
        row = jnp.zeros((1, B_ROWS), F32)
        for pr in range(B_STACK):
            h = B_GROUP * gi + 2 * pr + e
            row = jnp.where(block == pr, sink_ref[0:1, h:h + 1], row)
        rows.append(row)
    return rows


def _b_scores_t(q_ref, kvv, bias_scr, gi, b, left, first_blocks):
    kcat = _stack_pair(kvv[:, gi * LANES:(gi + 1) * LANES])
    vcat = _stack_pair(kvv[:, (B_KV_HEADS + gi) * LANES:(B_KV_HEADS + gi + 1) * LANES])
    qs = _b_stack(q_ref, gi) * SCALE
    sc = lax.dot_general(kcat, qs, NT, preferred_element_type=F32) + bias_scr[gi]
    if first_blocks:
        row = lax.broadcasted_iota(jnp.int32, (2 * B_WIN, 1), 0)
        row = jnp.where(row >= B_WIN, row - B_WIN, row)
        sc = jnp.where((row // KB + (b - left)) >= 0, sc, MASKED)
    return kcat, vcat, qs, sc


def _attn_b_fwd(qb, kvx, gate, base, sinks):
    s = qb.shape[0]
    pad = kvx.shape[0] - s
    nb = s // TQ
    left = B_KBLOCKS - 1

    def body(q_ref, kv_ref, g_ref, base_ref, sink_ref, o_ref, u_ref, lse_ref, bias_scr):
        b = pl.program_id(0)

        @pl.when(b == 0)
        def _():
            _b_build_bias(base_ref, bias_scr)

        def step(first_blocks):
            kvv = _window(kv_ref, b, pad, B_WIN, slice(None))
            upper = lax.broadcasted_iota(jnp.int32, (LANES, B_ROWS), 0) < HEAD_DIM
            lse_rows = []
            for gi in range(B_KV_HEADS):
                kcat, vcat, qs, sc = _b_scores_t(q_ref, kvv, bias_scr, gi, b, left, first_blocks)
                sink = _b_sink_rows(sink_ref, gi)
                ps, inv = [], []
                for e in range(2):
                    sh = sc[e * B_WIN:(e + 1) * B_WIN]
                    m = jnp.maximum(jnp.max(sh, axis=0, keepdims=True), sink[e])
                    ex = jnp.exp(sh - m)
                    l = jnp.sum(ex, axis=0, keepdims=True) + jnp.exp(sink[e] - m)
                    ps.append(ex.astype(BF16))
                    inv.append(1.0 / l)
                    lse_rows.append(m + jnp.log(l))
                pt = jnp.concatenate(ps, axis=0)
                ot = lax.dot_general(vcat, pt, TN, preferred_element_type=F32) * jnp.where(upper, inv[0], inv[1])
                ov = ot.T
                for pr in range(B_STACK):
                    pair = B_STACK * gi + pr
                    o_ref[:, pair * LANES:(pair + 1) * LANES] = ov[pr * TQ:(pr + 1) * TQ]
            lse_ref[0] = jnp.concatenate(lse_rows + [jnp.zeros((8 - len(lse_rows), B_ROWS), F32)], axis=0)
            sg, _ = _silu_parts(g_ref[...])
            u_ref[...] = (o_ref[...] * sg).astype(BF16)

        @pl.when(b < left)
        def _():
            step(True)

        @pl.when(b >= left)
        def _():
            step(False)

    row = pl.BlockSpec((TQ, D_MODEL), lambda b: (b, 0))
    return pl.pallas_call(
        body, name="attn_b_fwd", grid=(nb,),
        in_specs=[row, pl.BlockSpec((pad + s, B_KVX), lambda b: (0, 0)), row,
                  pl.BlockSpec((HEADS, 1, B_WIDE), lambda b: (0, 0, 0)), pl.BlockSpec((1, HEADS), lambda b: (0, 0))],
        out_specs=[row, row, pl.BlockSpec((1, 8, B_ROWS), lambda b: (b, 0, 0))],
        out_shape=[jax.ShapeDtypeStruct((s, D_MODEL), F32), jax.ShapeDtypeStruct((s, D_MODEL), BF16),
                   jax.ShapeDtypeStruct((nb, 8, B_ROWS), F32)],
        scratch_shapes=[pltpu.VMEM((B_KV_HEADS, 2 * B_WIN, B_ROWS), F32)],
        compiler_params=_params(("arbitrary",)),
    )(qb, kvx, gate, base, sinks)


def _attn_b_bwd(qb, kvx, gate, o, du, lse, base, sinks):
    s = qb.shape[0]
    pad = kvx.shape[0] - s
    nb = s // TQ
    left = B_KBLOCKS - 1
    half = D_MODEL // 2

    def body(q_ref, kv_ref, g_ref, o_ref, du_ref, lse_ref, base_ref, sink_ref, dz_ref, dkv_ref, dsum_ref,
             dsink_ref, bias_scr, dbias_acc, dkv_acc, dsink_acc):
        b = pl.program_id(0)

        @pl.when(b == 0)
        def _():
            _b_build_bias(base_ref, bias_scr)
            dbias_acc[...] = jnp.zeros_like(dbias_acc)
            dkv_acc[...] = jnp.zeros_like(dkv_acc)
            dsink_acc[...] = jnp.zeros_like(dsink_acc)

        def step(first_blocks):
            kvv = _window(kv_ref, b, pad, B_WIN, slice(None))
            sg, dsg = _silu_parts(g_ref[...])
            duv = du_ref[...]
            ov = o_ref[...]
            do = duv * sg
            dgate = (duv * ov * dsg).astype(BF16)
            dz_ref[2] = dgate[:, :half]
            dz_ref[3] = dgate[:, half:]
            do_o = do * ov
            do_bf = do.astype(BF16)
            lse_all = lse_ref[0]
            dsink_rows = []
            for gi in range(B_KV_HEADS):
                kcat, vcat, qs, sc = _b_scores_t(q_ref, kvv, bias_scr, gi, b, left, first_blocks)
                dos = _b_stack(do_bf, gi)
                doo_t = _b_stack(do_o, gi).T
                delta = (jnp.sum(doo_t[:HEAD_DIM], axis=0, keepdims=True),
                         jnp.sum(doo_t[HEAD_DIM:], axis=0, keepdims=True))
                sink = _b_sink_rows(sink_ref, gi)
                dp = lax.dot_general(vcat, dos, NT, preferred_element_type=F32)
                ps, dss = [], []
                for e in range(2):
                    lse_e = lse_all[2 * gi + e:2 * gi + e + 1]
                    delta_e = delta[e]
                    p = jnp.exp(sc[e * B_WIN:(e + 1) * B_WIN] - lse_e)
                    ps.append(p.astype(BF16))
                    dss.append(p * (dp[e * B_WIN:(e + 1) * B_WIN] - delta_e))
                    dsink_rows.append(-jnp.exp(sink[e] - lse_e) * delta_e)
                ds = jnp.concatenate(dss, axis=0)
                dbias_acc[gi] += ds
                dsb = ds.astype(BF16)
                dq = (lax.dot_general(kcat, dsb, TN, preferred_element_type=F32) * SCALE).T.astype(BF16)
                for pr in range(B_STACK):
                    dz_ref[gi, :, pr * LANES:(pr + 1) * LANES] = dq[pr * TQ:(pr + 1) * TQ]
                dk = _unstack_pair(jnp.dot(dsb, qs, preferred_element_type=F32), B_WIN)
                dv = _unstack_pair(jnp.dot(jnp.concatenate(ps, axis=0), dos, preferred_element_type=F32), B_WIN)
                krows = pl.ds(pl.multiple_of(b * TQ + pad - (B_WIN - TQ), KB), B_WIN)
                dkv_acc[krows, gi * LANES:(gi + 1) * LANES] += dk
                dkv_acc[krows, (B_KV_HEADS + gi) * LANES:(B_KV_HEADS + gi + 1) * LANES] += dv
            dsink_acc[...] += jnp.concatenate(
                dsink_rows + [jnp.zeros((8 - len(dsink_rows), B_ROWS), F32)], axis=0)

        @pl.when(b < left)
        def _():
            step(True)

        @pl.when(b >= left)
        def _():
            step(False)

        @pl.when(b == nb - 1)
        def _():
            lo_s = _lane_lo(s)
            for which in range(2):
                folded = []
                for gi in range(B_KV_HEADS):
                    part = dkv_acc[pad:pad + s, (which * B_KV_HEADS + gi) * LANES:(which * B_KV_HEADS + gi + 1) * LANES]
                    folded.append(part + pltpu.roll(part, HEAD_DIM, 1))
                dkv_ref[:, which * LANES:(which + 1) * LANES] = jnp.where(lo_s, folded[0], folded[1]).astype(BF16)
            lane8 = lax.broadcasted_iota(jnp.int32, dsink_ref.shape, 1)
            tot = jnp.zeros(dsink_ref.shape, F32)
            for h in range(HEADS):
                gi, pr, e = _b_head_place(h)
                dsum_ref[h] = _toeplitz_sum_t(
                    dbias_acc[gi, e * B_WIN:(e + 1) * B_WIN, pr * TQ:(pr + 1) * TQ], B_WIN)
                per_query = dsink_acc[2 * gi + e:2 * gi + e + 1, pr * TQ:(pr + 1) * TQ]
                tot = jnp.where(lane8 == h, jnp.sum(per_query, axis=1, keepdims=True), tot)
            dsink_ref[...] = tot

    row = pl.BlockSpec((TQ, D_MODEL), lambda b: (b, 0))
    base_spec = pl.BlockSpec((HEADS, 1, B_WIDE), lambda b: (0, 0, 0))
    return pl.pallas_call(
        body, name="attn_b_bwd", grid=(nb,),
        in_specs=[row, pl.BlockSpec((pad + s, B_KVX), lambda b: (0, 0)), row, row, row,
                  pl.BlockSpec((1, 8, B_ROWS), lambda b: (b, 0, 0)), base_spec,
                  pl.BlockSpec((1, HEADS), lambda b: (0, 0))],
        out_specs=[pl.BlockSpec((4, TQ, half), lambda b: (0, b, 0)),
                   pl.BlockSpec((s, 2 * LANES), lambda b: (0, 0)), base_spec,
                   pl.BlockSpec((8, LANES), lambda b: (0, 0))],
        out_shape=[jax.ShapeDtypeStruct((4, s, half), BF16), jax.ShapeDtypeStruct((s, 2 * LANES), BF16),
                   jax.ShapeDtypeStruct((HEADS, 1, B_WIDE), F32), jax.ShapeDtypeStruct((8, LANES), F32)],
        scratch_shapes=[pltpu.VMEM((B_KV_HEADS, 2 * B_WIN, B_ROWS), F32),
                        pltpu.VMEM((B_KV_HEADS, 2 * B_WIN, B_ROWS), F32),
                        pltpu.VMEM((pad + s, B_KVX), F32), pltpu.VMEM((8, B_ROWS), F32)],
        compiler_params=_params(("arbitrary",)),
    )(qb, kvx, gate, o, du, lse, base, sinks)


def _t5_bucket(rel):
    nb = T5_BUCKETS // 2
    max_exact = nb // 2
    ret = jnp.where(rel > 0, nb, 0)
    n = jnp.abs(rel)
    nf = jnp.maximum(n, 1).astype(jnp.float32)
    large = max_exact + (jnp.log(nf / max_exact) / math.log(T5_MAX_DIST / max_exact)
                         * (nb - max_exact)).astype(jnp.int32)
    large = jnp.minimum(large, nb - 1)
    return ret + jnp.where(n < max_exact, n, large)


def _a_offset_onehot():
    c = np.arange(A_WIN + TQ)
    dist = A_LEFT_CHUNKS * CHUNK + TQ - 1 - c
    idx = np.clip(dist, -A_REL_CLIP, A_REL_CLIP) + A_REL_CLIP
    onehot = np.zeros((A_WIN + TQ, 2 * A_REL_CLIP + 1), np.float32)
    onehot[c, idx] = 1.0
    return jnp.asarray(onehot)


def _b_offset_onehot():
    c = jnp.arange(B_WIN + TQ, dtype=jnp.int32)
    rel = c - (TQ - 1) - B_LEFT_CHUNKS * CHUNK
    return (_t5_bucket(rel)[:, None] == jnp.arange(T5_BUCKETS)[None, :]).astype(F32)


def _diag_rows(onehot, table):
    rows = jnp.dot(onehot, table.astype(F32), precision=lax.Precision.HIGHEST)
    return rows.T.reshape(HEADS, 1, onehot.shape[0])


def _diag_rows_grad(onehot, ddiag):
    return jnp.dot(ddiag.reshape(HEADS, onehot.shape[0]), onehot, precision=lax.Precision.HIGHEST).T


def _position():
    x, y, c = lax.axis_index("x"), lax.axis_index("y"), lax.axis_index("c")
    chips = [(1 - x, y), (x, 1 - y), (1 - x, 1 - y)]
    return x, y, c, chips


ANY = pl.BlockSpec(memory_space=pl.ANY)


def _allgather_hosted(shards, split):
    n = len(shards)

    def part(ref, t, half):
        if not split[t]:
            return ref
        rows = shards[t].shape[0] // 2
        return ref.at[pl.ds(half * rows, rows)]

    def copies(kind, ins, outs, sems):
        send_sems, recv_sems, pass_send, pass_recv, local_sems = sems
        x, y, c, chips = _position()
        mine = 2 * x + y
        if kind == "local":
            return [pltpu.make_async_copy(ins[t], outs[t].at[mine], local_sems.at[t]) for t in range(n)]
        made = []
        for t in range(n):
            for j, chip in enumerate(chips):
                theirs = 2 * chip[0] + chip[1]
                far = dict(send_sem=send_sems.at[3 * t + j], recv_sem=recv_sems.at[3 * t + j],
                           device_id=(chip[0], chip[1], c), device_id_type=MESH)
                near = dict(send_sem=pass_send.at[3 * t + j], recv_sem=pass_recv.at[3 * t + j],
                            device_id=(x, y, 1 - c), device_id_type=MESH)
                here = part(outs[t].at[theirs], t, c)
                if kind == "send":
                    made.append(pltpu.make_async_remote_copy(
                        src_ref=part(ins[t], t, c), dst_ref=part(outs[t].at[mine], t, c), **far))
                elif kind == "landed":
                    made.append(pltpu.make_async_remote_copy(src_ref=here, dst_ref=here, **far))
                elif not split[t]:
                    made.append(None)
                elif kind == "pass":
                    made.append(pltpu.make_async_remote_copy(src_ref=here, dst_ref=here, **near))
                else:
                    other = part(outs[t].at[theirs], t, 1 - c)
                    made.append(pltpu.make_async_remote_copy(src_ref=other, dst_ref=other, **near))
        return made

    def first(ins, outs, sems):
        for cp in copies("local", ins, outs, sems) + copies("send", ins, outs, sems):
            cp.start()

    def middle(ins, outs, sems):
        for got, cp in zip(copies("landed", ins, outs, sems), copies("pass", ins, outs, sems)):
            got.wait_recv()
            if cp is not None:
                cp.start()

    def last(ins, outs, sems):
        for cp in copies("passed", ins, outs, sems):
            if cp is not None:
                cp.wait_recv()
        for cp in copies("send", ins, outs, sems) + copies("pass", ins, outs, sems):
            if cp is not None:
                cp.wait_send()
        for cp in copies("local", ins, outs, sems):
            cp.wait()

    return _Hosted(shards, [jax.ShapeDtypeStruct((4,) + w.shape, w.dtype) for w in shards],
                   [pltpu.SemaphoreType.DMA((3 * n,))] * 4 + [pltpu.SemaphoreType.DMA((n,))],
                   first, middle, last)


def _allgather_routed(shards):
    n = len(shards)

    def piece(block_ref, t, c, quarter=None):
        half = shards[t].shape[0] // 2
        if quarter is None:
            return block_ref.at[pl.ds(c * half, half)]
        return block_ref.at[pl.ds(c * half + quarter * (half // 2), half // 2)]

    def copies(kind, ins, outs, sems):
        ici_send, ici_recv, pass_send, pass_recv, local_sems = sems
        x, y, c, chips = _position()
        mine = 2 * x + y
        if kind == "local":
            return [pltpu.make_async_copy(ins[t], outs[t].at[mine], local_sems.at[t]) for t in range(n)]
        ids = [2 * chip[0] + chip[1] for chip in chips]
        made = []
        for t in range(n):
            def ici(k, to):
                return dict(send_sem=ici_send.at[4 * t + k], recv_sem=ici_recv.at[4 * t + k],
                            device_id=(chips[to][0], chips[to][1], c), device_id_type=MESH)

            def d2d(k):
                return dict(send_sem=pass_send.at[4 * t + k], recv_sem=pass_recv.at[4 * t + k],
                            device_id=(x, y, 1 - c), device_id_type=MESH)

            def same(ref, where):
                return pltpu.make_async_remote_copy(src_ref=ref, dst_ref=ref, **where)

            if kind == "send":
                for k in range(2):
                    made.append(pltpu.make_async_remote_copy(
                        src_ref=piece(ins[t], t, c), dst_ref=piece(outs[t].at[mine], t, c), **ici(k, k)))
            elif kind == "landed":
                made += [same(piece(outs[t].at[ids[k]], t, c), ici(k, k)) for k in range(2)]
            elif kind == "forward":
                made.append(same(piece(outs[t].at[ids[0]], t, c, 0), ici(2, 1)))
                made.append(same(piece(outs[t].at[ids[1]], t, c, 1), ici(3, 0)))
            elif kind == "arrived":
                made.append(same(piece(outs[t].at[ids[2]], t, c, 0), ici(2, 1)))
                made.append(same(piece(outs[t].at[ids[2]], t, c, 1), ici(3, 0)))
            else:
                core = 1 - c if kind == "passed" else c
                if kind in ("pass halves", "passed"):
                    made += [same(piece(outs[t].at[ids[k]], t, core), d2d(k)) for k in range(2)]
                if kind in ("pass quarters", "passed"):
                    made += [same(piece(outs[t].at[ids[2]], t, core, k), d2d(2 + k)) for k in range(2)]
        return made

    def first(ins, outs, sems):
        for cp in copies("local", ins, outs, sems) + copies("send", ins, outs, sems):
            cp.start()

    def middle(ins, outs, sems):
        for got, onward, near in zip(copies("landed", ins, outs, sems), copies("forward", ins, outs, sems),
                                     copies("pass halves", ins, outs, sems)):
            got.wait_recv()
            near.start()
            onward.start()

    def last(ins, outs, sems):
        quarters = copies("pass quarters", ins, outs, sems)
        for got, near in zip(copies("arrived", ins, outs, sems), quarters):
            got.wait_recv()
            near.start()
        for cp in copies("passed", ins, outs, sems):
            cp.wait_recv()
        for cp in (copies("send", ins, outs, sems) + copies("forward", ins, outs, sems)
                   + copies("pass halves", ins, outs, sems) + quarters):
            cp.wait_send()
        for cp in copies("local", ins, outs, sems):
            cp.wait()

    return _Hosted(shards, [jax.ShapeDtypeStruct((4,) + w.shape, w.dtype) for w in shards],
                   [pltpu.SemaphoreType.DMA((4 * n,))] * 4 + [pltpu.SemaphoreType.DMA((n,))],
                   first, middle, last)


def _scatter_hosted(grads):
    n = len(grads)

    def copies(ins, outs, sems):
        send_sems, recv_sems = sems
        x, y, c, chips = _position()
        return [pltpu.make_async_remote_copy(
            src_ref=ins[t].at[2 * chip[0] + chip[1]], dst_ref=outs[t].at[j],
            send_sem=send_sems.at[3 * t + j], recv_sem=recv_sems.at[3 * t + j],
            device_id=(chip[0], chip[1], c), device_id_type=MESH)
            for t in range(n) for j, chip in enumerate(chips)]

    def first(ins, outs, sems):
        for cp in copies(ins, outs, sems):
            cp.start()

    def last(ins, outs, sems):
        for cp in copies(ins, outs, sems):
            cp.wait()

    return _Hosted(grads, [jax.ShapeDtypeStruct((3,) + g.shape[1:], g.dtype) for g in grads],
                   [pltpu.SemaphoreType.DMA((3 * n,))] * 2, first, None, last)


GATHER_PEERS = "x and y neighbours (same core) and the sibling core"
SCATTER_PEERS = "the same core of the three other chips"


def _run_on_sequencer(name, hosted, peers, collective_id):
    ins = [jax.new_ref(a, memory_space=pltpu.MemorySpace.HBM) for a in hosted.inputs]
    outs = [jax.empty_ref(shape, memory_space=pltpu.MemorySpace.HBM) for shape in hosted.out_shapes]

    @pl.kernel(mesh=plsc.ScalarSubcoreMesh(axis_name="sequencer", num_cores=1), name=name,
               scratch_types=tuple(hosted.sems), compiler_params=pltpu.CompilerParams(collective_id=collective_id))
    def launch(*sems):
        x, y, c, chips = _position()
        if peers == GATHER_PEERS:
            devices = [(chip[0], chip[1], c) for chip in chips[:2]] + [(x, y, 1 - c)]
        else:
            devices = [(chip[0], chip[1], c) for chip in chips]
        barrier = pltpu.get_barrier_semaphore()
        for device in devices:
            pl.semaphore_signal(barrier, inc=1, device_id=device, device_id_type=MESH)
        pl.semaphore_wait(barrier, len(devices))
        hosted.first(ins, outs, sems)
        if hosted.middle is not None:
            hosted.middle(ins, outs, sems)
        hosted.last(ins, outs, sems)

    launch()
    return [o[...] for o in outs]


def _run_alone(name, hosted):
    n_in = len(hosted.inputs)
    n_out = len(hosted.out_shapes)

    def body(*refs):
        ins, outs, sems = refs[:n_in], refs[n_in:n_in + n_out], refs[n_in + n_out:]
        hosted.first(ins, outs, sems)
        if hosted.middle is not None:
            hosted.middle(ins, outs, sems)
        hosted.last(ins, outs, sems)

    return pl.pallas_call(
        body, name=name, in_specs=[ANY] * n_in, out_specs=[ANY] * n_out, out_shape=hosted.out_shapes,
        scratch_shapes=hosted.sems)(*hosted.inputs)


def _swap_with_sibling(blocks):
    n = len(blocks)

    def body(*refs):
        ins, outs = refs[:n], refs[n:2 * n]
        send_sems, recv_sems = refs[2 * n:]
        x, y, c, _ = _position()
        sends = [pltpu.make_async_remote_copy(
            src_ref=ins[t], dst_ref=outs[t], send_sem=send_sems.at[t], recv_sem=recv_sems.at[t],
            device_id=(x, y, 1 - c), device_id_type=MESH) for t in range(n)]
        for cp in sends:
            cp.start()
        for cp in sends:
            cp.wait()

    return pl.pallas_call(
        body, name="swap_with_sibling",
        in_specs=[ANY] * n, out_specs=[ANY] * n,
        out_shape=[jax.ShapeDtypeStruct(b.shape, b.dtype) for b in blocks],
        scratch_shapes=[pltpu.SemaphoreType.DMA((n,))] * 2,
    )(*blocks)


def _small_step(partials, extras, ws, ms, vs, shard_of):
    n = len(partials)
    terms = list(partials) + list(extras)
    nt = len(terms)
    rows = [t for t in range(nt) if terms[t].shape[0] == 1]
    mats = [t for t in range(nt) if terms[t].shape[0] != 1]
    row_block = (8, max(terms[t].shape[1] for t in rows))
    assert len(rows) <= row_block[0]
    sent = [row_block] + [terms[t].shape for t in mats]

    def body(*refs):
        ins, refs = refs[:nt], refs[nt:]
        w_refs, refs = refs[:n], refs[n:]
        m_refs, refs = refs[:n], refs[n:]
        v_refs, refs = refs[:n], refs[n:]
        outs, refs = refs[:4 * n + nt - n], refs[4 * n + nt - n:]
        slots, (packed, send_sems, recv_sems) = refs[:len(sent)], refs[len(sent):]
        x, y, c, _ = _position()
        me = 4 * x + 2 * y + c
        packed[...] = jnp.zeros_like(packed)
        for i, t in enumerate(rows):
            packed[i:i + 1, 0:terms[t].shape[1]] = ins[t][...]
        sources = [packed] + [ins[t] for t in mats]
        sends = []
        for j, src in enumerate(sources):
            slots[j][me] = src[...]
            for k in range(1, 8):
                peer = (x ^ (k >> 2), y ^ ((k >> 1) & 1), c ^ (k & 1))
                sends.append(pltpu.make_async_remote_copy(
                    src_ref=src, dst_ref=slots[j].at[me], send_sem=send_sems.at[7 * j + k - 1],
                    recv_sem=recv_sems.at[7 * j + k - 1], device_id=peer, device_id_type=MESH))
        for cp in sends:
            cp.start()
        for j, src in enumerate(sources):
            for k in range(1, 8):
                pltpu.make_async_remote_copy(
                    src_ref=src, dst_ref=slots[j].at[me ^ k], send_sem=send_sems.at[7 * j + k - 1],
                    recv_sem=recv_sems.at[7 * j + k - 1], device_id=(x, y, c), device_id_type=MESH).wait_recv()
        for cp in sends:
            cp.wait_send()
        sums = []
        for j in range(len(sources)):
            g = slots[j][0]
            for dev in range(1, 8):
                g = g + slots[j][dev]
            sums.append(g)
        chip = 2 * x + y
        for t in range(nt):
            if t in rows:
                i = rows.index(t)
                g = sums[0][i:i + 1, 0:terms[t].shape[1]]
            else:
                g = sums[1 + mats.index(t)]
            if t >= n:
                outs[4 * n + t - n][...] = g
                continue
            if shard_of[t]:
                width = ws[t].shape[-1]
                mine = jnp.zeros(ws[t].shape, F32)
                for s in range(4):
                    mine = jnp.where(chip == s, g[:, s * width:(s + 1) * width], mine)
                g = mine
            delta, mn, vn = _adamw_math(w_refs[t][...], g, m_refs[t][...], v_refs[t][...])
            outs[4 * t][...] = g
            outs[4 * t + 1][...] = delta
            outs[4 * t + 2][...] = mn
            outs[4 * t + 3][...] = vn

    vmem = pl.BlockSpec(memory_space=pltpu.VMEM)
    out_shapes = []
    for t in range(n):
        out_shapes += [jax.ShapeDtypeStruct(ws[t].shape, F32)] * 4
    out_shapes += [jax.ShapeDtypeStruct(a.shape, F32) for a in extras]
    out_shapes += [jax.ShapeDtypeStruct((8,) + tuple(shape), F32) for shape in sent]
    res = pl.pallas_call(
        body, name="small_step",
        in_specs=[vmem] * (nt + 3 * n), out_specs=[vmem] * len(out_shapes), out_shape=out_shapes,
        scratch_shapes=[pltpu.VMEM(row_block, F32)] + [pltpu.SemaphoreType.DMA((7 * len(sent),))] * 2,
    )(*terms, *ws, *ms, *vs)
    return [res[4 * t:4 * t + 4] for t in range(n)], res[4 * n:4 * n + nt - n]


def _adamw_math(w, g, m, v):
    m = ADAM_B1 * m + (1.0 - ADAM_B1) * g
    v = ADAM_B2 * v + (1.0 - ADAM_B2) * (g * g)
    m_hat = m / (1.0 - ADAM_B1 ** ADAM_STEP)
    v_hat = v / (1.0 - ADAM_B2 ** ADAM_STEP)
    delta = -ADAM_LR * (m_hat / (jnp.sqrt(v_hat) + ADAM_EPS) + ADAM_WD * w)
    return delta, m, v


def _row_tile(rows):
    return 256 if rows % 256 == 0 else rows


def _sum_partials(name, own, recv, chip):
    rows, cols = own.shape[1:]
    tr = _row_tile(rows)

    def body(chip_ref, own_ref, recv_ref, o_ref):
        acc = own_ref[...]
        for j in range(3):
            acc = acc + recv_ref[j].astype(F32)
        o_ref[...] = acc

    return pl.pallas_call(
        body, name=name,
        grid_spec=pltpu.PrefetchScalarGridSpec(
            num_scalar_prefetch=1, grid=(rows // tr,),
            in_specs=[pl.BlockSpec((None, tr, cols), lambda i, chip_ref: (chip_ref[0], i, 0)),
                      pl.BlockSpec((3, tr, cols), lambda i, chip_ref: (0, i, 0))],
            out_specs=pl.BlockSpec((tr, cols), lambda i, chip_ref: (i, 0))),
        out_shape=jax.ShapeDtypeStruct((rows, cols), F32),
        compiler_params=_params(("parallel",)),
    )(chip.reshape(1).astype(jnp.int32), own, recv)


def _adamw(name, w, m, v, g_parts):
    rows, cols = w.shape
    tr = _row_tile(rows)
    n = len(g_parts)

    def body(w_ref, m_ref, v_ref, *refs):
        g_refs = refs[:n]
        go_ref, d_ref, mo_ref, vo_ref = refs[n:]
        g = g_refs[0][...]
        for r in g_refs[1:]:
            g = g + r[...]
        delta, mn, vn = _adamw_math(w_ref[...], g, m_ref[...], v_ref[...])
        go_ref[...] = g
        d_ref[...] = delta
        mo_ref[...] = mn
        vo_ref[...] = vn

    spec = pl.BlockSpec((tr, cols), lambda i: (i, 0))
    return pl.pallas_call(
        body, name=name, grid=(rows // tr,),
        in_specs=[spec] * (3 + n), out_specs=[spec] * 4,
        out_shape=[jax.ShapeDtypeStruct((rows, cols), F32)] * 4,
        compiler_params=_params(("parallel",)),
    )(w, m, v, *g_parts)


def _local_step(x, target, ga, wa_in, rel_bias, later_shards, gk, t5, gb, sinks, gf):
    s, d = x.shape
    tm = min(TM_DENSE, s)
    nt = s // tm
    half = d // 2
    row = pl.BlockSpec((tm, d), lambda i: (i, 0))
    whole = lambda shape: pl.BlockSpec(shape, lambda *_: (0,) * len(shape))

    after_first = (wa_in[0, :1, :1] * 0).astype(BF16)
    gathered = _run_on_sequencer("allgather_later", _allgather_routed([sh + after_first for sh in later_shards]),
                                 GATHER_PEERS, 2)
    n1, = _norm_fwd("norm_a", x, ga)
    zqkv = _matmul("proj_a_qkv", n1, wa_in, dims=NN, grid=(3, nt + 1), zero_axis=1,
                   a_spec=pl.BlockSpec((tm, d), lambda j, i: (jnp.maximum(i - 1, 0), 0)),
                   b_spec=pl.BlockSpec((None, d, d), lambda j, i: (j, 0, 0)),
                   o_spec=pl.BlockSpec((None, tm, d), lambda j, i: (j, i, 0)),
                   out_shape=(3, tm + s, d), out_dtype=BF16)
    gate_a = _matmul("proj_a_gate", n1, wa_in, dims=NN, grid=(nt,),
                     a_spec=row, b_spec=pl.BlockSpec((None, d, d), lambda i: (3, 0, 0)), o_spec=row,
                     out_shape=(s, d), out_dtype=F32)
    onehot_a = _a_offset_onehot()
    diag_a = _diag_rows(onehot_a, rel_bias)
    (o_a, u_a, lse_a), _ = _attn_a_fwd(zqkv, gate_a, diag_a)
    wa_out, wkv, wb_in, wb_out = gathered
    wa_out = wa_out.reshape(d, d)
    wkv = wkv.reshape(d, -1)
    wb_out = wb_out.reshape(d, d)
    h1 = _matmul("out_a", u_a, wa_out, dims=NN, grid=(nt,), a_spec=row, b_spec=whole((d, d)), o_spec=row,
                 out_shape=(s, d), out_dtype=F32, resid=x, resid_spec=row)

    nk, n2 = _norm_fwd("norm_kv_b", h1, jnp.concatenate([gk, gb], axis=0))
    kvw = wkv.shape[1]
    wkv_x = jnp.concatenate([wkv[:, (i // 2) * HEAD_DIM:(i // 2 + 1) * HEAD_DIM] for i in range(8)], axis=1)
    kvx = _matmul("proj_kv", nk, wkv_x, dims=NN, grid=(nt + 1,), zero_axis=0,
                  a_spec=pl.BlockSpec((tm, d), lambda i: (jnp.maximum(i - 1, 0), 0)), b_spec=whole((d, B_KVX)),
                  o_spec=pl.BlockSpec((tm, B_KVX), lambda i: (i, 0)), out_shape=(tm + s, B_KVX), out_dtype=BF16)
    qb = _matmul("proj_b_q", n2, wb_in, dims=NN, grid=(2, nt),
                 a_spec=pl.BlockSpec((tm, d), lambda j, i: (i, 0)),
                 b_spec=pl.BlockSpec((None, d, half), lambda j, i: (j, 0, 0)),
                 o_spec=pl.BlockSpec((tm, half), lambda j, i: (i, j)), out_shape=(s, d), out_dtype=BF16)
    gate_b = _matmul("proj_b_gate", n2, wb_in, dims=NN, grid=(2, nt),
                     a_spec=pl.BlockSpec((tm, d), lambda j, i: (i, 0)),
                     b_spec=pl.BlockSpec((None, d, half), lambda j, i: (2 + j, 0, 0)),
                     o_spec=pl.BlockSpec((tm, half), lambda j, i: (i, j)), out_shape=(s, d), out_dtype=F32)
    onehot_b = _b_offset_onehot()
    base_b = jnp.roll(_diag_rows(onehot_b, t5)[..., ::-1], TQ, axis=-1)
    o_b, u_b, lse_b = _attn_b_fwd(qb, kvx, gate_b, base_b, sinks)
    h2 = _matmul("out_b", u_b, wb_out, dims=NN, grid=(nt,), a_spec=row, b_spec=whole((d, d)), o_spec=row,
                 out_shape=(s, d), out_dtype=F32, resid=h1, resid_spec=row)

    dh2, loss, d_gf = _loss_head(h2, target, gf)

    du_b = _matmul("dout_b", dh2, wb_out, dims=NT, grid=(nt,), a_spec=row, b_spec=whole((d, d)), o_spec=row,
                   out_shape=(s, d), out_dtype=F32)
    d_wb_out = _matmul("dw_out_b", u_b, dh2, dims=TN, grid=(2,),
                       a_spec=whole((s, d)), b_spec=pl.BlockSpec((s, half), lambda j: (0, j)),
                       o_spec=pl.BlockSpec((d, half), lambda j: (0, j)),
                       out_shape=(d, d), out_dtype=F32, also_bf16=True)
    dz_b, dkv, dsum_b, dsinks = _attn_b_bwd(qb, kvx, gate_b, o_b, du_b, lse_b, base_b, sinks)
    ddiag_b = jnp.roll(dsum_b[..., ::-1], -1, axis=-1)
    d_wb_in = _matmul("dw_in_b", n2, dz_b, dims=TN, grid=(4,),
                      a_spec=whole((s, d)), b_spec=pl.BlockSpec((None, s, half), lambda j: (j, 0, 0)),
                      o_spec=pl.BlockSpec((None, d, half), lambda j: (j, 0, 0)),
                      out_shape=(4, d, half), out_dtype=F32, also_bf16=True)
    d_wkv = _matmul("dw_kv", nk, dkv, dims=TN, grid=(1,),
                    a_spec=whole((s, d)), b_spec=whole((s, kvw)), o_spec=whole((d, kvw)),
                    out_shape=(d, kvw), out_dtype=F32, also_bf16=True)
    dh1, d_gkb = _proj_norm_bwd("dproj_kv_b", h1, dh2, jnp.concatenate([gk, gb], axis=0),
                                [(dkv[None], wkv[None]), (dz_b, wb_in)])

    du_a = _matmul("dout_a", dh1, wa_out, dims=NT, grid=(nt,), a_spec=row, b_spec=whole((d, d)), o_spec=row,
                   out_shape=(s, d), out_dtype=F32)
    d_wa_out = _matmul("dw_out_a", u_a, dh1, dims=TN, grid=(2,),
                       a_spec=whole((s, d)), b_spec=pl.BlockSpec((s, half), lambda j: (0, j)),
                       o_spec=pl.BlockSpec((d, half), lambda j: (0, j)),
                       out_shape=(d, d), out_dtype=F32, also_bf16=True)
    early = dict(a_w_out=[g.reshape(4, d // 4, d) for g in d_wa_out],
                 kv_w=[g.reshape(4, d // 4, kvw) for g in d_wkv], b_w_in=list(d_wb_in),
                 b_w_out=[g.reshape(4, d // 4, d) for g in d_wb_out])
    early_recv = _run_on_sequencer("scatter_early", _scatter_hosted([early[n][1] for n in early]),
                                   SCATTER_PEERS, 3)
    (dz_a, ddiag_a), _ = _attn_a_bwd(zqkv, gate_a, o_a, du_a, lse_a, diag_a)
    d_wa_in = _matmul("dw_in_a", n1, dz_a, dims=TN, grid=(4, 2),
                      a_spec=whole((s, d)), b_spec=pl.BlockSpec((None, s, half), lambda j, h: (j, 0, h)),
                      o_spec=pl.BlockSpec((None, d, half), lambda j, h: (j, 0, h)),
                      out_shape=(4, d, d), out_dtype=F32, also_bf16=True)
    late_recv = _run_on_sequencer("scatter_a_w_in", _scatter_hosted([d_wa_in[1]]), SCATTER_PEERS, 0)
    grad_x, d_ga = _proj_norm_bwd("dproj_a", x, dh1, ga, [(dz_a, wa_in)])

    small = dict(a_norm=d_ga, kv_norm=d_gkb[0:1], b_norm=d_gkb[1:2], b_sinks=dsinks[0:1, :HEADS], final_norm=d_gf)
    small["by_offset"] = dict(a_rel_bias=(onehot_a, ddiag_a.reshape(HEADS, -1)),
                              t5_bias=(onehot_b, ddiag_b.reshape(HEADS, -1)))
    own = dict(a_w_in=d_wa_in[0], **{n: early[n][0] for n in early})
    received = dict(a_w_in=late_recv[0], **dict(zip(early, early_recv)))
    return loss, grad_x, small, own, received


SMALL = ("a_norm", "kv_norm", "b_norm", "b_sinks", "final_norm")
TABLES = ("a_rel_bias", "t5_bias")
BIG = ("a_w_in", "a_w_out", "kv_w", "b_w_in", "b_w_out")
ORDER = ("a_norm", "a_w_in", "a_rel_bias", "a_w_out", "kv_norm", "kv_w", "t5_bias", "b_norm", "b_w_in",
         "b_sinks", "b_w_out", "final_norm")


def kernel(x, a_norm, a_w_in, a_rel_bias, a_w_out, kv_norm, kv_w, t5_bias, b_norm, b_w_in, b_sinks, b_w_out, final_norm, loss_target, m_a_norm, m_a_w_in, m_a_rel_bias, m_a_w_out, m_kv_norm, m_kv_w, m_t5_bias, m_b_norm, m_b_w_in, m_b_sinks, m_b_w_out, m_final_norm, v_a_norm, v_a_w_in, v_a_rel_bias, v_a_w_out, v_kv_norm, v_kv_w, v_t5_bias, v_b_norm, v_b_w_in, v_b_sinks, v_b_w_out, v_final_norm):
    w = dict(a_norm=a_norm, a_w_in=a_w_in, a_rel_bias=a_rel_bias, a_w_out=a_w_out, kv_norm=kv_norm, kv_w=kv_w,
             t5_bias=t5_bias, b_norm=b_norm, b_w_in=b_w_in, b_sinks=b_sinks, b_w_out=b_w_out,
             final_norm=final_norm)
    m = dict(a_norm=m_a_norm, a_w_in=m_a_w_in, a_rel_bias=m_a_rel_bias, a_w_out=m_a_w_out, kv_norm=m_kv_norm,
             kv_w=m_kv_w, t5_bias=m_t5_bias, b_norm=m_b_norm, b_w_in=m_b_w_in, b_sinks=m_b_sinks,
             b_w_out=m_b_w_out, final_norm=m_final_norm)
    v = dict(a_norm=v_a_norm, a_w_in=v_a_w_in, a_rel_bias=v_a_rel_bias, a_w_out=v_a_w_out, kv_norm=v_kv_norm,
             kv_w=v_kv_w, t5_bias=v_t5_bias, b_norm=v_b_norm, b_w_in=v_b_w_in, b_sinks=v_b_sinks,
             b_w_out=v_b_w_out, final_norm=v_final_norm)
    d = D_MODEL
    chip = 2 * lax.axis_index("x") + lax.axis_index("y")

    shard2d = dict(a_w_in=a_w_in[0], a_w_out=a_w_out[0], kv_w=kv_w, b_w_in=b_w_in[0], b_w_out=b_w_out[0])

    wa_in, = _run_on_sequencer("allgather_first", _allgather_routed([shard2d["a_w_in"].astype(BF16)]),
                               GATHER_PEERS, 1)
    ga, = _run_alone("allgather_norm", _allgather_hosted([a_norm], [False]))
    ga = ga.reshape(1, d)

    loss, grad_x, small, own, received = _local_step(
        x[0], loss_target[0], ga, wa_in, a_rel_bias[0], [shard2d[n].astype(BF16) for n in BIG[1:]],
        kv_norm.reshape(1, d), t5_bias, b_norm, b_sinks, final_norm.reshape(1, d))

    out = {}
    as2d = lambda a: a.reshape(-1, a.shape[-1])
    small_res, (loss_sum, *offset_sums) = _small_step(
        [small[n] for n in SMALL], [loss] + [small["by_offset"][n][1] for n in TABLES],
        [as2d(w[n]) for n in SMALL], [as2d(m[n]) for n in SMALL], [as2d(v[n]) for n in SMALL],
        [n == "a_norm" for n in SMALL])
    for n, res in zip(SMALL, small_res):
        out[n] = [r.reshape(w[n].shape) for r in res]
    loss_out = loss_sum.reshape(())
    for n, summed in zip(TABLES, offset_sums):
        grad = _diag_rows_grad(small["by_offset"][n][0], summed)
        res = _adamw("adamw_" + n, as2d(w[n]), as2d(m[n]), as2d(v[n]), [grad])
        out[n] = [r.reshape(w[n].shape) for r in res]

    core_sums = [_sum_partials("sum_" + n, own[n], received[n], chip) for n in BIG]
    sibling_sums = _swap_with_sibling(core_sums)

    for n, mine, theirs in zip(BIG, core_sums, sibling_sums):
        res = _adamw("adamw_" + n, shard2d[n], m[n].reshape(shard2d[n].shape), v[n].reshape(shard2d[n].shape),
                     [mine, theirs])
        out[n] = [r.reshape(w[n].shape) for r in res]

    grads = [out[n][0] for n in ORDER]
    deltas = [out[n][1] for n in ORDER]
    new_m = [out[n][2] for n in ORDER]
    new_v = [out[n][3] for n in ORDER]
    return (loss_out, grad_x[None], *grads, *deltas, *new_m, *new_v)
```

```python
import functools
import math

import jax
import jax.numpy as jnp
import numpy as np
from jax import lax
from jax.experimental import pallas as pl
from jax.experimental.pallas import tpu as pltpu
from jax.experimental.pallas import tpu_sc as plsc

F32 = jnp.float32
BF16 = jnp.bfloat16
MESH = pl.DeviceIdType.MESH

D_MODEL = 1024
HEADS = 16
HEAD_DIM = 64
CHUNK = 64
RMS_EPS = 1e-6
SCALE = HEAD_DIM ** -0.5
A_LEFT_CHUNKS = 8
A_REL_CLIP = 256
B_LEFT_CHUNKS = 2
B_KV_HEADS = 2
B_GROUP = HEADS // B_KV_HEADS
T5_BUCKETS = 32
T5_MAX_DIST = 128
ADAM_LR = 0.001
ADAM_B1 = 0.9
ADAM_B2 = 0.999
ADAM_EPS = 1e-08
ADAM_WD = 0.01
ADAM_STEP = 10

MASKED = -1e30
LANES = 128
TQ = 128
A_PAIRS = 2
A_PAIRS_FWD = 4
KB = 128
A_KBLOCKS = A_LEFT_CHUNKS * CHUNK // KB + 1
B_KBLOCKS = B_LEFT_CHUNKS * CHUNK // KB + 1
A_WIN = A_KBLOCKS * KB
B_WIN = B_KBLOCKS * KB
TM = 512
TM_DENSE = 1024
TM_PARTS = 512
VMEM_LIMIT = 56 * 1024 * 1024

NT = (((1,), (1,)), ((), ()))
TN = (((0,), (0,)), ((), ()))
NN = (((1,), (0,)), ((), ()))


def _params(sem=None):
    return pltpu.CompilerParams(dimension_semantics=sem, vmem_limit_bytes=VMEM_LIMIT)


class _Hosted:
    def __init__(self, inputs, out_shapes, sems, first, middle, last):
        self.inputs, self.out_shapes, self.sems = list(inputs), list(out_shapes), list(sems)
        self.first, self.middle, self.last = first, middle, last


def _call(body, *, name, grid, in_specs, out_specs, out_shape, args, scratch_shapes=(), sem=None, hosted=None):
    in_specs, out_specs, out_shape = list(in_specs), list(out_specs), list(out_shape)
    scratch_shapes = list(scratch_shapes)
    if hosted is None:
        out = pl.pallas_call(
            body, name=name, grid=grid, in_specs=in_specs, out_specs=out_specs, out_shape=out_shape,
            scratch_shapes=scratch_shapes, compiler_params=_params(sem))(*args)
        return list(out), []
    n_in, n_out, n_scr = len(in_specs), len(out_shape), len(scratch_shapes)
    h_in, h_out = len(hosted.inputs), len(hosted.out_shapes)
    total = int(np.prod(grid)) if grid else 1

    def wrapped(*refs):
        ins, refs = refs[:n_in], refs[n_in:]
        h_ins, refs = refs[:h_in], refs[h_in:]
        outs, refs = refs[:n_out], refs[n_out:]
        h_outs, refs = refs[:h_out], refs[h_out:]
        scr, h_sems = refs[:n_scr], refs[n_scr:]
        step = 0
        for axis, size in enumerate(grid):
            step = step * size + pl.program_id(axis)

        @pl.when(step == 0)
        def _():
            hosted.first(h_ins, h_outs, h_sems)

        body(*ins, *outs, *scr)
        if hosted.middle is not None:
            @pl.when(step == total // 2)
            def _():
                hosted.middle(h_ins, h_outs, h_sems)

        @pl.when(step == total - 1)
        def _():
            hosted.last(h_ins, h_outs, h_sems)

    out = pl.pallas_call(
        wrapped, name=name, grid=grid, in_specs=in_specs + [ANY] * h_in, out_specs=out_specs + [ANY] * h_out,
        out_shape=out_shape + hosted.out_shapes, scratch_shapes=scratch_shapes + hosted.sems,
        compiler_params=_params(("arbitrary",) * len(grid)))(*args, *hosted.inputs)
    return list(out[:n_out]), list(out[n_out:])


def _matmul(name, a, b, *, dims, grid, a_spec, b_spec, o_spec, out_shape, out_dtype,
            parts=1, resid=None, resid_spec=None, also_bf16=False, hosted=None, zero_axis=None):
    def body(*refs):
        if zero_axis is None:
            product(*refs)
        else:
            @pl.when(pl.program_id(zero_axis) == 0)
            def _():
                refs[2][...] = jnp.zeros_like(refs[2])

            @pl.when(pl.program_id(zero_axis) > 0)
            def _():
                product(*refs)

    def product(*refs):
        a_ref, b_ref = refs[:2]
        r_ref = refs[2] if resid is not None else None
        o_ref = refs[3] if resid is not None else refs[2]
        if parts == 1:
            prod = lax.dot_general(a_ref[...].astype(BF16), b_ref[...].astype(BF16), dims,
                                   preferred_element_type=F32)
        else:
            prod = None
            for part in range(parts):
                term = lax.dot_general(a_ref[part].astype(BF16), b_ref[part].astype(BF16), dims,
                                       preferred_element_type=F32)
                prod = term if prod is None else prod + term
        if resid is not None:
            prod = r_ref[...] + prod
        o_ref[...] = prod.astype(out_dtype)
        if also_bf16:
            refs[-1][...] = prod.astype(BF16)

    in_specs = [a_spec, b_spec]
    args = [a, b]
    if resid is not None:
        in_specs.append(resid_spec)
        args.append(resid)
    sem = ["parallel"] * len(grid)
    out_specs = [o_spec]
    out_shapes = [jax.ShapeDtypeStruct(out_shape, out_dtype)]
    if also_bf16:
        out_specs.append(o_spec)
        out_shapes.append(jax.ShapeDtypeStruct(out_shape, BF16))
    out, extra = _call(body, name=name, grid=grid, in_specs=in_specs, out_specs=out_specs, out_shape=out_shapes,
                       args=args, sem=tuple(sem), hosted=hosted)
    res = out[0] if not also_bf16 else tuple(out)
    return res if hosted is None else (res, extra)


def _rms_rows(x):
    return lax.rsqrt(jnp.mean(x * x, axis=-1, keepdims=True) + RMS_EPS)


def _norm_fwd(name, x, gains):
    s, d = x.shape
    n = gains.shape[0]

    def body(x_ref, g_ref, *o_refs):
        xv = x_ref[...]
        xh = xv * _rms_rows(xv)
        for i in range(n):
            o_refs[i][...] = (xh * g_ref[i:i + 1, :]).astype(BF16)

    row = pl.BlockSpec((TM, d), lambda i: (i, 0))
    return pl.pallas_call(
        body, name=name, grid=(s // TM,),
        in_specs=[row, pl.BlockSpec((n, d), lambda i: (0, 0))],
        out_specs=[row] * n,
        out_shape=[jax.ShapeDtypeStruct((s, d), BF16)] * n,
        compiler_params=_params(("parallel",)),
    )(x, gains)


def _proj_norm_bwd(name, x, dres, gains, branches):
    s, d = x.shape
    n = len(branches)
    tm = min(TM_PARTS, s)

    def body(x_ref, r_ref, g_ref, *refs):
        ab_refs, dx_ref, dg_ref = refs[:2 * n], refs[2 * n], refs[2 * n + 1]
        i = pl.program_id(0)
        xv = x_ref[...]
        r = _rms_rows(xv)
        xh = xv * r

        @pl.when(i == 0)
        def _():
            dg_ref[...] = jnp.zeros_like(dg_ref)

        a = None
        for j in range(n):
            a_ref, b_ref = ab_refs[2 * j], ab_refs[2 * j + 1]
            dn = None
            for part in range(a_ref.shape[0]):
                term = lax.dot_general(a_ref[part], b_ref[part], NT, preferred_element_type=F32)
                dn = term if dn is None else dn + term
            t = dn * g_ref[j:j + 1, :]
            a = t if a is None else a + t
            dg_ref[j:j + 1, :] += jnp.sum(dn * xh, axis=0, keepdims=True)
        dx_ref[...] = r_ref[...] + r * (a - xh * jnp.mean(xh * a, axis=-1, keepdims=True))

    row = pl.BlockSpec((tm, d), lambda i: (i, 0))
    small = pl.BlockSpec((n, d), lambda i: (0, 0))
    ab_specs, ab_args = [], []
    for a, b in branches:
        ab_specs += [pl.BlockSpec((a.shape[0], tm, a.shape[2]), lambda i: (0, i, 0)),
                     pl.BlockSpec(b.shape, lambda i: (0, 0, 0))]
        ab_args += [a, b]
    return pl.pallas_call(
        body, name=name, grid=(s // tm,),
        in_specs=[row, row, small] + ab_specs,
        out_specs=[row, small],
        out_shape=[jax.ShapeDtypeStruct((s, d), F32), jax.ShapeDtypeStruct((n, d), F32)],
        compiler_params=_params(("arbitrary",)),
    )(x, dres, gains, *ab_args)


def _out_norms(name, u, w_out, resid, gains):
    s, d = resid.shape
    n = gains.shape[0]
    tm = min(TM_DENSE, s)

    def body(u_ref, w_ref, r_ref, g_ref, h_ref, *o_refs):
        hv = r_ref[...] + jnp.dot(u_ref[...], w_ref[...], preferred_element_type=F32)
        h_ref[...] = hv
        hh = hv * _rms_rows(hv)
        for i in range(n):
            o_refs[i][...] = (hh * g_ref[i:i + 1, :]).astype(BF16)

    row = pl.BlockSpec((tm, d), lambda i: (i, 0))
    return pl.pallas_call(
        body, name=name, grid=(s // tm,),
        in_specs=[row, pl.BlockSpec((d, d), lambda i: (0, 0)), row, pl.BlockSpec((n, d), lambda i: (0, 0))],
        out_specs=[row] * (n + 1),
        out_shape=[jax.ShapeDtypeStruct((s, d), F32)] + [jax.ShapeDtypeStruct((s, d), BF16)] * n,
        compiler_params=_params(("parallel",)),
    )(u, w_out, resid, gains)


def _out_loss_head(u, w_out, resid, target, gain):
    s, d = resid.shape
    tm = min(TM_PARTS, s)

    def body(u_ref, w_ref, r_ref, t_ref, g_ref, dh_ref, loss_ref, dg_ref):
        i = pl.program_id(0)
        hv = r_ref[...] + jnp.dot(u_ref[...], w_ref[...], preferred_element_type=F32)
        r = _rms_rows(hv)
        hh = hv * r
        g = g_ref[...]
        err = hh * g - t_ref[...]
        part = 0.5 * jnp.sum(jnp.sum(err * err, axis=-1, keepdims=True) * (1.0 / d), axis=0, keepdims=True)
        dy = err * (1.0 / d)
        a = dy * g
        dh_ref[...] = r * (a - hh * jnp.mean(hh * a, axis=-1, keepdims=True))
        dg = jnp.sum(dy * hh, axis=0, keepdims=True)

        @pl.when(i == 0)
        def _():
            loss_ref[...] = part
            dg_ref[...] = dg

        @pl.when(i > 0)
        def _():
            loss_ref[...] += part
            dg_ref[...] += dg

    row = pl.BlockSpec((tm, d), lambda i: (i, 0))
    return pl.pallas_call(
        body, name="out_b_loss_head", grid=(s // tm,),
        in_specs=[row, pl.BlockSpec((d, d), lambda i: (0, 0)), row, row, pl.BlockSpec((1, d), lambda i: (0, 0))],
        out_specs=[row, pl.BlockSpec((1, 1), lambda i: (0, 0)), pl.BlockSpec((1, d), lambda i: (0, 0))],
        out_shape=[jax.ShapeDtypeStruct((s, d), F32), jax.ShapeDtypeStruct((1, 1), F32),
                   jax.ShapeDtypeStruct((1, d), F32)],
        compiler_params=_params(("arbitrary",)),
    )(u, w_out, resid, target, gain)


def _silu_parts(g):
    sig = jax.nn.sigmoid(g)
    return g * sig, sig * (1.0 + g * (1.0 - sig))


def _lane_lo(rows):
    return lax.broadcasted_iota(jnp.int32, (rows, LANES), 1) < HEAD_DIM


def _stack_pair(x):
    lo = _lane_lo(x.shape[0])
    zero = jnp.zeros_like(x)
    return jnp.concatenate([jnp.where(lo, x, zero), jnp.where(lo, zero, x)], axis=0)


def _unstack_pair(y, w):
    return jnp.where(_lane_lo(w), y[:w], y[w:])


def _block_valid(b, left_blocks, width):
    col = lax.broadcasted_iota(jnp.int32, (1, 2 * width), 1)
    col = jnp.where(col >= width, col - width, col)
    return (col // KB + (b - left_blocks)) >= 0


def _toeplitz_tile(diag_row, width, left_chunks):
    wide = width + TQ
    rolled = pltpu.roll(jnp.broadcast_to(diag_row, (TQ, wide)), 1, 1, stride=1, stride_axis=0)
    i = lax.broadcasted_iota(jnp.int32, (TQ, width), 0) // CHUNK
    j = lax.broadcasted_iota(jnp.int32, (TQ, width), 1) // CHUNK
    dc = i + left_chunks - j
    return jnp.where((dc >= 0) & (dc <= left_chunks), rolled[:, TQ:], MASKED)


def _toeplitz_sum(tile, width):
    flip = (lax.broadcasted_iota(jnp.int32, (TQ, TQ), 0) + lax.broadcasted_iota(jnp.int32, (TQ, TQ), 1)
            == TQ - 1).astype(F32)
    reversed_rows = jnp.dot(flip, tile, precision=lax.Precision.HIGHEST, preferred_element_type=F32)
    padded = jnp.concatenate([reversed_rows, jnp.zeros((TQ, TQ), F32)], axis=1)
    rolled = pltpu.roll(padded, 0, 1, stride=1, stride_axis=0)
    return jnp.sum(rolled, axis=0, keepdims=True)


def _softmax_pair(sc, w, sink=None):
    ps, inv, lses = [], [], []
    for e in range(2):
        sh = sc[:, e * w:(e + 1) * w]
        m = jnp.max(sh, axis=-1, keepdims=True)
        if sink is not None:
            m = jnp.maximum(m, sink[e])
        ex = jnp.exp(sh - m)
        l = jnp.sum(ex, axis=-1, keepdims=True)
        if sink is not None:
            l = l + jnp.exp(sink[e] - m)
        ps.append(ex.astype(BF16))
        inv.append(1.0 / l)
        lses.append(m + jnp.log(l))
    return jnp.concatenate(ps, axis=-1), inv, lses


def _softmax_pair_bwd(sc, dp, lse, delta, w):
    ps, dss = [], []
    for e in range(2):
        p = jnp.exp(sc[:, e * w:(e + 1) * w] - lse[e])
        ps.append(p)
        dss.append(p * (dp[:, e * w:(e + 1) * w] - delta[e]))
    return jnp.concatenate(ps, axis=-1), jnp.concatenate(dss, axis=-1)


def _pair_rowsums(x, lo):
    zero = jnp.zeros_like(x)
    return (jnp.sum(jnp.where(lo, x, zero), axis=-1, keepdims=True),
            jnp.sum(jnp.where(lo, zero, x), axis=-1, keepdims=True))


def _a_qkv_specs(rows, pad, pw):
    return [pl.BlockSpec((None, TQ, pw), lambda p, b: (0, b + pad // TQ, p)),
            pl.BlockSpec((None, rows, pw), lambda p, b: (1, 0, p)),
            pl.BlockSpec((None, rows, pw), lambda p, b: (2, 0, p))]


def _window(ref, b, pad, win, lanes):
    start = pl.multiple_of(b * TQ + pad - (win - TQ), KB)
    return ref[pl.ds(start, win), lanes]


def _attn_a_fwd(zqkv, g, diag, hosted=None):
    s = g.shape[0]
    pad = zqkv.shape[1] - s
    nb = s // TQ
    left = A_KBLOCKS - 1
    pairs = A_PAIRS_FWD
    pw = pairs * LANES
    wide = A_WIN + TQ

    def body(q_ref, k_ref, v_ref, g_ref, diag_ref, o_ref, u_ref, lse_ref, bias_scr):
        b = pl.program_id(1)

        @pl.when(b == 0)
        def _():
            for hh in range(2 * pairs):
                bias_scr[hh // 2, :, (hh % 2) * A_WIN:(hh % 2 + 1) * A_WIN] = _toeplitz_tile(
                    diag_ref[hh], A_WIN, A_LEFT_CHUNKS)

        def step(first_blocks):
            lo = _lane_lo(TQ)
            for pp in range(pairs):
                ln = slice(pp * LANES, (pp + 1) * LANES)
                kcat = _stack_pair(_window(k_ref, b, pad, A_WIN, ln))
                vcat = _stack_pair(_window(v_ref, b, pad, A_WIN, ln))
                sc = lax.dot_general(q_ref[:, ln] * SCALE, kcat, NT, preferred_element_type=F32) + bias_scr[pp]
                if first_blocks:
                    sc = jnp.where(_block_valid(b, left, A_WIN), sc, MASKED)
                p, inv, lses = _softmax_pair(sc, A_WIN)
                ov = jnp.dot(p, vcat, preferred_element_type=F32) * jnp.where(lo, inv[0], inv[1])
                o_ref[:, ln] = ov
                lse_ref[pp] = jnp.where(lo, lses[0], lses[1])
                sg, _ = _silu_parts(g_ref[:, ln])
                u_ref[:, ln] = (ov * sg).astype(BF16)

        @pl.when(b < left)
        def _():
            step(True)

        @pl.when(b >= left)
        def _():
            step(False)

    tile = pl.BlockSpec((TQ, pw), lambda p, b: (b, p))
    return _call(
        body, name="attn_a_fwd", grid=(HEADS // 2 // pairs, nb),
        in_specs=_a_qkv_specs(pad + s, pad, pw) + [
            tile, pl.BlockSpec((2 * pairs, 1, wide), lambda p, b: (p, 0, 0))],
        out_specs=[tile, tile, pl.BlockSpec((pairs, TQ, LANES), lambda p, b: (p, b, 0))],
        out_shape=[jax.ShapeDtypeStruct((s, D_MODEL), F32), jax.ShapeDtypeStruct((s, D_MODEL), BF16),
                   jax.ShapeDtypeStruct((HEADS // 2, s, LANES), F32)],
        scratch_shapes=[pltpu.VMEM((pairs, TQ, 2 * A_WIN), F32)],
        sem=("parallel", "arbitrary"), hosted=hosted,
        args=(zqkv, zqkv, zqkv, g, diag))


def _attn_a_bwd(zqkv, g, o, du, lse, diag, hosted=None):
    s = g.shape[0]
    pad = zqkv.shape[1] - s
    nb = s // TQ
    left = A_KBLOCKS - 1
    pw = A_PAIRS * LANES
    wide = A_WIN + TQ

    def body(q_ref, k_ref, v_ref, g_ref, o_ref, du_ref, lse_ref, diag_ref, dz_ref, ddiag_ref,
             bias_scr, dbias_acc, dk_acc, dv_acc):
        b = pl.program_id(1)

        @pl.when(b == 0)
        def _():
            for hh in range(2 * A_PAIRS):
                bias_scr[hh // 2, :, (hh % 2) * A_WIN:(hh % 2 + 1) * A_WIN] = _toeplitz_tile(
                    diag_ref[hh], A_WIN, A_LEFT_CHUNKS)
            dbias_acc[...] = jnp.zeros_like(dbias_acc)
            dk_acc[...] = jnp.zeros_like(dk_acc)
            dv_acc[...] = jnp.zeros_like(dv_acc)

        def step(first_blocks):
            lo = _lane_lo(TQ)
            upper = lax.broadcasted_iota(jnp.int32, (LANES, A_WIN), 0) < HEAD_DIM
            rows = pl.ds(pl.multiple_of(b * TQ, TQ), TQ)
            sg, dsg = _silu_parts(g_ref[...])
            duv = du_ref[...]
            ov = o_ref[...]
            do = duv * sg
            dz_ref[3, rows, :] = (duv * ov * dsg).astype(BF16)
            do_o = do * ov
            do_bf = do.astype(BF16)
            for pp in range(A_PAIRS):
                ln = slice(pp * LANES, (pp + 1) * LANES)
                q = q_ref[:, ln] * SCALE
                kcat = _stack_pair(_window(k_ref, b, pad, A_WIN, ln))
                vcat = _stack_pair(_window(v_ref, b, pad, A_WIN, ln))
                sc = lax.dot_general(q, kcat, NT, preferred_element_type=F32) + bias_scr[pp]
                if first_blocks:
                    sc = jnp.where(_block_valid(b, left, A_WIN), sc, MASKED)
                lse_t = lse_ref[pp]
                dp = lax.dot_general(do_bf[:, ln], vcat, NT, preferred_element_type=F32)
                p, ds = _softmax_pair_bwd(sc, dp, (lse_t[:, 0:1], lse_t[:, HEAD_DIM:HEAD_DIM + 1]),
                                          _pair_rowsums(do_o[:, ln], lo), A_WIN)
                dbias_acc[pp] += ds
                dsb = ds.astype(BF16)
                dz_ref[0, rows, ln] = (jnp.dot(dsb, kcat, preferred_element_type=F32) * SCALE).astype(BF16)
                dkt = lax.dot_general(q, dsb, TN, preferred_element_type=F32)
                dvt = lax.dot_general(do_bf[:, ln], p.astype(BF16), TN, preferred_element_type=F32)
                dkt = jnp.where(upper, dkt[:, :A_WIN], dkt[:, A_WIN:])
                dvt = jnp.where(upper, dvt[:, :A_WIN], dvt[:, A_WIN:])
                for t in range(A_KBLOCKS):
                    blk = b + (pad // KB - left + t)
                    dk_acc[blk, ln, :] += dkt[:, t * KB:(t + 1) * KB]
                    dv_acc[blk, ln, :] += dvt[:, t * KB:(t + 1) * KB]

        @pl.when(b < left)
        def _():
            step(True)

        @pl.when(b >= left)
        def _():
            step(False)

        @pl.when(b == nb - 1)
        def _():
            for kb in range(s // KB):
                dz_ref[1, kb * KB:(kb + 1) * KB, :] = dk_acc[pad // KB + kb].T.astype(BF16)
                dz_ref[2, kb * KB:(kb + 1) * KB, :] = dv_acc[pad // KB + kb].T.astype(BF16)
            for hh in range(2 * A_PAIRS):
                ddiag_ref[hh] = _toeplitz_sum(
                    dbias_acc[hh // 2, :, (hh % 2) * A_WIN:(hh % 2 + 1) * A_WIN], A_WIN)

    tile = pl.BlockSpec((TQ, pw), lambda p, b: (b, p))
    diag_spec = pl.BlockSpec((2 * A_PAIRS, 1, wide), lambda p, b: (p, 0, 0))
    return _call(
        body, name="attn_a_bwd", grid=(HEADS // 2 // A_PAIRS, nb),
        in_specs=_a_qkv_specs(pad + s, pad, pw) + [
            tile, tile, tile, pl.BlockSpec((A_PAIRS, TQ, LANES), lambda p, b: (p, b, 0)), diag_spec],
        out_specs=[pl.BlockSpec((4, s, pw), lambda p, b: (0, 0, p)), diag_spec],
        out_shape=[jax.ShapeDtypeStruct((4, s, D_MODEL), BF16),
                   jax.ShapeDtypeStruct((HEADS, 1, wide), F32)],
        scratch_shapes=[pltpu.VMEM((A_PAIRS, TQ, 2 * A_WIN), F32), pltpu.VMEM((A_PAIRS, TQ, 2 * A_WIN), F32),
                        pltpu.VMEM(((pad + s) // KB, pw, KB), F32), pltpu.VMEM(((pad + s) // KB, pw, KB), F32)],
        sem=("parallel", "arbitrary"), hosted=hosted,
        args=(zqkv, zqkv, zqkv, g, o, du, lse, diag))


B_STACK = B_GROUP // 2
B_KVX = 4 * LANES
B_ROWS = B_STACK * TQ
B_WIDE = B_WIN + TQ


def _b_head_place(h):
    return h // B_GROUP, (h % B_GROUP) // 2, h % 2


def _toeplitz_tile_t(base_row, width, left_chunks):
    wide = width + TQ
    rolled = pltpu.roll(jnp.broadcast_to(base_row, (width, wide)), 0, 1, stride=1, stride_axis=0)
    j = lax.broadcasted_iota(jnp.int32, (width, TQ), 0) // CHUNK
    i = lax.broadcasted_iota(jnp.int32, (width, TQ), 1) // CHUNK
    dc = i + left_chunks - j
    return jnp.where((dc >= 0) & (dc <= left_chunks), rolled[:, :TQ], MASKED)


def _toeplitz_sum_t(tile_t, width):
    flip = (lax.broadcasted_iota(jnp.int32, (width, width), 0) + lax.broadcasted_iota(jnp.int32, (width, width), 1)
            == width - 1).astype(F32)
    reversed_rows = jnp.dot(flip, tile_t, precision=lax.Precision.HIGHEST, preferred_element_type=F32)
    padded = jnp.concatenate([reversed_rows, jnp.zeros((width, width), F32)], axis=1)
    rolled = pltpu.roll(padded, 0, 1, stride=1, stride_axis=0)
    return jnp.sum(rolled, axis=0, keepdims=True)


def _b_build_bias(base_ref, bias_scr):
    for h in range(HEADS):
        gi, pr, e = _b_head_place(h)
        bias_scr[gi, e * B_WIN:(e + 1) * B_WIN, pr * TQ:(pr + 1) * TQ] = _toeplitz_tile_t(
            base_ref[h], B_WIN, B_LEFT_CHUNKS)


def _b_stack(x, gi):
    return jnp.concatenate(
        [x[:, (B_STACK * gi + pr) * LANES:(B_STACK * gi + pr + 1) * LANES] for pr in range(B_STACK)], axis=0)


def _b_sink_rows(sink_ref, gi):
    block = lax.broadcasted_iota(jnp.int32, (1, B_ROWS), 1) // TQ
    rows = []
    for e in range(2):
        row = jnp.zeros((1, B_ROWS), F32)
        for pr in range(B_STACK):
            h = B_GROUP * gi + 2 * pr + e
            row = jnp.where(block == pr, sink_ref[0:1, h:h + 1], row)
        rows.append(row)
    return rows


def _b_scores_t(q_ref, kvv, bias_scr, gi, b, left, first_blocks):
    kcat = _stack_pair(kvv[:, gi * LANES:(gi + 1) * LANES])
    vcat = _stack_pair(kvv[:, (B_KV_HEADS + gi) * LANES:(B_KV_HEADS + gi + 1) * LANES])
    qs = _b_stack(q_ref, gi) * SCALE
    sc = lax.dot_general(kcat, qs, NT, preferred_element_type=F32) + bias_scr[gi]
    if first_blocks:
        row = lax.broadcasted_iota(jnp.int32, (2 * B_WIN, 1), 0)
        row = jnp.where(row >= B_WIN, row - B_WIN, row)
        sc = jnp.where((row // KB + (b - left)) >= 0, sc, MASKED)
    return kcat, vcat, qs, sc


def _attn_b_fwd(qb, kvx, gate, base, sinks):
    s = qb.shape[0]
    pad = kvx.shape[0] - s
    nb = s // TQ
    left = B_KBLOCKS - 1

    def body(q_ref, kv_ref, g_ref, base_ref, sink_ref, o_ref, u_ref, lse_ref, bias_scr):
        b = pl.program_id(0)

        @pl.when(b == 0)
        def _():
            _b_build_bias(base_ref, bias_scr)

        def step(first_blocks):
            kvv = _window(kv_ref, b, pad, B_WIN, slice(None))
            upper = lax.broadcasted_iota(jnp.int32, (LANES, B_ROWS), 0) < HEAD_DIM
            lse_rows = []
            for gi in range(B_KV_HEADS):
                kcat, vcat, qs, sc = _b_scores_t(q_ref, kvv, bias_scr, gi, b, left, first_blocks)
                sink = _b_sink_rows(sink_ref, gi)
                ps, inv = [], []
                for e in range(2):
                    sh = sc[e * B_WIN:(e + 1) * B_WIN]
                    m = jnp.maximum(jnp.max(sh, axis=0, keepdims=True), sink[e])
                    ex = jnp.exp(sh - m)
                    l = jnp.sum(ex, axis=0, keepdims=True) + jnp.exp(sink[e] - m)
                    ps.append(ex.astype(BF16))
                    inv.append(1.0 / l)
                    lse_rows.append(m + jnp.log(l))
                pt = jnp.concatenate(ps, axis=0)
                ot = lax.dot_general(vcat, pt, TN, preferred_element_type=F32) * jnp.where(upper, inv[0], inv[1])
                ov = ot.T
                for pr in range(B_STACK):
                    pair = B_STACK * gi + pr
                    o_ref[:, pair * LANES:(pair + 1) * LANES] = ov[pr * TQ:(pr + 1) * TQ]
            lse_ref[0] = jnp.concatenate(lse_rows + [jnp.zeros((8 - len(lse_rows), B_ROWS), F32)], axis=0)
            sg, _ = _silu_parts(g_ref[...])
            u_ref[...] = (o_ref[...] * sg).astype(BF16)

        @pl.when(b < left)
        def _():
            step(True)

        @pl.when(b >= left)
        def _():
            step(False)

    row = pl.BlockSpec((TQ, D_MODEL), lambda b: (b, 0))
    return pl.pallas_call(
        body, name="attn_b_fwd", grid=(nb,),
        in_specs=[row, pl.BlockSpec((pad + s, B_KVX), lambda b: (0, 0)), row,
                  pl.BlockSpec((HEADS, 1, B_WIDE), lambda b: (0, 0, 0)), pl.BlockSpec((1, HEADS), lambda b: (0, 0))],
        out_specs=[row, row, pl.BlockSpec((1, 8, B_ROWS), lambda b: (b, 0, 0))],
        out_shape=[jax.ShapeDtypeStruct((s, D_MODEL), F32), jax.ShapeDtypeStruct((s, D_MODEL), BF16),
                   jax.ShapeDtypeStruct((nb, 8, B_ROWS), F32)],
        scratch_shapes=[pltpu.VMEM((B_KV_HEADS, 2 * B_WIN, B_ROWS), F32)],
        compiler_params=_params(("arbitrary",)),
    )(qb, kvx, gate, base, sinks)


def _attn_b_bwd(qb, kvx, gate, o, du, lse, base, sinks):
    s = qb.shape[0]
    pad = kvx.shape[0] - s
    nb = s // TQ
    left = B_KBLOCKS - 1
    half = D_MODEL // 2

    def body(q_ref, kv_ref, g_ref, o_ref, du_ref, lse_ref, base_ref, sink_ref, dz_ref, dkv_ref, dsum_ref,
             dsink_ref, bias_scr, dbias_acc, dkv_acc, dsink_acc):
        b = pl.program_id(0)

        @pl.when(b == 0)
        def _():
            _b_build_bias(base_ref, bias_scr)
            dbias_acc[...] = jnp.zeros_like(dbias_acc)
            dkv_acc[...] = jnp.zeros_like(dkv_acc)
            dsink_acc[...] = jnp.zeros_like(dsink_acc)

        def step(first_blocks):
            kvv = _window(kv_ref, b, pad, B_WIN, slice(None))
            sg, dsg = _silu_parts(g_ref[...])
            duv = du_ref[...]
            ov = o_ref[...]
            do = duv * sg
            dgate = (duv * ov * dsg).astype(BF16)
            dz_ref[2] = dgate[:, :half]
            dz_ref[3] = dgate[:, half:]
            do_o = do * ov
            do_bf = do.astype(BF16)
            lse_all = lse_ref[0]
            dsink_rows = []
            for gi in range(B_KV_HEADS):
                kcat, vcat, qs, sc = _b_scores_t(q_ref, kvv, bias_scr, gi, b, left, first_blocks)
                dos = _b_stack(do_bf, gi)
                doo_t = _b_stack(do_o, gi).T
                delta = (jnp.sum(doo_t[:HEAD_DIM], axis=0, keepdims=True),
                         jnp.sum(doo_t[HEAD_DIM:], axis=0, keepdims=True))
                sink = _b_sink_rows(sink_ref, gi)
                dp = lax.dot_general(vcat, dos, NT, preferred_element_type=F32)
                ps, dss = [], []
                for e in range(2):
                    lse_e = lse_all[2 * gi + e:2 * gi + e + 1]
                    delta_e = delta[e]
                    p = jnp.exp(sc[e * B_WIN:(e + 1) * B_WIN] - lse_e)
                    ps.append(p.astype(BF16))
                    dss.append(p * (dp[e * B_WIN:(e + 1) * B_WIN] - delta_e))
                    dsink_rows.append(-jnp.exp(sink[e] - lse_e) * delta_e)
                ds = jnp.concatenate(dss, axis=0)
                dbias_acc[gi] += ds
                dsb = ds.astype(BF16)
                dq = (lax.dot_general(kcat, dsb, TN, preferred_element_type=F32) * SCALE).T.astype(BF16)
                for pr in range(B_STACK):
                    dz_ref[gi, :, pr * LANES:(pr + 1) * LANES] = dq[pr * TQ:(pr + 1) * TQ]
                dk = _unstack_pair(jnp.dot(dsb, qs, preferred_element_type=F32), B_WIN)
                dv = _unstack_pair(jnp.dot(jnp.concatenate(ps, axis=0), dos, preferred_element_type=F32), B_WIN)
                krows = pl.ds(pl.multiple_of(b * TQ + pad - (B_WIN - TQ), KB), B_WIN)
                dkv_acc[krows, gi * LANES:(gi + 1) * LANES] += dk
                dkv_acc[krows, (B_KV_HEADS + gi) * LANES:(B_KV_HEADS + gi + 1) * LANES] += dv
            dsink_acc[...] += jnp.concatenate(
                dsink_rows + [jnp.zeros((8 - len(dsink_rows), B_ROWS), F32)], axis=0)

        @pl.when(b < left)
        def _():
            step(True)

        @pl.when(b >= left)
        def _():
            step(False)

        @pl.when(b == nb - 1)
        def _():
            lo_s = _lane_lo(s)
            for which in range(2):
                folded = []
                for gi in range(B_KV_HEADS):
                    part = dkv_acc[pad:pad + s, (which * B_KV_HEADS + gi) * LANES:(which * B_KV_HEADS + gi + 1) * LANES]
                    folded.append(part + pltpu.roll(part, HEAD_DIM, 1))
                dkv_ref[:, which * LANES:(which + 1) * LANES] = jnp.where(lo_s, folded[0], folded[1]).astype(BF16)
            lane8 = lax.broadcasted_iota(jnp.int32, dsink_ref.shape, 1)
            tot = jnp.zeros(dsink_ref.shape, F32)
            for h in range(HEADS):
                gi, pr, e = _b_head_place(h)
                dsum_ref[h] = _toeplitz_sum_t(
                    dbias_acc[gi, e * B_WIN:(e + 1) * B_WIN, pr * TQ:(pr + 1) * TQ], B_WIN)
                per_query = dsink_acc[2 * gi + e:2 * gi + e + 1, pr * TQ:(pr + 1) * TQ]
                tot = jnp.where(lane8 == h, jnp.sum(per_query, axis=1, keepdims=True), tot)
            dsink_ref[...] = tot

    row = pl.BlockSpec((TQ, D_MODEL), lambda b: (b, 0))
    base_spec = pl.BlockSpec((HEADS, 1, B_WIDE), lambda b: (0, 0, 0))
    return pl.pallas_call(
        body, name="attn_b_bwd", grid=(nb,),
        in_specs=[row, pl.BlockSpec((pad + s, B_KVX), lambda b: (0, 0)), row, row, row,
                  pl.BlockSpec((1, 8, B_ROWS), lambda b: (b, 0, 0)), base_spec,
                  pl.BlockSpec((1, HEADS), lambda b: (0, 0))],
        out_specs=[pl.BlockSpec((4, TQ, half), lambda b: (0, b, 0)),
                   pl.BlockSpec((s, 2 * LANES), lambda b: (0, 0)), base_spec,
                   pl.BlockSpec((8, LANES), lambda b: (0, 0))],
        out_shape=[jax.ShapeDtypeStruct((4, s, half), BF16), jax.ShapeDtypeStruct((s, 2 * LANES), BF16),
                   jax.ShapeDtypeStruct((HEADS, 1, B_WIDE), F32), jax.ShapeDtypeStruct((8, LANES), F32)],
        scratch_shapes=[pltpu.VMEM((B_KV_HEADS, 2 * B_WIN, B_ROWS), F32),
                        pltpu.VMEM((B_KV_HEADS, 2 * B_WIN, B_ROWS), F32),
                        pltpu.VMEM((pad + s, B_KVX), F32), pltpu.VMEM((8, B_ROWS), F32)],
        compiler_params=_params(("arbitrary",)),
    )(qb, kvx, gate, o, du, lse, base, sinks)


def _t5_bucket(rel):
    nb = T5_BUCKETS // 2
    max_exact = nb // 2
    ret = jnp.where(rel > 0, nb, 0)
    n = jnp.abs(rel)
    nf = jnp.maximum(n, 1).astype(jnp.float32)
    large = max_exact + (jnp.log(nf / max_exact) / math.log(T5_MAX_DIST / max_exact)
                         * (nb - max_exact)).astype(jnp.int32)
    large = jnp.minimum(large, nb - 1)
    return ret + jnp.where(n < max_exact, n, large)


def _a_offset_onehot():
    c = np.arange(A_WIN + TQ)
    dist = A_LEFT_CHUNKS * CHUNK + TQ - 1 - c
    idx = np.clip(dist, -A_REL_CLIP, A_REL_CLIP) + A_REL_CLIP
    onehot = np.zeros((A_WIN + TQ, 2 * A_REL_CLIP + 1), np.float32)
    onehot[c, idx] = 1.0
    return jnp.asarray(onehot)


def _b_offset_onehot():
    c = jnp.arange(B_WIN + TQ, dtype=jnp.int32)
    rel = c - (TQ - 1) - B_LEFT_CHUNKS * CHUNK
    return (_t5_bucket(rel)[:, None] == jnp.arange(T5_BUCKETS)[None, :]).astype(F32)


def _diag_rows(onehot, table):
    rows = jnp.dot(onehot, table.astype(F32), precision=lax.Precision.HIGHEST)
    return rows.T.reshape(HEADS, 1, onehot.shape[0])


def _diag_rows_grad(onehot, ddiag):
    return jnp.dot(ddiag.reshape(HEADS, onehot.shape[0]), onehot, precision=lax.Precision.HIGHEST).T


def _position():
    x, y, c = lax.axis_index("x"), lax.axis_index("y"), lax.axis_index("c")
    chips = [(1 - x, y), (x, 1 - y), (1 - x, 1 - y)]
    return x, y, c, chips


ANY = pl.BlockSpec(memory_space=pl.ANY)


def _allgather_hosted(shards, split):
    n = len(shards)

    def part(ref, t, half):
        if not split[t]:
            return ref
        rows = shards[t].shape[0] // 2
        return ref.at[pl.ds(half * rows, rows)]

    def copies(kind, ins, outs, sems):
        send_sems, recv_sems, pass_send, pass_recv, local_sems = sems
        x, y, c, chips = _position()
        mine = 2 * x + y
        if kind == "local":
            return [pltpu.make_async_copy(ins[t], outs[t].at[mine], local_sems.at[t]) for t in range(n)]
        made = []
        for t in range(n):
            for j, chip in enumerate(chips):
                theirs = 2 * chip[0] + chip[1]
                far = dict(send_sem=send_sems.at[3 * t + j], recv_sem=recv_sems.at[3 * t + j],
                           device_id=(chip[0], chip[1], c), device_id_type=MESH)
                near = dict(send_sem=pass_send.at[3 * t + j], recv_sem=pass_recv.at[3 * t + j],
                            device_id=(x, y, 1 - c), device_id_type=MESH)
                here = part(outs[t].at[theirs], t, c)
                if kind == "send":
                    made.append(pltpu.make_async_remote_copy(
                        src_ref=part(ins[t], t, c), dst_ref=part(outs[t].at[mine], t, c), **far))
                elif kind == "landed":
                    made.append(pltpu.make_async_remote_copy(src_ref=here, dst_ref=here, **far))
                elif not split[t]:
                    made.append(None)
                elif kind == "pass":
                    made.append(pltpu.make_async_remote_copy(src_ref=here, dst_ref=here, **near))
                else:
                    other = part(outs[t].at[theirs], t, 1 - c)
                    made.append(pltpu.make_async_remote_copy(src_ref=other, dst_ref=other, **near))
        return made

    def first(ins, outs, sems):
        for cp in copies("local", ins, outs, sems) + copies("send", ins, outs, sems):
            cp.start()

    def middle(ins, outs, sems):
        for got, cp in zip(copies("landed", ins, outs, sems), copies("pass", ins, outs, sems)):
            got.wait_recv()
            if cp is not None:
                cp.start()

    def last(ins, outs, sems):
        for cp in copies("passed", ins, outs, sems):
            if cp is not None:
                cp.wait_recv()
        for cp in copies("send", ins, outs, sems) + copies("pass", ins, outs, sems):
            if cp is not None:
                cp.wait_send()
        for cp in copies("local", ins, outs, sems):
            cp.wait()

    return _Hosted(shards, [jax.ShapeDtypeStruct((4,) + w.shape, w.dtype) for w in shards],
                   [pltpu.SemaphoreType.DMA((3 * n,))] * 4 + [pltpu.SemaphoreType.DMA((n,))],
                   first, middle, last)


def _allgather_routed(shards):
    n = len(shards)

    def piece(block_ref, t, c, quarter=None):
        half = shards[t].shape[0] // 2
        if quarter is None:
            return block_ref.at[pl.ds(c * half, half)]
        return block_ref.at[pl.ds(c * half + quarter * (half // 2), half // 2)]

    def copies(kind, ins, outs, sems):
        ici_send, ici_recv, pass_send, pass_recv, local_sems = sems
        x, y, c, chips = _position()
        mine = 2 * x + y
        if kind == "local":
            return [pltpu.make_async_copy(ins[t], outs[t].at[mine], local_sems.at[t]) for t in range(n)]
        ids = [2 * chip[0] + chip[1] for chip in chips]
        made = []
        for t in range(n):
            def ici(k, to):
                return dict(send_sem=ici_send.at[4 * t + k], recv_sem=ici_recv.at[4 * t + k],
                            device_id=(chips[to][0], chips[to][1], c), device_id_type=MESH)

            def d2d(k):
                return dict(send_sem=pass_send.at[4 * t + k], recv_sem=pass_recv.at[4 * t + k],
                            device_id=(x, y, 1 - c), device_id_type=MESH)

            def same(ref, where):
                return pltpu.make_async_remote_copy(src_ref=ref, dst_ref=ref, **where)

            if kind == "send":
                for k in range(2):
                    made.append(pltpu.make_async_remote_copy(
                        src_ref=piece(ins[t], t, c), dst_ref=piece(outs[t].at[mine], t, c), **ici(k, k)))
            elif kind == "landed":
                made += [same(piece(outs[t].at[ids[k]], t, c), ici(k, k)) for k in range(2)]
            elif kind == "forward":
                made.append(same(piece(outs[t].at[ids[0]], t, c, 0), ici(2, 1)))
                made.append(same(piece(outs[t].at[ids[1]], t, c, 1), ici(3, 0)))
            elif kind == "arrived":
                made.append(same(piece(outs[t].at[ids[2]], t, c, 0), ici(2, 1)))
                made.append(same(piece(outs[t].at[ids[2]], t, c, 1), ici(3, 0)))
            else:
                core = 1 - c if kind == "passed" else c
                if kind in ("pass halves", "passed"):
                    made += [same(piece(outs[t].at[ids[k]], t, core), d2d(k)) for k in range(2)]
                if kind in ("pass quarters", "passed"):
                    made += [same(piece(outs[t].at[ids[2]], t, core, k), d2d(2 + k)) for k in range(2)]
        return made

    def first(ins, outs, sems):
        for cp in copies("local", ins, outs, sems) + copies("send", ins, outs, sems):
            cp.start()

    def middle(ins, outs, sems):
        for got, onward, near in zip(copies("landed", ins, outs, sems), copies("forward", ins, outs, sems),
                                     copies("pass halves", ins, outs, sems)):
            got.wait_recv()
            near.start()
            onward.start()

    def last(ins, outs, sems):
        quarters = copies("pass quarters", ins, outs, sems)
        for got, near in zip(copies("arrived", ins, outs, sems), quarters):
            got.wait_recv()
            near.start()
        for cp in copies("passed", ins, outs, sems):
            cp.wait_recv()
        for cp in (copies("send", ins, outs, sems) + copies("forward", ins, outs, sems)
                   + copies("pass halves", ins, outs, sems) + quarters):
            cp.wait_send()
        for cp in copies("local", ins, outs, sems):
            cp.wait()

    return _Hosted(shards, [jax.ShapeDtypeStruct((4,) + w.shape, w.dtype) for w in shards],
                   [pltpu.SemaphoreType.DMA((4 * n,))] * 4 + [pltpu.SemaphoreType.DMA((n,))],
                   first, middle, last)


def _scatter_hosted(grads):
    n = len(grads)

    def copies(ins, outs, sems):
        send_sems, recv_sems = sems
        x, y, c, chips = _position()
        return [pltpu.make_async_remote_copy(
            src_ref=ins[t].at[2 * chip[0] + chip[1]], dst_ref=outs[t].at[j],
            send_sem=send_sems.at[3 * t + j], recv_sem=recv_sems.at[3 * t + j],
            device_id=(chip[0], chip[1], c), device_id_type=MESH)
            for t in range(n) for j, chip in enumerate(chips)]

    def first(ins, outs, sems):
        for cp in copies(ins, outs, sems):
            cp.start()

    def last(ins, outs, sems):
        for cp in copies(ins, outs, sems):
            cp.wait()

    return _Hosted(grads, [jax.ShapeDtypeStruct((3,) + g.shape[1:], g.dtype) for g in grads],
                   [pltpu.SemaphoreType.DMA((3 * n,))] * 2, first, None, last)


GATHER_PEERS = "x and y neighbours (same core) and the sibling core"
SCATTER_PEERS = "the same core of the three other chips"


def _run_on_sequencer(name, hosted, peers, collective_id):
    ins = [jax.new_ref(a, memory_space=pltpu.MemorySpace.HBM) for a in hosted.inputs]
    outs = [jax.empty_ref(shape, memory_space=pltpu.MemorySpace.HBM) for shape in hosted.out_shapes]

    @pl.kernel(mesh=plsc.ScalarSubcoreMesh(axis_name="sequencer", num_cores=1), name=name,
               scratch_types=tuple(hosted.sems), compiler_params=pltpu.CompilerParams(collective_id=collective_id))
    def launch(*sems):
        x, y, c, chips = _position()
        if peers == GATHER_PEERS:
            devices = [(chip[0], chip[1], c) for chip in chips[:2]] + [(x, y, 1 - c)]
        else:
            devices = [(chip[0], chip[1], c) for chip in chips]
        barrier = pltpu.get_barrier_semaphore()
        for device in devices:
            pl.semaphore_signal(barrier, inc=1, device_id=device, device_id_type=MESH)
        pl.semaphore_wait(barrier, len(devices))
        hosted.first(ins, outs, sems)
        if hosted.middle is not None:
            hosted.middle(ins, outs, sems)
        hosted.last(ins, outs, sems)

    launch()
    return [o[...] for o in outs]


def _run_alone(name, hosted):
    n_in = len(hosted.inputs)
    n_out = len(hosted.out_shapes)

    def body(*refs):
        ins, outs, sems = refs[:n_in], refs[n_in:n_in + n_out], refs[n_in + n_out:]
        hosted.first(ins, outs, sems)
        if hosted.middle is not None:
            hosted.middle(ins, outs, sems)
        hosted.last(ins, outs, sems)

    return pl.pallas_call(
        body, name=name, in_specs=[ANY] * n_in, out_specs=[ANY] * n_out, out_shape=hosted.out_shapes,
        scratch_shapes=hosted.sems)(*hosted.inputs)


def _swap_with_sibling(blocks):
    n = len(blocks)

    def body(*refs):
        ins, outs = refs[:n], refs[n:2 * n]
        send_sems, recv_sems = refs[2 * n:]
        x, y, c, _ = _position()
        sends = [pltpu.make_async_remote_copy(
            src_ref=ins[t], dst_ref=outs[t], send_sem=send_sems.at[t], recv_sem=recv_sems.at[t],
            device_id=(x, y, 1 - c), device_id_type=MESH) for t in range(n)]
        for cp in sends:
            cp.start()
        for cp in sends:
            cp.wait()

    return pl.pallas_call(
        body, name="swap_with_sibling",
        in_specs=[ANY] * n, out_specs=[ANY] * n,
        out_shape=[jax.ShapeDtypeStruct(b.shape, b.dtype) for b in blocks],
        scratch_shapes=[pltpu.SemaphoreType.DMA((n,))] * 2,
    )(*blocks)


def _small_step(partials, extras, ws, ms, vs, shard_of):
    n = len(partials)
    terms = list(partials) + list(extras)
    nt = len(terms)
    rows = [t for t in range(nt) if terms[t].shape[0] == 1]
    mats = [t for t in range(nt) if terms[t].shape[0] != 1]
    row_block = (8, max(terms[t].shape[1] for t in rows))
    assert len(rows) <= row_block[0]
    sent = [row_block] + [terms[t].shape for t in mats]

    def body(*refs):
        ins, refs = refs[:nt], refs[nt:]
        w_refs, refs = refs[:n], refs[n:]
        m_refs, refs = refs[:n], refs[n:]
        v_refs, refs = refs[:n], refs[n:]
        outs, refs = refs[:4 * n + nt - n], refs[4 * n + nt - n:]
        slots, (packed, send_sems, recv_sems) = refs[:len(sent)], refs[len(sent):]
        x, y, c, _ = _position()
        me = 4 * x + 2 * y + c
        packed[...] = jnp.zeros_like(packed)
        for i, t in enumerate(rows):
            packed[i:i + 1, 0:terms[t].shape[1]] = ins[t][...]
        sources = [packed] + [ins[t] for t in mats]
        sends = []
        for j, src in enumerate(sources):
            slots[j][me] = src[...]
            for k in range(1, 8):
                peer = (x ^ (k >> 2), y ^ ((k >> 1) & 1), c ^ (k & 1))
                sends.append(pltpu.make_async_remote_copy(
                    src_ref=src, dst_ref=slots[j].at[me], send_sem=send_sems.at[7 * j + k - 1],
                    recv_sem=recv_sems.at[7 * j + k - 1], device_id=peer, device_id_type=MESH))
        for cp in sends:
            cp.start()
        for j, src in enumerate(sources):
            for k in range(1, 8):
                pltpu.make_async_remote_copy(
                    src_ref=src, dst_ref=slots[j].at[me ^ k], send_sem=send_sems.at[7 * j + k - 1],
                    recv_sem=recv_sems.at[7 * j + k - 1], device_id=(x, y, c), device_id_type=MESH).wait_recv()
        for cp in sends:
            cp.wait_send()
        sums = []
        for j in range(len(sources)):
            g = slots[j][0]
            for dev in range(1, 8):
                g = g + slots[j][dev]
            sums.append(g)
        chip = 2 * x + y
        for t in range(nt):
            if t in rows:
                i = rows.index(t)
                g = sums[0][i:i + 1, 0:terms[t].shape[1]]
            else:
                g = sums[1 + mats.index(t)]
            if t >= n:
                outs[4 * n + t - n][...] = g
                continue
            if shard_of[t]:
                width = ws[t].shape[-1]
                mine = jnp.zeros(ws[t].shape, F32)
                for s in range(4):
                    mine = jnp.where(chip == s, g[:, s * width:(s + 1) * width], mine)
                g = mine
            delta, mn, vn = _adamw_math(w_refs[t][...], g, m_refs[t][...], v_refs[t][...])
            outs[4 * t][...] = g
            outs[4 * t + 1][...] = delta
            outs[4 * t + 2][...] = mn
            outs[4 * t + 3][...] = vn

    vmem = pl.BlockSpec(memory_space=pltpu.VMEM)
    out_shapes = []
    for t in range(n):
        out_shapes += [jax.ShapeDtypeStruct(ws[t].shape, F32)] * 4
    out_shapes += [jax.ShapeDtypeStruct(a.shape, F32) for a in extras]
    out_shapes += [jax.ShapeDtypeStruct((8,) + tuple(shape), F32) for shape in sent]
    res = pl.pallas_call(
        body, name="small_step",
        in_specs=[vmem] * (nt + 3 * n), out_specs=[vmem] * len(out_shapes), out_shape=out_shapes,
        scratch_shapes=[pltpu.VMEM(row_block, F32)] + [pltpu.SemaphoreType.DMA((7 * len(sent),))] * 2,
    )(*terms, *ws, *ms, *vs)
    return [res[4 * t:4 * t + 4] for t in range(n)], res[4 * n:4 * n + nt - n]


def _adamw_math(w, g, m, v):
    m = ADAM_B1 * m + (1.0 - ADAM_B1) * g
    v = ADAM_B2 * v + (1.0 - ADAM_B2) * (g * g)
    m_hat = m / (1.0 - ADAM_B1 ** ADAM_STEP)
    v_hat = v / (1.0 - ADAM_B2 ** ADAM_STEP)
    delta = -ADAM_LR * (m_hat / (jnp.sqrt(v_hat) + ADAM_EPS) + ADAM_WD * w)
    return delta, m, v


def _row_tile(rows):
    return 256 if rows % 256 == 0 else rows


def _sum_partials(name, own, recv, chip):
    rows, cols = own.shape[1:]
    tr = _row_tile(rows)

    def body(chip_ref, own_ref, recv_ref, o_ref):
        acc = own_ref[...]
        for j in range(3):
            acc = acc + recv_ref[j].astype(F32)
        o_ref[...] = acc

    return pl.pallas_call(
        body, name=name,
        grid_spec=pltpu.PrefetchScalarGridSpec(
            num_scalar_prefetch=1, grid=(rows // tr,),
            in_specs=[pl.BlockSpec((None, tr, cols), lambda i, chip_ref: (chip_ref[0], i, 0)),
                      pl.BlockSpec((3, tr, cols), lambda i, chip_ref: (0, i, 0))],
            out_specs=pl.BlockSpec((tr, cols), lambda i, chip_ref: (i, 0))),
        out_shape=jax.ShapeDtypeStruct((rows, cols), F32),
        compiler_params=_params(("parallel",)),
    )(chip.reshape(1).astype(jnp.int32), own, recv)


def _adamw(name, w, m, v, g_parts):
    rows, cols = w.shape
    tr = _row_tile(rows)
    n = len(g_parts)

    def body(w_ref, m_ref, v_ref, *refs):
        g_refs = refs[:n]
        go_ref, d_ref, mo_ref, vo_ref = refs[n:]
        g = g_refs[0][...]
        for r in g_refs[1:]:
            g = g + r[...]
        delta, mn, vn = _adamw_math(w_ref[...], g, m_ref[...], v_ref[...])
        go_ref[...] = g
        d_ref[...] = delta
        mo_ref[...] = mn
        vo_ref[...] = vn

    spec = pl.BlockSpec((tr, cols), lambda i: (i, 0))
    return pl.pallas_call(
        body, name=name, grid=(rows // tr,),
        in_specs=[spec] * (3 + n), out_specs=[spec] * 4,
        out_shape=[jax.ShapeDtypeStruct((rows, cols), F32)] * 4,
        compiler_params=_params(("parallel",)),
    )(w, m, v, *g_parts)


def _local_step(x, target, ga, wa_in, rel_bias, later_shards, gk, t5, gb, sinks, gf):
    s, d = x.shape
    tm = min(TM_DENSE, s)
    nt = s // tm
    half = d // 2
    row = pl.BlockSpec((tm, d), lambda i: (i, 0))
    whole = lambda shape: pl.BlockSpec(shape, lambda *_: (0,) * len(shape))

    n1, = _norm_fwd("norm_a", x, ga)
    zqkv = _matmul("proj_a_qkv", n1, wa_in, dims=NN, grid=(3, nt + 1), zero_axis=1,
                   a_spec=pl.BlockSpec((tm, d), lambda j, i: (jnp.maximum(i - 1, 0), 0)),
                   b_spec=pl.BlockSpec((None, d, d), lambda j, i: (j, 0, 0)),
                   o_spec=pl.BlockSpec((None, tm, d), lambda j, i: (j, i, 0)),
                   out_shape=(3, tm + s, d), out_dtype=BF16)
    gate_a = _matmul("proj_a_gate", n1, wa_in, dims=NN, grid=(nt,),
                     a_spec=row, b_spec=pl.BlockSpec((None, d, d), lambda i: (3, 0, 0)), o_spec=row,
                     out_shape=(s, d), out_dtype=F32)
    onehot_a = _a_offset_onehot()
    diag_a = _diag_rows(onehot_a, rel_bias)
    (o_a, u_a, lse_a), gathered = _attn_a_fwd(zqkv, gate_a, diag_a, hosted=_allgather_routed(later_shards))
    wa_out, wkv, wb_in, wb_out = gathered
    wa_out = wa_out.reshape(d, d)
    wkv = wkv.reshape(d, -1)
    wb_out = wb_out.reshape(d, d)
    h1, nk, n2 = _out_norms("out_a_norms", u_a, wa_out, x, jnp.concatenate([gk, gb], axis=0))
    kvw = wkv.shape[1]
    wkv_x = jnp.concatenate([wkv[:, (i // 2) * HEAD_DIM:(i // 2 + 1) * HEAD_DIM] for i in range(8)], axis=1)
    kvx = _matmul("proj_kv", nk, wkv_x, dims=NN, grid=(nt + 1,), zero_axis=0,
                  a_spec=pl.BlockSpec((tm, d), lambda i: (jnp.maximum(i - 1, 0), 0)), b_spec=whole((d, B_KVX)),
                  o_spec=pl.BlockSpec((tm, B_KVX), lambda i: (i, 0)), out_shape=(tm + s, B_KVX), out_dtype=BF16)
    qb = _matmul("proj_b_q", n2, wb_in, dims=NN, grid=(2, nt),
                 a_spec=pl.BlockSpec((tm, d), lambda j, i: (i, 0)),
                 b_spec=pl.BlockSpec((None, d, half), lambda j, i: (j, 0, 0)),
                 o_spec=pl.BlockSpec((tm, half), lambda j, i: (i, j)), out_shape=(s, d), out_dtype=BF16)
    gate_b = _matmul("proj_b_gate", n2, wb_in, dims=NN, grid=(2, nt),
                     a_spec=pl.BlockSpec((tm, d), lambda j, i: (i, 0)),
                     b_spec=pl.BlockSpec((None, d, half), lambda j, i: (2 + j, 0, 0)),
                     o_spec=pl.BlockSpec((tm, half), lambda j, i: (i, j)), out_shape=(s, d), out_dtype=F32)
    onehot_b = _b_offset_onehot()
    base_b = jnp.roll(_diag_rows(onehot_b, t5)[..., ::-1], TQ, axis=-1)
    o_b, u_b, lse_b = _attn_b_fwd(qb, kvx, gate_b, base_b, sinks)
    dh2, loss, d_gf = _out_loss_head(u_b, wb_out, h1, target, gf)

    du_b = _matmul("dout_b", dh2, wb_out, dims=NT, grid=(nt,), a_spec=row, b_spec=whole((d, d)), o_spec=row,
                   out_shape=(s, d), out_dtype=F32)
    d_wb_out = _matmul("dw_out_b", u_b, dh2, dims=TN, grid=(2,),
                       a_spec=whole((s, d)), b_spec=pl.BlockSpec((s, half), lambda j: (0, j)),
                       o_spec=pl.BlockSpec((d, half), lambda j: (0, j)),
                       out_shape=(d, d), out_dtype=F32, also_bf16=True)
    dz_b, dkv, dsum_b, dsinks = _attn_b_bwd(qb, kvx, gate_b, o_b, du_b, lse_b, base_b, sinks)
    ddiag_b = jnp.roll(dsum_b[..., ::-1], -1, axis=-1)
    d_wb_in = _matmul("dw_in_b", n2, dz_b, dims=TN, grid=(4,),
                      a_spec=whole((s, d)), b_spec=pl.BlockSpec((None, s, half), lambda j: (j, 0, 0)),
                      o_spec=pl.BlockSpec((None, d, half), lambda j: (j, 0, 0)),
                      out_shape=(4, d, half), out_dtype=F32, also_bf16=True)
    d_wkv = _matmul("dw_kv", nk, dkv, dims=TN, grid=(1,),
                    a_spec=whole((s, d)), b_spec=whole((s, kvw)), o_spec=whole((d, kvw)),
                    out_shape=(d, kvw), out_dtype=F32, also_bf16=True)
    dh1, d_gkb = _proj_norm_bwd("dproj_kv_b", h1, dh2, jnp.concatenate([gk, gb], axis=0),
                                [(dkv[None], wkv[None]), (dz_b, wb_in)])

    du_a = _matmul("dout_a", dh1, wa_out, dims=NT, grid=(nt,), a_spec=row, b_spec=whole((d, d)), o_spec=row,
                   out_shape=(s, d), out_dtype=F32)
    d_wa_out = _matmul("dw_out_a", u_a, dh1, dims=TN, grid=(2,),
                       a_spec=whole((s, d)), b_spec=pl.BlockSpec((s, half), lambda j: (0, j)),
                       o_spec=pl.BlockSpec((d, half), lambda j: (0, j)),
                       out_shape=(d, d), out_dtype=F32, also_bf16=True)
    early = dict(a_w_out=[g.reshape(4, d // 4, d) for g in d_wa_out],
                 kv_w=[g.reshape(4, d // 4, kvw) for g in d_wkv], b_w_in=list(d_wb_in),
                 b_w_out=[g.reshape(4, d // 4, d) for g in d_wb_out])
    (dz_a, ddiag_a), early_recv = _attn_a_bwd(
        zqkv, gate_a, o_a, du_a, lse_a, diag_a, hosted=_scatter_hosted([early[n][1] for n in early]))
    d_wa_in = _matmul("dw_in_a", n1, dz_a, dims=TN, grid=(4, 2),
                      a_spec=whole((s, d)), b_spec=pl.BlockSpec((None, s, half), lambda j, h: (j, 0, h)),
                      o_spec=pl.BlockSpec((None, d, half), lambda j, h: (j, 0, h)),
                      out_shape=(4, d, d), out_dtype=F32, also_bf16=True)
    late_recv = _run_on_sequencer("scatter_a_w_in", _scatter_hosted([d_wa_in[1]]), SCATTER_PEERS, 0)
    grad_x, d_ga = _proj_norm_bwd("dproj_a", x, dh1, ga, [(dz_a, wa_in)])

    small = dict(a_norm=d_ga, kv_norm=d_gkb[0:1], b_norm=d_gkb[1:2], b_sinks=dsinks[0:1, :HEADS], final_norm=d_gf)
    small["by_offset"] = dict(a_rel_bias=(onehot_a, ddiag_a.reshape(HEADS, -1)),
                              t5_bias=(onehot_b, ddiag_b.reshape(HEADS, -1)))
    own = dict(a_w_in=d_wa_in[0], **{n: early[n][0] for n in early})
    received = dict(a_w_in=late_recv[0], **dict(zip(early, early_recv)))
    return loss, grad_x, small, own, received


SMALL = ("a_norm", "kv_norm", "b_norm", "b_sinks", "final_norm")
TABLES = ("a_rel_bias", "t5_bias")
BIG = ("a_w_in", "a_w_out", "kv_w", "b_w_in", "b_w_out")
ORDER = ("a_norm", "a_w_in", "a_rel_bias", "a_w_out", "kv_norm", "kv_w", "t5_bias", "b_norm", "b_w_in",
         "b_sinks", "b_w_out", "final_norm")


def kernel(x, a_norm, a_w_in, a_rel_bias, a_w_out, kv_norm, kv_w, t5_bias, b_norm, b_w_in, b_sinks, b_w_out, final_norm, loss_target, m_a_norm, m_a_w_in, m_a_rel_bias, m_a_w_out, m_kv_norm, m_kv_w, m_t5_bias, m_b_norm, m_b_w_in, m_b_sinks, m_b_w_out, m_final_norm, v_a_norm, v_a_w_in, v_a_rel_bias, v_a_w_out, v_kv_norm, v_kv_w, v_t5_bias, v_b_norm, v_b_w_in, v_b_sinks, v_b_w_out, v_final_norm):
    w = dict(a_norm=a_norm, a_w_in=a_w_in, a_rel_bias=a_rel_bias, a_w_out=a_w_out, kv_norm=kv_norm, kv_w=kv_w,
             t5_bias=t5_bias, b_norm=b_norm, b_w_in=b_w_in, b_sinks=b_sinks, b_w_out=b_w_out,
             final_norm=final_norm)
    m = dict(a_norm=m_a_norm, a_w_in=m_a_w_in, a_rel_bias=m_a_rel_bias, a_w_out=m_a_w_out, kv_norm=m_kv_norm,
             kv_w=m_kv_w, t5_bias=m_t5_bias, b_norm=m_b_norm, b_w_in=m_b_w_in, b_sinks=m_b_sinks,
             b_w_out=m_b_w_out, final_norm=m_final_norm)
    v = dict(a_norm=v_a_norm, a_w_in=v_a_w_in, a_rel_bias=v_a_rel_bias, a_w_out=v_a_w_out, kv_norm=v_kv_norm,
             kv_w=v_kv_w, t5_bias=v_t5_bias, b_norm=v_b_norm, b_w_in=v_b_w_in, b_sinks=v_b_sinks,
             b_w_out=v_b_w_out, final_norm=v_final_norm)
    d = D_MODEL
    chip = 2 * lax.axis_index("x") + lax.axis_index("y")

    shard2d = dict(a_w_in=a_w_in[0], a_w_out=a_w_out[0], kv_w=kv_w, b_w_in=b_w_in[0], b_w_out=b_w_out[0])

    wa_in, = _run_on_sequencer("allgather_first", _allgather_routed([shard2d["a_w_in"].astype(BF16)]),
                               GATHER_PEERS, 1)
    ga, = _run_alone("allgather_norm", _allgather_hosted([a_norm], [False]))
    ga = ga.reshape(1, d)

    loss, grad_x, small, own, received = _local_step(
        x[0], loss_target[0], ga, wa_in, a_rel_bias[0], [shard2d[n].astype(BF16) for n in BIG[1:]],
        kv_norm.reshape(1, d), t5_bias, b_norm, b_sinks, final_norm.reshape(1, d))

    out = {}
    as2d = lambda a: a.reshape(-1, a.shape[-1])
    small_res, (loss_sum, *offset_sums) = _small_step(
        [small[n] for n in SMALL], [loss] + [small["by_offset"][n][1] for n in TABLES],
        [as2d(w[n]) for n in SMALL], [as2d(m[n]) for n in SMALL], [as2d(v[n]) for n in SMALL],
        [n == "a_norm" for n in SMALL])
    for n, res in zip(SMALL, small_res):
        out[n] = [r.reshape(w[n].shape) for r in res]
    loss_out = loss_sum.reshape(())
    for n, summed in zip(TABLES, offset_sums):
        grad = _diag_rows_grad(small["by_offset"][n][0], summed)
        res = _adamw("adamw_" + n, as2d(w[n]), as2d(m[n]), as2d(v[n]), [grad])
        out[n] = [r.reshape(w[n].shape) for r in res]

    core_sums = [_sum_partials("sum_" + n, own[n], received[n], chip) for n in BIG]
    sibling_sums = _swap_with_sibling(core_sums)

    for n, mine, theirs in zip(BIG, core_sums, sibling_sums):
        res = _adamw("adamw_" + n, shard2d[n], m[n].reshape(shard2d[n].shape), v[n].reshape(shard2d[n].shape),
                     [mine, theirs])
        out[n] = [r.reshape(w[n].shape) for r in res]

    grads = [out[n][0] for n in ORDER]
    deltas = [out[n][1] for n in ORDER]
    new_m = [out[n][2] for n in ORDER]
    new_v = [out[n][3] for n in ORDER]
    return (loss_out, grad_x[None], *grads, *deltas, *new_m, *new_v)
```

```python
import functools
import math

import jax
import jax.numpy as jnp
import numpy as np
from jax import lax
from jax.experimental import pallas as pl
from jax.experimental.pallas import tpu as pltpu
from jax.experimental.pallas import tpu_sc as plsc

F32 = jnp.float32
BF16 = jnp.bfloat16
MESH = pl.DeviceIdType.MESH

D_MODEL = 1024
HEADS = 16
HEAD_DIM = 64
CHUNK = 64
RMS_EPS = 1e-6
SCALE = HEAD_DIM ** -0.5
A_LEFT_CHUNKS = 8
A_REL_CLIP = 256
B_LEFT_CHUNKS = 2
B_KV_HEADS = 2
B_GROUP = HEADS // B_KV_HEADS
T5_BUCKETS = 32
T5_MAX_DIST = 128
ADAM_LR = 0.001
ADAM_B1 = 0.9
ADAM_B2 = 0.999
ADAM_EPS = 1e-08
ADAM_WD = 0.01
ADAM_STEP = 10

MASKED = -1e30
LANES = 128
TQ = 128
A_PAIRS = 2
A_PAIRS_FWD = 4
KB = 128
A_KBLOCKS = A_LEFT_CHUNKS * CHUNK // KB + 1
B_KBLOCKS = B_LEFT_CHUNKS * CHUNK // KB + 1
A_WIN = A_KBLOCKS * KB
B_WIN = B_KBLOCKS * KB
TM = 512
TM_DENSE = 1024
TM_PARTS = 512
VMEM_LIMIT = 56 * 1024 * 1024

NT = (((1,), (1,)), ((), ()))
TN = (((0,), (0,)), ((), ()))
NN = (((1,), (0,)), ((), ()))


def _params(sem=None):
    return pltpu.CompilerParams(dimension_semantics=sem, vmem_limit_bytes=VMEM_LIMIT)


class _Hosted:
    def __init__(self, inputs, out_shapes, sems, first, middle, last):
        self.inputs, self.out_shapes, self.sems = list(inputs), list(out_shapes), list(sems)
        self.first, self.middle, self.last = first, middle, last


def _call(body, *, name, grid, in_specs, out_specs, out_shape, args, scratch_shapes=(), sem=None, hosted=None):
    in_specs, out_specs, out_shape = list(in_specs), list(out_specs), list(out_shape)
    scratch_shapes = list(scratch_shapes)
    if hosted is None:
        out = pl.pallas_call(
            body, name=name, grid=grid, in_specs=in_specs, out_specs=out_specs, out_shape=out_shape,
            scratch_shapes=scratch_shapes, compiler_params=_params(sem))(*args)
        return list(out), []
    n_in, n_out, n_scr = len(in_specs), len(out_shape), len(scratch_shapes)
    h_in, h_out = len(hosted.inputs), len(hosted.out_shapes)
    total = int(np.prod(grid)) if grid else 1

    def wrapped(*refs):
        ins, refs = refs[:n_in], refs[n_in:]
        h_ins, refs = refs[:h_in], refs[h_in:]
        outs, refs = refs[:n_out], refs[n_out:]
        h_outs, refs = refs[:h_out], refs[h_out:]
        scr, h_sems = refs[:n_scr], refs[n_scr:]
        step = 0
        for axis, size in enumerate(grid):
            step = step * size + pl.program_id(axis)

        @pl.when(step == 0)
        def _():
            hosted.first(h_ins, h_outs, h_sems)

        body(*ins, *outs, *scr)
        if hosted.middle is not None:
            @pl.when(step == total // 2)
            def _():
                hosted.middle(h_ins, h_outs, h_sems)

        @pl.when(step == total - 1)
        def _():
            hosted.last(h_ins, h_outs, h_sems)

    out = pl.pallas_call(
        wrapped, name=name, grid=grid, in_specs=in_specs + [ANY] * h_in, out_specs=out_specs + [ANY] * h_out,
        out_shape=out_shape + hosted.out_shapes, scratch_shapes=scratch_shapes + hosted.sems,
        compiler_params=_params(("arbitrary",) * len(grid)))(*args, *hosted.inputs)
    return list(out[:n_out]), list(out[n_out:])


def _matmul(name, a, b, *, dims, grid, a_spec, b_spec, o_spec, out_shape, out_dtype,
            parts=1, resid=None, resid_spec=None, also_bf16=False, hosted=None, zero_axis=None):
    def body(*refs):
        if zero_axis is None:
            product(*refs)
        else:
            @pl.when(pl.program_id(zero_axis) == 0)
            def _():
                refs[2][...] = jnp.zeros_like(refs[2])

            @pl.when(pl.program_id(zero_axis) > 0)
            def _():
                product(*refs)

    def product(*refs):
        a_ref, b_ref = refs[:2]
        r_ref = refs[2] if resid is not None else None
        o_ref = refs[3] if resid is not None else refs[2]
        if parts == 1:
            prod = lax.dot_general(a_ref[...].astype(BF16), b_ref[...].astype(BF16), dims,
                                   preferred_element_type=F32)
        else:
            prod = None
            for part in range(parts):
                term = lax.dot_general(a_ref[part].astype(BF16), b_ref[part].astype(BF16), dims,
                                       preferred_element_type=F32)
                prod = term if prod is None else prod + term
        if resid is not None:
            prod = r_ref[...] + prod
        o_ref[...] = prod.astype(out_dtype)
        if also_bf16:
            refs[-1][...] = prod.astype(BF16)

    in_specs = [a_spec, b_spec]
    args = [a, b]
    if resid is not None:
        in_specs.append(resid_spec)
        args.append(resid)
    sem = ["parallel"] * len(grid)
    out_specs = [o_spec]
    out_shapes = [jax.ShapeDtypeStruct(out_shape, out_dtype)]
    if also_bf16:
        out_specs.append(o_spec)
        out_shapes.append(jax.ShapeDtypeStruct(out_shape, BF16))
    out, extra = _call(body, name=name, grid=grid, in_specs=in_specs, out_specs=out_specs, out_shape=out_shapes,
                       args=args, sem=tuple(sem), hosted=hosted)
    res = out[0] if not also_bf16 else tuple(out)
    return res if hosted is None else (res, extra)


def _rms_rows(x):
    return lax.rsqrt(jnp.mean(x * x, axis=-1, keepdims=True) + RMS_EPS)


def _norm_fwd(name, x, gains):
    s, d = x.shape
    n = gains.shape[0]

    def body(x_ref, g_ref, *o_refs):
        xv = x_ref[...]
        xh = xv * _rms_rows(xv)
        for i in range(n):
            o_refs[i][...] = (xh * g_ref[i:i + 1, :]).astype(BF16)

    row = pl.BlockSpec((TM, d), lambda i: (i, 0))
    return pl.pallas_call(
        body, name=name, grid=(s // TM,),
        in_specs=[row, pl.BlockSpec((n, d), lambda i: (0, 0))],
        out_specs=[row] * n,
        out_shape=[jax.ShapeDtypeStruct((s, d), BF16)] * n,
        compiler_params=_params(("parallel",)),
    )(x, gains)


def _proj_norm_bwd(name, x, dres, gains, branches):
    s, d = x.shape
    n = len(branches)
    tm = min(TM_PARTS, s)

    def body(x_ref, r_ref, g_ref, *refs):
        ab_refs, dx_ref, dg_ref = refs[:2 * n], refs[2 * n], refs[2 * n + 1]
        i = pl.program_id(0)
        xv = x_ref[...]
        r = _rms_rows(xv)
        xh = xv * r

        @pl.when(i == 0)
        def _():
            dg_ref[...] = jnp.zeros_like(dg_ref)

        a = None
        for j in range(n):
            a_ref, b_ref = ab_refs[2 * j], ab_refs[2 * j + 1]
            dn = None
            for part in range(a_ref.shape[0]):
                term = lax.dot_general(a_ref[part], b_ref[part], NT, preferred_element_type=F32)
                dn = term if dn is None else dn + term
            t = dn * g_ref[j:j + 1, :]
            a = t if a is None else a + t
            dg_ref[j:j + 1, :] += jnp.sum(dn * xh, axis=0, keepdims=True)
        dx_ref[...] = r_ref[...] + r * (a - xh * jnp.mean(xh * a, axis=-1, keepdims=True))

    row = pl.BlockSpec((tm, d), lambda i: (i, 0))
    small = pl.BlockSpec((n, d), lambda i: (0, 0))
    ab_specs, ab_args = [], []
    for a, b in branches:
        ab_specs += [pl.BlockSpec((a.shape[0], tm, a.shape[2]), lambda i: (0, i, 0)),
                     pl.BlockSpec(b.shape, lambda i: (0, 0, 0))]
        ab_args += [a, b]
    return pl.pallas_call(
        body, name=name, grid=(s // tm,),
        in_specs=[row, row, small] + ab_specs,
        out_specs=[row, small],
        out_shape=[jax.ShapeDtypeStruct((s, d), F32), jax.ShapeDtypeStruct((n, d), F32)],
        compiler_params=_params(("arbitrary",)),
    )(x, dres, gains, *ab_args)


def _out_norms(name, u, w_out, resid, gains):
    s, d = resid.shape
    n = gains.shape[0]
    tm = min(TM_DENSE, s)

    def body(u_ref, w_ref, r_ref, g_ref, h_ref, *o_refs):
        hv = r_ref[...] + jnp.dot(u_ref[...], w_ref[...], preferred_element_type=F32)
        h_ref[...] = hv
        hh = hv * _rms_rows(hv)
        for i in range(n):
            o_refs[i][...] = (hh * g_ref[i:i + 1, :]).astype(BF16)

    row = pl.BlockSpec((tm, d), lambda i: (i, 0))
    return pl.pallas_call(
        body, name=name, grid=(s // tm,),
        in_specs=[row, pl.BlockSpec((d, d), lambda i: (0, 0)), row, pl.BlockSpec((n, d), lambda i: (0, 0))],
        out_specs=[row] * (n + 1),
        out_shape=[jax.ShapeDtypeStruct((s, d), F32)] + [jax.ShapeDtypeStruct((s, d), BF16)] * n,
        compiler_params=_params(("parallel",)),
    )(u, w_out, resid, gains)


def _out_loss_head(u, w_out, resid, target, gain):
    s, d = resid.shape
    tm = min(TM_PARTS, s)

    def body(u_ref, w_ref, r_ref, t_ref, g_ref, dh_ref, loss_ref, dg_ref):
        i = pl.program_id(0)
        hv = r_ref[...] + jnp.dot(u_ref[...], w_ref[...], preferred_element_type=F32)
        r = _rms_rows(hv)
        hh = hv * r
        g = g_ref[...]
        err = hh * g - t_ref[...]
        part = 0.5 * jnp.sum(jnp.sum(err * err, axis=-1, keepdims=True) * (1.0 / d), axis=0, keepdims=True)
        dy = err * (1.0 / d)
        a = dy * g
        dh_ref[...] = r * (a - hh * jnp.mean(hh * a, axis=-1, keepdims=True))
        dg = jnp.sum(dy * hh, axis=0, keepdims=True)

        @pl.when(i == 0)
        def _():
            loss_ref[...] = part
            dg_ref[...] = dg

        @pl.when(i > 0)
        def _():
            loss_ref[...] += part
            dg_ref[...] += dg

    row = pl.BlockSpec((tm, d), lambda i: (i, 0))
    return pl.pallas_call(
        body, name="out_b_loss_head", grid=(s // tm,),
        in_specs=[row, pl.BlockSpec((d, d), lambda i: (0, 0)), row, row, pl.BlockSpec((1, d), lambda i: (0, 0))],
        out_specs=[row, pl.BlockSpec((1, 1), lambda i: (0, 0)), pl.BlockSpec((1, d), lambda i: (0, 0))],
        out_shape=[jax.ShapeDtypeStruct((s, d), F32), jax.ShapeDtypeStruct((1, 1), F32),
                   jax.ShapeDtypeStruct((1, d), F32)],
        compiler_params=_params(("arbitrary",)),
    )(u, w_out, resid, target, gain)


def _silu_parts(g):
    sig = jax.nn.sigmoid(g)
    return g * sig, sig * (1.0 + g * (1.0 - sig))


def _lane_lo(rows):
    return lax.broadcasted_iota(jnp.int32, (rows, LANES), 1) < HEAD_DIM


def _stack_pair(x):
    lo = _lane_lo(x.shape[0])
    zero = jnp.zeros_like(x)
    return jnp.concatenate([jnp.where(lo, x, zero), jnp.where(lo, zero, x)], axis=0)


def _unstack_pair(y, w):
    return jnp.where(_lane_lo(w), y[:w], y[w:])


def _block_valid(b, left_blocks, width):
    col = lax.broadcasted_iota(jnp.int32, (1, 2 * width), 1)
    col = jnp.where(col >= width, col - width, col)
    return (col // KB + (b - left_blocks)) >= 0


def _toeplitz_tile(diag_row, width, left_chunks):
    wide = width + TQ
    rolled = pltpu.roll(jnp.broadcast_to(diag_row, (TQ, wide)), 1, 1, stride=1, stride_axis=0)
    i = lax.broadcasted_iota(jnp.int32, (TQ, width), 0) // CHUNK
    j = lax.broadcasted_iota(jnp.int32, (TQ, width), 1) // CHUNK
    dc = i + left_chunks - j
    return jnp.where((dc >= 0) & (dc <= left_chunks), rolled[:, TQ:], MASKED)


def _toeplitz_sum(tile, width):
    flip = (lax.broadcasted_iota(jnp.int32, (TQ, TQ), 0) + lax.broadcasted_iota(jnp.int32, (TQ, TQ), 1)
            == TQ - 1).astype(F32)
    reversed_rows = jnp.dot(flip, tile, precision=lax.Precision.HIGHEST, preferred_element_type=F32)
    padded = jnp.concatenate([reversed_rows, jnp.zeros((TQ, TQ), F32)], axis=1)
    rolled = pltpu.roll(padded, 0, 1, stride=1, stride_axis=0)
    return jnp.sum(rolled, axis=0, keepdims=True)


def _softmax_pair(sc, w, sink=None):
    ps, inv, lses = [], [], []
    for e in range(2):
        sh = sc[:, e * w:(e + 1) * w]
        m = jnp.max(sh, axis=-1, keepdims=True)
        if sink is not None:
            m = jnp.maximum(m, sink[e])
        ex = jnp.exp(sh - m)
        l = jnp.sum(ex, axis=-1, keepdims=True)
        if sink is not None:
            l = l + jnp.exp(sink[e] - m)
        ps.append(ex.astype(BF16))
        inv.append(1.0 / l)
        lses.append(m + jnp.log(l))
    return jnp.concatenate(ps, axis=-1), inv, lses


def _softmax_pair_bwd(sc, dp, lse, delta, w):
    ps, dss = [], []
    for e in range(2):
        p = jnp.exp(sc[:, e * w:(e + 1) * w] - lse[e])
        ps.append(p)
        dss.append(p * (dp[:, e * w:(e + 1) * w] - delta[e]))
    return jnp.concatenate(ps, axis=-1), jnp.concatenate(dss, axis=-1)


def _pair_rowsums(x, lo):
    zero = jnp.zeros_like(x)
    return (jnp.sum(jnp.where(lo, x, zero), axis=-1, keepdims=True),
            jnp.sum(jnp.where(lo, zero, x), axis=-1, keepdims=True))


def _a_qkv_specs(rows, pad, pw):
    return [pl.BlockSpec((None, TQ, pw), lambda p, b: (0, b + pad // TQ, p)),
            pl.BlockSpec((None, rows, pw), lambda p, b: (1, 0, p)),
            pl.BlockSpec((None, rows, pw), lambda p, b: (2, 0, p))]


def _window(ref, b, pad, win, lanes):
    start = pl.multiple_of(b * TQ + pad - (win - TQ), KB)
    return ref[pl.ds(start, win), lanes]


def _attn_a_fwd(zqkv, g, diag, hosted=None):
    s = g.shape[0]
    pad = zqkv.shape[1] - s
    nb = s // TQ
    left = A_KBLOCKS - 1
    pairs = A_PAIRS_FWD
    pw = pairs * LANES
    wide = A_WIN + TQ

    def body(q_ref, k_ref, v_ref, g_ref, diag_ref, o_ref, u_ref, lse_ref, bias_scr):
        b = pl.program_id(1)

        @pl.when(b == 0)
        def _():
            for hh in range(2 * pairs):
                bias_scr[hh // 2, :, (hh % 2) * A_WIN:(hh % 2 + 1) * A_WIN] = _toeplitz_tile(
                    diag_ref[hh], A_WIN, A_LEFT_CHUNKS)

        def step(first_blocks):
            lo = _lane_lo(TQ)
            for pp in range(pairs):
                ln = slice(pp * LANES, (pp + 1) * LANES)
                kcat = _stack_pair(_window(k_ref, b, pad, A_WIN, ln))
                vcat = _stack_pair(_window(v_ref, b, pad, A_WIN, ln))
                sc = lax.dot_general(q_ref[:, ln] * SCALE, kcat, NT, preferred_element_type=F32) + bias_scr[pp]
                if first_blocks:
                    sc = jnp.where(_block_valid(b, left, A_WIN), sc, MASKED)
                p, inv, lses = _softmax_pair(sc, A_WIN)
                ov = jnp.dot(p, vcat, preferred_element_type=F32) * jnp.where(lo, inv[0], inv[1])
                o_ref[:, ln] = ov
                lse_ref[pp] = jnp.where(lo, lses[0], lses[1])
                sg, _ = _silu_parts(g_ref[:, ln])
                u_ref[:, ln] = (ov * sg).astype(BF16)

        @pl.when(b < left)
        def _():
            step(True)

        @pl.when(b >= left)
        def _():
            step(False)

    tile = pl.BlockSpec((TQ, pw), lambda p, b: (b, p))
    return _call(
        body, name="attn_a_fwd", grid=(HEADS // 2 // pairs, nb),
        in_specs=_a_qkv_specs(pad + s, pad, pw) + [
            tile, pl.BlockSpec((2 * pairs, 1, wide), lambda p, b: (p, 0, 0))],
        out_specs=[tile, tile, pl.BlockSpec((pairs, TQ, LANES), lambda p, b: (p, b, 0))],
        out_shape=[jax.ShapeDtypeStruct((s, D_MODEL), F32), jax.ShapeDtypeStruct((s, D_MODEL), BF16),
                   jax.ShapeDtypeStruct((HEADS // 2, s, LANES), F32)],
        scratch_shapes=[pltpu.VMEM((pairs, TQ, 2 * A_WIN), F32)],
        sem=("parallel", "arbitrary"), hosted=hosted,
        args=(zqkv, zqkv, zqkv, g, diag))


def _attn_a_bwd(zqkv, g, o, du, lse, diag, hosted=None):
    s = g.shape[0]
    pad = zqkv.shape[1] - s
    nb = s // TQ
    left = A_KBLOCKS - 1
    pw = A_PAIRS * LANES
    wide = A_WIN + TQ

    def body(q_ref, k_ref, v_ref, g_ref, o_ref, du_ref, lse_ref, diag_ref, dz_ref, ddiag_ref,
             bias_scr, dbias_acc, dk_acc, dv_acc):
        b = pl.program_id(1)

        @pl.when(b == 0)
        def _():
            for hh in range(2 * A_PAIRS):
                bias_scr[hh // 2, :, (hh % 2) * A_WIN:(hh % 2 + 1) * A_WIN] = _toeplitz_tile(
                    diag_ref[hh], A_WIN, A_LEFT_CHUNKS)
            dbias_acc[...] = jnp.zeros_like(dbias_acc)
            dk_acc[...] = jnp.zeros_like(dk_acc)
            dv_acc[...] = jnp.zeros_like(dv_acc)

        def step(first_blocks):
            lo = _lane_lo(TQ)
            upper = lax.broadcasted_iota(jnp.int32, (LANES, A_WIN), 0) < HEAD_DIM
            rows = pl.ds(pl.multiple_of(b * TQ, TQ), TQ)
            sg, dsg = _silu_parts(g_ref[...])
            duv = du_ref[...]
            ov = o_ref[...]
            do = duv * sg
            dz_ref[3, rows, :] = (duv * ov * dsg).astype(BF16)
            do_o = do * ov
            do_bf = do.astype(BF16)
            for pp in range(A_PAIRS):
                ln = slice(pp * LANES, (pp + 1) * LANES)
                q = q_ref[:, ln] * SCALE
                kcat = _stack_pair(_window(k_ref, b, pad, A_WIN, ln))
                vcat = _stack_pair(_window(v_ref, b, pad, A_WIN, ln))
                sc = lax.dot_general(q, kcat, NT, preferred_element_type=F32) + bias_scr[pp]
                if first_blocks:
                    sc = jnp.where(_block_valid(b, left, A_WIN), sc, MASKED)
                lse_t = lse_ref[pp]
                dp = lax.dot_general(do_bf[:, ln], vcat, NT, preferred_element_type=F32)
                p, ds = _softmax_pair_bwd(sc, dp, (lse_t[:, 0:1], lse_t[:, HEAD_DIM:HEAD_DIM + 1]),
                                          _pair_rowsums(do_o[:, ln], lo), A_WIN)
                dbias_acc[pp] += ds
                dsb = ds.astype(BF16)
                dz_ref[0, rows, ln] = (jnp.dot(dsb, kcat, preferred_element_type=F32) * SCALE).astype(BF16)
                dkt = lax.dot_general(q, dsb, TN, preferred_element_type=F32)
                dvt = lax.dot_general(do_bf[:, ln], p.astype(BF16), TN, preferred_element_type=F32)
                dkt = jnp.where(upper, dkt[:, :A_WIN], dkt[:, A_WIN:])
                dvt = jnp.where(upper, dvt[:, :A_WIN], dvt[:, A_WIN:])
                for t in range(A_KBLOCKS):
                    blk = b + (pad // KB - left + t)
                    dk_acc[blk, ln, :] += dkt[:, t * KB:(t + 1) * KB]
                    dv_acc[blk, ln, :] += dvt[:, t * KB:(t + 1) * KB]

        @pl.when(b < left)
        def _():
            step(True)

        @pl.when(b >= left)
        def _():
            step(False)

        @pl.when(b == nb - 1)
        def _():
            for kb in range(s // KB):
                dz_ref[1, kb * KB:(kb + 1) * KB, :] = dk_acc[pad // KB + kb].T.astype(BF16)
                dz_ref[2, kb * KB:(kb + 1) * KB, :] = dv_acc[pad // KB + kb].T.astype(BF16)
            for hh in range(2 * A_PAIRS):
                ddiag_ref[hh] = _toeplitz_sum(
                    dbias_acc[hh // 2, :, (hh % 2) * A_WIN:(hh % 2 + 1) * A_WIN], A_WIN)

    tile = pl.BlockSpec((TQ, pw), lambda p, b: (b, p))
    diag_spec = pl.BlockSpec((2 * A_PAIRS, 1, wide), lambda p, b: (p, 0, 0))
    return _call(
        body, name="attn_a_bwd", grid=(HEADS // 2 // A_PAIRS, nb),
        in_specs=_a_qkv_specs(pad + s, pad, pw) + [
            tile, tile, tile, pl.BlockSpec((A_PAIRS, TQ, LANES), lambda p, b: (p, b, 0)), diag_spec],
        out_specs=[pl.BlockSpec((4, s, pw), lambda p, b: (0, 0, p)), diag_spec],
        out_shape=[jax.ShapeDtypeStruct((4, s, D_MODEL), BF16),
                   jax.ShapeDtypeStruct((HEADS, 1, wide), F32)],
        scratch_shapes=[pltpu.VMEM((A_PAIRS, TQ, 2 * A_WIN), F32), pltpu.VMEM((A_PAIRS, TQ, 2 * A_WIN), F32),
                        pltpu.VMEM(((pad + s) // KB, pw, KB), F32), pltpu.VMEM(((pad + s) // KB, pw, KB), F32)],
        sem=("parallel", "arbitrary"), hosted=hosted,
        args=(zqkv, zqkv, zqkv, g, o, du, lse, diag))


B_STACK = B_GROUP // 2
B_KVX = 4 * LANES
B_ROWS = B_STACK * TQ
B_WIDE = B_WIN + TQ


def _b_head_place(h):
    return h // B_GROUP, (h % B_GROUP) // 2, h % 2


def _toeplitz_tile_t(base_row, width, left_chunks):
    wide = width + TQ
    rolled = pltpu.roll(jnp.broadcast_to(base_row, (width, wide)), 0, 1, stride=1, stride_axis=0)
    j = lax.broadcasted_iota(jnp.int32, (width, TQ), 0) // CHUNK
    i = lax.broadcasted_iota(jnp.int32, (width, TQ), 1) // CHUNK
    dc = i + left_chunks - j
    return jnp.where((dc >= 0) & (dc <= left_chunks), rolled[:, :TQ], MASKED)


def _toeplitz_sum_t(tile_t, width):
    flip = (lax.broadcasted_iota(jnp.int32, (width, width), 0) + lax.broadcasted_iota(jnp.int32, (width, width), 1)
            == width - 1).astype(F32)
    reversed_rows = jnp.dot(flip, tile_t, precision=lax.Precision.HIGHEST, preferred_element_type=F32)
    padded = jnp.concatenate([reversed_rows, jnp.zeros((width, width), F32)], axis=1)
    rolled = pltpu.roll(padded, 0, 1, stride=1, stride_axis=0)
    return jnp.sum(rolled, axis=0, keepdims=True)


def _b_build_bias(base_ref, bias_scr):
    for h in range(HEADS):
        gi, pr, e = _b_head_place(h)
        bias_scr[gi, e * B_WIN:(e + 1) * B_WIN, pr * TQ:(pr + 1) * TQ] = _toeplitz_tile_t(
            base_ref[h], B_WIN, B_LEFT_CHUNKS)


def _b_stack(x, gi):
    return jnp.concatenate(
        [x[:, (B_STACK * gi + pr) * LANES:(B_STACK * gi + pr + 1) * LANES] for pr in range(B_STACK)], axis=0)


def _b_sink_rows(sink_ref, gi):
    block = lax.broadcasted_iota(jnp.int32, (1, B_ROWS), 1) // TQ
    rows = []
    for e in range(2):
        row = jnp.zeros((1, B_ROWS), F32)
        for pr in range(B_STACK):
            h = B_GROUP * gi + 2 * pr + e
            row = jnp.where(block == pr, sink_ref[0:1, h:h + 1], row)
        rows.append(row)
    return rows


def _b_scores_t(q_ref, kvv, bias_scr, gi, b, left, first_blocks):
    kcat = _stack_pair(kvv[:, gi * LANES:(gi + 1) * LANES])
    vcat = _stack_pair(kvv[:, (B_KV_HEADS + gi) * LANES:(B_KV_HEADS + gi + 1) * LANES])
    qs = _b_stack(q_ref, gi) * SCALE
    sc = lax.dot_general(kcat, qs, NT, preferred_element_type=F32) + bias_scr[gi]
    if first_blocks:
        row = lax.broadcasted_iota(jnp.int32, (2 * B_WIN, 1), 0)
        row = jnp.where(row >= B_WIN, row - B_WIN, row)
        sc = jnp.where((row // KB + (b - left)) >= 0, sc, MASKED)
    return kcat, vcat, qs, sc


def _attn_b_fwd(qb, kvx, gate, base, sinks):
    s = qb.shape[0]
    pad = kvx.shape[0] - s
    nb = s // TQ
    left = B_KBLOCKS - 1

    def body(q_ref, kv_ref, g_ref, base_ref, sink_ref, o_ref, u_ref, lse_ref, bias_scr):
        b = pl.program_id(0)

        @pl.when(b == 0)
        def _():
            _b_build_bias(base_ref, bias_scr)

        def step(first_blocks):
            kvv = _window(kv_ref, b, pad, B_WIN, slice(None))
            upper = lax.broadcasted_iota(jnp.int32, (LANES, B_ROWS), 0) < HEAD_DIM
            lse_rows = []
            for gi in range(B_KV_HEADS):
                kcat, vcat, qs, sc = _b_scores_t(q_ref, kvv, bias_scr, gi, b, left, first_blocks)
                sink = _b_sink_rows(sink_ref, gi)
                ps, inv = [], []
                for e in range(2):
                    sh = sc[e * B_WIN:(e + 1) * B_WIN]
                    m = jnp.maximum(jnp.max(sh, axis=0, keepdims=True), sink[e])
                    ex = jnp.exp(sh - m)
                    l = jnp.sum(ex, axis=0, keepdims=True) + jnp.exp(sink[e] - m)
                    ps.append(ex.astype(BF16))
                    inv.append(1.0 / l)
                    lse_rows.append(m + jnp.log(l))
                pt = jnp.concatenate(ps, axis=0)
                ot = lax.dot_general(vcat, pt, TN, preferred_element_type=F32) * jnp.where(upper, inv[0], inv[1])
                ov = ot.T
                for pr in range(B_STACK):
                    pair = B_STACK * gi + pr
                    o_ref[:, pair * LANES:(pair + 1) * LANES] = ov[pr * TQ:(pr + 1) * TQ]
            lse_ref[0] = jnp.concatenate(lse_rows + [jnp.zeros((8 - len(lse_rows), B_ROWS), F32)], axis=0)
            sg, _ = _silu_parts(g_ref[...])
            u_ref[...] = (o_ref[...] * sg).astype(BF16)

        @pl.when(b < left)
        def _():
            step(True)

        @pl.when(b >= left)
        def _():
            step(False)

    row = pl.BlockSpec((TQ, D_MODEL), lambda b: (b, 0))
    return pl.pallas_call(
        body, name="attn_b_fwd", grid=(nb,),
        in_specs=[row, pl.BlockSpec((pad + s, B_KVX), lambda b: (0, 0)), row,
                  pl.BlockSpec((HEADS, 1, B_WIDE), lambda b: (0, 0, 0)), pl.BlockSpec((1, HEADS), lambda b: (0, 0))],
        out_specs=[row, row, pl.BlockSpec((1, 8, B_ROWS), lambda b: (b, 0, 0))],
        out_shape=[jax.ShapeDtypeStruct((s, D_MODEL), F32), jax.ShapeDtypeStruct((s, D_MODEL), BF16),
                   jax.ShapeDtypeStruct((nb, 8, B_ROWS), F32)],
        scratch_shapes=[pltpu.VMEM((B_KV_HEADS, 2 * B_WIN, B_ROWS), F32)],
        compiler_params=_params(("arbitrary",)),
    )(qb, kvx, gate, base, sinks)


def _attn_b_bwd(qb, kvx, gate, o, du, lse, base, sinks):
    s = qb.shape[0]
    pad = kvx.shape[0] - s
    nb = s // TQ
    left = B_KBLOCKS - 1
    half = D_MODEL // 2

    def body(q_ref, kv_ref, g_ref, o_ref, du_ref, lse_ref, base_ref, sink_ref, dz_ref, dkv_ref, dsum_ref,
             dsink_ref, bias_scr, dbias_acc, dkv_acc, dsink_acc):
        b = pl.program_id(0)

        @pl.when(b == 0)
        def _():
            _b_build_bias(base_ref, bias_scr)
            dbias_acc[...] = jnp.zeros_like(dbias_acc)
            dkv_acc[...] = jnp.zeros_like(dkv_acc)
            dsink_acc[...] = jnp.zeros_like(dsink_acc)

        def step(first_blocks):
            kvv = _window(kv_ref, b, pad, B_WIN, slice(None))
            sg, dsg = _silu_parts(g_ref[...])
            duv = du_ref[...]
            ov = o_ref[...]
            do = duv * sg
            dgate = (duv * ov * dsg).astype(BF16)
            dz_ref[2] = dgate[:, :half]
            dz_ref[3] = dgate[:, half:]
            do_o = do * ov
            do_bf = do.astype(BF16)
            lse_all = lse_ref[0]
            dsink_rows = []
            for gi in range(B_KV_HEADS):
                kcat, vcat, qs, sc = _b_scores_t(q_ref, kvv, bias_scr, gi, b, left, first_blocks)
                dos = _b_stack(do_bf, gi)
                doo_t = _b_stack(do_o, gi).T
                delta = (jnp.sum(doo_t[:HEAD_DIM], axis=0, keepdims=True),
                         jnp.sum(doo_t[HEAD_DIM:], axis=0, keepdims=True))
                sink = _b_sink_rows(sink_ref, gi)
                dp = lax.dot_general(vcat, dos, NT, preferred_element_type=F32)
                ps, dss = [], []
                for e in range(2):
                    lse_e = lse_all[2 * gi + e:2 * gi + e + 1]
                    delta_e = delta[e]
                    p = jnp.exp(sc[e * B_WIN:(e + 1) * B_WIN] - lse_e)
                    ps.append(p.astype(BF16))
                    dss.append(p * (dp[e * B_WIN:(e + 1) * B_WIN] - delta_e))
                    dsink_rows.append(-jnp.exp(sink[e] - lse_e) * delta_e)
                ds = jnp.concatenate(dss, axis=0)
                dbias_acc[gi] += ds
                dsb = ds.astype(BF16)
                dq = (lax.dot_general(kcat, dsb, TN, preferred_element_type=F32) * SCALE).T.astype(BF16)
                for pr in range(B_STACK):
                    dz_ref[gi, :, pr * LANES:(pr + 1) * LANES] = dq[pr * TQ:(pr + 1) * TQ]
                dk = _unstack_pair(jnp.dot(dsb, qs, preferred_element_type=F32), B_WIN)
                dv = _unstack_pair(jnp.dot(jnp.concatenate(ps, axis=0), dos, preferred_element_type=F32), B_WIN)
                krows = pl.ds(pl.multiple_of(b * TQ + pad - (B_WIN - TQ), KB), B_WIN)
                dkv_acc[krows, gi * LANES:(gi + 1) * LANES] += dk
                dkv_acc[krows, (B_KV_HEADS + gi) * LANES:(B_KV_HEADS + gi + 1) * LANES] += dv
            dsink_acc[...] += jnp.concatenate(
                dsink_rows + [jnp.zeros((8 - len(dsink_rows), B_ROWS), F32)], axis=0)

        @pl.when(b < left)
        def _():
            step(True)

        @pl.when(b >= left)
        def _():
            step(False)

        @pl.when(b == nb - 1)
        def _():
            lo_s = _lane_lo(s)
            for which in range(2):
                folded = []
                for gi in range(B_KV_HEADS):
                    part = dkv_acc[pad:pad + s, (which * B_KV_HEADS + gi) * LANES:(which * B_KV_HEADS + gi + 1) * LANES]
                    folded.append(part + pltpu.roll(part, HEAD_DIM, 1))
                dkv_ref[:, which * LANES:(which + 1) * LANES] = jnp.where(lo_s, folded[0], folded[1]).astype(BF16)
            lane8 = lax.broadcasted_iota(jnp.int32, dsink_ref.shape, 1)
            tot = jnp.zeros(dsink_ref.shape, F32)
            for h in range(HEADS):
                gi, pr, e = _b_head_place(h)
                dsum_ref[h] = _toeplitz_sum_t(
                    dbias_acc[gi, e * B_WIN:(e + 1) * B_WIN, pr * TQ:(pr + 1) * TQ], B_WIN)
                per_query = dsink_acc[2 * gi + e:2 * gi + e + 1, pr * TQ:(pr + 1) * TQ]
                tot = jnp.where(lane8 == h, jnp.sum(per_query, axis=1, keepdims=True), tot)
            dsink_ref[...] = tot

    row = pl.BlockSpec((TQ, D_MODEL), lambda b: (b, 0))
    base_spec = pl.BlockSpec((HEADS, 1, B_WIDE), lambda b: (0, 0, 0))
    return pl.pallas_call(
        body, name="attn_b_bwd", grid=(nb,),
        in_specs=[row, pl.BlockSpec((pad + s, B_KVX), lambda b: (0, 0)), row, row, row,
                  pl.BlockSpec((1, 8, B_ROWS), lambda b: (b, 0, 0)), base_spec,
                  pl.BlockSpec((1, HEADS), lambda b: (0, 0))],
        out_specs=[pl.BlockSpec((4, TQ, half), lambda b: (0, b, 0)),
                   pl.BlockSpec((s, 2 * LANES), lambda b: (0, 0)), base_spec,
                   pl.BlockSpec((8, LANES), lambda b: (0, 0))],
        out_shape=[jax.ShapeDtypeStruct((4, s, half), BF16), jax.ShapeDtypeStruct((s, 2 * LANES), BF16),
                   jax.ShapeDtypeStruct((HEADS, 1, B_WIDE), F32), jax.ShapeDtypeStruct((8, LANES), F32)],
        scratch_shapes=[pltpu.VMEM((B_KV_HEADS, 2 * B_WIN, B_ROWS), F32),
                        pltpu.VMEM((B_KV_HEADS, 2 * B_WIN, B_ROWS), F32),
                        pltpu.VMEM((pad + s, B_KVX), F32), pltpu.VMEM((8, B_ROWS), F32)],
        compiler_params=_params(("arbitrary",)),
    )(qb, kvx, gate, o, du, lse, base, sinks)


def _t5_bucket(rel):
    nb = T5_BUCKETS // 2
    max_exact = nb // 2
    ret = jnp.where(rel > 0, nb, 0)
    n = jnp.abs(rel)
    nf = jnp.maximum(n, 1).astype(jnp.float32)
    large = max_exact + (jnp.log(nf / max_exact) / math.log(T5_MAX_DIST / max_exact)
                         * (nb - max_exact)).astype(jnp.int32)
    large = jnp.minimum(large, nb - 1)
    return ret + jnp.where(n < max_exact, n, large)


def _a_offset_onehot():
    c = np.arange(A_WIN + TQ)
    dist = A_LEFT_CHUNKS * CHUNK + TQ - 1 - c
    idx = np.clip(dist, -A_REL_CLIP, A_REL_CLIP) + A_REL_CLIP
    onehot = np.zeros((A_WIN + TQ, 2 * A_REL_CLIP + 1), np.float32)
    onehot[c, idx] = 1.0
    return jnp.asarray(onehot)


def _b_offset_onehot():
    c = jnp.arange(B_WIN + TQ, dtype=jnp.int32)
    rel = c - (TQ - 1) - B_LEFT_CHUNKS * CHUNK
    return (_t5_bucket(rel)[:, None] == jnp.arange(T5_BUCKETS)[None, :]).astype(F32)


def _diag_rows(onehot, table):
    rows = jnp.dot(onehot, table.astype(F32), precision=lax.Precision.HIGHEST)
    return rows.T.reshape(HEADS, 1, onehot.shape[0])


def _diag_rows_grad(onehot, ddiag):
    return jnp.dot(ddiag.reshape(HEADS, onehot.shape[0]), onehot, precision=lax.Precision.HIGHEST).T


def _position():
    x, y, c = lax.axis_index("x"), lax.axis_index("y"), lax.axis_index("c")
    chips = [(1 - x, y), (x, 1 - y), (1 - x, 1 - y)]
    return x, y, c, chips


ANY = pl.BlockSpec(memory_space=pl.ANY)


def _allgather_hosted(shards, split):
    n = len(shards)

    def part(ref, t, half):
        if not split[t]:
            return ref
        rows = shards[t].shape[0] // 2
        return ref.at[pl.ds(half * rows, rows)]

    def copies(kind, ins, outs, sems):
        send_sems, recv_sems, pass_send, pass_recv, local_sems = sems
        x, y, c, chips = _position()
        mine = 2 * x + y
        if kind == "local":
            return [pltpu.make_async_copy(ins[t], outs[t].at[mine], local_sems.at[t]) for t in range(n)]
        made = []
        for t in range(n):
            for j, chip in enumerate(chips):
                theirs = 2 * chip[0] + chip[1]
                far = dict(send_sem=send_sems.at[3 * t + j], recv_sem=recv_sems.at[3 * t + j],
                           device_id=(chip[0], chip[1], c), device_id_type=MESH)
                near = dict(send_sem=pass_send.at[3 * t + j], recv_sem=pass_recv.at[3 * t + j],
                            device_id=(x, y, 1 - c), device_id_type=MESH)
                here = part(outs[t].at[theirs], t, c)
                if kind == "send":
                    made.append(pltpu.make_async_remote_copy(
                        src_ref=part(ins[t], t, c), dst_ref=part(outs[t].at[mine], t, c), **far))
                elif kind == "landed":
                    made.append(pltpu.make_async_remote_copy(src_ref=here, dst_ref=here, **far))
                elif not split[t]:
                    made.append(None)
                elif kind == "pass":
                    made.append(pltpu.make_async_remote_copy(src_ref=here, dst_ref=here, **near))
                else:
                    other = part(outs[t].at[theirs], t, 1 - c)
                    made.append(pltpu.make_async_remote_copy(src_ref=other, dst_ref=other, **near))
        return made

    def first(ins, outs, sems):
        for cp in copies("local", ins, outs, sems) + copies("send", ins, outs, sems):
            cp.start()

    def middle(ins, outs, sems):
        for got, cp in zip(copies("landed", ins, outs, sems), copies("pass", ins, outs, sems)):
            got.wait_recv()
            if cp is not None:
                cp.start()

    def last(ins, outs, sems):
        for cp in copies("passed", ins, outs, sems):
            if cp is not None:
                cp.wait_recv()
        for cp in copies("send", ins, outs, sems) + copies("pass", ins, outs, sems):
            if cp is not None:
                cp.wait_send()
        for cp in copies("local", ins, outs, sems):
            cp.wait()

    return _Hosted(shards, [jax.ShapeDtypeStruct((4,) + w.shape, w.dtype) for w in shards],
                   [pltpu.SemaphoreType.DMA((3 * n,))] * 4 + [pltpu.SemaphoreType.DMA((n,))],
                   first, middle, last)


def _allgather_routed(shards):
    n = len(shards)

    def piece(block_ref, t, c, quarter=None):
        half = shards[t].shape[0] // 2
        if quarter is None:
            return block_ref.at[pl.ds(c * half, half)]
        return block_ref.at[pl.ds(c * half + quarter * (half // 2), half // 2)]

    def copies(kind, ins, outs, sems):
        ici_send, ici_recv, pass_send, pass_recv, local_sems = sems
        x, y, c, chips = _position()
        mine = 2 * x + y
        if kind == "local":
            return [pltpu.make_async_copy(ins[t], outs[t].at[mine], local_sems.at[t]) for t in range(n)]
        ids = [2 * chip[0] + chip[1] for chip in chips]
        made = []
        for t in range(n):
            def ici(k, to):
                return dict(send_sem=ici_send.at[4 * t + k], recv_sem=ici_recv.at[4 * t + k],
                            device_id=(chips[to][0], chips[to][1], c), device_id_type=MESH)

            def d2d(k):
                return dict(send_sem=pass_send.at[4 * t + k], recv_sem=pass_recv.at[4 * t + k],
                            device_id=(x, y, 1 - c), device_id_type=MESH)

            def same(ref, where):
                return pltpu.make_async_remote_copy(src_ref=ref, dst_ref=ref, **where)

            if kind == "send":
                for k in range(2):
                    made.append(pltpu.make_async_remote_copy(
                        src_ref=piece(ins[t], t, c), dst_ref=piece(outs[t].at[mine], t, c), **ici(k, k)))
            elif kind == "landed":
                made += [same(piece(outs[t].at[ids[k]], t, c), ici(k, k)) for k in range(2)]
            elif kind == "forward":
                made.append(same(piece(outs[t].at[ids[0]], t, c, 0), ici(2, 1)))
                made.append(same(piece(outs[t].at[ids[1]], t, c, 1), ici(3, 0)))
            elif kind == "arrived":
                made.append(same(piece(outs[t].at[ids[2]], t, c, 0), ici(2, 1)))
                made.append(same(piece(outs[t].at[ids[2]], t, c, 1), ici(3, 0)))
            else:
                core = 1 - c if kind == "passed" else c
                if kind in ("pass halves", "passed"):
                    made += [same(piece(outs[t].at[ids[k]], t, core), d2d(k)) for k in range(2)]
                if kind in ("pass quarters", "passed"):
                    made += [same(piece(outs[t].at[ids[2]], t, core, k), d2d(2 + k)) for k in range(2)]
        return made

    def first(ins, outs, sems):
        for cp in copies("local", ins, outs, sems) + copies("send", ins, outs, sems):
            cp.start()

    def middle(ins, outs, sems):
        for got, onward, near in zip(copies("landed", ins, outs, sems), copies("forward", ins, outs, sems),
                                     copies("pass halves", ins, outs, sems)):
            got.wait_recv()
            near.start()
            onward.start()

    def last(ins, outs, sems):
        quarters = copies("pass quarters", ins, outs, sems)
        for got, near in zip(copies("arrived", ins, outs, sems), quarters):
            got.wait_recv()
            near.start()
        for cp in copies("passed", ins, outs, sems):
            cp.wait_recv()
        for cp in (copies("send", ins, outs, sems) + copies("forward", ins, outs, sems)
                   + copies("pass halves", ins, outs, sems) + quarters):
            cp.wait_send()
        for cp in copies("local", ins, outs, sems):
            cp.wait()

    return _Hosted(shards, [jax.ShapeDtypeStruct((4,) + w.shape, w.dtype) for w in shards],
                   [pltpu.SemaphoreType.DMA((4 * n,))] * 4 + [pltpu.SemaphoreType.DMA((n,))],
                   first, middle, last)


def _scatter_hosted(grads):
    n = len(grads)

    def copies(ins, outs, sems):
        send_sems, recv_sems = sems
        x, y, c, chips = _position()
        return [pltpu.make_async_remote_copy(
            src_ref=ins[t].at[2 * chip[0] + chip[1]], dst_ref=outs[t].at[j],
            send_sem=send_sems.at[3 * t + j], recv_sem=recv_sems.at[3 * t + j],
            device_id=(chip[0], chip[1], c), device_id_type=MESH)
            for t in range(n) for j, chip in enumerate(chips)]

    def first(ins, outs, sems):
        for cp in copies(ins, outs, sems):
            cp.start()

    def last(ins, outs, sems):
        for cp in copies(ins, outs, sems):
            cp.wait()

    return _Hosted(grads, [jax.ShapeDtypeStruct((3,) + g.shape[1:], g.dtype) for g in grads],
                   [pltpu.SemaphoreType.DMA((3 * n,))] * 2, first, None, last)


GATHER_PEERS = "x and y neighbours (same core) and the sibling core"
SCATTER_PEERS = "the same core of the three other chips"
EVERYONE = "the seven other devices"


def _run_on_sequencer(name, hosted, peers, collective_id):
    ins = [jax.new_ref(a, memory_space=pltpu.MemorySpace.HBM) for a in hosted.inputs]
    outs = [jax.empty_ref(shape, memory_space=pltpu.MemorySpace.HBM) for shape in hosted.out_shapes]

    @pl.kernel(mesh=plsc.ScalarSubcoreMesh(axis_name="sequencer", num_cores=1), name=name,
               scratch_types=tuple(hosted.sems), compiler_params=pltpu.CompilerParams(collective_id=collective_id))
    def launch(*sems):
        x, y, c, chips = _position()
        if peers == GATHER_PEERS:
            devices = [(chip[0], chip[1], c) for chip in chips[:2]] + [(x, y, 1 - c)]
        elif peers == SCATTER_PEERS:
            devices = [(chip[0], chip[1], c) for chip in chips]
        else:
            devices = [(x ^ (k >> 2), y ^ ((k >> 1) & 1), c ^ (k & 1)) for k in range(1, 8)]
        barrier = pltpu.get_barrier_semaphore()
        for device in devices:
            pl.semaphore_signal(barrier, inc=1, device_id=device, device_id_type=MESH)
        pl.semaphore_wait(barrier, len(devices))
        hosted.first(ins, outs, sems)
        if hosted.middle is not None:
            hosted.middle(ins, outs, sems)
        hosted.last(ins, outs, sems)

    launch()
    return [o[...] for o in outs]


def _run_alone(name, hosted):
    n_in = len(hosted.inputs)
    n_out = len(hosted.out_shapes)

    def body(*refs):
        ins, outs, sems = refs[:n_in], refs[n_in:n_in + n_out], refs[n_in + n_out:]
        hosted.first(ins, outs, sems)
        if hosted.middle is not None:
            hosted.middle(ins, outs, sems)
        hosted.last(ins, outs, sems)

    return pl.pallas_call(
        body, name=name, in_specs=[ANY] * n_in, out_specs=[ANY] * n_out, out_shape=hosted.out_shapes,
        scratch_shapes=hosted.sems)(*hosted.inputs)


def _swap_with_sibling(blocks):
    n = len(blocks)

    def body(*refs):
        ins, outs = refs[:n], refs[n:2 * n]
        send_sems, recv_sems = refs[2 * n:]
        x, y, c, _ = _position()
        sends = [pltpu.make_async_remote_copy(
            src_ref=ins[t], dst_ref=outs[t], send_sem=send_sems.at[t], recv_sem=recv_sems.at[t],
            device_id=(x, y, 1 - c), device_id_type=MESH) for t in range(n)]
        for cp in sends:
            cp.start()
        for cp in sends:
            cp.wait()

    return pl.pallas_call(
        body, name="swap_with_sibling",
        in_specs=[ANY] * n, out_specs=[ANY] * n,
        out_shape=[jax.ShapeDtypeStruct(b.shape, b.dtype) for b in blocks],
        scratch_shapes=[pltpu.SemaphoreType.DMA((n,))] * 2,
    )(*blocks)


def _everyone_hosted(terms):
    nt = len(terms)

    def copies(kind, ins, outs, sems):
        send_sems, recv_sems, local_sems = sems
        x, y, c, _ = _position()
        me = 4 * x + 2 * y + c
        if kind == "local":
            return [pltpu.make_async_copy(ins[t], outs[t].at[me], local_sems.at[t]) for t in range(nt)]
        made = []
        for t in range(nt):
            for k in range(1, 8):
                peer = (x ^ (k >> 2), y ^ ((k >> 1) & 1), c ^ (k & 1))
                slot = me if kind == "send" else me ^ k
                made.append(pltpu.make_async_remote_copy(
                    src_ref=ins[t], dst_ref=outs[t].at[slot], send_sem=send_sems.at[7 * t + k - 1],
                    recv_sem=recv_sems.at[7 * t + k - 1], device_id=peer, device_id_type=MESH))
        return made

    def first(ins, outs, sems):
        for cp in copies("local", ins, outs, sems) + copies("send", ins, outs, sems):
            cp.start()

    def last(ins, outs, sems):
        for cp in copies("landed", ins, outs, sems):
            cp.wait_recv()
        for cp in copies("send", ins, outs, sems):
            cp.wait_send()
        for cp in copies("local", ins, outs, sems):
            cp.wait()

    return _Hosted(terms, [jax.ShapeDtypeStruct((8,) + a.shape, F32) for a in terms],
                   [pltpu.SemaphoreType.DMA((7 * nt,))] * 2 + [pltpu.SemaphoreType.DMA((nt,))], first, None, last)


def _small_step(partials, extras, ws, ms, vs, shard_of):
    n = len(partials)
    terms = list(partials) + list(extras)
    nt = len(terms)
    rows = [t for t in range(nt) if terms[t].shape[0] == 1]
    mats = [t for t in range(nt) if terms[t].shape[0] != 1]
    row_block = (8, max(terms[t].shape[1] for t in rows))
    assert len(rows) <= row_block[0]
    vmem = pl.BlockSpec(memory_space=pltpu.VMEM)

    def pack(*refs):
        packed = refs[-1]
        packed[...] = jnp.zeros_like(packed)
        for i, t in enumerate(rows):
            packed[i:i + 1, 0:terms[t].shape[1]] = refs[i][...]

    packed = pl.pallas_call(pack, name="small_pack", in_specs=[vmem] * len(rows), out_specs=vmem,
                            out_shape=jax.ShapeDtypeStruct(row_block, F32))(*[terms[t] for t in rows])
    slots = _run_on_sequencer("allgather_small", _everyone_hosted([packed] + [terms[t] for t in mats]),
                              EVERYONE, 2)

    def body(*refs):
        slot_refs, refs = refs[:len(slots)], refs[len(slots):]
        w_refs, refs = refs[:n], refs[n:]
        m_refs, refs = refs[:n], refs[n:]
        v_refs, outs = refs[:n], refs[n:]
        sums = []
        for ref in slot_refs:
            g = ref[0]
            for dev in range(1, 8):
                g = g + ref[dev]
            sums.append(g)
        chip = 2 * lax.axis_index("x") + lax.axis_index("y")
        for t in range(nt):
            if t in rows:
                i = rows.index(t)
                g = sums[0][i:i + 1, 0:terms[t].shape[1]]
            else:
                g = sums[1 + mats.index(t)]
            if t >= n:
                outs[4 * n + t - n][...] = g
                continue
            if shard_of[t]:
                width = ws[t].shape[-1]
                mine = jnp.zeros(ws[t].shape, F32)
                for s in range(4):
                    mine = jnp.where(chip == s, g[:, s * width:(s + 1) * width], mine)
                g = mine
            delta, mn, vn = _adamw_math(w_refs[t][...], g, m_refs[t][...], v_refs[t][...])
            outs[4 * t][...] = g
            outs[4 * t + 1][...] = delta
            outs[4 * t + 2][...] = mn
            outs[4 * t + 3][...] = vn

    out_shapes = []
    for t in range(n):
        out_shapes += [jax.ShapeDtypeStruct(ws[t].shape, F32)] * 4
    out_shapes += [jax.ShapeDtypeStruct(a.shape, F32) for a in extras]
    res = pl.pallas_call(
        body, name="small_step",
        in_specs=[vmem] * (len(slots) + 3 * n), out_specs=[vmem] * len(out_shapes), out_shape=out_shapes,
    )(*slots, *ws, *ms, *vs)
    return [res[4 * t:4 * t + 4] for t in range(n)], res[4 * n:4 * n + nt - n]


def _adamw_math(w, g, m, v):
    m = ADAM_B1 * m + (1.0 - ADAM_B1) * g
    v = ADAM_B2 * v + (1.0 - ADAM_B2) * (g * g)
    m_hat = m / (1.0 - ADAM_B1 ** ADAM_STEP)
    v_hat = v / (1.0 - ADAM_B2 ** ADAM_STEP)
    delta = -ADAM_LR * (m_hat / (jnp.sqrt(v_hat) + ADAM_EPS) + ADAM_WD * w)
    return delta, m, v


def _row_tile(rows):
    return 256 if rows % 256 == 0 else rows


def _sum_partials(name, own, recv, chip):
    rows, cols = own.shape[1:]
    tr = _row_tile(rows)

    def body(chip_ref, own_ref, recv_ref, o_ref):
        acc = own_ref[...]
        for j in range(3):
            acc = acc + recv_ref[j].astype(F32)
        o_ref[...] = acc

    return pl.pallas_call(
        body, name=name,
        grid_spec=pltpu.PrefetchScalarGridSpec(
            num_scalar_prefetch=1, grid=(rows // tr,),
            in_specs=[pl.BlockSpec((None, tr, cols), lambda i, chip_ref: (chip_ref[0], i, 0)),
                      pl.BlockSpec((3, tr, cols), lambda i, chip_ref: (0, i, 0))],
            out_specs=pl.BlockSpec((tr, cols), lambda i, chip_ref: (i, 0))),
        out_shape=jax.ShapeDtypeStruct((rows, cols), F32),
        compiler_params=_params(("parallel",)),
    )(chip.reshape(1).astype(jnp.int32), own, recv)


def _adamw(name, w, m, v, g_parts):
    rows, cols = w.shape
    tr = _row_tile(rows)
    n = len(g_parts)

    def body(w_ref, m_ref, v_ref, *refs):
        g_refs = refs[:n]
        go_ref, d_ref, mo_ref, vo_ref = refs[n:]
        g = g_refs[0][...]
        for r in g_refs[1:]:
            g = g + r[...]
        delta, mn, vn = _adamw_math(w_ref[...], g, m_ref[...], v_ref[...])
        go_ref[...] = g
        d_ref[...] = delta
        mo_ref[...] = mn
        vo_ref[...] = vn

    spec = pl.BlockSpec((tr, cols), lambda i: (i, 0))
    return pl.pallas_call(
        body, name=name, grid=(rows // tr,),
        in_specs=[spec] * (3 + n), out_specs=[spec] * 4,
        out_shape=[jax.ShapeDtypeStruct((rows, cols), F32)] * 4,
        compiler_params=_params(("parallel",)),
    )(w, m, v, *g_parts)


def _local_step(x, target, ga, wa_in, rel_bias, later_shards, gk, t5, gb, sinks, gf):
    s, d = x.shape
    tm = min(TM_DENSE, s)
    nt = s // tm
    half = d // 2
    row = pl.BlockSpec((tm, d), lambda i: (i, 0))
    whole = lambda shape: pl.BlockSpec(shape, lambda *_: (0,) * len(shape))

    n1, = _norm_fwd("norm_a", x, ga)
    zqkv = _matmul("proj_a_qkv", n1, wa_in, dims=NN, grid=(3, nt + 1), zero_axis=1,
                   a_spec=pl.BlockSpec((tm, d), lambda j, i: (jnp.maximum(i - 1, 0), 0)),
                   b_spec=pl.BlockSpec((None, d, d), lambda j, i: (j, 0, 0)),
                   o_spec=pl.BlockSpec((None, tm, d), lambda j, i: (j, i, 0)),
                   out_shape=(3, tm + s, d), out_dtype=BF16)
    gate_a = _matmul("proj_a_gate", n1, wa_in, dims=NN, grid=(nt,),
                     a_spec=row, b_spec=pl.BlockSpec((None, d, d), lambda i: (3, 0, 0)), o_spec=row,
                     out_shape=(s, d), out_dtype=F32)
    onehot_a = _a_offset_onehot()
    diag_a = _diag_rows(onehot_a, rel_bias)
    (o_a, u_a, lse_a), gathered = _attn_a_fwd(zqkv, gate_a, diag_a, hosted=_allgather_routed(later_shards))
    wa_out, wkv, wb_in, wb_out = gathered
    wa_out = wa_out.reshape(d, d)
    wkv = wkv.reshape(d, -1)
    wb_out = wb_out.reshape(d, d)
    h1, nk, n2 = _out_norms("out_a_norms", u_a, wa_out, x, jnp.concatenate([gk, gb], axis=0))
    kvw = wkv.shape[1]
    wkv_x = jnp.concatenate([wkv[:, (i // 2) * HEAD_DIM:(i // 2 + 1) * HEAD_DIM] for i in range(8)], axis=1)
    kvx = _matmul("proj_kv", nk, wkv_x, dims=NN, grid=(nt + 1,), zero_axis=0,
                  a_spec=pl.BlockSpec((tm, d), lambda i: (jnp.maximum(i - 1, 0), 0)), b_spec=whole((d, B_KVX)),
                  o_spec=pl.BlockSpec((tm, B_KVX), lambda i: (i, 0)), out_shape=(tm + s, B_KVX), out_dtype=BF16)
    qb = _matmul("proj_b_q", n2, wb_in, dims=NN, grid=(2, nt),
                 a_spec=pl.BlockSpec((tm, d), lambda j, i: (i, 0)),
                 b_spec=pl.BlockSpec((None, d, half), lambda j, i: (j, 0, 0)),
                 o_spec=pl.BlockSpec((tm, half), lambda j, i: (i, j)), out_shape=(s, d), out_dtype=BF16)
    gate_b = _matmul("proj_b_gate", n2, wb_in, dims=NN, grid=(2, nt),
                     a_spec=pl.BlockSpec((tm, d), lambda j, i: (i, 0)),
                     b_spec=pl.BlockSpec((None, d, half), lambda j, i: (2 + j, 0, 0)),
                     o_spec=pl.BlockSpec((tm, half), lambda j, i: (i, j)), out_shape=(s, d), out_dtype=F32)
    onehot_b = _b_offset_onehot()
    base_b = jnp.roll(_diag_rows(onehot_b, t5)[..., ::-1], TQ, axis=-1)
    o_b, u_b, lse_b = _attn_b_fwd(qb, kvx, gate_b, base_b, sinks)
    dh2, loss, d_gf = _out_loss_head(u_b, wb_out, h1, target, gf)

    du_b = _matmul("dout_b", dh2, wb_out, dims=NT, grid=(nt,), a_spec=row, b_spec=whole((d, d)), o_spec=row,
                   out_shape=(s, d), out_dtype=F32)
    d_wb_out = _matmul("dw_out_b", u_b, dh2, dims=TN, grid=(2,),
                       a_spec=whole((s, d)), b_spec=pl.BlockSpec((s, half), lambda j: (0, j)),
                       o_spec=pl.BlockSpec((d, half), lambda j: (0, j)),
                       out_shape=(d, d), out_dtype=F32, also_bf16=True)
    dz_b, dkv, dsum_b, dsinks = _attn_b_bwd(qb, kvx, gate_b, o_b, du_b, lse_b, base_b, sinks)
    ddiag_b = jnp.roll(dsum_b[..., ::-1], -1, axis=-1)
    d_wb_in = _matmul("dw_in_b", n2, dz_b, dims=TN, grid=(4,),
                      a_spec=whole((s, d)), b_spec=pl.BlockSpec((None, s, half), lambda j: (j, 0, 0)),
                      o_spec=pl.BlockSpec((None, d, half), lambda j: (j, 0, 0)),
                      out_shape=(4, d, half), out_dtype=F32, also_bf16=True)
    d_wkv = _matmul("dw_kv", nk, dkv, dims=TN, grid=(1,),
                    a_spec=whole((s, d)), b_spec=whole((s, kvw)), o_spec=whole((d, kvw)),
                    out_shape=(d, kvw), out_dtype=F32, also_bf16=True)
    dh1, d_gkb = _proj_norm_bwd("dproj_kv_b", h1, dh2, jnp.concatenate([gk, gb], axis=0),
                                [(dkv[None], wkv[None]), (dz_b, wb_in)])

    du_a = _matmul("dout_a", dh1, wa_out, dims=NT, grid=(nt,), a_spec=row, b_spec=whole((d, d)), o_spec=row,
                   out_shape=(s, d), out_dtype=F32)
    d_wa_out = _matmul("dw_out_a", u_a, dh1, dims=TN, grid=(2,),
                       a_spec=whole((s, d)), b_spec=pl.BlockSpec((s, half), lambda j: (0, j)),
                       o_spec=pl.BlockSpec((d, half), lambda j: (0, j)),
                       out_shape=(d, d), out_dtype=F32, also_bf16=True)
    early = dict(a_w_out=[g.reshape(4, d // 4, d) for g in d_wa_out],
                 kv_w=[g.reshape(4, d // 4, kvw) for g in d_wkv], b_w_in=list(d_wb_in),
                 b_w_out=[g.reshape(4, d // 4, d) for g in d_wb_out])
    (dz_a, ddiag_a), early_recv = _attn_a_bwd(
        zqkv, gate_a, o_a, du_a, lse_a, diag_a, hosted=_scatter_hosted([early[n][1] for n in early]))
    d_wa_in = _matmul("dw_in_a", n1, dz_a, dims=TN, grid=(4, 2),
                      a_spec=whole((s, d)), b_spec=pl.BlockSpec((None, s, half), lambda j, h: (j, 0, h)),
                      o_spec=pl.BlockSpec((None, d, half), lambda j, h: (j, 0, h)),
                      out_shape=(4, d, d), out_dtype=F32, also_bf16=True)
    late_recv = _run_on_sequencer("scatter_a_w_in", _scatter_hosted([d_wa_in[1]]), SCATTER_PEERS, 0)
    grad_x, d_ga = _proj_norm_bwd("dproj_a", x, dh1, ga, [(dz_a, wa_in)])

    small = dict(a_norm=d_ga, kv_norm=d_gkb[0:1], b_norm=d_gkb[1:2], b_sinks=dsinks[0:1, :HEADS], final_norm=d_gf)
    small["by_offset"] = dict(a_rel_bias=(onehot_a, ddiag_a.reshape(HEADS, -1)),
                              t5_bias=(onehot_b, ddiag_b.reshape(HEADS, -1)))
    own = dict(a_w_in=d_wa_in[0], **{n: early[n][0] for n in early})
    received = dict(a_w_in=late_recv[0], **dict(zip(early, early_recv)))
    return loss, grad_x, small, own, received


SMALL = ("a_norm", "kv_norm", "b_norm", "b_sinks", "final_norm")
TABLES = ("a_rel_bias", "t5_bias")
BIG = ("a_w_in", "a_w_out", "kv_w", "b_w_in", "b_w_out")
ORDER = ("a_norm", "a_w_in", "a_rel_bias", "a_w_out", "kv_norm", "kv_w", "t5_bias", "b_norm", "b_w_in",
         "b_sinks", "b_w_out", "final_norm")


def kernel(x, a_norm, a_w_in, a_rel_bias, a_w_out, kv_norm, kv_w, t5_bias, b_norm, b_w_in, b_sinks, b_w_out, final_norm, loss_target, m_a_norm, m_a_w_in, m_a_rel_bias, m_a_w_out, m_kv_norm, m_kv_w, m_t5_bias, m_b_norm, m_b_w_in, m_b_sinks, m_b_w_out, m_final_norm, v_a_norm, v_a_w_in, v_a_rel_bias, v_a_w_out, v_kv_norm, v_kv_w, v_t5_bias, v_b_norm, v_b_w_in, v_b_sinks, v_b_w_out, v_final_norm):
    w = dict(a_norm=a_norm, a_w_in=a_w_in, a_rel_bias=a_rel_bias, a_w_out=a_w_out, kv_norm=kv_norm, kv_w=kv_w,
             t5_bias=t5_bias, b_norm=b_norm, b_w_in=b_w_in, b_sinks=b_sinks, b_w_out=b_w_out,
             final_norm=final_norm)
    m = dict(a_norm=m_a_norm, a_w_in=m_a_w_in, a_rel_bias=m_a_rel_bias, a_w_out=m_a_w_out, kv_norm=m_kv_norm,
             kv_w=m_kv_w, t5_bias=m_t5_bias, b_norm=m_b_norm, b_w_in=m_b_w_in, b_sinks=m_b_sinks,
             b_w_out=m_b_w_out, final_norm=m_final_norm)
    v = dict(a_norm=v_a_norm, a_w_in=v_a_w_in, a_rel_bias=v_a_rel_bias, a_w_out=v_a_w_out, kv_norm=v_kv_norm,
             kv_w=v_kv_w, t5_bias=v_t5_bias, b_norm=v_b_norm, b_w_in=v_b_w_in, b_sinks=v_b_sinks,
             b_w_out=v_b_w_out, final_norm=v_final_norm)
    d = D_MODEL
    chip = 2 * lax.axis_index("x") + lax.axis_index("y")

    shard2d = dict(a_w_in=a_w_in[0], a_w_out=a_w_out[0], kv_w=kv_w, b_w_in=b_w_in[0], b_w_out=b_w_out[0])

    wa_in, = _run_on_sequencer("allgather_first", _allgather_routed([shard2d["a_w_in"].astype(BF16)]),
                               GATHER_PEERS, 1)
    ga, = _run_alone("allgather_norm", _allgather_hosted([a_norm], [False]))
    ga = ga.reshape(1, d)

    loss, grad_x, small, own, received = _local_step(
        x[0], loss_target[0], ga, wa_in, a_rel_bias[0], [shard2d[n].astype(BF16) for n in BIG[1:]],
        kv_norm.reshape(1, d), t5_bias, b_norm, b_sinks, final_norm.reshape(1, d))

    out = {}
    as2d = lambda a: a.reshape(-1, a.shape[-1])
    small_res, (loss_sum, *offset_sums) = _small_step(
        [small[n] for n in SMALL], [loss] + [small["by_offset"][n][1] for n in TABLES],
        [as2d(w[n]) for n in SMALL], [as2d(m[n]) for n in SMALL], [as2d(v[n]) for n in SMALL],
        [n == "a_norm" for n in SMALL])
    for n, res in zip(SMALL, small_res):
        out[n] = [r.reshape(w[n].shape) for r in res]
    loss_out = loss_sum.reshape(())
    for n, summed in zip(TABLES, offset_sums):
        grad = _diag_rows_grad(small["by_offset"][n][0], summed)
        res = _adamw("adamw_" + n, as2d(w[n]), as2d(m[n]), as2d(v[n]), [grad])
        out[n] = [r.reshape(w[n].shape) for r in res]

    core_sums = [_sum_partials("sum_" + n, own[n], received[n], chip) for n in BIG]
    sibling_sums = _swap_with_sibling(core_sums)

    for n, mine, theirs in zip(BIG, core_sums, sibling_sums):
        res = _adamw("adamw_" + n, shard2d[n], m[n].reshape(shard2d[n].shape), v[n].reshape(shard2d[n].shape),
                     [mine, theirs])
        out[n] = [r.reshape(w[n].shape) for r in res]

    grads = [out[n][0] for n in ORDER]
    deltas = [out[n][1] for n in ORDER]
    new_m = [out[n][2] for n in ORDER]
    new_v = [out[n][3] for n in ORDER]
    return (loss_out, grad_x[None], *grads, *deltas, *new_m, *new_v)
```

```python
import functools
import math

import jax
import jax.numpy as jnp
import numpy as np
from jax import lax
from jax.experimental import pallas as pl
from jax.experimental.pallas import tpu as pltpu
from jax.experimental.pallas import tpu_sc as plsc

F32 = jnp.float32
BF16 = jnp.bfloat16
MESH = pl.DeviceIdType.MESH

D_MODEL = 1024
HEADS = 16
HEAD_DIM = 64
CHUNK = 64
RMS_EPS = 1e-6
SCALE = HEAD_DIM ** -0.5
A_LEFT_CHUNKS = 8
A_REL_CLIP = 256
B_LEFT_CHUNKS = 2
B_KV_HEADS = 2
B_GROUP = HEADS // B_KV_HEADS
T5_BUCKETS = 32
T5_MAX_DIST = 128
ADAM_LR = 0.001
ADAM_B1 = 0.9
ADAM_B2 = 0.999
ADAM_EPS = 1e-08
ADAM_WD = 0.01
ADAM_STEP = 10

MASKED = -1e30
LANES = 128
TQ = 128
A_PAIRS = 2
A_PAIRS_FWD = 4
KB = 128
A_KBLOCKS = A_LEFT_CHUNKS * CHUNK // KB + 1
B_KBLOCKS = B_LEFT_CHUNKS * CHUNK // KB + 1
A_WIN = A_KBLOCKS * KB
B_WIN = B_KBLOCKS * KB
TM = 512
TM_DENSE = 1024
TM_PARTS = 512
VMEM_LIMIT = 56 * 1024 * 1024

NT = (((1,), (1,)), ((), ()))
TN = (((0,), (0,)), ((), ()))
NN = (((1,), (0,)), ((), ()))


def _params(sem=None):
    return pltpu.CompilerParams(dimension_semantics=sem, vmem_limit_bytes=VMEM_LIMIT)


class _Hosted:
    def __init__(self, inputs, out_shapes, sems, first, middle, last):
        self.inputs, self.out_shapes, self.sems = list(inputs), list(out_shapes), list(sems)
        self.first, self.middle, self.last = first, middle, last


def _call(body, *, name, grid, in_specs, out_specs, out_shape, args, scratch_shapes=(), sem=None, hosted=None):
    in_specs, out_specs, out_shape = list(in_specs), list(out_specs), list(out_shape)
    scratch_shapes = list(scratch_shapes)
    if hosted is None:
        out = pl.pallas_call(
            body, name=name, grid=grid, in_specs=in_specs, out_specs=out_specs, out_shape=out_shape,
            scratch_shapes=scratch_shapes, compiler_params=_params(sem))(*args)
        return list(out), []
    n_in, n_out, n_scr = len(in_specs), len(out_shape), len(scratch_shapes)
    h_in, h_out = len(hosted.inputs), len(hosted.out_shapes)
    total = int(np.prod(grid)) if grid else 1

    def wrapped(*refs):
        ins, refs = refs[:n_in], refs[n_in:]
        h_ins, refs = refs[:h_in], refs[h_in:]
        outs, refs = refs[:n_out], refs[n_out:]
        h_outs, refs = refs[:h_out], refs[h_out:]
        scr, h_sems = refs[:n_scr], refs[n_scr:]
        step = 0
        for axis, size in enumerate(grid):
            step = step * size + pl.program_id(axis)

        @pl.when(step == 0)
        def _():
            hosted.first(h_ins, h_outs, h_sems)

        body(*ins, *outs, *scr)
        if hosted.middle is not None:
            @pl.when(step == total // 2)
            def _():
                hosted.middle(h_ins, h_outs, h_sems)

        @pl.when(step == total - 1)
        def _():
            hosted.last(h_ins, h_outs, h_sems)

    out = pl.pallas_call(
        wrapped, name=name, grid=grid, in_specs=in_specs + [ANY] * h_in, out_specs=out_specs + [ANY] * h_out,
        out_shape=out_shape + hosted.out_shapes, scratch_shapes=scratch_shapes + hosted.sems,
        compiler_params=_params(("arbitrary",) * len(grid)))(*args, *hosted.inputs)
    return list(out[:n_out]), list(out[n_out:])


def _matmul(name, a, b, *, dims, grid, a_spec, b_spec, o_spec, out_shape, out_dtype,
            parts=1, resid=None, resid_spec=None, also_bf16=False, hosted=None, zero_axis=None):
    def body(*refs):
        if zero_axis is None:
            product(*refs)
        else:
            @pl.when(pl.program_id(zero_axis) == 0)
            def _():
                refs[2][...] = jnp.zeros_like(refs[2])

            @pl.when(pl.program_id(zero_axis) > 0)
            def _():
                product(*refs)

    def product(*refs):
        a_ref, b_ref = refs[:2]
        r_ref = refs[2] if resid is not None else None
        o_ref = refs[3] if resid is not None else refs[2]
        if parts == 1:
            prod = lax.dot_general(a_ref[...].astype(BF16), b_ref[...].astype(BF16), dims,
                                   preferred_element_type=F32)
        else:
            prod = None
            for part in range(parts):
                term = lax.dot_general(a_ref[part].astype(BF16), b_ref[part].astype(BF16), dims,
                                       preferred_element_type=F32)
                prod = term if prod is None else prod + term
        if resid is not None:
            prod = r_ref[...] + prod
        o_ref[...] = prod.astype(out_dtype)
        if also_bf16:
            refs[-1][...] = prod.astype(BF16)

    in_specs = [a_spec, b_spec]
    args = [a, b]
    if resid is not None:
        in_specs.append(resid_spec)
        args.append(resid)
    sem = ["parallel"] * len(grid)
    out_specs = [o_spec]
    out_shapes = [jax.ShapeDtypeStruct(out_shape, out_dtype)]
    if also_bf16:
        out_specs.append(o_spec)
        out_shapes.append(jax.ShapeDtypeStruct(out_shape, BF16))
    out, extra = _call(body, name=name, grid=grid, in_specs=in_specs, out_specs=out_specs, out_shape=out_shapes,
                       args=args, sem=tuple(sem), hosted=hosted)
    res = out[0] if not also_bf16 else tuple(out)
    return res if hosted is None else (res, extra)


def _rms_rows(x):
    return lax.rsqrt(jnp.mean(x * x, axis=-1, keepdims=True) + RMS_EPS)


def _norm_fwd(name, x, gains):
    s, d = x.shape
    n = gains.shape[0]

    def body(x_ref, g_ref, *o_refs):
        xv = x_ref[...]
        xh = xv * _rms_rows(xv)
        for i in range(n):
            o_refs[i][...] = (xh * g_ref[i:i + 1, :]).astype(BF16)

    row = pl.BlockSpec((TM, d), lambda i: (i, 0))
    return pl.pallas_call(
        body, name=name, grid=(s // TM,),
        in_specs=[row, pl.BlockSpec((n, d), lambda i: (0, 0))],
        out_specs=[row] * n,
        out_shape=[jax.ShapeDtypeStruct((s, d), BF16)] * n,
        compiler_params=_params(("parallel",)),
    )(x, gains)


def _proj_norm_bwd(name, x, dres, gains, branches):
    s, d = x.shape
    n = len(branches)
    tm = min(TM_PARTS, s)

    def body(x_ref, r_ref, g_ref, *refs):
        ab_refs, dx_ref, dg_ref = refs[:2 * n], refs[2 * n], refs[2 * n + 1]
        i = pl.program_id(0)
        xv = x_ref[...]
        r = _rms_rows(xv)
        xh = xv * r

        @pl.when(i == 0)
        def _():
            dg_ref[...] = jnp.zeros_like(dg_ref)

        a = None
        for j in range(n):
            a_ref, b_ref = ab_refs[2 * j], ab_refs[2 * j + 1]
            dn = None
            for part in range(a_ref.shape[0]):
                term = lax.dot_general(a_ref[part], b_ref[part], NT, preferred_element_type=F32)
                dn = term if dn is None else dn + term
            t = dn * g_ref[j:j + 1, :]
            a = t if a is None else a + t
            dg_ref[j:j + 1, :] += jnp.sum(dn * xh, axis=0, keepdims=True)
        dx_ref[...] = r_ref[...] + r * (a - xh * jnp.mean(xh * a, axis=-1, keepdims=True))

    row = pl.BlockSpec((tm, d), lambda i: (i, 0))
    small = pl.BlockSpec((n, d), lambda i: (0, 0))
    ab_specs, ab_args = [], []
    for a, b in branches:
        ab_specs += [pl.BlockSpec((a.shape[0], tm, a.shape[2]), lambda i: (0, i, 0)),
                     pl.BlockSpec(b.shape, lambda i: (0, 0, 0))]
        ab_args += [a, b]
    return pl.pallas_call(
        body, name=name, grid=(s // tm,),
        in_specs=[row, row, small] + ab_specs,
        out_specs=[row, small],
        out_shape=[jax.ShapeDtypeStruct((s, d), F32), jax.ShapeDtypeStruct((n, d), F32)],
        compiler_params=_params(("arbitrary",)),
    )(x, dres, gains, *ab_args)


def _out_norms(name, u, w_out, resid, gains):
    s, d = resid.shape
    n = gains.shape[0]
    tm = min(TM_DENSE, s)

    def body(u_ref, w_ref, r_ref, g_ref, h_ref, *o_refs):
        hv = r_ref[...] + jnp.dot(u_ref[...], w_ref[...], preferred_element_type=F32)
        h_ref[...] = hv
        hh = hv * _rms_rows(hv)
        for i in range(n):
            o_refs[i][...] = (hh * g_ref[i:i + 1, :]).astype(BF16)

    row = pl.BlockSpec((tm, d), lambda i: (i, 0))
    return pl.pallas_call(
        body, name=name, grid=(s // tm,),
        in_specs=[row, pl.BlockSpec((d, d), lambda i: (0, 0)), row, pl.BlockSpec((n, d), lambda i: (0, 0))],
        out_specs=[row] * (n + 1),
        out_shape=[jax.ShapeDtypeStruct((s, d), F32)] + [jax.ShapeDtypeStruct((s, d), BF16)] * n,
        compiler_params=_params(("parallel",)),
    )(u, w_out, resid, gains)


def _out_loss_head(u, w_out, resid, target, gain):
    s, d = resid.shape
    tm = min(TM_PARTS, s)

    def body(u_ref, w_ref, r_ref, t_ref, g_ref, dh_ref, loss_ref, dg_ref):
        i = pl.program_id(0)
        hv = r_ref[...] + jnp.dot(u_ref[...], w_ref[...], preferred_element_type=F32)
        r = _rms_rows(hv)
        hh = hv * r
        g = g_ref[...]
        err = hh * g - t_ref[...]
        part = 0.5 * jnp.sum(jnp.sum(err * err, axis=-1, keepdims=True) * (1.0 / d), axis=0, keepdims=True)
        dy = err * (1.0 / d)
        a = dy * g
        dh_ref[...] = r * (a - hh * jnp.mean(hh * a, axis=-1, keepdims=True))
        dg = jnp.sum(dy * hh, axis=0, keepdims=True)

        @pl.when(i == 0)
        def _():
            loss_ref[...] = part
            dg_ref[...] = dg

        @pl.when(i > 0)
        def _():
            loss_ref[...] += part
            dg_ref[...] += dg

    row = pl.BlockSpec((tm, d), lambda i: (i, 0))
    return pl.pallas_call(
        body, name="out_b_loss_head", grid=(s // tm,),
        in_specs=[row, pl.BlockSpec((d, d), lambda i: (0, 0)), row, row, pl.BlockSpec((1, d), lambda i: (0, 0))],
        out_specs=[row, pl.BlockSpec((1, 1), lambda i: (0, 0)), pl.BlockSpec((1, d), lambda i: (0, 0))],
        out_shape=[jax.ShapeDtypeStruct((s, d), F32), jax.ShapeDtypeStruct((1, 1), F32),
                   jax.ShapeDtypeStruct((1, d), F32)],
        compiler_params=_params(("arbitrary",)),
    )(u, w_out, resid, target, gain)


def _silu_parts(g):
    sig = jax.nn.sigmoid(g)
    return g * sig, sig * (1.0 + g * (1.0 - sig))


def _lane_lo(rows):
    return lax.broadcasted_iota(jnp.int32, (rows, LANES), 1) < HEAD_DIM


def _stack_pair(x):
    lo = _lane_lo(x.shape[0])
    zero = jnp.zeros_like(x)
    return jnp.concatenate([jnp.where(lo, x, zero), jnp.where(lo, zero, x)], axis=0)


def _unstack_pair(y, w):
    return jnp.where(_lane_lo(w), y[:w], y[w:])


def _block_valid(b, left_blocks, width):
    col = lax.broadcasted_iota(jnp.int32, (1, 2 * width), 1)
    col = jnp.where(col >= width, col - width, col)
    return (col // KB + (b - left_blocks)) >= 0


def _toeplitz_tile(diag_row, width, left_chunks):
    wide = width + TQ
    rolled = pltpu.roll(jnp.broadcast_to(diag_row, (TQ, wide)), 1, 1, stride=1, stride_axis=0)
    i = lax.broadcasted_iota(jnp.int32, (TQ, width), 0) // CHUNK
    j = lax.broadcasted_iota(jnp.int32, (TQ, width), 1) // CHUNK
    dc = i + left_chunks - j
    return jnp.where((dc >= 0) & (dc <= left_chunks), rolled[:, TQ:], MASKED)


def _toeplitz_sum(tile, width):
    flip = (lax.broadcasted_iota(jnp.int32, (TQ, TQ), 0) + lax.broadcasted_iota(jnp.int32, (TQ, TQ), 1)
            == TQ - 1).astype(F32)
    reversed_rows = jnp.dot(flip, tile, precision=lax.Precision.HIGHEST, preferred_element_type=F32)
    padded = jnp.concatenate([reversed_rows, jnp.zeros((TQ, TQ), F32)], axis=1)
    rolled = pltpu.roll(padded, 0, 1, stride=1, stride_axis=0)
    return jnp.sum(rolled, axis=0, keepdims=True)


def _softmax_pair(sc, w, sink=None):
    ps, inv, lses = [], [], []
    for e in range(2):
        sh = sc[:, e * w:(e + 1) * w]
        m = jnp.max(sh, axis=-1, keepdims=True)
        if sink is not None:
            m = jnp.maximum(m, sink[e])
        ex = jnp.exp(sh - m)
        l = jnp.sum(ex, axis=-1, keepdims=True)
        if sink is not None:
            l = l + jnp.exp(sink[e] - m)
        ps.append(ex.astype(BF16))
        inv.append(1.0 / l)
        lses.append(m + jnp.log(l))
    return jnp.concatenate(ps, axis=-1), inv, lses


def _softmax_pair_bwd(sc, dp, lse, delta, w):
    ps, dss = [], []
    for e in range(2):
        p = jnp.exp(sc[:, e * w:(e + 1) * w] - lse[e])
        ps.append(p)
        dss.append(p * (dp[:, e * w:(e + 1) * w] - delta[e]))
    return jnp.concatenate(ps, axis=-1), jnp.concatenate(dss, axis=-1)


def _pair_rowsums(x, lo):
    zero = jnp.zeros_like(x)
    return (jnp.sum(jnp.where(lo, x, zero), axis=-1, keepdims=True),
            jnp.sum(jnp.where(lo, zero, x), axis=-1, keepdims=True))


def _a_qkv_specs(rows, pad, pw):
    return [pl.BlockSpec((None, TQ, pw), lambda p, b: (0, b + pad // TQ, p)),
            pl.BlockSpec((None, rows, pw), lambda p, b: (1, 0, p)),
            pl.BlockSpec((None, rows, pw), lambda p, b: (2, 0, p))]


def _window(ref, b, pad, win, lanes):
    start = pl.multiple_of(b * TQ + pad - (win - TQ), KB)
    return ref[pl.ds(start, win), lanes]


def _attn_a_fwd(zqkv, g, diag, hosted=None):
    s = g.shape[0]
    pad = zqkv.shape[1] - s
    nb = s // TQ
    left = A_KBLOCKS - 1
    pairs = A_PAIRS_FWD
    pw = pairs * LANES
    wide = A_WIN + TQ

    def body(q_ref, k_ref, v_ref, g_ref, diag_ref, o_ref, u_ref, lse_ref, bias_scr):
        b = pl.program_id(1)

        @pl.when(b == 0)
        def _():
            for hh in range(2 * pairs):
                bias_scr[hh // 2, :, (hh % 2) * A_WIN:(hh % 2 + 1) * A_WIN] = _toeplitz_tile(
                    diag_ref[hh], A_WIN, A_LEFT_CHUNKS)

        def step(first_blocks):
            lo = _lane_lo(TQ)
            for pp in range(pairs):
                ln = slice(pp * LANES, (pp + 1) * LANES)
                kcat = _stack_pair(_window(k_ref, b, pad, A_WIN, ln))
                vcat = _stack_pair(_window(v_ref, b, pad, A_WIN, ln))
                sc = lax.dot_general(q_ref[:, ln] * SCALE, kcat, NT, preferred_element_type=F32) + bias_scr[pp]
                if first_blocks:
                    sc = jnp.where(_block_valid(b, left, A_WIN), sc, MASKED)
                p, inv, lses = _softmax_pair(sc, A_WIN)
                ov = jnp.dot(p, vcat, preferred_element_type=F32) * jnp.where(lo, inv[0], inv[1])
                o_ref[:, ln] = ov
                lse_ref[pp] = jnp.where(lo, lses[0], lses[1])
                sg, _ = _silu_parts(g_ref[:, ln])
                u_ref[:, ln] = (ov * sg).astype(BF16)

        @pl.when(b < left)
        def _():
            step(True)

        @pl.when(b >= left)
        def _():
            step(False)

    tile = pl.BlockSpec((TQ, pw), lambda p, b: (b, p))
    return _call(
        body, name="attn_a_fwd", grid=(HEADS // 2 // pairs, nb),
        in_specs=_a_qkv_specs(pad + s, pad, pw) + [
            tile, pl.BlockSpec((2 * pairs, 1, wide), lambda p, b: (p, 0, 0))],
        out_specs=[tile, tile, pl.BlockSpec((pairs, TQ, LANES), lambda p, b: (p, b, 0))],
        out_shape=[jax.ShapeDtypeStruct((s, D_MODEL), F32), jax.ShapeDtypeStruct((s, D_MODEL), BF16),
                   jax.ShapeDtypeStruct((HEADS // 2, s, LANES), F32)],
        scratch_shapes=[pltpu.VMEM((pairs, TQ, 2 * A_WIN), F32)],
        sem=("parallel", "arbitrary"), hosted=hosted,
        args=(zqkv, zqkv, zqkv, g, diag))


def _attn_a_bwd(zqkv, g, o, du, lse, diag, hosted=None):
    s = g.shape[0]
    pad = zqkv.shape[1] - s
    nb = s // TQ
    left = A_KBLOCKS - 1
    pw = A_PAIRS * LANES
    wide = A_WIN + TQ

    def body(q_ref, k_ref, v_ref, g_ref, o_ref, du_ref, lse_ref, diag_ref, dz_ref, ddiag_ref,
             bias_scr, dbias_acc, dk_acc, dv_acc):
        b = pl.program_id(1)

        @pl.when(b == 0)
        def _():
            for hh in range(2 * A_PAIRS):
                bias_scr[hh // 2, :, (hh % 2) * A_WIN:(hh % 2 + 1) * A_WIN] = _toeplitz_tile(
                    diag_ref[hh], A_WIN, A_LEFT_CHUNKS)
            dbias_acc[...] = jnp.zeros_like(dbias_acc)
            dk_acc[...] = jnp.zeros_like(dk_acc)
            dv_acc[...] = jnp.zeros_like(dv_acc)

        def step(first_blocks):
            lo = _lane_lo(TQ)
            upper = lax.broadcasted_iota(jnp.int32, (LANES, A_WIN), 0) < HEAD_DIM
            rows = pl.ds(pl.multiple_of(b * TQ, TQ), TQ)
            sg, dsg = _silu_parts(g_ref[...])
            duv = du_ref[...]
            ov = o_ref[...]
            do = duv * sg
            dz_ref[3, rows, :] = (duv * ov * dsg).astype(BF16)
            do_o = do * ov
            do_bf = do.astype(BF16)
            for pp in range(A_PAIRS):
                ln = slice(pp * LANES, (pp + 1) * LANES)
                q = q_ref[:, ln] * SCALE
                kcat = _stack_pair(_window(k_ref, b, pad, A_WIN, ln))
                vcat = _stack_pair(_window(v_ref, b, pad, A_WIN, ln))
                sc = lax.dot_general(q, kcat, NT, preferred_element_type=F32) + bias_scr[pp]
                if first_blocks:
                    sc = jnp.where(_block_valid(b, left, A_WIN), sc, MASKED)
                lse_t = lse_ref[pp]
                dp = lax.dot_general(do_bf[:, ln], vcat, NT, preferred_element_type=F32)
                p, ds = _softmax_pair_bwd(sc, dp, (lse_t[:, 0:1], lse_t[:, HEAD_DIM:HEAD_DIM + 1]),
                                          _pair_rowsums(do_o[:, ln], lo), A_WIN)
                dbias_acc[pp] += ds
                dsb = ds.astype(BF16)
                dz_ref[0, rows, ln] = (jnp.dot(dsb, kcat, preferred_element_type=F32) * SCALE).astype(BF16)
                dkt = lax.dot_general(q, dsb, TN, preferred_element_type=F32)
                dvt = lax.dot_general(do_bf[:, ln], p.astype(BF16), TN, preferred_element_type=F32)
                dkt = jnp.where(upper, dkt[:, :A_WIN], dkt[:, A_WIN:])
                dvt = jnp.where(upper, dvt[:, :A_WIN], dvt[:, A_WIN:])
                for t in range(A_KBLOCKS):
                    blk = b + (pad // KB - left + t)
                    dk_acc[blk, ln, :] += dkt[:, t * KB:(t + 1) * KB]
                    dv_acc[blk, ln, :] += dvt[:, t * KB:(t + 1) * KB]

        @pl.when(b < left)
        def _():
            step(True)

        @pl.when(b >= left)
        def _():
            step(False)

        @pl.when(b == nb - 1)
        def _():
            for kb in range(s // KB):
                dz_ref[1, kb * KB:(kb + 1) * KB, :] = dk_acc[pad // KB + kb].T.astype(BF16)
                dz_ref[2, kb * KB:(kb + 1) * KB, :] = dv_acc[pad // KB + kb].T.astype(BF16)
            for hh in range(2 * A_PAIRS):
                ddiag_ref[hh] = _toeplitz_sum(
                    dbias_acc[hh // 2, :, (hh % 2) * A_WIN:(hh % 2 + 1) * A_WIN], A_WIN)

    tile = pl.BlockSpec((TQ, pw), lambda p, b: (b, p))
    diag_spec = pl.BlockSpec((2 * A_PAIRS, 1, wide), lambda p, b: (p, 0, 0))
    return _call(
        body, name="attn_a_bwd", grid=(HEADS // 2 // A_PAIRS, nb),
        in_specs=_a_qkv_specs(pad + s, pad, pw) + [
            tile, tile, tile, pl.BlockSpec((A_PAIRS, TQ, LANES), lambda p, b: (p, b, 0)), diag_spec],
        out_specs=[pl.BlockSpec((4, s, pw), lambda p, b: (0, 0, p)), diag_spec],
        out_shape=[jax.ShapeDtypeStruct((4, s, D_MODEL), BF16),
                   jax.ShapeDtypeStruct((HEADS, 1, wide), F32)],
        scratch_shapes=[pltpu.VMEM((A_PAIRS, TQ, 2 * A_WIN), F32), pltpu.VMEM((A_PAIRS, TQ, 2 * A_WIN), F32),
                        pltpu.VMEM(((pad + s) // KB, pw, KB), F32), pltpu.VMEM(((pad + s) // KB, pw, KB), F32)],
        sem=("parallel", "arbitrary"), hosted=hosted,
        args=(zqkv, zqkv, zqkv, g, o, du, lse, diag))


B_STACK = B_GROUP // 2
B_KVX = 4 * LANES
B_ROWS = B_STACK * TQ
B_WIDE = B_WIN + TQ


def _b_head_place(h):
    return h // B_GROUP, (h % B_GROUP) // 2, h % 2


def _toeplitz_tile_t(base_row, width, left_chunks):
    wide = width + TQ
    rolled = pltpu.roll(jnp.broadcast_to(base_row, (width, wide)), 0, 1, stride=1, stride_axis=0)
    j = lax.broadcasted_iota(jnp.int32, (width, TQ), 0) // CHUNK
    i = lax.broadcasted_iota(jnp.int32, (width, TQ), 1) // CHUNK
    dc = i + left_chunks - j
    return jnp.where((dc >= 0) & (dc <= left_chunks), rolled[:, :TQ], MASKED)


def _toeplitz_sum_t(tile_t, width):
    flip = (lax.broadcasted_iota(jnp.int32, (width, width), 0) + lax.broadcasted_iota(jnp.int32, (width, width), 1)
            == width - 1).astype(F32)
    reversed_rows = jnp.dot(flip, tile_t, precision=lax.Precision.HIGHEST, preferred_element_type=F32)
    padded = jnp.concatenate([reversed_rows, jnp.zeros((width, width), F32)], axis=1)
    rolled = pltpu.roll(padded, 0, 1, stride=1, stride_axis=0)
    return jnp.sum(rolled, axis=0, keepdims=True)


def _b_build_bias(base_ref, bias_scr):
    for h in range(HEADS):
        gi, pr, e = _b_head_place(h)
        bias_scr[gi, e * B_WIN:(e + 1) * B_WIN, pr * TQ:(pr + 1) * TQ] = _toeplitz_tile_t(
            base_ref[h], B_WIN, B_LEFT_CHUNKS)


def _b_stack(x, gi):
    return jnp.concatenate(
        [x[:, (B_STACK * gi + pr) * LANES:(B_STACK * gi + pr + 1) * LANES] for pr in range(B_STACK)], axis=0)


def _b_sink_rows(sink_ref, gi):
    block = lax.broadcasted_iota(jnp.int32, (1, B_ROWS), 1) // TQ
    rows = []
    for e in range(2):
        row = jnp.zeros((1, B_ROWS), F32)
        for pr in range(B_STACK):
            h = B_GROUP * gi + 2 * pr + e
            row = jnp.where(block == pr, sink_ref[0:1, h:h + 1], row)
        rows.append(row)
    return rows


def _b_scores_t(q_ref, kvv, bias_scr, gi, b, left, first_blocks):
    kcat = _stack_pair(kvv[:, gi * LANES:(gi + 1) * LANES])
    vcat = _stack_pair(kvv[:, (B_KV_HEADS + gi) * LANES:(B_KV_HEADS + gi + 1) * LANES])
    qs = _b_stack(q_ref, gi) * SCALE
    sc = lax.dot_general(kcat, qs, NT, preferred_element_type=F32) + bias_scr[gi]
    if first_blocks:
        row = lax.broadcasted_iota(jnp.int32, (2 * B_WIN, 1), 0)
        row = jnp.where(row >= B_WIN, row - B_WIN, row)
        sc = jnp.where((row // KB + (b - left)) >= 0, sc, MASKED)
    return kcat, vcat, qs, sc


def _attn_b_fwd(qb, kvx, gate, base, sinks):
    s = qb.shape[0]
    pad = kvx.shape[0] - s
    nb = s // TQ
    left = B_KBLOCKS - 1

    def body(q_ref, kv_ref, g_ref, base_ref, sink_ref, o_ref, u_ref, lse_ref, bias_scr):
        b = pl.program_id(0)

        @pl.when(b == 0)
        def _():
            _b_build_bias(base_ref, bias_scr)

        def step(first_blocks):
            kvv = _window(kv_ref, b, pad, B_WIN, slice(None))
            upper = lax.broadcasted_iota(jnp.int32, (LANES, B_ROWS), 0) < HEAD_DIM
            lse_rows = []
            for gi in range(B_KV_HEADS):
                kcat, vcat, qs, sc = _b_scores_t(q_ref, kvv, bias_scr, gi, b, left, first_blocks)
                sink = _b_sink_rows(sink_ref, gi)
                ps, inv = [], []
                for e in range(2):
                    sh = sc[e * B_WIN:(e + 1) * B_WIN]
                    m = jnp.maximum(jnp.max(sh, axis=0, keepdims=True), sink[e])
                    ex = jnp.exp(sh - m)
                    l = jnp.sum(ex, axis=0, keepdims=True) + jnp.exp(sink[e] - m)
                    ps.append(ex.astype(BF16))
                    inv.append(1.0 / l)
                    lse_rows.append(m + jnp.log(l))
                pt = jnp.concatenate(ps, axis=0)
                ot = lax.dot_general(vcat, pt, TN, preferred_element_type=F32) * jnp.where(upper, inv[0], inv[1])
                ov = ot.T
                for pr in range(B_STACK):
                    pair = B_STACK * gi + pr
                    o_ref[:, pair * LANES:(pair + 1) * LANES] = ov[pr * TQ:(pr + 1) * TQ]
            lse_ref[0] = jnp.concatenate(lse_rows + [jnp.zeros((8 - len(lse_rows), B_ROWS), F32)], axis=0)
            sg, _ = _silu_parts(g_ref[...])
            u_ref[...] = (o_ref[...] * sg).astype(BF16)

        @pl.when(b < left)
        def _():
            step(True)

        @pl.when(b >= left)
        def _():
            step(False)

    row = pl.BlockSpec((TQ, D_MODEL), lambda b: (b, 0))
    return pl.pallas_call(
        body, name="attn_b_fwd", grid=(nb,),
        in_specs=[row, pl.BlockSpec((pad + s, B_KVX), lambda b: (0, 0)), row,
                  pl.BlockSpec((HEADS, 1, B_WIDE), lambda b: (0, 0, 0)), pl.BlockSpec((1, HEADS), lambda b: (0, 0))],
        out_specs=[row, row, pl.BlockSpec((1, 8, B_ROWS), lambda b: (b, 0, 0))],
        out_shape=[jax.ShapeDtypeStruct((s, D_MODEL), F32), jax.ShapeDtypeStruct((s, D_MODEL), BF16),
                   jax.ShapeDtypeStruct((nb, 8, B_ROWS), F32)],
        scratch_shapes=[pltpu.VMEM((B_KV_HEADS, 2 * B_WIN, B_ROWS), F32)],
        compiler_params=_params(("arbitrary",)),
    )(qb, kvx, gate, base, sinks)


def _attn_b_bwd(qb, kvx, gate, o, du, lse, base, sinks):
    s = qb.shape[0]
    pad = kvx.shape[0] - s
    nb = s // TQ
    left = B_KBLOCKS - 1
    half = D_MODEL // 2

    def body(q_ref, kv_ref, g_ref, o_ref, du_ref, lse_ref, base_ref, sink_ref, dz_ref, dkv_ref, dsum_ref,
             dsink_ref, bias_scr, dbias_acc, dkv_acc, dsink_acc):
        b = pl.program_id(0)

        @pl.when(b == 0)
        def _():
            _b_build_bias(base_ref, bias_scr)
            dbias_acc[...] = jnp.zeros_like(dbias_acc)
            dkv_acc[...] = jnp.zeros_like(dkv_acc)
            dsink_acc[...] = jnp.zeros_like(dsink_acc)

        def step(first_blocks):
            kvv = _window(kv_ref, b, pad, B_WIN, slice(None))
            sg, dsg = _silu_parts(g_ref[...])
            duv = du_ref[...]
            ov = o_ref[...]
            do = duv * sg
            dgate = (duv * ov * dsg).astype(BF16)
            dz_ref[2] = dgate[:, :half]
            dz_ref[3] = dgate[:, half:]
            do_o = do * ov
            do_bf = do.astype(BF16)
            lse_all = lse_ref[0]
            dsink_rows = []
            for gi in range(B_KV_HEADS):
                kcat, vcat, qs, sc = _b_scores_t(q_ref, kvv, bias_scr, gi, b, left, first_blocks)
                dos = _b_stack(do_bf, gi)
                doo_t = _b_stack(do_o, gi).T
                delta = (jnp.sum(doo_t[:HEAD_DIM], axis=0, keepdims=True),
                         jnp.sum(doo_t[HEAD_DIM:], axis=0, keepdims=True))
                sink = _b_sink_rows(sink_ref, gi)
                dp = lax.dot_general(vcat, dos, NT, preferred_element_type=F32)
                ps, dss = [], []
                for e in range(2):
                    lse_e = lse_all[2 * gi + e:2 * gi + e + 1]
                    delta_e = delta[e]
                    p = jnp.exp(sc[e * B_WIN:(e + 1) * B_WIN] - lse_e)
                    ps.append(p.astype(BF16))
                    dss.append(p * (dp[e * B_WIN:(e + 1) * B_WIN] - delta_e))
                    dsink_rows.append(-jnp.exp(sink[e] - lse_e) * delta_e)
                ds = jnp.concatenate(dss, axis=0)
                dbias_acc[gi] += ds
                dsb = ds.astype(BF16)
                dq = (lax.dot_general(kcat, dsb, TN, preferred_element_type=F32) * SCALE).T.astype(BF16)
                for pr in range(B_STACK):
                    dz_ref[gi, :, pr * LANES:(pr + 1) * LANES] = dq[pr * TQ:(pr + 1) * TQ]
                dk = _unstack_pair(jnp.dot(dsb, qs, preferred_element_type=F32), B_WIN)
                dv = _unstack_pair(jnp.dot(jnp.concatenate(ps, axis=0), dos, preferred_element_type=F32), B_WIN)
                krows = pl.ds(pl.multiple_of(b * TQ + pad - (B_WIN - TQ), KB), B_WIN)
                dkv_acc[krows, gi * LANES:(gi + 1) * LANES] += dk
                dkv_acc[krows, (B_KV_HEADS + gi) * LANES:(B_KV_HEADS + gi + 1) * LANES] += dv
            dsink_acc[...] += jnp.concatenate(
                dsink_rows + [jnp.zeros((8 - len(dsink_rows), B_ROWS), F32)], axis=0)

        @pl.when(b < left)
        def _():
            step(True)

        @pl.when(b >= left)
        def _():
            step(False)

        @pl.when(b == nb - 1)
        def _():
            lo_s = _lane_lo(s)
            for which in range(2):
                folded = []
                for gi in range(B_KV_HEADS):
                    part = dkv_acc[pad:pad + s, (which * B_KV_HEADS + gi) * LANES:(which * B_KV_HEADS + gi + 1) * LANES]
                    folded.append(part + pltpu.roll(part, HEAD_DIM, 1))
                dkv_ref[:, which * LANES:(which + 1) * LANES] = jnp.where(lo_s, folded[0], folded[1]).astype(BF16)
            lane8 = lax.broadcasted_iota(jnp.int32, dsink_ref.shape, 1)
            tot = jnp.zeros(dsink_ref.shape, F32)
            for h in range(HEADS):
                gi, pr, e = _b_head_place(h)
                dsum_ref[h] = _toeplitz_sum_t(
                    dbias_acc[gi, e * B_WIN:(e + 1) * B_WIN, pr * TQ:(pr + 1) * TQ], B_WIN)
                per_query = dsink_acc[2 * gi + e:2 * gi + e + 1, pr * TQ:(pr + 1) * TQ]
                tot = jnp.where(lane8 == h, jnp.sum(per_query, axis=1, keepdims=True), tot)
            dsink_ref[...] = tot

    row = pl.BlockSpec((TQ, D_MODEL), lambda b: (b, 0))
    base_spec = pl.BlockSpec((HEADS, 1, B_WIDE), lambda b: (0, 0, 0))
    return pl.pallas_call(
        body, name="attn_b_bwd", grid=(nb,),
        in_specs=[row, pl.BlockSpec((pad + s, B_KVX), lambda b: (0, 0)), row, row, row,
                  pl.BlockSpec((1, 8, B_ROWS), lambda b: (b, 0, 0)), base_spec,
                  pl.BlockSpec((1, HEADS), lambda b: (0, 0))],
        out_specs=[pl.BlockSpec((4, TQ, half), lambda b: (0, b, 0)),
                   pl.BlockSpec((s, 2 * LANES), lambda b: (0, 0)), base_spec,
                   pl.BlockSpec((8, LANES), lambda b: (0, 0))],
        out_shape=[jax.ShapeDtypeStruct((4, s, half), BF16), jax.ShapeDtypeStruct((s, 2 * LANES), BF16),
                   jax.ShapeDtypeStruct((HEADS, 1, B_WIDE), F32), jax.ShapeDtypeStruct((8, LANES), F32)],
        scratch_shapes=[pltpu.VMEM((B_KV_HEADS, 2 * B_WIN, B_ROWS), F32),
                        pltpu.VMEM((B_KV_HEADS, 2 * B_WIN, B_ROWS), F32),
                        pltpu.VMEM((pad + s, B_KVX), F32), pltpu.VMEM((8, B_ROWS), F32)],
        compiler_params=_params(("arbitrary",)),
    )(qb, kvx, gate, o, du, lse, base, sinks)


def _t5_bucket(rel):
    nb = T5_BUCKETS // 2
    max_exact = nb // 2
    ret = jnp.where(rel > 0, nb, 0)
    n = jnp.abs(rel)
    nf = jnp.maximum(n, 1).astype(jnp.float32)
    large = max_exact + (jnp.log(nf / max_exact) / math.log(T5_MAX_DIST / max_exact)
                         * (nb - max_exact)).astype(jnp.int32)
    large = jnp.minimum(large, nb - 1)
    return ret + jnp.where(n < max_exact, n, large)


def _a_offset_onehot():
    c = np.arange(A_WIN + TQ)
    dist = A_LEFT_CHUNKS * CHUNK + TQ - 1 - c
    idx = np.clip(dist, -A_REL_CLIP, A_REL_CLIP) + A_REL_CLIP
    onehot = np.zeros((A_WIN + TQ, 2 * A_REL_CLIP + 1), np.float32)
    onehot[c, idx] = 1.0
    return jnp.asarray(onehot)


def _b_offset_onehot():
    c = jnp.arange(B_WIN + TQ, dtype=jnp.int32)
    rel = c - (TQ - 1) - B_LEFT_CHUNKS * CHUNK
    return (_t5_bucket(rel)[:, None] == jnp.arange(T5_BUCKETS)[None, :]).astype(F32)


def _diag_rows(onehot, table):
    rows = jnp.dot(onehot, table.astype(F32), precision=lax.Precision.HIGHEST)
    return rows.T.reshape(HEADS, 1, onehot.shape[0])


def _diag_rows_grad(onehot, ddiag):
    return jnp.dot(ddiag.reshape(HEADS, onehot.shape[0]), onehot, precision=lax.Precision.HIGHEST).T


def _position():
    x, y, c = lax.axis_index("x"), lax.axis_index("y"), lax.axis_index("c")
    chips = [(1 - x, y), (x, 1 - y), (1 - x, 1 - y)]
    return x, y, c, chips


ANY = pl.BlockSpec(memory_space=pl.ANY)


def _allgather_hosted(shards, split):
    n = len(shards)

    def part(ref, t, half):
        if not split[t]:
            return ref
        rows = shards[t].shape[0] // 2
        return ref.at[pl.ds(half * rows, rows)]

    def copies(kind, ins, outs, sems):
        send_sems, recv_sems, pass_send, pass_recv, local_sems = sems
        x, y, c, chips = _position()
        mine = 2 * x + y
        if kind == "local":
            return [pltpu.make_async_copy(ins[t], outs[t].at[mine], local_sems.at[t]) for t in range(n)]
        made = []
        for t in range(n):
            for j, chip in enumerate(chips):
                theirs = 2 * chip[0] + chip[1]
                far = dict(send_sem=send_sems.at[3 * t + j], recv_sem=recv_sems.at[3 * t + j],
                           device_id=(chip[0], chip[1], c), device_id_type=MESH)
                near = dict(send_sem=pass_send.at[3 * t + j], recv_sem=pass_recv.at[3 * t + j],
                            device_id=(x, y, 1 - c), device_id_type=MESH)
                here = part(outs[t].at[theirs], t, c)
                if kind == "send":
                    made.append(pltpu.make_async_remote_copy(
                        src_ref=part(ins[t], t, c), dst_ref=part(outs[t].at[mine], t, c), **far))
                elif kind == "landed":
                    made.append(pltpu.make_async_remote_copy(src_ref=here, dst_ref=here, **far))
                elif not split[t]:
                    made.append(None)
                elif kind == "pass":
                    made.append(pltpu.make_async_remote_copy(src_ref=here, dst_ref=here, **near))
                else:
                    other = part(outs[t].at[theirs], t, 1 - c)
                    made.append(pltpu.make_async_remote_copy(src_ref=other, dst_ref=other, **near))
        return made

    def first(ins, outs, sems):
        for cp in copies("local", ins, outs, sems) + copies("send", ins, outs, sems):
            cp.start()

    def middle(ins, outs, sems):
        for got, cp in zip(copies("landed", ins, outs, sems), copies("pass", ins, outs, sems)):
            got.wait_recv()
            if cp is not None:
                cp.start()

    def last(ins, outs, sems):
        for cp in copies("passed", ins, outs, sems):
            if cp is not None:
                cp.wait_recv()
        for cp in copies("send", ins, outs, sems) + copies("pass", ins, outs, sems):
            if cp is not None:
                cp.wait_send()
        for cp in copies("local", ins, outs, sems):
            cp.wait()

    return _Hosted(shards, [jax.ShapeDtypeStruct((4,) + w.shape, w.dtype) for w in shards],
                   [pltpu.SemaphoreType.DMA((3 * n,))] * 4 + [pltpu.SemaphoreType.DMA((n,))],
                   first, middle, last)


def _allgather_routed(shards):
    n = len(shards)

    def piece(block_ref, t, c, quarter=None):
        half = shards[t].shape[0] // 2
        if quarter is None:
            return block_ref.at[pl.ds(c * half, half)]
        return block_ref.at[pl.ds(c * half + quarter * (half // 2), half // 2)]

    def copies(kind, ins, outs, sems):
        ici_send, ici_recv, pass_send, pass_recv, local_sems = sems
        x, y, c, chips = _position()
        mine = 2 * x + y
        if kind == "local":
            return [pltpu.make_async_copy(ins[t], outs[t].at[mine], local_sems.at[t]) for t in range(n)]
        ids = [2 * chip[0] + chip[1] for chip in chips]
        made = []
        for t in range(n):
            def ici(k, to):
                return dict(send_sem=ici_send.at[4 * t + k], recv_sem=ici_recv.at[4 * t + k],
                            device_id=(chips[to][0], chips[to][1], c), device_id_type=MESH)

            def d2d(k):
                return dict(send_sem=pass_send.at[4 * t + k], recv_sem=pass_recv.at[4 * t + k],
                            device_id=(x, y, 1 - c), device_id_type=MESH)

            def same(ref, where):
                return pltpu.make_async_remote_copy(src_ref=ref, dst_ref=ref, **where)

            if kind == "send":
                for k in range(2):
                    made.append(pltpu.make_async_remote_copy(
                        src_ref=piece(ins[t], t, c), dst_ref=piece(outs[t].at[mine], t, c), **ici(k, k)))
            elif kind == "landed":
                made += [same(piece(outs[t].at[ids[k]], t, c), ici(k, k)) for k in range(2)]
            elif kind == "forward":
                made.append(same(piece(outs[t].at[ids[0]], t, c, 0), ici(2, 1)))
                made.append(same(piece(outs[t].at[ids[1]], t, c, 1), ici(3, 0)))
            elif kind == "arrived":
                made.append(same(piece(outs[t].at[ids[2]], t, c, 0), ici(2, 1)))
                made.append(same(piece(outs[t].at[ids[2]], t, c, 1), ici(3, 0)))
            else:
                core = 1 - c if kind == "passed" else c
                if kind in ("pass halves", "passed"):
                    made += [same(piece(outs[t].at[ids[k]], t, core), d2d(k)) for k in range(2)]
                if kind in ("pass quarters", "passed"):
                    made += [same(piece(outs[t].at[ids[2]], t, core, k), d2d(2 + k)) for k in range(2)]
        return made

    def first(ins, outs, sems):
        for cp in copies("local", ins, outs, sems) + copies("send", ins, outs, sems):
            cp.start()

    def middle(ins, outs, sems):
        for got, onward, near in zip(copies("landed", ins, outs, sems), copies("forward", ins, outs, sems),
                                     copies("pass halves", ins, outs, sems)):
            got.wait_recv()
            near.start()
            onward.start()

    def last(ins, outs, sems):
        quarters = copies("pass quarters", ins, outs, sems)
        for got, near in zip(copies("arrived", ins, outs, sems), quarters):
            got.wait_recv()
            near.start()
        for cp in copies("passed", ins, outs, sems):
            cp.wait_recv()
        for cp in (copies("send", ins, outs, sems) + copies("forward", ins, outs, sems)
                   + copies("pass halves", ins, outs, sems) + quarters):
            cp.wait_send()
        for cp in copies("local", ins, outs, sems):
            cp.wait()

    return _Hosted(shards, [jax.ShapeDtypeStruct((4,) + w.shape, w.dtype) for w in shards],
                   [pltpu.SemaphoreType.DMA((4 * n,))] * 4 + [pltpu.SemaphoreType.DMA((n,))],
                   first, middle, last)


def _scatter_hosted(grads):
    n = len(grads)

    def copies(ins, outs, sems):
        send_sems, recv_sems = sems
        x, y, c, chips = _position()
        return [pltpu.make_async_remote_copy(
            src_ref=ins[t].at[2 * chip[0] + chip[1]], dst_ref=outs[t].at[j],
            send_sem=send_sems.at[3 * t + j], recv_sem=recv_sems.at[3 * t + j],
            device_id=(chip[0], chip[1], c), device_id_type=MESH)
            for t in range(n) for j, chip in enumerate(chips)]

    def first(ins, outs, sems):
        for cp in copies(ins, outs, sems):
            cp.start()

    def last(ins, outs, sems):
        for cp in copies(ins, outs, sems):
            cp.wait()

    return _Hosted(grads, [jax.ShapeDtypeStruct((3,) + g.shape[1:], g.dtype) for g in grads],
                   [pltpu.SemaphoreType.DMA((3 * n,))] * 2, first, None, last)


GATHER_PEERS = "x and y neighbours (same core) and the sibling core"
SCATTER_PEERS = "the same core of the three other chips"
EVERYONE = "the seven other devices"


def _run_on_sequencer(name, hosted, peers, collective_id):
    ins = [jax.new_ref(a, memory_space=pltpu.MemorySpace.HBM) for a in hosted.inputs]
    outs = [jax.empty_ref(shape, memory_space=pltpu.MemorySpace.HBM) for shape in hosted.out_shapes]

    @pl.kernel(mesh=plsc.ScalarSubcoreMesh(axis_name="sequencer", num_cores=1), name=name,
               scratch_types=tuple(hosted.sems), compiler_params=pltpu.CompilerParams(collective_id=collective_id))
    def launch(*sems):
        x, y, c, chips = _position()
        if peers == GATHER_PEERS:
            devices = [(chip[0], chip[1], c) for chip in chips[:2]] + [(x, y, 1 - c)]
        elif peers == SCATTER_PEERS:
            devices = [(chip[0], chip[1], c) for chip in chips]
        else:
            devices = [(x ^ (k >> 2), y ^ ((k >> 1) & 1), c ^ (k & 1)) for k in range(1, 8)]
        barrier = pltpu.get_barrier_semaphore()
        for device in devices:
            pl.semaphore_signal(barrier, inc=1, device_id=device, device_id_type=MESH)
        pl.semaphore_wait(barrier, len(devices))
        hosted.first(ins, outs, sems)
        if hosted.middle is not None:
            hosted.middle(ins, outs, sems)
        hosted.last(ins, outs, sems)

    launch()
    return [o[...] for o in outs]


def _run_alone(name, hosted):
    n_in = len(hosted.inputs)
    n_out = len(hosted.out_shapes)

    def body(*refs):
        ins, outs, sems = refs[:n_in], refs[n_in:n_in + n_out], refs[n_in + n_out:]
        hosted.first(ins, outs, sems)
        if hosted.middle is not None:
            hosted.middle(ins, outs, sems)
        hosted.last(ins, outs, sems)

    return pl.pallas_call(
        body, name=name, in_specs=[ANY] * n_in, out_specs=[ANY] * n_out, out_shape=hosted.out_shapes,
        scratch_shapes=hosted.sems)(*hosted.inputs)


def _swap_with_sibling(name, blocks):
    n = len(blocks)

    def body(*refs):
        ins, outs = refs[:n], refs[n:2 * n]
        send_sems, recv_sems = refs[2 * n:]
        x, y, c, _ = _position()
        sends = [pltpu.make_async_remote_copy(
            src_ref=ins[t], dst_ref=outs[t], send_sem=send_sems.at[t], recv_sem=recv_sems.at[t],
            device_id=(x, y, 1 - c), device_id_type=MESH) for t in range(n)]
        for cp in sends:
            cp.start()
        for cp in sends:
            cp.wait()

    return pl.pallas_call(
        body, name=name,
        in_specs=[ANY] * n, out_specs=[ANY] * n,
        out_shape=[jax.ShapeDtypeStruct(b.shape, b.dtype) for b in blocks],
        scratch_shapes=[pltpu.SemaphoreType.DMA((n,))] * 2,
    )(*blocks)


def _everyone_hosted(terms):
    nt = len(terms)

    def copies(kind, ins, outs, sems):
        send_sems, recv_sems, local_sems = sems
        x, y, c, _ = _position()
        me = 4 * x + 2 * y + c
        if kind == "local":
            return [pltpu.make_async_copy(ins[t], outs[t].at[me], local_sems.at[t]) for t in range(nt)]
        made = []
        for t in range(nt):
            for k in range(1, 8):
                peer = (x ^ (k >> 2), y ^ ((k >> 1) & 1), c ^ (k & 1))
                slot = me if kind == "send" else me ^ k
                made.append(pltpu.make_async_remote_copy(
                    src_ref=ins[t], dst_ref=outs[t].at[slot], send_sem=send_sems.at[7 * t + k - 1],
                    recv_sem=recv_sems.at[7 * t + k - 1], device_id=peer, device_id_type=MESH))
        return made

    def first(ins, outs, sems):
        for cp in copies("local", ins, outs, sems) + copies("send", ins, outs, sems):
            cp.start()

    def last(ins, outs, sems):
        for cp in copies("landed", ins, outs, sems):
            cp.wait_recv()
        for cp in copies("send", ins, outs, sems):
            cp.wait_send()
        for cp in copies("local", ins, outs, sems):
            cp.wait()

    return _Hosted(terms, [jax.ShapeDtypeStruct((8,) + a.shape, F32) for a in terms],
                   [pltpu.SemaphoreType.DMA((7 * nt,))] * 2 + [pltpu.SemaphoreType.DMA((nt,))], first, None, last)


def _small_step(partials, extras, ws, ms, vs, shard_of):
    n = len(partials)
    terms = list(partials) + list(extras)
    nt = len(terms)
    rows = [t for t in range(nt) if terms[t].shape[0] == 1]
    mats = [t for t in range(nt) if terms[t].shape[0] != 1]
    row_block = (8, max(terms[t].shape[1] for t in rows))
    assert len(rows) <= row_block[0]
    vmem = pl.BlockSpec(memory_space=pltpu.VMEM)

    def pack(*refs):
        packed = refs[-1]
        packed[...] = jnp.zeros_like(packed)
        for i, t in enumerate(rows):
            packed[i:i + 1, 0:terms[t].shape[1]] = refs[i][...]

    packed = pl.pallas_call(pack, name="small_pack", in_specs=[vmem] * len(rows), out_specs=vmem,
                            out_shape=jax.ShapeDtypeStruct(row_block, F32))(*[terms[t] for t in rows])
    slots = _run_on_sequencer("allgather_small", _everyone_hosted([packed] + [terms[t] for t in mats]),
                              EVERYONE, 2)

    def body(*refs):
        slot_refs, refs = refs[:len(slots)], refs[len(slots):]
        w_refs, refs = refs[:n], refs[n:]
        m_refs, refs = refs[:n], refs[n:]
        v_refs, outs = refs[:n], refs[n:]
        sums = []
        for ref in slot_refs:
            g = ref[0]
            for dev in range(1, 8):
                g = g + ref[dev]
            sums.append(g)
        chip = 2 * lax.axis_index("x") + lax.axis_index("y")
        for t in range(nt):
            if t in rows:
                i = rows.index(t)
                g = sums[0][i:i + 1, 0:terms[t].shape[1]]
            else:
                g = sums[1 + mats.index(t)]
            if t >= n:
                outs[4 * n + t - n][...] = g
                continue
            if shard_of[t]:
                width = ws[t].shape[-1]
                mine = jnp.zeros(ws[t].shape, F32)
                for s in range(4):
                    mine = jnp.where(chip == s, g[:, s * width:(s + 1) * width], mine)
                g = mine
            delta, mn, vn = _adamw_math(w_refs[t][...], g, m_refs[t][...], v_refs[t][...])
            outs[4 * t][...] = g
            outs[4 * t + 1][...] = delta
            outs[4 * t + 2][...] = mn
            outs[4 * t + 3][...] = vn

    out_shapes = []
    for t in range(n):
        out_shapes += [jax.ShapeDtypeStruct(ws[t].shape, F32)] * 4
    out_shapes += [jax.ShapeDtypeStruct(a.shape, F32) for a in extras]
    res = pl.pallas_call(
        body, name="small_step",
        in_specs=[vmem] * (len(slots) + 3 * n), out_specs=[vmem] * len(out_shapes), out_shape=out_shapes,
    )(*slots, *ws, *ms, *vs)
    return [res[4 * t:4 * t + 4] for t in range(n)], res[4 * n:4 * n + nt - n]


def _adamw_math(w, g, m, v):
    m = ADAM_B1 * m + (1.0 - ADAM_B1) * g
    v = ADAM_B2 * v + (1.0 - ADAM_B2) * (g * g)
    m_hat = m / (1.0 - ADAM_B1 ** ADAM_STEP)
    v_hat = v / (1.0 - ADAM_B2 ** ADAM_STEP)
    delta = -ADAM_LR * (m_hat / (jnp.sqrt(v_hat) + ADAM_EPS) + ADAM_WD * w)
    return delta, m, v


def _row_tile(rows):
    return 256 if rows % 256 == 0 else rows


def _sum_partials(name, own, recv, chip):
    rows, cols = own.shape[1:]
    tr = _row_tile(rows)

    def body(chip_ref, own_ref, recv_ref, o_ref):
        acc = own_ref[...]
        for j in range(3):
            acc = acc + recv_ref[j].astype(F32)
        o_ref[...] = acc

    return pl.pallas_call(
        body, name=name,
        grid_spec=pltpu.PrefetchScalarGridSpec(
            num_scalar_prefetch=1, grid=(rows // tr,),
            in_specs=[pl.BlockSpec((None, tr, cols), lambda i, chip_ref: (chip_ref[0], i, 0)),
                      pl.BlockSpec((3, tr, cols), lambda i, chip_ref: (0, i, 0))],
            out_specs=pl.BlockSpec((tr, cols), lambda i, chip_ref: (i, 0))),
        out_shape=jax.ShapeDtypeStruct((rows, cols), F32),
        compiler_params=_params(("parallel",)),
    )(chip.reshape(1).astype(jnp.int32), own, recv)


def _adamw(name, w, m, v, g_parts):
    rows, cols = w.shape
    tr = _row_tile(rows)
    n = len(g_parts)

    def body(w_ref, m_ref, v_ref, *refs):
        g_refs = refs[:n]
        go_ref, d_ref, mo_ref, vo_ref = refs[n:]
        g = g_refs[0][...]
        for r in g_refs[1:]:
            g = g + r[...]
        delta, mn, vn = _adamw_math(w_ref[...], g, m_ref[...], v_ref[...])
        go_ref[...] = g
        d_ref[...] = delta
        mo_ref[...] = mn
        vo_ref[...] = vn

    spec = pl.BlockSpec((tr, cols), lambda i: (i, 0))
    return pl.pallas_call(
        body, name=name, grid=(rows // tr,),
        in_specs=[spec] * (3 + n), out_specs=[spec] * 4,
        out_shape=[jax.ShapeDtypeStruct((rows, cols), F32)] * 4,
        compiler_params=_params(("parallel",)),
    )(w, m, v, *g_parts)


def _local_step(x, target, ga, wa_in, rel_bias, later_shards, gk, t5, gb, sinks, gf):
    s, d = x.shape
    tm = min(TM_DENSE, s)
    nt = s // tm
    half = d // 2
    row = pl.BlockSpec((tm, d), lambda i: (i, 0))
    whole = lambda shape: pl.BlockSpec(shape, lambda *_: (0,) * len(shape))

    n1, = _norm_fwd("norm_a", x, ga)
    zqkv = _matmul("proj_a_qkv", n1, wa_in, dims=NN, grid=(3, nt + 1), zero_axis=1,
                   a_spec=pl.BlockSpec((tm, d), lambda j, i: (jnp.maximum(i - 1, 0), 0)),
                   b_spec=pl.BlockSpec((None, d, d), lambda j, i: (j, 0, 0)),
                   o_spec=pl.BlockSpec((None, tm, d), lambda j, i: (j, i, 0)),
                   out_shape=(3, tm + s, d), out_dtype=BF16)
    gate_a = _matmul("proj_a_gate", n1, wa_in, dims=NN, grid=(nt,),
                     a_spec=row, b_spec=pl.BlockSpec((None, d, d), lambda i: (3, 0, 0)), o_spec=row,
                     out_shape=(s, d), out_dtype=F32)
    onehot_a = _a_offset_onehot()
    diag_a = _diag_rows(onehot_a, rel_bias)
    (o_a, u_a, lse_a), gathered = _attn_a_fwd(zqkv, gate_a, diag_a, hosted=_allgather_routed(later_shards))
    wa_out, wkv, wb_in, wb_out = gathered
    wa_out = wa_out.reshape(d, d)
    wkv = wkv.reshape(d, -1)
    wb_out = wb_out.reshape(d, d)
    h1, nk, n2 = _out_norms("out_a_norms", u_a, wa_out, x, jnp.concatenate([gk, gb], axis=0))
    kvw = wkv.shape[1]
    wkv_x = jnp.concatenate([wkv[:, (i // 2) * HEAD_DIM:(i // 2 + 1) * HEAD_DIM] for i in range(8)], axis=1)
    kvx = _matmul("proj_kv", nk, wkv_x, dims=NN, grid=(nt + 1,), zero_axis=0,
                  a_spec=pl.BlockSpec((tm, d), lambda i: (jnp.maximum(i - 1, 0), 0)), b_spec=whole((d, B_KVX)),
                  o_spec=pl.BlockSpec((tm, B_KVX), lambda i: (i, 0)), out_shape=(tm + s, B_KVX), out_dtype=BF16)
    qb = _matmul("proj_b_q", n2, wb_in, dims=NN, grid=(2, nt),
                 a_spec=pl.BlockSpec((tm, d), lambda j, i: (i, 0)),
                 b_spec=pl.BlockSpec((None, d, half), lambda j, i: (j, 0, 0)),
                 o_spec=pl.BlockSpec((tm, half), lambda j, i: (i, j)), out_shape=(s, d), out_dtype=BF16)
    gate_b = _matmul("proj_b_gate", n2, wb_in, dims=NN, grid=(2, nt),
                     a_spec=pl.BlockSpec((tm, d), lambda j, i: (i, 0)),
                     b_spec=pl.BlockSpec((None, d, half), lambda j, i: (2 + j, 0, 0)),
                     o_spec=pl.BlockSpec((tm, half), lambda j, i: (i, j)), out_shape=(s, d), out_dtype=F32)
    onehot_b = _b_offset_onehot()
    base_b = jnp.roll(_diag_rows(onehot_b, t5)[..., ::-1], TQ, axis=-1)
    o_b, u_b, lse_b = _attn_b_fwd(qb, kvx, gate_b, base_b, sinks)
    dh2, loss, d_gf = _out_loss_head(u_b, wb_out, h1, target, gf)

    du_b = _matmul("dout_b", dh2, wb_out, dims=NT, grid=(nt,), a_spec=row, b_spec=whole((d, d)), o_spec=row,
                   out_shape=(s, d), out_dtype=F32)
    d_wb_out = _matmul("dw_out_b", u_b, dh2, dims=TN, grid=(2,),
                       a_spec=whole((s, d)), b_spec=pl.BlockSpec((s, half), lambda j: (0, j)),
                       o_spec=pl.BlockSpec((d, half), lambda j: (0, j)),
                       out_shape=(d, d), out_dtype=F32, also_bf16=True)
    dz_b, dkv, dsum_b, dsinks = _attn_b_bwd(qb, kvx, gate_b, o_b, du_b, lse_b, base_b, sinks)
    ddiag_b = jnp.roll(dsum_b[..., ::-1], -1, axis=-1)
    d_wb_in = _matmul("dw_in_b", n2, dz_b, dims=TN, grid=(4,),
                      a_spec=whole((s, d)), b_spec=pl.BlockSpec((None, s, half), lambda j: (j, 0, 0)),
                      o_spec=pl.BlockSpec((None, d, half), lambda j: (j, 0, 0)),
                      out_shape=(4, d, half), out_dtype=F32, also_bf16=True)
    d_wkv = _matmul("dw_kv", nk, dkv, dims=TN, grid=(1,),
                    a_spec=whole((s, d)), b_spec=whole((s, kvw)), o_spec=whole((d, kvw)),
                    out_shape=(d, kvw), out_dtype=F32, also_bf16=True)
    dh1, d_gkb = _proj_norm_bwd("dproj_kv_b", h1, dh2, jnp.concatenate([gk, gb], axis=0),
                                [(dkv[None], wkv[None]), (dz_b, wb_in)])

    du_a = _matmul("dout_a", dh1, wa_out, dims=NT, grid=(nt,), a_spec=row, b_spec=whole((d, d)), o_spec=row,
                   out_shape=(s, d), out_dtype=F32)
    d_wa_out = _matmul("dw_out_a", u_a, dh1, dims=TN, grid=(2,),
                       a_spec=whole((s, d)), b_spec=pl.BlockSpec((s, half), lambda j: (0, j)),
                       o_spec=pl.BlockSpec((d, half), lambda j: (0, j)),
                       out_shape=(d, d), out_dtype=F32, also_bf16=True)
    early = dict(a_w_out=[g.reshape(4, d // 4, d) for g in d_wa_out],
                 kv_w=[g.reshape(4, d // 4, kvw) for g in d_wkv], b_w_in=list(d_wb_in),
                 b_w_out=[g.reshape(4, d // 4, d) for g in d_wb_out])
    (dz_a, ddiag_a), early_recv = _attn_a_bwd(
        zqkv, gate_a, o_a, du_a, lse_a, diag_a, hosted=_scatter_hosted([early[n][1] for n in early]))
    d_wa_in = _matmul("dw_in_a", n1, dz_a, dims=TN, grid=(4, 2),
                      a_spec=whole((s, d)), b_spec=pl.BlockSpec((None, s, half), lambda j, h: (j, 0, h)),
                      o_spec=pl.BlockSpec((None, d, half), lambda j, h: (j, 0, h)),
                      out_shape=(4, d, d), out_dtype=F32, also_bf16=True)
    late_recv = _run_on_sequencer("scatter_a_w_in", _scatter_hosted([d_wa_in[1]]), SCATTER_PEERS, 0)
    grad_x, d_ga = _proj_norm_bwd("dproj_a", x, dh1, ga, [(dz_a, wa_in)])

    small = dict(a_norm=d_ga, kv_norm=d_gkb[0:1], b_norm=d_gkb[1:2], b_sinks=dsinks[0:1, :HEADS], final_norm=d_gf)
    small["by_offset"] = dict(a_rel_bias=(onehot_a, ddiag_a.reshape(HEADS, -1)),
                              t5_bias=(onehot_b, ddiag_b.reshape(HEADS, -1)))
    own = dict(a_w_in=d_wa_in[0], **{n: early[n][0] for n in early})
    received = dict(a_w_in=late_recv[0], **dict(zip(early, early_recv)))
    return loss, grad_x, small, own, received


SMALL = ("a_norm", "kv_norm", "b_norm", "b_sinks", "final_norm")
TABLES = ("a_rel_bias", "t5_bias")
BIG = ("a_w_in", "a_w_out", "kv_w", "b_w_in", "b_w_out")
ORDER = ("a_norm", "a_w_in", "a_rel_bias", "a_w_out", "kv_norm", "kv_w", "t5_bias", "b_norm", "b_w_in",
         "b_sinks", "b_w_out", "final_norm")


def kernel(x, a_norm, a_w_in, a_rel_bias, a_w_out, kv_norm, kv_w, t5_bias, b_norm, b_w_in, b_sinks, b_w_out, final_norm, loss_target, m_a_norm, m_a_w_in, m_a_rel_bias, m_a_w_out, m_kv_norm, m_kv_w, m_t5_bias, m_b_norm, m_b_w_in, m_b_sinks, m_b_w_out, m_final_norm, v_a_norm, v_a_w_in, v_a_rel_bias, v_a_w_out, v_kv_norm, v_kv_w, v_t5_bias, v_b_norm, v_b_w_in, v_b_sinks, v_b_w_out, v_final_norm):
    w = dict(a_norm=a_norm, a_w_in=a_w_in, a_rel_bias=a_rel_bias, a_w_out=a_w_out, kv_norm=kv_norm, kv_w=kv_w,
             t5_bias=t5_bias, b_norm=b_norm, b_w_in=b_w_in, b_sinks=b_sinks, b_w_out=b_w_out,
             final_norm=final_norm)
    m = dict(a_norm=m_a_norm, a_w_in=m_a_w_in, a_rel_bias=m_a_rel_bias, a_w_out=m_a_w_out, kv_norm=m_kv_norm,
             kv_w=m_kv_w, t5_bias=m_t5_bias, b_norm=m_b_norm, b_w_in=m_b_w_in, b_sinks=m_b_sinks,
             b_w_out=m_b_w_out, final_norm=m_final_norm)
    v = dict(a_norm=v_a_norm, a_w_in=v_a_w_in, a_rel_bias=v_a_rel_bias, a_w_out=v_a_w_out, kv_norm=v_kv_norm,
             kv_w=v_kv_w, t5_bias=v_t5_bias, b_norm=v_b_norm, b_w_in=v_b_w_in, b_sinks=v_b_sinks,
             b_w_out=v_b_w_out, final_norm=v_final_norm)
    d = D_MODEL
    chip = 2 * lax.axis_index("x") + lax.axis_index("y")

    shard2d = dict(a_w_in=a_w_in[0], a_w_out=a_w_out[0], kv_w=kv_w, b_w_in=b_w_in[0], b_w_out=b_w_out[0])

    wa_in, = _run_on_sequencer("allgather_first", _allgather_routed([shard2d["a_w_in"].astype(BF16)]),
                               GATHER_PEERS, 1)
    ga, = _run_alone("allgather_norm", _allgather_hosted([a_norm], [False]))
    ga = ga.reshape(1, d)

    loss, grad_x, small, own, received = _local_step(
        x[0], loss_target[0], ga, wa_in, a_rel_bias[0], [shard2d[n].astype(BF16) for n in BIG[1:]],
        kv_norm.reshape(1, d), t5_bias, b_norm, b_sinks, final_norm.reshape(1, d))

    out = {}
    as2d = lambda a: a.reshape(-1, a.shape[-1])
    small_res, (loss_sum, *offset_sums) = _small_step(
        [small[n] for n in SMALL], [loss] + [small["by_offset"][n][1] for n in TABLES],
        [as2d(w[n]) for n in SMALL], [as2d(m[n]) for n in SMALL], [as2d(v[n]) for n in SMALL],
        [n == "a_norm" for n in SMALL])
    for n, res in zip(SMALL, small_res):
        out[n] = [r.reshape(w[n].shape) for r in res]
    loss_out = loss_sum.reshape(())
    for n, summed in zip(TABLES, offset_sums):
        grad = _diag_rows_grad(small["by_offset"][n][0], summed)
        res = _adamw("adamw_" + n, as2d(w[n]), as2d(m[n]), as2d(v[n]), [grad])
        out[n] = [r.reshape(w[n].shape) for r in res]

    core_sums = [_sum_partials("sum_" + n, own[n], received[n], chip) for n in BIG]
    sibling_sums = (_swap_with_sibling("swap_last", core_sums[:1])
                    + _swap_with_sibling("swap_early", core_sums[1:]))

    for n, mine, theirs in zip(BIG, core_sums, sibling_sums):
        res = _adamw("adamw_" + n, shard2d[n], m[n].reshape(shard2d[n].shape), v[n].reshape(shard2d[n].shape),
                     [mine, theirs])
        out[n] = [r.reshape(w[n].shape) for r in res]

    grads = [out[n][0] for n in ORDER]
    deltas = [out[n][1] for n in ORDER]
    new_m = [out[n][2] for n in ORDER]
    new_v = [out[n][3] for n in ORDER]
    return (loss_out, grad_x[None], *grads, *deltas, *new_m, *new_v)
```

```python
import functools
import math

import jax
import jax.numpy as jnp
import numpy as np
from jax import lax
from jax.experimental import pallas as pl
from jax.experimental.pallas import tpu as pltpu
from jax.experimental.pallas import tpu_sc as plsc

F32 = jnp.float32
BF16 = jnp.bfloat16
MESH = pl.DeviceIdType.MESH

D_MODEL = 1024
HEADS = 16
HEAD_DIM = 64
CHUNK = 64
RMS_EPS = 1e-6
SCALE = HEAD_DIM ** -0.5
A_LEFT_CHUNKS = 8
A_REL_CLIP = 256
B_LEFT_CHUNKS = 2
B_KV_HEADS = 2
B_GROUP = HEADS // B_KV_HEADS
T5_BUCKETS = 32
T5_MAX_DIST = 128
ADAM_LR = 0.001
ADAM_B1 = 0.9
ADAM_B2 = 0.999
ADAM_EPS = 1e-08
ADAM_WD = 0.01
ADAM_STEP = 10

MASKED = -1e30
LANES = 128
TQ = 128
A_PAIRS = 2
A_PAIRS_FWD = 4
KB = 128
A_KBLOCKS = A_LEFT_CHUNKS * CHUNK // KB + 1
B_KBLOCKS = B_LEFT_CHUNKS * CHUNK // KB + 1
A_WIN = A_KBLOCKS * KB
B_WIN = B_KBLOCKS * KB
TM = 512
TM_DENSE = 1024
TM_PARTS = 512
VMEM_LIMIT = 56 * 1024 * 1024

NT = (((1,), (1,)), ((), ()))
TN = (((0,), (0,)), ((), ()))
NN = (((1,), (0,)), ((), ()))


def _params(sem=None):
    return pltpu.CompilerParams(dimension_semantics=sem, vmem_limit_bytes=VMEM_LIMIT)


class _Hosted:
    def __init__(self, inputs, out_shapes, sems, first, middle, last):
        self.inputs, self.out_shapes, self.sems = list(inputs), list(out_shapes), list(sems)
        self.first, self.middle, self.last = first, middle, last


def _call(body, *, name, grid, in_specs, out_specs, out_shape, args, scratch_shapes=(), sem=None, hosted=None):
    in_specs, out_specs, out_shape = list(in_specs), list(out_specs), list(out_shape)
    scratch_shapes = list(scratch_shapes)
    if hosted is None:
        out = pl.pallas_call(
            body, name=name, grid=grid, in_specs=in_specs, out_specs=out_specs, out_shape=out_shape,
            scratch_shapes=scratch_shapes, compiler_params=_params(sem))(*args)
        return list(out), []
    n_in, n_out, n_scr = len(in_specs), len(out_shape), len(scratch_shapes)
    h_in, h_out = len(hosted.inputs), len(hosted.out_shapes)
    total = int(np.prod(grid)) if grid else 1

    def wrapped(*refs):
        ins, refs = refs[:n_in], refs[n_in:]
        h_ins, refs = refs[:h_in], refs[h_in:]
        outs, refs = refs[:n_out], refs[n_out:]
        h_outs, refs = refs[:h_out], refs[h_out:]
        scr, h_sems = refs[:n_scr], refs[n_scr:]
        step = 0
        for axis, size in enumerate(grid):
            step = step * size + pl.program_id(axis)

        @pl.when(step == 0)
        def _():
            hosted.first(h_ins, h_outs, h_sems)

        body(*ins, *outs, *scr)
        if hosted.middle is not None:
            @pl.when(step == total // 2)
            def _():
                hosted.middle(h_ins, h_outs, h_sems)

        @pl.when(step == total - 1)
        def _():
            hosted.last(h_ins, h_outs, h_sems)

    out = pl.pallas_call(
        wrapped, name=name, grid=grid, in_specs=in_specs + [ANY] * h_in, out_specs=out_specs + [ANY] * h_out,
        out_shape=out_shape + hosted.out_shapes, scratch_shapes=scratch_shapes + hosted.sems,
        compiler_params=_params(("arbitrary",) * len(grid)))(*args, *hosted.inputs)
    return list(out[:n_out]), list(out[n_out:])


def _matmul(name, a, b, *, dims, grid, a_spec, b_spec, o_spec, out_shape, out_dtype,
            parts=1, resid=None, resid_spec=None, also_bf16=False, hosted=None, zero_axis=None):
    def body(*refs):
        if zero_axis is None:
            product(*refs)
        else:
            @pl.when(pl.program_id(zero_axis) == 0)
            def _():
                refs[2][...] = jnp.zeros_like(refs[2])

            @pl.when(pl.program_id(zero_axis) > 0)
            def _():
                product(*refs)

    def product(*refs):
        a_ref, b_ref = refs[:2]
        r_ref = refs[2] if resid is not None else None
        o_ref = refs[3] if resid is not None else refs[2]
        if parts == 1:
            prod = lax.dot_general(a_ref[...].astype(BF16), b_ref[...].astype(BF16), dims,
                                   preferred_element_type=F32)
        else:
            prod = None
            for part in range(parts):
                term = lax.dot_general(a_ref[part].astype(BF16), b_ref[part].astype(BF16), dims,
                                       preferred_element_type=F32)
                prod = term if prod is None else prod + term
        if resid is not None:
            prod = r_ref[...] + prod
        o_ref[...] = prod.astype(out_dtype)
        if also_bf16:
            refs[-1][...] = prod.astype(BF16)

    in_specs = [a_spec, b_spec]
    args = [a, b]
    if resid is not None:
        in_specs.append(resid_spec)
        args.append(resid)
    sem = ["parallel"] * len(grid)
    out_specs = [o_spec]
    out_shapes = [jax.ShapeDtypeStruct(out_shape, out_dtype)]
    if also_bf16:
        out_specs.append(o_spec)
        out_shapes.append(jax.ShapeDtypeStruct(out_shape, BF16))
    out, extra = _call(body, name=name, grid=grid, in_specs=in_specs, out_specs=out_specs, out_shape=out_shapes,
                       args=args, sem=tuple(sem), hosted=hosted)
    res = out[0] if not also_bf16 else tuple(out)
    return res if hosted is None else (res, extra)


def _rms_rows(x):
    return lax.rsqrt(jnp.mean(x * x, axis=-1, keepdims=True) + RMS_EPS)


def _norm_fwd(name, x, gains):
    s, d = x.shape
    n = gains.shape[0]

    def body(x_ref, g_ref, *o_refs):
        xv = x_ref[...]
        xh = xv * _rms_rows(xv)
        for i in range(n):
            o_refs[i][...] = (xh * g_ref[i:i + 1, :]).astype(BF16)

    row = pl.BlockSpec((TM, d), lambda i: (i, 0))
    return pl.pallas_call(
        body, name=name, grid=(s // TM,),
        in_specs=[row, pl.BlockSpec((n, d), lambda i: (0, 0))],
        out_specs=[row] * n,
        out_shape=[jax.ShapeDtypeStruct((s, d), BF16)] * n,
        compiler_params=_params(("parallel",)),
    )(x, gains)


def _proj_norm_bwd(name, x, dres, gains, branches):
    s, d = x.shape
    n = len(branches)
    tm = min(TM_PARTS, s)

    def body(x_ref, r_ref, g_ref, *refs):
        ab_refs, dx_ref, dg_ref = refs[:2 * n], refs[2 * n], refs[2 * n + 1]
        i = pl.program_id(0)
        xv = x_ref[...]
        r = _rms_rows(xv)
        xh = xv * r

        @pl.when(i == 0)
        def _():
            dg_ref[...] = jnp.zeros_like(dg_ref)

        a = None
        for j in range(n):
            a_ref, b_ref = ab_refs[2 * j], ab_refs[2 * j + 1]
            dn = None
            for part in range(a_ref.shape[0]):
                term = lax.dot_general(a_ref[part], b_ref[part], NT, preferred_element_type=F32)
                dn = term if dn is None else dn + term
            t = dn * g_ref[j:j + 1, :]
            a = t if a is None else a + t
            dg_ref[j:j + 1, :] += jnp.sum(dn * xh, axis=0, keepdims=True)
        dx_ref[...] = r_ref[...] + r * (a - xh * jnp.mean(xh * a, axis=-1, keepdims=True))

    row = pl.BlockSpec((tm, d), lambda i: (i, 0))
    small = pl.BlockSpec((n, d), lambda i: (0, 0))
    ab_specs, ab_args = [], []
    for a, b in branches:
        ab_specs += [pl.BlockSpec((a.shape[0], tm, a.shape[2]), lambda i: (0, i, 0)),
                     pl.BlockSpec(b.shape, lambda i: (0, 0, 0))]
        ab_args += [a, b]
    return pl.pallas_call(
        body, name=name, grid=(s // tm,),
        in_specs=[row, row, small] + ab_specs,
        out_specs=[row, small],
        out_shape=[jax.ShapeDtypeStruct((s, d), F32), jax.ShapeDtypeStruct((n, d), F32)],
        compiler_params=_params(("arbitrary",)),
    )(x, dres, gains, *ab_args)


def _out_norms(name, u, w_out, resid, gains):
    s, d = resid.shape
    n = gains.shape[0]
    tm = min(TM_DENSE, s)

    def body(u_ref, w_ref, r_ref, g_ref, h_ref, *o_refs):
        hv = r_ref[...] + jnp.dot(u_ref[...], w_ref[...], preferred_element_type=F32)
        h_ref[...] = hv
        hh = hv * _rms_rows(hv)
        for i in range(n):
            o_refs[i][...] = (hh * g_ref[i:i + 1, :]).astype(BF16)

    row = pl.BlockSpec((tm, d), lambda i: (i, 0))
    return pl.pallas_call(
        body, name=name, grid=(s // tm,),
        in_specs=[row, pl.BlockSpec((d, d), lambda i: (0, 0)), row, pl.BlockSpec((n, d), lambda i: (0, 0))],
        out_specs=[row] * (n + 1),
        out_shape=[jax.ShapeDtypeStruct((s, d), F32)] + [jax.ShapeDtypeStruct((s, d), BF16)] * n,
        compiler_params=_params(("parallel",)),
    )(u, w_out, resid, gains)


def _out_loss_head(u, w_out, resid, target, gain):
    s, d = resid.shape
    tm = min(TM_PARTS, s)

    def body(u_ref, w_ref, r_ref, t_ref, g_ref, dh_ref, loss_ref, dg_ref):
        i = pl.program_id(0)
        hv = r_ref[...] + jnp.dot(u_ref[...], w_ref[...], preferred_element_type=F32)
        r = _rms_rows(hv)
        hh = hv * r
        g = g_ref[...]
        err = hh * g - t_ref[...]
        part = 0.5 * jnp.sum(jnp.sum(err * err, axis=-1, keepdims=True) * (1.0 / d), axis=0, keepdims=True)
        dy = err * (1.0 / d)
        a = dy * g
        dh_ref[...] = r * (a - hh * jnp.mean(hh * a, axis=-1, keepdims=True))
        dg = jnp.sum(dy * hh, axis=0, keepdims=True)

        @pl.when(i == 0)
        def _():
            loss_ref[...] = part
            dg_ref[...] = dg

        @pl.when(i > 0)
        def _():
            loss_ref[...] += part
            dg_ref[...] += dg

    row = pl.BlockSpec((tm, d), lambda i: (i, 0))
    return pl.pallas_call(
        body, name="out_b_loss_head", grid=(s // tm,),
        in_specs=[row, pl.BlockSpec((d, d), lambda i: (0, 0)), row, row, pl.BlockSpec((1, d), lambda i: (0, 0))],
        out_specs=[row, pl.BlockSpec((1, 1), lambda i: (0, 0)), pl.BlockSpec((1, d), lambda i: (0, 0))],
        out_shape=[jax.ShapeDtypeStruct((s, d), F32), jax.ShapeDtypeStruct((1, 1), F32),
                   jax.ShapeDtypeStruct((1, d), F32)],
        compiler_params=_params(("arbitrary",)),
    )(u, w_out, resid, target, gain)


def _silu_parts(g):
    sig = jax.nn.sigmoid(g)
    return g * sig, sig * (1.0 + g * (1.0 - sig))


def _lane_lo(rows):
    return lax.broadcasted_iota(jnp.int32, (rows, LANES), 1) < HEAD_DIM


def _stack_pair(x):
    lo = _lane_lo(x.shape[0])
    zero = jnp.zeros_like(x)
    return jnp.concatenate([jnp.where(lo, x, zero), jnp.where(lo, zero, x)], axis=0)


def _unstack_pair(y, w):
    return jnp.where(_lane_lo(w), y[:w], y[w:])


def _block_valid(b, left_blocks, width):
    col = lax.broadcasted_iota(jnp.int32, (1, 2 * width), 1)
    col = jnp.where(col >= width, col - width, col)
    return (col // KB + (b - left_blocks)) >= 0


def _toeplitz_tile(diag_row, width, left_chunks):
    wide = width + TQ
    rolled = pltpu.roll(jnp.broadcast_to(diag_row, (TQ, wide)), 1, 1, stride=1, stride_axis=0)
    i = lax.broadcasted_iota(jnp.int32, (TQ, width), 0) // CHUNK
    j = lax.broadcasted_iota(jnp.int32, (TQ, width), 1) // CHUNK
    dc = i + left_chunks - j
    return jnp.where((dc >= 0) & (dc <= left_chunks), rolled[:, TQ:], MASKED)


def _toeplitz_sum(tile, width):
    flip = (lax.broadcasted_iota(jnp.int32, (TQ, TQ), 0) + lax.broadcasted_iota(jnp.int32, (TQ, TQ), 1)
            == TQ - 1).astype(F32)
    reversed_rows = jnp.dot(flip, tile, precision=lax.Precision.HIGHEST, preferred_element_type=F32)
    padded = jnp.concatenate([reversed_rows, jnp.zeros((TQ, TQ), F32)], axis=1)
    rolled = pltpu.roll(padded, 0, 1, stride=1, stride_axis=0)
    return jnp.sum(rolled, axis=0, keepdims=True)


def _softmax_pair(sc, w, sink=None):
    ps, inv, lses = [], [], []
    for e in range(2):
        sh = sc[:, e * w:(e + 1) * w]
        m = jnp.max(sh, axis=-1, keepdims=True)
        if sink is not None:
            m = jnp.maximum(m, sink[e])
        ex = jnp.exp(sh - m)
        l = jnp.sum(ex, axis=-1, keepdims=True)
        if sink is not None:
            l = l + jnp.exp(sink[e] - m)
        ps.append(ex.astype(BF16))
        inv.append(1.0 / l)
        lses.append(m + jnp.log(l))
    return jnp.concatenate(ps, axis=-1), inv, lses


def _softmax_pair_bwd(sc, dp, lse, delta, w):
    ps, dss = [], []
    for e in range(2):
        p = jnp.exp(sc[:, e * w:(e + 1) * w] - lse[e])
        ps.append(p)
        dss.append(p * (dp[:, e * w:(e + 1) * w] - delta[e]))
    return jnp.concatenate(ps, axis=-1), jnp.concatenate(dss, axis=-1)


def _pair_rowsums(x, lo):
    zero = jnp.zeros_like(x)
    return (jnp.sum(jnp.where(lo, x, zero), axis=-1, keepdims=True),
            jnp.sum(jnp.where(lo, zero, x), axis=-1, keepdims=True))


def _a_qkv_specs(rows, pad, pw):
    return [pl.BlockSpec((None, TQ, pw), lambda p, b: (0, b + pad // TQ, p)),
            pl.BlockSpec((None, rows, pw), lambda p, b: (1, 0, p)),
            pl.BlockSpec((None, rows, pw), lambda p, b: (2, 0, p))]


def _window(ref, b, pad, win, lanes):
    start = pl.multiple_of(b * TQ + pad - (win - TQ), KB)
    return ref[pl.ds(start, win), lanes]


def _attn_a_fwd(zqkv, g, diag, hosted=None):
    s = g.shape[0]
    pad = zqkv.shape[1] - s
    nb = s // TQ
    left = A_KBLOCKS - 1
    pairs = A_PAIRS_FWD
    pw = pairs * LANES
    wide = A_WIN + TQ

    def body(q_ref, k_ref, v_ref, g_ref, diag_ref, o_ref, u_ref, lse_ref, bias_scr):
        b = pl.program_id(1)

        @pl.when(b == 0)
        def _():
            for hh in range(2 * pairs):
                bias_scr[hh // 2, :, (hh % 2) * A_WIN:(hh % 2 + 1) * A_WIN] = _toeplitz_tile(
                    diag_ref[hh], A_WIN, A_LEFT_CHUNKS)

        def step(first_blocks):
            lo = _lane_lo(TQ)
            for pp in range(pairs):
                ln = slice(pp * LANES, (pp + 1) * LANES)
                kcat = _stack_pair(_window(k_ref, b, pad, A_WIN, ln))
                vcat = _stack_pair(_window(v_ref, b, pad, A_WIN, ln))
                sc = lax.dot_general(q_ref[:, ln] * SCALE, kcat, NT, preferred_element_type=F32) + bias_scr[pp]
                if first_blocks:
                    sc = jnp.where(_block_valid(b, left, A_WIN), sc, MASKED)
                p, inv, lses = _softmax_pair(sc, A_WIN)
                ov = jnp.dot(p, vcat, preferred_element_type=F32) * jnp.where(lo, inv[0], inv[1])
                o_ref[:, ln] = ov
                lse_ref[pp] = jnp.where(lo, lses[0], lses[1])
                sg, _ = _silu_parts(g_ref[:, ln])
                u_ref[:, ln] = (ov * sg).astype(BF16)

        @pl.when(b < left)
        def _():
            step(True)

        @pl.when(b >= left)
        def _():
            step(False)

    tile = pl.BlockSpec((TQ, pw), lambda p, b: (b, p))
    return _call(
        body, name="attn_a_fwd", grid=(HEADS // 2 // pairs, nb),
        in_specs=_a_qkv_specs(pad + s, pad, pw) + [
            tile, pl.BlockSpec((2 * pairs, 1, wide), lambda p, b: (p, 0, 0))],
        out_specs=[tile, tile, pl.BlockSpec((pairs, TQ, LANES), lambda p, b: (p, b, 0))],
        out_shape=[jax.ShapeDtypeStruct((s, D_MODEL), F32), jax.ShapeDtypeStruct((s, D_MODEL), BF16),
                   jax.ShapeDtypeStruct((HEADS // 2, s, LANES), F32)],
        scratch_shapes=[pltpu.VMEM((pairs, TQ, 2 * A_WIN), F32)],
        sem=("parallel", "arbitrary"), hosted=hosted,
        args=(zqkv, zqkv, zqkv, g, diag))


def _attn_a_bwd(zqkv, g, o, du, lse, diag, hosted=None):
    s = g.shape[0]
    pad = zqkv.shape[1] - s
    nb = s // TQ
    left = A_KBLOCKS - 1
    pw = A_PAIRS * LANES
    wide = A_WIN + TQ

    def body(q_ref, k_ref, v_ref, g_ref, o_ref, du_ref, lse_ref, diag_ref, dz_ref, ddiag_ref,
             bias_scr, dbias_acc, dk_acc, dv_acc):
        b = pl.program_id(1)

        @pl.when(b == 0)
        def _():
            for hh in range(2 * A_PAIRS):
                bias_scr[hh // 2, :, (hh % 2) * A_WIN:(hh % 2 + 1) * A_WIN] = _toeplitz_tile(
                    diag_ref[hh], A_WIN, A_LEFT_CHUNKS)
            dbias_acc[...] = jnp.zeros_like(dbias_acc)
            dk_acc[...] = jnp.zeros_like(dk_acc)
            dv_acc[...] = jnp.zeros_like(dv_acc)

        def step(first_blocks):
            lo = _lane_lo(TQ)
            upper = lax.broadcasted_iota(jnp.int32, (LANES, A_WIN), 0) < HEAD_DIM
            rows = pl.ds(pl.multiple_of(b * TQ, TQ), TQ)
            sg, dsg = _silu_parts(g_ref[...])
            duv = du_ref[...]
            ov = o_ref[...]
            do = duv * sg
            dz_ref[3, rows, :] = (duv * ov * dsg).astype(BF16)
            do_o = do * ov
            do_bf = do.astype(BF16)
            for pp in range(A_PAIRS):
                ln = slice(pp * LANES, (pp + 1) * LANES)
                q = q_ref[:, ln] * SCALE
                kcat = _stack_pair(_window(k_ref, b, pad, A_WIN, ln))
                vcat = _stack_pair(_window(v_ref, b, pad, A_WIN, ln))
                sc = lax.dot_general(q, kcat, NT, preferred_element_type=F32) + bias_scr[pp]
                if first_blocks:
                    sc = jnp.where(_block_valid(b, left, A_WIN), sc, MASKED)
                lse_t = lse_ref[pp]
                dp = lax.dot_general(do_bf[:, ln], vcat, NT, preferred_element_type=F32)
                p, ds = _softmax_pair_bwd(sc, dp, (lse_t[:, 0:1], lse_t[:, HEAD_DIM:HEAD_DIM + 1]),
                                          _pair_rowsums(do_o[:, ln], lo), A_WIN)
                dbias_acc[pp] += ds
                dsb = ds.astype(BF16)
                dz_ref[0, rows, ln] = (jnp.dot(dsb, kcat, preferred_element_type=F32) * SCALE).astype(BF16)
                dkt = lax.dot_general(q, dsb, TN, preferred_element_type=F32)
                dvt = lax.dot_general(do_bf[:, ln], p.astype(BF16), TN, preferred_element_type=F32)
                dkt = jnp.where(upper, dkt[:, :A_WIN], dkt[:, A_WIN:])
                dvt = jnp.where(upper, dvt[:, :A_WIN], dvt[:, A_WIN:])
                for t in range(A_KBLOCKS):
                    blk = b + (pad // KB - left + t)
                    dk_acc[blk, ln, :] += dkt[:, t * KB:(t + 1) * KB]
                    dv_acc[blk, ln, :] += dvt[:, t * KB:(t + 1) * KB]

        @pl.when(b < left)
        def _():
            step(True)

        @pl.when(b >= left)
        def _():
            step(False)

        @pl.when(b == nb - 1)
        def _():
            for kb in range(s // KB):
                dz_ref[1, kb * KB:(kb + 1) * KB, :] = dk_acc[pad // KB + kb].T.astype(BF16)
                dz_ref[2, kb * KB:(kb + 1) * KB, :] = dv_acc[pad // KB + kb].T.astype(BF16)
            for hh in range(2 * A_PAIRS):
                ddiag_ref[hh] = _toeplitz_sum(
                    dbias_acc[hh // 2, :, (hh % 2) * A_WIN:(hh % 2 + 1) * A_WIN], A_WIN)

    tile = pl.BlockSpec((TQ, pw), lambda p, b: (b, p))
    diag_spec = pl.BlockSpec((2 * A_PAIRS, 1, wide), lambda p, b: (p, 0, 0))
    return _call(
        body, name="attn_a_bwd", grid=(HEADS // 2 // A_PAIRS, nb),
        in_specs=_a_qkv_specs(pad + s, pad, pw) + [
            tile, tile, tile, pl.BlockSpec((A_PAIRS, TQ, LANES), lambda p, b: (p, b, 0)), diag_spec],
        out_specs=[pl.BlockSpec((4, s, pw), lambda p, b: (0, 0, p)), diag_spec],
        out_shape=[jax.ShapeDtypeStruct((4, s, D_MODEL), BF16),
                   jax.ShapeDtypeStruct((HEADS, 1, wide), F32)],
        scratch_shapes=[pltpu.VMEM((A_PAIRS, TQ, 2 * A_WIN), F32), pltpu.VMEM((A_PAIRS, TQ, 2 * A_WIN), F32),
                        pltpu.VMEM(((pad + s) // KB, pw, KB), F32), pltpu.VMEM(((pad + s) // KB, pw, KB), F32)],
        sem=("parallel", "arbitrary"), hosted=hosted,
        args=(zqkv, zqkv, zqkv, g, o, du, lse, diag))


B_STACK = B_GROUP // 2
B_KVX = 4 * LANES
B_ROWS = B_STACK * TQ
B_WIDE = B_WIN + TQ


def _b_head_place(h):
    return h // B_GROUP, (h % B_GROUP) // 2, h % 2


def _toeplitz_tile_t(base_row, width, left_chunks):
    wide = width + TQ
    rolled = pltpu.roll(jnp.broadcast_to(base_row, (width, wide)), 0, 1, stride=1, stride_axis=0)
    j = lax.broadcasted_iota(jnp.int32, (width, TQ), 0) // CHUNK
    i = lax.broadcasted_iota(jnp.int32, (width, TQ), 1) // CHUNK
    dc = i + left_chunks - j
    return jnp.where((dc >= 0) & (dc <= left_chunks), rolled[:, :TQ], MASKED)


def _toeplitz_sum_t(tile_t, width):
    flip = (lax.broadcasted_iota(jnp.int32, (width, width), 0) + lax.broadcasted_iota(jnp.int32, (width, width), 1)
            == width - 1).astype(F32)
    reversed_rows = jnp.dot(flip, tile_t, precision=lax.Precision.HIGHEST, preferred_element_type=F32)
    padded = jnp.concatenate([reversed_rows, jnp.zeros((width, width), F32)], axis=1)
    rolled = pltpu.roll(padded, 0, 1, stride=1, stride_axis=0)
    return jnp.sum(rolled, axis=0, keepdims=True)


def _b_build_bias(base_ref, bias_scr):
    for h in range(HEADS):
        gi, pr, e = _b_head_place(h)
        bias_scr[gi, e * B_WIN:(e + 1) * B_WIN, pr * TQ:(pr + 1) * TQ] = _toeplitz_tile_t(
            base_ref[h], B_WIN, B_LEFT_CHUNKS)


def _b_stack(x, gi):
    return jnp.concatenate(
        [x[:, (B_STACK * gi + pr) * LANES:(B_STACK * gi + pr + 1) * LANES] for pr in range(B_STACK)], axis=0)


def _b_sink_rows(sink_ref, gi):
    block = lax.broadcasted_iota(jnp.int32, (1, B_ROWS), 1) // TQ
    rows = []
    for e in range(2):
        row = jnp.zeros((1, B_ROWS), F32)
        for pr in range(B_STACK):
            h = B_GROUP * gi + 2 * pr + e
            row = jnp.where(block == pr, sink_ref[0:1, h:h + 1], row)
        rows.append(row)
    return rows


def _b_scores_t(q_ref, kvv, bias_scr, gi, b, left, first_blocks):
    kcat = _stack_pair(kvv[:, gi * LANES:(gi + 1) * LANES])
    vcat = _stack_pair(kvv[:, (B_KV_HEADS + gi) * LANES:(B_KV_HEADS + gi + 1) * LANES])
    qs = _b_stack(q_ref, gi) * SCALE
    sc = lax.dot_general(kcat, qs, NT, preferred_element_type=F32) + bias_scr[gi]
    if first_blocks:
        row = lax.broadcasted_iota(jnp.int32, (2 * B_WIN, 1), 0)
        row = jnp.where(row >= B_WIN, row - B_WIN, row)
        sc = jnp.where((row // KB + (b - left)) >= 0, sc, MASKED)
    return kcat, vcat, qs, sc


def _attn_b_fwd(qb, kvx, gate, base, sinks):
    s = qb.shape[0]
    pad = kvx.shape[0] - s
    nb = s // TQ
    left = B_KBLOCKS - 1

    def body(q_ref, kv_ref, g_ref, base_ref, sink_ref, o_ref, u_ref, lse_ref, bias_scr):
        b = pl.program_id(0)

        @pl.when(b == 0)
        def _():
            _b_build_bias(base_ref, bias_scr)

        def step(first_blocks):
            kvv = _window(kv_ref, b, pad, B_WIN, slice(None))
            upper = lax.broadcasted_iota(jnp.int32, (LANES, B_ROWS), 0) < HEAD_DIM
            lse_rows = []
            for gi in range(B_KV_HEADS):
                kcat, vcat, qs, sc = _b_scores_t(q_ref, kvv, bias_scr, gi, b, left, first_blocks)
                sink = _b_sink_rows(sink_ref, gi)
                ps, inv = [], []
                for e in range(2):
                    sh = sc[e * B_WIN:(e + 1) * B_WIN]
                    m = jnp.maximum(jnp.max(sh, axis=0, keepdims=True), sink[e])
                    ex = jnp.exp(sh - m)
                    l = jnp.sum(ex, axis=0, keepdims=True) + jnp.exp(sink[e] - m)
                    ps.append(ex.astype(BF16))
                    inv.append(1.0 / l)
                    lse_rows.append(m + jnp.log(l))
                pt = jnp.concatenate(ps, axis=0)
                ot = lax.dot_general(vcat, pt, TN, preferred_element_type=F32) * jnp.where(upper, inv[0], inv[1])
                ov = ot.T
                for pr in range(B_STACK):
                    pair = B_STACK * gi + pr
                    o_ref[:, pair * LANES:(pair + 1) * LANES] = ov[pr * TQ:(pr + 1) * TQ]
            lse_ref[0] = jnp.concatenate(lse_rows + [jnp.zeros((8 - len(lse_rows), B_ROWS), F32)], axis=0)
            sg, _ = _silu_parts(g_ref[...])
            u_ref[...] = (o_ref[...] * sg).astype(BF16)

        @pl.when(b < left)
        def _():
            step(True)

        @pl.when(b >= left)
        def _():
            step(False)

    row = pl.BlockSpec((TQ, D_MODEL), lambda b: (b, 0))
    return pl.pallas_call(
        body, name="attn_b_fwd", grid=(nb,),
        in_specs=[row, pl.BlockSpec((pad + s, B_KVX), lambda b: (0, 0)), row,
                  pl.BlockSpec((HEADS, 1, B_WIDE), lambda b: (0, 0, 0)), pl.BlockSpec((1, HEADS), lambda b: (0, 0))],
        out_specs=[row, row, pl.BlockSpec((1, 8, B_ROWS), lambda b: (b, 0, 0))],
        out_shape=[jax.ShapeDtypeStruct((s, D_MODEL), F32), jax.ShapeDtypeStruct((s, D_MODEL), BF16),
                   jax.ShapeDtypeStruct((nb, 8, B_ROWS), F32)],
        scratch_shapes=[pltpu.VMEM((B_KV_HEADS, 2 * B_WIN, B_ROWS), F32)],
        compiler_params=_params(("arbitrary",)),
    )(qb, kvx, gate, base, sinks)


def _attn_b_bwd(qb, kvx, gate, o, du, lse, base, sinks):
    s = qb.shape[0]
    pad = kvx.shape[0] - s
    nb = s // TQ
    left = B_KBLOCKS - 1
    half = D_MODEL // 2

    def body(q_ref, kv_ref, g_ref, o_ref, du_ref, lse_ref, base_ref, sink_ref, dz_ref, dkv_ref, dsum_ref,
             dsink_ref, bias_scr, dbias_acc, dkv_acc, dsink_acc):
        b = pl.program_id(0)

        @pl.when(b == 0)
        def _():
            _b_build_bias(base_ref, bias_scr)
            dbias_acc[...] = jnp.zeros_like(dbias_acc)
            dkv_acc[...] = jnp.zeros_like(dkv_acc)
            dsink_acc[...] = jnp.zeros_like(dsink_acc)

        def step(first_blocks):
            kvv = _window(kv_ref, b, pad, B_WIN, slice(None))
            sg, dsg = _silu_parts(g_ref[...])
            duv = du_ref[...]
            ov = o_ref[...]
            do = duv * sg
            dgate = (duv * ov * dsg).astype(BF16)
            dz_ref[2] = dgate[:, :half]
            dz_ref[3] = dgate[:, half:]
            do_o = do * ov
            do_bf = do.astype(BF16)
            lse_all = lse_ref[0]
            dsink_rows = []
            for gi in range(B_KV_HEADS):
                kcat, vcat, qs, sc = _b_scores_t(q_ref, kvv, bias_scr, gi, b, left, first_blocks)
                dos = _b_stack(do_bf, gi)
                doo_t = _b_stack(do_o, gi).T
                delta = (jnp.sum(doo_t[:HEAD_DIM], axis=0, keepdims=True),
                         jnp.sum(doo_t[HEAD_DIM:], axis=0, keepdims=True))
                sink = _b_sink_rows(sink_ref, gi)
                dp = lax.dot_general(vcat, dos, NT, preferred_element_type=F32)
                ps, dss = [], []
                for e in range(2):
                    lse_e = lse_all[2 * gi + e:2 * gi + e + 1]
                    delta_e = delta[e]
                    p = jnp.exp(sc[e * B_WIN:(e + 1) * B_WIN] - lse_e)
                    ps.append(p.astype(BF16))
                    dss.append(p * (dp[e * B_WIN:(e + 1) * B_WIN] - delta_e))
                    dsink_rows.append(-jnp.exp(sink[e] - lse_e) * delta_e)
                ds = jnp.concatenate(dss, axis=0)
                dbias_acc[gi] += ds
                dsb = ds.astype(BF16)
                dq = (lax.dot_general(kcat, dsb, TN, preferred_element_type=F32) * SCALE).T.astype(BF16)
                for pr in range(B_STACK):
                    dz_ref[gi, :, pr * LANES:(pr + 1) * LANES] = dq[pr * TQ:(pr + 1) * TQ]
                dk = _unstack_pair(jnp.dot(dsb, qs, preferred_element_type=F32), B_WIN)
                dv = _unstack_pair(jnp.dot(jnp.concatenate(ps, axis=0), dos, preferred_element_type=F32), B_WIN)
                krows = pl.ds(pl.multiple_of(b * TQ + pad - (B_WIN - TQ), KB), B_WIN)
                dkv_acc[krows, gi * LANES:(gi + 1) * LANES] += dk
                dkv_acc[krows, (B_KV_HEADS + gi) * LANES:(B_KV_HEADS + gi + 1) * LANES] += dv
            dsink_acc[...] += jnp.concatenate(
                dsink_rows + [jnp.zeros((8 - len(dsink_rows), B_ROWS), F32)], axis=0)

        @pl.when(b < left)
        def _():
            step(True)

        @pl.when(b >= left)
        def _():
            step(False)

        @pl.when(b == nb - 1)
        def _():
            lo_s = _lane_lo(s)
            for which in range(2):
                folded = []
                for gi in range(B_KV_HEADS):
                    part = dkv_acc[pad:pad + s, (which * B_KV_HEADS + gi) * LANES:(which * B_KV_HEADS + gi + 1) * LANES]
                    folded.append(part + pltpu.roll(part, HEAD_DIM, 1))
                dkv_ref[:, which * LANES:(which + 1) * LANES] = jnp.where(lo_s, folded[0], folded[1]).astype(BF16)
            lane8 = lax.broadcasted_iota(jnp.int32, dsink_ref.shape, 1)
            tot = jnp.zeros(dsink_ref.shape, F32)
            for h in range(HEADS):
                gi, pr, e = _b_head_place(h)
                dsum_ref[h] = _toeplitz_sum_t(
                    dbias_acc[gi, e * B_WIN:(e + 1) * B_WIN, pr * TQ:(pr + 1) * TQ], B_WIN)
                per_query = dsink_acc[2 * gi + e:2 * gi + e + 1, pr * TQ:(pr + 1) * TQ]
                tot = jnp.where(lane8 == h, jnp.sum(per_query, axis=1, keepdims=True), tot)
            dsink_ref[...] = tot

    row = pl.BlockSpec((TQ, D_MODEL), lambda b: (b, 0))
    base_spec = pl.BlockSpec((HEADS, 1, B_WIDE), lambda b: (0, 0, 0))
    return pl.pallas_call(
        body, name="attn_b_bwd", grid=(nb,),
        in_specs=[row, pl.BlockSpec((pad + s, B_KVX), lambda b: (0, 0)), row, row, row,
                  pl.BlockSpec((1, 8, B_ROWS), lambda b: (b, 0, 0)), base_spec,
                  pl.BlockSpec((1, HEADS), lambda b: (0, 0))],
        out_specs=[pl.BlockSpec((4, TQ, half), lambda b: (0, b, 0)),
                   pl.BlockSpec((s, 2 * LANES), lambda b: (0, 0)), base_spec,
                   pl.BlockSpec((8, LANES), lambda b: (0, 0))],
        out_shape=[jax.ShapeDtypeStruct((4, s, half), BF16), jax.ShapeDtypeStruct((s, 2 * LANES), BF16),
                   jax.ShapeDtypeStruct((HEADS, 1, B_WIDE), F32), jax.ShapeDtypeStruct((8, LANES), F32)],
        scratch_shapes=[pltpu.VMEM((B_KV_HEADS, 2 * B_WIN, B_ROWS), F32),
                        pltpu.VMEM((B_KV_HEADS, 2 * B_WIN, B_ROWS), F32),
                        pltpu.VMEM((pad + s, B_KVX), F32), pltpu.VMEM((8, B_ROWS), F32)],
        compiler_params=_params(("arbitrary",)),
    )(qb, kvx, gate, o, du, lse, base, sinks)


def _t5_bucket(rel):
    nb = T5_BUCKETS // 2
    max_exact = nb // 2
    ret = jnp.where(rel > 0, nb, 0)
    n = jnp.abs(rel)
    nf = jnp.maximum(n, 1).astype(jnp.float32)
    large = max_exact + (jnp.log(nf / max_exact) / math.log(T5_MAX_DIST / max_exact)
                         * (nb - max_exact)).astype(jnp.int32)
    large = jnp.minimum(large, nb - 1)
    return ret + jnp.where(n < max_exact, n, large)


def _a_offset_onehot():
    c = np.arange(A_WIN + TQ)
    dist = A_LEFT_CHUNKS * CHUNK + TQ - 1 - c
    idx = np.clip(dist, -A_REL_CLIP, A_REL_CLIP) + A_REL_CLIP
    onehot = np.zeros((A_WIN + TQ, 2 * A_REL_CLIP + 1), np.float32)
    onehot[c, idx] = 1.0
    return jnp.asarray(onehot)


def _b_offset_onehot():
    c = jnp.arange(B_WIN + TQ, dtype=jnp.int32)
    rel = c - (TQ - 1) - B_LEFT_CHUNKS * CHUNK
    return (_t5_bucket(rel)[:, None] == jnp.arange(T5_BUCKETS)[None, :]).astype(F32)


def _diag_rows(onehot, table):
    rows = jnp.dot(onehot, table.astype(F32), precision=lax.Precision.HIGHEST)
    return rows.T.reshape(HEADS, 1, onehot.shape[0])


def _diag_rows_grad(onehot, ddiag):
    return jnp.dot(ddiag.reshape(HEADS, onehot.shape[0]), onehot, precision=lax.Precision.HIGHEST).T


def _position():
    x, y, c = lax.axis_index("x"), lax.axis_index("y"), lax.axis_index("c")
    chips = [(1 - x, y), (x, 1 - y), (1 - x, 1 - y)]
    return x, y, c, chips


ANY = pl.BlockSpec(memory_space=pl.ANY)


def _allgather_hosted(shards, split):
    n = len(shards)

    def part(ref, t, half):
        if not split[t]:
            return ref
        rows = shards[t].shape[0] // 2
        return ref.at[pl.ds(half * rows, rows)]

    def copies(kind, ins, outs, sems):
        send_sems, recv_sems, pass_send, pass_recv, local_sems = sems
        x, y, c, chips = _position()
        mine = 2 * x + y
        if kind == "local":
            return [pltpu.make_async_copy(ins[t], outs[t].at[mine], local_sems.at[t]) for t in range(n)]
        made = []
        for t in range(n):
            for j, chip in enumerate(chips):
                theirs = 2 * chip[0] + chip[1]
                far = dict(send_sem=send_sems.at[3 * t + j], recv_sem=recv_sems.at[3 * t + j],
                           device_id=(chip[0], chip[1], c), device_id_type=MESH)
                near = dict(send_sem=pass_send.at[3 * t + j], recv_sem=pass_recv.at[3 * t + j],
                            device_id=(x, y, 1 - c), device_id_type=MESH)
                here = part(outs[t].at[theirs], t, c)
                if kind == "send":
                    made.append(pltpu.make_async_remote_copy(
                        src_ref=part(ins[t], t, c), dst_ref=part(outs[t].at[mine], t, c), **far))
                elif kind == "landed":
                    made.append(pltpu.make_async_remote_copy(src_ref=here, dst_ref=here, **far))
                elif not split[t]:
                    made.append(None)
                elif kind == "pass":
                    made.append(pltpu.make_async_remote_copy(src_ref=here, dst_ref=here, **near))
                else:
                    other = part(outs[t].at[theirs], t, 1 - c)
                    made.append(pltpu.make_async_remote_copy(src_ref=other, dst_ref=other, **near))
        return made

    def first(ins, outs, sems):
        for cp in copies("local", ins, outs, sems) + copies("send", ins, outs, sems):
            cp.start()

    def middle(ins, outs, sems):
        for got, cp in zip(copies("landed", ins, outs, sems), copies("pass", ins, outs, sems)):
            got.wait_recv()
            if cp is not None:
                cp.start()

    def last(ins, outs, sems):
        for cp in copies("passed", ins, outs, sems):
            if cp is not None:
                cp.wait_recv()
        for cp in copies("send", ins, outs, sems) + copies("pass", ins, outs, sems):
            if cp is not None:
                cp.wait_send()
        for cp in copies("local", ins, outs, sems):
            cp.wait()

    return _Hosted(shards, [jax.ShapeDtypeStruct((4,) + w.shape, w.dtype) for w in shards],
                   [pltpu.SemaphoreType.DMA((3 * n,))] * 4 + [pltpu.SemaphoreType.DMA((n,))],
                   first, middle, last)


def _allgather_routed(shards):
    n = len(shards)

    def piece(block_ref, t, c, quarter=None):
        half = shards[t].shape[0] // 2
        if quarter is None:
            return block_ref.at[pl.ds(c * half, half)]
        return block_ref.at[pl.ds(c * half + quarter * (half // 2), half // 2)]

    def copies(kind, ins, outs, sems):
        ici_send, ici_recv, pass_send, pass_recv, local_sems = sems
        x, y, c, chips = _position()
        mine = 2 * x + y
        if kind == "local":
            return [pltpu.make_async_copy(ins[t], outs[t].at[mine], local_sems.at[t]) for t in range(n)]
        ids = [2 * chip[0] + chip[1] for chip in chips]
        made = []
        for t in range(n):
            def ici(k, to):
                return dict(send_sem=ici_send.at[4 * t + k], recv_sem=ici_recv.at[4 * t + k],
                            device_id=(chips[to][0], chips[to][1], c), device_id_type=MESH)

            def d2d(k):
                return dict(send_sem=pass_send.at[4 * t + k], recv_sem=pass_recv.at[4 * t + k],
                            device_id=(x, y, 1 - c), device_id_type=MESH)

            def same(ref, where):
                return pltpu.make_async_remote_copy(src_ref=ref, dst_ref=ref, **where)

            if kind == "send":
                for k in range(2):
                    made.append(pltpu.make_async_remote_copy(
                        src_ref=piece(ins[t], t, c), dst_ref=piece(outs[t].at[mine], t, c), **ici(k, k)))
            elif kind == "landed":
                made += [same(piece(outs[t].at[ids[k]], t, c), ici(k, k)) for k in range(2)]
            elif kind == "forward":
                made.append(same(piece(outs[t].at[ids[0]], t, c, 0), ici(2, 1)))
                made.append(same(piece(outs[t].at[ids[1]], t, c, 1), ici(3, 0)))
            elif kind == "arrived":
                made.append(same(piece(outs[t].at[ids[2]], t, c, 0), ici(2, 1)))
                made.append(same(piece(outs[t].at[ids[2]], t, c, 1), ici(3, 0)))
            else:
                core = 1 - c if kind == "passed" else c
                if kind in ("pass halves", "passed"):
                    made += [same(piece(outs[t].at[ids[k]], t, core), d2d(k)) for k in range(2)]
                if kind in ("pass quarters", "passed"):
                    made += [same(piece(outs[t].at[ids[2]], t, core, k), d2d(2 + k)) for k in range(2)]
        return made

    def first(ins, outs, sems):
        for cp in copies("local", ins, outs, sems) + copies("send", ins, outs, sems):
            cp.start()

    def middle(ins, outs, sems):
        for got, onward, near in zip(copies("landed", ins, outs, sems), copies("forward", ins, outs, sems),
                                     copies("pass halves", ins, outs, sems)):
            got.wait_recv()
            near.start()
            onward.start()

    def last(ins, outs, sems):
        quarters = copies("pass quarters", ins, outs, sems)
        for got, near in zip(copies("arrived", ins, outs, sems), quarters):
            got.wait_recv()
            near.start()
        for cp in copies("passed", ins, outs, sems):
            cp.wait_recv()
        for cp in (copies("send", ins, outs, sems) + copies("forward", ins, outs, sems)
                   + copies("pass halves", ins, outs, sems) + quarters):
            cp.wait_send()
        for cp in copies("local", ins, outs, sems):
            cp.wait()

    return _Hosted(shards, [jax.ShapeDtypeStruct((4,) + w.shape, w.dtype) for w in shards],
                   [pltpu.SemaphoreType.DMA((4 * n,))] * 4 + [pltpu.SemaphoreType.DMA((n,))],
                   first, middle, last)


def _scatter_hosted(grads):
    n = len(grads)

    def copies(ins, outs, sems):
        send_sems, recv_sems = sems
        x, y, c, chips = _position()
        return [pltpu.make_async_remote_copy(
            src_ref=ins[t].at[2 * chip[0] + chip[1]], dst_ref=outs[t].at[j],
            send_sem=send_sems.at[3 * t + j], recv_sem=recv_sems.at[3 * t + j],
            device_id=(chip[0], chip[1], c), device_id_type=MESH)
            for t in range(n) for j, chip in enumerate(chips)]

    def first(ins, outs, sems):
        for cp in copies(ins, outs, sems):
            cp.start()

    def last(ins, outs, sems):
        for cp in copies(ins, outs, sems):
            cp.wait()

    return _Hosted(grads, [jax.ShapeDtypeStruct((3,) + g.shape[1:], g.dtype) for g in grads],
                   [pltpu.SemaphoreType.DMA((3 * n,))] * 2, first, None, last)


GATHER_PEERS = "x and y neighbours (same core) and the sibling core"
SCATTER_PEERS = "the same core of the three other chips"
EVERYONE = "the seven other devices"


def _run_on_sequencer(name, hosted, peers, collective_id):
    ins = [jax.new_ref(a, memory_space=pltpu.MemorySpace.HBM) for a in hosted.inputs]
    outs = [jax.empty_ref(shape, memory_space=pltpu.MemorySpace.HBM) for shape in hosted.out_shapes]

    @pl.kernel(mesh=plsc.ScalarSubcoreMesh(axis_name="sequencer", num_cores=1), name=name,
               scratch_types=tuple(hosted.sems), compiler_params=pltpu.CompilerParams(collective_id=collective_id))
    def launch(*sems):
        x, y, c, chips = _position()
        if peers == GATHER_PEERS:
            devices = [(chip[0], chip[1], c) for chip in chips[:2]] + [(x, y, 1 - c)]
        elif peers == SCATTER_PEERS:
            devices = [(chip[0], chip[1], c) for chip in chips]
        else:
            devices = [(x ^ (k >> 2), y ^ ((k >> 1) & 1), c ^ (k & 1)) for k in range(1, 8)]
        barrier = pltpu.get_barrier_semaphore()
        for device in devices:
            pl.semaphore_signal(barrier, inc=1, device_id=device, device_id_type=MESH)
        pl.semaphore_wait(barrier, len(devices))
        hosted.first(ins, outs, sems)
        if hosted.middle is not None:
            hosted.middle(ins, outs, sems)
        hosted.last(ins, outs, sems)

    launch()
    return [o[...] for o in outs]


def _run_alone(name, hosted):
    n_in = len(hosted.inputs)
    n_out = len(hosted.out_shapes)

    def body(*refs):
        ins, outs, sems = refs[:n_in], refs[n_in:n_in + n_out], refs[n_in + n_out:]
        hosted.first(ins, outs, sems)
        if hosted.middle is not None:
            hosted.middle(ins, outs, sems)
        hosted.last(ins, outs, sems)

    return pl.pallas_call(
        body, name=name, in_specs=[ANY] * n_in, out_specs=[ANY] * n_out, out_shape=hosted.out_shapes,
        scratch_shapes=hosted.sems)(*hosted.inputs)


def _gather_gain(shard):
    def body(in_ref, out_ref, send_sems, recv_sems):
        x, y, c, chips = _position()
        out_ref[2 * x + y] = in_ref[...]
        sends = [pltpu.make_async_remote_copy(
            src_ref=in_ref, dst_ref=out_ref.at[2 * x + y], send_sem=send_sems.at[j], recv_sem=recv_sems.at[j],
            device_id=(chip[0], chip[1], c), device_id_type=MESH) for j, chip in enumerate(chips)]
        for cp in sends:
            cp.start()
        for j, chip in enumerate(chips):
            pltpu.make_async_remote_copy(
                src_ref=in_ref, dst_ref=out_ref.at[2 * chip[0] + chip[1]], send_sem=send_sems.at[j],
                recv_sem=recv_sems.at[j], device_id=(chip[0], chip[1], c), device_id_type=MESH).wait_recv()
        for cp in sends:
            cp.wait_send()

    vmem = pl.BlockSpec(memory_space=pltpu.VMEM)
    return pl.pallas_call(
        body, name="gather_gain", in_specs=[vmem], out_specs=vmem,
        out_shape=jax.ShapeDtypeStruct((4,) + shard.shape, shard.dtype),
        scratch_shapes=[pltpu.SemaphoreType.DMA((3,))] * 2,
    )(shard)


def _swap_with_sibling(name, blocks):
    n = len(blocks)

    def body(*refs):
        ins, outs = refs[:n], refs[n:2 * n]
        send_sems, recv_sems = refs[2 * n:]
        x, y, c, _ = _position()
        sends = [pltpu.make_async_remote_copy(
            src_ref=ins[t], dst_ref=outs[t], send_sem=send_sems.at[t], recv_sem=recv_sems.at[t],
            device_id=(x, y, 1 - c), device_id_type=MESH) for t in range(n)]
        for cp in sends:
            cp.start()
        for cp in sends:
            cp.wait()

    return pl.pallas_call(
        body, name=name,
        in_specs=[ANY] * n, out_specs=[ANY] * n,
        out_shape=[jax.ShapeDtypeStruct(b.shape, b.dtype) for b in blocks],
        scratch_shapes=[pltpu.SemaphoreType.DMA((n,))] * 2,
    )(*blocks)


def _everyone_hosted(terms):
    nt = len(terms)

    def copies(kind, ins, outs, sems):
        send_sems, recv_sems, local_sems = sems
        x, y, c, _ = _position()
        me = 4 * x + 2 * y + c
        if kind == "local":
            return [pltpu.make_async_copy(ins[t], outs[t].at[me], local_sems.at[t]) for t in range(nt)]
        made = []
        for t in range(nt):
            for k in range(1, 8):
                peer = (x ^ (k >> 2), y ^ ((k >> 1) & 1), c ^ (k & 1))
                slot = me if kind == "send" else me ^ k
                made.append(pltpu.make_async_remote_copy(
                    src_ref=ins[t], dst_ref=outs[t].at[slot], send_sem=send_sems.at[7 * t + k - 1],
                    recv_sem=recv_sems.at[7 * t + k - 1], device_id=peer, device_id_type=MESH))
        return made

    def first(ins, outs, sems):
        for cp in copies("local", ins, outs, sems) + copies("send", ins, outs, sems):
            cp.start()

    def last(ins, outs, sems):
        for cp in copies("landed", ins, outs, sems):
            cp.wait_recv()
        for cp in copies("send", ins, outs, sems):
            cp.wait_send()
        for cp in copies("local", ins, outs, sems):
            cp.wait()

    return _Hosted(terms, [jax.ShapeDtypeStruct((8,) + a.shape, F32) for a in terms],
                   [pltpu.SemaphoreType.DMA((7 * nt,))] * 2 + [pltpu.SemaphoreType.DMA((nt,))], first, None, last)


def _small_step(partials, extras, ws, ms, vs, shard_of):
    n = len(partials)
    terms = list(partials) + list(extras)
    nt = len(terms)
    rows = [t for t in range(nt) if terms[t].shape[0] == 1]
    mats = [t for t in range(nt) if terms[t].shape[0] != 1]
    row_block = (8, max(terms[t].shape[1] for t in rows))
    assert len(rows) <= row_block[0]
    vmem = pl.BlockSpec(memory_space=pltpu.VMEM)

    def pack(*refs):
        packed = refs[-1]
        packed[...] = jnp.zeros_like(packed)
        for i, t in enumerate(rows):
            packed[i:i + 1, 0:terms[t].shape[1]] = refs[i][...]

    packed = pl.pallas_call(pack, name="small_pack", in_specs=[vmem] * len(rows), out_specs=vmem,
                            out_shape=jax.ShapeDtypeStruct(row_block, F32))(*[terms[t] for t in rows])
    slots = _run_on_sequencer("allgather_small", _everyone_hosted([packed] + [terms[t] for t in mats]),
                              EVERYONE, 2)

    def body(*refs):
        slot_refs, refs = refs[:len(slots)], refs[len(slots):]
        w_refs, refs = refs[:n], refs[n:]
        m_refs, refs = refs[:n], refs[n:]
        v_refs, outs = refs[:n], refs[n:]
        sums = []
        for ref in slot_refs:
            g = ref[0]
            for dev in range(1, 8):
                g = g + ref[dev]
            sums.append(g)
        chip = 2 * lax.axis_index("x") + lax.axis_index("y")
        for t in range(nt):
            if t in rows:
                i = rows.index(t)
                g = sums[0][i:i + 1, 0:terms[t].shape[1]]
            else:
                g = sums[1 + mats.index(t)]
            if t >= n:
                outs[4 * n + t - n][...] = g
                continue
            if shard_of[t]:
                width = ws[t].shape[-1]
                mine = jnp.zeros(ws[t].shape, F32)
                for s in range(4):
                    mine = jnp.where(chip == s, g[:, s * width:(s + 1) * width], mine)
                g = mine
            delta, mn, vn = _adamw_math(w_refs[t][...], g, m_refs[t][...], v_refs[t][...])
            outs[4 * t][...] = g
            outs[4 * t + 1][...] = delta
            outs[4 * t + 2][...] = mn
            outs[4 * t + 3][...] = vn

    out_shapes = []
    for t in range(n):
        out_shapes += [jax.ShapeDtypeStruct(ws[t].shape, F32)] * 4
    out_shapes += [jax.ShapeDtypeStruct(a.shape, F32) for a in extras]
    res = pl.pallas_call(
        body, name="small_step",
        in_specs=[vmem] * (len(slots) + 3 * n), out_specs=[vmem] * len(out_shapes), out_shape=out_shapes,
    )(*slots, *ws, *ms, *vs)
    return [res[4 * t:4 * t + 4] for t in range(n)], res[4 * n:4 * n + nt - n]


def _adamw_math(w, g, m, v):
    m = ADAM_B1 * m + (1.0 - ADAM_B1) * g
    v = ADAM_B2 * v + (1.0 - ADAM_B2) * (g * g)
    m_hat = m / (1.0 - ADAM_B1 ** ADAM_STEP)
    v_hat = v / (1.0 - ADAM_B2 ** ADAM_STEP)
    delta = -ADAM_LR * (m_hat / (jnp.sqrt(v_hat) + ADAM_EPS) + ADAM_WD * w)
    return delta, m, v


def _row_tile(rows):
    return 256 if rows % 256 == 0 else rows


def _sum_partials(name, own, recv, chip):
    rows, cols = own.shape[1:]
    tr = _row_tile(rows)

    def body(chip_ref, own_ref, recv_ref, o_ref):
        acc = own_ref[...]
        for j in range(3):
            acc = acc + recv_ref[j].astype(F32)
        o_ref[...] = acc

    return pl.pallas_call(
        body, name=name,
        grid_spec=pltpu.PrefetchScalarGridSpec(
            num_scalar_prefetch=1, grid=(rows // tr,),
            in_specs=[pl.BlockSpec((None, tr, cols), lambda i, chip_ref: (chip_ref[0], i, 0)),
                      pl.BlockSpec((3, tr, cols), lambda i, chip_ref: (0, i, 0))],
            out_specs=pl.BlockSpec((tr, cols), lambda i, chip_ref: (i, 0))),
        out_shape=jax.ShapeDtypeStruct((rows, cols), F32),
        compiler_params=_params(("parallel",)),
    )(chip.reshape(1).astype(jnp.int32), own, recv)


def _adamw(name, w, m, v, g_parts):
    rows, cols = w.shape
    tr = _row_tile(rows)
    n = len(g_parts)

    def body(w_ref, m_ref, v_ref, *refs):
        g_refs = refs[:n]
        go_ref, d_ref, mo_ref, vo_ref = refs[n:]
        g = g_refs[0][...]
        for r in g_refs[1:]:
            g = g + r[...]
        delta, mn, vn = _adamw_math(w_ref[...], g, m_ref[...], v_ref[...])
        go_ref[...] = g
        d_ref[...] = delta
        mo_ref[...] = mn
        vo_ref[...] = vn

    spec = pl.BlockSpec((tr, cols), lambda i: (i, 0))
    return pl.pallas_call(
        body, name=name, grid=(rows // tr,),
        in_specs=[spec] * (3 + n), out_specs=[spec] * 4,
        out_shape=[jax.ShapeDtypeStruct((rows, cols), F32)] * 4,
        compiler_params=_params(("parallel",)),
    )(w, m, v, *g_parts)


def _local_step(x, target, ga, wa_in, rel_bias, later_shards, gk, t5, gb, sinks, gf):
    s, d = x.shape
    tm = min(TM_DENSE, s)
    nt = s // tm
    half = d // 2
    row = pl.BlockSpec((tm, d), lambda i: (i, 0))
    whole = lambda shape: pl.BlockSpec(shape, lambda *_: (0,) * len(shape))

    n1, = _norm_fwd("norm_a", x, ga)
    zqkv = _matmul("proj_a_qkv", n1, wa_in, dims=NN, grid=(3, nt + 1), zero_axis=1,
                   a_spec=pl.BlockSpec((tm, d), lambda j, i: (jnp.maximum(i - 1, 0), 0)),
                   b_spec=pl.BlockSpec((None, d, d), lambda j, i: (j, 0, 0)),
                   o_spec=pl.BlockSpec((None, tm, d), lambda j, i: (j, i, 0)),
                   out_shape=(3, tm + s, d), out_dtype=BF16)
    gate_a = _matmul("proj_a_gate", n1, wa_in, dims=NN, grid=(nt,),
                     a_spec=row, b_spec=pl.BlockSpec((None, d, d), lambda i: (3, 0, 0)), o_spec=row,
                     out_shape=(s, d), out_dtype=F32)
    onehot_a = _a_offset_onehot()
    diag_a = _diag_rows(onehot_a, rel_bias)
    (o_a, u_a, lse_a), gathered = _attn_a_fwd(zqkv, gate_a, diag_a, hosted=_allgather_routed(later_shards))
    wa_out, wkv, wb_in, wb_out = gathered
    wa_out = wa_out.reshape(d, d)
    wkv = wkv.reshape(d, -1)
    wb_out = wb_out.reshape(d, d)
    h1, nk, n2 = _out_norms("out_a_norms", u_a, wa_out, x, jnp.concatenate([gk, gb], axis=0))
    kvw = wkv.shape[1]
    wkv_x = jnp.concatenate([wkv[:, (i // 2) * HEAD_DIM:(i // 2 + 1) * HEAD_DIM] for i in range(8)], axis=1)
    kvx = _matmul("proj_kv", nk, wkv_x, dims=NN, grid=(nt + 1,), zero_axis=0,
                  a_spec=pl.BlockSpec((tm, d), lambda i: (jnp.maximum(i - 1, 0), 0)), b_spec=whole((d, B_KVX)),
                  o_spec=pl.BlockSpec((tm, B_KVX), lambda i: (i, 0)), out_shape=(tm + s, B_KVX), out_dtype=BF16)
    qb = _matmul("proj_b_q", n2, wb_in, dims=NN, grid=(2, nt),
                 a_spec=pl.BlockSpec((tm, d), lambda j, i: (i, 0)),
                 b_spec=pl.BlockSpec((None, d, half), lambda j, i: (j, 0, 0)),
                 o_spec=pl.BlockSpec((tm, half), lambda j, i: (i, j)), out_shape=(s, d), out_dtype=BF16)
    gate_b = _matmul("proj_b_gate", n2, wb_in, dims=NN, grid=(2, nt),
                     a_spec=pl.BlockSpec((tm, d), lambda j, i: (i, 0)),
                     b_spec=pl.BlockSpec((None, d, half), lambda j, i: (2 + j, 0, 0)),
                     o_spec=pl.BlockSpec((tm, half), lambda j, i: (i, j)), out_shape=(s, d), out_dtype=F32)
    onehot_b = _b_offset_onehot()
    base_b = jnp.roll(_diag_rows(onehot_b, t5)[..., ::-1], TQ, axis=-1)
    o_b, u_b, lse_b = _attn_b_fwd(qb, kvx, gate_b, base_b, sinks)
    dh2, loss, d_gf = _out_loss_head(u_b, wb_out, h1, target, gf)

    du_b = _matmul("dout_b", dh2, wb_out, dims=NT, grid=(nt,), a_spec=row, b_spec=whole((d, d)), o_spec=row,
                   out_shape=(s, d), out_dtype=F32)
    d_wb_out = _matmul("dw_out_b", u_b, dh2, dims=TN, grid=(2,),
                       a_spec=whole((s, d)), b_spec=pl.BlockSpec((s, half), lambda j: (0, j)),
                       o_spec=pl.BlockSpec((d, half), lambda j: (0, j)),
                       out_shape=(d, d), out_dtype=F32, also_bf16=True)
    dz_b, dkv, dsum_b, dsinks = _attn_b_bwd(qb, kvx, gate_b, o_b, du_b, lse_b, base_b, sinks)
    ddiag_b = jnp.roll(dsum_b[..., ::-1], -1, axis=-1)
    d_wb_in = _matmul("dw_in_b", n2, dz_b, dims=TN, grid=(4,),
                      a_spec=whole((s, d)), b_spec=pl.BlockSpec((None, s, half), lambda j: (j, 0, 0)),
                      o_spec=pl.BlockSpec((None, d, half), lambda j: (j, 0, 0)),
                      out_shape=(4, d, half), out_dtype=F32, also_bf16=True)
    d_wkv = _matmul("dw_kv", nk, dkv, dims=TN, grid=(1,),
                    a_spec=whole((s, d)), b_spec=whole((s, kvw)), o_spec=whole((d, kvw)),
                    out_shape=(d, kvw), out_dtype=F32, also_bf16=True)
    dh1, d_gkb = _proj_norm_bwd("dproj_kv_b", h1, dh2, jnp.concatenate([gk, gb], axis=0),
                                [(dkv[None], wkv[None]), (dz_b, wb_in)])

    du_a = _matmul("dout_a", dh1, wa_out, dims=NT, grid=(nt,), a_spec=row, b_spec=whole((d, d)), o_spec=row,
                   out_shape=(s, d), out_dtype=F32)
    d_wa_out = _matmul("dw_out_a", u_a, dh1, dims=TN, grid=(2,),
                       a_spec=whole((s, d)), b_spec=pl.BlockSpec((s, half), lambda j: (0, j)),
                       o_spec=pl.BlockSpec((d, half), lambda j: (0, j)),
                       out_shape=(d, d), out_dtype=F32, also_bf16=True)
    early = dict(a_w_out=[g.reshape(4, d // 4, d) for g in d_wa_out],
                 kv_w=[g.reshape(4, d // 4, kvw) for g in d_wkv], b_w_in=list(d_wb_in),
                 b_w_out=[g.reshape(4, d // 4, d) for g in d_wb_out])
    early_recv = _run_on_sequencer("scatter_early", _scatter_hosted([early[n][1] for n in early]),
                                   SCATTER_PEERS, 3)
    (dz_a, ddiag_a), _ = _attn_a_bwd(zqkv, gate_a, o_a, du_a, lse_a, diag_a)
    d_wa_in = _matmul("dw_in_a", n1, dz_a, dims=TN, grid=(4, 2),
                      a_spec=whole((s, d)), b_spec=pl.BlockSpec((None, s, half), lambda j, h: (j, 0, h)),
                      o_spec=pl.BlockSpec((None, d, half), lambda j, h: (j, 0, h)),
                      out_shape=(4, d, d), out_dtype=F32, also_bf16=True)
    late_recv = _run_on_sequencer("scatter_a_w_in", _scatter_hosted([d_wa_in[1]]), SCATTER_PEERS, 0)
    grad_x, d_ga = _proj_norm_bwd("dproj_a", x, dh1, ga, [(dz_a, wa_in)])

    small = dict(a_norm=d_ga, kv_norm=d_gkb[0:1], b_norm=d_gkb[1:2], b_sinks=dsinks[0:1, :HEADS], final_norm=d_gf)
    small["by_offset"] = dict(a_rel_bias=(onehot_a, ddiag_a.reshape(HEADS, -1)),
                              t5_bias=(onehot_b, ddiag_b.reshape(HEADS, -1)))
    own = dict(a_w_in=d_wa_in[0], **{n: early[n][0] for n in early})
    received = dict(a_w_in=late_recv[0], **dict(zip(early, early_recv)))
    return loss, grad_x, small, own, received


SMALL = ("a_norm", "kv_norm", "b_norm", "b_sinks", "final_norm")
TABLES = ("a_rel_bias", "t5_bias")
BIG = ("a_w_in", "a_w_out", "kv_w", "b_w_in", "b_w_out")
ORDER = ("a_norm", "a_w_in", "a_rel_bias", "a_w_out", "kv_norm", "kv_w", "t5_bias", "b_norm", "b_w_in",
         "b_sinks", "b_w_out", "final_norm")


def kernel(x, a_norm, a_w_in, a_rel_bias, a_w_out, kv_norm, kv_w, t5_bias, b_norm, b_w_in, b_sinks, b_w_out, final_norm, loss_target, m_a_norm, m_a_w_in, m_a_rel_bias, m_a_w_out, m_kv_norm, m_kv_w, m_t5_bias, m_b_norm, m_b_w_in, m_b_sinks, m_b_w_out, m_final_norm, v_a_norm, v_a_w_in, v_a_rel_bias, v_a_w_out, v_kv_norm, v_kv_w, v_t5_bias, v_b_norm, v_b_w_in, v_b_sinks, v_b_w_out, v_final_norm):
    w = dict(a_norm=a_norm, a_w_in=a_w_in, a_rel_bias=a_rel_bias, a_w_out=a_w_out, kv_norm=kv_norm, kv_w=kv_w,
             t5_bias=t5_bias, b_norm=b_norm, b_w_in=b_w_in, b_sinks=b_sinks, b_w_out=b_w_out,
             final_norm=final_norm)
    m = dict(a_norm=m_a_norm, a_w_in=m_a_w_in, a_rel_bias=m_a_rel_bias, a_w_out=m_a_w_out, kv_norm=m_kv_norm,
             kv_w=m_kv_w, t5_bias=m_t5_bias, b_norm=m_b_norm, b_w_in=m_b_w_in, b_sinks=m_b_sinks,
             b_w_out=m_b_w_out, final_norm=m_final_norm)
    v = dict(a_norm=v_a_norm, a_w_in=v_a_w_in, a_rel_bias=v_a_rel_bias, a_w_out=v_a_w_out, kv_norm=v_kv_norm,
             kv_w=v_kv_w, t5_bias=v_t5_bias, b_norm=v_b_norm, b_w_in=v_b_w_in, b_sinks=v_b_sinks,
             b_w_out=v_b_w_out, final_norm=v_final_norm)
    d = D_MODEL
    chip = 2 * lax.axis_index("x") + lax.axis_index("y")

    shard2d = dict(a_w_in=a_w_in[0], a_w_out=a_w_out[0], kv_w=kv_w, b_w_in=b_w_in[0], b_w_out=b_w_out[0])

    wa_in, = _run_on_sequencer("allgather_first", _allgather_routed([shard2d["a_w_in"].astype(BF16)]),
                               GATHER_PEERS, 1)
    ga = _gather_gain(a_norm).reshape(1, d)

    loss, grad_x, small, own, received = _local_step(
        x[0], loss_target[0], ga, wa_in, a_rel_bias[0], [shard2d[n].astype(BF16) for n in BIG[1:]],
        kv_norm.reshape(1, d), t5_bias, b_norm, b_sinks, final_norm.reshape(1, d))

    out = {}
    as2d = lambda a: a.reshape(-1, a.shape[-1])
    small_res, (loss_sum, *offset_sums) = _small_step(
        [small[n] for n in SMALL], [loss] + [small["by_offset"][n][1] for n in TABLES],
        [as2d(w[n]) for n in SMALL], [as2d(m[n]) for n in SMALL], [as2d(v[n]) for n in SMALL],
        [n == "a_norm" for n in SMALL])
    for n, res in zip(SMALL, small_res):
        out[n] = [r.reshape(w[n].shape) for r in res]
    loss_out = loss_sum.reshape(())
    for n, summed in zip(TABLES, offset_sums):
        grad = _diag_rows_grad(small["by_offset"][n][0], summed)
        res = _adamw("adamw_" + n, as2d(w[n]), as2d(m[n]), as2d(v[n]), [grad])
        out[n] = [r.reshape(w[n].shape) for r in res]

    core_sums = [_sum_partials("sum_" + n, own[n], received[n], chip) for n in BIG]
    sibling_sums = (_swap_with_sibling("swap_last", core_sums[:1])
                    + _swap_with_sibling("swap_early", core_sums[1:]))

    for n, mine, theirs in zip(BIG, core_sums, sibling_sums):
        res = _adamw("adamw_" + n, shard2d[n], m[n].reshape(shard2d[n].shape), v[n].reshape(shard2d[n].shape),
                     [mine, theirs])
        out[n] = [r.reshape(w[n].shape) for r in res]

    grads = [out[n][0] for n in ORDER]
    deltas = [out[n][1] for n in ORDER]
    new_m = [out[n][2] for n in ORDER]
    new_v = [out[n][3] for n in ORDER]
    return (loss_out, grad_x[None], *grads, *deltas, *new_m, *new_v)
```

```python
import functools
import math

import jax
import jax.numpy as jnp
import numpy as np
from jax import lax
from jax.experimental import pallas as pl
from jax.experimental.pallas import tpu as pltpu
from jax.experimental.pallas import tpu_sc as plsc

F32 = jnp.float32
BF16 = jnp.bfloat16
MESH = pl.DeviceIdType.MESH

D_MODEL = 1024
HEADS = 16
HEAD_DIM = 64
CHUNK = 64
RMS_EPS = 1e-6
SCALE = HEAD_DIM ** -0.5
A_LEFT_CHUNKS = 8
A_REL_CLIP = 256
B_LEFT_CHUNKS = 2
B_KV_HEADS = 2
B_GROUP = HEADS // B_KV_HEADS
T5_BUCKETS = 32
T5_MAX_DIST = 128
ADAM_LR = 0.001
ADAM_B1 = 0.9
ADAM_B2 = 0.999
ADAM_EPS = 1e-08
ADAM_WD = 0.01
ADAM_STEP = 10

MASKED = -1e30
LANES = 128
TQ = 128
A_PAIRS = 2
A_PAIRS_FWD = 4
KB = 128
A_KBLOCKS = A_LEFT_CHUNKS * CHUNK // KB + 1
B_KBLOCKS = B_LEFT_CHUNKS * CHUNK // KB + 1
A_WIN = A_KBLOCKS * KB
B_WIN = B_KBLOCKS * KB
TM = 512
TM_DENSE = 1024
TM_PARTS = 512
VMEM_LIMIT = 56 * 1024 * 1024

NT = (((1,), (1,)), ((), ()))
TN = (((0,), (0,)), ((), ()))
NN = (((1,), (0,)), ((), ()))


def _params(sem=None):
    return pltpu.CompilerParams(dimension_semantics=sem, vmem_limit_bytes=VMEM_LIMIT)


class _Hosted:
    def __init__(self, inputs, out_shapes, sems, first, middle, last):
        self.inputs, self.out_shapes, self.sems = list(inputs), list(out_shapes), list(sems)
        self.first, self.middle, self.last = first, middle, last


def _call(body, *, name, grid, in_specs, out_specs, out_shape, args, scratch_shapes=(), sem=None, hosted=None):
    in_specs, out_specs, out_shape = list(in_specs), list(out_specs), list(out_shape)
    scratch_shapes = list(scratch_shapes)
    if hosted is None:
        out = pl.pallas_call(
            body, name=name, grid=grid, in_specs=in_specs, out_specs=out_specs, out_shape=out_shape,
            scratch_shapes=scratch_shapes, compiler_params=_params(sem))(*args)
        return list(out), []
    n_in, n_out, n_scr = len(in_specs), len(out_shape), len(scratch_shapes)
    h_in, h_out = len(hosted.inputs), len(hosted.out_shapes)
    total = int(np.prod(grid)) if grid else 1

    def wrapped(*refs):
        ins, refs = refs[:n_in], refs[n_in:]
        h_ins, refs = refs[:h_in], refs[h_in:]
        outs, refs = refs[:n_out], refs[n_out:]
        h_outs, refs = refs[:h_out], refs[h_out:]
        scr, h_sems = refs[:n_scr], refs[n_scr:]
        step = 0
        for axis, size in enumerate(grid):
            step = step * size + pl.program_id(axis)

        @pl.when(step == 0)
        def _():
            hosted.first(h_ins, h_outs, h_sems)

        body(*ins, *outs, *scr)
        if hosted.middle is not None:
            @pl.when(step == total // 2)
            def _():
                hosted.middle(h_ins, h_outs, h_sems)

        @pl.when(step == total - 1)
        def _():
            hosted.last(h_ins, h_outs, h_sems)

    out = pl.pallas_call(
        wrapped, name=name, grid=grid, in_specs=in_specs + [ANY] * h_in, out_specs=out_specs + [ANY] * h_out,
        out_shape=out_shape + hosted.out_shapes, scratch_shapes=scratch_shapes + hosted.sems,
        compiler_params=_params(("arbitrary",) * len(grid)))(*args, *hosted.inputs)
    return list(out[:n_out]), list(out[n_out:])


def _matmul(name, a, b, *, dims, grid, a_spec, b_spec, o_spec, out_shape, out_dtype,
            parts=1, resid=None, resid_spec=None, also_bf16=False, hosted=None, zero_axis=None):
    def body(*refs):
        if zero_axis is None:
            product(*refs)
        else:
            @pl.when(pl.program_id(zero_axis) == 0)
            def _():
                refs[2][...] = jnp.zeros_like(refs[2])

            @pl.when(pl.program_id(zero_axis) > 0)
            def _():
                product(*refs)

    def product(*refs):
        a_ref, b_ref = refs[:2]
        r_ref = refs[2] if resid is not None else None
        o_ref = refs[3] if resid is not None else refs[2]
        if parts == 1:
            prod = lax.dot_general(a_ref[...].astype(BF16), b_ref[...].astype(BF16), dims,
                                   preferred_element_type=F32)
        else:
            prod = None
            for part in range(parts):
                term = lax.dot_general(a_ref[part].astype(BF16), b_ref[part].astype(BF16), dims,
                                       preferred_element_type=F32)
                prod = term if prod is None else prod + term
        if resid is not None:
            prod = r_ref[...] + prod
        o_ref[...] = prod.astype(out_dtype)
        if also_bf16:
            refs[-1][...] = prod.astype(BF16)

    in_specs = [a_spec, b_spec]
    args = [a, b]
    if resid is not None:
        in_specs.append(resid_spec)
        args.append(resid)
    sem = ["parallel"] * len(grid)
    out_specs = [o_spec]
    out_shapes = [jax.ShapeDtypeStruct(out_shape, out_dtype)]
    if also_bf16:
        out_specs.append(o_spec)
        out_shapes.append(jax.ShapeDtypeStruct(out_shape, BF16))
    out, extra = _call(body, name=name, grid=grid, in_specs=in_specs, out_specs=out_specs, out_shape=out_shapes,
                       args=args, sem=tuple(sem), hosted=hosted)
    res = out[0] if not also_bf16 else tuple(out)
    return res if hosted is None else (res, extra)


def _rms_rows(x):
    return lax.rsqrt(jnp.mean(x * x, axis=-1, keepdims=True) + RMS_EPS)


def _norm_fwd(name, x, gains):
    s, d = x.shape
    n = gains.shape[0]

    def body(x_ref, g_ref, *o_refs):
        xv = x_ref[...]
        xh = xv * _rms_rows(xv)
        for i in range(n):
            o_refs[i][...] = (xh * g_ref[i:i + 1, :]).astype(BF16)

    row = pl.BlockSpec((TM, d), lambda i: (i, 0))
    return pl.pallas_call(
        body, name=name, grid=(s // TM,),
        in_specs=[row, pl.BlockSpec((n, d), lambda i: (0, 0))],
        out_specs=[row] * n,
        out_shape=[jax.ShapeDtypeStruct((s, d), BF16)] * n,
        compiler_params=_params(("parallel",)),
    )(x, gains)


def _proj_norm_bwd(name, x, dres, gains, branches):
    s, d = x.shape
    n = len(branches)
    tm = min(TM_PARTS, s)

    def body(x_ref, r_ref, g_ref, *refs):
        ab_refs, dx_ref, dg_ref = refs[:2 * n], refs[2 * n], refs[2 * n + 1]
        i = pl.program_id(0)
        xv = x_ref[...]
        r = _rms_rows(xv)
        xh = xv * r

        @pl.when(i == 0)
        def _():
            dg_ref[...] = jnp.zeros_like(dg_ref)

        a = None
        for j in range(n):
            a_ref, b_ref = ab_refs[2 * j], ab_refs[2 * j + 1]
            dn = None
            for part in range(a_ref.shape[0]):
                term = lax.dot_general(a_ref[part], b_ref[part], NT, preferred_element_type=F32)
                dn = term if dn is None else dn + term
            t = dn * g_ref[j:j + 1, :]
            a = t if a is None else a + t
            dg_ref[j:j + 1, :] += jnp.sum(dn * xh, axis=0, keepdims=True)
        dx_ref[...] = r_ref[...] + r * (a - xh * jnp.mean(xh * a, axis=-1, keepdims=True))

    row = pl.BlockSpec((tm, d), lambda i: (i, 0))
    small = pl.BlockSpec((n, d), lambda i: (0, 0))
    ab_specs, ab_args = [], []
    for a, b in branches:
        ab_specs += [pl.BlockSpec((a.shape[0], tm, a.shape[2]), lambda i: (0, i, 0)),
                     pl.BlockSpec(b.shape, lambda i: (0, 0, 0))]
        ab_args += [a, b]
    return pl.pallas_call(
        body, name=name, grid=(s // tm,),
        in_specs=[row, row, small] + ab_specs,
        out_specs=[row, small],
        out_shape=[jax.ShapeDtypeStruct((s, d), F32), jax.ShapeDtypeStruct((n, d), F32)],
        compiler_params=_params(("arbitrary",)),
    )(x, dres, gains, *ab_args)


def _out_norms(name, u, w_out, resid, gains):
    s, d = resid.shape
    n = gains.shape[0]
    tm = min(TM_DENSE, s)

    def body(u_ref, w_ref, r_ref, g_ref, h_ref, *o_refs):
        hv = r_ref[...] + jnp.dot(u_ref[...], w_ref[...], preferred_element_type=F32)
        h_ref[...] = hv
        hh = hv * _rms_rows(hv)
        for i in range(n):
            o_refs[i][...] = (hh * g_ref[i:i + 1, :]).astype(BF16)

    row = pl.BlockSpec((tm, d), lambda i: (i, 0))
    return pl.pallas_call(
        body, name=name, grid=(s // tm,),
        in_specs=[row, pl.BlockSpec((d, d), lambda i: (0, 0)), row, pl.BlockSpec((n, d), lambda i: (0, 0))],
        out_specs=[row] * (n + 1),
        out_shape=[jax.ShapeDtypeStruct((s, d), F32)] + [jax.ShapeDtypeStruct((s, d), BF16)] * n,
        compiler_params=_params(("parallel",)),
    )(u, w_out, resid, gains)


def _out_loss_head(u, w_out, resid, target, gain):
    s, d = resid.shape
    tm = min(TM_PARTS, s)

    def body(u_ref, w_ref, r_ref, t_ref, g_ref, dh_ref, loss_ref, dg_ref):
        i = pl.program_id(0)
        hv = r_ref[...] + jnp.dot(u_ref[...], w_ref[...], preferred_element_type=F32)
        r = _rms_rows(hv)
        hh = hv * r
        g = g_ref[...]
        err = hh * g - t_ref[...]
        part = 0.5 * jnp.sum(jnp.sum(err * err, axis=-1, keepdims=True) * (1.0 / d), axis=0, keepdims=True)
        dy = err * (1.0 / d)
        a = dy * g
        dh_ref[...] = r * (a - hh * jnp.mean(hh * a, axis=-1, keepdims=True))
        dg = jnp.sum(dy * hh, axis=0, keepdims=True)

        @pl.when(i == 0)
        def _():
            loss_ref[...] = part
            dg_ref[...] = dg

        @pl.when(i > 0)
        def _():
            loss_ref[...] += part
            dg_ref[...] += dg

    row = pl.BlockSpec((tm, d), lambda i: (i, 0))
    return pl.pallas_call(
        body, name="out_b_loss_head", grid=(s // tm,),
        in_specs=[row, pl.BlockSpec((d, d), lambda i: (0, 0)), row, row, pl.BlockSpec((1, d), lambda i: (0, 0))],
        out_specs=[row, pl.BlockSpec((1, 1), lambda i: (0, 0)), pl.BlockSpec((1, d), lambda i: (0, 0))],
        out_shape=[jax.ShapeDtypeStruct((s, d), F32), jax.ShapeDtypeStruct((1, 1), F32),
                   jax.ShapeDtypeStruct((1, d), F32)],
        compiler_params=_params(("arbitrary",)),
    )(u, w_out, resid, target, gain)


def _silu_parts(g):
    sig = jax.nn.sigmoid(g)
    return g * sig, sig * (1.0 + g * (1.0 - sig))


def _lane_lo(rows):
    return lax.broadcasted_iota(jnp.int32, (rows, LANES), 1) < HEAD_DIM


def _stack_pair(x):
    lo = _lane_lo(x.shape[0])
    zero = jnp.zeros_like(x)
    return jnp.concatenate([jnp.where(lo, x, zero), jnp.where(lo, zero, x)], axis=0)


def _unstack_pair(y, w):
    return jnp.where(_lane_lo(w), y[:w], y[w:])


def _block_valid(b, left_blocks, width):
    col = lax.broadcasted_iota(jnp.int32, (1, 2 * width), 1)
    col = jnp.where(col >= width, col - width, col)
    return (col // KB + (b - left_blocks)) >= 0


def _toeplitz_tile(diag_row, width, left_chunks):
    wide = width + TQ
    rolled = pltpu.roll(jnp.broadcast_to(diag_row, (TQ, wide)), 1, 1, stride=1, stride_axis=0)
    i = lax.broadcasted_iota(jnp.int32, (TQ, width), 0) // CHUNK
    j = lax.broadcasted_iota(jnp.int32, (TQ, width), 1) // CHUNK
    dc = i + left_chunks - j
    return jnp.where((dc >= 0) & (dc <= left_chunks), rolled[:, TQ:], MASKED)


def _toeplitz_sum(tile, width):
    flip = (lax.broadcasted_iota(jnp.int32, (TQ, TQ), 0) + lax.broadcasted_iota(jnp.int32, (TQ, TQ), 1)
            == TQ - 1).astype(F32)
    reversed_rows = jnp.dot(flip, tile, precision=lax.Precision.HIGHEST, preferred_element_type=F32)
    padded = jnp.concatenate([reversed_rows, jnp.zeros((TQ, TQ), F32)], axis=1)
    rolled = pltpu.roll(padded, 0, 1, stride=1, stride_axis=0)
    return jnp.sum(rolled, axis=0, keepdims=True)


def _softmax_pair(sc, w, sink=None):
    ps, inv, lses = [], [], []
    for e in range(2):
        sh = sc[:, e * w:(e + 1) * w]
        m = jnp.max(sh, axis=-1, keepdims=True)
        if sink is not None:
            m = jnp.maximum(m, sink[e])
        ex = jnp.exp(sh - m)
        l = jnp.sum(ex, axis=-1, keepdims=True)
        if sink is not None:
            l = l + jnp.exp(sink[e] - m)
        ps.append(ex.astype(BF16))
        inv.append(1.0 / l)
        lses.append(m + jnp.log(l))
    return jnp.concatenate(ps, axis=-1), inv, lses


def _softmax_pair_bwd(sc, dp, lse, delta, w):
    ps, dss = [], []
    for e in range(2):
        p = jnp.exp(sc[:, e * w:(e + 1) * w] - lse[e])
        ps.append(p)
        dss.append(p * (dp[:, e * w:(e + 1) * w] - delta[e]))
    return jnp.concatenate(ps, axis=-1), jnp.concatenate(dss, axis=-1)


def _pair_rowsums(x, lo):
    zero = jnp.zeros_like(x)
    return (jnp.sum(jnp.where(lo, x, zero), axis=-1, keepdims=True),
            jnp.sum(jnp.where(lo, zero, x), axis=-1, keepdims=True))


def _a_qkv_specs(rows, pad, pw):
    return [pl.BlockSpec((None, TQ, pw), lambda p, b: (0, b + pad // TQ, p)),
            pl.BlockSpec((None, rows, pw), lambda p, b: (1, 0, p)),
            pl.BlockSpec((None, rows, pw), lambda p, b: (2, 0, p))]


def _window(ref, b, pad, win, lanes):
    start = pl.multiple_of(b * TQ + pad - (win - TQ), KB)
    return ref[pl.ds(start, win), lanes]


def _attn_a_fwd(zqkv, g, diag, hosted=None):
    s = g.shape[0]
    pad = zqkv.shape[1] - s
    nb = s // TQ
    left = A_KBLOCKS - 1
    pairs = A_PAIRS_FWD
    pw = pairs * LANES
    wide = A_WIN + TQ

    def body(q_ref, k_ref, v_ref, g_ref, diag_ref, o_ref, u_ref, lse_ref, bias_scr):
        b = pl.program_id(1)

        @pl.when(b == 0)
        def _():
            for hh in range(2 * pairs):
                bias_scr[hh // 2, :, (hh % 2) * A_WIN:(hh % 2 + 1) * A_WIN] = _toeplitz_tile(
                    diag_ref[hh], A_WIN, A_LEFT_CHUNKS)

        def step(first_blocks):
            lo = _lane_lo(TQ)
            for pp in range(pairs):
                ln = slice(pp * LANES, (pp + 1) * LANES)
                kcat = _stack_pair(_window(k_ref, b, pad, A_WIN, ln))
                vcat = _stack_pair(_window(v_ref, b, pad, A_WIN, ln))
                sc = lax.dot_general(q_ref[:, ln] * SCALE, kcat, NT, preferred_element_type=F32) + bias_scr[pp]
                if first_blocks:
                    sc = jnp.where(_block_valid(b, left, A_WIN), sc, MASKED)
                p, inv, lses = _softmax_pair(sc, A_WIN)
                ov = jnp.dot(p, vcat, preferred_element_type=F32) * jnp.where(lo, inv[0], inv[1])
                o_ref[:, ln] = ov
                lse_ref[pp] = jnp.where(lo, lses[0], lses[1])
                sg, _ = _silu_parts(g_ref[:, ln])
                u_ref[:, ln] = (ov * sg).astype(BF16)

        @pl.when(b < left)
        def _():
            step(True)

        @pl.when(b >= left)
        def _():
            step(False)

    tile = pl.BlockSpec((TQ, pw), lambda p, b: (b, p))
    return _call(
        body, name="attn_a_fwd", grid=(HEADS // 2 // pairs, nb),
        in_specs=_a_qkv_specs(pad + s, pad, pw) + [
            tile, pl.BlockSpec((2 * pairs, 1, wide), lambda p, b: (p, 0, 0))],
        out_specs=[tile, tile, pl.BlockSpec((pairs, TQ, LANES), lambda p, b: (p, b, 0))],
        out_shape=[jax.ShapeDtypeStruct((s, D_MODEL), F32), jax.ShapeDtypeStruct((s, D_MODEL), BF16),
                   jax.ShapeDtypeStruct((HEADS // 2, s, LANES), F32)],
        scratch_shapes=[pltpu.VMEM((pairs, TQ, 2 * A_WIN), F32)],
        sem=("parallel", "arbitrary"), hosted=hosted,
        args=(zqkv, zqkv, zqkv, g, diag))


def _attn_a_bwd(zqkv, g, o, du, lse, diag, hosted=None):
    s = g.shape[0]
    pad = zqkv.shape[1] - s
    nb = s // TQ
    left = A_KBLOCKS - 1
    pw = A_PAIRS * LANES
    wide = A_WIN + TQ

    def body(q_ref, k_ref, v_ref, g_ref, o_ref, du_ref, lse_ref, diag_ref, dz_ref, ddiag_ref,
             bias_scr, dbias_acc, dk_acc, dv_acc):
        b = pl.program_id(1)

        @pl.when(b == 0)
        def _():
            for hh in range(2 * A_PAIRS):
                bias_scr[hh // 2, :, (hh % 2) * A_WIN:(hh % 2 + 1) * A_WIN] = _toeplitz_tile(
                    diag_ref[hh], A_WIN, A_LEFT_CHUNKS)
            dbias_acc[...] = jnp.zeros_like(dbias_acc)
            dk_acc[...] = jnp.zeros_like(dk_acc)
            dv_acc[...] = jnp.zeros_like(dv_acc)

        def step(first_blocks):
            lo = _lane_lo(TQ)
            upper = lax.broadcasted_iota(jnp.int32, (LANES, A_WIN), 0) < HEAD_DIM
            rows = pl.ds(pl.multiple_of(b * TQ, TQ), TQ)
            sg, dsg = _silu_parts(g_ref[...])
            duv = du_ref[...]
            ov = o_ref[...]
            do = duv * sg
            dz_ref[3, rows, :] = (duv * ov * dsg).astype(BF16)
            do_o = do * ov
            do_bf = do.astype(BF16)
            for pp in range(A_PAIRS):
                ln = slice(pp * LANES, (pp + 1) * LANES)
                q = q_ref[:, ln] * SCALE
                kcat = _stack_pair(_window(k_ref, b, pad, A_WIN, ln))
                vcat = _stack_pair(_window(v_ref, b, pad, A_WIN, ln))
                sc = lax.dot_general(q, kcat, NT, preferred_element_type=F32) + bias_scr[pp]
                if first_blocks:
                    sc = jnp.where(_block_valid(b, left, A_WIN), sc, MASKED)
                lse_t = lse_ref[pp]
                dp = lax.dot_general(do_bf[:, ln], vcat, NT, preferred_element_type=F32)
                p, ds = _softmax_pair_bwd(sc, dp, (lse_t[:, 0:1], lse_t[:, HEAD_DIM:HEAD_DIM + 1]),
                                          _pair_rowsums(do_o[:, ln], lo), A_WIN)
                dbias_acc[pp] += ds
                dsb = ds.astype(BF16)
                dz_ref[0, rows, ln] = (jnp.dot(dsb, kcat, preferred_element_type=F32) * SCALE).astype(BF16)
                dkt = lax.dot_general(q, dsb, TN, preferred_element_type=F32)
                dvt = lax.dot_general(do_bf[:, ln], p.astype(BF16), TN, preferred_element_type=F32)
                dkt = jnp.where(upper, dkt[:, :A_WIN], dkt[:, A_WIN:])
                dvt = jnp.where(upper, dvt[:, :A_WIN], dvt[:, A_WIN:])
                for t in range(A_KBLOCKS):
                    blk = b + (pad // KB - left + t)
                    dk_acc[blk, ln, :] += dkt[:, t * KB:(t + 1) * KB]
                    dv_acc[blk, ln, :] += dvt[:, t * KB:(t + 1) * KB]

        @pl.when(b < left)
        def _():
            step(True)

        @pl.when(b >= left)
        def _():
            step(False)

        @pl.when(b == nb - 1)
        def _():
            for kb in range(s // KB):
                dz_ref[1, kb * KB:(kb + 1) * KB, :] = dk_acc[pad // KB + kb].T.astype(BF16)
                dz_ref[2, kb * KB:(kb + 1) * KB, :] = dv_acc[pad // KB + kb].T.astype(BF16)
            for hh in range(2 * A_PAIRS):
                ddiag_ref[hh] = _toeplitz_sum(
                    dbias_acc[hh // 2, :, (hh % 2) * A_WIN:(hh % 2 + 1) * A_WIN], A_WIN)

    tile = pl.BlockSpec((TQ, pw), lambda p, b: (b, p))
    diag_spec = pl.BlockSpec((2 * A_PAIRS, 1, wide), lambda p, b: (p, 0, 0))
    return _call(
        body, name="attn_a_bwd", grid=(HEADS // 2 // A_PAIRS, nb),
        in_specs=_a_qkv_specs(pad + s, pad, pw) + [
            tile, tile, tile, pl.BlockSpec((A_PAIRS, TQ, LANES), lambda p, b: (p, b, 0)), diag_spec],
        out_specs=[pl.BlockSpec((4, s, pw), lambda p, b: (0, 0, p)), diag_spec],
        out_shape=[jax.ShapeDtypeStruct((4, s, D_MODEL), BF16),
                   jax.ShapeDtypeStruct((HEADS, 1, wide), F32)],
        scratch_shapes=[pltpu.VMEM((A_PAIRS, TQ, 2 * A_WIN), F32), pltpu.VMEM((A_PAIRS, TQ, 2 * A_WIN), F32),
                        pltpu.VMEM(((pad + s) // KB, pw, KB), F32), pltpu.VMEM(((pad + s) // KB, pw, KB), F32)],
        sem=("parallel", "arbitrary"), hosted=hosted,
        args=(zqkv, zqkv, zqkv, g, o, du, lse, diag))


B_STACK = B_GROUP // 2
B_KVX = 4 * LANES
B_ROWS = B_STACK * TQ
B_WIDE = B_WIN + TQ


def _b_head_place(h):
    return h // B_GROUP, (h % B_GROUP) // 2, h % 2


def _toeplitz_tile_t(base_row, width, left_chunks):
    wide = width + TQ
    rolled = pltpu.roll(jnp.broadcast_to(base_row, (width, wide)), 0, 1, stride=1, stride_axis=0)
    j = lax.broadcasted_iota(jnp.int32, (width, TQ), 0) // CHUNK
    i = lax.broadcasted_iota(jnp.int32, (width, TQ), 1) // CHUNK
    dc = i + left_chunks - j
    return jnp.where((dc >= 0) & (dc <= left_chunks), rolled[:, :TQ], MASKED)


def _toeplitz_sum_t(tile_t, width):
    flip = (lax.broadcasted_iota(jnp.int32, (width, width), 0) + lax.broadcasted_iota(jnp.int32, (width, width), 1)
            == width - 1).astype(F32)
    reversed_rows = jnp.dot(flip, tile_t, precision=lax.Precision.HIGHEST, preferred_element_type=F32)
    padded = jnp.concatenate([reversed_rows, jnp.zeros((width, width), F32)], axis=1)
    rolled = pltpu.roll(padded, 0, 1, stride=1, stride_axis=0)
    return jnp.sum(rolled, axis=0, keepdims=True)


def _b_build_bias(base_ref, bias_scr):
    for h in range(HEADS):
        gi, pr, e = _b_head_place(h)
        bias_scr[gi, e * B_WIN:(e + 1) * B_WIN, pr * TQ:(pr + 1) * TQ] = _toeplitz_tile_t(
            base_ref[h], B_WIN, B_LEFT_CHUNKS)


def _b_stack(x, gi):
    return jnp.concatenate(
        [x[:, (B_STACK * gi + pr) * LANES:(B_STACK * gi + pr + 1) * LANES] for pr in range(B_STACK)], axis=0)


def _b_sink_rows(sink_ref, gi):
    block = lax.broadcasted_iota(jnp.int32, (1, B_ROWS), 1) // TQ
    rows = []
    for e in range(2):
        row = jnp.zeros((1, B_ROWS), F32)
        for pr in range(B_STACK):
            h = B_GROUP * gi + 2 * pr + e
            row = jnp.where(block == pr, sink_ref[0:1, h:h + 1], row)
        rows.append(row)
    return rows


def _b_scores_t(q_ref, kvv, bias_scr, gi, b, left, first_blocks):
    kcat = _stack_pair(kvv[:, gi * LANES:(gi + 1) * LANES])
    vcat = _stack_pair(kvv[:, (B_KV_HEADS + gi) * LANES:(B_KV_HEADS + gi + 1) * LANES])
    qs = _b_stack(q_ref, gi) * SCALE
    sc = lax.dot_general(kcat, qs, NT, preferred_element_type=F32) + bias_scr[gi]
    if first_blocks:
        row = lax.broadcasted_iota(jnp.int32, (2 * B_WIN, 1), 0)
        row = jnp.where(row >= B_WIN, row - B_WIN, row)
        sc = jnp.where((row // KB + (b - left)) >= 0, sc, MASKED)
    return kcat, vcat, qs, sc


def _attn_b_fwd(qb, kvx, gate, base, sinks):
    s = qb.shape[0]
    pad = kvx.shape[0] - s
    nb = s // TQ
    left = B_KBLOCKS - 1

    def body(q_ref, kv_ref, g_ref, base_ref, sink_ref, o_ref, u_ref, lse_ref, bias_scr):
        b = pl.program_id(0)

        @pl.when(b == 0)
        def _():
            _b_build_bias(base_ref, bias_scr)

        def step(first_blocks):
            kvv = _window(kv_ref, b, pad, B_WIN, slice(None))
            upper = lax.broadcasted_iota(jnp.int32, (LANES, B_ROWS), 0) < HEAD_DIM
            lse_rows = []
            for gi in range(B_KV_HEADS):
                kcat, vcat, qs, sc = _b_scores_t(q_ref, kvv, bias_scr, gi, b, left, first_blocks)
                sink = _b_sink_rows(sink_ref, gi)
                ps, inv = [], []
                for e in range(2):
                    sh = sc[e * B_WIN:(e + 1) * B_WIN]
                    m = jnp.maximum(jnp.max(sh, axis=0, keepdims=True), sink[e])
                    ex = jnp.exp(sh - m)
                    l = jnp.sum(ex, axis=0, keepdims=True) + jnp.exp(sink[e] - m)
                    ps.append(ex.astype(BF16))
                    inv.append(1.0 / l)
                    lse_rows.append(m + jnp.log(l))
                pt = jnp.concatenate(ps, axis=0)
                ot = lax.dot_general(vcat, pt, TN, preferred_element_type=F32) * jnp.where(upper, inv[0], inv[1])
                ov = ot.T
                for pr in range(B_STACK):
                    pair = B_STACK * gi + pr
                    o_ref[:, pair * LANES:(pair + 1) * LANES] = ov[pr * TQ:(pr + 1) * TQ]
            lse_ref[0] = jnp.concatenate(lse_rows + [jnp.zeros((8 - len(lse_rows), B_ROWS), F32)], axis=0)
            sg, _ = _silu_parts(g_ref[...])
            u_ref[...] = (o_ref[...] * sg).astype(BF16)

        @pl.when(b < left)
        def _():
            step(True)

        @pl.when(b >= left)
        def _():
            step(False)

    row = pl.BlockSpec((TQ, D_MODEL), lambda b: (b, 0))
    return pl.pallas_call(
        body, name="attn_b_fwd", grid=(nb,),
        in_specs=[row, pl.BlockSpec((pad + s, B_KVX), lambda b: (0, 0)), row,
                  pl.BlockSpec((HEADS, 1, B_WIDE), lambda b: (0, 0, 0)), pl.BlockSpec((1, HEADS), lambda b: (0, 0))],
        out_specs=[row, row, pl.BlockSpec((1, 8, B_ROWS), lambda b: (b, 0, 0))],
        out_shape=[jax.ShapeDtypeStruct((s, D_MODEL), F32), jax.ShapeDtypeStruct((s, D_MODEL), BF16),
                   jax.ShapeDtypeStruct((nb, 8, B_ROWS), F32)],
        scratch_shapes=[pltpu.VMEM((B_KV_HEADS, 2 * B_WIN, B_ROWS), F32)],
        compiler_params=_params(("arbitrary",)),
    )(qb, kvx, gate, base, sinks)


def _attn_b_bwd(qb, kvx, gate, o, du, lse, base, sinks):
    s = qb.shape[0]
    pad = kvx.shape[0] - s
    nb = s // TQ
    left = B_KBLOCKS - 1
    half = D_MODEL // 2

    def body(q_ref, kv_ref, g_ref, o_ref, du_ref, lse_ref, base_ref, sink_ref, dz_ref, dkv_ref, dsum_ref,
             dsink_ref, bias_scr, dbias_acc, dkv_acc, dsink_acc):
        b = pl.program_id(0)

        @pl.when(b == 0)
        def _():
            _b_build_bias(base_ref, bias_scr)
            dbias_acc[...] = jnp.zeros_like(dbias_acc)
            dkv_acc[...] = jnp.zeros_like(dkv_acc)
            dsink_acc[...] = jnp.zeros_like(dsink_acc)

        def step(first_blocks):
            kvv = _window(kv_ref, b, pad, B_WIN, slice(None))
            sg, dsg = _silu_parts(g_ref[...])
            duv = du_ref[...]
            ov = o_ref[...]
            do = duv * sg
            dgate = (duv * ov * dsg).astype(BF16)
            dz_ref[2] = dgate[:, :half]
            dz_ref[3] = dgate[:, half:]
            do_o = do * ov
            do_bf = do.astype(BF16)
            lse_all = lse_ref[0]
            dsink_rows = []
            for gi in range(B_KV_HEADS):
                kcat, vcat, qs, sc = _b_scores_t(q_ref, kvv, bias_scr, gi, b, left, first_blocks)
                dos = _b_stack(do_bf, gi)
                doo_t = _b_stack(do_o, gi).T
                delta = (jnp.sum(doo_t[:HEAD_DIM], axis=0, keepdims=True),
                         jnp.sum(doo_t[HEAD_DIM:], axis=0, keepdims=True))
                sink = _b_sink_rows(sink_ref, gi)
                dp = lax.dot_general(vcat, dos, NT, preferred_element_type=F32)
                ps, dss = [], []
                for e in range(2):
                    lse_e = lse_all[2 * gi + e:2 * gi + e + 1]
                    delta_e = delta[e]
                    p = jnp.exp(sc[e * B_WIN:(e + 1) * B_WIN] - lse_e)
                    ps.append(p.astype(BF16))
                    dss.append(p * (dp[e * B_WIN:(e + 1) * B_WIN] - delta_e))
                    dsink_rows.append(-jnp.exp(sink[e] - lse_e) * delta_e)
                ds = jnp.concatenate(dss, axis=0)
                dbias_acc[gi] += ds
                dsb = ds.astype(BF16)
                dq = (lax.dot_general(kcat, dsb, TN, preferred_element_type=F32) * SCALE).T.astype(BF16)
                for pr in range(B_STACK):
                    dz_ref[gi, :, pr * LANES:(pr + 1) * LANES] = dq[pr * TQ:(pr + 1) * TQ]
                dk = _unstack_pair(jnp.dot(dsb, qs, preferred_element_type=F32), B_WIN)
                dv = _unstack_pair(jnp.dot(jnp.concatenate(ps, axis=0), dos, preferred_element_type=F32), B_WIN)
                krows = pl.ds(pl.multiple_of(b * TQ + pad - (B_WIN - TQ), KB), B_WIN)
                dkv_acc[krows, gi * LANES:(gi + 1) * LANES] += dk
                dkv_acc[krows, (B_KV_HEADS + gi) * LANES:(B_KV_HEADS + gi + 1) * LANES] += dv
            dsink_acc[...] += jnp.concatenate(
                dsink_rows + [jnp.zeros((8 - len(dsink_rows), B_ROWS), F32)], axis=0)

        @pl.when(b < left)
        def _():
            step(True)

        @pl.when(b >= left)
        def _():
            step(False)

        @pl.when(b == nb - 1)
        def _():
            lo_s = _lane_lo(s)
            for which in range(2):
                folded = []
                for gi in range(B_KV_HEADS):
                    part = dkv_acc[pad:pad + s, (which * B_KV_HEADS + gi) * LANES:(which * B_KV_HEADS + gi + 1) * LANES]
                    folded.append(part + pltpu.roll(part, HEAD_DIM, 1))
                dkv_ref[:, which * LANES:(which + 1) * LANES] = jnp.where(lo_s, folded[0], folded[1]).astype(BF16)
            lane8 = lax.broadcasted_iota(jnp.int32, dsink_ref.shape, 1)
            tot = jnp.zeros(dsink_ref.shape, F32)
            for h in range(HEADS):
                gi, pr, e = _b_head_place(h)
                dsum_ref[h] = _toeplitz_sum_t(
                    dbias_acc[gi, e * B_WIN:(e + 1) * B_WIN, pr * TQ:(pr + 1) * TQ], B_WIN)
                per_query = dsink_acc[2 * gi + e:2 * gi + e + 1, pr * TQ:(pr + 1) * TQ]
                tot = jnp.where(lane8 == h, jnp.sum(per_query, axis=1, keepdims=True), tot)
            dsink_ref[...] = tot

    row = pl.BlockSpec((TQ, D_MODEL), lambda b: (b, 0))
    base_spec = pl.BlockSpec((HEADS, 1, B_WIDE), lambda b: (0, 0, 0))
    return pl.pallas_call(
        body, name="attn_b_bwd", grid=(nb,),
        in_specs=[row, pl.BlockSpec((pad + s, B_KVX), lambda b: (0, 0)), row, row, row,
                  pl.BlockSpec((1, 8, B_ROWS), lambda b: (b, 0, 0)), base_spec,
                  pl.BlockSpec((1, HEADS), lambda b: (0, 0))],
        out_specs=[pl.BlockSpec((4, TQ, half), lambda b: (0, b, 0)),
                   pl.BlockSpec((s, 2 * LANES), lambda b: (0, 0)), base_spec,
                   pl.BlockSpec((8, LANES), lambda b: (0, 0))],
        out_shape=[jax.ShapeDtypeStruct((4, s, half), BF16), jax.ShapeDtypeStruct((s, 2 * LANES), BF16),
                   jax.ShapeDtypeStruct((HEADS, 1, B_WIDE), F32), jax.ShapeDtypeStruct((8, LANES), F32)],
        scratch_shapes=[pltpu.VMEM((B_KV_HEADS, 2 * B_WIN, B_ROWS), F32),
                        pltpu.VMEM((B_KV_HEADS, 2 * B_WIN, B_ROWS), F32),
                        pltpu.VMEM((pad + s, B_KVX), F32), pltpu.VMEM((8, B_ROWS), F32)],
        compiler_params=_params(("arbitrary",)),
    )(qb, kvx, gate, o, du, lse, base, sinks)


def _t5_bucket(rel):
    nb = T5_BUCKETS // 2
    max_exact = nb // 2
    ret = jnp.where(rel > 0, nb, 0)
    n = jnp.abs(rel)
    nf = jnp.maximum(n, 1).astype(jnp.float32)
    large = max_exact + (jnp.log(nf / max_exact) / math.log(T5_MAX_DIST / max_exact)
                         * (nb - max_exact)).astype(jnp.int32)
    large = jnp.minimum(large, nb - 1)
    return ret + jnp.where(n < max_exact, n, large)


def _a_offset_onehot():
    c = np.arange(A_WIN + TQ)
    dist = A_LEFT_CHUNKS * CHUNK + TQ - 1 - c
    idx = np.clip(dist, -A_REL_CLIP, A_REL_CLIP) + A_REL_CLIP
    onehot = np.zeros((A_WIN + TQ, 2 * A_REL_CLIP + 1), np.float32)
    onehot[c, idx] = 1.0
    return jnp.asarray(onehot)


def _b_offset_onehot():
    c = jnp.arange(B_WIN + TQ, dtype=jnp.int32)
    rel = c - (TQ - 1) - B_LEFT_CHUNKS * CHUNK
    return (_t5_bucket(rel)[:, None] == jnp.arange(T5_BUCKETS)[None, :]).astype(F32)


def _diag_rows(onehot, table):
    rows = jnp.dot(onehot, table.astype(F32), precision=lax.Precision.HIGHEST)
    return rows.T.reshape(HEADS, 1, onehot.shape[0])


def _diag_rows_grad(onehot, ddiag):
    return jnp.dot(ddiag.reshape(HEADS, onehot.shape[0]), onehot, precision=lax.Precision.HIGHEST).T


def _position():
    x, y, c = lax.axis_index("x"), lax.axis_index("y"), lax.axis_index("c")
    chips = [(1 - x, y), (x, 1 - y), (1 - x, 1 - y)]
    return x, y, c, chips


ANY = pl.BlockSpec(memory_space=pl.ANY)


def _allgather_hosted(shards, split):
    n = len(shards)

    def part(ref, t, half):
        if not split[t]:
            return ref
        rows = shards[t].shape[0] // 2
        return ref.at[pl.ds(half * rows, rows)]

    def copies(kind, ins, outs, sems):
        send_sems, recv_sems, pass_send, pass_recv, local_sems = sems
        x, y, c, chips = _position()
        mine = 2 * x + y
        if kind == "local":
            return [pltpu.make_async_copy(ins[t], outs[t].at[mine], local_sems.at[t]) for t in range(n)]
        made = []
        for t in range(n):
            for j, chip in enumerate(chips):
                theirs = 2 * chip[0] + chip[1]
                far = dict(send_sem=send_sems.at[3 * t + j], recv_sem=recv_sems.at[3 * t + j],
                           device_id=(chip[0], chip[1], c), device_id_type=MESH)
                near = dict(send_sem=pass_send.at[3 * t + j], recv_sem=pass_recv.at[3 * t + j],
                            device_id=(x, y, 1 - c), device_id_type=MESH)
                here = part(outs[t].at[theirs], t, c)
                if kind == "send":
                    made.append(pltpu.make_async_remote_copy(
                        src_ref=part(ins[t], t, c), dst_ref=part(outs[t].at[mine], t, c), **far))
                elif kind == "landed":
                    made.append(pltpu.make_async_remote_copy(src_ref=here, dst_ref=here, **far))
                elif not split[t]:
                    made.append(None)
                elif kind == "pass":
                    made.append(pltpu.make_async_remote_copy(src_ref=here, dst_ref=here, **near))
                else:
                    other = part(outs[t].at[theirs], t, 1 - c)
                    made.append(pltpu.make_async_remote_copy(src_ref=other, dst_ref=other, **near))
        return made

    def first(ins, outs, sems):
        for cp in copies("local", ins, outs, sems) + copies("send", ins, outs, sems):
            cp.start()

    def middle(ins, outs, sems):
        for got, cp in zip(copies("landed", ins, outs, sems), copies("pass", ins, outs, sems)):
            got.wait_recv()
            if cp is not None:
                cp.start()

    def last(ins, outs, sems):
        for cp in copies("passed", ins, outs, sems):
            if cp is not None:
                cp.wait_recv()
        for cp in copies("send", ins, outs, sems) + copies("pass", ins, outs, sems):
            if cp is not None:
                cp.wait_send()
        for cp in copies("local", ins, outs, sems):
            cp.wait()

    return _Hosted(shards, [jax.ShapeDtypeStruct((4,) + w.shape, w.dtype) for w in shards],
                   [pltpu.SemaphoreType.DMA((3 * n,))] * 4 + [pltpu.SemaphoreType.DMA((n,))],
                   first, middle, last)


def _allgather_routed(shards):
    n = len(shards)

    def piece(block_ref, t, c, quarter=None):
        half = shards[t].shape[0] // 2
        if quarter is None:
            return block_ref.at[pl.ds(c * half, half)]
        return block_ref.at[pl.ds(c * half + quarter * (half // 2), half // 2)]

    def copies(kind, ins, outs, sems):
        ici_send, ici_recv, pass_send, pass_recv, local_sems = sems
        x, y, c, chips = _position()
        mine = 2 * x + y
        if kind == "local":
            return [pltpu.make_async_copy(ins[t], outs[t].at[mine], local_sems.at[t]) for t in range(n)]
        ids = [2 * chip[0] + chip[1] for chip in chips]
        made = []
        for t in range(n):
            def ici(k, to):
                return dict(send_sem=ici_send.at[4 * t + k], recv_sem=ici_recv.at[4 * t + k],
                            device_id=(chips[to][0], chips[to][1], c), device_id_type=MESH)

            def d2d(k):
                return dict(send_sem=pass_send.at[4 * t + k], recv_sem=pass_recv.at[4 * t + k],
                            device_id=(x, y, 1 - c), device_id_type=MESH)

            def same(ref, where):
                return pltpu.make_async_remote_copy(src_ref=ref, dst_ref=ref, **where)

            if kind == "send":
                for k in range(2):
                    made.append(pltpu.make_async_remote_copy(
                        src_ref=piece(ins[t], t, c), dst_ref=piece(outs[t].at[mine], t, c), **ici(k, k)))
            elif kind == "landed":
                made += [same(piece(outs[t].at[ids[k]], t, c), ici(k, k)) for k in range(2)]
            elif kind == "forward":
                made.append(same(piece(outs[t].at[ids[0]], t, c, 0), ici(2, 1)))
                made.append(same(piece(outs[t].at[ids[1]], t, c, 1), ici(3, 0)))
            elif kind == "arrived":
                made.append(same(piece(outs[t].at[ids[2]], t, c, 0), ici(2, 1)))
                made.append(same(piece(outs[t].at[ids[2]], t, c, 1), ici(3, 0)))
            else:
                core = 1 - c if kind == "passed" else c
                if kind in ("pass halves", "passed"):
                    made += [same(piece(outs[t].at[ids[k]], t, core), d2d(k)) for k in range(2)]
                if kind in ("pass quarters", "passed"):
                    made += [same(piece(outs[t].at[ids[2]], t, core, k), d2d(2 + k)) for k in range(2)]
        return made

    def first(ins, outs, sems):
        for cp in copies("local", ins, outs, sems) + copies("send", ins, outs, sems):
            cp.start()

    def middle(ins, outs, sems):
        for got, onward, near in zip(copies("landed", ins, outs, sems), copies("forward", ins, outs, sems),
                                     copies("pass halves", ins, outs, sems)):
            got.wait_recv()
            near.start()
            onward.start()

    def last(ins, outs, sems):
        quarters = copies("pass quarters", ins, outs, sems)
        for got, near in zip(copies("arrived", ins, outs, sems), quarters):
            got.wait_recv()
            near.start()
        for cp in copies("passed", ins, outs, sems):
            cp.wait_recv()
        for cp in (copies("send", ins, outs, sems) + copies("forward", ins, outs, sems)
                   + copies("pass halves", ins, outs, sems) + quarters):
            cp.wait_send()
        for cp in copies("local", ins, outs, sems):
            cp.wait()

    return _Hosted(shards, [jax.ShapeDtypeStruct((4,) + w.shape, w.dtype) for w in shards],
                   [pltpu.SemaphoreType.DMA((4 * n,))] * 4 + [pltpu.SemaphoreType.DMA((n,))],
                   first, middle, last)


def _scatter_hosted(grads):
    n = len(grads)

    def copies(ins, outs, sems):
        send_sems, recv_sems = sems
        x, y, c, chips = _position()
        return [pltpu.make_async_remote_copy(
            src_ref=ins[t].at[2 * chip[0] + chip[1]], dst_ref=outs[t].at[j],
            send_sem=send_sems.at[3 * t + j], recv_sem=recv_sems.at[3 * t + j],
            device_id=(chip[0], chip[1], c), device_id_type=MESH)
            for t in range(n) for j, chip in enumerate(chips)]

    def first(ins, outs, sems):
        for cp in copies(ins, outs, sems):
            cp.start()

    def last(ins, outs, sems):
        for cp in copies(ins, outs, sems):
            cp.wait()

    return _Hosted(grads, [jax.ShapeDtypeStruct((3,) + g.shape[1:], g.dtype) for g in grads],
                   [pltpu.SemaphoreType.DMA((3 * n,))] * 2, first, None, last)


GATHER_PEERS = "x and y neighbours (same core) and the sibling core"
SCATTER_PEERS = "the same core of the three other chips"
EVERYONE = "the seven other devices"


def _run_on_sequencer(name, hosted, peers, collective_id):
    ins = [jax.new_ref(a, memory_space=pltpu.MemorySpace.HBM) for a in hosted.inputs]
    outs = [jax.empty_ref(shape, memory_space=pltpu.MemorySpace.HBM) for shape in hosted.out_shapes]

    @pl.kernel(mesh=plsc.ScalarSubcoreMesh(axis_name="sequencer", num_cores=1), name=name,
               scratch_types=tuple(hosted.sems), compiler_params=pltpu.CompilerParams(collective_id=collective_id))
    def launch(*sems):
        x, y, c, chips = _position()
        if peers == GATHER_PEERS:
            devices = [(chip[0], chip[1], c) for chip in chips[:2]] + [(x, y, 1 - c)]
        elif peers == SCATTER_PEERS:
            devices = [(chip[0], chip[1], c) for chip in chips]
        else:
            devices = [(x ^ (k >> 2), y ^ ((k >> 1) & 1), c ^ (k & 1)) for k in range(1, 8)]
        barrier = pltpu.get_barrier_semaphore()
        for device in devices:
            pl.semaphore_signal(barrier, inc=1, device_id=device, device_id_type=MESH)
        pl.semaphore_wait(barrier, len(devices))
        hosted.first(ins, outs, sems)
        if hosted.middle is not None:
            hosted.middle(ins, outs, sems)
        hosted.last(ins, outs, sems)

    launch()
    return [o[...] for o in outs]


def _run_alone(name, hosted):
    n_in = len(hosted.inputs)
    n_out = len(hosted.out_shapes)

    def body(*refs):
        ins, outs, sems = refs[:n_in], refs[n_in:n_in + n_out], refs[n_in + n_out:]
        hosted.first(ins, outs, sems)
        if hosted.middle is not None:
            hosted.middle(ins, outs, sems)
        hosted.last(ins, outs, sems)

    return pl.pallas_call(
        body, name=name, in_specs=[ANY] * n_in, out_specs=[ANY] * n_out, out_shape=hosted.out_shapes,
        scratch_shapes=hosted.sems)(*hosted.inputs)


def _gather_gain(shard):
    def body(in_ref, out_ref, send_sems, recv_sems):
        x, y, c, chips = _position()
        out_ref[2 * x + y] = in_ref[...]
        sends = [pltpu.make_async_remote_copy(
            src_ref=in_ref, dst_ref=out_ref.at[2 * x + y], send_sem=send_sems.at[j], recv_sem=recv_sems.at[j],
            device_id=(chip[0], chip[1], c), device_id_type=MESH) for j, chip in enumerate(chips)]
        for cp in sends:
            cp.start()
        for j, chip in enumerate(chips):
            pltpu.make_async_remote_copy(
                src_ref=in_ref, dst_ref=out_ref.at[2 * chip[0] + chip[1]], send_sem=send_sems.at[j],
                recv_sem=recv_sems.at[j], device_id=(chip[0], chip[1], c), device_id_type=MESH).wait_recv()
        for cp in sends:
            cp.wait_send()

    vmem = pl.BlockSpec(memory_space=pltpu.VMEM)
    return pl.pallas_call(
        body, name="gather_gain", in_specs=[vmem], out_specs=vmem,
        out_shape=jax.ShapeDtypeStruct((4,) + shard.shape, shard.dtype),
        scratch_shapes=[pltpu.SemaphoreType.DMA((3,))] * 2,
    )(shard)


def _swap_with_sibling(name, blocks):
    n = len(blocks)

    def body(*refs):
        ins, outs = refs[:n], refs[n:2 * n]
        send_sems, recv_sems = refs[2 * n:]
        x, y, c, _ = _position()
        sends = [pltpu.make_async_remote_copy(
            src_ref=ins[t], dst_ref=outs[t], send_sem=send_sems.at[t], recv_sem=recv_sems.at[t],
            device_id=(x, y, 1 - c), device_id_type=MESH) for t in range(n)]
        for cp in sends:
            cp.start()
        for cp in sends:
            cp.wait()

    return pl.pallas_call(
        body, name=name,
        in_specs=[ANY] * n, out_specs=[ANY] * n,
        out_shape=[jax.ShapeDtypeStruct(b.shape, b.dtype) for b in blocks],
        scratch_shapes=[pltpu.SemaphoreType.DMA((n,))] * 2,
    )(*blocks)


def _everyone_hosted(terms):
    nt = len(terms)

    def copies(kind, ins, outs, sems):
        send_sems, recv_sems, local_sems = sems
        x, y, c, _ = _position()
        me = 4 * x + 2 * y + c
        if kind == "local":
            return [pltpu.make_async_copy(ins[t], outs[t].at[me], local_sems.at[t]) for t in range(nt)]
        made = []
        for t in range(nt):
            for k in range(1, 8):
                peer = (x ^ (k >> 2), y ^ ((k >> 1) & 1), c ^ (k & 1))
                slot = me if kind == "send" else me ^ k
                made.append(pltpu.make_async_remote_copy(
                    src_ref=ins[t], dst_ref=outs[t].at[slot], send_sem=send_sems.at[7 * t + k - 1],
                    recv_sem=recv_sems.at[7 * t + k - 1], device_id=peer, device_id_type=MESH))
        return made

    def first(ins, outs, sems):
        for cp in copies("local", ins, outs, sems) + copies("send", ins, outs, sems):
            cp.start()

    def last(ins, outs, sems):
        for cp in copies("landed", ins, outs, sems):
            cp.wait_recv()
        for cp in copies("send", ins, outs, sems):
            cp.wait_send()
        for cp in copies("local", ins, outs, sems):
            cp.wait()

    return _Hosted(terms, [jax.ShapeDtypeStruct((8,) + a.shape, F32) for a in terms],
                   [pltpu.SemaphoreType.DMA((7 * nt,))] * 2 + [pltpu.SemaphoreType.DMA((nt,))], first, None, last)


def _small_step(partials, extras, ws, ms, vs, shard_of):
    n = len(partials)
    terms = list(partials) + list(extras)
    nt = len(terms)
    rows = [t for t in range(nt) if terms[t].shape[0] == 1]
    mats = [t for t in range(nt) if terms[t].shape[0] != 1]
    row_block = (8, max(terms[t].shape[1] for t in rows))
    assert len(rows) <= row_block[0]
    vmem = pl.BlockSpec(memory_space=pltpu.VMEM)

    def pack(*refs):
        packed = refs[-1]
        packed[...] = jnp.zeros_like(packed)
        for i, t in enumerate(rows):
            packed[i:i + 1, 0:terms[t].shape[1]] = refs[i][...]

    packed = pl.pallas_call(pack, name="small_pack", in_specs=[vmem] * len(rows), out_specs=vmem,
                            out_shape=jax.ShapeDtypeStruct(row_block, F32))(*[terms[t] for t in rows])
    slots = _run_on_sequencer("allgather_small", _everyone_hosted([packed] + [terms[t] for t in mats]),
                              EVERYONE, 2)

    def body(*refs):
        slot_refs, refs = refs[:len(slots)], refs[len(slots):]
        w_refs, refs = refs[:n], refs[n:]
        m_refs, refs = refs[:n], refs[n:]
        v_refs, outs = refs[:n], refs[n:]
        sums = []
        for ref in slot_refs:
            g = ref[0]
            for dev in range(1, 8):
                g = g + ref[dev]
            sums.append(g)
        chip = 2 * lax.axis_index("x") + lax.axis_index("y")
        for t in range(nt):
            if t in rows:
                i = rows.index(t)
                g = sums[0][i:i + 1, 0:terms[t].shape[1]]
            else:
                g = sums[1 + mats.index(t)]
            if t >= n:
                outs[4 * n + t - n][...] = g
                continue
            if shard_of[t]:
                width = ws[t].shape[-1]
                mine = jnp.zeros(ws[t].shape, F32)
                for s in range(4):
                    mine = jnp.where(chip == s, g[:, s * width:(s + 1) * width], mine)
                g = mine
            delta, mn, vn = _adamw_math(w_refs[t][...], g, m_refs[t][...], v_refs[t][...])
            outs[4 * t][...] = g
            outs[4 * t + 1][...] = delta
            outs[4 * t + 2][...] = mn
            outs[4 * t + 3][...] = vn

    out_shapes = []
    for t in range(n):
        out_shapes += [jax.ShapeDtypeStruct(ws[t].shape, F32)] * 4
    out_shapes += [jax.ShapeDtypeStruct(a.shape, F32) for a in extras]
    res = pl.pallas_call(
        body, name="small_step",
        in_specs=[vmem] * (len(slots) + 3 * n), out_specs=[vmem] * len(out_shapes), out_shape=out_shapes,
    )(*slots, *ws, *ms, *vs)
    return [res[4 * t:4 * t + 4] for t in range(n)], res[4 * n:4 * n + nt - n]


def _adamw_math(w, g, m, v):
    m = ADAM_B1 * m + (1.0 - ADAM_B1) * g
    v = ADAM_B2 * v + (1.0 - ADAM_B2) * (g * g)
    m_hat = m / (1.0 - ADAM_B1 ** ADAM_STEP)
    v_hat = v / (1.0 - ADAM_B2 ** ADAM_STEP)
    delta = -ADAM_LR * (m_hat / (jnp.sqrt(v_hat) + ADAM_EPS) + ADAM_WD * w)
    return delta, m, v


def _row_tile(rows):
    return 256 if rows % 256 == 0 else rows


def _sum_partials(name, own, recv, chip, after):
    rows, cols = own.shape[1:]
    tr = _row_tile(rows)

    def body(chip_ref, own_ref, recv_ref, after_ref, o_ref):
        acc = own_ref[...]
        for j in range(3):
            acc = acc + recv_ref[j].astype(F32)
        o_ref[...] = acc

    return pl.pallas_call(
        body, name=name,
        grid_spec=pltpu.PrefetchScalarGridSpec(
            num_scalar_prefetch=1, grid=(rows // tr,),
            in_specs=[pl.BlockSpec((None, tr, cols), lambda i, chip_ref: (chip_ref[0], i, 0)),
                      pl.BlockSpec((3, tr, cols), lambda i, chip_ref: (0, i, 0)), ANY],
            out_specs=pl.BlockSpec((tr, cols), lambda i, chip_ref: (i, 0))),
        out_shape=jax.ShapeDtypeStruct((rows, cols), F32),
        compiler_params=_params(("parallel",)),
    )(chip.reshape(1).astype(jnp.int32), own, recv, after)


def _adamw(name, w, m, v, g_parts):
    rows, cols = w.shape
    tr = _row_tile(rows)
    n = len(g_parts)

    def body(w_ref, m_ref, v_ref, *refs):
        g_refs = refs[:n]
        go_ref, d_ref, mo_ref, vo_ref = refs[n:]
        g = g_refs[0][...]
        for r in g_refs[1:]:
            g = g + r[...]
        delta, mn, vn = _adamw_math(w_ref[...], g, m_ref[...], v_ref[...])
        go_ref[...] = g
        d_ref[...] = delta
        mo_ref[...] = mn
        vo_ref[...] = vn

    spec = pl.BlockSpec((tr, cols), lambda i: (i, 0))
    return pl.pallas_call(
        body, name=name, grid=(rows // tr,),
        in_specs=[spec] * (3 + n), out_specs=[spec] * 4,
        out_shape=[jax.ShapeDtypeStruct((rows, cols), F32)] * 4,
        compiler_params=_params(("parallel",)),
    )(w, m, v, *g_parts)


def _local_step(x, target, ga, wa_in, rel_bias, later_shards, gk, t5, gb, sinks, gf):
    s, d = x.shape
    tm = min(TM_DENSE, s)
    nt = s // tm
    half = d // 2
    row = pl.BlockSpec((tm, d), lambda i: (i, 0))
    whole = lambda shape: pl.BlockSpec(shape, lambda *_: (0,) * len(shape))

    n1, = _norm_fwd("norm_a", x, ga)
    zqkv = _matmul("proj_a_qkv", n1, wa_in, dims=NN, grid=(3, nt + 1), zero_axis=1,
                   a_spec=pl.BlockSpec((tm, d), lambda j, i: (jnp.maximum(i - 1, 0), 0)),
                   b_spec=pl.BlockSpec((None, d, d), lambda j, i: (j, 0, 0)),
                   o_spec=pl.BlockSpec((None, tm, d), lambda j, i: (j, i, 0)),
                   out_shape=(3, tm + s, d), out_dtype=BF16)
    gate_a = _matmul("proj_a_gate", n1, wa_in, dims=NN, grid=(nt,),
                     a_spec=row, b_spec=pl.BlockSpec((None, d, d), lambda i: (3, 0, 0)), o_spec=row,
                     out_shape=(s, d), out_dtype=F32)
    onehot_a = _a_offset_onehot()
    diag_a = _diag_rows(onehot_a, rel_bias)
    (o_a, u_a, lse_a), gathered = _attn_a_fwd(zqkv, gate_a, diag_a, hosted=_allgather_routed(later_shards))
    wa_out, wkv, wb_in, wb_out = gathered
    wa_out = wa_out.reshape(d, d)
    wkv = wkv.reshape(d, -1)
    wb_out = wb_out.reshape(d, d)
    h1, nk, n2 = _out_norms("out_a_norms", u_a, wa_out, x, jnp.concatenate([gk, gb], axis=0))
    kvw = wkv.shape[1]
    wkv_x = jnp.concatenate([wkv[:, (i // 2) * HEAD_DIM:(i // 2 + 1) * HEAD_DIM] for i in range(8)], axis=1)
    kvx = _matmul("proj_kv", nk, wkv_x, dims=NN, grid=(nt + 1,), zero_axis=0,
                  a_spec=pl.BlockSpec((tm, d), lambda i: (jnp.maximum(i - 1, 0), 0)), b_spec=whole((d, B_KVX)),
                  o_spec=pl.BlockSpec((tm, B_KVX), lambda i: (i, 0)), out_shape=(tm + s, B_KVX), out_dtype=BF16)
    qb = _matmul("proj_b_q", n2, wb_in, dims=NN, grid=(2, nt),
                 a_spec=pl.BlockSpec((tm, d), lambda j, i: (i, 0)),
                 b_spec=pl.BlockSpec((None, d, half), lambda j, i: (j, 0, 0)),
                 o_spec=pl.BlockSpec((tm, half), lambda j, i: (i, j)), out_shape=(s, d), out_dtype=BF16)
    gate_b = _matmul("proj_b_gate", n2, wb_in, dims=NN, grid=(2, nt),
                     a_spec=pl.BlockSpec((tm, d), lambda j, i: (i, 0)),
                     b_spec=pl.BlockSpec((None, d, half), lambda j, i: (2 + j, 0, 0)),
                     o_spec=pl.BlockSpec((tm, half), lambda j, i: (i, j)), out_shape=(s, d), out_dtype=F32)
    onehot_b = _b_offset_onehot()
    base_b = jnp.roll(_diag_rows(onehot_b, t5)[..., ::-1], TQ, axis=-1)
    o_b, u_b, lse_b = _attn_b_fwd(qb, kvx, gate_b, base_b, sinks)
    dh2, loss, d_gf = _out_loss_head(u_b, wb_out, h1, target, gf)

    du_b = _matmul("dout_b", dh2, wb_out, dims=NT, grid=(nt,), a_spec=row, b_spec=whole((d, d)), o_spec=row,
                   out_shape=(s, d), out_dtype=F32)
    d_wb_out = _matmul("dw_out_b", u_b, dh2, dims=TN, grid=(2,),
                       a_spec=whole((s, d)), b_spec=pl.BlockSpec((s, half), lambda j: (0, j)),
                       o_spec=pl.BlockSpec((d, half), lambda j: (0, j)),
                       out_shape=(d, d), out_dtype=F32, also_bf16=True)
    dz_b, dkv, dsum_b, dsinks = _attn_b_bwd(qb, kvx, gate_b, o_b, du_b, lse_b, base_b, sinks)
    ddiag_b = jnp.roll(dsum_b[..., ::-1], -1, axis=-1)
    d_wb_in = _matmul("dw_in_b", n2, dz_b, dims=TN, grid=(4,),
                      a_spec=whole((s, d)), b_spec=pl.BlockSpec((None, s, half), lambda j: (j, 0, 0)),
                      o_spec=pl.BlockSpec((None, d, half), lambda j: (j, 0, 0)),
                      out_shape=(4, d, half), out_dtype=F32, also_bf16=True)
    d_wkv = _matmul("dw_kv", nk, dkv, dims=TN, grid=(1,),
                    a_spec=whole((s, d)), b_spec=whole((s, kvw)), o_spec=whole((d, kvw)),
                    out_shape=(d, kvw), out_dtype=F32, also_bf16=True)
    dh1, d_gkb = _proj_norm_bwd("dproj_kv_b", h1, dh2, jnp.concatenate([gk, gb], axis=0),
                                [(dkv[None], wkv[None]), (dz_b, wb_in)])

    du_a = _matmul("dout_a", dh1, wa_out, dims=NT, grid=(nt,), a_spec=row, b_spec=whole((d, d)), o_spec=row,
                   out_shape=(s, d), out_dtype=F32)
    d_wa_out = _matmul("dw_out_a", u_a, dh1, dims=TN, grid=(2,),
                       a_spec=whole((s, d)), b_spec=pl.BlockSpec((s, half), lambda j: (0, j)),
                       o_spec=pl.BlockSpec((d, half), lambda j: (0, j)),
                       out_shape=(d, d), out_dtype=F32, also_bf16=True)
    early = dict(a_w_out=[g.reshape(4, d // 4, d) for g in d_wa_out],
                 kv_w=[g.reshape(4, d // 4, kvw) for g in d_wkv], b_w_in=list(d_wb_in),
                 b_w_out=[g.reshape(4, d // 4, d) for g in d_wb_out])
    early_recv = _run_on_sequencer("scatter_early", _scatter_hosted([early[n][1] for n in early]),
                                   SCATTER_PEERS, 3)
    (dz_a, ddiag_a), _ = _attn_a_bwd(zqkv, gate_a, o_a, du_a, lse_a, diag_a)
    d_wa_in = _matmul("dw_in_a", n1, dz_a, dims=TN, grid=(4, 2),
                      a_spec=whole((s, d)), b_spec=pl.BlockSpec((None, s, half), lambda j, h: (j, 0, h)),
                      o_spec=pl.BlockSpec((None, d, half), lambda j, h: (j, 0, h)),
                      out_shape=(4, d, d), out_dtype=F32, also_bf16=True)
    late_recv = _run_on_sequencer("scatter_a_w_in", _scatter_hosted([d_wa_in[1]]), SCATTER_PEERS, 0)
    grad_x, d_ga = _proj_norm_bwd("dproj_a", x, dh1, ga, [(dz_a, wa_in)])

    small = dict(a_norm=d_ga, kv_norm=d_gkb[0:1], b_norm=d_gkb[1:2], b_sinks=dsinks[0:1, :HEADS], final_norm=d_gf)
    small["by_offset"] = dict(a_rel_bias=(onehot_a, ddiag_a.reshape(HEADS, -1)),
                              t5_bias=(onehot_b, ddiag_b.reshape(HEADS, -1)))
    own = dict(a_w_in=d_wa_in[0], **{n: early[n][0] for n in early})
    received = dict(a_w_in=late_recv[0], **dict(zip(early, early_recv)))
    return loss, grad_x, small, own, received, d_wa_in[1]


SMALL = ("a_norm", "kv_norm", "b_norm", "b_sinks", "final_norm")
TABLES = ("a_rel_bias", "t5_bias")
BIG = ("a_w_in", "a_w_out", "kv_w", "b_w_in", "b_w_out")
ORDER = ("a_norm", "a_w_in", "a_rel_bias", "a_w_out", "kv_norm", "kv_w", "t5_bias", "b_norm", "b_w_in",
         "b_sinks", "b_w_out", "final_norm")


def kernel(x, a_norm, a_w_in, a_rel_bias, a_w_out, kv_norm, kv_w, t5_bias, b_norm, b_w_in, b_sinks, b_w_out, final_norm, loss_target, m_a_norm, m_a_w_in, m_a_rel_bias, m_a_w_out, m_kv_norm, m_kv_w, m_t5_bias, m_b_norm, m_b_w_in, m_b_sinks, m_b_w_out, m_final_norm, v_a_norm, v_a_w_in, v_a_rel_bias, v_a_w_out, v_kv_norm, v_kv_w, v_t5_bias, v_b_norm, v_b_w_in, v_b_sinks, v_b_w_out, v_final_norm):
    w = dict(a_norm=a_norm, a_w_in=a_w_in, a_rel_bias=a_rel_bias, a_w_out=a_w_out, kv_norm=kv_norm, kv_w=kv_w,
             t5_bias=t5_bias, b_norm=b_norm, b_w_in=b_w_in, b_sinks=b_sinks, b_w_out=b_w_out,
             final_norm=final_norm)
    m = dict(a_norm=m_a_norm, a_w_in=m_a_w_in, a_rel_bias=m_a_rel_bias, a_w_out=m_a_w_out, kv_norm=m_kv_norm,
             kv_w=m_kv_w, t5_bias=m_t5_bias, b_norm=m_b_norm, b_w_in=m_b_w_in, b_sinks=m_b_sinks,
             b_w_out=m_b_w_out, final_norm=m_final_norm)
    v = dict(a_norm=v_a_norm, a_w_in=v_a_w_in, a_rel_bias=v_a_rel_bias, a_w_out=v_a_w_out, kv_norm=v_kv_norm,
             kv_w=v_kv_w, t5_bias=v_t5_bias, b_norm=v_b_norm, b_w_in=v_b_w_in, b_sinks=v_b_sinks,
             b_w_out=v_b_w_out, final_norm=v_final_norm)
    d = D_MODEL
    chip = 2 * lax.axis_index("x") + lax.axis_index("y")

    shard2d = dict(a_w_in=a_w_in[0], a_w_out=a_w_out[0], kv_w=kv_w, b_w_in=b_w_in[0], b_w_out=b_w_out[0])

    wa_in, = _run_on_sequencer("allgather_first", _allgather_routed([shard2d["a_w_in"].astype(BF16)]),
                               GATHER_PEERS, 1)
    ga = _gather_gain(a_norm).reshape(1, d)

    loss, grad_x, small, own, received, after_attention = _local_step(
        x[0], loss_target[0], ga, wa_in, a_rel_bias[0], [shard2d[n].astype(BF16) for n in BIG[1:]],
        kv_norm.reshape(1, d), t5_bias, b_norm, b_sinks, final_norm.reshape(1, d))

    out = {}
    as2d = lambda a: a.reshape(-1, a.shape[-1])
    small_res, (loss_sum, *offset_sums) = _small_step(
        [small[n] for n in SMALL], [loss] + [small["by_offset"][n][1] for n in TABLES],
        [as2d(w[n]) for n in SMALL], [as2d(m[n]) for n in SMALL], [as2d(v[n]) for n in SMALL],
        [n == "a_norm" for n in SMALL])
    for n, res in zip(SMALL, small_res):
        out[n] = [r.reshape(w[n].shape) for r in res]
    loss_out = loss_sum.reshape(())
    for n, summed in zip(TABLES, offset_sums):
        grad = _diag_rows_grad(small["by_offset"][n][0], summed)
        res = _adamw("adamw_" + n, as2d(w[n]), as2d(m[n]), as2d(v[n]), [grad])
        out[n] = [r.reshape(w[n].shape) for r in res]

    core_sums = [_sum_partials("sum_" + n, own[n], received[n], chip, after_attention) for n in BIG]
    sibling_sums = (_swap_with_sibling("swap_last", core_sums[:1])
                    + _swap_with_sibling("swap_early", core_sums[1:]))

    for n, mine, theirs in zip(BIG, core_sums, sibling_sums):
        res = _adamw("adamw_" + n, shard2d[n], m[n].reshape(shard2d[n].shape), v[n].reshape(shard2d[n].shape),
                     [mine, theirs])
        out[n] = [r.reshape(w[n].shape) for r in res]

    grads = [out[n][0] for n in ORDER]
    deltas = [out[n][1] for n in ORDER]
    new_m = [out[n][2] for n in ORDER]
    new_v = [out[n][3] for n in ORDER]
    return (loss_out, grad_x[None], *grads, *deltas, *new_m, *new_v)
```

```python
import functools
import math

import jax
import jax.numpy as jnp
import numpy as np
from jax import lax
from jax.experimental import pallas as pl
from jax.experimental.pallas import tpu as pltpu
from jax.experimental.pallas import tpu_sc as plsc

F32 = jnp.float32
BF16 = jnp.bfloat16
MESH = pl.DeviceIdType.MESH

D_MODEL = 1024
HEADS = 16
HEAD_DIM = 64
CHUNK = 64
RMS_EPS = 1e-6
SCALE = HEAD_DIM ** -0.5
A_LEFT_CHUNKS = 8
A_REL_CLIP = 256
B_LEFT_CHUNKS = 2
B_KV_HEADS = 2
B_GROUP = HEADS // B_KV_HEADS
T5_BUCKETS = 32
T5_MAX_DIST = 128
ADAM_LR = 0.001
ADAM_B1 = 0.9
ADAM_B2 = 0.999
ADAM_EPS = 1e-08
ADAM_WD = 0.01
ADAM_STEP = 10

MASKED = -1e30
LANES = 128
TQ = 128
A_PAIRS = 2
A_PAIRS_FWD = 4
KB = 128
A_KBLOCKS = A_LEFT_CHUNKS * CHUNK // KB + 1
B_KBLOCKS = B_LEFT_CHUNKS * CHUNK // KB + 1
A_WIN = A_KBLOCKS * KB
B_WIN = B_KBLOCKS * KB
TM = 512
TM_DENSE = 1024
TM_PARTS = 512
VMEM_LIMIT = 56 * 1024 * 1024

NT = (((1,), (1,)), ((), ()))
TN = (((0,), (0,)), ((), ()))
NN = (((1,), (0,)), ((), ()))


def _params(sem=None):
    return pltpu.CompilerParams(dimension_semantics=sem, vmem_limit_bytes=VMEM_LIMIT)


class _Hosted:
    def __init__(self, inputs, out_shapes, sems, first, middle, last):
        self.inputs, self.out_shapes, self.sems = list(inputs), list(out_shapes), list(sems)
        self.first, self.middle, self.last = first, middle, last


def _call(body, *, name, grid, in_specs, out_specs, out_shape, args, scratch_shapes=(), sem=None, hosted=None):
    in_specs, out_specs, out_shape = list(in_specs), list(out_specs), list(out_shape)
    scratch_shapes = list(scratch_shapes)
    if hosted is None:
        out = pl.pallas_call(
            body, name=name, grid=grid, in_specs=in_specs, out_specs=out_specs, out_shape=out_shape,
            scratch_shapes=scratch_shapes, compiler_params=_params(sem))(*args)
        return list(out), []
    n_in, n_out, n_scr = len(in_specs), len(out_shape), len(scratch_shapes)
    h_in, h_out = len(hosted.inputs), len(hosted.out_shapes)
    total = int(np.prod(grid)) if grid else 1

    def wrapped(*refs):
        ins, refs = refs[:n_in], refs[n_in:]
        h_ins, refs = refs[:h_in], refs[h_in:]
        outs, refs = refs[:n_out], refs[n_out:]
        h_outs, refs = refs[:h_out], refs[h_out:]
        scr, h_sems = refs[:n_scr], refs[n_scr:]
        step = 0
        for axis, size in enumerate(grid):
            step = step * size + pl.program_id(axis)

        if hosted.first is not None:
            @pl.when(step == 0)
            def _():
                hosted.first(h_ins, h_outs, h_sems)

        body(*ins, *outs, *scr)
        if hosted.middle is not None:
            @pl.when(step == total // 2)
            def _():
                hosted.middle(h_ins, h_outs, h_sems)

        if hosted.last is not None:
            @pl.when(step == total - 1)
            def _():
                hosted.last(h_ins, h_outs, h_sems)

    out = pl.pallas_call(
        wrapped, name=name, grid=grid, in_specs=in_specs + [ANY] * h_in, out_specs=out_specs + [ANY] * h_out,
        out_shape=out_shape + hosted.out_shapes, scratch_shapes=scratch_shapes + hosted.sems,
        compiler_params=_params(("arbitrary",) * len(grid)))(*args, *hosted.inputs)
    return list(out[:n_out]), list(out[n_out:])


def _matmul(name, a, b, *, dims, grid, a_spec, b_spec, o_spec, out_shape, out_dtype,
            parts=1, resid=None, resid_spec=None, also_bf16=False, hosted=None, zero_axis=None):
    def body(*refs):
        if zero_axis is None:
            product(*refs)
        else:
            @pl.when(pl.program_id(zero_axis) == 0)
            def _():
                refs[2][...] = jnp.zeros_like(refs[2])

            @pl.when(pl.program_id(zero_axis) > 0)
            def _():
                product(*refs)

    def product(*refs):
        a_ref, b_ref = refs[:2]
        r_ref = refs[2] if resid is not None else None
        o_ref = refs[3] if resid is not None else refs[2]
        if parts == 1:
            prod = lax.dot_general(a_ref[...].astype(BF16), b_ref[...].astype(BF16), dims,
                                   preferred_element_type=F32)
        else:
            prod = None
            for part in range(parts):
                term = lax.dot_general(a_ref[part].astype(BF16), b_ref[part].astype(BF16), dims,
                                       preferred_element_type=F32)
                prod = term if prod is None else prod + term
        if resid is not None:
            prod = r_ref[...] + prod
        o_ref[...] = prod.astype(out_dtype)
        if also_bf16:
            refs[-1][...] = prod.astype(BF16)

    in_specs = [a_spec, b_spec]
    args = [a, b]
    if resid is not None:
        in_specs.append(resid_spec)
        args.append(resid)
    sem = ["parallel"] * len(grid)
    out_specs = [o_spec]
    out_shapes = [jax.ShapeDtypeStruct(out_shape, out_dtype)]
    if also_bf16:
        out_specs.append(o_spec)
        out_shapes.append(jax.ShapeDtypeStruct(out_shape, BF16))
    out, extra = _call(body, name=name, grid=grid, in_specs=in_specs, out_specs=out_specs, out_shape=out_shapes,
                       args=args, sem=tuple(sem), hosted=hosted)
    res = out[0] if not also_bf16 else tuple(out)
    return res if hosted is None else (res, extra)


def _rms_rows(x):
    return lax.rsqrt(jnp.mean(x * x, axis=-1, keepdims=True) + RMS_EPS)


def _norm_fwd(name, x, gains):
    s, d = x.shape
    n = gains.shape[0]

    def body(x_ref, g_ref, *o_refs):
        xv = x_ref[...]
        xh = xv * _rms_rows(xv)
        for i in range(n):
            o_refs[i][...] = (xh * g_ref[i:i + 1, :]).astype(BF16)

    row = pl.BlockSpec((TM, d), lambda i: (i, 0))
    return pl.pallas_call(
        body, name=name, grid=(s // TM,),
        in_specs=[row, pl.BlockSpec((n, d), lambda i: (0, 0))],
        out_specs=[row] * n,
        out_shape=[jax.ShapeDtypeStruct((s, d), BF16)] * n,
        compiler_params=_params(("parallel",)),
    )(x, gains)


def _proj_norm_bwd(name, x, dres, gains, branches):
    s, d = x.shape
    n = len(branches)
    tm = min(TM_PARTS, s)

    def body(x_ref, r_ref, g_ref, *refs):
        ab_refs, dx_ref, dg_ref = refs[:2 * n], refs[2 * n], refs[2 * n + 1]
        i = pl.program_id(0)
        xv = x_ref[...]
        r = _rms_rows(xv)
        xh = xv * r

        @pl.when(i == 0)
        def _():
            dg_ref[...] = jnp.zeros_like(dg_ref)

        a = None
        for j in range(n):
            a_ref, b_ref = ab_refs[2 * j], ab_refs[2 * j + 1]
            dn = None
            for part in range(a_ref.shape[0]):
                term = lax.dot_general(a_ref[part], b_ref[part], NT, preferred_element_type=F32)
                dn = term if dn is None else dn + term
            t = dn * g_ref[j:j + 1, :]
            a = t if a is None else a + t
            dg_ref[j:j + 1, :] += jnp.sum(dn * xh, axis=0, keepdims=True)
        dx_ref[...] = r_ref[...] + r * (a - xh * jnp.mean(xh * a, axis=-1, keepdims=True))

    row = pl.BlockSpec((tm, d), lambda i: (i, 0))
    small = pl.BlockSpec((n, d), lambda i: (0, 0))
    ab_specs, ab_args = [], []
    for a, b in branches:
        ab_specs += [pl.BlockSpec((a.shape[0], tm, a.shape[2]), lambda i: (0, i, 0)),
                     pl.BlockSpec(b.shape, lambda i: (0, 0, 0))]
        ab_args += [a, b]
    return pl.pallas_call(
        body, name=name, grid=(s // tm,),
        in_specs=[row, row, small] + ab_specs,
        out_specs=[row, small],
        out_shape=[jax.ShapeDtypeStruct((s, d), F32), jax.ShapeDtypeStruct((n, d), F32)],
        compiler_params=_params(("arbitrary",)),
    )(x, dres, gains, *ab_args)


def _out_norms(name, u, w_out, resid, gains):
    s, d = resid.shape
    n = gains.shape[0]
    tm = min(TM_DENSE, s)

    def body(u_ref, w_ref, r_ref, g_ref, h_ref, *o_refs):
        hv = r_ref[...] + jnp.dot(u_ref[...], w_ref[...], preferred_element_type=F32)
        h_ref[...] = hv
        hh = hv * _rms_rows(hv)
        for i in range(n):
            o_refs[i][...] = (hh * g_ref[i:i + 1, :]).astype(BF16)

    row = pl.BlockSpec((tm, d), lambda i: (i, 0))
    return pl.pallas_call(
        body, name=name, grid=(s // tm,),
        in_specs=[row, pl.BlockSpec((d, d), lambda i: (0, 0)), row, pl.BlockSpec((n, d), lambda i: (0, 0))],
        out_specs=[row] * (n + 1),
        out_shape=[jax.ShapeDtypeStruct((s, d), F32)] + [jax.ShapeDtypeStruct((s, d), BF16)] * n,
        compiler_params=_params(("parallel",)),
    )(u, w_out, resid, gains)


def _out_loss_head(u, w_out, resid, target, gain):
    s, d = resid.shape
    tm = min(TM_PARTS, s)

    def body(u_ref, w_ref, r_ref, t_ref, g_ref, dh_ref, loss_ref, dg_ref):
        i = pl.program_id(0)
        hv = r_ref[...] + jnp.dot(u_ref[...], w_ref[...], preferred_element_type=F32)
        r = _rms_rows(hv)
        hh = hv * r
        g = g_ref[...]
        err = hh * g - t_ref[...]
        part = 0.5 * jnp.sum(jnp.sum(err * err, axis=-1, keepdims=True) * (1.0 / d), axis=0, keepdims=True)
        dy = err * (1.0 / d)
        a = dy * g
        dh_ref[...] = r * (a - hh * jnp.mean(hh * a, axis=-1, keepdims=True))
        dg = jnp.sum(dy * hh, axis=0, keepdims=True)

        @pl.when(i == 0)
        def _():
            loss_ref[...] = part
            dg_ref[...] = dg

        @pl.when(i > 0)
        def _():
            loss_ref[...] += part
            dg_ref[...] += dg

    row = pl.BlockSpec((tm, d), lambda i: (i, 0))
    return pl.pallas_call(
        body, name="out_b_loss_head", grid=(s // tm,),
        in_specs=[row, pl.BlockSpec((d, d), lambda i: (0, 0)), row, row, pl.BlockSpec((1, d), lambda i: (0, 0))],
        out_specs=[row, pl.BlockSpec((1, 1), lambda i: (0, 0)), pl.BlockSpec((1, d), lambda i: (0, 0))],
        out_shape=[jax.ShapeDtypeStruct((s, d), F32), jax.ShapeDtypeStruct((1, 1), F32),
                   jax.ShapeDtypeStruct((1, d), F32)],
        compiler_params=_params(("arbitrary",)),
    )(u, w_out, resid, target, gain)


def _silu_parts(g):
    sig = jax.nn.sigmoid(g)
    return g * sig, sig * (1.0 + g * (1.0 - sig))


def _lane_lo(rows):
    return lax.broadcasted_iota(jnp.int32, (rows, LANES), 1) < HEAD_DIM


def _stack_pair(x):
    lo = _lane_lo(x.shape[0])
    zero = jnp.zeros_like(x)
    return jnp.concatenate([jnp.where(lo, x, zero), jnp.where(lo, zero, x)], axis=0)


def _unstack_pair(y, w):
    return jnp.where(_lane_lo(w), y[:w], y[w:])


def _block_valid(b, left_blocks, width):
    col = lax.broadcasted_iota(jnp.int32, (1, 2 * width), 1)
    col = jnp.where(col >= width, col - width, col)
    return (col // KB + (b - left_blocks)) >= 0


def _toeplitz_tile(diag_row, width, left_chunks):
    wide = width + TQ
    rolled = pltpu.roll(jnp.broadcast_to(diag_row, (TQ, wide)), 1, 1, stride=1, stride_axis=0)
    i = lax.broadcasted_iota(jnp.int32, (TQ, width), 0) // CHUNK
    j = lax.broadcasted_iota(jnp.int32, (TQ, width), 1) // CHUNK
    dc = i + left_chunks - j
    return jnp.where((dc >= 0) & (dc <= left_chunks), rolled[:, TQ:], MASKED)


def _toeplitz_sum(tile, width):
    flip = (lax.broadcasted_iota(jnp.int32, (TQ, TQ), 0) + lax.broadcasted_iota(jnp.int32, (TQ, TQ), 1)
            == TQ - 1).astype(F32)
    reversed_rows = jnp.dot(flip, tile, precision=lax.Precision.HIGHEST, preferred_element_type=F32)
    padded = jnp.concatenate([reversed_rows, jnp.zeros((TQ, TQ), F32)], axis=1)
    rolled = pltpu.roll(padded, 0, 1, stride=1, stride_axis=0)
    return jnp.sum(rolled, axis=0, keepdims=True)


def _softmax_pair(sc, w, sink=None):
    ps, inv, lses = [], [], []
    for e in range(2):
        sh = sc[:, e * w:(e + 1) * w]
        m = jnp.max(sh, axis=-1, keepdims=True)
        if sink is not None:
            m = jnp.maximum(m, sink[e])
        ex = jnp.exp(sh - m)
        l = jnp.sum(ex, axis=-1, keepdims=True)
        if sink is not None:
            l = l + jnp.exp(sink[e] - m)
        ps.append(ex.astype(BF16))
        inv.append(1.0 / l)
        lses.append(m + jnp.log(l))
    return jnp.concatenate(ps, axis=-1), inv, lses


def _softmax_pair_bwd(sc, dp, lse, delta, w):
    ps, dss = [], []
    for e in range(2):
        p = jnp.exp(sc[:, e * w:(e + 1) * w] - lse[e])
        ps.append(p)
        dss.append(p * (dp[:, e * w:(e + 1) * w] - delta[e]))
    return jnp.concatenate(ps, axis=-1), jnp.concatenate(dss, axis=-1)


def _pair_rowsums(x, lo):
    zero = jnp.zeros_like(x)
    return (jnp.sum(jnp.where(lo, x, zero), axis=-1, keepdims=True),
            jnp.sum(jnp.where(lo, zero, x), axis=-1, keepdims=True))


def _a_qkv_specs(rows, pad, pw):
    return [pl.BlockSpec((None, TQ, pw), lambda p, b: (0, b + pad // TQ, p)),
            pl.BlockSpec((None, rows, pw), lambda p, b: (1, 0, p)),
            pl.BlockSpec((None, rows, pw), lambda p, b: (2, 0, p))]


def _window(ref, b, pad, win, lanes):
    start = pl.multiple_of(b * TQ + pad - (win - TQ), KB)
    return ref[pl.ds(start, win), lanes]


def _attn_a_fwd(zqkv, g, diag, hosted=None):
    s = g.shape[0]
    pad = zqkv.shape[1] - s
    nb = s // TQ
    left = A_KBLOCKS - 1
    pairs = A_PAIRS_FWD
    pw = pairs * LANES
    wide = A_WIN + TQ

    def body(q_ref, k_ref, v_ref, g_ref, diag_ref, o_ref, u_ref, lse_ref, bias_scr):
        b = pl.program_id(1)

        @pl.when(b == 0)
        def _():
            for hh in range(2 * pairs):
                bias_scr[hh // 2, :, (hh % 2) * A_WIN:(hh % 2 + 1) * A_WIN] = _toeplitz_tile(
                    diag_ref[hh], A_WIN, A_LEFT_CHUNKS)

        def step(first_blocks):
            lo = _lane_lo(TQ)
            for pp in range(pairs):
                ln = slice(pp * LANES, (pp + 1) * LANES)
                kcat = _stack_pair(_window(k_ref, b, pad, A_WIN, ln))
                vcat = _stack_pair(_window(v_ref, b, pad, A_WIN, ln))
                sc = lax.dot_general(q_ref[:, ln] * SCALE, kcat, NT, preferred_element_type=F32) + bias_scr[pp]
                if first_blocks:
                    sc = jnp.where(_block_valid(b, left, A_WIN), sc, MASKED)
                p, inv, lses = _softmax_pair(sc, A_WIN)
                ov = jnp.dot(p, vcat, preferred_element_type=F32) * jnp.where(lo, inv[0], inv[1])
                o_ref[:, ln] = ov
                lse_ref[pp] = jnp.where(lo, lses[0], lses[1])
                sg, _ = _silu_parts(g_ref[:, ln])
                u_ref[:, ln] = (ov * sg).astype(BF16)

        @pl.when(b < left)
        def _():
            step(True)

        @pl.when(b >= left)
        def _():
            step(False)

    tile = pl.BlockSpec((TQ, pw), lambda p, b: (b, p))
    return _call(
        body, name="attn_a_fwd", grid=(HEADS // 2 // pairs, nb),
        in_specs=_a_qkv_specs(pad + s, pad, pw) + [
            tile, pl.BlockSpec((2 * pairs, 1, wide), lambda p, b: (p, 0, 0))],
        out_specs=[tile, tile, pl.BlockSpec((pairs, TQ, LANES), lambda p, b: (p, b, 0))],
        out_shape=[jax.ShapeDtypeStruct((s, D_MODEL), F32), jax.ShapeDtypeStruct((s, D_MODEL), BF16),
                   jax.ShapeDtypeStruct((HEADS // 2, s, LANES), F32)],
        scratch_shapes=[pltpu.VMEM((pairs, TQ, 2 * A_WIN), F32)],
        sem=("parallel", "arbitrary"), hosted=hosted,
        args=(zqkv, zqkv, zqkv, g, diag))


def _attn_a_bwd(zqkv, g, o, du, lse, diag, hosted=None):
    s = g.shape[0]
    pad = zqkv.shape[1] - s
    nb = s // TQ
    left = A_KBLOCKS - 1
    pw = A_PAIRS * LANES
    wide = A_WIN + TQ

    def body(q_ref, k_ref, v_ref, g_ref, o_ref, du_ref, lse_ref, diag_ref, dz_ref, ddiag_ref,
             bias_scr, dbias_acc, dk_acc, dv_acc):
        b = pl.program_id(1)

        @pl.when(b == 0)
        def _():
            for hh in range(2 * A_PAIRS):
                bias_scr[hh // 2, :, (hh % 2) * A_WIN:(hh % 2 + 1) * A_WIN] = _toeplitz_tile(
                    diag_ref[hh], A_WIN, A_LEFT_CHUNKS)
            dbias_acc[...] = jnp.zeros_like(dbias_acc)
            dk_acc[...] = jnp.zeros_like(dk_acc)
            dv_acc[...] = jnp.zeros_like(dv_acc)

        def step(first_blocks):
            lo = _lane_lo(TQ)
            upper = lax.broadcasted_iota(jnp.int32, (LANES, A_WIN), 0) < HEAD_DIM
            rows = pl.ds(pl.multiple_of(b * TQ, TQ), TQ)
            sg, dsg = _silu_parts(g_ref[...])
            duv = du_ref[...]
            ov = o_ref[...]
            do = duv * sg
            dz_ref[3, rows, :] = (duv * ov * dsg).astype(BF16)
            do_o = do * ov
            do_bf = do.astype(BF16)
            for pp in range(A_PAIRS):
                ln = slice(pp * LANES, (pp + 1) * LANES)
                q = q_ref[:, ln] * SCALE
                kcat = _stack_pair(_window(k_ref, b, pad, A_WIN, ln))
                vcat = _stack_pair(_window(v_ref, b, pad, A_WIN, ln))
                sc = lax.dot_general(q, kcat, NT, preferred_element_type=F32) + bias_scr[pp]
                if first_blocks:
                    sc = jnp.where(_block_valid(b, left, A_WIN), sc, MASKED)
                lse_t = lse_ref[pp]
                dp = lax.dot_general(do_bf[:, ln], vcat, NT, preferred_element_type=F32)
                p, ds = _softmax_pair_bwd(sc, dp, (lse_t[:, 0:1], lse_t[:, HEAD_DIM:HEAD_DIM + 1]),
                                          _pair_rowsums(do_o[:, ln], lo), A_WIN)
                dbias_acc[pp] += ds
                dsb = ds.astype(BF16)
                dz_ref[0, rows, ln] = (jnp.dot(dsb, kcat, preferred_element_type=F32) * SCALE).astype(BF16)
                dkt = lax.dot_general(q, dsb, TN, preferred_element_type=F32)
                dvt = lax.dot_general(do_bf[:, ln], p.astype(BF16), TN, preferred_element_type=F32)
                dkt = jnp.where(upper, dkt[:, :A_WIN], dkt[:, A_WIN:])
                dvt = jnp.where(upper, dvt[:, :A_WIN], dvt[:, A_WIN:])
                for t in range(A_KBLOCKS):
                    blk = b + (pad // KB - left + t)
                    dk_acc[blk, ln, :] += dkt[:, t * KB:(t + 1) * KB]
                    dv_acc[blk, ln, :] += dvt[:, t * KB:(t + 1) * KB]

        @pl.when(b < left)
        def _():
            step(True)

        @pl.when(b >= left)
        def _():
            step(False)

        @pl.when(b == nb - 1)
        def _():
            for kb in range(s // KB):
                dz_ref[1, kb * KB:(kb + 1) * KB, :] = dk_acc[pad // KB + kb].T.astype(BF16)
                dz_ref[2, kb * KB:(kb + 1) * KB, :] = dv_acc[pad // KB + kb].T.astype(BF16)
            for hh in range(2 * A_PAIRS):
                ddiag_ref[hh] = _toeplitz_sum(
                    dbias_acc[hh // 2, :, (hh % 2) * A_WIN:(hh % 2 + 1) * A_WIN], A_WIN)

    tile = pl.BlockSpec((TQ, pw), lambda p, b: (b, p))
    diag_spec = pl.BlockSpec((2 * A_PAIRS, 1, wide), lambda p, b: (p, 0, 0))
    return _call(
        body, name="attn_a_bwd", grid=(HEADS // 2 // A_PAIRS, nb),
        in_specs=_a_qkv_specs(pad + s, pad, pw) + [
            tile, tile, tile, pl.BlockSpec((A_PAIRS, TQ, LANES), lambda p, b: (p, b, 0)), diag_spec],
        out_specs=[pl.BlockSpec((4, s, pw), lambda p, b: (0, 0, p)), diag_spec],
        out_shape=[jax.ShapeDtypeStruct((4, s, D_MODEL), BF16),
                   jax.ShapeDtypeStruct((HEADS, 1, wide), F32)],
        scratch_shapes=[pltpu.VMEM((A_PAIRS, TQ, 2 * A_WIN), F32), pltpu.VMEM((A_PAIRS, TQ, 2 * A_WIN), F32),
                        pltpu.VMEM(((pad + s) // KB, pw, KB), F32), pltpu.VMEM(((pad + s) // KB, pw, KB), F32)],
        sem=("parallel", "arbitrary"), hosted=hosted,
        args=(zqkv, zqkv, zqkv, g, o, du, lse, diag))


B_STACK = B_GROUP // 2
B_KVX = 4 * LANES
B_ROWS = B_STACK * TQ
B_WIDE = B_WIN + TQ


def _b_head_place(h):
    return h // B_GROUP, (h % B_GROUP) // 2, h % 2


def _toeplitz_tile_t(base_row, width, left_chunks):
    wide = width + TQ
    rolled = pltpu.roll(jnp.broadcast_to(base_row, (width, wide)), 0, 1, stride=1, stride_axis=0)
    j = lax.broadcasted_iota(jnp.int32, (width, TQ), 0) // CHUNK
    i = lax.broadcasted_iota(jnp.int32, (width, TQ), 1) // CHUNK
    dc = i + left_chunks - j
    return jnp.where((dc >= 0) & (dc <= left_chunks), rolled[:, :TQ], MASKED)


def _toeplitz_sum_t(tile_t, width):
    flip = (lax.broadcasted_iota(jnp.int32, (width, width), 0) + lax.broadcasted_iota(jnp.int32, (width, width), 1)
            == width - 1).astype(F32)
    reversed_rows = jnp.dot(flip, tile_t, precision=lax.Precision.HIGHEST, preferred_element_type=F32)
    padded = jnp.concatenate([reversed_rows, jnp.zeros((width, width), F32)], axis=1)
    rolled = pltpu.roll(padded, 0, 1, stride=1, stride_axis=0)
    return jnp.sum(rolled, axis=0, keepdims=True)


def _b_build_bias(base_ref, bias_scr):
    for h in range(HEADS):
        gi, pr, e = _b_head_place(h)
        bias_scr[gi, e * B_WIN:(e + 1) * B_WIN, pr * TQ:(pr + 1) * TQ] = _toeplitz_tile_t(
            base_ref[h], B_WIN, B_LEFT_CHUNKS)


def _b_stack(x, gi):
    return jnp.concatenate(
        [x[:, (B_STACK * gi + pr) * LANES:(B_STACK * gi + pr + 1) * LANES] for pr in range(B_STACK)], axis=0)


def _b_sink_rows(sink_ref, gi):
    block = lax.broadcasted_iota(jnp.int32, (1, B_ROWS), 1) // TQ
    rows = []
    for e in range(2):
        row = jnp.zeros((1, B_ROWS), F32)
        for pr in range(B_STACK):
            h = B_GROUP * gi + 2 * pr + e
            row = jnp.where(block == pr, sink_ref[0:1, h:h + 1], row)
        rows.append(row)
    return rows


def _b_scores_t(q_ref, kvv, bias_scr, gi, b, left, first_blocks):
    kcat = _stack_pair(kvv[:, gi * LANES:(gi + 1) * LANES])
    vcat = _stack_pair(kvv[:, (B_KV_HEADS + gi) * LANES:(B_KV_HEADS + gi + 1) * LANES])
    qs = _b_stack(q_ref, gi) * SCALE
    sc = lax.dot_general(kcat, qs, NT, preferred_element_type=F32) + bias_scr[gi]
    if first_blocks:
        row = lax.broadcasted_iota(jnp.int32, (2 * B_WIN, 1), 0)
        row = jnp.where(row >= B_WIN, row - B_WIN, row)
        sc = jnp.where((row // KB + (b - left)) >= 0, sc, MASKED)
    return kcat, vcat, qs, sc


def _attn_b_fwd(qb, kvx, gate, base, sinks):
    s = qb.shape[0]
    pad = kvx.shape[0] - s
    nb = s // TQ
    left = B_KBLOCKS - 1

    def body(q_ref, kv_ref, g_ref, base_ref, sink_ref, o_ref, u_ref, lse_ref, bias_scr):
        b = pl.program_id(0)

        @pl.when(b == 0)
        def _():
            _b_build_bias(base_ref, bias_scr)

        def step(first_blocks):
            kvv = _window(kv_ref, b, pad, B_WIN, slice(None))
            upper = lax.broadcasted_iota(jnp.int32, (LANES, B_ROWS), 0) < HEAD_DIM
            lse_rows = []
            for gi in range(B_KV_HEADS):
                kcat, vcat, qs, sc = _b_scores_t(q_ref, kvv, bias_scr, gi, b, left, first_blocks)
                sink = _b_sink_rows(sink_ref, gi)
                ps, inv = [], []
                for e in range(2):
                    sh = sc[e * B_WIN:(e + 1) * B_WIN]
                    m = jnp.maximum(jnp.max(sh, axis=0, keepdims=True), sink[e])
                    ex = jnp.exp(sh - m)
                    l = jnp.sum(ex, axis=0, keepdims=True) + jnp.exp(sink[e] - m)
                    ps.append(ex.astype(BF16))
                    inv.append(1.0 / l)
                    lse_rows.append(m + jnp.log(l))
                pt = jnp.concatenate(ps, axis=0)
                ot = lax.dot_general(vcat, pt, TN, preferred_element_type=F32) * jnp.where(upper, inv[0], inv[1])
                ov = ot.T
                for pr in range(B_STACK):
                    pair = B_STACK * gi + pr
                    o_ref[:, pair * LANES:(pair + 1) * LANES] = ov[pr * TQ:(pr + 1) * TQ]
            lse_ref[0] = jnp.concatenate(lse_rows + [jnp.zeros((8 - len(lse_rows), B_ROWS), F32)], axis=0)
            sg, _ = _silu_parts(g_ref[...])
            u_ref[...] = (o_ref[...] * sg).astype(BF16)

        @pl.when(b < left)
        def _():
            step(True)

        @pl.when(b >= left)
        def _():
            step(False)

    row = pl.BlockSpec((TQ, D_MODEL), lambda b: (b, 0))
    return pl.pallas_call(
        body, name="attn_b_fwd", grid=(nb,),
        in_specs=[row, pl.BlockSpec((pad + s, B_KVX), lambda b: (0, 0)), row,
                  pl.BlockSpec((HEADS, 1, B_WIDE), lambda b: (0, 0, 0)), pl.BlockSpec((1, HEADS), lambda b: (0, 0))],
        out_specs=[row, row, pl.BlockSpec((1, 8, B_ROWS), lambda b: (b, 0, 0))],
        out_shape=[jax.ShapeDtypeStruct((s, D_MODEL), F32), jax.ShapeDtypeStruct((s, D_MODEL), BF16),
                   jax.ShapeDtypeStruct((nb, 8, B_ROWS), F32)],
        scratch_shapes=[pltpu.VMEM((B_KV_HEADS, 2 * B_WIN, B_ROWS), F32)],
        compiler_params=_params(("arbitrary",)),
    )(qb, kvx, gate, base, sinks)


def _attn_b_bwd(qb, kvx, gate, o, du, lse, base, sinks):
    s = qb.shape[0]
    pad = kvx.shape[0] - s
    nb = s // TQ
    left = B_KBLOCKS - 1
    half = D_MODEL // 2

    def body(q_ref, kv_ref, g_ref, o_ref, du_ref, lse_ref, base_ref, sink_ref, dz_ref, dkv_ref, dsum_ref,
             dsink_ref, bias_scr, dbias_acc, dkv_acc, dsink_acc):
        b = pl.program_id(0)

        @pl.when(b == 0)
        def _():
            _b_build_bias(base_ref, bias_scr)
            dbias_acc[...] = jnp.zeros_like(dbias_acc)
            dkv_acc[...] = jnp.zeros_like(dkv_acc)
            dsink_acc[...] = jnp.zeros_like(dsink_acc)

        def step(first_blocks):
            kvv = _window(kv_ref, b, pad, B_WIN, slice(None))
            sg, dsg = _silu_parts(g_ref[...])
            duv = du_ref[...]
            ov = o_ref[...]
            do = duv * sg
            dgate = (duv * ov * dsg).astype(BF16)
            dz_ref[2] = dgate[:, :half]
            dz_ref[3] = dgate[:, half:]
            do_o = do * ov
            do_bf = do.astype(BF16)
            lse_all = lse_ref[0]
            dsink_rows = []
            for gi in range(B_KV_HEADS):
                kcat, vcat, qs, sc = _b_scores_t(q_ref, kvv, bias_scr, gi, b, left, first_blocks)
                dos = _b_stack(do_bf, gi)
                doo_t = _b_stack(do_o, gi).T
                delta = (jnp.sum(doo_t[:HEAD_DIM], axis=0, keepdims=True),
                         jnp.sum(doo_t[HEAD_DIM:], axis=0, keepdims=True))
                sink = _b_sink_rows(sink_ref, gi)
                dp = lax.dot_general(vcat, dos, NT, preferred_element_type=F32)
                ps, dss = [], []
                for e in range(2):
                    lse_e = lse_all[2 * gi + e:2 * gi + e + 1]
                    delta_e = delta[e]
                    p = jnp.exp(sc[e * B_WIN:(e + 1) * B_WIN] - lse_e)
                    ps.append(p.astype(BF16))
                    dss.append(p * (dp[e * B_WIN:(e + 1) * B_WIN] - delta_e))
                    dsink_rows.append(-jnp.exp(sink[e] - lse_e) * delta_e)
                ds = jnp.concatenate(dss, axis=0)
                dbias_acc[gi] += ds
                dsb = ds.astype(BF16)
                dq = (lax.dot_general(kcat, dsb, TN, preferred_element_type=F32) * SCALE).T.astype(BF16)
                for pr in range(B_STACK):
                    dz_ref[gi, :, pr * LANES:(pr + 1) * LANES] = dq[pr * TQ:(pr + 1) * TQ]
                dk = _unstack_pair(jnp.dot(dsb, qs, preferred_element_type=F32), B_WIN)
                dv = _unstack_pair(jnp.dot(jnp.concatenate(ps, axis=0), dos, preferred_element_type=F32), B_WIN)
                krows = pl.ds(pl.multiple_of(b * TQ + pad - (B_WIN - TQ), KB), B_WIN)
                dkv_acc[krows, gi * LANES:(gi + 1) * LANES] += dk
                dkv_acc[krows, (B_KV_HEADS + gi) * LANES:(B_KV_HEADS + gi + 1) * LANES] += dv
            dsink_acc[...] += jnp.concatenate(
                dsink_rows + [jnp.zeros((8 - len(dsink_rows), B_ROWS), F32)], axis=0)

        @pl.when(b < left)
        def _():
            step(True)

        @pl.when(b >= left)
        def _():
            step(False)

        @pl.when(b == nb - 1)
        def _():
            lo_s = _lane_lo(s)
            for which in range(2):
                folded = []
                for gi in range(B_KV_HEADS):
                    part = dkv_acc[pad:pad + s, (which * B_KV_HEADS + gi) * LANES:(which * B_KV_HEADS + gi + 1) * LANES]
                    folded.append(part + pltpu.roll(part, HEAD_DIM, 1))
                dkv_ref[:, which * LANES:(which + 1) * LANES] = jnp.where(lo_s, folded[0], folded[1]).astype(BF16)
            lane8 = lax.broadcasted_iota(jnp.int32, dsink_ref.shape, 1)
            tot = jnp.zeros(dsink_ref.shape, F32)
            for h in range(HEADS):
                gi, pr, e = _b_head_place(h)
                dsum_ref[h] = _toeplitz_sum_t(
                    dbias_acc[gi, e * B_WIN:(e + 1) * B_WIN, pr * TQ:(pr + 1) * TQ], B_WIN)
                per_query = dsink_acc[2 * gi + e:2 * gi + e + 1, pr * TQ:(pr + 1) * TQ]
                tot = jnp.where(lane8 == h, jnp.sum(per_query, axis=1, keepdims=True), tot)
            dsink_ref[...] = tot

    row = pl.BlockSpec((TQ, D_MODEL), lambda b: (b, 0))
    base_spec = pl.BlockSpec((HEADS, 1, B_WIDE), lambda b: (0, 0, 0))
    return pl.pallas_call(
        body, name="attn_b_bwd", grid=(nb,),
        in_specs=[row, pl.BlockSpec((pad + s, B_KVX), lambda b: (0, 0)), row, row, row,
                  pl.BlockSpec((1, 8, B_ROWS), lambda b: (b, 0, 0)), base_spec,
                  pl.BlockSpec((1, HEADS), lambda b: (0, 0))],
        out_specs=[pl.BlockSpec((4, TQ, half), lambda b: (0, b, 0)),
                   pl.BlockSpec((s, 2 * LANES), lambda b: (0, 0)), base_spec,
                   pl.BlockSpec((8, LANES), lambda b: (0, 0))],
        out_shape=[jax.ShapeDtypeStruct((4, s, half), BF16), jax.ShapeDtypeStruct((s, 2 * LANES), BF16),
                   jax.ShapeDtypeStruct((HEADS, 1, B_WIDE), F32), jax.ShapeDtypeStruct((8, LANES), F32)],
        scratch_shapes=[pltpu.VMEM((B_KV_HEADS, 2 * B_WIN, B_ROWS), F32),
                        pltpu.VMEM((B_KV_HEADS, 2 * B_WIN, B_ROWS), F32),
                        pltpu.VMEM((pad + s, B_KVX), F32), pltpu.VMEM((8, B_ROWS), F32)],
        compiler_params=_params(("arbitrary",)),
    )(qb, kvx, gate, o, du, lse, base, sinks)


def _t5_bucket(rel):
    nb = T5_BUCKETS // 2
    max_exact = nb // 2
    ret = jnp.where(rel > 0, nb, 0)
    n = jnp.abs(rel)
    nf = jnp.maximum(n, 1).astype(jnp.float32)
    large = max_exact + (jnp.log(nf / max_exact) / math.log(T5_MAX_DIST / max_exact)
                         * (nb - max_exact)).astype(jnp.int32)
    large = jnp.minimum(large, nb - 1)
    return ret + jnp.where(n < max_exact, n, large)


def _a_offset_onehot():
    c = np.arange(A_WIN + TQ)
    dist = A_LEFT_CHUNKS * CHUNK + TQ - 1 - c
    idx = np.clip(dist, -A_REL_CLIP, A_REL_CLIP) + A_REL_CLIP
    onehot = np.zeros((A_WIN + TQ, 2 * A_REL_CLIP + 1), np.float32)
    onehot[c, idx] = 1.0
    return jnp.asarray(onehot)


def _b_offset_onehot():
    c = jnp.arange(B_WIN + TQ, dtype=jnp.int32)
    rel = c - (TQ - 1) - B_LEFT_CHUNKS * CHUNK
    return (_t5_bucket(rel)[:, None] == jnp.arange(T5_BUCKETS)[None, :]).astype(F32)


def _diag_rows(onehot, table):
    rows = jnp.dot(onehot, table.astype(F32), precision=lax.Precision.HIGHEST)
    return rows.T.reshape(HEADS, 1, onehot.shape[0])


def _diag_rows_grad(onehot, ddiag):
    return jnp.dot(ddiag.reshape(HEADS, onehot.shape[0]), onehot, precision=lax.Precision.HIGHEST).T


def _position():
    x, y, c = lax.axis_index("x"), lax.axis_index("y"), lax.axis_index("c")
    chips = [(1 - x, y), (x, 1 - y), (1 - x, 1 - y)]
    return x, y, c, chips


ANY = pl.BlockSpec(memory_space=pl.ANY)


def _allgather_hosted(shards, split):
    n = len(shards)

    def part(ref, t, half):
        if not split[t]:
            return ref
        rows = shards[t].shape[0] // 2
        return ref.at[pl.ds(half * rows, rows)]

    def copies(kind, ins, outs, sems):
        send_sems, recv_sems, pass_send, pass_recv, local_sems = sems
        x, y, c, chips = _position()
        mine = 2 * x + y
        if kind == "local":
            return [pltpu.make_async_copy(ins[t], outs[t].at[mine], local_sems.at[t]) for t in range(n)]
        made = []
        for t in range(n):
            for j, chip in enumerate(chips):
                theirs = 2 * chip[0] + chip[1]
                far = dict(send_sem=send_sems.at[3 * t + j], recv_sem=recv_sems.at[3 * t + j],
                           device_id=(chip[0], chip[1], c), device_id_type=MESH)
                near = dict(send_sem=pass_send.at[3 * t + j], recv_sem=pass_recv.at[3 * t + j],
                            device_id=(x, y, 1 - c), device_id_type=MESH)
                here = part(outs[t].at[theirs], t, c)
                if kind == "send":
                    made.append(pltpu.make_async_remote_copy(
                        src_ref=part(ins[t], t, c), dst_ref=part(outs[t].at[mine], t, c), **far))
                elif kind == "landed":
                    made.append(pltpu.make_async_remote_copy(src_ref=here, dst_ref=here, **far))
                elif not split[t]:
                    made.append(None)
                elif kind == "pass":
                    made.append(pltpu.make_async_remote_copy(src_ref=here, dst_ref=here, **near))
                else:
                    other = part(outs[t].at[theirs], t, 1 - c)
                    made.append(pltpu.make_async_remote_copy(src_ref=other, dst_ref=other, **near))
        return made

    def first(ins, outs, sems):
        for cp in copies("local", ins, outs, sems) + copies("send", ins, outs, sems):
            cp.start()

    def middle(ins, outs, sems):
        for got, cp in zip(copies("landed", ins, outs, sems), copies("pass", ins, outs, sems)):
            got.wait_recv()
            if cp is not None:
                cp.start()

    def last(ins, outs, sems):
        for cp in copies("passed", ins, outs, sems):
            if cp is not None:
                cp.wait_recv()
        for cp in copies("send", ins, outs, sems) + copies("pass", ins, outs, sems):
            if cp is not None:
                cp.wait_send()
        for cp in copies("local", ins, outs, sems):
            cp.wait()

    return _Hosted(shards, [jax.ShapeDtypeStruct((4,) + w.shape, w.dtype) for w in shards],
                   [pltpu.SemaphoreType.DMA((3 * n,))] * 4 + [pltpu.SemaphoreType.DMA((n,))],
                   first, middle, last)


def _allgather_routed(shards):
    n = len(shards)

    def piece(block_ref, t, c, quarter=None):
        half = shards[t].shape[0] // 2
        if quarter is None:
            return block_ref.at[pl.ds(c * half, half)]
        return block_ref.at[pl.ds(c * half + quarter * (half // 2), half // 2)]

    def copies(kind, ins, outs, sems):
        ici_send, ici_recv, pass_send, pass_recv, local_sems = sems
        x, y, c, chips = _position()
        mine = 2 * x + y
        if kind == "local":
            return [pltpu.make_async_copy(ins[t], outs[t].at[mine], local_sems.at[t]) for t in range(n)]
        ids = [2 * chip[0] + chip[1] for chip in chips]
        made = []
        for t in range(n):
            def ici(k, to):
                return dict(send_sem=ici_send.at[4 * t + k], recv_sem=ici_recv.at[4 * t + k],
                            device_id=(chips[to][0], chips[to][1], c), device_id_type=MESH)

            def d2d(k):
                return dict(send_sem=pass_send.at[4 * t + k], recv_sem=pass_recv.at[4 * t + k],
                            device_id=(x, y, 1 - c), device_id_type=MESH)

            def same(ref, where):
                return pltpu.make_async_remote_copy(src_ref=ref, dst_ref=ref, **where)

            if kind == "send":
                for k in range(2):
                    made.append(pltpu.make_async_remote_copy(
                        src_ref=piece(ins[t], t, c), dst_ref=piece(outs[t].at[mine], t, c), **ici(k, k)))
            elif kind == "landed":
                made += [same(piece(outs[t].at[ids[k]], t, c), ici(k, k)) for k in range(2)]
            elif kind == "forward":
                made.append(same(piece(outs[t].at[ids[0]], t, c, 0), ici(2, 1)))
                made.append(same(piece(outs[t].at[ids[1]], t, c, 1), ici(3, 0)))
            elif kind == "arrived":
                made.append(same(piece(outs[t].at[ids[2]], t, c, 0), ici(2, 1)))
                made.append(same(piece(outs[t].at[ids[2]], t, c, 1), ici(3, 0)))
            else:
                core = 1 - c if kind == "passed" else c
                if kind in ("pass halves", "passed"):
                    made += [same(piece(outs[t].at[ids[k]], t, core), d2d(k)) for k in range(2)]
                if kind in ("pass quarters", "passed"):
                    made += [same(piece(outs[t].at[ids[2]], t, core, k), d2d(2 + k)) for k in range(2)]
        return made

    def first(ins, outs, sems):
        for cp in copies("local", ins, outs, sems) + copies("send", ins, outs, sems):
            cp.start()

    def middle(ins, outs, sems):
        for got, onward, near in zip(copies("landed", ins, outs, sems), copies("forward", ins, outs, sems),
                                     copies("pass halves", ins, outs, sems)):
            got.wait_recv()
            near.start()
            onward.start()

    def last(ins, outs, sems):
        quarters = copies("pass quarters", ins, outs, sems)
        for got, near in zip(copies("arrived", ins, outs, sems), quarters):
            got.wait_recv()
            near.start()
        for cp in copies("passed", ins, outs, sems):
            cp.wait_recv()
        for cp in (copies("send", ins, outs, sems) + copies("forward", ins, outs, sems)
                   + copies("pass halves", ins, outs, sems) + quarters):
            cp.wait_send()
        for cp in copies("local", ins, outs, sems):
            cp.wait()

    return _Hosted(shards, [jax.ShapeDtypeStruct((4,) + w.shape, w.dtype) for w in shards],
                   [pltpu.SemaphoreType.DMA((4 * n,))] * 4 + [pltpu.SemaphoreType.DMA((n,))],
                   first, middle, last)


def _scatter_hosted(grads):
    n = len(grads)

    def copies(ins, outs, sems):
        send_sems, recv_sems = sems
        x, y, c, chips = _position()
        return [pltpu.make_async_remote_copy(
            src_ref=ins[t].at[2 * chip[0] + chip[1]], dst_ref=outs[t].at[j],
            send_sem=send_sems.at[3 * t + j], recv_sem=recv_sems.at[3 * t + j],
            device_id=(chip[0], chip[1], c), device_id_type=MESH)
            for t in range(n) for j, chip in enumerate(chips)]

    def first(ins, outs, sems):
        for cp in copies(ins, outs, sems):
            cp.start()

    def last(ins, outs, sems):
        for cp in copies(ins, outs, sems):
            cp.wait()

    return _Hosted(grads, [jax.ShapeDtypeStruct((3,) + g.shape[1:], g.dtype) for g in grads],
                   [pltpu.SemaphoreType.DMA((3 * n,))] * 2, first, None, last)


GATHER_PEERS = "x and y neighbours (same core) and the sibling core"
SCATTER_PEERS = "the same core of the three other chips"
EVERYONE = "the seven other devices"


def _run_on_sequencer(name, hosted, peers, collective_id):
    ins = [jax.new_ref(a, memory_space=pltpu.MemorySpace.HBM) for a in hosted.inputs]
    outs = [jax.empty_ref(shape, memory_space=pltpu.MemorySpace.HBM) for shape in hosted.out_shapes]

    @pl.kernel(mesh=plsc.ScalarSubcoreMesh(axis_name="sequencer", num_cores=1), name=name,
               scratch_types=tuple(hosted.sems), compiler_params=pltpu.CompilerParams(collective_id=collective_id))
    def launch(*sems):
        x, y, c, chips = _position()
        if peers == GATHER_PEERS:
            devices = [(chip[0], chip[1], c) for chip in chips[:2]] + [(x, y, 1 - c)]
        elif peers == SCATTER_PEERS:
            devices = [(chip[0], chip[1], c) for chip in chips]
        else:
            devices = [(x ^ (k >> 2), y ^ ((k >> 1) & 1), c ^ (k & 1)) for k in range(1, 8)]
        barrier = pltpu.get_barrier_semaphore()
        for device in devices:
            pl.semaphore_signal(barrier, inc=1, device_id=device, device_id_type=MESH)
        pl.semaphore_wait(barrier, len(devices))
        hosted.first(ins, outs, sems)
        if hosted.middle is not None:
            hosted.middle(ins, outs, sems)
        hosted.last(ins, outs, sems)

    launch()
    return [o[...] for o in outs]


def _run_alone(name, hosted):
    n_in = len(hosted.inputs)
    n_out = len(hosted.out_shapes)

    def body(*refs):
        ins, outs, sems = refs[:n_in], refs[n_in:n_in + n_out], refs[n_in + n_out:]
        hosted.first(ins, outs, sems)
        if hosted.middle is not None:
            hosted.middle(ins, outs, sems)
        hosted.last(ins, outs, sems)

    return pl.pallas_call(
        body, name=name, in_specs=[ANY] * n_in, out_specs=[ANY] * n_out, out_shape=hosted.out_shapes,
        scratch_shapes=hosted.sems)(*hosted.inputs)


def _gather_gain(shard):
    def body(in_ref, out_ref, send_sems, recv_sems):
        x, y, c, chips = _position()
        out_ref[2 * x + y] = in_ref[...]
        sends = [pltpu.make_async_remote_copy(
            src_ref=in_ref, dst_ref=out_ref.at[2 * x + y], send_sem=send_sems.at[j], recv_sem=recv_sems.at[j],
            device_id=(chip[0], chip[1], c), device_id_type=MESH) for j, chip in enumerate(chips)]
        for cp in sends:
            cp.start()
        for j, chip in enumerate(chips):
            pltpu.make_async_remote_copy(
                src_ref=in_ref, dst_ref=out_ref.at[2 * chip[0] + chip[1]], send_sem=send_sems.at[j],
                recv_sem=recv_sems.at[j], device_id=(chip[0], chip[1], c), device_id_type=MESH).wait_recv()
        for cp in sends:
            cp.wait_send()

    vmem = pl.BlockSpec(memory_space=pltpu.VMEM)
    return pl.pallas_call(
        body, name="gather_gain", in_specs=[vmem], out_specs=vmem,
        out_shape=jax.ShapeDtypeStruct((4,) + shard.shape, shard.dtype),
        scratch_shapes=[pltpu.SemaphoreType.DMA((3,))] * 2,
    )(shard)


def _swap_with_sibling(name, blocks):
    n = len(blocks)

    def body(*refs):
        ins, outs = refs[:n], refs[n:2 * n]
        send_sems, recv_sems = refs[2 * n:]
        x, y, c, _ = _position()
        sends = [pltpu.make_async_remote_copy(
            src_ref=ins[t], dst_ref=outs[t], send_sem=send_sems.at[t], recv_sem=recv_sems.at[t],
            device_id=(x, y, 1 - c), device_id_type=MESH) for t in range(n)]
        for cp in sends:
            cp.start()
        for cp in sends:
            cp.wait()

    return pl.pallas_call(
        body, name=name,
        in_specs=[ANY] * n, out_specs=[ANY] * n,
        out_shape=[jax.ShapeDtypeStruct(b.shape, b.dtype) for b in blocks],
        scratch_shapes=[pltpu.SemaphoreType.DMA((n,))] * 2,
    )(*blocks)


def _everyone_hosted(terms):
    nt = len(terms)

    def copies(kind, ins, outs, sems):
        send_sems, recv_sems, local_sems = sems
        x, y, c, _ = _position()
        me = 4 * x + 2 * y + c
        if kind == "local":
            return [pltpu.make_async_copy(ins[t], outs[t].at[me], local_sems.at[t]) for t in range(nt)]
        made = []
        for t in range(nt):
            for k in range(1, 8):
                peer = (x ^ (k >> 2), y ^ ((k >> 1) & 1), c ^ (k & 1))
                slot = me if kind == "send" else me ^ k
                made.append(pltpu.make_async_remote_copy(
                    src_ref=ins[t], dst_ref=outs[t].at[slot], send_sem=send_sems.at[7 * t + k - 1],
                    recv_sem=recv_sems.at[7 * t + k - 1], device_id=peer, device_id_type=MESH))
        return made

    def first(ins, outs, sems):
        for cp in copies("local", ins, outs, sems) + copies("send", ins, outs, sems):
            cp.start()

    def last(ins, outs, sems):
        for cp in copies("landed", ins, outs, sems):
            cp.wait_recv()
        for cp in copies("send", ins, outs, sems):
            cp.wait_send()
        for cp in copies("local", ins, outs, sems):
            cp.wait()

    return _Hosted(terms, [jax.ShapeDtypeStruct((8,) + a.shape, F32) for a in terms],
                   [pltpu.SemaphoreType.DMA((7 * nt,))] * 2 + [pltpu.SemaphoreType.DMA((nt,))], first, None, last)


def _small_step(partials, extras, ws, ms, vs, shard_of):
    n = len(partials)
    terms = list(partials) + list(extras)
    nt = len(terms)
    rows = [t for t in range(nt) if terms[t].shape[0] == 1]
    mats = [t for t in range(nt) if terms[t].shape[0] != 1]
    row_block = (8, max(terms[t].shape[1] for t in rows))
    assert len(rows) <= row_block[0]
    vmem = pl.BlockSpec(memory_space=pltpu.VMEM)

    def pack(*refs):
        packed = refs[-1]
        packed[...] = jnp.zeros_like(packed)
        for i, t in enumerate(rows):
            packed[i:i + 1, 0:terms[t].shape[1]] = refs[i][...]

    packed = pl.pallas_call(pack, name="small_pack", in_specs=[vmem] * len(rows), out_specs=vmem,
                            out_shape=jax.ShapeDtypeStruct(row_block, F32))(*[terms[t] for t in rows])
    slots = _run_on_sequencer("allgather_small", _everyone_hosted([packed] + [terms[t] for t in mats]),
                              EVERYONE, 2)

    def body(*refs):
        slot_refs, refs = refs[:len(slots)], refs[len(slots):]
        w_refs, refs = refs[:n], refs[n:]
        m_refs, refs = refs[:n], refs[n:]
        v_refs, outs = refs[:n], refs[n:]
        sums = []
        for ref in slot_refs:
            g = ref[0]
            for dev in range(1, 8):
                g = g + ref[dev]
            sums.append(g)
        chip = 2 * lax.axis_index("x") + lax.axis_index("y")
        for t in range(nt):
            if t in rows:
                i = rows.index(t)
                g = sums[0][i:i + 1, 0:terms[t].shape[1]]
            else:
                g = sums[1 + mats.index(t)]
            if t >= n:
                outs[4 * n + t - n][...] = g
                continue
            if shard_of[t]:
                width = ws[t].shape[-1]
                mine = jnp.zeros(ws[t].shape, F32)
                for s in range(4):
                    mine = jnp.where(chip == s, g[:, s * width:(s + 1) * width], mine)
                g = mine
            delta, mn, vn = _adamw_math(w_refs[t][...], g, m_refs[t][...], v_refs[t][...])
            outs[4 * t][...] = g
            outs[4 * t + 1][...] = delta
            outs[4 * t + 2][...] = mn
            outs[4 * t + 3][...] = vn

    out_shapes = []
    for t in range(n):
        out_shapes += [jax.ShapeDtypeStruct(ws[t].shape, F32)] * 4
    out_shapes += [jax.ShapeDtypeStruct(a.shape, F32) for a in extras]
    res = pl.pallas_call(
        body, name="small_step",
        in_specs=[vmem] * (len(slots) + 3 * n), out_specs=[vmem] * len(out_shapes), out_shape=out_shapes,
    )(*slots, *ws, *ms, *vs)
    return [res[4 * t:4 * t + 4] for t in range(n)], res[4 * n:4 * n + nt - n]


def _adamw_math(w, g, m, v):
    m = ADAM_B1 * m + (1.0 - ADAM_B1) * g
    v = ADAM_B2 * v + (1.0 - ADAM_B2) * (g * g)
    m_hat = m / (1.0 - ADAM_B1 ** ADAM_STEP)
    v_hat = v / (1.0 - ADAM_B2 ** ADAM_STEP)
    delta = -ADAM_LR * (m_hat / (jnp.sqrt(v_hat) + ADAM_EPS) + ADAM_WD * w)
    return delta, m, v


def _row_tile(rows):
    return 256 if rows % 256 == 0 else rows


def _sum_partials(name, own, recv, chip, after):
    rows, cols = own.shape[1:]
    tr = _row_tile(rows)

    def body(chip_ref, own_ref, recv_ref, after_ref, o_ref):
        acc = own_ref[...]
        for j in range(3):
            acc = acc + recv_ref[j].astype(F32)
        o_ref[...] = acc

    return pl.pallas_call(
        body, name=name,
        grid_spec=pltpu.PrefetchScalarGridSpec(
            num_scalar_prefetch=1, grid=(rows // tr,),
            in_specs=[pl.BlockSpec((None, tr, cols), lambda i, chip_ref: (chip_ref[0], i, 0)),
                      pl.BlockSpec((3, tr, cols), lambda i, chip_ref: (0, i, 0)), ANY],
            out_specs=pl.BlockSpec((tr, cols), lambda i, chip_ref: (i, 0))),
        out_shape=jax.ShapeDtypeStruct((rows, cols), F32),
        compiler_params=_params(("parallel",)),
    )(chip.reshape(1).astype(jnp.int32), own, recv, after)


def _adamw(name, w, m, v, g_parts):
    rows, cols = w.shape
    tr = _row_tile(rows)
    n = len(g_parts)

    def body(w_ref, m_ref, v_ref, *refs):
        g_refs = refs[:n]
        go_ref, d_ref, mo_ref, vo_ref = refs[n:]
        g = g_refs[0][...]
        for r in g_refs[1:]:
            g = g + r[...]
        delta, mn, vn = _adamw_math(w_ref[...], g, m_ref[...], v_ref[...])
        go_ref[...] = g
        d_ref[...] = delta
        mo_ref[...] = mn
        vo_ref[...] = vn

    spec = pl.BlockSpec((tr, cols), lambda i: (i, 0))
    return pl.pallas_call(
        body, name=name, grid=(rows // tr,),
        in_specs=[spec] * (3 + n), out_specs=[spec] * 4,
        out_shape=[jax.ShapeDtypeStruct((rows, cols), F32)] * 4,
        compiler_params=_params(("parallel",)),
    )(w, m, v, *g_parts)


def _local_step(x, target, ga, wa_in, rel_bias, later_shards, gk, t5, gb, sinks, gf):
    s, d = x.shape
    tm = min(TM_DENSE, s)
    nt = s // tm
    half = d // 2
    row = pl.BlockSpec((tm, d), lambda i: (i, 0))
    whole = lambda shape: pl.BlockSpec(shape, lambda *_: (0,) * len(shape))

    n1, = _norm_fwd("norm_a", x, ga)
    zqkv = _matmul("proj_a_qkv", n1, wa_in, dims=NN, grid=(3, nt + 1), zero_axis=1,
                   a_spec=pl.BlockSpec((tm, d), lambda j, i: (jnp.maximum(i - 1, 0), 0)),
                   b_spec=pl.BlockSpec((None, d, d), lambda j, i: (j, 0, 0)),
                   o_spec=pl.BlockSpec((None, tm, d), lambda j, i: (j, i, 0)),
                   out_shape=(3, tm + s, d), out_dtype=BF16)
    gate_a = _matmul("proj_a_gate", n1, wa_in, dims=NN, grid=(nt,),
                     a_spec=row, b_spec=pl.BlockSpec((None, d, d), lambda i: (3, 0, 0)), o_spec=row,
                     out_shape=(s, d), out_dtype=F32)
    onehot_a = _a_offset_onehot()
    diag_a = _diag_rows(onehot_a, rel_bias)
    (o_a, u_a, lse_a), gathered = _attn_a_fwd(zqkv, gate_a, diag_a, hosted=_allgather_routed(later_shards))
    wa_out, wkv, wb_in, wb_out = gathered
    wa_out = wa_out.reshape(d, d)
    wkv = wkv.reshape(d, -1)
    wb_out = wb_out.reshape(d, d)
    h1, nk, n2 = _out_norms("out_a_norms", u_a, wa_out, x, jnp.concatenate([gk, gb], axis=0))
    kvw = wkv.shape[1]
    wkv_x = jnp.concatenate([wkv[:, (i // 2) * HEAD_DIM:(i // 2 + 1) * HEAD_DIM] for i in range(8)], axis=1)
    kvx = _matmul("proj_kv", nk, wkv_x, dims=NN, grid=(nt + 1,), zero_axis=0,
                  a_spec=pl.BlockSpec((tm, d), lambda i: (jnp.maximum(i - 1, 0), 0)), b_spec=whole((d, B_KVX)),
                  o_spec=pl.BlockSpec((tm, B_KVX), lambda i: (i, 0)), out_shape=(tm + s, B_KVX), out_dtype=BF16)
    qb = _matmul("proj_b_q", n2, wb_in, dims=NN, grid=(2, nt),
                 a_spec=pl.BlockSpec((tm, d), lambda j, i: (i, 0)),
                 b_spec=pl.BlockSpec((None, d, half), lambda j, i: (j, 0, 0)),
                 o_spec=pl.BlockSpec((tm, half), lambda j, i: (i, j)), out_shape=(s, d), out_dtype=BF16)
    gate_b = _matmul("proj_b_gate", n2, wb_in, dims=NN, grid=(2, nt),
                     a_spec=pl.BlockSpec((tm, d), lambda j, i: (i, 0)),
                     b_spec=pl.BlockSpec((None, d, half), lambda j, i: (2 + j, 0, 0)),
                     o_spec=pl.BlockSpec((tm, half), lambda j, i: (i, j)), out_shape=(s, d), out_dtype=F32)
    onehot_b = _b_offset_onehot()
    base_b = jnp.roll(_diag_rows(onehot_b, t5)[..., ::-1], TQ, axis=-1)
    o_b, u_b, lse_b = _attn_b_fwd(qb, kvx, gate_b, base_b, sinks)
    dh2, loss, d_gf = _out_loss_head(u_b, wb_out, h1, target, gf)

    du_b = _matmul("dout_b", dh2, wb_out, dims=NT, grid=(nt,), a_spec=row, b_spec=whole((d, d)), o_spec=row,
                   out_shape=(s, d), out_dtype=F32)
    d_wb_out = _matmul("dw_out_b", u_b, dh2, dims=TN, grid=(2,),
                       a_spec=whole((s, d)), b_spec=pl.BlockSpec((s, half), lambda j: (0, j)),
                       o_spec=pl.BlockSpec((d, half), lambda j: (0, j)),
                       out_shape=(d, d), out_dtype=F32, also_bf16=True)
    dz_b, dkv, dsum_b, dsinks = _attn_b_bwd(qb, kvx, gate_b, o_b, du_b, lse_b, base_b, sinks)
    ddiag_b = jnp.roll(dsum_b[..., ::-1], -1, axis=-1)
    d_wb_in = _matmul("dw_in_b", n2, dz_b, dims=TN, grid=(4,),
                      a_spec=whole((s, d)), b_spec=pl.BlockSpec((None, s, half), lambda j: (j, 0, 0)),
                      o_spec=pl.BlockSpec((None, d, half), lambda j: (j, 0, 0)),
                      out_shape=(4, d, half), out_dtype=F32, also_bf16=True)
    d_wkv = _matmul("dw_kv", nk, dkv, dims=TN, grid=(1,),
                    a_spec=whole((s, d)), b_spec=whole((s, kvw)), o_spec=whole((d, kvw)),
                    out_shape=(d, kvw), out_dtype=F32, also_bf16=True)
    dh1, d_gkb = _proj_norm_bwd("dproj_kv_b", h1, dh2, jnp.concatenate([gk, gb], axis=0),
                                [(dkv[None], wkv[None]), (dz_b, wb_in)])

    du_a = _matmul("dout_a", dh1, wa_out, dims=NT, grid=(nt,), a_spec=row, b_spec=whole((d, d)), o_spec=row,
                   out_shape=(s, d), out_dtype=F32)
    d_wa_out = _matmul("dw_out_a", u_a, dh1, dims=TN, grid=(2,),
                       a_spec=whole((s, d)), b_spec=pl.BlockSpec((s, half), lambda j: (0, j)),
                       o_spec=pl.BlockSpec((d, half), lambda j: (0, j)),
                       out_shape=(d, d), out_dtype=F32, also_bf16=True)
    early = dict(a_w_out=[g.reshape(4, d // 4, d) for g in d_wa_out],
                 kv_w=[g.reshape(4, d // 4, kvw) for g in d_wkv], b_w_in=list(d_wb_in),
                 b_w_out=[g.reshape(4, d // 4, d) for g in d_wb_out])
    early_recv = _run_on_sequencer("scatter_early", _scatter_hosted([early[n][1] for n in early]),
                                   SCATTER_PEERS, 3)
    (dz_a, ddiag_a), _ = _attn_a_bwd(zqkv, gate_a, o_a, du_a, lse_a, diag_a,
                                     hosted=_Hosted([early[n][1] for n in early], [], [], None, None, None))
    d_wa_in = _matmul("dw_in_a", n1, dz_a, dims=TN, grid=(4, 2),
                      a_spec=whole((s, d)), b_spec=pl.BlockSpec((None, s, half), lambda j, h: (j, 0, h)),
                      o_spec=pl.BlockSpec((None, d, half), lambda j, h: (j, 0, h)),
                      out_shape=(4, d, d), out_dtype=F32, also_bf16=True)
    late_recv = _run_on_sequencer("scatter_a_w_in", _scatter_hosted([d_wa_in[1]]), SCATTER_PEERS, 0)
    grad_x, d_ga = _proj_norm_bwd("dproj_a", x, dh1, ga, [(dz_a, wa_in)])

    small = dict(a_norm=d_ga, kv_norm=d_gkb[0:1], b_norm=d_gkb[1:2], b_sinks=dsinks[0:1, :HEADS], final_norm=d_gf)
    small["by_offset"] = dict(a_rel_bias=(onehot_a, ddiag_a.reshape(HEADS, -1)),
                              t5_bias=(onehot_b, ddiag_b.reshape(HEADS, -1)))
    own = dict(a_w_in=d_wa_in[0], **{n: early[n][0] for n in early})
    received = dict(a_w_in=late_recv[0], **dict(zip(early, early_recv)))
    return loss, grad_x, small, own, received, d_wa_in[1]


SMALL = ("a_norm", "kv_norm", "b_norm", "b_sinks", "final_norm")
TABLES = ("a_rel_bias", "t5_bias")
BIG = ("a_w_in", "a_w_out", "kv_w", "b_w_in", "b_w_out")
ORDER = ("a_norm", "a_w_in", "a_rel_bias", "a_w_out", "kv_norm", "kv_w", "t5_bias", "b_norm", "b_w_in",
         "b_sinks", "b_w_out", "final_norm")


def kernel(x, a_norm, a_w_in, a_rel_bias, a_w_out, kv_norm, kv_w, t5_bias, b_norm, b_w_in, b_sinks, b_w_out, final_norm, loss_target, m_a_norm, m_a_w_in, m_a_rel_bias, m_a_w_out, m_kv_norm, m_kv_w, m_t5_bias, m_b_norm, m_b_w_in, m_b_sinks, m_b_w_out, m_final_norm, v_a_norm, v_a_w_in, v_a_rel_bias, v_a_w_out, v_kv_norm, v_kv_w, v_t5_bias, v_b_norm, v_b_w_in, v_b_sinks, v_b_w_out, v_final_norm):
    w = dict(a_norm=a_norm, a_w_in=a_w_in, a_rel_bias=a_rel_bias, a_w_out=a_w_out, kv_norm=kv_norm, kv_w=kv_w,
             t5_bias=t5_bias, b_norm=b_norm, b_w_in=b_w_in, b_sinks=b_sinks, b_w_out=b_w_out,
             final_norm=final_norm)
    m = dict(a_norm=m_a_norm, a_w_in=m_a_w_in, a_rel_bias=m_a_rel_bias, a_w_out=m_a_w_out, kv_norm=m_kv_norm,
             kv_w=m_kv_w, t5_bias=m_t5_bias, b_norm=m_b_norm, b_w_in=m_b_w_in, b_sinks=m_b_sinks,
             b_w_out=m_b_w_out, final_norm=m_final_norm)
    v = dict(a_norm=v_a_norm, a_w_in=v_a_w_in, a_rel_bias=v_a_rel_bias, a_w_out=v_a_w_out, kv_norm=v_kv_norm,
             kv_w=v_kv_w, t5_bias=v_t5_bias, b_norm=v_b_norm, b_w_in=v_b_w_in, b_sinks=v_b_sinks,
             b_w_out=v_b_w_out, final_norm=v_final_norm)
    d = D_MODEL
    chip = 2 * lax.axis_index("x") + lax.axis_index("y")

    shard2d = dict(a_w_in=a_w_in[0], a_w_out=a_w_out[0], kv_w=kv_w, b_w_in=b_w_in[0], b_w_out=b_w_out[0])

    wa_in, = _run_on_sequencer("allgather_first", _allgather_routed([shard2d["a_w_in"].astype(BF16)]),
                               GATHER_PEERS, 1)
    ga = _gather_gain(a_norm).reshape(1, d)

    loss, grad_x, small, own, received, after_attention = _local_step(
        x[0], loss_target[0], ga, wa_in, a_rel_bias[0], [shard2d[n].astype(BF16) for n in BIG[1:]],
        kv_norm.reshape(1, d), t5_bias, b_norm, b_sinks, final_norm.reshape(1, d))

    out = {}
    as2d = lambda a: a.reshape(-1, a.shape[-1])
    small_res, (loss_sum, *offset_sums) = _small_step(
        [small[n] for n in SMALL], [loss] + [small["by_offset"][n][1] for n in TABLES],
        [as2d(w[n]) for n in SMALL], [as2d(m[n]) for n in SMALL], [as2d(v[n]) for n in SMALL],
        [n == "a_norm" for n in SMALL])
    for n, res in zip(SMALL, small_res):
        out[n] = [r.reshape(w[n].shape) for r in res]
    loss_out = loss_sum.reshape(())
    for n, summed in zip(TABLES, offset_sums):
        grad = _diag_rows_grad(small["by_offset"][n][0], summed)
        res = _adamw("adamw_" + n, as2d(w[n]), as2d(m[n]), as2d(v[n]), [grad])
        out[n] = [r.reshape(w[n].shape) for r in res]

    core_sums = [_sum_partials("sum_" + n, own[n], received[n], chip, after_attention) for n in BIG]
    sibling_sums = (_swap_with_sibling("swap_last", core_sums[:1])
                    + _swap_with_sibling("swap_early", core_sums[1:]))

    for n, mine, theirs in zip(BIG, core_sums, sibling_sums):
        res = _adamw("adamw_" + n, shard2d[n], m[n].reshape(shard2d[n].shape), v[n].reshape(shard2d[n].shape),
                     [mine, theirs])
        out[n] = [r.reshape(w[n].shape) for r in res]

    grads = [out[n][0] for n in ORDER]
    deltas = [out[n][1] for n in ORDER]
    new_m = [out[n][2] for n in ORDER]
    new_v = [out[n][3] for n in ORDER]
    return (loss_out, grad_x[None], *grads, *deltas, *new_m, *new_v)
```

```python
import functools
import math

import jax
import jax.numpy as jnp
import numpy as np
from jax import lax
from jax.experimental import pallas as pl
from jax.experimental.pallas import tpu as pltpu
from jax.experimental.pallas import tpu_sc as plsc

F32 = jnp.float32
BF16 = jnp.bfloat16
MESH = pl.DeviceIdType.MESH

D_MODEL = 1024
HEADS = 16
HEAD_DIM = 64
CHUNK = 64
RMS_EPS = 1e-6
SCALE = HEAD_DIM ** -0.5
A_LEFT_CHUNKS = 8
A_REL_CLIP = 256
B_LEFT_CHUNKS = 2
B_KV_HEADS = 2
B_GROUP = HEADS // B_KV_HEADS
T5_BUCKETS = 32
T5_MAX_DIST = 128
ADAM_LR = 0.001
ADAM_B1 = 0.9
ADAM_B2 = 0.999
ADAM_EPS = 1e-08
ADAM_WD = 0.01
ADAM_STEP = 10

MASKED = -1e30
LANES = 128
TQ = 128
A_PAIRS = 2
A_PAIRS_FWD = 4
KB = 128
A_KBLOCKS = A_LEFT_CHUNKS * CHUNK // KB + 1
B_KBLOCKS = B_LEFT_CHUNKS * CHUNK // KB + 1
A_WIN = A_KBLOCKS * KB
B_WIN = B_KBLOCKS * KB
TM = 512
TM_DENSE = 1024
TM_PARTS = 512
VMEM_LIMIT = 56 * 1024 * 1024

NT = (((1,), (1,)), ((), ()))
TN = (((0,), (0,)), ((), ()))
NN = (((1,), (0,)), ((), ()))


def _params(sem=None):
    return pltpu.CompilerParams(dimension_semantics=sem, vmem_limit_bytes=VMEM_LIMIT)


class _Hosted:
    def __init__(self, inputs, out_shapes, sems, first, middle, last):
        self.inputs, self.out_shapes, self.sems = list(inputs), list(out_shapes), list(sems)
        self.first, self.middle, self.last = first, middle, last


def _call(body, *, name, grid, in_specs, out_specs, out_shape, args, scratch_shapes=(), sem=None, hosted=None):
    in_specs, out_specs, out_shape = list(in_specs), list(out_specs), list(out_shape)
    scratch_shapes = list(scratch_shapes)
    if hosted is None:
        out = pl.pallas_call(
            body, name=name, grid=grid, in_specs=in_specs, out_specs=out_specs, out_shape=out_shape,
            scratch_shapes=scratch_shapes, compiler_params=_params(sem))(*args)
        return list(out), []
    n_in, n_out, n_scr = len(in_specs), len(out_shape), len(scratch_shapes)
    h_in, h_out = len(hosted.inputs), len(hosted.out_shapes)
    total = int(np.prod(grid)) if grid else 1

    def wrapped(*refs):
        ins, refs = refs[:n_in], refs[n_in:]
        h_ins, refs = refs[:h_in], refs[h_in:]
        outs, refs = refs[:n_out], refs[n_out:]
        h_outs, refs = refs[:h_out], refs[h_out:]
        scr, h_sems = refs[:n_scr], refs[n_scr:]
        step = 0
        for axis, size in enumerate(grid):
            step = step * size + pl.program_id(axis)

        if hosted.first is not None:
            @pl.when(step == 0)
            def _():
                hosted.first(h_ins, h_outs, h_sems)

        body(*ins, *outs, *scr)
        if hosted.middle is not None:
            @pl.when(step == total // 2)
            def _():
                hosted.middle(h_ins, h_outs, h_sems)

        if hosted.last is not None:
            @pl.when(step == total - 1)
            def _():
                hosted.last(h_ins, h_outs, h_sems)

    out = pl.pallas_call(
        wrapped, name=name, grid=grid, in_specs=in_specs + [ANY] * h_in, out_specs=out_specs + [ANY] * h_out,
        out_shape=out_shape + hosted.out_shapes, scratch_shapes=scratch_shapes + hosted.sems,
        compiler_params=_params(("arbitrary",) * len(grid)))(*args, *hosted.inputs)
    return list(out[:n_out]), list(out[n_out:])


def _matmul(name, a, b, *, dims, grid, a_spec, b_spec, o_spec, out_shape, out_dtype,
            parts=1, resid=None, resid_spec=None, also_bf16=False, hosted=None, zero_axis=None):
    def body(*refs):
        if zero_axis is None:
            product(*refs)
        else:
            @pl.when(pl.program_id(zero_axis) == 0)
            def _():
                refs[2][...] = jnp.zeros_like(refs[2])

            @pl.when(pl.program_id(zero_axis) > 0)
            def _():
                product(*refs)

    def product(*refs):
        a_ref, b_ref = refs[:2]
        r_ref = refs[2] if resid is not None else None
        o_ref = refs[3] if resid is not None else refs[2]
        if parts == 1:
            prod = lax.dot_general(a_ref[...].astype(BF16), b_ref[...].astype(BF16), dims,
                                   preferred_element_type=F32)
        else:
            prod = None
            for part in range(parts):
                term = lax.dot_general(a_ref[part].astype(BF16), b_ref[part].astype(BF16), dims,
                                       preferred_element_type=F32)
                prod = term if prod is None else prod + term
        if resid is not None:
            prod = r_ref[...] + prod
        o_ref[...] = prod.astype(out_dtype)
        if also_bf16:
            refs[-1][...] = prod.astype(BF16)

    in_specs = [a_spec, b_spec]
    args = [a, b]
    if resid is not None:
        in_specs.append(resid_spec)
        args.append(resid)
    sem = ["parallel"] * len(grid)
    out_specs = [o_spec]
    out_shapes = [jax.ShapeDtypeStruct(out_shape, out_dtype)]
    if also_bf16:
        out_specs.append(o_spec)
        out_shapes.append(jax.ShapeDtypeStruct(out_shape, BF16))
    out, extra = _call(body, name=name, grid=grid, in_specs=in_specs, out_specs=out_specs, out_shape=out_shapes,
                       args=args, sem=tuple(sem), hosted=hosted)
    res = out[0] if not also_bf16 else tuple(out)
    return res if hosted is None else (res, extra)


def _rms_rows(x):
    return lax.rsqrt(jnp.mean(x * x, axis=-1, keepdims=True) + RMS_EPS)


def _norm_fwd(name, x, gains):
    s, d = x.shape
    n = gains.shape[0]

    def body(x_ref, g_ref, *o_refs):
        xv = x_ref[...]
        xh = xv * _rms_rows(xv)
        for i in range(n):
            o_refs[i][...] = (xh * g_ref[i:i + 1, :]).astype(BF16)

    row = pl.BlockSpec((TM, d), lambda i: (i, 0))
    return pl.pallas_call(
        body, name=name, grid=(s // TM,),
        in_specs=[row, pl.BlockSpec((n, d), lambda i: (0, 0))],
        out_specs=[row] * n,
        out_shape=[jax.ShapeDtypeStruct((s, d), BF16)] * n,
        compiler_params=_params(("parallel",)),
    )(x, gains)


def _proj_norm_bwd(name, x, dres, gains, branches):
    s, d = x.shape
    n = len(branches)
    tm = min(TM_PARTS, s)

    def body(x_ref, r_ref, g_ref, *refs):
        ab_refs, dx_ref, dg_ref = refs[:2 * n], refs[2 * n], refs[2 * n + 1]
        i = pl.program_id(0)
        xv = x_ref[...]
        r = _rms_rows(xv)
        xh = xv * r

        @pl.when(i == 0)
        def _():
            dg_ref[...] = jnp.zeros_like(dg_ref)

        a = None
        for j in range(n):
            a_ref, b_ref = ab_refs[2 * j], ab_refs[2 * j + 1]
            dn = None
            for part in range(a_ref.shape[0]):
                term = lax.dot_general(a_ref[part], b_ref[part], NT, preferred_element_type=F32)
                dn = term if dn is None else dn + term
            t = dn * g_ref[j:j + 1, :]
            a = t if a is None else a + t
            dg_ref[j:j + 1, :] += jnp.sum(dn * xh, axis=0, keepdims=True)
        dx_ref[...] = r_ref[...] + r * (a - xh * jnp.mean(xh * a, axis=-1, keepdims=True))

    row = pl.BlockSpec((tm, d), lambda i: (i, 0))
    small = pl.BlockSpec((n, d), lambda i: (0, 0))
    ab_specs, ab_args = [], []
    for a, b in branches:
        ab_specs += [pl.BlockSpec((a.shape[0], tm, a.shape[2]), lambda i: (0, i, 0)),
                     pl.BlockSpec(b.shape, lambda i: (0, 0, 0))]
        ab_args += [a, b]
    return pl.pallas_call(
        body, name=name, grid=(s // tm,),
        in_specs=[row, row, small] + ab_specs,
        out_specs=[row, small],
        out_shape=[jax.ShapeDtypeStruct((s, d), F32), jax.ShapeDtypeStruct((n, d), F32)],
        compiler_params=_params(("arbitrary",)),
    )(x, dres, gains, *ab_args)


def _out_norms(name, u, w_out, resid, gains):
    s, d = resid.shape
    n = gains.shape[0]
    tm = min(TM_DENSE, s)

    def body(u_ref, w_ref, r_ref, g_ref, h_ref, *o_refs):
        hv = r_ref[...] + jnp.dot(u_ref[...], w_ref[...], preferred_element_type=F32)
        h_ref[...] = hv
        hh = hv * _rms_rows(hv)
        for i in range(n):
            o_refs[i][...] = (hh * g_ref[i:i + 1, :]).astype(BF16)

    row = pl.BlockSpec((tm, d), lambda i: (i, 0))
    return pl.pallas_call(
        body, name=name, grid=(s // tm,),
        in_specs=[row, pl.BlockSpec((d, d), lambda i: (0, 0)), row, pl.BlockSpec((n, d), lambda i: (0, 0))],
        out_specs=[row] * (n + 1),
        out_shape=[jax.ShapeDtypeStruct((s, d), F32)] + [jax.ShapeDtypeStruct((s, d), BF16)] * n,
        compiler_params=_params(("parallel",)),
    )(u, w_out, resid, gains)


def _out_loss_head(u, w_out, resid, target, gain):
    s, d = resid.shape
    tm = min(TM_PARTS, s)

    def body(u_ref, w_ref, r_ref, t_ref, g_ref, dh_ref, loss_ref, dg_ref):
        i = pl.program_id(0)
        hv = r_ref[...] + jnp.dot(u_ref[...], w_ref[...], preferred_element_type=F32)
        r = _rms_rows(hv)
        hh = hv * r
        g = g_ref[...]
        err = hh * g - t_ref[...]
        part = 0.5 * jnp.sum(jnp.sum(err * err, axis=-1, keepdims=True) * (1.0 / d), axis=0, keepdims=True)
        dy = err * (1.0 / d)
        a = dy * g
        dh_ref[...] = r * (a - hh * jnp.mean(hh * a, axis=-1, keepdims=True))
        dg = jnp.sum(dy * hh, axis=0, keepdims=True)

        @pl.when(i == 0)
        def _():
            loss_ref[...] = part
            dg_ref[...] = dg

        @pl.when(i > 0)
        def _():
            loss_ref[...] += part
            dg_ref[...] += dg

    row = pl.BlockSpec((tm, d), lambda i: (i, 0))
    return pl.pallas_call(
        body, name="out_b_loss_head", grid=(s // tm,),
        in_specs=[row, pl.BlockSpec((d, d), lambda i: (0, 0)), row, row, pl.BlockSpec((1, d), lambda i: (0, 0))],
        out_specs=[row, pl.BlockSpec((1, 1), lambda i: (0, 0)), pl.BlockSpec((1, d), lambda i: (0, 0))],
        out_shape=[jax.ShapeDtypeStruct((s, d), F32), jax.ShapeDtypeStruct((1, 1), F32),
                   jax.ShapeDtypeStruct((1, d), F32)],
        compiler_params=_params(("arbitrary",)),
    )(u, w_out, resid, target, gain)


def _silu_parts(g):
    sig = jax.nn.sigmoid(g)
    return g * sig, sig * (1.0 + g * (1.0 - sig))


def _lane_lo(rows):
    return lax.broadcasted_iota(jnp.int32, (rows, LANES), 1) < HEAD_DIM


def _stack_pair(x):
    lo = _lane_lo(x.shape[0])
    zero = jnp.zeros_like(x)
    return jnp.concatenate([jnp.where(lo, x, zero), jnp.where(lo, zero, x)], axis=0)


def _unstack_pair(y, w):
    return jnp.where(_lane_lo(w), y[:w], y[w:])


def _block_valid(b, left_blocks, width):
    col = lax.broadcasted_iota(jnp.int32, (1, 2 * width), 1)
    col = jnp.where(col >= width, col - width, col)
    return (col // KB + (b - left_blocks)) >= 0


def _toeplitz_tile(diag_row, width, left_chunks):
    wide = width + TQ
    rolled = pltpu.roll(jnp.broadcast_to(diag_row, (TQ, wide)), 1, 1, stride=1, stride_axis=0)
    i = lax.broadcasted_iota(jnp.int32, (TQ, width), 0) // CHUNK
    j = lax.broadcasted_iota(jnp.int32, (TQ, width), 1) // CHUNK
    dc = i + left_chunks - j
    return jnp.where((dc >= 0) & (dc <= left_chunks), rolled[:, TQ:], MASKED)


def _toeplitz_sum(tile, width):
    flip = (lax.broadcasted_iota(jnp.int32, (TQ, TQ), 0) + lax.broadcasted_iota(jnp.int32, (TQ, TQ), 1)
            == TQ - 1).astype(F32)
    reversed_rows = jnp.dot(flip, tile, precision=lax.Precision.HIGHEST, preferred_element_type=F32)
    padded = jnp.concatenate([reversed_rows, jnp.zeros((TQ, TQ), F32)], axis=1)
    rolled = pltpu.roll(padded, 0, 1, stride=1, stride_axis=0)
    return jnp.sum(rolled, axis=0, keepdims=True)


def _softmax_pair(sc, w, sink=None):
    ps, inv, lses = [], [], []
    for e in range(2):
        sh = sc[:, e * w:(e + 1) * w]
        m = jnp.max(sh, axis=-1, keepdims=True)
        if sink is not None:
            m = jnp.maximum(m, sink[e])
        ex = jnp.exp(sh - m)
        l = jnp.sum(ex, axis=-1, keepdims=True)
        if sink is not None:
            l = l + jnp.exp(sink[e] - m)
        ps.append(ex.astype(BF16))
        inv.append(1.0 / l)
        lses.append(m + jnp.log(l))
    return jnp.concatenate(ps, axis=-1), inv, lses


def _softmax_pair_bwd(sc, dp, lse, delta, w):
    ps, dss = [], []
    for e in range(2):
        p = jnp.exp(sc[:, e * w:(e + 1) * w] - lse[e])
        ps.append(p)
        dss.append(p * (dp[:, e * w:(e + 1) * w] - delta[e]))
    return jnp.concatenate(ps, axis=-1), jnp.concatenate(dss, axis=-1)


def _pair_rowsums(x, lo):
    zero = jnp.zeros_like(x)
    return (jnp.sum(jnp.where(lo, x, zero), axis=-1, keepdims=True),
            jnp.sum(jnp.where(lo, zero, x), axis=-1, keepdims=True))


def _a_qkv_specs(rows, pad, pw):
    return [pl.BlockSpec((None, TQ, pw), lambda p, b: (0, b + pad // TQ, p)),
            pl.BlockSpec((None, rows, pw), lambda p, b: (1, 0, p)),
            pl.BlockSpec((None, rows, pw), lambda p, b: (2, 0, p))]


def _window(ref, b, pad, win, lanes):
    start = pl.multiple_of(b * TQ + pad - (win - TQ), KB)
    return ref[pl.ds(start, win), lanes]


def _attn_a_fwd(zqkv, g, diag, hosted=None):
    s = g.shape[0]
    pad = zqkv.shape[1] - s
    nb = s // TQ
    left = A_KBLOCKS - 1
    pairs = A_PAIRS_FWD
    pw = pairs * LANES
    wide = A_WIN + TQ

    def body(q_ref, k_ref, v_ref, g_ref, diag_ref, o_ref, u_ref, lse_ref, bias_scr):
        b = pl.program_id(1)

        @pl.when(b == 0)
        def _():
            for hh in range(2 * pairs):
                bias_scr[hh // 2, :, (hh % 2) * A_WIN:(hh % 2 + 1) * A_WIN] = _toeplitz_tile(
                    diag_ref[hh], A_WIN, A_LEFT_CHUNKS)

        def step(first_blocks):
            lo = _lane_lo(TQ)
            for pp in range(pairs):
                ln = slice(pp * LANES, (pp + 1) * LANES)
                kcat = _stack_pair(_window(k_ref, b, pad, A_WIN, ln))
                vcat = _stack_pair(_window(v_ref, b, pad, A_WIN, ln))
                sc = lax.dot_general(q_ref[:, ln] * SCALE, kcat, NT, preferred_element_type=F32) + bias_scr[pp]
                if first_blocks:
                    sc = jnp.where(_block_valid(b, left, A_WIN), sc, MASKED)
                p, inv, lses = _softmax_pair(sc, A_WIN)
                ov = jnp.dot(p, vcat, preferred_element_type=F32) * jnp.where(lo, inv[0], inv[1])
                o_ref[:, ln] = ov
                lse_ref[pp] = jnp.where(lo, lses[0], lses[1])
                sg, _ = _silu_parts(g_ref[:, ln])
                u_ref[:, ln] = (ov * sg).astype(BF16)

        @pl.when(b < left)
        def _():
            step(True)

        @pl.when(b >= left)
        def _():
            step(False)

    tile = pl.BlockSpec((TQ, pw), lambda p, b: (b, p))
    return _call(
        body, name="attn_a_fwd", grid=(HEADS // 2 // pairs, nb),
        in_specs=_a_qkv_specs(pad + s, pad, pw) + [
            tile, pl.BlockSpec((2 * pairs, 1, wide), lambda p, b: (p, 0, 0))],
        out_specs=[tile, tile, pl.BlockSpec((pairs, TQ, LANES), lambda p, b: (p, b, 0))],
        out_shape=[jax.ShapeDtypeStruct((s, D_MODEL), F32), jax.ShapeDtypeStruct((s, D_MODEL), BF16),
                   jax.ShapeDtypeStruct((HEADS // 2, s, LANES), F32)],
        scratch_shapes=[pltpu.VMEM((pairs, TQ, 2 * A_WIN), F32)],
        sem=("parallel", "arbitrary"), hosted=hosted,
        args=(zqkv, zqkv, zqkv, g, diag))


def _attn_a_bwd(zqkv, g, o, du, lse, diag, hosted=None):
    s = g.shape[0]
    pad = zqkv.shape[1] - s
    nb = s // TQ
    left = A_KBLOCKS - 1
    pw = A_PAIRS * LANES
    wide = A_WIN + TQ

    def body(q_ref, k_ref, v_ref, g_ref, o_ref, du_ref, lse_ref, diag_ref, dz_ref, ddiag_ref,
             bias_scr, dbias_acc, dk_acc, dv_acc):
        b = pl.program_id(1)

        @pl.when(b == 0)
        def _():
            for hh in range(2 * A_PAIRS):
                bias_scr[hh // 2, :, (hh % 2) * A_WIN:(hh % 2 + 1) * A_WIN] = _toeplitz_tile(
                    diag_ref[hh], A_WIN, A_LEFT_CHUNKS)
            dbias_acc[...] = jnp.zeros_like(dbias_acc)
            dk_acc[...] = jnp.zeros_like(dk_acc)
            dv_acc[...] = jnp.zeros_like(dv_acc)

        def step(first_blocks):
            lo = _lane_lo(TQ)
            upper = lax.broadcasted_iota(jnp.int32, (LANES, A_WIN), 0) < HEAD_DIM
            rows = pl.ds(pl.multiple_of(b * TQ, TQ), TQ)
            sg, dsg = _silu_parts(g_ref[...])
            duv = du_ref[...]
            ov = o_ref[...]
            do = duv * sg
            dz_ref[3, rows, :] = (duv * ov * dsg).astype(BF16)
            do_o = do * ov
            do_bf = do.astype(BF16)
            for pp in range(A_PAIRS):
                ln = slice(pp * LANES, (pp + 1) * LANES)
                q = q_ref[:, ln] * SCALE
                kcat = _stack_pair(_window(k_ref, b, pad, A_WIN, ln))
                vcat = _stack_pair(_window(v_ref, b, pad, A_WIN, ln))
                sc = lax.dot_general(q, kcat, NT, preferred_element_type=F32) + bias_scr[pp]
                if first_blocks:
                    sc = jnp.where(_block_valid(b, left, A_WIN), sc, MASKED)
                lse_t = lse_ref[pp]
                dp = lax.dot_general(do_bf[:, ln], vcat, NT, preferred_element_type=F32)
                p, ds = _softmax_pair_bwd(sc, dp, (lse_t[:, 0:1], lse_t[:, HEAD_DIM:HEAD_DIM + 1]),
                                          _pair_rowsums(do_o[:, ln], lo), A_WIN)
                dbias_acc[pp] += ds
                dsb = ds.astype(BF16)
                dz_ref[0, rows, ln] = (jnp.dot(dsb, kcat, preferred_element_type=F32) * SCALE).astype(BF16)
                dkt = lax.dot_general(q, dsb, TN, preferred_element_type=F32)
                dvt = lax.dot_general(do_bf[:, ln], p.astype(BF16), TN, preferred_element_type=F32)
                dkt = jnp.where(upper, dkt[:, :A_WIN], dkt[:, A_WIN:])
                dvt = jnp.where(upper, dvt[:, :A_WIN], dvt[:, A_WIN:])
                for t in range(A_KBLOCKS):
                    blk = b + (pad // KB - left + t)
                    dk_acc[blk, ln, :] += dkt[:, t * KB:(t + 1) * KB]
                    dv_acc[blk, ln, :] += dvt[:, t * KB:(t + 1) * KB]

        @pl.when(b < left)
        def _():
            step(True)

        @pl.when(b >= left)
        def _():
            step(False)

        @pl.when(b == nb - 1)
        def _():
            for kb in range(s // KB):
                dz_ref[1, kb * KB:(kb + 1) * KB, :] = dk_acc[pad // KB + kb].T.astype(BF16)
                dz_ref[2, kb * KB:(kb + 1) * KB, :] = dv_acc[pad // KB + kb].T.astype(BF16)
            for hh in range(2 * A_PAIRS):
                ddiag_ref[hh] = _toeplitz_sum(
                    dbias_acc[hh // 2, :, (hh % 2) * A_WIN:(hh % 2 + 1) * A_WIN], A_WIN)

    tile = pl.BlockSpec((TQ, pw), lambda p, b: (b, p))
    diag_spec = pl.BlockSpec((2 * A_PAIRS, 1, wide), lambda p, b: (p, 0, 0))
    return _call(
        body, name="attn_a_bwd", grid=(HEADS // 2 // A_PAIRS, nb),
        in_specs=_a_qkv_specs(pad + s, pad, pw) + [
            tile, tile, tile, pl.BlockSpec((A_PAIRS, TQ, LANES), lambda p, b: (p, b, 0)), diag_spec],
        out_specs=[pl.BlockSpec((4, s, pw), lambda p, b: (0, 0, p)), diag_spec],
        out_shape=[jax.ShapeDtypeStruct((4, s, D_MODEL), BF16),
                   jax.ShapeDtypeStruct((HEADS, 1, wide), F32)],
        scratch_shapes=[pltpu.VMEM((A_PAIRS, TQ, 2 * A_WIN), F32), pltpu.VMEM((A_PAIRS, TQ, 2 * A_WIN), F32),
                        pltpu.VMEM(((pad + s) // KB, pw, KB), F32), pltpu.VMEM(((pad + s) // KB, pw, KB), F32)],
        sem=("parallel", "arbitrary"), hosted=hosted,
        args=(zqkv, zqkv, zqkv, g, o, du, lse, diag))


B_STACK = B_GROUP // 2
B_KVX = 4 * LANES
B_ROWS = B_STACK * TQ
B_WIDE = B_WIN + TQ


def _b_head_place(h):
    return h // B_GROUP, (h % B_GROUP) // 2, h % 2


def _toeplitz_tile_t(base_row, width, left_chunks):
    wide = width + TQ
    rolled = pltpu.roll(jnp.broadcast_to(base_row, (width, wide)), 0, 1, stride=1, stride_axis=0)
    j = lax.broadcasted_iota(jnp.int32, (width, TQ), 0) // CHUNK
    i = lax.broadcasted_iota(jnp.int32, (width, TQ), 1) // CHUNK
    dc = i + left_chunks - j
    return jnp.where((dc >= 0) & (dc <= left_chunks), rolled[:, :TQ], MASKED)


def _toeplitz_sum_t(tile_t, width):
    flip = (lax.broadcasted_iota(jnp.int32, (width, width), 0) + lax.broadcasted_iota(jnp.int32, (width, width), 1)
            == width - 1).astype(F32)
    reversed_rows = jnp.dot(flip, tile_t, precision=lax.Precision.HIGHEST, preferred_element_type=F32)
    padded = jnp.concatenate([reversed_rows, jnp.zeros((width, width), F32)], axis=1)
    rolled = pltpu.roll(padded, 0, 1, stride=1, stride_axis=0)
    return jnp.sum(rolled, axis=0, keepdims=True)


def _b_build_bias(base_ref, bias_scr):
    for h in range(HEADS):
        gi, pr, e = _b_head_place(h)
        bias_scr[gi, e * B_WIN:(e + 1) * B_WIN, pr * TQ:(pr + 1) * TQ] = _toeplitz_tile_t(
            base_ref[h], B_WIN, B_LEFT_CHUNKS)


def _b_stack(x, gi):
    return jnp.concatenate(
        [x[:, (B_STACK * gi + pr) * LANES:(B_STACK * gi + pr + 1) * LANES] for pr in range(B_STACK)], axis=0)


def _b_sink_rows(sink_ref, gi):
    block = lax.broadcasted_iota(jnp.int32, (1, B_ROWS), 1) // TQ
    rows = []
    for e in range(2):
        row = jnp.zeros((1, B_ROWS), F32)
        for pr in range(B_STACK):
            h = B_GROUP * gi + 2 * pr + e
            row = jnp.where(block == pr, sink_ref[0:1, h:h + 1], row)
        rows.append(row)
    return rows


def _b_scores_t(q_ref, kvv, bias_scr, gi, b, left, first_blocks):
    kcat = _stack_pair(kvv[:, gi * LANES:(gi + 1) * LANES])
    vcat = _stack_pair(kvv[:, (B_KV_HEADS + gi) * LANES:(B_KV_HEADS + gi + 1) * LANES])
    qs = _b_stack(q_ref, gi) * SCALE
    sc = lax.dot_general(kcat, qs, NT, preferred_element_type=F32) + bias_scr[gi]
    if first_blocks:
        row = lax.broadcasted_iota(jnp.int32, (2 * B_WIN, 1), 0)
        row = jnp.where(row >= B_WIN, row - B_WIN, row)
        sc = jnp.where((row // KB + (b - left)) >= 0, sc, MASKED)
    return kcat, vcat, qs, sc


def _attn_b_fwd(qb, kvx, gate, base, sinks):
    s = qb.shape[0]
    pad = kvx.shape[0] - s
    nb = s // TQ
    left = B_KBLOCKS - 1

    def body(q_ref, kv_ref, g_ref, base_ref, sink_ref, o_ref, u_ref, lse_ref, bias_scr):
        b = pl.program_id(0)

        @pl.when(b == 0)
        def _():
            _b_build_bias(base_ref, bias_scr)

        def step(first_blocks):
            kvv = _window(kv_ref, b, pad, B_WIN, slice(None))
            upper = lax.broadcasted_iota(jnp.int32, (LANES, B_ROWS), 0) < HEAD_DIM
            lse_rows = []
            for gi in range(B_KV_HEADS):
                kcat, vcat, qs, sc = _b_scores_t(q_ref, kvv, bias_scr, gi, b, left, first_blocks)
                sink = _b_sink_rows(sink_ref, gi)
                ps, inv = [], []
                for e in range(2):
                    sh = sc[e * B_WIN:(e + 1) * B_WIN]
                    m = jnp.maximum(jnp.max(sh, axis=0, keepdims=True), sink[e])
                    ex = jnp.exp(sh - m)
                    l = jnp.sum(ex, axis=0, keepdims=True) + jnp.exp(sink[e] - m)
                    ps.append(ex.astype(BF16))
                    inv.append(1.0 / l)
                    lse_rows.append(m + jnp.log(l))
                pt = jnp.concatenate(ps, axis=0)
                ot = lax.dot_general(vcat, pt, TN, preferred_element_type=F32) * jnp.where(upper, inv[0], inv[1])
                ov = ot.T
                for pr in range(B_STACK):
                    pair = B_STACK * gi + pr
                    o_ref[:, pair * LANES:(pair + 1) * LANES] = ov[pr * TQ:(pr + 1) * TQ]
            lse_ref[0] = jnp.concatenate(lse_rows + [jnp.zeros((8 - len(lse_rows), B_ROWS), F32)], axis=0)
            sg, _ = _silu_parts(g_ref[...])
            u_ref[...] = (o_ref[...] * sg).astype(BF16)

        @pl.when(b < left)
        def _():
            step(True)

        @pl.when(b >= left)
        def _():
            step(False)

    row = pl.BlockSpec((TQ, D_MODEL), lambda b: (b, 0))
    return pl.pallas_call(
        body, name="attn_b_fwd", grid=(nb,),
        in_specs=[row, pl.BlockSpec((pad + s, B_KVX), lambda b: (0, 0)), row,
                  pl.BlockSpec((HEADS, 1, B_WIDE), lambda b: (0, 0, 0)), pl.BlockSpec((1, HEADS), lambda b: (0, 0))],
        out_specs=[row, row, pl.BlockSpec((1, 8, B_ROWS), lambda b: (b, 0, 0))],
        out_shape=[jax.ShapeDtypeStruct((s, D_MODEL), F32), jax.ShapeDtypeStruct((s, D_MODEL), BF16),
                   jax.ShapeDtypeStruct((nb, 8, B_ROWS), F32)],
        scratch_shapes=[pltpu.VMEM((B_KV_HEADS, 2 * B_WIN, B_ROWS), F32)],
        compiler_params=_params(("arbitrary",)),
    )(qb, kvx, gate, base, sinks)


def _attn_b_bwd(qb, kvx, gate, o, du, lse, base, sinks):
    s = qb.shape[0]
    pad = kvx.shape[0] - s
    nb = s // TQ
    left = B_KBLOCKS - 1
    half = D_MODEL // 2

    def body(q_ref, kv_ref, g_ref, o_ref, du_ref, lse_ref, base_ref, sink_ref, dz_ref, dkv_ref, dsum_ref,
             dsink_ref, bias_scr, dbias_acc, dkv_acc, dsink_acc):
        b = pl.program_id(0)

        @pl.when(b == 0)
        def _():
            _b_build_bias(base_ref, bias_scr)
            dbias_acc[...] = jnp.zeros_like(dbias_acc)
            dkv_acc[...] = jnp.zeros_like(dkv_acc)
            dsink_acc[...] = jnp.zeros_like(dsink_acc)

        def step(first_blocks):
            kvv = _window(kv_ref, b, pad, B_WIN, slice(None))
            sg, dsg = _silu_parts(g_ref[...])
            duv = du_ref[...]
            ov = o_ref[...]
            do = duv * sg
            dgate = (duv * ov * dsg).astype(BF16)
            dz_ref[2] = dgate[:, :half]
            dz_ref[3] = dgate[:, half:]
            do_o = do * ov
            do_bf = do.astype(BF16)
            lse_all = lse_ref[0]
            dsink_rows = []
            for gi in range(B_KV_HEADS):
                kcat, vcat, qs, sc = _b_scores_t(q_ref, kvv, bias_scr, gi, b, left, first_blocks)
                dos = _b_stack(do_bf, gi)
                doo_t = _b_stack(do_o, gi).T
                delta = (jnp.sum(doo_t[:HEAD_DIM], axis=0, keepdims=True),
                         jnp.sum(doo_t[HEAD_DIM:], axis=0, keepdims=True))
                sink = _b_sink_rows(sink_ref, gi)
                dp = lax.dot_general(vcat, dos, NT, preferred_element_type=F32)
                ps, dss = [], []
                for e in range(2):
                    lse_e = lse_all[2 * gi + e:2 * gi + e + 1]
                    delta_e = delta[e]
                    p = jnp.exp(sc[e * B_WIN:(e + 1) * B_WIN] - lse_e)
                    ps.append(p.astype(BF16))
                    dss.append(p * (dp[e * B_WIN:(e + 1) * B_WIN] - delta_e))
                    dsink_rows.append(-jnp.exp(sink[e] - lse_e) * delta_e)
                ds = jnp.concatenate(dss, axis=0)
                dbias_acc[gi] += ds
                dsb = ds.astype(BF16)
                dq = (lax.dot_general(kcat, dsb, TN, preferred_element_type=F32) * SCALE).T.astype(BF16)
                for pr in range(B_STACK):
                    dz_ref[gi, :, pr * LANES:(pr + 1) * LANES] = dq[pr * TQ:(pr + 1) * TQ]
                dk = _unstack_pair(jnp.dot(dsb, qs, preferred_element_type=F32), B_WIN)
                dv = _unstack_pair(jnp.dot(jnp.concatenate(ps, axis=0), dos, preferred_element_type=F32), B_WIN)
                krows = pl.ds(pl.multiple_of(b * TQ + pad - (B_WIN - TQ), KB), B_WIN)
                dkv_acc[krows, gi * LANES:(gi + 1) * LANES] += dk
                dkv_acc[krows, (B_KV_HEADS + gi) * LANES:(B_KV_HEADS + gi + 1) * LANES] += dv
            dsink_acc[...] += jnp.concatenate(
                dsink_rows + [jnp.zeros((8 - len(dsink_rows), B_ROWS), F32)], axis=0)

        @pl.when(b < left)
        def _():
            step(True)

        @pl.when(b >= left)
        def _():
            step(False)

        @pl.when(b == nb - 1)
        def _():
            lo_s = _lane_lo(s)
            for which in range(2):
                folded = []
                for gi in range(B_KV_HEADS):
                    part = dkv_acc[pad:pad + s, (which * B_KV_HEADS + gi) * LANES:(which * B_KV_HEADS + gi + 1) * LANES]
                    folded.append(part + pltpu.roll(part, HEAD_DIM, 1))
                dkv_ref[:, which * LANES:(which + 1) * LANES] = jnp.where(lo_s, folded[0], folded[1]).astype(BF16)
            lane8 = lax.broadcasted_iota(jnp.int32, dsink_ref.shape, 1)
            tot = jnp.zeros(dsink_ref.shape, F32)
            for h in range(HEADS):
                gi, pr, e = _b_head_place(h)
                dsum_ref[h] = _toeplitz_sum_t(
                    dbias_acc[gi, e * B_WIN:(e + 1) * B_WIN, pr * TQ:(pr + 1) * TQ], B_WIN)
                per_query = dsink_acc[2 * gi + e:2 * gi + e + 1, pr * TQ:(pr + 1) * TQ]
                tot = jnp.where(lane8 == h, jnp.sum(per_query, axis=1, keepdims=True), tot)
            dsink_ref[...] = tot

    row = pl.BlockSpec((TQ, D_MODEL), lambda b: (b, 0))
    base_spec = pl.BlockSpec((HEADS, 1, B_WIDE), lambda b: (0, 0, 0))
    return pl.pallas_call(
        body, name="attn_b_bwd", grid=(nb,),
        in_specs=[row, pl.BlockSpec((pad + s, B_KVX), lambda b: (0, 0)), row, row, row,
                  pl.BlockSpec((1, 8, B_ROWS), lambda b: (b, 0, 0)), base_spec,
                  pl.BlockSpec((1, HEADS), lambda b: (0, 0))],
        out_specs=[pl.BlockSpec((4, TQ, half), lambda b: (0, b, 0)),
                   pl.BlockSpec((s, 2 * LANES), lambda b: (0, 0)), base_spec,
                   pl.BlockSpec((8, LANES), lambda b: (0, 0))],
        out_shape=[jax.ShapeDtypeStruct((4, s, half), BF16), jax.ShapeDtypeStruct((s, 2 * LANES), BF16),
                   jax.ShapeDtypeStruct((HEADS, 1, B_WIDE), F32), jax.ShapeDtypeStruct((8, LANES), F32)],
        scratch_shapes=[pltpu.VMEM((B_KV_HEADS, 2 * B_WIN, B_ROWS), F32),
                        pltpu.VMEM((B_KV_HEADS, 2 * B_WIN, B_ROWS), F32),
                        pltpu.VMEM((pad + s, B_KVX), F32), pltpu.VMEM((8, B_ROWS), F32)],
        compiler_params=_params(("arbitrary",)),
    )(qb, kvx, gate, o, du, lse, base, sinks)


def _t5_bucket(rel):
    nb = T5_BUCKETS // 2
    max_exact = nb // 2
    ret = jnp.where(rel > 0, nb, 0)
    n = jnp.abs(rel)
    nf = jnp.maximum(n, 1).astype(jnp.float32)
    large = max_exact + (jnp.log(nf / max_exact) / math.log(T5_MAX_DIST / max_exact)
                         * (nb - max_exact)).astype(jnp.int32)
    large = jnp.minimum(large, nb - 1)
    return ret + jnp.where(n < max_exact, n, large)


def _a_offset_onehot():
    c = np.arange(A_WIN + TQ)
    dist = A_LEFT_CHUNKS * CHUNK + TQ - 1 - c
    idx = np.clip(dist, -A_REL_CLIP, A_REL_CLIP) + A_REL_CLIP
    onehot = np.zeros((A_WIN + TQ, 2 * A_REL_CLIP + 1), np.float32)
    onehot[c, idx] = 1.0
    return jnp.asarray(onehot)


def _b_offset_onehot():
    c = jnp.arange(B_WIN + TQ, dtype=jnp.int32)
    rel = c - (TQ - 1) - B_LEFT_CHUNKS * CHUNK
    return (_t5_bucket(rel)[:, None] == jnp.arange(T5_BUCKETS)[None, :]).astype(F32)


def _diag_rows(onehot, table):
    rows = jnp.dot(onehot, table.astype(F32), precision=lax.Precision.HIGHEST)
    return rows.T.reshape(HEADS, 1, onehot.shape[0])


def _diag_rows_grad(onehot, ddiag):
    return jnp.dot(ddiag.reshape(HEADS, onehot.shape[0]), onehot, precision=lax.Precision.HIGHEST).T


def _position():
    x, y, c = lax.axis_index("x"), lax.axis_index("y"), lax.axis_index("c")
    chips = [(1 - x, y), (x, 1 - y), (1 - x, 1 - y)]
    return x, y, c, chips


ANY = pl.BlockSpec(memory_space=pl.ANY)


def _allgather_hosted(shards, split):
    n = len(shards)

    def part(ref, t, half):
        if not split[t]:
            return ref
        rows = shards[t].shape[0] // 2
        return ref.at[pl.ds(half * rows, rows)]

    def copies(kind, ins, outs, sems):
        send_sems, recv_sems, pass_send, pass_recv, local_sems = sems
        x, y, c, chips = _position()
        mine = 2 * x + y
        if kind == "local":
            return [pltpu.make_async_copy(ins[t], outs[t].at[mine], local_sems.at[t]) for t in range(n)]
        made = []
        for t in range(n):
            for j, chip in enumerate(chips):
                theirs = 2 * chip[0] + chip[1]
                far = dict(send_sem=send_sems.at[3 * t + j], recv_sem=recv_sems.at[3 * t + j],
                           device_id=(chip[0], chip[1], c), device_id_type=MESH)
                near = dict(send_sem=pass_send.at[3 * t + j], recv_sem=pass_recv.at[3 * t + j],
                            device_id=(x, y, 1 - c), device_id_type=MESH)
                here = part(outs[t].at[theirs], t, c)
                if kind == "send":
                    made.append(pltpu.make_async_remote_copy(
                        src_ref=part(ins[t], t, c), dst_ref=part(outs[t].at[mine], t, c), **far))
                elif kind == "landed":
                    made.append(pltpu.make_async_remote_copy(src_ref=here, dst_ref=here, **far))
                elif not split[t]:
                    made.append(None)
                elif kind == "pass":
                    made.append(pltpu.make_async_remote_copy(src_ref=here, dst_ref=here, **near))
                else:
                    other = part(outs[t].at[theirs], t, 1 - c)
                    made.append(pltpu.make_async_remote_copy(src_ref=other, dst_ref=other, **near))
        return made

    def first(ins, outs, sems):
        for cp in copies("local", ins, outs, sems) + copies("send", ins, outs, sems):
            cp.start()

    def middle(ins, outs, sems):
        for got, cp in zip(copies("landed", ins, outs, sems), copies("pass", ins, outs, sems)):
            got.wait_recv()
            if cp is not None:
                cp.start()

    def last(ins, outs, sems):
        for cp in copies("passed", ins, outs, sems):
            if cp is not None:
                cp.wait_recv()
        for cp in copies("send", ins, outs, sems) + copies("pass", ins, outs, sems):
            if cp is not None:
                cp.wait_send()
        for cp in copies("local", ins, outs, sems):
            cp.wait()

    return _Hosted(shards, [jax.ShapeDtypeStruct((4,) + w.shape, w.dtype) for w in shards],
                   [pltpu.SemaphoreType.DMA((3 * n,))] * 4 + [pltpu.SemaphoreType.DMA((n,))],
                   first, middle, last)


def _allgather_routed(shards):
    n = len(shards)

    def piece(block_ref, t, c, quarter=None):
        half = shards[t].shape[0] // 2
        if quarter is None:
            return block_ref.at[pl.ds(c * half, half)]
        return block_ref.at[pl.ds(c * half + quarter * (half // 2), half // 2)]

    def copies(kind, ins, outs, sems):
        ici_send, ici_recv, pass_send, pass_recv, local_sems = sems
        x, y, c, chips = _position()
        mine = 2 * x + y
        if kind == "local":
            return [pltpu.make_async_copy(ins[t], outs[t].at[mine], local_sems.at[t]) for t in range(n)]
        ids = [2 * chip[0] + chip[1] for chip in chips]
        made = []
        for t in range(n):
            def ici(k, to):
                return dict(send_sem=ici_send.at[4 * t + k], recv_sem=ici_recv.at[4 * t + k],
                            device_id=(chips[to][0], chips[to][1], c), device_id_type=MESH)

            def d2d(k):
                return dict(send_sem=pass_send.at[4 * t + k], recv_sem=pass_recv.at[4 * t + k],
                            device_id=(x, y, 1 - c), device_id_type=MESH)

            def same(ref, where):
                return pltpu.make_async_remote_copy(src_ref=ref, dst_ref=ref, **where)

            if kind == "send":
                for k in range(2):
                    made.append(pltpu.make_async_remote_copy(
                        src_ref=piece(ins[t], t, c), dst_ref=piece(outs[t].at[mine], t, c), **ici(k, k)))
            elif kind == "landed":
                made += [same(piece(outs[t].at[ids[k]], t, c), ici(k, k)) for k in range(2)]
            elif kind == "forward":
                made.append(same(piece(outs[t].at[ids[0]], t, c, 0), ici(2, 1)))
                made.append(same(piece(outs[t].at[ids[1]], t, c, 1), ici(3, 0)))
            elif kind == "arrived":
                made.append(same(piece(outs[t].at[ids[2]], t, c, 0), ici(2, 1)))
                made.append(same(piece(outs[t].at[ids[2]], t, c, 1), ici(3, 0)))
            else:
                core = 1 - c if kind == "passed" else c
                if kind in ("pass halves", "passed"):
                    made += [same(piece(outs[t].at[ids[k]], t, core), d2d(k)) for k in range(2)]
                if kind in ("pass quarters", "passed"):
                    made += [same(piece(outs[t].at[ids[2]], t, core, k), d2d(2 + k)) for k in range(2)]
        return made

    def first(ins, outs, sems):
        for cp in copies("local", ins, outs, sems) + copies("send", ins, outs, sems):
            cp.start()

    def middle(ins, outs, sems):
        for got, onward, near in zip(copies("landed", ins, outs, sems), copies("forward", ins, outs, sems),
                                     copies("pass halves", ins, outs, sems)):
            got.wait_recv()
            near.start()
            onward.start()

    def last(ins, outs, sems):
        quarters = copies("pass quarters", ins, outs, sems)
        for got, near in zip(copies("arrived", ins, outs, sems), quarters):
            got.wait_recv()
            near.start()
        for cp in copies("passed", ins, outs, sems):
            cp.wait_recv()
        for cp in (copies("send", ins, outs, sems) + copies("forward", ins, outs, sems)
                   + copies("pass halves", ins, outs, sems) + quarters):
            cp.wait_send()
        for cp in copies("local", ins, outs, sems):
            cp.wait()

    return _Hosted(shards, [jax.ShapeDtypeStruct((4,) + w.shape, w.dtype) for w in shards],
                   [pltpu.SemaphoreType.DMA((4 * n,))] * 4 + [pltpu.SemaphoreType.DMA((n,))],
                   first, middle, last)


def _scatter_hosted(grads):
    n = len(grads)

    def copies(ins, outs, sems):
        send_sems, recv_sems = sems
        x, y, c, chips = _position()
        return [pltpu.make_async_remote_copy(
            src_ref=ins[t].at[2 * chip[0] + chip[1]], dst_ref=outs[t].at[j],
            send_sem=send_sems.at[3 * t + j], recv_sem=recv_sems.at[3 * t + j],
            device_id=(chip[0], chip[1], c), device_id_type=MESH)
            for t in range(n) for j, chip in enumerate(chips)]

    def first(ins, outs, sems):
        for cp in copies(ins, outs, sems):
            cp.start()

    def last(ins, outs, sems):
        for cp in copies(ins, outs, sems):
            cp.wait()

    return _Hosted(grads, [jax.ShapeDtypeStruct((3,) + g.shape[1:], g.dtype) for g in grads],
                   [pltpu.SemaphoreType.DMA((3 * n,))] * 2, first, None, last)


GATHER_PEERS = "x and y neighbours (same core) and the sibling core"
SCATTER_PEERS = "the same core of the three other chips"
EVERYONE = "the seven other devices"


def _run_on_sequencer(name, hosted, peers, collective_id):
    ins = [jax.new_ref(a, memory_space=pltpu.MemorySpace.HBM) for a in hosted.inputs]
    outs = [jax.empty_ref(shape, memory_space=pltpu.MemorySpace.HBM) for shape in hosted.out_shapes]

    @pl.kernel(mesh=plsc.ScalarSubcoreMesh(axis_name="sequencer", num_cores=1), name=name,
               scratch_types=tuple(hosted.sems), compiler_params=pltpu.CompilerParams(collective_id=collective_id))
    def launch(*sems):
        x, y, c, chips = _position()
        if peers == GATHER_PEERS:
            devices = [(chip[0], chip[1], c) for chip in chips[:2]] + [(x, y, 1 - c)]
        elif peers == SCATTER_PEERS:
            devices = [(chip[0], chip[1], c) for chip in chips]
        else:
            devices = [(x ^ (k >> 2), y ^ ((k >> 1) & 1), c ^ (k & 1)) for k in range(1, 8)]
        barrier = pltpu.get_barrier_semaphore()
        for device in devices:
            pl.semaphore_signal(barrier, inc=1, device_id=device, device_id_type=MESH)
        pl.semaphore_wait(barrier, len(devices))
        hosted.first(ins, outs, sems)
        if hosted.middle is not None:
            hosted.middle(ins, outs, sems)
        hosted.last(ins, outs, sems)

    launch()
    return [o[...] for o in outs]


def _run_alone(name, hosted):
    n_in = len(hosted.inputs)
    n_out = len(hosted.out_shapes)

    def body(*refs):
        ins, outs, sems = refs[:n_in], refs[n_in:n_in + n_out], refs[n_in + n_out:]
        hosted.first(ins, outs, sems)
        if hosted.middle is not None:
            hosted.middle(ins, outs, sems)
        hosted.last(ins, outs, sems)

    return pl.pallas_call(
        body, name=name, in_specs=[ANY] * n_in, out_specs=[ANY] * n_out, out_shape=hosted.out_shapes,
        scratch_shapes=hosted.sems)(*hosted.inputs)


def _gather_gain(shard):
    def body(in_ref, out_ref, send_sems, recv_sems):
        x, y, c, chips = _position()
        out_ref[2 * x + y] = in_ref[...]
        sends = [pltpu.make_async_remote_copy(
            src_ref=in_ref, dst_ref=out_ref.at[2 * x + y], send_sem=send_sems.at[j], recv_sem=recv_sems.at[j],
            device_id=(chip[0], chip[1], c), device_id_type=MESH) for j, chip in enumerate(chips)]
        for cp in sends:
            cp.start()
        for j, chip in enumerate(chips):
            pltpu.make_async_remote_copy(
                src_ref=in_ref, dst_ref=out_ref.at[2 * chip[0] + chip[1]], send_sem=send_sems.at[j],
                recv_sem=recv_sems.at[j], device_id=(chip[0], chip[1], c), device_id_type=MESH).wait_recv()
        for cp in sends:
            cp.wait_send()

    vmem = pl.BlockSpec(memory_space=pltpu.VMEM)
    return pl.pallas_call(
        body, name="gather_gain", in_specs=[vmem], out_specs=vmem,
        out_shape=jax.ShapeDtypeStruct((4,) + shard.shape, shard.dtype),
        scratch_shapes=[pltpu.SemaphoreType.DMA((3,))] * 2,
    )(shard)


def _swap_with_sibling(name, blocks):
    n = len(blocks)

    def body(*refs):
        ins, outs = refs[:n], refs[n:2 * n]
        send_sems, recv_sems = refs[2 * n:]
        x, y, c, _ = _position()
        sends = [pltpu.make_async_remote_copy(
            src_ref=ins[t], dst_ref=outs[t], send_sem=send_sems.at[t], recv_sem=recv_sems.at[t],
            device_id=(x, y, 1 - c), device_id_type=MESH) for t in range(n)]
        for cp in sends:
            cp.start()
        for cp in sends:
            cp.wait()

    return pl.pallas_call(
        body, name=name,
        in_specs=[ANY] * n, out_specs=[ANY] * n,
        out_shape=[jax.ShapeDtypeStruct(b.shape, b.dtype) for b in blocks],
        scratch_shapes=[pltpu.SemaphoreType.DMA((n,))] * 2,
    )(*blocks)


def _everyone_hosted(terms):
    nt = len(terms)

    def copies(kind, ins, outs, sems):
        send_sems, recv_sems, local_sems = sems
        x, y, c, _ = _position()
        me = 4 * x + 2 * y + c
        if kind == "local":
            return [pltpu.make_async_copy(ins[t], outs[t].at[me], local_sems.at[t]) for t in range(nt)]
        made = []
        for t in range(nt):
            for k in range(1, 8):
                peer = (x ^ (k >> 2), y ^ ((k >> 1) & 1), c ^ (k & 1))
                slot = me if kind == "send" else me ^ k
                made.append(pltpu.make_async_remote_copy(
                    src_ref=ins[t], dst_ref=outs[t].at[slot], send_sem=send_sems.at[7 * t + k - 1],
                    recv_sem=recv_sems.at[7 * t + k - 1], device_id=peer, device_id_type=MESH))
        return made

    def first(ins, outs, sems):
        for cp in copies("local", ins, outs, sems) + copies("send", ins, outs, sems):
            cp.start()

    def last(ins, outs, sems):
        for cp in copies("landed", ins, outs, sems):
            cp.wait_recv()
        for cp in copies("send", ins, outs, sems):
            cp.wait_send()
        for cp in copies("local", ins, outs, sems):
            cp.wait()

    return _Hosted(terms, [jax.ShapeDtypeStruct((8,) + a.shape, F32) for a in terms],
                   [pltpu.SemaphoreType.DMA((7 * nt,))] * 2 + [pltpu.SemaphoreType.DMA((nt,))], first, None, last)


def _small_step(partials, extras, ws, ms, vs, shard_of):
    n = len(partials)
    terms = list(partials) + list(extras)
    nt = len(terms)
    rows = [t for t in range(nt) if terms[t].shape[0] == 1]
    mats = [t for t in range(nt) if terms[t].shape[0] != 1]
    row_block = (8, max(terms[t].shape[1] for t in rows))
    assert len(rows) <= row_block[0]
    vmem = pl.BlockSpec(memory_space=pltpu.VMEM)

    def pack(*refs):
        packed = refs[-1]
        packed[...] = jnp.zeros_like(packed)
        for i, t in enumerate(rows):
            packed[i:i + 1, 0:terms[t].shape[1]] = refs[i][...]

    packed = pl.pallas_call(pack, name="small_pack", in_specs=[vmem] * len(rows), out_specs=vmem,
                            out_shape=jax.ShapeDtypeStruct(row_block, F32))(*[terms[t] for t in rows])
    slots = _run_on_sequencer("allgather_small", _everyone_hosted([packed] + [terms[t] for t in mats]),
                              EVERYONE, 2)

    def body(*refs):
        slot_refs, refs = refs[:len(slots)], refs[len(slots):]
        w_refs, refs = refs[:n], refs[n:]
        m_refs, refs = refs[:n], refs[n:]
        v_refs, outs = refs[:n], refs[n:]
        sums = []
        for ref in slot_refs:
            g = ref[0]
            for dev in range(1, 8):
                g = g + ref[dev]
            sums.append(g)
        chip = 2 * lax.axis_index("x") + lax.axis_index("y")
        for t in range(nt):
            if t in rows:
                i = rows.index(t)
                g = sums[0][i:i + 1, 0:terms[t].shape[1]]
            else:
                g = sums[1 + mats.index(t)]
            if t >= n:
                outs[4 * n + t - n][...] = g
                continue
            if shard_of[t]:
                width = ws[t].shape[-1]
                mine = jnp.zeros(ws[t].shape, F32)
                for s in range(4):
                    mine = jnp.where(chip == s, g[:, s * width:(s + 1) * width], mine)
                g = mine
            delta, mn, vn = _adamw_math(w_refs[t][...], g, m_refs[t][...], v_refs[t][...])
            outs[4 * t][...] = g
            outs[4 * t + 1][...] = delta
            outs[4 * t + 2][...] = mn
            outs[4 * t + 3][...] = vn

    out_shapes = []
    for t in range(n):
        out_shapes += [jax.ShapeDtypeStruct(ws[t].shape, F32)] * 4
    out_shapes += [jax.ShapeDtypeStruct(a.shape, F32) for a in extras]
    res = pl.pallas_call(
        body, name="small_step",
        in_specs=[vmem] * (len(slots) + 3 * n), out_specs=[vmem] * len(out_shapes), out_shape=out_shapes,
    )(*slots, *ws, *ms, *vs)
    return [res[4 * t:4 * t + 4] for t in range(n)], res[4 * n:4 * n + nt - n]


def _adamw_math(w, g, m, v):
    m = ADAM_B1 * m + (1.0 - ADAM_B1) * g
    v = ADAM_B2 * v + (1.0 - ADAM_B2) * (g * g)
    m_hat = m / (1.0 - ADAM_B1 ** ADAM_STEP)
    v_hat = v / (1.0 - ADAM_B2 ** ADAM_STEP)
    delta = -ADAM_LR * (m_hat / (jnp.sqrt(v_hat) + ADAM_EPS) + ADAM_WD * w)
    return delta, m, v


def _row_tile(rows):
    return 256 if rows % 256 == 0 else rows


def _sum_partials(name, own, recv, chip, after):
    rows, cols = own.shape[1:]
    tr = _row_tile(rows)

    def body(chip_ref, own_ref, recv_ref, after_ref, o_ref):
        acc = own_ref[...]
        for j in range(3):
            acc = acc + recv_ref[j].astype(F32)
        o_ref[...] = acc

    return pl.pallas_call(
        body, name=name,
        grid_spec=pltpu.PrefetchScalarGridSpec(
            num_scalar_prefetch=1, grid=(rows // tr,),
            in_specs=[pl.BlockSpec((None, tr, cols), lambda i, chip_ref: (chip_ref[0], i, 0)),
                      pl.BlockSpec((3, tr, cols), lambda i, chip_ref: (0, i, 0)), ANY],
            out_specs=pl.BlockSpec((tr, cols), lambda i, chip_ref: (i, 0))),
        out_shape=jax.ShapeDtypeStruct((rows, cols), F32),
        compiler_params=_params(("parallel",)),
    )(chip.reshape(1).astype(jnp.int32), own, recv, after)


def _adamw(name, w, m, v, g_parts):
    rows, cols = w.shape
    tr = _row_tile(rows)
    n = len(g_parts)

    def body(w_ref, m_ref, v_ref, *refs):
        g_refs = refs[:n]
        go_ref, d_ref, mo_ref, vo_ref = refs[n:]
        g = g_refs[0][...]
        for r in g_refs[1:]:
            g = g + r[...]
        delta, mn, vn = _adamw_math(w_ref[...], g, m_ref[...], v_ref[...])
        go_ref[...] = g
        d_ref[...] = delta
        mo_ref[...] = mn
        vo_ref[...] = vn

    spec = pl.BlockSpec((tr, cols), lambda i: (i, 0))
    return pl.pallas_call(
        body, name=name, grid=(rows // tr,),
        in_specs=[spec] * (3 + n), out_specs=[spec] * 4,
        out_shape=[jax.ShapeDtypeStruct((rows, cols), F32)] * 4,
        compiler_params=_params(("parallel",)),
    )(w, m, v, *g_parts)


def _local_step(x, target, ga, wa_in, rel_bias, later_shards, gk, t5, gb, sinks, gf):
    s, d = x.shape
    tm = min(TM_DENSE, s)
    nt = s // tm
    half = d // 2
    row = pl.BlockSpec((tm, d), lambda i: (i, 0))
    whole = lambda shape: pl.BlockSpec(shape, lambda *_: (0,) * len(shape))

    n1, = _norm_fwd("norm_a", x, ga)
    zqkv = _matmul("proj_a_qkv", n1, wa_in, dims=NN, grid=(3, nt + 1), zero_axis=1,
                   a_spec=pl.BlockSpec((tm, d), lambda j, i: (jnp.maximum(i - 1, 0), 0)),
                   b_spec=pl.BlockSpec((None, d, d), lambda j, i: (j, 0, 0)),
                   o_spec=pl.BlockSpec((None, tm, d), lambda j, i: (j, i, 0)),
                   out_shape=(3, tm + s, d), out_dtype=BF16)
    gate_a = _matmul("proj_a_gate", n1, wa_in, dims=NN, grid=(nt,),
                     a_spec=row, b_spec=pl.BlockSpec((None, d, d), lambda i: (3, 0, 0)), o_spec=row,
                     out_shape=(s, d), out_dtype=F32)
    onehot_a = _a_offset_onehot()
    diag_a = _diag_rows(onehot_a, rel_bias)
    (o_a, u_a, lse_a), gathered = _attn_a_fwd(zqkv, gate_a, diag_a, hosted=_allgather_routed(later_shards))
    wa_out, wkv, wb_in, wb_out = gathered
    wa_out = wa_out.reshape(d, d)
    wkv = wkv.reshape(d, -1)
    wb_out = wb_out.reshape(d, d)
    h1, nk, n2 = _out_norms("out_a_norms", u_a, wa_out, x, jnp.concatenate([gk, gb], axis=0))
    kvw = wkv.shape[1]
    wkv_x = jnp.concatenate([wkv[:, (i // 2) * HEAD_DIM:(i // 2 + 1) * HEAD_DIM] for i in range(8)], axis=1)
    kvx = _matmul("proj_kv", nk, wkv_x, dims=NN, grid=(nt + 1,), zero_axis=0,
                  a_spec=pl.BlockSpec((tm, d), lambda i: (jnp.maximum(i - 1, 0), 0)), b_spec=whole((d, B_KVX)),
                  o_spec=pl.BlockSpec((tm, B_KVX), lambda i: (i, 0)), out_shape=(tm + s, B_KVX), out_dtype=BF16)
    qb = _matmul("proj_b_q", n2, wb_in, dims=NN, grid=(2, nt),
                 a_spec=pl.BlockSpec((tm, d), lambda j, i: (i, 0)),
                 b_spec=pl.BlockSpec((None, d, half), lambda j, i: (j, 0, 0)),
                 o_spec=pl.BlockSpec((tm, half), lambda j, i: (i, j)), out_shape=(s, d), out_dtype=BF16)
    gate_b = _matmul("proj_b_gate", n2, wb_in, dims=NN, grid=(2, nt),
                     a_spec=pl.BlockSpec((tm, d), lambda j, i: (i, 0)),
                     b_spec=pl.BlockSpec((None, d, half), lambda j, i: (2 + j, 0, 0)),
                     o_spec=pl.BlockSpec((tm, half), lambda j, i: (i, j)), out_shape=(s, d), out_dtype=F32)
    onehot_b = _b_offset_onehot()
    base_b = jnp.roll(_diag_rows(onehot_b, t5)[..., ::-1], TQ, axis=-1)
    o_b, u_b, lse_b = _attn_b_fwd(qb, kvx, gate_b, base_b, sinks)
    dh2, loss, d_gf = _out_loss_head(u_b, wb_out, h1, target, gf)

    du_b = _matmul("dout_b", dh2, wb_out, dims=NT, grid=(nt,), a_spec=row, b_spec=whole((d, d)), o_spec=row,
                   out_shape=(s, d), out_dtype=F32)
    d_wb_out = _matmul("dw_out_b", u_b, dh2, dims=TN, grid=(2,),
                       a_spec=whole((s, d)), b_spec=pl.BlockSpec((s, half), lambda j: (0, j)),
                       o_spec=pl.BlockSpec((d, half), lambda j: (0, j)),
                       out_shape=(d, d), out_dtype=F32, also_bf16=True)
    dz_b, dkv, dsum_b, dsinks = _attn_b_bwd(qb, kvx, gate_b, o_b, du_b, lse_b, base_b, sinks)
    ddiag_b = jnp.roll(dsum_b[..., ::-1], -1, axis=-1)
    d_wb_in = _matmul("dw_in_b", n2, dz_b, dims=TN, grid=(4,),
                      a_spec=whole((s, d)), b_spec=pl.BlockSpec((None, s, half), lambda j: (j, 0, 0)),
                      o_spec=pl.BlockSpec((None, d, half), lambda j: (j, 0, 0)),
                      out_shape=(4, d, half), out_dtype=F32, also_bf16=True)
    d_wkv = _matmul("dw_kv", nk, dkv, dims=TN, grid=(1,),
                    a_spec=whole((s, d)), b_spec=whole((s, kvw)), o_spec=whole((d, kvw)),
                    out_shape=(d, kvw), out_dtype=F32, also_bf16=True)
    dh1, d_gkb = _proj_norm_bwd("dproj_kv_b", h1, dh2, jnp.concatenate([gk, gb], axis=0),
                                [(dkv[None], wkv[None]), (dz_b, wb_in)])

    du_a = _matmul("dout_a", dh1, wa_out, dims=NT, grid=(nt,), a_spec=row, b_spec=whole((d, d)), o_spec=row,
                   out_shape=(s, d), out_dtype=F32)
    d_wa_out = _matmul("dw_out_a", u_a, dh1, dims=TN, grid=(2,),
                       a_spec=whole((s, d)), b_spec=pl.BlockSpec((s, half), lambda j: (0, j)),
                       o_spec=pl.BlockSpec((d, half), lambda j: (0, j)),
                       out_shape=(d, d), out_dtype=F32, also_bf16=True)
    early = dict(a_w_out=[g.reshape(4, d // 4, d) for g in d_wa_out],
                 kv_w=[g.reshape(4, d // 4, kvw) for g in d_wkv], b_w_in=list(d_wb_in),
                 b_w_out=[g.reshape(4, d // 4, d) for g in d_wb_out])
    (dz_a, ddiag_a), early_recv = _attn_a_bwd(
        zqkv, gate_a, o_a, du_a, lse_a, diag_a, hosted=_scatter_hosted([early[n][1] for n in early]))
    d_wa_in = _matmul("dw_in_a", n1, dz_a, dims=TN, grid=(4, 2),
                      a_spec=whole((s, d)), b_spec=pl.BlockSpec((None, s, half), lambda j, h: (j, 0, h)),
                      o_spec=pl.BlockSpec((None, d, half), lambda j, h: (j, 0, h)),
                      out_shape=(4, d, d), out_dtype=F32, also_bf16=True)
    late_recv = _run_on_sequencer("scatter_a_w_in", _scatter_hosted([d_wa_in[1]]), SCATTER_PEERS, 0)
    grad_x, d_ga = _proj_norm_bwd("dproj_a", x, dh1, ga, [(dz_a, wa_in)])

    small = dict(a_norm=d_ga, kv_norm=d_gkb[0:1], b_norm=d_gkb[1:2], b_sinks=dsinks[0:1, :HEADS], final_norm=d_gf)
    small["by_offset"] = dict(a_rel_bias=(onehot_a, ddiag_a.reshape(HEADS, -1)),
                              t5_bias=(onehot_b, ddiag_b.reshape(HEADS, -1)))
    own = dict(a_w_in=d_wa_in[0], **{n: early[n][0] for n in early})
    received = dict(a_w_in=late_recv[0], **dict(zip(early, early_recv)))
    return loss, grad_x, small, own, received, d_wa_in[1]


SMALL = ("a_norm", "kv_norm", "b_norm", "b_sinks", "final_norm")
TABLES = ("a_rel_bias", "t5_bias")
BIG = ("a_w_in", "a_w_out", "kv_w", "b_w_in", "b_w_out")
ORDER = ("a_norm", "a_w_in", "a_rel_bias", "a_w_out", "kv_norm", "kv_w", "t5_bias", "b_norm", "b_w_in",
         "b_sinks", "b_w_out", "final_norm")


def kernel(x, a_norm, a_w_in, a_rel_bias, a_w_out, kv_norm, kv_w, t5_bias, b_norm, b_w_in, b_sinks, b_w_out, final_norm, loss_target, m_a_norm, m_a_w_in, m_a_rel_bias, m_a_w_out, m_kv_norm, m_kv_w, m_t5_bias, m_b_norm, m_b_w_in, m_b_sinks, m_b_w_out, m_final_norm, v_a_norm, v_a_w_in, v_a_rel_bias, v_a_w_out, v_kv_norm, v_kv_w, v_t5_bias, v_b_norm, v_b_w_in, v_b_sinks, v_b_w_out, v_final_norm):
    w = dict(a_norm=a_norm, a_w_in=a_w_in, a_rel_bias=a_rel_bias, a_w_out=a_w_out, kv_norm=kv_norm, kv_w=kv_w,
             t5_bias=t5_bias, b_norm=b_norm, b_w_in=b_w_in, b_sinks=b_sinks, b_w_out=b_w_out,
             final_norm=final_norm)
    m = dict(a_norm=m_a_norm, a_w_in=m_a_w_in, a_rel_bias=m_a_rel_bias, a_w_out=m_a_w_out, kv_norm=m_kv_norm,
             kv_w=m_kv_w, t5_bias=m_t5_bias, b_norm=m_b_norm, b_w_in=m_b_w_in, b_sinks=m_b_sinks,
             b_w_out=m_b_w_out, final_norm=m_final_norm)
    v = dict(a_norm=v_a_norm, a_w_in=v_a_w_in, a_rel_bias=v_a_rel_bias, a_w_out=v_a_w_out, kv_norm=v_kv_norm,
             kv_w=v_kv_w, t5_bias=v_t5_bias, b_norm=v_b_norm, b_w_in=v_b_w_in, b_sinks=v_b_sinks,
             b_w_out=v_b_w_out, final_norm=v_final_norm)
    d = D_MODEL
    chip = 2 * lax.axis_index("x") + lax.axis_index("y")

    shard2d = dict(a_w_in=a_w_in[0], a_w_out=a_w_out[0], kv_w=kv_w, b_w_in=b_w_in[0], b_w_out=b_w_out[0])

    wa_in, = _run_on_sequencer("allgather_first", _allgather_routed([shard2d["a_w_in"].astype(BF16)]),
                               GATHER_PEERS, 1)
    ga = _gather_gain(a_norm).reshape(1, d)

    loss, grad_x, small, own, received, after_attention = _local_step(
        x[0], loss_target[0], ga, wa_in, a_rel_bias[0], [shard2d[n].astype(BF16) for n in BIG[1:]],
        kv_norm.reshape(1, d), t5_bias, b_norm, b_sinks, final_norm.reshape(1, d))

    out = {}
    as2d = lambda a: a.reshape(-1, a.shape[-1])
    small_res, (loss_sum, *offset_sums) = _small_step(
        [small[n] for n in SMALL], [loss] + [small["by_offset"][n][1] for n in TABLES],
        [as2d(w[n]) for n in SMALL], [as2d(m[n]) for n in SMALL], [as2d(v[n]) for n in SMALL],
        [n == "a_norm" for n in SMALL])
    for n, res in zip(SMALL, small_res):
        out[n] = [r.reshape(w[n].shape) for r in res]
    loss_out = loss_sum.reshape(())
    for n, summed in zip(TABLES, offset_sums):
        grad = _diag_rows_grad(small["by_offset"][n][0], summed)
        res = _adamw("adamw_" + n, as2d(w[n]), as2d(m[n]), as2d(v[n]), [grad])
        out[n] = [r.reshape(w[n].shape) for r in res]

    core_sums = [_sum_partials("sum_" + n, own[n], received[n], chip, after_attention) for n in BIG]
    sibling_sums = (_swap_with_sibling("swap_last", core_sums[:1])
                    + _swap_with_sibling("swap_early", core_sums[1:]))

    for n, mine, theirs in zip(BIG, core_sums, sibling_sums):
        res = _adamw("adamw_" + n, shard2d[n], m[n].reshape(shard2d[n].shape), v[n].reshape(shard2d[n].shape),
                     [mine, theirs])
        out[n] = [r.reshape(w[n].shape) for r in res]

    grads = [out[n][0] for n in ORDER]
    deltas = [out[n][1] for n in ORDER]
    new_m = [out[n][2] for n in ORDER]
    new_v = [out[n][3] for n in ORDER]
    return (loss_out, grad_x[None], *grads, *deltas, *new_m, *new_v)
```

```python
import functools
import math

import jax
import jax.numpy as jnp
import numpy as np
from jax import lax
from jax.experimental import pallas as pl
from jax.experimental.pallas import tpu as pltpu
from jax.experimental.pallas import tpu_sc as plsc

F32 = jnp.float32
BF16 = jnp.bfloat16
MESH = pl.DeviceIdType.MESH

D_MODEL = 1024
HEADS = 16
HEAD_DIM = 64
CHUNK = 64
RMS_EPS = 1e-6
SCALE = HEAD_DIM ** -0.5
A_LEFT_CHUNKS = 8
A_REL_CLIP = 256
B_LEFT_CHUNKS = 2
B_KV_HEADS = 2
B_GROUP = HEADS // B_KV_HEADS
T5_BUCKETS = 32
T5_MAX_DIST = 128
ADAM_LR = 0.001
ADAM_B1 = 0.9
ADAM_B2 = 0.999
ADAM_EPS = 1e-08
ADAM_WD = 0.01
ADAM_STEP = 10

MASKED = -1e30
LANES = 128
TQ = 128
A_PAIRS = 2
A_PAIRS_FWD = 4
KB = 128
A_KBLOCKS = A_LEFT_CHUNKS * CHUNK // KB + 1
B_KBLOCKS = B_LEFT_CHUNKS * CHUNK // KB + 1
A_WIN = A_KBLOCKS * KB
B_WIN = B_KBLOCKS * KB
TM = 512
TM_DENSE = 1024
TM_PARTS = 512
VMEM_LIMIT = 56 * 1024 * 1024

NT = (((1,), (1,)), ((), ()))
TN = (((0,), (0,)), ((), ()))
NN = (((1,), (0,)), ((), ()))


def _params(sem=None):
    return pltpu.CompilerParams(dimension_semantics=sem, vmem_limit_bytes=VMEM_LIMIT)


class _Hosted:
    def __init__(self, inputs, out_shapes, sems, first, middle, last):
        self.inputs, self.out_shapes, self.sems = list(inputs), list(out_shapes), list(sems)
        self.first, self.middle, self.last = first, middle, last


def _call(body, *, name, grid, in_specs, out_specs, out_shape, args, scratch_shapes=(), sem=None, hosted=None):
    in_specs, out_specs, out_shape = list(in_specs), list(out_specs), list(out_shape)
    scratch_shapes = list(scratch_shapes)
    if hosted is None:
        out = pl.pallas_call(
            body, name=name, grid=grid, in_specs=in_specs, out_specs=out_specs, out_shape=out_shape,
            scratch_shapes=scratch_shapes, compiler_params=_params(sem))(*args)
        return list(out), []
    n_in, n_out, n_scr = len(in_specs), len(out_shape), len(scratch_shapes)
    h_in, h_out = len(hosted.inputs), len(hosted.out_shapes)
    total = int(np.prod(grid)) if grid else 1

    def wrapped(*refs):
        ins, refs = refs[:n_in], refs[n_in:]
        h_ins, refs = refs[:h_in], refs[h_in:]
        outs, refs = refs[:n_out], refs[n_out:]
        h_outs, refs = refs[:h_out], refs[h_out:]
        scr, h_sems = refs[:n_scr], refs[n_scr:]
        step = 0
        for axis, size in enumerate(grid):
            step = step * size + pl.program_id(axis)

        if hosted.first is not None:
            @pl.when(step == 0)
            def _():
                hosted.first(h_ins, h_outs, h_sems)

        body(*ins, *outs, *scr)
        if hosted.middle is not None:
            @pl.when(step == total // 2)
            def _():
                hosted.middle(h_ins, h_outs, h_sems)

        if hosted.last is not None:
            @pl.when(step == total - 1)
            def _():
                hosted.last(h_ins, h_outs, h_sems)

    out = pl.pallas_call(
        wrapped, name=name, grid=grid, in_specs=in_specs + [ANY] * h_in, out_specs=out_specs + [ANY] * h_out,
        out_shape=out_shape + hosted.out_shapes, scratch_shapes=scratch_shapes + hosted.sems,
        compiler_params=_params(("arbitrary",) * len(grid)))(*args, *hosted.inputs)
    return list(out[:n_out]), list(out[n_out:])


def _matmul(name, a, b, *, dims, grid, a_spec, b_spec, o_spec, out_shape, out_dtype,
            parts=1, resid=None, resid_spec=None, also_bf16=False, hosted=None, zero_axis=None):
    def body(*refs):
        if zero_axis is None:
            product(*refs)
        else:
            @pl.when(pl.program_id(zero_axis) == 0)
            def _():
                refs[2][...] = jnp.zeros_like(refs[2])

            @pl.when(pl.program_id(zero_axis) > 0)
            def _():
                product(*refs)

    def product(*refs):
        a_ref, b_ref = refs[:2]
        r_ref = refs[2] if resid is not None else None
        o_ref = refs[3] if resid is not None else refs[2]
        if parts == 1:
            prod = lax.dot_general(a_ref[...].astype(BF16), b_ref[...].astype(BF16), dims,
                                   preferred_element_type=F32)
        else:
            prod = None
            for part in range(parts):
                term = lax.dot_general(a_ref[part].astype(BF16), b_ref[part].astype(BF16), dims,
                                       preferred_element_type=F32)
                prod = term if prod is None else prod + term
        if resid is not None:
            prod = r_ref[...] + prod
        o_ref[...] = prod.astype(out_dtype)
        if also_bf16:
            refs[-1][...] = prod.astype(BF16)

    in_specs = [a_spec, b_spec]
    args = [a, b]
    if resid is not None:
        in_specs.append(resid_spec)
        args.append(resid)
    sem = ["parallel"] * len(grid)
    out_specs = [o_spec]
    out_shapes = [jax.ShapeDtypeStruct(out_shape, out_dtype)]
    if also_bf16:
        out_specs.append(o_spec)
        out_shapes.append(jax.ShapeDtypeStruct(out_shape, BF16))
    out, extra = _call(body, name=name, grid=grid, in_specs=in_specs, out_specs=out_specs, out_shape=out_shapes,
                       args=args, sem=tuple(sem), hosted=hosted)
    res = out[0] if not also_bf16 else tuple(out)
    return res if hosted is None else (res, extra)


def _rms_rows(x):
    return lax.rsqrt(jnp.mean(x * x, axis=-1, keepdims=True) + RMS_EPS)


def _norm_fwd(name, x, gains):
    s, d = x.shape
    n = gains.shape[0]

    def body(x_ref, g_ref, *o_refs):
        xv = x_ref[...]
        xh = xv * _rms_rows(xv)
        for i in range(n):
            o_refs[i][...] = (xh * g_ref[i:i + 1, :]).astype(BF16)

    row = pl.BlockSpec((TM, d), lambda i: (i, 0))
    return pl.pallas_call(
        body, name=name, grid=(s // TM,),
        in_specs=[row, pl.BlockSpec((n, d), lambda i: (0, 0))],
        out_specs=[row] * n,
        out_shape=[jax.ShapeDtypeStruct((s, d), BF16)] * n,
        compiler_params=_params(("parallel",)),
    )(x, gains)


def _proj_norm_bwd(name, x, dres, gains, branches):
    s, d = x.shape
    n = len(branches)
    tm = min(TM_PARTS, s)

    def body(x_ref, r_ref, g_ref, *refs):
        ab_refs, dx_ref, dg_ref = refs[:2 * n], refs[2 * n], refs[2 * n + 1]
        i = pl.program_id(0)
        xv = x_ref[...]
        r = _rms_rows(xv)
        xh = xv * r

        @pl.when(i == 0)
        def _():
            dg_ref[...] = jnp.zeros_like(dg_ref)

        a = None
        for j in range(n):
            a_ref, b_ref = ab_refs[2 * j], ab_refs[2 * j + 1]
            dn = None
            for part in range(a_ref.shape[0]):
                term = lax.dot_general(a_ref[part], b_ref[part], NT, preferred_element_type=F32)
                dn = term if dn is None else dn + term
            t = dn * g_ref[j:j + 1, :]
            a = t if a is None else a + t
            dg_ref[j:j + 1, :] += jnp.sum(dn * xh, axis=0, keepdims=True)
        dx_ref[...] = r_ref[...] + r * (a - xh * jnp.mean(xh * a, axis=-1, keepdims=True))

    row = pl.BlockSpec((tm, d), lambda i: (i, 0))
    small = pl.BlockSpec((n, d), lambda i: (0, 0))
    ab_specs, ab_args = [], []
    for a, b in branches:
        ab_specs += [pl.BlockSpec((a.shape[0], tm, a.shape[2]), lambda i: (0, i, 0)),
                     pl.BlockSpec(b.shape, lambda i: (0, 0, 0))]
        ab_args += [a, b]
    return pl.pallas_call(
        body, name=name, grid=(s // tm,),
        in_specs=[row, row, small] + ab_specs,
        out_specs=[row, small],
        out_shape=[jax.ShapeDtypeStruct((s, d), F32), jax.ShapeDtypeStruct((n, d), F32)],
        compiler_params=_params(("arbitrary",)),
    )(x, dres, gains, *ab_args)


def _out_norms(name, u, w_out, resid, gains):
    s, d = resid.shape
    n = gains.shape[0]
    tm = min(TM_DENSE, s)

    def body(u_ref, w_ref, r_ref, g_ref, h_ref, *o_refs):
        hv = r_ref[...] + jnp.dot(u_ref[...], w_ref[...], preferred_element_type=F32)
        h_ref[...] = hv
        hh = hv * _rms_rows(hv)
        for i in range(n):
            o_refs[i][...] = (hh * g_ref[i:i + 1, :]).astype(BF16)

    row = pl.BlockSpec((tm, d), lambda i: (i, 0))
    return pl.pallas_call(
        body, name=name, grid=(s // tm,),
        in_specs=[row, pl.BlockSpec((d, d), lambda i: (0, 0)), row, pl.BlockSpec((n, d), lambda i: (0, 0))],
        out_specs=[row] * (n + 1),
        out_shape=[jax.ShapeDtypeStruct((s, d), F32)] + [jax.ShapeDtypeStruct((s, d), BF16)] * n,
        compiler_params=_params(("parallel",)),
    )(u, w_out, resid, gains)


def _out_loss_head(u, w_out, resid, target, gain):
    s, d = resid.shape
    tm = min(TM_PARTS, s)

    def body(u_ref, w_ref, r_ref, t_ref, g_ref, dh_ref, loss_ref, dg_ref):
        i = pl.program_id(0)
        hv = r_ref[...] + jnp.dot(u_ref[...], w_ref[...], preferred_element_type=F32)
        r = _rms_rows(hv)
        hh = hv * r
        g = g_ref[...]
        err = hh * g - t_ref[...]
        part = 0.5 * jnp.sum(jnp.sum(err * err, axis=-1, keepdims=True) * (1.0 / d), axis=0, keepdims=True)
        dy = err * (1.0 / d)
        a = dy * g
        dh_ref[...] = r * (a - hh * jnp.mean(hh * a, axis=-1, keepdims=True))
        dg = jnp.sum(dy * hh, axis=0, keepdims=True)

        @pl.when(i == 0)
        def _():
            loss_ref[...] = part
            dg_ref[...] = dg

        @pl.when(i > 0)
        def _():
            loss_ref[...] += part
            dg_ref[...] += dg

    row = pl.BlockSpec((tm, d), lambda i: (i, 0))
    return pl.pallas_call(
        body, name="out_b_loss_head", grid=(s // tm,),
        in_specs=[row, pl.BlockSpec((d, d), lambda i: (0, 0)), row, row, pl.BlockSpec((1, d), lambda i: (0, 0))],
        out_specs=[row, pl.BlockSpec((1, 1), lambda i: (0, 0)), pl.BlockSpec((1, d), lambda i: (0, 0))],
        out_shape=[jax.ShapeDtypeStruct((s, d), F32), jax.ShapeDtypeStruct((1, 1), F32),
                   jax.ShapeDtypeStruct((1, d), F32)],
        compiler_params=_params(("arbitrary",)),
    )(u, w_out, resid, target, gain)


def _silu_parts(g):
    sig = jax.nn.sigmoid(g)
    return g * sig, sig * (1.0 + g * (1.0 - sig))


def _lane_lo(rows):
    return lax.broadcasted_iota(jnp.int32, (rows, LANES), 1) < HEAD_DIM


def _stack_pair(x):
    lo = _lane_lo(x.shape[0])
    zero = jnp.zeros_like(x)
    return jnp.concatenate([jnp.where(lo, x, zero), jnp.where(lo, zero, x)], axis=0)


def _unstack_pair(y, w):
    return jnp.where(_lane_lo(w), y[:w], y[w:])


def _block_valid(b, left_blocks, width):
    col = lax.broadcasted_iota(jnp.int32, (1, 2 * width), 1)
    col = jnp.where(col >= width, col - width, col)
    return (col // KB + (b - left_blocks)) >= 0


def _toeplitz_tile(diag_row, width, left_chunks):
    wide = width + TQ
    rolled = pltpu.roll(jnp.broadcast_to(diag_row, (TQ, wide)), 1, 1, stride=1, stride_axis=0)
    i = lax.broadcasted_iota(jnp.int32, (TQ, width), 0) // CHUNK
    j = lax.broadcasted_iota(jnp.int32, (TQ, width), 1) // CHUNK
    dc = i + left_chunks - j
    return jnp.where((dc >= 0) & (dc <= left_chunks), rolled[:, TQ:], MASKED)


def _toeplitz_sum(tile, width):
    flip = (lax.broadcasted_iota(jnp.int32, (TQ, TQ), 0) + lax.broadcasted_iota(jnp.int32, (TQ, TQ), 1)
            == TQ - 1).astype(F32)
    reversed_rows = jnp.dot(flip, tile, precision=lax.Precision.HIGHEST, preferred_element_type=F32)
    padded = jnp.concatenate([reversed_rows, jnp.zeros((TQ, TQ), F32)], axis=1)
    rolled = pltpu.roll(padded, 0, 1, stride=1, stride_axis=0)
    return jnp.sum(rolled, axis=0, keepdims=True)


def _softmax_pair(sc, w, sink=None):
    ps, inv, lses = [], [], []
    for e in range(2):
        sh = sc[:, e * w:(e + 1) * w]
        m = jnp.max(sh, axis=-1, keepdims=True)
        if sink is not None:
            m = jnp.maximum(m, sink[e])
        ex = jnp.exp(sh - m)
        l = jnp.sum(ex, axis=-1, keepdims=True)
        if sink is not None:
            l = l + jnp.exp(sink[e] - m)
        ps.append(ex.astype(BF16))
        inv.append(1.0 / l)
        lses.append(m + jnp.log(l))
    return jnp.concatenate(ps, axis=-1), inv, lses


def _softmax_pair_bwd(sc, dp, lse, delta, w):
    ps, dss = [], []
    for e in range(2):
        p = jnp.exp(sc[:, e * w:(e + 1) * w] - lse[e])
        ps.append(p)
        dss.append(p * (dp[:, e * w:(e + 1) * w] - delta[e]))
    return jnp.concatenate(ps, axis=-1), jnp.concatenate(dss, axis=-1)


def _pair_rowsums(x, lo):
    zero = jnp.zeros_like(x)
    return (jnp.sum(jnp.where(lo, x, zero), axis=-1, keepdims=True),
            jnp.sum(jnp.where(lo, zero, x), axis=-1, keepdims=True))


def _a_qkv_specs(rows, pad, pw):
    return [pl.BlockSpec((None, TQ, pw), lambda p, b: (0, b + pad // TQ, p)),
            pl.BlockSpec((None, rows, pw), lambda p, b: (1, 0, p)),
            pl.BlockSpec((None, rows, pw), lambda p, b: (2, 0, p))]


def _window(ref, b, pad, win, lanes):
    start = pl.multiple_of(b * TQ + pad - (win - TQ), KB)
    return ref[pl.ds(start, win), lanes]


def _attn_a_fwd(zqkv, g, diag, hosted=None):
    s = g.shape[0]
    pad = zqkv.shape[1] - s
    nb = s // TQ
    left = A_KBLOCKS - 1
    pairs = A_PAIRS_FWD
    pw = pairs * LANES
    wide = A_WIN + TQ

    def body(q_ref, k_ref, v_ref, g_ref, diag_ref, o_ref, u_ref, lse_ref, bias_scr):
        b = pl.program_id(1)

        @pl.when(b == 0)
        def _():
            for hh in range(2 * pairs):
                bias_scr[hh // 2, :, (hh % 2) * A_WIN:(hh % 2 + 1) * A_WIN] = _toeplitz_tile(
                    diag_ref[hh], A_WIN, A_LEFT_CHUNKS)

        def step(first_blocks):
            lo = _lane_lo(TQ)
            for pp in range(pairs):
                ln = slice(pp * LANES, (pp + 1) * LANES)
                kcat = _stack_pair(_window(k_ref, b, pad, A_WIN, ln))
                vcat = _stack_pair(_window(v_ref, b, pad, A_WIN, ln))
                sc = lax.dot_general(q_ref[:, ln] * SCALE, kcat, NT, preferred_element_type=F32) + bias_scr[pp]
                if first_blocks:
                    sc = jnp.where(_block_valid(b, left, A_WIN), sc, MASKED)
                p, inv, lses = _softmax_pair(sc, A_WIN)
                ov = jnp.dot(p, vcat, preferred_element_type=F32) * jnp.where(lo, inv[0], inv[1])
                o_ref[:, ln] = ov
                lse_ref[pp] = jnp.where(lo, lses[0], lses[1])
                sg, _ = _silu_parts(g_ref[:, ln])
                u_ref[:, ln] = (ov * sg).astype(BF16)

        @pl.when(b < left)
        def _():
            step(True)

        @pl.when(b >= left)
        def _():
            step(False)

    tile = pl.BlockSpec((TQ, pw), lambda p, b: (b, p))
    return _call(
        body, name="attn_a_fwd", grid=(HEADS // 2 // pairs, nb),
        in_specs=_a_qkv_specs(pad + s, pad, pw) + [
            tile, pl.BlockSpec((2 * pairs, 1, wide), lambda p, b: (p, 0, 0))],
        out_specs=[tile, tile, pl.BlockSpec((pairs, TQ, LANES), lambda p, b: (p, b, 0))],
        out_shape=[jax.ShapeDtypeStruct((s, D_MODEL), F32), jax.ShapeDtypeStruct((s, D_MODEL), BF16),
                   jax.ShapeDtypeStruct((HEADS // 2, s, LANES), F32)],
        scratch_shapes=[pltpu.VMEM((pairs, TQ, 2 * A_WIN), F32)],
        sem=("parallel", "arbitrary"), hosted=hosted,
        args=(zqkv, zqkv, zqkv, g, diag))


def _attn_a_bwd(zqkv, g, o, du, lse, diag, hosted=None):
    s = g.shape[0]
    pad = zqkv.shape[1] - s
    nb = s // TQ
    left = A_KBLOCKS - 1
    pw = A_PAIRS * LANES
    wide = A_WIN + TQ

    def body(q_ref, k_ref, v_ref, g_ref, o_ref, du_ref, lse_ref, diag_ref, dz_ref, ddiag_ref,
             bias_scr, dbias_acc, dk_acc, dv_acc):
        b = pl.program_id(1)

        @pl.when(b == 0)
        def _():
            for hh in range(2 * A_PAIRS):
                bias_scr[hh // 2, :, (hh % 2) * A_WIN:(hh % 2 + 1) * A_WIN] = _toeplitz_tile(
                    diag_ref[hh], A_WIN, A_LEFT_CHUNKS)
            dbias_acc[...] = jnp.zeros_like(dbias_acc)
            dk_acc[...] = jnp.zeros_like(dk_acc)
            dv_acc[...] = jnp.zeros_like(dv_acc)

        def step(first_blocks):
            lo = _lane_lo(TQ)
            rows = pl.ds(pl.multiple_of(b * TQ, TQ), TQ)
            sg, dsg = _silu_parts(g_ref[...])
            duv = du_ref[...]
            ov = o_ref[...]
            do = duv * sg
            dz_ref[3, rows, :] = (duv * ov * dsg).astype(BF16)
            do_o = do * ov
            do_bf = do.astype(BF16)
            for pp in range(A_PAIRS):
                ln = slice(pp * LANES, (pp + 1) * LANES)
                q = q_ref[:, ln] * SCALE
                kcat = _stack_pair(_window(k_ref, b, pad, A_WIN, ln))
                vcat = _stack_pair(_window(v_ref, b, pad, A_WIN, ln))
                sc = lax.dot_general(q, kcat, NT, preferred_element_type=F32) + bias_scr[pp]
                if first_blocks:
                    sc = jnp.where(_block_valid(b, left, A_WIN), sc, MASKED)
                lse_t = lse_ref[pp]
                dp = lax.dot_general(do_bf[:, ln], vcat, NT, preferred_element_type=F32)
                p, ds = _softmax_pair_bwd(sc, dp, (lse_t[:, 0:1], lse_t[:, HEAD_DIM:HEAD_DIM + 1]),
                                          _pair_rowsums(do_o[:, ln], lo), A_WIN)
                dbias_acc[pp] += ds
                dsb = ds.astype(BF16)
                dz_ref[0, rows, ln] = (jnp.dot(dsb, kcat, preferred_element_type=F32) * SCALE).astype(BF16)
                pb = p.astype(BF16)
                dob = do_bf[:, ln]
                dkt = jnp.concatenate([
                    lax.dot_general(q[:, e * HEAD_DIM:(e + 1) * HEAD_DIM], dsb[:, e * A_WIN:(e + 1) * A_WIN], TN,
                                    preferred_element_type=F32) for e in range(2)], axis=0)
                dvt = jnp.concatenate([
                    lax.dot_general(dob[:, e * HEAD_DIM:(e + 1) * HEAD_DIM], pb[:, e * A_WIN:(e + 1) * A_WIN], TN,
                                    preferred_element_type=F32) for e in range(2)], axis=0)
                for t in range(A_KBLOCKS):
                    blk = b + (pad // KB - left + t)
                    dk_acc[blk, ln, :] += dkt[:, t * KB:(t + 1) * KB]
                    dv_acc[blk, ln, :] += dvt[:, t * KB:(t + 1) * KB]

        @pl.when(b < left)
        def _():
            step(True)

        @pl.when(b >= left)
        def _():
            step(False)

        @pl.when(b == nb - 1)
        def _():
            for kb in range(s // KB):
                dz_ref[1, kb * KB:(kb + 1) * KB, :] = dk_acc[pad // KB + kb].T.astype(BF16)
                dz_ref[2, kb * KB:(kb + 1) * KB, :] = dv_acc[pad // KB + kb].T.astype(BF16)
            for hh in range(2 * A_PAIRS):
                ddiag_ref[hh] = _toeplitz_sum(
                    dbias_acc[hh // 2, :, (hh % 2) * A_WIN:(hh % 2 + 1) * A_WIN], A_WIN)

    tile = pl.BlockSpec((TQ, pw), lambda p, b: (b, p))
    diag_spec = pl.BlockSpec((2 * A_PAIRS, 1, wide), lambda p, b: (p, 0, 0))
    return _call(
        body, name="attn_a_bwd", grid=(HEADS // 2 // A_PAIRS, nb),
        in_specs=_a_qkv_specs(pad + s, pad, pw) + [
            tile, tile, tile, pl.BlockSpec((A_PAIRS, TQ, LANES), lambda p, b: (p, b, 0)), diag_spec],
        out_specs=[pl.BlockSpec((4, s, pw), lambda p, b: (0, 0, p)), diag_spec],
        out_shape=[jax.ShapeDtypeStruct((4, s, D_MODEL), BF16),
                   jax.ShapeDtypeStruct((HEADS, 1, wide), F32)],
        scratch_shapes=[pltpu.VMEM((A_PAIRS, TQ, 2 * A_WIN), F32), pltpu.VMEM((A_PAIRS, TQ, 2 * A_WIN), F32),
                        pltpu.VMEM(((pad + s) // KB, pw, KB), F32), pltpu.VMEM(((pad + s) // KB, pw, KB), F32)],
        sem=("parallel", "arbitrary"), hosted=hosted,
        args=(zqkv, zqkv, zqkv, g, o, du, lse, diag))


B_STACK = B_GROUP // 2
B_KVX = 4 * LANES
B_ROWS = B_STACK * TQ
B_WIDE = B_WIN + TQ


def _b_head_place(h):
    return h // B_GROUP, (h % B_GROUP) // 2, h % 2


def _toeplitz_tile_t(base_row, width, left_chunks):
    wide = width + TQ
    rolled = pltpu.roll(jnp.broadcast_to(base_row, (width, wide)), 0, 1, stride=1, stride_axis=0)
    j = lax.broadcasted_iota(jnp.int32, (width, TQ), 0) // CHUNK
    i = lax.broadcasted_iota(jnp.int32, (width, TQ), 1) // CHUNK
    dc = i + left_chunks - j
    return jnp.where((dc >= 0) & (dc <= left_chunks), rolled[:, :TQ], MASKED)


def _toeplitz_sum_t(tile_t, width):
    flip = (lax.broadcasted_iota(jnp.int32, (width, width), 0) + lax.broadcasted_iota(jnp.int32, (width, width), 1)
            == width - 1).astype(F32)
    reversed_rows = jnp.dot(flip, tile_t, precision=lax.Precision.HIGHEST, preferred_element_type=F32)
    padded = jnp.concatenate([reversed_rows, jnp.zeros((width, width), F32)], axis=1)
    rolled = pltpu.roll(padded, 0, 1, stride=1, stride_axis=0)
    return jnp.sum(rolled, axis=0, keepdims=True)


def _b_build_bias(base_ref, bias_scr):
    for h in range(HEADS):
        gi, pr, e = _b_head_place(h)
        bias_scr[gi, e * B_WIN:(e + 1) * B_WIN, pr * TQ:(pr + 1) * TQ] = _toeplitz_tile_t(
            base_ref[h], B_WIN, B_LEFT_CHUNKS)


def _b_stack(x, gi):
    return jnp.concatenate(
        [x[:, (B_STACK * gi + pr) * LANES:(B_STACK * gi + pr + 1) * LANES] for pr in range(B_STACK)], axis=0)


def _b_sink_rows(sink_ref, gi):
    block = lax.broadcasted_iota(jnp.int32, (1, B_ROWS), 1) // TQ
    rows = []
    for e in range(2):
        row = jnp.zeros((1, B_ROWS), F32)
        for pr in range(B_STACK):
            h = B_GROUP * gi + 2 * pr + e
            row = jnp.where(block == pr, sink_ref[0:1, h:h + 1], row)
        rows.append(row)
    return rows


def _b_scores_t(q_ref, kvv, bias_scr, gi, b, left, first_blocks):
    kcat = _stack_pair(kvv[:, gi * LANES:(gi + 1) * LANES])
    vcat = _stack_pair(kvv[:, (B_KV_HEADS + gi) * LANES:(B_KV_HEADS + gi + 1) * LANES])
    qs = _b_stack(q_ref, gi) * SCALE
    sc = lax.dot_general(kcat, qs, NT, preferred_element_type=F32) + bias_scr[gi]
    if first_blocks:
        row = lax.broadcasted_iota(jnp.int32, (2 * B_WIN, 1), 0)
        row = jnp.where(row >= B_WIN, row - B_WIN, row)
        sc = jnp.where((row // KB + (b - left)) >= 0, sc, MASKED)
    return kcat, vcat, qs, sc


def _attn_b_fwd(qb, kvx, gate, base, sinks):
    s = qb.shape[0]
    pad = kvx.shape[0] - s
    nb = s // TQ
    left = B_KBLOCKS - 1

    def body(q_ref, kv_ref, g_ref, base_ref, sink_ref, o_ref, u_ref, lse_ref, bias_scr):
        b = pl.program_id(0)

        @pl.when(b == 0)
        def _():
            _b_build_bias(base_ref, bias_scr)

        def step(first_blocks):
            kvv = _window(kv_ref, b, pad, B_WIN, slice(None))
            upper = lax.broadcasted_iota(jnp.int32, (LANES, B_ROWS), 0) < HEAD_DIM
            lse_rows = []
            for gi in range(B_KV_HEADS):
                kcat, vcat, qs, sc = _b_scores_t(q_ref, kvv, bias_scr, gi, b, left, first_blocks)
                sink = _b_sink_rows(sink_ref, gi)
                ps, inv = [], []
                for e in range(2):
                    sh = sc[e * B_WIN:(e + 1) * B_WIN]
                    m = jnp.maximum(jnp.max(sh, axis=0, keepdims=True), sink[e])
                    ex = jnp.exp(sh - m)
                    l = jnp.sum(ex, axis=0, keepdims=True) + jnp.exp(sink[e] - m)
                    ps.append(ex.astype(BF16))
                    inv.append(1.0 / l)
                    lse_rows.append(m + jnp.log(l))
                pt = jnp.concatenate(ps, axis=0)
                ot = lax.dot_general(vcat, pt, TN, preferred_element_type=F32) * jnp.where(upper, inv[0], inv[1])
                ov = ot.T
                for pr in range(B_STACK):
                    pair = B_STACK * gi + pr
                    o_ref[:, pair * LANES:(pair + 1) * LANES] = ov[pr * TQ:(pr + 1) * TQ]
            lse_ref[0] = jnp.concatenate(lse_rows + [jnp.zeros((8 - len(lse_rows), B_ROWS), F32)], axis=0)
            sg, _ = _silu_parts(g_ref[...])
            u_ref[...] = (o_ref[...] * sg).astype(BF16)

        @pl.when(b < left)
        def _():
            step(True)

        @pl.when(b >= left)
        def _():
            step(False)

    row = pl.BlockSpec((TQ, D_MODEL), lambda b: (b, 0))
    return pl.pallas_call(
        body, name="attn_b_fwd", grid=(nb,),
        in_specs=[row, pl.BlockSpec((pad + s, B_KVX), lambda b: (0, 0)), row,
                  pl.BlockSpec((HEADS, 1, B_WIDE), lambda b: (0, 0, 0)), pl.BlockSpec((1, HEADS), lambda b: (0, 0))],
        out_specs=[row, row, pl.BlockSpec((1, 8, B_ROWS), lambda b: (b, 0, 0))],
        out_shape=[jax.ShapeDtypeStruct((s, D_MODEL), F32), jax.ShapeDtypeStruct((s, D_MODEL), BF16),
                   jax.ShapeDtypeStruct((nb, 8, B_ROWS), F32)],
        scratch_shapes=[pltpu.VMEM((B_KV_HEADS, 2 * B_WIN, B_ROWS), F32)],
        compiler_params=_params(("arbitrary",)),
    )(qb, kvx, gate, base, sinks)


def _attn_b_bwd(qb, kvx, gate, o, du, lse, base, sinks):
    s = qb.shape[0]
    pad = kvx.shape[0] - s
    nb = s // TQ
    left = B_KBLOCKS - 1
    half = D_MODEL // 2

    def body(q_ref, kv_ref, g_ref, o_ref, du_ref, lse_ref, base_ref, sink_ref, dz_ref, dkv_ref, dsum_ref,
             dsink_ref, bias_scr, dbias_acc, dkv_acc, dsink_acc):
        b = pl.program_id(0)

        @pl.when(b == 0)
        def _():
            _b_build_bias(base_ref, bias_scr)
            dbias_acc[...] = jnp.zeros_like(dbias_acc)
            dkv_acc[...] = jnp.zeros_like(dkv_acc)
            dsink_acc[...] = jnp.zeros_like(dsink_acc)

        def step(first_blocks):
            kvv = _window(kv_ref, b, pad, B_WIN, slice(None))
            sg, dsg = _silu_parts(g_ref[...])
            duv = du_ref[...]
            ov = o_ref[...]
            do = duv * sg
            dgate = (duv * ov * dsg).astype(BF16)
            dz_ref[2] = dgate[:, :half]
            dz_ref[3] = dgate[:, half:]
            do_o = do * ov
            do_bf = do.astype(BF16)
            lse_all = lse_ref[0]
            dsink_rows = []
            for gi in range(B_KV_HEADS):
                kcat, vcat, qs, sc = _b_scores_t(q_ref, kvv, bias_scr, gi, b, left, first_blocks)
                dos = _b_stack(do_bf, gi)
                doo_t = _b_stack(do_o, gi).T
                delta = (jnp.sum(doo_t[:HEAD_DIM], axis=0, keepdims=True),
                         jnp.sum(doo_t[HEAD_DIM:], axis=0, keepdims=True))
                sink = _b_sink_rows(sink_ref, gi)
                dp = lax.dot_general(vcat, dos, NT, preferred_element_type=F32)
                ps, dss = [], []
                for e in range(2):
                    lse_e = lse_all[2 * gi + e:2 * gi + e + 1]
                    delta_e = delta[e]
                    p = jnp.exp(sc[e * B_WIN:(e + 1) * B_WIN] - lse_e)
                    ps.append(p.astype(BF16))
                    dss.append(p * (dp[e * B_WIN:(e + 1) * B_WIN] - delta_e))
                    dsink_rows.append(-jnp.exp(sink[e] - lse_e) * delta_e)
                ds = jnp.concatenate(dss, axis=0)
                dbias_acc[gi] += ds
                dsb = ds.astype(BF16)
                dq = (lax.dot_general(kcat, dsb, TN, preferred_element_type=F32) * SCALE).T.astype(BF16)
                for pr in range(B_STACK):
                    dz_ref[gi, :, pr * LANES:(pr + 1) * LANES] = dq[pr * TQ:(pr + 1) * TQ]
                dk = _unstack_pair(jnp.dot(dsb, qs, preferred_element_type=F32), B_WIN)
                dv = _unstack_pair(jnp.dot(jnp.concatenate(ps, axis=0), dos, preferred_element_type=F32), B_WIN)
                krows = pl.ds(pl.multiple_of(b * TQ + pad - (B_WIN - TQ), KB), B_WIN)
                dkv_acc[krows, gi * LANES:(gi + 1) * LANES] += dk
                dkv_acc[krows, (B_KV_HEADS + gi) * LANES:(B_KV_HEADS + gi + 1) * LANES] += dv
            dsink_acc[...] += jnp.concatenate(
                dsink_rows + [jnp.zeros((8 - len(dsink_rows), B_ROWS), F32)], axis=0)

        @pl.when(b < left)
        def _():
            step(True)

        @pl.when(b >= left)
        def _():
            step(False)

        @pl.when(b == nb - 1)
        def _():
            lo_s = _lane_lo(s)
            for which in range(2):
                folded = []
                for gi in range(B_KV_HEADS):
                    part = dkv_acc[pad:pad + s, (which * B_KV_HEADS + gi) * LANES:(which * B_KV_HEADS + gi + 1) * LANES]
                    folded.append(part + pltpu.roll(part, HEAD_DIM, 1))
                dkv_ref[:, which * LANES:(which + 1) * LANES] = jnp.where(lo_s, folded[0], folded[1]).astype(BF16)
            lane8 = lax.broadcasted_iota(jnp.int32, dsink_ref.shape, 1)
            tot = jnp.zeros(dsink_ref.shape, F32)
            for h in range(HEADS):
                gi, pr, e = _b_head_place(h)
                dsum_ref[h] = _toeplitz_sum_t(
                    dbias_acc[gi, e * B_WIN:(e + 1) * B_WIN, pr * TQ:(pr + 1) * TQ], B_WIN)
                per_query = dsink_acc[2 * gi + e:2 * gi + e + 1, pr * TQ:(pr + 1) * TQ]
                tot = jnp.where(lane8 == h, jnp.sum(per_query, axis=1, keepdims=True), tot)
            dsink_ref[...] = tot

    row = pl.BlockSpec((TQ, D_MODEL), lambda b: (b, 0))
    base_spec = pl.BlockSpec((HEADS, 1, B_WIDE), lambda b: (0, 0, 0))
    return pl.pallas_call(
        body, name="attn_b_bwd", grid=(nb,),
        in_specs=[row, pl.BlockSpec((pad + s, B_KVX), lambda b: (0, 0)), row, row, row,
                  pl.BlockSpec((1, 8, B_ROWS), lambda b: (b, 0, 0)), base_spec,
                  pl.BlockSpec((1, HEADS), lambda b: (0, 0))],
        out_specs=[pl.BlockSpec((4, TQ, half), lambda b: (0, b, 0)),
                   pl.BlockSpec((s, 2 * LANES), lambda b: (0, 0)), base_spec,
                   pl.BlockSpec((8, LANES), lambda b: (0, 0))],
        out_shape=[jax.ShapeDtypeStruct((4, s, half), BF16), jax.ShapeDtypeStruct((s, 2 * LANES), BF16),
                   jax.ShapeDtypeStruct((HEADS, 1, B_WIDE), F32), jax.ShapeDtypeStruct((8, LANES), F32)],
        scratch_shapes=[pltpu.VMEM((B_KV_HEADS, 2 * B_WIN, B_ROWS), F32),
                        pltpu.VMEM((B_KV_HEADS, 2 * B_WIN, B_ROWS), F32),
                        pltpu.VMEM((pad + s, B_KVX), F32), pltpu.VMEM((8, B_ROWS), F32)],
        compiler_params=_params(("arbitrary",)),
    )(qb, kvx, gate, o, du, lse, base, sinks)


def _t5_bucket(rel):
    nb = T5_BUCKETS // 2
    max_exact = nb // 2
    ret = jnp.where(rel > 0, nb, 0)
    n = jnp.abs(rel)
    nf = jnp.maximum(n, 1).astype(jnp.float32)
    large = max_exact + (jnp.log(nf / max_exact) / math.log(T5_MAX_DIST / max_exact)
                         * (nb - max_exact)).astype(jnp.int32)
    large = jnp.minimum(large, nb - 1)
    return ret + jnp.where(n < max_exact, n, large)


def _a_offset_onehot():
    c = np.arange(A_WIN + TQ)
    dist = A_LEFT_CHUNKS * CHUNK + TQ - 1 - c
    idx = np.clip(dist, -A_REL_CLIP, A_REL_CLIP) + A_REL_CLIP
    onehot = np.zeros((A_WIN + TQ, 2 * A_REL_CLIP + 1), np.float32)
    onehot[c, idx] = 1.0
    return jnp.asarray(onehot)


def _b_offset_onehot():
    c = jnp.arange(B_WIN + TQ, dtype=jnp.int32)
    rel = c - (TQ - 1) - B_LEFT_CHUNKS * CHUNK
    return (_t5_bucket(rel)[:, None] == jnp.arange(T5_BUCKETS)[None, :]).astype(F32)


def _diag_rows(onehot, table):
    rows = jnp.dot(onehot, table.astype(F32), precision=lax.Precision.HIGHEST)
    return rows.T.reshape(HEADS, 1, onehot.shape[0])


def _diag_rows_grad(onehot, ddiag):
    return jnp.dot(ddiag.reshape(HEADS, onehot.shape[0]), onehot, precision=lax.Precision.HIGHEST).T


def _position():
    x, y, c = lax.axis_index("x"), lax.axis_index("y"), lax.axis_index("c")
    chips = [(1 - x, y), (x, 1 - y), (1 - x, 1 - y)]
    return x, y, c, chips


ANY = pl.BlockSpec(memory_space=pl.ANY)


def _allgather_hosted(shards, split):
    n = len(shards)

    def part(ref, t, half):
        if not split[t]:
            return ref
        rows = shards[t].shape[0] // 2
        return ref.at[pl.ds(half * rows, rows)]

    def copies(kind, ins, outs, sems):
        send_sems, recv_sems, pass_send, pass_recv, local_sems = sems
        x, y, c, chips = _position()
        mine = 2 * x + y
        if kind == "local":
            return [pltpu.make_async_copy(ins[t], outs[t].at[mine], local_sems.at[t]) for t in range(n)]
        made = []
        for t in range(n):
            for j, chip in enumerate(chips):
                theirs = 2 * chip[0] + chip[1]
                far = dict(send_sem=send_sems.at[3 * t + j], recv_sem=recv_sems.at[3 * t + j],
                           device_id=(chip[0], chip[1], c), device_id_type=MESH)
                near = dict(send_sem=pass_send.at[3 * t + j], recv_sem=pass_recv.at[3 * t + j],
                            device_id=(x, y, 1 - c), device_id_type=MESH)
                here = part(outs[t].at[theirs], t, c)
                if kind == "send":
                    made.append(pltpu.make_async_remote_copy(
                        src_ref=part(ins[t], t, c), dst_ref=part(outs[t].at[mine], t, c), **far))
                elif kind == "landed":
                    made.append(pltpu.make_async_remote_copy(src_ref=here, dst_ref=here, **far))
                elif not split[t]:
                    made.append(None)
                elif kind == "pass":
                    made.append(pltpu.make_async_remote_copy(src_ref=here, dst_ref=here, **near))
                else:
                    other = part(outs[t].at[theirs], t, 1 - c)
                    made.append(pltpu.make_async_remote_copy(src_ref=other, dst_ref=other, **near))
        return made

    def first(ins, outs, sems):
        for cp in copies("local", ins, outs, sems) + copies("send", ins, outs, sems):
            cp.start()

    def middle(ins, outs, sems):
        for got, cp in zip(copies("landed", ins, outs, sems), copies("pass", ins, outs, sems)):
            got.wait_recv()
            if cp is not None:
                cp.start()

    def last(ins, outs, sems):
        for cp in copies("passed", ins, outs, sems):
            if cp is not None:
                cp.wait_recv()
        for cp in copies("send", ins, outs, sems) + copies("pass", ins, outs, sems):
            if cp is not None:
                cp.wait_send()
        for cp in copies("local", ins, outs, sems):
            cp.wait()

    return _Hosted(shards, [jax.ShapeDtypeStruct((4,) + w.shape, w.dtype) for w in shards],
                   [pltpu.SemaphoreType.DMA((3 * n,))] * 4 + [pltpu.SemaphoreType.DMA((n,))],
                   first, middle, last)


def _allgather_routed(shards):
    n = len(shards)

    def piece(block_ref, t, c, quarter=None):
        half = shards[t].shape[0] // 2
        if quarter is None:
            return block_ref.at[pl.ds(c * half, half)]
        return block_ref.at[pl.ds(c * half + quarter * (half // 2), half // 2)]

    def copies(kind, ins, outs, sems):
        ici_send, ici_recv, pass_send, pass_recv, local_sems = sems
        x, y, c, chips = _position()
        mine = 2 * x + y
        if kind == "local":
            return [pltpu.make_async_copy(ins[t], outs[t].at[mine], local_sems.at[t]) for t in range(n)]
        ids = [2 * chip[0] + chip[1] for chip in chips]
        made = []
        for t in range(n):
            def ici(k, to):
                return dict(send_sem=ici_send.at[4 * t + k], recv_sem=ici_recv.at[4 * t + k],
                            device_id=(chips[to][0], chips[to][1], c), device_id_type=MESH)

            def d2d(k):
                return dict(send_sem=pass_send.at[4 * t + k], recv_sem=pass_recv.at[4 * t + k],
                            device_id=(x, y, 1 - c), device_id_type=MESH)

            def same(ref, where):
                return pltpu.make_async_remote_copy(src_ref=ref, dst_ref=ref, **where)

            if kind == "send":
                for k in range(2):
                    made.append(pltpu.make_async_remote_copy(
                        src_ref=piece(ins[t], t, c), dst_ref=piece(outs[t].at[mine], t, c), **ici(k, k)))
            elif kind == "landed":
                made += [same(piece(outs[t].at[ids[k]], t, c), ici(k, k)) for k in range(2)]
            elif kind == "forward":
                made.append(same(piece(outs[t].at[ids[0]], t, c, 0), ici(2, 1)))
                made.append(same(piece(outs[t].at[ids[1]], t, c, 1), ici(3, 0)))
            elif kind == "arrived":
                made.append(same(piece(outs[t].at[ids[2]], t, c, 0), ici(2, 1)))
                made.append(same(piece(outs[t].at[ids[2]], t, c, 1), ici(3, 0)))
            else:
                core = 1 - c if kind == "passed" else c
                if kind in ("pass halves", "passed"):
                    made += [same(piece(outs[t].at[ids[k]], t, core), d2d(k)) for k in range(2)]
                if kind in ("pass quarters", "passed"):
                    made += [same(piece(outs[t].at[ids[2]], t, core, k), d2d(2 + k)) for k in range(2)]
        return made

    def first(ins, outs, sems):
        for cp in copies("local", ins, outs, sems) + copies("send", ins, outs, sems):
            cp.start()

    def middle(ins, outs, sems):
        for got, onward, near in zip(copies("landed", ins, outs, sems), copies("forward", ins, outs, sems),
                                     copies("pass halves", ins, outs, sems)):
            got.wait_recv()
            near.start()
            onward.start()

    def last(ins, outs, sems):
        quarters = copies("pass quarters", ins, outs, sems)
        for got, near in zip(copies("arrived", ins, outs, sems), quarters):
            got.wait_recv()
            near.start()
        for cp in copies("passed", ins, outs, sems):
            cp.wait_recv()
        for cp in (copies("send", ins, outs, sems) + copies("forward", ins, outs, sems)
                   + copies("pass halves", ins, outs, sems) + quarters):
            cp.wait_send()
        for cp in copies("local", ins, outs, sems):
            cp.wait()

    return _Hosted(shards, [jax.ShapeDtypeStruct((4,) + w.shape, w.dtype) for w in shards],
                   [pltpu.SemaphoreType.DMA((4 * n,))] * 4 + [pltpu.SemaphoreType.DMA((n,))],
                   first, middle, last)


def _scatter_hosted(grads):
    n = len(grads)

    def copies(ins, outs, sems):
        send_sems, recv_sems = sems
        x, y, c, chips = _position()
        return [pltpu.make_async_remote_copy(
            src_ref=ins[t].at[2 * chip[0] + chip[1]], dst_ref=outs[t].at[j],
            send_sem=send_sems.at[3 * t + j], recv_sem=recv_sems.at[3 * t + j],
            device_id=(chip[0], chip[1], c), device_id_type=MESH)
            for t in range(n) for j, chip in enumerate(chips)]

    def first(ins, outs, sems):
        for cp in copies(ins, outs, sems):
            cp.start()

    def last(ins, outs, sems):
        for cp in copies(ins, outs, sems):
            cp.wait()

    return _Hosted(grads, [jax.ShapeDtypeStruct((3,) + g.shape[1:], g.dtype) for g in grads],
                   [pltpu.SemaphoreType.DMA((3 * n,))] * 2, first, None, last)


GATHER_PEERS = "x and y neighbours (same core) and the sibling core"
SCATTER_PEERS = "the same core of the three other chips"
EVERYONE = "the seven other devices"


def _run_on_sequencer(name, hosted, peers, collective_id):
    ins = [jax.new_ref(a, memory_space=pltpu.MemorySpace.HBM) for a in hosted.inputs]
    outs = [jax.empty_ref(shape, memory_space=pltpu.MemorySpace.HBM) for shape in hosted.out_shapes]

    @pl.kernel(mesh=plsc.ScalarSubcoreMesh(axis_name="sequencer", num_cores=1), name=name,
               scratch_types=tuple(hosted.sems), compiler_params=pltpu.CompilerParams(collective_id=collective_id))
    def launch(*sems):
        x, y, c, chips = _position()
        if peers == GATHER_PEERS:
            devices = [(chip[0], chip[1], c) for chip in chips[:2]] + [(x, y, 1 - c)]
        elif peers == SCATTER_PEERS:
            devices = [(chip[0], chip[1], c) for chip in chips]
        else:
            devices = [(x ^ (k >> 2), y ^ ((k >> 1) & 1), c ^ (k & 1)) for k in range(1, 8)]
        barrier = pltpu.get_barrier_semaphore()
        for device in devices:
            pl.semaphore_signal(barrier, inc=1, device_id=device, device_id_type=MESH)
        pl.semaphore_wait(barrier, len(devices))
        hosted.first(ins, outs, sems)
        if hosted.middle is not None:
            hosted.middle(ins, outs, sems)
        hosted.last(ins, outs, sems)

    launch()
    return [o[...] for o in outs]


def _run_alone(name, hosted):
    n_in = len(hosted.inputs)
    n_out = len(hosted.out_shapes)

    def body(*refs):
        ins, outs, sems = refs[:n_in], refs[n_in:n_in + n_out], refs[n_in + n_out:]
        hosted.first(ins, outs, sems)
        if hosted.middle is not None:
            hosted.middle(ins, outs, sems)
        hosted.last(ins, outs, sems)

    return pl.pallas_call(
        body, name=name, in_specs=[ANY] * n_in, out_specs=[ANY] * n_out, out_shape=hosted.out_shapes,
        scratch_shapes=hosted.sems)(*hosted.inputs)


def _gather_gain(shard):
    def body(in_ref, out_ref, send_sems, recv_sems):
        x, y, c, chips = _position()
        out_ref[2 * x + y] = in_ref[...]
        sends = [pltpu.make_async_remote_copy(
            src_ref=in_ref, dst_ref=out_ref.at[2 * x + y], send_sem=send_sems.at[j], recv_sem=recv_sems.at[j],
            device_id=(chip[0], chip[1], c), device_id_type=MESH) for j, chip in enumerate(chips)]
        for cp in sends:
            cp.start()
        for j, chip in enumerate(chips):
            pltpu.make_async_remote_copy(
                src_ref=in_ref, dst_ref=out_ref.at[2 * chip[0] + chip[1]], send_sem=send_sems.at[j],
                recv_sem=recv_sems.at[j], device_id=(chip[0], chip[1], c), device_id_type=MESH).wait_recv()
        for cp in sends:
            cp.wait_send()

    vmem = pl.BlockSpec(memory_space=pltpu.VMEM)
    return pl.pallas_call(
        body, name="gather_gain", in_specs=[vmem], out_specs=vmem,
        out_shape=jax.ShapeDtypeStruct((4,) + shard.shape, shard.dtype),
        scratch_shapes=[pltpu.SemaphoreType.DMA((3,))] * 2,
    )(shard)


def _swap_with_sibling(name, blocks):
    n = len(blocks)

    def body(*refs):
        ins, outs = refs[:n], refs[n:2 * n]
        send_sems, recv_sems = refs[2 * n:]
        x, y, c, _ = _position()
        sends = [pltpu.make_async_remote_copy(
            src_ref=ins[t], dst_ref=outs[t], send_sem=send_sems.at[t], recv_sem=recv_sems.at[t],
            device_id=(x, y, 1 - c), device_id_type=MESH) for t in range(n)]
        for cp in sends:
            cp.start()
        for cp in sends:
            cp.wait()

    return pl.pallas_call(
        body, name=name,
        in_specs=[ANY] * n, out_specs=[ANY] * n,
        out_shape=[jax.ShapeDtypeStruct(b.shape, b.dtype) for b in blocks],
        scratch_shapes=[pltpu.SemaphoreType.DMA((n,))] * 2,
    )(*blocks)


def _everyone_hosted(terms):
    nt = len(terms)

    def copies(kind, ins, outs, sems):
        send_sems, recv_sems, local_sems = sems
        x, y, c, _ = _position()
        me = 4 * x + 2 * y + c
        if kind == "local":
            return [pltpu.make_async_copy(ins[t], outs[t].at[me], local_sems.at[t]) for t in range(nt)]
        made = []
        for t in range(nt):
            for k in range(1, 8):
                peer = (x ^ (k >> 2), y ^ ((k >> 1) & 1), c ^ (k & 1))
                slot = me if kind == "send" else me ^ k
                made.append(pltpu.make_async_remote_copy(
                    src_ref=ins[t], dst_ref=outs[t].at[slot], send_sem=send_sems.at[7 * t + k - 1],
                    recv_sem=recv_sems.at[7 * t + k - 1], device_id=peer, device_id_type=MESH))
        return made

    def first(ins, outs, sems):
        for cp in copies("local", ins, outs, sems) + copies("send", ins, outs, sems):
            cp.start()

    def last(ins, outs, sems):
        for cp in copies("landed", ins, outs, sems):
            cp.wait_recv()
        for cp in copies("send", ins, outs, sems):
            cp.wait_send()
        for cp in copies("local", ins, outs, sems):
            cp.wait()

    return _Hosted(terms, [jax.ShapeDtypeStruct((8,) + a.shape, F32) for a in terms],
                   [pltpu.SemaphoreType.DMA((7 * nt,))] * 2 + [pltpu.SemaphoreType.DMA((nt,))], first, None, last)


def _small_step(partials, extras, ws, ms, vs, shard_of):
    n = len(partials)
    terms = list(partials) + list(extras)
    nt = len(terms)
    rows = [t for t in range(nt) if terms[t].shape[0] == 1]
    mats = [t for t in range(nt) if terms[t].shape[0] != 1]
    row_block = (8, max(terms[t].shape[1] for t in rows))
    assert len(rows) <= row_block[0]
    vmem = pl.BlockSpec(memory_space=pltpu.VMEM)

    def pack(*refs):
        packed = refs[-1]
        packed[...] = jnp.zeros_like(packed)
        for i, t in enumerate(rows):
            packed[i:i + 1, 0:terms[t].shape[1]] = refs[i][...]

    packed = pl.pallas_call(pack, name="small_pack", in_specs=[vmem] * len(rows), out_specs=vmem,
                            out_shape=jax.ShapeDtypeStruct(row_block, F32))(*[terms[t] for t in rows])
    slots = _run_on_sequencer("allgather_small", _everyone_hosted([packed] + [terms[t] for t in mats]),
                              EVERYONE, 2)

    def body(*refs):
        slot_refs, refs = refs[:len(slots)], refs[len(slots):]
        w_refs, refs = refs[:n], refs[n:]
        m_refs, refs = refs[:n], refs[n:]
        v_refs, outs = refs[:n], refs[n:]
        sums = []
        for ref in slot_refs:
            g = ref[0]
            for dev in range(1, 8):
                g = g + ref[dev]
            sums.append(g)
        chip = 2 * lax.axis_index("x") + lax.axis_index("y")
        for t in range(nt):
            if t in rows:
                i = rows.index(t)
                g = sums[0][i:i + 1, 0:terms[t].shape[1]]
            else:
                g = sums[1 + mats.index(t)]
            if t >= n:
                outs[4 * n + t - n][...] = g
                continue
            if shard_of[t]:
                width = ws[t].shape[-1]
                mine = jnp.zeros(ws[t].shape, F32)
                for s in range(4):
                    mine = jnp.where(chip == s, g[:, s * width:(s + 1) * width], mine)
                g = mine
            delta, mn, vn = _adamw_math(w_refs[t][...], g, m_refs[t][...], v_refs[t][...])
            outs[4 * t][...] = g
            outs[4 * t + 1][...] = delta
            outs[4 * t + 2][...] = mn
            outs[4 * t + 3][...] = vn

    out_shapes = []
    for t in range(n):
        out_shapes += [jax.ShapeDtypeStruct(ws[t].shape, F32)] * 4
    out_shapes += [jax.ShapeDtypeStruct(a.shape, F32) for a in extras]
    res = pl.pallas_call(
        body, name="small_step",
        in_specs=[vmem] * (len(slots) + 3 * n), out_specs=[vmem] * len(out_shapes), out_shape=out_shapes,
    )(*slots, *ws, *ms, *vs)
    return [res[4 * t:4 * t + 4] for t in range(n)], res[4 * n:4 * n + nt - n]


def _adamw_math(w, g, m, v):
    m = ADAM_B1 * m + (1.0 - ADAM_B1) * g
    v = ADAM_B2 * v + (1.0 - ADAM_B2) * (g * g)
    m_hat = m / (1.0 - ADAM_B1 ** ADAM_STEP)
    v_hat = v / (1.0 - ADAM_B2 ** ADAM_STEP)
    delta = -ADAM_LR * (m_hat / (jnp.sqrt(v_hat) + ADAM_EPS) + ADAM_WD * w)
    return delta, m, v


def _row_tile(rows):
    return 256 if rows % 256 == 0 else rows


def _sum_partials(name, own, recv, chip, after):
    rows, cols = own.shape[1:]
    tr = _row_tile(rows)

    def body(chip_ref, own_ref, recv_ref, after_ref, o_ref):
        acc = own_ref[...]
        for j in range(3):
            acc = acc + recv_ref[j].astype(F32)
        o_ref[...] = acc

    return pl.pallas_call(
        body, name=name,
        grid_spec=pltpu.PrefetchScalarGridSpec(
            num_scalar_prefetch=1, grid=(rows // tr,),
            in_specs=[pl.BlockSpec((None, tr, cols), lambda i, chip_ref: (chip_ref[0], i, 0)),
                      pl.BlockSpec((3, tr, cols), lambda i, chip_ref: (0, i, 0)), ANY],
            out_specs=pl.BlockSpec((tr, cols), lambda i, chip_ref: (i, 0))),
        out_shape=jax.ShapeDtypeStruct((rows, cols), F32),
        compiler_params=_params(("parallel",)),
    )(chip.reshape(1).astype(jnp.int32), own, recv, after)


def _adamw(name, w, m, v, g_parts):
    rows, cols = w.shape
    tr = _row_tile(rows)
    n = len(g_parts)

    def body(w_ref, m_ref, v_ref, *refs):
        g_refs = refs[:n]
        go_ref, d_ref, mo_ref, vo_ref = refs[n:]
        g = g_refs[0][...]
        for r in g_refs[1:]:
            g = g + r[...]
        delta, mn, vn = _adamw_math(w_ref[...], g, m_ref[...], v_ref[...])
        go_ref[...] = g
        d_ref[...] = delta
        mo_ref[...] = mn
        vo_ref[...] = vn

    spec = pl.BlockSpec((tr, cols), lambda i: (i, 0))
    return pl.pallas_call(
        body, name=name, grid=(rows // tr,),
        in_specs=[spec] * (3 + n), out_specs=[spec] * 4,
        out_shape=[jax.ShapeDtypeStruct((rows, cols), F32)] * 4,
        compiler_params=_params(("parallel",)),
    )(w, m, v, *g_parts)


def _local_step(x, target, ga, wa_in, rel_bias, later_shards, gk, t5, gb, sinks, gf):
    s, d = x.shape
    tm = min(TM_DENSE, s)
    nt = s // tm
    half = d // 2
    row = pl.BlockSpec((tm, d), lambda i: (i, 0))
    whole = lambda shape: pl.BlockSpec(shape, lambda *_: (0,) * len(shape))

    n1, = _norm_fwd("norm_a", x, ga)
    zqkv = _matmul("proj_a_qkv", n1, wa_in, dims=NN, grid=(3, nt + 1), zero_axis=1,
                   a_spec=pl.BlockSpec((tm, d), lambda j, i: (jnp.maximum(i - 1, 0), 0)),
                   b_spec=pl.BlockSpec((None, d, d), lambda j, i: (j, 0, 0)),
                   o_spec=pl.BlockSpec((None, tm, d), lambda j, i: (j, i, 0)),
                   out_shape=(3, tm + s, d), out_dtype=BF16)
    gate_a = _matmul("proj_a_gate", n1, wa_in, dims=NN, grid=(nt,),
                     a_spec=row, b_spec=pl.BlockSpec((None, d, d), lambda i: (3, 0, 0)), o_spec=row,
                     out_shape=(s, d), out_dtype=F32)
    onehot_a = _a_offset_onehot()
    diag_a = _diag_rows(onehot_a, rel_bias)
    (o_a, u_a, lse_a), gathered = _attn_a_fwd(zqkv, gate_a, diag_a, hosted=_allgather_routed(later_shards))
    wa_out, wkv, wb_in, wb_out = gathered
    wa_out = wa_out.reshape(d, d)
    wkv = wkv.reshape(d, -1)
    wb_out = wb_out.reshape(d, d)
    h1, nk, n2 = _out_norms("out_a_norms", u_a, wa_out, x, jnp.concatenate([gk, gb], axis=0))
    kvw = wkv.shape[1]
    wkv_x = jnp.concatenate([wkv[:, (i // 2) * HEAD_DIM:(i // 2 + 1) * HEAD_DIM] for i in range(8)], axis=1)
    kvx = _matmul("proj_kv", nk, wkv_x, dims=NN, grid=(nt + 1,), zero_axis=0,
                  a_spec=pl.BlockSpec((tm, d), lambda i: (jnp.maximum(i - 1, 0), 0)), b_spec=whole((d, B_KVX)),
                  o_spec=pl.BlockSpec((tm, B_KVX), lambda i: (i, 0)), out_shape=(tm + s, B_KVX), out_dtype=BF16)
    qb = _matmul("proj_b_q", n2, wb_in, dims=NN, grid=(2, nt),
                 a_spec=pl.BlockSpec((tm, d), lambda j, i: (i, 0)),
                 b_spec=pl.BlockSpec((None, d, half), lambda j, i: (j, 0, 0)),
                 o_spec=pl.BlockSpec((tm, half), lambda j, i: (i, j)), out_shape=(s, d), out_dtype=BF16)
    gate_b = _matmul("proj_b_gate", n2, wb_in, dims=NN, grid=(2, nt),
                     a_spec=pl.BlockSpec((tm, d), lambda j, i: (i, 0)),
                     b_spec=pl.BlockSpec((None, d, half), lambda j, i: (2 + j, 0, 0)),
                     o_spec=pl.BlockSpec((tm, half), lambda j, i: (i, j)), out_shape=(s, d), out_dtype=F32)
    onehot_b = _b_offset_onehot()
    base_b = jnp.roll(_diag_rows(onehot_b, t5)[..., ::-1], TQ, axis=-1)
    o_b, u_b, lse_b = _attn_b_fwd(qb, kvx, gate_b, base_b, sinks)
    dh2, loss, d_gf = _out_loss_head(u_b, wb_out, h1, target, gf)

    du_b = _matmul("dout_b", dh2, wb_out, dims=NT, grid=(nt,), a_spec=row, b_spec=whole((d, d)), o_spec=row,
                   out_shape=(s, d), out_dtype=F32)
    d_wb_out = _matmul("dw_out_b", u_b, dh2, dims=TN, grid=(2,),
                       a_spec=whole((s, d)), b_spec=pl.BlockSpec((s, half), lambda j: (0, j)),
                       o_spec=pl.BlockSpec((d, half), lambda j: (0, j)),
                       out_shape=(d, d), out_dtype=F32, also_bf16=True)
    dz_b, dkv, dsum_b, dsinks = _attn_b_bwd(qb, kvx, gate_b, o_b, du_b, lse_b, base_b, sinks)
    ddiag_b = jnp.roll(dsum_b[..., ::-1], -1, axis=-1)
    d_wb_in = _matmul("dw_in_b", n2, dz_b, dims=TN, grid=(4,),
                      a_spec=whole((s, d)), b_spec=pl.BlockSpec((None, s, half), lambda j: (j, 0, 0)),
                      o_spec=pl.BlockSpec((None, d, half), lambda j: (j, 0, 0)),
                      out_shape=(4, d, half), out_dtype=F32, also_bf16=True)
    d_wkv = _matmul("dw_kv", nk, dkv, dims=TN, grid=(1,),
                    a_spec=whole((s, d)), b_spec=whole((s, kvw)), o_spec=whole((d, kvw)),
                    out_shape=(d, kvw), out_dtype=F32, also_bf16=True)
    dh1, d_gkb = _proj_norm_bwd("dproj_kv_b", h1, dh2, jnp.concatenate([gk, gb], axis=0),
                                [(dkv[None], wkv[None]), (dz_b, wb_in)])

    du_a = _matmul("dout_a", dh1, wa_out, dims=NT, grid=(nt,), a_spec=row, b_spec=whole((d, d)), o_spec=row,
                   out_shape=(s, d), out_dtype=F32)
    d_wa_out = _matmul("dw_out_a", u_a, dh1, dims=TN, grid=(2,),
                       a_spec=whole((s, d)), b_spec=pl.BlockSpec((s, half), lambda j: (0, j)),
                       o_spec=pl.BlockSpec((d, half), lambda j: (0, j)),
                       out_shape=(d, d), out_dtype=F32, also_bf16=True)
    early = dict(a_w_out=[g.reshape(4, d // 4, d) for g in d_wa_out],
                 kv_w=[g.reshape(4, d // 4, kvw) for g in d_wkv], b_w_in=list(d_wb_in),
                 b_w_out=[g.reshape(4, d // 4, d) for g in d_wb_out])
    (dz_a, ddiag_a), early_recv = _attn_a_bwd(
        zqkv, gate_a, o_a, du_a, lse_a, diag_a, hosted=_scatter_hosted([early[n][1] for n in early]))
    d_wa_in = _matmul("dw_in_a", n1, dz_a, dims=TN, grid=(4, 2),
                      a_spec=whole((s, d)), b_spec=pl.BlockSpec((None, s, half), lambda j, h: (j, 0, h)),
                      o_spec=pl.BlockSpec((None, d, half), lambda j, h: (j, 0, h)),
                      out_shape=(4, d, d), out_dtype=F32, also_bf16=True)
    late_recv = _run_on_sequencer("scatter_a_w_in", _scatter_hosted([d_wa_in[1]]), SCATTER_PEERS, 0)
    grad_x, d_ga = _proj_norm_bwd("dproj_a", x, dh1, ga, [(dz_a, wa_in)])

    small = dict(a_norm=d_ga, kv_norm=d_gkb[0:1], b_norm=d_gkb[1:2], b_sinks=dsinks[0:1, :HEADS], final_norm=d_gf)
    small["by_offset"] = dict(a_rel_bias=(onehot_a, ddiag_a.reshape(HEADS, -1)),
                              t5_bias=(onehot_b, ddiag_b.reshape(HEADS, -1)))
    own = dict(a_w_in=d_wa_in[0], **{n: early[n][0] for n in early})
    received = dict(a_w_in=late_recv[0], **dict(zip(early, early_recv)))
    return loss, grad_x, small, own, received, d_wa_in[1]


SMALL = ("a_norm", "kv_norm", "b_norm", "b_sinks", "final_norm")
TABLES = ("a_rel_bias", "t5_bias")
BIG = ("a_w_in", "a_w_out", "kv_w", "b_w_in", "b_w_out")
ORDER = ("a_norm", "a_w_in", "a_rel_bias", "a_w_out", "kv_norm", "kv_w", "t5_bias", "b_norm", "b_w_in",
         "b_sinks", "b_w_out", "final_norm")


def kernel(x, a_norm, a_w_in, a_rel_bias, a_w_out, kv_norm, kv_w, t5_bias, b_norm, b_w_in, b_sinks, b_w_out, final_norm, loss_target, m_a_norm, m_a_w_in, m_a_rel_bias, m_a_w_out, m_kv_norm, m_kv_w, m_t5_bias, m_b_norm, m_b_w_in, m_b_sinks, m_b_w_out, m_final_norm, v_a_norm, v_a_w_in, v_a_rel_bias, v_a_w_out, v_kv_norm, v_kv_w, v_t5_bias, v_b_norm, v_b_w_in, v_b_sinks, v_b_w_out, v_final_norm):
    w = dict(a_norm=a_norm, a_w_in=a_w_in, a_rel_bias=a_rel_bias, a_w_out=a_w_out, kv_norm=kv_norm, kv_w=kv_w,
             t5_bias=t5_bias, b_norm=b_norm, b_w_in=b_w_in, b_sinks=b_sinks, b_w_out=b_w_out,
             final_norm=final_norm)
    m = dict(a_norm=m_a_norm, a_w_in=m_a_w_in, a_rel_bias=m_a_rel_bias, a_w_out=m_a_w_out, kv_norm=m_kv_norm,
             kv_w=m_kv_w, t5_bias=m_t5_bias, b_norm=m_b_norm, b_w_in=m_b_w_in, b_sinks=m_b_sinks,
             b_w_out=m_b_w_out, final_norm=m_final_norm)
    v = dict(a_norm=v_a_norm, a_w_in=v_a_w_in, a_rel_bias=v_a_rel_bias, a_w_out=v_a_w_out, kv_norm=v_kv_norm,
             kv_w=v_kv_w, t5_bias=v_t5_bias, b_norm=v_b_norm, b_w_in=v_b_w_in, b_sinks=v_b_sinks,
             b_w_out=v_b_w_out, final_norm=v_final_norm)
    d = D_MODEL
    chip = 2 * lax.axis_index("x") + lax.axis_index("y")

    shard2d = dict(a_w_in=a_w_in[0], a_w_out=a_w_out[0], kv_w=kv_w, b_w_in=b_w_in[0], b_w_out=b_w_out[0])

    wa_in, = _run_on_sequencer("allgather_first", _allgather_routed([shard2d["a_w_in"].astype(BF16)]),
                               GATHER_PEERS, 1)
    ga = _gather_gain(a_norm).reshape(1, d)

    loss, grad_x, small, own, received, after_attention = _local_step(
        x[0], loss_target[0], ga, wa_in, a_rel_bias[0], [shard2d[n].astype(BF16) for n in BIG[1:]],
        kv_norm.reshape(1, d), t5_bias, b_norm, b_sinks, final_norm.reshape(1, d))

    out = {}
    as2d = lambda a: a.reshape(-1, a.shape[-1])
    small_res, (loss_sum, *offset_sums) = _small_step(
        [small[n] for n in SMALL], [loss] + [small["by_offset"][n][1] for n in TABLES],
        [as2d(w[n]) for n in SMALL], [as2d(m[n]) for n in SMALL], [as2d(v[n]) for n in SMALL],
        [n == "a_norm" for n in SMALL])
    for n, res in zip(SMALL, small_res):
        out[n] = [r.reshape(w[n].shape) for r in res]
    loss_out = loss_sum.reshape(())
    for n, summed in zip(TABLES, offset_sums):
        grad = _diag_rows_grad(small["by_offset"][n][0], summed)
        res = _adamw("adamw_" + n, as2d(w[n]), as2d(m[n]), as2d(v[n]), [grad])
        out[n] = [r.reshape(w[n].shape) for r in res]

    core_sums = [_sum_partials("sum_" + n, own[n], received[n], chip, after_attention) for n in BIG]
    sibling_sums = (_swap_with_sibling("swap_last", core_sums[:1])
                    + _swap_with_sibling("swap_early", core_sums[1:]))

    for n, mine, theirs in zip(BIG, core_sums, sibling_sums):
        res = _adamw("adamw_" + n, shard2d[n], m[n].reshape(shard2d[n].shape), v[n].reshape(shard2d[n].shape),
                     [mine, theirs])
        out[n] = [r.reshape(w[n].shape) for r in res]

    grads = [out[n][0] for n in ORDER]
    deltas = [out[n][1] for n in ORDER]
    new_m = [out[n][2] for n in ORDER]
    new_v = [out[n][3] for n in ORDER]
    return (loss_out, grad_x[None], *grads, *deltas, *new_m, *new_v)
```

```python
import math

import jax
import jax.numpy as jnp
import numpy as np
from jax import lax
from jax.experimental import pallas as pl
from jax.experimental.pallas import tpu as pltpu
from jax.experimental.pallas import tpu_sc as plsc

F32 = jnp.float32
BF16 = jnp.bfloat16
MESH = pl.DeviceIdType.MESH

D_MODEL = 1024
HEADS = 16
HEAD_DIM = 64
CHUNK = 64
RMS_EPS = 1e-6
SCALE = HEAD_DIM ** -0.5
A_LEFT_CHUNKS = 8
A_REL_CLIP = 256
B_LEFT_CHUNKS = 2
B_KV_HEADS = 2
B_GROUP = HEADS // B_KV_HEADS
T5_BUCKETS = 32
T5_MAX_DIST = 128
ADAM_LR = 0.001
ADAM_B1 = 0.9
ADAM_B2 = 0.999
ADAM_EPS = 1e-08
ADAM_WD = 0.01
ADAM_STEP = 10

MASKED = -1e30
LANES = 128
TQ = 128
A_PAIRS = 2
A_PAIRS_FWD = 4
KB = 128
A_KBLOCKS = A_LEFT_CHUNKS * CHUNK // KB + 1
B_KBLOCKS = B_LEFT_CHUNKS * CHUNK // KB + 1
A_WIN = A_KBLOCKS * KB
B_WIN = B_KBLOCKS * KB
TM = 512
TM_DENSE = 1024
TM_PARTS = 512
VMEM_LIMIT = 56 * 1024 * 1024

NT = (((1,), (1,)), ((), ()))
TN = (((0,), (0,)), ((), ()))
NN = (((1,), (0,)), ((), ()))


def _params(sem=None):
    return pltpu.CompilerParams(dimension_semantics=sem, vmem_limit_bytes=VMEM_LIMIT)


class _Hosted:
    def __init__(self, inputs, out_shapes, sems, first, middle, last):
        self.inputs, self.out_shapes, self.sems = list(inputs), list(out_shapes), list(sems)
        self.first, self.middle, self.last = first, middle, last


def _call(body, *, name, grid, in_specs, out_specs, out_shape, args, scratch_shapes=(), sem=None, hosted=None):
    in_specs, out_specs, out_shape = list(in_specs), list(out_specs), list(out_shape)
    scratch_shapes = list(scratch_shapes)
    if hosted is None:
        out = pl.pallas_call(
            body, name=name, grid=grid, in_specs=in_specs, out_specs=out_specs, out_shape=out_shape,
            scratch_shapes=scratch_shapes, compiler_params=_params(sem))(*args)
        return list(out), []
    n_in, n_out, n_scr = len(in_specs), len(out_shape), len(scratch_shapes)
    h_in, h_out = len(hosted.inputs), len(hosted.out_shapes)
    total = int(np.prod(grid)) if grid else 1

    def wrapped(*refs):
        ins, refs = refs[:n_in], refs[n_in:]
        h_ins, refs = refs[:h_in], refs[h_in:]
        outs, refs = refs[:n_out], refs[n_out:]
        h_outs, refs = refs[:h_out], refs[h_out:]
        scr, h_sems = refs[:n_scr], refs[n_scr:]
        step = 0
        for axis, size in enumerate(grid):
            step = step * size + pl.program_id(axis)

        if hosted.first is not None:
            @pl.when(step == 0)
            def _():
                hosted.first(h_ins, h_outs, h_sems)

        body(*ins, *outs, *scr)
        if hosted.middle is not None:
            @pl.when(step == total // 2)
            def _():
                hosted.middle(h_ins, h_outs, h_sems)

        if hosted.last is not None:
            @pl.when(step == total - 1)
            def _():
                hosted.last(h_ins, h_outs, h_sems)

    out = pl.pallas_call(
        wrapped, name=name, grid=grid, in_specs=in_specs + [ANY] * h_in, out_specs=out_specs + [ANY] * h_out,
        out_shape=out_shape + hosted.out_shapes, scratch_shapes=scratch_shapes + hosted.sems,
        compiler_params=_params(("arbitrary",) * len(grid)))(*args, *hosted.inputs)
    return list(out[:n_out]), list(out[n_out:])


def _matmul(name, a, b, *, dims, grid, a_spec, b_spec, o_spec, out_shape, out_dtype,
            also_bf16=False, zero_axis=None):
    def body(*refs):
        if zero_axis is None:
            product(*refs)
        else:
            @pl.when(pl.program_id(zero_axis) == 0)
            def _():
                refs[2][...] = jnp.zeros_like(refs[2])

            @pl.when(pl.program_id(zero_axis) > 0)
            def _():
                product(*refs)

    def product(a_ref, b_ref, o_ref, *more):
        prod = lax.dot_general(a_ref[...].astype(BF16), b_ref[...].astype(BF16), dims,
                               preferred_element_type=F32)
        o_ref[...] = prod.astype(out_dtype)
        if also_bf16:
            more[0][...] = prod.astype(BF16)

    out_specs = [o_spec]
    out_shapes = [jax.ShapeDtypeStruct(out_shape, out_dtype)]
    if also_bf16:
        out_specs.append(o_spec)
        out_shapes.append(jax.ShapeDtypeStruct(out_shape, BF16))
    out, _ = _call(body, name=name, grid=grid, in_specs=[a_spec, b_spec], out_specs=out_specs,
                   out_shape=out_shapes, args=[a, b], sem=("parallel",) * len(grid))
    return out[0] if not also_bf16 else tuple(out)


def _rms_rows(x):
    return lax.rsqrt(jnp.mean(x * x, axis=-1, keepdims=True) + RMS_EPS)


def _norm_fwd(name, x, gains):
    s, d = x.shape
    n = gains.shape[0]

    def body(x_ref, g_ref, *o_refs):
        xv = x_ref[...]
        xh = xv * _rms_rows(xv)
        for i in range(n):
            o_refs[i][...] = (xh * g_ref[i:i + 1, :]).astype(BF16)

    row = pl.BlockSpec((TM, d), lambda i: (i, 0))
    return pl.pallas_call(
        body, name=name, grid=(s // TM,),
        in_specs=[row, pl.BlockSpec((n, d), lambda i: (0, 0))],
        out_specs=[row] * n,
        out_shape=[jax.ShapeDtypeStruct((s, d), BF16)] * n,
        compiler_params=_params(("parallel",)),
    )(x, gains)


def _proj_norm_bwd(name, x, dres, gains, branches):
    s, d = x.shape
    n = len(branches)
    tm = min(TM_PARTS, s)

    def body(x_ref, r_ref, g_ref, *refs):
        ab_refs, dx_ref, dg_ref = refs[:2 * n], refs[2 * n], refs[2 * n + 1]
        i = pl.program_id(0)
        xv = x_ref[...]
        r = _rms_rows(xv)
        xh = xv * r

        @pl.when(i == 0)
        def _():
            dg_ref[...] = jnp.zeros_like(dg_ref)

        a = None
        for j in range(n):
            a_ref, b_ref = ab_refs[2 * j], ab_refs[2 * j + 1]
            dn = None
            for part in range(a_ref.shape[0]):
                term = lax.dot_general(a_ref[part], b_ref[part], NT, preferred_element_type=F32)
                dn = term if dn is None else dn + term
            t = dn * g_ref[j:j + 1, :]
            a = t if a is None else a + t
            dg_ref[j:j + 1, :] += jnp.sum(dn * xh, axis=0, keepdims=True)
        dx_ref[...] = r_ref[...] + r * (a - xh * jnp.mean(xh * a, axis=-1, keepdims=True))

    row = pl.BlockSpec((tm, d), lambda i: (i, 0))
    small = pl.BlockSpec((n, d), lambda i: (0, 0))
    ab_specs, ab_args = [], []
    for a, b in branches:
        ab_specs += [pl.BlockSpec((a.shape[0], tm, a.shape[2]), lambda i: (0, i, 0)),
                     pl.BlockSpec(b.shape, lambda i: (0, 0, 0))]
        ab_args += [a, b]
    return pl.pallas_call(
        body, name=name, grid=(s // tm,),
        in_specs=[row, row, small] + ab_specs,
        out_specs=[row, small],
        out_shape=[jax.ShapeDtypeStruct((s, d), F32), jax.ShapeDtypeStruct((n, d), F32)],
        compiler_params=_params(("arbitrary",)),
    )(x, dres, gains, *ab_args)


def _out_norms(name, u, w_out, resid, gains):
    s, d = resid.shape
    n = gains.shape[0]
    tm = min(TM_DENSE, s)

    def body(u_ref, w_ref, r_ref, g_ref, h_ref, *o_refs):
        hv = r_ref[...] + jnp.dot(u_ref[...], w_ref[...], preferred_element_type=F32)
        h_ref[...] = hv
        hh = hv * _rms_rows(hv)
        for i in range(n):
            o_refs[i][...] = (hh * g_ref[i:i + 1, :]).astype(BF16)

    row = pl.BlockSpec((tm, d), lambda i: (i, 0))
    return pl.pallas_call(
        body, name=name, grid=(s // tm,),
        in_specs=[row, pl.BlockSpec((d, d), lambda i: (0, 0)), row, pl.BlockSpec((n, d), lambda i: (0, 0))],
        out_specs=[row] * (n + 1),
        out_shape=[jax.ShapeDtypeStruct((s, d), F32)] + [jax.ShapeDtypeStruct((s, d), BF16)] * n,
        compiler_params=_params(("parallel",)),
    )(u, w_out, resid, gains)


def _out_loss_head(u, w_out, resid, target, gain):
    s, d = resid.shape
    tm = min(TM_PARTS, s)

    def body(u_ref, w_ref, r_ref, t_ref, g_ref, dh_ref, loss_ref, dg_ref):
        i = pl.program_id(0)
        hv = r_ref[...] + jnp.dot(u_ref[...], w_ref[...], preferred_element_type=F32)
        r = _rms_rows(hv)
        hh = hv * r
        g = g_ref[...]
        err = hh * g - t_ref[...]
        part = 0.5 * jnp.sum(jnp.sum(err * err, axis=-1, keepdims=True) * (1.0 / d), axis=0, keepdims=True)
        dy = err * (1.0 / d)
        a = dy * g
        dh_ref[...] = r * (a - hh * jnp.mean(hh * a, axis=-1, keepdims=True))
        dg = jnp.sum(dy * hh, axis=0, keepdims=True)

        @pl.when(i == 0)
        def _():
            loss_ref[...] = part
            dg_ref[...] = dg

        @pl.when(i > 0)
        def _():
            loss_ref[...] += part
            dg_ref[...] += dg

    row = pl.BlockSpec((tm, d), lambda i: (i, 0))
    return pl.pallas_call(
        body, name="out_b_loss_head", grid=(s // tm,),
        in_specs=[row, pl.BlockSpec((d, d), lambda i: (0, 0)), row, row, pl.BlockSpec((1, d), lambda i: (0, 0))],
        out_specs=[row, pl.BlockSpec((1, 1), lambda i: (0, 0)), pl.BlockSpec((1, d), lambda i: (0, 0))],
        out_shape=[jax.ShapeDtypeStruct((s, d), F32), jax.ShapeDtypeStruct((1, 1), F32),
                   jax.ShapeDtypeStruct((1, d), F32)],
        compiler_params=_params(("arbitrary",)),
    )(u, w_out, resid, target, gain)


def _silu_parts(g):
    sig = jax.nn.sigmoid(g)
    return g * sig, sig * (1.0 + g * (1.0 - sig))


def _lane_lo(rows):
    return lax.broadcasted_iota(jnp.int32, (rows, LANES), 1) < HEAD_DIM


def _stack_pair(x):
    lo = _lane_lo(x.shape[0])
    zero = jnp.zeros_like(x)
    return jnp.concatenate([jnp.where(lo, x, zero), jnp.where(lo, zero, x)], axis=0)


def _unstack_pair(y, w):
    return jnp.where(_lane_lo(w), y[:w], y[w:])


def _block_valid(b, left_blocks, width):
    col = lax.broadcasted_iota(jnp.int32, (1, 2 * width), 1)
    col = jnp.where(col >= width, col - width, col)
    return (col // KB + (b - left_blocks)) >= 0


def _toeplitz_tile(diag_row, width, left_chunks):
    wide = width + TQ
    rolled = pltpu.roll(jnp.broadcast_to(diag_row, (TQ, wide)), 1, 1, stride=1, stride_axis=0)
    i = lax.broadcasted_iota(jnp.int32, (TQ, width), 0) // CHUNK
    j = lax.broadcasted_iota(jnp.int32, (TQ, width), 1) // CHUNK
    dc = i + left_chunks - j
    return jnp.where((dc >= 0) & (dc <= left_chunks), rolled[:, TQ:], MASKED)


def _toeplitz_sum(tile, width):
    flip = (lax.broadcasted_iota(jnp.int32, (TQ, TQ), 0) + lax.broadcasted_iota(jnp.int32, (TQ, TQ), 1)
            == TQ - 1).astype(F32)
    reversed_rows = jnp.dot(flip, tile, precision=lax.Precision.HIGHEST, preferred_element_type=F32)
    padded = jnp.concatenate([reversed_rows, jnp.zeros((TQ, TQ), F32)], axis=1)
    rolled = pltpu.roll(padded, 0, 1, stride=1, stride_axis=0)
    return jnp.sum(rolled, axis=0, keepdims=True)


def _softmax_pair(sc, w, sink=None):
    ps, inv, lses = [], [], []
    for e in range(2):
        sh = sc[:, e * w:(e + 1) * w]
        m = jnp.max(sh, axis=-1, keepdims=True)
        if sink is not None:
            m = jnp.maximum(m, sink[e])
        ex = jnp.exp(sh - m)
        l = jnp.sum(ex, axis=-1, keepdims=True)
        if sink is not None:
            l = l + jnp.exp(sink[e] - m)
        ps.append(ex.astype(BF16))
        inv.append(1.0 / l)
        lses.append(m + jnp.log(l))
    return jnp.concatenate(ps, axis=-1), inv, lses


def _softmax_pair_bwd(sc, dp, lse, delta, w):
    ps, dss = [], []
    for e in range(2):
        p = jnp.exp(sc[:, e * w:(e + 1) * w] - lse[e])
        ps.append(p)
        dss.append(p * (dp[:, e * w:(e + 1) * w] - delta[e]))
    return jnp.concatenate(ps, axis=-1), jnp.concatenate(dss, axis=-1)


def _pair_rowsums(x, lo):
    zero = jnp.zeros_like(x)
    return (jnp.sum(jnp.where(lo, x, zero), axis=-1, keepdims=True),
            jnp.sum(jnp.where(lo, zero, x), axis=-1, keepdims=True))


def _a_qkv_specs(rows, pad, pw):
    return [pl.BlockSpec((None, TQ, pw), lambda p, b: (0, b + pad // TQ, p)),
            pl.BlockSpec((None, rows, pw), lambda p, b: (1, 0, p)),
            pl.BlockSpec((None, rows, pw), lambda p, b: (2, 0, p))]


def _window(ref, b, pad, win, lanes):
    start = pl.multiple_of(b * TQ + pad - (win - TQ), KB)
    return ref[pl.ds(start, win), lanes]


def _attn_a_fwd(zqkv, g, diag, hosted=None):
    s = g.shape[0]
    pad = zqkv.shape[1] - s
    nb = s // TQ
    left = A_KBLOCKS - 1
    pairs = A_PAIRS_FWD
    pw = pairs * LANES
    wide = A_WIN + TQ

    def body(q_ref, k_ref, v_ref, g_ref, diag_ref, o_ref, u_ref, lse_ref, bias_scr):
        b = pl.program_id(1)

        @pl.when(b == 0)
        def _():
            for hh in range(2 * pairs):
                bias_scr[hh // 2, :, (hh % 2) * A_WIN:(hh % 2 + 1) * A_WIN] = _toeplitz_tile(
                    diag_ref[hh], A_WIN, A_LEFT_CHUNKS)

        def step(first_blocks):
            lo = _lane_lo(TQ)
            for pp in range(pairs):
                ln = slice(pp * LANES, (pp + 1) * LANES)
                kcat = _stack_pair(_window(k_ref, b, pad, A_WIN, ln))
                vcat = _stack_pair(_window(v_ref, b, pad, A_WIN, ln))
                sc = lax.dot_general(q_ref[:, ln] * SCALE, kcat, NT, preferred_element_type=F32) + bias_scr[pp]
                if first_blocks:
                    sc = jnp.where(_block_valid(b, left, A_WIN), sc, MASKED)
                p, inv, lses = _softmax_pair(sc, A_WIN)
                ov = jnp.dot(p, vcat, preferred_element_type=F32) * jnp.where(lo, inv[0], inv[1])
                o_ref[:, ln] = ov
                lse_ref[pp] = jnp.where(lo, lses[0], lses[1])
                sg, _ = _silu_parts(g_ref[:, ln])
                u_ref[:, ln] = (ov * sg).astype(BF16)

        @pl.when(b < left)
        def _():
            step(True)

        @pl.when(b >= left)
        def _():
            step(False)

    tile = pl.BlockSpec((TQ, pw), lambda p, b: (b, p))
    return _call(
        body, name="attn_a_fwd", grid=(HEADS // 2 // pairs, nb),
        in_specs=_a_qkv_specs(pad + s, pad, pw) + [
            tile, pl.BlockSpec((2 * pairs, 1, wide), lambda p, b: (p, 0, 0))],
        out_specs=[tile, tile, pl.BlockSpec((pairs, TQ, LANES), lambda p, b: (p, b, 0))],
        out_shape=[jax.ShapeDtypeStruct((s, D_MODEL), F32), jax.ShapeDtypeStruct((s, D_MODEL), BF16),
                   jax.ShapeDtypeStruct((HEADS // 2, s, LANES), F32)],
        scratch_shapes=[pltpu.VMEM((pairs, TQ, 2 * A_WIN), F32)],
        sem=("parallel", "arbitrary"), hosted=hosted,
        args=(zqkv, zqkv, zqkv, g, diag))


def _attn_a_bwd(zqkv, g, o, du, lse, diag, hosted=None):
    s = g.shape[0]
    pad = zqkv.shape[1] - s
    nb = s // TQ
    left = A_KBLOCKS - 1
    pw = A_PAIRS * LANES
    wide = A_WIN + TQ

    def body(q_ref, k_ref, v_ref, g_ref, o_ref, du_ref, lse_ref, diag_ref, dz_ref, ddiag_ref,
             bias_scr, dbias_acc, dk_acc, dv_acc):
        b = pl.program_id(1)

        @pl.when(b == 0)
        def _():
            for hh in range(2 * A_PAIRS):
                bias_scr[hh // 2, :, (hh % 2) * A_WIN:(hh % 2 + 1) * A_WIN] = _toeplitz_tile(
                    diag_ref[hh], A_WIN, A_LEFT_CHUNKS)
            dbias_acc[...] = jnp.zeros_like(dbias_acc)
            dk_acc[...] = jnp.zeros_like(dk_acc)
            dv_acc[...] = jnp.zeros_like(dv_acc)

        def step(first_blocks):
            lo = _lane_lo(TQ)
            rows = pl.ds(pl.multiple_of(b * TQ, TQ), TQ)
            sg, dsg = _silu_parts(g_ref[...])
            duv = du_ref[...]
            ov = o_ref[...]
            do = duv * sg
            dz_ref[3, rows, :] = (duv * ov * dsg).astype(BF16)
            do_o = do * ov
            do_bf = do.astype(BF16)
            for pp in range(A_PAIRS):
                ln = slice(pp * LANES, (pp + 1) * LANES)
                q = q_ref[:, ln] * SCALE
                kcat = _stack_pair(_window(k_ref, b, pad, A_WIN, ln))
                vcat = _stack_pair(_window(v_ref, b, pad, A_WIN, ln))
                sc = lax.dot_general(q, kcat, NT, preferred_element_type=F32) + bias_scr[pp]
                if first_blocks:
                    sc = jnp.where(_block_valid(b, left, A_WIN), sc, MASKED)
                lse_t = lse_ref[pp]
                dp = lax.dot_general(do_bf[:, ln], vcat, NT, preferred_element_type=F32)
                p, ds = _softmax_pair_bwd(sc, dp, (lse_t[:, 0:1], lse_t[:, HEAD_DIM:HEAD_DIM + 1]),
                                          _pair_rowsums(do_o[:, ln], lo), A_WIN)
                dbias_acc[pp] += ds
                dsb = ds.astype(BF16)
                dz_ref[0, rows, ln] = (jnp.dot(dsb, kcat, preferred_element_type=F32) * SCALE).astype(BF16)
                pb = p.astype(BF16)
                dob = do_bf[:, ln]
                dkt = jnp.concatenate([
                    lax.dot_general(q[:, e * HEAD_DIM:(e + 1) * HEAD_DIM], dsb[:, e * A_WIN:(e + 1) * A_WIN], TN,
                                    preferred_element_type=F32) for e in range(2)], axis=0)
                dvt = jnp.concatenate([
                    lax.dot_general(dob[:, e * HEAD_DIM:(e + 1) * HEAD_DIM], pb[:, e * A_WIN:(e + 1) * A_WIN], TN,
                                    preferred_element_type=F32) for e in range(2)], axis=0)
                for t in range(A_KBLOCKS):
                    blk = b + (pad // KB - left + t)
                    dk_acc[blk, ln, :] += dkt[:, t * KB:(t + 1) * KB]
                    dv_acc[blk, ln, :] += dvt[:, t * KB:(t + 1) * KB]

        @pl.when(b < left)
        def _():
            step(True)

        @pl.when(b >= left)
        def _():
            step(False)

        @pl.when(b == nb - 1)
        def _():
            for kb in range(s // KB):
                dz_ref[1, kb * KB:(kb + 1) * KB, :] = dk_acc[pad // KB + kb].T.astype(BF16)
                dz_ref[2, kb * KB:(kb + 1) * KB, :] = dv_acc[pad // KB + kb].T.astype(BF16)
            for hh in range(2 * A_PAIRS):
                ddiag_ref[hh] = _toeplitz_sum(
                    dbias_acc[hh // 2, :, (hh % 2) * A_WIN:(hh % 2 + 1) * A_WIN], A_WIN)

    tile = pl.BlockSpec((TQ, pw), lambda p, b: (b, p))
    diag_spec = pl.BlockSpec((2 * A_PAIRS, 1, wide), lambda p, b: (p, 0, 0))
    return _call(
        body, name="attn_a_bwd", grid=(HEADS // 2 // A_PAIRS, nb),
        in_specs=_a_qkv_specs(pad + s, pad, pw) + [
            tile, tile, tile, pl.BlockSpec((A_PAIRS, TQ, LANES), lambda p, b: (p, b, 0)), diag_spec],
        out_specs=[pl.BlockSpec((4, s, pw), lambda p, b: (0, 0, p)), diag_spec],
        out_shape=[jax.ShapeDtypeStruct((4, s, D_MODEL), BF16),
                   jax.ShapeDtypeStruct((HEADS, 1, wide), F32)],
        scratch_shapes=[pltpu.VMEM((A_PAIRS, TQ, 2 * A_WIN), F32), pltpu.VMEM((A_PAIRS, TQ, 2 * A_WIN), F32),
                        pltpu.VMEM(((pad + s) // KB, pw, KB), F32), pltpu.VMEM(((pad + s) // KB, pw, KB), F32)],
        sem=("parallel", "arbitrary"), hosted=hosted,
        args=(zqkv, zqkv, zqkv, g, o, du, lse, diag))


B_STACK = B_GROUP // 2
B_KVX = 4 * LANES
B_ROWS = B_STACK * TQ
B_WIDE = B_WIN + TQ


def _b_head_place(h):
    return h // B_GROUP, (h % B_GROUP) // 2, h % 2


def _toeplitz_tile_t(base_row, width, left_chunks):
    wide = width + TQ
    rolled = pltpu.roll(jnp.broadcast_to(base_row, (width, wide)), 0, 1, stride=1, stride_axis=0)
    j = lax.broadcasted_iota(jnp.int32, (width, TQ), 0) // CHUNK
    i = lax.broadcasted_iota(jnp.int32, (width, TQ), 1) // CHUNK
    dc = i + left_chunks - j
    return jnp.where((dc >= 0) & (dc <= left_chunks), rolled[:, :TQ], MASKED)


def _toeplitz_sum_t(tile_t, width):
    flip = (lax.broadcasted_iota(jnp.int32, (width, width), 0) + lax.broadcasted_iota(jnp.int32, (width, width), 1)
            == width - 1).astype(F32)
    reversed_rows = jnp.dot(flip, tile_t, precision=lax.Precision.HIGHEST, preferred_element_type=F32)
    padded = jnp.concatenate([reversed_rows, jnp.zeros((width, width), F32)], axis=1)
    rolled = pltpu.roll(padded, 0, 1, stride=1, stride_axis=0)
    return jnp.sum(rolled, axis=0, keepdims=True)


def _b_build_bias(base_ref, bias_scr):
    for h in range(HEADS):
        gi, pr, e = _b_head_place(h)
        bias_scr[gi, e * B_WIN:(e + 1) * B_WIN, pr * TQ:(pr + 1) * TQ] = _toeplitz_tile_t(
            base_ref[h], B_WIN, B_LEFT_CHUNKS)


def _b_stack(x, gi):
    return jnp.concatenate(
        [x[:, (B_STACK * gi + pr) * LANES:(B_STACK * gi + pr + 1) * LANES] for pr in range(B_STACK)], axis=0)


def _b_sink_rows(sink_ref, gi):
    block = lax.broadcasted_iota(jnp.int32, (1, B_ROWS), 1) // TQ
    rows = []
    for e in range(2):
        row = jnp.zeros((1, B_ROWS), F32)
        for pr in range(B_STACK):
            h = B_GROUP * gi + 2 * pr + e
            row = jnp.where(block == pr, sink_ref[0:1, h:h + 1], row)
        rows.append(row)
    return rows


def _b_scores_t(q_ref, kvv, bias_scr, gi, b, left, first_blocks):
    kcat = _stack_pair(kvv[:, gi * LANES:(gi + 1) * LANES])
    vcat = _stack_pair(kvv[:, (B_KV_HEADS + gi) * LANES:(B_KV_HEADS + gi + 1) * LANES])
    qs = _b_stack(q_ref, gi) * SCALE
    sc = lax.dot_general(kcat, qs, NT, preferred_element_type=F32) + bias_scr[gi]
    if first_blocks:
        row = lax.broadcasted_iota(jnp.int32, (2 * B_WIN, 1), 0)
        row = jnp.where(row >= B_WIN, row - B_WIN, row)
        sc = jnp.where((row // KB + (b - left)) >= 0, sc, MASKED)
    return kcat, vcat, qs, sc


def _attn_b_fwd(qb, kvx, gate, base, sinks):
    s = qb.shape[0]
    pad = kvx.shape[0] - s
    nb = s // TQ
    left = B_KBLOCKS - 1

    def body(q_ref, kv_ref, g_ref, base_ref, sink_ref, o_ref, u_ref, lse_ref, bias_scr):
        b = pl.program_id(0)

        @pl.when(b == 0)
        def _():
            _b_build_bias(base_ref, bias_scr)

        def step(first_blocks):
            kvv = _window(kv_ref, b, pad, B_WIN, slice(None))
            upper = lax.broadcasted_iota(jnp.int32, (LANES, B_ROWS), 0) < HEAD_DIM
            lse_rows = []
            for gi in range(B_KV_HEADS):
                kcat, vcat, qs, sc = _b_scores_t(q_ref, kvv, bias_scr, gi, b, left, first_blocks)
                sink = _b_sink_rows(sink_ref, gi)
                ps, inv = [], []
                for e in range(2):
                    sh = sc[e * B_WIN:(e + 1) * B_WIN]
                    m = jnp.maximum(jnp.max(sh, axis=0, keepdims=True), sink[e])
                    ex = jnp.exp(sh - m)
                    l = jnp.sum(ex, axis=0, keepdims=True) + jnp.exp(sink[e] - m)
                    ps.append(ex.astype(BF16))
                    inv.append(1.0 / l)
                    lse_rows.append(m + jnp.log(l))
                pt = jnp.concatenate(ps, axis=0)
                ot = lax.dot_general(vcat, pt, TN, preferred_element_type=F32) * jnp.where(upper, inv[0], inv[1])
                ov = ot.T
                for pr in range(B_STACK):
                    pair = B_STACK * gi + pr
                    o_ref[:, pair * LANES:(pair + 1) * LANES] = ov[pr * TQ:(pr + 1) * TQ]
            lse_ref[0] = jnp.concatenate(lse_rows + [jnp.zeros((8 - len(lse_rows), B_ROWS), F32)], axis=0)
            sg, _ = _silu_parts(g_ref[...])
            u_ref[...] = (o_ref[...] * sg).astype(BF16)

        @pl.when(b < left)
        def _():
            step(True)

        @pl.when(b >= left)
        def _():
            step(False)

    row = pl.BlockSpec((TQ, D_MODEL), lambda b: (b, 0))
    return pl.pallas_call(
        body, name="attn_b_fwd", grid=(nb,),
        in_specs=[row, pl.BlockSpec((pad + s, B_KVX), lambda b: (0, 0)), row,
                  pl.BlockSpec((HEADS, 1, B_WIDE), lambda b: (0, 0, 0)), pl.BlockSpec((1, HEADS), lambda b: (0, 0))],
        out_specs=[row, row, pl.BlockSpec((1, 8, B_ROWS), lambda b: (b, 0, 0))],
        out_shape=[jax.ShapeDtypeStruct((s, D_MODEL), F32), jax.ShapeDtypeStruct((s, D_MODEL), BF16),
                   jax.ShapeDtypeStruct((nb, 8, B_ROWS), F32)],
        scratch_shapes=[pltpu.VMEM((B_KV_HEADS, 2 * B_WIN, B_ROWS), F32)],
        compiler_params=_params(("arbitrary",)),
    )(qb, kvx, gate, base, sinks)


def _attn_b_bwd(qb, kvx, gate, o, du, lse, base, sinks):
    s = qb.shape[0]
    pad = kvx.shape[0] - s
    nb = s // TQ
    left = B_KBLOCKS - 1
    half = D_MODEL // 2

    def body(q_ref, kv_ref, g_ref, o_ref, du_ref, lse_ref, base_ref, sink_ref, dz_ref, dkv_ref, dsum_ref,
             dsink_ref, bias_scr, dbias_acc, dkv_acc, dsink_acc):
        b = pl.program_id(0)

        @pl.when(b == 0)
        def _():
            _b_build_bias(base_ref, bias_scr)
            dbias_acc[...] = jnp.zeros_like(dbias_acc)
            dkv_acc[...] = jnp.zeros_like(dkv_acc)
            dsink_acc[...] = jnp.zeros_like(dsink_acc)

        def step(first_blocks):
            kvv = _window(kv_ref, b, pad, B_WIN, slice(None))
            sg, dsg = _silu_parts(g_ref[...])
            duv = du_ref[...]
            ov = o_ref[...]
            do = duv * sg
            dgate = (duv * ov * dsg).astype(BF16)
            dz_ref[2] = dgate[:, :half]
            dz_ref[3] = dgate[:, half:]
            do_o = do * ov
            do_bf = do.astype(BF16)
            lse_all = lse_ref[0]
            dsink_rows = []
            for gi in range(B_KV_HEADS):
                kcat, vcat, qs, sc = _b_scores_t(q_ref, kvv, bias_scr, gi, b, left, first_blocks)
                dos = _b_stack(do_bf, gi)
                doo_t = _b_stack(do_o, gi).T
                delta = (jnp.sum(doo_t[:HEAD_DIM], axis=0, keepdims=True),
                         jnp.sum(doo_t[HEAD_DIM:], axis=0, keepdims=True))
                sink = _b_sink_rows(sink_ref, gi)
                dp = lax.dot_general(vcat, dos, NT, preferred_element_type=F32)
                ps, dss = [], []
                for e in range(2):
                    lse_e = lse_all[2 * gi + e:2 * gi + e + 1]
                    delta_e = delta[e]
                    p = jnp.exp(sc[e * B_WIN:(e + 1) * B_WIN] - lse_e)
                    ps.append(p.astype(BF16))
                    dss.append(p * (dp[e * B_WIN:(e + 1) * B_WIN] - delta_e))
                    dsink_rows.append(-jnp.exp(sink[e] - lse_e) * delta_e)
                ds = jnp.concatenate(dss, axis=0)
                dbias_acc[gi] += ds
                dsb = ds.astype(BF16)
                dq = (lax.dot_general(kcat, dsb, TN, preferred_element_type=F32) * SCALE).T.astype(BF16)
                for pr in range(B_STACK):
                    dz_ref[gi, :, pr * LANES:(pr + 1) * LANES] = dq[pr * TQ:(pr + 1) * TQ]
                dk = _unstack_pair(jnp.dot(dsb, qs, preferred_element_type=F32), B_WIN)
                dv = _unstack_pair(jnp.dot(jnp.concatenate(ps, axis=0), dos, preferred_element_type=F32), B_WIN)
                krows = pl.ds(pl.multiple_of(b * TQ + pad - (B_WIN - TQ), KB), B_WIN)
                dkv_acc[krows, gi * LANES:(gi + 1) * LANES] += dk
                dkv_acc[krows, (B_KV_HEADS + gi) * LANES:(B_KV_HEADS + gi + 1) * LANES] += dv
            dsink_acc[...] += jnp.concatenate(
                dsink_rows + [jnp.zeros((8 - len(dsink_rows), B_ROWS), F32)], axis=0)

        @pl.when(b < left)
        def _():
            step(True)

        @pl.when(b >= left)
        def _():
            step(False)

        @pl.when(b == nb - 1)
        def _():
            lo_s = _lane_lo(s)
            for which in range(2):
                folded = []
                for gi in range(B_KV_HEADS):
                    part = dkv_acc[pad:pad + s, (which * B_KV_HEADS + gi) * LANES:(which * B_KV_HEADS + gi + 1) * LANES]
                    folded.append(part + pltpu.roll(part, HEAD_DIM, 1))
                dkv_ref[:, which * LANES:(which + 1) * LANES] = jnp.where(lo_s, folded[0], folded[1]).astype(BF16)
            lane8 = lax.broadcasted_iota(jnp.int32, dsink_ref.shape, 1)
            tot = jnp.zeros(dsink_ref.shape, F32)
            for h in range(HEADS):
                gi, pr, e = _b_head_place(h)
                dsum_ref[h] = _toeplitz_sum_t(
                    dbias_acc[gi, e * B_WIN:(e + 1) * B_WIN, pr * TQ:(pr + 1) * TQ], B_WIN)
                per_query = dsink_acc[2 * gi + e:2 * gi + e + 1, pr * TQ:(pr + 1) * TQ]
                tot = jnp.where(lane8 == h, jnp.sum(per_query, axis=1, keepdims=True), tot)
            dsink_ref[...] = tot

    row = pl.BlockSpec((TQ, D_MODEL), lambda b: (b, 0))
    base_spec = pl.BlockSpec((HEADS, 1, B_WIDE), lambda b: (0, 0, 0))
    return pl.pallas_call(
        body, name="attn_b_bwd", grid=(nb,),
        in_specs=[row, pl.BlockSpec((pad + s, B_KVX), lambda b: (0, 0)), row, row, row,
                  pl.BlockSpec((1, 8, B_ROWS), lambda b: (b, 0, 0)), base_spec,
                  pl.BlockSpec((1, HEADS), lambda b: (0, 0))],
        out_specs=[pl.BlockSpec((4, TQ, half), lambda b: (0, b, 0)),
                   pl.BlockSpec((s, 2 * LANES), lambda b: (0, 0)), base_spec,
                   pl.BlockSpec((8, LANES), lambda b: (0, 0))],
        out_shape=[jax.ShapeDtypeStruct((4, s, half), BF16), jax.ShapeDtypeStruct((s, 2 * LANES), BF16),
                   jax.ShapeDtypeStruct((HEADS, 1, B_WIDE), F32), jax.ShapeDtypeStruct((8, LANES), F32)],
        scratch_shapes=[pltpu.VMEM((B_KV_HEADS, 2 * B_WIN, B_ROWS), F32),
                        pltpu.VMEM((B_KV_HEADS, 2 * B_WIN, B_ROWS), F32),
                        pltpu.VMEM((pad + s, B_KVX), F32), pltpu.VMEM((8, B_ROWS), F32)],
        compiler_params=_params(("arbitrary",)),
    )(qb, kvx, gate, o, du, lse, base, sinks)


def _t5_bucket(rel):
    nb = T5_BUCKETS // 2
    max_exact = nb // 2
    ret = jnp.where(rel > 0, nb, 0)
    n = jnp.abs(rel)
    nf = jnp.maximum(n, 1).astype(jnp.float32)
    large = max_exact + (jnp.log(nf / max_exact) / math.log(T5_MAX_DIST / max_exact)
                         * (nb - max_exact)).astype(jnp.int32)
    large = jnp.minimum(large, nb - 1)
    return ret + jnp.where(n < max_exact, n, large)


def _a_offset_onehot():
    c = np.arange(A_WIN + TQ)
    dist = A_LEFT_CHUNKS * CHUNK + TQ - 1 - c
    idx = np.clip(dist, -A_REL_CLIP, A_REL_CLIP) + A_REL_CLIP
    onehot = np.zeros((A_WIN + TQ, 2 * A_REL_CLIP + 1), np.float32)
    onehot[c, idx] = 1.0
    return jnp.asarray(onehot)


def _b_offset_onehot():
    c = jnp.arange(B_WIN + TQ, dtype=jnp.int32)
    rel = c - (TQ - 1) - B_LEFT_CHUNKS * CHUNK
    return (_t5_bucket(rel)[:, None] == jnp.arange(T5_BUCKETS)[None, :]).astype(F32)


def _diag_rows(onehot, table):
    rows = jnp.dot(onehot, table.astype(F32), precision=lax.Precision.HIGHEST)
    return rows.T.reshape(HEADS, 1, onehot.shape[0])


def _diag_rows_grad(onehot, ddiag):
    return jnp.dot(ddiag.reshape(HEADS, onehot.shape[0]), onehot, precision=lax.Precision.HIGHEST)


def _position():
    x, y, c = lax.axis_index("x"), lax.axis_index("y"), lax.axis_index("c")
    chips = [(1 - x, y), (x, 1 - y), (1 - x, 1 - y)]
    return x, y, c, chips


ANY = pl.BlockSpec(memory_space=pl.ANY)


def _allgather_routed(shards):
    n = len(shards)

    def piece(block_ref, t, c, quarter=None):
        half = shards[t].shape[0] // 2
        if quarter is None:
            return block_ref.at[pl.ds(c * half, half)]
        return block_ref.at[pl.ds(c * half + quarter * (half // 2), half // 2)]

    def copies(kind, ins, outs, sems):
        ici_send, ici_recv, pass_send, pass_recv, local_sems = sems
        x, y, c, chips = _position()
        mine = 2 * x + y
        if kind == "local":
            return [pltpu.make_async_copy(ins[t], outs[t].at[mine], local_sems.at[t]) for t in range(n)]
        ids = [2 * chip[0] + chip[1] for chip in chips]
        made = []
        for t in range(n):
            def ici(k, to):
                return dict(send_sem=ici_send.at[4 * t + k], recv_sem=ici_recv.at[4 * t + k],
                            device_id=(chips[to][0], chips[to][1], c), device_id_type=MESH)

            def d2d(k):
                return dict(send_sem=pass_send.at[4 * t + k], recv_sem=pass_recv.at[4 * t + k],
                            device_id=(x, y, 1 - c), device_id_type=MESH)

            def same(ref, where):
                return pltpu.make_async_remote_copy(src_ref=ref, dst_ref=ref, **where)

            if kind == "send":
                for k in range(2):
                    made.append(pltpu.make_async_remote_copy(
                        src_ref=piece(ins[t], t, c), dst_ref=piece(outs[t].at[mine], t, c), **ici(k, k)))
            elif kind == "landed":
                made += [same(piece(outs[t].at[ids[k]], t, c), ici(k, k)) for k in range(2)]
            elif kind == "forward":
                made.append(same(piece(outs[t].at[ids[0]], t, c, 0), ici(2, 1)))
                made.append(same(piece(outs[t].at[ids[1]], t, c, 1), ici(3, 0)))
            elif kind == "arrived":
                made.append(same(piece(outs[t].at[ids[2]], t, c, 0), ici(2, 1)))
                made.append(same(piece(outs[t].at[ids[2]], t, c, 1), ici(3, 0)))
            else:
                core = 1 - c if kind == "passed" else c
                if kind in ("pass halves", "passed"):
                    made += [same(piece(outs[t].at[ids[k]], t, core), d2d(k)) for k in range(2)]
                if kind in ("pass quarters", "passed"):
                    made += [same(piece(outs[t].at[ids[2]], t, core, k), d2d(2 + k)) for k in range(2)]
        return made

    def first(ins, outs, sems):
        for cp in copies("local", ins, outs, sems) + copies("send", ins, outs, sems):
            cp.start()

    def middle(ins, outs, sems):
        for got, onward, near in zip(copies("landed", ins, outs, sems), copies("forward", ins, outs, sems),
                                     copies("pass halves", ins, outs, sems)):
            got.wait_recv()
            near.start()
            onward.start()

    def last(ins, outs, sems):
        quarters = copies("pass quarters", ins, outs, sems)
        for got, near in zip(copies("arrived", ins, outs, sems), quarters):
            got.wait_recv()
            near.start()
        for cp in copies("passed", ins, outs, sems):
            cp.wait_recv()
        for cp in (copies("send", ins, outs, sems) + copies("forward", ins, outs, sems)
                   + copies("pass halves", ins, outs, sems) + quarters):
            cp.wait_send()
        for cp in copies("local", ins, outs, sems):
            cp.wait()

    return _Hosted(shards, [jax.ShapeDtypeStruct((4,) + w.shape, w.dtype) for w in shards],
                   [pltpu.SemaphoreType.DMA((4 * n,))] * 4 + [pltpu.SemaphoreType.DMA((n,))],
                   first, middle, last)


def _scatter_hosted(grads):
    n = len(grads)

    def copies(ins, outs, sems):
        send_sems, recv_sems = sems
        x, y, c, chips = _position()
        return [pltpu.make_async_remote_copy(
            src_ref=ins[t].at[2 * chip[0] + chip[1]], dst_ref=outs[t].at[j],
            send_sem=send_sems.at[3 * t + j], recv_sem=recv_sems.at[3 * t + j],
            device_id=(chip[0], chip[1], c), device_id_type=MESH)
            for t in range(n) for j, chip in enumerate(chips)]

    def first(ins, outs, sems):
        for cp in copies(ins, outs, sems):
            cp.start()

    def last(ins, outs, sems):
        for cp in copies(ins, outs, sems):
            cp.wait()

    return _Hosted(grads, [jax.ShapeDtypeStruct((3,) + g.shape[1:], g.dtype) for g in grads],
                   [pltpu.SemaphoreType.DMA((3 * n,))] * 2, first, None, last)


GATHER_PEERS = "x and y neighbours (same core) and the sibling core"
SCATTER_PEERS = "the same core of the three other chips"
EVERYONE = "the seven other devices"


def _run_on_sequencer(name, hosted, peers, collective_id):
    ins = [jax.new_ref(a, memory_space=pltpu.MemorySpace.HBM) for a in hosted.inputs]
    outs = [jax.empty_ref(shape, memory_space=pltpu.MemorySpace.HBM) for shape in hosted.out_shapes]

    @pl.kernel(mesh=plsc.ScalarSubcoreMesh(axis_name="sequencer", num_cores=1), name=name,
               scratch_types=tuple(hosted.sems), compiler_params=pltpu.CompilerParams(collective_id=collective_id))
    def launch(*sems):
        x, y, c, chips = _position()
        if peers == GATHER_PEERS:
            devices = [(chip[0], chip[1], c) for chip in chips[:2]] + [(x, y, 1 - c)]
        elif peers == SCATTER_PEERS:
            devices = [(chip[0], chip[1], c) for chip in chips]
        else:
            devices = [(x ^ (k >> 2), y ^ ((k >> 1) & 1), c ^ (k & 1)) for k in range(1, 8)]
        barrier = pltpu.get_barrier_semaphore()
        for device in devices:
            pl.semaphore_signal(barrier, inc=1, device_id=device, device_id_type=MESH)
        pl.semaphore_wait(barrier, len(devices))
        hosted.first(ins, outs, sems)
        if hosted.middle is not None:
            hosted.middle(ins, outs, sems)
        hosted.last(ins, outs, sems)

    launch()
    return [o[...] for o in outs]


def _gather_gain(shard):
    def body(in_ref, out_ref, send_sems, recv_sems):
        x, y, c, chips = _position()
        out_ref[2 * x + y] = in_ref[...]
        sends = [pltpu.make_async_remote_copy(
            src_ref=in_ref, dst_ref=out_ref.at[2 * x + y], send_sem=send_sems.at[j], recv_sem=recv_sems.at[j],
            device_id=(chip[0], chip[1], c), device_id_type=MESH) for j, chip in enumerate(chips)]
        for cp in sends:
            cp.start()
        for j, chip in enumerate(chips):
            pltpu.make_async_remote_copy(
                src_ref=in_ref, dst_ref=out_ref.at[2 * chip[0] + chip[1]], send_sem=send_sems.at[j],
                recv_sem=recv_sems.at[j], device_id=(chip[0], chip[1], c), device_id_type=MESH).wait_recv()
        for cp in sends:
            cp.wait_send()

    vmem = pl.BlockSpec(memory_space=pltpu.VMEM)
    return pl.pallas_call(
        body, name="gather_gain", in_specs=[vmem], out_specs=vmem,
        out_shape=jax.ShapeDtypeStruct((4,) + shard.shape, shard.dtype),
        scratch_shapes=[pltpu.SemaphoreType.DMA((3,))] * 2,
    )(shard)


def _swap_with_sibling(name, blocks):
    n = len(blocks)

    def body(*refs):
        ins, outs = refs[:n], refs[n:2 * n]
        send_sems, recv_sems = refs[2 * n:]
        x, y, c, _ = _position()
        sends = [pltpu.make_async_remote_copy(
            src_ref=ins[t], dst_ref=outs[t], send_sem=send_sems.at[t], recv_sem=recv_sems.at[t],
            device_id=(x, y, 1 - c), device_id_type=MESH) for t in range(n)]
        for cp in sends:
            cp.start()
        for cp in sends:
            cp.wait()

    return pl.pallas_call(
        body, name=name,
        in_specs=[ANY] * n, out_specs=[ANY] * n,
        out_shape=[jax.ShapeDtypeStruct(b.shape, b.dtype) for b in blocks],
        scratch_shapes=[pltpu.SemaphoreType.DMA((n,))] * 2,
    )(*blocks)


def _everyone_hosted(terms):
    nt = len(terms)

    def copies(kind, ins, outs, sems):
        send_sems, recv_sems, local_sems = sems
        x, y, c, _ = _position()
        me = 4 * x + 2 * y + c
        if kind == "local":
            return [pltpu.make_async_copy(ins[t], outs[t].at[me], local_sems.at[t]) for t in range(nt)]
        made = []
        for t in range(nt):
            for k in range(1, 8):
                peer = (x ^ (k >> 2), y ^ ((k >> 1) & 1), c ^ (k & 1))
                slot = me if kind == "send" else me ^ k
                made.append(pltpu.make_async_remote_copy(
                    src_ref=ins[t], dst_ref=outs[t].at[slot], send_sem=send_sems.at[7 * t + k - 1],
                    recv_sem=recv_sems.at[7 * t + k - 1], device_id=peer, device_id_type=MESH))
        return made

    def first(ins, outs, sems):
        for cp in copies("local", ins, outs, sems) + copies("send", ins, outs, sems):
            cp.start()

    def last(ins, outs, sems):
        for cp in copies("landed", ins, outs, sems):
            cp.wait_recv()
        for cp in copies("send", ins, outs, sems):
            cp.wait_send()
        for cp in copies("local", ins, outs, sems):
            cp.wait()

    return _Hosted(terms, [jax.ShapeDtypeStruct((8,) + a.shape, F32) for a in terms],
                   [pltpu.SemaphoreType.DMA((7 * nt,))] * 2 + [pltpu.SemaphoreType.DMA((nt,))], first, None, last)


def _small_step(partials, extras, ws, ms, vs, shard_of):
    n = len(partials)
    terms = list(partials) + list(extras)
    nt = len(terms)
    rows = [t for t in range(nt) if terms[t].shape[0] == 1]
    mats = [t for t in range(nt) if terms[t].shape[0] != 1]
    row_block = (8, max(terms[t].shape[1] for t in rows))
    assert len(rows) <= row_block[0]
    vmem = pl.BlockSpec(memory_space=pltpu.VMEM)

    def pack(*refs):
        packed = refs[-1]
        packed[...] = jnp.zeros_like(packed)
        for i, t in enumerate(rows):
            packed[i:i + 1, 0:terms[t].shape[1]] = refs[i][...]

    packed = pl.pallas_call(pack, name="small_pack", in_specs=[vmem] * len(rows), out_specs=vmem,
                            out_shape=jax.ShapeDtypeStruct(row_block, F32))(*[terms[t] for t in rows])
    slots = _run_on_sequencer("allgather_small", _everyone_hosted([packed] + [terms[t] for t in mats]),
                              EVERYONE, 2)

    def body(*refs):
        slot_refs, refs = refs[:len(slots)], refs[len(slots):]
        w_refs, refs = refs[:n], refs[n:]
        m_refs, refs = refs[:n], refs[n:]
        v_refs, outs = refs[:n], refs[n:]
        sums = []
        for ref in slot_refs:
            g = ref[0]
            for dev in range(1, 8):
                g = g + ref[dev]
            sums.append(g)
        chip = 2 * lax.axis_index("x") + lax.axis_index("y")
        for t in range(nt):
            if t in rows:
                i = rows.index(t)
                g = sums[0][i:i + 1, 0:terms[t].shape[1]]
            else:
                g = sums[1 + mats.index(t)]
            if t >= n:
                outs[4 * n + t - n][...] = g
                continue
            if shard_of[t]:
                width = ws[t].shape[-1]
                mine = jnp.zeros(ws[t].shape, F32)
                for s in range(4):
                    mine = jnp.where(chip == s, g[:, s * width:(s + 1) * width], mine)
                g = mine
            delta, mn, vn = _adamw_math(w_refs[t][...], g, m_refs[t][...], v_refs[t][...])
            outs[4 * t][...] = g
            outs[4 * t + 1][...] = delta
            outs[4 * t + 2][...] = mn
            outs[4 * t + 3][...] = vn

    out_shapes = []
    for t in range(n):
        out_shapes += [jax.ShapeDtypeStruct(ws[t].shape, F32)] * 4
    out_shapes += [jax.ShapeDtypeStruct(a.shape, F32) for a in extras]
    res = pl.pallas_call(
        body, name="small_step",
        in_specs=[vmem] * (len(slots) + 3 * n), out_specs=[vmem] * len(out_shapes), out_shape=out_shapes,
    )(*slots, *ws, *ms, *vs)
    return [res[4 * t:4 * t + 4] for t in range(n)], res[4 * n:4 * n + nt - n]


def _adamw_math(w, g, m, v):
    m = ADAM_B1 * m + (1.0 - ADAM_B1) * g
    v = ADAM_B2 * v + (1.0 - ADAM_B2) * (g * g)
    m_hat = m / (1.0 - ADAM_B1 ** ADAM_STEP)
    v_hat = v / (1.0 - ADAM_B2 ** ADAM_STEP)
    delta = -ADAM_LR * (m_hat / (jnp.sqrt(v_hat) + ADAM_EPS) + ADAM_WD * w)
    return delta, m, v


def _row_tile(rows):
    return 256 if rows % 256 == 0 else rows


def _sum_partials(name, own, recv, chip, after):
    rows, cols = own.shape[1:]
    tr = _row_tile(rows)

    def body(chip_ref, own_ref, recv_ref, after_ref, o_ref):
        acc = own_ref[...]
        for j in range(3):
            acc = acc + recv_ref[j].astype(F32)
        o_ref[...] = acc

    return pl.pallas_call(
        body, name=name,
        grid_spec=pltpu.PrefetchScalarGridSpec(
            num_scalar_prefetch=1, grid=(rows // tr,),
            in_specs=[pl.BlockSpec((None, tr, cols), lambda i, chip_ref: (chip_ref[0], i, 0)),
                      pl.BlockSpec((3, tr, cols), lambda i, chip_ref: (0, i, 0)), ANY],
            out_specs=pl.BlockSpec((tr, cols), lambda i, chip_ref: (i, 0))),
        out_shape=jax.ShapeDtypeStruct((rows, cols), F32),
        compiler_params=_params(("parallel",)),
    )(chip.reshape(1).astype(jnp.int32), own, recv, after)


def _adamw(name, w, m, v, g_parts):
    rows, cols = w.shape
    tr = _row_tile(rows)
    n = len(g_parts)

    def body(w_ref, m_ref, v_ref, *refs):
        g_refs = refs[:n]
        go_ref, d_ref, mo_ref, vo_ref = refs[n:]
        g = g_refs[0][...]
        for r in g_refs[1:]:
            g = g + r[...]
        delta, mn, vn = _adamw_math(w_ref[...], g, m_ref[...], v_ref[...])
        go_ref[...] = g
        d_ref[...] = delta
        mo_ref[...] = mn
        vo_ref[...] = vn

    spec = pl.BlockSpec((tr, cols), lambda i: (i, 0))
    return pl.pallas_call(
        body, name=name, grid=(rows // tr,),
        in_specs=[spec] * (3 + n), out_specs=[spec] * 4,
        out_shape=[jax.ShapeDtypeStruct((rows, cols), F32)] * 4,
        compiler_params=_params(("parallel",)),
    )(w, m, v, *g_parts)


def _local_step(x, target, ga, wa_in, rel_bias, later_shards, gk, t5, gb, sinks, gf):
    s, d = x.shape
    tm = min(TM_DENSE, s)
    nt = s // tm
    half = d // 2
    row = pl.BlockSpec((tm, d), lambda i: (i, 0))
    whole = lambda shape: pl.BlockSpec(shape, lambda *_: (0,) * len(shape))

    n1, = _norm_fwd("norm_a", x, ga)
    zqkv = _matmul("proj_a_qkv", n1, wa_in, dims=NN, grid=(3, nt + 1), zero_axis=1,
                   a_spec=pl.BlockSpec((tm, d), lambda j, i: (jnp.maximum(i - 1, 0), 0)),
                   b_spec=pl.BlockSpec((None, d, d), lambda j, i: (j, 0, 0)),
                   o_spec=pl.BlockSpec((None, tm, d), lambda j, i: (j, i, 0)),
                   out_shape=(3, tm + s, d), out_dtype=BF16)
    gate_a = _matmul("proj_a_gate", n1, wa_in, dims=NN, grid=(nt,),
                     a_spec=row, b_spec=pl.BlockSpec((None, d, d), lambda i: (3, 0, 0)), o_spec=row,
                     out_shape=(s, d), out_dtype=F32)
    onehot_a = _a_offset_onehot()
    diag_a = _diag_rows(onehot_a, rel_bias)
    (o_a, u_a, lse_a), gathered = _attn_a_fwd(zqkv, gate_a, diag_a, hosted=_allgather_routed(later_shards))
    wa_out, wkv, wb_in, wb_out, wkv_x = gathered
    wa_out = wa_out.reshape(d, d)
    wkv = wkv.reshape(d, -1)
    wkv_x = wkv_x.reshape(d, B_KVX)
    wb_out = wb_out.reshape(d, d)
    h1, nk, n2 = _out_norms("out_a_norms", u_a, wa_out, x, jnp.concatenate([gk, gb], axis=0))
    kvw = wkv.shape[1]
    kvx =_matmul("proj_kv", nk, wkv_x, dims=NN, grid=(nt + 1,), zero_axis=0,
                  a_spec=pl.BlockSpec((tm, d), lambda i: (jnp.maximum(i - 1, 0), 0)), b_spec=whole((d, B_KVX)),
                  o_spec=pl.BlockSpec((tm, B_KVX), lambda i: (i, 0)), out_shape=(tm + s, B_KVX), out_dtype=BF16)
    qb = _matmul("proj_b_q", n2, wb_in, dims=NN, grid=(2, nt),
                 a_spec=pl.BlockSpec((tm, d), lambda j, i: (i, 0)),
                 b_spec=pl.BlockSpec((None, d, half), lambda j, i: (j, 0, 0)),
                 o_spec=pl.BlockSpec((tm, half), lambda j, i: (i, j)), out_shape=(s, d), out_dtype=BF16)
    gate_b = _matmul("proj_b_gate", n2, wb_in, dims=NN, grid=(2, nt),
                     a_spec=pl.BlockSpec((tm, d), lambda j, i: (i, 0)),
                     b_spec=pl.BlockSpec((None, d, half), lambda j, i: (2 + j, 0, 0)),
                     o_spec=pl.BlockSpec((tm, half), lambda j, i: (i, j)), out_shape=(s, d), out_dtype=F32)
    onehot_b = _b_offset_onehot()
    base_b = jnp.roll(_diag_rows(onehot_b, t5)[..., ::-1], TQ, axis=-1)
    o_b, u_b, lse_b = _attn_b_fwd(qb, kvx, gate_b, base_b, sinks)
    dh2, loss, d_gf = _out_loss_head(u_b, wb_out, h1, target, gf)

    du_b = _matmul("dout_b", dh2, wb_out, dims=NT, grid=(nt,), a_spec=row, b_spec=whole((d, d)), o_spec=row,
                   out_shape=(s, d), out_dtype=F32)
    d_wb_out = _matmul("dw_out_b", u_b, dh2, dims=TN, grid=(2,),
                       a_spec=whole((s, d)), b_spec=pl.BlockSpec((s, half), lambda j: (0, j)),
                       o_spec=pl.BlockSpec((d, half), lambda j: (0, j)),
                       out_shape=(d, d), out_dtype=F32, also_bf16=True)
    dz_b, dkv, dsum_b, dsinks = _attn_b_bwd(qb, kvx, gate_b, o_b, du_b, lse_b, base_b, sinks)
    ddiag_b = jnp.roll(dsum_b[..., ::-1], -1, axis=-1)
    d_wb_in = _matmul("dw_in_b", n2, dz_b, dims=TN, grid=(4,),
                      a_spec=whole((s, d)), b_spec=pl.BlockSpec((None, s, half), lambda j: (j, 0, 0)),
                      o_spec=pl.BlockSpec((None, d, half), lambda j: (j, 0, 0)),
                      out_shape=(4, d, half), out_dtype=F32, also_bf16=True)
    d_wkv = _matmul("dw_kv", nk, dkv, dims=TN, grid=(1,),
                    a_spec=whole((s, d)), b_spec=whole((s, kvw)), o_spec=whole((d, kvw)),
                    out_shape=(d, kvw), out_dtype=F32, also_bf16=True)
    dh1, d_gkb = _proj_norm_bwd("dproj_kv_b", h1, dh2, jnp.concatenate([gk, gb], axis=0),
                                [(dkv[None], wkv[None]), (dz_b, wb_in)])

    du_a = _matmul("dout_a", dh1, wa_out, dims=NT, grid=(nt,), a_spec=row, b_spec=whole((d, d)), o_spec=row,
                   out_shape=(s, d), out_dtype=F32)
    d_wa_out = _matmul("dw_out_a", u_a, dh1, dims=TN, grid=(2,),
                       a_spec=whole((s, d)), b_spec=pl.BlockSpec((s, half), lambda j: (0, j)),
                       o_spec=pl.BlockSpec((d, half), lambda j: (0, j)),
                       out_shape=(d, d), out_dtype=F32, also_bf16=True)
    early = dict(a_w_out=[g.reshape(4, d // 4, d) for g in d_wa_out],
                 kv_w=[g.reshape(4, d // 4, kvw) for g in d_wkv], b_w_in=list(d_wb_in),
                 b_w_out=[g.reshape(4, d // 4, d) for g in d_wb_out])
    (dz_a, ddiag_a), early_recv = _attn_a_bwd(
        zqkv, gate_a, o_a, du_a, lse_a, diag_a, hosted=_scatter_hosted([early[n][1] for n in early]))
    d_wa_in = _matmul("dw_in_a", n1, dz_a, dims=TN, grid=(4, 2),
                      a_spec=whole((s, d)), b_spec=pl.BlockSpec((None, s, half), lambda j, h: (j, 0, h)),
                      o_spec=pl.BlockSpec((None, d, half), lambda j, h: (j, 0, h)),
                      out_shape=(4, d, d), out_dtype=F32, also_bf16=True)
    late_recv = _run_on_sequencer("scatter_a_w_in", _scatter_hosted([d_wa_in[1]]), SCATTER_PEERS, 0)
    grad_x, d_ga = _proj_norm_bwd("dproj_a", x, dh1, ga, [(dz_a, wa_in)])

    small = dict(a_norm=d_ga, kv_norm=d_gkb[0:1], b_norm=d_gkb[1:2], b_sinks=dsinks[0:1, :HEADS], final_norm=d_gf)
    small["by_offset"] = dict(a_rel_bias=(onehot_a, ddiag_a.reshape(HEADS, -1)),
                              t5_bias=(onehot_b, ddiag_b.reshape(HEADS, -1)))
    own = dict(a_w_in=d_wa_in[0], **{n: early[n][0] for n in early})
    received = dict(a_w_in=late_recv[0], **dict(zip(early, early_recv)))
    return loss, grad_x, small, own, received, d_wa_in[1]


SMALL = ("a_norm", "kv_norm", "b_norm", "b_sinks", "final_norm")
TABLES = ("a_rel_bias", "t5_bias")
BIG = ("a_w_in", "a_w_out", "kv_w", "b_w_in", "b_w_out")
ORDER = ("a_norm", "a_w_in", "a_rel_bias", "a_w_out", "kv_norm", "kv_w", "t5_bias", "b_norm", "b_w_in",
         "b_sinks", "b_w_out", "final_norm")


def kernel(x, a_norm, a_w_in, a_rel_bias, a_w_out, kv_norm, kv_w, t5_bias, b_norm, b_w_in, b_sinks, b_w_out, final_norm, loss_target, m_a_norm, m_a_w_in, m_a_rel_bias, m_a_w_out, m_kv_norm, m_kv_w, m_t5_bias, m_b_norm, m_b_w_in, m_b_sinks, m_b_w_out, m_final_norm, v_a_norm, v_a_w_in, v_a_rel_bias, v_a_w_out, v_kv_norm, v_kv_w, v_t5_bias, v_b_norm, v_b_w_in, v_b_sinks, v_b_w_out, v_final_norm):
    w = dict(a_norm=a_norm, a_w_in=a_w_in, a_rel_bias=a_rel_bias, a_w_out=a_w_out, kv_norm=kv_norm, kv_w=kv_w,
             t5_bias=t5_bias, b_norm=b_norm, b_w_in=b_w_in, b_sinks=b_sinks, b_w_out=b_w_out,
             final_norm=final_norm)
    m = dict(a_norm=m_a_norm, a_w_in=m_a_w_in, a_rel_bias=m_a_rel_bias, a_w_out=m_a_w_out, kv_norm=m_kv_norm,
             kv_w=m_kv_w, t5_bias=m_t5_bias, b_norm=m_b_norm, b_w_in=m_b_w_in, b_sinks=m_b_sinks,
             b_w_out=m_b_w_out, final_norm=m_final_norm)
    v = dict(a_norm=v_a_norm, a_w_in=v_a_w_in, a_rel_bias=v_a_rel_bias, a_w_out=v_a_w_out, kv_norm=v_kv_norm,
             kv_w=v_kv_w, t5_bias=v_t5_bias, b_norm=v_b_norm, b_w_in=v_b_w_in, b_sinks=v_b_sinks,
             b_w_out=v_b_w_out, final_norm=v_final_norm)
    d = D_MODEL
    chip = 2 * lax.axis_index("x") + lax.axis_index("y")

    shard2d = dict(a_w_in=a_w_in[0], a_w_out=a_w_out[0], kv_w=kv_w, b_w_in=b_w_in[0], b_w_out=b_w_out[0])

    wa_in, = _run_on_sequencer("allgather_first", _allgather_routed([shard2d["a_w_in"].astype(BF16)]),
                               GATHER_PEERS, 1)
    ga = _gather_gain(a_norm).reshape(1, d)

    later = [shard2d[n].astype(BF16) for n in BIG[1:]]
    kv_shard = later[BIG[1:].index("kv_w")]
    later.append(jnp.concatenate(
        [kv_shard[:, (i // 2) * HEAD_DIM:(i // 2 + 1) * HEAD_DIM] for i in range(B_KVX // HEAD_DIM)], axis=1))
    loss, grad_x, small, own, received, after_attention = _local_step(
        x[0], loss_target[0], ga, wa_in, a_rel_bias[0], later,
        kv_norm.reshape(1, d), t5_bias, b_norm, b_sinks, final_norm.reshape(1, d))

    out = {}
    as2d = lambda a: a.reshape(-1, a.shape[-1])
    small_res, (loss_sum, *offset_sums) = _small_step(
        [small[n] for n in SMALL], [loss] + [small["by_offset"][n][1] for n in TABLES],
        [as2d(w[n]) for n in SMALL], [as2d(m[n]) for n in SMALL], [as2d(v[n]) for n in SMALL],
        [n == "a_norm" for n in SMALL])
    for n, res in zip(SMALL, small_res):
        out[n] = [r.reshape(w[n].shape) for r in res]
    loss_out = loss_sum.reshape(())
    for n, summed in zip(TABLES, offset_sums):
        grad = _diag_rows_grad(small["by_offset"][n][0], summed)
        res = _adamw("adamw_" + n, as2d(w[n]).T, as2d(m[n]).T, as2d(v[n]).T, [grad])
        out[n] = [r.T.reshape(w[n].shape) for r in res]

    core_sums = [_sum_partials("sum_" + n, own[n], received[n], chip, after_attention) for n in BIG]
    sibling_sums = (_swap_with_sibling("swap_last", core_sums[:1])
                    + _swap_with_sibling("swap_early", core_sums[1:]))

    for n, mine, theirs in zip(BIG, core_sums, sibling_sums):
        res = _adamw("adamw_" + n, shard2d[n], m[n].reshape(shard2d[n].shape), v[n].reshape(shard2d[n].shape),
                     [mine, theirs])
        out[n] = [r.reshape(w[n].shape) for r in res]

    grads = [out[n][0] for n in ORDER]
    deltas = [out[n][1] for n in ORDER]
    new_m = [out[n][2] for n in ORDER]
    new_v = [out[n][3] for n in ORDER]
    return (loss_out, grad_x[None], *grads, *deltas, *new_m, *new_v)
```

```python
import math

import jax
import jax.numpy as jnp
import numpy as np
from jax import lax
from jax.experimental import pallas as pl
from jax.experimental.pallas import tpu as pltpu
from jax.experimental.pallas import tpu_sc as plsc

F32 = jnp.float32
BF16 = jnp.bfloat16
MESH = pl.DeviceIdType.MESH

D_MODEL = 1024
HEADS = 16
HEAD_DIM = 64
CHUNK = 64
RMS_EPS = 1e-6
SCALE = HEAD_DIM ** -0.5
A_LEFT_CHUNKS = 8
A_REL_CLIP = 256
B_LEFT_CHUNKS = 2
B_KV_HEADS = 2
B_GROUP = HEADS // B_KV_HEADS
T5_BUCKETS = 32
T5_MAX_DIST = 128
ADAM_LR = 0.001
ADAM_B1 = 0.9
ADAM_B2 = 0.999
ADAM_EPS = 1e-08
ADAM_WD = 0.01
ADAM_STEP = 10

MASKED = -1e30
LANES = 128
TQ = 128
A_PAIRS = 2
A_PAIRS_FWD = 4
KB = 128
A_KBLOCKS = A_LEFT_CHUNKS * CHUNK // KB + 1
B_KBLOCKS = B_LEFT_CHUNKS * CHUNK // KB + 1
A_WIN = A_KBLOCKS * KB
B_WIN = B_KBLOCKS * KB
TM = 512
TM_DENSE = 1024
TM_PARTS = 512
VMEM_LIMIT = 56 * 1024 * 1024

NT = (((1,), (1,)), ((), ()))
TN = (((0,), (0,)), ((), ()))
NN = (((1,), (0,)), ((), ()))


def _params(sem=None):
    return pltpu.CompilerParams(dimension_semantics=sem, vmem_limit_bytes=VMEM_LIMIT)


class _Hosted:
    def __init__(self, inputs, out_shapes, sems, first, middle, last):
        self.inputs, self.out_shapes, self.sems = list(inputs), list(out_shapes), list(sems)
        self.first, self.middle, self.last = first, middle, last


def _call(body, *, name, grid, in_specs, out_specs, out_shape, args, scratch_shapes=(), sem=None, hosted=None):
    in_specs, out_specs, out_shape = list(in_specs), list(out_specs), list(out_shape)
    scratch_shapes = list(scratch_shapes)
    if hosted is None:
        out = pl.pallas_call(
            body, name=name, grid=grid, in_specs=in_specs, out_specs=out_specs, out_shape=out_shape,
            scratch_shapes=scratch_shapes, compiler_params=_params(sem))(*args)
        return list(out), []
    n_in, n_out, n_scr = len(in_specs), len(out_shape), len(scratch_shapes)
    h_in, h_out = len(hosted.inputs), len(hosted.out_shapes)
    total = int(np.prod(grid)) if grid else 1

    def wrapped(*refs):
        ins, refs = refs[:n_in], refs[n_in:]
        h_ins, refs = refs[:h_in], refs[h_in:]
        outs, refs = refs[:n_out], refs[n_out:]
        h_outs, refs = refs[:h_out], refs[h_out:]
        scr, h_sems = refs[:n_scr], refs[n_scr:]
        step = 0
        for axis, size in enumerate(grid):
            step = step * size + pl.program_id(axis)

        if hosted.first is not None:
            @pl.when(step == 0)
            def _():
                hosted.first(h_ins, h_outs, h_sems)

        body(*ins, *outs, *scr)
        if hosted.middle is not None:
            @pl.when(step == total // 2)
            def _():
                hosted.middle(h_ins, h_outs, h_sems)

        if hosted.last is not None:
            @pl.when(step == total - 1)
            def _():
                hosted.last(h_ins, h_outs, h_sems)

    out = pl.pallas_call(
        wrapped, name=name, grid=grid, in_specs=in_specs + [ANY] * h_in, out_specs=out_specs + [ANY] * h_out,
        out_shape=out_shape + hosted.out_shapes, scratch_shapes=scratch_shapes + hosted.sems,
        compiler_params=_params(("arbitrary",) * len(grid)))(*args, *hosted.inputs)
    return list(out[:n_out]), list(out[n_out:])


def _matmul(name, a, b, *, dims, grid, a_spec, b_spec, o_spec, out_shape, out_dtype,
            also_bf16=False, zero_axis=None):
    def body(*refs):
        if zero_axis is None:
            product(*refs)
        else:
            @pl.when(pl.program_id(zero_axis) == 0)
            def _():
                refs[2][...] = jnp.zeros_like(refs[2])

            @pl.when(pl.program_id(zero_axis) > 0)
            def _():
                product(*refs)

    def product(a_ref, b_ref, o_ref, *more):
        prod = lax.dot_general(a_ref[...].astype(BF16), b_ref[...].astype(BF16), dims,
                               preferred_element_type=F32)
        o_ref[...] = prod.astype(out_dtype)
        if also_bf16:
            more[0][...] = prod.astype(BF16)

    out_specs = [o_spec]
    out_shapes = [jax.ShapeDtypeStruct(out_shape, out_dtype)]
    if also_bf16:
        out_specs.append(o_spec)
        out_shapes.append(jax.ShapeDtypeStruct(out_shape, BF16))
    out, _ = _call(body, name=name, grid=grid, in_specs=[a_spec, b_spec], out_specs=out_specs,
                   out_shape=out_shapes, args=[a, b], sem=("parallel",) * len(grid))
    return out[0] if not also_bf16 else tuple(out)


def _rms_rows(x):
    return lax.rsqrt(jnp.mean(x * x, axis=-1, keepdims=True) + RMS_EPS)


def _norm_fwd(name, x, gains):
    s, d = x.shape
    n = gains.shape[0]

    def body(x_ref, g_ref, *o_refs):
        xv = x_ref[...]
        xh = xv * _rms_rows(xv)
        for i in range(n):
            o_refs[i][...] = (xh * g_ref[i:i + 1, :]).astype(BF16)

    row = pl.BlockSpec((TM, d), lambda i: (i, 0))
    return pl.pallas_call(
        body, name=name, grid=(s // TM,),
        in_specs=[row, pl.BlockSpec((n, d), lambda i: (0, 0))],
        out_specs=[row] * n,
        out_shape=[jax.ShapeDtypeStruct((s, d), BF16)] * n,
        compiler_params=_params(("parallel",)),
    )(x, gains)


def _proj_norm_bwd(name, x, dres, gains, branches):
    s, d = x.shape
    n = len(branches)
    tm = min(TM_PARTS, s)

    def body(x_ref, r_ref, g_ref, *refs):
        ab_refs, dx_ref, dg_ref = refs[:2 * n], refs[2 * n], refs[2 * n + 1]
        i = pl.program_id(0)
        xv = x_ref[...]
        r = _rms_rows(xv)
        xh = xv * r

        @pl.when(i == 0)
        def _():
            dg_ref[...] = jnp.zeros_like(dg_ref)

        a = None
        for j in range(n):
            a_ref, b_ref = ab_refs[2 * j], ab_refs[2 * j + 1]
            dn = None
            for part in range(a_ref.shape[0]):
                term = lax.dot_general(a_ref[part], b_ref[part], NT, preferred_element_type=F32)
                dn = term if dn is None else dn + term
            t = dn * g_ref[j:j + 1, :]
            a = t if a is None else a + t
            dg_ref[j:j + 1, :] += jnp.sum(dn * xh, axis=0, keepdims=True)
        dx_ref[...] = r_ref[...] + r * (a - xh * jnp.mean(xh * a, axis=-1, keepdims=True))

    row = pl.BlockSpec((tm, d), lambda i: (i, 0))
    small = pl.BlockSpec((n, d), lambda i: (0, 0))
    ab_specs, ab_args = [], []
    for a, b in branches:
        ab_specs += [pl.BlockSpec((a.shape[0], tm, a.shape[2]), lambda i: (0, i, 0)),
                     pl.BlockSpec(b.shape, lambda i: (0, 0, 0))]
        ab_args += [a, b]
    return pl.pallas_call(
        body, name=name, grid=(s // tm,),
        in_specs=[row, row, small] + ab_specs,
        out_specs=[row, small],
        out_shape=[jax.ShapeDtypeStruct((s, d), F32), jax.ShapeDtypeStruct((n, d), F32)],
        compiler_params=_params(("arbitrary",)),
    )(x, dres, gains, *ab_args)


def _out_norms(name, u, w_out, resid, gains):
    s, d = resid.shape
    n = gains.shape[0]
    tm = min(TM_DENSE, s)

    def body(u_ref, w_ref, r_ref, g_ref, h_ref, *o_refs):
        hv = r_ref[...] + jnp.dot(u_ref[...], w_ref[...], preferred_element_type=F32)
        h_ref[...] = hv
        hh = hv * _rms_rows(hv)
        for i in range(n):
            o_refs[i][...] = (hh * g_ref[i:i + 1, :]).astype(BF16)

    row = pl.BlockSpec((tm, d), lambda i: (i, 0))
    return pl.pallas_call(
        body, name=name, grid=(s // tm,),
        in_specs=[row, pl.BlockSpec((d, d), lambda i: (0, 0)), row, pl.BlockSpec((n, d), lambda i: (0, 0))],
        out_specs=[row] * (n + 1),
        out_shape=[jax.ShapeDtypeStruct((s, d), F32)] + [jax.ShapeDtypeStruct((s, d), BF16)] * n,
        compiler_params=_params(("parallel",)),
    )(u, w_out, resid, gains)


def _out_loss_head(u, w_out, resid, target, gain):
    s, d = resid.shape
    tm = min(TM_PARTS, s)

    def body(u_ref, w_ref, r_ref, t_ref, g_ref, dh_ref, loss_ref, dg_ref):
        i = pl.program_id(0)
        hv = r_ref[...] + jnp.dot(u_ref[...], w_ref[...], preferred_element_type=F32)
        r = _rms_rows(hv)
        hh = hv * r
        g = g_ref[...]
        err = hh * g - t_ref[...]
        part = 0.5 * jnp.sum(jnp.sum(err * err, axis=-1, keepdims=True) * (1.0 / d), axis=0, keepdims=True)
        dy = err * (1.0 / d)
        a = dy * g
        dh_ref[...] = r * (a - hh * jnp.mean(hh * a, axis=-1, keepdims=True))
        dg = jnp.sum(dy * hh, axis=0, keepdims=True)

        @pl.when(i == 0)
        def _():
            loss_ref[...] = part
            dg_ref[...] = dg

        @pl.when(i > 0)
        def _():
            loss_ref[...] += part
            dg_ref[...] += dg

    row = pl.BlockSpec((tm, d), lambda i: (i, 0))
    return pl.pallas_call(
        body, name="out_b_loss_head", grid=(s // tm,),
        in_specs=[row, pl.BlockSpec((d, d), lambda i: (0, 0)), row, row, pl.BlockSpec((1, d), lambda i: (0, 0))],
        out_specs=[row, pl.BlockSpec((1, 1), lambda i: (0, 0)), pl.BlockSpec((1, d), lambda i: (0, 0))],
        out_shape=[jax.ShapeDtypeStruct((s, d), F32), jax.ShapeDtypeStruct((1, 1), F32),
                   jax.ShapeDtypeStruct((1, d), F32)],
        compiler_params=_params(("arbitrary",)),
    )(u, w_out, resid, target, gain)


def _silu_parts(g):
    sig = jax.nn.sigmoid(g)
    return g * sig, sig * (1.0 + g * (1.0 - sig))


def _lane_lo(rows):
    return lax.broadcasted_iota(jnp.int32, (rows, LANES), 1) < HEAD_DIM


def _stack_pair(x):
    lo = _lane_lo(x.shape[0])
    zero = jnp.zeros_like(x)
    return jnp.concatenate([jnp.where(lo, x, zero), jnp.where(lo, zero, x)], axis=0)


def _unstack_pair(y, w):
    return jnp.where(_lane_lo(w), y[:w], y[w:])


def _block_valid(b, left_blocks, width):
    col = lax.broadcasted_iota(jnp.int32, (1, 2 * width), 1)
    col = jnp.where(col >= width, col - width, col)
    return (col // KB + (b - left_blocks)) >= 0


def _toeplitz_tile(diag_row, width, left_chunks):
    wide = width + TQ
    rolled = pltpu.roll(jnp.broadcast_to(diag_row, (TQ, wide)), 1, 1, stride=1, stride_axis=0)
    i = lax.broadcasted_iota(jnp.int32, (TQ, width), 0) // CHUNK
    j = lax.broadcasted_iota(jnp.int32, (TQ, width), 1) // CHUNK
    dc = i + left_chunks - j
    return jnp.where((dc >= 0) & (dc <= left_chunks), rolled[:, TQ:], MASKED)


def _toeplitz_sum(tile, width):
    flip = (lax.broadcasted_iota(jnp.int32, (TQ, TQ), 0) + lax.broadcasted_iota(jnp.int32, (TQ, TQ), 1)
            == TQ - 1).astype(F32)
    reversed_rows = jnp.dot(flip, tile, precision=lax.Precision.HIGHEST, preferred_element_type=F32)
    padded = jnp.concatenate([reversed_rows, jnp.zeros((TQ, TQ), F32)], axis=1)
    rolled = pltpu.roll(padded, 0, 1, stride=1, stride_axis=0)
    return jnp.sum(rolled, axis=0, keepdims=True)


def _softmax_pair(sc, w, sink=None):
    ps, inv, lses = [], [], []
    for e in range(2):
        sh = sc[:, e * w:(e + 1) * w]
        m = jnp.max(sh, axis=-1, keepdims=True)
        if sink is not None:
            m = jnp.maximum(m, sink[e])
        ex = jnp.exp(sh - m)
        l = jnp.sum(ex, axis=-1, keepdims=True)
        if sink is not None:
            l = l + jnp.exp(sink[e] - m)
        ps.append(ex.astype(BF16))
        inv.append(1.0 / l)
        lses.append(m + jnp.log(l))
    return jnp.concatenate(ps, axis=-1), inv, lses


def _softmax_pair_bwd(sc, dp, lse, delta, w):
    ps, dss = [], []
    for e in range(2):
        p = jnp.exp(sc[:, e * w:(e + 1) * w] - lse[e])
        ps.append(p)
        dss.append(p * (dp[:, e * w:(e + 1) * w] - delta[e]))
    return jnp.concatenate(ps, axis=-1), jnp.concatenate(dss, axis=-1)


def _pair_rowsums(x, lo):
    zero = jnp.zeros_like(x)
    return (jnp.sum(jnp.where(lo, x, zero), axis=-1, keepdims=True),
            jnp.sum(jnp.where(lo, zero, x), axis=-1, keepdims=True))


def _a_qkv_specs(rows, pad, pw):
    return [pl.BlockSpec((None, TQ, pw), lambda p, b: (0, b + pad // TQ, p)),
            pl.BlockSpec((None, rows, pw), lambda p, b: (1, 0, p)),
            pl.BlockSpec((None, rows, pw), lambda p, b: (2, 0, p))]


def _window(ref, b, pad, win, lanes):
    start = pl.multiple_of(b * TQ + pad - (win - TQ), KB)
    return ref[pl.ds(start, win), lanes]


def _attn_a_fwd(zqkv, g, diag, hosted=None):
    s = g.shape[0]
    pad = zqkv.shape[1] - s
    nb = s // TQ
    left = A_KBLOCKS - 1
    pairs = A_PAIRS_FWD
    pw = pairs * LANES
    wide = A_WIN + TQ

    def body(q_ref, k_ref, v_ref, g_ref, diag_ref, o_ref, u_ref, lse_ref, bias_scr):
        b = pl.program_id(1)

        @pl.when(b == 0)
        def _():
            for hh in range(2 * pairs):
                bias_scr[hh // 2, :, (hh % 2) * A_WIN:(hh % 2 + 1) * A_WIN] = _toeplitz_tile(
                    diag_ref[hh], A_WIN, A_LEFT_CHUNKS)

        def step(first_blocks):
            lo = _lane_lo(TQ)
            for pp in range(pairs):
                ln = slice(pp * LANES, (pp + 1) * LANES)
                kcat = _stack_pair(_window(k_ref, b, pad, A_WIN, ln))
                vcat = _stack_pair(_window(v_ref, b, pad, A_WIN, ln))
                sc = lax.dot_general(q_ref[:, ln] * SCALE, kcat, NT, preferred_element_type=F32) + bias_scr[pp]
                if first_blocks:
                    sc = jnp.where(_block_valid(b, left, A_WIN), sc, MASKED)
                p, inv, lses = _softmax_pair(sc, A_WIN)
                ov = jnp.dot(p, vcat, preferred_element_type=F32) * jnp.where(lo, inv[0], inv[1])
                o_ref[:, ln] = ov
                lse_ref[pp] = jnp.where(lo, lses[0], lses[1])
                sg, _ = _silu_parts(g_ref[:, ln])
                u_ref[:, ln] = (ov * sg).astype(BF16)

        @pl.when(b < left)
        def _():
            step(True)

        @pl.when(b >= left)
        def _():
            step(False)

    tile = pl.BlockSpec((TQ, pw), lambda p, b: (b, p))
    return _call(
        body, name="attn_a_fwd", grid=(HEADS // 2 // pairs, nb),
        in_specs=_a_qkv_specs(pad + s, pad, pw) + [
            tile, pl.BlockSpec((2 * pairs, 1, wide), lambda p, b: (p, 0, 0))],
        out_specs=[tile, tile, pl.BlockSpec((pairs, TQ, LANES), lambda p, b: (p, b, 0))],
        out_shape=[jax.ShapeDtypeStruct((s, D_MODEL), F32), jax.ShapeDtypeStruct((s, D_MODEL), BF16),
                   jax.ShapeDtypeStruct((HEADS // 2, s, LANES), F32)],
        scratch_shapes=[pltpu.VMEM((pairs, TQ, 2 * A_WIN), F32)],
        sem=("parallel", "arbitrary"), hosted=hosted,
        args=(zqkv, zqkv, zqkv, g, diag))


def _attn_a_bwd(zqkv, g, o, du, lse, diag, hosted=None):
    s = g.shape[0]
    pad = zqkv.shape[1] - s
    nb = s // TQ
    left = A_KBLOCKS - 1
    pw = A_PAIRS * LANES
    wide = A_WIN + TQ

    def body(q_ref, k_ref, v_ref, g_ref, o_ref, du_ref, lse_ref, diag_ref, dz_ref, ddiag_ref,
             bias_scr, dbias_acc, dk_acc, dv_acc):
        b = pl.program_id(1)

        @pl.when(b == 0)
        def _():
            for hh in range(2 * A_PAIRS):
                bias_scr[hh // 2, :, (hh % 2) * A_WIN:(hh % 2 + 1) * A_WIN] = _toeplitz_tile(
                    diag_ref[hh], A_WIN, A_LEFT_CHUNKS)
            dbias_acc[...] = jnp.zeros_like(dbias_acc)
            dk_acc[...] = jnp.zeros_like(dk_acc)
            dv_acc[...] = jnp.zeros_like(dv_acc)

        def step(first_blocks):
            lo = _lane_lo(TQ)
            rows = pl.ds(pl.multiple_of(b * TQ, TQ), TQ)
            sg, dsg = _silu_parts(g_ref[...])
            duv = du_ref[...]
            ov = o_ref[...]
            do = duv * sg
            dz_ref[3, rows, :] = (duv * ov * dsg).astype(BF16)
            do_o = do * ov
            do_bf = do.astype(BF16)
            for pp in range(A_PAIRS):
                ln = slice(pp * LANES, (pp + 1) * LANES)
                q = q_ref[:, ln] * SCALE
                kcat = _stack_pair(_window(k_ref, b, pad, A_WIN, ln))
                vcat = _stack_pair(_window(v_ref, b, pad, A_WIN, ln))
                sc = lax.dot_general(q, kcat, NT, preferred_element_type=F32) + bias_scr[pp]
                if first_blocks:
                    sc = jnp.where(_block_valid(b, left, A_WIN), sc, MASKED)
                lse_t = lse_ref[pp]
                dp = lax.dot_general(do_bf[:, ln], vcat, NT, preferred_element_type=F32)
                p, ds = _softmax_pair_bwd(sc, dp, (lse_t[:, 0:1], lse_t[:, HEAD_DIM:HEAD_DIM + 1]),
                                          _pair_rowsums(do_o[:, ln], lo), A_WIN)
                dbias_acc[pp] += ds
                dsb = ds.astype(BF16)
                dz_ref[0, rows, ln] = (jnp.dot(dsb, kcat, preferred_element_type=F32) * SCALE).astype(BF16)
                pb = p.astype(BF16)
                dob = do_bf[:, ln]
                dkt = jnp.concatenate([
                    lax.dot_general(q[:, e * HEAD_DIM:(e + 1) * HEAD_DIM], dsb[:, e * A_WIN:(e + 1) * A_WIN], TN,
                                    preferred_element_type=F32) for e in range(2)], axis=0)
                dvt = jnp.concatenate([
                    lax.dot_general(dob[:, e * HEAD_DIM:(e + 1) * HEAD_DIM], pb[:, e * A_WIN:(e + 1) * A_WIN], TN,
                                    preferred_element_type=F32) for e in range(2)], axis=0)
                for t in range(A_KBLOCKS):
                    blk = b + (pad // KB - left + t)
                    dk_acc[blk, ln, :] += dkt[:, t * KB:(t + 1) * KB]
                    dv_acc[blk, ln, :] += dvt[:, t * KB:(t + 1) * KB]

        @pl.when(b < left)
        def _():
            step(True)

        @pl.when(b >= left)
        def _():
            step(False)

        @pl.when(b == nb - 1)
        def _():
            for kb in range(s // KB):
                dz_ref[1, kb * KB:(kb + 1) * KB, :] = dk_acc[pad // KB + kb].T.astype(BF16)
                dz_ref[2, kb * KB:(kb + 1) * KB, :] = dv_acc[pad // KB + kb].T.astype(BF16)
            for hh in range(2 * A_PAIRS):
                ddiag_ref[hh] = _toeplitz_sum(
                    dbias_acc[hh // 2, :, (hh % 2) * A_WIN:(hh % 2 + 1) * A_WIN], A_WIN)

    tile = pl.BlockSpec((TQ, pw), lambda p, b: (b, p))
    diag_spec = pl.BlockSpec((2 * A_PAIRS, 1, wide), lambda p, b: (p, 0, 0))
    return _call(
        body, name="attn_a_bwd", grid=(HEADS // 2 // A_PAIRS, nb),
        in_specs=_a_qkv_specs(pad + s, pad, pw) + [
            tile, tile, tile, pl.BlockSpec((A_PAIRS, TQ, LANES), lambda p, b: (p, b, 0)), diag_spec],
        out_specs=[pl.BlockSpec((4, s, pw), lambda p, b: (0, 0, p)), diag_spec],
        out_shape=[jax.ShapeDtypeStruct((4, s, D_MODEL), BF16),
                   jax.ShapeDtypeStruct((HEADS, 1, wide), F32)],
        scratch_shapes=[pltpu.VMEM((A_PAIRS, TQ, 2 * A_WIN), F32), pltpu.VMEM((A_PAIRS, TQ, 2 * A_WIN), F32),
                        pltpu.VMEM(((pad + s) // KB, pw, KB), F32), pltpu.VMEM(((pad + s) // KB, pw, KB), F32)],
        sem=("parallel", "arbitrary"), hosted=hosted,
        args=(zqkv, zqkv, zqkv, g, o, du, lse, diag))


B_STACK = B_GROUP // 2
B_KVX = 4 * LANES
B_ROWS = B_STACK * TQ
B_WIDE = B_WIN + TQ


def _b_head_place(h):
    return h // B_GROUP, (h % B_GROUP) // 2, h % 2


def _toeplitz_tile_t(base_row, width, left_chunks):
    wide = width + TQ
    rolled = pltpu.roll(jnp.broadcast_to(base_row, (width, wide)), 0, 1, stride=1, stride_axis=0)
    j = lax.broadcasted_iota(jnp.int32, (width, TQ), 0) // CHUNK
    i = lax.broadcasted_iota(jnp.int32, (width, TQ), 1) // CHUNK
    dc = i + left_chunks - j
    return jnp.where((dc >= 0) & (dc <= left_chunks), rolled[:, :TQ], MASKED)


def _toeplitz_sum_t(tile_t, width):
    flip = (lax.broadcasted_iota(jnp.int32, (width, width), 0) + lax.broadcasted_iota(jnp.int32, (width, width), 1)
            == width - 1).astype(F32)
    reversed_rows = jnp.dot(flip, tile_t, precision=lax.Precision.HIGHEST, preferred_element_type=F32)
    padded = jnp.concatenate([reversed_rows, jnp.zeros((width, width), F32)], axis=1)
    rolled = pltpu.roll(padded, 0, 1, stride=1, stride_axis=0)
    return jnp.sum(rolled, axis=0, keepdims=True)


def _b_build_bias(base_ref, bias_scr):
    for h in range(HEADS):
        gi, pr, e = _b_head_place(h)
        bias_scr[gi, e * B_WIN:(e + 1) * B_WIN, pr * TQ:(pr + 1) * TQ] = _toeplitz_tile_t(
            base_ref[h], B_WIN, B_LEFT_CHUNKS)


def _b_stack(x, gi):
    return jnp.concatenate(
        [x[:, (B_STACK * gi + pr) * LANES:(B_STACK * gi + pr + 1) * LANES] for pr in range(B_STACK)], axis=0)


def _b_sink_rows(sink_ref, gi):
    block = lax.broadcasted_iota(jnp.int32, (1, B_ROWS), 1) // TQ
    rows = []
    for e in range(2):
        row = jnp.zeros((1, B_ROWS), F32)
        for pr in range(B_STACK):
            h = B_GROUP * gi + 2 * pr + e
            row = jnp.where(block == pr, sink_ref[0:1, h:h + 1], row)
        rows.append(row)
    return rows


def _b_scores_t(q_ref, kvv, bias_scr, gi, b, left, first_blocks):
    kcat = _stack_pair(kvv[:, gi * LANES:(gi + 1) * LANES])
    vcat = _stack_pair(kvv[:, (B_KV_HEADS + gi) * LANES:(B_KV_HEADS + gi + 1) * LANES])
    qs = _b_stack(q_ref, gi) * SCALE
    sc = lax.dot_general(kcat, qs, NT, preferred_element_type=F32) + bias_scr[gi]
    if first_blocks:
        row = lax.broadcasted_iota(jnp.int32, (2 * B_WIN, 1), 0)
        row = jnp.where(row >= B_WIN, row - B_WIN, row)
        sc = jnp.where((row // KB + (b - left)) >= 0, sc, MASKED)
    return kcat, vcat, qs, sc


def _attn_b_fwd(qb, kvx, gate, base, sinks):
    s = qb.shape[0]
    pad = kvx.shape[0] - s
    nb = s // TQ
    left = B_KBLOCKS - 1

    def body(q_ref, kv_ref, g_ref, base_ref, sink_ref, o_ref, u_ref, lse_ref, bias_scr):
        b = pl.program_id(0)

        @pl.when(b == 0)
        def _():
            _b_build_bias(base_ref, bias_scr)

        def step(first_blocks):
            kvv = _window(kv_ref, b, pad, B_WIN, slice(None))
            upper = lax.broadcasted_iota(jnp.int32, (LANES, B_ROWS), 0) < HEAD_DIM
            lse_rows = []
            for gi in range(B_KV_HEADS):
                kcat, vcat, qs, sc = _b_scores_t(q_ref, kvv, bias_scr, gi, b, left, first_blocks)
                sink = _b_sink_rows(sink_ref, gi)
                ps, inv = [], []
                for e in range(2):
                    sh = sc[e * B_WIN:(e + 1) * B_WIN]
                    m = jnp.maximum(jnp.max(sh, axis=0, keepdims=True), sink[e])
                    ex = jnp.exp(sh - m)
                    l = jnp.sum(ex, axis=0, keepdims=True) + jnp.exp(sink[e] - m)
                    ps.append(ex.astype(BF16))
                    inv.append(1.0 / l)
                    lse_rows.append(m + jnp.log(l))
                pt = jnp.concatenate(ps, axis=0)
                ot = lax.dot_general(vcat, pt, TN, preferred_element_type=F32) * jnp.where(upper, inv[0], inv[1])
                ov = ot.T
                for pr in range(B_STACK):
                    pair = B_STACK * gi + pr
                    o_ref[:, pair * LANES:(pair + 1) * LANES] = ov[pr * TQ:(pr + 1) * TQ]
            lse_ref[0] = jnp.concatenate(lse_rows + [jnp.zeros((8 - len(lse_rows), B_ROWS), F32)], axis=0)
            sg, _ = _silu_parts(g_ref[...])
            u_ref[...] = (o_ref[...] * sg).astype(BF16)

        @pl.when(b < left)
        def _():
            step(True)

        @pl.when(b >= left)
        def _():
            step(False)

    row = pl.BlockSpec((TQ, D_MODEL), lambda b: (b, 0))
    return pl.pallas_call(
        body, name="attn_b_fwd", grid=(nb,),
        in_specs=[row, pl.BlockSpec((pad + s, B_KVX), lambda b: (0, 0)), row,
                  pl.BlockSpec((HEADS, 1, B_WIDE), lambda b: (0, 0, 0)), pl.BlockSpec((1, HEADS), lambda b: (0, 0))],
        out_specs=[row, row, pl.BlockSpec((1, 8, B_ROWS), lambda b: (b, 0, 0))],
        out_shape=[jax.ShapeDtypeStruct((s, D_MODEL), F32), jax.ShapeDtypeStruct((s, D_MODEL), BF16),
                   jax.ShapeDtypeStruct((nb, 8, B_ROWS), F32)],
        scratch_shapes=[pltpu.VMEM((B_KV_HEADS, 2 * B_WIN, B_ROWS), F32)],
        compiler_params=_params(("arbitrary",)),
    )(qb, kvx, gate, base, sinks)


def _attn_b_bwd(qb, kvx, gate, o, du, lse, base, sinks):
    s = qb.shape[0]
    pad = kvx.shape[0] - s
    nb = s // TQ
    left = B_KBLOCKS - 1
    half = D_MODEL // 2

    def body(q_ref, kv_ref, g_ref, o_ref, du_ref, lse_ref, base_ref, sink_ref, dz_ref, dkv_ref, dsum_ref,
             dsink_ref, bias_scr, dbias_acc, dkv_acc, dsink_acc):
        b = pl.program_id(0)

        @pl.when(b == 0)
        def _():
            _b_build_bias(base_ref, bias_scr)
            dbias_acc[...] = jnp.zeros_like(dbias_acc)
            dkv_acc[...] = jnp.zeros_like(dkv_acc)
            dsink_acc[...] = jnp.zeros_like(dsink_acc)

        def step(first_blocks):
            kvv = _window(kv_ref, b, pad, B_WIN, slice(None))
            sg, dsg = _silu_parts(g_ref[...])
            duv = du_ref[...]
            ov = o_ref[...]
            do = duv * sg
            dgate = (duv * ov * dsg).astype(BF16)
            dz_ref[2] = dgate[:, :half]
            dz_ref[3] = dgate[:, half:]
            do_o = do * ov
            do_bf = do.astype(BF16)
            lse_all = lse_ref[0]
            dsink_rows = []
            for gi in range(B_KV_HEADS):
                kcat, vcat, qs, sc = _b_scores_t(q_ref, kvv, bias_scr, gi, b, left, first_blocks)
                dos = _b_stack(do_bf, gi)
                doo_t = _b_stack(do_o, gi).T
                delta = (jnp.sum(doo_t[:HEAD_DIM], axis=0, keepdims=True),
                         jnp.sum(doo_t[HEAD_DIM:], axis=0, keepdims=True))
                sink = _b_sink_rows(sink_ref, gi)
                dp = lax.dot_general(vcat, dos, NT, preferred_element_type=F32)
                ps, dss = [], []
                for e in range(2):
                    lse_e = lse_all[2 * gi + e:2 * gi + e + 1]
                    delta_e = delta[e]
                    p = jnp.exp(sc[e * B_WIN:(e + 1) * B_WIN] - lse_e)
                    ps.append(p.astype(BF16))
                    dss.append(p * (dp[e * B_WIN:(e + 1) * B_WIN] - delta_e))
                    dsink_rows.append(-jnp.exp(sink[e] - lse_e) * delta_e)
                ds = jnp.concatenate(dss, axis=0)
                dbias_acc[gi] += ds
                dsb = ds.astype(BF16)
                dq = (lax.dot_general(kcat, dsb, TN, preferred_element_type=F32) * SCALE).T.astype(BF16)
                for pr in range(B_STACK):
                    dz_ref[gi, :, pr * LANES:(pr + 1) * LANES] = dq[pr * TQ:(pr + 1) * TQ]
                dk = _unstack_pair(jnp.dot(dsb, qs, preferred_element_type=F32), B_WIN)
                dv = _unstack_pair(jnp.dot(jnp.concatenate(ps, axis=0), dos, preferred_element_type=F32), B_WIN)
                krows = pl.ds(pl.multiple_of(b * TQ + pad - (B_WIN - TQ), KB), B_WIN)
                dkv_acc[krows, gi * LANES:(gi + 1) * LANES] += dk
                dkv_acc[krows, (B_KV_HEADS + gi) * LANES:(B_KV_HEADS + gi + 1) * LANES] += dv
            dsink_acc[...] += jnp.concatenate(
                dsink_rows + [jnp.zeros((8 - len(dsink_rows), B_ROWS), F32)], axis=0)

        @pl.when(b < left)
        def _():
            step(True)

        @pl.when(b >= left)
        def _():
            step(False)

        @pl.when(b == nb - 1)
        def _():
            lo_s = _lane_lo(s)
            for which in range(2):
                folded = []
                for gi in range(B_KV_HEADS):
                    part = dkv_acc[pad:pad + s, (which * B_KV_HEADS + gi) * LANES:(which * B_KV_HEADS + gi + 1) * LANES]
                    folded.append(part + pltpu.roll(part, HEAD_DIM, 1))
                dkv_ref[:, which * LANES:(which + 1) * LANES] = jnp.where(lo_s, folded[0], folded[1]).astype(BF16)
            lane8 = lax.broadcasted_iota(jnp.int32, dsink_ref.shape, 1)
            tot = jnp.zeros(dsink_ref.shape, F32)
            for h in range(HEADS):
                gi, pr, e = _b_head_place(h)
                dsum_ref[h] = _toeplitz_sum_t(
                    dbias_acc[gi, e * B_WIN:(e + 1) * B_WIN, pr * TQ:(pr + 1) * TQ], B_WIN)
                per_query = dsink_acc[2 * gi + e:2 * gi + e + 1, pr * TQ:(pr + 1) * TQ]
                tot = jnp.where(lane8 == h, jnp.sum(per_query, axis=1, keepdims=True), tot)
            dsink_ref[...] = tot

    row = pl.BlockSpec((TQ, D_MODEL), lambda b: (b, 0))
    base_spec = pl.BlockSpec((HEADS, 1, B_WIDE), lambda b: (0, 0, 0))
    return pl.pallas_call(
        body, name="attn_b_bwd", grid=(nb,),
        in_specs=[row, pl.BlockSpec((pad + s, B_KVX), lambda b: (0, 0)), row, row, row,
                  pl.BlockSpec((1, 8, B_ROWS), lambda b: (b, 0, 0)), base_spec,
                  pl.BlockSpec((1, HEADS), lambda b: (0, 0))],
        out_specs=[pl.BlockSpec((4, TQ, half), lambda b: (0, b, 0)),
                   pl.BlockSpec((s, 2 * LANES), lambda b: (0, 0)), base_spec,
                   pl.BlockSpec((8, LANES), lambda b: (0, 0))],
        out_shape=[jax.ShapeDtypeStruct((4, s, half), BF16), jax.ShapeDtypeStruct((s, 2 * LANES), BF16),
                   jax.ShapeDtypeStruct((HEADS, 1, B_WIDE), F32), jax.ShapeDtypeStruct((8, LANES), F32)],
        scratch_shapes=[pltpu.VMEM((B_KV_HEADS, 2 * B_WIN, B_ROWS), F32),
                        pltpu.VMEM((B_KV_HEADS, 2 * B_WIN, B_ROWS), F32),
                        pltpu.VMEM((pad + s, B_KVX), F32), pltpu.VMEM((8, B_ROWS), F32)],
        compiler_params=_params(("arbitrary",)),
    )(qb, kvx, gate, o, du, lse, base, sinks)


def _t5_bucket(rel):
    nb = T5_BUCKETS // 2
    max_exact = nb // 2
    ret = jnp.where(rel > 0, nb, 0)
    n = jnp.abs(rel)
    nf = jnp.maximum(n, 1).astype(jnp.float32)
    large = max_exact + (jnp.log(nf / max_exact) / math.log(T5_MAX_DIST / max_exact)
                         * (nb - max_exact)).astype(jnp.int32)
    large = jnp.minimum(large, nb - 1)
    return ret + jnp.where(n < max_exact, n, large)


def _a_offset_onehot():
    c = np.arange(A_WIN + TQ)
    dist = A_LEFT_CHUNKS * CHUNK + TQ - 1 - c
    idx = np.clip(dist, -A_REL_CLIP, A_REL_CLIP) + A_REL_CLIP
    onehot = np.zeros((A_WIN + TQ, 2 * A_REL_CLIP + 1), np.float32)
    onehot[c, idx] = 1.0
    return jnp.asarray(onehot)


def _b_offset_onehot():
    c = jnp.arange(B_WIN + TQ, dtype=jnp.int32)
    rel = c - (TQ - 1) - B_LEFT_CHUNKS * CHUNK
    return (_t5_bucket(rel)[:, None] == jnp.arange(T5_BUCKETS)[None, :]).astype(F32)


def _diag_rows(onehot, table):
    rows = jnp.dot(onehot, table.astype(F32), precision=lax.Precision.HIGHEST)
    return rows.T.reshape(HEADS, 1, onehot.shape[0])


def _diag_rows_grad(onehot, ddiag):
    return jnp.dot(ddiag.reshape(HEADS, onehot.shape[0]), onehot, precision=lax.Precision.HIGHEST)


def _position():
    x, y, c = lax.axis_index("x"), lax.axis_index("y"), lax.axis_index("c")
    chips = [(1 - x, y), (x, 1 - y), (1 - x, 1 - y)]
    return x, y, c, chips


ANY = pl.BlockSpec(memory_space=pl.ANY)


def _allgather_routed(shards):
    n = len(shards)

    def piece(block_ref, t, c, quarter=None):
        half = shards[t].shape[0] // 2
        if quarter is None:
            return block_ref.at[pl.ds(c * half, half)]
        return block_ref.at[pl.ds(c * half + quarter * (half // 2), half // 2)]

    def copies(kind, ins, outs, sems):
        ici_send, ici_recv, pass_send, pass_recv, local_sems = sems
        x, y, c, chips = _position()
        mine = 2 * x + y
        if kind == "local":
            return [pltpu.make_async_copy(ins[t], outs[t].at[mine], local_sems.at[t]) for t in range(n)]
        ids = [2 * chip[0] + chip[1] for chip in chips]
        made = []
        for t in range(n):
            def ici(k, to):
                return dict(send_sem=ici_send.at[4 * t + k], recv_sem=ici_recv.at[4 * t + k],
                            device_id=(chips[to][0], chips[to][1], c), device_id_type=MESH)

            def d2d(k):
                return dict(send_sem=pass_send.at[4 * t + k], recv_sem=pass_recv.at[4 * t + k],
                            device_id=(x, y, 1 - c), device_id_type=MESH)

            def same(ref, where):
                return pltpu.make_async_remote_copy(src_ref=ref, dst_ref=ref, **where)

            if kind == "send":
                for k in range(2):
                    made.append(pltpu.make_async_remote_copy(
                        src_ref=piece(ins[t], t, c), dst_ref=piece(outs[t].at[mine], t, c), **ici(k, k)))
            elif kind == "landed":
                made += [same(piece(outs[t].at[ids[k]], t, c), ici(k, k)) for k in range(2)]
            elif kind == "forward":
                made.append(same(piece(outs[t].at[ids[0]], t, c, 0), ici(2, 1)))
                made.append(same(piece(outs[t].at[ids[1]], t, c, 1), ici(3, 0)))
            elif kind == "arrived":
                made.append(same(piece(outs[t].at[ids[2]], t, c, 0), ici(2, 1)))
                made.append(same(piece(outs[t].at[ids[2]], t, c, 1), ici(3, 0)))
            else:
                core = 1 - c if kind == "passed" else c
                if kind in ("pass halves", "passed"):
                    made += [same(piece(outs[t].at[ids[k]], t, core), d2d(k)) for k in range(2)]
                if kind in ("pass quarters", "passed"):
                    made += [same(piece(outs[t].at[ids[2]], t, core, k), d2d(2 + k)) for k in range(2)]
        return made

    def first(ins, outs, sems):
        for cp in copies("local", ins, outs, sems) + copies("send", ins, outs, sems):
            cp.start()

    def middle(ins, outs, sems):
        for got, onward, near in zip(copies("landed", ins, outs, sems), copies("forward", ins, outs, sems),
                                     copies("pass halves", ins, outs, sems)):
            got.wait_recv()
            near.start()
            onward.start()

    def last(ins, outs, sems):
        quarters = copies("pass quarters", ins, outs, sems)
        for got, near in zip(copies("arrived", ins, outs, sems), quarters):
            got.wait_recv()
            near.start()
        for cp in copies("passed", ins, outs, sems):
            cp.wait_recv()
        for cp in (copies("send", ins, outs, sems) + copies("forward", ins, outs, sems)
                   + copies("pass halves", ins, outs, sems) + quarters):
            cp.wait_send()
        for cp in copies("local", ins, outs, sems):
            cp.wait()

    return _Hosted(shards, [jax.ShapeDtypeStruct((4,) + w.shape, w.dtype) for w in shards],
                   [pltpu.SemaphoreType.DMA((4 * n,))] * 4 + [pltpu.SemaphoreType.DMA((n,))],
                   first, middle, last)


def _scatter_hosted(grads):
    n = len(grads)

    def copies(ins, outs, sems):
        send_sems, recv_sems = sems
        x, y, c, chips = _position()
        return [pltpu.make_async_remote_copy(
            src_ref=ins[t].at[2 * chip[0] + chip[1]], dst_ref=outs[t].at[j],
            send_sem=send_sems.at[3 * t + j], recv_sem=recv_sems.at[3 * t + j],
            device_id=(chip[0], chip[1], c), device_id_type=MESH)
            for t in range(n) for j, chip in enumerate(chips)]

    def first(ins, outs, sems):
        for cp in copies(ins, outs, sems):
            cp.start()

    def last(ins, outs, sems):
        for cp in copies(ins, outs, sems):
            cp.wait()

    return _Hosted(grads, [jax.ShapeDtypeStruct((3,) + g.shape[1:], g.dtype) for g in grads],
                   [pltpu.SemaphoreType.DMA((3 * n,))] * 2, first, None, last)


GATHER_PEERS = "x and y neighbours (same core) and the sibling core"
SCATTER_PEERS = "the same core of the three other chips"
EVERYONE = "the seven other devices"


def _run_on_sequencer(name, hosted, peers, collective_id):
    ins = [jax.new_ref(a, memory_space=pltpu.MemorySpace.HBM) for a in hosted.inputs]
    outs = [jax.empty_ref(shape, memory_space=pltpu.MemorySpace.HBM) for shape in hosted.out_shapes]

    @pl.kernel(mesh=plsc.ScalarSubcoreMesh(axis_name="sequencer", num_cores=1), name=name,
               scratch_types=tuple(hosted.sems), compiler_params=pltpu.CompilerParams(collective_id=collective_id))
    def launch(*sems):
        x, y, c, chips = _position()
        if peers == GATHER_PEERS:
            devices = [(chip[0], chip[1], c) for chip in chips[:2]] + [(x, y, 1 - c)]
        elif peers == SCATTER_PEERS:
            devices = [(chip[0], chip[1], c) for chip in chips]
        else:
            devices = [(x ^ (k >> 2), y ^ ((k >> 1) & 1), c ^ (k & 1)) for k in range(1, 8)]
        barrier = pltpu.get_barrier_semaphore()
        for device in devices:
            pl.semaphore_signal(barrier, inc=1, device_id=device, device_id_type=MESH)
        pl.semaphore_wait(barrier, len(devices))
        hosted.first(ins, outs, sems)
        if hosted.middle is not None:
            hosted.middle(ins, outs, sems)
        hosted.last(ins, outs, sems)

    launch()
    return [o[...] for o in outs]


def _gather_gain(shard):
    def body(in_ref, out_ref, send_sems, recv_sems):
        x, y, c, chips = _position()
        out_ref[2 * x + y] = in_ref[...]
        sends = [pltpu.make_async_remote_copy(
            src_ref=in_ref, dst_ref=out_ref.at[2 * x + y], send_sem=send_sems.at[j], recv_sem=recv_sems.at[j],
            device_id=(chip[0], chip[1], c), device_id_type=MESH) for j, chip in enumerate(chips)]
        for cp in sends:
            cp.start()
        for j, chip in enumerate(chips):
            pltpu.make_async_remote_copy(
                src_ref=in_ref, dst_ref=out_ref.at[2 * chip[0] + chip[1]], send_sem=send_sems.at[j],
                recv_sem=recv_sems.at[j], device_id=(chip[0], chip[1], c), device_id_type=MESH).wait_recv()
        for cp in sends:
            cp.wait_send()

    vmem = pl.BlockSpec(memory_space=pltpu.VMEM)
    return pl.pallas_call(
        body, name="gather_gain", in_specs=[vmem], out_specs=vmem,
        out_shape=jax.ShapeDtypeStruct((4,) + shard.shape, shard.dtype),
        scratch_shapes=[pltpu.SemaphoreType.DMA((3,))] * 2,
    )(shard)


def _swap_hosted(blocks):
    n = len(blocks)

    def copies(ins, outs, sems):
        send_sems, recv_sems = sems
        x, y, c, _ = _position()
        return [pltpu.make_async_remote_copy(
            src_ref=ins[t], dst_ref=outs[t], send_sem=send_sems.at[t], recv_sem=recv_sems.at[t],
            device_id=(x, y, 1 - c), device_id_type=MESH) for t in range(n)]

    def first(ins, outs, sems):
        for cp in copies(ins, outs, sems):
            cp.start()

    def last(ins, outs, sems):
        for cp in copies(ins, outs, sems):
            cp.wait()

    return _Hosted(blocks, [jax.ShapeDtypeStruct(b.shape, b.dtype) for b in blocks],
                   [pltpu.SemaphoreType.DMA((n,))] * 2, first, None, last)


def _swap_with_sibling(name, blocks):
    hosted = _swap_hosted(blocks)
    n = len(blocks)

    def body(*refs):
        ins, outs, sems = refs[:n], refs[n:2 * n], refs[2 * n:]
        hosted.first(ins, outs, sems)
        hosted.last(ins, outs, sems)

    return pl.pallas_call(
        body, name=name, in_specs=[ANY] * n, out_specs=[ANY] * n, out_shape=hosted.out_shapes,
        scratch_shapes=hosted.sems)(*blocks)


def _everyone_hosted(terms):
    nt = len(terms)

    def copies(kind, ins, outs, sems):
        send_sems, recv_sems, local_sems = sems
        x, y, c, _ = _position()
        me = 4 * x + 2 * y + c
        if kind == "local":
            return [pltpu.make_async_copy(ins[t], outs[t].at[me], local_sems.at[t]) for t in range(nt)]
        made = []
        for t in range(nt):
            for k in range(1, 8):
                peer = (x ^ (k >> 2), y ^ ((k >> 1) & 1), c ^ (k & 1))
                slot = me if kind == "send" else me ^ k
                made.append(pltpu.make_async_remote_copy(
                    src_ref=ins[t], dst_ref=outs[t].at[slot], send_sem=send_sems.at[7 * t + k - 1],
                    recv_sem=recv_sems.at[7 * t + k - 1], device_id=peer, device_id_type=MESH))
        return made

    def first(ins, outs, sems):
        for cp in copies("local", ins, outs, sems) + copies("send", ins, outs, sems):
            cp.start()

    def last(ins, outs, sems):
        for cp in copies("landed", ins, outs, sems):
            cp.wait_recv()
        for cp in copies("send", ins, outs, sems):
            cp.wait_send()
        for cp in copies("local", ins, outs, sems):
            cp.wait()

    return _Hosted(terms, [jax.ShapeDtypeStruct((8,) + a.shape, F32) for a in terms],
                   [pltpu.SemaphoreType.DMA((7 * nt,))] * 2 + [pltpu.SemaphoreType.DMA((nt,))], first, None, last)


def _small_step(partials, extras, ws, ms, vs, shard_of):
    n = len(partials)
    terms = list(partials) + list(extras)
    nt = len(terms)
    rows = [t for t in range(nt) if terms[t].shape[0] == 1]
    mats = [t for t in range(nt) if terms[t].shape[0] != 1]
    row_block = (8, max(terms[t].shape[1] for t in rows))
    assert len(rows) <= row_block[0]
    vmem = pl.BlockSpec(memory_space=pltpu.VMEM)

    def pack(*refs):
        packed = refs[-1]
        packed[...] = jnp.zeros_like(packed)
        for i, t in enumerate(rows):
            packed[i:i + 1, 0:terms[t].shape[1]] = refs[i][...]

    packed = pl.pallas_call(pack, name="small_pack", in_specs=[vmem] * len(rows), out_specs=vmem,
                            out_shape=jax.ShapeDtypeStruct(row_block, F32))(*[terms[t] for t in rows])
    slots = _run_on_sequencer("allgather_small", _everyone_hosted([packed] + [terms[t] for t in mats]),
                              EVERYONE, 2)

    def body(*refs):
        slot_refs, refs = refs[:len(slots)], refs[len(slots):]
        w_refs, refs = refs[:n], refs[n:]
        m_refs, refs = refs[:n], refs[n:]
        v_refs, outs = refs[:n], refs[n:]
        sums = []
        for ref in slot_refs:
            g = ref[0]
            for dev in range(1, 8):
                g = g + ref[dev]
            sums.append(g)
        chip = 2 * lax.axis_index("x") + lax.axis_index("y")
        for t in range(nt):
            if t in rows:
                i = rows.index(t)
                g = sums[0][i:i + 1, 0:terms[t].shape[1]]
            else:
                g = sums[1 + mats.index(t)]
            if t >= n:
                outs[4 * n + t - n][...] = g
                continue
            if shard_of[t]:
                width = ws[t].shape[-1]
                mine = jnp.zeros(ws[t].shape, F32)
                for s in range(4):
                    mine = jnp.where(chip == s, g[:, s * width:(s + 1) * width], mine)
                g = mine
            delta, mn, vn = _adamw_math(w_refs[t][...], g, m_refs[t][...], v_refs[t][...])
            outs[4 * t][...] = g
            outs[4 * t + 1][...] = delta
            outs[4 * t + 2][...] = mn
            outs[4 * t + 3][...] = vn

    out_shapes = []
    for t in range(n):
        out_shapes += [jax.ShapeDtypeStruct(ws[t].shape, F32)] * 4
    out_shapes += [jax.ShapeDtypeStruct(a.shape, F32) for a in extras]
    res = pl.pallas_call(
        body, name="small_step",
        in_specs=[vmem] * (len(slots) + 3 * n), out_specs=[vmem] * len(out_shapes), out_shape=out_shapes,
    )(*slots, *ws, *ms, *vs)
    return [res[4 * t:4 * t + 4] for t in range(n)], res[4 * n:4 * n + nt - n]


def _adamw_math(w, g, m, v):
    m = ADAM_B1 * m + (1.0 - ADAM_B1) * g
    v = ADAM_B2 * v + (1.0 - ADAM_B2) * (g * g)
    m_hat = m / (1.0 - ADAM_B1 ** ADAM_STEP)
    v_hat = v / (1.0 - ADAM_B2 ** ADAM_STEP)
    delta = -ADAM_LR * (m_hat / (jnp.sqrt(v_hat) + ADAM_EPS) + ADAM_WD * w)
    return delta, m, v


def _row_tile(rows):
    return 256 if rows % 256 == 0 else rows


def _sum_partials(name, own, recv, chip, after):
    rows, cols = own.shape[1:]
    tr = _row_tile(rows)

    def body(chip_ref, own_ref, recv_ref, after_ref, o_ref):
        acc = own_ref[...]
        for j in range(3):
            acc = acc + recv_ref[j].astype(F32)
        o_ref[...] = acc

    return pl.pallas_call(
        body, name=name,
        grid_spec=pltpu.PrefetchScalarGridSpec(
            num_scalar_prefetch=1, grid=(rows // tr,),
            in_specs=[pl.BlockSpec((None, tr, cols), lambda i, chip_ref: (chip_ref[0], i, 0)),
                      pl.BlockSpec((3, tr, cols), lambda i, chip_ref: (0, i, 0)), ANY],
            out_specs=pl.BlockSpec((tr, cols), lambda i, chip_ref: (i, 0))),
        out_shape=jax.ShapeDtypeStruct((rows, cols), F32),
        compiler_params=_params(("parallel",)),
    )(chip.reshape(1).astype(jnp.int32), own, recv, after)


def _adamw(name, w, m, v, g_parts, hosted=None):
    rows, cols = w.shape
    tr = _row_tile(rows)
    n = len(g_parts)

    def body(w_ref, m_ref, v_ref, *refs):
        g_refs = refs[:n]
        go_ref, d_ref, mo_ref, vo_ref = refs[n:]
        g = g_refs[0][...]
        for r in g_refs[1:]:
            g = g + r[...]
        delta, mn, vn = _adamw_math(w_ref[...], g, m_ref[...], v_ref[...])
        go_ref[...] = g
        d_ref[...] = delta
        mo_ref[...] = mn
        vo_ref[...] = vn

    spec = pl.BlockSpec((tr, cols), lambda i: (i, 0))
    return _call(
        body, name=name, grid=(rows // tr,), in_specs=[spec] * (3 + n), out_specs=[spec] * 4,
        out_shape=[jax.ShapeDtypeStruct((rows, cols), F32)] * 4, sem=("parallel",), hosted=hosted,
        args=(w, m, v, *g_parts))


def _local_step(x, target, ga, wa_in, rel_bias, later_shards, gk, t5, gb, sinks, gf):
    s, d = x.shape
    tm = min(TM_DENSE, s)
    nt = s // tm
    half = d // 2
    row = pl.BlockSpec((tm, d), lambda i: (i, 0))
    whole = lambda shape: pl.BlockSpec(shape, lambda *_: (0,) * len(shape))

    n1, = _norm_fwd("norm_a", x, ga)
    zqkv = _matmul("proj_a_qkv", n1, wa_in, dims=NN, grid=(3, nt + 1), zero_axis=1,
                   a_spec=pl.BlockSpec((tm, d), lambda j, i: (jnp.maximum(i - 1, 0), 0)),
                   b_spec=pl.BlockSpec((None, d, d), lambda j, i: (j, 0, 0)),
                   o_spec=pl.BlockSpec((None, tm, d), lambda j, i: (j, i, 0)),
                   out_shape=(3, tm + s, d), out_dtype=BF16)
    gate_a = _matmul("proj_a_gate", n1, wa_in, dims=NN, grid=(nt,),
                     a_spec=row, b_spec=pl.BlockSpec((None, d, d), lambda i: (3, 0, 0)), o_spec=row,
                     out_shape=(s, d), out_dtype=F32)
    onehot_a = _a_offset_onehot()
    diag_a = _diag_rows(onehot_a, rel_bias)
    (o_a, u_a, lse_a), gathered = _attn_a_fwd(zqkv, gate_a, diag_a, hosted=_allgather_routed(later_shards))
    wa_out, wkv, wb_in, wb_out, wkv_x = gathered
    wa_out = wa_out.reshape(d, d)
    wkv = wkv.reshape(d, -1)
    wkv_x = wkv_x.reshape(d, B_KVX)
    wb_out = wb_out.reshape(d, d)
    h1, nk, n2 = _out_norms("out_a_norms", u_a, wa_out, x, jnp.concatenate([gk, gb], axis=0))
    kvw = wkv.shape[1]
    kvx =_matmul("proj_kv", nk, wkv_x, dims=NN, grid=(nt + 1,), zero_axis=0,
                  a_spec=pl.BlockSpec((tm, d), lambda i: (jnp.maximum(i - 1, 0), 0)), b_spec=whole((d, B_KVX)),
                  o_spec=pl.BlockSpec((tm, B_KVX), lambda i: (i, 0)), out_shape=(tm + s, B_KVX), out_dtype=BF16)
    qb = _matmul("proj_b_q", n2, wb_in, dims=NN, grid=(2, nt),
                 a_spec=pl.BlockSpec((tm, d), lambda j, i: (i, 0)),
                 b_spec=pl.BlockSpec((None, d, half), lambda j, i: (j, 0, 0)),
                 o_spec=pl.BlockSpec((tm, half), lambda j, i: (i, j)), out_shape=(s, d), out_dtype=BF16)
    gate_b = _matmul("proj_b_gate", n2, wb_in, dims=NN, grid=(2, nt),
                     a_spec=pl.BlockSpec((tm, d), lambda j, i: (i, 0)),
                     b_spec=pl.BlockSpec((None, d, half), lambda j, i: (2 + j, 0, 0)),
                     o_spec=pl.BlockSpec((tm, half), lambda j, i: (i, j)), out_shape=(s, d), out_dtype=F32)
    onehot_b = _b_offset_onehot()
    base_b = jnp.roll(_diag_rows(onehot_b, t5)[..., ::-1], TQ, axis=-1)
    o_b, u_b, lse_b = _attn_b_fwd(qb, kvx, gate_b, base_b, sinks)
    dh2, loss, d_gf = _out_loss_head(u_b, wb_out, h1, target, gf)

    du_b = _matmul("dout_b", dh2, wb_out, dims=NT, grid=(nt,), a_spec=row, b_spec=whole((d, d)), o_spec=row,
                   out_shape=(s, d), out_dtype=F32)
    d_wb_out = _matmul("dw_out_b", u_b, dh2, dims=TN, grid=(2,),
                       a_spec=whole((s, d)), b_spec=pl.BlockSpec((s, half), lambda j: (0, j)),
                       o_spec=pl.BlockSpec((d, half), lambda j: (0, j)),
                       out_shape=(d, d), out_dtype=F32, also_bf16=True)
    dz_b, dkv, dsum_b, dsinks = _attn_b_bwd(qb, kvx, gate_b, o_b, du_b, lse_b, base_b, sinks)
    ddiag_b = jnp.roll(dsum_b[..., ::-1], -1, axis=-1)
    d_wb_in = _matmul("dw_in_b", n2, dz_b, dims=TN, grid=(4,),
                      a_spec=whole((s, d)), b_spec=pl.BlockSpec((None, s, half), lambda j: (j, 0, 0)),
                      o_spec=pl.BlockSpec((None, d, half), lambda j: (j, 0, 0)),
                      out_shape=(4, d, half), out_dtype=F32, also_bf16=True)
    d_wkv = _matmul("dw_kv", nk, dkv, dims=TN, grid=(1,),
                    a_spec=whole((s, d)), b_spec=whole((s, kvw)), o_spec=whole((d, kvw)),
                    out_shape=(d, kvw), out_dtype=F32, also_bf16=True)
    dh1, d_gkb = _proj_norm_bwd("dproj_kv_b", h1, dh2, jnp.concatenate([gk, gb], axis=0),
                                [(dkv[None], wkv[None]), (dz_b, wb_in)])

    du_a = _matmul("dout_a", dh1, wa_out, dims=NT, grid=(nt,), a_spec=row, b_spec=whole((d, d)), o_spec=row,
                   out_shape=(s, d), out_dtype=F32)
    d_wa_out = _matmul("dw_out_a", u_a, dh1, dims=TN, grid=(2,),
                       a_spec=whole((s, d)), b_spec=pl.BlockSpec((s, half), lambda j: (0, j)),
                       o_spec=pl.BlockSpec((d, half), lambda j: (0, j)),
                       out_shape=(d, d), out_dtype=F32, also_bf16=True)
    early = dict(a_w_out=[g.reshape(4, d // 4, d) for g in d_wa_out],
                 kv_w=[g.reshape(4, d // 4, kvw) for g in d_wkv], b_w_in=list(d_wb_in),
                 b_w_out=[g.reshape(4, d // 4, d) for g in d_wb_out])
    (dz_a, ddiag_a), early_recv = _attn_a_bwd(
        zqkv, gate_a, o_a, du_a, lse_a, diag_a, hosted=_scatter_hosted([early[n][1] for n in early]))
    d_wa_in = _matmul("dw_in_a", n1, dz_a, dims=TN, grid=(4, 2),
                      a_spec=whole((s, d)), b_spec=pl.BlockSpec((None, s, half), lambda j, h: (j, 0, h)),
                      o_spec=pl.BlockSpec((None, d, half), lambda j, h: (j, 0, h)),
                      out_shape=(4, d, d), out_dtype=F32, also_bf16=True)
    late_recv = _run_on_sequencer("scatter_a_w_in", _scatter_hosted([d_wa_in[1]]), SCATTER_PEERS, 0)
    grad_x, d_ga = _proj_norm_bwd("dproj_a", x, dh1, ga, [(dz_a, wa_in)])

    small = dict(a_norm=d_ga, kv_norm=d_gkb[0:1], b_norm=d_gkb[1:2], b_sinks=dsinks[0:1, :HEADS], final_norm=d_gf)
    small["by_offset"] = dict(a_rel_bias=(onehot_a, ddiag_a.reshape(HEADS, -1)),
                              t5_bias=(onehot_b, ddiag_b.reshape(HEADS, -1)))
    own = dict(a_w_in=d_wa_in[0], **{n: early[n][0] for n in early})
    received = dict(a_w_in=late_recv[0], **dict(zip(early, early_recv)))
    return loss, grad_x, small, own, received, d_wa_in[1]


SMALL = ("a_norm", "kv_norm", "b_norm", "b_sinks", "final_norm")
TABLES = ("a_rel_bias", "t5_bias")
BIG = ("a_w_in", "a_w_out", "kv_w", "b_w_in", "b_w_out")
ORDER = ("a_norm", "a_w_in", "a_rel_bias", "a_w_out", "kv_norm", "kv_w", "t5_bias", "b_norm", "b_w_in",
         "b_sinks", "b_w_out", "final_norm")


def kernel(x, a_norm, a_w_in, a_rel_bias, a_w_out, kv_norm, kv_w, t5_bias, b_norm, b_w_in, b_sinks, b_w_out, final_norm, loss_target, m_a_norm, m_a_w_in, m_a_rel_bias, m_a_w_out, m_kv_norm, m_kv_w, m_t5_bias, m_b_norm, m_b_w_in, m_b_sinks, m_b_w_out, m_final_norm, v_a_norm, v_a_w_in, v_a_rel_bias, v_a_w_out, v_kv_norm, v_kv_w, v_t5_bias, v_b_norm, v_b_w_in, v_b_sinks, v_b_w_out, v_final_norm):
    w = dict(a_norm=a_norm, a_w_in=a_w_in, a_rel_bias=a_rel_bias, a_w_out=a_w_out, kv_norm=kv_norm, kv_w=kv_w,
             t5_bias=t5_bias, b_norm=b_norm, b_w_in=b_w_in, b_sinks=b_sinks, b_w_out=b_w_out,
             final_norm=final_norm)
    m = dict(a_norm=m_a_norm, a_w_in=m_a_w_in, a_rel_bias=m_a_rel_bias, a_w_out=m_a_w_out, kv_norm=m_kv_norm,
             kv_w=m_kv_w, t5_bias=m_t5_bias, b_norm=m_b_norm, b_w_in=m_b_w_in, b_sinks=m_b_sinks,
             b_w_out=m_b_w_out, final_norm=m_final_norm)
    v = dict(a_norm=v_a_norm, a_w_in=v_a_w_in, a_rel_bias=v_a_rel_bias, a_w_out=v_a_w_out, kv_norm=v_kv_norm,
             kv_w=v_kv_w, t5_bias=v_t5_bias, b_norm=v_b_norm, b_w_in=v_b_w_in, b_sinks=v_b_sinks,
             b_w_out=v_b_w_out, final_norm=v_final_norm)
    d = D_MODEL
    chip = 2 * lax.axis_index("x") + lax.axis_index("y")

    shard2d = dict(a_w_in=a_w_in[0], a_w_out=a_w_out[0], kv_w=kv_w, b_w_in=b_w_in[0], b_w_out=b_w_out[0])

    wa_in, = _run_on_sequencer("allgather_first", _allgather_routed([shard2d["a_w_in"].astype(BF16)]),
                               GATHER_PEERS, 1)
    ga = _gather_gain(a_norm).reshape(1, d)

    later = [shard2d[n].astype(BF16) for n in BIG[1:]]
    kv_shard = later[BIG[1:].index("kv_w")]
    later.append(jnp.concatenate(
        [kv_shard[:, (i // 2) * HEAD_DIM:(i // 2 + 1) * HEAD_DIM] for i in range(B_KVX // HEAD_DIM)], axis=1))
    loss, grad_x, small, own, received, after_attention = _local_step(
        x[0], loss_target[0], ga, wa_in, a_rel_bias[0], later,
        kv_norm.reshape(1, d), t5_bias, b_norm, b_sinks, final_norm.reshape(1, d))

    out = {}
    as2d = lambda a: a.reshape(-1, a.shape[-1])
    small_res, (loss_sum, *offset_sums) = _small_step(
        [small[n] for n in SMALL], [loss] + [small["by_offset"][n][1] for n in TABLES],
        [as2d(w[n]) for n in SMALL], [as2d(m[n]) for n in SMALL], [as2d(v[n]) for n in SMALL],
        [n == "a_norm" for n in SMALL])
    for n, res in zip(SMALL, small_res):
        out[n] = [r.reshape(w[n].shape) for r in res]
    loss_out = loss_sum.reshape(())
    for n, summed in zip(TABLES, offset_sums):
        grad = _diag_rows_grad(small["by_offset"][n][0], summed)
        res, _ = _adamw("adamw_" + n, as2d(w[n]).T, as2d(m[n]).T, as2d(v[n]).T, [grad])
        out[n] = [r.T.reshape(w[n].shape) for r in res]

    core_sums = {n: _sum_partials("sum_" + n, own[n], received[n], chip, after_attention) for n in BIG}

    alone = ("a_w_out", "kv_w", "b_w_out")
    sibling_sums = dict(zip(alone, _swap_with_sibling("swap_early", [core_sums[n] for n in alone])))
    for n, rides_along in (("b_w_out", "b_w_in"), ("a_w_out", None), ("kv_w", None), ("b_w_in", "a_w_in"),
                           ("a_w_in", None)):
        res, swapped = _adamw(
            "adamw_" + n, shard2d[n], m[n].reshape(shard2d[n].shape), v[n].reshape(shard2d[n].shape),
            [core_sums[n], sibling_sums[n]],
            hosted=None if rides_along is None else _swap_hosted([core_sums[rides_along]]))
        if rides_along is not None:
            sibling_sums[rides_along], = swapped
        out[n] = [r.reshape(w[n].shape) for r in res]

    grads = [out[n][0] for n in ORDER]
    deltas = [out[n][1] for n in ORDER]
    new_m = [out[n][2] for n in ORDER]
    new_v = [out[n][3] for n in ORDER]
    return (loss_out, grad_x[None], *grads, *deltas, *new_m, *new_v)
```

```python
import math

import jax
import jax.numpy as jnp
import numpy as np
from jax import lax
from jax.experimental import pallas as pl
from jax.experimental.pallas import tpu as pltpu
from jax.experimental.pallas import tpu_sc as plsc

F32 = jnp.float32
BF16 = jnp.bfloat16
MESH = pl.DeviceIdType.MESH

D_MODEL = 1024
HEADS = 16
HEAD_DIM = 64
CHUNK = 64
RMS_EPS = 1e-6
SCALE = HEAD_DIM ** -0.5
A_LEFT_CHUNKS = 8
A_REL_CLIP = 256
B_LEFT_CHUNKS = 2
B_KV_HEADS = 2
B_GROUP = HEADS // B_KV_HEADS
T5_BUCKETS = 32
T5_MAX_DIST = 128
ADAM_LR = 0.001
ADAM_B1 = 0.9
ADAM_B2 = 0.999
ADAM_EPS = 1e-08
ADAM_WD = 0.01
ADAM_STEP = 10

MASKED = -1e30
LANES = 128
TQ = 128
A_PAIRS = 2
A_PAIRS_FWD = 4
KB = 128
A_KBLOCKS = A_LEFT_CHUNKS * CHUNK // KB + 1
B_KBLOCKS = B_LEFT_CHUNKS * CHUNK // KB + 1
A_WIN = A_KBLOCKS * KB
B_WIN = B_KBLOCKS * KB
TM = 512
TM_DENSE = 1024
TM_PARTS = 512
VMEM_LIMIT = 56 * 1024 * 1024

NT = (((1,), (1,)), ((), ()))
TN = (((0,), (0,)), ((), ()))
NN = (((1,), (0,)), ((), ()))


def _params(sem=None):
    return pltpu.CompilerParams(dimension_semantics=sem, vmem_limit_bytes=VMEM_LIMIT)


class _Hosted:
    def __init__(self, inputs, out_shapes, sems, first, middle, last):
        self.inputs, self.out_shapes, self.sems = list(inputs), list(out_shapes), list(sems)
        self.first, self.middle, self.last = first, middle, last


def _call(body, *, name, grid, in_specs, out_specs, out_shape, args, scratch_shapes=(), sem=None, hosted=None,
          aliases=None):
    in_specs, out_specs, out_shape = list(in_specs), list(out_specs), list(out_shape)
    scratch_shapes = list(scratch_shapes)
    if hosted is None:
        out = pl.pallas_call(
            body, name=name, grid=grid, in_specs=in_specs, out_specs=out_specs, out_shape=out_shape,
            scratch_shapes=scratch_shapes, input_output_aliases=aliases or {},
            compiler_params=_params(sem))(*args)
        return list(out), []
    assert aliases is None
    n_in, n_out, n_scr = len(in_specs), len(out_shape), len(scratch_shapes)
    h_in, h_out = len(hosted.inputs), len(hosted.out_shapes)
    total = int(np.prod(grid)) if grid else 1

    def wrapped(*refs):
        ins, refs = refs[:n_in], refs[n_in:]
        h_ins, refs = refs[:h_in], refs[h_in:]
        outs, refs = refs[:n_out], refs[n_out:]
        h_outs, refs = refs[:h_out], refs[h_out:]
        scr, h_sems = refs[:n_scr], refs[n_scr:]
        step = 0
        for axis, size in enumerate(grid):
            step = step * size + pl.program_id(axis)

        if hosted.first is not None:
            @pl.when(step == 0)
            def _():
                hosted.first(h_ins, h_outs, h_sems)

        body(*ins, *outs, *scr)
        if hosted.middle is not None:
            @pl.when(step == total // 2)
            def _():
                hosted.middle(h_ins, h_outs, h_sems)

        if hosted.last is not None:
            @pl.when(step == total - 1)
            def _():
                hosted.last(h_ins, h_outs, h_sems)

    out = pl.pallas_call(
        wrapped, name=name, grid=grid, in_specs=in_specs + [ANY] * h_in, out_specs=out_specs + [ANY] * h_out,
        out_shape=out_shape + hosted.out_shapes, scratch_shapes=scratch_shapes + hosted.sems,
        compiler_params=_params(("arbitrary",) * len(grid)))(*args, *hosted.inputs)
    return list(out[:n_out]), list(out[n_out:])


def _matmul(name, a, b, *, dims, grid, a_spec, b_spec, o_spec, out_shape, out_dtype,
            also_bf16=False, zero_axis=None, into=None):
    def body(*refs):
        if into is not None:
            refs = refs[:2] + refs[3:]
        if zero_axis is None:
            product(*refs)
        else:
            @pl.when(pl.program_id(zero_axis) == 0)
            def _():
                refs[2][...] = jnp.zeros_like(refs[2])

            @pl.when(pl.program_id(zero_axis) > 0)
            def _():
                product(*refs)

    def product(a_ref, b_ref, o_ref, *more):
        prod = lax.dot_general(a_ref[...].astype(BF16), b_ref[...].astype(BF16), dims,
                               preferred_element_type=F32)
        o_ref[...] = prod.astype(out_dtype)
        if also_bf16:
            more[0][...] = prod.astype(BF16)

    out_specs = [o_spec]
    out_shapes = [jax.ShapeDtypeStruct(out_shape, out_dtype)]
    if also_bf16:
        out_specs.append(o_spec)
        out_shapes.append(jax.ShapeDtypeStruct(out_shape, BF16))
    out, _ = _call(body, name=name, grid=grid, in_specs=[a_spec, b_spec] + ([] if into is None else [ANY]),
                   out_specs=out_specs, out_shape=out_shapes, args=[a, b] + ([] if into is None else [into]),
                   sem=("parallel",) * len(grid), aliases=None if into is None else {2: 0})
    return out[0] if not also_bf16 else tuple(out)


def _rms_rows(x):
    return lax.rsqrt(jnp.mean(x * x, axis=-1, keepdims=True) + RMS_EPS)


def _norm_fwd(name, x, gains):
    s, d = x.shape
    n = gains.shape[0]

    def body(x_ref, g_ref, *o_refs):
        xv = x_ref[...]
        xh = xv * _rms_rows(xv)
        for i in range(n):
            o_refs[i][...] = (xh * g_ref[i:i + 1, :]).astype(BF16)

    row = pl.BlockSpec((TM, d), lambda i: (i, 0))
    return pl.pallas_call(
        body, name=name, grid=(s // TM,),
        in_specs=[row, pl.BlockSpec((n, d), lambda i: (0, 0))],
        out_specs=[row] * n,
        out_shape=[jax.ShapeDtypeStruct((s, d), BF16)] * n,
        compiler_params=_params(("parallel",)),
    )(x, gains)


def _proj_norm_bwd(name, x, dres, gains, branches):
    s, d = x.shape
    n = len(branches)
    n_ab = 2 * sum(len(bs) for _, bs in branches)
    tm = min(TM_PARTS, s)

    def body(x_ref, r_ref, g_ref, *refs):
        ab_refs, dx_ref, dg_ref = list(refs[:n_ab]), refs[n_ab], refs[n_ab + 1]
        i = pl.program_id(0)
        xv = x_ref[...]
        r = _rms_rows(xv)
        xh = xv * r

        @pl.when(i == 0)
        def _():
            dg_ref[...] = jnp.zeros_like(dg_ref)

        a = None
        for j in range(n):
            dn = None
            for _ in branches[j][1]:
                a_ref, b_ref = ab_refs.pop(0), ab_refs.pop(0)
                for part in range(a_ref.shape[0]):
                    term = lax.dot_general(a_ref[part], b_ref[part], NT, preferred_element_type=F32)
                    dn = term if dn is None else dn + term
            t = dn * g_ref[j:j + 1, :]
            a = t if a is None else a + t
            dg_ref[j:j + 1, :] += jnp.sum(dn * xh, axis=0, keepdims=True)
        dx_ref[...] = r_ref[...] + r * (a - xh * jnp.mean(xh * a, axis=-1, keepdims=True))

    row = pl.BlockSpec((tm, d), lambda i: (i, 0))
    small = pl.BlockSpec((n, d), lambda i: (0, 0))
    ab_specs, ab_args = [], []
    for a, bs in branches:
        for k, b in enumerate(bs):
            ab_specs += [pl.BlockSpec((a.shape[0], tm, b.shape[2]), lambda i, k=k: (0, i, k)),
                         pl.BlockSpec(b.shape, lambda i: (0, 0, 0))]
            ab_args += [a, b]
    return pl.pallas_call(
        body, name=name, grid=(s // tm,),
        in_specs=[row, row, small] + ab_specs,
        out_specs=[row, small],
        out_shape=[jax.ShapeDtypeStruct((s, d), F32), jax.ShapeDtypeStruct((n, d), F32)],
        compiler_params=_params(("arbitrary",)),
    )(x, dres, gains, *ab_args)


def _out_norms(name, u, w_out, resid, gains):
    s, d = resid.shape
    n = gains.shape[0]
    tm = min(TM_DENSE, s)

    def body(u_ref, w_ref, r_ref, g_ref, h_ref, *o_refs):
        hv = r_ref[...] + jnp.dot(u_ref[...], w_ref[...], preferred_element_type=F32)
        h_ref[...] = hv
        hh = hv * _rms_rows(hv)
        for i in range(n):
            o_refs[i][...] = (hh * g_ref[i:i + 1, :]).astype(BF16)

    row = pl.BlockSpec((tm, d), lambda i: (i, 0))
    return pl.pallas_call(
        body, name=name, grid=(s // tm,),
        in_specs=[row, pl.BlockSpec((d, d), lambda i: (0, 0)), row, pl.BlockSpec((n, d), lambda i: (0, 0))],
        out_specs=[row] * (n + 1),
        out_shape=[jax.ShapeDtypeStruct((s, d), F32)] + [jax.ShapeDtypeStruct((s, d), BF16)] * n,
        compiler_params=_params(("parallel",)),
    )(u, w_out, resid, gains)


def _out_loss_head(u, w_out, resid, target, gain):
    s, d = resid.shape
    tm = min(TM_PARTS, s)

    def body(u_ref, w_ref, r_ref, t_ref, g_ref, dh_ref, loss_ref, dg_ref):
        i = pl.program_id(0)
        hv = r_ref[...] + jnp.dot(u_ref[...], w_ref[...], preferred_element_type=F32)
        r = _rms_rows(hv)
        hh = hv * r
        g = g_ref[...]
        err = hh * g - t_ref[...]
        part = 0.5 * jnp.sum(jnp.sum(err * err, axis=-1, keepdims=True) * (1.0 / d), axis=0, keepdims=True)
        dy = err * (1.0 / d)
        a = dy * g
        dh_ref[...] = r * (a - hh * jnp.mean(hh * a, axis=-1, keepdims=True))
        dg = jnp.sum(dy * hh, axis=0, keepdims=True)

        @pl.when(i == 0)
        def _():
            loss_ref[...] = part
            dg_ref[...] = dg

        @pl.when(i > 0)
        def _():
            loss_ref[...] += part
            dg_ref[...] += dg

    row = pl.BlockSpec((tm, d), lambda i: (i, 0))
    return pl.pallas_call(
        body, name="out_b_loss_head", grid=(s // tm,),
        in_specs=[row, pl.BlockSpec((d, d), lambda i: (0, 0)), row, row, pl.BlockSpec((1, d), lambda i: (0, 0))],
        out_specs=[row, pl.BlockSpec((1, 1), lambda i: (0, 0)), pl.BlockSpec((1, d), lambda i: (0, 0))],
        out_shape=[jax.ShapeDtypeStruct((s, d), F32), jax.ShapeDtypeStruct((1, 1), F32),
                   jax.ShapeDtypeStruct((1, d), F32)],
        compiler_params=_params(("arbitrary",)),
    )(u, w_out, resid, target, gain)


def _silu_parts(g):
    sig = jax.nn.sigmoid(g)
    return g * sig, sig * (1.0 + g * (1.0 - sig))


def _lane_lo(rows):
    return lax.broadcasted_iota(jnp.int32, (rows, LANES), 1) < HEAD_DIM


def _stack_pair(x):
    lo = _lane_lo(x.shape[0])
    zero = jnp.zeros_like(x)
    return jnp.concatenate([jnp.where(lo, x, zero), jnp.where(lo, zero, x)], axis=0)


def _unstack_pair(y, w):
    return jnp.where(_lane_lo(w), y[:w], y[w:])


def _block_valid(b, left_blocks, width):
    col = lax.broadcasted_iota(jnp.int32, (1, 2 * width), 1)
    col = jnp.where(col >= width, col - width, col)
    return (col // KB + (b - left_blocks)) >= 0


def _toeplitz_tile(diag_row, width, left_chunks):
    wide = width + TQ
    rolled = pltpu.roll(jnp.broadcast_to(diag_row, (TQ, wide)), 1, 1, stride=1, stride_axis=0)
    i = lax.broadcasted_iota(jnp.int32, (TQ, width), 0) // CHUNK
    j = lax.broadcasted_iota(jnp.int32, (TQ, width), 1) // CHUNK
    dc = i + left_chunks - j
    return jnp.where((dc >= 0) & (dc <= left_chunks), rolled[:, TQ:], MASKED)


def _toeplitz_sum(tile, width):
    flip = (lax.broadcasted_iota(jnp.int32, (TQ, TQ), 0) + lax.broadcasted_iota(jnp.int32, (TQ, TQ), 1)
            == TQ - 1).astype(F32)
    reversed_rows = jnp.dot(flip, tile, precision=lax.Precision.HIGHEST, preferred_element_type=F32)
    padded = jnp.concatenate([reversed_rows, jnp.zeros((TQ, TQ), F32)], axis=1)
    rolled = pltpu.roll(padded, 0, 1, stride=1, stride_axis=0)
    return jnp.sum(rolled, axis=0, keepdims=True)


def _softmax_pair(sc, w, sink=None):
    ps, inv, lses = [], [], []
    for e in range(2):
        sh = sc[:, e * w:(e + 1) * w]
        m = jnp.max(sh, axis=-1, keepdims=True)
        if sink is not None:
            m = jnp.maximum(m, sink[e])
        ex = jnp.exp(sh - m)
        l = jnp.sum(ex, axis=-1, keepdims=True)
        if sink is not None:
            l = l + jnp.exp(sink[e] - m)
        ps.append(ex.astype(BF16))
        inv.append(1.0 / l)
        lses.append(m + jnp.log(l))
    return jnp.concatenate(ps, axis=-1), inv, lses


def _softmax_pair_bwd(sc, dp, lse, delta, w):
    ps, dss = [], []
    for e in range(2):
        p = jnp.exp(sc[:, e * w:(e + 1) * w] - lse[e])
        ps.append(p)
        dss.append(p * (dp[:, e * w:(e + 1) * w] - delta[e]))
    return jnp.concatenate(ps, axis=-1), jnp.concatenate(dss, axis=-1)


def _pair_rowsums(x, lo):
    zero = jnp.zeros_like(x)
    return (jnp.sum(jnp.where(lo, x, zero), axis=-1, keepdims=True),
            jnp.sum(jnp.where(lo, zero, x), axis=-1, keepdims=True))


def _a_qkv_specs(rows, pad, pw):
    return [pl.BlockSpec((None, TQ, pw), lambda p, b: (0, b + pad // TQ, p)),
            pl.BlockSpec((None, rows, pw), lambda p, b: (1, 0, p)),
            pl.BlockSpec((None, rows, pw), lambda p, b: (2, 0, p))]


def _window(ref, b, pad, win, lanes):
    start = pl.multiple_of(b * TQ + pad - (win - TQ), KB)
    return ref[pl.ds(start, win), lanes]


def _attn_a_fwd(zqkv, g, diag, hosted=None):
    s = g.shape[0]
    pad = zqkv.shape[1] - s
    nb = s // TQ
    left = A_KBLOCKS - 1
    pairs = A_PAIRS_FWD
    pw = pairs * LANES
    wide = A_WIN + TQ

    def body(q_ref, k_ref, v_ref, g_ref, diag_ref, o_ref, u_ref, lse_ref, bias_scr):
        b = pl.program_id(1)

        @pl.when(b == 0)
        def _():
            for hh in range(2 * pairs):
                bias_scr[hh // 2, :, (hh % 2) * A_WIN:(hh % 2 + 1) * A_WIN] = _toeplitz_tile(
                    diag_ref[hh], A_WIN, A_LEFT_CHUNKS)

        def step(first_blocks):
            lo = _lane_lo(TQ)
            for pp in range(pairs):
                ln = slice(pp * LANES, (pp + 1) * LANES)
                kcat = _stack_pair(_window(k_ref, b, pad, A_WIN, ln))
                vcat = _stack_pair(_window(v_ref, b, pad, A_WIN, ln))
                sc = lax.dot_general(q_ref[:, ln] * SCALE, kcat, NT, preferred_element_type=F32) + bias_scr[pp]
                if first_blocks:
                    sc = jnp.where(_block_valid(b, left, A_WIN), sc, MASKED)
                p, inv, lses = _softmax_pair(sc, A_WIN)
                ov = jnp.dot(p, vcat, preferred_element_type=F32) * jnp.where(lo, inv[0], inv[1])
                o_ref[:, ln] = ov
                lse_ref[pp] = jnp.where(lo, lses[0], lses[1])
                sg, _ = _silu_parts(g_ref[:, ln])
                u_ref[:, ln] = (ov * sg).astype(BF16)

        @pl.when(b < left)
        def _():
            step(True)

        @pl.when(b >= left)
        def _():
            step(False)

    tile = pl.BlockSpec((TQ, pw), lambda p, b: (b, p))
    return _call(
        body, name="attn_a_fwd", grid=(HEADS // 2 // pairs, nb),
        in_specs=_a_qkv_specs(pad + s, pad, pw) + [
            tile, pl.BlockSpec((2 * pairs, 1, wide), lambda p, b: (p, 0, 0))],
        out_specs=[tile, tile, pl.BlockSpec((pairs, TQ, LANES), lambda p, b: (p, b, 0))],
        out_shape=[jax.ShapeDtypeStruct((s, D_MODEL), F32), jax.ShapeDtypeStruct((s, D_MODEL), BF16),
                   jax.ShapeDtypeStruct((HEADS // 2, s, LANES), F32)],
        scratch_shapes=[pltpu.VMEM((pairs, TQ, 2 * A_WIN), F32)],
        sem=("parallel", "arbitrary"), hosted=hosted,
        args=(zqkv, zqkv, zqkv, g, diag))


def _attn_a_bwd(zqkv, g, o, du, lse, diag, hosted=None):
    s = g.shape[0]
    pad = zqkv.shape[1] - s
    nb = s // TQ
    left = A_KBLOCKS - 1
    pw = A_PAIRS * LANES
    wide = A_WIN + TQ

    def body(q_ref, k_ref, v_ref, g_ref, o_ref, du_ref, lse_ref, diag_ref, dz_ref, ddiag_ref,
             bias_scr, dbias_acc, dk_acc, dv_acc):
        b = pl.program_id(1)

        @pl.when(b == 0)
        def _():
            for hh in range(2 * A_PAIRS):
                bias_scr[hh // 2, :, (hh % 2) * A_WIN:(hh % 2 + 1) * A_WIN] = _toeplitz_tile(
                    diag_ref[hh], A_WIN, A_LEFT_CHUNKS)
            dbias_acc[...] = jnp.zeros_like(dbias_acc)
            dk_acc[...] = jnp.zeros_like(dk_acc)
            dv_acc[...] = jnp.zeros_like(dv_acc)

        def step(first_blocks):
            lo = _lane_lo(TQ)
            rows = pl.ds(pl.multiple_of(b * TQ, TQ), TQ)
            sg, dsg = _silu_parts(g_ref[...])
            duv = du_ref[...]
            ov = o_ref[...]
            do = duv * sg
            dz_ref[3, rows, :] = (duv * ov * dsg).astype(BF16)
            do_o = do * ov
            do_bf = do.astype(BF16)
            for pp in range(A_PAIRS):
                ln = slice(pp * LANES, (pp + 1) * LANES)
                q = q_ref[:, ln] * SCALE
                kcat = _stack_pair(_window(k_ref, b, pad, A_WIN, ln))
                vcat = _stack_pair(_window(v_ref, b, pad, A_WIN, ln))
                sc = lax.dot_general(q, kcat, NT, preferred_element_type=F32) + bias_scr[pp]
                if first_blocks:
                    sc = jnp.where(_block_valid(b, left, A_WIN), sc, MASKED)
                lse_t = lse_ref[pp]
                dp = lax.dot_general(do_bf[:, ln], vcat, NT, preferred_element_type=F32)
                p, ds = _softmax_pair_bwd(sc, dp, (lse_t[:, 0:1], lse_t[:, HEAD_DIM:HEAD_DIM + 1]),
                                          _pair_rowsums(do_o[:, ln], lo), A_WIN)
                dbias_acc[pp] += ds
                dsb = ds.astype(BF16)
                dz_ref[0, rows, ln] = (jnp.dot(dsb, kcat, preferred_element_type=F32) * SCALE).astype(BF16)
                pb = p.astype(BF16)
                dob = do_bf[:, ln]
                dkt = jnp.concatenate([
                    lax.dot_general(q[:, e * HEAD_DIM:(e + 1) * HEAD_DIM], dsb[:, e * A_WIN:(e + 1) * A_WIN], TN,
                                    preferred_element_type=F32) for e in range(2)], axis=0)
                dvt = jnp.concatenate([
                    lax.dot_general(dob[:, e * HEAD_DIM:(e + 1) * HEAD_DIM], pb[:, e * A_WIN:(e + 1) * A_WIN], TN,
                                    preferred_element_type=F32) for e in range(2)], axis=0)
                for t in range(A_KBLOCKS):
                    blk = b + (pad // KB - left + t)
                    dk_acc[blk, ln, :] += dkt[:, t * KB:(t + 1) * KB]
                    dv_acc[blk, ln, :] += dvt[:, t * KB:(t + 1) * KB]

        @pl.when(b < left)
        def _():
            step(True)

        @pl.when(b >= left)
        def _():
            step(False)

        @pl.when(b == nb - 1)
        def _():
            for kb in range(s // KB):
                dz_ref[1, kb * KB:(kb + 1) * KB, :] = dk_acc[pad // KB + kb].T.astype(BF16)
                dz_ref[2, kb * KB:(kb + 1) * KB, :] = dv_acc[pad // KB + kb].T.astype(BF16)
            for hh in range(2 * A_PAIRS):
                ddiag_ref[hh] = _toeplitz_sum(
                    dbias_acc[hh // 2, :, (hh % 2) * A_WIN:(hh % 2 + 1) * A_WIN], A_WIN)

    tile = pl.BlockSpec((TQ, pw), lambda p, b: (b, p))
    diag_spec = pl.BlockSpec((2 * A_PAIRS, 1, wide), lambda p, b: (p, 0, 0))
    return _call(
        body, name="attn_a_bwd", grid=(HEADS // 2 // A_PAIRS, nb),
        in_specs=_a_qkv_specs(pad + s, pad, pw) + [
            tile, tile, tile, pl.BlockSpec((A_PAIRS, TQ, LANES), lambda p, b: (p, b, 0)), diag_spec],
        out_specs=[pl.BlockSpec((4, s, pw), lambda p, b: (0, 0, p)), diag_spec],
        out_shape=[jax.ShapeDtypeStruct((4, s, D_MODEL), BF16),
                   jax.ShapeDtypeStruct((HEADS, 1, wide), F32)],
        scratch_shapes=[pltpu.VMEM((A_PAIRS, TQ, 2 * A_WIN), F32), pltpu.VMEM((A_PAIRS, TQ, 2 * A_WIN), F32),
                        pltpu.VMEM(((pad + s) // KB, pw, KB), F32), pltpu.VMEM(((pad + s) // KB, pw, KB), F32)],
        sem=("parallel", "arbitrary"), hosted=hosted,
        args=(zqkv, zqkv, zqkv, g, o, du, lse, diag))


B_STACK = B_GROUP // 2
B_KVX = 4 * LANES
B_ROWS = B_STACK * TQ
B_WIDE = B_WIN + TQ


def _b_head_place(h):
    return h // B_GROUP, (h % B_GROUP) // 2, h % 2


def _toeplitz_tile_t(base_row, width, left_chunks):
    wide = width + TQ
    rolled = pltpu.roll(jnp.broadcast_to(base_row, (width, wide)), 0, 1, stride=1, stride_axis=0)
    j = lax.broadcasted_iota(jnp.int32, (width, TQ), 0) // CHUNK
    i = lax.broadcasted_iota(jnp.int32, (width, TQ), 1) // CHUNK
    dc = i + left_chunks - j
    return jnp.where((dc >= 0) & (dc <= left_chunks), rolled[:, :TQ], MASKED)


def _toeplitz_sum_t(tile_t, width):
    flip = (lax.broadcasted_iota(jnp.int32, (width, width), 0) + lax.broadcasted_iota(jnp.int32, (width, width), 1)
            == width - 1).astype(F32)
    reversed_rows = jnp.dot(flip, tile_t, precision=lax.Precision.HIGHEST, preferred_element_type=F32)
    padded = jnp.concatenate([reversed_rows, jnp.zeros((width, width), F32)], axis=1)
    rolled = pltpu.roll(padded, 0, 1, stride=1, stride_axis=0)
    return jnp.sum(rolled, axis=0, keepdims=True)


def _b_build_bias(base_ref, bias_scr):
    for h in range(HEADS):
        gi, pr, e = _b_head_place(h)
        bias_scr[gi, e * B_WIN:(e + 1) * B_WIN, pr * TQ:(pr + 1) * TQ] = _toeplitz_tile_t(
            base_ref[h], B_WIN, B_LEFT_CHUNKS)


def _b_stack(x, gi):
    return jnp.concatenate(
        [x[:, (B_STACK * gi + pr) * LANES:(B_STACK * gi + pr + 1) * LANES] for pr in range(B_STACK)], axis=0)


def _b_sink_rows(sink_ref, gi):
    block = lax.broadcasted_iota(jnp.int32, (1, B_ROWS), 1) // TQ
    rows = []
    for e in range(2):
        row = jnp.zeros((1, B_ROWS), F32)
        for pr in range(B_STACK):
            h = B_GROUP * gi + 2 * pr + e
            row = jnp.where(block == pr, sink_ref[0:1, h:h + 1], row)
        rows.append(row)
    return rows


def _b_scores_t(q_ref, kvv, bias_scr, gi, b, left, first_blocks):
    kcat = _stack_pair(kvv[:, gi * LANES:(gi + 1) * LANES])
    vcat = _stack_pair(kvv[:, (B_KV_HEADS + gi) * LANES:(B_KV_HEADS + gi + 1) * LANES])
    qs = _b_stack(q_ref, gi) * SCALE
    sc = lax.dot_general(kcat, qs, NT, preferred_element_type=F32) + bias_scr[gi]
    if first_blocks:
        row = lax.broadcasted_iota(jnp.int32, (2 * B_WIN, 1), 0)
        row = jnp.where(row >= B_WIN, row - B_WIN, row)
        sc = jnp.where((row // KB + (b - left)) >= 0, sc, MASKED)
    return kcat, vcat, qs, sc


def _attn_b_fwd(qb, kvx, gate, base, sinks):
    s = qb.shape[0]
    pad = kvx.shape[0] - s
    nb = s // TQ
    left = B_KBLOCKS - 1

    def body(q_ref, kv_ref, g_ref, base_ref, sink_ref, o_ref, u_ref, lse_ref, bias_scr):
        b = pl.program_id(0)

        @pl.when(b == 0)
        def _():
            _b_build_bias(base_ref, bias_scr)

        def step(first_blocks):
            kvv = _window(kv_ref, b, pad, B_WIN, slice(None))
            upper = lax.broadcasted_iota(jnp.int32, (LANES, B_ROWS), 0) < HEAD_DIM
            lse_rows = []
            for gi in range(B_KV_HEADS):
                kcat, vcat, qs, sc = _b_scores_t(q_ref, kvv, bias_scr, gi, b, left, first_blocks)
                sink = _b_sink_rows(sink_ref, gi)
                ps, inv = [], []
                for e in range(2):
                    sh = sc[e * B_WIN:(e + 1) * B_WIN]
                    m = jnp.maximum(jnp.max(sh, axis=0, keepdims=True), sink[e])
                    ex = jnp.exp(sh - m)
                    l = jnp.sum(ex, axis=0, keepdims=True) + jnp.exp(sink[e] - m)
                    ps.append(ex.astype(BF16))
                    inv.append(1.0 / l)
                    lse_rows.append(m + jnp.log(l))
                pt = jnp.concatenate(ps, axis=0)
                ot = lax.dot_general(vcat, pt, TN, preferred_element_type=F32) * jnp.where(upper, inv[0], inv[1])
                ov = ot.T
                for pr in range(B_STACK):
                    pair = B_STACK * gi + pr
                    o_ref[:, pair * LANES:(pair + 1) * LANES] = ov[pr * TQ:(pr + 1) * TQ]
            lse_ref[0] = jnp.concatenate(lse_rows + [jnp.zeros((8 - len(lse_rows), B_ROWS), F32)], axis=0)
            sg, _ = _silu_parts(g_ref[...])
            u_ref[...] = (o_ref[...] * sg).astype(BF16)

        @pl.when(b < left)
        def _():
            step(True)

        @pl.when(b >= left)
        def _():
            step(False)

    row = pl.BlockSpec((TQ, D_MODEL), lambda b: (b, 0))
    return pl.pallas_call(
        body, name="attn_b_fwd", grid=(nb,),
        in_specs=[row, pl.BlockSpec((pad + s, B_KVX), lambda b: (0, 0)), row,
                  pl.BlockSpec((HEADS, 1, B_WIDE), lambda b: (0, 0, 0)), pl.BlockSpec((1, HEADS), lambda b: (0, 0))],
        out_specs=[row, row, pl.BlockSpec((1, 8, B_ROWS), lambda b: (b, 0, 0))],
        out_shape=[jax.ShapeDtypeStruct((s, D_MODEL), F32), jax.ShapeDtypeStruct((s, D_MODEL), BF16),
                   jax.ShapeDtypeStruct((nb, 8, B_ROWS), F32)],
        scratch_shapes=[pltpu.VMEM((B_KV_HEADS, 2 * B_WIN, B_ROWS), F32)],
        compiler_params=_params(("arbitrary",)),
    )(qb, kvx, gate, base, sinks)


def _attn_b_bwd(qb, kvx, gate, o, du, lse, base, sinks):
    s = qb.shape[0]
    pad = kvx.shape[0] - s
    nb = s // TQ
    left = B_KBLOCKS - 1
    half = D_MODEL // 2

    def body(q_ref, kv_ref, g_ref, o_ref, du_ref, lse_ref, base_ref, sink_ref, dz_ref, dkv_ref, dsum_ref,
             dsink_ref, bias_scr, dbias_acc, dkv_acc, dsink_acc):
        b = pl.program_id(0)

        @pl.when(b == 0)
        def _():
            _b_build_bias(base_ref, bias_scr)
            dbias_acc[...] = jnp.zeros_like(dbias_acc)
            dkv_acc[...] = jnp.zeros_like(dkv_acc)
            dsink_acc[...] = jnp.zeros_like(dsink_acc)

        def step(first_blocks):
            kvv = _window(kv_ref, b, pad, B_WIN, slice(None))
            sg, dsg = _silu_parts(g_ref[...])
            duv = du_ref[...]
            ov = o_ref[...]
            do = duv * sg
            dgate = (duv * ov * dsg).astype(BF16)
            dz_ref[2] = dgate[:, :half]
            dz_ref[3] = dgate[:, half:]
            do_o = do * ov
            do_bf = do.astype(BF16)
            lse_all = lse_ref[0]
            dsink_rows = []
            for gi in range(B_KV_HEADS):
                kcat, vcat, qs, sc = _b_scores_t(q_ref, kvv, bias_scr, gi, b, left, first_blocks)
                dos = _b_stack(do_bf, gi)
                doo_t = _b_stack(do_o, gi).T
                delta = (jnp.sum(doo_t[:HEAD_DIM], axis=0, keepdims=True),
                         jnp.sum(doo_t[HEAD_DIM:], axis=0, keepdims=True))
                sink = _b_sink_rows(sink_ref, gi)
                dp = lax.dot_general(vcat, dos, NT, preferred_element_type=F32)
                ps, dss = [], []
                for e in range(2):
                    lse_e = lse_all[2 * gi + e:2 * gi + e + 1]
                    delta_e = delta[e]
                    p = jnp.exp(sc[e * B_WIN:(e + 1) * B_WIN] - lse_e)
                    ps.append(p.astype(BF16))
                    dss.append(p * (dp[e * B_WIN:(e + 1) * B_WIN] - delta_e))
                    dsink_rows.append(-jnp.exp(sink[e] - lse_e) * delta_e)
                ds = jnp.concatenate(dss, axis=0)
                dbias_acc[gi] += ds
                dsb = ds.astype(BF16)
                dq = (lax.dot_general(kcat, dsb, TN, preferred_element_type=F32) * SCALE).T.astype(BF16)
                for pr in range(B_STACK):
                    dz_ref[gi, :, pr * LANES:(pr + 1) * LANES] = dq[pr * TQ:(pr + 1) * TQ]
                dk = _unstack_pair(jnp.dot(dsb, qs, preferred_element_type=F32), B_WIN)
                dv = _unstack_pair(jnp.dot(jnp.concatenate(ps, axis=0), dos, preferred_element_type=F32), B_WIN)
                krows = pl.ds(pl.multiple_of(b * TQ + pad - (B_WIN - TQ), KB), B_WIN)
                dkv_acc[krows, gi * LANES:(gi + 1) * LANES] += dk
                dkv_acc[krows, (B_KV_HEADS + gi) * LANES:(B_KV_HEADS + gi + 1) * LANES] += dv
            dsink_acc[...] += jnp.concatenate(
                dsink_rows + [jnp.zeros((8 - len(dsink_rows), B_ROWS), F32)], axis=0)

        @pl.when(b < left)
        def _():
            step(True)

        @pl.when(b >= left)
        def _():
            step(False)

        @pl.when(b == nb - 1)
        def _():
            lo_s = _lane_lo(s)
            for which in range(2):
                folded = []
                for gi in range(B_KV_HEADS):
                    part = dkv_acc[pad:pad + s, (which * B_KV_HEADS + gi) * LANES:(which * B_KV_HEADS + gi + 1) * LANES]
                    folded.append(part + pltpu.roll(part, HEAD_DIM, 1))
                dkv_ref[:, which * LANES:(which + 1) * LANES] = jnp.where(lo_s, folded[0], folded[1]).astype(BF16)
            lane8 = lax.broadcasted_iota(jnp.int32, dsink_ref.shape, 1)
            tot = jnp.zeros(dsink_ref.shape, F32)
            for h in range(HEADS):
                gi, pr, e = _b_head_place(h)
                dsum_ref[h] = _toeplitz_sum_t(
                    dbias_acc[gi, e * B_WIN:(e + 1) * B_WIN, pr * TQ:(pr + 1) * TQ], B_WIN)
                per_query = dsink_acc[2 * gi + e:2 * gi + e + 1, pr * TQ:(pr + 1) * TQ]
                tot = jnp.where(lane8 == h, jnp.sum(per_query, axis=1, keepdims=True), tot)
            dsink_ref[...] = tot

    row = pl.BlockSpec((TQ, D_MODEL), lambda b: (b, 0))
    base_spec = pl.BlockSpec((HEADS, 1, B_WIDE), lambda b: (0, 0, 0))
    return pl.pallas_call(
        body, name="attn_b_bwd", grid=(nb,),
        in_specs=[row, pl.BlockSpec((pad + s, B_KVX), lambda b: (0, 0)), row, row, row,
                  pl.BlockSpec((1, 8, B_ROWS), lambda b: (b, 0, 0)), base_spec,
                  pl.BlockSpec((1, HEADS), lambda b: (0, 0))],
        out_specs=[pl.BlockSpec((4, TQ, half), lambda b: (0, b, 0)),
                   pl.BlockSpec((s, 2 * LANES), lambda b: (0, 0)), base_spec,
                   pl.BlockSpec((8, LANES), lambda b: (0, 0))],
        out_shape=[jax.ShapeDtypeStruct((4, s, half), BF16), jax.ShapeDtypeStruct((s, 2 * LANES), BF16),
                   jax.ShapeDtypeStruct((HEADS, 1, B_WIDE), F32), jax.ShapeDtypeStruct((8, LANES), F32)],
        scratch_shapes=[pltpu.VMEM((B_KV_HEADS, 2 * B_WIN, B_ROWS), F32),
                        pltpu.VMEM((B_KV_HEADS, 2 * B_WIN, B_ROWS), F32),
                        pltpu.VMEM((pad + s, B_KVX), F32), pltpu.VMEM((8, B_ROWS), F32)],
        compiler_params=_params(("arbitrary",)),
    )(qb, kvx, gate, o, du, lse, base, sinks)


def _t5_bucket(rel):
    nb = T5_BUCKETS // 2
    max_exact = nb // 2
    ret = jnp.where(rel > 0, nb, 0)
    n = jnp.abs(rel)
    nf = jnp.maximum(n, 1).astype(jnp.float32)
    large = max_exact + (jnp.log(nf / max_exact) / math.log(T5_MAX_DIST / max_exact)
                         * (nb - max_exact)).astype(jnp.int32)
    large = jnp.minimum(large, nb - 1)
    return ret + jnp.where(n < max_exact, n, large)


def _a_offset_onehot():
    c = np.arange(A_WIN + TQ)
    dist = A_LEFT_CHUNKS * CHUNK + TQ - 1 - c
    idx = np.clip(dist, -A_REL_CLIP, A_REL_CLIP) + A_REL_CLIP
    onehot = np.zeros((A_WIN + TQ, 2 * A_REL_CLIP + 1), np.float32)
    onehot[c, idx] = 1.0
    return jnp.asarray(onehot)


def _b_offset_onehot():
    c = jnp.arange(B_WIN + TQ, dtype=jnp.int32)
    rel = c - (TQ - 1) - B_LEFT_CHUNKS * CHUNK
    return (_t5_bucket(rel)[:, None] == jnp.arange(T5_BUCKETS)[None, :]).astype(F32)


def _diag_rows(onehot, table):
    rows = jnp.dot(onehot, table.astype(F32), precision=lax.Precision.HIGHEST)
    return rows.T.reshape(HEADS, 1, onehot.shape[0])


def _diag_rows_grad(onehot, ddiag):
    return jnp.dot(ddiag.reshape(HEADS, onehot.shape[0]), onehot, precision=lax.Precision.HIGHEST)


def _position():
    x, y, c = lax.axis_index("x"), lax.axis_index("y"), lax.axis_index("c")
    chips = [(1 - x, y), (x, 1 - y), (1 - x, 1 - y)]
    return x, y, c, chips


ANY = pl.BlockSpec(memory_space=pl.ANY)


def _allgather_routed(shards):
    n = len(shards)

    def piece(block_ref, t, c, quarter=None):
        half = shards[t].shape[0] // 2
        if quarter is None:
            return block_ref.at[pl.ds(c * half, half)]
        return block_ref.at[pl.ds(c * half + quarter * (half // 2), half // 2)]

    def copies(kind, ins, outs, sems):
        ici_send, ici_recv, pass_send, pass_recv, local_sems = sems
        x, y, c, chips = _position()
        mine = 2 * x + y
        if kind == "local":
            return [pltpu.make_async_copy(ins[t], outs[t].at[mine], local_sems.at[t]) for t in range(n)]
        ids = [2 * chip[0] + chip[1] for chip in chips]
        made = []
        for t in range(n):
            def ici(k, to):
                return dict(send_sem=ici_send.at[4 * t + k], recv_sem=ici_recv.at[4 * t + k],
                            device_id=(chips[to][0], chips[to][1], c), device_id_type=MESH)

            def d2d(k):
                return dict(send_sem=pass_send.at[4 * t + k], recv_sem=pass_recv.at[4 * t + k],
                            device_id=(x, y, 1 - c), device_id_type=MESH)

            def same(ref, where):
                return pltpu.make_async_remote_copy(src_ref=ref, dst_ref=ref, **where)

            if kind == "send":
                for k in range(2):
                    made.append(pltpu.make_async_remote_copy(
                        src_ref=piece(ins[t], t, c), dst_ref=piece(outs[t].at[mine], t, c), **ici(k, k)))
            elif kind == "landed":
                made += [same(piece(outs[t].at[ids[k]], t, c), ici(k, k)) for k in range(2)]
            elif kind == "forward":
                made.append(same(piece(outs[t].at[ids[0]], t, c, 0), ici(2, 1)))
                made.append(same(piece(outs[t].at[ids[1]], t, c, 1), ici(3, 0)))
            elif kind == "arrived":
                made.append(same(piece(outs[t].at[ids[2]], t, c, 0), ici(2, 1)))
                made.append(same(piece(outs[t].at[ids[2]], t, c, 1), ici(3, 0)))
            else:
                core = 1 - c if kind == "passed" else c
                if kind in ("pass halves", "passed"):
                    made += [same(piece(outs[t].at[ids[k]], t, core), d2d(k)) for k in range(2)]
                if kind in ("pass quarters", "passed"):
                    made += [same(piece(outs[t].at[ids[2]], t, core, k), d2d(2 + k)) for k in range(2)]
        return made

    def first(ins, outs, sems):
        for cp in copies("local", ins, outs, sems) + copies("send", ins, outs, sems):
            cp.start()

    def middle(ins, outs, sems):
        for got, onward, near in zip(copies("landed", ins, outs, sems), copies("forward", ins, outs, sems),
                                     copies("pass halves", ins, outs, sems)):
            got.wait_recv()
            near.start()
            onward.start()

    def last(ins, outs, sems):
        quarters = copies("pass quarters", ins, outs, sems)
        for got, near in zip(copies("arrived", ins, outs, sems), quarters):
            got.wait_recv()
            near.start()
        for cp in copies("passed", ins, outs, sems):
            cp.wait_recv()
        for cp in (copies("send", ins, outs, sems) + copies("forward", ins, outs, sems)
                   + copies("pass halves", ins, outs, sems) + quarters):
            cp.wait_send()
        for cp in copies("local", ins, outs, sems):
            cp.wait()

    return _Hosted(shards, [jax.ShapeDtypeStruct((4,) + w.shape, w.dtype) for w in shards],
                   [pltpu.SemaphoreType.DMA((4 * n,))] * 4 + [pltpu.SemaphoreType.DMA((n,))],
                   first, middle, last)


def _scatter_hosted(grads):
    n = len(grads)

    def copies(ins, outs, sems):
        send_sems, recv_sems = sems
        x, y, c, chips = _position()
        return [pltpu.make_async_remote_copy(
            src_ref=ins[t].at[2 * chip[0] + chip[1]], dst_ref=outs[t].at[j],
            send_sem=send_sems.at[3 * t + j], recv_sem=recv_sems.at[3 * t + j],
            device_id=(chip[0], chip[1], c), device_id_type=MESH)
            for t in range(n) for j, chip in enumerate(chips)]

    def first(ins, outs, sems):
        for cp in copies(ins, outs, sems):
            cp.start()

    def last(ins, outs, sems):
        for cp in copies(ins, outs, sems):
            cp.wait()

    return _Hosted(grads, [jax.ShapeDtypeStruct((3,) + g.shape[1:], g.dtype) for g in grads],
                   [pltpu.SemaphoreType.DMA((3 * n,))] * 2, first, None, last)


GATHER_PEERS = "x and y neighbours (same core) and the sibling core"
SCATTER_PEERS = "the same core of the three other chips"
EVERYONE = "the seven other devices"


def _run_on_sequencer(name, hosted, peers, collective_id):
    ins = [jax.new_ref(a, memory_space=pltpu.MemorySpace.HBM) for a in hosted.inputs]
    outs = [jax.empty_ref(shape, memory_space=pltpu.MemorySpace.HBM) for shape in hosted.out_shapes]

    @pl.kernel(mesh=plsc.ScalarSubcoreMesh(axis_name="sequencer", num_cores=1), name=name,
               scratch_types=tuple(hosted.sems), compiler_params=pltpu.CompilerParams(collective_id=collective_id))
    def launch(*sems):
        x, y, c, chips = _position()
        if peers == GATHER_PEERS:
            devices = [(chip[0], chip[1], c) for chip in chips[:2]] + [(x, y, 1 - c)]
        elif peers == SCATTER_PEERS:
            devices = [(chip[0], chip[1], c) for chip in chips]
        else:
            devices = [(x ^ (k >> 2), y ^ ((k >> 1) & 1), c ^ (k & 1)) for k in range(1, 8)]
        barrier = pltpu.get_barrier_semaphore()
        for device in devices:
            pl.semaphore_signal(barrier, inc=1, device_id=device, device_id_type=MESH)
        pl.semaphore_wait(barrier, len(devices))
        hosted.first(ins, outs, sems)
        if hosted.middle is not None:
            hosted.middle(ins, outs, sems)
        hosted.last(ins, outs, sems)

    launch()
    return [o[...] for o in outs]


def _gather_gain(shard):
    def body(in_ref, out_ref, send_sems, recv_sems):
        x, y, c, chips = _position()
        out_ref[2 * x + y] = in_ref[...]
        sends = [pltpu.make_async_remote_copy(
            src_ref=in_ref, dst_ref=out_ref.at[2 * x + y], send_sem=send_sems.at[j], recv_sem=recv_sems.at[j],
            device_id=(chip[0], chip[1], c), device_id_type=MESH) for j, chip in enumerate(chips)]
        for cp in sends:
            cp.start()
        for j, chip in enumerate(chips):
            pltpu.make_async_remote_copy(
                src_ref=in_ref, dst_ref=out_ref.at[2 * chip[0] + chip[1]], send_sem=send_sems.at[j],
                recv_sem=recv_sems.at[j], device_id=(chip[0], chip[1], c), device_id_type=MESH).wait_recv()
        for cp in sends:
            cp.wait_send()

    vmem = pl.BlockSpec(memory_space=pltpu.VMEM)
    return pl.pallas_call(
        body, name="gather_gain", in_specs=[vmem], out_specs=vmem,
        out_shape=jax.ShapeDtypeStruct((4,) + shard.shape, shard.dtype),
        scratch_shapes=[pltpu.SemaphoreType.DMA((3,))] * 2,
    )(shard)


def _swap_with_sibling(name, blocks):
    n = len(blocks)

    def body(*refs):
        ins, outs = refs[:n], refs[n:2 * n]
        send_sems, recv_sems = refs[2 * n:]
        x, y, c, _ = _position()
        sends = [pltpu.make_async_remote_copy(
            src_ref=ins[t], dst_ref=outs[t], send_sem=send_sems.at[t], recv_sem=recv_sems.at[t],
            device_id=(x, y, 1 - c), device_id_type=MESH) for t in range(n)]
        for cp in sends:
            cp.start()
        for cp in sends:
            cp.wait()

    return pl.pallas_call(
        body, name=name,
        in_specs=[ANY] * n, out_specs=[ANY] * n,
        out_shape=[jax.ShapeDtypeStruct(b.shape, b.dtype) for b in blocks],
        scratch_shapes=[pltpu.SemaphoreType.DMA((n,))] * 2,
    )(*blocks)


def _everyone_hosted(terms):
    nt = len(terms)

    def copies(kind, ins, outs, sems):
        send_sems, recv_sems, local_sems = sems
        x, y, c, _ = _position()
        me = 4 * x + 2 * y + c
        if kind == "local":
            return [pltpu.make_async_copy(ins[t], outs[t].at[me], local_sems.at[t]) for t in range(nt)]
        made = []
        for t in range(nt):
            for k in range(1, 8):
                peer = (x ^ (k >> 2), y ^ ((k >> 1) & 1), c ^ (k & 1))
                slot = me if kind == "send" else me ^ k
                made.append(pltpu.make_async_remote_copy(
                    src_ref=ins[t], dst_ref=outs[t].at[slot], send_sem=send_sems.at[7 * t + k - 1],
                    recv_sem=recv_sems.at[7 * t + k - 1], device_id=peer, device_id_type=MESH))
        return made

    def first(ins, outs, sems):
        for cp in copies("local", ins, outs, sems) + copies("send", ins, outs, sems):
            cp.start()

    def last(ins, outs, sems):
        for cp in copies("landed", ins, outs, sems):
            cp.wait_recv()
        for cp in copies("send", ins, outs, sems):
            cp.wait_send()
        for cp in copies("local", ins, outs, sems):
            cp.wait()

    return _Hosted(terms, [jax.ShapeDtypeStruct((8,) + a.shape, F32) for a in terms],
                   [pltpu.SemaphoreType.DMA((7 * nt,))] * 2 + [pltpu.SemaphoreType.DMA((nt,))], first, None, last)


def _small_step(partials, extras, ws, ms, vs, shard_of):
    n = len(partials)
    terms = list(partials) + list(extras)
    nt = len(terms)
    rows = [t for t in range(nt) if terms[t].shape[0] == 1]
    mats = [t for t in range(nt) if terms[t].shape[0] != 1]
    row_block = (8, max(terms[t].shape[1] for t in rows))
    assert len(rows) <= row_block[0]
    vmem = pl.BlockSpec(memory_space=pltpu.VMEM)

    def pack(*refs):
        packed = refs[-1]
        packed[...] = jnp.zeros_like(packed)
        for i, t in enumerate(rows):
            packed[i:i + 1, 0:terms[t].shape[1]] = refs[i][...]

    packed = pl.pallas_call(pack, name="small_pack", in_specs=[vmem] * len(rows), out_specs=vmem,
                            out_shape=jax.ShapeDtypeStruct(row_block, F32))(*[terms[t] for t in rows])
    slots = _run_on_sequencer("allgather_small", _everyone_hosted([packed] + [terms[t] for t in mats]),
                              EVERYONE, 2)

    def body(*refs):
        slot_refs, refs = refs[:len(slots)], refs[len(slots):]
        w_refs, refs = refs[:n], refs[n:]
        m_refs, refs = refs[:n], refs[n:]
        v_refs, outs = refs[:n], refs[n:]
        sums = []
        for ref in slot_refs:
            g = ref[0]
            for dev in range(1, 8):
                g = g + ref[dev]
            sums.append(g)
        chip = 2 * lax.axis_index("x") + lax.axis_index("y")
        for t in range(nt):
            if t in rows:
                i = rows.index(t)
                g = sums[0][i:i + 1, 0:terms[t].shape[1]]
            else:
                g = sums[1 + mats.index(t)]
            if t >= n:
                outs[4 * n + t - n][...] = g
                continue
            if shard_of[t]:
                width = ws[t].shape[-1]
                mine = jnp.zeros(ws[t].shape, F32)
                for s in range(4):
                    mine = jnp.where(chip == s, g[:, s * width:(s + 1) * width], mine)
                g = mine
            delta, mn, vn = _adamw_math(w_refs[t][...], g, m_refs[t][...], v_refs[t][...])
            outs[4 * t][...] = g
            outs[4 * t + 1][...] = delta
            outs[4 * t + 2][...] = mn
            outs[4 * t + 3][...] = vn

    out_shapes = []
    for t in range(n):
        out_shapes += [jax.ShapeDtypeStruct(ws[t].shape, F32)] * 4
    out_shapes += [jax.ShapeDtypeStruct(a.shape, F32) for a in extras]
    res = pl.pallas_call(
        body, name="small_step",
        in_specs=[vmem] * (len(slots) + 3 * n), out_specs=[vmem] * len(out_shapes), out_shape=out_shapes,
    )(*slots, *ws, *ms, *vs)
    return [res[4 * t:4 * t + 4] for t in range(n)], res[4 * n:4 * n + nt - n]


def _adamw_math(w, g, m, v):
    m = ADAM_B1 * m + (1.0 - ADAM_B1) * g
    v = ADAM_B2 * v + (1.0 - ADAM_B2) * (g * g)
    m_hat = m / (1.0 - ADAM_B1 ** ADAM_STEP)
    v_hat = v / (1.0 - ADAM_B2 ** ADAM_STEP)
    delta = -ADAM_LR * (m_hat / (jnp.sqrt(v_hat) + ADAM_EPS) + ADAM_WD * w)
    return delta, m, v


def _row_tile(rows):
    return 256 if rows % 256 == 0 else rows


def _sum_partials(name, own, recv, chip, after):
    rows, cols = own.shape[1:]
    tr = _row_tile(rows)

    def body(chip_ref, own_ref, recv_ref, after_ref, o_ref):
        acc = own_ref[...]
        for j in range(3):
            acc = acc + recv_ref[j].astype(F32)
        o_ref[...] = acc

    return pl.pallas_call(
        body, name=name,
        grid_spec=pltpu.PrefetchScalarGridSpec(
            num_scalar_prefetch=1, grid=(rows // tr,),
            in_specs=[pl.BlockSpec((None, tr, cols), lambda i, chip_ref: (chip_ref[0], i, 0)),
                      pl.BlockSpec((3, tr, cols), lambda i, chip_ref: (0, i, 0)), ANY],
            out_specs=pl.BlockSpec((tr, cols), lambda i, chip_ref: (i, 0))),
        out_shape=jax.ShapeDtypeStruct((rows, cols), F32),
        compiler_params=_params(("parallel",)),
    )(chip.reshape(1).astype(jnp.int32), own, recv, after)


def _adamw(name, w, m, v, g_parts):
    rows, cols = w.shape
    tr = _row_tile(rows)
    n = len(g_parts)

    def body(w_ref, m_ref, v_ref, *refs):
        g_refs = refs[:n]
        go_ref, d_ref, mo_ref, vo_ref = refs[n:]
        g = g_refs[0][...]
        for r in g_refs[1:]:
            g = g + r[...]
        delta, mn, vn = _adamw_math(w_ref[...], g, m_ref[...], v_ref[...])
        go_ref[...] = g
        d_ref[...] = delta
        mo_ref[...] = mn
        vo_ref[...] = vn

    spec = pl.BlockSpec((tr, cols), lambda i: (i, 0))
    return pl.pallas_call(
        body, name=name, grid=(rows // tr,),
        in_specs=[spec] * (3 + n), out_specs=[spec] * 4,
        out_shape=[jax.ShapeDtypeStruct((rows, cols), F32)] * 4,
        compiler_params=_params(("parallel",)),
    )(w, m, v, *g_parts)


def _local_step(x, target, ga, wa_in, rel_bias, later_shards, gk, t5, gb, sinks, gf):
    s, d = x.shape
    tm = min(TM_DENSE, s)
    nt = s // tm
    half = d // 2
    row = pl.BlockSpec((tm, d), lambda i: (i, 0))
    whole = lambda shape: pl.BlockSpec(shape, lambda *_: (0,) * len(shape))

    n1, = _norm_fwd("norm_a", x, ga)
    zqkv = gate_a = None
    for h, (wa_half, tag) in enumerate(zip(wa_in, ("first", "second"))):
        zqkv = _matmul("proj_a_qkv_" + tag, n1, wa_half, dims=NN, grid=(3, nt + 1), zero_axis=1, into=zqkv,
                       a_spec=pl.BlockSpec((tm, d), lambda j, i: (jnp.maximum(i - 1, 0), 0)),
                       b_spec=pl.BlockSpec((None, d, half), lambda j, i: (j, 0, 0)),
                       o_spec=pl.BlockSpec((None, tm, half), lambda j, i, h=h: (j, i, h)),
                       out_shape=(3, tm + s, d), out_dtype=BF16)
        gate_a = _matmul("proj_a_gate_" + tag, n1, wa_half, dims=NN, grid=(nt,), into=gate_a,
                         a_spec=row, b_spec=pl.BlockSpec((None, d, half), lambda i: (3, 0, 0)),
                         o_spec=pl.BlockSpec((tm, half), lambda i, h=h: (i, h)),
                         out_shape=(s, d), out_dtype=F32)
    onehot_a = _a_offset_onehot()
    diag_a = _diag_rows(onehot_a, rel_bias)
    (o_a, u_a, lse_a), gathered = _attn_a_fwd(zqkv, gate_a, diag_a, hosted=_allgather_routed(later_shards))
    wa_out, wkv, wb_in, wb_out, wkv_x = gathered
    wa_out = wa_out.reshape(d, d)
    wkv = wkv.reshape(d, -1)
    wkv_x = wkv_x.reshape(d, B_KVX)
    wb_out = wb_out.reshape(d, d)
    h1, nk, n2 = _out_norms("out_a_norms", u_a, wa_out, x, jnp.concatenate([gk, gb], axis=0))
    kvw = wkv.shape[1]
    kvx =_matmul("proj_kv", nk, wkv_x, dims=NN, grid=(nt + 1,), zero_axis=0,
                  a_spec=pl.BlockSpec((tm, d), lambda i: (jnp.maximum(i - 1, 0), 0)), b_spec=whole((d, B_KVX)),
                  o_spec=pl.BlockSpec((tm, B_KVX), lambda i: (i, 0)), out_shape=(tm + s, B_KVX), out_dtype=BF16)
    qb = _matmul("proj_b_q", n2, wb_in, dims=NN, grid=(2, nt),
                 a_spec=pl.BlockSpec((tm, d), lambda j, i: (i, 0)),
                 b_spec=pl.BlockSpec((None, d, half), lambda j, i: (j, 0, 0)),
                 o_spec=pl.BlockSpec((tm, half), lambda j, i: (i, j)), out_shape=(s, d), out_dtype=BF16)
    gate_b = _matmul("proj_b_gate", n2, wb_in, dims=NN, grid=(2, nt),
                     a_spec=pl.BlockSpec((tm, d), lambda j, i: (i, 0)),
                     b_spec=pl.BlockSpec((None, d, half), lambda j, i: (2 + j, 0, 0)),
                     o_spec=pl.BlockSpec((tm, half), lambda j, i: (i, j)), out_shape=(s, d), out_dtype=F32)
    onehot_b = _b_offset_onehot()
    base_b = jnp.roll(_diag_rows(onehot_b, t5)[..., ::-1], TQ, axis=-1)
    o_b, u_b, lse_b = _attn_b_fwd(qb, kvx, gate_b, base_b, sinks)
    dh2, loss, d_gf = _out_loss_head(u_b, wb_out, h1, target, gf)

    du_b = _matmul("dout_b", dh2, wb_out, dims=NT, grid=(nt,), a_spec=row, b_spec=whole((d, d)), o_spec=row,
                   out_shape=(s, d), out_dtype=F32)
    d_wb_out = _matmul("dw_out_b", u_b, dh2, dims=TN, grid=(2,),
                       a_spec=whole((s, d)), b_spec=pl.BlockSpec((s, half), lambda j: (0, j)),
                       o_spec=pl.BlockSpec((d, half), lambda j: (0, j)),
                       out_shape=(d, d), out_dtype=F32, also_bf16=True)
    dz_b, dkv, dsum_b, dsinks = _attn_b_bwd(qb, kvx, gate_b, o_b, du_b, lse_b, base_b, sinks)
    ddiag_b = jnp.roll(dsum_b[..., ::-1], -1, axis=-1)
    d_wb_in = _matmul("dw_in_b", n2, dz_b, dims=TN, grid=(4,),
                      a_spec=whole((s, d)), b_spec=pl.BlockSpec((None, s, half), lambda j: (j, 0, 0)),
                      o_spec=pl.BlockSpec((None, d, half), lambda j: (j, 0, 0)),
                      out_shape=(4, d, half), out_dtype=F32, also_bf16=True)
    d_wkv = _matmul("dw_kv", nk, dkv, dims=TN, grid=(1,),
                    a_spec=whole((s, d)), b_spec=whole((s, kvw)), o_spec=whole((d, kvw)),
                    out_shape=(d, kvw), out_dtype=F32, also_bf16=True)
    dh1, d_gkb = _proj_norm_bwd("dproj_kv_b", h1, dh2, jnp.concatenate([gk, gb], axis=0),
                                [(dkv[None], [wkv[None]]), (dz_b, [wb_in])])

    du_a = _matmul("dout_a", dh1, wa_out, dims=NT, grid=(nt,), a_spec=row, b_spec=whole((d, d)), o_spec=row,
                   out_shape=(s, d), out_dtype=F32)
    d_wa_out = _matmul("dw_out_a", u_a, dh1, dims=TN, grid=(2,),
                       a_spec=whole((s, d)), b_spec=pl.BlockSpec((s, half), lambda j: (0, j)),
                       o_spec=pl.BlockSpec((d, half), lambda j: (0, j)),
                       out_shape=(d, d), out_dtype=F32, also_bf16=True)
    early = dict(a_w_out=[g.reshape(4, d // 4, d) for g in d_wa_out],
                 kv_w=[g.reshape(4, d // 4, kvw) for g in d_wkv], b_w_in=list(d_wb_in),
                 b_w_out=[g.reshape(4, d // 4, d) for g in d_wb_out])
    (dz_a, ddiag_a), early_recv = _attn_a_bwd(
        zqkv, gate_a, o_a, du_a, lse_a, diag_a, hosted=_scatter_hosted([early[n][1] for n in early]))
    d_wa_in = _matmul("dw_in_a", n1, dz_a, dims=TN, grid=(4, 2),
                      a_spec=whole((s, d)), b_spec=pl.BlockSpec((None, s, half), lambda j, h: (j, 0, h)),
                      o_spec=pl.BlockSpec((None, d, half), lambda j, h: (j, 0, h)),
                      out_shape=(4, d, d), out_dtype=F32, also_bf16=True)
    late_recv = _run_on_sequencer("scatter_a_w_in", _scatter_hosted([d_wa_in[1]]), SCATTER_PEERS, 0)
    grad_x, d_ga = _proj_norm_bwd("dproj_a", x, dh1, ga, [(dz_a, list(wa_in))])

    small = dict(a_norm=d_ga, kv_norm=d_gkb[0:1], b_norm=d_gkb[1:2], b_sinks=dsinks[0:1, :HEADS], final_norm=d_gf)
    small["by_offset"] = dict(a_rel_bias=(onehot_a, ddiag_a.reshape(HEADS, -1)),
                              t5_bias=(onehot_b, ddiag_b.reshape(HEADS, -1)))
    own = dict(a_w_in=d_wa_in[0], **{n: early[n][0] for n in early})
    received = dict(a_w_in=late_recv[0], **dict(zip(early, early_recv)))
    return loss, grad_x, small, own, received, d_wa_in[1]


SMALL = ("a_norm", "kv_norm", "b_norm", "b_sinks", "final_norm")
TABLES = ("a_rel_bias", "t5_bias")
BIG = ("a_w_in", "a_w_out", "kv_w", "b_w_in", "b_w_out")
ORDER = ("a_norm", "a_w_in", "a_rel_bias", "a_w_out", "kv_norm", "kv_w", "t5_bias", "b_norm", "b_w_in",
         "b_sinks", "b_w_out", "final_norm")


def kernel(x, a_norm, a_w_in, a_rel_bias, a_w_out, kv_norm, kv_w, t5_bias, b_norm, b_w_in, b_sinks, b_w_out, final_norm, loss_target, m_a_norm, m_a_w_in, m_a_rel_bias, m_a_w_out, m_kv_norm, m_kv_w, m_t5_bias, m_b_norm, m_b_w_in, m_b_sinks, m_b_w_out, m_final_norm, v_a_norm, v_a_w_in, v_a_rel_bias, v_a_w_out, v_kv_norm, v_kv_w, v_t5_bias, v_b_norm, v_b_w_in, v_b_sinks, v_b_w_out, v_final_norm):
    w = dict(a_norm=a_norm, a_w_in=a_w_in, a_rel_bias=a_rel_bias, a_w_out=a_w_out, kv_norm=kv_norm, kv_w=kv_w,
             t5_bias=t5_bias, b_norm=b_norm, b_w_in=b_w_in, b_sinks=b_sinks, b_w_out=b_w_out,
             final_norm=final_norm)
    m = dict(a_norm=m_a_norm, a_w_in=m_a_w_in, a_rel_bias=m_a_rel_bias, a_w_out=m_a_w_out, kv_norm=m_kv_norm,
             kv_w=m_kv_w, t5_bias=m_t5_bias, b_norm=m_b_norm, b_w_in=m_b_w_in, b_sinks=m_b_sinks,
             b_w_out=m_b_w_out, final_norm=m_final_norm)
    v = dict(a_norm=v_a_norm, a_w_in=v_a_w_in, a_rel_bias=v_a_rel_bias, a_w_out=v_a_w_out, kv_norm=v_kv_norm,
             kv_w=v_kv_w, t5_bias=v_t5_bias, b_norm=v_b_norm, b_w_in=v_b_w_in, b_sinks=v_b_sinks,
             b_w_out=v_b_w_out, final_norm=v_final_norm)
    d = D_MODEL
    chip = 2 * lax.axis_index("x") + lax.axis_index("y")

    shard2d = dict(a_w_in=a_w_in[0], a_w_out=a_w_out[0], kv_w=kv_w, b_w_in=b_w_in[0], b_w_out=b_w_out[0])

    first = shard2d["a_w_in"].astype(BF16)
    wa_in = [_run_on_sequencer("allgather_" + tag, _allgather_routed([first[:, h * (d // 2):(h + 1) * (d // 2)]]),
                               GATHER_PEERS, collective_id)[0]
             for h, (tag, collective_id) in enumerate((("first", 1), ("second", 3)))]
    ga = _gather_gain(a_norm).reshape(1, d)

    later = [shard2d[n].astype(BF16) for n in BIG[1:]]
    kv_shard = later[BIG[1:].index("kv_w")]
    later.append(jnp.concatenate(
        [kv_shard[:, (i // 2) * HEAD_DIM:(i // 2 + 1) * HEAD_DIM] for i in range(B_KVX // HEAD_DIM)], axis=1))
    loss, grad_x, small, own, received, after_attention = _local_step(
        x[0], loss_target[0], ga, wa_in, a_rel_bias[0], later,
        kv_norm.reshape(1, d), t5_bias, b_norm, b_sinks, final_norm.reshape(1, d))

    out = {}
    as2d = lambda a: a.reshape(-1, a.shape[-1])
    small_res, (loss_sum, *offset_sums) = _small_step(
        [small[n] for n in SMALL], [loss] + [small["by_offset"][n][1] for n in TABLES],
        [as2d(w[n]) for n in SMALL], [as2d(m[n]) for n in SMALL], [as2d(v[n]) for n in SMALL],
        [n == "a_norm" for n in SMALL])
    for n, res in zip(SMALL, small_res):
        out[n] = [r.reshape(w[n].shape) for r in res]
    loss_out = loss_sum.reshape(())
    for n, summed in zip(TABLES, offset_sums):
        grad = _diag_rows_grad(small["by_offset"][n][0], summed)
        res = _adamw("adamw_" + n, as2d(w[n]).T, as2d(m[n]).T, as2d(v[n]).T, [grad])
        out[n] = [r.T.reshape(w[n].shape) for r in res]

    core_sums = [_sum_partials("sum_" + n, own[n], received[n], chip, after_attention) for n in BIG]
    sibling_sums = (_swap_with_sibling("swap_last", core_sums[:1])
                    + _swap_with_sibling("swap_early", core_sums[1:]))

    for n, mine, theirs in zip(BIG, core_sums, sibling_sums):
        res = _adamw("adamw_" + n, shard2d[n], m[n].reshape(shard2d[n].shape), v[n].reshape(shard2d[n].shape),
                     [mine, theirs])
        out[n] = [r.reshape(w[n].shape) for r in res]

    grads = [out[n][0] for n in ORDER]
    deltas = [out[n][1] for n in ORDER]
    new_m = [out[n][2] for n in ORDER]
    new_v = [out[n][3] for n in ORDER]
    return (loss_out, grad_x[None], *grads, *deltas, *new_m, *new_v)
```

```python
import math

import jax
import jax.numpy as jnp
import numpy as np
from jax import lax
from jax.experimental import pallas as pl
from jax.experimental.pallas import tpu as pltpu
from jax.experimental.pallas import tpu_sc as plsc

F32 = jnp.float32
BF16 = jnp.bfloat16
MESH = pl.DeviceIdType.MESH

D_MODEL = 1024
HEADS = 16
HEAD_DIM = 64
CHUNK = 64
RMS_EPS = 1e-6
SCALE = HEAD_DIM ** -0.5
A_LEFT_CHUNKS = 8
A_REL_CLIP = 256
B_LEFT_CHUNKS = 2
B_KV_HEADS = 2
B_GROUP = HEADS // B_KV_HEADS
T5_BUCKETS = 32
T5_MAX_DIST = 128
ADAM_LR = 0.001
ADAM_B1 = 0.9
ADAM_B2 = 0.999
ADAM_EPS = 1e-08
ADAM_WD = 0.01
ADAM_STEP = 10

MASKED = -1e30
LANES = 128
TQ = 128
A_PAIRS = 2
A_PAIRS_FWD = 4
KB = 128
A_KBLOCKS = A_LEFT_CHUNKS * CHUNK // KB + 1
B_KBLOCKS = B_LEFT_CHUNKS * CHUNK // KB + 1
A_WIN = A_KBLOCKS * KB
B_WIN = B_KBLOCKS * KB
TM = 512
TM_DENSE = 1024
TM_HALF = 2048
TM_PARTS = 512
VMEM_LIMIT = 56 * 1024 * 1024

NT = (((1,), (1,)), ((), ()))
TN = (((0,), (0,)), ((), ()))
NN = (((1,), (0,)), ((), ()))


def _params(sem=None):
    return pltpu.CompilerParams(dimension_semantics=sem, vmem_limit_bytes=VMEM_LIMIT)


class _Hosted:
    def __init__(self, inputs, out_shapes, sems, first, middle, last):
        self.inputs, self.out_shapes, self.sems = list(inputs), list(out_shapes), list(sems)
        self.first, self.middle, self.last = first, middle, last


def _call(body, *, name, grid, in_specs, out_specs, out_shape, args, scratch_shapes=(), sem=None, hosted=None,
          aliases=None):
    in_specs, out_specs, out_shape = list(in_specs), list(out_specs), list(out_shape)
    scratch_shapes = list(scratch_shapes)
    if hosted is None:
        out = pl.pallas_call(
            body, name=name, grid=grid, in_specs=in_specs, out_specs=out_specs, out_shape=out_shape,
            scratch_shapes=scratch_shapes, input_output_aliases=aliases or {},
            compiler_params=_params(sem))(*args)
        return list(out), []
    assert aliases is None
    n_in, n_out, n_scr = len(in_specs), len(out_shape), len(scratch_shapes)
    h_in, h_out = len(hosted.inputs), len(hosted.out_shapes)
    total = int(np.prod(grid)) if grid else 1

    def wrapped(*refs):
        ins, refs = refs[:n_in], refs[n_in:]
        h_ins, refs = refs[:h_in], refs[h_in:]
        outs, refs = refs[:n_out], refs[n_out:]
        h_outs, refs = refs[:h_out], refs[h_out:]
        scr, h_sems = refs[:n_scr], refs[n_scr:]
        step = 0
        for axis, size in enumerate(grid):
            step = step * size + pl.program_id(axis)

        if hosted.first is not None:
            @pl.when(step == 0)
            def _():
                hosted.first(h_ins, h_outs, h_sems)

        body(*ins, *outs, *scr)
        if hosted.middle is not None:
            @pl.when(step == total // 2)
            def _():
                hosted.middle(h_ins, h_outs, h_sems)

        if hosted.last is not None:
            @pl.when(step == total - 1)
            def _():
                hosted.last(h_ins, h_outs, h_sems)

    out = pl.pallas_call(
        wrapped, name=name, grid=grid, in_specs=in_specs + [ANY] * h_in, out_specs=out_specs + [ANY] * h_out,
        out_shape=out_shape + hosted.out_shapes, scratch_shapes=scratch_shapes + hosted.sems,
        compiler_params=_params(("arbitrary",) * len(grid)))(*args, *hosted.inputs)
    return list(out[:n_out]), list(out[n_out:])


def _matmul(name, a, b, *, dims, grid, a_spec, b_spec, o_spec, out_shape, out_dtype,
            also_bf16=False, zero_axis=None, into=None):
    def body(*refs):
        if into is not None:
            refs = refs[:2] + refs[3:]
        if zero_axis is None:
            product(*refs)
        else:
            @pl.when(pl.program_id(zero_axis) == 0)
            def _():
                refs[2][...] = jnp.zeros_like(refs[2])

            @pl.when(pl.program_id(zero_axis) > 0)
            def _():
                product(*refs)

    def product(a_ref, b_ref, o_ref, *more):
        prod = lax.dot_general(a_ref[...].astype(BF16), b_ref[...].astype(BF16), dims,
                               preferred_element_type=F32)
        o_ref[...] = prod.astype(out_dtype)
        if also_bf16:
            more[0][...] = prod.astype(BF16)

    out_specs = [o_spec]
    out_shapes = [jax.ShapeDtypeStruct(out_shape, out_dtype)]
    if also_bf16:
        out_specs.append(o_spec)
        out_shapes.append(jax.ShapeDtypeStruct(out_shape, BF16))
    out, _ = _call(body, name=name, grid=grid, in_specs=[a_spec, b_spec] + ([] if into is None else [ANY]),
                   out_specs=out_specs, out_shape=out_shapes, args=[a, b] + ([] if into is None else [into]),
                   sem=("parallel",) * len(grid), aliases=None if into is None else {2: 0})
    return out[0] if not also_bf16 else tuple(out)


def _rms_rows(x):
    return lax.rsqrt(jnp.mean(x * x, axis=-1, keepdims=True) + RMS_EPS)


def _norm_fwd(name, x, gains):
    s, d = x.shape
    n = gains.shape[0]

    def body(x_ref, g_ref, *o_refs):
        xv = x_ref[...]
        xh = xv * _rms_rows(xv)
        for i in range(n):
            o_refs[i][...] = (xh * g_ref[i:i + 1, :]).astype(BF16)

    row = pl.BlockSpec((TM, d), lambda i: (i, 0))
    return pl.pallas_call(
        body, name=name, grid=(s // TM,),
        in_specs=[row, pl.BlockSpec((n, d), lambda i: (0, 0))],
        out_specs=[row] * n,
        out_shape=[jax.ShapeDtypeStruct((s, d), BF16)] * n,
        compiler_params=_params(("parallel",)),
    )(x, gains)


def _proj_norm_bwd(name, x, dres, gains, branches):
    s, d = x.shape
    n = len(branches)
    n_ab = 2 * sum(len(bs) for _, bs in branches)
    tm = min(TM_PARTS, s)

    def body(x_ref, r_ref, g_ref, *refs):
        ab_refs, dx_ref, dg_ref = list(refs[:n_ab]), refs[n_ab], refs[n_ab + 1]
        i = pl.program_id(0)
        xv = x_ref[...]
        r = _rms_rows(xv)
        xh = xv * r

        @pl.when(i == 0)
        def _():
            dg_ref[...] = jnp.zeros_like(dg_ref)

        a = None
        for j in range(n):
            dn = None
            for _ in branches[j][1]:
                a_ref, b_ref = ab_refs.pop(0), ab_refs.pop(0)
                for part in range(a_ref.shape[0]):
                    term = lax.dot_general(a_ref[part], b_ref[part], NT, preferred_element_type=F32)
                    dn = term if dn is None else dn + term
            t = dn * g_ref[j:j + 1, :]
            a = t if a is None else a + t
            dg_ref[j:j + 1, :] += jnp.sum(dn * xh, axis=0, keepdims=True)
        dx_ref[...] = r_ref[...] + r * (a - xh * jnp.mean(xh * a, axis=-1, keepdims=True))

    row = pl.BlockSpec((tm, d), lambda i: (i, 0))
    small = pl.BlockSpec((n, d), lambda i: (0, 0))
    ab_specs, ab_args = [], []
    for a, bs in branches:
        for k, b in enumerate(bs):
            ab_specs += [pl.BlockSpec((a.shape[0], tm, b.shape[2]), lambda i, k=k: (0, i, k)),
                         pl.BlockSpec(b.shape, lambda i: (0, 0, 0))]
            ab_args += [a, b]
    return pl.pallas_call(
        body, name=name, grid=(s // tm,),
        in_specs=[row, row, small] + ab_specs,
        out_specs=[row, small],
        out_shape=[jax.ShapeDtypeStruct((s, d), F32), jax.ShapeDtypeStruct((n, d), F32)],
        compiler_params=_params(("arbitrary",)),
    )(x, dres, gains, *ab_args)


def _out_norms(name, u, w_out, resid, gains):
    s, d = resid.shape
    n = gains.shape[0]
    tm = min(TM_DENSE, s)

    def body(u_ref, w_ref, r_ref, g_ref, h_ref, *o_refs):
        hv = r_ref[...] + jnp.dot(u_ref[...], w_ref[...], preferred_element_type=F32)
        h_ref[...] = hv
        hh = hv * _rms_rows(hv)
        for i in range(n):
            o_refs[i][...] = (hh * g_ref[i:i + 1, :]).astype(BF16)

    row = pl.BlockSpec((tm, d), lambda i: (i, 0))
    return pl.pallas_call(
        body, name=name, grid=(s // tm,),
        in_specs=[row, pl.BlockSpec((d, d), lambda i: (0, 0)), row, pl.BlockSpec((n, d), lambda i: (0, 0))],
        out_specs=[row] * (n + 1),
        out_shape=[jax.ShapeDtypeStruct((s, d), F32)] + [jax.ShapeDtypeStruct((s, d), BF16)] * n,
        compiler_params=_params(("parallel",)),
    )(u, w_out, resid, gains)


def _out_loss_head(u, w_out, resid, target, gain):
    s, d = resid.shape
    tm = min(TM_PARTS, s)

    def body(u_ref, w_ref, r_ref, t_ref, g_ref, dh_ref, loss_ref, dg_ref):
        i = pl.program_id(0)
        hv = r_ref[...] + jnp.dot(u_ref[...], w_ref[...], preferred_element_type=F32)
        r = _rms_rows(hv)
        hh = hv * r
        g = g_ref[...]
        err = hh * g - t_ref[...]
        part = 0.5 * jnp.sum(jnp.sum(err * err, axis=-1, keepdims=True) * (1.0 / d), axis=0, keepdims=True)
        dy = err * (1.0 / d)
        a = dy * g
        dh_ref[...] = r * (a - hh * jnp.mean(hh * a, axis=-1, keepdims=True))
        dg = jnp.sum(dy * hh, axis=0, keepdims=True)

        @pl.when(i == 0)
        def _():
            loss_ref[...] = part
            dg_ref[...] = dg

        @pl.when(i > 0)
        def _():
            loss_ref[...] += part
            dg_ref[...] += dg

    row = pl.BlockSpec((tm, d), lambda i: (i, 0))
    return pl.pallas_call(
        body, name="out_b_loss_head", grid=(s // tm,),
        in_specs=[row, pl.BlockSpec((d, d), lambda i: (0, 0)), row, row, pl.BlockSpec((1, d), lambda i: (0, 0))],
        out_specs=[row, pl.BlockSpec((1, 1), lambda i: (0, 0)), pl.BlockSpec((1, d), lambda i: (0, 0))],
        out_shape=[jax.ShapeDtypeStruct((s, d), F32), jax.ShapeDtypeStruct((1, 1), F32),
                   jax.ShapeDtypeStruct((1, d), F32)],
        compiler_params=_params(("arbitrary",)),
    )(u, w_out, resid, target, gain)


def _silu_parts(g):
    sig = jax.nn.sigmoid(g)
    return g * sig, sig * (1.0 + g * (1.0 - sig))


def _lane_lo(rows):
    return lax.broadcasted_iota(jnp.int32, (rows, LANES), 1) < HEAD_DIM


def _stack_pair(x):
    lo = _lane_lo(x.shape[0])
    zero = jnp.zeros_like(x)
    return jnp.concatenate([jnp.where(lo, x, zero), jnp.where(lo, zero, x)], axis=0)


def _unstack_pair(y, w):
    return jnp.where(_lane_lo(w), y[:w], y[w:])


def _block_valid(b, left_blocks, width):
    col = lax.broadcasted_iota(jnp.int32, (1, 2 * width), 1)
    col = jnp.where(col >= width, col - width, col)
    return (col // KB + (b - left_blocks)) >= 0


def _toeplitz_tile(diag_row, width, left_chunks):
    wide = width + TQ
    rolled = pltpu.roll(jnp.broadcast_to(diag_row, (TQ, wide)), 1, 1, stride=1, stride_axis=0)
    i = lax.broadcasted_iota(jnp.int32, (TQ, width), 0) // CHUNK
    j = lax.broadcasted_iota(jnp.int32, (TQ, width), 1) // CHUNK
    dc = i + left_chunks - j
    return jnp.where((dc >= 0) & (dc <= left_chunks), rolled[:, TQ:], MASKED)


def _toeplitz_sum(tile, width):
    flip = (lax.broadcasted_iota(jnp.int32, (TQ, TQ), 0) + lax.broadcasted_iota(jnp.int32, (TQ, TQ), 1)
            == TQ - 1).astype(F32)
    reversed_rows = jnp.dot(flip, tile, precision=lax.Precision.HIGHEST, preferred_element_type=F32)
    padded = jnp.concatenate([reversed_rows, jnp.zeros((TQ, TQ), F32)], axis=1)
    rolled = pltpu.roll(padded, 0, 1, stride=1, stride_axis=0)
    return jnp.sum(rolled, axis=0, keepdims=True)


def _softmax_pair(sc, w, sink=None):
    ps, inv, lses = [], [], []
    for e in range(2):
        sh = sc[:, e * w:(e + 1) * w]
        m = jnp.max(sh, axis=-1, keepdims=True)
        if sink is not None:
            m = jnp.maximum(m, sink[e])
        ex = jnp.exp(sh - m)
        l = jnp.sum(ex, axis=-1, keepdims=True)
        if sink is not None:
            l = l + jnp.exp(sink[e] - m)
        ps.append(ex.astype(BF16))
        inv.append(1.0 / l)
        lses.append(m + jnp.log(l))
    return jnp.concatenate(ps, axis=-1), inv, lses


def _softmax_pair_bwd(sc, dp, lse, delta, w):
    ps, dss = [], []
    for e in range(2):
        p = jnp.exp(sc[:, e * w:(e + 1) * w] - lse[e])
        ps.append(p)
        dss.append(p * (dp[:, e * w:(e + 1) * w] - delta[e]))
    return jnp.concatenate(ps, axis=-1), jnp.concatenate(dss, axis=-1)


def _pair_rowsums(x, lo):
    zero = jnp.zeros_like(x)
    return (jnp.sum(jnp.where(lo, x, zero), axis=-1, keepdims=True),
            jnp.sum(jnp.where(lo, zero, x), axis=-1, keepdims=True))


def _a_qkv_specs(rows, pad, pw):
    return [pl.BlockSpec((None, TQ, pw), lambda p, b: (0, b + pad // TQ, p)),
            pl.BlockSpec((None, rows, pw), lambda p, b: (1, 0, p)),
            pl.BlockSpec((None, rows, pw), lambda p, b: (2, 0, p))]


def _window(ref, b, pad, win, lanes):
    start = pl.multiple_of(b * TQ + pad - (win - TQ), KB)
    return ref[pl.ds(start, win), lanes]


def _attn_a_fwd(zqkv, g, diag, hosted=None):
    s = g.shape[0]
    pad = zqkv.shape[1] - s
    nb = s // TQ
    left = A_KBLOCKS - 1
    pairs = A_PAIRS_FWD
    pw = pairs * LANES
    wide = A_WIN + TQ

    def body(q_ref, k_ref, v_ref, g_ref, diag_ref, o_ref, u_ref, lse_ref, bias_scr):
        b = pl.program_id(1)

        @pl.when(b == 0)
        def _():
            for hh in range(2 * pairs):
                bias_scr[hh // 2, :, (hh % 2) * A_WIN:(hh % 2 + 1) * A_WIN] = _toeplitz_tile(
                    diag_ref[hh], A_WIN, A_LEFT_CHUNKS)

        def step(first_blocks):
            lo = _lane_lo(TQ)
            for pp in range(pairs):
                ln = slice(pp * LANES, (pp + 1) * LANES)
                kcat = _stack_pair(_window(k_ref, b, pad, A_WIN, ln))
                vcat = _stack_pair(_window(v_ref, b, pad, A_WIN, ln))
                sc = lax.dot_general(q_ref[:, ln] * SCALE, kcat, NT, preferred_element_type=F32) + bias_scr[pp]
                if first_blocks:
                    sc = jnp.where(_block_valid(b, left, A_WIN), sc, MASKED)
                p, inv, lses = _softmax_pair(sc, A_WIN)
                ov = jnp.dot(p, vcat, preferred_element_type=F32) * jnp.where(lo, inv[0], inv[1])
                o_ref[:, ln] = ov
                lse_ref[pp] = jnp.where(lo, lses[0], lses[1])
                sg, _ = _silu_parts(g_ref[:, ln])
                u_ref[:, ln] = (ov * sg).astype(BF16)

        @pl.when(b < left)
        def _():
            step(True)

        @pl.when(b >= left)
        def _():
            step(False)

    tile = pl.BlockSpec((TQ, pw), lambda p, b: (b, p))
    return _call(
        body, name="attn_a_fwd", grid=(HEADS // 2 // pairs, nb),
        in_specs=_a_qkv_specs(pad + s, pad, pw) + [
            tile, pl.BlockSpec((2 * pairs, 1, wide), lambda p, b: (p, 0, 0))],
        out_specs=[tile, tile, pl.BlockSpec((pairs, TQ, LANES), lambda p, b: (p, b, 0))],
        out_shape=[jax.ShapeDtypeStruct((s, D_MODEL), F32), jax.ShapeDtypeStruct((s, D_MODEL), BF16),
                   jax.ShapeDtypeStruct((HEADS // 2, s, LANES), F32)],
        scratch_shapes=[pltpu.VMEM((pairs, TQ, 2 * A_WIN), F32)],
        sem=("parallel", "arbitrary"), hosted=hosted,
        args=(zqkv, zqkv, zqkv, g, diag))


def _attn_a_bwd(zqkv, g, o, du, lse, diag, hosted=None):
    s = g.shape[0]
    pad = zqkv.shape[1] - s
    nb = s // TQ
    left = A_KBLOCKS - 1
    pw = A_PAIRS * LANES
    wide = A_WIN + TQ

    def body(q_ref, k_ref, v_ref, g_ref, o_ref, du_ref, lse_ref, diag_ref, dz_ref, ddiag_ref,
             bias_scr, dbias_acc, dk_acc, dv_acc):
        b = pl.program_id(1)

        @pl.when(b == 0)
        def _():
            for hh in range(2 * A_PAIRS):
                bias_scr[hh // 2, :, (hh % 2) * A_WIN:(hh % 2 + 1) * A_WIN] = _toeplitz_tile(
                    diag_ref[hh], A_WIN, A_LEFT_CHUNKS)
            dbias_acc[...] = jnp.zeros_like(dbias_acc)
            dk_acc[...] = jnp.zeros_like(dk_acc)
            dv_acc[...] = jnp.zeros_like(dv_acc)

        def step(first_blocks):
            lo = _lane_lo(TQ)
            rows = pl.ds(pl.multiple_of(b * TQ, TQ), TQ)
            sg, dsg = _silu_parts(g_ref[...])
            duv = du_ref[...]
            ov = o_ref[...]
            do = duv * sg
            dz_ref[3, rows, :] = (duv * ov * dsg).astype(BF16)
            do_o = do * ov
            do_bf = do.astype(BF16)
            for pp in range(A_PAIRS):
                ln = slice(pp * LANES, (pp + 1) * LANES)
                q = q_ref[:, ln] * SCALE
                kcat = _stack_pair(_window(k_ref, b, pad, A_WIN, ln))
                vcat = _stack_pair(_window(v_ref, b, pad, A_WIN, ln))
                sc = lax.dot_general(q, kcat, NT, preferred_element_type=F32) + bias_scr[pp]
                if first_blocks:
                    sc = jnp.where(_block_valid(b, left, A_WIN), sc, MASKED)
                lse_t = lse_ref[pp]
                dp = lax.dot_general(do_bf[:, ln], vcat, NT, preferred_element_type=F32)
                p, ds = _softmax_pair_bwd(sc, dp, (lse_t[:, 0:1], lse_t[:, HEAD_DIM:HEAD_DIM + 1]),
                                          _pair_rowsums(do_o[:, ln], lo), A_WIN)
                dbias_acc[pp] += ds
                dsb = ds.astype(BF16)
                dz_ref[0, rows, ln] = (jnp.dot(dsb, kcat, preferred_element_type=F32) * SCALE).astype(BF16)
                pb = p.astype(BF16)
                dob = do_bf[:, ln]
                dkt = jnp.concatenate([
                    lax.dot_general(q[:, e * HEAD_DIM:(e + 1) * HEAD_DIM], dsb[:, e * A_WIN:(e + 1) * A_WIN], TN,
                                    preferred_element_type=F32) for e in range(2)], axis=0)
                dvt = jnp.concatenate([
                    lax.dot_general(dob[:, e * HEAD_DIM:(e + 1) * HEAD_DIM], pb[:, e * A_WIN:(e + 1) * A_WIN], TN,
                                    preferred_element_type=F32) for e in range(2)], axis=0)
                for t in range(A_KBLOCKS):
                    blk = b + (pad // KB - left + t)
                    dk_acc[blk, ln, :] += dkt[:, t * KB:(t + 1) * KB]
                    dv_acc[blk, ln, :] += dvt[:, t * KB:(t + 1) * KB]

        @pl.when(b < left)
        def _():
            step(True)

        @pl.when(b >= left)
        def _():
            step(False)

        @pl.when(b == nb - 1)
        def _():
            for kb in range(s // KB):
                dz_ref[1, kb * KB:(kb + 1) * KB, :] = dk_acc[pad // KB + kb].T.astype(BF16)
                dz_ref[2, kb * KB:(kb + 1) * KB, :] = dv_acc[pad // KB + kb].T.astype(BF16)
            for hh in range(2 * A_PAIRS):
                ddiag_ref[hh] = _toeplitz_sum(
                    dbias_acc[hh // 2, :, (hh % 2) * A_WIN:(hh % 2 + 1) * A_WIN], A_WIN)

    tile = pl.BlockSpec((TQ, pw), lambda p, b: (b, p))
    diag_spec = pl.BlockSpec((2 * A_PAIRS, 1, wide), lambda p, b: (p, 0, 0))
    return _call(
        body, name="attn_a_bwd", grid=(HEADS // 2 // A_PAIRS, nb),
        in_specs=_a_qkv_specs(pad + s, pad, pw) + [
            tile, tile, tile, pl.BlockSpec((A_PAIRS, TQ, LANES), lambda p, b: (p, b, 0)), diag_spec],
        out_specs=[pl.BlockSpec((4, s, pw), lambda p, b: (0, 0, p)), diag_spec],
        out_shape=[jax.ShapeDtypeStruct((4, s, D_MODEL), BF16),
                   jax.ShapeDtypeStruct((HEADS, 1, wide), F32)],
        scratch_shapes=[pltpu.VMEM((A_PAIRS, TQ, 2 * A_WIN), F32), pltpu.VMEM((A_PAIRS, TQ, 2 * A_WIN), F32),
                        pltpu.VMEM(((pad + s) // KB, pw, KB), F32), pltpu.VMEM(((pad + s) // KB, pw, KB), F32)],
        sem=("parallel", "arbitrary"), hosted=hosted,
        args=(zqkv, zqkv, zqkv, g, o, du, lse, diag))


B_STACK = B_GROUP // 2
B_KVX = 4 * LANES
B_ROWS = B_STACK * TQ
B_WIDE = B_WIN + TQ


def _b_head_place(h):
    return h // B_GROUP, (h % B_GROUP) // 2, h % 2


def _toeplitz_tile_t(base_row, width, left_chunks):
    wide = width + TQ
    rolled = pltpu.roll(jnp.broadcast_to(base_row, (width, wide)), 0, 1, stride=1, stride_axis=0)
    j = lax.broadcasted_iota(jnp.int32, (width, TQ), 0) // CHUNK
    i = lax.broadcasted_iota(jnp.int32, (width, TQ), 1) // CHUNK
    dc = i + left_chunks - j
    return jnp.where((dc >= 0) & (dc <= left_chunks), rolled[:, :TQ], MASKED)


def _toeplitz_sum_t(tile_t, width):
    flip = (lax.broadcasted_iota(jnp.int32, (width, width), 0) + lax.broadcasted_iota(jnp.int32, (width, width), 1)
            == width - 1).astype(F32)
    reversed_rows = jnp.dot(flip, tile_t, precision=lax.Precision.HIGHEST, preferred_element_type=F32)
    padded = jnp.concatenate([reversed_rows, jnp.zeros((width, width), F32)], axis=1)
    rolled = pltpu.roll(padded, 0, 1, stride=1, stride_axis=0)
    return jnp.sum(rolled, axis=0, keepdims=True)


def _b_build_bias(base_ref, bias_scr):
    for h in range(HEADS):
        gi, pr, e = _b_head_place(h)
        bias_scr[gi, e * B_WIN:(e + 1) * B_WIN, pr * TQ:(pr + 1) * TQ] = _toeplitz_tile_t(
            base_ref[h], B_WIN, B_LEFT_CHUNKS)


def _b_stack(x, gi):
    return jnp.concatenate(
        [x[:, (B_STACK * gi + pr) * LANES:(B_STACK * gi + pr + 1) * LANES] for pr in range(B_STACK)], axis=0)


def _b_sink_rows(sink_ref, gi):
    block = lax.broadcasted_iota(jnp.int32, (1, B_ROWS), 1) // TQ
    rows = []
    for e in range(2):
        row = jnp.zeros((1, B_ROWS), F32)
        for pr in range(B_STACK):
            h = B_GROUP * gi + 2 * pr + e
            row = jnp.where(block == pr, sink_ref[0:1, h:h + 1], row)
        rows.append(row)
    return rows


def _b_scores_t(q_ref, kvv, bias_scr, gi, b, left, first_blocks):
    kcat = _stack_pair(kvv[:, gi * LANES:(gi + 1) * LANES])
    vcat = _stack_pair(kvv[:, (B_KV_HEADS + gi) * LANES:(B_KV_HEADS + gi + 1) * LANES])
    qs = _b_stack(q_ref, gi) * SCALE
    sc = lax.dot_general(kcat, qs, NT, preferred_element_type=F32) + bias_scr[gi]
    if first_blocks:
        row = lax.broadcasted_iota(jnp.int32, (2 * B_WIN, 1), 0)
        row = jnp.where(row >= B_WIN, row - B_WIN, row)
        sc = jnp.where((row // KB + (b - left)) >= 0, sc, MASKED)
    return kcat, vcat, qs, sc


def _attn_b_fwd(qb, kvx, gate, base, sinks):
    s = qb.shape[0]
    pad = kvx.shape[0] - s
    nb = s // TQ
    left = B_KBLOCKS - 1

    def body(q_ref, kv_ref, g_ref, base_ref, sink_ref, o_ref, u_ref, lse_ref, bias_scr):
        b = pl.program_id(0)

        @pl.when(b == 0)
        def _():
            _b_build_bias(base_ref, bias_scr)

        def step(first_blocks):
            kvv = _window(kv_ref, b, pad, B_WIN, slice(None))
            upper = lax.broadcasted_iota(jnp.int32, (LANES, B_ROWS), 0) < HEAD_DIM
            lse_rows = []
            for gi in range(B_KV_HEADS):
                kcat, vcat, qs, sc = _b_scores_t(q_ref, kvv, bias_scr, gi, b, left, first_blocks)
                sink = _b_sink_rows(sink_ref, gi)
                ps, inv = [], []
                for e in range(2):
                    sh = sc[e * B_WIN:(e + 1) * B_WIN]
                    m = jnp.maximum(jnp.max(sh, axis=0, keepdims=True), sink[e])
                    ex = jnp.exp(sh - m)
                    l = jnp.sum(ex, axis=0, keepdims=True) + jnp.exp(sink[e] - m)
                    ps.append(ex.astype(BF16))
                    inv.append(1.0 / l)
                    lse_rows.append(m + jnp.log(l))
                pt = jnp.concatenate(ps, axis=0)
                ot = lax.dot_general(vcat, pt, TN, preferred_element_type=F32) * jnp.where(upper, inv[0], inv[1])
                ov = ot.T
                for pr in range(B_STACK):
                    pair = B_STACK * gi + pr
                    o_ref[:, pair * LANES:(pair + 1) * LANES] = ov[pr * TQ:(pr + 1) * TQ]
            lse_ref[0] = jnp.concatenate(lse_rows + [jnp.zeros((8 - len(lse_rows), B_ROWS), F32)], axis=0)
            sg, _ = _silu_parts(g_ref[...])
            u_ref[...] = (o_ref[...] * sg).astype(BF16)

        @pl.when(b < left)
        def _():
            step(True)

        @pl.when(b >= left)
        def _():
            step(False)

    row = pl.BlockSpec((TQ, D_MODEL), lambda b: (b, 0))
    return pl.pallas_call(
        body, name="attn_b_fwd", grid=(nb,),
        in_specs=[row, pl.BlockSpec((pad + s, B_KVX), lambda b: (0, 0)), row,
                  pl.BlockSpec((HEADS, 1, B_WIDE), lambda b: (0, 0, 0)), pl.BlockSpec((1, HEADS), lambda b: (0, 0))],
        out_specs=[row, row, pl.BlockSpec((1, 8, B_ROWS), lambda b: (b, 0, 0))],
        out_shape=[jax.ShapeDtypeStruct((s, D_MODEL), F32), jax.ShapeDtypeStruct((s, D_MODEL), BF16),
                   jax.ShapeDtypeStruct((nb, 8, B_ROWS), F32)],
        scratch_shapes=[pltpu.VMEM((B_KV_HEADS, 2 * B_WIN, B_ROWS), F32)],
        compiler_params=_params(("arbitrary",)),
    )(qb, kvx, gate, base, sinks)


def _attn_b_bwd(qb, kvx, gate, o, du, lse, base, sinks):
    s = qb.shape[0]
    pad = kvx.shape[0] - s
    nb = s // TQ
    left = B_KBLOCKS - 1
    half = D_MODEL // 2

    def body(q_ref, kv_ref, g_ref, o_ref, du_ref, lse_ref, base_ref, sink_ref, dz_ref, dkv_ref, dsum_ref,
             dsink_ref, bias_scr, dbias_acc, dkv_acc, dsink_acc):
        b = pl.program_id(0)

        @pl.when(b == 0)
        def _():
            _b_build_bias(base_ref, bias_scr)
            dbias_acc[...] = jnp.zeros_like(dbias_acc)
            dkv_acc[...] = jnp.zeros_like(dkv_acc)
            dsink_acc[...] = jnp.zeros_like(dsink_acc)

        def step(first_blocks):
            kvv = _window(kv_ref, b, pad, B_WIN, slice(None))
            sg, dsg = _silu_parts(g_ref[...])
            duv = du_ref[...]
            ov = o_ref[...]
            do = duv * sg
            dgate = (duv * ov * dsg).astype(BF16)
            dz_ref[2] = dgate[:, :half]
            dz_ref[3] = dgate[:, half:]
            do_o = do * ov
            do_bf = do.astype(BF16)
            lse_all = lse_ref[0]
            dsink_rows = []
            for gi in range(B_KV_HEADS):
                kcat, vcat, qs, sc = _b_scores_t(q_ref, kvv, bias_scr, gi, b, left, first_blocks)
                dos = _b_stack(do_bf, gi)
                doo_t = _b_stack(do_o, gi).T
                delta = (jnp.sum(doo_t[:HEAD_DIM], axis=0, keepdims=True),
                         jnp.sum(doo_t[HEAD_DIM:], axis=0, keepdims=True))
                sink = _b_sink_rows(sink_ref, gi)
                dp = lax.dot_general(vcat, dos, NT, preferred_element_type=F32)
                ps, dss = [], []
                for e in range(2):
                    lse_e = lse_all[2 * gi + e:2 * gi + e + 1]
                    delta_e = delta[e]
                    p = jnp.exp(sc[e * B_WIN:(e + 1) * B_WIN] - lse_e)
                    ps.append(p.astype(BF16))
                    dss.append(p * (dp[e * B_WIN:(e + 1) * B_WIN] - delta_e))
                    dsink_rows.append(-jnp.exp(sink[e] - lse_e) * delta_e)
                ds = jnp.concatenate(dss, axis=0)
                dbias_acc[gi] += ds
                dsb = ds.astype(BF16)
                dq = (lax.dot_general(kcat, dsb, TN, preferred_element_type=F32) * SCALE).T.astype(BF16)
                for pr in range(B_STACK):
                    dz_ref[gi, :, pr * LANES:(pr + 1) * LANES] = dq[pr * TQ:(pr + 1) * TQ]
                dk = _unstack_pair(jnp.dot(dsb, qs, preferred_element_type=F32), B_WIN)
                dv = _unstack_pair(jnp.dot(jnp.concatenate(ps, axis=0), dos, preferred_element_type=F32), B_WIN)
                krows = pl.ds(pl.multiple_of(b * TQ + pad - (B_WIN - TQ), KB), B_WIN)
                dkv_acc[krows, gi * LANES:(gi + 1) * LANES] += dk
                dkv_acc[krows, (B_KV_HEADS + gi) * LANES:(B_KV_HEADS + gi + 1) * LANES] += dv
            dsink_acc[...] += jnp.concatenate(
                dsink_rows + [jnp.zeros((8 - len(dsink_rows), B_ROWS), F32)], axis=0)

        @pl.when(b < left)
        def _():
            step(True)

        @pl.when(b >= left)
        def _():
            step(False)

        @pl.when(b == nb - 1)
        def _():
            lo_s = _lane_lo(s)
            for which in range(2):
                folded = []
                for gi in range(B_KV_HEADS):
                    part = dkv_acc[pad:pad + s, (which * B_KV_HEADS + gi) * LANES:(which * B_KV_HEADS + gi + 1) * LANES]
                    folded.append(part + pltpu.roll(part, HEAD_DIM, 1))
                dkv_ref[:, which * LANES:(which + 1) * LANES] = jnp.where(lo_s, folded[0], folded[1]).astype(BF16)
            lane8 = lax.broadcasted_iota(jnp.int32, dsink_ref.shape, 1)
            tot = jnp.zeros(dsink_ref.shape, F32)
            for h in range(HEADS):
                gi, pr, e = _b_head_place(h)
                dsum_ref[h] = _toeplitz_sum_t(
                    dbias_acc[gi, e * B_WIN:(e + 1) * B_WIN, pr * TQ:(pr + 1) * TQ], B_WIN)
                per_query = dsink_acc[2 * gi + e:2 * gi + e + 1, pr * TQ:(pr + 1) * TQ]
                tot = jnp.where(lane8 == h, jnp.sum(per_query, axis=1, keepdims=True), tot)
            dsink_ref[...] = tot

    row = pl.BlockSpec((TQ, D_MODEL), lambda b: (b, 0))
    base_spec = pl.BlockSpec((HEADS, 1, B_WIDE), lambda b: (0, 0, 0))
    return pl.pallas_call(
        body, name="attn_b_bwd", grid=(nb,),
        in_specs=[row, pl.BlockSpec((pad + s, B_KVX), lambda b: (0, 0)), row, row, row,
                  pl.BlockSpec((1, 8, B_ROWS), lambda b: (b, 0, 0)), base_spec,
                  pl.BlockSpec((1, HEADS), lambda b: (0, 0))],
        out_specs=[pl.BlockSpec((4, TQ, half), lambda b: (0, b, 0)),
                   pl.BlockSpec((s, 2 * LANES), lambda b: (0, 0)), base_spec,
                   pl.BlockSpec((8, LANES), lambda b: (0, 0))],
        out_shape=[jax.ShapeDtypeStruct((4, s, half), BF16), jax.ShapeDtypeStruct((s, 2 * LANES), BF16),
                   jax.ShapeDtypeStruct((HEADS, 1, B_WIDE), F32), jax.ShapeDtypeStruct((8, LANES), F32)],
        scratch_shapes=[pltpu.VMEM((B_KV_HEADS, 2 * B_WIN, B_ROWS), F32),
                        pltpu.VMEM((B_KV_HEADS, 2 * B_WIN, B_ROWS), F32),
                        pltpu.VMEM((pad + s, B_KVX), F32), pltpu.VMEM((8, B_ROWS), F32)],
        compiler_params=_params(("arbitrary",)),
    )(qb, kvx, gate, o, du, lse, base, sinks)


def _t5_bucket(rel):
    nb = T5_BUCKETS // 2
    max_exact = nb // 2
    ret = jnp.where(rel > 0, nb, 0)
    n = jnp.abs(rel)
    nf = jnp.maximum(n, 1).astype(jnp.float32)
    large = max_exact + (jnp.log(nf / max_exact) / math.log(T5_MAX_DIST / max_exact)
                         * (nb - max_exact)).astype(jnp.int32)
    large = jnp.minimum(large, nb - 1)
    return ret + jnp.where(n < max_exact, n, large)


def _a_offset_onehot():
    c = np.arange(A_WIN + TQ)
    dist = A_LEFT_CHUNKS * CHUNK + TQ - 1 - c
    idx = np.clip(dist, -A_REL_CLIP, A_REL_CLIP) + A_REL_CLIP
    onehot = np.zeros((A_WIN + TQ, 2 * A_REL_CLIP + 1), np.float32)
    onehot[c, idx] = 1.0
    return jnp.asarray(onehot)


def _b_offset_onehot():
    c = jnp.arange(B_WIN + TQ, dtype=jnp.int32)
    rel = c - (TQ - 1) - B_LEFT_CHUNKS * CHUNK
    return (_t5_bucket(rel)[:, None] == jnp.arange(T5_BUCKETS)[None, :]).astype(F32)


def _diag_rows(onehot, table):
    rows = jnp.dot(onehot, table.astype(F32), precision=lax.Precision.HIGHEST)
    return rows.T.reshape(HEADS, 1, onehot.shape[0])


def _diag_rows_grad(onehot, ddiag):
    return jnp.dot(ddiag.reshape(HEADS, onehot.shape[0]), onehot, precision=lax.Precision.HIGHEST)


def _position():
    x, y, c = lax.axis_index("x"), lax.axis_index("y"), lax.axis_index("c")
    chips = [(1 - x, y), (x, 1 - y), (1 - x, 1 - y)]
    return x, y, c, chips


ANY = pl.BlockSpec(memory_space=pl.ANY)


def _allgather_routed(shards):
    n = len(shards)

    def piece(block_ref, t, c, quarter=None):
        half = shards[t].shape[0] // 2
        if quarter is None:
            return block_ref.at[pl.ds(c * half, half)]
        return block_ref.at[pl.ds(c * half + quarter * (half // 2), half // 2)]

    def copies(kind, ins, outs, sems):
        ici_send, ici_recv, pass_send, pass_recv, local_sems = sems
        x, y, c, chips = _position()
        mine = 2 * x + y
        if kind == "local":
            return [pltpu.make_async_copy(ins[t], outs[t].at[mine], local_sems.at[t]) for t in range(n)]
        ids = [2 * chip[0] + chip[1] for chip in chips]
        made = []
        for t in range(n):
            def ici(k, to):
                return dict(send_sem=ici_send.at[4 * t + k], recv_sem=ici_recv.at[4 * t + k],
                            device_id=(chips[to][0], chips[to][1], c), device_id_type=MESH)

            def d2d(k):
                return dict(send_sem=pass_send.at[4 * t + k], recv_sem=pass_recv.at[4 * t + k],
                            device_id=(x, y, 1 - c), device_id_type=MESH)

            def same(ref, where):
                return pltpu.make_async_remote_copy(src_ref=ref, dst_ref=ref, **where)

            if kind == "send":
                for k in range(2):
                    made.append(pltpu.make_async_remote_copy(
                        src_ref=piece(ins[t], t, c), dst_ref=piece(outs[t].at[mine], t, c), **ici(k, k)))
            elif kind == "landed":
                made += [same(piece(outs[t].at[ids[k]], t, c), ici(k, k)) for k in range(2)]
            elif kind == "forward":
                made.append(same(piece(outs[t].at[ids[0]], t, c, 0), ici(2, 1)))
                made.append(same(piece(outs[t].at[ids[1]], t, c, 1), ici(3, 0)))
            elif kind == "arrived":
                made.append(same(piece(outs[t].at[ids[2]], t, c, 0), ici(2, 1)))
                made.append(same(piece(outs[t].at[ids[2]], t, c, 1), ici(3, 0)))
            else:
                core = 1 - c if kind == "passed" else c
                if kind in ("pass halves", "passed"):
                    made += [same(piece(outs[t].at[ids[k]], t, core), d2d(k)) for k in range(2)]
                if kind in ("pass quarters", "passed"):
                    made += [same(piece(outs[t].at[ids[2]], t, core, k), d2d(2 + k)) for k in range(2)]
        return made

    def first(ins, outs, sems):
        for cp in copies("local", ins, outs, sems) + copies("send", ins, outs, sems):
            cp.start()

    def middle(ins, outs, sems):
        for got, onward, near in zip(copies("landed", ins, outs, sems), copies("forward", ins, outs, sems),
                                     copies("pass halves", ins, outs, sems)):
            got.wait_recv()
            near.start()
            onward.start()

    def last(ins, outs, sems):
        quarters = copies("pass quarters", ins, outs, sems)
        for got, near in zip(copies("arrived", ins, outs, sems), quarters):
            got.wait_recv()
            near.start()
        for cp in copies("passed", ins, outs, sems):
            cp.wait_recv()
        for cp in (copies("send", ins, outs, sems) + copies("forward", ins, outs, sems)
                   + copies("pass halves", ins, outs, sems) + quarters):
            cp.wait_send()
        for cp in copies("local", ins, outs, sems):
            cp.wait()

    return _Hosted(shards, [jax.ShapeDtypeStruct((4,) + w.shape, w.dtype) for w in shards],
                   [pltpu.SemaphoreType.DMA((4 * n,))] * 4 + [pltpu.SemaphoreType.DMA((n,))],
                   first, middle, last)


def _scatter_hosted(grads):
    n = len(grads)

    def copies(ins, outs, sems):
        send_sems, recv_sems = sems
        x, y, c, chips = _position()
        return [pltpu.make_async_remote_copy(
            src_ref=ins[t].at[2 * chip[0] + chip[1]], dst_ref=outs[t].at[j],
            send_sem=send_sems.at[3 * t + j], recv_sem=recv_sems.at[3 * t + j],
            device_id=(chip[0], chip[1], c), device_id_type=MESH)
            for t in range(n) for j, chip in enumerate(chips)]

    def first(ins, outs, sems):
        for cp in copies(ins, outs, sems):
            cp.start()

    def last(ins, outs, sems):
        for cp in copies(ins, outs, sems):
            cp.wait()

    return _Hosted(grads, [jax.ShapeDtypeStruct((3,) + g.shape[1:], g.dtype) for g in grads],
                   [pltpu.SemaphoreType.DMA((3 * n,))] * 2, first, None, last)


GATHER_PEERS = "x and y neighbours (same core) and the sibling core"
SCATTER_PEERS = "the same core of the three other chips"
EVERYONE = "the seven other devices"


def _run_on_sequencer(name, hosted, peers, collective_id):
    ins = [jax.new_ref(a, memory_space=pltpu.MemorySpace.HBM) for a in hosted.inputs]
    outs = [jax.empty_ref(shape, memory_space=pltpu.MemorySpace.HBM) for shape in hosted.out_shapes]

    @pl.kernel(mesh=plsc.ScalarSubcoreMesh(axis_name="sequencer", num_cores=1), name=name,
               scratch_types=tuple(hosted.sems), compiler_params=pltpu.CompilerParams(collective_id=collective_id))
    def launch(*sems):
        x, y, c, chips = _position()
        if peers == GATHER_PEERS:
            devices = [(chip[0], chip[1], c) for chip in chips[:2]] + [(x, y, 1 - c)]
        elif peers == SCATTER_PEERS:
            devices = [(chip[0], chip[1], c) for chip in chips]
        else:
            devices = [(x ^ (k >> 2), y ^ ((k >> 1) & 1), c ^ (k & 1)) for k in range(1, 8)]
        barrier = pltpu.get_barrier_semaphore()
        for device in devices:
            pl.semaphore_signal(barrier, inc=1, device_id=device, device_id_type=MESH)
        pl.semaphore_wait(barrier, len(devices))
        hosted.first(ins, outs, sems)
        if hosted.middle is not None:
            hosted.middle(ins, outs, sems)
        hosted.last(ins, outs, sems)

    launch()
    return [o[...] for o in outs]


def _gather_gain(shard):
    def body(in_ref, out_ref, send_sems, recv_sems):
        x, y, c, chips = _position()
        out_ref[2 * x + y] = in_ref[...]
        sends = [pltpu.make_async_remote_copy(
            src_ref=in_ref, dst_ref=out_ref.at[2 * x + y], send_sem=send_sems.at[j], recv_sem=recv_sems.at[j],
            device_id=(chip[0], chip[1], c), device_id_type=MESH) for j, chip in enumerate(chips)]
        for cp in sends:
            cp.start()
        for j, chip in enumerate(chips):
            pltpu.make_async_remote_copy(
                src_ref=in_ref, dst_ref=out_ref.at[2 * chip[0] + chip[1]], send_sem=send_sems.at[j],
                recv_sem=recv_sems.at[j], device_id=(chip[0], chip[1], c), device_id_type=MESH).wait_recv()
        for cp in sends:
            cp.wait_send()

    vmem = pl.BlockSpec(memory_space=pltpu.VMEM)
    return pl.pallas_call(
        body, name="gather_gain", in_specs=[vmem], out_specs=vmem,
        out_shape=jax.ShapeDtypeStruct((4,) + shard.shape, shard.dtype),
        scratch_shapes=[pltpu.SemaphoreType.DMA((3,))] * 2,
    )(shard)


def _swap_with_sibling(name, blocks):
    n = len(blocks)

    def body(*refs):
        ins, outs = refs[:n], refs[n:2 * n]
        send_sems, recv_sems = refs[2 * n:]
        x, y, c, _ = _position()
        sends = [pltpu.make_async_remote_copy(
            src_ref=ins[t], dst_ref=outs[t], send_sem=send_sems.at[t], recv_sem=recv_sems.at[t],
            device_id=(x, y, 1 - c), device_id_type=MESH) for t in range(n)]
        for cp in sends:
            cp.start()
        for cp in sends:
            cp.wait()

    return pl.pallas_call(
        body, name=name,
        in_specs=[ANY] * n, out_specs=[ANY] * n,
        out_shape=[jax.ShapeDtypeStruct(b.shape, b.dtype) for b in blocks],
        scratch_shapes=[pltpu.SemaphoreType.DMA((n,))] * 2,
    )(*blocks)


def _everyone_hosted(terms):
    nt = len(terms)

    def copies(kind, ins, outs, sems):
        send_sems, recv_sems, local_sems = sems
        x, y, c, _ = _position()
        me = 4 * x + 2 * y + c
        if kind == "local":
            return [pltpu.make_async_copy(ins[t], outs[t].at[me], local_sems.at[t]) for t in range(nt)]
        made = []
        for t in range(nt):
            for k in range(1, 8):
                peer = (x ^ (k >> 2), y ^ ((k >> 1) & 1), c ^ (k & 1))
                slot = me if kind == "send" else me ^ k
                made.append(pltpu.make_async_remote_copy(
                    src_ref=ins[t], dst_ref=outs[t].at[slot], send_sem=send_sems.at[7 * t + k - 1],
                    recv_sem=recv_sems.at[7 * t + k - 1], device_id=peer, device_id_type=MESH))
        return made

    def first(ins, outs, sems):
        for cp in copies("local", ins, outs, sems) + copies("send", ins, outs, sems):
            cp.start()

    def last(ins, outs, sems):
        for cp in copies("landed", ins, outs, sems):
            cp.wait_recv()
        for cp in copies("send", ins, outs, sems):
            cp.wait_send()
        for cp in copies("local", ins, outs, sems):
            cp.wait()

    return _Hosted(terms, [jax.ShapeDtypeStruct((8,) + a.shape, F32) for a in terms],
                   [pltpu.SemaphoreType.DMA((7 * nt,))] * 2 + [pltpu.SemaphoreType.DMA((nt,))], first, None, last)


def _small_step(partials, extras, ws, ms, vs, shard_of):
    n = len(partials)
    terms = list(partials) + list(extras)
    nt = len(terms)
    rows = [t for t in range(nt) if terms[t].shape[0] == 1]
    mats = [t for t in range(nt) if terms[t].shape[0] != 1]
    row_block = (8, max(terms[t].shape[1] for t in rows))
    assert len(rows) <= row_block[0]
    vmem = pl.BlockSpec(memory_space=pltpu.VMEM)

    def pack(*refs):
        packed = refs[-1]
        packed[...] = jnp.zeros_like(packed)
        for i, t in enumerate(rows):
            packed[i:i + 1, 0:terms[t].shape[1]] = refs[i][...]

    packed = pl.pallas_call(pack, name="small_pack", in_specs=[vmem] * len(rows), out_specs=vmem,
                            out_shape=jax.ShapeDtypeStruct(row_block, F32))(*[terms[t] for t in rows])
    slots = _run_on_sequencer("allgather_small", _everyone_hosted([packed] + [terms[t] for t in mats]),
                              EVERYONE, 2)

    def body(*refs):
        slot_refs, refs = refs[:len(slots)], refs[len(slots):]
        w_refs, refs = refs[:n], refs[n:]
        m_refs, refs = refs[:n], refs[n:]
        v_refs, outs = refs[:n], refs[n:]
        sums = []
        for ref in slot_refs:
            g = ref[0]
            for dev in range(1, 8):
                g = g + ref[dev]
            sums.append(g)
        chip = 2 * lax.axis_index("x") + lax.axis_index("y")
        for t in range(nt):
            if t in rows:
                i = rows.index(t)
                g = sums[0][i:i + 1, 0:terms[t].shape[1]]
            else:
                g = sums[1 + mats.index(t)]
            if t >= n:
                outs[4 * n + t - n][...] = g
                continue
            if shard_of[t]:
                width = ws[t].shape[-1]
                mine = jnp.zeros(ws[t].shape, F32)
                for s in range(4):
                    mine = jnp.where(chip == s, g[:, s * width:(s + 1) * width], mine)
                g = mine
            delta, mn, vn = _adamw_math(w_refs[t][...], g, m_refs[t][...], v_refs[t][...])
            outs[4 * t][...] = g
            outs[4 * t + 1][...] = delta
            outs[4 * t + 2][...] = mn
            outs[4 * t + 3][...] = vn

    out_shapes = []
    for t in range(n):
        out_shapes += [jax.ShapeDtypeStruct(ws[t].shape, F32)] * 4
    out_shapes += [jax.ShapeDtypeStruct(a.shape, F32) for a in extras]
    res = pl.pallas_call(
        body, name="small_step",
        in_specs=[vmem] * (len(slots) + 3 * n), out_specs=[vmem] * len(out_shapes), out_shape=out_shapes,
    )(*slots, *ws, *ms, *vs)
    return [res[4 * t:4 * t + 4] for t in range(n)], res[4 * n:4 * n + nt - n]


def _adamw_math(w, g, m, v):
    m = ADAM_B1 * m + (1.0 - ADAM_B1) * g
    v = ADAM_B2 * v + (1.0 - ADAM_B2) * (g * g)
    m_hat = m / (1.0 - ADAM_B1 ** ADAM_STEP)
    v_hat = v / (1.0 - ADAM_B2 ** ADAM_STEP)
    delta = -ADAM_LR * (m_hat / (jnp.sqrt(v_hat) + ADAM_EPS) + ADAM_WD * w)
    return delta, m, v


def _row_tile(rows):
    return 256 if rows % 256 == 0 else rows


def _sum_partials(name, own, recv, chip, after):
    rows, cols = own.shape[1:]
    tr = _row_tile(rows)

    def body(chip_ref, own_ref, recv_ref, after_ref, o_ref):
        acc = own_ref[...]
        for j in range(3):
            acc = acc + recv_ref[j].astype(F32)
        o_ref[...] = acc

    return pl.pallas_call(
        body, name=name,
        grid_spec=pltpu.PrefetchScalarGridSpec(
            num_scalar_prefetch=1, grid=(rows // tr,),
            in_specs=[pl.BlockSpec((None, tr, cols), lambda i, chip_ref: (chip_ref[0], i, 0)),
                      pl.BlockSpec((3, tr, cols), lambda i, chip_ref: (0, i, 0)), ANY],
            out_specs=pl.BlockSpec((tr, cols), lambda i, chip_ref: (i, 0))),
        out_shape=jax.ShapeDtypeStruct((rows, cols), F32),
        compiler_params=_params(("parallel",)),
    )(chip.reshape(1).astype(jnp.int32), own, recv, after)


def _adamw(name, w, m, v, g_parts):
    rows, cols = w.shape
    tr = _row_tile(rows)
    n = len(g_parts)

    def body(w_ref, m_ref, v_ref, *refs):
        g_refs = refs[:n]
        go_ref, d_ref, mo_ref, vo_ref = refs[n:]
        g = g_refs[0][...]
        for r in g_refs[1:]:
            g = g + r[...]
        delta, mn, vn = _adamw_math(w_ref[...], g, m_ref[...], v_ref[...])
        go_ref[...] = g
        d_ref[...] = delta
        mo_ref[...] = mn
        vo_ref[...] = vn

    spec = pl.BlockSpec((tr, cols), lambda i: (i, 0))
    return pl.pallas_call(
        body, name=name, grid=(rows // tr,),
        in_specs=[spec] * (3 + n), out_specs=[spec] * 4,
        out_shape=[jax.ShapeDtypeStruct((rows, cols), F32)] * 4,
        compiler_params=_params(("parallel",)),
    )(w, m, v, *g_parts)


def _local_step(x, target, ga, wa_in, rel_bias, later_shards, gk, t5, gb, sinks, gf):
    s, d = x.shape
    tm = min(TM_DENSE, s)
    nt = s // tm
    half = d // 2
    row = pl.BlockSpec((tm, d), lambda i: (i, 0))
    whole = lambda shape: pl.BlockSpec(shape, lambda *_: (0,) * len(shape))

    n1, = _norm_fwd("norm_a", x, ga)
    zqkv = gate_a = None
    ta = min(TM_HALF, s)
    for h, (wa_half, tag) in enumerate(zip(wa_in, ("first", "second"))):
        zqkv = _matmul("proj_a_qkv_" + tag, n1, wa_half, dims=NN, grid=(3, s // ta + 1), zero_axis=1, into=zqkv,
                       a_spec=pl.BlockSpec((ta, d), lambda j, i: (jnp.maximum(i - 1, 0), 0)),
                       b_spec=pl.BlockSpec((None, d, half), lambda j, i: (j, 0, 0)),
                       o_spec=pl.BlockSpec((None, ta, half), lambda j, i, h=h: (j, i, h)),
                       out_shape=(3, ta + s, d), out_dtype=BF16)
        gate_a = _matmul("proj_a_gate_" + tag, n1, wa_half, dims=NN, grid=(s // ta,), into=gate_a,
                         a_spec=pl.BlockSpec((ta, d), lambda i: (i, 0)),
                         b_spec=pl.BlockSpec((None, d, half), lambda i: (3, 0, 0)),
                         o_spec=pl.BlockSpec((ta, half), lambda i, h=h: (i, h)),
                         out_shape=(s, d), out_dtype=F32)
    onehot_a = _a_offset_onehot()
    diag_a = _diag_rows(onehot_a, rel_bias)
    (o_a, u_a, lse_a), gathered = _attn_a_fwd(zqkv, gate_a, diag_a, hosted=_allgather_routed(later_shards))
    wa_out, wkv, wb_in, wb_out, wkv_x = gathered
    wa_out = wa_out.reshape(d, d)
    wkv = wkv.reshape(d, -1)
    wkv_x = wkv_x.reshape(d, B_KVX)
    wb_out = wb_out.reshape(d, d)
    h1, nk, n2 = _out_norms("out_a_norms", u_a, wa_out, x, jnp.concatenate([gk, gb], axis=0))
    kvw = wkv.shape[1]
    kvx =_matmul("proj_kv", nk, wkv_x, dims=NN, grid=(nt + 1,), zero_axis=0,
                  a_spec=pl.BlockSpec((tm, d), lambda i: (jnp.maximum(i - 1, 0), 0)), b_spec=whole((d, B_KVX)),
                  o_spec=pl.BlockSpec((tm, B_KVX), lambda i: (i, 0)), out_shape=(tm + s, B_KVX), out_dtype=BF16)
    qb = _matmul("proj_b_q", n2, wb_in, dims=NN, grid=(2, nt),
                 a_spec=pl.BlockSpec((tm, d), lambda j, i: (i, 0)),
                 b_spec=pl.BlockSpec((None, d, half), lambda j, i: (j, 0, 0)),
                 o_spec=pl.BlockSpec((tm, half), lambda j, i: (i, j)), out_shape=(s, d), out_dtype=BF16)
    gate_b = _matmul("proj_b_gate", n2, wb_in, dims=NN, grid=(2, nt),
                     a_spec=pl.BlockSpec((tm, d), lambda j, i: (i, 0)),
                     b_spec=pl.BlockSpec((None, d, half), lambda j, i: (2 + j, 0, 0)),
                     o_spec=pl.BlockSpec((tm, half), lambda j, i: (i, j)), out_shape=(s, d), out_dtype=F32)
    onehot_b = _b_offset_onehot()
    base_b = jnp.roll(_diag_rows(onehot_b, t5)[..., ::-1], TQ, axis=-1)
    o_b, u_b, lse_b = _attn_b_fwd(qb, kvx, gate_b, base_b, sinks)
    dh2, loss, d_gf = _out_loss_head(u_b, wb_out, h1, target, gf)

    du_b = _matmul("dout_b", dh2, wb_out, dims=NT, grid=(nt,), a_spec=row, b_spec=whole((d, d)), o_spec=row,
                   out_shape=(s, d), out_dtype=F32)
    d_wb_out = _matmul("dw_out_b", u_b, dh2, dims=TN, grid=(2,),
                       a_spec=whole((s, d)), b_spec=pl.BlockSpec((s, half), lambda j: (0, j)),
                       o_spec=pl.BlockSpec((d, half), lambda j: (0, j)),
                       out_shape=(d, d), out_dtype=F32, also_bf16=True)
    dz_b, dkv, dsum_b, dsinks = _attn_b_bwd(qb, kvx, gate_b, o_b, du_b, lse_b, base_b, sinks)
    ddiag_b = jnp.roll(dsum_b[..., ::-1], -1, axis=-1)
    d_wb_in = _matmul("dw_in_b", n2, dz_b, dims=TN, grid=(4,),
                      a_spec=whole((s, d)), b_spec=pl.BlockSpec((None, s, half), lambda j: (j, 0, 0)),
                      o_spec=pl.BlockSpec((None, d, half), lambda j: (j, 0, 0)),
                      out_shape=(4, d, half), out_dtype=F32, also_bf16=True)
    d_wkv = _matmul("dw_kv", nk, dkv, dims=TN, grid=(1,),
                    a_spec=whole((s, d)), b_spec=whole((s, kvw)), o_spec=whole((d, kvw)),
                    out_shape=(d, kvw), out_dtype=F32, also_bf16=True)
    dh1, d_gkb = _proj_norm_bwd("dproj_kv_b", h1, dh2, jnp.concatenate([gk, gb], axis=0),
                                [(dkv[None], [wkv[None]]), (dz_b, [wb_in])])

    du_a = _matmul("dout_a", dh1, wa_out, dims=NT, grid=(nt,), a_spec=row, b_spec=whole((d, d)), o_spec=row,
                   out_shape=(s, d), out_dtype=F32)
    d_wa_out = _matmul("dw_out_a", u_a, dh1, dims=TN, grid=(2,),
                       a_spec=whole((s, d)), b_spec=pl.BlockSpec((s, half), lambda j: (0, j)),
                       o_spec=pl.BlockSpec((d, half), lambda j: (0, j)),
                       out_shape=(d, d), out_dtype=F32, also_bf16=True)
    early = dict(a_w_out=[g.reshape(4, d // 4, d) for g in d_wa_out],
                 kv_w=[g.reshape(4, d // 4, kvw) for g in d_wkv], b_w_in=list(d_wb_in),
                 b_w_out=[g.reshape(4, d // 4, d) for g in d_wb_out])
    (dz_a, ddiag_a), early_recv = _attn_a_bwd(
        zqkv, gate_a, o_a, du_a, lse_a, diag_a, hosted=_scatter_hosted([early[n][1] for n in early]))
    d_wa_in = _matmul("dw_in_a", n1, dz_a, dims=TN, grid=(4, 2),
                      a_spec=whole((s, d)), b_spec=pl.BlockSpec((None, s, half), lambda j, h: (j, 0, h)),
                      o_spec=pl.BlockSpec((None, d, half), lambda j, h: (j, 0, h)),
                      out_shape=(4, d, d), out_dtype=F32, also_bf16=True)
    late_recv = _run_on_sequencer("scatter_a_w_in", _scatter_hosted([d_wa_in[1]]), SCATTER_PEERS, 0)
    grad_x, d_ga = _proj_norm_bwd("dproj_a", x, dh1, ga, [(dz_a, list(wa_in))])

    small = dict(a_norm=d_ga, kv_norm=d_gkb[0:1], b_norm=d_gkb[1:2], b_sinks=dsinks[0:1, :HEADS], final_norm=d_gf)
    small["by_offset"] = dict(a_rel_bias=(onehot_a, ddiag_a.reshape(HEADS, -1)),
                              t5_bias=(onehot_b, ddiag_b.reshape(HEADS, -1)))
    own = dict(a_w_in=d_wa_in[0], **{n: early[n][0] for n in early})
    received = dict(a_w_in=late_recv[0], **dict(zip(early, early_recv)))
    return loss, grad_x, small, own, received, d_wa_in[1]


SMALL = ("a_norm", "kv_norm", "b_norm", "b_sinks", "final_norm")
TABLES = ("a_rel_bias", "t5_bias")
BIG = ("a_w_in", "a_w_out", "kv_w", "b_w_in", "b_w_out")
ORDER = ("a_norm", "a_w_in", "a_rel_bias", "a_w_out", "kv_norm", "kv_w", "t5_bias", "b_norm", "b_w_in",
         "b_sinks", "b_w_out", "final_norm")


def kernel(x, a_norm, a_w_in, a_rel_bias, a_w_out, kv_norm, kv_w, t5_bias, b_norm, b_w_in, b_sinks, b_w_out, final_norm, loss_target, m_a_norm, m_a_w_in, m_a_rel_bias, m_a_w_out, m_kv_norm, m_kv_w, m_t5_bias, m_b_norm, m_b_w_in, m_b_sinks, m_b_w_out, m_final_norm, v_a_norm, v_a_w_in, v_a_rel_bias, v_a_w_out, v_kv_norm, v_kv_w, v_t5_bias, v_b_norm, v_b_w_in, v_b_sinks, v_b_w_out, v_final_norm):
    w = dict(a_norm=a_norm, a_w_in=a_w_in, a_rel_bias=a_rel_bias, a_w_out=a_w_out, kv_norm=kv_norm, kv_w=kv_w,
             t5_bias=t5_bias, b_norm=b_norm, b_w_in=b_w_in, b_sinks=b_sinks, b_w_out=b_w_out,
             final_norm=final_norm)
    m = dict(a_norm=m_a_norm, a_w_in=m_a_w_in, a_rel_bias=m_a_rel_bias, a_w_out=m_a_w_out, kv_norm=m_kv_norm,
             kv_w=m_kv_w, t5_bias=m_t5_bias, b_norm=m_b_norm, b_w_in=m_b_w_in, b_sinks=m_b_sinks,
             b_w_out=m_b_w_out, final_norm=m_final_norm)
    v = dict(a_norm=v_a_norm, a_w_in=v_a_w_in, a_rel_bias=v_a_rel_bias, a_w_out=v_a_w_out, kv_norm=v_kv_norm,
             kv_w=v_kv_w, t5_bias=v_t5_bias, b_norm=v_b_norm, b_w_in=v_b_w_in, b_sinks=v_b_sinks,
             b_w_out=v_b_w_out, final_norm=v_final_norm)
    d = D_MODEL
    chip = 2 * lax.axis_index("x") + lax.axis_index("y")

    shard2d = dict(a_w_in=a_w_in[0], a_w_out=a_w_out[0], kv_w=kv_w, b_w_in=b_w_in[0], b_w_out=b_w_out[0])

    first = shard2d["a_w_in"].astype(BF16)
    wa_in = [_run_on_sequencer("allgather_" + tag, _allgather_routed([first[:, h * (d // 2):(h + 1) * (d // 2)]]),
                               GATHER_PEERS, collective_id)[0]
             for h, (tag, collective_id) in enumerate((("first", 1), ("second", 3)))]
    ga = _gather_gain(a_norm).reshape(1, d)

    later = [shard2d[n].astype(BF16) for n in BIG[1:]]
    kv_shard = later[BIG[1:].index("kv_w")]
    later.append(jnp.concatenate(
        [kv_shard[:, (i // 2) * HEAD_DIM:(i // 2 + 1) * HEAD_DIM] for i in range(B_KVX // HEAD_DIM)], axis=1))
    loss, grad_x, small, own, received, after_attention = _local_step(
        x[0], loss_target[0], ga, wa_in, a_rel_bias[0], later,
        kv_norm.reshape(1, d), t5_bias, b_norm, b_sinks, final_norm.reshape(1, d))

    out = {}
    as2d = lambda a: a.reshape(-1, a.shape[-1])
    small_res, (loss_sum, *offset_sums) = _small_step(
        [small[n] for n in SMALL], [loss] + [small["by_offset"][n][1] for n in TABLES],
        [as2d(w[n]) for n in SMALL], [as2d(m[n]) for n in SMALL], [as2d(v[n]) for n in SMALL],
        [n == "a_norm" for n in SMALL])
    for n, res in zip(SMALL, small_res):
        out[n] = [r.reshape(w[n].shape) for r in res]
    loss_out = loss_sum.reshape(())
    for n, summed in zip(TABLES, offset_sums):
        grad = _diag_rows_grad(small["by_offset"][n][0], summed)
        res = _adamw("adamw_" + n, as2d(w[n]).T, as2d(m[n]).T, as2d(v[n]).T, [grad])
        out[n] = [r.T.reshape(w[n].shape) for r in res]

    core_sums = [_sum_partials("sum_" + n, own[n], received[n], chip, after_attention) for n in BIG]
    sibling_sums = (_swap_with_sibling("swap_last", core_sums[:1])
                    + _swap_with_sibling("swap_early", core_sums[1:]))

    for n, mine, theirs in zip(BIG, core_sums, sibling_sums):
        res = _adamw("adamw_" + n, shard2d[n], m[n].reshape(shard2d[n].shape), v[n].reshape(shard2d[n].shape),
                     [mine, theirs])
        out[n] = [r.reshape(w[n].shape) for r in res]

    grads = [out[n][0] for n in ORDER]
    deltas = [out[n][1] for n in ORDER]
    new_m = [out[n][2] for n in ORDER]
    new_v = [out[n][3] for n in ORDER]
    return (loss_out, grad_x[None], *grads, *deltas, *new_m, *new_v)
```

```python
import math

import jax
import jax.numpy as jnp
import numpy as np
from jax import lax
from jax.experimental import pallas as pl
from jax.experimental.pallas import tpu as pltpu
from jax.experimental.pallas import tpu_sc as plsc

F32 = jnp.float32
BF16 = jnp.bfloat16
MESH = pl.DeviceIdType.MESH

D_MODEL = 1024
HEADS = 16
HEAD_DIM = 64
CHUNK = 64
RMS_EPS = 1e-6
SCALE = HEAD_DIM ** -0.5
A_LEFT_CHUNKS = 8
A_REL_CLIP = 256
B_LEFT_CHUNKS = 2
B_KV_HEADS = 2
B_GROUP = HEADS // B_KV_HEADS
T5_BUCKETS = 32
T5_MAX_DIST = 128
ADAM_LR = 0.001
ADAM_B1 = 0.9
ADAM_B2 = 0.999
ADAM_EPS = 1e-08
ADAM_WD = 0.01
ADAM_STEP = 10

MASKED = -1e30
LANES = 128
TQ = 128
A_PAIRS = 2
A_PAIRS_FWD = 4
KB = 128
A_KBLOCKS = A_LEFT_CHUNKS * CHUNK // KB + 1
B_KBLOCKS = B_LEFT_CHUNKS * CHUNK // KB + 1
A_WIN = A_KBLOCKS * KB
B_WIN = B_KBLOCKS * KB
TM = 512
TM_DENSE = 1024
TM_HALF = 2048
TM_PARTS = 512
VMEM_LIMIT = 56 * 1024 * 1024

NT = (((1,), (1,)), ((), ()))
TN = (((0,), (0,)), ((), ()))
NN = (((1,), (0,)), ((), ()))


def _params(sem=None):
    return pltpu.CompilerParams(dimension_semantics=sem, vmem_limit_bytes=VMEM_LIMIT)


class _Hosted:
    def __init__(self, inputs, out_shapes, sems, first, middle, last):
        self.inputs, self.out_shapes, self.sems = list(inputs), list(out_shapes), list(sems)
        self.first, self.middle, self.last = first, middle, last


def _call(body, *, name, grid, in_specs, out_specs, out_shape, args, scratch_shapes=(), sem=None, hosted=None,
          aliases=None):
    in_specs, out_specs, out_shape = list(in_specs), list(out_specs), list(out_shape)
    scratch_shapes = list(scratch_shapes)
    if hosted is None:
        out = pl.pallas_call(
            body, name=name, grid=grid, in_specs=in_specs, out_specs=out_specs, out_shape=out_shape,
            scratch_shapes=scratch_shapes, input_output_aliases=aliases or {},
            compiler_params=_params(sem))(*args)
        return list(out), []
    assert aliases is None
    n_in, n_out, n_scr = len(in_specs), len(out_shape), len(scratch_shapes)
    h_in, h_out = len(hosted.inputs), len(hosted.out_shapes)
    total = int(np.prod(grid)) if grid else 1

    def wrapped(*refs):
        ins, refs = refs[:n_in], refs[n_in:]
        h_ins, refs = refs[:h_in], refs[h_in:]
        outs, refs = refs[:n_out], refs[n_out:]
        h_outs, refs = refs[:h_out], refs[h_out:]
        scr, h_sems = refs[:n_scr], refs[n_scr:]
        step = 0
        for axis, size in enumerate(grid):
            step = step * size + pl.program_id(axis)

        if hosted.first is not None:
            @pl.when(step == 0)
            def _():
                hosted.first(h_ins, h_outs, h_sems)

        body(*ins, *outs, *scr)
        if hosted.middle is not None:
            @pl.when(step == total // 2)
            def _():
                hosted.middle(h_ins, h_outs, h_sems)

        if hosted.last is not None:
            @pl.when(step == total - 1)
            def _():
                hosted.last(h_ins, h_outs, h_sems)

    out = pl.pallas_call(
        wrapped, name=name, grid=grid, in_specs=in_specs + [ANY] * h_in, out_specs=out_specs + [ANY] * h_out,
        out_shape=out_shape + hosted.out_shapes, scratch_shapes=scratch_shapes + hosted.sems,
        compiler_params=_params(("arbitrary",) * len(grid)))(*args, *hosted.inputs)
    return list(out[:n_out]), list(out[n_out:])


def _matmul(name, a, b, *, dims, grid, a_spec, b_spec, o_spec, out_shape, out_dtype,
            also_bf16=False, zero_axis=None):
    def body(*refs):
        if zero_axis is None:
            product(*refs)
        else:
            @pl.when(pl.program_id(zero_axis) == 0)
            def _():
                refs[2][...] = jnp.zeros_like(refs[2])

            @pl.when(pl.program_id(zero_axis) > 0)
            def _():
                product(*refs)

    def product(a_ref, b_ref, o_ref, *more):
        prod = lax.dot_general(a_ref[...].astype(BF16), b_ref[...].astype(BF16), dims,
                               preferred_element_type=F32)
        o_ref[...] = prod.astype(out_dtype)
        if also_bf16:
            more[0][...] = prod.astype(BF16)

    out_specs = [o_spec]
    out_shapes = [jax.ShapeDtypeStruct(out_shape, out_dtype)]
    if also_bf16:
        out_specs.append(o_spec)
        out_shapes.append(jax.ShapeDtypeStruct(out_shape, BF16))
    out, _ = _call(body, name=name, grid=grid, in_specs=[a_spec, b_spec], out_specs=out_specs,
                   out_shape=out_shapes, args=[a, b], sem=("parallel",) * len(grid))
    return out[0] if not also_bf16 else tuple(out)


def _proj_a_half(name, n1, w, h, into):
    s, d = n1.shape
    half = w.shape[2]
    ta = min(TM_HALF, s)

    def body(a_ref, w_ref, *refs):
        z_ref, g_ref = refs[-2:]
        i, j = pl.program_id(0), pl.program_id(1)

        @pl.when((i == 0) & (j < 3))
        def _():
            z_ref[...] = jnp.zeros_like(z_ref)

        @pl.when((i > 0) & (j < 3))
        def _():
            z_ref[...] = jnp.dot(a_ref[...], w_ref[...], preferred_element_type=F32).astype(BF16)

        @pl.when((i > 0) & (j == 3))
        def _():
            g_ref[...] = jnp.dot(a_ref[...], w_ref[...], preferred_element_type=F32)

    out, _ = _call(
        body, name=name, grid=(s // ta + 1, 4),
        in_specs=[pl.BlockSpec((ta, d), lambda i, j: (jnp.maximum(i - 1, 0), 0)),
                  pl.BlockSpec((None, d, half), lambda i, j: (j, 0, 0))] + ([] if into is None else [ANY, ANY]),
        out_specs=[pl.BlockSpec((None, ta, half), lambda i, j: (jnp.minimum(j, 2), i, h)),
                   pl.BlockSpec((ta, half), lambda i, j: (jnp.maximum(i - 1, 0), h))],
        out_shape=[jax.ShapeDtypeStruct((3, ta + s, d), BF16), jax.ShapeDtypeStruct((s, d), F32)],
        args=[n1, w] + ([] if into is None else list(into)), sem=("arbitrary", "arbitrary"),
        aliases=None if into is None else {2: 0, 3: 1})
    return out


def _rms_rows(x):
    return lax.rsqrt(jnp.mean(x * x, axis=-1, keepdims=True) + RMS_EPS)


def _norm_fwd(name, x, gains):
    s, d = x.shape
    n = gains.shape[0]

    def body(x_ref, g_ref, *o_refs):
        xv = x_ref[...]
        xh = xv * _rms_rows(xv)
        for i in range(n):
            o_refs[i][...] = (xh * g_ref[i:i + 1, :]).astype(BF16)

    row = pl.BlockSpec((TM, d), lambda i: (i, 0))
    return pl.pallas_call(
        body, name=name, grid=(s // TM,),
        in_specs=[row, pl.BlockSpec((n, d), lambda i: (0, 0))],
        out_specs=[row] * n,
        out_shape=[jax.ShapeDtypeStruct((s, d), BF16)] * n,
        compiler_params=_params(("parallel",)),
    )(x, gains)


def _proj_norm_bwd(name, x, dres, gains, branches):
    s, d = x.shape
    n = len(branches)
    n_ab = 2 * sum(len(bs) for _, bs in branches)
    tm = min(TM_PARTS, s)

    def body(x_ref, r_ref, g_ref, *refs):
        ab_refs, dx_ref, dg_ref = list(refs[:n_ab]), refs[n_ab], refs[n_ab + 1]
        i = pl.program_id(0)
        xv = x_ref[...]
        r = _rms_rows(xv)
        xh = xv * r

        @pl.when(i == 0)
        def _():
            dg_ref[...] = jnp.zeros_like(dg_ref)

        a = None
        for j in range(n):
            dn = None
            for _ in branches[j][1]:
                a_ref, b_ref = ab_refs.pop(0), ab_refs.pop(0)
                for part in range(a_ref.shape[0]):
                    term = lax.dot_general(a_ref[part], b_ref[part], NT, preferred_element_type=F32)
                    dn = term if dn is None else dn + term
            t = dn * g_ref[j:j + 1, :]
            a = t if a is None else a + t
            dg_ref[j:j + 1, :] += jnp.sum(dn * xh, axis=0, keepdims=True)
        dx_ref[...] = r_ref[...] + r * (a - xh * jnp.mean(xh * a, axis=-1, keepdims=True))

    row = pl.BlockSpec((tm, d), lambda i: (i, 0))
    small = pl.BlockSpec((n, d), lambda i: (0, 0))
    ab_specs, ab_args = [], []
    for a, bs in branches:
        for k, b in enumerate(bs):
            ab_specs += [pl.BlockSpec((a.shape[0], tm, b.shape[2]), lambda i, k=k: (0, i, k)),
                         pl.BlockSpec(b.shape, lambda i: (0, 0, 0))]
            ab_args += [a, b]
    return pl.pallas_call(
        body, name=name, grid=(s // tm,),
        in_specs=[row, row, small] + ab_specs,
        out_specs=[row, small],
        out_shape=[jax.ShapeDtypeStruct((s, d), F32), jax.ShapeDtypeStruct((n, d), F32)],
        compiler_params=_params(("arbitrary",)),
    )(x, dres, gains, *ab_args)


def _out_norms(name, u, w_out, resid, gains):
    s, d = resid.shape
    n = gains.shape[0]
    tm = min(TM_DENSE, s)

    def body(u_ref, w_ref, r_ref, g_ref, h_ref, *o_refs):
        hv = r_ref[...] + jnp.dot(u_ref[...], w_ref[...], preferred_element_type=F32)
        h_ref[...] = hv
        hh = hv * _rms_rows(hv)
        for i in range(n):
            o_refs[i][...] = (hh * g_ref[i:i + 1, :]).astype(BF16)

    row = pl.BlockSpec((tm, d), lambda i: (i, 0))
    return pl.pallas_call(
        body, name=name, grid=(s // tm,),
        in_specs=[row, pl.BlockSpec((d, d), lambda i: (0, 0)), row, pl.BlockSpec((n, d), lambda i: (0, 0))],
        out_specs=[row] * (n + 1),
        out_shape=[jax.ShapeDtypeStruct((s, d), F32)] + [jax.ShapeDtypeStruct((s, d), BF16)] * n,
        compiler_params=_params(("parallel",)),
    )(u, w_out, resid, gains)


def _out_loss_head(u, w_out, resid, target, gain):
    s, d = resid.shape
    tm = min(TM_PARTS, s)

    def body(u_ref, w_ref, r_ref, t_ref, g_ref, dh_ref, loss_ref, dg_ref):
        i = pl.program_id(0)
        hv = r_ref[...] + jnp.dot(u_ref[...], w_ref[...], preferred_element_type=F32)
        r = _rms_rows(hv)
        hh = hv * r
        g = g_ref[...]
        err = hh * g - t_ref[...]
        part = 0.5 * jnp.sum(jnp.sum(err * err, axis=-1, keepdims=True) * (1.0 / d), axis=0, keepdims=True)
        dy = err * (1.0 / d)
        a = dy * g
        dh_ref[...] = r * (a - hh * jnp.mean(hh * a, axis=-1, keepdims=True))
        dg = jnp.sum(dy * hh, axis=0, keepdims=True)

        @pl.when(i == 0)
        def _():
            loss_ref[...] = part
            dg_ref[...] = dg

        @pl.when(i > 0)
        def _():
            loss_ref[...] += part
            dg_ref[...] += dg

    row = pl.BlockSpec((tm, d), lambda i: (i, 0))
    return pl.pallas_call(
        body, name="out_b_loss_head", grid=(s // tm,),
        in_specs=[row, pl.BlockSpec((d, d), lambda i: (0, 0)), row, row, pl.BlockSpec((1, d), lambda i: (0, 0))],
        out_specs=[row, pl.BlockSpec((1, 1), lambda i: (0, 0)), pl.BlockSpec((1, d), lambda i: (0, 0))],
        out_shape=[jax.ShapeDtypeStruct((s, d), F32), jax.ShapeDtypeStruct((1, 1), F32),
                   jax.ShapeDtypeStruct((1, d), F32)],
        compiler_params=_params(("arbitrary",)),
    )(u, w_out, resid, target, gain)


def _silu_parts(g):
    sig = jax.nn.sigmoid(g)
    return g * sig, sig * (1.0 + g * (1.0 - sig))


def _lane_lo(rows):
    return lax.broadcasted_iota(jnp.int32, (rows, LANES), 1) < HEAD_DIM


def _stack_pair(x):
    lo = _lane_lo(x.shape[0])
    zero = jnp.zeros_like(x)
    return jnp.concatenate([jnp.where(lo, x, zero), jnp.where(lo, zero, x)], axis=0)


def _unstack_pair(y, w):
    return jnp.where(_lane_lo(w), y[:w], y[w:])


def _block_valid(b, left_blocks, width):
    col = lax.broadcasted_iota(jnp.int32, (1, 2 * width), 1)
    col = jnp.where(col >= width, col - width, col)
    return (col // KB + (b - left_blocks)) >= 0


def _toeplitz_tile(diag_row, width, left_chunks):
    wide = width + TQ
    rolled = pltpu.roll(jnp.broadcast_to(diag_row, (TQ, wide)), 1, 1, stride=1, stride_axis=0)
    i = lax.broadcasted_iota(jnp.int32, (TQ, width), 0) // CHUNK
    j = lax.broadcasted_iota(jnp.int32, (TQ, width), 1) // CHUNK
    dc = i + left_chunks - j
    return jnp.where((dc >= 0) & (dc <= left_chunks), rolled[:, TQ:], MASKED)


def _toeplitz_sum(tile, width):
    flip = (lax.broadcasted_iota(jnp.int32, (TQ, TQ), 0) + lax.broadcasted_iota(jnp.int32, (TQ, TQ), 1)
            == TQ - 1).astype(F32)
    reversed_rows = jnp.dot(flip, tile, precision=lax.Precision.HIGHEST, preferred_element_type=F32)
    padded = jnp.concatenate([reversed_rows, jnp.zeros((TQ, TQ), F32)], axis=1)
    rolled = pltpu.roll(padded, 0, 1, stride=1, stride_axis=0)
    return jnp.sum(rolled, axis=0, keepdims=True)


def _softmax_pair(sc, w, sink=None):
    ps, inv, lses = [], [], []
    for e in range(2):
        sh = sc[:, e * w:(e + 1) * w]
        m = jnp.max(sh, axis=-1, keepdims=True)
        if sink is not None:
            m = jnp.maximum(m, sink[e])
        ex = jnp.exp(sh - m)
        l = jnp.sum(ex, axis=-1, keepdims=True)
        if sink is not None:
            l = l + jnp.exp(sink[e] - m)
        ps.append(ex.astype(BF16))
        inv.append(1.0 / l)
        lses.append(m + jnp.log(l))
    return jnp.concatenate(ps, axis=-1), inv, lses


def _softmax_pair_bwd(sc, dp, lse, delta, w):
    ps, dss = [], []
    for e in range(2):
        p = jnp.exp(sc[:, e * w:(e + 1) * w] - lse[e])
        ps.append(p)
        dss.append(p * (dp[:, e * w:(e + 1) * w] - delta[e]))
    return jnp.concatenate(ps, axis=-1), jnp.concatenate(dss, axis=-1)


def _pair_rowsums(x, lo):
    zero = jnp.zeros_like(x)
    return (jnp.sum(jnp.where(lo, x, zero), axis=-1, keepdims=True),
            jnp.sum(jnp.where(lo, zero, x), axis=-1, keepdims=True))


def _a_qkv_specs(rows, pad, pw):
    return [pl.BlockSpec((None, TQ, pw), lambda p, b: (0, b + pad // TQ, p)),
            pl.BlockSpec((None, rows, pw), lambda p, b: (1, 0, p)),
            pl.BlockSpec((None, rows, pw), lambda p, b: (2, 0, p))]


def _window(ref, b, pad, win, lanes):
    start = pl.multiple_of(b * TQ + pad - (win - TQ), KB)
    return ref[pl.ds(start, win), lanes]


def _attn_a_fwd(zqkv, g, diag, hosted=None):
    s = g.shape[0]
    pad = zqkv.shape[1] - s
    nb = s // TQ
    left = A_KBLOCKS - 1
    pairs = A_PAIRS_FWD
    pw = pairs * LANES
    wide = A_WIN + TQ

    def body(q_ref, k_ref, v_ref, g_ref, diag_ref, o_ref, u_ref, lse_ref, bias_scr):
        b = pl.program_id(1)

        @pl.when(b == 0)
        def _():
            for hh in range(2 * pairs):
                bias_scr[hh // 2, :, (hh % 2) * A_WIN:(hh % 2 + 1) * A_WIN] = _toeplitz_tile(
                    diag_ref[hh], A_WIN, A_LEFT_CHUNKS)

        def step(first_blocks):
            lo = _lane_lo(TQ)
            for pp in range(pairs):
                ln = slice(pp * LANES, (pp + 1) * LANES)
                kcat = _stack_pair(_window(k_ref, b, pad, A_WIN, ln))
                vcat = _stack_pair(_window(v_ref, b, pad, A_WIN, ln))
                sc = lax.dot_general(q_ref[:, ln] * SCALE, kcat, NT, preferred_element_type=F32) + bias_scr[pp]
                if first_blocks:
                    sc = jnp.where(_block_valid(b, left, A_WIN), sc, MASKED)
                p, inv, lses = _softmax_pair(sc, A_WIN)
                ov = jnp.dot(p, vcat, preferred_element_type=F32) * jnp.where(lo, inv[0], inv[1])
                o_ref[:, ln] = ov
                lse_ref[pp] = jnp.where(lo, lses[0], lses[1])
                sg, _ = _silu_parts(g_ref[:, ln])
                u_ref[:, ln] = (ov * sg).astype(BF16)

        @pl.when(b < left)
        def _():
            step(True)

        @pl.when(b >= left)
        def _():
            step(False)

    tile = pl.BlockSpec((TQ, pw), lambda p, b: (b, p))
    return _call(
        body, name="attn_a_fwd", grid=(HEADS // 2 // pairs, nb),
        in_specs=_a_qkv_specs(pad + s, pad, pw) + [
            tile, pl.BlockSpec((2 * pairs, 1, wide), lambda p, b: (p, 0, 0))],
        out_specs=[tile, tile, pl.BlockSpec((pairs, TQ, LANES), lambda p, b: (p, b, 0))],
        out_shape=[jax.ShapeDtypeStruct((s, D_MODEL), F32), jax.ShapeDtypeStruct((s, D_MODEL), BF16),
                   jax.ShapeDtypeStruct((HEADS // 2, s, LANES), F32)],
        scratch_shapes=[pltpu.VMEM((pairs, TQ, 2 * A_WIN), F32)],
        sem=("parallel", "arbitrary"), hosted=hosted,
        args=(zqkv, zqkv, zqkv, g, diag))


def _attn_a_bwd(zqkv, g, o, du, lse, diag, hosted=None):
    s = g.shape[0]
    pad = zqkv.shape[1] - s
    nb = s // TQ
    left = A_KBLOCKS - 1
    pw = A_PAIRS * LANES
    wide = A_WIN + TQ

    def body(q_ref, k_ref, v_ref, g_ref, o_ref, du_ref, lse_ref, diag_ref, dz_ref, ddiag_ref,
             bias_scr, dbias_acc, dk_acc, dv_acc):
        b = pl.program_id(1)

        @pl.when(b == 0)
        def _():
            for hh in range(2 * A_PAIRS):
                bias_scr[hh // 2, :, (hh % 2) * A_WIN:(hh % 2 + 1) * A_WIN] = _toeplitz_tile(
                    diag_ref[hh], A_WIN, A_LEFT_CHUNKS)
            dbias_acc[...] = jnp.zeros_like(dbias_acc)
            dk_acc[...] = jnp.zeros_like(dk_acc)
            dv_acc[...] = jnp.zeros_like(dv_acc)

        def step(first_blocks):
            lo = _lane_lo(TQ)
            rows = pl.ds(pl.multiple_of(b * TQ, TQ), TQ)
            sg, dsg = _silu_parts(g_ref[...])
            duv = du_ref[...]
            ov = o_ref[...]
            do = duv * sg
            dz_ref[3, rows, :] = (duv * ov * dsg).astype(BF16)
            do_o = do * ov
            do_bf = do.astype(BF16)
            for pp in range(A_PAIRS):
                ln = slice(pp * LANES, (pp + 1) * LANES)
                q = q_ref[:, ln] * SCALE
                kcat = _stack_pair(_window(k_ref, b, pad, A_WIN, ln))
                vcat = _stack_pair(_window(v_ref, b, pad, A_WIN, ln))
                sc = lax.dot_general(q, kcat, NT, preferred_element_type=F32) + bias_scr[pp]
                if first_blocks:
                    sc = jnp.where(_block_valid(b, left, A_WIN), sc, MASKED)
                lse_t = lse_ref[pp]
                dp = lax.dot_general(do_bf[:, ln], vcat, NT, preferred_element_type=F32)
                p, ds = _softmax_pair_bwd(sc, dp, (lse_t[:, 0:1], lse_t[:, HEAD_DIM:HEAD_DIM + 1]),
                                          _pair_rowsums(do_o[:, ln], lo), A_WIN)
                dbias_acc[pp] += ds
                dsb = ds.astype(BF16)
                dz_ref[0, rows, ln] = (jnp.dot(dsb, kcat, preferred_element_type=F32) * SCALE).astype(BF16)
                pb = p.astype(BF16)
                dob = do_bf[:, ln]
                dkt = jnp.concatenate([
                    lax.dot_general(q[:, e * HEAD_DIM:(e + 1) * HEAD_DIM], dsb[:, e * A_WIN:(e + 1) * A_WIN], TN,
                                    preferred_element_type=F32) for e in range(2)], axis=0)
                dvt = jnp.concatenate([
                    lax.dot_general(dob[:, e * HEAD_DIM:(e + 1) * HEAD_DIM], pb[:, e * A_WIN:(e + 1) * A_WIN], TN,
                                    preferred_element_type=F32) for e in range(2)], axis=0)
                for t in range(A_KBLOCKS):
                    blk = b + (pad // KB - left + t)
                    dk_acc[blk, ln, :] += dkt[:, t * KB:(t + 1) * KB]
                    dv_acc[blk, ln, :] += dvt[:, t * KB:(t + 1) * KB]

        @pl.when(b < left)
        def _():
            step(True)

        @pl.when(b >= left)
        def _():
            step(False)

        @pl.when(b == nb - 1)
        def _():
            for kb in range(s // KB):
                dz_ref[1, kb * KB:(kb + 1) * KB, :] = dk_acc[pad // KB + kb].T.astype(BF16)
                dz_ref[2, kb * KB:(kb + 1) * KB, :] = dv_acc[pad // KB + kb].T.astype(BF16)
            for hh in range(2 * A_PAIRS):
                ddiag_ref[hh] = _toeplitz_sum(
                    dbias_acc[hh // 2, :, (hh % 2) * A_WIN:(hh % 2 + 1) * A_WIN], A_WIN)

    tile = pl.BlockSpec((TQ, pw), lambda p, b: (b, p))
    diag_spec = pl.BlockSpec((2 * A_PAIRS, 1, wide), lambda p, b: (p, 0, 0))
    return _call(
        body, name="attn_a_bwd", grid=(HEADS // 2 // A_PAIRS, nb),
        in_specs=_a_qkv_specs(pad + s, pad, pw) + [
            tile, tile, tile, pl.BlockSpec((A_PAIRS, TQ, LANES), lambda p, b: (p, b, 0)), diag_spec],
        out_specs=[pl.BlockSpec((4, s, pw), lambda p, b: (0, 0, p)), diag_spec],
        out_shape=[jax.ShapeDtypeStruct((4, s, D_MODEL), BF16),
                   jax.ShapeDtypeStruct((HEADS, 1, wide), F32)],
        scratch_shapes=[pltpu.VMEM((A_PAIRS, TQ, 2 * A_WIN), F32), pltpu.VMEM((A_PAIRS, TQ, 2 * A_WIN), F32),
                        pltpu.VMEM(((pad + s) // KB, pw, KB), F32), pltpu.VMEM(((pad + s) // KB, pw, KB), F32)],
        sem=("parallel", "arbitrary"), hosted=hosted,
        args=(zqkv, zqkv, zqkv, g, o, du, lse, diag))


B_STACK = B_GROUP // 2
B_KVX = 4 * LANES
B_ROWS = B_STACK * TQ
B_WIDE = B_WIN + TQ


def _b_head_place(h):
    return h // B_GROUP, (h % B_GROUP) // 2, h % 2


def _toeplitz_tile_t(base_row, width, left_chunks):
    wide = width + TQ
    rolled = pltpu.roll(jnp.broadcast_to(base_row, (width, wide)), 0, 1, stride=1, stride_axis=0)
    j = lax.broadcasted_iota(jnp.int32, (width, TQ), 0) // CHUNK
    i = lax.broadcasted_iota(jnp.int32, (width, TQ), 1) // CHUNK
    dc = i + left_chunks - j
    return jnp.where((dc >= 0) & (dc <= left_chunks), rolled[:, :TQ], MASKED)


def _toeplitz_sum_t(tile_t, width):
    flip = (lax.broadcasted_iota(jnp.int32, (width, width), 0) + lax.broadcasted_iota(jnp.int32, (width, width), 1)
            == width - 1).astype(F32)
    reversed_rows = jnp.dot(flip, tile_t, precision=lax.Precision.HIGHEST, preferred_element_type=F32)
    padded = jnp.concatenate([reversed_rows, jnp.zeros((width, width), F32)], axis=1)
    rolled = pltpu.roll(padded, 0, 1, stride=1, stride_axis=0)
    return jnp.sum(rolled, axis=0, keepdims=True)


def _b_build_bias(base_ref, bias_scr):
    for h in range(HEADS):
        gi, pr, e = _b_head_place(h)
        bias_scr[gi, e * B_WIN:(e + 1) * B_WIN, pr * TQ:(pr + 1) * TQ] = _toeplitz_tile_t(
            base_ref[h], B_WIN, B_LEFT_CHUNKS)


def _b_stack(x, gi):
    return jnp.concatenate(
        [x[:, (B_STACK * gi + pr) * LANES:(B_STACK * gi + pr + 1) * LANES] for pr in range(B_STACK)], axis=0)


def _b_sink_rows(sink_ref, gi):
    block = lax.broadcasted_iota(jnp.int32, (1, B_ROWS), 1) // TQ
    rows = []
    for e in range(2):
        row = jnp.zeros((1, B_ROWS), F32)
        for pr in range(B_STACK):
            h = B_GROUP * gi + 2 * pr + e
            row = jnp.where(block == pr, sink_ref[0:1, h:h + 1], row)
        rows.append(row)
    return rows


def _b_scores_t(q_ref, kvv, bias_scr, gi, b, left, first_blocks):
    kcat = _stack_pair(kvv[:, gi * LANES:(gi + 1) * LANES])
    vcat = _stack_pair(kvv[:, (B_KV_HEADS + gi) * LANES:(B_KV_HEADS + gi + 1) * LANES])
    qs = _b_stack(q_ref, gi) * SCALE
    sc = lax.dot_general(kcat, qs, NT, preferred_element_type=F32) + bias_scr[gi]
    if first_blocks:
        row = lax.broadcasted_iota(jnp.int32, (2 * B_WIN, 1), 0)
        row = jnp.where(row >= B_WIN, row - B_WIN, row)
        sc = jnp.where((row // KB + (b - left)) >= 0, sc, MASKED)
    return kcat, vcat, qs, sc


def _attn_b_fwd(qb, kvx, gate, base, sinks):
    s = qb.shape[0]
    pad = kvx.shape[0] - s
    nb = s // TQ
    left = B_KBLOCKS - 1

    def body(q_ref, kv_ref, g_ref, base_ref, sink_ref, o_ref, u_ref, lse_ref, bias_scr):
        b = pl.program_id(0)

        @pl.when(b == 0)
        def _():
            _b_build_bias(base_ref, bias_scr)

        def step(first_blocks):
            kvv = _window(kv_ref, b, pad, B_WIN, slice(None))
            upper = lax.broadcasted_iota(jnp.int32, (LANES, B_ROWS), 0) < HEAD_DIM
            lse_rows = []
            for gi in range(B_KV_HEADS):
                kcat, vcat, qs, sc = _b_scores_t(q_ref, kvv, bias_scr, gi, b, left, first_blocks)
                sink = _b_sink_rows(sink_ref, gi)
                ps, inv = [], []
                for e in range(2):
                    sh = sc[e * B_WIN:(e + 1) * B_WIN]
                    m = jnp.maximum(jnp.max(sh, axis=0, keepdims=True), sink[e])
                    ex = jnp.exp(sh - m)
                    l = jnp.sum(ex, axis=0, keepdims=True) + jnp.exp(sink[e] - m)
                    ps.append(ex.astype(BF16))
                    inv.append(1.0 / l)
                    lse_rows.append(m + jnp.log(l))
                pt = jnp.concatenate(ps, axis=0)
                ot = lax.dot_general(vcat, pt, TN, preferred_element_type=F32) * jnp.where(upper, inv[0], inv[1])
                ov = ot.T
                for pr in range(B_STACK):
                    pair = B_STACK * gi + pr
                    o_ref[:, pair * LANES:(pair + 1) * LANES] = ov[pr * TQ:(pr + 1) * TQ]
            lse_ref[0] = jnp.concatenate(lse_rows + [jnp.zeros((8 - len(lse_rows), B_ROWS), F32)], axis=0)
            sg, _ = _silu_parts(g_ref[...])
            u_ref[...] = (o_ref[...] * sg).astype(BF16)

        @pl.when(b < left)
        def _():
            step(True)

        @pl.when(b >= left)
        def _():
            step(False)

    row = pl.BlockSpec((TQ, D_MODEL), lambda b: (b, 0))
    return pl.pallas_call(
        body, name="attn_b_fwd", grid=(nb,),
        in_specs=[row, pl.BlockSpec((pad + s, B_KVX), lambda b: (0, 0)), row,
                  pl.BlockSpec((HEADS, 1, B_WIDE), lambda b: (0, 0, 0)), pl.BlockSpec((1, HEADS), lambda b: (0, 0))],
        out_specs=[row, row, pl.BlockSpec((1, 8, B_ROWS), lambda b: (b, 0, 0))],
        out_shape=[jax.ShapeDtypeStruct((s, D_MODEL), F32), jax.ShapeDtypeStruct((s, D_MODEL), BF16),
                   jax.ShapeDtypeStruct((nb, 8, B_ROWS), F32)],
        scratch_shapes=[pltpu.VMEM((B_KV_HEADS, 2 * B_WIN, B_ROWS), F32)],
        compiler_params=_params(("arbitrary",)),
    )(qb, kvx, gate, base, sinks)


def _attn_b_bwd(qb, kvx, gate, o, du, lse, base, sinks):
    s = qb.shape[0]
    pad = kvx.shape[0] - s
    nb = s // TQ
    left = B_KBLOCKS - 1
    half = D_MODEL // 2

    def body(q_ref, kv_ref, g_ref, o_ref, du_ref, lse_ref, base_ref, sink_ref, dz_ref, dkv_ref, dsum_ref,
             dsink_ref, bias_scr, dbias_acc, dkv_acc, dsink_acc):
        b = pl.program_id(0)

        @pl.when(b == 0)
        def _():
            _b_build_bias(base_ref, bias_scr)
            dbias_acc[...] = jnp.zeros_like(dbias_acc)
            dkv_acc[...] = jnp.zeros_like(dkv_acc)
            dsink_acc[...] = jnp.zeros_like(dsink_acc)

        def step(first_blocks):
            kvv = _window(kv_ref, b, pad, B_WIN, slice(None))
            sg, dsg = _silu_parts(g_ref[...])
            duv = du_ref[...]
            ov = o_ref[...]
            do = duv * sg
            dgate = (duv * ov * dsg).astype(BF16)
            dz_ref[2] = dgate[:, :half]
            dz_ref[3] = dgate[:, half:]
            do_o = do * ov
            do_bf = do.astype(BF16)
            lse_all = lse_ref[0]
            dsink_rows = []
            for gi in range(B_KV_HEADS):
                kcat, vcat, qs, sc = _b_scores_t(q_ref, kvv, bias_scr, gi, b, left, first_blocks)
                dos = _b_stack(do_bf, gi)
                doo_t = _b_stack(do_o, gi).T
                delta = (jnp.sum(doo_t[:HEAD_DIM], axis=0, keepdims=True),
                         jnp.sum(doo_t[HEAD_DIM:], axis=0, keepdims=True))
                sink = _b_sink_rows(sink_ref, gi)
                dp = lax.dot_general(vcat, dos, NT, preferred_element_type=F32)
                ps, dss = [], []
                for e in range(2):
                    lse_e = lse_all[2 * gi + e:2 * gi + e + 1]
                    delta_e = delta[e]
                    p = jnp.exp(sc[e * B_WIN:(e + 1) * B_WIN] - lse_e)
                    ps.append(p.astype(BF16))
                    dss.append(p * (dp[e * B_WIN:(e + 1) * B_WIN] - delta_e))
                    dsink_rows.append(-jnp.exp(sink[e] - lse_e) * delta_e)
                ds = jnp.concatenate(dss, axis=0)
                dbias_acc[gi] += ds
                dsb = ds.astype(BF16)
                dq = (lax.dot_general(kcat, dsb, TN, preferred_element_type=F32) * SCALE).T.astype(BF16)
                for pr in range(B_STACK):
                    dz_ref[gi, :, pr * LANES:(pr + 1) * LANES] = dq[pr * TQ:(pr + 1) * TQ]
                dk = _unstack_pair(jnp.dot(dsb, qs, preferred_element_type=F32), B_WIN)
                dv = _unstack_pair(jnp.dot(jnp.concatenate(ps, axis=0), dos, preferred_element_type=F32), B_WIN)
                krows = pl.ds(pl.multiple_of(b * TQ + pad - (B_WIN - TQ), KB), B_WIN)
                dkv_acc[krows, gi * LANES:(gi + 1) * LANES] += dk
                dkv_acc[krows, (B_KV_HEADS + gi) * LANES:(B_KV_HEADS + gi + 1) * LANES] += dv
            dsink_acc[...] += jnp.concatenate(
                dsink_rows + [jnp.zeros((8 - len(dsink_rows), B_ROWS), F32)], axis=0)

        @pl.when(b < left)
        def _():
            step(True)

        @pl.when(b >= left)
        def _():
            step(False)

        @pl.when(b == nb - 1)
        def _():
            lo_s = _lane_lo(s)
            for which in range(2):
                folded = []
                for gi in range(B_KV_HEADS):
                    part = dkv_acc[pad:pad + s, (which * B_KV_HEADS + gi) * LANES:(which * B_KV_HEADS + gi + 1) * LANES]
                    folded.append(part + pltpu.roll(part, HEAD_DIM, 1))
                dkv_ref[:, which * LANES:(which + 1) * LANES] = jnp.where(lo_s, folded[0], folded[1]).astype(BF16)
            lane8 = lax.broadcasted_iota(jnp.int32, dsink_ref.shape, 1)
            tot = jnp.zeros(dsink_ref.shape, F32)
            for h in range(HEADS):
                gi, pr, e = _b_head_place(h)
                dsum_ref[h] = _toeplitz_sum_t(
                    dbias_acc[gi, e * B_WIN:(e + 1) * B_WIN, pr * TQ:(pr + 1) * TQ], B_WIN)
                per_query = dsink_acc[2 * gi + e:2 * gi + e + 1, pr * TQ:(pr + 1) * TQ]
                tot = jnp.where(lane8 == h, jnp.sum(per_query, axis=1, keepdims=True), tot)
            dsink_ref[...] = tot

    row = pl.BlockSpec((TQ, D_MODEL), lambda b: (b, 0))
    base_spec = pl.BlockSpec((HEADS, 1, B_WIDE), lambda b: (0, 0, 0))
    return pl.pallas_call(
        body, name="attn_b_bwd", grid=(nb,),
        in_specs=[row, pl.BlockSpec((pad + s, B_KVX), lambda b: (0, 0)), row, row, row,
                  pl.BlockSpec((1, 8, B_ROWS), lambda b: (b, 0, 0)), base_spec,
                  pl.BlockSpec((1, HEADS), lambda b: (0, 0))],
        out_specs=[pl.BlockSpec((4, TQ, half), lambda b: (0, b, 0)),
                   pl.BlockSpec((s, 2 * LANES), lambda b: (0, 0)), base_spec,
                   pl.BlockSpec((8, LANES), lambda b: (0, 0))],
        out_shape=[jax.ShapeDtypeStruct((4, s, half), BF16), jax.ShapeDtypeStruct((s, 2 * LANES), BF16),
                   jax.ShapeDtypeStruct((HEADS, 1, B_WIDE), F32), jax.ShapeDtypeStruct((8, LANES), F32)],
        scratch_shapes=[pltpu.VMEM((B_KV_HEADS, 2 * B_WIN, B_ROWS), F32),
                        pltpu.VMEM((B_KV_HEADS, 2 * B_WIN, B_ROWS), F32),
                        pltpu.VMEM((pad + s, B_KVX), F32), pltpu.VMEM((8, B_ROWS), F32)],
        compiler_params=_params(("arbitrary",)),
    )(qb, kvx, gate, o, du, lse, base, sinks)


def _t5_bucket(rel):
    nb = T5_BUCKETS // 2
    max_exact = nb // 2
    ret = jnp.where(rel > 0, nb, 0)
    n = jnp.abs(rel)
    nf = jnp.maximum(n, 1).astype(jnp.float32)
    large = max_exact + (jnp.log(nf / max_exact) / math.log(T5_MAX_DIST / max_exact)
                         * (nb - max_exact)).astype(jnp.int32)
    large = jnp.minimum(large, nb - 1)
    return ret + jnp.where(n < max_exact, n, large)


def _a_offset_onehot():
    c = np.arange(A_WIN + TQ)
    dist = A_LEFT_CHUNKS * CHUNK + TQ - 1 - c
    idx = np.clip(dist, -A_REL_CLIP, A_REL_CLIP) + A_REL_CLIP
    onehot = np.zeros((A_WIN + TQ, 2 * A_REL_CLIP + 1), np.float32)
    onehot[c, idx] = 1.0
    return jnp.asarray(onehot)


def _b_offset_onehot():
    c = jnp.arange(B_WIN + TQ, dtype=jnp.int32)
    rel = c - (TQ - 1) - B_LEFT_CHUNKS * CHUNK
    return (_t5_bucket(rel)[:, None] == jnp.arange(T5_BUCKETS)[None, :]).astype(F32)


def _diag_rows(onehot, table):
    rows = jnp.dot(onehot, table.astype(F32), precision=lax.Precision.HIGHEST)
    return rows.T.reshape(HEADS, 1, onehot.shape[0])


def _diag_rows_grad(onehot, ddiag):
    return jnp.dot(ddiag.reshape(HEADS, onehot.shape[0]), onehot, precision=lax.Precision.HIGHEST)


def _position():
    x, y, c = lax.axis_index("x"), lax.axis_index("y"), lax.axis_index("c")
    chips = [(1 - x, y), (x, 1 - y), (1 - x, 1 - y)]
    return x, y, c, chips


ANY = pl.BlockSpec(memory_space=pl.ANY)


def _allgather_routed(shards):
    n = len(shards)

    def piece(block_ref, t, c, quarter=None):
        half = shards[t].shape[0] // 2
        if quarter is None:
            return block_ref.at[pl.ds(c * half, half)]
        return block_ref.at[pl.ds(c * half + quarter * (half // 2), half // 2)]

    def copies(kind, ins, outs, sems):
        ici_send, ici_recv, pass_send, pass_recv, local_sems = sems
        x, y, c, chips = _position()
        mine = 2 * x + y
        if kind == "local":
            return [pltpu.make_async_copy(ins[t], outs[t].at[mine], local_sems.at[t]) for t in range(n)]
        ids = [2 * chip[0] + chip[1] for chip in chips]
        made = []
        for t in range(n):
            def ici(k, to):
                return dict(send_sem=ici_send.at[4 * t + k], recv_sem=ici_recv.at[4 * t + k],
                            device_id=(chips[to][0], chips[to][1], c), device_id_type=MESH)

            def d2d(k):
                return dict(send_sem=pass_send.at[4 * t + k], recv_sem=pass_recv.at[4 * t + k],
                            device_id=(x, y, 1 - c), device_id_type=MESH)

            def same(ref, where):
                return pltpu.make_async_remote_copy(src_ref=ref, dst_ref=ref, **where)

            if kind == "send":
                for k in range(2):
                    made.append(pltpu.make_async_remote_copy(
                        src_ref=piece(ins[t], t, c), dst_ref=piece(outs[t].at[mine], t, c), **ici(k, k)))
            elif kind == "landed":
                made += [same(piece(outs[t].at[ids[k]], t, c), ici(k, k)) for k in range(2)]
            elif kind == "forward":
                made.append(same(piece(outs[t].at[ids[0]], t, c, 0), ici(2, 1)))
                made.append(same(piece(outs[t].at[ids[1]], t, c, 1), ici(3, 0)))
            elif kind == "arrived":
                made.append(same(piece(outs[t].at[ids[2]], t, c, 0), ici(2, 1)))
                made.append(same(piece(outs[t].at[ids[2]], t, c, 1), ici(3, 0)))
            else:
                core = 1 - c if kind == "passed" else c
                if kind in ("pass halves", "passed"):
                    made += [same(piece(outs[t].at[ids[k]], t, core), d2d(k)) for k in range(2)]
                if kind in ("pass quarters", "passed"):
                    made += [same(piece(outs[t].at[ids[2]], t, core, k), d2d(2 + k)) for k in range(2)]
        return made

    def first(ins, outs, sems):
        for cp in copies("local", ins, outs, sems) + copies("send", ins, outs, sems):
            cp.start()

    def middle(ins, outs, sems):
        for got, onward, near in zip(copies("landed", ins, outs, sems), copies("forward", ins, outs, sems),
                                     copies("pass halves", ins, outs, sems)):
            got.wait_recv()
            near.start()
            onward.start()

    def last(ins, outs, sems):
        quarters = copies("pass quarters", ins, outs, sems)
        for got, near in zip(copies("arrived", ins, outs, sems), quarters):
            got.wait_recv()
            near.start()
        for cp in copies("passed", ins, outs, sems):
            cp.wait_recv()
        for cp in (copies("send", ins, outs, sems) + copies("forward", ins, outs, sems)
                   + copies("pass halves", ins, outs, sems) + quarters):
            cp.wait_send()
        for cp in copies("local", ins, outs, sems):
            cp.wait()

    return _Hosted(shards, [jax.ShapeDtypeStruct((4,) + w.shape, w.dtype) for w in shards],
                   [pltpu.SemaphoreType.DMA((4 * n,))] * 4 + [pltpu.SemaphoreType.DMA((n,))],
                   first, middle, last)


def _scatter_hosted(grads):
    n = len(grads)

    def copies(ins, outs, sems):
        send_sems, recv_sems = sems
        x, y, c, chips = _position()
        return [pltpu.make_async_remote_copy(
            src_ref=ins[t].at[2 * chip[0] + chip[1]], dst_ref=outs[t].at[j],
            send_sem=send_sems.at[3 * t + j], recv_sem=recv_sems.at[3 * t + j],
            device_id=(chip[0], chip[1], c), device_id_type=MESH)
            for t in range(n) for j, chip in enumerate(chips)]

    def first(ins, outs, sems):
        for cp in copies(ins, outs, sems):
            cp.start()

    def last(ins, outs, sems):
        for cp in copies(ins, outs, sems):
            cp.wait()

    return _Hosted(grads, [jax.ShapeDtypeStruct((3,) + g.shape[1:], g.dtype) for g in grads],
                   [pltpu.SemaphoreType.DMA((3 * n,))] * 2, first, None, last)


GATHER_PEERS = "x and y neighbours (same core) and the sibling core"
SCATTER_PEERS = "the same core of the three other chips"
EVERYONE = "the seven other devices"


def _run_on_sequencer(name, hosted, peers, collective_id):
    ins = [jax.new_ref(a, memory_space=pltpu.MemorySpace.HBM) for a in hosted.inputs]
    outs = [jax.empty_ref(shape, memory_space=pltpu.MemorySpace.HBM) for shape in hosted.out_shapes]

    @pl.kernel(mesh=plsc.ScalarSubcoreMesh(axis_name="sequencer", num_cores=1), name=name,
               scratch_types=tuple(hosted.sems), compiler_params=pltpu.CompilerParams(collective_id=collective_id))
    def launch(*sems):
        x, y, c, chips = _position()
        if peers == GATHER_PEERS:
            devices = [(chip[0], chip[1], c) for chip in chips[:2]] + [(x, y, 1 - c)]
        elif peers == SCATTER_PEERS:
            devices = [(chip[0], chip[1], c) for chip in chips]
        else:
            devices = [(x ^ (k >> 2), y ^ ((k >> 1) & 1), c ^ (k & 1)) for k in range(1, 8)]
        barrier = pltpu.get_barrier_semaphore()
        for device in devices:
            pl.semaphore_signal(barrier, inc=1, device_id=device, device_id_type=MESH)
        pl.semaphore_wait(barrier, len(devices))
        hosted.first(ins, outs, sems)
        if hosted.middle is not None:
            hosted.middle(ins, outs, sems)
        hosted.last(ins, outs, sems)

    launch()
    return [o[...] for o in outs]


def _gather_gain(shard):
    def body(in_ref, out_ref, send_sems, recv_sems):
        x, y, c, chips = _position()
        out_ref[2 * x + y] = in_ref[...]
        sends = [pltpu.make_async_remote_copy(
            src_ref=in_ref, dst_ref=out_ref.at[2 * x + y], send_sem=send_sems.at[j], recv_sem=recv_sems.at[j],
            device_id=(chip[0], chip[1], c), device_id_type=MESH) for j, chip in enumerate(chips)]
        for cp in sends:
            cp.start()
        for j, chip in enumerate(chips):
            pltpu.make_async_remote_copy(
                src_ref=in_ref, dst_ref=out_ref.at[2 * chip[0] + chip[1]], send_sem=send_sems.at[j],
                recv_sem=recv_sems.at[j], device_id=(chip[0], chip[1], c), device_id_type=MESH).wait_recv()
        for cp in sends:
            cp.wait_send()

    vmem = pl.BlockSpec(memory_space=pltpu.VMEM)
    return pl.pallas_call(
        body, name="gather_gain", in_specs=[vmem], out_specs=vmem,
        out_shape=jax.ShapeDtypeStruct((4,) + shard.shape, shard.dtype),
        scratch_shapes=[pltpu.SemaphoreType.DMA((3,))] * 2,
    )(shard)


def _swap_with_sibling(name, blocks):
    n = len(blocks)

    def body(*refs):
        ins, outs = refs[:n], refs[n:2 * n]
        send_sems, recv_sems = refs[2 * n:]
        x, y, c, _ = _position()
        sends = [pltpu.make_async_remote_copy(
            src_ref=ins[t], dst_ref=outs[t], send_sem=send_sems.at[t], recv_sem=recv_sems.at[t],
            device_id=(x, y, 1 - c), device_id_type=MESH) for t in range(n)]
        for cp in sends:
            cp.start()
        for cp in sends:
            cp.wait()

    return pl.pallas_call(
        body, name=name,
        in_specs=[ANY] * n, out_specs=[ANY] * n,
        out_shape=[jax.ShapeDtypeStruct(b.shape, b.dtype) for b in blocks],
        scratch_shapes=[pltpu.SemaphoreType.DMA((n,))] * 2,
    )(*blocks)


def _everyone_hosted(terms):
    nt = len(terms)

    def copies(kind, ins, outs, sems):
        send_sems, recv_sems, local_sems = sems
        x, y, c, _ = _position()
        me = 4 * x + 2 * y + c
        if kind == "local":
            return [pltpu.make_async_copy(ins[t], outs[t].at[me], local_sems.at[t]) for t in range(nt)]
        made = []
        for t in range(nt):
            for k in range(1, 8):
                peer = (x ^ (k >> 2), y ^ ((k >> 1) & 1), c ^ (k & 1))
                slot = me if kind == "send" else me ^ k
                made.append(pltpu.make_async_remote_copy(
                    src_ref=ins[t], dst_ref=outs[t].at[slot], send_sem=send_sems.at[7 * t + k - 1],
                    recv_sem=recv_sems.at[7 * t + k - 1], device_id=peer, device_id_type=MESH))
        return made

    def first(ins, outs, sems):
        for cp in copies("local", ins, outs, sems) + copies("send", ins, outs, sems):
            cp.start()

    def last(ins, outs, sems):
        for cp in copies("landed", ins, outs, sems):
            cp.wait_recv()
        for cp in copies("send", ins, outs, sems):
            cp.wait_send()
        for cp in copies("local", ins, outs, sems):
            cp.wait()

    return _Hosted(terms, [jax.ShapeDtypeStruct((8,) + a.shape, F32) for a in terms],
                   [pltpu.SemaphoreType.DMA((7 * nt,))] * 2 + [pltpu.SemaphoreType.DMA((nt,))], first, None, last)


def _small_step(partials, extras, ws, ms, vs, shard_of):
    n = len(partials)
    terms = list(partials) + list(extras)
    nt = len(terms)
    rows = [t for t in range(nt) if terms[t].shape[0] == 1]
    mats = [t for t in range(nt) if terms[t].shape[0] != 1]
    row_block = (8, max(terms[t].shape[1] for t in rows))
    assert len(rows) <= row_block[0]
    vmem = pl.BlockSpec(memory_space=pltpu.VMEM)

    def pack(*refs):
        packed = refs[-1]
        packed[...] = jnp.zeros_like(packed)
        for i, t in enumerate(rows):
            packed[i:i + 1, 0:terms[t].shape[1]] = refs[i][...]

    packed = pl.pallas_call(pack, name="small_pack", in_specs=[vmem] * len(rows), out_specs=vmem,
                            out_shape=jax.ShapeDtypeStruct(row_block, F32))(*[terms[t] for t in rows])
    slots = _run_on_sequencer("allgather_small", _everyone_hosted([packed] + [terms[t] for t in mats]),
                              EVERYONE, 2)

    def body(*refs):
        slot_refs, refs = refs[:len(slots)], refs[len(slots):]
        w_refs, refs = refs[:n], refs[n:]
        m_refs, refs = refs[:n], refs[n:]
        v_refs, outs = refs[:n], refs[n:]
        sums = []
        for ref in slot_refs:
            g = ref[0]
            for dev in range(1, 8):
                g = g + ref[dev]
            sums.append(g)
        chip = 2 * lax.axis_index("x") + lax.axis_index("y")
        for t in range(nt):
            if t in rows:
                i = rows.index(t)
                g = sums[0][i:i + 1, 0:terms[t].shape[1]]
            else:
                g = sums[1 + mats.index(t)]
            if t >= n:
                outs[4 * n + t - n][...] = g
                continue
            if shard_of[t]:
                width = ws[t].shape[-1]
                mine = jnp.zeros(ws[t].shape, F32)
                for s in range(4):
                    mine = jnp.where(chip == s, g[:, s * width:(s + 1) * width], mine)
                g = mine
            delta, mn, vn = _adamw_math(w_refs[t][...], g, m_refs[t][...], v_refs[t][...])
            outs[4 * t][...] = g
            outs[4 * t + 1][...] = delta
            outs[4 * t + 2][...] = mn
            outs[4 * t + 3][...] = vn

    out_shapes = []
    for t in range(n):
        out_shapes += [jax.ShapeDtypeStruct(ws[t].shape, F32)] * 4
    out_shapes += [jax.ShapeDtypeStruct(a.shape, F32) for a in extras]
    res = pl.pallas_call(
        body, name="small_step",
        in_specs=[vmem] * (len(slots) + 3 * n), out_specs=[vmem] * len(out_shapes), out_shape=out_shapes,
    )(*slots, *ws, *ms, *vs)
    return [res[4 * t:4 * t + 4] for t in range(n)], res[4 * n:4 * n + nt - n]


def _adamw_math(w, g, m, v):
    m = ADAM_B1 * m + (1.0 - ADAM_B1) * g
    v = ADAM_B2 * v + (1.0 - ADAM_B2) * (g * g)
    m_hat = m / (1.0 - ADAM_B1 ** ADAM_STEP)
    v_hat = v / (1.0 - ADAM_B2 ** ADAM_STEP)
    delta = -ADAM_LR * (m_hat / (jnp.sqrt(v_hat) + ADAM_EPS) + ADAM_WD * w)
    return delta, m, v


def _row_tile(rows):
    return 256 if rows % 256 == 0 else rows


def _sum_partials(name, own, recv, chip, after):
    rows, cols = own.shape[1:]
    tr = _row_tile(rows)

    def body(chip_ref, own_ref, recv_ref, after_ref, o_ref):
        acc = own_ref[...]
        for j in range(3):
            acc = acc + recv_ref[j].astype(F32)
        o_ref[...] = acc

    return pl.pallas_call(
        body, name=name,
        grid_spec=pltpu.PrefetchScalarGridSpec(
            num_scalar_prefetch=1, grid=(rows // tr,),
            in_specs=[pl.BlockSpec((None, tr, cols), lambda i, chip_ref: (chip_ref[0], i, 0)),
                      pl.BlockSpec((3, tr, cols), lambda i, chip_ref: (0, i, 0)), ANY],
            out_specs=pl.BlockSpec((tr, cols), lambda i, chip_ref: (i, 0))),
        out_shape=jax.ShapeDtypeStruct((rows, cols), F32),
        compiler_params=_params(("parallel",)),
    )(chip.reshape(1).astype(jnp.int32), own, recv, after)


def _adamw(name, w, m, v, g_parts):
    rows, cols = w.shape
    tr = _row_tile(rows)
    n = len(g_parts)

    def body(w_ref, m_ref, v_ref, *refs):
        g_refs = refs[:n]
        go_ref, d_ref, mo_ref, vo_ref = refs[n:]
        g = g_refs[0][...]
        for r in g_refs[1:]:
            g = g + r[...]
        delta, mn, vn = _adamw_math(w_ref[...], g, m_ref[...], v_ref[...])
        go_ref[...] = g
        d_ref[...] = delta
        mo_ref[...] = mn
        vo_ref[...] = vn

    spec = pl.BlockSpec((tr, cols), lambda i: (i, 0))
    return pl.pallas_call(
        body, name=name, grid=(rows // tr,),
        in_specs=[spec] * (3 + n), out_specs=[spec] * 4,
        out_shape=[jax.ShapeDtypeStruct((rows, cols), F32)] * 4,
        compiler_params=_params(("parallel",)),
    )(w, m, v, *g_parts)


def _local_step(x, target, ga, wa_in, rel_bias, later_shards, gk, t5, gb, sinks, gf):
    s, d = x.shape
    tm = min(TM_DENSE, s)
    nt = s // tm
    half = d // 2
    row = pl.BlockSpec((tm, d), lambda i: (i, 0))
    whole = lambda shape: pl.BlockSpec(shape, lambda *_: (0,) * len(shape))

    n1, = _norm_fwd("norm_a", x, ga)
    projected = None
    for h, (wa_half, tag) in enumerate(zip(wa_in, ("first", "second"))):
        projected = _proj_a_half("proj_a_" + tag, n1, wa_half, h, projected)
    zqkv, gate_a = projected
    onehot_a = _a_offset_onehot()
    diag_a = _diag_rows(onehot_a, rel_bias)
    (o_a, u_a, lse_a), gathered = _attn_a_fwd(zqkv, gate_a, diag_a, hosted=_allgather_routed(later_shards))
    wa_out, wkv, wb_in, wb_out, wkv_x = gathered
    wa_out = wa_out.reshape(d, d)
    wkv = wkv.reshape(d, -1)
    wkv_x = wkv_x.reshape(d, B_KVX)
    wb_out = wb_out.reshape(d, d)
    h1, nk, n2 = _out_norms("out_a_norms", u_a, wa_out, x, jnp.concatenate([gk, gb], axis=0))
    kvw = wkv.shape[1]
    kvx =_matmul("proj_kv", nk, wkv_x, dims=NN, grid=(nt + 1,), zero_axis=0,
                  a_spec=pl.BlockSpec((tm, d), lambda i: (jnp.maximum(i - 1, 0), 0)), b_spec=whole((d, B_KVX)),
                  o_spec=pl.BlockSpec((tm, B_KVX), lambda i: (i, 0)), out_shape=(tm + s, B_KVX), out_dtype=BF16)
    qb = _matmul("proj_b_q", n2, wb_in, dims=NN, grid=(2, nt),
                 a_spec=pl.BlockSpec((tm, d), lambda j, i: (i, 0)),
                 b_spec=pl.BlockSpec((None, d, half), lambda j, i: (j, 0, 0)),
                 o_spec=pl.BlockSpec((tm, half), lambda j, i: (i, j)), out_shape=(s, d), out_dtype=BF16)
    gate_b = _matmul("proj_b_gate", n2, wb_in, dims=NN, grid=(2, nt),
                     a_spec=pl.BlockSpec((tm, d), lambda j, i: (i, 0)),
                     b_spec=pl.BlockSpec((None, d, half), lambda j, i: (2 + j, 0, 0)),
                     o_spec=pl.BlockSpec((tm, half), lambda j, i: (i, j)), out_shape=(s, d), out_dtype=F32)
    onehot_b = _b_offset_onehot()
    base_b = jnp.roll(_diag_rows(onehot_b, t5)[..., ::-1], TQ, axis=-1)
    o_b, u_b, lse_b = _attn_b_fwd(qb, kvx, gate_b, base_b, sinks)
    dh2, loss, d_gf = _out_loss_head(u_b, wb_out, h1, target, gf)

    du_b = _matmul("dout_b", dh2, wb_out, dims=NT, grid=(nt,), a_spec=row, b_spec=whole((d, d)), o_spec=row,
                   out_shape=(s, d), out_dtype=F32)
    d_wb_out = _matmul("dw_out_b", u_b, dh2, dims=TN, grid=(2,),
                       a_spec=whole((s, d)), b_spec=pl.BlockSpec((s, half), lambda j: (0, j)),
                       o_spec=pl.BlockSpec((d, half), lambda j: (0, j)),
                       out_shape=(d, d), out_dtype=F32, also_bf16=True)
    dz_b, dkv, dsum_b, dsinks = _attn_b_bwd(qb, kvx, gate_b, o_b, du_b, lse_b, base_b, sinks)
    ddiag_b = jnp.roll(dsum_b[..., ::-1], -1, axis=-1)
    d_wb_in = _matmul("dw_in_b", n2, dz_b, dims=TN, grid=(4,),
                      a_spec=whole((s, d)), b_spec=pl.BlockSpec((None, s, half), lambda j: (j, 0, 0)),
                      o_spec=pl.BlockSpec((None, d, half), lambda j: (j, 0, 0)),
                      out_shape=(4, d, half), out_dtype=F32, also_bf16=True)
    d_wkv = _matmul("dw_kv", nk, dkv, dims=TN, grid=(1,),
                    a_spec=whole((s, d)), b_spec=whole((s, kvw)), o_spec=whole((d, kvw)),
                    out_shape=(d, kvw), out_dtype=F32, also_bf16=True)
    dh1, d_gkb = _proj_norm_bwd("dproj_kv_b", h1, dh2, jnp.concatenate([gk, gb], axis=0),
                                [(dkv[None], [wkv[None]]), (dz_b, [wb_in])])

    du_a = _matmul("dout_a", dh1, wa_out, dims=NT, grid=(nt,), a_spec=row, b_spec=whole((d, d)), o_spec=row,
                   out_shape=(s, d), out_dtype=F32)
    d_wa_out = _matmul("dw_out_a", u_a, dh1, dims=TN, grid=(2,),
                       a_spec=whole((s, d)), b_spec=pl.BlockSpec((s, half), lambda j: (0, j)),
                       o_spec=pl.BlockSpec((d, half), lambda j: (0, j)),
                       out_shape=(d, d), out_dtype=F32, also_bf16=True)
    early = dict(a_w_out=[g.reshape(4, d // 4, d) for g in d_wa_out],
                 kv_w=[g.reshape(4, d // 4, kvw) for g in d_wkv], b_w_in=list(d_wb_in),
                 b_w_out=[g.reshape(4, d // 4, d) for g in d_wb_out])
    (dz_a, ddiag_a), early_recv = _attn_a_bwd(
        zqkv, gate_a, o_a, du_a, lse_a, diag_a, hosted=_scatter_hosted([early[n][1] for n in early]))
    d_wa_in = _matmul("dw_in_a", n1, dz_a, dims=TN, grid=(4, 2),
                      a_spec=whole((s, d)), b_spec=pl.BlockSpec((None, s, half), lambda j, h: (j, 0, h)),
                      o_spec=pl.BlockSpec((None, d, half), lambda j, h: (j, 0, h)),
                      out_shape=(4, d, d), out_dtype=F32, also_bf16=True)
    late_recv = _run_on_sequencer("scatter_a_w_in", _scatter_hosted([d_wa_in[1]]), SCATTER_PEERS, 0)
    grad_x, d_ga = _proj_norm_bwd("dproj_a", x, dh1, ga, [(dz_a, list(wa_in))])

    small = dict(a_norm=d_ga, kv_norm=d_gkb[0:1], b_norm=d_gkb[1:2], b_sinks=dsinks[0:1, :HEADS], final_norm=d_gf)
    small["by_offset"] = dict(a_rel_bias=(onehot_a, ddiag_a.reshape(HEADS, -1)),
                              t5_bias=(onehot_b, ddiag_b.reshape(HEADS, -1)))
    own = dict(a_w_in=d_wa_in[0], **{n: early[n][0] for n in early})
    received = dict(a_w_in=late_recv[0], **dict(zip(early, early_recv)))
    return loss, grad_x, small, own, received, d_wa_in[1]


SMALL = ("a_norm", "kv_norm", "b_norm", "b_sinks", "final_norm")
TABLES = ("a_rel_bias", "t5_bias")
BIG = ("a_w_in", "a_w_out", "kv_w", "b_w_in", "b_w_out")
ORDER = ("a_norm", "a_w_in", "a_rel_bias", "a_w_out", "kv_norm", "kv_w", "t5_bias", "b_norm", "b_w_in",
         "b_sinks", "b_w_out", "final_norm")


def kernel(x, a_norm, a_w_in, a_rel_bias, a_w_out, kv_norm, kv_w, t5_bias, b_norm, b_w_in, b_sinks, b_w_out, final_norm, loss_target, m_a_norm, m_a_w_in, m_a_rel_bias, m_a_w_out, m_kv_norm, m_kv_w, m_t5_bias, m_b_norm, m_b_w_in, m_b_sinks, m_b_w_out, m_final_norm, v_a_norm, v_a_w_in, v_a_rel_bias, v_a_w_out, v_kv_norm, v_kv_w, v_t5_bias, v_b_norm, v_b_w_in, v_b_sinks, v_b_w_out, v_final_norm):
    w = dict(a_norm=a_norm, a_w_in=a_w_in, a_rel_bias=a_rel_bias, a_w_out=a_w_out, kv_norm=kv_norm, kv_w=kv_w,
             t5_bias=t5_bias, b_norm=b_norm, b_w_in=b_w_in, b_sinks=b_sinks, b_w_out=b_w_out,
             final_norm=final_norm)
    m = dict(a_norm=m_a_norm, a_w_in=m_a_w_in, a_rel_bias=m_a_rel_bias, a_w_out=m_a_w_out, kv_norm=m_kv_norm,
             kv_w=m_kv_w, t5_bias=m_t5_bias, b_norm=m_b_norm, b_w_in=m_b_w_in, b_sinks=m_b_sinks,
             b_w_out=m_b_w_out, final_norm=m_final_norm)
    v = dict(a_norm=v_a_norm, a_w_in=v_a_w_in, a_rel_bias=v_a_rel_bias, a_w_out=v_a_w_out, kv_norm=v_kv_norm,
             kv_w=v_kv_w, t5_bias=v_t5_bias, b_norm=v_b_norm, b_w_in=v_b_w_in, b_sinks=v_b_sinks,
             b_w_out=v_b_w_out, final_norm=v_final_norm)
    d = D_MODEL
    chip = 2 * lax.axis_index("x") + lax.axis_index("y")

    shard2d = dict(a_w_in=a_w_in[0], a_w_out=a_w_out[0], kv_w=kv_w, b_w_in=b_w_in[0], b_w_out=b_w_out[0])

    first = shard2d["a_w_in"].astype(BF16)
    wa_in = [_run_on_sequencer("allgather_" + tag, _allgather_routed([first[:, h * (d // 2):(h + 1) * (d // 2)]]),
                               GATHER_PEERS, collective_id)[0]
             for h, (tag, collective_id) in enumerate((("first", 1), ("second", 3)))]
    ga = _gather_gain(a_norm).reshape(1, d)

    later = [shard2d[n].astype(BF16) for n in BIG[1:]]
    kv_shard = later[BIG[1:].index("kv_w")]
    later.append(jnp.concatenate(
        [kv_shard[:, (i // 2) * HEAD_DIM:(i // 2 + 1) * HEAD_DIM] for i in range(B_KVX // HEAD_DIM)], axis=1))
    loss, grad_x, small, own, received, after_attention = _local_step(
        x[0], loss_target[0], ga, wa_in, a_rel_bias[0], later,
        kv_norm.reshape(1, d), t5_bias, b_norm, b_sinks, final_norm.reshape(1, d))

    out = {}
    as2d = lambda a: a.reshape(-1, a.shape[-1])
    small_res, (loss_sum, *offset_sums) = _small_step(
        [small[n] for n in SMALL], [loss] + [small["by_offset"][n][1] for n in TABLES],
        [as2d(w[n]) for n in SMALL], [as2d(m[n]) for n in SMALL], [as2d(v[n]) for n in SMALL],
        [n == "a_norm" for n in SMALL])
    for n, res in zip(SMALL, small_res):
        out[n] = [r.reshape(w[n].shape) for r in res]
    loss_out = loss_sum.reshape(())
    for n, summed in zip(TABLES, offset_sums):
        grad = _diag_rows_grad(small["by_offset"][n][0], summed)
        res = _adamw("adamw_" + n, as2d(w[n]).T, as2d(m[n]).T, as2d(v[n]).T, [grad])
        out[n] = [r.T.reshape(w[n].shape) for r in res]

    core_sums = [_sum_partials("sum_" + n, own[n], received[n], chip, after_attention) for n in BIG]
    sibling_sums = (_swap_with_sibling("swap_last", core_sums[:1])
                    + _swap_with_sibling("swap_early", core_sums[1:]))

    for n, mine, theirs in zip(BIG, core_sums, sibling_sums):
        res = _adamw("adamw_" + n, shard2d[n], m[n].reshape(shard2d[n].shape), v[n].reshape(shard2d[n].shape),
                     [mine, theirs])
        out[n] = [r.reshape(w[n].shape) for r in res]

    grads = [out[n][0] for n in ORDER]
    deltas = [out[n][1] for n in ORDER]
    new_m = [out[n][2] for n in ORDER]
    new_v = [out[n][3] for n in ORDER]
    return (loss_out, grad_x[None], *grads, *deltas, *new_m, *new_v)
```

```python
import math

import jax
import jax.numpy as jnp
import numpy as np
from jax import lax
from jax.experimental import pallas as pl
from jax.experimental.pallas import tpu as pltpu
from jax.experimental.pallas import tpu_sc as plsc

F32 = jnp.float32
BF16 = jnp.bfloat16
MESH = pl.DeviceIdType.MESH

D_MODEL = 1024
HEADS = 16
HEAD_DIM = 64
CHUNK = 64
RMS_EPS = 1e-6
SCALE = HEAD_DIM ** -0.5
A_LEFT_CHUNKS = 8
A_REL_CLIP = 256
B_LEFT_CHUNKS = 2
B_KV_HEADS = 2
B_GROUP = HEADS // B_KV_HEADS
T5_BUCKETS = 32
T5_MAX_DIST = 128
ADAM_LR = 0.001
ADAM_B1 = 0.9
ADAM_B2 = 0.999
ADAM_EPS = 1e-08
ADAM_WD = 0.01
ADAM_STEP = 10

MASKED = -1e30
LANES = 128
TQ = 128
A_PAIRS = 2
A_PAIRS_FWD = 4
KB = 128
A_KBLOCKS = A_LEFT_CHUNKS * CHUNK // KB + 1
B_KBLOCKS = B_LEFT_CHUNKS * CHUNK // KB + 1
A_WIN = A_KBLOCKS * KB
B_WIN = B_KBLOCKS * KB
TM = 512
TM_DENSE = 1024
TM_HALF = 2048
TM_PARTS = 512
VMEM_LIMIT = 56 * 1024 * 1024

NT = (((1,), (1,)), ((), ()))
TN = (((0,), (0,)), ((), ()))
NN = (((1,), (0,)), ((), ()))


def _params(sem=None):
    return pltpu.CompilerParams(dimension_semantics=sem, vmem_limit_bytes=VMEM_LIMIT)


class _Hosted:
    def __init__(self, inputs, out_shapes, sems, first, middle, last):
        self.inputs, self.out_shapes, self.sems = list(inputs), list(out_shapes), list(sems)
        self.first, self.middle, self.last = first, middle, last


def _call(body, *, name, grid, in_specs, out_specs, out_shape, args, scratch_shapes=(), sem=None, hosted=None,
          aliases=None):
    in_specs, out_specs, out_shape = list(in_specs), list(out_specs), list(out_shape)
    scratch_shapes = list(scratch_shapes)
    if hosted is None:
        out = pl.pallas_call(
            body, name=name, grid=grid, in_specs=in_specs, out_specs=out_specs, out_shape=out_shape,
            scratch_shapes=scratch_shapes, input_output_aliases=aliases or {},
            compiler_params=_params(sem))(*args)
        return list(out), []
    assert aliases is None
    n_in, n_out, n_scr = len(in_specs), len(out_shape), len(scratch_shapes)
    h_in, h_out = len(hosted.inputs), len(hosted.out_shapes)
    total = int(np.prod(grid)) if grid else 1

    def wrapped(*refs):
        ins, refs = refs[:n_in], refs[n_in:]
        h_ins, refs = refs[:h_in], refs[h_in:]
        outs, refs = refs[:n_out], refs[n_out:]
        h_outs, refs = refs[:h_out], refs[h_out:]
        scr, h_sems = refs[:n_scr], refs[n_scr:]
        step = 0
        for axis, size in enumerate(grid):
            step = step * size + pl.program_id(axis)

        if hosted.first is not None:
            @pl.when(step == 0)
            def _():
                hosted.first(h_ins, h_outs, h_sems)

        body(*ins, *outs, *scr)
        if hosted.middle is not None:
            @pl.when(step == total // 2)
            def _():
                hosted.middle(h_ins, h_outs, h_sems)

        if hosted.last is not None:
            @pl.when(step == total - 1)
            def _():
                hosted.last(h_ins, h_outs, h_sems)

    out = pl.pallas_call(
        wrapped, name=name, grid=grid, in_specs=in_specs + [ANY] * h_in, out_specs=out_specs + [ANY] * h_out,
        out_shape=out_shape + hosted.out_shapes, scratch_shapes=scratch_shapes + hosted.sems,
        compiler_params=_params(("arbitrary",) * len(grid)))(*args, *hosted.inputs)
    return list(out[:n_out]), list(out[n_out:])


def _matmul(name, a, b, *, dims, grid, a_spec, b_spec, o_spec, out_shape, out_dtype,
            also_bf16=False, zero_axis=None):
    def body(*refs):
        if zero_axis is None:
            product(*refs)
        else:
            @pl.when(pl.program_id(zero_axis) == 0)
            def _():
                refs[2][...] = jnp.zeros_like(refs[2])

            @pl.when(pl.program_id(zero_axis) > 0)
            def _():
                product(*refs)

    def product(a_ref, b_ref, o_ref, *more):
        prod = lax.dot_general(a_ref[...].astype(BF16), b_ref[...].astype(BF16), dims,
                               preferred_element_type=F32)
        o_ref[...] = prod.astype(out_dtype)
        if also_bf16:
            more[0][...] = prod.astype(BF16)

    out_specs = [o_spec]
    out_shapes = [jax.ShapeDtypeStruct(out_shape, out_dtype)]
    if also_bf16:
        out_specs.append(o_spec)
        out_shapes.append(jax.ShapeDtypeStruct(out_shape, BF16))
    out, _ = _call(body, name=name, grid=grid, in_specs=[a_spec, b_spec], out_specs=out_specs,
                   out_shape=out_shapes, args=[a, b], sem=("parallel",) * len(grid))
    return out[0] if not also_bf16 else tuple(out)


def _proj_a_half(name, n1, w, h, into):
    s, d = n1.shape
    half = w.shape[2]
    ta = min(TM_HALF, s)

    def body(a_ref, w_ref, *refs):
        z_ref, g_ref = refs[-2:]
        i, j = pl.program_id(0), pl.program_id(1)

        @pl.when((i == 0) & (j < 3))
        def _():
            z_ref[...] = jnp.zeros_like(z_ref)

        @pl.when((i > 0) & (j < 3))
        def _():
            z_ref[...] = jnp.dot(a_ref[...], w_ref[...], preferred_element_type=F32).astype(BF16)

        @pl.when((i > 0) & (j == 3))
        def _():
            g_ref[...] = jnp.dot(a_ref[...], w_ref[...], preferred_element_type=F32)

    out, _ = _call(
        body, name=name, grid=(s // ta + 1, 4),
        in_specs=[pl.BlockSpec((ta, d), lambda i, j: (jnp.maximum(i - 1, 0), 0)),
                  pl.BlockSpec((None, d, half), lambda i, j: (j, 0, 0))] + ([] if into is None else [ANY, ANY]),
        out_specs=[pl.BlockSpec((None, ta, half), lambda i, j: (jnp.minimum(j, 2), i, h)),
                   pl.BlockSpec((ta, half), lambda i, j: (jnp.maximum(i - 1, 0), h))],
        out_shape=[jax.ShapeDtypeStruct((3, ta + s, d), BF16), jax.ShapeDtypeStruct((s, d), F32)],
        args=[n1, w] + ([] if into is None else list(into)), sem=("arbitrary", "arbitrary"),
        aliases=None if into is None else {2: 0, 3: 1})
    return out


def _proj_b(n2, w):
    s, d = n2.shape
    half = w.shape[2]
    ta = min(TM_HALF, s)

    def body(a_ref, w_ref, q_ref, g_ref):
        j = pl.program_id(1)

        @pl.when(j < 2)
        def _():
            q_ref[...] = jnp.dot(a_ref[...], w_ref[...], preferred_element_type=F32).astype(BF16)

        @pl.when(j >= 2)
        def _():
            g_ref[...] = jnp.dot(a_ref[...], w_ref[...], preferred_element_type=F32)

    out, _ = _call(
        body, name="proj_b", grid=(s // ta, 4),
        in_specs=[pl.BlockSpec((ta, d), lambda i, j: (i, 0)), pl.BlockSpec((None, d, half), lambda i, j: (j, 0, 0))],
        out_specs=[pl.BlockSpec((ta, half), lambda i, j: (i, jnp.minimum(j, 1))),
                   pl.BlockSpec((ta, half), lambda i, j: (i, jnp.maximum(j - 2, 0)))],
        out_shape=[jax.ShapeDtypeStruct((s, d), BF16), jax.ShapeDtypeStruct((s, d), F32)],
        args=[n2, w], sem=("arbitrary", "arbitrary"))
    return out


def _dw_out(name, u, dh):
    s, d = u.shape
    tk = min(TM, s)
    last = s // tk - 1

    def body(u_ref, dh_ref, o_ref, ob_ref):
        k = pl.program_id(0)
        prod = lax.dot_general(u_ref[...], dh_ref[...].astype(BF16), TN, preferred_element_type=F32)

        @pl.when(k == 0)
        def _():
            o_ref[...] = prod

        @pl.when(k > 0)
        def _():
            o_ref[...] += prod

        @pl.when(k == last)
        def _():
            ob_ref[...] = o_ref[...].astype(BF16)

    tile = pl.BlockSpec((tk, d), lambda k: (k, 0))
    whole = pl.BlockSpec((d, d), lambda k: (0, 0))
    out, _ = _call(body, name=name, grid=(s // tk,), in_specs=[tile, tile], out_specs=[whole, whole],
                   out_shape=[jax.ShapeDtypeStruct((d, d), F32), jax.ShapeDtypeStruct((d, d), BF16)],
                   args=[u, dh], sem=("arbitrary",))
    return tuple(out)


def _rms_rows(x):
    return lax.rsqrt(jnp.mean(x * x, axis=-1, keepdims=True) + RMS_EPS)


def _norm_fwd(name, x, gains):
    s, d = x.shape
    n = gains.shape[0]

    def body(x_ref, g_ref, *o_refs):
        xv = x_ref[...]
        xh = xv * _rms_rows(xv)
        for i in range(n):
            o_refs[i][...] = (xh * g_ref[i:i + 1, :]).astype(BF16)

    row = pl.BlockSpec((TM, d), lambda i: (i, 0))
    return pl.pallas_call(
        body, name=name, grid=(s // TM,),
        in_specs=[row, pl.BlockSpec((n, d), lambda i: (0, 0))],
        out_specs=[row] * n,
        out_shape=[jax.ShapeDtypeStruct((s, d), BF16)] * n,
        compiler_params=_params(("parallel",)),
    )(x, gains)


def _proj_norm_bwd(name, x, dres, gains, branches):
    s, d = x.shape
    n = len(branches)
    n_ab = 2 * sum(len(bs) for _, bs in branches)
    tm = min(TM_PARTS, s)

    def body(x_ref, r_ref, g_ref, *refs):
        ab_refs, dx_ref, dg_ref = list(refs[:n_ab]), refs[n_ab], refs[n_ab + 1]
        i = pl.program_id(0)
        xv = x_ref[...]
        r = _rms_rows(xv)
        xh = xv * r

        @pl.when(i == 0)
        def _():
            dg_ref[...] = jnp.zeros_like(dg_ref)

        a = None
        for j in range(n):
            dn = None
            for _ in branches[j][1]:
                a_ref, b_ref = ab_refs.pop(0), ab_refs.pop(0)
                for part in range(a_ref.shape[0]):
                    term = lax.dot_general(a_ref[part], b_ref[part], NT, preferred_element_type=F32)
                    dn = term if dn is None else dn + term
            t = dn * g_ref[j:j + 1, :]
            a = t if a is None else a + t
            dg_ref[j:j + 1, :] += jnp.sum(dn * xh, axis=0, keepdims=True)
        dx_ref[...] = r_ref[...] + r * (a - xh * jnp.mean(xh * a, axis=-1, keepdims=True))

    row = pl.BlockSpec((tm, d), lambda i: (i, 0))
    small = pl.BlockSpec((n, d), lambda i: (0, 0))
    ab_specs, ab_args = [], []
    for a, bs in branches:
        for k, b in enumerate(bs):
            ab_specs += [pl.BlockSpec((a.shape[0], tm, b.shape[2]), lambda i, k=k: (0, i, k)),
                         pl.BlockSpec(b.shape, lambda i: (0, 0, 0))]
            ab_args += [a, b]
    return pl.pallas_call(
        body, name=name, grid=(s // tm,),
        in_specs=[row, row, small] + ab_specs,
        out_specs=[row, small],
        out_shape=[jax.ShapeDtypeStruct((s, d), F32), jax.ShapeDtypeStruct((n, d), F32)],
        compiler_params=_params(("arbitrary",)),
    )(x, dres, gains, *ab_args)


def _out_norms(name, u, w_out, resid, gains):
    s, d = resid.shape
    n = gains.shape[0]
    tm = min(TM_DENSE, s)

    def body(u_ref, w_ref, r_ref, g_ref, h_ref, *o_refs):
        hv = r_ref[...] + jnp.dot(u_ref[...], w_ref[...], preferred_element_type=F32)
        h_ref[...] = hv
        hh = hv * _rms_rows(hv)
        for i in range(n):
            o_refs[i][...] = (hh * g_ref[i:i + 1, :]).astype(BF16)

    row = pl.BlockSpec((tm, d), lambda i: (i, 0))
    return pl.pallas_call(
        body, name=name, grid=(s // tm,),
        in_specs=[row, pl.BlockSpec((d, d), lambda i: (0, 0)), row, pl.BlockSpec((n, d), lambda i: (0, 0))],
        out_specs=[row] * (n + 1),
        out_shape=[jax.ShapeDtypeStruct((s, d), F32)] + [jax.ShapeDtypeStruct((s, d), BF16)] * n,
        compiler_params=_params(("parallel",)),
    )(u, w_out, resid, gains)


def _out_loss_head(u, w_out, resid, target, gain):
    s, d = resid.shape
    tm = min(TM_PARTS, s)

    def body(u_ref, w_ref, r_ref, t_ref, g_ref, dh_ref, loss_ref, dg_ref):
        i = pl.program_id(0)
        hv = r_ref[...] + jnp.dot(u_ref[...], w_ref[...], preferred_element_type=F32)
        r = _rms_rows(hv)
        hh = hv * r
        g = g_ref[...]
        err = hh * g - t_ref[...]
        part = 0.5 * jnp.sum(jnp.sum(err * err, axis=-1, keepdims=True) * (1.0 / d), axis=0, keepdims=True)
        dy = err * (1.0 / d)
        a = dy * g
        dh_ref[...] = r * (a - hh * jnp.mean(hh * a, axis=-1, keepdims=True))
        dg = jnp.sum(dy * hh, axis=0, keepdims=True)

        @pl.when(i == 0)
        def _():
            loss_ref[...] = part
            dg_ref[...] = dg

        @pl.when(i > 0)
        def _():
            loss_ref[...] += part
            dg_ref[...] += dg

    row = pl.BlockSpec((tm, d), lambda i: (i, 0))
    return pl.pallas_call(
        body, name="out_b_loss_head", grid=(s // tm,),
        in_specs=[row, pl.BlockSpec((d, d), lambda i: (0, 0)), row, row, pl.BlockSpec((1, d), lambda i: (0, 0))],
        out_specs=[row, pl.BlockSpec((1, 1), lambda i: (0, 0)), pl.BlockSpec((1, d), lambda i: (0, 0))],
        out_shape=[jax.ShapeDtypeStruct((s, d), F32), jax.ShapeDtypeStruct((1, 1), F32),
                   jax.ShapeDtypeStruct((1, d), F32)],
        compiler_params=_params(("arbitrary",)),
    )(u, w_out, resid, target, gain)


def _silu_parts(g):
    sig = jax.nn.sigmoid(g)
    return g * sig, sig * (1.0 + g * (1.0 - sig))


def _lane_lo(rows):
    return lax.broadcasted_iota(jnp.int32, (rows, LANES), 1) < HEAD_DIM


def _stack_pair(x):
    lo = _lane_lo(x.shape[0])
    zero = jnp.zeros_like(x)
    return jnp.concatenate([jnp.where(lo, x, zero), jnp.where(lo, zero, x)], axis=0)


def _unstack_pair(y, w):
    return jnp.where(_lane_lo(w), y[:w], y[w:])


def _block_valid(b, left_blocks, width):
    col = lax.broadcasted_iota(jnp.int32, (1, 2 * width), 1)
    col = jnp.where(col >= width, col - width, col)
    return (col // KB + (b - left_blocks)) >= 0


def _toeplitz_tile(diag_row, width, left_chunks):
    wide = width + TQ
    rolled = pltpu.roll(jnp.broadcast_to(diag_row, (TQ, wide)), 1, 1, stride=1, stride_axis=0)
    i = lax.broadcasted_iota(jnp.int32, (TQ, width), 0) // CHUNK
    j = lax.broadcasted_iota(jnp.int32, (TQ, width), 1) // CHUNK
    dc = i + left_chunks - j
    return jnp.where((dc >= 0) & (dc <= left_chunks), rolled[:, TQ:], MASKED)


def _toeplitz_sum(tile, width):
    flip = (lax.broadcasted_iota(jnp.int32, (TQ, TQ), 0) + lax.broadcasted_iota(jnp.int32, (TQ, TQ), 1)
            == TQ - 1).astype(F32)
    reversed_rows = jnp.dot(flip, tile, precision=lax.Precision.HIGHEST, preferred_element_type=F32)
    padded = jnp.concatenate([reversed_rows, jnp.zeros((TQ, TQ), F32)], axis=1)
    rolled = pltpu.roll(padded, 0, 1, stride=1, stride_axis=0)
    return jnp.sum(rolled, axis=0, keepdims=True)


def _softmax_pair(sc, w, sink=None):
    ps, inv, lses = [], [], []
    for e in range(2):
        sh = sc[:, e * w:(e + 1) * w]
        m = jnp.max(sh, axis=-1, keepdims=True)
        if sink is not None:
            m = jnp.maximum(m, sink[e])
        ex = jnp.exp(sh - m)
        l = jnp.sum(ex, axis=-1, keepdims=True)
        if sink is not None:
            l = l + jnp.exp(sink[e] - m)
        ps.append(ex.astype(BF16))
        inv.append(1.0 / l)
        lses.append(m + jnp.log(l))
    return jnp.concatenate(ps, axis=-1), inv, lses


def _softmax_pair_bwd(sc, dp, lse, delta, w):
    ps, dss = [], []
    for e in range(2):
        p = jnp.exp(sc[:, e * w:(e + 1) * w] - lse[e])
        ps.append(p)
        dss.append(p * (dp[:, e * w:(e + 1) * w] - delta[e]))
    return jnp.concatenate(ps, axis=-1), jnp.concatenate(dss, axis=-1)


def _pair_rowsums(x, lo):
    zero = jnp.zeros_like(x)
    return (jnp.sum(jnp.where(lo, x, zero), axis=-1, keepdims=True),
            jnp.sum(jnp.where(lo, zero, x), axis=-1, keepdims=True))


def _a_qkv_specs(rows, pad, pw):
    return [pl.BlockSpec((None, TQ, pw), lambda p, b: (0, b + pad // TQ, p)),
            pl.BlockSpec((None, rows, pw), lambda p, b: (1, 0, p)),
            pl.BlockSpec((None, rows, pw), lambda p, b: (2, 0, p))]


def _window(ref, b, pad, win, lanes):
    start = pl.multiple_of(b * TQ + pad - (win - TQ), KB)
    return ref[pl.ds(start, win), lanes]


def _attn_a_fwd(zqkv, g, diag, hosted=None):
    s = g.shape[0]
    pad = zqkv.shape[1] - s
    nb = s // TQ
    left = A_KBLOCKS - 1
    pairs = A_PAIRS_FWD
    pw = pairs * LANES
    wide = A_WIN + TQ

    def body(q_ref, k_ref, v_ref, g_ref, diag_ref, o_ref, u_ref, lse_ref, bias_scr):
        b = pl.program_id(1)

        @pl.when(b == 0)
        def _():
            for hh in range(2 * pairs):
                bias_scr[hh // 2, :, (hh % 2) * A_WIN:(hh % 2 + 1) * A_WIN] = _toeplitz_tile(
                    diag_ref[hh], A_WIN, A_LEFT_CHUNKS)

        def step(first_blocks):
            lo = _lane_lo(TQ)
            for pp in range(pairs):
                ln = slice(pp * LANES, (pp + 1) * LANES)
                kcat = _stack_pair(_window(k_ref, b, pad, A_WIN, ln))
                vcat = _stack_pair(_window(v_ref, b, pad, A_WIN, ln))
                sc = lax.dot_general(q_ref[:, ln] * SCALE, kcat, NT, preferred_element_type=F32) + bias_scr[pp]
                if first_blocks:
                    sc = jnp.where(_block_valid(b, left, A_WIN), sc, MASKED)
                p, inv, lses = _softmax_pair(sc, A_WIN)
                ov = jnp.dot(p, vcat, preferred_element_type=F32) * jnp.where(lo, inv[0], inv[1])
                o_ref[:, ln] = ov
                lse_ref[pp] = jnp.where(lo, lses[0], lses[1])
                sg, _ = _silu_parts(g_ref[:, ln])
                u_ref[:, ln] = (ov * sg).astype(BF16)

        @pl.when(b < left)
        def _():
            step(True)

        @pl.when(b >= left)
        def _():
            step(False)

    tile = pl.BlockSpec((TQ, pw), lambda p, b: (b, p))
    return _call(
        body, name="attn_a_fwd", grid=(HEADS // 2 // pairs, nb),
        in_specs=_a_qkv_specs(pad + s, pad, pw) + [
            tile, pl.BlockSpec((2 * pairs, 1, wide), lambda p, b: (p, 0, 0))],
        out_specs=[tile, tile, pl.BlockSpec((pairs, TQ, LANES), lambda p, b: (p, b, 0))],
        out_shape=[jax.ShapeDtypeStruct((s, D_MODEL), F32), jax.ShapeDtypeStruct((s, D_MODEL), BF16),
                   jax.ShapeDtypeStruct((HEADS // 2, s, LANES), F32)],
        scratch_shapes=[pltpu.VMEM((pairs, TQ, 2 * A_WIN), F32)],
        sem=("parallel", "arbitrary"), hosted=hosted,
        args=(zqkv, zqkv, zqkv, g, diag))


def _attn_a_bwd(zqkv, g, o, du, lse, diag, hosted=None):
    s = g.shape[0]
    pad = zqkv.shape[1] - s
    nb = s // TQ
    left = A_KBLOCKS - 1
    pw = A_PAIRS * LANES
    wide = A_WIN + TQ

    def body(q_ref, k_ref, v_ref, g_ref, o_ref, du_ref, lse_ref, diag_ref, dz_ref, ddiag_ref,
             bias_scr, dbias_acc, dk_acc, dv_acc):
        b = pl.program_id(1)

        @pl.when(b == 0)
        def _():
            for hh in range(2 * A_PAIRS):
                bias_scr[hh // 2, :, (hh % 2) * A_WIN:(hh % 2 + 1) * A_WIN] = _toeplitz_tile(
                    diag_ref[hh], A_WIN, A_LEFT_CHUNKS)
            dbias_acc[...] = jnp.zeros_like(dbias_acc)
            dk_acc[...] = jnp.zeros_like(dk_acc)
            dv_acc[...] = jnp.zeros_like(dv_acc)

        def step(first_blocks):
            lo = _lane_lo(TQ)
            rows = pl.ds(pl.multiple_of(b * TQ, TQ), TQ)
            sg, dsg = _silu_parts(g_ref[...])
            duv = du_ref[...]
            ov = o_ref[...]
            do = duv * sg
            dz_ref[3, rows, :] = (duv * ov * dsg).astype(BF16)
            do_o = do * ov
            do_bf = do.astype(BF16)
            for pp in range(A_PAIRS):
                ln = slice(pp * LANES, (pp + 1) * LANES)
                q = q_ref[:, ln] * SCALE
                kcat = _stack_pair(_window(k_ref, b, pad, A_WIN, ln))
                vcat = _stack_pair(_window(v_ref, b, pad, A_WIN, ln))
                sc = lax.dot_general(q, kcat, NT, preferred_element_type=F32) + bias_scr[pp]
                if first_blocks:
                    sc = jnp.where(_block_valid(b, left, A_WIN), sc, MASKED)
                lse_t = lse_ref[pp]
                dp = lax.dot_general(do_bf[:, ln], vcat, NT, preferred_element_type=F32)
                p, ds = _softmax_pair_bwd(sc, dp, (lse_t[:, 0:1], lse_t[:, HEAD_DIM:HEAD_DIM + 1]),
                                          _pair_rowsums(do_o[:, ln], lo), A_WIN)
                dbias_acc[pp] += ds
                dsb = ds.astype(BF16)
                dz_ref[0, rows, ln] = (jnp.dot(dsb, kcat, preferred_element_type=F32) * SCALE).astype(BF16)
                pb = p.astype(BF16)
                dob = do_bf[:, ln]
                dkt = jnp.concatenate([
                    lax.dot_general(q[:, e * HEAD_DIM:(e + 1) * HEAD_DIM], dsb[:, e * A_WIN:(e + 1) * A_WIN], TN,
                                    preferred_element_type=F32) for e in range(2)], axis=0)
                dvt = jnp.concatenate([
                    lax.dot_general(dob[:, e * HEAD_DIM:(e + 1) * HEAD_DIM], pb[:, e * A_WIN:(e + 1) * A_WIN], TN,
                                    preferred_element_type=F32) for e in range(2)], axis=0)
                for t in range(A_KBLOCKS):
                    blk = b + (pad // KB - left + t)
                    dk_acc[blk, ln, :] += dkt[:, t * KB:(t + 1) * KB]
                    dv_acc[blk, ln, :] += dvt[:, t * KB:(t + 1) * KB]

        @pl.when(b < left)
        def _():
            step(True)

        @pl.when(b >= left)
        def _():
            step(False)

        @pl.when(b == nb - 1)
        def _():
            for kb in range(s // KB):
                dz_ref[1, kb * KB:(kb + 1) * KB, :] = dk_acc[pad // KB + kb].T.astype(BF16)
                dz_ref[2, kb * KB:(kb + 1) * KB, :] = dv_acc[pad // KB + kb].T.astype(BF16)
            for hh in range(2 * A_PAIRS):
                ddiag_ref[hh] = _toeplitz_sum(
                    dbias_acc[hh // 2, :, (hh % 2) * A_WIN:(hh % 2 + 1) * A_WIN], A_WIN)

    tile = pl.BlockSpec((TQ, pw), lambda p, b: (b, p))
    diag_spec = pl.BlockSpec((2 * A_PAIRS, 1, wide), lambda p, b: (p, 0, 0))
    return _call(
        body, name="attn_a_bwd", grid=(HEADS // 2 // A_PAIRS, nb),
        in_specs=_a_qkv_specs(pad + s, pad, pw) + [
            tile, tile, tile, pl.BlockSpec((A_PAIRS, TQ, LANES), lambda p, b: (p, b, 0)), diag_spec],
        out_specs=[pl.BlockSpec((4, s, pw), lambda p, b: (0, 0, p)), diag_spec],
        out_shape=[jax.ShapeDtypeStruct((4, s, D_MODEL), BF16),
                   jax.ShapeDtypeStruct((HEADS, 1, wide), F32)],
        scratch_shapes=[pltpu.VMEM((A_PAIRS, TQ, 2 * A_WIN), F32), pltpu.VMEM((A_PAIRS, TQ, 2 * A_WIN), F32),
                        pltpu.VMEM(((pad + s) // KB, pw, KB), F32), pltpu.VMEM(((pad + s) // KB, pw, KB), F32)],
        sem=("parallel", "arbitrary"), hosted=hosted,
        args=(zqkv, zqkv, zqkv, g, o, du, lse, diag))


B_STACK = B_GROUP // 2
B_KVX = 4 * LANES
B_ROWS = B_STACK * TQ
B_WIDE = B_WIN + TQ


def _b_head_place(h):
    return h // B_GROUP, (h % B_GROUP) // 2, h % 2


def _toeplitz_tile_t(base_row, width, left_chunks):
    wide = width + TQ
    rolled = pltpu.roll(jnp.broadcast_to(base_row, (width, wide)), 0, 1, stride=1, stride_axis=0)
    j = lax.broadcasted_iota(jnp.int32, (width, TQ), 0) // CHUNK
    i = lax.broadcasted_iota(jnp.int32, (width, TQ), 1) // CHUNK
    dc = i + left_chunks - j
    return jnp.where((dc >= 0) & (dc <= left_chunks), rolled[:, :TQ], MASKED)


def _toeplitz_sum_t(tile_t, width):
    flip = (lax.broadcasted_iota(jnp.int32, (width, width), 0) + lax.broadcasted_iota(jnp.int32, (width, width), 1)
            == width - 1).astype(F32)
    reversed_rows = jnp.dot(flip, tile_t, precision=lax.Precision.HIGHEST, preferred_element_type=F32)
    padded = jnp.concatenate([reversed_rows, jnp.zeros((width, width), F32)], axis=1)
    rolled = pltpu.roll(padded, 0, 1, stride=1, stride_axis=0)
    return jnp.sum(rolled, axis=0, keepdims=True)


def _b_build_bias(base_ref, bias_scr):
    for h in range(HEADS):
        gi, pr, e = _b_head_place(h)
        bias_scr[gi, e * B_WIN:(e + 1) * B_WIN, pr * TQ:(pr + 1) * TQ] = _toeplitz_tile_t(
            base_ref[h], B_WIN, B_LEFT_CHUNKS)


def _b_stack(x, gi):
    return jnp.concatenate(
        [x[:, (B_STACK * gi + pr) * LANES:(B_STACK * gi + pr + 1) * LANES] for pr in range(B_STACK)], axis=0)


def _b_sink_rows(sink_ref, gi):
    block = lax.broadcasted_iota(jnp.int32, (1, B_ROWS), 1) // TQ
    rows = []
    for e in range(2):
        row = jnp.zeros((1, B_ROWS), F32)
        for pr in range(B_STACK):
            h = B_GROUP * gi + 2 * pr + e
            row = jnp.where(block == pr, sink_ref[0:1, h:h + 1], row)
        rows.append(row)
    return rows


def _b_scores_t(q_ref, kvv, bias_scr, gi, b, left, first_blocks):
    kcat = _stack_pair(kvv[:, gi * LANES:(gi + 1) * LANES])
    vcat = _stack_pair(kvv[:, (B_KV_HEADS + gi) * LANES:(B_KV_HEADS + gi + 1) * LANES])
    qs = _b_stack(q_ref, gi) * SCALE
    sc = lax.dot_general(kcat, qs, NT, preferred_element_type=F32) + bias_scr[gi]
    if first_blocks:
        row = lax.broadcasted_iota(jnp.int32, (2 * B_WIN, 1), 0)
        row = jnp.where(row >= B_WIN, row - B_WIN, row)
        sc = jnp.where((row // KB + (b - left)) >= 0, sc, MASKED)
    return kcat, vcat, qs, sc


def _attn_b_fwd(qb, kvx, gate, base, sinks):
    s = qb.shape[0]
    pad = kvx.shape[0] - s
    nb = s // TQ
    left = B_KBLOCKS - 1

    def body(q_ref, kv_ref, g_ref, base_ref, sink_ref, o_ref, u_ref, lse_ref, bias_scr):
        b = pl.program_id(0)

        @pl.when(b == 0)
        def _():
            _b_build_bias(base_ref, bias_scr)

        def step(first_blocks):
            kvv = _window(kv_ref, b, pad, B_WIN, slice(None))
            upper = lax.broadcasted_iota(jnp.int32, (LANES, B_ROWS), 0) < HEAD_DIM
            lse_rows = []
            for gi in range(B_KV_HEADS):
                kcat, vcat, qs, sc = _b_scores_t(q_ref, kvv, bias_scr, gi, b, left, first_blocks)
                sink = _b_sink_rows(sink_ref, gi)
                ps, inv = [], []
                for e in range(2):
                    sh = sc[e * B_WIN:(e + 1) * B_WIN]
                    m = jnp.maximum(jnp.max(sh, axis=0, keepdims=True), sink[e])
                    ex = jnp.exp(sh - m)
                    l = jnp.sum(ex, axis=0, keepdims=True) + jnp.exp(sink[e] - m)
                    ps.append(ex.astype(BF16))
                    inv.append(1.0 / l)
                    lse_rows.append(m + jnp.log(l))
                pt = jnp.concatenate(ps, axis=0)
                ot = lax.dot_general(vcat, pt, TN, preferred_element_type=F32) * jnp.where(upper, inv[0], inv[1])
                ov = ot.T
                for pr in range(B_STACK):
                    pair = B_STACK * gi + pr
                    o_ref[:, pair * LANES:(pair + 1) * LANES] = ov[pr * TQ:(pr + 1) * TQ]
            lse_ref[0] = jnp.concatenate(lse_rows + [jnp.zeros((8 - len(lse_rows), B_ROWS), F32)], axis=0)
            sg, _ = _silu_parts(g_ref[...])
            u_ref[...] = (o_ref[...] * sg).astype(BF16)

        @pl.when(b < left)
        def _():
            step(True)

        @pl.when(b >= left)
        def _():
            step(False)

    row = pl.BlockSpec((TQ, D_MODEL), lambda b: (b, 0))
    return pl.pallas_call(
        body, name="attn_b_fwd", grid=(nb,),
        in_specs=[row, pl.BlockSpec((pad + s, B_KVX), lambda b: (0, 0)), row,
                  pl.BlockSpec((HEADS, 1, B_WIDE), lambda b: (0, 0, 0)), pl.BlockSpec((1, HEADS), lambda b: (0, 0))],
        out_specs=[row, row, pl.BlockSpec((1, 8, B_ROWS), lambda b: (b, 0, 0))],
        out_shape=[jax.ShapeDtypeStruct((s, D_MODEL), F32), jax.ShapeDtypeStruct((s, D_MODEL), BF16),
                   jax.ShapeDtypeStruct((nb, 8, B_ROWS), F32)],
        scratch_shapes=[pltpu.VMEM((B_KV_HEADS, 2 * B_WIN, B_ROWS), F32)],
        compiler_params=_params(("arbitrary",)),
    )(qb, kvx, gate, base, sinks)


def _attn_b_bwd(qb, kvx, gate, o, du, lse, base, sinks):
    s = qb.shape[0]
    pad = kvx.shape[0] - s
    nb = s // TQ
    left = B_KBLOCKS - 1
    half = D_MODEL // 2

    def body(q_ref, kv_ref, g_ref, o_ref, du_ref, lse_ref, base_ref, sink_ref, dz_ref, dkv_ref, dsum_ref,
             dsink_ref, bias_scr, dbias_acc, dkv_acc, dsink_acc):
        b = pl.program_id(0)

        @pl.when(b == 0)
        def _():
            _b_build_bias(base_ref, bias_scr)
            dbias_acc[...] = jnp.zeros_like(dbias_acc)
            dkv_acc[...] = jnp.zeros_like(dkv_acc)
            dsink_acc[...] = jnp.zeros_like(dsink_acc)

        def step(first_blocks):
            kvv = _window(kv_ref, b, pad, B_WIN, slice(None))
            sg, dsg = _silu_parts(g_ref[...])
            duv = du_ref[...]
            ov = o_ref[...]
            do = duv * sg
            dgate = (duv * ov * dsg).astype(BF16)
            dz_ref[2] = dgate[:, :half]
            dz_ref[3] = dgate[:, half:]
            do_o = do * ov
            do_bf = do.astype(BF16)
            lse_all = lse_ref[0]
            dsink_rows = []
            for gi in range(B_KV_HEADS):
                kcat, vcat, qs, sc = _b_scores_t(q_ref, kvv, bias_scr, gi, b, left, first_blocks)
                dos = _b_stack(do_bf, gi)
                doo_t = _b_stack(do_o, gi).T
                delta = (jnp.sum(doo_t[:HEAD_DIM], axis=0, keepdims=True),
                         jnp.sum(doo_t[HEAD_DIM:], axis=0, keepdims=True))
                sink = _b_sink_rows(sink_ref, gi)
                dp = lax.dot_general(vcat, dos, NT, preferred_element_type=F32)
                ps, dss = [], []
                for e in range(2):
                    lse_e = lse_all[2 * gi + e:2 * gi + e + 1]
                    delta_e = delta[e]
                    p = jnp.exp(sc[e * B_WIN:(e + 1) * B_WIN] - lse_e)
                    ps.append(p.astype(BF16))
                    dss.append(p * (dp[e * B_WIN:(e + 1) * B_WIN] - delta_e))
                    dsink_rows.append(-jnp.exp(sink[e] - lse_e) * delta_e)
                ds = jnp.concatenate(dss, axis=0)
                dbias_acc[gi] += ds
                dsb = ds.astype(BF16)
                dq = (lax.dot_general(kcat, dsb, TN, preferred_element_type=F32) * SCALE).T.astype(BF16)
                for pr in range(B_STACK):
                    dz_ref[gi, :, pr * LANES:(pr + 1) * LANES] = dq[pr * TQ:(pr + 1) * TQ]
                dk = _unstack_pair(jnp.dot(dsb, qs, preferred_element_type=F32), B_WIN)
                dv = _unstack_pair(jnp.dot(jnp.concatenate(ps, axis=0), dos, preferred_element_type=F32), B_WIN)
                krows = pl.ds(pl.multiple_of(b * TQ + pad - (B_WIN - TQ), KB), B_WIN)
                dkv_acc[krows, gi * LANES:(gi + 1) * LANES] += dk
                dkv_acc[krows, (B_KV_HEADS + gi) * LANES:(B_KV_HEADS + gi + 1) * LANES] += dv
            dsink_acc[...] += jnp.concatenate(
                dsink_rows + [jnp.zeros((8 - len(dsink_rows), B_ROWS), F32)], axis=0)

        @pl.when(b < left)
        def _():
            step(True)

        @pl.when(b >= left)
        def _():
            step(False)

        @pl.when(b == nb - 1)
        def _():
            lo_s = _lane_lo(s)
            for which in range(2):
                folded = []
                for gi in range(B_KV_HEADS):
                    part = dkv_acc[pad:pad + s, (which * B_KV_HEADS + gi) * LANES:(which * B_KV_HEADS + gi + 1) * LANES]
                    folded.append(part + pltpu.roll(part, HEAD_DIM, 1))
                dkv_ref[:, which * LANES:(which + 1) * LANES] = jnp.where(lo_s, folded[0], folded[1]).astype(BF16)
            lane8 = lax.broadcasted_iota(jnp.int32, dsink_ref.shape, 1)
            tot = jnp.zeros(dsink_ref.shape, F32)
            for h in range(HEADS):
                gi, pr, e = _b_head_place(h)
                dsum_ref[h] = _toeplitz_sum_t(
                    dbias_acc[gi, e * B_WIN:(e + 1) * B_WIN, pr * TQ:(pr + 1) * TQ], B_WIN)
                per_query = dsink_acc[2 * gi + e:2 * gi + e + 1, pr * TQ:(pr + 1) * TQ]
                tot = jnp.where(lane8 == h, jnp.sum(per_query, axis=1, keepdims=True), tot)
            dsink_ref[...] = tot

    row = pl.BlockSpec((TQ, D_MODEL), lambda b: (b, 0))
    base_spec = pl.BlockSpec((HEADS, 1, B_WIDE), lambda b: (0, 0, 0))
    return pl.pallas_call(
        body, name="attn_b_bwd", grid=(nb,),
        in_specs=[row, pl.BlockSpec((pad + s, B_KVX), lambda b: (0, 0)), row, row, row,
                  pl.BlockSpec((1, 8, B_ROWS), lambda b: (b, 0, 0)), base_spec,
                  pl.BlockSpec((1, HEADS), lambda b: (0, 0))],
        out_specs=[pl.BlockSpec((4, TQ, half), lambda b: (0, b, 0)),
                   pl.BlockSpec((s, 2 * LANES), lambda b: (0, 0)), base_spec,
                   pl.BlockSpec((8, LANES), lambda b: (0, 0))],
        out_shape=[jax.ShapeDtypeStruct((4, s, half), BF16), jax.ShapeDtypeStruct((s, 2 * LANES), BF16),
                   jax.ShapeDtypeStruct((HEADS, 1, B_WIDE), F32), jax.ShapeDtypeStruct((8, LANES), F32)],
        scratch_shapes=[pltpu.VMEM((B_KV_HEADS, 2 * B_WIN, B_ROWS), F32),
                        pltpu.VMEM((B_KV_HEADS, 2 * B_WIN, B_ROWS), F32),
                        pltpu.VMEM((pad + s, B_KVX), F32), pltpu.VMEM((8, B_ROWS), F32)],
        compiler_params=_params(("arbitrary",)),
    )(qb, kvx, gate, o, du, lse, base, sinks)


def _t5_bucket(rel):
    nb = T5_BUCKETS // 2
    max_exact = nb // 2
    ret = jnp.where(rel > 0, nb, 0)
    n = jnp.abs(rel)
    nf = jnp.maximum(n, 1).astype(jnp.float32)
    large = max_exact + (jnp.log(nf / max_exact) / math.log(T5_MAX_DIST / max_exact)
                         * (nb - max_exact)).astype(jnp.int32)
    large = jnp.minimum(large, nb - 1)
    return ret + jnp.where(n < max_exact, n, large)


def _a_offset_onehot():
    c = np.arange(A_WIN + TQ)
    dist = A_LEFT_CHUNKS * CHUNK + TQ - 1 - c
    idx = np.clip(dist, -A_REL_CLIP, A_REL_CLIP) + A_REL_CLIP
    onehot = np.zeros((A_WIN + TQ, 2 * A_REL_CLIP + 1), np.float32)
    onehot[c, idx] = 1.0
    return jnp.asarray(onehot)


def _b_offset_onehot():
    c = jnp.arange(B_WIN + TQ, dtype=jnp.int32)
    rel = c - (TQ - 1) - B_LEFT_CHUNKS * CHUNK
    return (_t5_bucket(rel)[:, None] == jnp.arange(T5_BUCKETS)[None, :]).astype(F32)


def _diag_rows(onehot, table):
    rows = jnp.dot(onehot, table.astype(F32), precision=lax.Precision.HIGHEST)
    return rows.T.reshape(HEADS, 1, onehot.shape[0])


def _diag_rows_grad(onehot, ddiag):
    return jnp.dot(ddiag.reshape(HEADS, onehot.shape[0]), onehot, precision=lax.Precision.HIGHEST)


def _position():
    x, y, c = lax.axis_index("x"), lax.axis_index("y"), lax.axis_index("c")
    chips = [(1 - x, y), (x, 1 - y), (1 - x, 1 - y)]
    return x, y, c, chips


ANY = pl.BlockSpec(memory_space=pl.ANY)


def _allgather_routed(shards):
    n = len(shards)

    def piece(block_ref, t, c, quarter=None):
        half = shards[t].shape[0] // 2
        if quarter is None:
            return block_ref.at[pl.ds(c * half, half)]
        return block_ref.at[pl.ds(c * half + quarter * (half // 2), half // 2)]

    def copies(kind, ins, outs, sems):
        ici_send, ici_recv, pass_send, pass_recv, local_sems = sems
        x, y, c, chips = _position()
        mine = 2 * x + y
        if kind == "local":
            return [pltpu.make_async_copy(ins[t], outs[t].at[mine], local_sems.at[t]) for t in range(n)]
        ids = [2 * chip[0] + chip[1] for chip in chips]
        made = []
        for t in range(n):
            def ici(k, to):
                return dict(send_sem=ici_send.at[4 * t + k], recv_sem=ici_recv.at[4 * t + k],
                            device_id=(chips[to][0], chips[to][1], c), device_id_type=MESH)

            def d2d(k):
                return dict(send_sem=pass_send.at[4 * t + k], recv_sem=pass_recv.at[4 * t + k],
                            device_id=(x, y, 1 - c), device_id_type=MESH)

            def same(ref, where):
                return pltpu.make_async_remote_copy(src_ref=ref, dst_ref=ref, **where)

            if kind == "send":
                for k in range(2):
                    made.append(pltpu.make_async_remote_copy(
                        src_ref=piece(ins[t], t, c), dst_ref=piece(outs[t].at[mine], t, c), **ici(k, k)))
            elif kind == "landed":
                made += [same(piece(outs[t].at[ids[k]], t, c), ici(k, k)) for k in range(2)]
            elif kind == "forward":
                made.append(same(piece(outs[t].at[ids[0]], t, c, 0), ici(2, 1)))
                made.append(same(piece(outs[t].at[ids[1]], t, c, 1), ici(3, 0)))
            elif kind == "arrived":
                made.append(same(piece(outs[t].at[ids[2]], t, c, 0), ici(2, 1)))
                made.append(same(piece(outs[t].at[ids[2]], t, c, 1), ici(3, 0)))
            else:
                core = 1 - c if kind == "passed" else c
                if kind in ("pass halves", "passed"):
                    made += [same(piece(outs[t].at[ids[k]], t, core), d2d(k)) for k in range(2)]
                if kind in ("pass quarters", "passed"):
                    made += [same(piece(outs[t].at[ids[2]], t, core, k), d2d(2 + k)) for k in range(2)]
        return made

    def first(ins, outs, sems):
        for cp in copies("local", ins, outs, sems) + copies("send", ins, outs, sems):
            cp.start()

    def middle(ins, outs, sems):
        for got, onward, near in zip(copies("landed", ins, outs, sems), copies("forward", ins, outs, sems),
                                     copies("pass halves", ins, outs, sems)):
            got.wait_recv()
            near.start()
            onward.start()

    def last(ins, outs, sems):
        quarters = copies("pass quarters", ins, outs, sems)
        for got, near in zip(copies("arrived", ins, outs, sems), quarters):
            got.wait_recv()
            near.start()
        for cp in copies("passed", ins, outs, sems):
            cp.wait_recv()
        for cp in (copies("send", ins, outs, sems) + copies("forward", ins, outs, sems)
                   + copies("pass halves", ins, outs, sems) + quarters):
            cp.wait_send()
        for cp in copies("local", ins, outs, sems):
            cp.wait()

    return _Hosted(shards, [jax.ShapeDtypeStruct((4,) + w.shape, w.dtype) for w in shards],
                   [pltpu.SemaphoreType.DMA((4 * n,))] * 4 + [pltpu.SemaphoreType.DMA((n,))],
                   first, middle, last)


def _scatter_hosted(grads):
    n = len(grads)

    def copies(ins, outs, sems):
        send_sems, recv_sems = sems
        x, y, c, chips = _position()
        return [pltpu.make_async_remote_copy(
            src_ref=ins[t].at[2 * chip[0] + chip[1]], dst_ref=outs[t].at[j],
            send_sem=send_sems.at[3 * t + j], recv_sem=recv_sems.at[3 * t + j],
            device_id=(chip[0], chip[1], c), device_id_type=MESH)
            for t in range(n) for j, chip in enumerate(chips)]

    def first(ins, outs, sems):
        for cp in copies(ins, outs, sems):
            cp.start()

    def last(ins, outs, sems):
        for cp in copies(ins, outs, sems):
            cp.wait()

    return _Hosted(grads, [jax.ShapeDtypeStruct((3,) + g.shape[1:], g.dtype) for g in grads],
                   [pltpu.SemaphoreType.DMA((3 * n,))] * 2, first, None, last)


GATHER_PEERS = "x and y neighbours (same core) and the sibling core"
SCATTER_PEERS = "the same core of the three other chips"
EVERYONE = "the seven other devices"


def _run_on_sequencer(name, hosted, peers, collective_id):
    ins = [jax.new_ref(a, memory_space=pltpu.MemorySpace.HBM) for a in hosted.inputs]
    outs = [jax.empty_ref(shape, memory_space=pltpu.MemorySpace.HBM) for shape in hosted.out_shapes]

    @pl.kernel(mesh=plsc.ScalarSubcoreMesh(axis_name="sequencer", num_cores=1), name=name,
               scratch_types=tuple(hosted.sems), compiler_params=pltpu.CompilerParams(collective_id=collective_id))
    def launch(*sems):
        x, y, c, chips = _position()
        if peers == GATHER_PEERS:
            devices = [(chip[0], chip[1], c) for chip in chips[:2]] + [(x, y, 1 - c)]
        elif peers == SCATTER_PEERS:
            devices = [(chip[0], chip[1], c) for chip in chips]
        else:
            devices = [(x ^ (k >> 2), y ^ ((k >> 1) & 1), c ^ (k & 1)) for k in range(1, 8)]
        barrier = pltpu.get_barrier_semaphore()
        for device in devices:
            pl.semaphore_signal(barrier, inc=1, device_id=device, device_id_type=MESH)
        pl.semaphore_wait(barrier, len(devices))
        hosted.first(ins, outs, sems)
        if hosted.middle is not None:
            hosted.middle(ins, outs, sems)
        hosted.last(ins, outs, sems)

    launch()
    return [o[...] for o in outs]


def _gather_gain(shard):
    def body(in_ref, out_ref, send_sems, recv_sems):
        x, y, c, chips = _position()
        out_ref[2 * x + y] = in_ref[...]
        sends = [pltpu.make_async_remote_copy(
            src_ref=in_ref, dst_ref=out_ref.at[2 * x + y], send_sem=send_sems.at[j], recv_sem=recv_sems.at[j],
            device_id=(chip[0], chip[1], c), device_id_type=MESH) for j, chip in enumerate(chips)]
        for cp in sends:
            cp.start()
        for j, chip in enumerate(chips):
            pltpu.make_async_remote_copy(
                src_ref=in_ref, dst_ref=out_ref.at[2 * chip[0] + chip[1]], send_sem=send_sems.at[j],
                recv_sem=recv_sems.at[j], device_id=(chip[0], chip[1], c), device_id_type=MESH).wait_recv()
        for cp in sends:
            cp.wait_send()

    vmem = pl.BlockSpec(memory_space=pltpu.VMEM)
    return pl.pallas_call(
        body, name="gather_gain", in_specs=[vmem], out_specs=vmem,
        out_shape=jax.ShapeDtypeStruct((4,) + shard.shape, shard.dtype),
        scratch_shapes=[pltpu.SemaphoreType.DMA((3,))] * 2,
    )(shard)


def _swap_with_sibling(name, blocks):
    n = len(blocks)

    def body(*refs):
        ins, outs = refs[:n], refs[n:2 * n]
        send_sems, recv_sems = refs[2 * n:]
        x, y, c, _ = _position()
        sends = [pltpu.make_async_remote_copy(
            src_ref=ins[t], dst_ref=outs[t], send_sem=send_sems.at[t], recv_sem=recv_sems.at[t],
            device_id=(x, y, 1 - c), device_id_type=MESH) for t in range(n)]
        for cp in sends:
            cp.start()
        for cp in sends:
            cp.wait()

    return pl.pallas_call(
        body, name=name,
        in_specs=[ANY] * n, out_specs=[ANY] * n,
        out_shape=[jax.ShapeDtypeStruct(b.shape, b.dtype) for b in blocks],
        scratch_shapes=[pltpu.SemaphoreType.DMA((n,))] * 2,
    )(*blocks)


def _everyone_hosted(terms):
    nt = len(terms)

    def copies(kind, ins, outs, sems):
        send_sems, recv_sems, local_sems = sems
        x, y, c, _ = _position()
        me = 4 * x + 2 * y + c
        if kind == "local":
            return [pltpu.make_async_copy(ins[t], outs[t].at[me], local_sems.at[t]) for t in range(nt)]
        made = []
        for t in range(nt):
            for k in range(1, 8):
                peer = (x ^ (k >> 2), y ^ ((k >> 1) & 1), c ^ (k & 1))
                slot = me if kind == "send" else me ^ k
                made.append(pltpu.make_async_remote_copy(
                    src_ref=ins[t], dst_ref=outs[t].at[slot], send_sem=send_sems.at[7 * t + k - 1],
                    recv_sem=recv_sems.at[7 * t + k - 1], device_id=peer, device_id_type=MESH))
        return made

    def first(ins, outs, sems):
        for cp in copies("local", ins, outs, sems) + copies("send", ins, outs, sems):
            cp.start()

    def last(ins, outs, sems):
        for cp in copies("landed", ins, outs, sems):
            cp.wait_recv()
        for cp in copies("send", ins, outs, sems):
            cp.wait_send()
        for cp in copies("local", ins, outs, sems):
            cp.wait()

    return _Hosted(terms, [jax.ShapeDtypeStruct((8,) + a.shape, F32) for a in terms],
                   [pltpu.SemaphoreType.DMA((7 * nt,))] * 2 + [pltpu.SemaphoreType.DMA((nt,))], first, None, last)


def _small_step(partials, extras, ws, ms, vs, shard_of):
    n = len(partials)
    terms = list(partials) + list(extras)
    nt = len(terms)
    rows = [t for t in range(nt) if terms[t].shape[0] == 1]
    mats = [t for t in range(nt) if terms[t].shape[0] != 1]
    row_block = (8, max(terms[t].shape[1] for t in rows))
    assert len(rows) <= row_block[0]
    vmem = pl.BlockSpec(memory_space=pltpu.VMEM)

    def pack(*refs):
        packed = refs[-1]
        packed[...] = jnp.zeros_like(packed)
        for i, t in enumerate(rows):
            packed[i:i + 1, 0:terms[t].shape[1]] = refs[i][...]

    packed = pl.pallas_call(pack, name="small_pack", in_specs=[vmem] * len(rows), out_specs=vmem,
                            out_shape=jax.ShapeDtypeStruct(row_block, F32))(*[terms[t] for t in rows])
    slots = _run_on_sequencer("allgather_small", _everyone_hosted([packed] + [terms[t] for t in mats]),
                              EVERYONE, 2)

    def body(*refs):
        slot_refs, refs = refs[:len(slots)], refs[len(slots):]
        w_refs, refs = refs[:n], refs[n:]
        m_refs, refs = refs[:n], refs[n:]
        v_refs, outs = refs[:n], refs[n:]
        sums = []
        for ref in slot_refs:
            g = ref[0]
            for dev in range(1, 8):
                g = g + ref[dev]
            sums.append(g)
        chip = 2 * lax.axis_index("x") + lax.axis_index("y")
        for t in range(nt):
            if t in rows:
                i = rows.index(t)
                g = sums[0][i:i + 1, 0:terms[t].shape[1]]
            else:
                g = sums[1 + mats.index(t)]
            if t >= n:
                outs[4 * n + t - n][...] = g
                continue
            if shard_of[t]:
                width = ws[t].shape[-1]
                mine = jnp.zeros(ws[t].shape, F32)
                for s in range(4):
                    mine = jnp.where(chip == s, g[:, s * width:(s + 1) * width], mine)
                g = mine
            delta, mn, vn = _adamw_math(w_refs[t][...], g, m_refs[t][...], v_refs[t][...])
            outs[4 * t][...] = g
            outs[4 * t + 1][...] = delta
            outs[4 * t + 2][...] = mn
            outs[4 * t + 3][...] = vn

    out_shapes = []
    for t in range(n):
        out_shapes += [jax.ShapeDtypeStruct(ws[t].shape, F32)] * 4
    out_shapes += [jax.ShapeDtypeStruct(a.shape, F32) for a in extras]
    res = pl.pallas_call(
        body, name="small_step",
        in_specs=[vmem] * (len(slots) + 3 * n), out_specs=[vmem] * len(out_shapes), out_shape=out_shapes,
    )(*slots, *ws, *ms, *vs)
    return [res[4 * t:4 * t + 4] for t in range(n)], res[4 * n:4 * n + nt - n]


def _adamw_math(w, g, m, v):
    m = ADAM_B1 * m + (1.0 - ADAM_B1) * g
    v = ADAM_B2 * v + (1.0 - ADAM_B2) * (g * g)
    m_hat = m / (1.0 - ADAM_B1 ** ADAM_STEP)
    v_hat = v / (1.0 - ADAM_B2 ** ADAM_STEP)
    delta = -ADAM_LR * (m_hat / (jnp.sqrt(v_hat) + ADAM_EPS) + ADAM_WD * w)
    return delta, m, v


def _row_tile(rows):
    return 256 if rows % 256 == 0 else rows


def _sum_partials(name, own, recv, chip, after):
    rows, cols = own.shape[1:]
    tr = _row_tile(rows)

    def body(chip_ref, own_ref, recv_ref, after_ref, o_ref):
        acc = own_ref[...]
        for j in range(3):
            acc = acc + recv_ref[j].astype(F32)
        o_ref[...] = acc

    return pl.pallas_call(
        body, name=name,
        grid_spec=pltpu.PrefetchScalarGridSpec(
            num_scalar_prefetch=1, grid=(rows // tr,),
            in_specs=[pl.BlockSpec((None, tr, cols), lambda i, chip_ref: (chip_ref[0], i, 0)),
                      pl.BlockSpec((3, tr, cols), lambda i, chip_ref: (0, i, 0)), ANY],
            out_specs=pl.BlockSpec((tr, cols), lambda i, chip_ref: (i, 0))),
        out_shape=jax.ShapeDtypeStruct((rows, cols), F32),
        compiler_params=_params(("parallel",)),
    )(chip.reshape(1).astype(jnp.int32), own, recv, after)


def _adamw(name, w, m, v, g_parts):
    rows, cols = w.shape
    tr = _row_tile(rows)
    n = len(g_parts)

    def body(w_ref, m_ref, v_ref, *refs):
        g_refs = refs[:n]
        go_ref, d_ref, mo_ref, vo_ref = refs[n:]
        g = g_refs[0][...]
        for r in g_refs[1:]:
            g = g + r[...]
        delta, mn, vn = _adamw_math(w_ref[...], g, m_ref[...], v_ref[...])
        go_ref[...] = g
        d_ref[...] = delta
        mo_ref[...] = mn
        vo_ref[...] = vn

    spec = pl.BlockSpec((tr, cols), lambda i: (i, 0))
    return pl.pallas_call(
        body, name=name, grid=(rows // tr,),
        in_specs=[spec] * (3 + n), out_specs=[spec] * 4,
        out_shape=[jax.ShapeDtypeStruct((rows, cols), F32)] * 4,
        compiler_params=_params(("parallel",)),
    )(w, m, v, *g_parts)


def _local_step(x, target, ga, wa_in, rel_bias, later_shards, gk, t5, gb, sinks, gf):
    s, d = x.shape
    tm = min(TM_DENSE, s)
    nt = s // tm
    half = d // 2
    row = pl.BlockSpec((tm, d), lambda i: (i, 0))
    whole = lambda shape: pl.BlockSpec(shape, lambda *_: (0,) * len(shape))

    n1, = _norm_fwd("norm_a", x, ga)
    projected = None
    for h, (wa_half, tag) in enumerate(zip(wa_in, ("first", "second"))):
        projected = _proj_a_half("proj_a_" + tag, n1, wa_half, h, projected)
    zqkv, gate_a = projected
    onehot_a = _a_offset_onehot()
    diag_a = _diag_rows(onehot_a, rel_bias)
    (o_a, u_a, lse_a), gathered = _attn_a_fwd(zqkv, gate_a, diag_a, hosted=_allgather_routed(later_shards))
    wa_out, wkv, wb_in, wb_out, wkv_x = gathered
    wa_out = wa_out.reshape(d, d)
    wkv = wkv.reshape(d, -1)
    wkv_x = wkv_x.reshape(d, B_KVX)
    wb_out = wb_out.reshape(d, d)
    h1, nk, n2 = _out_norms("out_a_norms", u_a, wa_out, x, jnp.concatenate([gk, gb], axis=0))
    kvw = wkv.shape[1]
    kvx =_matmul("proj_kv", nk, wkv_x, dims=NN, grid=(nt + 1,), zero_axis=0,
                  a_spec=pl.BlockSpec((tm, d), lambda i: (jnp.maximum(i - 1, 0), 0)), b_spec=whole((d, B_KVX)),
                  o_spec=pl.BlockSpec((tm, B_KVX), lambda i: (i, 0)), out_shape=(tm + s, B_KVX), out_dtype=BF16)
    qb, gate_b = _proj_b(n2, wb_in)
    onehot_b = _b_offset_onehot()
    base_b = jnp.roll(_diag_rows(onehot_b, t5)[..., ::-1], TQ, axis=-1)
    o_b, u_b, lse_b = _attn_b_fwd(qb, kvx, gate_b, base_b, sinks)
    dh2, loss, d_gf = _out_loss_head(u_b, wb_out, h1, target, gf)

    du_b = _matmul("dout_b", dh2, wb_out, dims=NT, grid=(nt,), a_spec=row, b_spec=whole((d, d)), o_spec=row,
                   out_shape=(s, d), out_dtype=F32)
    d_wb_out = _dw_out("dw_out_b", u_b, dh2)
    dz_b, dkv, dsum_b, dsinks = _attn_b_bwd(qb, kvx, gate_b, o_b, du_b, lse_b, base_b, sinks)
    ddiag_b = jnp.roll(dsum_b[..., ::-1], -1, axis=-1)
    d_wb_in = _matmul("dw_in_b", n2, dz_b, dims=TN, grid=(4,),
                      a_spec=whole((s, d)), b_spec=pl.BlockSpec((None, s, half), lambda j: (j, 0, 0)),
                      o_spec=pl.BlockSpec((None, d, half), lambda j: (j, 0, 0)),
                      out_shape=(4, d, half), out_dtype=F32, also_bf16=True)
    d_wkv = _matmul("dw_kv", nk, dkv, dims=TN, grid=(1,),
                    a_spec=whole((s, d)), b_spec=whole((s, kvw)), o_spec=whole((d, kvw)),
                    out_shape=(d, kvw), out_dtype=F32, also_bf16=True)
    dh1, d_gkb = _proj_norm_bwd("dproj_kv_b", h1, dh2, jnp.concatenate([gk, gb], axis=0),
                                [(dkv[None], [wkv[None]]), (dz_b, [wb_in])])

    du_a = _matmul("dout_a", dh1, wa_out, dims=NT, grid=(nt,), a_spec=row, b_spec=whole((d, d)), o_spec=row,
                   out_shape=(s, d), out_dtype=F32)
    d_wa_out = _dw_out("dw_out_a", u_a, dh1)
    early = dict(a_w_out=[g.reshape(4, d // 4, d) for g in d_wa_out],
                 kv_w=[g.reshape(4, d // 4, kvw) for g in d_wkv], b_w_in=list(d_wb_in),
                 b_w_out=[g.reshape(4, d // 4, d) for g in d_wb_out])
    (dz_a, ddiag_a), early_recv = _attn_a_bwd(
        zqkv, gate_a, o_a, du_a, lse_a, diag_a, hosted=_scatter_hosted([early[n][1] for n in early]))
    d_wa_in = _matmul("dw_in_a", n1, dz_a, dims=TN, grid=(4, 2),
                      a_spec=whole((s, d)), b_spec=pl.BlockSpec((None, s, half), lambda j, h: (j, 0, h)),
                      o_spec=pl.BlockSpec((None, d, half), lambda j, h: (j, 0, h)),
                      out_shape=(4, d, d), out_dtype=F32, also_bf16=True)
    late_recv = _run_on_sequencer("scatter_a_w_in", _scatter_hosted([d_wa_in[1]]), SCATTER_PEERS, 0)
    grad_x, d_ga = _proj_norm_bwd("dproj_a", x, dh1, ga, [(dz_a, list(wa_in))])

    small = dict(a_norm=d_ga, kv_norm=d_gkb[0:1], b_norm=d_gkb[1:2], b_sinks=dsinks[0:1, :HEADS], final_norm=d_gf)
    small["by_offset"] = dict(a_rel_bias=(onehot_a, ddiag_a.reshape(HEADS, -1)),
                              t5_bias=(onehot_b, ddiag_b.reshape(HEADS, -1)))
    own = dict(a_w_in=d_wa_in[0], **{n: early[n][0] for n in early})
    received = dict(a_w_in=late_recv[0], **dict(zip(early, early_recv)))
    return loss, grad_x, small, own, received, d_wa_in[1]


SMALL = ("a_norm", "kv_norm", "b_norm", "b_sinks", "final_norm")
TABLES = ("a_rel_bias", "t5_bias")
BIG = ("a_w_in", "a_w_out", "kv_w", "b_w_in", "b_w_out")
ORDER = ("a_norm", "a_w_in", "a_rel_bias", "a_w_out", "kv_norm", "kv_w", "t5_bias", "b_norm", "b_w_in",
         "b_sinks", "b_w_out", "final_norm")


def kernel(x, a_norm, a_w_in, a_rel_bias, a_w_out, kv_norm, kv_w, t5_bias, b_norm, b_w_in, b_sinks, b_w_out, final_norm, loss_target, m_a_norm, m_a_w_in, m_a_rel_bias, m_a_w_out, m_kv_norm, m_kv_w, m_t5_bias, m_b_norm, m_b_w_in, m_b_sinks, m_b_w_out, m_final_norm, v_a_norm, v_a_w_in, v_a_rel_bias, v_a_w_out, v_kv_norm, v_kv_w, v_t5_bias, v_b_norm, v_b_w_in, v_b_sinks, v_b_w_out, v_final_norm):
    w = dict(a_norm=a_norm, a_w_in=a_w_in, a_rel_bias=a_rel_bias, a_w_out=a_w_out, kv_norm=kv_norm, kv_w=kv_w,
             t5_bias=t5_bias, b_norm=b_norm, b_w_in=b_w_in, b_sinks=b_sinks, b_w_out=b_w_out,
             final_norm=final_norm)
    m = dict(a_norm=m_a_norm, a_w_in=m_a_w_in, a_rel_bias=m_a_rel_bias, a_w_out=m_a_w_out, kv_norm=m_kv_norm,
             kv_w=m_kv_w, t5_bias=m_t5_bias, b_norm=m_b_norm, b_w_in=m_b_w_in, b_sinks=m_b_sinks,
             b_w_out=m_b_w_out, final_norm=m_final_norm)
    v = dict(a_norm=v_a_norm, a_w_in=v_a_w_in, a_rel_bias=v_a_rel_bias, a_w_out=v_a_w_out, kv_norm=v_kv_norm,
             kv_w=v_kv_w, t5_bias=v_t5_bias, b_norm=v_b_norm, b_w_in=v_b_w_in, b_sinks=v_b_sinks,
             b_w_out=v_b_w_out, final_norm=v_final_norm)
    d = D_MODEL
    chip = 2 * lax.axis_index("x") + lax.axis_index("y")

    shard2d = dict(a_w_in=a_w_in[0], a_w_out=a_w_out[0], kv_w=kv_w, b_w_in=b_w_in[0], b_w_out=b_w_out[0])

    first = shard2d["a_w_in"].astype(BF16)
    wa_in = [_run_on_sequencer("allgather_" + tag, _allgather_routed([first[:, h * (d // 2):(h + 1) * (d // 2)]]),
                               GATHER_PEERS, collective_id)[0]
             for h, (tag, collective_id) in enumerate((("first", 1), ("second", 3)))]
    ga = _gather_gain(a_norm).reshape(1, d)

    later = [shard2d[n].astype(BF16) for n in BIG[1:]]
    kv_shard = later[BIG[1:].index("kv_w")]
    later.append(jnp.concatenate(
        [kv_shard[:, (i // 2) * HEAD_DIM:(i // 2 + 1) * HEAD_DIM] for i in range(B_KVX // HEAD_DIM)], axis=1))
    loss, grad_x, small, own, received, after_attention = _local_step(
        x[0], loss_target[0], ga, wa_in, a_rel_bias[0], later,
        kv_norm.reshape(1, d), t5_bias, b_norm, b_sinks, final_norm.reshape(1, d))

    out = {}
    as2d = lambda a: a.reshape(-1, a.shape[-1])
    small_res, (loss_sum, *offset_sums) = _small_step(
        [small[n] for n in SMALL], [loss] + [small["by_offset"][n][1] for n in TABLES],
        [as2d(w[n]) for n in SMALL], [as2d(m[n]) for n in SMALL], [as2d(v[n]) for n in SMALL],
        [n == "a_norm" for n in SMALL])
    for n, res in zip(SMALL, small_res):
        out[n] = [r.reshape(w[n].shape) for r in res]
    loss_out = loss_sum.reshape(())
    for n, summed in zip(TABLES, offset_sums):
        grad = _diag_rows_grad(small["by_offset"][n][0], summed)
        res = _adamw("adamw_" + n, as2d(w[n]).T, as2d(m[n]).T, as2d(v[n]).T, [grad])
        out[n] = [r.T.reshape(w[n].shape) for r in res]

    core_sums = [_sum_partials("sum_" + n, own[n], received[n], chip, after_attention) for n in BIG]
    sibling_sums = (_swap_with_sibling("swap_last", core_sums[:1])
                    + _swap_with_sibling("swap_early", core_sums[1:]))

    for n, mine, theirs in zip(BIG, core_sums, sibling_sums):
        res = _adamw("adamw_" + n, shard2d[n], m[n].reshape(shard2d[n].shape), v[n].reshape(shard2d[n].shape),
                     [mine, theirs])
        out[n] = [r.reshape(w[n].shape) for r in res]

    grads = [out[n][0] for n in ORDER]
    deltas = [out[n][1] for n in ORDER]
    new_m = [out[n][2] for n in ORDER]
    new_v = [out[n][3] for n in ORDER]
    return (loss_out, grad_x[None], *grads, *deltas, *new_m, *new_v)
```

```python
import math

import jax
import jax.numpy as jnp
import numpy as np
from jax import lax
from jax.experimental import pallas as pl
from jax.experimental.pallas import tpu as pltpu
from jax.experimental.pallas import tpu_sc as plsc

F32 = jnp.float32
BF16 = jnp.bfloat16
MESH = pl.DeviceIdType.MESH

D_MODEL = 1024
HEADS = 16
HEAD_DIM = 64
CHUNK = 64
RMS_EPS = 1e-6
SCALE = HEAD_DIM ** -0.5
A_LEFT_CHUNKS = 8
A_REL_CLIP = 256
B_LEFT_CHUNKS = 2
B_KV_HEADS = 2
B_GROUP = HEADS // B_KV_HEADS
T5_BUCKETS = 32
T5_MAX_DIST = 128
ADAM_LR = 0.001
ADAM_B1 = 0.9
ADAM_B2 = 0.999
ADAM_EPS = 1e-08
ADAM_WD = 0.01
ADAM_STEP = 10

MASKED = -1e30
LANES = 128
TQ = 128
A_PAIRS = 2
A_PAIRS_FWD = 4
KB = 128
A_KBLOCKS = A_LEFT_CHUNKS * CHUNK // KB + 1
B_KBLOCKS = B_LEFT_CHUNKS * CHUNK // KB + 1
A_WIN = A_KBLOCKS * KB
B_WIN = B_KBLOCKS * KB
TM = 512
TM_DENSE = 1024
TM_HALF = 2048
TM_PARTS = 512
VMEM_LIMIT = 56 * 1024 * 1024

NT = (((1,), (1,)), ((), ()))
TN = (((0,), (0,)), ((), ()))
NN = (((1,), (0,)), ((), ()))


def _params(sem=None):
    return pltpu.CompilerParams(dimension_semantics=sem, vmem_limit_bytes=VMEM_LIMIT)


class _Hosted:
    def __init__(self, inputs, out_shapes, sems, first, middle, last):
        self.inputs, self.out_shapes, self.sems = list(inputs), list(out_shapes), list(sems)
        self.first, self.middle, self.last = first, middle, last


def _call(body, *, name, grid, in_specs, out_specs, out_shape, args, scratch_shapes=(), sem=None, hosted=None,
          aliases=None):
    in_specs, out_specs, out_shape = list(in_specs), list(out_specs), list(out_shape)
    scratch_shapes = list(scratch_shapes)
    if hosted is None:
        out = pl.pallas_call(
            body, name=name, grid=grid, in_specs=in_specs, out_specs=out_specs, out_shape=out_shape,
            scratch_shapes=scratch_shapes, input_output_aliases=aliases or {},
            compiler_params=_params(sem))(*args)
        return list(out), []
    assert aliases is None
    n_in, n_out, n_scr = len(in_specs), len(out_shape), len(scratch_shapes)
    h_in, h_out = len(hosted.inputs), len(hosted.out_shapes)
    total = int(np.prod(grid)) if grid else 1

    def wrapped(*refs):
        ins, refs = refs[:n_in], refs[n_in:]
        h_ins, refs = refs[:h_in], refs[h_in:]
        outs, refs = refs[:n_out], refs[n_out:]
        h_outs, refs = refs[:h_out], refs[h_out:]
        scr, h_sems = refs[:n_scr], refs[n_scr:]
        step = 0
        for axis, size in enumerate(grid):
            step = step * size + pl.program_id(axis)

        if hosted.first is not None:
            @pl.when(step == 0)
            def _():
                hosted.first(h_ins, h_outs, h_sems)

        body(*ins, *outs, *scr)
        if hosted.middle is not None:
            @pl.when(step == total // 2)
            def _():
                hosted.middle(h_ins, h_outs, h_sems)

        if hosted.last is not None:
            @pl.when(step == total - 1)
            def _():
                hosted.last(h_ins, h_outs, h_sems)

    out = pl.pallas_call(
        wrapped, name=name, grid=grid, in_specs=in_specs + [ANY] * h_in, out_specs=out_specs + [ANY] * h_out,
        out_shape=out_shape + hosted.out_shapes, scratch_shapes=scratch_shapes + hosted.sems,
        compiler_params=_params(("arbitrary",) * len(grid)))(*args, *hosted.inputs)
    return list(out[:n_out]), list(out[n_out:])


def _matmul(name, a, b, *, dims, grid, a_spec, b_spec, o_spec, out_shape, out_dtype,
            also_bf16=False, zero_axis=None):
    def body(*refs):
        if zero_axis is None:
            product(*refs)
        else:
            @pl.when(pl.program_id(zero_axis) == 0)
            def _():
                refs[2][...] = jnp.zeros_like(refs[2])

            @pl.when(pl.program_id(zero_axis) > 0)
            def _():
                product(*refs)

    def product(a_ref, b_ref, o_ref, *more):
        prod = lax.dot_general(a_ref[...].astype(BF16), b_ref[...].astype(BF16), dims,
                               preferred_element_type=F32)
        o_ref[...] = prod.astype(out_dtype)
        if also_bf16:
            more[0][...] = prod.astype(BF16)

    out_specs = [o_spec]
    out_shapes = [jax.ShapeDtypeStruct(out_shape, out_dtype)]
    if also_bf16:
        out_specs.append(o_spec)
        out_shapes.append(jax.ShapeDtypeStruct(out_shape, BF16))
    out, _ = _call(body, name=name, grid=grid, in_specs=[a_spec, b_spec], out_specs=out_specs,
                   out_shape=out_shapes, args=[a, b], sem=("parallel",) * len(grid))
    return out[0] if not also_bf16 else tuple(out)


def _proj_a_half(name, n1, w, h, into):
    s, d = n1.shape
    half = w.shape[2]
    ta = min(TM_HALF, s)

    def body(a_ref, w_ref, *refs):
        z_ref, g_ref = refs[-2:]
        i, j = pl.program_id(0), pl.program_id(1)

        @pl.when((i == 0) & (j < 3))
        def _():
            z_ref[...] = jnp.zeros_like(z_ref)

        @pl.when((i > 0) & (j < 3))
        def _():
            z_ref[...] = jnp.dot(a_ref[...], w_ref[...], preferred_element_type=F32).astype(BF16)

        @pl.when((i > 0) & (j == 3))
        def _():
            g_ref[...] = jnp.dot(a_ref[...], w_ref[...], preferred_element_type=F32)

    out, _ = _call(
        body, name=name, grid=(s // ta + 1, 4),
        in_specs=[pl.BlockSpec((ta, d), lambda i, j: (jnp.maximum(i - 1, 0), 0)),
                  pl.BlockSpec((None, d, half), lambda i, j: (j, 0, 0))] + ([] if into is None else [ANY, ANY]),
        out_specs=[pl.BlockSpec((None, ta, half), lambda i, j: (jnp.minimum(j, 2), i, h)),
                   pl.BlockSpec((ta, half), lambda i, j: (jnp.maximum(i - 1, 0), h))],
        out_shape=[jax.ShapeDtypeStruct((3, ta + s, d), BF16), jax.ShapeDtypeStruct((s, d), F32)],
        args=[n1, w] + ([] if into is None else list(into)), sem=("arbitrary", "arbitrary"),
        aliases=None if into is None else {2: 0, 3: 1})
    return out


def _proj_b(n2, w):
    s, d = n2.shape
    half = w.shape[2]
    ta = min(TM_HALF, s)

    def body(a_ref, w_ref, q_ref, g_ref):
        j = pl.program_id(1)

        @pl.when(j < 2)
        def _():
            q_ref[...] = jnp.dot(a_ref[...], w_ref[...], preferred_element_type=F32).astype(BF16)

        @pl.when(j >= 2)
        def _():
            g_ref[...] = jnp.dot(a_ref[...], w_ref[...], preferred_element_type=F32)

    out, _ = _call(
        body, name="proj_b", grid=(s // ta, 4),
        in_specs=[pl.BlockSpec((ta, d), lambda i, j: (i, 0)), pl.BlockSpec((None, d, half), lambda i, j: (j, 0, 0))],
        out_specs=[pl.BlockSpec((ta, half), lambda i, j: (i, jnp.minimum(j, 1))),
                   pl.BlockSpec((ta, half), lambda i, j: (i, jnp.maximum(j - 2, 0)))],
        out_shape=[jax.ShapeDtypeStruct((s, d), BF16), jax.ShapeDtypeStruct((s, d), F32)],
        args=[n2, w], sem=("arbitrary", "arbitrary"))
    return out


def _rms_rows(x):
    return lax.rsqrt(jnp.mean(x * x, axis=-1, keepdims=True) + RMS_EPS)


def _norm_fwd(name, x, gains):
    s, d = x.shape
    n = gains.shape[0]

    def body(x_ref, g_ref, *o_refs):
        xv = x_ref[...]
        xh = xv * _rms_rows(xv)
        for i in range(n):
            o_refs[i][...] = (xh * g_ref[i:i + 1, :]).astype(BF16)

    row = pl.BlockSpec((TM, d), lambda i: (i, 0))
    return pl.pallas_call(
        body, name=name, grid=(s // TM,),
        in_specs=[row, pl.BlockSpec((n, d), lambda i: (0, 0))],
        out_specs=[row] * n,
        out_shape=[jax.ShapeDtypeStruct((s, d), BF16)] * n,
        compiler_params=_params(("parallel",)),
    )(x, gains)


def _proj_norm_bwd(name, x, dres, gains, branches, also_bf16=False):
    s, d = x.shape
    n = len(branches)
    n_ab = 2 * sum(len(bs) for _, bs in branches)
    tm = min(TM_PARTS, s)

    def body(x_ref, r_ref, g_ref, *refs):
        ab_refs, dx_ref, dg_ref = list(refs[:n_ab]), refs[n_ab], refs[n_ab + 1]
        i = pl.program_id(0)
        xv = x_ref[...]
        r = _rms_rows(xv)
        xh = xv * r

        @pl.when(i == 0)
        def _():
            dg_ref[...] = jnp.zeros_like(dg_ref)

        a = None
        for j in range(n):
            dn = None
            for _ in branches[j][1]:
                a_ref, b_ref = ab_refs.pop(0), ab_refs.pop(0)
                for part in range(a_ref.shape[0]):
                    term = lax.dot_general(a_ref[part], b_ref[part], NT, preferred_element_type=F32)
                    dn = term if dn is None else dn + term
            t = dn * g_ref[j:j + 1, :]
            a = t if a is None else a + t
            dg_ref[j:j + 1, :] += jnp.sum(dn * xh, axis=0, keepdims=True)
        dx = r_ref[...] + r * (a - xh * jnp.mean(xh * a, axis=-1, keepdims=True))
        dx_ref[...] = dx
        if also_bf16:
            refs[n_ab + 2][...] = dx.astype(BF16)

    row = pl.BlockSpec((tm, d), lambda i: (i, 0))
    small = pl.BlockSpec((n, d), lambda i: (0, 0))
    ab_specs, ab_args = [], []
    for a, bs in branches:
        for k, b in enumerate(bs):
            ab_specs += [pl.BlockSpec((a.shape[0], tm, b.shape[2]), lambda i, k=k: (0, i, k)),
                         pl.BlockSpec(b.shape, lambda i: (0, 0, 0))]
            ab_args += [a, b]
    return pl.pallas_call(
        body, name=name, grid=(s // tm,),
        in_specs=[row, row, small] + ab_specs,
        out_specs=[row, small] + [row] * also_bf16,
        out_shape=[jax.ShapeDtypeStruct((s, d), F32), jax.ShapeDtypeStruct((n, d), F32)]
        + [jax.ShapeDtypeStruct((s, d), BF16)] * also_bf16,
        compiler_params=_params(("arbitrary",)),
    )(x, dres, gains, *ab_args)


def _out_norms(name, u, w_out, resid, gains):
    s, d = resid.shape
    n = gains.shape[0]
    tm = min(TM_DENSE, s)

    def body(u_ref, w_ref, r_ref, g_ref, h_ref, *o_refs):
        hv = r_ref[...] + jnp.dot(u_ref[...], w_ref[...], preferred_element_type=F32)
        h_ref[...] = hv
        hh = hv * _rms_rows(hv)
        for i in range(n):
            o_refs[i][...] = (hh * g_ref[i:i + 1, :]).astype(BF16)

    row = pl.BlockSpec((tm, d), lambda i: (i, 0))
    return pl.pallas_call(
        body, name=name, grid=(s // tm,),
        in_specs=[row, pl.BlockSpec((d, d), lambda i: (0, 0)), row, pl.BlockSpec((n, d), lambda i: (0, 0))],
        out_specs=[row] * (n + 1),
        out_shape=[jax.ShapeDtypeStruct((s, d), F32)] + [jax.ShapeDtypeStruct((s, d), BF16)] * n,
        compiler_params=_params(("parallel",)),
    )(u, w_out, resid, gains)


def _out_loss_head(u, w_out, resid, target, gain):
    s, d = resid.shape
    tm = min(TM_PARTS, s)

    def body(u_ref, w_ref, r_ref, t_ref, g_ref, dh_ref, loss_ref, dg_ref, dhb_ref):
        i = pl.program_id(0)
        hv = r_ref[...] + jnp.dot(u_ref[...], w_ref[...], preferred_element_type=F32)
        r = _rms_rows(hv)
        hh = hv * r
        g = g_ref[...]
        err = hh * g - t_ref[...]
        part = 0.5 * jnp.sum(jnp.sum(err * err, axis=-1, keepdims=True) * (1.0 / d), axis=0, keepdims=True)
        dy = err * (1.0 / d)
        a = dy * g
        dh = r * (a - hh * jnp.mean(hh * a, axis=-1, keepdims=True))
        dh_ref[...] = dh
        dhb_ref[...] = dh.astype(BF16)
        dg = jnp.sum(dy * hh, axis=0, keepdims=True)

        @pl.when(i == 0)
        def _():
            loss_ref[...] = part
            dg_ref[...] = dg

        @pl.when(i > 0)
        def _():
            loss_ref[...] += part
            dg_ref[...] += dg

    row = pl.BlockSpec((tm, d), lambda i: (i, 0))
    return pl.pallas_call(
        body, name="out_b_loss_head", grid=(s // tm,),
        in_specs=[row, pl.BlockSpec((d, d), lambda i: (0, 0)), row, row, pl.BlockSpec((1, d), lambda i: (0, 0))],
        out_specs=[row, pl.BlockSpec((1, 1), lambda i: (0, 0)), pl.BlockSpec((1, d), lambda i: (0, 0)), row],
        out_shape=[jax.ShapeDtypeStruct((s, d), F32), jax.ShapeDtypeStruct((1, 1), F32),
                   jax.ShapeDtypeStruct((1, d), F32), jax.ShapeDtypeStruct((s, d), BF16)],
        compiler_params=_params(("arbitrary",)),
    )(u, w_out, resid, target, gain)


def _silu_parts(g):
    sig = jax.nn.sigmoid(g)
    return g * sig, sig * (1.0 + g * (1.0 - sig))


def _lane_lo(rows):
    return lax.broadcasted_iota(jnp.int32, (rows, LANES), 1) < HEAD_DIM


def _stack_pair(x):
    lo = _lane_lo(x.shape[0])
    zero = jnp.zeros_like(x)
    return jnp.concatenate([jnp.where(lo, x, zero), jnp.where(lo, zero, x)], axis=0)


def _unstack_pair(y, w):
    return jnp.where(_lane_lo(w), y[:w], y[w:])


def _block_valid(b, left_blocks, width):
    col = lax.broadcasted_iota(jnp.int32, (1, 2 * width), 1)
    col = jnp.where(col >= width, col - width, col)
    return (col // KB + (b - left_blocks)) >= 0


def _toeplitz_tile(diag_row, width, left_chunks):
    wide = width + TQ
    rolled = pltpu.roll(jnp.broadcast_to(diag_row, (TQ, wide)), 1, 1, stride=1, stride_axis=0)
    i = lax.broadcasted_iota(jnp.int32, (TQ, width), 0) // CHUNK
    j = lax.broadcasted_iota(jnp.int32, (TQ, width), 1) // CHUNK
    dc = i + left_chunks - j
    return jnp.where((dc >= 0) & (dc <= left_chunks), rolled[:, TQ:], MASKED)


def _toeplitz_sum(tile, width):
    flip = (lax.broadcasted_iota(jnp.int32, (TQ, TQ), 0) + lax.broadcasted_iota(jnp.int32, (TQ, TQ), 1)
            == TQ - 1).astype(F32)
    reversed_rows = jnp.dot(flip, tile, precision=lax.Precision.HIGHEST, preferred_element_type=F32)
    padded = jnp.concatenate([reversed_rows, jnp.zeros((TQ, TQ), F32)], axis=1)
    rolled = pltpu.roll(padded, 0, 1, stride=1, stride_axis=0)
    return jnp.sum(rolled, axis=0, keepdims=True)


def _softmax_pair(sc, w, sink=None):
    ps, inv, lses = [], [], []
    for e in range(2):
        sh = sc[:, e * w:(e + 1) * w]
        m = jnp.max(sh, axis=-1, keepdims=True)
        if sink is not None:
            m = jnp.maximum(m, sink[e])
        ex = jnp.exp(sh - m)
        l = jnp.sum(ex, axis=-1, keepdims=True)
        if sink is not None:
            l = l + jnp.exp(sink[e] - m)
        ps.append(ex.astype(BF16))
        inv.append(1.0 / l)
        lses.append(m + jnp.log(l))
    return jnp.concatenate(ps, axis=-1), inv, lses


def _softmax_pair_bwd(sc, dp, lse, delta, w):
    ps, dss = [], []
    for e in range(2):
        p = jnp.exp(sc[:, e * w:(e + 1) * w] - lse[e])
        ps.append(p)
        dss.append(p * (dp[:, e * w:(e + 1) * w] - delta[e]))
    return jnp.concatenate(ps, axis=-1), jnp.concatenate(dss, axis=-1)


def _pair_rowsums(x, lo):
    zero = jnp.zeros_like(x)
    return (jnp.sum(jnp.where(lo, x, zero), axis=-1, keepdims=True),
            jnp.sum(jnp.where(lo, zero, x), axis=-1, keepdims=True))


def _a_qkv_specs(rows, pad, pw):
    return [pl.BlockSpec((None, TQ, pw), lambda p, b: (0, b + pad // TQ, p)),
            pl.BlockSpec((None, rows, pw), lambda p, b: (1, 0, p)),
            pl.BlockSpec((None, rows, pw), lambda p, b: (2, 0, p))]


def _window(ref, b, pad, win, lanes):
    start = pl.multiple_of(b * TQ + pad - (win - TQ), KB)
    return ref[pl.ds(start, win), lanes]


def _attn_a_fwd(zqkv, g, diag, hosted=None):
    s = g.shape[0]
    pad = zqkv.shape[1] - s
    nb = s // TQ
    left = A_KBLOCKS - 1
    pairs = A_PAIRS_FWD
    pw = pairs * LANES
    wide = A_WIN + TQ

    def body(q_ref, k_ref, v_ref, g_ref, diag_ref, o_ref, u_ref, lse_ref, bias_scr):
        b = pl.program_id(1)

        @pl.when(b == 0)
        def _():
            for hh in range(2 * pairs):
                bias_scr[hh // 2, :, (hh % 2) * A_WIN:(hh % 2 + 1) * A_WIN] = _toeplitz_tile(
                    diag_ref[hh], A_WIN, A_LEFT_CHUNKS)

        def step(first_blocks):
            lo = _lane_lo(TQ)
            for pp in range(pairs):
                ln = slice(pp * LANES, (pp + 1) * LANES)
                kcat = _stack_pair(_window(k_ref, b, pad, A_WIN, ln))
                vcat = _stack_pair(_window(v_ref, b, pad, A_WIN, ln))
                sc = lax.dot_general(q_ref[:, ln] * SCALE, kcat, NT, preferred_element_type=F32) + bias_scr[pp]
                if first_blocks:
                    sc = jnp.where(_block_valid(b, left, A_WIN), sc, MASKED)
                p, inv, lses = _softmax_pair(sc, A_WIN)
                ov = jnp.dot(p, vcat, preferred_element_type=F32) * jnp.where(lo, inv[0], inv[1])
                o_ref[:, ln] = ov
                lse_ref[pp] = jnp.where(lo, lses[0], lses[1])
                sg, _ = _silu_parts(g_ref[:, ln])
                u_ref[:, ln] = (ov * sg).astype(BF16)

        @pl.when(b < left)
        def _():
            step(True)

        @pl.when(b >= left)
        def _():
            step(False)

    tile = pl.BlockSpec((TQ, pw), lambda p, b: (b, p))
    return _call(
        body, name="attn_a_fwd", grid=(HEADS // 2 // pairs, nb),
        in_specs=_a_qkv_specs(pad + s, pad, pw) + [
            tile, pl.BlockSpec((2 * pairs, 1, wide), lambda p, b: (p, 0, 0))],
        out_specs=[tile, tile, pl.BlockSpec((pairs, TQ, LANES), lambda p, b: (p, b, 0))],
        out_shape=[jax.ShapeDtypeStruct((s, D_MODEL), F32), jax.ShapeDtypeStruct((s, D_MODEL), BF16),
                   jax.ShapeDtypeStruct((HEADS // 2, s, LANES), F32)],
        scratch_shapes=[pltpu.VMEM((pairs, TQ, 2 * A_WIN), F32)],
        sem=("parallel", "arbitrary"), hosted=hosted,
        args=(zqkv, zqkv, zqkv, g, diag))


def _attn_a_bwd(zqkv, g, o, du, lse, diag, hosted=None):
    s = g.shape[0]
    pad = zqkv.shape[1] - s
    nb = s // TQ
    left = A_KBLOCKS - 1
    pw = A_PAIRS * LANES
    wide = A_WIN + TQ

    def body(q_ref, k_ref, v_ref, g_ref, o_ref, du_ref, lse_ref, diag_ref, dz_ref, ddiag_ref,
             bias_scr, dbias_acc, dk_acc, dv_acc):
        b = pl.program_id(1)

        @pl.when(b == 0)
        def _():
            for hh in range(2 * A_PAIRS):
                bias_scr[hh // 2, :, (hh % 2) * A_WIN:(hh % 2 + 1) * A_WIN] = _toeplitz_tile(
                    diag_ref[hh], A_WIN, A_LEFT_CHUNKS)
            dbias_acc[...] = jnp.zeros_like(dbias_acc)
            dk_acc[...] = jnp.zeros_like(dk_acc)
            dv_acc[...] = jnp.zeros_like(dv_acc)

        def step(first_blocks):
            lo = _lane_lo(TQ)
            rows = pl.ds(pl.multiple_of(b * TQ, TQ), TQ)
            sg, dsg = _silu_parts(g_ref[...])
            duv = du_ref[...]
            ov = o_ref[...]
            do = duv * sg
            dz_ref[3, rows, :] = (duv * ov * dsg).astype(BF16)
            do_o = do * ov
            do_bf = do.astype(BF16)
            for pp in range(A_PAIRS):
                ln = slice(pp * LANES, (pp + 1) * LANES)
                q = q_ref[:, ln] * SCALE
                kcat = _stack_pair(_window(k_ref, b, pad, A_WIN, ln))
                vcat = _stack_pair(_window(v_ref, b, pad, A_WIN, ln))
                sc = lax.dot_general(q, kcat, NT, preferred_element_type=F32) + bias_scr[pp]
                if first_blocks:
                    sc = jnp.where(_block_valid(b, left, A_WIN), sc, MASKED)
                lse_t = lse_ref[pp]
                dp = lax.dot_general(do_bf[:, ln], vcat, NT, preferred_element_type=F32)
                p, ds = _softmax_pair_bwd(sc, dp, (lse_t[:, 0:1], lse_t[:, HEAD_DIM:HEAD_DIM + 1]),
                                          _pair_rowsums(do_o[:, ln], lo), A_WIN)
                dbias_acc[pp] += ds
                dsb = ds.astype(BF16)
                dz_ref[0, rows, ln] = (jnp.dot(dsb, kcat, preferred_element_type=F32) * SCALE).astype(BF16)
                pb = p.astype(BF16)
                dob = do_bf[:, ln]
                dkt = jnp.concatenate([
                    lax.dot_general(q[:, e * HEAD_DIM:(e + 1) * HEAD_DIM], dsb[:, e * A_WIN:(e + 1) * A_WIN], TN,
                                    preferred_element_type=F32) for e in range(2)], axis=0)
                dvt = jnp.concatenate([
                    lax.dot_general(dob[:, e * HEAD_DIM:(e + 1) * HEAD_DIM], pb[:, e * A_WIN:(e + 1) * A_WIN], TN,
                                    preferred_element_type=F32) for e in range(2)], axis=0)
                for t in range(A_KBLOCKS):
                    blk = b + (pad // KB - left + t)
                    dk_acc[blk, ln, :] += dkt[:, t * KB:(t + 1) * KB]
                    dv_acc[blk, ln, :] += dvt[:, t * KB:(t + 1) * KB]

        @pl.when(b < left)
        def _():
            step(True)

        @pl.when(b >= left)
        def _():
            step(False)

        @pl.when(b == nb - 1)
        def _():
            for kb in range(s // KB):
                dz_ref[1, kb * KB:(kb + 1) * KB, :] = dk_acc[pad // KB + kb].T.astype(BF16)
                dz_ref[2, kb * KB:(kb + 1) * KB, :] = dv_acc[pad // KB + kb].T.astype(BF16)
            for hh in range(2 * A_PAIRS):
                ddiag_ref[hh] = _toeplitz_sum(
                    dbias_acc[hh // 2, :, (hh % 2) * A_WIN:(hh % 2 + 1) * A_WIN], A_WIN)

    tile = pl.BlockSpec((TQ, pw), lambda p, b: (b, p))
    diag_spec = pl.BlockSpec((2 * A_PAIRS, 1, wide), lambda p, b: (p, 0, 0))
    return _call(
        body, name="attn_a_bwd", grid=(HEADS // 2 // A_PAIRS, nb),
        in_specs=_a_qkv_specs(pad + s, pad, pw) + [
            tile, tile, tile, pl.BlockSpec((A_PAIRS, TQ, LANES), lambda p, b: (p, b, 0)), diag_spec],
        out_specs=[pl.BlockSpec((4, s, pw), lambda p, b: (0, 0, p)), diag_spec],
        out_shape=[jax.ShapeDtypeStruct((4, s, D_MODEL), BF16),
                   jax.ShapeDtypeStruct((HEADS, 1, wide), F32)],
        scratch_shapes=[pltpu.VMEM((A_PAIRS, TQ, 2 * A_WIN), F32), pltpu.VMEM((A_PAIRS, TQ, 2 * A_WIN), F32),
                        pltpu.VMEM(((pad + s) // KB, pw, KB), F32), pltpu.VMEM(((pad + s) // KB, pw, KB), F32)],
        sem=("parallel", "arbitrary"), hosted=hosted,
        args=(zqkv, zqkv, zqkv, g, o, du, lse, diag))


B_STACK = B_GROUP // 2
B_KVX = 4 * LANES
B_ROWS = B_STACK * TQ
B_WIDE = B_WIN + TQ


def _b_head_place(h):
    return h // B_GROUP, (h % B_GROUP) // 2, h % 2


def _toeplitz_tile_t(base_row, width, left_chunks):
    wide = width + TQ
    rolled = pltpu.roll(jnp.broadcast_to(base_row, (width, wide)), 0, 1, stride=1, stride_axis=0)
    j = lax.broadcasted_iota(jnp.int32, (width, TQ), 0) // CHUNK
    i = lax.broadcasted_iota(jnp.int32, (width, TQ), 1) // CHUNK
    dc = i + left_chunks - j
    return jnp.where((dc >= 0) & (dc <= left_chunks), rolled[:, :TQ], MASKED)


def _toeplitz_sum_t(tile_t, width):
    flip = (lax.broadcasted_iota(jnp.int32, (width, width), 0) + lax.broadcasted_iota(jnp.int32, (width, width), 1)
            == width - 1).astype(F32)
    reversed_rows = jnp.dot(flip, tile_t, precision=lax.Precision.HIGHEST, preferred_element_type=F32)
    padded = jnp.concatenate([reversed_rows, jnp.zeros((width, width), F32)], axis=1)
    rolled = pltpu.roll(padded, 0, 1, stride=1, stride_axis=0)
    return jnp.sum(rolled, axis=0, keepdims=True)


def _b_build_bias(base_ref, bias_scr):
    for h in range(HEADS):
        gi, pr, e = _b_head_place(h)
        bias_scr[gi, e * B_WIN:(e + 1) * B_WIN, pr * TQ:(pr + 1) * TQ] = _toeplitz_tile_t(
            base_ref[h], B_WIN, B_LEFT_CHUNKS)


def _b_stack(x, gi):
    return jnp.concatenate(
        [x[:, (B_STACK * gi + pr) * LANES:(B_STACK * gi + pr + 1) * LANES] for pr in range(B_STACK)], axis=0)


def _b_sink_rows(sink_ref, gi):
    block = lax.broadcasted_iota(jnp.int32, (1, B_ROWS), 1) // TQ
    rows = []
    for e in range(2):
        row = jnp.zeros((1, B_ROWS), F32)
        for pr in range(B_STACK):
            h = B_GROUP * gi + 2 * pr + e
            row = jnp.where(block == pr, sink_ref[0:1, h:h + 1], row)
        rows.append(row)
    return rows


def _b_scores_t(q_ref, kvv, bias_scr, gi, b, left, first_blocks):
    kcat = _stack_pair(kvv[:, gi * LANES:(gi + 1) * LANES])
    vcat = _stack_pair(kvv[:, (B_KV_HEADS + gi) * LANES:(B_KV_HEADS + gi + 1) * LANES])
    qs = _b_stack(q_ref, gi) * SCALE
    sc = lax.dot_general(kcat, qs, NT, preferred_element_type=F32) + bias_scr[gi]
    if first_blocks:
        row = lax.broadcasted_iota(jnp.int32, (2 * B_WIN, 1), 0)
        row = jnp.where(row >= B_WIN, row - B_WIN, row)
        sc = jnp.where((row // KB + (b - left)) >= 0, sc, MASKED)
    return kcat, vcat, qs, sc


def _attn_b_fwd(qb, kvx, gate, base, sinks):
    s = qb.shape[0]
    pad = kvx.shape[0] - s
    nb = s // TQ
    left = B_KBLOCKS - 1

    def body(q_ref, kv_ref, g_ref, base_ref, sink_ref, o_ref, u_ref, lse_ref, bias_scr):
        b = pl.program_id(0)

        @pl.when(b == 0)
        def _():
            _b_build_bias(base_ref, bias_scr)

        def step(first_blocks):
            kvv = _window(kv_ref, b, pad, B_WIN, slice(None))
            upper = lax.broadcasted_iota(jnp.int32, (LANES, B_ROWS), 0) < HEAD_DIM
            lse_rows = []
            for gi in range(B_KV_HEADS):
                kcat, vcat, qs, sc = _b_scores_t(q_ref, kvv, bias_scr, gi, b, left, first_blocks)
                sink = _b_sink_rows(sink_ref, gi)
                ps, inv = [], []
                for e in range(2):
                    sh = sc[e * B_WIN:(e + 1) * B_WIN]
                    m = jnp.maximum(jnp.max(sh, axis=0, keepdims=True), sink[e])
                    ex = jnp.exp(sh - m)
                    l = jnp.sum(ex, axis=0, keepdims=True) + jnp.exp(sink[e] - m)
                    ps.append(ex.astype(BF16))
                    inv.append(1.0 / l)
                    lse_rows.append(m + jnp.log(l))
                pt = jnp.concatenate(ps, axis=0)
                ot = lax.dot_general(vcat, pt, TN, preferred_element_type=F32) * jnp.where(upper, inv[0], inv[1])
                ov = ot.T
                for pr in range(B_STACK):
                    pair = B_STACK * gi + pr
                    o_ref[:, pair * LANES:(pair + 1) * LANES] = ov[pr * TQ:(pr + 1) * TQ]
            lse_ref[0] = jnp.concatenate(lse_rows + [jnp.zeros((8 - len(lse_rows), B_ROWS), F32)], axis=0)
            sg, _ = _silu_parts(g_ref[...])
            u_ref[...] = (o_ref[...] * sg).astype(BF16)

        @pl.when(b < left)
        def _():
            step(True)

        @pl.when(b >= left)
        def _():
            step(False)

    row = pl.BlockSpec((TQ, D_MODEL), lambda b: (b, 0))
    return pl.pallas_call(
        body, name="attn_b_fwd", grid=(nb,),
        in_specs=[row, pl.BlockSpec((pad + s, B_KVX), lambda b: (0, 0)), row,
                  pl.BlockSpec((HEADS, 1, B_WIDE), lambda b: (0, 0, 0)), pl.BlockSpec((1, HEADS), lambda b: (0, 0))],
        out_specs=[row, row, pl.BlockSpec((1, 8, B_ROWS), lambda b: (b, 0, 0))],
        out_shape=[jax.ShapeDtypeStruct((s, D_MODEL), F32), jax.ShapeDtypeStruct((s, D_MODEL), BF16),
                   jax.ShapeDtypeStruct((nb, 8, B_ROWS), F32)],
        scratch_shapes=[pltpu.VMEM((B_KV_HEADS, 2 * B_WIN, B_ROWS), F32)],
        compiler_params=_params(("arbitrary",)),
    )(qb, kvx, gate, base, sinks)


def _attn_b_bwd(qb, kvx, gate, o, du, lse, base, sinks):
    s = qb.shape[0]
    pad = kvx.shape[0] - s
    nb = s // TQ
    left = B_KBLOCKS - 1
    half = D_MODEL // 2

    def body(q_ref, kv_ref, g_ref, o_ref, du_ref, lse_ref, base_ref, sink_ref, dz_ref, dkv_ref, dsum_ref,
             dsink_ref, bias_scr, dbias_acc, dkv_acc, dsink_acc):
        b = pl.program_id(0)

        @pl.when(b == 0)
        def _():
            _b_build_bias(base_ref, bias_scr)
            dbias_acc[...] = jnp.zeros_like(dbias_acc)
            dkv_acc[...] = jnp.zeros_like(dkv_acc)
            dsink_acc[...] = jnp.zeros_like(dsink_acc)

        def step(first_blocks):
            kvv = _window(kv_ref, b, pad, B_WIN, slice(None))
            sg, dsg = _silu_parts(g_ref[...])
            duv = du_ref[...]
            ov = o_ref[...]
            do = duv * sg
            dgate = (duv * ov * dsg).astype(BF16)
            dz_ref[2] = dgate[:, :half]
            dz_ref[3] = dgate[:, half:]
            do_o = do * ov
            do_bf = do.astype(BF16)
            lse_all = lse_ref[0]
            dsink_rows = []
            for gi in range(B_KV_HEADS):
                kcat, vcat, qs, sc = _b_scores_t(q_ref, kvv, bias_scr, gi, b, left, first_blocks)
                dos = _b_stack(do_bf, gi)
                doo_t = _b_stack(do_o, gi).T
                delta = (jnp.sum(doo_t[:HEAD_DIM], axis=0, keepdims=True),
                         jnp.sum(doo_t[HEAD_DIM:], axis=0, keepdims=True))
                sink = _b_sink_rows(sink_ref, gi)
                dp = lax.dot_general(vcat, dos, NT, preferred_element_type=F32)
                ps, dss = [], []
                for e in range(2):
                    lse_e = lse_all[2 * gi + e:2 * gi + e + 1]
                    delta_e = delta[e]
                    p = jnp.exp(sc[e * B_WIN:(e + 1) * B_WIN] - lse_e)
                    ps.append(p.astype(BF16))
                    dss.append(p * (dp[e * B_WIN:(e + 1) * B_WIN] - delta_e))
                    dsink_rows.append(-jnp.exp(sink[e] - lse_e) * delta_e)
                ds = jnp.concatenate(dss, axis=0)
                dbias_acc[gi] += ds
                dsb = ds.astype(BF16)
                dq = (lax.dot_general(kcat, dsb, TN, preferred_element_type=F32) * SCALE).T.astype(BF16)
                for pr in range(B_STACK):
                    dz_ref[gi, :, pr * LANES:(pr + 1) * LANES] = dq[pr * TQ:(pr + 1) * TQ]
                dk = _unstack_pair(jnp.dot(dsb, qs, preferred_element_type=F32), B_WIN)
                dv = _unstack_pair(jnp.dot(jnp.concatenate(ps, axis=0), dos, preferred_element_type=F32), B_WIN)
                krows = pl.ds(pl.multiple_of(b * TQ + pad - (B_WIN - TQ), KB), B_WIN)
                dkv_acc[krows, gi * LANES:(gi + 1) * LANES] += dk
                dkv_acc[krows, (B_KV_HEADS + gi) * LANES:(B_KV_HEADS + gi + 1) * LANES] += dv
            dsink_acc[...] += jnp.concatenate(
                dsink_rows + [jnp.zeros((8 - len(dsink_rows), B_ROWS), F32)], axis=0)

        @pl.when(b < left)
        def _():
            step(True)

        @pl.when(b >= left)
        def _():
            step(False)

        @pl.when(b == nb - 1)
        def _():
            lo_s = _lane_lo(s)
            for which in range(2):
                folded = []
                for gi in range(B_KV_HEADS):
                    part = dkv_acc[pad:pad + s, (which * B_KV_HEADS + gi) * LANES:(which * B_KV_HEADS + gi + 1) * LANES]
                    folded.append(part + pltpu.roll(part, HEAD_DIM, 1))
                dkv_ref[:, which * LANES:(which + 1) * LANES] = jnp.where(lo_s, folded[0], folded[1]).astype(BF16)
            lane8 = lax.broadcasted_iota(jnp.int32, dsink_ref.shape, 1)
            tot = jnp.zeros(dsink_ref.shape, F32)
            for h in range(HEADS):
                gi, pr, e = _b_head_place(h)
                dsum_ref[h] = _toeplitz_sum_t(
                    dbias_acc[gi, e * B_WIN:(e + 1) * B_WIN, pr * TQ:(pr + 1) * TQ], B_WIN)
                per_query = dsink_acc[2 * gi + e:2 * gi + e + 1, pr * TQ:(pr + 1) * TQ]
                tot = jnp.where(lane8 == h, jnp.sum(per_query, axis=1, keepdims=True), tot)
            dsink_ref[...] = tot

    row = pl.BlockSpec((TQ, D_MODEL), lambda b: (b, 0))
    base_spec = pl.BlockSpec((HEADS, 1, B_WIDE), lambda b: (0, 0, 0))
    return pl.pallas_call(
        body, name="attn_b_bwd", grid=(nb,),
        in_specs=[row, pl.BlockSpec((pad + s, B_KVX), lambda b: (0, 0)), row, row, row,
                  pl.BlockSpec((1, 8, B_ROWS), lambda b: (b, 0, 0)), base_spec,
                  pl.BlockSpec((1, HEADS), lambda b: (0, 0))],
        out_specs=[pl.BlockSpec((4, TQ, half), lambda b: (0, b, 0)),
                   pl.BlockSpec((s, 2 * LANES), lambda b: (0, 0)), base_spec,
                   pl.BlockSpec((8, LANES), lambda b: (0, 0))],
        out_shape=[jax.ShapeDtypeStruct((4, s, half), BF16), jax.ShapeDtypeStruct((s, 2 * LANES), BF16),
                   jax.ShapeDtypeStruct((HEADS, 1, B_WIDE), F32), jax.ShapeDtypeStruct((8, LANES), F32)],
        scratch_shapes=[pltpu.VMEM((B_KV_HEADS, 2 * B_WIN, B_ROWS), F32),
                        pltpu.VMEM((B_KV_HEADS, 2 * B_WIN, B_ROWS), F32),
                        pltpu.VMEM((pad + s, B_KVX), F32), pltpu.VMEM((8, B_ROWS), F32)],
        compiler_params=_params(("arbitrary",)),
    )(qb, kvx, gate, o, du, lse, base, sinks)


def _t5_bucket(rel):
    nb = T5_BUCKETS // 2
    max_exact = nb // 2
    ret = jnp.where(rel > 0, nb, 0)
    n = jnp.abs(rel)
    nf = jnp.maximum(n, 1).astype(jnp.float32)
    large = max_exact + (jnp.log(nf / max_exact) / math.log(T5_MAX_DIST / max_exact)
                         * (nb - max_exact)).astype(jnp.int32)
    large = jnp.minimum(large, nb - 1)
    return ret + jnp.where(n < max_exact, n, large)


def _a_offset_onehot():
    c = np.arange(A_WIN + TQ)
    dist = A_LEFT_CHUNKS * CHUNK + TQ - 1 - c
    idx = np.clip(dist, -A_REL_CLIP, A_REL_CLIP) + A_REL_CLIP
    onehot = np.zeros((A_WIN + TQ, 2 * A_REL_CLIP + 1), np.float32)
    onehot[c, idx] = 1.0
    return jnp.asarray(onehot)


def _b_offset_onehot():
    c = jnp.arange(B_WIN + TQ, dtype=jnp.int32)
    rel = c - (TQ - 1) - B_LEFT_CHUNKS * CHUNK
    return (_t5_bucket(rel)[:, None] == jnp.arange(T5_BUCKETS)[None, :]).astype(F32)


def _diag_rows(onehot, table):
    rows = jnp.dot(onehot, table.astype(F32), precision=lax.Precision.HIGHEST)
    return rows.T.reshape(HEADS, 1, onehot.shape[0])


def _diag_rows_grad(onehot, ddiag):
    return jnp.dot(ddiag.reshape(HEADS, onehot.shape[0]), onehot, precision=lax.Precision.HIGHEST)


def _position():
    x, y, c = lax.axis_index("x"), lax.axis_index("y"), lax.axis_index("c")
    chips = [(1 - x, y), (x, 1 - y), (1 - x, 1 - y)]
    return x, y, c, chips


ANY = pl.BlockSpec(memory_space=pl.ANY)


def _allgather_routed(shards):
    n = len(shards)

    def piece(block_ref, t, c, quarter=None):
        half = shards[t].shape[0] // 2
        if quarter is None:
            return block_ref.at[pl.ds(c * half, half)]
        return block_ref.at[pl.ds(c * half + quarter * (half // 2), half // 2)]

    def copies(kind, ins, outs, sems):
        ici_send, ici_recv, pass_send, pass_recv, local_sems = sems
        x, y, c, chips = _position()
        mine = 2 * x + y
        if kind == "local":
            return [pltpu.make_async_copy(ins[t], outs[t].at[mine], local_sems.at[t]) for t in range(n)]
        ids = [2 * chip[0] + chip[1] for chip in chips]
        made = []
        for t in range(n):
            def ici(k, to):
                return dict(send_sem=ici_send.at[4 * t + k], recv_sem=ici_recv.at[4 * t + k],
                            device_id=(chips[to][0], chips[to][1], c), device_id_type=MESH)

            def d2d(k):
                return dict(send_sem=pass_send.at[4 * t + k], recv_sem=pass_recv.at[4 * t + k],
                            device_id=(x, y, 1 - c), device_id_type=MESH)

            def same(ref, where):
                return pltpu.make_async_remote_copy(src_ref=ref, dst_ref=ref, **where)

            if kind == "send":
                for k in range(2):
                    made.append(pltpu.make_async_remote_copy(
                        src_ref=piece(ins[t], t, c), dst_ref=piece(outs[t].at[mine], t, c), **ici(k, k)))
            elif kind == "landed":
                made += [same(piece(outs[t].at[ids[k]], t, c), ici(k, k)) for k in range(2)]
            elif kind == "forward":
                made.append(same(piece(outs[t].at[ids[0]], t, c, 0), ici(2, 1)))
                made.append(same(piece(outs[t].at[ids[1]], t, c, 1), ici(3, 0)))
            elif kind == "arrived":
                made.append(same(piece(outs[t].at[ids[2]], t, c, 0), ici(2, 1)))
                made.append(same(piece(outs[t].at[ids[2]], t, c, 1), ici(3, 0)))
            else:
                core = 1 - c if kind == "passed" else c
                if kind in ("pass halves", "passed"):
                    made += [same(piece(outs[t].at[ids[k]], t, core), d2d(k)) for k in range(2)]
                if kind in ("pass quarters", "passed"):
                    made += [same(piece(outs[t].at[ids[2]], t, core, k), d2d(2 + k)) for k in range(2)]
        return made

    def first(ins, outs, sems):
        for cp in copies("local", ins, outs, sems) + copies("send", ins, outs, sems):
            cp.start()

    def middle(ins, outs, sems):
        for got, onward, near in zip(copies("landed", ins, outs, sems), copies("forward", ins, outs, sems),
                                     copies("pass halves", ins, outs, sems)):
            got.wait_recv()
            near.start()
            onward.start()

    def last(ins, outs, sems):
        quarters = copies("pass quarters", ins, outs, sems)
        for got, near in zip(copies("arrived", ins, outs, sems), quarters):
            got.wait_recv()
            near.start()
        for cp in copies("passed", ins, outs, sems):
            cp.wait_recv()
        for cp in (copies("send", ins, outs, sems) + copies("forward", ins, outs, sems)
                   + copies("pass halves", ins, outs, sems) + quarters):
            cp.wait_send()
        for cp in copies("local", ins, outs, sems):
            cp.wait()

    return _Hosted(shards, [jax.ShapeDtypeStruct((4,) + w.shape, w.dtype) for w in shards],
                   [pltpu.SemaphoreType.DMA((4 * n,))] * 4 + [pltpu.SemaphoreType.DMA((n,))],
                   first, middle, last)


def _scatter_hosted(grads):
    n = len(grads)

    def copies(ins, outs, sems):
        send_sems, recv_sems = sems
        x, y, c, chips = _position()
        return [pltpu.make_async_remote_copy(
            src_ref=ins[t].at[2 * chip[0] + chip[1]], dst_ref=outs[t].at[j],
            send_sem=send_sems.at[3 * t + j], recv_sem=recv_sems.at[3 * t + j],
            device_id=(chip[0], chip[1], c), device_id_type=MESH)
            for t in range(n) for j, chip in enumerate(chips)]

    def first(ins, outs, sems):
        for cp in copies(ins, outs, sems):
            cp.start()

    def last(ins, outs, sems):
        for cp in copies(ins, outs, sems):
            cp.wait()

    return _Hosted(grads, [jax.ShapeDtypeStruct((3,) + g.shape[1:], g.dtype) for g in grads],
                   [pltpu.SemaphoreType.DMA((3 * n,))] * 2, first, None, last)


GATHER_PEERS = "x and y neighbours (same core) and the sibling core"
SCATTER_PEERS = "the same core of the three other chips"
EVERYONE = "the seven other devices"


def _run_on_sequencer(name, hosted, peers, collective_id):
    ins = [jax.new_ref(a, memory_space=pltpu.MemorySpace.HBM) for a in hosted.inputs]
    outs = [jax.empty_ref(shape, memory_space=pltpu.MemorySpace.HBM) for shape in hosted.out_shapes]

    @pl.kernel(mesh=plsc.ScalarSubcoreMesh(axis_name="sequencer", num_cores=1), name=name,
               scratch_types=tuple(hosted.sems), compiler_params=pltpu.CompilerParams(collective_id=collective_id))
    def launch(*sems):
        x, y, c, chips = _position()
        if peers == GATHER_PEERS:
            devices = [(chip[0], chip[1], c) for chip in chips[:2]] + [(x, y, 1 - c)]
        elif peers == SCATTER_PEERS:
            devices = [(chip[0], chip[1], c) for chip in chips]
        else:
            devices = [(x ^ (k >> 2), y ^ ((k >> 1) & 1), c ^ (k & 1)) for k in range(1, 8)]
        barrier = pltpu.get_barrier_semaphore()
        for device in devices:
            pl.semaphore_signal(barrier, inc=1, device_id=device, device_id_type=MESH)
        pl.semaphore_wait(barrier, len(devices))
        hosted.first(ins, outs, sems)
        if hosted.middle is not None:
            hosted.middle(ins, outs, sems)
        hosted.last(ins, outs, sems)

    launch()
    return [o[...] for o in outs]


def _gather_gain(shard):
    def body(in_ref, out_ref, send_sems, recv_sems):
        x, y, c, chips = _position()
        out_ref[2 * x + y] = in_ref[...]
        sends = [pltpu.make_async_remote_copy(
            src_ref=in_ref, dst_ref=out_ref.at[2 * x + y], send_sem=send_sems.at[j], recv_sem=recv_sems.at[j],
            device_id=(chip[0], chip[1], c), device_id_type=MESH) for j, chip in enumerate(chips)]
        for cp in sends:
            cp.start()
        for j, chip in enumerate(chips):
            pltpu.make_async_remote_copy(
                src_ref=in_ref, dst_ref=out_ref.at[2 * chip[0] + chip[1]], send_sem=send_sems.at[j],
                recv_sem=recv_sems.at[j], device_id=(chip[0], chip[1], c), device_id_type=MESH).wait_recv()
        for cp in sends:
            cp.wait_send()

    vmem = pl.BlockSpec(memory_space=pltpu.VMEM)
    return pl.pallas_call(
        body, name="gather_gain", in_specs=[vmem], out_specs=vmem,
        out_shape=jax.ShapeDtypeStruct((4,) + shard.shape, shard.dtype),
        scratch_shapes=[pltpu.SemaphoreType.DMA((3,))] * 2,
    )(shard)


def _swap_with_sibling(name, blocks):
    n = len(blocks)

    def body(*refs):
        ins, outs = refs[:n], refs[n:2 * n]
        send_sems, recv_sems = refs[2 * n:]
        x, y, c, _ = _position()
        sends = [pltpu.make_async_remote_copy(
            src_ref=ins[t], dst_ref=outs[t], send_sem=send_sems.at[t], recv_sem=recv_sems.at[t],
            device_id=(x, y, 1 - c), device_id_type=MESH) for t in range(n)]
        for cp in sends:
            cp.start()
        for cp in sends:
            cp.wait()

    return pl.pallas_call(
        body, name=name,
        in_specs=[ANY] * n, out_specs=[ANY] * n,
        out_shape=[jax.ShapeDtypeStruct(b.shape, b.dtype) for b in blocks],
        scratch_shapes=[pltpu.SemaphoreType.DMA((n,))] * 2,
    )(*blocks)


def _everyone_hosted(terms):
    nt = len(terms)

    def copies(kind, ins, outs, sems):
        send_sems, recv_sems, local_sems = sems
        x, y, c, _ = _position()
        me = 4 * x + 2 * y + c
        if kind == "local":
            return [pltpu.make_async_copy(ins[t], outs[t].at[me], local_sems.at[t]) for t in range(nt)]
        made = []
        for t in range(nt):
            for k in range(1, 8):
                peer = (x ^ (k >> 2), y ^ ((k >> 1) & 1), c ^ (k & 1))
                slot = me if kind == "send" else me ^ k
                made.append(pltpu.make_async_remote_copy(
                    src_ref=ins[t], dst_ref=outs[t].at[slot], send_sem=send_sems.at[7 * t + k - 1],
                    recv_sem=recv_sems.at[7 * t + k - 1], device_id=peer, device_id_type=MESH))
        return made

    def first(ins, outs, sems):
        for cp in copies("local", ins, outs, sems) + copies("send", ins, outs, sems):
            cp.start()

    def last(ins, outs, sems):
        for cp in copies("landed", ins, outs, sems):
            cp.wait_recv()
        for cp in copies("send", ins, outs, sems):
            cp.wait_send()
        for cp in copies("local", ins, outs, sems):
            cp.wait()

    return _Hosted(terms, [jax.ShapeDtypeStruct((8,) + a.shape, F32) for a in terms],
                   [pltpu.SemaphoreType.DMA((7 * nt,))] * 2 + [pltpu.SemaphoreType.DMA((nt,))], first, None, last)


def _small_step(partials, extras, ws, ms, vs, shard_of):
    n = len(partials)
    terms = list(partials) + list(extras)
    nt = len(terms)
    rows = [t for t in range(nt) if terms[t].shape[0] == 1]
    mats = [t for t in range(nt) if terms[t].shape[0] != 1]
    row_block = (8, max(terms[t].shape[1] for t in rows))
    assert len(rows) <= row_block[0]
    vmem = pl.BlockSpec(memory_space=pltpu.VMEM)

    def pack(*refs):
        packed = refs[-1]
        packed[...] = jnp.zeros_like(packed)
        for i, t in enumerate(rows):
            packed[i:i + 1, 0:terms[t].shape[1]] = refs[i][...]

    packed = pl.pallas_call(pack, name="small_pack", in_specs=[vmem] * len(rows), out_specs=vmem,
                            out_shape=jax.ShapeDtypeStruct(row_block, F32))(*[terms[t] for t in rows])
    slots = _run_on_sequencer("allgather_small", _everyone_hosted([packed] + [terms[t] for t in mats]),
                              EVERYONE, 2)

    def body(*refs):
        slot_refs, refs = refs[:len(slots)], refs[len(slots):]
        w_refs, refs = refs[:n], refs[n:]
        m_refs, refs = refs[:n], refs[n:]
        v_refs, outs = refs[:n], refs[n:]
        sums = []
        for ref in slot_refs:
            g = ref[0]
            for dev in range(1, 8):
                g = g + ref[dev]
            sums.append(g)
        chip = 2 * lax.axis_index("x") + lax.axis_index("y")
        for t in range(nt):
            if t in rows:
                i = rows.index(t)
                g = sums[0][i:i + 1, 0:terms[t].shape[1]]
            else:
                g = sums[1 + mats.index(t)]
            if t >= n:
                outs[4 * n + t - n][...] = g
                continue
            if shard_of[t]:
                width = ws[t].shape[-1]
                mine = jnp.zeros(ws[t].shape, F32)
                for s in range(4):
                    mine = jnp.where(chip == s, g[:, s * width:(s + 1) * width], mine)
                g = mine
            delta, mn, vn = _adamw_math(w_refs[t][...], g, m_refs[t][...], v_refs[t][...])
            outs[4 * t][...] = g
            outs[4 * t + 1][...] = delta
            outs[4 * t + 2][...] = mn
            outs[4 * t + 3][...] = vn

    out_shapes = []
    for t in range(n):
        out_shapes += [jax.ShapeDtypeStruct(ws[t].shape, F32)] * 4
    out_shapes += [jax.ShapeDtypeStruct(a.shape, F32) for a in extras]
    res = pl.pallas_call(
        body, name="small_step",
        in_specs=[vmem] * (len(slots) + 3 * n), out_specs=[vmem] * len(out_shapes), out_shape=out_shapes,
    )(*slots, *ws, *ms, *vs)
    return [res[4 * t:4 * t + 4] for t in range(n)], res[4 * n:4 * n + nt - n]


def _adamw_math(w, g, m, v):
    m = ADAM_B1 * m + (1.0 - ADAM_B1) * g
    v = ADAM_B2 * v + (1.0 - ADAM_B2) * (g * g)
    m_hat = m / (1.0 - ADAM_B1 ** ADAM_STEP)
    v_hat = v / (1.0 - ADAM_B2 ** ADAM_STEP)
    delta = -ADAM_LR * (m_hat / (jnp.sqrt(v_hat) + ADAM_EPS) + ADAM_WD * w)
    return delta, m, v


def _row_tile(rows):
    return 256 if rows % 256 == 0 else rows


def _sum_partials(name, own, recv, chip, after):
    rows, cols = own.shape[1:]
    tr = _row_tile(rows)

    def body(chip_ref, own_ref, recv_ref, after_ref, o_ref):
        acc = own_ref[...]
        for j in range(3):
            acc = acc + recv_ref[j].astype(F32)
        o_ref[...] = acc

    return pl.pallas_call(
        body, name=name,
        grid_spec=pltpu.PrefetchScalarGridSpec(
            num_scalar_prefetch=1, grid=(rows // tr,),
            in_specs=[pl.BlockSpec((None, tr, cols), lambda i, chip_ref: (chip_ref[0], i, 0)),
                      pl.BlockSpec((3, tr, cols), lambda i, chip_ref: (0, i, 0)), ANY],
            out_specs=pl.BlockSpec((tr, cols), lambda i, chip_ref: (i, 0))),
        out_shape=jax.ShapeDtypeStruct((rows, cols), F32),
        compiler_params=_params(("parallel",)),
    )(chip.reshape(1).astype(jnp.int32), own, recv, after)


def _adamw(name, w, m, v, g_parts):
    rows, cols = w.shape
    tr = _row_tile(rows)
    n = len(g_parts)

    def body(w_ref, m_ref, v_ref, *refs):
        g_refs = refs[:n]
        go_ref, d_ref, mo_ref, vo_ref = refs[n:]
        g = g_refs[0][...]
        for r in g_refs[1:]:
            g = g + r[...]
        delta, mn, vn = _adamw_math(w_ref[...], g, m_ref[...], v_ref[...])
        go_ref[...] = g
        d_ref[...] = delta
        mo_ref[...] = mn
        vo_ref[...] = vn

    spec = pl.BlockSpec((tr, cols), lambda i: (i, 0))
    return pl.pallas_call(
        body, name=name, grid=(rows // tr,),
        in_specs=[spec] * (3 + n), out_specs=[spec] * 4,
        out_shape=[jax.ShapeDtypeStruct((rows, cols), F32)] * 4,
        compiler_params=_params(("parallel",)),
    )(w, m, v, *g_parts)


def _local_step(x, target, ga, wa_in, rel_bias, later_shards, gk, t5, gb, sinks, gf):
    s, d = x.shape
    tm = min(TM_DENSE, s)
    nt = s // tm
    half = d // 2
    row = pl.BlockSpec((tm, d), lambda i: (i, 0))
    whole = lambda shape: pl.BlockSpec(shape, lambda *_: (0,) * len(shape))

    n1, = _norm_fwd("norm_a", x, ga)
    projected = None
    for h, (wa_half, tag) in enumerate(zip(wa_in, ("first", "second"))):
        projected = _proj_a_half("proj_a_" + tag, n1, wa_half, h, projected)
    zqkv, gate_a = projected
    onehot_a = _a_offset_onehot()
    diag_a = _diag_rows(onehot_a, rel_bias)
    (o_a, u_a, lse_a), gathered = _attn_a_fwd(zqkv, gate_a, diag_a, hosted=_allgather_routed(later_shards))
    wa_out, wkv, wb_in, wb_out, wkv_x = gathered
    wa_out = wa_out.reshape(d, d)
    wkv = wkv.reshape(d, -1)
    wkv_x = wkv_x.reshape(d, B_KVX)
    wb_out = wb_out.reshape(d, d)
    h1, nk, n2 = _out_norms("out_a_norms", u_a, wa_out, x, jnp.concatenate([gk, gb], axis=0))
    kvw = wkv.shape[1]
    kvx =_matmul("proj_kv", nk, wkv_x, dims=NN, grid=(nt + 1,), zero_axis=0,
                  a_spec=pl.BlockSpec((tm, d), lambda i: (jnp.maximum(i - 1, 0), 0)), b_spec=whole((d, B_KVX)),
                  o_spec=pl.BlockSpec((tm, B_KVX), lambda i: (i, 0)), out_shape=(tm + s, B_KVX), out_dtype=BF16)
    qb, gate_b = _proj_b(n2, wb_in)
    onehot_b = _b_offset_onehot()
    base_b = jnp.roll(_diag_rows(onehot_b, t5)[..., ::-1], TQ, axis=-1)
    o_b, u_b, lse_b = _attn_b_fwd(qb, kvx, gate_b, base_b, sinks)
    dh2, loss, d_gf, dh2_bf16 = _out_loss_head(u_b, wb_out, h1, target, gf)

    du_b = _matmul("dout_b", dh2_bf16, wb_out, dims=NT, grid=(nt,), a_spec=row, b_spec=whole((d, d)), o_spec=row,
                   out_shape=(s, d), out_dtype=F32)
    d_wb_out = _matmul("dw_out_b", u_b, dh2_bf16, dims=TN, grid=(2,),
                       a_spec=whole((s, d)), b_spec=pl.BlockSpec((s, half), lambda j: (0, j)),
                       o_spec=pl.BlockSpec((d, half), lambda j: (0, j)),
                       out_shape=(d, d), out_dtype=F32, also_bf16=True)
    dz_b, dkv, dsum_b, dsinks = _attn_b_bwd(qb, kvx, gate_b, o_b, du_b, lse_b, base_b, sinks)
    ddiag_b = jnp.roll(dsum_b[..., ::-1], -1, axis=-1)
    d_wb_in = _matmul("dw_in_b", n2, dz_b, dims=TN, grid=(4,),
                      a_spec=whole((s, d)), b_spec=pl.BlockSpec((None, s, half), lambda j: (j, 0, 0)),
                      o_spec=pl.BlockSpec((None, d, half), lambda j: (j, 0, 0)),
                      out_shape=(4, d, half), out_dtype=F32, also_bf16=True)
    d_wkv = _matmul("dw_kv", nk, dkv, dims=TN, grid=(1,),
                    a_spec=whole((s, d)), b_spec=whole((s, kvw)), o_spec=whole((d, kvw)),
                    out_shape=(d, kvw), out_dtype=F32, also_bf16=True)
    dh1, d_gkb, dh1_bf16 = _proj_norm_bwd("dproj_kv_b", h1, dh2, jnp.concatenate([gk, gb], axis=0),
                                          [(dkv[None], [wkv[None]]), (dz_b, [wb_in])], also_bf16=True)

    du_a = _matmul("dout_a", dh1_bf16, wa_out, dims=NT, grid=(nt,), a_spec=row, b_spec=whole((d, d)), o_spec=row,
                   out_shape=(s, d), out_dtype=F32)
    d_wa_out = _matmul("dw_out_a", u_a, dh1_bf16, dims=TN, grid=(2,),
                       a_spec=whole((s, d)), b_spec=pl.BlockSpec((s, half), lambda j: (0, j)),
                       o_spec=pl.BlockSpec((d, half), lambda j: (0, j)),
                       out_shape=(d, d), out_dtype=F32, also_bf16=True)
    early = dict(a_w_out=[g.reshape(4, d // 4, d) for g in d_wa_out],
                 kv_w=[g.reshape(4, d // 4, kvw) for g in d_wkv], b_w_in=list(d_wb_in),
                 b_w_out=[g.reshape(4, d // 4, d) for g in d_wb_out])
    (dz_a, ddiag_a), early_recv = _attn_a_bwd(
        zqkv, gate_a, o_a, du_a, lse_a, diag_a, hosted=_scatter_hosted([early[n][1] for n in early]))
    d_wa_in = _matmul("dw_in_a", n1, dz_a, dims=TN, grid=(4, 2),
                      a_spec=whole((s, d)), b_spec=pl.BlockSpec((None, s, half), lambda j, h: (j, 0, h)),
                      o_spec=pl.BlockSpec((None, d, half), lambda j, h: (j, 0, h)),
                      out_shape=(4, d, d), out_dtype=F32, also_bf16=True)
    late_recv = _run_on_sequencer("scatter_a_w_in", _scatter_hosted([d_wa_in[1]]), SCATTER_PEERS, 0)
    grad_x, d_ga = _proj_norm_bwd("dproj_a", x, dh1, ga, [(dz_a, list(wa_in))])

    small = dict(a_norm=d_ga, kv_norm=d_gkb[0:1], b_norm=d_gkb[1:2], b_sinks=dsinks[0:1, :HEADS], final_norm=d_gf)
    small["by_offset"] = dict(a_rel_bias=(onehot_a, ddiag_a.reshape(HEADS, -1)),
                              t5_bias=(onehot_b, ddiag_b.reshape(HEADS, -1)))
    own = dict(a_w_in=d_wa_in[0], **{n: early[n][0] for n in early})
    received = dict(a_w_in=late_recv[0], **dict(zip(early, early_recv)))
    return loss, grad_x, small, own, received, d_wa_in[1]


SMALL = ("a_norm", "kv_norm", "b_norm", "b_sinks", "final_norm")
TABLES = ("a_rel_bias", "t5_bias")
BIG = ("a_w_in", "a_w_out", "kv_w", "b_w_in", "b_w_out")
ORDER = ("a_norm", "a_w_in", "a_rel_bias", "a_w_out", "kv_norm", "kv_w", "t5_bias", "b_norm", "b_w_in",
         "b_sinks", "b_w_out", "final_norm")


def kernel(x, a_norm, a_w_in, a_rel_bias, a_w_out, kv_norm, kv_w, t5_bias, b_norm, b_w_in, b_sinks, b_w_out, final_norm, loss_target, m_a_norm, m_a_w_in, m_a_rel_bias, m_a_w_out, m_kv_norm, m_kv_w, m_t5_bias, m_b_norm, m_b_w_in, m_b_sinks, m_b_w_out, m_final_norm, v_a_norm, v_a_w_in, v_a_rel_bias, v_a_w_out, v_kv_norm, v_kv_w, v_t5_bias, v_b_norm, v_b_w_in, v_b_sinks, v_b_w_out, v_final_norm):
    w = dict(a_norm=a_norm, a_w_in=a_w_in, a_rel_bias=a_rel_bias, a_w_out=a_w_out, kv_norm=kv_norm, kv_w=kv_w,
             t5_bias=t5_bias, b_norm=b_norm, b_w_in=b_w_in, b_sinks=b_sinks, b_w_out=b_w_out,
             final_norm=final_norm)
    m = dict(a_norm=m_a_norm, a_w_in=m_a_w_in, a_rel_bias=m_a_rel_bias, a_w_out=m_a_w_out, kv_norm=m_kv_norm,
             kv_w=m_kv_w, t5_bias=m_t5_bias, b_norm=m_b_norm, b_w_in=m_b_w_in, b_sinks=m_b_sinks,
             b_w_out=m_b_w_out, final_norm=m_final_norm)
    v = dict(a_norm=v_a_norm, a_w_in=v_a_w_in, a_rel_bias=v_a_rel_bias, a_w_out=v_a_w_out, kv_norm=v_kv_norm,
             kv_w=v_kv_w, t5_bias=v_t5_bias, b_norm=v_b_norm, b_w_in=v_b_w_in, b_sinks=v_b_sinks,
             b_w_out=v_b_w_out, final_norm=v_final_norm)
    d = D_MODEL
    chip = 2 * lax.axis_index("x") + lax.axis_index("y")

    shard2d = dict(a_w_in=a_w_in[0], a_w_out=a_w_out[0], kv_w=kv_w, b_w_in=b_w_in[0], b_w_out=b_w_out[0])

    first = shard2d["a_w_in"].astype(BF16)
    wa_in = [_run_on_sequencer("allgather_" + tag, _allgather_routed([first[:, h * (d // 2):(h + 1) * (d // 2)]]),
                               GATHER_PEERS, collective_id)[0]
             for h, (tag, collective_id) in enumerate((("first", 1), ("second", 3)))]
    ga = _gather_gain(a_norm).reshape(1, d)

    later = [shard2d[n].astype(BF16) for n in BIG[1:]]
    kv_shard = later[BIG[1:].index("kv_w")]
    later.append(jnp.concatenate(
        [kv_shard[:, (i // 2) * HEAD_DIM:(i // 2 + 1) * HEAD_DIM] for i in range(B_KVX // HEAD_DIM)], axis=1))
    loss, grad_x, small, own, received, after_attention = _local_step(
        x[0], loss_target[0], ga, wa_in, a_rel_bias[0], later,
        kv_norm.reshape(1, d), t5_bias, b_norm, b_sinks, final_norm.reshape(1, d))

    out = {}
    as2d = lambda a: a.reshape(-1, a.shape[-1])
    small_res, (loss_sum, *offset_sums) = _small_step(
        [small[n] for n in SMALL], [loss] + [small["by_offset"][n][1] for n in TABLES],
        [as2d(w[n]) for n in SMALL], [as2d(m[n]) for n in SMALL], [as2d(v[n]) for n in SMALL],
        [n == "a_norm" for n in SMALL])
    for n, res in zip(SMALL, small_res):
        out[n] = [r.reshape(w[n].shape) for r in res]
    loss_out = loss_sum.reshape(())
    for n, summed in zip(TABLES, offset_sums):
        grad = _diag_rows_grad(small["by_offset"][n][0], summed)
        res = _adamw("adamw_" + n, as2d(w[n]).T, as2d(m[n]).T, as2d(v[n]).T, [grad])
        out[n] = [r.T.reshape(w[n].shape) for r in res]

    core_sums = [_sum_partials("sum_" + n, own[n], received[n], chip, after_attention) for n in BIG]
    sibling_sums = (_swap_with_sibling("swap_last", core_sums[:1])
                    + _swap_with_sibling("swap_early", core_sums[1:]))

    for n, mine, theirs in zip(BIG, core_sums, sibling_sums):
        res = _adamw("adamw_" + n, shard2d[n], m[n].reshape(shard2d[n].shape), v[n].reshape(shard2d[n].shape),
                     [mine, theirs])
        out[n] = [r.reshape(w[n].shape) for r in res]

    grads = [out[n][0] for n in ORDER]
    deltas = [out[n][1] for n in ORDER]
    new_m = [out[n][2] for n in ORDER]
    new_v = [out[n][3] for n in ORDER]
    return (loss_out, grad_x[None], *grads, *deltas, *new_m, *new_v)
```

```python
import math

import jax
import jax.numpy as jnp
import numpy as np
from jax import lax
from jax.experimental import pallas as pl
from jax.experimental.pallas import tpu as pltpu
from jax.experimental.pallas import tpu_sc as plsc

F32 = jnp.float32
BF16 = jnp.bfloat16
MESH = pl.DeviceIdType.MESH

D_MODEL = 1024
HEADS = 16
HEAD_DIM = 64
CHUNK = 64
RMS_EPS = 1e-6
SCALE = HEAD_DIM ** -0.5
A_LEFT_CHUNKS = 8
A_REL_CLIP = 256
B_LEFT_CHUNKS = 2
B_KV_HEADS = 2
B_GROUP = HEADS // B_KV_HEADS
T5_BUCKETS = 32
T5_MAX_DIST = 128
ADAM_LR = 0.001
ADAM_B1 = 0.9
ADAM_B2 = 0.999
ADAM_EPS = 1e-08
ADAM_WD = 0.01
ADAM_STEP = 10

MASKED = -1e30
LANES = 128
TQ = 128
A_PAIRS = 2
A_PAIRS_FWD = 4
KB = 128
A_KBLOCKS = A_LEFT_CHUNKS * CHUNK // KB + 1
B_KBLOCKS = B_LEFT_CHUNKS * CHUNK // KB + 1
A_WIN = A_KBLOCKS * KB
B_WIN = B_KBLOCKS * KB
TM = 512
TM_DENSE = 1024
TM_HALF = 2048
STREAM_BUFFERS = 3
TM_PARTS = 512
VMEM_LIMIT = 56 * 1024 * 1024

NT = (((1,), (1,)), ((), ()))
TN = (((0,), (0,)), ((), ()))
NN = (((1,), (0,)), ((), ()))


def _params(sem=None):
    return pltpu.CompilerParams(dimension_semantics=sem, vmem_limit_bytes=VMEM_LIMIT)


class _Hosted:
    def __init__(self, inputs, out_shapes, sems, first, middle, last):
        self.inputs, self.out_shapes, self.sems = list(inputs), list(out_shapes), list(sems)
        self.first, self.middle, self.last = first, middle, last


def _call(body, *, name, grid, in_specs, out_specs, out_shape, args, scratch_shapes=(), sem=None, hosted=None,
          aliases=None):
    in_specs, out_specs, out_shape = list(in_specs), list(out_specs), list(out_shape)
    scratch_shapes = list(scratch_shapes)
    if hosted is None:
        out = pl.pallas_call(
            body, name=name, grid=grid, in_specs=in_specs, out_specs=out_specs, out_shape=out_shape,
            scratch_shapes=scratch_shapes, input_output_aliases=aliases or {},
            compiler_params=_params(sem))(*args)
        return list(out), []
    assert aliases is None
    n_in, n_out, n_scr = len(in_specs), len(out_shape), len(scratch_shapes)
    h_in, h_out = len(hosted.inputs), len(hosted.out_shapes)
    total = int(np.prod(grid)) if grid else 1

    def wrapped(*refs):
        ins, refs = refs[:n_in], refs[n_in:]
        h_ins, refs = refs[:h_in], refs[h_in:]
        outs, refs = refs[:n_out], refs[n_out:]
        h_outs, refs = refs[:h_out], refs[h_out:]
        scr, h_sems = refs[:n_scr], refs[n_scr:]
        step = 0
        for axis, size in enumerate(grid):
            step = step * size + pl.program_id(axis)

        if hosted.first is not None:
            @pl.when(step == 0)
            def _():
                hosted.first(h_ins, h_outs, h_sems)

        body(*ins, *outs, *scr)
        if hosted.middle is not None:
            @pl.when(step == total // 2)
            def _():
                hosted.middle(h_ins, h_outs, h_sems)

        if hosted.last is not None:
            @pl.when(step == total - 1)
            def _():
                hosted.last(h_ins, h_outs, h_sems)

    out = pl.pallas_call(
        wrapped, name=name, grid=grid, in_specs=in_specs + [ANY] * h_in, out_specs=out_specs + [ANY] * h_out,
        out_shape=out_shape + hosted.out_shapes, scratch_shapes=scratch_shapes + hosted.sems,
        compiler_params=_params(("arbitrary",) * len(grid)))(*args, *hosted.inputs)
    return list(out[:n_out]), list(out[n_out:])


def _matmul(name, a, b, *, dims, grid, a_spec, b_spec, o_spec, out_shape, out_dtype,
            also_bf16=False, zero_axis=None):
    def body(*refs):
        if zero_axis is None:
            product(*refs)
        else:
            @pl.when(pl.program_id(zero_axis) == 0)
            def _():
                refs[2][...] = jnp.zeros_like(refs[2])

            @pl.when(pl.program_id(zero_axis) > 0)
            def _():
                product(*refs)

    def product(a_ref, b_ref, o_ref, *more):
        prod = lax.dot_general(a_ref[...].astype(BF16), b_ref[...].astype(BF16), dims,
                               preferred_element_type=F32)
        o_ref[...] = prod.astype(out_dtype)
        if also_bf16:
            more[0][...] = prod.astype(BF16)

    out_specs = [o_spec]
    out_shapes = [jax.ShapeDtypeStruct(out_shape, out_dtype)]
    if also_bf16:
        out_specs.append(o_spec)
        out_shapes.append(jax.ShapeDtypeStruct(out_shape, BF16))
    out, _ = _call(body, name=name, grid=grid, in_specs=[a_spec, b_spec], out_specs=out_specs,
                   out_shape=out_shapes, args=[a, b], sem=("parallel",) * len(grid))
    return out[0] if not also_bf16 else tuple(out)


def _proj_a_half(name, n1, w, h, into):
    s, d = n1.shape
    half = w.shape[2]
    ta = min(TM_HALF, s)

    def body(a_ref, w_ref, *refs):
        z_ref, g_ref = refs[-2:]
        i, j = pl.program_id(0), pl.program_id(1)

        @pl.when((i == 0) & (j < 3))
        def _():
            z_ref[...] = jnp.zeros_like(z_ref)

        @pl.when((i > 0) & (j < 3))
        def _():
            z_ref[...] = jnp.dot(a_ref[...], w_ref[...], preferred_element_type=F32).astype(BF16)

        @pl.when((i > 0) & (j == 3))
        def _():
            g_ref[...] = jnp.dot(a_ref[...], w_ref[...], preferred_element_type=F32)

    out, _ = _call(
        body, name=name, grid=(s // ta + 1, 4),
        in_specs=[pl.BlockSpec((ta, d), lambda i, j: (jnp.maximum(i - 1, 0), 0)),
                  pl.BlockSpec((None, d, half), lambda i, j: (j, 0, 0))] + ([] if into is None else [ANY, ANY]),
        out_specs=[pl.BlockSpec((None, ta, half), lambda i, j: (jnp.minimum(j, 2), i, h)),
                   pl.BlockSpec((ta, half), lambda i, j: (jnp.maximum(i - 1, 0), h))],
        out_shape=[jax.ShapeDtypeStruct((3, ta + s, d), BF16), jax.ShapeDtypeStruct((s, d), F32)],
        args=[n1, w] + ([] if into is None else list(into)), sem=("arbitrary", "arbitrary"),
        aliases=None if into is None else {2: 0, 3: 1})
    return out


def _proj_b(n2, w):
    s, d = n2.shape
    half = w.shape[2]
    ta = min(TM_HALF, s)

    def body(a_ref, w_ref, q_ref, g_ref):
        j = pl.program_id(1)

        @pl.when(j < 2)
        def _():
            q_ref[...] = jnp.dot(a_ref[...], w_ref[...], preferred_element_type=F32).astype(BF16)

        @pl.when(j >= 2)
        def _():
            g_ref[...] = jnp.dot(a_ref[...], w_ref[...], preferred_element_type=F32)

    out, _ = _call(
        body, name="proj_b", grid=(s // ta, 4),
        in_specs=[pl.BlockSpec((ta, d), lambda i, j: (i, 0)), pl.BlockSpec((None, d, half), lambda i, j: (j, 0, 0))],
        out_specs=[pl.BlockSpec((ta, half), lambda i, j: (i, jnp.minimum(j, 1))),
                   pl.BlockSpec((ta, half), lambda i, j: (i, jnp.maximum(j - 2, 0)))],
        out_shape=[jax.ShapeDtypeStruct((s, d), BF16), jax.ShapeDtypeStruct((s, d), F32)],
        args=[n2, w], sem=("arbitrary", "arbitrary"))
    return out


def _rms_rows(x):
    return lax.rsqrt(jnp.mean(x * x, axis=-1, keepdims=True) + RMS_EPS)


def _norm_fwd(name, x, gains):
    s, d = x.shape
    n = gains.shape[0]

    def body(x_ref, g_ref, *o_refs):
        xv = x_ref[...]
        xh = xv * _rms_rows(xv)
        for i in range(n):
            o_refs[i][...] = (xh * g_ref[i:i + 1, :]).astype(BF16)

    row = pl.BlockSpec((TM, d), lambda i: (i, 0))
    return pl.pallas_call(
        body, name=name, grid=(s // TM,),
        in_specs=[row, pl.BlockSpec((n, d), lambda i: (0, 0))],
        out_specs=[row] * n,
        out_shape=[jax.ShapeDtypeStruct((s, d), BF16)] * n,
        compiler_params=_params(("parallel",)),
    )(x, gains)


def _proj_norm_bwd(name, x, dres, gains, branches):
    s, d = x.shape
    n = len(branches)
    n_ab = 2 * sum(len(bs) for _, bs in branches)
    tm = min(TM_PARTS, s)

    def body(x_ref, r_ref, g_ref, *refs):
        ab_refs, dx_ref, dg_ref = list(refs[:n_ab]), refs[n_ab], refs[n_ab + 1]
        i = pl.program_id(0)
        xv = x_ref[...]
        r = _rms_rows(xv)
        xh = xv * r

        @pl.when(i == 0)
        def _():
            dg_ref[...] = jnp.zeros_like(dg_ref)

        a = None
        for j in range(n):
            dn = None
            for _ in branches[j][1]:
                a_ref, b_ref = ab_refs.pop(0), ab_refs.pop(0)
                for part in range(a_ref.shape[0]):
                    term = lax.dot_general(a_ref[part], b_ref[part], NT, preferred_element_type=F32)
                    dn = term if dn is None else dn + term
            t = dn * g_ref[j:j + 1, :]
            a = t if a is None else a + t
            dg_ref[j:j + 1, :] += jnp.sum(dn * xh, axis=0, keepdims=True)
        dx_ref[...] = r_ref[...] + r * (a - xh * jnp.mean(xh * a, axis=-1, keepdims=True))

    row = pl.BlockSpec((tm, d), lambda i: (i, 0))
    small = pl.BlockSpec((n, d), lambda i: (0, 0))
    ab_specs, ab_args = [], []
    for a, bs in branches:
        for k, b in enumerate(bs):
            ab_specs += [pl.BlockSpec((a.shape[0], tm, b.shape[2]), lambda i, k=k: (0, i, k)),
                         pl.BlockSpec(b.shape, lambda i: (0, 0, 0))]
            ab_args += [a, b]
    return pl.pallas_call(
        body, name=name, grid=(s // tm,),
        in_specs=[row, row, small] + ab_specs,
        out_specs=[row, small],
        out_shape=[jax.ShapeDtypeStruct((s, d), F32), jax.ShapeDtypeStruct((n, d), F32)],
        compiler_params=_params(("arbitrary",)),
    )(x, dres, gains, *ab_args)


def _out_norms(name, u, w_out, resid, gains):
    s, d = resid.shape
    n = gains.shape[0]
    tm = min(TM_DENSE, s)

    def body(u_ref, w_ref, r_ref, g_ref, h_ref, *o_refs):
        hv = r_ref[...] + jnp.dot(u_ref[...], w_ref[...], preferred_element_type=F32)
        h_ref[...] = hv
        hh = hv * _rms_rows(hv)
        for i in range(n):
            o_refs[i][...] = (hh * g_ref[i:i + 1, :]).astype(BF16)

    row = pl.BlockSpec((tm, d), lambda i: (i, 0))
    return pl.pallas_call(
        body, name=name, grid=(s // tm,),
        in_specs=[row, pl.BlockSpec((d, d), lambda i: (0, 0)), row, pl.BlockSpec((n, d), lambda i: (0, 0))],
        out_specs=[row] * (n + 1),
        out_shape=[jax.ShapeDtypeStruct((s, d), F32)] + [jax.ShapeDtypeStruct((s, d), BF16)] * n,
        compiler_params=_params(("parallel",)),
    )(u, w_out, resid, gains)


def _out_loss_head(u, w_out, resid, target, gain):
    s, d = resid.shape
    tm = min(TM_PARTS, s)

    def body(u_ref, w_ref, r_ref, t_ref, g_ref, dh_ref, loss_ref, dg_ref):
        i = pl.program_id(0)
        hv = r_ref[...] + jnp.dot(u_ref[...], w_ref[...], preferred_element_type=F32)
        r = _rms_rows(hv)
        hh = hv * r
        g = g_ref[...]
        err = hh * g - t_ref[...]
        part = 0.5 * jnp.sum(jnp.sum(err * err, axis=-1, keepdims=True) * (1.0 / d), axis=0, keepdims=True)
        dy = err * (1.0 / d)
        a = dy * g
        dh_ref[...] = r * (a - hh * jnp.mean(hh * a, axis=-1, keepdims=True))
        dg = jnp.sum(dy * hh, axis=0, keepdims=True)

        @pl.when(i == 0)
        def _():
            loss_ref[...] = part
            dg_ref[...] = dg

        @pl.when(i > 0)
        def _():
            loss_ref[...] += part
            dg_ref[...] += dg

    row = pl.BlockSpec((tm, d), lambda i: (i, 0))
    return pl.pallas_call(
        body, name="out_b_loss_head", grid=(s // tm,),
        in_specs=[row, pl.BlockSpec((d, d), lambda i: (0, 0)), row, row, pl.BlockSpec((1, d), lambda i: (0, 0))],
        out_specs=[row, pl.BlockSpec((1, 1), lambda i: (0, 0)), pl.BlockSpec((1, d), lambda i: (0, 0))],
        out_shape=[jax.ShapeDtypeStruct((s, d), F32), jax.ShapeDtypeStruct((1, 1), F32),
                   jax.ShapeDtypeStruct((1, d), F32)],
        compiler_params=_params(("arbitrary",)),
    )(u, w_out, resid, target, gain)


def _silu_parts(g):
    sig = jax.nn.sigmoid(g)
    return g * sig, sig * (1.0 + g * (1.0 - sig))


def _lane_lo(rows):
    return lax.broadcasted_iota(jnp.int32, (rows, LANES), 1) < HEAD_DIM


def _stack_pair(x):
    lo = _lane_lo(x.shape[0])
    zero = jnp.zeros_like(x)
    return jnp.concatenate([jnp.where(lo, x, zero), jnp.where(lo, zero, x)], axis=0)


def _unstack_pair(y, w):
    return jnp.where(_lane_lo(w), y[:w], y[w:])


def _block_valid(b, left_blocks, width):
    col = lax.broadcasted_iota(jnp.int32, (1, 2 * width), 1)
    col = jnp.where(col >= width, col - width, col)
    return (col // KB + (b - left_blocks)) >= 0


def _toeplitz_tile(diag_row, width, left_chunks):
    wide = width + TQ
    rolled = pltpu.roll(jnp.broadcast_to(diag_row, (TQ, wide)), 1, 1, stride=1, stride_axis=0)
    i = lax.broadcasted_iota(jnp.int32, (TQ, width), 0) // CHUNK
    j = lax.broadcasted_iota(jnp.int32, (TQ, width), 1) // CHUNK
    dc = i + left_chunks - j
    return jnp.where((dc >= 0) & (dc <= left_chunks), rolled[:, TQ:], MASKED)


def _toeplitz_sum(tile, width):
    flip = (lax.broadcasted_iota(jnp.int32, (TQ, TQ), 0) + lax.broadcasted_iota(jnp.int32, (TQ, TQ), 1)
            == TQ - 1).astype(F32)
    reversed_rows = jnp.dot(flip, tile, precision=lax.Precision.HIGHEST, preferred_element_type=F32)
    padded = jnp.concatenate([reversed_rows, jnp.zeros((TQ, TQ), F32)], axis=1)
    rolled = pltpu.roll(padded, 0, 1, stride=1, stride_axis=0)
    return jnp.sum(rolled, axis=0, keepdims=True)


def _softmax_pair(sc, w, sink=None):
    ps, inv, lses = [], [], []
    for e in range(2):
        sh = sc[:, e * w:(e + 1) * w]
        m = jnp.max(sh, axis=-1, keepdims=True)
        if sink is not None:
            m = jnp.maximum(m, sink[e])
        ex = jnp.exp(sh - m)
        l = jnp.sum(ex, axis=-1, keepdims=True)
        if sink is not None:
            l = l + jnp.exp(sink[e] - m)
        ps.append(ex.astype(BF16))
        inv.append(1.0 / l)
        lses.append(m + jnp.log(l))
    return jnp.concatenate(ps, axis=-1), inv, lses


def _softmax_pair_bwd(sc, dp, lse, delta, w):
    ps, dss = [], []
    for e in range(2):
        p = jnp.exp(sc[:, e * w:(e + 1) * w] - lse[e])
        ps.append(p)
        dss.append(p * (dp[:, e * w:(e + 1) * w] - delta[e]))
    return jnp.concatenate(ps, axis=-1), jnp.concatenate(dss, axis=-1)


def _pair_rowsums(x, lo):
    zero = jnp.zeros_like(x)
    return (jnp.sum(jnp.where(lo, x, zero), axis=-1, keepdims=True),
            jnp.sum(jnp.where(lo, zero, x), axis=-1, keepdims=True))


def _a_qkv_specs(rows, pad, pw):
    return [pl.BlockSpec((None, TQ, pw), lambda p, b: (0, b + pad // TQ, p)),
            pl.BlockSpec((None, rows, pw), lambda p, b: (1, 0, p)),
            pl.BlockSpec((None, rows, pw), lambda p, b: (2, 0, p))]


def _window(ref, b, pad, win, lanes):
    start = pl.multiple_of(b * TQ + pad - (win - TQ), KB)
    return ref[pl.ds(start, win), lanes]


def _attn_a_fwd(zqkv, g, diag, hosted=None):
    s = g.shape[0]
    pad = zqkv.shape[1] - s
    nb = s // TQ
    left = A_KBLOCKS - 1
    pairs = A_PAIRS_FWD
    pw = pairs * LANES
    wide = A_WIN + TQ

    def body(q_ref, k_ref, v_ref, g_ref, diag_ref, o_ref, u_ref, lse_ref, bias_scr):
        b = pl.program_id(1)

        @pl.when(b == 0)
        def _():
            for hh in range(2 * pairs):
                bias_scr[hh // 2, :, (hh % 2) * A_WIN:(hh % 2 + 1) * A_WIN] = _toeplitz_tile(
                    diag_ref[hh], A_WIN, A_LEFT_CHUNKS)

        def step(first_blocks):
            lo = _lane_lo(TQ)
            for pp in range(pairs):
                ln = slice(pp * LANES, (pp + 1) * LANES)
                kcat = _stack_pair(_window(k_ref, b, pad, A_WIN, ln))
                vcat = _stack_pair(_window(v_ref, b, pad, A_WIN, ln))
                sc = lax.dot_general(q_ref[:, ln] * SCALE, kcat, NT, preferred_element_type=F32) + bias_scr[pp]
                if first_blocks:
                    sc = jnp.where(_block_valid(b, left, A_WIN), sc, MASKED)
                p, inv, lses = _softmax_pair(sc, A_WIN)
                ov = jnp.dot(p, vcat, preferred_element_type=F32) * jnp.where(lo, inv[0], inv[1])
                o_ref[:, ln] = ov
                lse_ref[pp] = jnp.where(lo, lses[0], lses[1])
                sg, _ = _silu_parts(g_ref[:, ln])
                u_ref[:, ln] = (ov * sg).astype(BF16)

        @pl.when(b < left)
        def _():
            step(True)

        @pl.when(b >= left)
        def _():
            step(False)

    tile = pl.BlockSpec((TQ, pw), lambda p, b: (b, p))
    return _call(
        body, name="attn_a_fwd", grid=(HEADS // 2 // pairs, nb),
        in_specs=_a_qkv_specs(pad + s, pad, pw) + [
            tile, pl.BlockSpec((2 * pairs, 1, wide), lambda p, b: (p, 0, 0))],
        out_specs=[tile, tile, pl.BlockSpec((pairs, TQ, LANES), lambda p, b: (p, b, 0))],
        out_shape=[jax.ShapeDtypeStruct((s, D_MODEL), F32), jax.ShapeDtypeStruct((s, D_MODEL), BF16),
                   jax.ShapeDtypeStruct((HEADS // 2, s, LANES), F32)],
        scratch_shapes=[pltpu.VMEM((pairs, TQ, 2 * A_WIN), F32)],
        sem=("parallel", "arbitrary"), hosted=hosted,
        args=(zqkv, zqkv, zqkv, g, diag))


def _attn_a_bwd(zqkv, g, o, du, lse, diag, hosted=None):
    s = g.shape[0]
    pad = zqkv.shape[1] - s
    nb = s // TQ
    left = A_KBLOCKS - 1
    pw = A_PAIRS * LANES
    wide = A_WIN + TQ

    def body(q_ref, k_ref, v_ref, g_ref, o_ref, du_ref, lse_ref, diag_ref, dz_ref, ddiag_ref,
             bias_scr, dbias_acc, dk_acc, dv_acc):
        b = pl.program_id(1)

        @pl.when(b == 0)
        def _():
            for hh in range(2 * A_PAIRS):
                bias_scr[hh // 2, :, (hh % 2) * A_WIN:(hh % 2 + 1) * A_WIN] = _toeplitz_tile(
                    diag_ref[hh], A_WIN, A_LEFT_CHUNKS)
            dbias_acc[...] = jnp.zeros_like(dbias_acc)
            dk_acc[...] = jnp.zeros_like(dk_acc)
            dv_acc[...] = jnp.zeros_like(dv_acc)

        def step(first_blocks):
            lo = _lane_lo(TQ)
            rows = pl.ds(pl.multiple_of(b * TQ, TQ), TQ)
            sg, dsg = _silu_parts(g_ref[...])
            duv = du_ref[...]
            ov = o_ref[...]
            do = duv * sg
            dz_ref[3, rows, :] = (duv * ov * dsg).astype(BF16)
            do_o = do * ov
            do_bf = do.astype(BF16)
            for pp in range(A_PAIRS):
                ln = slice(pp * LANES, (pp + 1) * LANES)
                q = q_ref[:, ln] * SCALE
                kcat = _stack_pair(_window(k_ref, b, pad, A_WIN, ln))
                vcat = _stack_pair(_window(v_ref, b, pad, A_WIN, ln))
                sc = lax.dot_general(q, kcat, NT, preferred_element_type=F32) + bias_scr[pp]
                if first_blocks:
                    sc = jnp.where(_block_valid(b, left, A_WIN), sc, MASKED)
                lse_t = lse_ref[pp]
                dp = lax.dot_general(do_bf[:, ln], vcat, NT, preferred_element_type=F32)
                p, ds = _softmax_pair_bwd(sc, dp, (lse_t[:, 0:1], lse_t[:, HEAD_DIM:HEAD_DIM + 1]),
                                          _pair_rowsums(do_o[:, ln], lo), A_WIN)
                dbias_acc[pp] += ds
                dsb = ds.astype(BF16)
                dz_ref[0, rows, ln] = (jnp.dot(dsb, kcat, preferred_element_type=F32) * SCALE).astype(BF16)
                pb = p.astype(BF16)
                dob = do_bf[:, ln]
                dkt = jnp.concatenate([
                    lax.dot_general(q[:, e * HEAD_DIM:(e + 1) * HEAD_DIM], dsb[:, e * A_WIN:(e + 1) * A_WIN], TN,
                                    preferred_element_type=F32) for e in range(2)], axis=0)
                dvt = jnp.concatenate([
                    lax.dot_general(dob[:, e * HEAD_DIM:(e + 1) * HEAD_DIM], pb[:, e * A_WIN:(e + 1) * A_WIN], TN,
                                    preferred_element_type=F32) for e in range(2)], axis=0)
                for t in range(A_KBLOCKS):
                    blk = b + (pad // KB - left + t)
                    dk_acc[blk, ln, :] += dkt[:, t * KB:(t + 1) * KB]
                    dv_acc[blk, ln, :] += dvt[:, t * KB:(t + 1) * KB]

        @pl.when(b < left)
        def _():
            step(True)

        @pl.when(b >= left)
        def _():
            step(False)

        @pl.when(b == nb - 1)
        def _():
            for kb in range(s // KB):
                dz_ref[1, kb * KB:(kb + 1) * KB, :] = dk_acc[pad // KB + kb].T.astype(BF16)
                dz_ref[2, kb * KB:(kb + 1) * KB, :] = dv_acc[pad // KB + kb].T.astype(BF16)
            for hh in range(2 * A_PAIRS):
                ddiag_ref[hh] = _toeplitz_sum(
                    dbias_acc[hh // 2, :, (hh % 2) * A_WIN:(hh % 2 + 1) * A_WIN], A_WIN)

    tile = pl.BlockSpec((TQ, pw), lambda p, b: (b, p))
    diag_spec = pl.BlockSpec((2 * A_PAIRS, 1, wide), lambda p, b: (p, 0, 0))
    return _call(
        body, name="attn_a_bwd", grid=(HEADS // 2 // A_PAIRS, nb),
        in_specs=_a_qkv_specs(pad + s, pad, pw) + [
            tile, tile, tile, pl.BlockSpec((A_PAIRS, TQ, LANES), lambda p, b: (p, b, 0)), diag_spec],
        out_specs=[pl.BlockSpec((4, s, pw), lambda p, b: (0, 0, p)), diag_spec],
        out_shape=[jax.ShapeDtypeStruct((4, s, D_MODEL), BF16),
                   jax.ShapeDtypeStruct((HEADS, 1, wide), F32)],
        scratch_shapes=[pltpu.VMEM((A_PAIRS, TQ, 2 * A_WIN), F32), pltpu.VMEM((A_PAIRS, TQ, 2 * A_WIN), F32),
                        pltpu.VMEM(((pad + s) // KB, pw, KB), F32), pltpu.VMEM(((pad + s) // KB, pw, KB), F32)],
        sem=("parallel", "arbitrary"), hosted=hosted,
        args=(zqkv, zqkv, zqkv, g, o, du, lse, diag))


B_STACK = B_GROUP // 2
B_KVX = 4 * LANES
B_ROWS = B_STACK * TQ
B_WIDE = B_WIN + TQ


def _b_head_place(h):
    return h // B_GROUP, (h % B_GROUP) // 2, h % 2


def _toeplitz_tile_t(base_row, width, left_chunks):
    wide = width + TQ
    rolled = pltpu.roll(jnp.broadcast_to(base_row, (width, wide)), 0, 1, stride=1, stride_axis=0)
    j = lax.broadcasted_iota(jnp.int32, (width, TQ), 0) // CHUNK
    i = lax.broadcasted_iota(jnp.int32, (width, TQ), 1) // CHUNK
    dc = i + left_chunks - j
    return jnp.where((dc >= 0) & (dc <= left_chunks), rolled[:, :TQ], MASKED)


def _toeplitz_sum_t(tile_t, width):
    flip = (lax.broadcasted_iota(jnp.int32, (width, width), 0) + lax.broadcasted_iota(jnp.int32, (width, width), 1)
            == width - 1).astype(F32)
    reversed_rows = jnp.dot(flip, tile_t, precision=lax.Precision.HIGHEST, preferred_element_type=F32)
    padded = jnp.concatenate([reversed_rows, jnp.zeros((width, width), F32)], axis=1)
    rolled = pltpu.roll(padded, 0, 1, stride=1, stride_axis=0)
    return jnp.sum(rolled, axis=0, keepdims=True)


def _b_build_bias(base_ref, bias_scr):
    for h in range(HEADS):
        gi, pr, e = _b_head_place(h)
        bias_scr[gi, e * B_WIN:(e + 1) * B_WIN, pr * TQ:(pr + 1) * TQ] = _toeplitz_tile_t(
            base_ref[h], B_WIN, B_LEFT_CHUNKS)


def _b_stack(x, gi):
    return jnp.concatenate(
        [x[:, (B_STACK * gi + pr) * LANES:(B_STACK * gi + pr + 1) * LANES] for pr in range(B_STACK)], axis=0)


def _b_sink_rows(sink_ref, gi):
    block = lax.broadcasted_iota(jnp.int32, (1, B_ROWS), 1) // TQ
    rows = []
    for e in range(2):
        row = jnp.zeros((1, B_ROWS), F32)
        for pr in range(B_STACK):
            h = B_GROUP * gi + 2 * pr + e
            row = jnp.where(block == pr, sink_ref[0:1, h:h + 1], row)
        rows.append(row)
    return rows


def _b_scores_t(q_ref, kvv, bias_scr, gi, b, left, first_blocks):
    kcat = _stack_pair(kvv[:, gi * LANES:(gi + 1) * LANES])
    vcat = _stack_pair(kvv[:, (B_KV_HEADS + gi) * LANES:(B_KV_HEADS + gi + 1) * LANES])
    qs = _b_stack(q_ref, gi) * SCALE
    sc = lax.dot_general(kcat, qs, NT, preferred_element_type=F32) + bias_scr[gi]
    if first_blocks:
        row = lax.broadcasted_iota(jnp.int32, (2 * B_WIN, 1), 0)
        row = jnp.where(row >= B_WIN, row - B_WIN, row)
        sc = jnp.where((row // KB + (b - left)) >= 0, sc, MASKED)
    return kcat, vcat, qs, sc


def _attn_b_fwd(qb, kvx, gate, base, sinks):
    s = qb.shape[0]
    pad = kvx.shape[0] - s
    nb = s // TQ
    left = B_KBLOCKS - 1

    def body(q_ref, kv_ref, g_ref, base_ref, sink_ref, o_ref, u_ref, lse_ref, bias_scr):
        b = pl.program_id(0)

        @pl.when(b == 0)
        def _():
            _b_build_bias(base_ref, bias_scr)

        def step(first_blocks):
            kvv = _window(kv_ref, b, pad, B_WIN, slice(None))
            upper = lax.broadcasted_iota(jnp.int32, (LANES, B_ROWS), 0) < HEAD_DIM
            lse_rows = []
            for gi in range(B_KV_HEADS):
                kcat, vcat, qs, sc = _b_scores_t(q_ref, kvv, bias_scr, gi, b, left, first_blocks)
                sink = _b_sink_rows(sink_ref, gi)
                ps, inv = [], []
                for e in range(2):
                    sh = sc[e * B_WIN:(e + 1) * B_WIN]
                    m = jnp.maximum(jnp.max(sh, axis=0, keepdims=True), sink[e])
                    ex = jnp.exp(sh - m)
                    l = jnp.sum(ex, axis=0, keepdims=True) + jnp.exp(sink[e] - m)
                    ps.append(ex.astype(BF16))
                    inv.append(1.0 / l)
                    lse_rows.append(m + jnp.log(l))
                pt = jnp.concatenate(ps, axis=0)
                ot = lax.dot_general(vcat, pt, TN, preferred_element_type=F32) * jnp.where(upper, inv[0], inv[1])
                ov = ot.T
                for pr in range(B_STACK):
                    pair = B_STACK * gi + pr
                    o_ref[:, pair * LANES:(pair + 1) * LANES] = ov[pr * TQ:(pr + 1) * TQ]
            lse_ref[0] = jnp.concatenate(lse_rows + [jnp.zeros((8 - len(lse_rows), B_ROWS), F32)], axis=0)
            sg, _ = _silu_parts(g_ref[...])
            u_ref[...] = (o_ref[...] * sg).astype(BF16)

        @pl.when(b < left)
        def _():
            step(True)

        @pl.when(b >= left)
        def _():
            step(False)

    row = pl.BlockSpec((TQ, D_MODEL), lambda b: (b, 0))
    return pl.pallas_call(
        body, name="attn_b_fwd", grid=(nb,),
        in_specs=[row, pl.BlockSpec((pad + s, B_KVX), lambda b: (0, 0)), row,
                  pl.BlockSpec((HEADS, 1, B_WIDE), lambda b: (0, 0, 0)), pl.BlockSpec((1, HEADS), lambda b: (0, 0))],
        out_specs=[row, row, pl.BlockSpec((1, 8, B_ROWS), lambda b: (b, 0, 0))],
        out_shape=[jax.ShapeDtypeStruct((s, D_MODEL), F32), jax.ShapeDtypeStruct((s, D_MODEL), BF16),
                   jax.ShapeDtypeStruct((nb, 8, B_ROWS), F32)],
        scratch_shapes=[pltpu.VMEM((B_KV_HEADS, 2 * B_WIN, B_ROWS), F32)],
        compiler_params=_params(("arbitrary",)),
    )(qb, kvx, gate, base, sinks)


def _attn_b_bwd(qb, kvx, gate, o, du, lse, base, sinks):
    s = qb.shape[0]
    pad = kvx.shape[0] - s
    nb = s // TQ
    left = B_KBLOCKS - 1
    half = D_MODEL // 2

    def body(q_ref, kv_ref, g_ref, o_ref, du_ref, lse_ref, base_ref, sink_ref, dz_ref, dkv_ref, dsum_ref,
             dsink_ref, bias_scr, dbias_acc, dkv_acc, dsink_acc):
        b = pl.program_id(0)

        @pl.when(b == 0)
        def _():
            _b_build_bias(base_ref, bias_scr)
            dbias_acc[...] = jnp.zeros_like(dbias_acc)
            dkv_acc[...] = jnp.zeros_like(dkv_acc)
            dsink_acc[...] = jnp.zeros_like(dsink_acc)

        def step(first_blocks):
            kvv = _window(kv_ref, b, pad, B_WIN, slice(None))
            sg, dsg = _silu_parts(g_ref[...])
            duv = du_ref[...]
            ov = o_ref[...]
            do = duv * sg
            dgate = (duv * ov * dsg).astype(BF16)
            dz_ref[2] = dgate[:, :half]
            dz_ref[3] = dgate[:, half:]
            do_o = do * ov
            do_bf = do.astype(BF16)
            lse_all = lse_ref[0]
            dsink_rows = []
            for gi in range(B_KV_HEADS):
                kcat, vcat, qs, sc = _b_scores_t(q_ref, kvv, bias_scr, gi, b, left, first_blocks)
                dos = _b_stack(do_bf, gi)
                doo_t = _b_stack(do_o, gi).T
                delta = (jnp.sum(doo_t[:HEAD_DIM], axis=0, keepdims=True),
                         jnp.sum(doo_t[HEAD_DIM:], axis=0, keepdims=True))
                sink = _b_sink_rows(sink_ref, gi)
                dp = lax.dot_general(vcat, dos, NT, preferred_element_type=F32)
                ps, dss = [], []
                for e in range(2):
                    lse_e = lse_all[2 * gi + e:2 * gi + e + 1]
                    delta_e = delta[e]
                    p = jnp.exp(sc[e * B_WIN:(e + 1) * B_WIN] - lse_e)
                    ps.append(p.astype(BF16))
                    dss.append(p * (dp[e * B_WIN:(e + 1) * B_WIN] - delta_e))
                    dsink_rows.append(-jnp.exp(sink[e] - lse_e) * delta_e)
                ds = jnp.concatenate(dss, axis=0)
                dbias_acc[gi] += ds
                dsb = ds.astype(BF16)
                dq = (lax.dot_general(kcat, dsb, TN, preferred_element_type=F32) * SCALE).T.astype(BF16)
                for pr in range(B_STACK):
                    dz_ref[gi, :, pr * LANES:(pr + 1) * LANES] = dq[pr * TQ:(pr + 1) * TQ]
                dk = _unstack_pair(jnp.dot(dsb, qs, preferred_element_type=F32), B_WIN)
                dv = _unstack_pair(jnp.dot(jnp.concatenate(ps, axis=0), dos, preferred_element_type=F32), B_WIN)
                krows = pl.ds(pl.multiple_of(b * TQ + pad - (B_WIN - TQ), KB), B_WIN)
                dkv_acc[krows, gi * LANES:(gi + 1) * LANES] += dk
                dkv_acc[krows, (B_KV_HEADS + gi) * LANES:(B_KV_HEADS + gi + 1) * LANES] += dv
            dsink_acc[...] += jnp.concatenate(
                dsink_rows + [jnp.zeros((8 - len(dsink_rows), B_ROWS), F32)], axis=0)

        @pl.when(b < left)
        def _():
            step(True)

        @pl.when(b >= left)
        def _():
            step(False)

        @pl.when(b == nb - 1)
        def _():
            lo_s = _lane_lo(s)
            for which in range(2):
                folded = []
                for gi in range(B_KV_HEADS):
                    part = dkv_acc[pad:pad + s, (which * B_KV_HEADS + gi) * LANES:(which * B_KV_HEADS + gi + 1) * LANES]
                    folded.append(part + pltpu.roll(part, HEAD_DIM, 1))
                dkv_ref[:, which * LANES:(which + 1) * LANES] = jnp.where(lo_s, folded[0], folded[1]).astype(BF16)
            lane8 = lax.broadcasted_iota(jnp.int32, dsink_ref.shape, 1)
            tot = jnp.zeros(dsink_ref.shape, F32)
            for h in range(HEADS):
                gi, pr, e = _b_head_place(h)
                dsum_ref[h] = _toeplitz_sum_t(
                    dbias_acc[gi, e * B_WIN:(e + 1) * B_WIN, pr * TQ:(pr + 1) * TQ], B_WIN)
                per_query = dsink_acc[2 * gi + e:2 * gi + e + 1, pr * TQ:(pr + 1) * TQ]
                tot = jnp.where(lane8 == h, jnp.sum(per_query, axis=1, keepdims=True), tot)
            dsink_ref[...] = tot

    row = pl.BlockSpec((TQ, D_MODEL), lambda b: (b, 0))
    base_spec = pl.BlockSpec((HEADS, 1, B_WIDE), lambda b: (0, 0, 0))
    return pl.pallas_call(
        body, name="attn_b_bwd", grid=(nb,),
        in_specs=[row, pl.BlockSpec((pad + s, B_KVX), lambda b: (0, 0)), row, row, row,
                  pl.BlockSpec((1, 8, B_ROWS), lambda b: (b, 0, 0)), base_spec,
                  pl.BlockSpec((1, HEADS), lambda b: (0, 0))],
        out_specs=[pl.BlockSpec((4, TQ, half), lambda b: (0, b, 0)),
                   pl.BlockSpec((s, 2 * LANES), lambda b: (0, 0)), base_spec,
                   pl.BlockSpec((8, LANES), lambda b: (0, 0))],
        out_shape=[jax.ShapeDtypeStruct((4, s, half), BF16), jax.ShapeDtypeStruct((s, 2 * LANES), BF16),
                   jax.ShapeDtypeStruct((HEADS, 1, B_WIDE), F32), jax.ShapeDtypeStruct((8, LANES), F32)],
        scratch_shapes=[pltpu.VMEM((B_KV_HEADS, 2 * B_WIN, B_ROWS), F32),
                        pltpu.VMEM((B_KV_HEADS, 2 * B_WIN, B_ROWS), F32),
                        pltpu.VMEM((pad + s, B_KVX), F32), pltpu.VMEM((8, B_ROWS), F32)],
        compiler_params=_params(("arbitrary",)),
    )(qb, kvx, gate, o, du, lse, base, sinks)


def _t5_bucket(rel):
    nb = T5_BUCKETS // 2
    max_exact = nb // 2
    ret = jnp.where(rel > 0, nb, 0)
    n = jnp.abs(rel)
    nf = jnp.maximum(n, 1).astype(jnp.float32)
    large = max_exact + (jnp.log(nf / max_exact) / math.log(T5_MAX_DIST / max_exact)
                         * (nb - max_exact)).astype(jnp.int32)
    large = jnp.minimum(large, nb - 1)
    return ret + jnp.where(n < max_exact, n, large)


def _a_offset_onehot():
    c = np.arange(A_WIN + TQ)
    dist = A_LEFT_CHUNKS * CHUNK + TQ - 1 - c
    idx = np.clip(dist, -A_REL_CLIP, A_REL_CLIP) + A_REL_CLIP
    onehot = np.zeros((A_WIN + TQ, 2 * A_REL_CLIP + 1), np.float32)
    onehot[c, idx] = 1.0
    return jnp.asarray(onehot)


def _b_offset_onehot():
    c = jnp.arange(B_WIN + TQ, dtype=jnp.int32)
    rel = c - (TQ - 1) - B_LEFT_CHUNKS * CHUNK
    return (_t5_bucket(rel)[:, None] == jnp.arange(T5_BUCKETS)[None, :]).astype(F32)


def _diag_rows(onehot, table):
    rows = jnp.dot(onehot, table.astype(F32), precision=lax.Precision.HIGHEST)
    return rows.T.reshape(HEADS, 1, onehot.shape[0])


def _diag_rows_grad(onehot, ddiag):
    return jnp.dot(ddiag.reshape(HEADS, onehot.shape[0]), onehot, precision=lax.Precision.HIGHEST)


def _position():
    x, y, c = lax.axis_index("x"), lax.axis_index("y"), lax.axis_index("c")
    chips = [(1 - x, y), (x, 1 - y), (1 - x, 1 - y)]
    return x, y, c, chips


ANY = pl.BlockSpec(memory_space=pl.ANY)


def _allgather_routed(shards):
    n = len(shards)

    def piece(block_ref, t, c, quarter=None):
        half = shards[t].shape[0] // 2
        if quarter is None:
            return block_ref.at[pl.ds(c * half, half)]
        return block_ref.at[pl.ds(c * half + quarter * (half // 2), half // 2)]

    def copies(kind, ins, outs, sems):
        ici_send, ici_recv, pass_send, pass_recv, local_sems = sems
        x, y, c, chips = _position()
        mine = 2 * x + y
        if kind == "local":
            return [pltpu.make_async_copy(ins[t], outs[t].at[mine], local_sems.at[t]) for t in range(n)]
        ids = [2 * chip[0] + chip[1] for chip in chips]
        made = []
        for t in range(n):
            def ici(k, to):
                return dict(send_sem=ici_send.at[4 * t + k], recv_sem=ici_recv.at[4 * t + k],
                            device_id=(chips[to][0], chips[to][1], c), device_id_type=MESH)

            def d2d(k):
                return dict(send_sem=pass_send.at[4 * t + k], recv_sem=pass_recv.at[4 * t + k],
                            device_id=(x, y, 1 - c), device_id_type=MESH)

            def same(ref, where):
                return pltpu.make_async_remote_copy(src_ref=ref, dst_ref=ref, **where)

            if kind == "send":
                for k in range(2):
                    made.append(pltpu.make_async_remote_copy(
                        src_ref=piece(ins[t], t, c), dst_ref=piece(outs[t].at[mine], t, c), **ici(k, k)))
            elif kind == "landed":
                made += [same(piece(outs[t].at[ids[k]], t, c), ici(k, k)) for k in range(2)]
            elif kind == "forward":
                made.append(same(piece(outs[t].at[ids[0]], t, c, 0), ici(2, 1)))
                made.append(same(piece(outs[t].at[ids[1]], t, c, 1), ici(3, 0)))
            elif kind == "arrived":
                made.append(same(piece(outs[t].at[ids[2]], t, c, 0), ici(2, 1)))
                made.append(same(piece(outs[t].at[ids[2]], t, c, 1), ici(3, 0)))
            else:
                core = 1 - c if kind == "passed" else c
                if kind in ("pass halves", "passed"):
                    made += [same(piece(outs[t].at[ids[k]], t, core), d2d(k)) for k in range(2)]
                if kind in ("pass quarters", "passed"):
                    made += [same(piece(outs[t].at[ids[2]], t, core, k), d2d(2 + k)) for k in range(2)]
        return made

    def first(ins, outs, sems):
        for cp in copies("local", ins, outs, sems) + copies("send", ins, outs, sems):
            cp.start()

    def middle(ins, outs, sems):
        for got, onward, near in zip(copies("landed", ins, outs, sems), copies("forward", ins, outs, sems),
                                     copies("pass halves", ins, outs, sems)):
            got.wait_recv()
            near.start()
            onward.start()

    def last(ins, outs, sems):
        quarters = copies("pass quarters", ins, outs, sems)
        for got, near in zip(copies("arrived", ins, outs, sems), quarters):
            got.wait_recv()
            near.start()
        for cp in copies("passed", ins, outs, sems):
            cp.wait_recv()
        for cp in (copies("send", ins, outs, sems) + copies("forward", ins, outs, sems)
                   + copies("pass halves", ins, outs, sems) + quarters):
            cp.wait_send()
        for cp in copies("local", ins, outs, sems):
            cp.wait()

    return _Hosted(shards, [jax.ShapeDtypeStruct((4,) + w.shape, w.dtype) for w in shards],
                   [pltpu.SemaphoreType.DMA((4 * n,))] * 4 + [pltpu.SemaphoreType.DMA((n,))],
                   first, middle, last)


def _scatter_hosted(grads):
    n = len(grads)

    def copies(ins, outs, sems):
        send_sems, recv_sems = sems
        x, y, c, chips = _position()
        return [pltpu.make_async_remote_copy(
            src_ref=ins[t].at[2 * chip[0] + chip[1]], dst_ref=outs[t].at[j],
            send_sem=send_sems.at[3 * t + j], recv_sem=recv_sems.at[3 * t + j],
            device_id=(chip[0], chip[1], c), device_id_type=MESH)
            for t in range(n) for j, chip in enumerate(chips)]

    def first(ins, outs, sems):
        for cp in copies(ins, outs, sems):
            cp.start()

    def last(ins, outs, sems):
        for cp in copies(ins, outs, sems):
            cp.wait()

    return _Hosted(grads, [jax.ShapeDtypeStruct((3,) + g.shape[1:], g.dtype) for g in grads],
                   [pltpu.SemaphoreType.DMA((3 * n,))] * 2, first, None, last)


GATHER_PEERS = "x and y neighbours (same core) and the sibling core"
SCATTER_PEERS = "the same core of the three other chips"
EVERYONE = "the seven other devices"


def _run_on_sequencer(name, hosted, peers, collective_id):
    ins = [jax.new_ref(a, memory_space=pltpu.MemorySpace.HBM) for a in hosted.inputs]
    outs = [jax.empty_ref(shape, memory_space=pltpu.MemorySpace.HBM) for shape in hosted.out_shapes]

    @pl.kernel(mesh=plsc.ScalarSubcoreMesh(axis_name="sequencer", num_cores=1), name=name,
               scratch_types=tuple(hosted.sems), compiler_params=pltpu.CompilerParams(collective_id=collective_id))
    def launch(*sems):
        x, y, c, chips = _position()
        if peers == GATHER_PEERS:
            devices = [(chip[0], chip[1], c) for chip in chips[:2]] + [(x, y, 1 - c)]
        elif peers == SCATTER_PEERS:
            devices = [(chip[0], chip[1], c) for chip in chips]
        else:
            devices = [(x ^ (k >> 2), y ^ ((k >> 1) & 1), c ^ (k & 1)) for k in range(1, 8)]
        barrier = pltpu.get_barrier_semaphore()
        for device in devices:
            pl.semaphore_signal(barrier, inc=1, device_id=device, device_id_type=MESH)
        pl.semaphore_wait(barrier, len(devices))
        hosted.first(ins, outs, sems)
        if hosted.middle is not None:
            hosted.middle(ins, outs, sems)
        hosted.last(ins, outs, sems)

    launch()
    return [o[...] for o in outs]


def _gather_gain(shard):
    def body(in_ref, out_ref, send_sems, recv_sems):
        x, y, c, chips = _position()
        out_ref[2 * x + y] = in_ref[...]
        sends = [pltpu.make_async_remote_copy(
            src_ref=in_ref, dst_ref=out_ref.at[2 * x + y], send_sem=send_sems.at[j], recv_sem=recv_sems.at[j],
            device_id=(chip[0], chip[1], c), device_id_type=MESH) for j, chip in enumerate(chips)]
        for cp in sends:
            cp.start()
        for j, chip in enumerate(chips):
            pltpu.make_async_remote_copy(
                src_ref=in_ref, dst_ref=out_ref.at[2 * chip[0] + chip[1]], send_sem=send_sems.at[j],
                recv_sem=recv_sems.at[j], device_id=(chip[0], chip[1], c), device_id_type=MESH).wait_recv()
        for cp in sends:
            cp.wait_send()

    vmem = pl.BlockSpec(memory_space=pltpu.VMEM)
    return pl.pallas_call(
        body, name="gather_gain", in_specs=[vmem], out_specs=vmem,
        out_shape=jax.ShapeDtypeStruct((4,) + shard.shape, shard.dtype),
        scratch_shapes=[pltpu.SemaphoreType.DMA((3,))] * 2,
    )(shard)


def _swap_with_sibling(name, blocks):
    n = len(blocks)

    def body(*refs):
        ins, outs = refs[:n], refs[n:2 * n]
        send_sems, recv_sems = refs[2 * n:]
        x, y, c, _ = _position()
        sends = [pltpu.make_async_remote_copy(
            src_ref=ins[t], dst_ref=outs[t], send_sem=send_sems.at[t], recv_sem=recv_sems.at[t],
            device_id=(x, y, 1 - c), device_id_type=MESH) for t in range(n)]
        for cp in sends:
            cp.start()
        for cp in sends:
            cp.wait()

    return pl.pallas_call(
        body, name=name,
        in_specs=[ANY] * n, out_specs=[ANY] * n,
        out_shape=[jax.ShapeDtypeStruct(b.shape, b.dtype) for b in blocks],
        scratch_shapes=[pltpu.SemaphoreType.DMA((n,))] * 2,
    )(*blocks)


def _everyone_hosted(terms):
    nt = len(terms)

    def copies(kind, ins, outs, sems):
        send_sems, recv_sems, local_sems = sems
        x, y, c, _ = _position()
        me = 4 * x + 2 * y + c
        if kind == "local":
            return [pltpu.make_async_copy(ins[t], outs[t].at[me], local_sems.at[t]) for t in range(nt)]
        made = []
        for t in range(nt):
            for k in range(1, 8):
                peer = (x ^ (k >> 2), y ^ ((k >> 1) & 1), c ^ (k & 1))
                slot = me if kind == "send" else me ^ k
                made.append(pltpu.make_async_remote_copy(
                    src_ref=ins[t], dst_ref=outs[t].at[slot], send_sem=send_sems.at[7 * t + k - 1],
                    recv_sem=recv_sems.at[7 * t + k - 1], device_id=peer, device_id_type=MESH))
        return made

    def first(ins, outs, sems):
        for cp in copies("local", ins, outs, sems) + copies("send", ins, outs, sems):
            cp.start()

    def last(ins, outs, sems):
        for cp in copies("landed", ins, outs, sems):
            cp.wait_recv()
        for cp in copies("send", ins, outs, sems):
            cp.wait_send()
        for cp in copies("local", ins, outs, sems):
            cp.wait()

    return _Hosted(terms, [jax.ShapeDtypeStruct((8,) + a.shape, F32) for a in terms],
                   [pltpu.SemaphoreType.DMA((7 * nt,))] * 2 + [pltpu.SemaphoreType.DMA((nt,))], first, None, last)


def _small_step(partials, extras, ws, ms, vs, shard_of):
    n = len(partials)
    terms = list(partials) + list(extras)
    nt = len(terms)
    rows = [t for t in range(nt) if terms[t].shape[0] == 1]
    mats = [t for t in range(nt) if terms[t].shape[0] != 1]
    row_block = (8, max(terms[t].shape[1] for t in rows))
    assert len(rows) <= row_block[0]
    vmem = pl.BlockSpec(memory_space=pltpu.VMEM)

    def pack(*refs):
        packed = refs[-1]
        packed[...] = jnp.zeros_like(packed)
        for i, t in enumerate(rows):
            packed[i:i + 1, 0:terms[t].shape[1]] = refs[i][...]

    packed = pl.pallas_call(pack, name="small_pack", in_specs=[vmem] * len(rows), out_specs=vmem,
                            out_shape=jax.ShapeDtypeStruct(row_block, F32))(*[terms[t] for t in rows])
    slots = _run_on_sequencer("allgather_small", _everyone_hosted([packed] + [terms[t] for t in mats]),
                              EVERYONE, 2)

    def body(*refs):
        slot_refs, refs = refs[:len(slots)], refs[len(slots):]
        w_refs, refs = refs[:n], refs[n:]
        m_refs, refs = refs[:n], refs[n:]
        v_refs, outs = refs[:n], refs[n:]
        sums = []
        for ref in slot_refs:
            g = ref[0]
            for dev in range(1, 8):
                g = g + ref[dev]
            sums.append(g)
        chip = 2 * lax.axis_index("x") + lax.axis_index("y")
        for t in range(nt):
            if t in rows:
                i = rows.index(t)
                g = sums[0][i:i + 1, 0:terms[t].shape[1]]
            else:
                g = sums[1 + mats.index(t)]
            if t >= n:
                outs[4 * n + t - n][...] = g
                continue
            if shard_of[t]:
                width = ws[t].shape[-1]
                mine = jnp.zeros(ws[t].shape, F32)
                for s in range(4):
                    mine = jnp.where(chip == s, g[:, s * width:(s + 1) * width], mine)
                g = mine
            delta, mn, vn = _adamw_math(w_refs[t][...], g, m_refs[t][...], v_refs[t][...])
            outs[4 * t][...] = g
            outs[4 * t + 1][...] = delta
            outs[4 * t + 2][...] = mn
            outs[4 * t + 3][...] = vn

    out_shapes = []
    for t in range(n):
        out_shapes += [jax.ShapeDtypeStruct(ws[t].shape, F32)] * 4
    out_shapes += [jax.ShapeDtypeStruct(a.shape, F32) for a in extras]
    res = pl.pallas_call(
        body, name="small_step",
        in_specs=[vmem] * (len(slots) + 3 * n), out_specs=[vmem] * len(out_shapes), out_shape=out_shapes,
    )(*slots, *ws, *ms, *vs)
    return [res[4 * t:4 * t + 4] for t in range(n)], res[4 * n:4 * n + nt - n]


def _adamw_math(w, g, m, v):
    m = ADAM_B1 * m + (1.0 - ADAM_B1) * g
    v = ADAM_B2 * v + (1.0 - ADAM_B2) * (g * g)
    m_hat = m / (1.0 - ADAM_B1 ** ADAM_STEP)
    v_hat = v / (1.0 - ADAM_B2 ** ADAM_STEP)
    delta = -ADAM_LR * (m_hat / (jnp.sqrt(v_hat) + ADAM_EPS) + ADAM_WD * w)
    return delta, m, v


def _row_tile(rows):
    return 256 if rows % 256 == 0 else rows


def _sum_partials(name, own, recv, chip, after):
    rows, cols = own.shape[1:]
    tr = _row_tile(rows)

    def body(chip_ref, own_ref, recv_ref, after_ref, o_ref):
        acc = own_ref[...]
        for j in range(3):
            acc = acc + recv_ref[j].astype(F32)
        o_ref[...] = acc

    return pl.pallas_call(
        body, name=name,
        grid_spec=pltpu.PrefetchScalarGridSpec(
            num_scalar_prefetch=1, grid=(rows // tr,),
            in_specs=[pl.BlockSpec((None, tr, cols), lambda i, chip_ref: (chip_ref[0], i, 0)),
                      pl.BlockSpec((3, tr, cols), lambda i, chip_ref: (0, i, 0)), ANY],
            out_specs=pl.BlockSpec((tr, cols), lambda i, chip_ref: (i, 0))),
        out_shape=jax.ShapeDtypeStruct((rows, cols), F32),
        compiler_params=_params(("parallel",)),
    )(chip.reshape(1).astype(jnp.int32), own, recv, after)


def _adamw_ring(name, body, inputs, tr, spec):
    rows, cols = inputs[0].shape
    n_in = len(inputs)
    steps = rows // tr

    def ring(*refs):
        in_refs, out_refs = refs[:n_in], refs[n_in:n_in + 4]
        buf, sems = refs[n_in + 4:]
        i = pl.program_id(0)

        def copies(step, slot):
            return [pltpu.make_async_copy(in_refs[t].at[pl.ds(pl.multiple_of(step * tr, tr), tr)],
                                          buf.at[slot, t], sems.at[slot, t]) for t in range(n_in)]

        @pl.when(i == 0)
        def _():
            for s in range(min(STREAM_BUFFERS, steps)):
                for cp in copies(s, s):
                    cp.start()

        ahead = i + STREAM_BUFFERS - 1

        @pl.when((i > 0) & (ahead < steps))
        def _():
            for cp in copies(ahead, ahead % STREAM_BUFFERS):
                cp.start()

        slot = i % STREAM_BUFFERS
        for cp in copies(i, slot):
            cp.wait()
        body(*[buf.at[slot, t] for t in range(n_in)], *out_refs)

    return pl.pallas_call(
        ring, name=name, grid=(steps,), in_specs=[ANY] * n_in, out_specs=[spec] * 4,
        out_shape=[jax.ShapeDtypeStruct((rows, cols), F32)] * 4,
        scratch_shapes=[pltpu.VMEM((STREAM_BUFFERS, n_in, tr, cols), F32),
                        pltpu.SemaphoreType.DMA((STREAM_BUFFERS, n_in))],
        compiler_params=_params(("arbitrary",)),
    )(*inputs)


def _adamw(name, w, m, v, g_parts):
    rows, cols = w.shape
    tr = _row_tile(rows)
    n = len(g_parts)

    def body(w_ref, m_ref, v_ref, *refs):
        g_refs = refs[:n]
        go_ref, d_ref, mo_ref, vo_ref = refs[n:]
        g = g_refs[0][...]
        for r in g_refs[1:]:
            g = g + r[...]
        delta, mn, vn = _adamw_math(w_ref[...], g, m_ref[...], v_ref[...])
        go_ref[...] = g
        d_ref[...] = delta
        mo_ref[...] = mn
        vo_ref[...] = vn

    spec = pl.BlockSpec((tr, cols), lambda i: (i, 0))
    steps = rows // tr
    if steps > 2:
        return _adamw_ring(name, body, [w, m, v, *g_parts], tr, spec)
    return pl.pallas_call(
        body, name=name, grid=(steps,),
        in_specs=[spec] * (3 + n), out_specs=[spec] * 4,
        out_shape=[jax.ShapeDtypeStruct((rows, cols), F32)] * 4,
        compiler_params=_params(("parallel",)),
    )(w, m, v, *g_parts)


def _local_step(x, target, ga, wa_in, rel_bias, later_shards, gk, t5, gb, sinks, gf):
    s, d = x.shape
    tm = min(TM_DENSE, s)
    nt = s // tm
    half = d // 2
    row = pl.BlockSpec((tm, d), lambda i: (i, 0))
    whole = lambda shape: pl.BlockSpec(shape, lambda *_: (0,) * len(shape))

    n1, = _norm_fwd("norm_a", x, ga)
    projected = None
    for h, (wa_half, tag) in enumerate(zip(wa_in, ("first", "second"))):
        projected = _proj_a_half("proj_a_" + tag, n1, wa_half, h, projected)
    zqkv, gate_a = projected
    onehot_a = _a_offset_onehot()
    diag_a = _diag_rows(onehot_a, rel_bias)
    (o_a, u_a, lse_a), gathered = _attn_a_fwd(zqkv, gate_a, diag_a, hosted=_allgather_routed(later_shards))
    wa_out, wkv, wb_in, wb_out, wkv_x = gathered
    wa_out = wa_out.reshape(d, d)
    wkv = wkv.reshape(d, -1)
    wkv_x = wkv_x.reshape(d, B_KVX)
    wb_out = wb_out.reshape(d, d)
    h1, nk, n2 = _out_norms("out_a_norms", u_a, wa_out, x, jnp.concatenate([gk, gb], axis=0))
    kvw = wkv.shape[1]
    kvx =_matmul("proj_kv", nk, wkv_x, dims=NN, grid=(nt + 1,), zero_axis=0,
                  a_spec=pl.BlockSpec((tm, d), lambda i: (jnp.maximum(i - 1, 0), 0)), b_spec=whole((d, B_KVX)),
                  o_spec=pl.BlockSpec((tm, B_KVX), lambda i: (i, 0)), out_shape=(tm + s, B_KVX), out_dtype=BF16)
    qb, gate_b = _proj_b(n2, wb_in)
    onehot_b = _b_offset_onehot()
    base_b = jnp.roll(_diag_rows(onehot_b, t5)[..., ::-1], TQ, axis=-1)
    o_b, u_b, lse_b = _attn_b_fwd(qb, kvx, gate_b, base_b, sinks)
    dh2, loss, d_gf = _out_loss_head(u_b, wb_out, h1, target, gf)

    du_b = _matmul("dout_b", dh2, wb_out, dims=NT, grid=(nt,), a_spec=row, b_spec=whole((d, d)), o_spec=row,
                   out_shape=(s, d), out_dtype=F32)
    d_wb_out = _matmul("dw_out_b", u_b, dh2, dims=TN, grid=(2,),
                       a_spec=whole((s, d)), b_spec=pl.BlockSpec((s, half), lambda j: (0, j)),
                       o_spec=pl.BlockSpec((d, half), lambda j: (0, j)),
                       out_shape=(d, d), out_dtype=F32, also_bf16=True)
    dz_b, dkv, dsum_b, dsinks = _attn_b_bwd(qb, kvx, gate_b, o_b, du_b, lse_b, base_b, sinks)
    ddiag_b = jnp.roll(dsum_b[..., ::-1], -1, axis=-1)
    d_wb_in = _matmul("dw_in_b", n2, dz_b, dims=TN, grid=(4,),
                      a_spec=whole((s, d)), b_spec=pl.BlockSpec((None, s, half), lambda j: (j, 0, 0)),
                      o_spec=pl.BlockSpec((None, d, half), lambda j: (j, 0, 0)),
                      out_shape=(4, d, half), out_dtype=F32, also_bf16=True)
    d_wkv = _matmul("dw_kv", nk, dkv, dims=TN, grid=(1,),
                    a_spec=whole((s, d)), b_spec=whole((s, kvw)), o_spec=whole((d, kvw)),
                    out_shape=(d, kvw), out_dtype=F32, also_bf16=True)
    dh1, d_gkb = _proj_norm_bwd("dproj_kv_b", h1, dh2, jnp.concatenate([gk, gb], axis=0),
                                [(dkv[None], [wkv[None]]), (dz_b, [wb_in])])

    du_a = _matmul("dout_a", dh1, wa_out, dims=NT, grid=(nt,), a_spec=row, b_spec=whole((d, d)), o_spec=row,
                   out_shape=(s, d), out_dtype=F32)
    d_wa_out = _matmul("dw_out_a", u_a, dh1, dims=TN, grid=(2,),
                       a_spec=whole((s, d)), b_spec=pl.BlockSpec((s, half), lambda j: (0, j)),
                       o_spec=pl.BlockSpec((d, half), lambda j: (0, j)),
                       out_shape=(d, d), out_dtype=F32, also_bf16=True)
    early = dict(a_w_out=[g.reshape(4, d // 4, d) for g in d_wa_out],
                 kv_w=[g.reshape(4, d // 4, kvw) for g in d_wkv], b_w_in=list(d_wb_in),
                 b_w_out=[g.reshape(4, d // 4, d) for g in d_wb_out])
    (dz_a, ddiag_a), early_recv = _attn_a_bwd(
        zqkv, gate_a, o_a, du_a, lse_a, diag_a, hosted=_scatter_hosted([early[n][1] for n in early]))
    d_wa_in = _matmul("dw_in_a", n1, dz_a, dims=TN, grid=(4, 2),
                      a_spec=whole((s, d)), b_spec=pl.BlockSpec((None, s, half), lambda j, h: (j, 0, h)),
                      o_spec=pl.BlockSpec((None, d, half), lambda j, h: (j, 0, h)),
                      out_shape=(4, d, d), out_dtype=F32, also_bf16=True)
    late_recv = _run_on_sequencer("scatter_a_w_in", _scatter_hosted([d_wa_in[1]]), SCATTER_PEERS, 0)
    grad_x, d_ga = _proj_norm_bwd("dproj_a", x, dh1, ga, [(dz_a, list(wa_in))])

    small = dict(a_norm=d_ga, kv_norm=d_gkb[0:1], b_norm=d_gkb[1:2], b_sinks=dsinks[0:1, :HEADS], final_norm=d_gf)
    small["by_offset"] = dict(a_rel_bias=(onehot_a, ddiag_a.reshape(HEADS, -1)),
                              t5_bias=(onehot_b, ddiag_b.reshape(HEADS, -1)))
    own = dict(a_w_in=d_wa_in[0], **{n: early[n][0] for n in early})
    received = dict(a_w_in=late_recv[0], **dict(zip(early, early_recv)))
    return loss, grad_x, small, own, received, d_wa_in[1]


SMALL = ("a_norm", "kv_norm", "b_norm", "b_sinks", "final_norm")
TABLES = ("a_rel_bias", "t5_bias")
BIG = ("a_w_in", "a_w_out", "kv_w", "b_w_in", "b_w_out")
ORDER = ("a_norm", "a_w_in", "a_rel_bias", "a_w_out", "kv_norm", "kv_w", "t5_bias", "b_norm", "b_w_in",
         "b_sinks", "b_w_out", "final_norm")


def kernel(x, a_norm, a_w_in, a_rel_bias, a_w_out, kv_norm, kv_w, t5_bias, b_norm, b_w_in, b_sinks, b_w_out, final_norm, loss_target, m_a_norm, m_a_w_in, m_a_rel_bias, m_a_w_out, m_kv_norm, m_kv_w, m_t5_bias, m_b_norm, m_b_w_in, m_b_sinks, m_b_w_out, m_final_norm, v_a_norm, v_a_w_in, v_a_rel_bias, v_a_w_out, v_kv_norm, v_kv_w, v_t5_bias, v_b_norm, v_b_w_in, v_b_sinks, v_b_w_out, v_final_norm):
    w = dict(a_norm=a_norm, a_w_in=a_w_in, a_rel_bias=a_rel_bias, a_w_out=a_w_out, kv_norm=kv_norm, kv_w=kv_w,
             t5_bias=t5_bias, b_norm=b_norm, b_w_in=b_w_in, b_sinks=b_sinks, b_w_out=b_w_out,
             final_norm=final_norm)
    m = dict(a_norm=m_a_norm, a_w_in=m_a_w_in, a_rel_bias=m_a_rel_bias, a_w_out=m_a_w_out, kv_norm=m_kv_norm,
             kv_w=m_kv_w, t5_bias=m_t5_bias, b_norm=m_b_norm, b_w_in=m_b_w_in, b_sinks=m_b_sinks,
             b_w_out=m_b_w_out, final_norm=m_final_norm)
    v = dict(a_norm=v_a_norm, a_w_in=v_a_w_in, a_rel_bias=v_a_rel_bias, a_w_out=v_a_w_out, kv_norm=v_kv_norm,
             kv_w=v_kv_w, t5_bias=v_t5_bias, b_norm=v_b_norm, b_w_in=v_b_w_in, b_sinks=v_b_sinks,
             b_w_out=v_b_w_out, final_norm=v_final_norm)
    d = D_MODEL
    chip = 2 * lax.axis_index("x") + lax.axis_index("y")

    shard2d = dict(a_w_in=a_w_in[0], a_w_out=a_w_out[0], kv_w=kv_w, b_w_in=b_w_in[0], b_w_out=b_w_out[0])

    first = shard2d["a_w_in"].astype(BF16)
    wa_in = [_run_on_sequencer("allgather_" + tag, _allgather_routed([first[:, h * (d // 2):(h + 1) * (d // 2)]]),
                               GATHER_PEERS, collective_id)[0]
             for h, (tag, collective_id) in enumerate((("first", 1), ("second", 3)))]
    ga = _gather_gain(a_norm).reshape(1, d)

    later = [shard2d[n].astype(BF16) for n in BIG[1:]]
    kv_shard = later[BIG[1:].index("kv_w")]
    later.append(jnp.concatenate(
        [kv_shard[:, (i // 2) * HEAD_DIM:(i // 2 + 1) * HEAD_DIM] for i in range(B_KVX // HEAD_DIM)], axis=1))
    loss, grad_x, small, own, received, after_attention = _local_step(
        x[0], loss_target[0], ga, wa_in, a_rel_bias[0], later,
        kv_norm.reshape(1, d), t5_bias, b_norm, b_sinks, final_norm.reshape(1, d))

    out = {}
    as2d = lambda a: a.reshape(-1, a.shape[-1])
    small_res, (loss_sum, *offset_sums) = _small_step(
        [small[n] for n in SMALL], [loss] + [small["by_offset"][n][1] for n in TABLES],
        [as2d(w[n]) for n in SMALL], [as2d(m[n]) for n in SMALL], [as2d(v[n]) for n in SMALL],
        [n == "a_norm" for n in SMALL])
    for n, res in zip(SMALL, small_res):
        out[n] = [r.reshape(w[n].shape) for r in res]
    loss_out = loss_sum.reshape(())
    for n, summed in zip(TABLES, offset_sums):
        grad = _diag_rows_grad(small["by_offset"][n][0], summed)
        res = _adamw("adamw_" + n, as2d(w[n]).T, as2d(m[n]).T, as2d(v[n]).T, [grad])
        out[n] = [r.T.reshape(w[n].shape) for r in res]

    core_sums = [_sum_partials("sum_" + n, own[n], received[n], chip, after_attention) for n in BIG]
    sibling_sums = (_swap_with_sibling("swap_last", core_sums[:1])
                    + _swap_with_sibling("swap_early", core_sums[1:]))

    for n, mine, theirs in zip(BIG, core_sums, sibling_sums):
        res = _adamw("adamw_" + n, shard2d[n], m[n].reshape(shard2d[n].shape), v[n].reshape(shard2d[n].shape),
                     [mine, theirs])
        out[n] = [r.reshape(w[n].shape) for r in res]

    grads = [out[n][0] for n in ORDER]
    deltas = [out[n][1] for n in ORDER]
    new_m = [out[n][2] for n in ORDER]
    new_v = [out[n][3] for n in ORDER]
    return (loss_out, grad_x[None], *grads, *deltas, *new_m, *new_v)
```

```python
import math

import jax
import jax.numpy as jnp
import numpy as np
from jax import lax
from jax.experimental import pallas as pl
from jax.experimental.pallas import tpu as pltpu
from jax.experimental.pallas import tpu_sc as plsc

F32 = jnp.float32
BF16 = jnp.bfloat16
MESH = pl.DeviceIdType.MESH

D_MODEL = 1024
HEADS = 16
HEAD_DIM = 64
CHUNK = 64
RMS_EPS = 1e-6
SCALE = HEAD_DIM ** -0.5
A_LEFT_CHUNKS = 8
A_REL_CLIP = 256
B_LEFT_CHUNKS = 2
B_KV_HEADS = 2
B_GROUP = HEADS // B_KV_HEADS
T5_BUCKETS = 32
T5_MAX_DIST = 128
ADAM_LR = 0.001
ADAM_B1 = 0.9
ADAM_B2 = 0.999
ADAM_EPS = 1e-08
ADAM_WD = 0.01
ADAM_STEP = 10

MASKED = -1e30
LANES = 128
TQ = 128
A_PAIRS = 2
A_PAIRS_FWD = 4
KB = 128
A_KBLOCKS = A_LEFT_CHUNKS * CHUNK // KB + 1
B_KBLOCKS = B_LEFT_CHUNKS * CHUNK // KB + 1
A_WIN = A_KBLOCKS * KB
B_WIN = B_KBLOCKS * KB
TM = 512
TM_DENSE = 1024
TM_HALF = 2048
STREAM_BUFFERS = 3
TM_PARTS = 512
VMEM_LIMIT = 56 * 1024 * 1024

NT = (((1,), (1,)), ((), ()))
TN = (((0,), (0,)), ((), ()))
NN = (((1,), (0,)), ((), ()))


def _params(sem=None):
    return pltpu.CompilerParams(dimension_semantics=sem, vmem_limit_bytes=VMEM_LIMIT)


class _Hosted:
    def __init__(self, inputs, out_shapes, sems, first, middle, last):
        self.inputs, self.out_shapes, self.sems = list(inputs), list(out_shapes), list(sems)
        self.first, self.middle, self.last = first, middle, last


def _call(body, *, name, grid, in_specs, out_specs, out_shape, args, scratch_shapes=(), sem=None, hosted=None,
          aliases=None):
    in_specs, out_specs, out_shape = list(in_specs), list(out_specs), list(out_shape)
    scratch_shapes = list(scratch_shapes)
    if hosted is None:
        out = pl.pallas_call(
            body, name=name, grid=grid, in_specs=in_specs, out_specs=out_specs, out_shape=out_shape,
            scratch_shapes=scratch_shapes, input_output_aliases=aliases or {},
            compiler_params=_params(sem))(*args)
        return list(out), []
    assert aliases is None
    n_in, n_out, n_scr = len(in_specs), len(out_shape), len(scratch_shapes)
    h_in, h_out = len(hosted.inputs), len(hosted.out_shapes)
    total = int(np.prod(grid)) if grid else 1

    def wrapped(*refs):
        ins, refs = refs[:n_in], refs[n_in:]
        h_ins, refs = refs[:h_in], refs[h_in:]
        outs, refs = refs[:n_out], refs[n_out:]
        h_outs, refs = refs[:h_out], refs[h_out:]
        scr, h_sems = refs[:n_scr], refs[n_scr:]
        step = 0
        for axis, size in enumerate(grid):
            step = step * size + pl.program_id(axis)

        if hosted.first is not None:
            @pl.when(step == 0)
            def _():
                hosted.first(h_ins, h_outs, h_sems)

        body(*ins, *outs, *scr)
        if hosted.middle is not None:
            @pl.when(step == total // 2)
            def _():
                hosted.middle(h_ins, h_outs, h_sems)

        if hosted.last is not None:
            @pl.when(step == total - 1)
            def _():
                hosted.last(h_ins, h_outs, h_sems)

    out = pl.pallas_call(
        wrapped, name=name, grid=grid, in_specs=in_specs + [ANY] * h_in, out_specs=out_specs + [ANY] * h_out,
        out_shape=out_shape + hosted.out_shapes, scratch_shapes=scratch_shapes + hosted.sems,
        compiler_params=_params(("arbitrary",) * len(grid)))(*args, *hosted.inputs)
    return list(out[:n_out]), list(out[n_out:])


def _matmul(name, a, b, *, dims, grid, a_spec, b_spec, o_spec, out_shape, out_dtype,
            also_bf16=False, zero_axis=None):
    def body(*refs):
        if zero_axis is None:
            product(*refs)
        else:
            @pl.when(pl.program_id(zero_axis) == 0)
            def _():
                refs[2][...] = jnp.zeros_like(refs[2])

            @pl.when(pl.program_id(zero_axis) > 0)
            def _():
                product(*refs)

    def product(a_ref, b_ref, o_ref, *more):
        prod = lax.dot_general(a_ref[...].astype(BF16), b_ref[...].astype(BF16), dims,
                               preferred_element_type=F32)
        o_ref[...] = prod.astype(out_dtype)
        if also_bf16:
            more[0][...] = prod.astype(BF16)

    out_specs = [o_spec]
    out_shapes = [jax.ShapeDtypeStruct(out_shape, out_dtype)]
    if also_bf16:
        out_specs.append(o_spec)
        out_shapes.append(jax.ShapeDtypeStruct(out_shape, BF16))
    out, _ = _call(body, name=name, grid=grid, in_specs=[a_spec, b_spec], out_specs=out_specs,
                   out_shape=out_shapes, args=[a, b], sem=("parallel",) * len(grid))
    return out[0] if not also_bf16 else tuple(out)


def _proj_a_half(name, n1, w, h, into):
    s, d = n1.shape
    half = w.shape[2]
    ta = min(TM_HALF, s)

    def body(a_ref, w_ref, *refs):
        z_ref, g_ref = refs[-2:]
        i, j = pl.program_id(0), pl.program_id(1)

        @pl.when((i == 0) & (j < 3))
        def _():
            z_ref[...] = jnp.zeros_like(z_ref)

        @pl.when((i > 0) & (j < 3))
        def _():
            z_ref[...] = jnp.dot(a_ref[...], w_ref[...], preferred_element_type=F32).astype(BF16)

        @pl.when((i > 0) & (j == 3))
        def _():
            g_ref[...] = jnp.dot(a_ref[...], w_ref[...], preferred_element_type=F32)

    out, _ = _call(
        body, name=name, grid=(s // ta + 1, 4),
        in_specs=[pl.BlockSpec((ta, d), lambda i, j: (jnp.maximum(i - 1, 0), 0)),
                  pl.BlockSpec((None, d, half), lambda i, j: (j, 0, 0))] + ([] if into is None else [ANY, ANY]),
        out_specs=[pl.BlockSpec((None, ta, half), lambda i, j: (jnp.minimum(j, 2), i, h)),
                   pl.BlockSpec((ta, half), lambda i, j: (jnp.maximum(i - 1, 0), h))],
        out_shape=[jax.ShapeDtypeStruct((3, ta + s, d), BF16), jax.ShapeDtypeStruct((s, d), F32)],
        args=[n1, w] + ([] if into is None else list(into)), sem=("arbitrary", "arbitrary"),
        aliases=None if into is None else {2: 0, 3: 1})
    return out


def _proj_b(n2, w):
    s, d = n2.shape
    half = w.shape[2]
    ta = min(TM_HALF, s)

    def body(a_ref, w_ref, q_ref, g_ref):
        j = pl.program_id(1)

        @pl.when(j < 2)
        def _():
            q_ref[...] = jnp.dot(a_ref[...], w_ref[...], preferred_element_type=F32).astype(BF16)

        @pl.when(j >= 2)
        def _():
            g_ref[...] = jnp.dot(a_ref[...], w_ref[...], preferred_element_type=F32)

    out, _ = _call(
        body, name="proj_b", grid=(s // ta, 4),
        in_specs=[pl.BlockSpec((ta, d), lambda i, j: (i, 0)), pl.BlockSpec((None, d, half), lambda i, j: (j, 0, 0))],
        out_specs=[pl.BlockSpec((ta, half), lambda i, j: (i, jnp.minimum(j, 1))),
                   pl.BlockSpec((ta, half), lambda i, j: (i, jnp.maximum(j - 2, 0)))],
        out_shape=[jax.ShapeDtypeStruct((s, d), BF16), jax.ShapeDtypeStruct((s, d), F32)],
        args=[n2, w], sem=("arbitrary", "arbitrary"))
    return out


def _rms_rows(x):
    return lax.rsqrt(jnp.mean(x * x, axis=-1, keepdims=True) + RMS_EPS)


def _norm_fwd(name, x, gains):
    s, d = x.shape
    n = gains.shape[0]

    def body(x_ref, g_ref, *o_refs):
        xv = x_ref[...]
        xh = xv * _rms_rows(xv)
        for i in range(n):
            o_refs[i][...] = (xh * g_ref[i:i + 1, :]).astype(BF16)

    row = pl.BlockSpec((TM, d), lambda i: (i, 0))
    return pl.pallas_call(
        body, name=name, grid=(s // TM,),
        in_specs=[row, pl.BlockSpec((n, d), lambda i: (0, 0))],
        out_specs=[row] * n,
        out_shape=[jax.ShapeDtypeStruct((s, d), BF16)] * n,
        compiler_params=_params(("parallel",)),
    )(x, gains)


def _proj_norm_bwd(name, x, dres, gains, branches):
    s, d = x.shape
    n = len(branches)
    n_ab = 2 * sum(len(bs) for _, bs in branches)
    tm = min(TM_PARTS, s)

    def body(x_ref, r_ref, g_ref, *refs):
        ab_refs, dx_ref, dg_ref = list(refs[:n_ab]), refs[n_ab], refs[n_ab + 1]
        i = pl.program_id(0)
        xv = x_ref[...]
        r = _rms_rows(xv)
        xh = xv * r

        @pl.when(i == 0)
        def _():
            dg_ref[...] = jnp.zeros_like(dg_ref)

        a = None
        for j in range(n):
            dn = None
            for _ in branches[j][1]:
                a_ref, b_ref = ab_refs.pop(0), ab_refs.pop(0)
                for part in range(a_ref.shape[0]):
                    term = lax.dot_general(a_ref[part], b_ref[part], NT, preferred_element_type=F32)
                    dn = term if dn is None else dn + term
            t = dn * g_ref[j:j + 1, :]
            a = t if a is None else a + t
            dg_ref[j:j + 1, :] += jnp.sum(dn * xh, axis=0, keepdims=True)
        dx_ref[...] = r_ref[...] + r * (a - xh * jnp.mean(xh * a, axis=-1, keepdims=True))

    row = pl.BlockSpec((tm, d), lambda i: (i, 0))
    small = pl.BlockSpec((n, d), lambda i: (0, 0))
    ab_specs, ab_args = [], []
    for a, bs in branches:
        for k, b in enumerate(bs):
            ab_specs += [pl.BlockSpec((a.shape[0], tm, b.shape[2]), lambda i, k=k: (0, i, k)),
                         pl.BlockSpec(b.shape, lambda i: (0, 0, 0))]
            ab_args += [a, b]
    return pl.pallas_call(
        body, name=name, grid=(s // tm,),
        in_specs=[row, row, small] + ab_specs,
        out_specs=[row, small],
        out_shape=[jax.ShapeDtypeStruct((s, d), F32), jax.ShapeDtypeStruct((n, d), F32)],
        compiler_params=_params(("arbitrary",)),
    )(x, dres, gains, *ab_args)


def _out_norms(name, u, w_out, resid, gains):
    s, d = resid.shape
    n = gains.shape[0]
    tm = min(TM, s)

    def body(u_ref, w_ref, r_ref, g_ref, h_ref, *o_refs):
        hv = r_ref[...] + jnp.dot(u_ref[...], w_ref[...], preferred_element_type=F32)
        h_ref[...] = hv
        hh = hv * _rms_rows(hv)
        for i in range(n):
            o_refs[i][...] = (hh * g_ref[i:i + 1, :]).astype(BF16)

    row = pl.BlockSpec((tm, d), lambda i: (i, 0))
    return pl.pallas_call(
        body, name=name, grid=(s // tm,),
        in_specs=[row, pl.BlockSpec((d, d), lambda i: (0, 0)), row, pl.BlockSpec((n, d), lambda i: (0, 0))],
        out_specs=[row] * (n + 1),
        out_shape=[jax.ShapeDtypeStruct((s, d), F32)] + [jax.ShapeDtypeStruct((s, d), BF16)] * n,
        compiler_params=_params(("parallel",)),
    )(u, w_out, resid, gains)


def _out_loss_head(u, w_out, resid, target, gain):
    s, d = resid.shape
    tm = min(TM_PARTS, s)

    def body(u_ref, w_ref, r_ref, t_ref, g_ref, dh_ref, loss_ref, dg_ref):
        i = pl.program_id(0)
        hv = r_ref[...] + jnp.dot(u_ref[...], w_ref[...], preferred_element_type=F32)
        r = _rms_rows(hv)
        hh = hv * r
        g = g_ref[...]
        err = hh * g - t_ref[...]
        part = 0.5 * jnp.sum(jnp.sum(err * err, axis=-1, keepdims=True) * (1.0 / d), axis=0, keepdims=True)
        dy = err * (1.0 / d)
        a = dy * g
        dh_ref[...] = r * (a - hh * jnp.mean(hh * a, axis=-1, keepdims=True))
        dg = jnp.sum(dy * hh, axis=0, keepdims=True)

        @pl.when(i == 0)
        def _():
            loss_ref[...] = part
            dg_ref[...] = dg

        @pl.when(i > 0)
        def _():
            loss_ref[...] += part
            dg_ref[...] += dg

    row = pl.BlockSpec((tm, d), lambda i: (i, 0))
    return pl.pallas_call(
        body, name="out_b_loss_head", grid=(s // tm,),
        in_specs=[row, pl.BlockSpec((d, d), lambda i: (0, 0)), row, row, pl.BlockSpec((1, d), lambda i: (0, 0))],
        out_specs=[row, pl.BlockSpec((1, 1), lambda i: (0, 0)), pl.BlockSpec((1, d), lambda i: (0, 0))],
        out_shape=[jax.ShapeDtypeStruct((s, d), F32), jax.ShapeDtypeStruct((1, 1), F32),
                   jax.ShapeDtypeStruct((1, d), F32)],
        compiler_params=_params(("arbitrary",)),
    )(u, w_out, resid, target, gain)


def _silu_parts(g):
    sig = jax.nn.sigmoid(g)
    return g * sig, sig * (1.0 + g * (1.0 - sig))


def _lane_lo(rows):
    return lax.broadcasted_iota(jnp.int32, (rows, LANES), 1) < HEAD_DIM


def _stack_pair(x):
    lo = _lane_lo(x.shape[0])
    zero = jnp.zeros_like(x)
    return jnp.concatenate([jnp.where(lo, x, zero), jnp.where(lo, zero, x)], axis=0)


def _unstack_pair(y, w):
    return jnp.where(_lane_lo(w), y[:w], y[w:])


def _block_valid(b, left_blocks, width):
    col = lax.broadcasted_iota(jnp.int32, (1, 2 * width), 1)
    col = jnp.where(col >= width, col - width, col)
    return (col // KB + (b - left_blocks)) >= 0


def _toeplitz_tile(diag_row, width, left_chunks):
    wide = width + TQ
    rolled = pltpu.roll(jnp.broadcast_to(diag_row, (TQ, wide)), 1, 1, stride=1, stride_axis=0)
    i = lax.broadcasted_iota(jnp.int32, (TQ, width), 0) // CHUNK
    j = lax.broadcasted_iota(jnp.int32, (TQ, width), 1) // CHUNK
    dc = i + left_chunks - j
    return jnp.where((dc >= 0) & (dc <= left_chunks), rolled[:, TQ:], MASKED)


def _toeplitz_sum(tile, width):
    flip = (lax.broadcasted_iota(jnp.int32, (TQ, TQ), 0) + lax.broadcasted_iota(jnp.int32, (TQ, TQ), 1)
            == TQ - 1).astype(F32)
    reversed_rows = jnp.dot(flip, tile, precision=lax.Precision.HIGHEST, preferred_element_type=F32)
    padded = jnp.concatenate([reversed_rows, jnp.zeros((TQ, TQ), F32)], axis=1)
    rolled = pltpu.roll(padded, 0, 1, stride=1, stride_axis=0)
    return jnp.sum(rolled, axis=0, keepdims=True)


def _softmax_pair(sc, w, sink=None):
    ps, inv, lses = [], [], []
    for e in range(2):
        sh = sc[:, e * w:(e + 1) * w]
        m = jnp.max(sh, axis=-1, keepdims=True)
        if sink is not None:
            m = jnp.maximum(m, sink[e])
        ex = jnp.exp(sh - m)
        l = jnp.sum(ex, axis=-1, keepdims=True)
        if sink is not None:
            l = l + jnp.exp(sink[e] - m)
        ps.append(ex.astype(BF16))
        inv.append(1.0 / l)
        lses.append(m + jnp.log(l))
    return jnp.concatenate(ps, axis=-1), inv, lses


def _softmax_pair_bwd(sc, dp, lse, delta, w):
    ps, dss = [], []
    for e in range(2):
        p = jnp.exp(sc[:, e * w:(e + 1) * w] - lse[e])
        ps.append(p)
        dss.append(p * (dp[:, e * w:(e + 1) * w] - delta[e]))
    return jnp.concatenate(ps, axis=-1), jnp.concatenate(dss, axis=-1)


def _pair_rowsums(x, lo):
    zero = jnp.zeros_like(x)
    return (jnp.sum(jnp.where(lo, x, zero), axis=-1, keepdims=True),
            jnp.sum(jnp.where(lo, zero, x), axis=-1, keepdims=True))


def _a_qkv_specs(rows, pad, pw):
    return [pl.BlockSpec((None, TQ, pw), lambda p, b: (0, b + pad // TQ, p)),
            pl.BlockSpec((None, rows, pw), lambda p, b: (1, 0, p)),
            pl.BlockSpec((None, rows, pw), lambda p, b: (2, 0, p))]


def _window(ref, b, pad, win, lanes):
    start = pl.multiple_of(b * TQ + pad - (win - TQ), KB)
    return ref[pl.ds(start, win), lanes]


def _attn_a_fwd(zqkv, g, diag, hosted=None):
    s = g.shape[0]
    pad = zqkv.shape[1] - s
    nb = s // TQ
    left = A_KBLOCKS - 1
    pairs = A_PAIRS_FWD
    pw = pairs * LANES
    wide = A_WIN + TQ

    def body(q_ref, k_ref, v_ref, g_ref, diag_ref, o_ref, u_ref, lse_ref, bias_scr):
        b = pl.program_id(1)

        @pl.when(b == 0)
        def _():
            for hh in range(2 * pairs):
                bias_scr[hh // 2, :, (hh % 2) * A_WIN:(hh % 2 + 1) * A_WIN] = _toeplitz_tile(
                    diag_ref[hh], A_WIN, A_LEFT_CHUNKS)

        def step(first_blocks):
            lo = _lane_lo(TQ)
            for pp in range(pairs):
                ln = slice(pp * LANES, (pp + 1) * LANES)
                kcat = _stack_pair(_window(k_ref, b, pad, A_WIN, ln))
                vcat = _stack_pair(_window(v_ref, b, pad, A_WIN, ln))
                sc = lax.dot_general(q_ref[:, ln] * SCALE, kcat, NT, preferred_element_type=F32) + bias_scr[pp]
                if first_blocks:
                    sc = jnp.where(_block_valid(b, left, A_WIN), sc, MASKED)
                p, inv, lses = _softmax_pair(sc, A_WIN)
                ov = jnp.dot(p, vcat, preferred_element_type=F32) * jnp.where(lo, inv[0], inv[1])
                o_ref[:, ln] = ov
                lse_ref[pp] = jnp.where(lo, lses[0], lses[1])
                sg, _ = _silu_parts(g_ref[:, ln])
                u_ref[:, ln] = (ov * sg).astype(BF16)

        @pl.when(b < left)
        def _():
            step(True)

        @pl.when(b >= left)
        def _():
            step(False)

    tile = pl.BlockSpec((TQ, pw), lambda p, b: (b, p))
    return _call(
        body, name="attn_a_fwd", grid=(HEADS // 2 // pairs, nb),
        in_specs=_a_qkv_specs(pad + s, pad, pw) + [
            tile, pl.BlockSpec((2 * pairs, 1, wide), lambda p, b: (p, 0, 0))],
        out_specs=[tile, tile, pl.BlockSpec((pairs, TQ, LANES), lambda p, b: (p, b, 0))],
        out_shape=[jax.ShapeDtypeStruct((s, D_MODEL), F32), jax.ShapeDtypeStruct((s, D_MODEL), BF16),
                   jax.ShapeDtypeStruct((HEADS // 2, s, LANES), F32)],
        scratch_shapes=[pltpu.VMEM((pairs, TQ, 2 * A_WIN), F32)],
        sem=("parallel", "arbitrary"), hosted=hosted,
        args=(zqkv, zqkv, zqkv, g, diag))


def _attn_a_bwd(zqkv, g, o, du, lse, diag, hosted=None):
    s = g.shape[0]
    pad = zqkv.shape[1] - s
    nb = s // TQ
    left = A_KBLOCKS - 1
    pw = A_PAIRS * LANES
    wide = A_WIN + TQ

    def body(q_ref, k_ref, v_ref, g_ref, o_ref, du_ref, lse_ref, diag_ref, dz_ref, ddiag_ref,
             bias_scr, dbias_acc, dk_acc, dv_acc):
        b = pl.program_id(1)

        @pl.when(b == 0)
        def _():
            for hh in range(2 * A_PAIRS):
                bias_scr[hh // 2, :, (hh % 2) * A_WIN:(hh % 2 + 1) * A_WIN] = _toeplitz_tile(
                    diag_ref[hh], A_WIN, A_LEFT_CHUNKS)
            dbias_acc[...] = jnp.zeros_like(dbias_acc)
            dk_acc[...] = jnp.zeros_like(dk_acc)
            dv_acc[...] = jnp.zeros_like(dv_acc)

        def step(first_blocks):
            lo = _lane_lo(TQ)
            rows = pl.ds(pl.multiple_of(b * TQ, TQ), TQ)
            sg, dsg = _silu_parts(g_ref[...])
            duv = du_ref[...]
            ov = o_ref[...]
            do = duv * sg
            dz_ref[3, rows, :] = (duv * ov * dsg).astype(BF16)
            do_o = do * ov
            do_bf = do.astype(BF16)
            for pp in range(A_PAIRS):
                ln = slice(pp * LANES, (pp + 1) * LANES)
                q = q_ref[:, ln] * SCALE
                kcat = _stack_pair(_window(k_ref, b, pad, A_WIN, ln))
                vcat = _stack_pair(_window(v_ref, b, pad, A_WIN, ln))
                sc = lax.dot_general(q, kcat, NT, preferred_element_type=F32) + bias_scr[pp]
                if first_blocks:
                    sc = jnp.where(_block_valid(b, left, A_WIN), sc, MASKED)
                lse_t = lse_ref[pp]
                dp = lax.dot_general(do_bf[:, ln], vcat, NT, preferred_element_type=F32)
                p, ds = _softmax_pair_bwd(sc, dp, (lse_t[:, 0:1], lse_t[:, HEAD_DIM:HEAD_DIM + 1]),
                                          _pair_rowsums(do_o[:, ln], lo), A_WIN)
                dbias_acc[pp] += ds
                dsb = ds.astype(BF16)
                dz_ref[0, rows, ln] = (jnp.dot(dsb, kcat, preferred_element_type=F32) * SCALE).astype(BF16)
                pb = p.astype(BF16)
                dob = do_bf[:, ln]
                dkt = jnp.concatenate([
                    lax.dot_general(q[:, e * HEAD_DIM:(e + 1) * HEAD_DIM], dsb[:, e * A_WIN:(e + 1) * A_WIN], TN,
                                    preferred_element_type=F32) for e in range(2)], axis=0)
                dvt = jnp.concatenate([
                    lax.dot_general(dob[:, e * HEAD_DIM:(e + 1) * HEAD_DIM], pb[:, e * A_WIN:(e + 1) * A_WIN], TN,
                                    preferred_element_type=F32) for e in range(2)], axis=0)
                for t in range(A_KBLOCKS):
                    blk = b + (pad // KB - left + t)
                    dk_acc[blk, ln, :] += dkt[:, t * KB:(t + 1) * KB]
                    dv_acc[blk, ln, :] += dvt[:, t * KB:(t + 1) * KB]

        @pl.when(b < left)
        def _():
            step(True)

        @pl.when(b >= left)
        def _():
            step(False)

        @pl.when(b == nb - 1)
        def _():
            for kb in range(s // KB):
                dz_ref[1, kb * KB:(kb + 1) * KB, :] = dk_acc[pad // KB + kb].T.astype(BF16)
                dz_ref[2, kb * KB:(kb + 1) * KB, :] = dv_acc[pad // KB + kb].T.astype(BF16)
            for hh in range(2 * A_PAIRS):
                ddiag_ref[hh] = _toeplitz_sum(
                    dbias_acc[hh // 2, :, (hh % 2) * A_WIN:(hh % 2 + 1) * A_WIN], A_WIN)

    tile = pl.BlockSpec((TQ, pw), lambda p, b: (b, p))
    diag_spec = pl.BlockSpec((2 * A_PAIRS, 1, wide), lambda p, b: (p, 0, 0))
    return _call(
        body, name="attn_a_bwd", grid=(HEADS // 2 // A_PAIRS, nb),
        in_specs=_a_qkv_specs(pad + s, pad, pw) + [
            tile, tile, tile, pl.BlockSpec((A_PAIRS, TQ, LANES), lambda p, b: (p, b, 0)), diag_spec],
        out_specs=[pl.BlockSpec((4, s, pw), lambda p, b: (0, 0, p)), diag_spec],
        out_shape=[jax.ShapeDtypeStruct((4, s, D_MODEL), BF16),
                   jax.ShapeDtypeStruct((HEADS, 1, wide), F32)],
        scratch_shapes=[pltpu.VMEM((A_PAIRS, TQ, 2 * A_WIN), F32), pltpu.VMEM((A_PAIRS, TQ, 2 * A_WIN), F32),
                        pltpu.VMEM(((pad + s) // KB, pw, KB), F32), pltpu.VMEM(((pad + s) // KB, pw, KB), F32)],
        sem=("parallel", "arbitrary"), hosted=hosted,
        args=(zqkv, zqkv, zqkv, g, o, du, lse, diag))


B_STACK = B_GROUP // 2
B_KVX = 4 * LANES
B_ROWS = B_STACK * TQ
B_WIDE = B_WIN + TQ


def _b_head_place(h):
    return h // B_GROUP, (h % B_GROUP) // 2, h % 2


def _toeplitz_tile_t(base_row, width, left_chunks):
    wide = width + TQ
    rolled = pltpu.roll(jnp.broadcast_to(base_row, (width, wide)), 0, 1, stride=1, stride_axis=0)
    j = lax.broadcasted_iota(jnp.int32, (width, TQ), 0) // CHUNK
    i = lax.broadcasted_iota(jnp.int32, (width, TQ), 1) // CHUNK
    dc = i + left_chunks - j
    return jnp.where((dc >= 0) & (dc <= left_chunks), rolled[:, :TQ], MASKED)


def _toeplitz_sum_t(tile_t, width):
    flip = (lax.broadcasted_iota(jnp.int32, (width, width), 0) + lax.broadcasted_iota(jnp.int32, (width, width), 1)
            == width - 1).astype(F32)
    reversed_rows = jnp.dot(flip, tile_t, precision=lax.Precision.HIGHEST, preferred_element_type=F32)
    padded = jnp.concatenate([reversed_rows, jnp.zeros((width, width), F32)], axis=1)
    rolled = pltpu.roll(padded, 0, 1, stride=1, stride_axis=0)
    return jnp.sum(rolled, axis=0, keepdims=True)


def _b_build_bias(base_ref, bias_scr):
    for h in range(HEADS):
        gi, pr, e = _b_head_place(h)
        bias_scr[gi, e * B_WIN:(e + 1) * B_WIN, pr * TQ:(pr + 1) * TQ] = _toeplitz_tile_t(
            base_ref[h], B_WIN, B_LEFT_CHUNKS)


def _b_stack(x, gi):
    return jnp.concatenate(
        [x[:, (B_STACK * gi + pr) * LANES:(B_STACK * gi + pr + 1) * LANES] for pr in range(B_STACK)], axis=0)


def _b_sink_rows(sink_ref, gi):
    block = lax.broadcasted_iota(jnp.int32, (1, B_ROWS), 1) // TQ
    rows = []
    for e in range(2):
        row = jnp.zeros((1, B_ROWS), F32)
        for pr in range(B_STACK):
            h = B_GROUP * gi + 2 * pr + e
            row = jnp.where(block == pr, sink_ref[0:1, h:h + 1], row)
        rows.append(row)
    return rows


def _b_scores_t(q_ref, kvv, bias_scr, gi, b, left, first_blocks):
    kcat = _stack_pair(kvv[:, gi * LANES:(gi + 1) * LANES])
    vcat = _stack_pair(kvv[:, (B_KV_HEADS + gi) * LANES:(B_KV_HEADS + gi + 1) * LANES])
    qs = _b_stack(q_ref, gi) * SCALE
    sc = lax.dot_general(kcat, qs, NT, preferred_element_type=F32) + bias_scr[gi]
    if first_blocks:
        row = lax.broadcasted_iota(jnp.int32, (2 * B_WIN, 1), 0)
        row = jnp.where(row >= B_WIN, row - B_WIN, row)
        sc = jnp.where((row // KB + (b - left)) >= 0, sc, MASKED)
    return kcat, vcat, qs, sc


def _attn_b_fwd(qb, kvx, gate, base, sinks):
    s = qb.shape[0]
    pad = kvx.shape[0] - s
    nb = s // TQ
    left = B_KBLOCKS - 1

    def body(q_ref, kv_ref, g_ref, base_ref, sink_ref, o_ref, u_ref, lse_ref, bias_scr):
        b = pl.program_id(0)

        @pl.when(b == 0)
        def _():
            _b_build_bias(base_ref, bias_scr)

        def step(first_blocks):
            kvv = _window(kv_ref, b, pad, B_WIN, slice(None))
            upper = lax.broadcasted_iota(jnp.int32, (LANES, B_ROWS), 0) < HEAD_DIM
            lse_rows = []
            for gi in range(B_KV_HEADS):
                kcat, vcat, qs, sc = _b_scores_t(q_ref, kvv, bias_scr, gi, b, left, first_blocks)
                sink = _b_sink_rows(sink_ref, gi)
                ps, inv = [], []
                for e in range(2):
                    sh = sc[e * B_WIN:(e + 1) * B_WIN]
                    m = jnp.maximum(jnp.max(sh, axis=0, keepdims=True), sink[e])
                    ex = jnp.exp(sh - m)
                    l = jnp.sum(ex, axis=0, keepdims=True) + jnp.exp(sink[e] - m)
                    ps.append(ex.astype(BF16))
                    inv.append(1.0 / l)
                    lse_rows.append(m + jnp.log(l))
                pt = jnp.concatenate(ps, axis=0)
                ot = lax.dot_general(vcat, pt, TN, preferred_element_type=F32) * jnp.where(upper, inv[0], inv[1])
                ov = ot.T
                for pr in range(B_STACK):
                    pair = B_STACK * gi + pr
                    o_ref[:, pair * LANES:(pair + 1) * LANES] = ov[pr * TQ:(pr + 1) * TQ]
            lse_ref[0] = jnp.concatenate(lse_rows + [jnp.zeros((8 - len(lse_rows), B_ROWS), F32)], axis=0)
            sg, _ = _silu_parts(g_ref[...])
            u_ref[...] = (o_ref[...] * sg).astype(BF16)

        @pl.when(b < left)
        def _():
            step(True)

        @pl.when(b >= left)
        def _():
            step(False)

    row = pl.BlockSpec((TQ, D_MODEL), lambda b: (b, 0))
    return pl.pallas_call(
        body, name="attn_b_fwd", grid=(nb,),
        in_specs=[row, pl.BlockSpec((pad + s, B_KVX), lambda b: (0, 0)), row,
                  pl.BlockSpec((HEADS, 1, B_WIDE), lambda b: (0, 0, 0)), pl.BlockSpec((1, HEADS), lambda b: (0, 0))],
        out_specs=[row, row, pl.BlockSpec((1, 8, B_ROWS), lambda b: (b, 0, 0))],
        out_shape=[jax.ShapeDtypeStruct((s, D_MODEL), F32), jax.ShapeDtypeStruct((s, D_MODEL), BF16),
                   jax.ShapeDtypeStruct((nb, 8, B_ROWS), F32)],
        scratch_shapes=[pltpu.VMEM((B_KV_HEADS, 2 * B_WIN, B_ROWS), F32)],
        compiler_params=_params(("arbitrary",)),
    )(qb, kvx, gate, base, sinks)


def _attn_b_bwd(qb, kvx, gate, o, du, lse, base, sinks):
    s = qb.shape[0]
    pad = kvx.shape[0] - s
    nb = s // TQ
    left = B_KBLOCKS - 1
    half = D_MODEL // 2

    def body(q_ref, kv_ref, g_ref, o_ref, du_ref, lse_ref, base_ref, sink_ref, dz_ref, dkv_ref, dsum_ref,
             dsink_ref, bias_scr, dbias_acc, dkv_acc, dsink_acc):
        b = pl.program_id(0)

        @pl.when(b == 0)
        def _():
            _b_build_bias(base_ref, bias_scr)
            dbias_acc[...] = jnp.zeros_like(dbias_acc)
            dkv_acc[...] = jnp.zeros_like(dkv_acc)
            dsink_acc[...] = jnp.zeros_like(dsink_acc)

        def step(first_blocks):
            kvv = _window(kv_ref, b, pad, B_WIN, slice(None))
            sg, dsg = _silu_parts(g_ref[...])
            duv = du_ref[...]
            ov = o_ref[...]
            do = duv * sg
            dgate = (duv * ov * dsg).astype(BF16)
            dz_ref[2] = dgate[:, :half]
            dz_ref[3] = dgate[:, half:]
            do_o = do * ov
            do_bf = do.astype(BF16)
            lse_all = lse_ref[0]
            dsink_rows = []
            for gi in range(B_KV_HEADS):
                kcat, vcat, qs, sc = _b_scores_t(q_ref, kvv, bias_scr, gi, b, left, first_blocks)
                dos = _b_stack(do_bf, gi)
                doo_t = _b_stack(do_o, gi).T
                delta = (jnp.sum(doo_t[:HEAD_DIM], axis=0, keepdims=True),
                         jnp.sum(doo_t[HEAD_DIM:], axis=0, keepdims=True))
                sink = _b_sink_rows(sink_ref, gi)
                dp = lax.dot_general(vcat, dos, NT, preferred_element_type=F32)
                ps, dss = [], []
                for e in range(2):
                    lse_e = lse_all[2 * gi + e:2 * gi + e + 1]
                    delta_e = delta[e]
                    p = jnp.exp(sc[e * B_WIN:(e + 1) * B_WIN] - lse_e)
                    ps.append(p.astype(BF16))
                    dss.append(p * (dp[e * B_WIN:(e + 1) * B_WIN] - delta_e))
                    dsink_rows.append(-jnp.exp(sink[e] - lse_e) * delta_e)
                ds = jnp.concatenate(dss, axis=0)
                dbias_acc[gi] += ds
                dsb = ds.astype(BF16)
                dq = (lax.dot_general(kcat, dsb, TN, preferred_element_type=F32) * SCALE).T.astype(BF16)
                for pr in range(B_STACK):
                    dz_ref[gi, :, pr * LANES:(pr + 1) * LANES] = dq[pr * TQ:(pr + 1) * TQ]
                dk = _unstack_pair(jnp.dot(dsb, qs, preferred_element_type=F32), B_WIN)
                dv = _unstack_pair(jnp.dot(jnp.concatenate(ps, axis=0), dos, preferred_element_type=F32), B_WIN)
                krows = pl.ds(pl.multiple_of(b * TQ + pad - (B_WIN - TQ), KB), B_WIN)
                dkv_acc[krows, gi * LANES:(gi + 1) * LANES] += dk
                dkv_acc[krows, (B_KV_HEADS + gi) * LANES:(B_KV_HEADS + gi + 1) * LANES] += dv
            dsink_acc[...] += jnp.concatenate(
                dsink_rows + [jnp.zeros((8 - len(dsink_rows), B_ROWS), F32)], axis=0)

        @pl.when(b < left)
        def _():
            step(True)

        @pl.when(b >= left)
        def _():
            step(False)

        @pl.when(b == nb - 1)
        def _():
            lo_s = _lane_lo(s)
            for which in range(2):
                folded = []
                for gi in range(B_KV_HEADS):
                    part = dkv_acc[pad:pad + s, (which * B_KV_HEADS + gi) * LANES:(which * B_KV_HEADS + gi + 1) * LANES]
                    folded.append(part + pltpu.roll(part, HEAD_DIM, 1))
                dkv_ref[:, which * LANES:(which + 1) * LANES] = jnp.where(lo_s, folded[0], folded[1]).astype(BF16)
            lane8 = lax.broadcasted_iota(jnp.int32, dsink_ref.shape, 1)
            tot = jnp.zeros(dsink_ref.shape, F32)
            for h in range(HEADS):
                gi, pr, e = _b_head_place(h)
                dsum_ref[h] = _toeplitz_sum_t(
                    dbias_acc[gi, e * B_WIN:(e + 1) * B_WIN, pr * TQ:(pr + 1) * TQ], B_WIN)
                per_query = dsink_acc[2 * gi + e:2 * gi + e + 1, pr * TQ:(pr + 1) * TQ]
                tot = jnp.where(lane8 == h, jnp.sum(per_query, axis=1, keepdims=True), tot)
            dsink_ref[...] = tot

    row = pl.BlockSpec((TQ, D_MODEL), lambda b: (b, 0))
    base_spec = pl.BlockSpec((HEADS, 1, B_WIDE), lambda b: (0, 0, 0))
    return pl.pallas_call(
        body, name="attn_b_bwd", grid=(nb,),
        in_specs=[row, pl.BlockSpec((pad + s, B_KVX), lambda b: (0, 0)), row, row, row,
                  pl.BlockSpec((1, 8, B_ROWS), lambda b: (b, 0, 0)), base_spec,
                  pl.BlockSpec((1, HEADS), lambda b: (0, 0))],
        out_specs=[pl.BlockSpec((4, TQ, half), lambda b: (0, b, 0)),
                   pl.BlockSpec((s, 2 * LANES), lambda b: (0, 0)), base_spec,
                   pl.BlockSpec((8, LANES), lambda b: (0, 0))],
        out_shape=[jax.ShapeDtypeStruct((4, s, half), BF16), jax.ShapeDtypeStruct((s, 2 * LANES), BF16),
                   jax.ShapeDtypeStruct((HEADS, 1, B_WIDE), F32), jax.ShapeDtypeStruct((8, LANES), F32)],
        scratch_shapes=[pltpu.VMEM((B_KV_HEADS, 2 * B_WIN, B_ROWS), F32),
                        pltpu.VMEM((B_KV_HEADS, 2 * B_WIN, B_ROWS), F32),
                        pltpu.VMEM((pad + s, B_KVX), F32), pltpu.VMEM((8, B_ROWS), F32)],
        compiler_params=_params(("arbitrary",)),
    )(qb, kvx, gate, o, du, lse, base, sinks)


def _t5_bucket(rel):
    nb = T5_BUCKETS // 2
    max_exact = nb // 2
    ret = jnp.where(rel > 0, nb, 0)
    n = jnp.abs(rel)
    nf = jnp.maximum(n, 1).astype(jnp.float32)
    large = max_exact + (jnp.log(nf / max_exact) / math.log(T5_MAX_DIST / max_exact)
                         * (nb - max_exact)).astype(jnp.int32)
    large = jnp.minimum(large, nb - 1)
    return ret + jnp.where(n < max_exact, n, large)


def _a_offset_onehot():
    c = np.arange(A_WIN + TQ)
    dist = A_LEFT_CHUNKS * CHUNK + TQ - 1 - c
    idx = np.clip(dist, -A_REL_CLIP, A_REL_CLIP) + A_REL_CLIP
    onehot = np.zeros((A_WIN + TQ, 2 * A_REL_CLIP + 1), np.float32)
    onehot[c, idx] = 1.0
    return jnp.asarray(onehot)


def _b_offset_onehot():
    c = jnp.arange(B_WIN + TQ, dtype=jnp.int32)
    rel = c - (TQ - 1) - B_LEFT_CHUNKS * CHUNK
    return (_t5_bucket(rel)[:, None] == jnp.arange(T5_BUCKETS)[None, :]).astype(F32)


def _diag_rows(onehot, table):
    rows = jnp.dot(onehot, table.astype(F32), precision=lax.Precision.HIGHEST)
    return rows.T.reshape(HEADS, 1, onehot.shape[0])


def _diag_rows_grad(onehot, ddiag):
    return jnp.dot(ddiag.reshape(HEADS, onehot.shape[0]), onehot, precision=lax.Precision.HIGHEST)


def _position():
    x, y, c = lax.axis_index("x"), lax.axis_index("y"), lax.axis_index("c")
    chips = [(1 - x, y), (x, 1 - y), (1 - x, 1 - y)]
    return x, y, c, chips


ANY = pl.BlockSpec(memory_space=pl.ANY)


def _allgather_routed(shards):
    n = len(shards)

    def piece(block_ref, t, c, quarter=None):
        half = shards[t].shape[0] // 2
        if quarter is None:
            return block_ref.at[pl.ds(c * half, half)]
        return block_ref.at[pl.ds(c * half + quarter * (half // 2), half // 2)]

    def copies(kind, ins, outs, sems):
        ici_send, ici_recv, pass_send, pass_recv, local_sems = sems
        x, y, c, chips = _position()
        mine = 2 * x + y
        if kind == "local":
            return [pltpu.make_async_copy(ins[t], outs[t].at[mine], local_sems.at[t]) for t in range(n)]
        ids = [2 * chip[0] + chip[1] for chip in chips]
        made = []
        for t in range(n):
            def ici(k, to):
                return dict(send_sem=ici_send.at[4 * t + k], recv_sem=ici_recv.at[4 * t + k],
                            device_id=(chips[to][0], chips[to][1], c), device_id_type=MESH)

            def d2d(k):
                return dict(send_sem=pass_send.at[4 * t + k], recv_sem=pass_recv.at[4 * t + k],
                            device_id=(x, y, 1 - c), device_id_type=MESH)

            def same(ref, where):
                return pltpu.make_async_remote_copy(src_ref=ref, dst_ref=ref, **where)

            if kind == "send":
                for k in range(2):
                    made.append(pltpu.make_async_remote_copy(
                        src_ref=piece(ins[t], t, c), dst_ref=piece(outs[t].at[mine], t, c), **ici(k, k)))
            elif kind == "landed":
                made += [same(piece(outs[t].at[ids[k]], t, c), ici(k, k)) for k in range(2)]
            elif kind == "forward":
                made.append(same(piece(outs[t].at[ids[0]], t, c, 0), ici(2, 1)))
                made.append(same(piece(outs[t].at[ids[1]], t, c, 1), ici(3, 0)))
            elif kind == "arrived":
                made.append(same(piece(outs[t].at[ids[2]], t, c, 0), ici(2, 1)))
                made.append(same(piece(outs[t].at[ids[2]], t, c, 1), ici(3, 0)))
            else:
                core = 1 - c if kind == "passed" else c
                if kind in ("pass halves", "passed"):
                    made += [same(piece(outs[t].at[ids[k]], t, core), d2d(k)) for k in range(2)]
                if kind in ("pass quarters", "passed"):
                    made += [same(piece(outs[t].at[ids[2]], t, core, k), d2d(2 + k)) for k in range(2)]
        return made

    def first(ins, outs, sems):
        for cp in copies("local", ins, outs, sems) + copies("send", ins, outs, sems):
            cp.start()

    def middle(ins, outs, sems):
        for got, onward, near in zip(copies("landed", ins, outs, sems), copies("forward", ins, outs, sems),
                                     copies("pass halves", ins, outs, sems)):
            got.wait_recv()
            near.start()
            onward.start()

    def last(ins, outs, sems):
        quarters = copies("pass quarters", ins, outs, sems)
        for got, near in zip(copies("arrived", ins, outs, sems), quarters):
            got.wait_recv()
            near.start()
        for cp in copies("passed", ins, outs, sems):
            cp.wait_recv()
        for cp in (copies("send", ins, outs, sems) + copies("forward", ins, outs, sems)
                   + copies("pass halves", ins, outs, sems) + quarters):
            cp.wait_send()
        for cp in copies("local", ins, outs, sems):
            cp.wait()

    return _Hosted(shards, [jax.ShapeDtypeStruct((4,) + w.shape, w.dtype) for w in shards],
                   [pltpu.SemaphoreType.DMA((4 * n,))] * 4 + [pltpu.SemaphoreType.DMA((n,))],
                   first, middle, last)


def _scatter_hosted(grads):
    n = len(grads)

    def copies(ins, outs, sems):
        send_sems, recv_sems = sems
        x, y, c, chips = _position()
        return [pltpu.make_async_remote_copy(
            src_ref=ins[t].at[2 * chip[0] + chip[1]], dst_ref=outs[t].at[j],
            send_sem=send_sems.at[3 * t + j], recv_sem=recv_sems.at[3 * t + j],
            device_id=(chip[0], chip[1], c), device_id_type=MESH)
            for t in range(n) for j, chip in enumerate(chips)]

    def first(ins, outs, sems):
        for cp in copies(ins, outs, sems):
            cp.start()

    def last(ins, outs, sems):
        for cp in copies(ins, outs, sems):
            cp.wait()

    return _Hosted(grads, [jax.ShapeDtypeStruct((3,) + g.shape[1:], g.dtype) for g in grads],
                   [pltpu.SemaphoreType.DMA((3 * n,))] * 2, first, None, last)


GATHER_PEERS = "x and y neighbours (same core) and the sibling core"
SCATTER_PEERS = "the same core of the three other chips"
EVERYONE = "the seven other devices"


def _run_on_sequencer(name, hosted, peers, collective_id):
    ins = [jax.new_ref(a, memory_space=pltpu.MemorySpace.HBM) for a in hosted.inputs]
    outs = [jax.empty_ref(shape, memory_space=pltpu.MemorySpace.HBM) for shape in hosted.out_shapes]

    @pl.kernel(mesh=plsc.ScalarSubcoreMesh(axis_name="sequencer", num_cores=1), name=name,
               scratch_types=tuple(hosted.sems), compiler_params=pltpu.CompilerParams(collective_id=collective_id))
    def launch(*sems):
        x, y, c, chips = _position()
        if peers == GATHER_PEERS:
            devices = [(chip[0], chip[1], c) for chip in chips[:2]] + [(x, y, 1 - c)]
        elif peers == SCATTER_PEERS:
            devices = [(chip[0], chip[1], c) for chip in chips]
        else:
            devices = [(x ^ (k >> 2), y ^ ((k >> 1) & 1), c ^ (k & 1)) for k in range(1, 8)]
        barrier = pltpu.get_barrier_semaphore()
        for device in devices:
            pl.semaphore_signal(barrier, inc=1, device_id=device, device_id_type=MESH)
        pl.semaphore_wait(barrier, len(devices))
        hosted.first(ins, outs, sems)
        if hosted.middle is not None:
            hosted.middle(ins, outs, sems)
        hosted.last(ins, outs, sems)

    launch()
    return [o[...] for o in outs]


def _gather_gain(shard):
    def body(in_ref, out_ref, send_sems, recv_sems):
        x, y, c, chips = _position()
        out_ref[2 * x + y] = in_ref[...]
        sends = [pltpu.make_async_remote_copy(
            src_ref=in_ref, dst_ref=out_ref.at[2 * x + y], send_sem=send_sems.at[j], recv_sem=recv_sems.at[j],
            device_id=(chip[0], chip[1], c), device_id_type=MESH) for j, chip in enumerate(chips)]
        for cp in sends:
            cp.start()
        for j, chip in enumerate(chips):
            pltpu.make_async_remote_copy(
                src_ref=in_ref, dst_ref=out_ref.at[2 * chip[0] + chip[1]], send_sem=send_sems.at[j],
                recv_sem=recv_sems.at[j], device_id=(chip[0], chip[1], c), device_id_type=MESH).wait_recv()
        for cp in sends:
            cp.wait_send()

    vmem = pl.BlockSpec(memory_space=pltpu.VMEM)
    return pl.pallas_call(
        body, name="gather_gain", in_specs=[vmem], out_specs=vmem,
        out_shape=jax.ShapeDtypeStruct((4,) + shard.shape, shard.dtype),
        scratch_shapes=[pltpu.SemaphoreType.DMA((3,))] * 2,
    )(shard)


def _swap_with_sibling(name, blocks):
    n = len(blocks)

    def body(*refs):
        ins, outs = refs[:n], refs[n:2 * n]
        send_sems, recv_sems = refs[2 * n:]
        x, y, c, _ = _position()
        sends = [pltpu.make_async_remote_copy(
            src_ref=ins[t], dst_ref=outs[t], send_sem=send_sems.at[t], recv_sem=recv_sems.at[t],
            device_id=(x, y, 1 - c), device_id_type=MESH) for t in range(n)]
        for cp in sends:
            cp.start()
        for cp in sends:
            cp.wait()

    return pl.pallas_call(
        body, name=name,
        in_specs=[ANY] * n, out_specs=[ANY] * n,
        out_shape=[jax.ShapeDtypeStruct(b.shape, b.dtype) for b in blocks],
        scratch_shapes=[pltpu.SemaphoreType.DMA((n,))] * 2,
    )(*blocks)


def _everyone_hosted(terms):
    nt = len(terms)

    def copies(kind, ins, outs, sems):
        send_sems, recv_sems, local_sems = sems
        x, y, c, _ = _position()
        me = 4 * x + 2 * y + c
        if kind == "local":
            return [pltpu.make_async_copy(ins[t], outs[t].at[me], local_sems.at[t]) for t in range(nt)]
        made = []
        for t in range(nt):
            for k in range(1, 8):
                peer = (x ^ (k >> 2), y ^ ((k >> 1) & 1), c ^ (k & 1))
                slot = me if kind == "send" else me ^ k
                made.append(pltpu.make_async_remote_copy(
                    src_ref=ins[t], dst_ref=outs[t].at[slot], send_sem=send_sems.at[7 * t + k - 1],
                    recv_sem=recv_sems.at[7 * t + k - 1], device_id=peer, device_id_type=MESH))
        return made

    def first(ins, outs, sems):
        for cp in copies("local", ins, outs, sems) + copies("send", ins, outs, sems):
            cp.start()

    def last(ins, outs, sems):
        for cp in copies("landed", ins, outs, sems):
            cp.wait_recv()
        for cp in copies("send", ins, outs, sems):
            cp.wait_send()
        for cp in copies("local", ins, outs, sems):
            cp.wait()

    return _Hosted(terms, [jax.ShapeDtypeStruct((8,) + a.shape, F32) for a in terms],
                   [pltpu.SemaphoreType.DMA((7 * nt,))] * 2 + [pltpu.SemaphoreType.DMA((nt,))], first, None, last)


def _small_step(partials, extras, ws, ms, vs, shard_of):
    n = len(partials)
    terms = list(partials) + list(extras)
    nt = len(terms)
    rows = [t for t in range(nt) if terms[t].shape[0] == 1]
    mats = [t for t in range(nt) if terms[t].shape[0] != 1]
    row_block = (8, max(terms[t].shape[1] for t in rows))
    assert len(rows) <= row_block[0]
    vmem = pl.BlockSpec(memory_space=pltpu.VMEM)

    def pack(*refs):
        packed = refs[-1]
        packed[...] = jnp.zeros_like(packed)
        for i, t in enumerate(rows):
            packed[i:i + 1, 0:terms[t].shape[1]] = refs[i][...]

    packed = pl.pallas_call(pack, name="small_pack", in_specs=[vmem] * len(rows), out_specs=vmem,
                            out_shape=jax.ShapeDtypeStruct(row_block, F32))(*[terms[t] for t in rows])
    slots = _run_on_sequencer("allgather_small", _everyone_hosted([packed] + [terms[t] for t in mats]),
                              EVERYONE, 2)

    def body(*refs):
        slot_refs, refs = refs[:len(slots)], refs[len(slots):]
        w_refs, refs = refs[:n], refs[n:]
        m_refs, refs = refs[:n], refs[n:]
        v_refs, outs = refs[:n], refs[n:]
        sums = []
        for ref in slot_refs:
            g = ref[0]
            for dev in range(1, 8):
                g = g + ref[dev]
            sums.append(g)
        chip = 2 * lax.axis_index("x") + lax.axis_index("y")
        for t in range(nt):
            if t in rows:
                i = rows.index(t)
                g = sums[0][i:i + 1, 0:terms[t].shape[1]]
            else:
                g = sums[1 + mats.index(t)]
            if t >= n:
                outs[4 * n + t - n][...] = g
                continue
            if shard_of[t]:
                width = ws[t].shape[-1]
                mine = jnp.zeros(ws[t].shape, F32)
                for s in range(4):
                    mine = jnp.where(chip == s, g[:, s * width:(s + 1) * width], mine)
                g = mine
            delta, mn, vn = _adamw_math(w_refs[t][...], g, m_refs[t][...], v_refs[t][...])
            outs[4 * t][...] = g
            outs[4 * t + 1][...] = delta
            outs[4 * t + 2][...] = mn
            outs[4 * t + 3][...] = vn

    out_shapes = []
    for t in range(n):
        out_shapes += [jax.ShapeDtypeStruct(ws[t].shape, F32)] * 4
    out_shapes += [jax.ShapeDtypeStruct(a.shape, F32) for a in extras]
    res = pl.pallas_call(
        body, name="small_step",
        in_specs=[vmem] * (len(slots) + 3 * n), out_specs=[vmem] * len(out_shapes), out_shape=out_shapes,
    )(*slots, *ws, *ms, *vs)
    return [res[4 * t:4 * t + 4] for t in range(n)], res[4 * n:4 * n + nt - n]


def _adamw_math(w, g, m, v):
    m = ADAM_B1 * m + (1.0 - ADAM_B1) * g
    v = ADAM_B2 * v + (1.0 - ADAM_B2) * (g * g)
    m_hat = m / (1.0 - ADAM_B1 ** ADAM_STEP)
    v_hat = v / (1.0 - ADAM_B2 ** ADAM_STEP)
    delta = -ADAM_LR * (m_hat / (jnp.sqrt(v_hat) + ADAM_EPS) + ADAM_WD * w)
    return delta, m, v


def _row_tile(rows):
    return 256 if rows % 256 == 0 else rows


def _sum_partials(name, own, recv, chip, after):
    rows, cols = own.shape[1:]
    tr = _row_tile(rows)

    def body(chip_ref, own_ref, recv_ref, after_ref, o_ref):
        acc = own_ref[...]
        for j in range(3):
            acc = acc + recv_ref[j].astype(F32)
        o_ref[...] = acc

    return pl.pallas_call(
        body, name=name,
        grid_spec=pltpu.PrefetchScalarGridSpec(
            num_scalar_prefetch=1, grid=(rows // tr,),
            in_specs=[pl.BlockSpec((None, tr, cols), lambda i, chip_ref: (chip_ref[0], i, 0)),
                      pl.BlockSpec((3, tr, cols), lambda i, chip_ref: (0, i, 0)), ANY],
            out_specs=pl.BlockSpec((tr, cols), lambda i, chip_ref: (i, 0))),
        out_shape=jax.ShapeDtypeStruct((rows, cols), F32),
        compiler_params=_params(("parallel",)),
    )(chip.reshape(1).astype(jnp.int32), own, recv, after)


def _adamw_ring(name, body, inputs, tr, spec):
    rows, cols = inputs[0].shape
    n_in = len(inputs)
    steps = rows // tr

    def ring(*refs):
        in_refs, out_refs = refs[:n_in], refs[n_in:n_in + 4]
        buf, sems = refs[n_in + 4:]
        i = pl.program_id(0)

        def copies(step, slot):
            return [pltpu.make_async_copy(in_refs[t].at[pl.ds(pl.multiple_of(step * tr, tr), tr)],
                                          buf.at[slot, t], sems.at[slot, t]) for t in range(n_in)]

        @pl.when(i == 0)
        def _():
            for s in range(min(STREAM_BUFFERS, steps)):
                for cp in copies(s, s):
                    cp.start()

        ahead = i + STREAM_BUFFERS - 1

        @pl.when((i > 0) & (ahead < steps))
        def _():
            for cp in copies(ahead, ahead % STREAM_BUFFERS):
                cp.start()

        slot = i % STREAM_BUFFERS
        for cp in copies(i, slot):
            cp.wait()
        body(*[buf.at[slot, t] for t in range(n_in)], *out_refs)

    return pl.pallas_call(
        ring, name=name, grid=(steps,), in_specs=[ANY] * n_in, out_specs=[spec] * 4,
        out_shape=[jax.ShapeDtypeStruct((rows, cols), F32)] * 4,
        scratch_shapes=[pltpu.VMEM((STREAM_BUFFERS, n_in, tr, cols), F32),
                        pltpu.SemaphoreType.DMA((STREAM_BUFFERS, n_in))],
        compiler_params=_params(("arbitrary",)),
    )(*inputs)


def _adamw(name, w, m, v, g_parts):
    rows, cols = w.shape
    tr = _row_tile(rows)
    n = len(g_parts)

    def body(w_ref, m_ref, v_ref, *refs):
        g_refs = refs[:n]
        go_ref, d_ref, mo_ref, vo_ref = refs[n:]
        g = g_refs[0][...]
        for r in g_refs[1:]:
            g = g + r[...]
        delta, mn, vn = _adamw_math(w_ref[...], g, m_ref[...], v_ref[...])
        go_ref[...] = g
        d_ref[...] = delta
        mo_ref[...] = mn
        vo_ref[...] = vn

    spec = pl.BlockSpec((tr, cols), lambda i: (i, 0))
    steps = rows // tr
    if steps > 2:
        return _adamw_ring(name, body, [w, m, v, *g_parts], tr, spec)
    return pl.pallas_call(
        body, name=name, grid=(steps,),
        in_specs=[spec] * (3 + n), out_specs=[spec] * 4,
        out_shape=[jax.ShapeDtypeStruct((rows, cols), F32)] * 4,
        compiler_params=_params(("parallel",)),
    )(w, m, v, *g_parts)


def _local_step(x, target, ga, wa_in, rel_bias, later_shards, gk, t5, gb, sinks, gf):
    s, d = x.shape
    tm = min(TM_DENSE, s)
    nt = s // tm
    half = d // 2
    row = pl.BlockSpec((tm, d), lambda i: (i, 0))
    whole = lambda shape: pl.BlockSpec(shape, lambda *_: (0,) * len(shape))

    n1, = _norm_fwd("norm_a", x, ga)
    projected = None
    for h, (wa_half, tag) in enumerate(zip(wa_in, ("first", "second"))):
        projected = _proj_a_half("proj_a_" + tag, n1, wa_half, h, projected)
    zqkv, gate_a = projected
    onehot_a = _a_offset_onehot()
    diag_a = _diag_rows(onehot_a, rel_bias)
    (o_a, u_a, lse_a), gathered = _attn_a_fwd(zqkv, gate_a, diag_a, hosted=_allgather_routed(later_shards))
    wa_out, wkv, wb_in, wb_out, wkv_x = gathered
    wa_out = wa_out.reshape(d, d)
    wkv = wkv.reshape(d, -1)
    wkv_x = wkv_x.reshape(d, B_KVX)
    wb_out = wb_out.reshape(d, d)
    h1, nk, n2 = _out_norms("out_a_norms", u_a, wa_out, x, jnp.concatenate([gk, gb], axis=0))
    kvw = wkv.shape[1]
    kvx =_matmul("proj_kv", nk, wkv_x, dims=NN, grid=(nt + 1,), zero_axis=0,
                  a_spec=pl.BlockSpec((tm, d), lambda i: (jnp.maximum(i - 1, 0), 0)), b_spec=whole((d, B_KVX)),
                  o_spec=pl.BlockSpec((tm, B_KVX), lambda i: (i, 0)), out_shape=(tm + s, B_KVX), out_dtype=BF16)
    qb, gate_b = _proj_b(n2, wb_in)
    onehot_b = _b_offset_onehot()
    base_b = jnp.roll(_diag_rows(onehot_b, t5)[..., ::-1], TQ, axis=-1)
    o_b, u_b, lse_b = _attn_b_fwd(qb, kvx, gate_b, base_b, sinks)
    dh2, loss, d_gf = _out_loss_head(u_b, wb_out, h1, target, gf)

    du_b = _matmul("dout_b", dh2, wb_out, dims=NT, grid=(nt,), a_spec=row, b_spec=whole((d, d)), o_spec=row,
                   out_shape=(s, d), out_dtype=F32)
    d_wb_out = _matmul("dw_out_b", u_b, dh2, dims=TN, grid=(2,),
                       a_spec=whole((s, d)), b_spec=pl.BlockSpec((s, half), lambda j: (0, j)),
                       o_spec=pl.BlockSpec((d, half), lambda j: (0, j)),
                       out_shape=(d, d), out_dtype=F32, also_bf16=True)
    dz_b, dkv, dsum_b, dsinks = _attn_b_bwd(qb, kvx, gate_b, o_b, du_b, lse_b, base_b, sinks)
    ddiag_b = jnp.roll(dsum_b[..., ::-1], -1, axis=-1)
    d_wb_in = _matmul("dw_in_b", n2, dz_b, dims=TN, grid=(4,),
                      a_spec=whole((s, d)), b_spec=pl.BlockSpec((None, s, half), lambda j: (j, 0, 0)),
                      o_spec=pl.BlockSpec((None, d, half), lambda j: (j, 0, 0)),
                      out_shape=(4, d, half), out_dtype=F32, also_bf16=True)
    d_wkv = _matmul("dw_kv", nk, dkv, dims=TN, grid=(1,),
                    a_spec=whole((s, d)), b_spec=whole((s, kvw)), o_spec=whole((d, kvw)),
                    out_shape=(d, kvw), out_dtype=F32, also_bf16=True)
    dh1, d_gkb = _proj_norm_bwd("dproj_kv_b", h1, dh2, jnp.concatenate([gk, gb], axis=0),
                                [(dkv[None], [wkv[None]]), (dz_b, [wb_in])])

    du_a = _matmul("dout_a", dh1, wa_out, dims=NT, grid=(nt,), a_spec=row, b_spec=whole((d, d)), o_spec=row,
                   out_shape=(s, d), out_dtype=F32)
    d_wa_out = _matmul("dw_out_a", u_a, dh1, dims=TN, grid=(2,),
                       a_spec=whole((s, d)), b_spec=pl.BlockSpec((s, half), lambda j: (0, j)),
                       o_spec=pl.BlockSpec((d, half), lambda j: (0, j)),
                       out_shape=(d, d), out_dtype=F32, also_bf16=True)
    early = dict(a_w_out=[g.reshape(4, d // 4, d) for g in d_wa_out],
                 kv_w=[g.reshape(4, d // 4, kvw) for g in d_wkv], b_w_in=list(d_wb_in),
                 b_w_out=[g.reshape(4, d // 4, d) for g in d_wb_out])
    (dz_a, ddiag_a), early_recv = _attn_a_bwd(
        zqkv, gate_a, o_a, du_a, lse_a, diag_a, hosted=_scatter_hosted([early[n][1] for n in early]))
    d_wa_in = _matmul("dw_in_a", n1, dz_a, dims=TN, grid=(4, 2),
                      a_spec=whole((s, d)), b_spec=pl.BlockSpec((None, s, half), lambda j, h: (j, 0, h)),
                      o_spec=pl.BlockSpec((None, d, half), lambda j, h: (j, 0, h)),
                      out_shape=(4, d, d), out_dtype=F32, also_bf16=True)
    late_recv = _run_on_sequencer("scatter_a_w_in", _scatter_hosted([d_wa_in[1]]), SCATTER_PEERS, 0)
    grad_x, d_ga = _proj_norm_bwd("dproj_a", x, dh1, ga, [(dz_a, list(wa_in))])

    small = dict(a_norm=d_ga, kv_norm=d_gkb[0:1], b_norm=d_gkb[1:2], b_sinks=dsinks[0:1, :HEADS], final_norm=d_gf)
    small["by_offset"] = dict(a_rel_bias=(onehot_a, ddiag_a.reshape(HEADS, -1)),
                              t5_bias=(onehot_b, ddiag_b.reshape(HEADS, -1)))
    own = dict(a_w_in=d_wa_in[0], **{n: early[n][0] for n in early})
    received = dict(a_w_in=late_recv[0], **dict(zip(early, early_recv)))
    return loss, grad_x, small, own, received, d_wa_in[1]


SMALL = ("a_norm", "kv_norm", "b_norm", "b_sinks", "final_norm")
TABLES = ("a_rel_bias", "t5_bias")
BIG = ("a_w_in", "a_w_out", "kv_w", "b_w_in", "b_w_out")
ORDER = ("a_norm", "a_w_in", "a_rel_bias", "a_w_out", "kv_norm", "kv_w", "t5_bias", "b_norm", "b_w_in",
         "b_sinks", "b_w_out", "final_norm")


def kernel(x, a_norm, a_w_in, a_rel_bias, a_w_out, kv_norm, kv_w, t5_bias, b_norm, b_w_in, b_sinks, b_w_out, final_norm, loss_target, m_a_norm, m_a_w_in, m_a_rel_bias, m_a_w_out, m_kv_norm, m_kv_w, m_t5_bias, m_b_norm, m_b_w_in, m_b_sinks, m_b_w_out, m_final_norm, v_a_norm, v_a_w_in, v_a_rel_bias, v_a_w_out, v_kv_norm, v_kv_w, v_t5_bias, v_b_norm, v_b_w_in, v_b_sinks, v_b_w_out, v_final_norm):
    w = dict(a_norm=a_norm, a_w_in=a_w_in, a_rel_bias=a_rel_bias, a_w_out=a_w_out, kv_norm=kv_norm, kv_w=kv_w,
             t5_bias=t5_bias, b_norm=b_norm, b_w_in=b_w_in, b_sinks=b_sinks, b_w_out=b_w_out,
             final_norm=final_norm)
    m = dict(a_norm=m_a_norm, a_w_in=m_a_w_in, a_rel_bias=m_a_rel_bias, a_w_out=m_a_w_out, kv_norm=m_kv_norm,
             kv_w=m_kv_w, t5_bias=m_t5_bias, b_norm=m_b_norm, b_w_in=m_b_w_in, b_sinks=m_b_sinks,
             b_w_out=m_b_w_out, final_norm=m_final_norm)
    v = dict(a_norm=v_a_norm, a_w_in=v_a_w_in, a_rel_bias=v_a_rel_bias, a_w_out=v_a_w_out, kv_norm=v_kv_norm,
             kv_w=v_kv_w, t5_bias=v_t5_bias, b_norm=v_b_norm, b_w_in=v_b_w_in, b_sinks=v_b_sinks,
             b_w_out=v_b_w_out, final_norm=v_final_norm)
    d = D_MODEL
    chip = 2 * lax.axis_index("x") + lax.axis_index("y")

    shard2d = dict(a_w_in=a_w_in[0], a_w_out=a_w_out[0], kv_w=kv_w, b_w_in=b_w_in[0], b_w_out=b_w_out[0])

    first = shard2d["a_w_in"].astype(BF16)
    wa_in = [_run_on_sequencer("allgather_" + tag, _allgather_routed([first[:, h * (d // 2):(h + 1) * (d // 2)]]),
                               GATHER_PEERS, collective_id)[0]
             for h, (tag, collective_id) in enumerate((("first", 1), ("second", 3)))]
    ga = _gather_gain(a_norm).reshape(1, d)

    later = [shard2d[n].astype(BF16) for n in BIG[1:]]
    kv_shard = later[BIG[1:].index("kv_w")]
    later.append(jnp.concatenate(
        [kv_shard[:, (i // 2) * HEAD_DIM:(i // 2 + 1) * HEAD_DIM] for i in range(B_KVX // HEAD_DIM)], axis=1))
    loss, grad_x, small, own, received, after_attention = _local_step(
        x[0], loss_target[0], ga, wa_in, a_rel_bias[0], later,
        kv_norm.reshape(1, d), t5_bias, b_norm, b_sinks, final_norm.reshape(1, d))

    out = {}
    as2d = lambda a: a.reshape(-1, a.shape[-1])
    small_res, (loss_sum, *offset_sums) = _small_step(
        [small[n] for n in SMALL], [loss] + [small["by_offset"][n][1] for n in TABLES],
        [as2d(w[n]) for n in SMALL], [as2d(m[n]) for n in SMALL], [as2d(v[n]) for n in SMALL],
        [n == "a_norm" for n in SMALL])
    for n, res in zip(SMALL, small_res):
        out[n] = [r.reshape(w[n].shape) for r in res]
    loss_out = loss_sum.reshape(())
    for n, summed in zip(TABLES, offset_sums):
        grad = _diag_rows_grad(small["by_offset"][n][0], summed)
        res = _adamw("adamw_" + n, as2d(w[n]).T, as2d(m[n]).T, as2d(v[n]).T, [grad])
        out[n] = [r.T.reshape(w[n].shape) for r in res]

    core_sums = [_sum_partials("sum_" + n, own[n], received[n], chip, after_attention) for n in BIG]
    sibling_sums = (_swap_with_sibling("swap_last", core_sums[:1])
                    + _swap_with_sibling("swap_early", core_sums[1:]))

    for n, mine, theirs in zip(BIG, core_sums, sibling_sums):
        res = _adamw("adamw_" + n, shard2d[n], m[n].reshape(shard2d[n].shape), v[n].reshape(shard2d[n].shape),
                     [mine, theirs])
        out[n] = [r.reshape(w[n].shape) for r in res]

    grads = [out[n][0] for n in ORDER]
    deltas = [out[n][1] for n in ORDER]
    new_m = [out[n][2] for n in ORDER]
    new_v = [out[n][3] for n in ORDER]
    return (loss_out, grad_x[None], *grads, *deltas, *new_m, *new_v)
```

```python
import math

import jax
import jax.numpy as jnp
import numpy as np
from jax import lax
from jax.experimental import pallas as pl
from jax.experimental.pallas import tpu as pltpu
from jax.experimental.pallas import tpu_sc as plsc

F32 = jnp.float32
BF16 = jnp.bfloat16
MESH = pl.DeviceIdType.MESH

D_MODEL = 1024
HEADS = 16
HEAD_DIM = 64
CHUNK = 64
RMS_EPS = 1e-6
SCALE = HEAD_DIM ** -0.5
A_LEFT_CHUNKS = 8
A_REL_CLIP = 256
B_LEFT_CHUNKS = 2
B_KV_HEADS = 2
B_GROUP = HEADS // B_KV_HEADS
T5_BUCKETS = 32
T5_MAX_DIST = 128
ADAM_LR = 0.001
ADAM_B1 = 0.9
ADAM_B2 = 0.999
ADAM_EPS = 1e-08
ADAM_WD = 0.01
ADAM_STEP = 10

MASKED = -1e30
LANES = 128
TQ = 128
A_PAIRS = 2
A_PAIRS_FWD = 4
KB = 128
A_KBLOCKS = A_LEFT_CHUNKS * CHUNK // KB + 1
B_KBLOCKS = B_LEFT_CHUNKS * CHUNK // KB + 1
A_WIN = A_KBLOCKS * KB
B_WIN = B_KBLOCKS * KB
TM = 512
TM_DENSE = 1024
TM_HALF = 2048
STREAM_BUFFERS = 3
TM_PARTS = 512
VMEM_LIMIT = 56 * 1024 * 1024

NT = (((1,), (1,)), ((), ()))
TN = (((0,), (0,)), ((), ()))
NN = (((1,), (0,)), ((), ()))


def _params(sem=None):
    return pltpu.CompilerParams(dimension_semantics=sem, vmem_limit_bytes=VMEM_LIMIT)


class _Hosted:
    def __init__(self, inputs, out_shapes, sems, first, middle, last):
        self.inputs, self.out_shapes, self.sems = list(inputs), list(out_shapes), list(sems)
        self.first, self.middle, self.last = first, middle, last


def _call(body, *, name, grid, in_specs, out_specs, out_shape, args, scratch_shapes=(), sem=None, hosted=None,
          aliases=None):
    in_specs, out_specs, out_shape = list(in_specs), list(out_specs), list(out_shape)
    scratch_shapes = list(scratch_shapes)
    if hosted is None:
        out = pl.pallas_call(
            body, name=name, grid=grid, in_specs=in_specs, out_specs=out_specs, out_shape=out_shape,
            scratch_shapes=scratch_shapes, input_output_aliases=aliases or {},
            compiler_params=_params(sem))(*args)
        return list(out), []
    assert aliases is None
    n_in, n_out, n_scr = len(in_specs), len(out_shape), len(scratch_shapes)
    h_in, h_out = len(hosted.inputs), len(hosted.out_shapes)
    total = int(np.prod(grid)) if grid else 1

    def wrapped(*refs):
        ins, refs = refs[:n_in], refs[n_in:]
        h_ins, refs = refs[:h_in], refs[h_in:]
        outs, refs = refs[:n_out], refs[n_out:]
        h_outs, refs = refs[:h_out], refs[h_out:]
        scr, h_sems = refs[:n_scr], refs[n_scr:]
        step = 0
        for axis, size in enumerate(grid):
            step = step * size + pl.program_id(axis)

        if hosted.first is not None:
            @pl.when(step == 0)
            def _():
                hosted.first(h_ins, h_outs, h_sems)

        body(*ins, *outs, *scr)
        if hosted.middle is not None:
            @pl.when(step == total // 2)
            def _():
                hosted.middle(h_ins, h_outs, h_sems)

        if hosted.last is not None:
            @pl.when(step == total - 1)
            def _():
                hosted.last(h_ins, h_outs, h_sems)

    out = pl.pallas_call(
        wrapped, name=name, grid=grid, in_specs=in_specs + [ANY] * h_in, out_specs=out_specs + [ANY] * h_out,
        out_shape=out_shape + hosted.out_shapes, scratch_shapes=scratch_shapes + hosted.sems,
        compiler_params=_params(("arbitrary",) * len(grid)))(*args, *hosted.inputs)
    return list(out[:n_out]), list(out[n_out:])


def _matmul(name, a, b, *, dims, grid, a_spec, b_spec, o_spec, out_shape, out_dtype,
            also_bf16=False, zero_axis=None):
    def body(*refs):
        if zero_axis is None:
            product(*refs)
        else:
            @pl.when(pl.program_id(zero_axis) == 0)
            def _():
                refs[2][...] = jnp.zeros_like(refs[2])

            @pl.when(pl.program_id(zero_axis) > 0)
            def _():
                product(*refs)

    def product(a_ref, b_ref, o_ref, *more):
        prod = lax.dot_general(a_ref[...].astype(BF16), b_ref[...].astype(BF16), dims,
                               preferred_element_type=F32)
        o_ref[...] = prod.astype(out_dtype)
        if also_bf16:
            more[0][...] = prod.astype(BF16)

    out_specs = [o_spec]
    out_shapes = [jax.ShapeDtypeStruct(out_shape, out_dtype)]
    if also_bf16:
        out_specs.append(o_spec)
        out_shapes.append(jax.ShapeDtypeStruct(out_shape, BF16))
    out, _ = _call(body, name=name, grid=grid, in_specs=[a_spec, b_spec], out_specs=out_specs,
                   out_shape=out_shapes, args=[a, b], sem=("parallel",) * len(grid))
    return out[0] if not also_bf16 else tuple(out)


def _proj_a_half(name, n1, w, h, into):
    s, d = n1.shape
    half = w.shape[2]
    ta = min(TM_HALF, s)

    def body(a_ref, w_ref, *refs):
        z_ref, g_ref = refs[-2:]
        i, j = pl.program_id(0), pl.program_id(1)

        @pl.when((i == 0) & (j < 3))
        def _():
            z_ref[...] = jnp.zeros_like(z_ref)

        @pl.when((i > 0) & (j < 3))
        def _():
            z_ref[...] = jnp.dot(a_ref[...], w_ref[...], preferred_element_type=F32).astype(BF16)

        @pl.when((i > 0) & (j == 3))
        def _():
            g_ref[...] = jnp.dot(a_ref[...], w_ref[...], preferred_element_type=F32)

    out, _ = _call(
        body, name=name, grid=(s // ta + 1, 4),
        in_specs=[pl.BlockSpec((ta, d), lambda i, j: (jnp.maximum(i - 1, 0), 0)),
                  pl.BlockSpec((None, d, half), lambda i, j: (j, 0, 0))] + ([] if into is None else [ANY, ANY]),
        out_specs=[pl.BlockSpec((None, ta, half), lambda i, j: (jnp.minimum(j, 2), i, h)),
                   pl.BlockSpec((ta, half), lambda i, j: (jnp.maximum(i - 1, 0), h))],
        out_shape=[jax.ShapeDtypeStruct((3, ta + s, d), BF16), jax.ShapeDtypeStruct((s, d), F32)],
        args=[n1, w] + ([] if into is None else list(into)), sem=("arbitrary", "arbitrary"),
        aliases=None if into is None else {2: 0, 3: 1})
    return out


def _proj_b(n2, w):
    s, d = n2.shape
    half = w.shape[2]
    ta = min(TM_HALF, s)

    def body(a_ref, w_ref, q_ref, g_ref):
        j = pl.program_id(1)

        @pl.when(j < 2)
        def _():
            q_ref[...] = jnp.dot(a_ref[...], w_ref[...], preferred_element_type=F32).astype(BF16)

        @pl.when(j >= 2)
        def _():
            g_ref[...] = jnp.dot(a_ref[...], w_ref[...], preferred_element_type=F32)

    out, _ = _call(
        body, name="proj_b", grid=(s // ta, 4),
        in_specs=[pl.BlockSpec((ta, d), lambda i, j: (i, 0)), pl.BlockSpec((None, d, half), lambda i, j: (j, 0, 0))],
        out_specs=[pl.BlockSpec((ta, half), lambda i, j: (i, jnp.minimum(j, 1))),
                   pl.BlockSpec((ta, half), lambda i, j: (i, jnp.maximum(j - 2, 0)))],
        out_shape=[jax.ShapeDtypeStruct((s, d), BF16), jax.ShapeDtypeStruct((s, d), F32)],
        args=[n2, w], sem=("arbitrary", "arbitrary"))
    return out


def _rms_rows(x):
    return lax.rsqrt(jnp.mean(x * x, axis=-1, keepdims=True) + RMS_EPS)


def _norm_fwd(name, x, gains):
    s, d = x.shape
    n = gains.shape[0]

    def body(x_ref, g_ref, *o_refs):
        xv = x_ref[...]
        xh = xv * _rms_rows(xv)
        for i in range(n):
            o_refs[i][...] = (xh * g_ref[i:i + 1, :]).astype(BF16)

    row = pl.BlockSpec((TM, d), lambda i: (i, 0))
    return pl.pallas_call(
        body, name=name, grid=(s // TM,),
        in_specs=[row, pl.BlockSpec((n, d), lambda i: (0, 0))],
        out_specs=[row] * n,
        out_shape=[jax.ShapeDtypeStruct((s, d), BF16)] * n,
        compiler_params=_params(("parallel",)),
    )(x, gains)


def _proj_norm_bwd(name, x, dres, gains, branches):
    s, d = x.shape
    n = len(branches)
    n_ab = 2 * sum(len(bs) for _, bs in branches)
    tm = min(TM_PARTS, s)

    def body(x_ref, r_ref, g_ref, *refs):
        ab_refs, dx_ref, dg_ref = list(refs[:n_ab]), refs[n_ab], refs[n_ab + 1]
        i = pl.program_id(0)
        xv = x_ref[...]
        r = _rms_rows(xv)
        xh = xv * r

        @pl.when(i == 0)
        def _():
            dg_ref[...] = jnp.zeros_like(dg_ref)

        a = None
        for j in range(n):
            dn = None
            for _ in branches[j][1]:
                a_ref, b_ref = ab_refs.pop(0), ab_refs.pop(0)
                for part in range(a_ref.shape[0]):
                    term = lax.dot_general(a_ref[part], b_ref[part], NT, preferred_element_type=F32)
                    dn = term if dn is None else dn + term
            t = dn * g_ref[j:j + 1, :]
            a = t if a is None else a + t
            dg_ref[j:j + 1, :] += jnp.sum(dn * xh, axis=0, keepdims=True)
        dx_ref[...] = r_ref[...] + r * (a - xh * jnp.mean(xh * a, axis=-1, keepdims=True))

    row = pl.BlockSpec((tm, d), lambda i: (i, 0))
    small = pl.BlockSpec((n, d), lambda i: (0, 0))
    ab_specs, ab_args = [], []
    for a, bs in branches:
        for k, b in enumerate(bs):
            ab_specs += [pl.BlockSpec((a.shape[0], tm, b.shape[2]), lambda i, k=k: (0, i, k)),
                         pl.BlockSpec(b.shape, lambda i: (0, 0, 0))]
            ab_args += [a, b]
    return pl.pallas_call(
        body, name=name, grid=(s // tm,),
        in_specs=[row, row, small] + ab_specs,
        out_specs=[row, small],
        out_shape=[jax.ShapeDtypeStruct((s, d), F32), jax.ShapeDtypeStruct((n, d), F32)],
        compiler_params=_params(("arbitrary",)),
    )(x, dres, gains, *ab_args)


def _out_norms(name, u, w_out, resid, gains):
    s, d = resid.shape
    n = gains.shape[0]
    tm = min(TM, s)

    def body(u_ref, w_ref, r_ref, g_ref, h_ref, *o_refs):
        hv = r_ref[...] + jnp.dot(u_ref[...], w_ref[...], preferred_element_type=F32)
        h_ref[...] = hv
        hh = hv * _rms_rows(hv)
        for i in range(n):
            o_refs[i][...] = (hh * g_ref[i:i + 1, :]).astype(BF16)

    row = pl.BlockSpec((tm, d), lambda i: (i, 0))
    return pl.pallas_call(
        body, name=name, grid=(s // tm,),
        in_specs=[row, pl.BlockSpec((d, d), lambda i: (0, 0)), row, pl.BlockSpec((n, d), lambda i: (0, 0))],
        out_specs=[row] * (n + 1),
        out_shape=[jax.ShapeDtypeStruct((s, d), F32)] + [jax.ShapeDtypeStruct((s, d), BF16)] * n,
        compiler_params=_params(("parallel",)),
    )(u, w_out, resid, gains)


def _out_loss_head(u, w_out, resid, target, gain):
    s, d = resid.shape
    tm = min(TM_PARTS, s)

    def body(u_ref, w_ref, r_ref, t_ref, g_ref, dh_ref, loss_ref, dg_ref):
        i = pl.program_id(0)
        hv = r_ref[...] + jnp.dot(u_ref[...], w_ref[...], preferred_element_type=F32)
        r = _rms_rows(hv)
        hh = hv * r
        g = g_ref[...]
        err = hh * g - t_ref[...]
        part = 0.5 * jnp.sum(jnp.sum(err * err, axis=-1, keepdims=True) * (1.0 / d), axis=0, keepdims=True)
        dy = err * (1.0 / d)
        a = dy * g
        dh_ref[...] = r * (a - hh * jnp.mean(hh * a, axis=-1, keepdims=True))
        dg = jnp.sum(dy * hh, axis=0, keepdims=True)

        @pl.when(i == 0)
        def _():
            loss_ref[...] = part
            dg_ref[...] = dg

        @pl.when(i > 0)
        def _():
            loss_ref[...] += part
            dg_ref[...] += dg

    row = pl.BlockSpec((tm, d), lambda i: (i, 0))
    return pl.pallas_call(
        body, name="out_b_loss_head", grid=(s // tm,),
        in_specs=[row, pl.BlockSpec((d, d), lambda i: (0, 0)), row, row, pl.BlockSpec((1, d), lambda i: (0, 0))],
        out_specs=[row, pl.BlockSpec((1, 1), lambda i: (0, 0)), pl.BlockSpec((1, d), lambda i: (0, 0))],
        out_shape=[jax.ShapeDtypeStruct((s, d), F32), jax.ShapeDtypeStruct((1, 1), F32),
                   jax.ShapeDtypeStruct((1, d), F32)],
        compiler_params=_params(("arbitrary",)),
    )(u, w_out, resid, target, gain)


def _silu_parts(g):
    sig = jax.nn.sigmoid(g)
    return g * sig, sig * (1.0 + g * (1.0 - sig))


def _lane_lo(rows):
    return lax.broadcasted_iota(jnp.int32, (rows, LANES), 1) < HEAD_DIM


def _stack_pair(x):
    lo = _lane_lo(x.shape[0])
    zero = jnp.zeros_like(x)
    return jnp.concatenate([jnp.where(lo, x, zero), jnp.where(lo, zero, x)], axis=0)


def _unstack_pair(y, w):
    return jnp.where(_lane_lo(w), y[:w], y[w:])


def _block_valid(b, left_blocks, width):
    col = lax.broadcasted_iota(jnp.int32, (1, 2 * width), 1)
    col = jnp.where(col >= width, col - width, col)
    return (col // KB + (b - left_blocks)) >= 0


def _toeplitz_tile(diag_row, width, left_chunks):
    wide = width + TQ
    rolled = pltpu.roll(jnp.broadcast_to(diag_row, (TQ, wide)), 1, 1, stride=1, stride_axis=0)
    i = lax.broadcasted_iota(jnp.int32, (TQ, width), 0) // CHUNK
    j = lax.broadcasted_iota(jnp.int32, (TQ, width), 1) // CHUNK
    dc = i + left_chunks - j
    return jnp.where((dc >= 0) & (dc <= left_chunks), rolled[:, TQ:], MASKED)


def _toeplitz_sum(tile, width):
    flip = (lax.broadcasted_iota(jnp.int32, (TQ, TQ), 0) + lax.broadcasted_iota(jnp.int32, (TQ, TQ), 1)
            == TQ - 1).astype(F32)
    reversed_rows = jnp.dot(flip, tile, precision=lax.Precision.HIGHEST, preferred_element_type=F32)
    padded = jnp.concatenate([reversed_rows, jnp.zeros((TQ, TQ), F32)], axis=1)
    rolled = pltpu.roll(padded, 0, 1, stride=1, stride_axis=0)
    return jnp.sum(rolled, axis=0, keepdims=True)


def _softmax_pair(sc, w, sink=None):
    ps, inv, lses = [], [], []
    for e in range(2):
        sh = sc[:, e * w:(e + 1) * w]
        m = jnp.max(sh, axis=-1, keepdims=True)
        if sink is not None:
            m = jnp.maximum(m, sink[e])
        ex = jnp.exp(sh - m)
        l = jnp.sum(ex, axis=-1, keepdims=True)
        if sink is not None:
            l = l + jnp.exp(sink[e] - m)
        ps.append(ex.astype(BF16))
        inv.append(1.0 / l)
        lses.append(m + jnp.log(l))
    return jnp.concatenate(ps, axis=-1), inv, lses


def _softmax_pair_bwd(sc, dp, lse, delta, w):
    ps, dss = [], []
    for e in range(2):
        p = jnp.exp(sc[:, e * w:(e + 1) * w] - lse[e])
        ps.append(p)
        dss.append(p * (dp[:, e * w:(e + 1) * w] - delta[e]))
    return jnp.concatenate(ps, axis=-1), jnp.concatenate(dss, axis=-1)


def _pair_rowsums(x, lo):
    zero = jnp.zeros_like(x)
    return (jnp.sum(jnp.where(lo, x, zero), axis=-1, keepdims=True),
            jnp.sum(jnp.where(lo, zero, x), axis=-1, keepdims=True))


def _a_qkv_specs(rows, pad, pw):
    return [pl.BlockSpec((None, TQ, pw), lambda p, b: (0, b + pad // TQ, p)),
            pl.BlockSpec((None, rows, pw), lambda p, b: (1, 0, p)),
            pl.BlockSpec((None, rows, pw), lambda p, b: (2, 0, p))]


def _window(ref, b, pad, win, lanes):
    start = pl.multiple_of(b * TQ + pad - (win - TQ), KB)
    return ref[pl.ds(start, win), lanes]


def _attn_a_fwd(zqkv, g, diag, hosted=None):
    s = g.shape[0]
    pad = zqkv.shape[1] - s
    nb = s // TQ
    left = A_KBLOCKS - 1
    pairs = A_PAIRS_FWD
    pw = pairs * LANES
    wide = A_WIN + TQ

    def body(q_ref, k_ref, v_ref, g_ref, diag_ref, o_ref, u_ref, lse_ref, bias_scr):
        b = pl.program_id(1)

        @pl.when(b == 0)
        def _():
            for hh in range(2 * pairs):
                bias_scr[hh // 2, :, (hh % 2) * A_WIN:(hh % 2 + 1) * A_WIN] = _toeplitz_tile(
                    diag_ref[hh], A_WIN, A_LEFT_CHUNKS)

        def step(first_blocks):
            lo = _lane_lo(TQ)
            for pp in range(pairs):
                ln = slice(pp * LANES, (pp + 1) * LANES)
                kcat = _stack_pair(_window(k_ref, b, pad, A_WIN, ln))
                vcat = _stack_pair(_window(v_ref, b, pad, A_WIN, ln))
                sc = lax.dot_general(q_ref[:, ln] * SCALE, kcat, NT, preferred_element_type=F32) + bias_scr[pp]
                if first_blocks:
                    sc = jnp.where(_block_valid(b, left, A_WIN), sc, MASKED)
                p, inv, lses = _softmax_pair(sc, A_WIN)
                ov = jnp.dot(p, vcat, preferred_element_type=F32) * jnp.where(lo, inv[0], inv[1])
                o_ref[:, ln] = ov
                lse_ref[pp] = jnp.where(lo, lses[0], lses[1])
                sg, _ = _silu_parts(g_ref[:, ln])
                u_ref[:, ln] = (ov * sg).astype(BF16)

        @pl.when(b < left)
        def _():
            step(True)

        @pl.when(b >= left)
        def _():
            step(False)

    tile = pl.BlockSpec((TQ, pw), lambda p, b: (b, p))
    return _call(
        body, name="attn_a_fwd", grid=(HEADS // 2 // pairs, nb),
        in_specs=_a_qkv_specs(pad + s, pad, pw) + [
            tile, pl.BlockSpec((2 * pairs, 1, wide), lambda p, b: (p, 0, 0))],
        out_specs=[tile, tile, pl.BlockSpec((pairs, TQ, LANES), lambda p, b: (p, b, 0))],
        out_shape=[jax.ShapeDtypeStruct((s, D_MODEL), F32), jax.ShapeDtypeStruct((s, D_MODEL), BF16),
                   jax.ShapeDtypeStruct((HEADS // 2, s, LANES), F32)],
        scratch_shapes=[pltpu.VMEM((pairs, TQ, 2 * A_WIN), F32)],
        sem=("parallel", "arbitrary"), hosted=hosted,
        args=(zqkv, zqkv, zqkv, g, diag))


def _attn_a_bwd(zqkv, g, o, du, lse, diag, hosted=None):
    s = g.shape[0]
    pad = zqkv.shape[1] - s
    nb = s // TQ
    left = A_KBLOCKS - 1
    pw = A_PAIRS * LANES
    wide = A_WIN + TQ

    def body(q_ref, k_ref, v_ref, g_ref, o_ref, du_ref, lse_ref, diag_ref, dz_ref, ddiag_ref,
             bias_scr, dbias_acc, dk_acc, dv_acc):
        b = pl.program_id(1)

        @pl.when(b == 0)
        def _():
            for hh in range(2 * A_PAIRS):
                bias_scr[hh // 2, :, (hh % 2) * A_WIN:(hh % 2 + 1) * A_WIN] = _toeplitz_tile(
                    diag_ref[hh], A_WIN, A_LEFT_CHUNKS)
            dbias_acc[...] = jnp.zeros_like(dbias_acc)
            dk_acc[...] = jnp.zeros_like(dk_acc)
            dv_acc[...] = jnp.zeros_like(dv_acc)

        def step(first_blocks):
            lo = _lane_lo(TQ)
            rows = pl.ds(pl.multiple_of(b * TQ, TQ), TQ)
            sg, dsg = _silu_parts(g_ref[...])
            duv = du_ref[...]
            ov = o_ref[...]
            do = duv * sg
            dz_ref[3, rows, :] = (duv * ov * dsg).astype(BF16)
            do_o = do * ov
            do_bf = do.astype(BF16)
            for pp in range(A_PAIRS):
                ln = slice(pp * LANES, (pp + 1) * LANES)
                q = q_ref[:, ln] * SCALE
                kcat = _stack_pair(_window(k_ref, b, pad, A_WIN, ln))
                vcat = _stack_pair(_window(v_ref, b, pad, A_WIN, ln))
                sc = lax.dot_general(q, kcat, NT, preferred_element_type=F32) + bias_scr[pp]
                if first_blocks:
                    sc = jnp.where(_block_valid(b, left, A_WIN), sc, MASKED)
                lse_t = lse_ref[pp]
                dp = lax.dot_general(do_bf[:, ln], vcat, NT, preferred_element_type=F32)
                p, ds = _softmax_pair_bwd(sc, dp, (lse_t[:, 0:1], lse_t[:, HEAD_DIM:HEAD_DIM + 1]),
                                          _pair_rowsums(do_o[:, ln], lo), A_WIN)
                dbias_acc[pp] += ds
                dsb = ds.astype(BF16)
                dz_ref[0, rows, ln] = (jnp.dot(dsb, kcat, preferred_element_type=F32) * SCALE).astype(BF16)
                pb = p.astype(BF16)
                dob = do_bf[:, ln]
                dkt = jnp.concatenate([
                    lax.dot_general(q[:, e * HEAD_DIM:(e + 1) * HEAD_DIM], dsb[:, e * A_WIN:(e + 1) * A_WIN], TN,
                                    preferred_element_type=F32) for e in range(2)], axis=0)
                dvt = jnp.concatenate([
                    lax.dot_general(dob[:, e * HEAD_DIM:(e + 1) * HEAD_DIM], pb[:, e * A_WIN:(e + 1) * A_WIN], TN,
                                    preferred_element_type=F32) for e in range(2)], axis=0)
                for t in range(A_KBLOCKS):
                    blk = b + (pad // KB - left + t)
                    dk_acc[blk, ln, :] += dkt[:, t * KB:(t + 1) * KB]
                    dv_acc[blk, ln, :] += dvt[:, t * KB:(t + 1) * KB]

        @pl.when(b < left)
        def _():
            step(True)

        @pl.when(b >= left)
        def _():
            step(False)

        @pl.when(b == nb - 1)
        def _():
            for kb in range(s // KB):
                dz_ref[1, kb * KB:(kb + 1) * KB, :] = dk_acc[pad // KB + kb].T.astype(BF16)
                dz_ref[2, kb * KB:(kb + 1) * KB, :] = dv_acc[pad // KB + kb].T.astype(BF16)
            for hh in range(2 * A_PAIRS):
                ddiag_ref[hh] = _toeplitz_sum(
                    dbias_acc[hh // 2, :, (hh % 2) * A_WIN:(hh % 2 + 1) * A_WIN], A_WIN)

    tile = pl.BlockSpec((TQ, pw), lambda p, b: (b, p))
    diag_spec = pl.BlockSpec((2 * A_PAIRS, 1, wide), lambda p, b: (p, 0, 0))
    return _call(
        body, name="attn_a_bwd", grid=(HEADS // 2 // A_PAIRS, nb),
        in_specs=_a_qkv_specs(pad + s, pad, pw) + [
            tile, tile, tile, pl.BlockSpec((A_PAIRS, TQ, LANES), lambda p, b: (p, b, 0)), diag_spec],
        out_specs=[pl.BlockSpec((4, s, pw), lambda p, b: (0, 0, p)), diag_spec],
        out_shape=[jax.ShapeDtypeStruct((4, s, D_MODEL), BF16),
                   jax.ShapeDtypeStruct((HEADS, 1, wide), F32)],
        scratch_shapes=[pltpu.VMEM((A_PAIRS, TQ, 2 * A_WIN), F32), pltpu.VMEM((A_PAIRS, TQ, 2 * A_WIN), F32),
                        pltpu.VMEM(((pad + s) // KB, pw, KB), F32), pltpu.VMEM(((pad + s) // KB, pw, KB), F32)],
        sem=("parallel", "arbitrary"), hosted=hosted,
        args=(zqkv, zqkv, zqkv, g, o, du, lse, diag))


B_STACK = B_GROUP // 2
B_KVX = 4 * LANES
B_ROWS = B_STACK * TQ
B_WIDE = B_WIN + TQ


def _b_head_place(h):
    return h // B_GROUP, (h % B_GROUP) // 2, h % 2


def _toeplitz_tile_t(base_row, width, left_chunks):
    wide = width + TQ
    rolled = pltpu.roll(jnp.broadcast_to(base_row, (width, wide)), 0, 1, stride=1, stride_axis=0)
    j = lax.broadcasted_iota(jnp.int32, (width, TQ), 0) // CHUNK
    i = lax.broadcasted_iota(jnp.int32, (width, TQ), 1) // CHUNK
    dc = i + left_chunks - j
    return jnp.where((dc >= 0) & (dc <= left_chunks), rolled[:, :TQ], MASKED)


def _toeplitz_sum_t(tile_t, width):
    flip = (lax.broadcasted_iota(jnp.int32, (width, width), 0) + lax.broadcasted_iota(jnp.int32, (width, width), 1)
            == width - 1).astype(F32)
    reversed_rows = jnp.dot(flip, tile_t, precision=lax.Precision.HIGHEST, preferred_element_type=F32)
    padded = jnp.concatenate([reversed_rows, jnp.zeros((width, width), F32)], axis=1)
    rolled = pltpu.roll(padded, 0, 1, stride=1, stride_axis=0)
    return jnp.sum(rolled, axis=0, keepdims=True)


def _b_build_bias(base_ref, bias_scr):
    for h in range(HEADS):
        gi, pr, e = _b_head_place(h)
        bias_scr[gi, e * B_WIN:(e + 1) * B_WIN, pr * TQ:(pr + 1) * TQ] = _toeplitz_tile_t(
            base_ref[h], B_WIN, B_LEFT_CHUNKS)


def _b_stack(x, gi):
    return jnp.concatenate(
        [x[:, (B_STACK * gi + pr) * LANES:(B_STACK * gi + pr + 1) * LANES] for pr in range(B_STACK)], axis=0)


def _b_sink_rows(sink_ref, gi):
    block = lax.broadcasted_iota(jnp.int32, (1, B_ROWS), 1) // TQ
    rows = []
    for e in range(2):
        row = jnp.zeros((1, B_ROWS), F32)
        for pr in range(B_STACK):
            h = B_GROUP * gi + 2 * pr + e
            row = jnp.where(block == pr, sink_ref[0:1, h:h + 1], row)
        rows.append(row)
    return rows


def _b_scores_t(q_ref, kvv, bias_scr, gi, b, left, first_blocks):
    kcat = _stack_pair(kvv[:, gi * LANES:(gi + 1) * LANES])
    vcat = _stack_pair(kvv[:, (B_KV_HEADS + gi) * LANES:(B_KV_HEADS + gi + 1) * LANES])
    qs = _b_stack(q_ref, gi) * SCALE
    sc = lax.dot_general(kcat, qs, NT, preferred_element_type=F32) + bias_scr[gi]
    if first_blocks:
        row = lax.broadcasted_iota(jnp.int32, (2 * B_WIN, 1), 0)
        row = jnp.where(row >= B_WIN, row - B_WIN, row)
        sc = jnp.where((row // KB + (b - left)) >= 0, sc, MASKED)
    return kcat, vcat, qs, sc


def _attn_b_fwd(qb, kvx, gate, base, sinks):
    s = qb.shape[0]
    pad = kvx.shape[0] - s
    nb = s // TQ
    left = B_KBLOCKS - 1

    def body(q_ref, kv_ref, g_ref, base_ref, sink_ref, o_ref, u_ref, lse_ref, bias_scr):
        b = pl.program_id(0)

        @pl.when(b == 0)
        def _():
            _b_build_bias(base_ref, bias_scr)

        def step(first_blocks):
            kvv = _window(kv_ref, b, pad, B_WIN, slice(None))
            upper = lax.broadcasted_iota(jnp.int32, (LANES, B_ROWS), 0) < HEAD_DIM
            lse_rows = []
            for gi in range(B_KV_HEADS):
                kcat, vcat, qs, sc = _b_scores_t(q_ref, kvv, bias_scr, gi, b, left, first_blocks)
                sink = _b_sink_rows(sink_ref, gi)
                ps, inv = [], []
                for e in range(2):
                    sh = sc[e * B_WIN:(e + 1) * B_WIN]
                    m = jnp.maximum(jnp.max(sh, axis=0, keepdims=True), sink[e])
                    ex = jnp.exp(sh - m)
                    l = jnp.sum(ex, axis=0, keepdims=True) + jnp.exp(sink[e] - m)
                    ps.append(ex.astype(BF16))
                    inv.append(1.0 / l)
                    lse_rows.append(m + jnp.log(l))
                pt = jnp.concatenate(ps, axis=0)
                ot = lax.dot_general(vcat, pt, TN, preferred_element_type=F32) * jnp.where(upper, inv[0], inv[1])
                ov = ot.T
                for pr in range(B_STACK):
                    pair = B_STACK * gi + pr
                    o_ref[:, pair * LANES:(pair + 1) * LANES] = ov[pr * TQ:(pr + 1) * TQ]
            lse_ref[0] = jnp.concatenate(lse_rows + [jnp.zeros((8 - len(lse_rows), B_ROWS), F32)], axis=0)
            sg, _ = _silu_parts(g_ref[...])
            u_ref[...] = (o_ref[...] * sg).astype(BF16)

        @pl.when(b < left)
        def _():
            step(True)

        @pl.when(b >= left)
        def _():
            step(False)

    row = pl.BlockSpec((TQ, D_MODEL), lambda b: (b, 0))
    return pl.pallas_call(
        body, name="attn_b_fwd", grid=(nb,),
        in_specs=[row, pl.BlockSpec((pad + s, B_KVX), lambda b: (0, 0)), row,
                  pl.BlockSpec((HEADS, 1, B_WIDE), lambda b: (0, 0, 0)), pl.BlockSpec((1, HEADS), lambda b: (0, 0))],
        out_specs=[row, row, pl.BlockSpec((1, 8, B_ROWS), lambda b: (b, 0, 0))],
        out_shape=[jax.ShapeDtypeStruct((s, D_MODEL), F32), jax.ShapeDtypeStruct((s, D_MODEL), BF16),
                   jax.ShapeDtypeStruct((nb, 8, B_ROWS), F32)],
        scratch_shapes=[pltpu.VMEM((B_KV_HEADS, 2 * B_WIN, B_ROWS), F32)],
        compiler_params=_params(("arbitrary",)),
    )(qb, kvx, gate, base, sinks)


def _attn_b_bwd(qb, kvx, gate, o, du, lse, base, sinks):
    s = qb.shape[0]
    pad = kvx.shape[0] - s
    nb = s // TQ
    left = B_KBLOCKS - 1
    half = D_MODEL // 2

    def body(q_ref, kv_ref, g_ref, o_ref, du_ref, lse_ref, base_ref, sink_ref, dz_ref, dkv_ref, dsum_ref,
             dsink_ref, bias_scr, dbias_acc, dkv_acc, dsink_acc):
        b = pl.program_id(0)

        @pl.when(b == 0)
        def _():
            _b_build_bias(base_ref, bias_scr)
            dbias_acc[...] = jnp.zeros_like(dbias_acc)
            dkv_acc[...] = jnp.zeros_like(dkv_acc)
            dsink_acc[...] = jnp.zeros_like(dsink_acc)

        def step(first_blocks):
            kvv = _window(kv_ref, b, pad, B_WIN, slice(None))
            sg, dsg = _silu_parts(g_ref[...])
            duv = du_ref[...]
            ov = o_ref[...]
            do = duv * sg
            dgate = (duv * ov * dsg).astype(BF16)
            dz_ref[2] = dgate[:, :half]
            dz_ref[3] = dgate[:, half:]
            do_o = do * ov
            do_bf = do.astype(BF16)
            lse_all = lse_ref[0]
            dsink_rows = []
            for gi in range(B_KV_HEADS):
                kcat, vcat, qs, sc = _b_scores_t(q_ref, kvv, bias_scr, gi, b, left, first_blocks)
                dos = _b_stack(do_bf, gi)
                doo_t = _b_stack(do_o, gi).T
                delta = (jnp.sum(doo_t[:HEAD_DIM], axis=0, keepdims=True),
                         jnp.sum(doo_t[HEAD_DIM:], axis=0, keepdims=True))
                sink = _b_sink_rows(sink_ref, gi)
                dp = lax.dot_general(vcat, dos, NT, preferred_element_type=F32)
                ps, dss = [], []
                for e in range(2):
                    lse_e = lse_all[2 * gi + e:2 * gi + e + 1]
                    delta_e = delta[e]
                    p = jnp.exp(sc[e * B_WIN:(e + 1) * B_WIN] - lse_e)
                    ps.append(p.astype(BF16))
                    dss.append(p * (dp[e * B_WIN:(e + 1) * B_WIN] - delta_e))
                    dsink_rows.append(-jnp.exp(sink[e] - lse_e) * delta_e)
                ds = jnp.concatenate(dss, axis=0)
                dbias_acc[gi] += ds
                dsb = ds.astype(BF16)
                dq = (lax.dot_general(kcat, dsb, TN, preferred_element_type=F32) * SCALE).T.astype(BF16)
                for pr in range(B_STACK):
                    dz_ref[gi, :, pr * LANES:(pr + 1) * LANES] = dq[pr * TQ:(pr + 1) * TQ]
                dk = _unstack_pair(jnp.dot(dsb, qs, preferred_element_type=F32), B_WIN)
                dv = _unstack_pair(jnp.dot(jnp.concatenate(ps, axis=0), dos, preferred_element_type=F32), B_WIN)
                krows = pl.ds(pl.multiple_of(b * TQ + pad - (B_WIN - TQ), KB), B_WIN)
                dkv_acc[krows, gi * LANES:(gi + 1) * LANES] += dk
                dkv_acc[krows, (B_KV_HEADS + gi) * LANES:(B_KV_HEADS + gi + 1) * LANES] += dv
            dsink_acc[...] += jnp.concatenate(
                dsink_rows + [jnp.zeros((8 - len(dsink_rows), B_ROWS), F32)], axis=0)

        @pl.when(b < left)
        def _():
            step(True)

        @pl.when(b >= left)
        def _():
            step(False)

        @pl.when(b == nb - 1)
        def _():
            lo_s = _lane_lo(s)
            for which in range(2):
                folded = []
                for gi in range(B_KV_HEADS):
                    part = dkv_acc[pad:pad + s, (which * B_KV_HEADS + gi) * LANES:(which * B_KV_HEADS + gi + 1) * LANES]
                    folded.append(part + pltpu.roll(part, HEAD_DIM, 1))
                dkv_ref[:, which * LANES:(which + 1) * LANES] = jnp.where(lo_s, folded[0], folded[1]).astype(BF16)
            lane8 = lax.broadcasted_iota(jnp.int32, dsink_ref.shape, 1)
            tot = jnp.zeros(dsink_ref.shape, F32)
            for h in range(HEADS):
                gi, pr, e = _b_head_place(h)
                dsum_ref[h] = _toeplitz_sum_t(
                    dbias_acc[gi, e * B_WIN:(e + 1) * B_WIN, pr * TQ:(pr + 1) * TQ], B_WIN)
                per_query = dsink_acc[2 * gi + e:2 * gi + e + 1, pr * TQ:(pr + 1) * TQ]
                tot = jnp.where(lane8 == h, jnp.sum(per_query, axis=1, keepdims=True), tot)
            dsink_ref[...] = tot

    row = pl.BlockSpec((TQ, D_MODEL), lambda b: (b, 0))
    base_spec = pl.BlockSpec((HEADS, 1, B_WIDE), lambda b: (0, 0, 0))
    return pl.pallas_call(
        body, name="attn_b_bwd", grid=(nb,),
        in_specs=[row, pl.BlockSpec((pad + s, B_KVX), lambda b: (0, 0)), row, row, row,
                  pl.BlockSpec((1, 8, B_ROWS), lambda b: (b, 0, 0)), base_spec,
                  pl.BlockSpec((1, HEADS), lambda b: (0, 0))],
        out_specs=[pl.BlockSpec((4, TQ, half), lambda b: (0, b, 0)),
                   pl.BlockSpec((s, 2 * LANES), lambda b: (0, 0)), base_spec,
                   pl.BlockSpec((8, LANES), lambda b: (0, 0))],
        out_shape=[jax.ShapeDtypeStruct((4, s, half), BF16), jax.ShapeDtypeStruct((s, 2 * LANES), BF16),
                   jax.ShapeDtypeStruct((HEADS, 1, B_WIDE), F32), jax.ShapeDtypeStruct((8, LANES), F32)],
        scratch_shapes=[pltpu.VMEM((B_KV_HEADS, 2 * B_WIN, B_ROWS), F32),
                        pltpu.VMEM((B_KV_HEADS, 2 * B_WIN, B_ROWS), F32),
                        pltpu.VMEM((pad + s, B_KVX), F32), pltpu.VMEM((8, B_ROWS), F32)],
        compiler_params=_params(("arbitrary",)),
    )(qb, kvx, gate, o, du, lse, base, sinks)


def _t5_bucket(rel):
    nb = T5_BUCKETS // 2
    max_exact = nb // 2
    ret = jnp.where(rel > 0, nb, 0)
    n = jnp.abs(rel)
    nf = jnp.maximum(n, 1).astype(jnp.float32)
    large = max_exact + (jnp.log(nf / max_exact) / math.log(T5_MAX_DIST / max_exact)
                         * (nb - max_exact)).astype(jnp.int32)
    large = jnp.minimum(large, nb - 1)
    return ret + jnp.where(n < max_exact, n, large)


def _a_offset_onehot():
    c = np.arange(A_WIN + TQ)
    dist = A_LEFT_CHUNKS * CHUNK + TQ - 1 - c
    idx = np.clip(dist, -A_REL_CLIP, A_REL_CLIP) + A_REL_CLIP
    onehot = np.zeros((A_WIN + TQ, 2 * A_REL_CLIP + 1), np.float32)
    onehot[c, idx] = 1.0
    return jnp.asarray(onehot)


def _b_offset_onehot():
    c = jnp.arange(B_WIN + TQ, dtype=jnp.int32)
    rel = c - (TQ - 1) - B_LEFT_CHUNKS * CHUNK
    return (_t5_bucket(rel)[:, None] == jnp.arange(T5_BUCKETS)[None, :]).astype(F32)


def _diag_rows(onehot, table):
    rows = jnp.dot(onehot, table.astype(F32), precision=lax.Precision.HIGHEST)
    return rows.T.reshape(HEADS, 1, onehot.shape[0])


def _diag_rows_grad(onehot, ddiag):
    return jnp.dot(ddiag.reshape(HEADS, onehot.shape[0]), onehot, precision=lax.Precision.HIGHEST)


def _position():
    x, y, c = lax.axis_index("x"), lax.axis_index("y"), lax.axis_index("c")
    chips = [(1 - x, y), (x, 1 - y), (1 - x, 1 - y)]
    return x, y, c, chips


ANY = pl.BlockSpec(memory_space=pl.ANY)


def _allgather_routed(shards):
    n = len(shards)

    def piece(block_ref, t, c, quarter=None):
        half = shards[t].shape[0] // 2
        if quarter is None:
            return block_ref.at[pl.ds(c * half, half)]
        return block_ref.at[pl.ds(c * half + quarter * (half // 2), half // 2)]

    def copies(kind, ins, outs, sems):
        ici_send, ici_recv, pass_send, pass_recv, local_sems = sems
        x, y, c, chips = _position()
        mine = 2 * x + y
        if kind == "local":
            return [pltpu.make_async_copy(ins[t], outs[t].at[mine], local_sems.at[t]) for t in range(n)]
        ids = [2 * chip[0] + chip[1] for chip in chips]
        made = []
        for t in range(n):
            def ici(k, to):
                return dict(send_sem=ici_send.at[4 * t + k], recv_sem=ici_recv.at[4 * t + k],
                            device_id=(chips[to][0], chips[to][1], c), device_id_type=MESH)

            def d2d(k):
                return dict(send_sem=pass_send.at[4 * t + k], recv_sem=pass_recv.at[4 * t + k],
                            device_id=(x, y, 1 - c), device_id_type=MESH)

            def same(ref, where):
                return pltpu.make_async_remote_copy(src_ref=ref, dst_ref=ref, **where)

            if kind == "send":
                for k in range(2):
                    made.append(pltpu.make_async_remote_copy(
                        src_ref=piece(ins[t], t, c), dst_ref=piece(outs[t].at[mine], t, c), **ici(k, k)))
            elif kind == "landed":
                made += [same(piece(outs[t].at[ids[k]], t, c), ici(k, k)) for k in range(2)]
            elif kind == "forward":
                made.append(same(piece(outs[t].at[ids[0]], t, c, 0), ici(2, 1)))
                made.append(same(piece(outs[t].at[ids[1]], t, c, 1), ici(3, 0)))
            elif kind == "arrived":
                made.append(same(piece(outs[t].at[ids[2]], t, c, 0), ici(2, 1)))
                made.append(same(piece(outs[t].at[ids[2]], t, c, 1), ici(3, 0)))
            else:
                core = 1 - c if kind == "passed" else c
                if kind in ("pass halves", "passed"):
                    made += [same(piece(outs[t].at[ids[k]], t, core), d2d(k)) for k in range(2)]
                if kind in ("pass quarters", "passed"):
                    made += [same(piece(outs[t].at[ids[2]], t, core, k), d2d(2 + k)) for k in range(2)]
        return made

    def first(ins, outs, sems):
        for cp in copies("local", ins, outs, sems) + copies("send", ins, outs, sems):
            cp.start()

    def middle(ins, outs, sems):
        for got, onward, near in zip(copies("landed", ins, outs, sems), copies("forward", ins, outs, sems),
                                     copies("pass halves", ins, outs, sems)):
            got.wait_recv()
            near.start()
            onward.start()

    def last(ins, outs, sems):
        quarters = copies("pass quarters", ins, outs, sems)
        for got, near in zip(copies("arrived", ins, outs, sems), quarters):
            got.wait_recv()
            near.start()
        for cp in copies("passed", ins, outs, sems):
            cp.wait_recv()
        for cp in (copies("send", ins, outs, sems) + copies("forward", ins, outs, sems)
                   + copies("pass halves", ins, outs, sems) + quarters):
            cp.wait_send()
        for cp in copies("local", ins, outs, sems):
            cp.wait()

    return _Hosted(shards, [jax.ShapeDtypeStruct((4,) + w.shape, w.dtype) for w in shards],
                   [pltpu.SemaphoreType.DMA((4 * n,))] * 4 + [pltpu.SemaphoreType.DMA((n,))],
                   first, middle, last)


def _scatter_hosted(grads):
    n = len(grads)

    def copies(ins, outs, sems):
        send_sems, recv_sems = sems
        x, y, c, chips = _position()
        return [pltpu.make_async_remote_copy(
            src_ref=ins[t].at[2 * chip[0] + chip[1]], dst_ref=outs[t].at[j],
            send_sem=send_sems.at[3 * t + j], recv_sem=recv_sems.at[3 * t + j],
            device_id=(chip[0], chip[1], c), device_id_type=MESH)
            for t in range(n) for j, chip in enumerate(chips)]

    def first(ins, outs, sems):
        for cp in copies(ins, outs, sems):
            cp.start()

    def last(ins, outs, sems):
        for cp in copies(ins, outs, sems):
            cp.wait()

    return _Hosted(grads, [jax.ShapeDtypeStruct((3,) + g.shape[1:], g.dtype) for g in grads],
                   [pltpu.SemaphoreType.DMA((3 * n,))] * 2, first, None, last)


GATHER_PEERS = "x and y neighbours (same core) and the sibling core"
SCATTER_PEERS = "the same core of the three other chips"
EVERYONE = "the seven other devices"


def _run_on_sequencer(name, hosted, peers, collective_id):
    ins = [jax.new_ref(a, memory_space=pltpu.MemorySpace.HBM) for a in hosted.inputs]
    outs = [jax.empty_ref(shape, memory_space=pltpu.MemorySpace.HBM) for shape in hosted.out_shapes]

    @pl.kernel(mesh=plsc.ScalarSubcoreMesh(axis_name="sequencer", num_cores=1), name=name,
               scratch_types=tuple(hosted.sems), compiler_params=pltpu.CompilerParams(collective_id=collective_id))
    def launch(*sems):
        x, y, c, chips = _position()
        if peers == GATHER_PEERS:
            devices = [(chip[0], chip[1], c) for chip in chips[:2]] + [(x, y, 1 - c)]
        elif peers == SCATTER_PEERS:
            devices = [(chip[0], chip[1], c) for chip in chips]
        else:
            devices = [(x ^ (k >> 2), y ^ ((k >> 1) & 1), c ^ (k & 1)) for k in range(1, 8)]
        barrier = pltpu.get_barrier_semaphore()
        for device in devices:
            pl.semaphore_signal(barrier, inc=1, device_id=device, device_id_type=MESH)
        pl.semaphore_wait(barrier, len(devices))
        hosted.first(ins, outs, sems)
        if hosted.middle is not None:
            hosted.middle(ins, outs, sems)
        hosted.last(ins, outs, sems)

    launch()
    return [o[...] for o in outs]


def _gather_gain(shard):
    def body(in_ref, out_ref, send_sems, recv_sems):
        x, y, c, chips = _position()
        out_ref[2 * x + y] = in_ref[...]
        sends = [pltpu.make_async_remote_copy(
            src_ref=in_ref, dst_ref=out_ref.at[2 * x + y], send_sem=send_sems.at[j], recv_sem=recv_sems.at[j],
            device_id=(chip[0], chip[1], c), device_id_type=MESH) for j, chip in enumerate(chips)]
        for cp in sends:
            cp.start()
        for j, chip in enumerate(chips):
            pltpu.make_async_remote_copy(
                src_ref=in_ref, dst_ref=out_ref.at[2 * chip[0] + chip[1]], send_sem=send_sems.at[j],
                recv_sem=recv_sems.at[j], device_id=(chip[0], chip[1], c), device_id_type=MESH).wait_recv()
        for cp in sends:
            cp.wait_send()

    vmem = pl.BlockSpec(memory_space=pltpu.VMEM)
    return pl.pallas_call(
        body, name="gather_gain", in_specs=[vmem], out_specs=vmem,
        out_shape=jax.ShapeDtypeStruct((4,) + shard.shape, shard.dtype),
        scratch_shapes=[pltpu.SemaphoreType.DMA((3,))] * 2,
    )(shard)


def _swap_with_sibling(name, blocks):
    n = len(blocks)

    def body(*refs):
        ins, outs = refs[:n], refs[n:2 * n]
        send_sems, recv_sems = refs[2 * n:]
        x, y, c, _ = _position()
        sends = [pltpu.make_async_remote_copy(
            src_ref=ins[t], dst_ref=outs[t], send_sem=send_sems.at[t], recv_sem=recv_sems.at[t],
            device_id=(x, y, 1 - c), device_id_type=MESH) for t in range(n)]
        for cp in sends:
            cp.start()
        for cp in sends:
            cp.wait()

    return pl.pallas_call(
        body, name=name,
        in_specs=[ANY] * n, out_specs=[ANY] * n,
        out_shape=[jax.ShapeDtypeStruct(b.shape, b.dtype) for b in blocks],
        scratch_shapes=[pltpu.SemaphoreType.DMA((n,))] * 2,
    )(*blocks)


def _everyone_hosted(terms):
    nt = len(terms)

    def copies(kind, ins, outs, sems):
        send_sems, recv_sems, local_sems = sems
        x, y, c, _ = _position()
        me = 4 * x + 2 * y + c
        if kind == "local":
            return [pltpu.make_async_copy(ins[t], outs[t].at[me], local_sems.at[t]) for t in range(nt)]
        made = []
        for t in range(nt):
            for k in range(1, 8):
                peer = (x ^ (k >> 2), y ^ ((k >> 1) & 1), c ^ (k & 1))
                slot = me if kind == "send" else me ^ k
                made.append(pltpu.make_async_remote_copy(
                    src_ref=ins[t], dst_ref=outs[t].at[slot], send_sem=send_sems.at[7 * t + k - 1],
                    recv_sem=recv_sems.at[7 * t + k - 1], device_id=peer, device_id_type=MESH))
        return made

    def first(ins, outs, sems):
        for cp in copies("local", ins, outs, sems) + copies("send", ins, outs, sems):
            cp.start()

    def last(ins, outs, sems):
        for cp in copies("landed", ins, outs, sems):
            cp.wait_recv()
        for cp in copies("send", ins, outs, sems):
            cp.wait_send()
        for cp in copies("local", ins, outs, sems):
            cp.wait()

    return _Hosted(terms, [jax.ShapeDtypeStruct((8,) + a.shape, F32) for a in terms],
                   [pltpu.SemaphoreType.DMA((7 * nt,))] * 2 + [pltpu.SemaphoreType.DMA((nt,))], first, None, last)


def _small_step(partials, extras, ws, ms, vs, shard_of):
    n = len(partials)
    terms = list(partials) + list(extras)
    nt = len(terms)
    rows = [t for t in range(nt) if terms[t].shape[0] == 1]
    mats = [t for t in range(nt) if terms[t].shape[0] != 1]
    row_block = (8, max(terms[t].shape[1] for t in rows))
    assert len(rows) <= row_block[0]
    vmem = pl.BlockSpec(memory_space=pltpu.VMEM)

    def pack(*refs):
        packed = refs[-1]
        packed[...] = jnp.zeros_like(packed)
        for i, t in enumerate(rows):
            packed[i:i + 1, 0:terms[t].shape[1]] = refs[i][...]

    packed = pl.pallas_call(pack, name="small_pack", in_specs=[vmem] * len(rows), out_specs=vmem,
                            out_shape=jax.ShapeDtypeStruct(row_block, F32))(*[terms[t] for t in rows])
    slots = _run_on_sequencer("allgather_small", _everyone_hosted([packed] + [terms[t] for t in mats]),
                              EVERYONE, 2)

    def body(*refs):
        slot_refs, refs = refs[:len(slots)], refs[len(slots):]
        w_refs, refs = refs[:n], refs[n:]
        m_refs, refs = refs[:n], refs[n:]
        v_refs, outs = refs[:n], refs[n:]
        sums = []
        for ref in slot_refs:
            g = ref[0]
            for dev in range(1, 8):
                g = g + ref[dev]
            sums.append(g)
        chip = 2 * lax.axis_index("x") + lax.axis_index("y")
        for t in range(nt):
            if t in rows:
                i = rows.index(t)
                g = sums[0][i:i + 1, 0:terms[t].shape[1]]
            else:
                g = sums[1 + mats.index(t)]
            if t >= n:
                outs[4 * n + t - n][...] = g
                continue
            if shard_of[t]:
                width = ws[t].shape[-1]
                mine = jnp.zeros(ws[t].shape, F32)
                for s in range(4):
                    mine = jnp.where(chip == s, g[:, s * width:(s + 1) * width], mine)
                g = mine
            delta, mn, vn = _adamw_math(w_refs[t][...], g, m_refs[t][...], v_refs[t][...])
            outs[4 * t][...] = g
            outs[4 * t + 1][...] = delta
            outs[4 * t + 2][...] = mn
            outs[4 * t + 3][...] = vn

    out_shapes = []
    for t in range(n):
        out_shapes += [jax.ShapeDtypeStruct(ws[t].shape, F32)] * 4
    out_shapes += [jax.ShapeDtypeStruct(a.shape, F32) for a in extras]
    res = pl.pallas_call(
        body, name="small_step",
        in_specs=[vmem] * (len(slots) + 3 * n), out_specs=[vmem] * len(out_shapes), out_shape=out_shapes,
    )(*slots, *ws, *ms, *vs)
    return [res[4 * t:4 * t + 4] for t in range(n)], res[4 * n:4 * n + nt - n]


def _adamw_math(w, g, m, v):
    m = ADAM_B1 * m + (1.0 - ADAM_B1) * g
    v = ADAM_B2 * v + (1.0 - ADAM_B2) * (g * g)
    m_hat = m / (1.0 - ADAM_B1 ** ADAM_STEP)
    v_hat = v / (1.0 - ADAM_B2 ** ADAM_STEP)
    delta = -ADAM_LR * (m_hat / (jnp.sqrt(v_hat) + ADAM_EPS) + ADAM_WD * w)
    return delta, m, v


def _row_tile(rows):
    return 256 if rows % 256 == 0 else rows


def _sum_partials(name, own, recv, chip, after):
    rows, cols = own.shape[1:]
    tr = _row_tile(rows)

    def body(chip_ref, own_ref, recv_ref, after_ref, o_ref):
        acc = own_ref[...]
        for j in range(3):
            acc = acc + recv_ref[j].astype(F32)
        o_ref[...] = acc

    return pl.pallas_call(
        body, name=name,
        grid_spec=pltpu.PrefetchScalarGridSpec(
            num_scalar_prefetch=1, grid=(rows // tr,),
            in_specs=[pl.BlockSpec((None, tr, cols), lambda i, chip_ref: (chip_ref[0], i, 0)),
                      pl.BlockSpec((3, tr, cols), lambda i, chip_ref: (0, i, 0)), ANY],
            out_specs=pl.BlockSpec((tr, cols), lambda i, chip_ref: (i, 0))),
        out_shape=jax.ShapeDtypeStruct((rows, cols), F32),
        compiler_params=_params(("parallel",)),
    )(chip.reshape(1).astype(jnp.int32), own, recv, after)


def _adamw_ring(name, body, inputs, tr, spec):
    rows, cols = inputs[0].shape
    n_in = len(inputs)
    steps = rows // tr

    def ring(*refs):
        in_refs, out_refs = refs[:n_in], refs[n_in:n_in + 4]
        buf, sems = refs[n_in + 4:]
        i = pl.program_id(0)

        def copies(step, slot):
            return [pltpu.make_async_copy(in_refs[t].at[pl.ds(pl.multiple_of(step * tr, tr), tr)],
                                          buf.at[slot, t], sems.at[slot, t]) for t in range(n_in)]

        @pl.when(i == 0)
        def _():
            for s in range(min(STREAM_BUFFERS, steps)):
                for cp in copies(s, s):
                    cp.start()

        ahead = i + STREAM_BUFFERS - 1

        @pl.when((i > 0) & (ahead < steps))
        def _():
            for cp in copies(ahead, ahead % STREAM_BUFFERS):
                cp.start()

        slot = i % STREAM_BUFFERS
        for cp in copies(i, slot):
            cp.wait()
        body(*[buf.at[slot, t] for t in range(n_in)], *out_refs)

    return pl.pallas_call(
        ring, name=name, grid=(steps,), in_specs=[ANY] * n_in, out_specs=[spec] * 4,
        out_shape=[jax.ShapeDtypeStruct((rows, cols), F32)] * 4,
        scratch_shapes=[pltpu.VMEM((STREAM_BUFFERS, n_in, tr, cols), F32),
                        pltpu.SemaphoreType.DMA((STREAM_BUFFERS, n_in))],
        compiler_params=_params(("arbitrary",)),
    )(*inputs)


def _adamw(name, w, m, v, g_parts):
    rows, cols = w.shape
    tr = _row_tile(rows)
    n = len(g_parts)

    def body(w_ref, m_ref, v_ref, *refs):
        g_refs = refs[:n]
        go_ref, d_ref, mo_ref, vo_ref = refs[n:]
        g = g_refs[0][...]
        for r in g_refs[1:]:
            g = g + r[...]
        delta, mn, vn = _adamw_math(w_ref[...], g, m_ref[...], v_ref[...])
        go_ref[...] = g
        d_ref[...] = delta
        mo_ref[...] = mn
        vo_ref[...] = vn

    spec = pl.BlockSpec((tr, cols), lambda i: (i, 0))
    steps = rows // tr
    if steps > 2:
        return _adamw_ring(name, body, [w, m, v, *g_parts], tr, spec)
    return pl.pallas_call(
        body, name=name, grid=(steps,),
        in_specs=[spec] * (3 + n), out_specs=[spec] * 4,
        out_shape=[jax.ShapeDtypeStruct((rows, cols), F32)] * 4,
        compiler_params=_params(("parallel",)),
    )(w, m, v, *g_parts)


def _local_step(x, target, ga, wa_in, rel_bias, later_shards, gk, t5, gb, sinks, gf):
    s, d = x.shape
    tm = min(TM_DENSE, s)
    nt = s // tm
    half = d // 2
    row = pl.BlockSpec((tm, d), lambda i: (i, 0))
    short = pl.BlockSpec((min(TM, s), d), lambda i: (i, 0))
    whole = lambda shape: pl.BlockSpec(shape, lambda *_: (0,) * len(shape))

    n1, = _norm_fwd("norm_a", x, ga)
    projected = None
    for h, (wa_half, tag) in enumerate(zip(wa_in, ("first", "second"))):
        projected = _proj_a_half("proj_a_" + tag, n1, wa_half, h, projected)
    zqkv, gate_a = projected
    onehot_a = _a_offset_onehot()
    diag_a = _diag_rows(onehot_a, rel_bias)
    (o_a, u_a, lse_a), gathered = _attn_a_fwd(zqkv, gate_a, diag_a, hosted=_allgather_routed(later_shards))
    wa_out, wkv, wb_in, wb_out, wkv_x = gathered
    wa_out = wa_out.reshape(d, d)
    wkv = wkv.reshape(d, -1)
    wkv_x = wkv_x.reshape(d, B_KVX)
    wb_out = wb_out.reshape(d, d)
    h1, nk, n2 = _out_norms("out_a_norms", u_a, wa_out, x, jnp.concatenate([gk, gb], axis=0))
    kvw = wkv.shape[1]
    kvx =_matmul("proj_kv", nk, wkv_x, dims=NN, grid=(nt + 1,), zero_axis=0,
                  a_spec=pl.BlockSpec((tm, d), lambda i: (jnp.maximum(i - 1, 0), 0)), b_spec=whole((d, B_KVX)),
                  o_spec=pl.BlockSpec((tm, B_KVX), lambda i: (i, 0)), out_shape=(tm + s, B_KVX), out_dtype=BF16)
    qb, gate_b = _proj_b(n2, wb_in)
    onehot_b = _b_offset_onehot()
    base_b = jnp.roll(_diag_rows(onehot_b, t5)[..., ::-1], TQ, axis=-1)
    o_b, u_b, lse_b = _attn_b_fwd(qb, kvx, gate_b, base_b, sinks)
    dh2, loss, d_gf = _out_loss_head(u_b, wb_out, h1, target, gf)

    du_b = _matmul("dout_b", dh2, wb_out, dims=NT, grid=(s // min(TM, s),), a_spec=short, b_spec=whole((d, d)), o_spec=short,
                   out_shape=(s, d), out_dtype=F32)
    d_wb_out = _matmul("dw_out_b", u_b, dh2, dims=TN, grid=(2,),
                       a_spec=whole((s, d)), b_spec=pl.BlockSpec((s, half), lambda j: (0, j)),
                       o_spec=pl.BlockSpec((d, half), lambda j: (0, j)),
                       out_shape=(d, d), out_dtype=F32, also_bf16=True)
    dz_b, dkv, dsum_b, dsinks = _attn_b_bwd(qb, kvx, gate_b, o_b, du_b, lse_b, base_b, sinks)
    ddiag_b = jnp.roll(dsum_b[..., ::-1], -1, axis=-1)
    d_wb_in = _matmul("dw_in_b", n2, dz_b, dims=TN, grid=(4,),
                      a_spec=whole((s, d)), b_spec=pl.BlockSpec((None, s, half), lambda j: (j, 0, 0)),
                      o_spec=pl.BlockSpec((None, d, half), lambda j: (j, 0, 0)),
                      out_shape=(4, d, half), out_dtype=F32, also_bf16=True)
    d_wkv = _matmul("dw_kv", nk, dkv, dims=TN, grid=(1,),
                    a_spec=whole((s, d)), b_spec=whole((s, kvw)), o_spec=whole((d, kvw)),
                    out_shape=(d, kvw), out_dtype=F32, also_bf16=True)
    dh1, d_gkb = _proj_norm_bwd("dproj_kv_b", h1, dh2, jnp.concatenate([gk, gb], axis=0),
                                [(dkv[None], [wkv[None]]), (dz_b, [wb_in])])

    du_a = _matmul("dout_a", dh1, wa_out, dims=NT, grid=(s // min(TM, s),), a_spec=short, b_spec=whole((d, d)), o_spec=short,
                   out_shape=(s, d), out_dtype=F32)
    d_wa_out = _matmul("dw_out_a", u_a, dh1, dims=TN, grid=(2,),
                       a_spec=whole((s, d)), b_spec=pl.BlockSpec((s, half), lambda j: (0, j)),
                       o_spec=pl.BlockSpec((d, half), lambda j: (0, j)),
                       out_shape=(d, d), out_dtype=F32, also_bf16=True)
    early = dict(a_w_out=[g.reshape(4, d // 4, d) for g in d_wa_out],
                 kv_w=[g.reshape(4, d // 4, kvw) for g in d_wkv], b_w_in=list(d_wb_in),
                 b_w_out=[g.reshape(4, d // 4, d) for g in d_wb_out])
    (dz_a, ddiag_a), early_recv = _attn_a_bwd(
        zqkv, gate_a, o_a, du_a, lse_a, diag_a, hosted=_scatter_hosted([early[n][1] for n in early]))
    d_wa_in = _matmul("dw_in_a", n1, dz_a, dims=TN, grid=(4, 2),
                      a_spec=whole((s, d)), b_spec=pl.BlockSpec((None, s, half), lambda j, h: (j, 0, h)),
                      o_spec=pl.BlockSpec((None, d, half), lambda j, h: (j, 0, h)),
                      out_shape=(4, d, d), out_dtype=F32, also_bf16=True)
    late_recv = _run_on_sequencer("scatter_a_w_in", _scatter_hosted([d_wa_in[1]]), SCATTER_PEERS, 0)
    grad_x, d_ga = _proj_norm_bwd("dproj_a", x, dh1, ga, [(dz_a, list(wa_in))])

    small = dict(a_norm=d_ga, kv_norm=d_gkb[0:1], b_norm=d_gkb[1:2], b_sinks=dsinks[0:1, :HEADS], final_norm=d_gf)
    small["by_offset"] = dict(a_rel_bias=(onehot_a, ddiag_a.reshape(HEADS, -1)),
                              t5_bias=(onehot_b, ddiag_b.reshape(HEADS, -1)))
    own = dict(a_w_in=d_wa_in[0], **{n: early[n][0] for n in early})
    received = dict(a_w_in=late_recv[0], **dict(zip(early, early_recv)))
    return loss, grad_x, small, own, received, d_wa_in[1]


SMALL = ("a_norm", "kv_norm", "b_norm", "b_sinks", "final_norm")
TABLES = ("a_rel_bias", "t5_bias")
BIG = ("a_w_in", "a_w_out", "kv_w", "b_w_in", "b_w_out")
ORDER = ("a_norm", "a_w_in", "a_rel_bias", "a_w_out", "kv_norm", "kv_w", "t5_bias", "b_norm", "b_w_in",
         "b_sinks", "b_w_out", "final_norm")


def kernel(x, a_norm, a_w_in, a_rel_bias, a_w_out, kv_norm, kv_w, t5_bias, b_norm, b_w_in, b_sinks, b_w_out, final_norm, loss_target, m_a_norm, m_a_w_in, m_a_rel_bias, m_a_w_out, m_kv_norm, m_kv_w, m_t5_bias, m_b_norm, m_b_w_in, m_b_sinks, m_b_w_out, m_final_norm, v_a_norm, v_a_w_in, v_a_rel_bias, v_a_w_out, v_kv_norm, v_kv_w, v_t5_bias, v_b_norm, v_b_w_in, v_b_sinks, v_b_w_out, v_final_norm):
    w = dict(a_norm=a_norm, a_w_in=a_w_in, a_rel_bias=a_rel_bias, a_w_out=a_w_out, kv_norm=kv_norm, kv_w=kv_w,
             t5_bias=t5_bias, b_norm=b_norm, b_w_in=b_w_in, b_sinks=b_sinks, b_w_out=b_w_out,
             final_norm=final_norm)
    m = dict(a_norm=m_a_norm, a_w_in=m_a_w_in, a_rel_bias=m_a_rel_bias, a_w_out=m_a_w_out, kv_norm=m_kv_norm,
             kv_w=m_kv_w, t5_bias=m_t5_bias, b_norm=m_b_norm, b_w_in=m_b_w_in, b_sinks=m_b_sinks,
             b_w_out=m_b_w_out, final_norm=m_final_norm)
    v = dict(a_norm=v_a_norm, a_w_in=v_a_w_in, a_rel_bias=v_a_rel_bias, a_w_out=v_a_w_out, kv_norm=v_kv_norm,
             kv_w=v_kv_w, t5_bias=v_t5_bias, b_norm=v_b_norm, b_w_in=v_b_w_in, b_sinks=v_b_sinks,
             b_w_out=v_b_w_out, final_norm=v_final_norm)
    d = D_MODEL
    chip = 2 * lax.axis_index("x") + lax.axis_index("y")

    shard2d = dict(a_w_in=a_w_in[0], a_w_out=a_w_out[0], kv_w=kv_w, b_w_in=b_w_in[0], b_w_out=b_w_out[0])

    first = shard2d["a_w_in"].astype(BF16)
    wa_in = [_run_on_sequencer("allgather_" + tag, _allgather_routed([first[:, h * (d // 2):(h + 1) * (d // 2)]]),
                               GATHER_PEERS, collective_id)[0]
             for h, (tag, collective_id) in enumerate((("first", 1), ("second", 3)))]
    ga = _gather_gain(a_norm).reshape(1, d)

    later = [shard2d[n].astype(BF16) for n in BIG[1:]]
    kv_shard = later[BIG[1:].index("kv_w")]
    later.append(jnp.concatenate(
        [kv_shard[:, (i // 2) * HEAD_DIM:(i // 2 + 1) * HEAD_DIM] for i in range(B_KVX // HEAD_DIM)], axis=1))
    loss, grad_x, small, own, received, after_attention = _local_step(
        x[0], loss_target[0], ga, wa_in, a_rel_bias[0], later,
        kv_norm.reshape(1, d), t5_bias, b_norm, b_sinks, final_norm.reshape(1, d))

    out = {}
    as2d = lambda a: a.reshape(-1, a.shape[-1])
    small_res, (loss_sum, *offset_sums) = _small_step(
        [small[n] for n in SMALL], [loss] + [small["by_offset"][n][1] for n in TABLES],
        [as2d(w[n]) for n in SMALL], [as2d(m[n]) for n in SMALL], [as2d(v[n]) for n in SMALL],
        [n == "a_norm" for n in SMALL])
    for n, res in zip(SMALL, small_res):
        out[n] = [r.reshape(w[n].shape) for r in res]
    loss_out = loss_sum.reshape(())
    for n, summed in zip(TABLES, offset_sums):
        grad = _diag_rows_grad(small["by_offset"][n][0], summed)
        res = _adamw("adamw_" + n, as2d(w[n]).T, as2d(m[n]).T, as2d(v[n]).T, [grad])
        out[n] = [r.T.reshape(w[n].shape) for r in res]

    core_sums = [_sum_partials("sum_" + n, own[n], received[n], chip, after_attention) for n in BIG]
    sibling_sums = (_swap_with_sibling("swap_last", core_sums[:1])
                    + _swap_with_sibling("swap_early", core_sums[1:]))

    for n, mine, theirs in zip(BIG, core_sums, sibling_sums):
        res = _adamw("adamw_" + n, shard2d[n], m[n].reshape(shard2d[n].shape), v[n].reshape(shard2d[n].shape),
                     [mine, theirs])
        out[n] = [r.reshape(w[n].shape) for r in res]

    grads = [out[n][0] for n in ORDER]
    deltas = [out[n][1] for n in ORDER]
    new_m = [out[n][2] for n in ORDER]
    new_v = [out[n][3] for n in ORDER]
    return (loss_out, grad_x[None], *grads, *deltas, *new_m, *new_v)
```
